```python
import jax, jax.numpy as jnp
from jax import lax
import numpy as np

D_MODEL = 1024
BATCH = 8
SEQ = 4096
DEPTH = 1

HEAD_DIM = 64
GRID_W = 64
NA_HEADS = 8
NA_WIN_ROWS = 8
NA_WIN_COLS = 16
NA_QBLOCK_COLS = 16
NA_KBLOCK_COLS = 32
DIL_GROUPS = ((128, 1), (512, 4), (2048, 16))
DIL_HEADS_PER_GROUP = 4
DIL_HEADS = DIL_HEADS_PER_GROUP * len(DIL_GROUPS)
NA_WIDTH = NA_HEADS * HEAD_DIM
DIL_WIDTH = DIL_HEADS * HEAD_DIM
DIL_OUT_WIDTH = DIL_HEADS_PER_GROUP * HEAD_DIM
IN_WIDTHS = (NA_WIDTH, NA_WIDTH, NA_WIDTH, DIL_WIDTH, DIL_WIDTH, DIL_WIDTH, D_MODEL, D_MODEL)
IN_WIDTH = sum(IN_WIDTHS)
D_FF = 4 * D_MODEL
PLE_DIM = 256
ROPE_THETA = 10000.0
RMS_EPS = 1e-6
NEG_INF = -1e30

kernel_name = "hybrid_na_dilated_gated_encoder"


def rms_norm(x, g):
    xf = x.astype(jnp.float32)
    y = xf * lax.rsqrt(jnp.mean(xf * xf, axis=-1, keepdims=True) + RMS_EPS)
    return (y * g.astype(jnp.float32)).astype(x.dtype)


def rotary(x, positions):
    half = HEAD_DIM // 2
    inv_freq = ROPE_THETA ** (-jnp.arange(half, dtype=jnp.float32) / half)
    ang = positions.astype(jnp.float32)[:, None, :, None] * inv_freq
    cos, sin = jnp.cos(ang), jnp.sin(ang)
    xf = x.astype(jnp.float32)
    x1, x2 = xf[..., :half], xf[..., half:]
    return jnp.concatenate([x1 * cos - x2 * sin, x2 * cos + x1 * sin], axis=-1).astype(x.dtype)


def neighborhood_attention(q, k, v, rpb):
    b, h, s, dh = q.shape
    rows = s // GRID_W
    wr = min(NA_WIN_ROWS, rows)
    n_cb = GRID_W // NA_QBLOCK_COLS
    r = np.arange(rows)
    rs = np.clip(r - wr // 2, 0, rows - wr)
    row_idx = rs[:, None] + np.arange(wr)[None, :]
    row_off = row_idx - r[:, None] + (NA_WIN_ROWS - 1)
    c = np.arange(GRID_W)
    cs = np.clip(c - NA_WIN_COLS // 2, 0, GRID_W - NA_WIN_COLS)
    cb = np.arange(n_cb)
    kc0 = np.clip(cb * NA_QBLOCK_COLS - NA_WIN_COLS // 2, 0, GRID_W - NA_KBLOCK_COLS)
    col_idx = kc0[:, None] + np.arange(NA_KBLOCK_COLS)[None, :]
    qcol = cb[:, None] * NA_QBLOCK_COLS + np.arange(NA_QBLOCK_COLS)[None, :]
    kcol = col_idx[:, None, :]
    qs = cs[qcol][:, :, None]
    col_valid = (kcol >= qs) & (kcol < qs + NA_WIN_COLS)
    col_off = np.clip(kcol - qcol[:, :, None], -(NA_WIN_COLS - 1), NA_WIN_COLS - 1) + (NA_WIN_COLS - 1)
    bias = jnp.take(rpb[:, row_off].astype(jnp.float32), col_off, axis=-1)
    bias = jnp.where(col_valid, bias, NEG_INF).transpose(0, 1, 3, 4, 2, 5)

    qg = (q * (dh ** -0.5)).reshape(b, h, rows, n_cb, NA_QBLOCK_COLS, dh)
    kc = jnp.take(k.reshape(b, h, rows, GRID_W, dh), col_idx, axis=3)
    vc = jnp.take(v.reshape(b, h, rows, GRID_W, dh), col_idx, axis=3)
    scores = jnp.stack(
        [jnp.einsum('bhrcqd,bhrckd->bhrcqk', qg, jnp.take(kc, row_idx[:, i], axis=2)).astype(jnp.float32)
         for i in range(wr)], axis=-2)
    scores = scores + bias[None]
    probs = jax.nn.softmax(scores.reshape(*scores.shape[:-2], wr * NA_KBLOCK_COLS), axis=-1)
    probs = probs.reshape(scores.shape).astype(v.dtype)
    out = jnp.einsum('bhrcqk,bhrckd->bhrcqd', probs[..., 0, :], jnp.take(vc, row_idx[:, 0], axis=2))
    for i in range(1, wr):
        out = out + jnp.einsum('bhrcqk,bhrckd->bhrcqd', probs[..., i, :], jnp.take(vc, row_idx[:, i], axis=2))
    return out.reshape(b, h, s, dh)


def banded_attention(q, k, v, radius):
    *lead, L, dh = q.shape
    blk = radius
    nb = -(-L // blk)
    lp = nb * blk
    nlead = len(lead)
    qb = jnp.pad(q, [(0, 0)] * nlead + [(0, lp - L), (0, 0)]).reshape(*lead, nb, blk, dh)

    def band(t):
        tb = jnp.pad(t, [(0, 0)] * nlead + [(blk, lp - L + blk), (0, 0)]).reshape(*lead, nb + 2, blk, dh)
        return jnp.concatenate([tb[..., :-2, :, :], tb[..., 1:-1, :, :], tb[..., 2:, :, :]], axis=-2)

    kb, vb = band(k), band(v)
    s = jnp.einsum('...nqd,...nkd->...nqk', qb, kb).astype(jnp.float32) * (dh ** -0.5)
    qi = np.arange(lp).reshape(nb, blk)[:, :, None]
    kj = np.arange(nb)[:, None, None] * blk - blk + np.arange(3 * blk)[None, None, :]
    valid = (kj >= 0) & (kj < L) & (np.abs(qi - kj) <= radius)
    s = jnp.where(valid, s, NEG_INF)
    lse = jax.nn.logsumexp(s, axis=-1)
    pr = jnp.exp(s - lse[..., None]).astype(v.dtype)
    o = jnp.einsum('...nqk,...nkd->...nqd', pr, vb)
    return o.reshape(*lead, lp, dh)[..., :L, :], lse.reshape(*lead, lp)[..., :L]


def dilated_attention(q, k, v):
    b, _, s, dh = q.shape
    hg = DIL_HEADS_PER_GROUP
    outs, lses = [], []
    for g, (window, dil) in enumerate(DIL_GROUPS):
        radius = window // (2 * dil)

        def split(t):
            return t[:, g * hg:(g + 1) * hg].reshape(b, hg, s // dil, dil, dh).swapaxes(2, 3)

        o, lse = banded_attention(split(q), split(k), split(v), radius)
        outs.append(o.swapaxes(2, 3).reshape(b, hg, s, dh))
        lses.append(lse.swapaxes(2, 3).reshape(b, hg, s))
    w = jax.nn.softmax(jnp.stack(lses), axis=0).astype(q.dtype)
    return jnp.einsum('gbhs,gbhsd->bhsd', w, jnp.stack(outs))


def _fwd_setup_inputs(seed: int = 0) -> dict:
    key = jax.random.key(seed)
    ks = jax.random.split(key, 20)
    f32 = jnp.float32

    def nrm(k, shape, fan_in):
        return jax.random.normal(k, shape, f32) * (fan_in ** -0.5)

    def gain(k, shape):
        return 1.0 + 0.01 * jax.random.normal(k, shape, f32)

    return {
        "x": jax.random.normal(ks[0], (BATCH, SEQ, D_MODEL), f32),
        "p": jax.random.normal(ks[1], (DEPTH, BATCH, SEQ, PLE_DIM), f32),
        "positions": (jnp.arange(SEQ, dtype=jnp.int32)[None, :]
                      + jax.random.randint(ks[2], (BATCH, 1), 0, 1024, dtype=jnp.int32)),
        "g_mix": gain(ks[3], (DEPTH, D_MODEL)),
        "w_in": nrm(ks[4], (DEPTH, D_MODEL, IN_WIDTH), D_MODEL),
        "rpb": 0.02 * jax.random.normal(ks[5], (DEPTH, NA_HEADS, 2 * NA_WIN_ROWS - 1, 2 * NA_WIN_COLS - 1), f32),
        "w_branch_na": nrm(ks[6], (DEPTH, NA_WIDTH, D_MODEL), NA_WIDTH),
        "w_branch_dil": nrm(ks[7], (DEPTH, DIL_OUT_WIDTH, D_MODEL), DIL_OUT_WIDTH),
        "w_out": nrm(ks[8], (DEPTH, D_MODEL, D_MODEL), D_MODEL),
        "g_mlp": gain(ks[9], (DEPTH, D_MODEL)),
        "w_up": nrm(ks[10], (DEPTH, D_MODEL, D_FF), D_MODEL),
        "w_down": nrm(ks[11], (DEPTH, D_FF, D_MODEL), D_FF),
        "g_ple": gain(ks[12], (DEPTH, D_MODEL)),
        "w_ple_gate": nrm(ks[13], (DEPTH, D_MODEL, D_MODEL), D_MODEL),
        "w_ple_proj": nrm(ks[14], (DEPTH, PLE_DIM, D_MODEL), PLE_DIM),
        "g_final": gain(ks[15], (D_MODEL,)),
    }


def _fwd_reference(x, p, positions, g_mix, w_in, rpb, w_branch_na, w_branch_dil, w_out, g_mlp, w_up, w_down,
              g_ple, w_ple_gate, w_ple_proj, g_final):
    b, s, _ = x.shape
    split_points = [int(v) for v in np.cumsum(IN_WIDTHS)[:-1]]

    def heads(t, n):
        return t.reshape(b, s, n, HEAD_DIM).transpose(0, 2, 1, 3)

    def merge(t):
        return t.transpose(0, 2, 1, 3).reshape(b, s, -1)

    h = x
    for i in range(DEPTH):
        a = rms_norm(h, g_mix[i])
        qa, ka, va, qd, kd, vd, gate_na, gate_dil = jnp.split(a @ w_in[i], split_points, axis=-1)
        y_na = merge(neighborhood_attention(heads(qa, NA_HEADS), heads(ka, NA_HEADS), heads(va, NA_HEADS), rpb[i]))
        y_dil = merge(dilated_attention(rotary(heads(qd, DIL_HEADS), positions),
                                        rotary(heads(kd, DIL_HEADS), positions),
                                        heads(vd, DIL_HEADS)))
        mixed = (jax.nn.sigmoid(gate_na) * (y_na @ w_branch_na[i])
                 + jax.nn.sigmoid(gate_dil) * (y_dil @ w_branch_dil[i]))
        h = h + mixed @ w_out[i]
        c = rms_norm(h, g_mlp[i])
        h = h + jnp.square(jax.nn.relu(c @ w_up[i])) @ w_down[i]
        e = rms_norm(h, g_ple[i])
        h = h + jax.nn.sigmoid(e @ w_ple_gate[i]) * (p[i] @ w_ple_proj[i])
    return rms_norm(h, g_final)


import jax as _jax
import jax.numpy as _jnp

TWIN_FORMAT = 'train_step'
FWD_PARAMS = ['x', 'p', 'positions', 'g_mix', 'w_in', 'rpb', 'w_branch_na', 'w_branch_dil', 'w_out', 'g_mlp', 'w_up', 'w_down', 'g_ple', 'w_ple_gate', 'w_ple_proj', 'g_final']
TWIN_WEIGHTS = ['g_mix', 'w_in', 'rpb', 'w_branch_na', 'w_branch_dil', 'w_out', 'g_mlp', 'w_up', 'w_down', 'g_ple', 'w_ple_gate', 'w_ple_proj', 'g_final']
TWIN_DIFF_INPUT = 'x'
TWIN_INPUTS = ['x', 'p', 'positions', 'g_mix', 'w_in', 'rpb', 'w_branch_na', 'w_branch_dil', 'w_out', 'g_mlp', 'w_up', 'w_down', 'g_ple', 'w_ple_gate', 'w_ple_proj', 'g_final', 'loss_target', 'm_g_mix', 'm_w_in', 'm_rpb', 'm_w_branch_na', 'm_w_branch_dil', 'm_w_out', 'm_g_mlp', 'm_w_up', 'm_w_down', 'm_g_ple', 'm_w_ple_gate', 'm_w_ple_proj', 'm_g_final', 'v_g_mix', 'v_w_in', 'v_rpb', 'v_w_branch_na', 'v_w_branch_dil', 'v_w_out', 'v_g_mlp', 'v_w_up', 'v_w_down', 'v_g_ple', 'v_w_ple_gate', 'v_w_ple_proj', 'v_g_final']
TWIN_OUTPUTS = ['loss', 'grad_x', 'grad_g_mix', 'grad_w_in', 'grad_rpb', 'grad_w_branch_na', 'grad_w_branch_dil', 'grad_w_out', 'grad_g_mlp', 'grad_w_up', 'grad_w_down', 'grad_g_ple', 'grad_w_ple_gate', 'grad_w_ple_proj', 'grad_g_final', 'delta_g_mix', 'delta_w_in', 'delta_rpb', 'delta_w_branch_na', 'delta_w_branch_dil', 'delta_w_out', 'delta_g_mlp', 'delta_w_up', 'delta_w_down', 'delta_g_ple', 'delta_w_ple_gate', 'delta_w_ple_proj', 'delta_g_final', 'new_m_g_mix', 'new_m_w_in', 'new_m_rpb', 'new_m_w_branch_na', 'new_m_w_branch_dil', 'new_m_w_out', 'new_m_g_mlp', 'new_m_w_up', 'new_m_w_down', 'new_m_g_ple', 'new_m_w_ple_gate', 'new_m_w_ple_proj', 'new_m_g_final', 'new_v_g_mix', 'new_v_w_in', 'new_v_rpb', 'new_v_w_branch_na', 'new_v_w_branch_dil', 'new_v_w_out', 'new_v_g_mlp', 'new_v_w_up', 'new_v_w_down', 'new_v_g_ple', 'new_v_w_ple_gate', 'new_v_w_ple_proj', 'new_v_g_final']
TWIN_LEAF_KINDS = {'loss': 'loss', 'grad_x': 'grad_x', 'grad_g_mix': 'grad_w', 'grad_w_in': 'grad_w', 'grad_rpb': 'grad_w', 'grad_w_branch_na': 'grad_w', 'grad_w_branch_dil': 'grad_w', 'grad_w_out': 'grad_w', 'grad_g_mlp': 'grad_w', 'grad_w_up': 'grad_w', 'grad_w_down': 'grad_w', 'grad_g_ple': 'grad_w', 'grad_w_ple_gate': 'grad_w', 'grad_w_ple_proj': 'grad_w', 'grad_g_final': 'grad_w', 'delta_g_mix': 'delta_w', 'delta_w_in': 'delta_w', 'delta_rpb': 'delta_w', 'delta_w_branch_na': 'delta_w', 'delta_w_branch_dil': 'delta_w', 'delta_w_out': 'delta_w', 'delta_g_mlp': 'delta_w', 'delta_w_up': 'delta_w', 'delta_w_down': 'delta_w', 'delta_g_ple': 'delta_w', 'delta_w_ple_gate': 'delta_w', 'delta_w_ple_proj': 'delta_w', 'delta_g_final': 'delta_w', 'new_m_g_mix': 'new_m', 'new_m_w_in': 'new_m', 'new_m_rpb': 'new_m', 'new_m_w_branch_na': 'new_m', 'new_m_w_branch_dil': 'new_m', 'new_m_w_out': 'new_m', 'new_m_g_mlp': 'new_m', 'new_m_w_up': 'new_m', 'new_m_w_down': 'new_m', 'new_m_g_ple': 'new_m', 'new_m_w_ple_gate': 'new_m', 'new_m_w_ple_proj': 'new_m', 'new_m_g_final': 'new_m', 'new_v_g_mix': 'new_v', 'new_v_w_in': 'new_v', 'new_v_rpb': 'new_v', 'new_v_w_branch_na': 'new_v', 'new_v_w_branch_dil': 'new_v', 'new_v_w_out': 'new_v', 'new_v_g_mlp': 'new_v', 'new_v_w_up': 'new_v', 'new_v_w_down': 'new_v', 'new_v_g_ple': 'new_v', 'new_v_w_ple_gate': 'new_v', 'new_v_w_ple_proj': 'new_v', 'new_v_g_final': 'new_v'}


def _forward(args):
    return _fwd_reference(*[args[k] for k in FWD_PARAMS])


def _output_shape():
    out = _jax.eval_shape(lambda: _forward(_fwd_setup_inputs(0)))
    return out.shape, out.dtype

N_MICROBATCH = 1
ADAM_LR = 0.001
ADAM_B1 = 0.9
ADAM_B2 = 0.999
ADAM_EPS = 1e-08
ADAM_WD = 0.01
ADAM_STEP = 10
PER_EXAMPLE_BATCH_AXIS = {'x': 0, 'p': 1, 'positions': 0, 'loss_target': 0}
SHARED_INPUTS = []
_WEIGHT_DTYPES = {'g_mix': _jnp.float32, 'w_in': _jnp.float32, 'rpb': _jnp.float32, 'w_branch_na': _jnp.float32, 'w_branch_dil': _jnp.float32, 'w_out': _jnp.float32, 'g_mlp': _jnp.float32, 'w_up': _jnp.float32, 'w_down': _jnp.float32, 'g_ple': _jnp.float32, 'w_ple_gate': _jnp.float32, 'w_ple_proj': _jnp.float32, 'g_final': _jnp.float32}
MOMENT_SCALE = {'g_mix': 4.035674e-02, 'w_in': 1.653246e-02, 'rpb': 9.546617e-03, 'w_branch_na': 1.971101e-02, 'w_branch_dil': 1.190653e-02, 'w_out': 2.303789e-02, 'g_mlp': 1.570681e-01, 'w_up': 7.543820e-02, 'w_down': 1.324548e-01, 'g_ple': 2.348947e-02, 'w_ple_gate': 2.272743e-02, 'w_ple_proj': 5.738017e-02, 'g_final': 3.223297e+01}


def _to_microbatches(a, axis):
    t = _jnp.moveaxis(a, axis, 0)
    t = t.reshape((N_MICROBATCH, t.shape[0] // N_MICROBATCH) + t.shape[1:])
    return _jnp.moveaxis(t, 1, axis + 1)


def setup_inputs(seed: int = 0) -> dict:
    inp = _fwd_setup_inputs(seed)
    key = _jax.random.fold_in(_jax.random.key(seed), 7919)
    shape, _ = _output_shape()
    out = dict(inp)
    out["loss_target"] = _jax.random.normal(_jax.random.fold_in(key, 0), shape, _jnp.float32)
    for i, name in enumerate(TWIN_WEIGHTS):
        w = inp[name].astype(_jnp.float32)
        if MOMENT_SCALE is None:
            s = _jnp.sqrt(_jnp.mean(_jnp.square(w)) + 1e-30)
        else:
            s = MOMENT_SCALE[name]
        km, kv = _jax.random.split(_jax.random.fold_in(key, i + 1))
        out[name] = w
        out["m_" + name] = s * _jax.random.normal(km, w.shape, _jnp.float32)
        out["v_" + name] = (s * s) * _jax.random.uniform(kv, w.shape, _jnp.float32, 0.5, 1.5)
    if N_MICROBATCH > 1:
        for name, axis in PER_EXAMPLE_BATCH_AXIS.items():
            out[name] = _to_microbatches(out[name], axis)
    return {'x': out['x'], 'p': out['p'], 'positions': out['positions'], 'g_mix': out['g_mix'], 'w_in': out['w_in'], 'rpb': out['rpb'], 'w_branch_na': out['w_branch_na'], 'w_branch_dil': out['w_branch_dil'], 'w_out': out['w_out'], 'g_mlp': out['g_mlp'], 'w_up': out['w_up'], 'w_down': out['w_down'], 'g_ple': out['g_ple'], 'w_ple_gate': out['w_ple_gate'], 'w_ple_proj': out['w_ple_proj'], 'g_final': out['g_final'], 'loss_target': out['loss_target'], 'm_g_mix': out['m_g_mix'], 'm_w_in': out['m_w_in'], 'm_rpb': out['m_rpb'], 'm_w_branch_na': out['m_w_branch_na'], 'm_w_branch_dil': out['m_w_branch_dil'], 'm_w_out': out['m_w_out'], 'm_g_mlp': out['m_g_mlp'], 'm_w_up': out['m_w_up'], 'm_w_down': out['m_w_down'], 'm_g_ple': out['m_g_ple'], 'm_w_ple_gate': out['m_w_ple_gate'], 'm_w_ple_proj': out['m_w_ple_proj'], 'm_g_final': out['m_g_final'], 'v_g_mix': out['v_g_mix'], 'v_w_in': out['v_w_in'], 'v_rpb': out['v_rpb'], 'v_w_branch_na': out['v_w_branch_na'], 'v_w_branch_dil': out['v_w_branch_dil'], 'v_w_out': out['v_w_out'], 'v_g_mlp': out['v_g_mlp'], 'v_w_up': out['v_w_up'], 'v_w_down': out['v_w_down'], 'v_g_ple': out['v_g_ple'], 'v_w_ple_gate': out['v_w_ple_gate'], 'v_w_ple_proj': out['v_w_ple_proj'], 'v_g_final': out['v_g_final']}


def _loss(weights, diff, rest, loss_target):
    with _jax.named_scope("forward"):
        args = {**rest, TWIN_DIFF_INPUT: diff, **{k: w.astype(_WEIGHT_DTYPES[k]) for k, w in weights.items()}}
        y = _forward(args)
    with _jax.named_scope("loss_head"):
        err = _jnp.square(y.astype(_jnp.float32) - loss_target)
        return 0.5 * _jnp.sum(_jnp.mean(err, axis=-1)) if err.ndim else 0.5 * err


def _adamw(w, g, m, v):
    m = ADAM_B1 * m + (1.0 - ADAM_B1) * g
    v = ADAM_B2 * v + (1.0 - ADAM_B2) * _jnp.square(g)
    m_hat = m / (1.0 - ADAM_B1 ** ADAM_STEP)
    v_hat = v / (1.0 - ADAM_B2 ** ADAM_STEP)
    delta = -ADAM_LR * (m_hat / (_jnp.sqrt(v_hat) + ADAM_EPS) + ADAM_WD * w)
    return delta, m, v


def reference(x, p, positions, g_mix, w_in, rpb, w_branch_na, w_branch_dil, w_out, g_mlp, w_up, w_down, g_ple, w_ple_gate, w_ple_proj, g_final, loss_target, m_g_mix, m_w_in, m_rpb, m_w_branch_na, m_w_branch_dil, m_w_out, m_g_mlp, m_w_up, m_w_down, m_g_ple, m_w_ple_gate, m_w_ple_proj, m_g_final, v_g_mix, v_w_in, v_rpb, v_w_branch_na, v_w_branch_dil, v_w_out, v_g_mlp, v_w_up, v_w_down, v_g_ple, v_w_ple_gate, v_w_ple_proj, v_g_final):
    given = dict(x=x, p=p, positions=positions, g_mix=g_mix, w_in=w_in, rpb=rpb, w_branch_na=w_branch_na, w_branch_dil=w_branch_dil, w_out=w_out, g_mlp=g_mlp, w_up=w_up, w_down=w_down, g_ple=g_ple, w_ple_gate=w_ple_gate, w_ple_proj=w_ple_proj, g_final=g_final, loss_target=loss_target, m_g_mix=m_g_mix, m_w_in=m_w_in, m_rpb=m_rpb, m_w_branch_na=m_w_branch_na, m_w_branch_dil=m_w_branch_dil, m_w_out=m_w_out, m_g_mlp=m_g_mlp, m_w_up=m_w_up, m_w_down=m_w_down, m_g_ple=m_g_ple, m_w_ple_gate=m_w_ple_gate, m_w_ple_proj=m_w_ple_proj, m_g_final=m_g_final, v_g_mix=v_g_mix, v_w_in=v_w_in, v_rpb=v_rpb, v_w_branch_na=v_w_branch_na, v_w_branch_dil=v_w_branch_dil, v_w_out=v_w_out, v_g_mlp=v_g_mlp, v_w_up=v_w_up, v_w_down=v_w_down, v_g_ple=v_g_ple, v_w_ple_gate=v_w_ple_gate, v_w_ple_proj=v_w_ple_proj, v_g_final=v_g_final)
    weights = {n: given[n] for n in TWIN_WEIGHTS}
    shared = {n: given[n] for n in SHARED_INPUTS}
    per_example = {n: given[n] for n in ['x', 'p', 'positions']}
    grad_fn = _jax.value_and_grad(_loss, argnums=(0, 1))

    def one_microbatch(ex, loss_target):
        ex = dict(ex)
        diff = ex.pop(TWIN_DIFF_INPUT)
        return grad_fn(weights, diff, {**shared, **ex}, loss_target)

    if N_MICROBATCH == 1:
        loss, (grad_w, grad_x) = one_microbatch(per_example, given["loss_target"])
    else:
        def body(carry, xs):
            loss_sum, grad_sum = carry
            l_k, (gw_k, gx_k) = one_microbatch(xs[0], xs[1])
            with _jax.named_scope("update"):
                return (loss_sum + l_k, _jax.tree.map(_jnp.add, grad_sum, gw_k)), gx_k

        init = (_jnp.zeros((), _jnp.float32), _jax.tree.map(_jnp.zeros_like, weights))
        (loss, grad_w), grad_x = _jax.lax.scan(body, init, (per_example, given["loss_target"]))
    with _jax.named_scope("update"):
        delta_w, new_m, new_v = {}, {}, {}
        for n in TWIN_WEIGHTS:
            delta_w[n], new_m[n], new_v[n] = _adamw(weights[n], grad_w[n], given["m_" + n], given["v_" + n])
    return (loss, grad_x, *[grad_w[n] for n in TWIN_WEIGHTS], *[delta_w[n] for n in TWIN_WEIGHTS],
            *[new_m[n] for n in TWIN_WEIGHTS], *[new_v[n] for n in TWIN_WEIGHTS])
```

```python
import functools

import numpy as np
import jax
import jax.numpy as jnp
from jax import lax
from jax.experimental import pallas as pl
from jax.experimental.pallas import tpu as pltpu

F32 = jnp.float32
BF16 = jnp.bfloat16

D_MODEL = 1024
HEAD_DIM = 64
GRID_W = 64
NA_WIDTH = 512
DIL_WIDTH = 768
DIL_OUT = 256
D_FF = 4096
IN_WIDTH = 5888
DIL_DILATIONS = (1, 4, 16)
DIL_RADIUS = 64
NA_WIN_ROWS = 8
RMS_EPS = 1e-6
NEG_INF = -1e30
QK_SCALE = HEAD_DIM ** -0.5

ADAM_LR = 0.001
ADAM_B1 = 0.9
ADAM_B2 = 0.999
ADAM_EPS = 1e-08
ADAM_WD = 0.01
ADAM_STEP = 10

N_DEV = 8
VMEM_LIMIT = 56 * 1024 * 1024
MESH = pl.DeviceIdType.MESH

NT_DIMS = (((1,), (1,)), ((), ()))
TN_DIMS = (((0,), (0,)), ((), ()))


def _sds(shape, dtype):
    return jax.ShapeDtypeStruct(shape, dtype)


def _params(*sem):
    return pltpu.CompilerParams(dimension_semantics=sem, vmem_limit_bytes=VMEM_LIMIT)


def _rows(tm, width, col=0):
    return pl.BlockSpec((tm, width), lambda i, c=col: (i, c))


def _const(shape):
    zeros = (0,) * len(shape)
    return pl.BlockSpec(shape, lambda i: zeros)


def _matmul(a, b, *, ta=False, tb=False, out_dtype, tm, tn, tk, name):
    m, k = (a.shape[1], a.shape[0]) if ta else a.shape
    n = b.shape[0] if tb else b.shape[1]
    tm, tn, tk = min(tm, m), min(tn, n), min(tk, k)
    nk = k // tk
    dims = (((0 if ta else 1,), (1 if tb else 0,)), ((), ()))

    def body(a_ref, b_ref, o_ref, *acc):
        part = lax.dot_general(a_ref[...], b_ref[...], dims, preferred_element_type=F32)
        if nk == 1:
            o_ref[...] = part.astype(o_ref.dtype)
            return
        acc_ref, = acc
        kk = pl.program_id(2)

        @pl.when(kk == 0)
        def _():
            acc_ref[...] = part

        @pl.when(kk > 0)
        def _():
            acc_ref[...] += part

        @pl.when(kk == nk - 1)
        def _():
            o_ref[...] = acc_ref[...].astype(o_ref.dtype)

    a_spec = (pl.BlockSpec((tk, tm), lambda j, i, kk: (kk, i)) if ta
              else pl.BlockSpec((tm, tk), lambda j, i, kk: (i, kk)))
    b_spec = (pl.BlockSpec((tn, tk), lambda j, i, kk: (j, kk)) if tb
              else pl.BlockSpec((tk, tn), lambda j, i, kk: (kk, j)))
    return pl.pallas_call(
        body, name=name, grid=(n // tn, m // tm, nk),
        in_specs=[a_spec, b_spec],
        out_specs=pl.BlockSpec((tm, tn), lambda j, i, kk: (i, j)),
        out_shape=_sds((m, n), out_dtype),
        scratch_shapes=[] if nk == 1 else [pltpu.VMEM((tm, tn), F32)],
        compiler_params=_params("parallel", "parallel", "arbitrary"),
    )(a, b)


def _rstd(h):
    return lax.rsqrt(jnp.mean(h * h, axis=-1, keepdims=True) + RMS_EPS)


def _sigmoid(z):
    return 1.0 / (1.0 + jnp.exp(-z))


def _rms_fwd(x, g, *, tm, name):
    n = x.shape[0]

    def body(x_ref, g_ref, o_ref):
        h = x_ref[...]
        o_ref[...] = (h * _rstd(h) * g_ref[...]).astype(BF16)

    return pl.pallas_call(
        body, name=name, grid=(n // tm,),
        in_specs=[_rows(tm, D_MODEL), _const((1, D_MODEL))],
        out_specs=_rows(tm, D_MODEL), out_shape=_sds((n, D_MODEL), BF16),
        compiler_params=_params("parallel"),
    )(x, g)


def _swap_halves(t):
    width = t.shape[1]
    lane = lax.broadcasted_iota(jnp.int32, t.shape, 1)
    return jnp.where((lane & 63) < 32, pltpu.roll(t, width - 32, 1), pltpu.roll(t, 32, 1))


def _dil_spec(dil, tm):
    return pl.BlockSpec((dil, tm // dil, 256), lambda i: (0, i, 0))


def _dil_scratch(tm):
    return pltpu.VMEM((2, tm, 128), F32)


def _load_token_order(src, scr, dil, tm):
    if dil == 1:
        return src[0]
    for j in range(dil):
        for c in range(2):
            scr[c, pl.ds(j, tm // dil, stride=dil), :] = src[j, :, c * 128:(c + 1) * 128]
    return jnp.concatenate([scr[0], scr[1]], axis=1)


def _store_dil_order(val, dst, scr, dil, tm):
    if dil == 1:
        dst[0] = val.astype(dst.dtype)
        return
    for c in range(2):
        scr[c] = val[:, c * 128:(c + 1) * 128]
    for j in range(dil):
        for c in range(2):
            dst[j, :, c * 128:(c + 1) * 128] = scr[c, pl.ds(j, tm // dil, stride=dil), :].astype(dst.dtype)


def _split_proj(proj, cos_t, sin_t, *, tm, name):
    n = proj.shape[0]
    n_dil = len(DIL_DILATIONS)

    def body(*refs):
        na_in = refs[0:3]
        dil_in = refs[3:3 + 3 * n_dil]
        gate_in = refs[12:20]
        cos_ref, sin_ref = refs[20:22]
        outs = refs[22:]
        na_out = outs[0:3]
        dil_out = outs[3:12]
        sn_ref, sd_ref = outs[12:14]
        scr = outs[14]
        for t in range(3):
            na_out[t][...] = na_in[t][...].astype(BF16)
        cosv, sinv = cos_ref[...], sin_ref[...]
        for t in range(3):
            for gi, dil in enumerate(DIL_DILATIONS):
                val = dil_in[t * n_dil + gi][...]
                if t < 2:
                    val = val * cosv + _swap_halves(val) * sinv
                _store_dil_order(val, dil_out[t * n_dil + gi], scr, dil, tm)
        for c in range(4):
            sn_ref[:, c * 256:(c + 1) * 256] = _sigmoid(gate_in[c][...])
            sd_ref[:, c * 256:(c + 1) * 256] = _sigmoid(gate_in[4 + c][...])

    in_specs = [_rows(tm, NA_WIDTH, c) for c in range(3)]
    in_specs += [_rows(tm, 256, 6 + c) for c in range(9)]
    in_specs += [_rows(tm, 256, 15 + c) for c in range(8)]
    in_specs += [_rows(tm, 256), _rows(tm, 256)]
    out_specs = [_rows(tm, NA_WIDTH)] * 3
    out_shape = [_sds((n, NA_WIDTH), BF16)] * 3
    for _ in range(3):
        for dil in DIL_DILATIONS:
            out_specs.append(pl.BlockSpec((dil, tm // dil, 256), lambda i: (0, i, 0)))
            out_shape.append(_sds((dil, n // dil, 256), BF16))
    out_specs += [_rows(tm, D_MODEL)] * 2
    out_shape += [_sds((n, D_MODEL), F32)] * 2
    res = pl.pallas_call(
        body, name=name, grid=(n // tm,),
        in_specs=in_specs, out_specs=out_specs, out_shape=out_shape,
        scratch_shapes=[_dil_scratch(tm)],
        compiler_params=_params("parallel"),
    )(*([proj] * 20), cos_t, sin_t)
    return res[0:3], res[3:6], res[6:9], res[9:12], res[12], res[13]


def _gate_mix(sn, bn, sd, bd, *, tm, name):
    n = sn.shape[0]

    def body(sn_ref, bn_ref, sd_ref, bd_ref, o_ref):
        o_ref[...] = (sn_ref[...] * bn_ref[...] + sd_ref[...] * bd_ref[...]).astype(BF16)

    return pl.pallas_call(
        body, name=name, grid=(n // tm,), in_specs=[_rows(tm, D_MODEL)] * 4,
        out_specs=_rows(tm, D_MODEL), out_shape=_sds((n, D_MODEL), BF16),
        compiler_params=_params("parallel"),
    )(sn, bn, sd, bd)


def _residual_rms(h, delta, g, *, tm, name):
    n = h.shape[0]

    def body(h_ref, d_ref, g_ref, hn_ref, z_ref):
        hn = h_ref[...] + d_ref[...]
        hn_ref[...] = hn
        z_ref[...] = (hn * _rstd(hn) * g_ref[...]).astype(BF16)

    return pl.pallas_call(
        body, name=name, grid=(n // tm,),
        in_specs=[_rows(tm, D_MODEL), _rows(tm, D_MODEL), _const((1, D_MODEL))],
        out_specs=[_rows(tm, D_MODEL)] * 2,
        out_shape=[_sds((n, D_MODEL), F32), _sds((n, D_MODEL), BF16)],
        compiler_params=_params("parallel"),
    )(h, delta, g)


def _relu_sq(u, *, tm, name):
    n, w = u.shape

    def body(u_ref, f_ref):
        r = jnp.maximum(u_ref[...], 0.0)
        f_ref[...] = (r * r).astype(BF16)

    return pl.pallas_call(
        body, name=name, grid=(n // tm,), in_specs=[_rows(tm, w)],
        out_specs=_rows(tm, w), out_shape=_sds((n, w), BF16),
        compiler_params=_params("parallel"),
    )(u)


def _relu_sq_bwd(df, u, *, tm, name):
    n, w = u.shape

    def body(df_ref, u_ref, o_ref):
        o_ref[...] = (df_ref[...] * (2.0 * jnp.maximum(u_ref[...], 0.0))).astype(BF16)

    return pl.pallas_call(
        body, name=name, grid=(n // tm,), in_specs=[_rows(tm, w)] * 2,
        out_specs=_rows(tm, w), out_shape=_sds((n, w), BF16),
        compiler_params=_params("parallel"),
    )(df, u)


def _tail(h2, gt, pp, target, g_final, *, tm, name):
    n = h2.shape[0]

    def body(h2_ref, gt_ref, pp_ref, t_ref, g_ref, dh3_ref, dpp_ref, dgt_ref, dg_ref, loss_ref):
        i = pl.program_id(0)

        @pl.when(i == 0)
        def _():
            dg_ref[...] = jnp.zeros_like(dg_ref)
            loss_ref[...] = jnp.zeros_like(loss_ref)

        sg = _sigmoid(gt_ref[...])
        pp_v = pp_ref[...]
        h3 = h2_ref[...] + sg * pp_v
        r3 = _rstd(h3)
        n3 = h3 * r3
        g = g_ref[...]
        err = n3 * g - t_ref[...]
        loss_ref[...] += 0.5 * jnp.sum(jnp.sum(err * err, axis=-1, keepdims=True) / D_MODEL)
        dy = err / D_MODEL
        dg_ref[...] += jnp.sum(dy * n3, axis=0, keepdims=True)
        dn = dy * g
        dh3 = r3 * (dn - n3 * jnp.mean(dn * n3, axis=-1, keepdims=True))
        dh3_ref[...] = dh3
        dpp_ref[...] = (dh3 * sg).astype(BF16)
        dgt_ref[...] = (dh3 * pp_v * sg * (1.0 - sg)).astype(BF16)

    return pl.pallas_call(
        body, name=name, grid=(n // tm,),
        in_specs=[_rows(tm, D_MODEL)] * 4 + [_const((1, D_MODEL))],
        out_specs=[_rows(tm, D_MODEL)] * 3 + [_const((1, D_MODEL)), _const((1, 128))],
        out_shape=[_sds((n, D_MODEL), F32), _sds((n, D_MODEL), BF16), _sds((n, D_MODEL), BF16),
                   _sds((1, D_MODEL), F32), _sds((1, 128), F32)],
        compiler_params=_params("arbitrary"),
    )(h2, gt, pp, target, g_final)


def _rms_bwd(dz, h, g, dres, *, tm, name, want_bf16=True):
    n = h.shape[0]

    def body(dz_ref, h_ref, g_ref, dres_ref, dh_ref, *rest):
        if want_bf16:
            dhb_ref, dg_ref = rest
        else:
            dg_ref, = rest
        i = pl.program_id(0)

        @pl.when(i == 0)
        def _():
            dg_ref[...] = jnp.zeros_like(dg_ref)

        hv = h_ref[...]
        r = _rstd(hv)
        nrm = hv * r
        dz_v = dz_ref[...]
        dg_ref[...] += jnp.sum(dz_v * nrm, axis=0, keepdims=True)
        dn = dz_v * g_ref[...]
        dh = dres_ref[...] + r * (dn - nrm * jnp.mean(dn * nrm, axis=-1, keepdims=True))
        dh_ref[...] = dh
        if want_bf16:
            dhb_ref[...] = dh.astype(BF16)

    out_specs = [_rows(tm, D_MODEL)]
    out_shape = [_sds((n, D_MODEL), F32)]
    if want_bf16:
        out_specs.append(_rows(tm, D_MODEL))
        out_shape.append(_sds((n, D_MODEL), BF16))
    out_specs.append(_const((1, D_MODEL)))
    out_shape.append(_sds((1, D_MODEL), F32))
    return pl.pallas_call(
        body, name=name, grid=(n // tm,),
        in_specs=[_rows(tm, D_MODEL), _rows(tm, D_MODEL), _const((1, D_MODEL)), _rows(tm, D_MODEL)],
        out_specs=out_specs, out_shape=out_shape,
        compiler_params=_params("arbitrary"),
    )(dz, h, g, dres)


def _gate_bwd(dmixed, sn, bn, sd, bd, *, tm, name):
    n = sn.shape[0]

    def body(dm_ref, sn_ref, bn_ref, sd_ref, bd_ref, dbn_ref, dbd_ref, dgn_ref, dgd_ref):
        dm = dm_ref[...]
        s1, s2 = sn_ref[...], sd_ref[...]
        dbn_ref[...] = (dm * s1).astype(BF16)
        dbd_ref[...] = (dm * s2).astype(BF16)
        dgn_ref[...] = (dm * bn_ref[...] * s1 * (1.0 - s1)).astype(BF16)
        dgd_ref[...] = (dm * bd_ref[...] * s2 * (1.0 - s2)).astype(BF16)

    return pl.pallas_call(
        body, name=name, grid=(n // tm,), in_specs=[_rows(tm, D_MODEL)] * 5,
        out_specs=[_rows(tm, D_MODEL)] * 4, out_shape=[_sds((n, D_MODEL), BF16)] * 4,
        compiler_params=_params("parallel"),
    )(dmixed, sn, bn, sd, bd)


def _assemble_dproj(dna, ddil_q, ddil_k, ddil_v, dgn, dgd, cos_t, sin_t, *, tm, name):
    n = dgn.shape[0]

    def body(*refs):
        dq_ref, dk_ref, dv_ref = refs[0:3]
        dil_in = refs[3:12]
        dgn_ref, dgd_ref, cos_ref, sin_ref, o_ref, scr = refs[12:18]
        o_ref[:, 0:512] = dq_ref[...]
        o_ref[:, 512:1024] = dk_ref[...].astype(BF16)
        o_ref[:, 1024:1536] = dv_ref[...].astype(BF16)
        cosv, sinv = cos_ref[...], sin_ref[...]
        for t in range(3):
            for gi, dil in enumerate(DIL_DILATIONS):
                val = _load_token_order(dil_in[t * 3 + gi], scr, dil, tm)
                if t < 2:
                    val = val * cosv + _swap_halves(val * sinv)
                c0 = 1536 + t * DIL_WIDTH + gi * 256
                o_ref[:, c0:c0 + 256] = val.astype(BF16)
        o_ref[:, 3840:4864] = dgn_ref[...]
        o_ref[:, 4864:5888] = dgd_ref[...]

    in_specs = [_rows(tm, NA_WIDTH)] * 3
    for _ in range(3):
        for dil in DIL_DILATIONS:
            in_specs.append(pl.BlockSpec((dil, tm // dil, 256), lambda i: (0, i, 0)))
    in_specs += [_rows(tm, D_MODEL)] * 2 + [_rows(tm, 256)] * 2
    return pl.pallas_call(
        body, name=name, grid=(n // tm,), in_specs=in_specs,
        out_specs=_rows(tm, IN_WIDTH), out_shape=_sds((n, IN_WIDTH), BF16),
        scratch_shapes=[_dil_scratch(tm)],
        compiler_params=_params("parallel"),
    )(*dna, *ddil_q, *ddil_k, *ddil_v, dgn, dgd, cos_t, sin_t)


def _na_bias(rb_ref, bias_scr):
    shape = (GRID_W, NA_WIN_ROWS * GRID_W)
    qc = lax.broadcasted_iota(jnp.int32, shape, 0)
    kc = lax.broadcasted_iota(jnp.int32, shape, 1) & (GRID_W - 1)
    cs = jnp.clip(qc - 8, 0, GRID_W - 16)
    valid = (kc >= cs) & (kc < cs + 16)
    for hh in range(2):
        for di in range(NA_WIN_ROWS):
            t = jnp.broadcast_to(rb_ref[hh, di:di + 1, :], shape)
            t = pltpu.roll(t, shape[1] - 15, 1)
            for b in range(6):
                t = jnp.where(((qc >> b) & 1) == 1, pltpu.roll(t, 1 << b, 1), t)
            bias_scr[hh, di] = jnp.where(valid, t, NEG_INF)


def _na_scores(q_ref, k_ref, v_ref, bias_scr, r, n_rows):
    lane = lax.broadcasted_iota(jnp.int32, (GRID_W, 128), 1)
    rs = jnp.clip(r - NA_WIN_ROWS // 2, 0, n_rows - NA_WIN_ROWS)
    di = r - rs
    off = pl.multiple_of(rs * GRID_W, GRID_W)
    kw = k_ref[pl.ds(off, NA_WIN_ROWS * GRID_W), :]
    vw = v_ref[pl.ds(off, NA_WIN_ROWS * GRID_W), :]
    qr = q_ref[pl.ds(pl.multiple_of(r * GRID_W, GRID_W), GRID_W), :].astype(F32)
    qs = jnp.concatenate([jnp.where(lane < 64, qr, 0.0), jnp.where(lane >= 64, qr, 0.0)], axis=0).astype(BF16)
    s = lax.dot_general(qs, kw, NT_DIMS, preferred_element_type=F32) * QK_SCALE
    s = s + jnp.concatenate([bias_scr[0, di], bias_scr[1, di]], axis=0)
    m = jnp.max(s, axis=-1, keepdims=True)
    e = jnp.exp(s - m)
    p = e * (1.0 / jnp.sum(e, axis=-1, keepdims=True))
    return p, qs, kw, vw, off, di, lane


def _na_fwd(q, k, v, rb, *, name):
    n = q.shape[0]
    n_rows = n // GRID_W

    def body(q_ref, k_ref, v_ref, rb_ref, o_ref, bias_scr):
        _na_bias(rb_ref, bias_scr)

        def row(r, carry):
            p, _, _, vw, _, _, lane = _na_scores(q_ref, k_ref, v_ref, bias_scr, r, n_rows)
            o2 = jnp.dot(p.astype(BF16), vw, preferred_element_type=F32)
            o = jnp.where(lane < 64, o2[:GRID_W], o2[GRID_W:])
            o_ref[pl.ds(pl.multiple_of(r * GRID_W, GRID_W), GRID_W), :] = o.astype(BF16)
            return carry

        lax.fori_loop(0, n_rows, row, 0)

    col = pl.BlockSpec((n, 128), lambda h: (0, h))
    return pl.pallas_call(
        body, name=name, grid=(NA_WIDTH // 128,),
        in_specs=[col, col, col, pl.BlockSpec((2, NA_WIN_ROWS, 512), lambda h: (h, 0, 0))],
        out_specs=col, out_shape=_sds((n, NA_WIDTH), BF16),
        scratch_shapes=[pltpu.VMEM((2, NA_WIN_ROWS, GRID_W, 512), F32)],
        compiler_params=_params("parallel"),
    )(q, k, v, rb)


def _na_bwd(q, k, v, do, rb, *, name):
    n = q.shape[0]
    n_rows = n // GRID_W
    win = NA_WIN_ROWS * GRID_W

    def body(q_ref, k_ref, v_ref, do_ref, rb_ref, dq_ref, dk_ref, dv_ref, drb_ref, bias_scr, acc_scr):
        _na_bias(rb_ref, bias_scr)
        acc_scr[...] = jnp.zeros_like(acc_scr)
        dk_ref[...] = jnp.zeros_like(dk_ref)
        dv_ref[...] = jnp.zeros_like(dv_ref)

        def row(r, carry):
            p, qs, kw, vw, off, di, lane = _na_scores(q_ref, k_ref, v_ref, bias_scr, r, n_rows)
            rows_r = pl.ds(pl.multiple_of(r * GRID_W, GRID_W), GRID_W)
            dor = do_ref[rows_r, :].astype(F32)
            dos = jnp.concatenate([jnp.where(lane < 64, dor, 0.0), jnp.where(lane >= 64, dor, 0.0)],
                                  axis=0).astype(BF16)
            dp = lax.dot_general(dos, vw, NT_DIMS, preferred_element_type=F32)
            ds = p * (dp - jnp.sum(p * dp, axis=-1, keepdims=True))
            acc_scr[0, di] += ds[:GRID_W]
            acc_scr[1, di] += ds[GRID_W:]
            dsb = ds.astype(BF16)
            dq2 = jnp.dot(dsb, kw, preferred_element_type=F32)
            dq_ref[rows_r, :] = (jnp.where(lane < 64, dq2[:GRID_W], dq2[GRID_W:]) * QK_SCALE).astype(BF16)
            dk_ref[pl.ds(off, win), :] += lax.dot_general(dsb, qs, TN_DIMS, preferred_element_type=F32) * QK_SCALE
            dv_ref[pl.ds(off, win), :] += lax.dot_general(p.astype(BF16), dos, TN_DIMS, preferred_element_type=F32)
            return carry

        lax.fori_loop(0, n_rows, row, 0)

        qc = lax.broadcasted_iota(jnp.int32, (GRID_W, win), 0)
        for hh in range(2):
            for di in range(NA_WIN_ROWS):
                t = acc_scr[hh, di]
                for b in range(6):
                    t = jnp.where(((qc >> b) & 1) == 1, pltpu.roll(t, win - (1 << b), 1), t)
                t = pltpu.roll(t, 15, 1)
                drb_ref[hh, :, di * win:(di + 1) * win] = jnp.sum(t, axis=0, keepdims=True)

    col = pl.BlockSpec((n, 128), lambda h: (0, h))
    return pl.pallas_call(
        body, name=name, grid=(NA_WIDTH // 128,),
        in_specs=[col, col, col, col, pl.BlockSpec((2, NA_WIN_ROWS, 512), lambda h: (h, 0, 0))],
        out_specs=[col, col, col, pl.BlockSpec((2, 1, NA_WIN_ROWS * win), lambda h: (h, 0, 0))],
        out_shape=[_sds((n, NA_WIDTH), BF16), _sds((n, NA_WIDTH), F32), _sds((n, NA_WIDTH), F32),
                   _sds((8, 1, NA_WIN_ROWS * win), F32)],
        scratch_shapes=[pltpu.VMEM((2, NA_WIN_ROWS, GRID_W, win), F32),
                        pltpu.VMEM((2, NA_WIN_ROWS, GRID_W, win), F32)],
        compiler_params=_params("parallel"),
    )(q, k, v, do, rb)


def _rpb_table(rpb2):
    ro = np.arange(NA_WIN_ROWS)[None, :] - np.arange(NA_WIN_ROWS)[:, None] + (NA_WIN_ROWS - 1)
    t = rpb2[:, ro, :]
    t = jnp.pad(t, ((0, 0), (0, 0), (0, 0), (0, GRID_W - t.shape[-1])))
    return t.reshape(8, NA_WIN_ROWS, NA_WIN_ROWS * GRID_W)


def _rpb_grad(drb, *, name):
    kdim = drb.shape[1]

    def body(x_ref, o_ref):
        kk = lax.broadcasted_iota(jnp.int32, (512, 512), 0)
        jj = lax.broadcasted_iota(jnp.int32, (512, 512), 1)
        wi, co = kk >> 6, kk & 63
        acc = jnp.zeros((8, 512), F32)
        for di in range(NA_WIN_ROWS):
            hit = ((wi - di + (NA_WIN_ROWS - 1)) == (jj >> 5)) & (co == (jj & 31)) & (co < 31)
            onehot = jnp.where(hit, 1.0, 0.0).astype(F32)
            acc = acc + jnp.dot(x_ref[:, di * 512:(di + 1) * 512], onehot, preferred_element_type=F32,
                                precision=lax.Precision.HIGHEST)
        o_ref[...] = acc

    return pl.pallas_call(
        body, name=name, grid=(1,),
        in_specs=[_const((8, kdim))], out_specs=_const((8, 512)), out_shape=_sds((8, 512), F32),
        compiler_params=_params("arbitrary"),
    )(drb)


def _dil_blocks(length):
    qb = min(128, length)
    return qb, min(qb + 2 * DIL_RADIUS, length)


def _dil_scores(q_ref, k_ref, v_ref, i, qb, win, length):
    start = pl.multiple_of(jnp.clip(i * qb - DIL_RADIUS, 0, length - win), DIL_RADIUS)
    kw = k_ref[0, pl.ds(start, win), :]
    vw = v_ref[0, pl.ds(start, win), :]
    qv = q_ref[0].astype(F32)
    lane = lax.broadcasted_iota(jnp.int32, (qb, 256), 1)
    qs = jnp.concatenate([jnp.where((lane >> 6) == h, qv, 0.0) for h in range(4)], axis=0).astype(BF16)
    s = lax.dot_general(qs, kw, NT_DIMS, preferred_element_type=F32) * QK_SCALE
    qi = i * qb + (lax.broadcasted_iota(jnp.int32, (4 * qb, win), 0) & (qb - 1))
    kj = start + lax.broadcasted_iota(jnp.int32, (4 * qb, win), 1)
    s = jnp.where(jnp.abs(qi - kj) <= DIL_RADIUS, s, NEG_INF)
    return s, qs, kw, vw, start, lane


def _pick_heads(stacked, lane, qb):
    out = jnp.zeros((qb, 256), stacked.dtype)
    for h in range(4):
        out = jnp.where((lane >> 6) == h, stacked[h * qb:(h + 1) * qb], out)
    return out


def _stack_head_cols(t, qb):
    return jnp.concatenate([t[:, 64 * h:64 * h + 1] for h in range(4)], axis=0)


def _dil_fwd(q, k, v, *, name):
    dil, length, _ = q.shape
    qb, win = _dil_blocks(length)

    def body(q_ref, k_ref, v_ref, o_ref, lse_ref):
        i = pl.program_id(1)
        s, _, _, vw, _, lane = _dil_scores(q_ref, k_ref, v_ref, i, qb, win, length)
        m = jnp.max(s, axis=-1, keepdims=True)
        lse = m + jnp.log(jnp.sum(jnp.exp(s - m), axis=-1, keepdims=True))
        p = jnp.exp(s - lse)
        o4 = jnp.dot(p.astype(BF16), vw, preferred_element_type=F32)
        o_ref[0] = _pick_heads(o4, lane, qb)
        lse_ref[0] = _pick_heads(jnp.broadcast_to(lse, (4 * qb, 256)), lane, qb)

    seq = pl.BlockSpec((1, length, 256), lambda j, i: (j, 0, 0))
    blk = pl.BlockSpec((1, qb, 256), lambda j, i: (j, i, 0))
    return pl.pallas_call(
        body, name=name, grid=(dil, length // qb),
        in_specs=[blk, seq, seq], out_specs=[blk, blk],
        out_shape=[_sds((dil, length, 256), F32)] * 2,
        compiler_params=_params("parallel", "parallel"),
    )(q, k, v)


def _dil_bwd(q, k, v, do, lse, cc, *, name):
    dil, length, _ = q.shape
    qb, win = _dil_blocks(length)

    def body(q_ref, k_ref, v_ref, do_ref, lse_ref, cc_ref, dq_ref, dk_ref, dv_ref):
        i = pl.program_id(1)

        @pl.when(i == 0)
        def _():
            dk_ref[...] = jnp.zeros_like(dk_ref)
            dv_ref[...] = jnp.zeros_like(dv_ref)

        s, qs, kw, vw, start, lane = _dil_scores(q_ref, k_ref, v_ref, i, qb, win, length)
        p = jnp.exp(s - _stack_head_cols(lse_ref[0], qb))
        dov = do_ref[0].astype(F32)
        dos = jnp.concatenate([jnp.where((lane >> 6) == h, dov, 0.0) for h in range(4)], axis=0).astype(BF16)
        dp = lax.dot_general(dos, vw, NT_DIMS, preferred_element_type=F32)
        ds = p * (dp + _stack_head_cols(cc_ref[0], qb))
        dsb = ds.astype(BF16)
        dq4 = jnp.dot(dsb, kw, preferred_element_type=F32)
        dq_ref[0] = _pick_heads(dq4, lane, qb) * QK_SCALE
        dk_ref[0, pl.ds(start, win), :] += lax.dot_general(dsb, qs, TN_DIMS, preferred_element_type=F32) * QK_SCALE
        dv_ref[0, pl.ds(start, win), :] += lax.dot_general(p.astype(BF16), dos, TN_DIMS, preferred_element_type=F32)

    seq = pl.BlockSpec((1, length, 256), lambda j, i: (j, 0, 0))
    blk = pl.BlockSpec((1, qb, 256), lambda j, i: (j, i, 0))
    return pl.pallas_call(
        body, name=name, grid=(dil, length // qb),
        in_specs=[blk, seq, seq, blk, blk, blk], out_specs=[blk, seq, seq],
        out_shape=[_sds((dil, length, 256), F32)] * 3,
        compiler_params=_params("parallel", "arbitrary"),
    )(q, k, v, do, lse, cc)


def _merge_weights(lses):
    m = jnp.maximum(jnp.maximum(lses[0], lses[1]), lses[2])
    es = [jnp.exp(t - m) for t in lses]
    inv = 1.0 / (es[0] + es[1] + es[2])
    return [e * inv for e in es]


def _dil_merge(outs, lses, *, tm, name):
    n = outs[0].shape[1]

    def body(*refs):
        o_in, l_in = refs[0:3], refs[3:6]
        y_ref, yb_ref, scr = refs[6:9]
        lv = [_load_token_order(l_in[g], scr, d, tm) for g, d in enumerate(DIL_DILATIONS)]
        ws = _merge_weights(lv)
        y = jnp.zeros((tm, 256), F32)
        for g, d in enumerate(DIL_DILATIONS):
            y = y + ws[g] * _load_token_order(o_in[g], scr, d, tm)
        y_ref[...] = y
        yb_ref[...] = y.astype(BF16)

    specs = [_dil_spec(d, tm) for d in DIL_DILATIONS]
    return pl.pallas_call(
        body, name=name, grid=(n // tm,), in_specs=specs + specs,
        out_specs=[_rows(tm, 256)] * 2, out_shape=[_sds((n, 256), F32), _sds((n, 256), BF16)],
        scratch_shapes=[_dil_scratch(tm)],
        compiler_params=_params("parallel"),
    )(*outs, *lses)


def _dil_merge_bwd(dy, y, lses, *, tm, name):
    n = dy.shape[0]

    def body(*refs):
        dy_ref, y_ref = refs[0:2]
        l_in = refs[2:5]
        do_out, cc_out = refs[5:8], refs[8:11]
        scr = refs[11]
        lv = [_load_token_order(l_in[g], scr, d, tm) for g, d in enumerate(DIL_DILATIONS)]
        ws = _merge_weights(lv)
        dyv = dy_ref[...]
        rr = lax.broadcasted_iota(jnp.int32, (256, 256), 0) >> 6
        cc = lax.broadcasted_iota(jnp.int32, (256, 256), 1) >> 6
        ones = jnp.where(rr == cc, 1.0, 0.0).astype(F32)
        tsum = jnp.dot(dyv * y_ref[...], ones, preferred_element_type=F32,
                       precision=lax.Precision.HIGHEST)
        for g, d in enumerate(DIL_DILATIONS):
            _store_dil_order(ws[g] * dyv, do_out[g], scr, d, tm)
            _store_dil_order(-ws[g] * tsum, cc_out[g], scr, d, tm)

    specs = [_dil_spec(d, tm) for d in DIL_DILATIONS]
    res = pl.pallas_call(
        body, name=name, grid=(n // tm,),
        in_specs=[_rows(tm, 256)] * 2 + specs,
        out_specs=specs + specs,
        out_shape=[_sds((d, n // d, 256), BF16) for d in DIL_DILATIONS]
                  + [_sds((d, n // d, 256), F32) for d in DIL_DILATIONS],
        scratch_shapes=[_dil_scratch(tm)],
        compiler_params=_params("parallel"),
    )(dy, y, *lses)
    return res[0:3], res[3:6]


def _local_step(x, p_bf16, positions, target, g_mix, g_mlp, g_ple, g_final, rpb2,
                w_in, w_bna, w_bd, w_out, w_up, w_down, w_pg, w_pp):
    n = x.shape[0]
    tm = 256
    half = HEAD_DIM // 2
    inv_freq = 10000.0 ** (-jnp.arange(half, dtype=F32) / half)
    ang = positions.astype(F32)[:, None] * inv_freq
    cos, sin = jnp.cos(ang), jnp.sin(ang)
    cos_t = jnp.tile(jnp.concatenate([cos, cos], axis=-1), (1, 4))
    sin_t = jnp.tile(jnp.concatenate([-sin, sin], axis=-1), (1, 4))
    rb = _rpb_table(rpb2)

    a = _rms_fwd(x, g_mix, tm=tm, name="rms_mix")
    proj = _matmul(a, w_in, out_dtype=F32, tm=512, tn=2944, tk=1024, name="mm_in")
    na_qkv, dq_g, dk_g, dv_g, sn, sd = _split_proj(proj, cos_t, sin_t, tm=tm, name="split_proj")
    y_na = _na_fwd(*na_qkv, rb, name="na_fwd")
    d_out, d_lse = [], []
    for g in range(3):
        o, lse = _dil_fwd(dq_g[g], dk_g[g], dv_g[g], name=f"dil_fwd{g}")
        d_out.append(o)
        d_lse.append(lse)
    y_dil, y_dil_b = _dil_merge(d_out, d_lse, tm=tm, name="dil_merge")
    bn = _matmul(y_na, w_bna, out_dtype=F32, tm=512, tn=1024, tk=512, name="mm_bna")
    bd = _matmul(y_dil_b, w_bd, out_dtype=F32, tm=512, tn=1024, tk=256, name="mm_bd")
    mixed = _gate_mix(sn, bn, sd, bd, tm=tm, name="gate_mix")
    mo = _matmul(mixed, w_out, out_dtype=F32, tm=512, tn=1024, tk=1024, name="mm_out")
    h1, c = _residual_rms(x, mo, g_mlp, tm=tm, name="res_rms_mlp")
    u = _matmul(c, w_up, out_dtype=F32, tm=512, tn=2048, tk=1024, name="mm_up")
    f = _relu_sq(u, tm=128, name="relu_sq")
    dn = _matmul(f, w_down, out_dtype=F32, tm=512, tn=1024, tk=2048, name="mm_down")
    h2, e = _residual_rms(h1, dn, g_ple, tm=tm, name="res_rms_ple")
    gt = _matmul(e, w_pg, out_dtype=F32, tm=512, tn=1024, tk=1024, name="mm_pg")
    pp = _matmul(p_bf16, w_pp, out_dtype=F32, tm=512, tn=1024, tk=256, name="mm_pp")

    dh3, dpp, dgt, dg_final, loss = _tail(h2, gt, pp, target, g_final, tm=tm, name="tail")
    gw_pp = _matmul(p_bf16, dpp, ta=True, out_dtype=BF16, tm=256, tn=1024, tk=512, name="mm_gw_pp")
    gw_pg = _matmul(e, dgt, ta=True, out_dtype=BF16, tm=512, tn=1024, tk=512, name="mm_gw_pg")
    de = _matmul(dgt, w_pg, tb=True, out_dtype=F32, tm=512, tn=1024, tk=1024, name="mm_de")
    dh2, dh2_b, dg_ple = _rms_bwd(de, h2, g_ple, dh3, tm=tm, name="rms_bwd_ple")
    df = _matmul(dh2_b, w_down, tb=True, out_dtype=F32, tm=512, tn=2048, tk=1024, name="mm_df")
    gw_down = _matmul(f, dh2_b, ta=True, out_dtype=BF16, tm=1024, tn=1024, tk=512, name="mm_gw_down")
    du = _relu_sq_bwd(df, u, tm=128, name="relu_sq_bwd")
    gw_up = _matmul(c, du, ta=True, out_dtype=BF16, tm=512, tn=2048, tk=512, name="mm_gw_up")
    dc = _matmul(du, w_up, tb=True, out_dtype=F32, tm=512, tn=1024, tk=2048, name="mm_dc")
    dh1, dh1_b, dg_mlp = _rms_bwd(dc, h1, g_mlp, dh2, tm=tm, name="rms_bwd_mlp")
    dmixed = _matmul(dh1_b, w_out, tb=True, out_dtype=F32, tm=512, tn=1024, tk=1024, name="mm_dmixed")
    gw_out = _matmul(mixed, dh1_b, ta=True, out_dtype=BF16, tm=512, tn=1024, tk=512, name="mm_gw_out")
    dbn, dbd, dgn, dgd = _gate_bwd(dmixed, sn, bn, sd, bd, tm=tm, name="gate_bwd")
    gw_bna = _matmul(y_na, dbn, ta=True, out_dtype=BF16, tm=512, tn=1024, tk=512, name="mm_gw_bna")
    dy_na = _matmul(dbn, w_bna, tb=True, out_dtype=BF16, tm=512, tn=512, tk=1024, name="mm_dy_na")
    gw_bd = _matmul(y_dil_b, dbd, ta=True, out_dtype=BF16, tm=256, tn=1024, tk=512, name="mm_gw_bd")
    dy_dil = _matmul(dbd, w_bd, tb=True, out_dtype=F32, tm=512, tn=256, tk=1024, name="mm_dy_dil")
    dna = _na_bwd(*na_qkv, dy_na, rb, name="na_bwd")
    drpb = _rpb_grad(dna[3].reshape(8, -1), name="rpb_grad")
    do_g, cc_g = _dil_merge_bwd(dy_dil, y_dil, d_lse, tm=tm, name="dil_merge_bwd")
    ddq, ddk, ddv = [], [], []
    for g in range(3):
        r = _dil_bwd(dq_g[g], dk_g[g], dv_g[g], do_g[g], d_lse[g], cc_g[g], name=f"dil_bwd{g}")
        ddq.append(r[0])
        ddk.append(r[1])
        ddv.append(r[2])
    dproj = _assemble_dproj(dna[0:3], ddq, ddk, ddv, dgn, dgd, cos_t, sin_t, tm=tm, name="assemble_dproj")
    gw_in = _matmul(a, dproj, ta=True, out_dtype=BF16, tm=512, tn=2944, tk=512, name="mm_gw_in")
    da = _matmul(dproj, w_in, tb=True, out_dtype=F32, tm=512, tn=1024, tk=2944, name="mm_da")
    dx, dg_mix = _rms_bwd(da, x, g_mix, dh1, tm=tm, name="rms_bwd_mix", want_bf16=False)
    return (loss, dx, (gw_in, gw_bna, gw_bd, gw_out, gw_up, gw_down, gw_pg, gw_pp),
            (dg_mix, dg_mlp, dg_ple, dg_final), drpb)


def _my_index():
    return 4 * lax.axis_index("x") + 2 * lax.axis_index("y") + lax.axis_index("c")


def _peer(k):
    x, y, c = lax.axis_index("x"), lax.axis_index("y"), lax.axis_index("c")
    px = 1 - x if k & 4 else x
    py = 1 - y if k & 2 else y
    pc = 1 - c if k & 1 else c
    return (px, py, pc), 4 * px + 2 * py + pc


_WEIGHTS = (("w_in", 1, 736), ("w_branch_na", 1, 128), ("w_branch_dil", 1, 128), ("w_out", 0, 128),
            ("w_up", 1, 512), ("w_down", 0, 512), ("w_ple_gate", 0, 128), ("w_ple_proj", 1, 128))


def _all_gather_weights(shards):
    n_w = len(shards)

    def body(*refs):
        ins, outs, stage = refs[0:n_w], refs[n_w:2 * n_w], refs[2 * n_w:3 * n_w]
        send_sems, recv_sems, local_sems = refs[3 * n_w:]
        me = _my_index()
        for w in range(n_w):
            stage[w][...] = ins[w][...].astype(BF16)
        for w in range(n_w):
            pltpu.make_async_copy(stage[w], outs[w].at[me], local_sems.at[w]).start()
            for k in range(1, N_DEV):
                dev, _ = _peer(k)
                pltpu.make_async_remote_copy(
                    src_ref=stage[w], dst_ref=outs[w].at[me],
                    send_sem=send_sems.at[w * 7 + k - 1], recv_sem=recv_sems.at[w * 7 + k - 1],
                    device_id=dev, device_id_type=MESH).start()
        for w in range(n_w):
            for k in range(1, N_DEV):
                dev, idx = _peer(k)
                pltpu.make_async_remote_copy(
                    src_ref=stage[w], dst_ref=outs[w].at[idx],
                    send_sem=send_sems.at[w * 7 + k - 1], recv_sem=recv_sems.at[w * 7 + k - 1],
                    device_id=dev, device_id_type=MESH).wait()
            pltpu.make_async_copy(stage[w], outs[w].at[me], local_sems.at[w]).wait()

    return pl.pallas_call(
        body, name="all_gather_weights",
        in_specs=[pl.BlockSpec(memory_space=pltpu.VMEM)] * n_w,
        out_specs=[pl.BlockSpec(memory_space=pl.ANY)] * n_w,
        out_shape=[_sds((N_DEV,) + s.shape, BF16) for s in shards],
        scratch_shapes=[pltpu.VMEM(s.shape, BF16) for s in shards]
                       + [pltpu.SemaphoreType.DMA((n_w * 7,)), pltpu.SemaphoreType.DMA((n_w * 7,)),
                          pltpu.SemaphoreType.DMA((n_w,))],
        compiler_params=pltpu.CompilerParams(vmem_limit_bytes=VMEM_LIMIT),
    )(*shards)


def _exchange_grads(chunked, small):
    n_w = len(chunked)
    n_c = n_w + 1

    def body(*refs):
        ins, small_ref = refs[0:n_w], refs[n_w]
        outs, small_out = refs[n_c:n_c + n_w], refs[n_c + n_w]
        send_sems, recv_sems, local_sems = refs[2 * n_c:]
        me = _my_index()
        for w in range(n_w):
            pltpu.make_async_copy(ins[w].at[me], outs[w].at[0], local_sems.at[w]).start()
        pltpu.make_async_copy(small_ref, small_out.at[me], local_sems.at[n_w]).start()

        def copies(k):
            dev, idx = _peer(k)
            cps = [pltpu.make_async_remote_copy(
                src_ref=ins[w].at[idx], dst_ref=outs[w].at[k],
                send_sem=send_sems.at[w * 7 + k - 1], recv_sem=recv_sems.at[w * 7 + k - 1],
                device_id=dev, device_id_type=MESH) for w in range(n_w)]
            cps.append(pltpu.make_async_remote_copy(
                src_ref=small_ref, dst_ref=small_out.at[me],
                send_sem=send_sems.at[n_w * 7 + k - 1], recv_sem=recv_sems.at[n_w * 7 + k - 1],
                device_id=dev, device_id_type=MESH))
            return cps, idx

        for k in range(1, N_DEV):
            for cp in copies(k)[0]:
                cp.start()
        for k in range(1, N_DEV):
            cps, idx = copies(k)
            for cp in cps[:-1]:
                cp.wait()
            pltpu.make_async_remote_copy(
                src_ref=small_ref, dst_ref=small_out.at[idx],
                send_sem=send_sems.at[n_w * 7 + k - 1], recv_sem=recv_sems.at[n_w * 7 + k - 1],
                device_id=_peer(k)[0], device_id_type=MESH).wait()
        for w in range(n_w):
            pltpu.make_async_copy(ins[w].at[me], outs[w].at[0], local_sems.at[w]).wait()
        pltpu.make_async_copy(small_ref, small_out.at[me], local_sems.at[n_w]).wait()

    return pl.pallas_call(
        body, name="exchange_grads",
        in_specs=[pl.BlockSpec(memory_space=pl.ANY)] * n_c,
        out_specs=[pl.BlockSpec(memory_space=pl.ANY)] * n_c,
        out_shape=[_sds(t.shape, BF16) for t in chunked] + [_sds((N_DEV,) + small.shape, F32)],
        scratch_shapes=[pltpu.SemaphoreType.DMA((n_c * 7,)), pltpu.SemaphoreType.DMA((n_c * 7,)),
                        pltpu.SemaphoreType.DMA((n_c,))],
        compiler_params=pltpu.CompilerParams(vmem_limit_bytes=VMEM_LIMIT),
    )(*chunked, small)


def _adamw(w, g, m, v):
    m = ADAM_B1 * m + (1.0 - ADAM_B1) * g
    v = ADAM_B2 * v + (1.0 - ADAM_B2) * (g * g)
    m_hat = m / (1.0 - ADAM_B1 ** ADAM_STEP)
    v_hat = v / (1.0 - ADAM_B2 ** ADAM_STEP)
    delta = -ADAM_LR * (m_hat / (jnp.sqrt(v_hat) + ADAM_EPS) + ADAM_WD * w)
    return delta, m, v


def _sum_adamw(parts, w, m, v, *, tr, name):
    rows, cols = w.shape

    def body(p_ref, w_ref, m_ref, v_ref, g_ref, d_ref, nm_ref, nv_ref):
        g = p_ref[0].astype(F32)
        for s in range(1, N_DEV):
            g = g + p_ref[s].astype(F32)
        g_ref[...] = g
        d_ref[...], nm_ref[...], nv_ref[...] = _adamw(w_ref[...], g, m_ref[...], v_ref[...])

    blk = pl.BlockSpec((tr, cols), lambda i: (i, 0))
    return pl.pallas_call(
        body, name=name, grid=(rows // tr,),
        in_specs=[pl.BlockSpec((N_DEV, tr, cols), lambda i: (0, i, 0)), blk, blk, blk],
        out_specs=[blk] * 4, out_shape=[_sds((rows, cols), F32)] * 4,
        compiler_params=_params("parallel"),
    )(parts, w, m, v)


_RPB_SIZE = 8 * 15 * 31


def _pack_small(g_mix, g_mlp, g_ple, g_final, rpb, loss_row):
    flat = jnp.concatenate([g_mix.reshape(-1), g_mlp.reshape(-1), g_ple.reshape(-1), g_final.reshape(-1),
                            rpb.reshape(-1), jnp.zeros((3840 - _RPB_SIZE,), F32), loss_row.reshape(-1),
                            jnp.zeros((128,), F32)])
    return flat.reshape(64, 128)


def _unpack_small(t):
    flat = t.reshape(-1)
    return (flat[0:1024].reshape(1, 1024), flat[4096:4096 + _RPB_SIZE].reshape(1, 8, 15, 31),
            flat[1024:2048].reshape(1, 1024), flat[2048:3072].reshape(1, 1024), flat[3072:4096])


def kernel(x, p, positions, g_mix, w_in, rpb, w_branch_na, w_branch_dil, w_out, g_mlp, w_up, w_down, g_ple, w_ple_gate, w_ple_proj, g_final, loss_target, m_g_mix, m_w_in, m_rpb, m_w_branch_na, m_w_branch_dil, m_w_out, m_g_mlp, m_w_up, m_w_down, m_g_ple, m_w_ple_gate, m_w_ple_proj, m_g_final, v_g_mix, v_w_in, v_rpb, v_w_branch_na, v_w_branch_dil, v_w_out, v_g_mlp, v_w_up, v_w_down, v_g_ple, v_w_ple_gate, v_w_ple_proj, v_g_final):
    sharded = dict(w_in=(w_in, m_w_in, v_w_in), w_branch_na=(w_branch_na, m_w_branch_na, v_w_branch_na),
                   w_branch_dil=(w_branch_dil, m_w_branch_dil, v_w_branch_dil), w_out=(w_out, m_w_out, v_w_out),
                   w_up=(w_up, m_w_up, v_w_up), w_down=(w_down, m_w_down, v_w_down),
                   w_ple_gate=(w_ple_gate, m_w_ple_gate, v_w_ple_gate),
                   w_ple_proj=(w_ple_proj, m_w_ple_proj, v_w_ple_proj))
    shards = {k: tuple(t[0] for t in val) for k, val in sharded.items()}

    gathered = _all_gather_weights([shards[name][0] for name, _, _ in _WEIGHTS])
    full = []
    for (name, axis, _), g in zip(_WEIGHTS, gathered):
        if axis == 0:
            full.append(g.reshape(-1, g.shape[2]))
        else:
            full.append(jnp.transpose(g, (1, 0, 2)).reshape(g.shape[1], -1))

    loss, dx, gws, dgs, drpb = _local_step(
        x[0], p[0, 0].astype(BF16), positions[0], loss_target[0],
        g_mix, g_mlp, g_ple, g_final.reshape(1, -1), rpb[0], *full)

    chunked = []
    for (name, axis, width), gw in zip(_WEIGHTS, gws):
        if axis == 0:
            chunked.append(gw.reshape(N_DEV, width, gw.shape[1]))
        else:
            chunked.append(jnp.transpose(gw.reshape(gw.shape[0], N_DEV, width), (1, 0, 2)))
    drpb3 = drpb.reshape(8, 16, 32)[:, :15, :31]
    small = _pack_small(dgs[0], dgs[1], dgs[2], dgs[3], drpb3, loss)
    exchanged = _exchange_grads(chunked, small)

    out = {}
    for (name, _, _), parts in zip(_WEIGHTS, exchanged[:-1]):
        w, m, v = shards[name]
        res = _sum_adamw(parts, w, m, v, tr=min(128, w.shape[0]), name="adamw_" + name)
        out[name] = [t[None] for t in res]
    small_w = _pack_small(g_mix, g_mlp, g_ple, g_final, rpb, jnp.zeros((128,), F32))
    small_m = _pack_small(m_g_mix, m_g_mlp, m_g_ple, m_g_final, m_rpb, jnp.zeros((128,), F32))
    small_v = _pack_small(v_g_mix, v_g_mlp, v_g_ple, v_g_final, v_rpb, jnp.zeros((128,), F32))
    res = _sum_adamw(exchanged[-1], small_w, small_m, small_v, tr=64, name="adamw_small")
    unpacked = [_unpack_small(t) for t in res]
    for i, name in enumerate(("g_mix", "rpb", "g_mlp", "g_ple", "g_final")):
        out[name] = [u[i] for u in unpacked]
    loss_total = res[0][62, 0]

    order = ("g_mix", "w_in", "rpb", "w_branch_na", "w_branch_dil", "w_out", "g_mlp", "w_up", "w_down",
             "g_ple", "w_ple_gate", "w_ple_proj", "g_final")
    grads = [out[k][0] for k in order]
    deltas = [out[k][1] for k in order]
    new_m = [out[k][2] for k in order]
    new_v = [out[k][3] for k in order]
    return (loss_total, dx[None], *grads, *deltas, *new_m, *new_v)
```

```python
import functools

import numpy as np
import jax
import jax.numpy as jnp
from jax import lax
from jax.experimental import pallas as pl
from jax.experimental.pallas import tpu as pltpu

F32 = jnp.float32
BF16 = jnp.bfloat16

D_MODEL = 1024
HEAD_DIM = 64
GRID_W = 64
NA_WIDTH = 512
DIL_WIDTH = 768
DIL_OUT = 256
D_FF = 4096
IN_WIDTH = 5888
DIL_DILATIONS = (1, 4, 16)
DIL_RADIUS = 64
NA_WIN_ROWS = 8
RMS_EPS = 1e-6
NEG_INF = -1e30
QK_SCALE = HEAD_DIM ** -0.5

ADAM_LR = 0.001
ADAM_B1 = 0.9
ADAM_B2 = 0.999
ADAM_EPS = 1e-08
ADAM_WD = 0.01
ADAM_STEP = 10

N_DEV = 8
VMEM_LIMIT = 56 * 1024 * 1024
MESH = pl.DeviceIdType.MESH

NT_DIMS = (((1,), (1,)), ((), ()))
TN_DIMS = (((0,), (0,)), ((), ()))


def _sds(shape, dtype):
    return jax.ShapeDtypeStruct(shape, dtype)


def _params(*sem):
    return pltpu.CompilerParams(dimension_semantics=sem, vmem_limit_bytes=VMEM_LIMIT)


def _rows(tm, width, col=0):
    return pl.BlockSpec((tm, width), lambda i, c=col: (i, c))


def _const(shape):
    zeros = (0,) * len(shape)
    return pl.BlockSpec(shape, lambda i: zeros)


def _my_index():
    return 4 * lax.axis_index("x") + 2 * lax.axis_index("y") + lax.axis_index("c")


def _peer(k):
    x, y, c = lax.axis_index("x"), lax.axis_index("y"), lax.axis_index("c")
    px = 1 - x if k & 4 else x
    py = 1 - y if k & 2 else y
    pc = 1 - c if k & 1 else c
    return (px, py, pc), 4 * px + 2 * py + pc


class _CommPlan:
    def __init__(self, ins, out_shapes, n_remote, n_local, start, wait):
        self.ins, self.out_shapes = list(ins), list(out_shapes)
        self.n_remote, self.n_local = n_remote, n_local
        self.start, self.wait = start, wait


def _plan_all_gather(shards):
    n_w = len(shards)

    def remote(ins, outs, send, recv, w, k, slot):
        dev, idx = _peer(k)
        return pltpu.make_async_remote_copy(
            src_ref=ins[w], dst_ref=outs[w].at[idx if slot is None else slot],
            send_sem=send.at[w * 7 + k - 1], recv_sem=recv.at[w * 7 + k - 1],
            device_id=dev, device_id_type=MESH)

    def start(ins, outs, send, recv, local):
        me = _my_index()
        for w in range(n_w):
            pltpu.make_async_copy(ins[w], outs[w].at[me], local.at[w]).start()
            for k in range(1, N_DEV):
                remote(ins, outs, send, recv, w, k, me).start()

    def wait(ins, outs, send, recv, local):
        me = _my_index()
        for w in range(n_w):
            for k in range(1, N_DEV):
                remote(ins, outs, send, recv, w, k, None).wait()
            pltpu.make_async_copy(ins[w], outs[w].at[me], local.at[w]).wait()

    return _CommPlan(shards, [_sds((N_DEV,) + s.shape, s.dtype) for s in shards], n_w * 7, n_w, start, wait)


def _plan_exchange(chunked):
    n_w = len(chunked)

    def remote(ins, outs, send, recv, w, k):
        dev, idx = _peer(k)
        return pltpu.make_async_remote_copy(
            src_ref=ins[w].at[idx], dst_ref=outs[w].at[k],
            send_sem=send.at[w * 7 + k - 1], recv_sem=recv.at[w * 7 + k - 1],
            device_id=dev, device_id_type=MESH)

    def start(ins, outs, send, recv, local):
        me = _my_index()
        for w in range(n_w):
            pltpu.make_async_copy(ins[w].at[me], outs[w].at[0], local.at[w]).start()
        for k in range(1, N_DEV):
            for w in range(n_w):
                remote(ins, outs, send, recv, w, k).start()

    def wait(ins, outs, send, recv, local):
        me = _my_index()
        for k in range(1, N_DEV):
            for w in range(n_w):
                remote(ins, outs, send, recv, w, k).wait()
        for w in range(n_w):
            pltpu.make_async_copy(ins[w].at[me], outs[w].at[0], local.at[w]).wait()

    return _CommPlan(chunked, [_sds(t.shape, t.dtype) for t in chunked], n_w * 7, n_w, start, wait)


def _plan_share(block):
    def remote(ins, outs, send, recv, k, slot):
        dev, idx = _peer(k)
        return pltpu.make_async_remote_copy(
            src_ref=ins[0], dst_ref=outs[0].at[idx if slot is None else slot],
            send_sem=send.at[k - 1], recv_sem=recv.at[k - 1], device_id=dev, device_id_type=MESH)

    def start(ins, outs, send, recv, local):
        me = _my_index()
        pltpu.make_async_copy(ins[0], outs[0].at[me], local.at[0]).start()
        for k in range(1, N_DEV):
            remote(ins, outs, send, recv, k, me).start()

    def wait(ins, outs, send, recv, local):
        for k in range(1, N_DEV):
            remote(ins, outs, send, recv, k, None).wait()
        pltpu.make_async_copy(ins[0], outs[0].at[_my_index()], local.at[0]).wait()

    return _CommPlan([block], [_sds((N_DEV,) + block.shape, block.dtype)], 7, 1, start, wait)


def _hosted_call(body, plan, *, name, grid, in_specs, out_specs, out_shape, scratch_shapes, args):
    n_in, n_out, n_scr = len(in_specs), len(out_specs), len(scratch_shapes)
    sem = ("arbitrary",) * len(grid)
    if plan is None:
        def plain(*refs):
            body(refs[:n_in], refs[n_in:n_in + n_out], refs[n_in + n_out:])

        res = pl.pallas_call(plain, name=name, grid=grid, in_specs=in_specs, out_specs=out_specs,
                             out_shape=out_shape, scratch_shapes=scratch_shapes,
                             compiler_params=_params(*sem))(*args)
        return list(res), None

    n_ci, n_co = len(plan.ins), len(plan.out_shapes)
    hbm = pl.BlockSpec(memory_space=pl.ANY)

    def hosted(*refs):
        ins, refs = refs[:n_in], refs[n_in:]
        c_ins, refs = refs[:n_ci], refs[n_ci:]
        outs, refs = refs[:n_out], refs[n_out:]
        c_outs, refs = refs[:n_co], refs[n_co:]
        scr, sems = refs[:n_scr], refs[n_scr:]
        first = functools.reduce(jnp.logical_and, [pl.program_id(d) == 0 for d in range(len(grid))])
        last = functools.reduce(jnp.logical_and, [pl.program_id(d) == grid[d] - 1 for d in range(len(grid))])

        @pl.when(first)
        def _():
            plan.start(c_ins, c_outs, *sems)

        body(ins, outs, scr)

        @pl.when(last)
        def _():
            plan.wait(c_ins, c_outs, *sems)

    res = pl.pallas_call(
        hosted, name=name, grid=grid,
        in_specs=list(in_specs) + [hbm] * n_ci, out_specs=list(out_specs) + [hbm] * n_co,
        out_shape=list(out_shape) + plan.out_shapes,
        scratch_shapes=list(scratch_shapes) + [pltpu.SemaphoreType.DMA((plan.n_remote,)),
                                               pltpu.SemaphoreType.DMA((plan.n_remote,)),
                                               pltpu.SemaphoreType.DMA((plan.n_local,))],
        compiler_params=_params(*sem),
    )(*args, *plan.ins)
    return list(res[:n_out]), list(res[n_out:])


def _comm_only(plan, *, name):
    hbm = pl.BlockSpec(memory_space=pl.ANY)
    n_ci, n_co = len(plan.ins), len(plan.out_shapes)

    def body(*refs):
        c_ins, c_outs, sems = refs[:n_ci], refs[n_ci:n_ci + n_co], refs[n_ci + n_co:]
        plan.start(c_ins, c_outs, *sems)
        plan.wait(c_ins, c_outs, *sems)

    return pl.pallas_call(
        body, name=name, in_specs=[hbm] * n_ci, out_specs=[hbm] * n_co, out_shape=plan.out_shapes,
        scratch_shapes=[pltpu.SemaphoreType.DMA((plan.n_remote,)), pltpu.SemaphoreType.DMA((plan.n_remote,)),
                        pltpu.SemaphoreType.DMA((plan.n_local,))],
        compiler_params=pltpu.CompilerParams(vmem_limit_bytes=VMEM_LIMIT),
    )(*plan.ins)


def _matmul(a, b, *, ta=False, tb=False, out_dtype, tm, tn, tk, name, comm=None):
    m, k = (a.shape[1], a.shape[0]) if ta else a.shape
    n = b.shape[0] if tb else b.shape[1]
    tm, tn, tk = min(tm, m), min(tn, n), min(tk, k)
    nk = k // tk
    dims = (((0 if ta else 1,), (1 if tb else 0,)), ((), ()))

    def body(ins, outs, acc):
        a_ref, b_ref = ins
        o_ref, = outs
        part = lax.dot_general(a_ref[...], b_ref[...], dims, preferred_element_type=F32)
        if nk == 1:
            o_ref[...] = part.astype(o_ref.dtype)
            return
        acc_ref, = acc
        kk = pl.program_id(2)

        @pl.when(kk == 0)
        def _():
            acc_ref[...] = part

        @pl.when(kk > 0)
        def _():
            acc_ref[...] += part

        @pl.when(kk == nk - 1)
        def _():
            o_ref[...] = acc_ref[...].astype(o_ref.dtype)

    a_spec = (pl.BlockSpec((tk, tm), lambda j, i, kk: (kk, i)) if ta
              else pl.BlockSpec((tm, tk), lambda j, i, kk: (i, kk)))
    b_spec = (pl.BlockSpec((tn, tk), lambda j, i, kk: (j, kk)) if tb
              else pl.BlockSpec((tk, tn), lambda j, i, kk: (kk, j)))
    res, comm_res = _hosted_call(
        body, comm, name=name, grid=(n // tn, m // tm, nk),
        in_specs=[a_spec, b_spec],
        out_specs=[pl.BlockSpec((tm, tn), lambda j, i, kk: (i, j))],
        out_shape=[_sds((m, n), out_dtype)],
        scratch_shapes=[] if nk == 1 else [pltpu.VMEM((tm, tn), F32)],
        args=(a, b))
    return res[0] if comm is None else (res[0], comm_res)


def _rstd(h):
    return lax.rsqrt(jnp.mean(h * h, axis=-1, keepdims=True) + RMS_EPS)


def _sigmoid(z):
    return 1.0 / (1.0 + jnp.exp(-z))


def _rms_fwd(x, g, *, tm, name):
    n = x.shape[0]

    def body(x_ref, g_ref, o_ref):
        h = x_ref[...]
        o_ref[...] = (h * _rstd(h) * g_ref[...]).astype(BF16)

    return pl.pallas_call(
        body, name=name, grid=(n // tm,),
        in_specs=[_rows(tm, D_MODEL), _const((1, D_MODEL))],
        out_specs=_rows(tm, D_MODEL), out_shape=_sds((n, D_MODEL), BF16),
        compiler_params=_params("parallel"),
    )(x, g)


def _swap_halves(t):
    width = t.shape[1]
    lane = lax.broadcasted_iota(jnp.int32, t.shape, 1)
    return jnp.where((lane & 63) < 32, pltpu.roll(t, width - 32, 1), pltpu.roll(t, 32, 1))


def _dil_spec(dil, tm):
    return pl.BlockSpec((dil, tm // dil, 256), lambda i: (0, i, 0))


def _dil_scratch(tm):
    return pltpu.VMEM((2, tm, 128), F32)


def _load_token_order(src, scr, dil, tm):
    if dil == 1:
        return src[0]
    for j in range(dil):
        for c in range(2):
            scr[c, pl.ds(j, tm // dil, stride=dil), :] = src[j, :, c * 128:(c + 1) * 128]
    return jnp.concatenate([scr[0], scr[1]], axis=1)


def _store_dil_order(val, dst, scr, dil, tm):
    if dil == 1:
        dst[0] = val.astype(dst.dtype)
        return
    for c in range(2):
        scr[c] = val[:, c * 128:(c + 1) * 128]
    for j in range(dil):
        for c in range(2):
            dst[j, :, c * 128:(c + 1) * 128] = scr[c, pl.ds(j, tm // dil, stride=dil), :].astype(dst.dtype)


def _split_proj(proj, cos_t, sin_t, *, tm, name):
    n = proj.shape[0]
    n_dil = len(DIL_DILATIONS)

    def body(*refs):
        na_in = refs[0:3]
        dil_in = refs[3:3 + 3 * n_dil]
        gate_in = refs[12:20]
        cos_ref, sin_ref = refs[20:22]
        outs = refs[22:]
        na_out = outs[0:3]
        dil_out = outs[3:12]
        sn_ref, sd_ref = outs[12:14]
        scr = outs[14]
        for t in range(3):
            na_out[t][...] = na_in[t][...].astype(BF16)
        cosv, sinv = cos_ref[...], sin_ref[...]
        for t in range(3):
            for gi, dil in enumerate(DIL_DILATIONS):
                val = dil_in[t * n_dil + gi][...]
                if t < 2:
                    val = val * cosv + _swap_halves(val) * sinv
                _store_dil_order(val, dil_out[t * n_dil + gi], scr, dil, tm)
        for c in range(4):
            sn_ref[:, c * 256:(c + 1) * 256] = _sigmoid(gate_in[c][...])
            sd_ref[:, c * 256:(c + 1) * 256] = _sigmoid(gate_in[4 + c][...])

    in_specs = [_rows(tm, NA_WIDTH, c) for c in range(3)]
    in_specs += [_rows(tm, 256, 6 + c) for c in range(9)]
    in_specs += [_rows(tm, 256, 15 + c) for c in range(8)]
    in_specs += [_rows(tm, 256), _rows(tm, 256)]
    out_specs = [_rows(tm, NA_WIDTH)] * 3
    out_shape = [_sds((n, NA_WIDTH), BF16)] * 3
    for _ in range(3):
        for dil in DIL_DILATIONS:
            out_specs.append(pl.BlockSpec((dil, tm // dil, 256), lambda i: (0, i, 0)))
            out_shape.append(_sds((dil, n // dil, 256), BF16))
    out_specs += [_rows(tm, D_MODEL)] * 2
    out_shape += [_sds((n, D_MODEL), F32)] * 2
    res = pl.pallas_call(
        body, name=name, grid=(n // tm,),
        in_specs=in_specs, out_specs=out_specs, out_shape=out_shape,
        scratch_shapes=[_dil_scratch(tm)],
        compiler_params=_params("parallel"),
    )(*([proj] * 20), cos_t, sin_t)
    return res[0:3], res[3:6], res[6:9], res[9:12], res[12], res[13]


def _gate_mix(sn, bn, sd, bd, *, tm, name):
    n = sn.shape[0]

    def body(sn_ref, bn_ref, sd_ref, bd_ref, o_ref):
        o_ref[...] = (sn_ref[...] * bn_ref[...] + sd_ref[...] * bd_ref[...]).astype(BF16)

    return pl.pallas_call(
        body, name=name, grid=(n // tm,), in_specs=[_rows(tm, D_MODEL)] * 4,
        out_specs=_rows(tm, D_MODEL), out_shape=_sds((n, D_MODEL), BF16),
        compiler_params=_params("parallel"),
    )(sn, bn, sd, bd)


def _residual_rms(h, delta, g, *, tm, name):
    n = h.shape[0]

    def body(h_ref, d_ref, g_ref, hn_ref, z_ref):
        hn = h_ref[...] + d_ref[...]
        hn_ref[...] = hn
        z_ref[...] = (hn * _rstd(hn) * g_ref[...]).astype(BF16)

    return pl.pallas_call(
        body, name=name, grid=(n // tm,),
        in_specs=[_rows(tm, D_MODEL), _rows(tm, D_MODEL), _const((1, D_MODEL))],
        out_specs=[_rows(tm, D_MODEL)] * 2,
        out_shape=[_sds((n, D_MODEL), F32), _sds((n, D_MODEL), BF16)],
        compiler_params=_params("parallel"),
    )(h, delta, g)


def _relu_sq(u, *, tm, name):
    n, w = u.shape

    def body(u_ref, f_ref):
        r = jnp.maximum(u_ref[...], 0.0)
        f_ref[...] = (r * r).astype(BF16)

    return pl.pallas_call(
        body, name=name, grid=(n // tm,), in_specs=[_rows(tm, w)],
        out_specs=_rows(tm, w), out_shape=_sds((n, w), BF16),
        compiler_params=_params("parallel"),
    )(u)


def _relu_sq_bwd(df, u, *, tm, name):
    n, w = u.shape

    def body(df_ref, u_ref, o_ref):
        o_ref[...] = (df_ref[...] * (2.0 * jnp.maximum(u_ref[...], 0.0))).astype(BF16)

    return pl.pallas_call(
        body, name=name, grid=(n // tm,), in_specs=[_rows(tm, w)] * 2,
        out_specs=_rows(tm, w), out_shape=_sds((n, w), BF16),
        compiler_params=_params("parallel"),
    )(df, u)


def _tail(h2, gt, pp, target, g_final, *, tm, name):
    n = h2.shape[0]

    def body(h2_ref, gt_ref, pp_ref, t_ref, g_ref, dh3_ref, dpp_ref, dgt_ref, dg_ref, loss_ref):
        i = pl.program_id(0)

        @pl.when(i == 0)
        def _():
            dg_ref[...] = jnp.zeros_like(dg_ref)
            loss_ref[...] = jnp.zeros_like(loss_ref)

        sg = _sigmoid(gt_ref[...])
        pp_v = pp_ref[...]
        h3 = h2_ref[...] + sg * pp_v
        r3 = _rstd(h3)
        n3 = h3 * r3
        g = g_ref[...]
        err = n3 * g - t_ref[...]
        loss_ref[...] += 0.5 * jnp.sum(jnp.sum(err * err, axis=-1, keepdims=True) / D_MODEL)
        dy = err / D_MODEL
        dg_ref[...] += jnp.sum(dy * n3, axis=0, keepdims=True)
        dn = dy * g
        dh3 = r3 * (dn - n3 * jnp.mean(dn * n3, axis=-1, keepdims=True))
        dh3_ref[...] = dh3
        dpp_ref[...] = (dh3 * sg).astype(BF16)
        dgt_ref[...] = (dh3 * pp_v * sg * (1.0 - sg)).astype(BF16)

    return pl.pallas_call(
        body, name=name, grid=(n // tm,),
        in_specs=[_rows(tm, D_MODEL)] * 4 + [_const((1, D_MODEL))],
        out_specs=[_rows(tm, D_MODEL)] * 3 + [_const((1, D_MODEL)), _const((1, 128))],
        out_shape=[_sds((n, D_MODEL), F32), _sds((n, D_MODEL), BF16), _sds((n, D_MODEL), BF16),
                   _sds((1, D_MODEL), F32), _sds((1, 128), F32)],
        compiler_params=_params("arbitrary"),
    )(h2, gt, pp, target, g_final)


def _rms_bwd(dz, h, g, dres, *, tm, name, want_bf16=True):
    n = h.shape[0]

    def body(dz_ref, h_ref, g_ref, dres_ref, dh_ref, *rest):
        if want_bf16:
            dhb_ref, dg_ref = rest
        else:
            dg_ref, = rest
        i = pl.program_id(0)

        @pl.when(i == 0)
        def _():
            dg_ref[...] = jnp.zeros_like(dg_ref)

        hv = h_ref[...]
        r = _rstd(hv)
        nrm = hv * r
        dz_v = dz_ref[...]
        dg_ref[...] += jnp.sum(dz_v * nrm, axis=0, keepdims=True)
        dn = dz_v * g_ref[...]
        dh = dres_ref[...] + r * (dn - nrm * jnp.mean(dn * nrm, axis=-1, keepdims=True))
        dh_ref[...] = dh
        if want_bf16:
            dhb_ref[...] = dh.astype(BF16)

    out_specs = [_rows(tm, D_MODEL)]
    out_shape = [_sds((n, D_MODEL), F32)]
    if want_bf16:
        out_specs.append(_rows(tm, D_MODEL))
        out_shape.append(_sds((n, D_MODEL), BF16))
    out_specs.append(_const((1, D_MODEL)))
    out_shape.append(_sds((1, D_MODEL), F32))
    return pl.pallas_call(
        body, name=name, grid=(n // tm,),
        in_specs=[_rows(tm, D_MODEL), _rows(tm, D_MODEL), _const((1, D_MODEL)), _rows(tm, D_MODEL)],
        out_specs=out_specs, out_shape=out_shape,
        compiler_params=_params("arbitrary"),
    )(dz, h, g, dres)


def _gate_bwd(dmixed, sn, bn, sd, bd, *, tm, name):
    n = sn.shape[0]

    def body(dm_ref, sn_ref, bn_ref, sd_ref, bd_ref, dbn_ref, dbd_ref, dgn_ref, dgd_ref):
        dm = dm_ref[...]
        s1, s2 = sn_ref[...], sd_ref[...]
        dbn_ref[...] = (dm * s1).astype(BF16)
        dbd_ref[...] = (dm * s2).astype(BF16)
        dgn_ref[...] = (dm * bn_ref[...] * s1 * (1.0 - s1)).astype(BF16)
        dgd_ref[...] = (dm * bd_ref[...] * s2 * (1.0 - s2)).astype(BF16)

    return pl.pallas_call(
        body, name=name, grid=(n // tm,), in_specs=[_rows(tm, D_MODEL)] * 5,
        out_specs=[_rows(tm, D_MODEL)] * 4, out_shape=[_sds((n, D_MODEL), BF16)] * 4,
        compiler_params=_params("parallel"),
    )(dmixed, sn, bn, sd, bd)


def _assemble_dproj(dna, ddil_q, ddil_k, ddil_v, dgn, dgd, cos_t, sin_t, *, tm, name):
    n = dgn.shape[0]

    def body(*refs):
        dq_ref, dk_ref, dv_ref = refs[0:3]
        dil_in = refs[3:12]
        dgn_ref, dgd_ref, cos_ref, sin_ref, o_ref, scr = refs[12:18]
        o_ref[:, 0:512] = dq_ref[...]
        o_ref[:, 512:1024] = dk_ref[...].astype(BF16)
        o_ref[:, 1024:1536] = dv_ref[...].astype(BF16)
        cosv, sinv = cos_ref[...], sin_ref[...]
        for t in range(3):
            for gi, dil in enumerate(DIL_DILATIONS):
                val = _load_token_order(dil_in[t * 3 + gi], scr, dil, tm)
                if t < 2:
                    val = val * cosv + _swap_halves(val * sinv)
                c0 = 1536 + t * DIL_WIDTH + gi * 256
                o_ref[:, c0:c0 + 256] = val.astype(BF16)
        o_ref[:, 3840:4864] = dgn_ref[...]
        o_ref[:, 4864:5888] = dgd_ref[...]

    in_specs = [_rows(tm, NA_WIDTH)] * 3
    for _ in range(3):
        for dil in DIL_DILATIONS:
            in_specs.append(pl.BlockSpec((dil, tm // dil, 256), lambda i: (0, i, 0)))
    in_specs += [_rows(tm, D_MODEL)] * 2 + [_rows(tm, 256)] * 2
    return pl.pallas_call(
        body, name=name, grid=(n // tm,), in_specs=in_specs,
        out_specs=_rows(tm, IN_WIDTH), out_shape=_sds((n, IN_WIDTH), BF16),
        scratch_shapes=[_dil_scratch(tm)],
        compiler_params=_params("parallel"),
    )(*dna, *ddil_q, *ddil_k, *ddil_v, dgn, dgd, cos_t, sin_t)


def _na_bias(rb_ref, bias_scr):
    shape = (GRID_W, NA_WIN_ROWS * GRID_W)
    qc = lax.broadcasted_iota(jnp.int32, shape, 0)
    kc = lax.broadcasted_iota(jnp.int32, shape, 1) & (GRID_W - 1)
    cs = jnp.clip(qc - 8, 0, GRID_W - 16)
    valid = (kc >= cs) & (kc < cs + 16)
    for hh in range(2):
        for di in range(NA_WIN_ROWS):
            t = jnp.broadcast_to(rb_ref[hh, di:di + 1, :], shape)
            t = pltpu.roll(t, shape[1] - 15, 1)
            for b in range(6):
                t = jnp.where(((qc >> b) & 1) == 1, pltpu.roll(t, 1 << b, 1), t)
            bias_scr[hh, di] = jnp.where(valid, t, NEG_INF)


def _na_scores(q_ref, k_ref, v_ref, bias_scr, r, n_rows):
    lane = lax.broadcasted_iota(jnp.int32, (GRID_W, 128), 1)
    rs = jnp.clip(r - NA_WIN_ROWS // 2, 0, n_rows - NA_WIN_ROWS)
    di = r - rs
    off = pl.multiple_of(rs * GRID_W, GRID_W)
    kw = k_ref[pl.ds(off, NA_WIN_ROWS * GRID_W), :]
    vw = v_ref[pl.ds(off, NA_WIN_ROWS * GRID_W), :]
    qr = q_ref[pl.ds(pl.multiple_of(r * GRID_W, GRID_W), GRID_W), :].astype(F32)
    qs = jnp.concatenate([jnp.where(lane < 64, qr, 0.0), jnp.where(lane >= 64, qr, 0.0)], axis=0).astype(BF16)
    s = lax.dot_general(qs, kw, NT_DIMS, preferred_element_type=F32) * QK_SCALE
    s = s + jnp.concatenate([bias_scr[0, di], bias_scr[1, di]], axis=0)
    m = jnp.max(s, axis=-1, keepdims=True)
    e = jnp.exp(s - m)
    p = e * (1.0 / jnp.sum(e, axis=-1, keepdims=True))
    return p, qs, kw, vw, off, di, lane


def _na_fwd(q, k, v, rb, *, name, comm=None):
    n = q.shape[0]
    n_rows = n // GRID_W

    def body(ins, outs, scr):
        q_ref, k_ref, v_ref, rb_ref = ins
        o_ref, = outs
        bias_scr, = scr
        _na_bias(rb_ref, bias_scr)

        def row(r, carry):
            p, _, _, vw, _, _, lane = _na_scores(q_ref, k_ref, v_ref, bias_scr, r, n_rows)
            o2 = jnp.dot(p.astype(BF16), vw, preferred_element_type=F32)
            o = jnp.where(lane < 64, o2[:GRID_W], o2[GRID_W:])
            o_ref[pl.ds(pl.multiple_of(r * GRID_W, GRID_W), GRID_W), :] = o.astype(BF16)
            return carry

        lax.fori_loop(0, n_rows, row, 0)

    col = pl.BlockSpec((n, 128), lambda h: (0, h))
    res, comm_res = _hosted_call(
        body, comm, name=name, grid=(NA_WIDTH // 128,),
        in_specs=[col, col, col, pl.BlockSpec((2, NA_WIN_ROWS, 512), lambda h: (h, 0, 0))],
        out_specs=[col], out_shape=[_sds((n, NA_WIDTH), BF16)],
        scratch_shapes=[pltpu.VMEM((2, NA_WIN_ROWS, GRID_W, 512), F32)],
        args=(q, k, v, rb))
    return res[0] if comm is None else (res[0], comm_res)


def _na_bwd(q, k, v, do, rb, *, name, comm=None):
    n = q.shape[0]
    n_rows = n // GRID_W
    win = NA_WIN_ROWS * GRID_W

    def body(ins, outs, scr):
        q_ref, k_ref, v_ref, do_ref, rb_ref = ins
        dq_ref, dk_ref, dv_ref, drb_ref = outs
        bias_scr, acc_scr = scr
        _na_bias(rb_ref, bias_scr)
        acc_scr[...] = jnp.zeros_like(acc_scr)
        dk_ref[...] = jnp.zeros_like(dk_ref)
        dv_ref[...] = jnp.zeros_like(dv_ref)

        def row(r, carry):
            p, qs, kw, vw, off, di, lane = _na_scores(q_ref, k_ref, v_ref, bias_scr, r, n_rows)
            rows_r = pl.ds(pl.multiple_of(r * GRID_W, GRID_W), GRID_W)
            dor = do_ref[rows_r, :].astype(F32)
            dos = jnp.concatenate([jnp.where(lane < 64, dor, 0.0), jnp.where(lane >= 64, dor, 0.0)],
                                  axis=0).astype(BF16)
            dp = lax.dot_general(dos, vw, NT_DIMS, preferred_element_type=F32)
            ds = p * (dp - jnp.sum(p * dp, axis=-1, keepdims=True))
            acc_scr[0, di] += ds[:GRID_W]
            acc_scr[1, di] += ds[GRID_W:]
            dsb = ds.astype(BF16)
            dq2 = jnp.dot(dsb, kw, preferred_element_type=F32)
            dq_ref[rows_r, :] = (jnp.where(lane < 64, dq2[:GRID_W], dq2[GRID_W:]) * QK_SCALE).astype(BF16)
            dk_ref[pl.ds(off, win), :] += lax.dot_general(dsb, qs, TN_DIMS, preferred_element_type=F32) * QK_SCALE
            dv_ref[pl.ds(off, win), :] += lax.dot_general(p.astype(BF16), dos, TN_DIMS, preferred_element_type=F32)
            return carry

        lax.fori_loop(0, n_rows, row, 0)

        qc = lax.broadcasted_iota(jnp.int32, (GRID_W, win), 0)
        for hh in range(2):
            for di in range(NA_WIN_ROWS):
                t = acc_scr[hh, di]
                for b in range(6):
                    t = jnp.where(((qc >> b) & 1) == 1, pltpu.roll(t, win - (1 << b), 1), t)
                t = pltpu.roll(t, 15, 1)
                drb_ref[hh, :, di * win:(di + 1) * win] = jnp.sum(t, axis=0, keepdims=True)

    col = pl.BlockSpec((n, 128), lambda h: (0, h))
    res, comm_res = _hosted_call(
        body, comm, name=name, grid=(NA_WIDTH // 128,),
        in_specs=[col, col, col, col, pl.BlockSpec((2, NA_WIN_ROWS, 512), lambda h: (h, 0, 0))],
        out_specs=[col, col, col, pl.BlockSpec((2, 1, NA_WIN_ROWS * win), lambda h: (h, 0, 0))],
        out_shape=[_sds((n, NA_WIDTH), BF16), _sds((n, NA_WIDTH), F32), _sds((n, NA_WIDTH), F32),
                   _sds((8, 1, NA_WIN_ROWS * win), F32)],
        scratch_shapes=[pltpu.VMEM((2, NA_WIN_ROWS, GRID_W, win), F32),
                        pltpu.VMEM((2, NA_WIN_ROWS, GRID_W, win), F32)],
        args=(q, k, v, do, rb))
    return res if comm is None else (res, comm_res)


def _rpb_table(rpb2):
    ro = np.arange(NA_WIN_ROWS)[None, :] - np.arange(NA_WIN_ROWS)[:, None] + (NA_WIN_ROWS - 1)
    t = rpb2[:, ro, :]
    t = jnp.pad(t, ((0, 0), (0, 0), (0, 0), (0, GRID_W - t.shape[-1])))
    return t.reshape(8, NA_WIN_ROWS, NA_WIN_ROWS * GRID_W)


def _rpb_grad(drb, *, name):
    kdim = drb.shape[1]

    def body(x_ref, o_ref):
        kk = lax.broadcasted_iota(jnp.int32, (512, 512), 0)
        jj = lax.broadcasted_iota(jnp.int32, (512, 512), 1)
        wi, co = kk >> 6, kk & 63
        acc = jnp.zeros((8, 512), F32)
        for di in range(NA_WIN_ROWS):
            hit = ((wi - di + (NA_WIN_ROWS - 1)) == (jj >> 5)) & (co == (jj & 31)) & (co < 31)
            onehot = jnp.where(hit, 1.0, 0.0).astype(F32)
            acc = acc + jnp.dot(x_ref[:, di * 512:(di + 1) * 512], onehot, preferred_element_type=F32,
                                precision=lax.Precision.HIGHEST)
        o_ref[...] = acc

    return pl.pallas_call(
        body, name=name, grid=(1,),
        in_specs=[_const((8, kdim))], out_specs=_const((8, 512)), out_shape=_sds((8, 512), F32),
        compiler_params=_params("arbitrary"),
    )(drb)


def _dil_blocks(length):
    qb = min(128, length)
    return qb, min(qb + 2 * DIL_RADIUS, length)


def _dil_scores(q_ref, k_ref, v_ref, i, qb, win, length):
    start = pl.multiple_of(jnp.clip(i * qb - DIL_RADIUS, 0, length - win), DIL_RADIUS)
    kw = k_ref[0, pl.ds(start, win), :]
    vw = v_ref[0, pl.ds(start, win), :]
    qv = q_ref[0].astype(F32)
    lane = lax.broadcasted_iota(jnp.int32, (qb, 256), 1)
    qs = jnp.concatenate([jnp.where((lane >> 6) == h, qv, 0.0) for h in range(4)], axis=0).astype(BF16)
    s = lax.dot_general(qs, kw, NT_DIMS, preferred_element_type=F32) * QK_SCALE
    qi = i * qb + (lax.broadcasted_iota(jnp.int32, (4 * qb, win), 0) & (qb - 1))
    kj = start + lax.broadcasted_iota(jnp.int32, (4 * qb, win), 1)
    s = jnp.where(jnp.abs(qi - kj) <= DIL_RADIUS, s, NEG_INF)
    return s, qs, kw, vw, start, lane


def _pick_heads(stacked, lane, qb):
    out = jnp.zeros((qb, 256), stacked.dtype)
    for h in range(4):
        out = jnp.where((lane >> 6) == h, stacked[h * qb:(h + 1) * qb], out)
    return out


def _stack_head_cols(t, qb):
    return jnp.concatenate([t[:, 64 * h:64 * h + 1] for h in range(4)], axis=0)


def _dil_fwd(q, k, v, *, name):
    dil, length, _ = q.shape
    qb, win = _dil_blocks(length)

    def body(q_ref, k_ref, v_ref, o_ref, lse_ref):
        i = pl.program_id(1)
        s, _, _, vw, _, lane = _dil_scores(q_ref, k_ref, v_ref, i, qb, win, length)
        m = jnp.max(s, axis=-1, keepdims=True)
        lse = m + jnp.log(jnp.sum(jnp.exp(s - m), axis=-1, keepdims=True))
        p = jnp.exp(s - lse)
        o4 = jnp.dot(p.astype(BF16), vw, preferred_element_type=F32)
        o_ref[0] = _pick_heads(o4, lane, qb)
        lse_ref[0] = _pick_heads(jnp.broadcast_to(lse, (4 * qb, 256)), lane, qb)

    seq = pl.BlockSpec((1, length, 256), lambda j, i: (j, 0, 0))
    blk = pl.BlockSpec((1, qb, 256), lambda j, i: (j, i, 0))
    return pl.pallas_call(
        body, name=name, grid=(dil, length // qb),
        in_specs=[blk, seq, seq], out_specs=[blk, blk],
        out_shape=[_sds((dil, length, 256), F32)] * 2,
        compiler_params=_params("parallel", "parallel"),
    )(q, k, v)


def _dil_bwd(q, k, v, do, lse, cc, *, name):
    dil, length, _ = q.shape
    qb, win = _dil_blocks(length)

    def body(q_ref, k_ref, v_ref, do_ref, lse_ref, cc_ref, dq_ref, dk_ref, dv_ref):
        i = pl.program_id(1)

        @pl.when(i == 0)
        def _():
            dk_ref[...] = jnp.zeros_like(dk_ref)
            dv_ref[...] = jnp.zeros_like(dv_ref)

        s, qs, kw, vw, start, lane = _dil_scores(q_ref, k_ref, v_ref, i, qb, win, length)
        p = jnp.exp(s - _stack_head_cols(lse_ref[0], qb))
        dov = do_ref[0].astype(F32)
        dos = jnp.concatenate([jnp.where((lane >> 6) == h, dov, 0.0) for h in range(4)], axis=0).astype(BF16)
        dp = lax.dot_general(dos, vw, NT_DIMS, preferred_element_type=F32)
        ds = p * (dp + _stack_head_cols(cc_ref[0], qb))
        dsb = ds.astype(BF16)
        dq4 = jnp.dot(dsb, kw, preferred_element_type=F32)
        dq_ref[0] = _pick_heads(dq4, lane, qb) * QK_SCALE
        dk_ref[0, pl.ds(start, win), :] += lax.dot_general(dsb, qs, TN_DIMS, preferred_element_type=F32) * QK_SCALE
        dv_ref[0, pl.ds(start, win), :] += lax.dot_general(p.astype(BF16), dos, TN_DIMS, preferred_element_type=F32)

    seq = pl.BlockSpec((1, length, 256), lambda j, i: (j, 0, 0))
    blk = pl.BlockSpec((1, qb, 256), lambda j, i: (j, i, 0))
    return pl.pallas_call(
        body, name=name, grid=(dil, length // qb),
        in_specs=[blk, seq, seq, blk, blk, blk], out_specs=[blk, seq, seq],
        out_shape=[_sds((dil, length, 256), F32)] * 3,
        compiler_params=_params("parallel", "arbitrary"),
    )(q, k, v, do, lse, cc)


def _merge_weights(lses):
    m = jnp.maximum(jnp.maximum(lses[0], lses[1]), lses[2])
    es = [jnp.exp(t - m) for t in lses]
    inv = 1.0 / (es[0] + es[1] + es[2])
    return [e * inv for e in es]


def _dil_merge(outs, lses, *, tm, name):
    n = outs[0].shape[1]

    def body(*refs):
        o_in, l_in = refs[0:3], refs[3:6]
        y_ref, yb_ref, scr = refs[6:9]
        lv = [_load_token_order(l_in[g], scr, d, tm) for g, d in enumerate(DIL_DILATIONS)]
        ws = _merge_weights(lv)
        y = jnp.zeros((tm, 256), F32)
        for g, d in enumerate(DIL_DILATIONS):
            y = y + ws[g] * _load_token_order(o_in[g], scr, d, tm)
        y_ref[...] = y
        yb_ref[...] = y.astype(BF16)

    specs = [_dil_spec(d, tm) for d in DIL_DILATIONS]
    return pl.pallas_call(
        body, name=name, grid=(n // tm,), in_specs=specs + specs,
        out_specs=[_rows(tm, 256)] * 2, out_shape=[_sds((n, 256), F32), _sds((n, 256), BF16)],
        scratch_shapes=[_dil_scratch(tm)],
        compiler_params=_params("parallel"),
    )(*outs, *lses)


def _dil_merge_bwd(dy, y, lses, *, tm, name):
    n = dy.shape[0]

    def body(*refs):
        dy_ref, y_ref = refs[0:2]
        l_in = refs[2:5]
        do_out, cc_out = refs[5:8], refs[8:11]
        scr = refs[11]
        lv = [_load_token_order(l_in[g], scr, d, tm) for g, d in enumerate(DIL_DILATIONS)]
        ws = _merge_weights(lv)
        dyv = dy_ref[...]
        rr = lax.broadcasted_iota(jnp.int32, (256, 256), 0) >> 6
        cc = lax.broadcasted_iota(jnp.int32, (256, 256), 1) >> 6
        ones = jnp.where(rr == cc, 1.0, 0.0).astype(F32)
        tsum = jnp.dot(dyv * y_ref[...], ones, preferred_element_type=F32,
                       precision=lax.Precision.HIGHEST)
        for g, d in enumerate(DIL_DILATIONS):
            _store_dil_order(ws[g] * dyv, do_out[g], scr, d, tm)
            _store_dil_order(-ws[g] * tsum, cc_out[g], scr, d, tm)

    specs = [_dil_spec(d, tm) for d in DIL_DILATIONS]
    res = pl.pallas_call(
        body, name=name, grid=(n // tm,),
        in_specs=[_rows(tm, 256)] * 2 + specs,
        out_specs=specs + specs,
        out_shape=[_sds((d, n // d, 256), BF16) for d in DIL_DILATIONS]
                  + [_sds((d, n // d, 256), F32) for d in DIL_DILATIONS],
        scratch_shapes=[_dil_scratch(tm)],
        compiler_params=_params("parallel"),
    )(dy, y, *lses)
    return res[0:3], res[3:6]


_WEIGHTS = (("w_in", 1, 736), ("w_branch_na", 1, 128), ("w_branch_dil", 1, 128), ("w_out", 0, 128),
            ("w_up", 1, 512), ("w_down", 0, 512), ("w_ple_gate", 0, 128), ("w_ple_proj", 1, 128))
_W_IN, _W_BNA, _W_BD, _W_OUT, _W_UP, _W_DOWN, _W_PG, _W_PP = range(8)
_GATHER_EARLY = (_W_BNA, _W_BD, _W_OUT, _W_PG, _W_PP)
_GATHER_LATE = (_W_UP, _W_DOWN)


def _to_full(widx, gathered):
    if _WEIGHTS[widx][1] == 0:
        return gathered.reshape(-1, gathered.shape[2])
    return jnp.transpose(gathered, (1, 0, 2)).reshape(gathered.shape[1], -1)


def _to_chunks(widx, mat):
    _, axis, width = _WEIGHTS[widx]
    if axis == 0:
        return mat.reshape(N_DEV, width, mat.shape[1])
    return jnp.transpose(mat.reshape(mat.shape[0], N_DEV, width), (1, 0, 2))


def _local_step(x, p_bf16, positions, target, g_mix, g_mlp, g_ple, g_final, rpb2, w_in, rest, distributed):
    tm = 256
    half = HEAD_DIM // 2
    inv_freq = 10000.0 ** (-jnp.arange(half, dtype=F32) / half)
    ang = positions.astype(F32)[:, None] * inv_freq
    cos, sin = jnp.cos(ang), jnp.sin(ang)
    cos_t = jnp.tile(jnp.concatenate([cos, cos], axis=-1), (1, 4))
    sin_t = jnp.tile(jnp.concatenate([-sin, sin], axis=-1), (1, 4))
    rb = _rpb_table(rpb2)
    wts = [w_in] + [None] * 7
    if not distributed:
        wts[1:] = rest

    a = _rms_fwd(x, g_mix, tm=tm, name="rms_mix")
    early = _plan_all_gather([rest[i - 1] for i in _GATHER_EARLY]) if distributed else None
    proj = _matmul(a, w_in, out_dtype=F32, tm=512, tn=2944, tk=1024, name="mm_in", comm=early)
    if distributed:
        proj, got = proj
        for i, g in zip(_GATHER_EARLY, got):
            wts[i] = _to_full(i, g)
    na_qkv, dq_g, dk_g, dv_g, sn, sd = _split_proj(proj, cos_t, sin_t, tm=tm, name="split_proj")
    late = _plan_all_gather([rest[i - 1] for i in _GATHER_LATE]) if distributed else None
    y_na = _na_fwd(*na_qkv, rb, name="na_fwd", comm=late)
    if distributed:
        y_na, got = y_na
        for i, g in zip(_GATHER_LATE, got):
            wts[i] = _to_full(i, g)
    _, w_bna, w_bd, w_out, w_up, w_down, w_pg, w_pp = wts
    d_out, d_lse = [], []
    for g in range(3):
        o, lse = _dil_fwd(dq_g[g], dk_g[g], dv_g[g], name=f"dil_fwd{g}")
        d_out.append(o)
        d_lse.append(lse)
    y_dil, y_dil_b = _dil_merge(d_out, d_lse, tm=tm, name="dil_merge")
    bn = _matmul(y_na, w_bna, out_dtype=F32, tm=512, tn=1024, tk=512, name="mm_bna")
    bd = _matmul(y_dil_b, w_bd, out_dtype=F32, tm=512, tn=1024, tk=256, name="mm_bd")
    mixed = _gate_mix(sn, bn, sd, bd, tm=tm, name="gate_mix")
    mo = _matmul(mixed, w_out, out_dtype=F32, tm=512, tn=1024, tk=1024, name="mm_out")
    h1, c = _residual_rms(x, mo, g_mlp, tm=tm, name="res_rms_mlp")
    u = _matmul(c, w_up, out_dtype=F32, tm=512, tn=2048, tk=1024, name="mm_up")
    f = _relu_sq(u, tm=128, name="relu_sq")
    dn = _matmul(f, w_down, out_dtype=F32, tm=512, tn=1024, tk=2048, name="mm_down")
    h2, e = _residual_rms(h1, dn, g_ple, tm=tm, name="res_rms_ple")
    gt = _matmul(e, w_pg, out_dtype=F32, tm=512, tn=1024, tk=1024, name="mm_pg")
    pp = _matmul(p_bf16, w_pp, out_dtype=F32, tm=512, tn=1024, tk=256, name="mm_pp")

    gws = [None] * 8
    dh3, dpp, dgt, dg_final, loss = _tail(h2, gt, pp, target, g_final, tm=tm, name="tail")
    gws[_W_PP] = _matmul(p_bf16, dpp, ta=True, out_dtype=BF16, tm=256, tn=1024, tk=512, name="mm_gw_pp")
    gws[_W_PG] = _matmul(e, dgt, ta=True, out_dtype=BF16, tm=512, tn=1024, tk=512, name="mm_gw_pg")
    de = _matmul(dgt, w_pg, tb=True, out_dtype=F32, tm=512, tn=1024, tk=1024, name="mm_de")
    dh2, dh2_b, dg_ple = _rms_bwd(de, h2, g_ple, dh3, tm=tm, name="rms_bwd_ple")
    df = _matmul(dh2_b, w_down, tb=True, out_dtype=F32, tm=512, tn=2048, tk=1024, name="mm_df")
    gws[_W_DOWN] = _matmul(f, dh2_b, ta=True, out_dtype=BF16, tm=1024, tn=1024, tk=512, name="mm_gw_down")
    du = _relu_sq_bwd(df, u, tm=128, name="relu_sq_bwd")
    gws[_W_UP] = _matmul(c, du, ta=True, out_dtype=BF16, tm=512, tn=2048, tk=512, name="mm_gw_up")
    dc = _matmul(du, w_up, tb=True, out_dtype=F32, tm=512, tn=1024, tk=2048, name="mm_dc")
    dh1, dh1_b, dg_mlp = _rms_bwd(dc, h1, g_mlp, dh2, tm=tm, name="rms_bwd_mlp")
    dmixed = _matmul(dh1_b, w_out, tb=True, out_dtype=F32, tm=512, tn=1024, tk=1024, name="mm_dmixed")
    gws[_W_OUT] = _matmul(mixed, dh1_b, ta=True, out_dtype=BF16, tm=512, tn=1024, tk=512, name="mm_gw_out")
    dbn, dbd, dgn, dgd = _gate_bwd(dmixed, sn, bn, sd, bd, tm=tm, name="gate_bwd")
    gws[_W_BNA] = _matmul(y_na, dbn, ta=True, out_dtype=BF16, tm=512, tn=1024, tk=512, name="mm_gw_bna")
    dy_na = _matmul(dbn, w_bna, tb=True, out_dtype=BF16, tm=512, tn=512, tk=1024, name="mm_dy_na")
    gws[_W_BD] = _matmul(y_dil_b, dbd, ta=True, out_dtype=BF16, tm=256, tn=1024, tk=512, name="mm_gw_bd")
    dy_dil = _matmul(dbd, w_bd, tb=True, out_dtype=F32, tm=512, tn=256, tk=1024, name="mm_dy_dil")
    others = [i for i in range(8) if i != _W_IN]
    swap = _plan_exchange([_to_chunks(i, gws[i]) for i in others]) if distributed else None
    dna = _na_bwd(*na_qkv, dy_na, rb, name="na_bwd", comm=swap)
    if distributed:
        dna, got = dna
        for i, g in zip(others, got):
            gws[i] = g
    drpb = _rpb_grad(dna[3].reshape(8, -1), name="rpb_grad")
    do_g, cc_g = _dil_merge_bwd(dy_dil, y_dil, d_lse, tm=tm, name="dil_merge_bwd")
    ddq, ddk, ddv = [], [], []
    for g in range(3):
        r = _dil_bwd(dq_g[g], dk_g[g], dv_g[g], do_g[g], d_lse[g], cc_g[g], name=f"dil_bwd{g}")
        ddq.append(r[0])
        ddk.append(r[1])
        ddv.append(r[2])
    dproj = _assemble_dproj(dna[0:3], ddq, ddk, ddv, dgn, dgd, cos_t, sin_t, tm=tm, name="assemble_dproj")
    gw_in = _matmul(a, dproj, ta=True, out_dtype=BF16, tm=512, tn=2944, tk=512, name="mm_gw_in")
    swap_in = _plan_exchange([_to_chunks(_W_IN, gw_in)]) if distributed else None
    da = _matmul(dproj, w_in, tb=True, out_dtype=F32, tm=512, tn=1024, tk=2944, name="mm_da", comm=swap_in)
    if distributed:
        da, (gw_in,) = da
    gws[_W_IN] = gw_in
    dx, dg_mix = _rms_bwd(da, x, g_mix, dh1, tm=tm, name="rms_bwd_mix", want_bf16=False)
    return loss, dx, gws, (dg_mix, dg_mlp, dg_ple, dg_final), drpb


def _gather_w_in(shard):
    def body(w_ref, o_ref):
        o_ref[...] = w_ref[...].astype(BF16)

    rows, cols = shard.shape
    blk = pl.BlockSpec((256, cols), lambda i: (i, 0))
    shard_b = pl.pallas_call(body, name="cast_w_in", grid=(rows // 256,), in_specs=[blk], out_specs=blk,
                             out_shape=_sds(shard.shape, BF16), compiler_params=_params("parallel"))(shard)
    gathered, = _comm_only(_plan_all_gather([shard_b]), name="all_gather_w_in")
    return _to_full(_W_IN, gathered)


def _adamw(w, g, m, v):
    m = ADAM_B1 * m + (1.0 - ADAM_B1) * g
    v = ADAM_B2 * v + (1.0 - ADAM_B2) * (g * g)
    m_hat = m / (1.0 - ADAM_B1 ** ADAM_STEP)
    v_hat = v / (1.0 - ADAM_B2 ** ADAM_STEP)
    delta = -ADAM_LR * (m_hat / (jnp.sqrt(v_hat) + ADAM_EPS) + ADAM_WD * w)
    return delta, m, v


def _sum_adamw(parts, w, m, v, *, tr, name):
    rows, cols = w.shape

    def body(p_ref, w_ref, m_ref, v_ref, g_ref, d_ref, nm_ref, nv_ref):
        g = p_ref[0].astype(F32)
        for s in range(1, N_DEV):
            g = g + p_ref[s].astype(F32)
        g_ref[...] = g
        d_ref[...], nm_ref[...], nv_ref[...] = _adamw(w_ref[...], g, m_ref[...], v_ref[...])

    blk = pl.BlockSpec((tr, cols), lambda i: (i, 0))
    return pl.pallas_call(
        body, name=name, grid=(rows // tr,),
        in_specs=[pl.BlockSpec((N_DEV, tr, cols), lambda i: (0, i, 0)), blk, blk, blk],
        out_specs=[blk] * 4, out_shape=[_sds((rows, cols), F32)] * 4,
        compiler_params=_params("parallel"),
    )(parts, w, m, v)


_RPB_SIZE = 8 * 15 * 31


def _pack_small(g_mix, g_mlp, g_ple, g_final, rpb, loss_row):
    flat = jnp.concatenate([g_mix.reshape(-1), g_mlp.reshape(-1), g_ple.reshape(-1), g_final.reshape(-1),
                            rpb.reshape(-1), jnp.zeros((3840 - _RPB_SIZE,), F32), loss_row.reshape(-1),
                            jnp.zeros((128,), F32)])
    return flat.reshape(64, 128)


def _unpack_small(t):
    flat = t.reshape(-1)
    return (flat[0:1024].reshape(1, 1024), flat[4096:4096 + _RPB_SIZE].reshape(1, 8, 15, 31),
            flat[1024:2048].reshape(1, 1024), flat[2048:3072].reshape(1, 1024), flat[3072:4096])


def kernel(x, p, positions, g_mix, w_in, rpb, w_branch_na, w_branch_dil, w_out, g_mlp, w_up, w_down, g_ple, w_ple_gate, w_ple_proj, g_final, loss_target, m_g_mix, m_w_in, m_rpb, m_w_branch_na, m_w_branch_dil, m_w_out, m_g_mlp, m_w_up, m_w_down, m_g_ple, m_w_ple_gate, m_w_ple_proj, m_g_final, v_g_mix, v_w_in, v_rpb, v_w_branch_na, v_w_branch_dil, v_w_out, v_g_mlp, v_w_up, v_w_down, v_g_ple, v_w_ple_gate, v_w_ple_proj, v_g_final):
    sharded = dict(w_in=(w_in, m_w_in, v_w_in), w_branch_na=(w_branch_na, m_w_branch_na, v_w_branch_na),
                   w_branch_dil=(w_branch_dil, m_w_branch_dil, v_w_branch_dil), w_out=(w_out, m_w_out, v_w_out),
                   w_up=(w_up, m_w_up, v_w_up), w_down=(w_down, m_w_down, v_w_down),
                   w_ple_gate=(w_ple_gate, m_w_ple_gate, v_w_ple_gate),
                   w_ple_proj=(w_ple_proj, m_w_ple_proj, v_w_ple_proj))
    shards = {k: tuple(t[0] for t in val) for k, val in sharded.items()}

    w_in_full = _gather_w_in(shards["w_in"][0])
    rest = [shards[name][0].astype(BF16) for name, _, _ in _WEIGHTS[1:]]

    loss, dx, parts, dgs, drpb = _local_step(
        x[0], p[0, 0].astype(BF16), positions[0], loss_target[0],
        g_mix, g_mlp, g_ple, g_final.reshape(1, -1), rpb[0], w_in_full, rest, True)

    drpb3 = drpb.reshape(8, 16, 32)[:, :15, :31]
    small = _pack_small(dgs[0], dgs[1], dgs[2], dgs[3], drpb3, loss)
    small_all, = _comm_only(_plan_share(small), name="share_small")

    out = {}
    for (name, _, _), part in zip(_WEIGHTS, parts):
        w, m, v = shards[name]
        res = _sum_adamw(part, w, m, v, tr=min(128, w.shape[0]), name="adamw_" + name)
        out[name] = [t[None] for t in res]
    small_w = _pack_small(g_mix, g_mlp, g_ple, g_final, rpb, jnp.zeros((128,), F32))
    small_m = _pack_small(m_g_mix, m_g_mlp, m_g_ple, m_g_final, m_rpb, jnp.zeros((128,), F32))
    small_v = _pack_small(v_g_mix, v_g_mlp, v_g_ple, v_g_final, v_rpb, jnp.zeros((128,), F32))
    res = _sum_adamw(small_all, small_w, small_m, small_v, tr=64, name="adamw_small")
    unpacked = [_unpack_small(t) for t in res]
    for i, name in enumerate(("g_mix", "rpb", "g_mlp", "g_ple", "g_final")):
        out[name] = [u[i] for u in unpacked]
    loss_total = res[0][62, 0]

    order = ("g_mix", "w_in", "rpb", "w_branch_na", "w_branch_dil", "w_out", "g_mlp", "w_up", "w_down",
             "g_ple", "w_ple_gate", "w_ple_proj", "g_final")
    grads = [out[k][0] for k in order]
    deltas = [out[k][1] for k in order]
    new_m = [out[k][2] for k in order]
    new_v = [out[k][3] for k in order]
    return (loss_total, dx[None], *grads, *deltas, *new_m, *new_v)
```

```python
import functools

import numpy as np
import jax
import jax.numpy as jnp
from jax import lax
from jax.experimental import pallas as pl
from jax.experimental.pallas import tpu as pltpu

F32 = jnp.float32
BF16 = jnp.bfloat16

D_MODEL = 1024
HEAD_DIM = 64
GRID_W = 64
NA_WIDTH = 512
DIL_WIDTH = 768
DIL_OUT = 256
D_FF = 4096
IN_WIDTH = 5888
DIL_DILATIONS = (1, 4, 16)
DIL_RADIUS = 64
NA_WIN_ROWS = 8
RMS_EPS = 1e-6
NEG_INF = -1e30
QK_SCALE = HEAD_DIM ** -0.5

ADAM_LR = 0.001
ADAM_B1 = 0.9
ADAM_B2 = 0.999
ADAM_EPS = 1e-08
ADAM_WD = 0.01
ADAM_STEP = 10

N_DEV = 8
VMEM_LIMIT = 56 * 1024 * 1024
MESH = pl.DeviceIdType.MESH

NT_DIMS = (((1,), (1,)), ((), ()))
TN_DIMS = (((0,), (0,)), ((), ()))


def _sds(shape, dtype):
    return jax.ShapeDtypeStruct(shape, dtype)


def _params(*sem):
    return pltpu.CompilerParams(dimension_semantics=sem, vmem_limit_bytes=VMEM_LIMIT)


def _rows(tm, width, col=0):
    return pl.BlockSpec((tm, width), lambda i, c=col: (i, c))


def _const(shape):
    zeros = (0,) * len(shape)
    return pl.BlockSpec(shape, lambda i: zeros)


def _my_index():
    return 4 * lax.axis_index("x") + 2 * lax.axis_index("y") + lax.axis_index("c")


def _peer(k):
    x, y, c = lax.axis_index("x"), lax.axis_index("y"), lax.axis_index("c")
    px = 1 - x if k & 4 else x
    py = 1 - y if k & 2 else y
    pc = 1 - c if k & 1 else c
    return (px, py, pc), 4 * px + 2 * py + pc


class _CommPlan:
    def __init__(self, ins, out_shapes, n_remote, n_local, start, wait):
        self.ins, self.out_shapes = list(ins), list(out_shapes)
        self.n_remote, self.n_local = n_remote, n_local
        self.start, self.wait = start, wait


def _plan_all_gather(shards):
    n_w = len(shards)

    def remote(ins, outs, send, recv, w, k, slot):
        dev, idx = _peer(k)
        return pltpu.make_async_remote_copy(
            src_ref=ins[w], dst_ref=outs[w].at[idx if slot is None else slot],
            send_sem=send.at[w * 7 + k - 1], recv_sem=recv.at[w * 7 + k - 1],
            device_id=dev, device_id_type=MESH)

    def start(ins, outs, send, recv, local):
        me = _my_index()
        for w in range(n_w):
            pltpu.make_async_copy(ins[w], outs[w].at[me], local.at[w]).start()
            for k in range(1, N_DEV):
                remote(ins, outs, send, recv, w, k, me).start()

    def wait(ins, outs, send, recv, local):
        me = _my_index()
        for w in range(n_w):
            for k in range(1, N_DEV):
                remote(ins, outs, send, recv, w, k, None).wait()
            pltpu.make_async_copy(ins[w], outs[w].at[me], local.at[w]).wait()

    return _CommPlan(shards, [_sds((N_DEV,) + s.shape, s.dtype) for s in shards], n_w * 7, n_w, start, wait)


def _plan_exchange(chunked):
    n_w = len(chunked)

    def remote(ins, outs, send, recv, w, k):
        dev, idx = _peer(k)
        return pltpu.make_async_remote_copy(
            src_ref=ins[w].at[idx], dst_ref=outs[w].at[k],
            send_sem=send.at[w * 7 + k - 1], recv_sem=recv.at[w * 7 + k - 1],
            device_id=dev, device_id_type=MESH)

    def start(ins, outs, send, recv, local):
        me = _my_index()
        for w in range(n_w):
            pltpu.make_async_copy(ins[w].at[me], outs[w].at[0], local.at[w]).start()
        for k in range(1, N_DEV):
            for w in range(n_w):
                remote(ins, outs, send, recv, w, k).start()

    def wait(ins, outs, send, recv, local):
        me = _my_index()
        for k in range(1, N_DEV):
            for w in range(n_w):
                remote(ins, outs, send, recv, w, k).wait()
        for w in range(n_w):
            pltpu.make_async_copy(ins[w].at[me], outs[w].at[0], local.at[w]).wait()

    return _CommPlan(chunked, [_sds(t.shape, t.dtype) for t in chunked], n_w * 7, n_w, start, wait)


def _plan_share(block):
    def remote(ins, outs, send, recv, k, slot):
        dev, idx = _peer(k)
        return pltpu.make_async_remote_copy(
            src_ref=ins[0], dst_ref=outs[0].at[idx if slot is None else slot],
            send_sem=send.at[k - 1], recv_sem=recv.at[k - 1], device_id=dev, device_id_type=MESH)

    def start(ins, outs, send, recv, local):
        me = _my_index()
        pltpu.make_async_copy(ins[0], outs[0].at[me], local.at[0]).start()
        for k in range(1, N_DEV):
            remote(ins, outs, send, recv, k, me).start()

    def wait(ins, outs, send, recv, local):
        for k in range(1, N_DEV):
            remote(ins, outs, send, recv, k, None).wait()
        pltpu.make_async_copy(ins[0], outs[0].at[_my_index()], local.at[0]).wait()

    return _CommPlan([block], [_sds((N_DEV,) + block.shape, block.dtype)], 7, 1, start, wait)


def _hosted_call(body, plan, *, name, grid, in_specs, out_specs, out_shape, scratch_shapes, args):
    n_in, n_out, n_scr = len(in_specs), len(out_specs), len(scratch_shapes)
    sem = ("arbitrary",) * len(grid)
    if plan is None:
        def plain(*refs):
            body(refs[:n_in], refs[n_in:n_in + n_out], refs[n_in + n_out:])

        res = pl.pallas_call(plain, name=name, grid=grid, in_specs=in_specs, out_specs=out_specs,
                             out_shape=out_shape, scratch_shapes=scratch_shapes,
                             compiler_params=_params(*sem))(*args)
        return list(res), None

    n_ci, n_co = len(plan.ins), len(plan.out_shapes)
    hbm = pl.BlockSpec(memory_space=pl.ANY)

    def hosted(*refs):
        ins, refs = refs[:n_in], refs[n_in:]
        c_ins, refs = refs[:n_ci], refs[n_ci:]
        outs, refs = refs[:n_out], refs[n_out:]
        c_outs, refs = refs[:n_co], refs[n_co:]
        scr, sems = refs[:n_scr], refs[n_scr:]
        first = functools.reduce(jnp.logical_and, [pl.program_id(d) == 0 for d in range(len(grid))])
        last = functools.reduce(jnp.logical_and, [pl.program_id(d) == grid[d] - 1 for d in range(len(grid))])

        @pl.when(first)
        def _():
            plan.start(c_ins, c_outs, *sems)

        body(ins, outs, scr)

        @pl.when(last)
        def _():
            plan.wait(c_ins, c_outs, *sems)

    res = pl.pallas_call(
        hosted, name=name, grid=grid,
        in_specs=list(in_specs) + [hbm] * n_ci, out_specs=list(out_specs) + [hbm] * n_co,
        out_shape=list(out_shape) + plan.out_shapes,
        scratch_shapes=list(scratch_shapes) + [pltpu.SemaphoreType.DMA((plan.n_remote,)),
                                               pltpu.SemaphoreType.DMA((plan.n_remote,)),
                                               pltpu.SemaphoreType.DMA((plan.n_local,))],
        compiler_params=_params(*sem),
    )(*args, *plan.ins)
    return list(res[:n_out]), list(res[n_out:])


def _comm_only(plan, *, name):
    hbm = pl.BlockSpec(memory_space=pl.ANY)
    n_ci, n_co = len(plan.ins), len(plan.out_shapes)

    def body(*refs):
        c_ins, c_outs, sems = refs[:n_ci], refs[n_ci:n_ci + n_co], refs[n_ci + n_co:]
        plan.start(c_ins, c_outs, *sems)
        plan.wait(c_ins, c_outs, *sems)

    return pl.pallas_call(
        body, name=name, in_specs=[hbm] * n_ci, out_specs=[hbm] * n_co, out_shape=plan.out_shapes,
        scratch_shapes=[pltpu.SemaphoreType.DMA((plan.n_remote,)), pltpu.SemaphoreType.DMA((plan.n_remote,)),
                        pltpu.SemaphoreType.DMA((plan.n_local,))],
        compiler_params=pltpu.CompilerParams(vmem_limit_bytes=VMEM_LIMIT),
    )(*plan.ins)


def _matmul(a, b, *, ta=False, tb=False, out_dtype, tm, tn, tk, name, comm=None):
    m, k = (a.shape[1], a.shape[0]) if ta else a.shape
    n = b.shape[0] if tb else b.shape[1]
    tm, tn, tk = min(tm, m), min(tn, n), min(tk, k)
    nk = k // tk
    dims = (((0 if ta else 1,), (1 if tb else 0,)), ((), ()))

    def body(ins, outs, acc):
        a_ref, b_ref = ins
        o_ref, = outs
        part = lax.dot_general(a_ref[...], b_ref[...], dims, preferred_element_type=F32)
        if nk == 1:
            o_ref[...] = part.astype(o_ref.dtype)
            return
        acc_ref, = acc
        kk = pl.program_id(2)

        @pl.when(kk == 0)
        def _():
            acc_ref[...] = part

        @pl.when(kk > 0)
        def _():
            acc_ref[...] += part

        @pl.when(kk == nk - 1)
        def _():
            o_ref[...] = acc_ref[...].astype(o_ref.dtype)

    a_spec = (pl.BlockSpec((tk, tm), lambda j, i, kk: (kk, i)) if ta
              else pl.BlockSpec((tm, tk), lambda j, i, kk: (i, kk)))
    b_spec = (pl.BlockSpec((tn, tk), lambda j, i, kk: (j, kk)) if tb
              else pl.BlockSpec((tk, tn), lambda j, i, kk: (kk, j)))
    res, comm_res = _hosted_call(
        body, comm, name=name, grid=(n // tn, m // tm, nk),
        in_specs=[a_spec, b_spec],
        out_specs=[pl.BlockSpec((tm, tn), lambda j, i, kk: (i, j))],
        out_shape=[_sds((m, n), out_dtype)],
        scratch_shapes=[] if nk == 1 else [pltpu.VMEM((tm, tn), F32)],
        args=(a, b))
    return res[0] if comm is None else (res[0], comm_res)


def _rstd(h):
    return lax.rsqrt(jnp.mean(h * h, axis=-1, keepdims=True) + RMS_EPS)


def _sigmoid(z):
    return 1.0 / (1.0 + jnp.exp(-z))


def _rms_fwd(x, g, *, tm, name):
    n = x.shape[0]

    def body(x_ref, g_ref, o_ref):
        h = x_ref[...]
        o_ref[...] = (h * _rstd(h) * g_ref[...]).astype(BF16)

    return pl.pallas_call(
        body, name=name, grid=(n // tm,),
        in_specs=[_rows(tm, D_MODEL), _const((1, D_MODEL))],
        out_specs=_rows(tm, D_MODEL), out_shape=_sds((n, D_MODEL), BF16),
        compiler_params=_params("parallel"),
    )(x, g)


def _swap_halves(t):
    width = t.shape[1]
    lane = lax.broadcasted_iota(jnp.int32, t.shape, 1)
    return jnp.where((lane & 63) < 32, pltpu.roll(t, width - 32, 1), pltpu.roll(t, 32, 1))


def _dil_spec(dil, tm):
    return pl.BlockSpec((dil, tm // dil, 256), lambda i: (0, i, 0))


def _dil_scratch(tm):
    return pltpu.VMEM((2, tm, 128), F32)


def _load_token_order(src, scr, dil, tm):
    if dil == 1:
        return src[0]
    for j in range(dil):
        for c in range(2):
            scr[c, pl.ds(j, tm // dil, stride=dil), :] = src[j, :, c * 128:(c + 1) * 128]
    return jnp.concatenate([scr[0], scr[1]], axis=1)


def _store_dil_order(val, dst, scr, dil, tm):
    if dil == 1:
        dst[0] = val.astype(dst.dtype)
        return
    for c in range(2):
        scr[c] = val[:, c * 128:(c + 1) * 128]
    for j in range(dil):
        for c in range(2):
            dst[j, :, c * 128:(c + 1) * 128] = scr[c, pl.ds(j, tm // dil, stride=dil), :].astype(dst.dtype)


def _split_proj(proj, cos_t, sin_t, *, tm, name):
    n = proj.shape[0]
    n_dil = len(DIL_DILATIONS)

    def body(*refs):
        na_in = refs[0:3]
        dil_in = refs[3:3 + 3 * n_dil]
        gate_in = refs[12:20]
        cos_ref, sin_ref = refs[20:22]
        outs = refs[22:]
        na_out = outs[0:3]
        dil_out = outs[3:12]
        sn_ref, sd_ref = outs[12:14]
        scr = outs[14]
        for t in range(3):
            na_out[t][...] = na_in[t][...].astype(BF16)
        cosv, sinv = cos_ref[...], sin_ref[...]
        for t in range(3):
            for gi, dil in enumerate(DIL_DILATIONS):
                val = dil_in[t * n_dil + gi][...]
                if t < 2:
                    val = val * cosv + _swap_halves(val) * sinv
                _store_dil_order(val, dil_out[t * n_dil + gi], scr, dil, tm)
        for c in range(4):
            sn_ref[:, c * 256:(c + 1) * 256] = _sigmoid(gate_in[c][...])
            sd_ref[:, c * 256:(c + 1) * 256] = _sigmoid(gate_in[4 + c][...])

    in_specs = [_rows(tm, NA_WIDTH, c) for c in range(3)]
    in_specs += [_rows(tm, 256, 6 + c) for c in range(9)]
    in_specs += [_rows(tm, 256, 15 + c) for c in range(8)]
    in_specs += [_rows(tm, 256), _rows(tm, 256)]
    out_specs = [_rows(tm, NA_WIDTH)] * 3
    out_shape = [_sds((n, NA_WIDTH), BF16)] * 3
    for _ in range(3):
        for dil in DIL_DILATIONS:
            out_specs.append(pl.BlockSpec((dil, tm // dil, 256), lambda i: (0, i, 0)))
            out_shape.append(_sds((dil, n // dil, 256), BF16))
    out_specs += [_rows(tm, D_MODEL)] * 2
    out_shape += [_sds((n, D_MODEL), F32)] * 2
    res = pl.pallas_call(
        body, name=name, grid=(n // tm,),
        in_specs=in_specs, out_specs=out_specs, out_shape=out_shape,
        scratch_shapes=[_dil_scratch(tm)],
        compiler_params=_params("parallel"),
    )(*([proj] * 20), cos_t, sin_t)
    return res[0:3], res[3:6], res[6:9], res[9:12], res[12], res[13]


def _gate_mix(sn, bn, sd, bd, *, tm, name):
    n = sn.shape[0]

    def body(sn_ref, bn_ref, sd_ref, bd_ref, o_ref):
        o_ref[...] = (sn_ref[...] * bn_ref[...] + sd_ref[...] * bd_ref[...]).astype(BF16)

    return pl.pallas_call(
        body, name=name, grid=(n // tm,), in_specs=[_rows(tm, D_MODEL)] * 4,
        out_specs=_rows(tm, D_MODEL), out_shape=_sds((n, D_MODEL), BF16),
        compiler_params=_params("parallel"),
    )(sn, bn, sd, bd)


def _residual_rms(h, delta, g, *, tm, name):
    n = h.shape[0]

    def body(h_ref, d_ref, g_ref, hn_ref, z_ref):
        hn = h_ref[...] + d_ref[...]
        hn_ref[...] = hn
        z_ref[...] = (hn * _rstd(hn) * g_ref[...]).astype(BF16)

    return pl.pallas_call(
        body, name=name, grid=(n // tm,),
        in_specs=[_rows(tm, D_MODEL), _rows(tm, D_MODEL), _const((1, D_MODEL))],
        out_specs=[_rows(tm, D_MODEL)] * 2,
        out_shape=[_sds((n, D_MODEL), F32), _sds((n, D_MODEL), BF16)],
        compiler_params=_params("parallel"),
    )(h, delta, g)


def _relu_sq(u, *, tm, name):
    n, w = u.shape

    def body(u_ref, f_ref):
        r = jnp.maximum(u_ref[...], 0.0)
        f_ref[...] = (r * r).astype(BF16)

    return pl.pallas_call(
        body, name=name, grid=(n // tm,), in_specs=[_rows(tm, w)],
        out_specs=_rows(tm, w), out_shape=_sds((n, w), BF16),
        compiler_params=_params("parallel"),
    )(u)


def _relu_sq_bwd(df, u, *, tm, name):
    n, w = u.shape

    def body(df_ref, u_ref, o_ref):
        o_ref[...] = (df_ref[...] * (2.0 * jnp.maximum(u_ref[...], 0.0))).astype(BF16)

    return pl.pallas_call(
        body, name=name, grid=(n // tm,), in_specs=[_rows(tm, w)] * 2,
        out_specs=_rows(tm, w), out_shape=_sds((n, w), BF16),
        compiler_params=_params("parallel"),
    )(df, u)


def _tail(h2, gt, pp, target, g_final, *, tm, name):
    n = h2.shape[0]

    def body(h2_ref, gt_ref, pp_ref, t_ref, g_ref, dh3_ref, dpp_ref, dgt_ref, dg_ref, loss_ref):
        i = pl.program_id(0)

        @pl.when(i == 0)
        def _():
            dg_ref[...] = jnp.zeros_like(dg_ref)
            loss_ref[...] = jnp.zeros_like(loss_ref)

        sg = _sigmoid(gt_ref[...])
        pp_v = pp_ref[...]
        h3 = h2_ref[...] + sg * pp_v
        r3 = _rstd(h3)
        n3 = h3 * r3
        g = g_ref[...]
        err = n3 * g - t_ref[...]
        loss_ref[...] += 0.5 * jnp.sum(jnp.sum(err * err, axis=-1, keepdims=True) / D_MODEL)
        dy = err / D_MODEL
        dg_ref[...] += jnp.sum(dy * n3, axis=0, keepdims=True)
        dn = dy * g
        dh3 = r3 * (dn - n3 * jnp.mean(dn * n3, axis=-1, keepdims=True))
        dh3_ref[...] = dh3
        dpp_ref[...] = (dh3 * sg).astype(BF16)
        dgt_ref[...] = (dh3 * pp_v * sg * (1.0 - sg)).astype(BF16)

    return pl.pallas_call(
        body, name=name, grid=(n // tm,),
        in_specs=[_rows(tm, D_MODEL)] * 4 + [_const((1, D_MODEL))],
        out_specs=[_rows(tm, D_MODEL)] * 3 + [_const((1, D_MODEL)), _const((1, 128))],
        out_shape=[_sds((n, D_MODEL), F32), _sds((n, D_MODEL), BF16), _sds((n, D_MODEL), BF16),
                   _sds((1, D_MODEL), F32), _sds((1, 128), F32)],
        compiler_params=_params("arbitrary"),
    )(h2, gt, pp, target, g_final)


def _rms_bwd(dz, h, g, dres, *, tm, name, want_bf16=True):
    n = h.shape[0]

    def body(dz_ref, h_ref, g_ref, dres_ref, dh_ref, *rest):
        if want_bf16:
            dhb_ref, dg_ref = rest
        else:
            dg_ref, = rest
        i = pl.program_id(0)

        @pl.when(i == 0)
        def _():
            dg_ref[...] = jnp.zeros_like(dg_ref)

        hv = h_ref[...]
        r = _rstd(hv)
        nrm = hv * r
        dz_v = dz_ref[...]
        dg_ref[...] += jnp.sum(dz_v * nrm, axis=0, keepdims=True)
        dn = dz_v * g_ref[...]
        dh = dres_ref[...] + r * (dn - nrm * jnp.mean(dn * nrm, axis=-1, keepdims=True))
        dh_ref[...] = dh
        if want_bf16:
            dhb_ref[...] = dh.astype(BF16)

    out_specs = [_rows(tm, D_MODEL)]
    out_shape = [_sds((n, D_MODEL), F32)]
    if want_bf16:
        out_specs.append(_rows(tm, D_MODEL))
        out_shape.append(_sds((n, D_MODEL), BF16))
    out_specs.append(_const((1, D_MODEL)))
    out_shape.append(_sds((1, D_MODEL), F32))
    return pl.pallas_call(
        body, name=name, grid=(n // tm,),
        in_specs=[_rows(tm, D_MODEL), _rows(tm, D_MODEL), _const((1, D_MODEL)), _rows(tm, D_MODEL)],
        out_specs=out_specs, out_shape=out_shape,
        compiler_params=_params("arbitrary"),
    )(dz, h, g, dres)


def _gate_bwd(dmixed, sn, bn, sd, bd, *, tm, name):
    n = sn.shape[0]

    def body(dm_ref, sn_ref, bn_ref, sd_ref, bd_ref, dbn_ref, dbd_ref, dgn_ref, dgd_ref):
        dm = dm_ref[...]
        s1, s2 = sn_ref[...], sd_ref[...]
        dbn_ref[...] = (dm * s1).astype(BF16)
        dbd_ref[...] = (dm * s2).astype(BF16)
        dgn_ref[...] = (dm * bn_ref[...] * s1 * (1.0 - s1)).astype(BF16)
        dgd_ref[...] = (dm * bd_ref[...] * s2 * (1.0 - s2)).astype(BF16)

    return pl.pallas_call(
        body, name=name, grid=(n // tm,), in_specs=[_rows(tm, D_MODEL)] * 5,
        out_specs=[_rows(tm, D_MODEL)] * 4, out_shape=[_sds((n, D_MODEL), BF16)] * 4,
        compiler_params=_params("parallel"),
    )(dmixed, sn, bn, sd, bd)


def _assemble_dproj(dna, ddil_q, ddil_k, ddil_v, dgn, dgd, cos_t, sin_t, *, tm, name):
    n = dgn.shape[0]

    def body(*refs):
        dq_ref, dk_ref, dv_ref = refs[0:3]
        dil_in = refs[3:12]
        dgn_ref, dgd_ref, cos_ref, sin_ref, o_ref, scr = refs[12:18]
        o_ref[:, 0:512] = dq_ref[...]
        o_ref[:, 512:1024] = dk_ref[...].astype(BF16)
        o_ref[:, 1024:1536] = dv_ref[...].astype(BF16)
        cosv, sinv = cos_ref[...], sin_ref[...]
        for t in range(3):
            for gi, dil in enumerate(DIL_DILATIONS):
                val = _load_token_order(dil_in[t * 3 + gi], scr, dil, tm)
                if t < 2:
                    val = val * cosv + _swap_halves(val * sinv)
                c0 = 1536 + t * DIL_WIDTH + gi * 256
                o_ref[:, c0:c0 + 256] = val.astype(BF16)
        o_ref[:, 3840:4864] = dgn_ref[...]
        o_ref[:, 4864:5888] = dgd_ref[...]

    in_specs = [_rows(tm, NA_WIDTH)] * 3
    for _ in range(3):
        for dil in DIL_DILATIONS:
            in_specs.append(pl.BlockSpec((dil, tm // dil, 256), lambda i: (0, i, 0)))
    in_specs += [_rows(tm, D_MODEL)] * 2 + [_rows(tm, 256)] * 2
    return pl.pallas_call(
        body, name=name, grid=(n // tm,), in_specs=in_specs,
        out_specs=_rows(tm, IN_WIDTH), out_shape=_sds((n, IN_WIDTH), BF16),
        scratch_shapes=[_dil_scratch(tm)],
        compiler_params=_params("parallel"),
    )(*dna, *ddil_q, *ddil_k, *ddil_v, dgn, dgd, cos_t, sin_t)


N_ROW_OFF = 2 * NA_WIN_ROWS - 1
N_PAIRS = N_ROW_OFF - 1
RB_WIDTH = (N_ROW_OFF + 1) * GRID_W


def _na_bias(rb_ref, pair_scr):
    shape = (GRID_W, RB_WIDTH)
    qc = lax.broadcasted_iota(jnp.int32, shape, 0)
    qc2 = lax.broadcasted_iota(jnp.int32, (GRID_W, 128), 0)
    kc2 = lax.broadcasted_iota(jnp.int32, (GRID_W, 128), 1) & (GRID_W - 1)
    cs = jnp.clip(qc2 - 8, 0, GRID_W - 16)
    valid = (kc2 >= cs) & (kc2 < cs + 16)
    for hh in range(2):
        t = jnp.broadcast_to(rb_ref[hh], shape)
        t = pltpu.roll(t, RB_WIDTH - 15, 1)
        for b in range(6):
            t = jnp.where(((qc >> b) & 1) == 1, pltpu.roll(t, 1 << b, 1), t)
        t_odd = pltpu.roll(t, RB_WIDTH - GRID_W, 1)
        for ro in range(N_PAIRS):
            src = t if ro % 2 == 0 else t_odd
            base = (ro // 2) * 128
            pair_scr[hh, ro] = jnp.where(valid, src[:, base:base + 128], NEG_INF)


NA_GROUP_FWD = 4
NA_GROUP_BWD = 4


def _stack_heads(ref, r):
    lane = lax.broadcasted_iota(jnp.int32, (GRID_W, 128), 1)
    t = ref[pl.ds(pl.multiple_of(r * GRID_W, GRID_W), GRID_W), :].astype(F32)
    return jnp.concatenate([jnp.where(lane < 64, t, 0.0), jnp.where(lane >= 64, t, 0.0)], axis=0).astype(BF16)


def _unstack_heads(t2):
    lane = lax.broadcasted_iota(jnp.int32, (GRID_W, 128), 1)
    return jnp.where(lane < 64, t2[:GRID_W], t2[GRID_W:])


def _na_window(k_ref, v_ref, r, n_rows):
    rs = jnp.clip(r - NA_WIN_ROWS // 2, 0, n_rows - NA_WIN_ROWS)
    ro0 = (NA_WIN_ROWS - 1) - (r - rs)
    off = pl.multiple_of(rs * GRID_W, GRID_W)
    kw = k_ref[pl.ds(off, NA_WIN_ROWS * GRID_W), :]
    vw = v_ref[pl.ds(off, NA_WIN_ROWS * GRID_W), :]
    return kw, vw, off, ro0


def _na_probs(s_raw, pair_scr, ro0):
    bias = [jnp.concatenate([pair_scr[hh, ro0 + 2 * j] for j in range(NA_WIN_ROWS // 2)], axis=1)
            for hh in range(2)]
    s = s_raw * QK_SCALE + jnp.concatenate(bias, axis=0)
    m = jnp.max(s, axis=-1, keepdims=True)
    e = jnp.exp(s - m)
    return e * (1.0 / jnp.sum(e, axis=-1, keepdims=True))


def _na_fwd(q, k, v, rb, *, name, comm=None):
    n = q.shape[0]
    n_rows = n // GRID_W

    def body(ins, outs, scr):
        q_ref, k_ref, v_ref, rb_ref = ins
        o_ref, = outs
        pair_scr, = scr
        _na_bias(rb_ref, pair_scr)

        def group(g, carry):
            rows = [g * NA_GROUP_FWD + t for t in range(NA_GROUP_FWD)]
            wins = [_na_window(k_ref, v_ref, r, n_rows) for r in rows]
            raw = [lax.dot_general(_stack_heads(q_ref, r), w[0], NT_DIMS, preferred_element_type=F32)
                   for r, w in zip(rows, wins)]
            probs = [_na_probs(s, pair_scr, w[3]) for s, w in zip(raw, wins)]
            outs2 = [jnp.dot(p.astype(BF16), w[1], preferred_element_type=F32) for p, w in zip(probs, wins)]
            for r, o2 in zip(rows, outs2):
                o_ref[pl.ds(pl.multiple_of(r * GRID_W, GRID_W), GRID_W), :] = _unstack_heads(o2).astype(BF16)
            return carry

        lax.fori_loop(0, n_rows // NA_GROUP_FWD, group, 0)

    col = pl.BlockSpec((n, 128), lambda h: (0, h))
    res, comm_res = _hosted_call(
        body, comm, name=name, grid=(NA_WIDTH // 128,),
        in_specs=[col, col, col, pl.BlockSpec((2, 1, RB_WIDTH), lambda h: (h, 0, 0))],
        out_specs=[col], out_shape=[_sds((n, NA_WIDTH), BF16)],
        scratch_shapes=[pltpu.VMEM((2, N_PAIRS, GRID_W, 128), F32)],
        args=(q, k, v, rb))
    return res[0] if comm is None else (res[0], comm_res)


def _na_bwd(q, k, v, do, rb, *, name, comm=None):
    n = q.shape[0]
    n_rows = n // GRID_W
    win = NA_WIN_ROWS * GRID_W

    def body(ins, outs, scr):
        q_ref, k_ref, v_ref, do_ref, rb_ref = ins
        dq_ref, dk_ref, dv_ref, drb_ref = outs
        pair_scr, acc_scr = scr
        _na_bias(rb_ref, pair_scr)
        acc_scr[...] = jnp.zeros_like(acc_scr)
        dk_ref[...] = jnp.zeros_like(dk_ref)
        dv_ref[...] = jnp.zeros_like(dv_ref)

        def group(g, carry):
            rows = [g * NA_GROUP_BWD + t for t in range(NA_GROUP_BWD)]
            wins = [_na_window(k_ref, v_ref, r, n_rows) for r in rows]
            qss = [_stack_heads(q_ref, r) for r in rows]
            doss = [_stack_heads(do_ref, r) for r in rows]
            raw = [lax.dot_general(qs, w[0], NT_DIMS, preferred_element_type=F32) for qs, w in zip(qss, wins)]
            dps = [lax.dot_general(dos, w[1], NT_DIMS, preferred_element_type=F32) for dos, w in zip(doss, wins)]
            probs = [_na_probs(s, pair_scr, w[3]) for s, w in zip(raw, wins)]
            dss = [p * (dp - jnp.sum(p * dp, axis=-1, keepdims=True)) for p, dp in zip(probs, dps)]
            dsbs = [ds.astype(BF16) for ds in dss]
            dq2s = [jnp.dot(dsb, w[0], preferred_element_type=F32) for dsb, w in zip(dsbs, wins)]
            dkws = [lax.dot_general(dsb, qs, TN_DIMS, preferred_element_type=F32) for dsb, qs in zip(dsbs, qss)]
            dvws = [lax.dot_general(p.astype(BF16), dos, TN_DIMS, preferred_element_type=F32)
                    for p, dos in zip(probs, doss)]
            for t, r in enumerate(rows):
                _, _, off, ro0 = wins[t]
                for hh in range(2):
                    for j in range(NA_WIN_ROWS // 2):
                        acc_scr[hh, ro0 + 2 * j] += dss[t][hh * GRID_W:(hh + 1) * GRID_W, j * 128:(j + 1) * 128]
                dq_ref[pl.ds(pl.multiple_of(r * GRID_W, GRID_W), GRID_W), :] = (
                    _unstack_heads(dq2s[t]) * QK_SCALE).astype(BF16)
                dk_ref[pl.ds(off, win), :] += dkws[t] * QK_SCALE
                dv_ref[pl.ds(off, win), :] += dvws[t]
            return carry

        lax.fori_loop(0, n_rows // NA_GROUP_BWD, group, 0)

        qc = lax.broadcasted_iota(jnp.int32, (N_PAIRS * GRID_W, 128), 0)
        for hh in range(2):
            t = acc_scr[hh].reshape(N_PAIRS * GRID_W, 128)
            for b in range(6):
                t = jnp.where(((qc >> b) & 1) == 1, pltpu.roll(t, 128 - (1 << b), 1), t)
            t = pltpu.roll(t, 15, 1)
            drb_ref[hh] = jnp.sum(t.reshape(N_PAIRS, GRID_W, 128), axis=1)

    col = pl.BlockSpec((n, 128), lambda h: (0, h))
    res, comm_res = _hosted_call(
        body, comm, name=name, grid=(NA_WIDTH // 128,),
        in_specs=[col, col, col, col, pl.BlockSpec((2, 1, RB_WIDTH), lambda h: (h, 0, 0))],
        out_specs=[col, col, col, pl.BlockSpec((2, N_PAIRS, 128), lambda h: (h, 0, 0))],
        out_shape=[_sds((n, NA_WIDTH), BF16), _sds((n, NA_WIDTH), F32), _sds((n, NA_WIDTH), F32),
                   _sds((8, N_PAIRS, 128), F32)],
        scratch_shapes=[pltpu.VMEM((2, N_PAIRS, GRID_W, 128), F32),
                        pltpu.VMEM((2, N_PAIRS, GRID_W, 128), F32)],
        args=(q, k, v, do, rb))
    return res if comm is None else (res, comm_res)


def _rpb_table(rpb2):
    t = jnp.pad(rpb2, ((0, 0), (0, 1), (0, GRID_W - rpb2.shape[-1])))
    return t.reshape(8, 1, RB_WIDTH)


def _rpb_grad(drb, *, name):
    kdim = drb.shape[1]

    def body(x_ref, o_ref):
        kk = lax.broadcasted_iota(jnp.int32, (128, 512), 0)
        jj = lax.broadcasted_iota(jnp.int32, (128, 512), 1)
        half, co = kk >> 6, kk & 63
        acc = jnp.zeros((8, 512), F32)
        for ro in range(N_PAIRS):
            hit = ((ro + half) == (jj >> 5)) & (co == (jj & 31)) & (co < 31)
            onehot = jnp.where(hit, 1.0, 0.0).astype(F32)
            acc = acc + jnp.dot(x_ref[:, ro * 128:(ro + 1) * 128], onehot, preferred_element_type=F32,
                                precision=lax.Precision.HIGHEST)
        o_ref[...] = acc

    return pl.pallas_call(
        body, name=name, grid=(1,),
        in_specs=[_const((8, kdim))], out_specs=_const((8, 512)), out_shape=_sds((8, 512), F32),
        compiler_params=_params("arbitrary"),
    )(drb)


def _dil_blocks(length):
    qb = min(128, length)
    return qb, min(qb + 2 * DIL_RADIUS, length)


def _dil_scores(q_ref, k_ref, v_ref, i, qb, win, length):
    start = pl.multiple_of(jnp.clip(i * qb - DIL_RADIUS, 0, length - win), DIL_RADIUS)
    kw = k_ref[0, pl.ds(start, win), :]
    vw = v_ref[0, pl.ds(start, win), :]
    qv = q_ref[0].astype(F32)
    lane = lax.broadcasted_iota(jnp.int32, (qb, 256), 1)
    qs = jnp.concatenate([jnp.where((lane >> 6) == h, qv, 0.0) for h in range(4)], axis=0).astype(BF16)
    s = lax.dot_general(qs, kw, NT_DIMS, preferred_element_type=F32) * QK_SCALE
    qi = i * qb + (lax.broadcasted_iota(jnp.int32, (4 * qb, win), 0) & (qb - 1))
    kj = start + lax.broadcasted_iota(jnp.int32, (4 * qb, win), 1)
    s = jnp.where(jnp.abs(qi - kj) <= DIL_RADIUS, s, NEG_INF)
    return s, qs, kw, vw, start, lane


def _pick_heads(stacked, lane, qb):
    out = jnp.zeros((qb, 256), stacked.dtype)
    for h in range(4):
        out = jnp.where((lane >> 6) == h, stacked[h * qb:(h + 1) * qb], out)
    return out


def _stack_head_cols(t, qb):
    return jnp.concatenate([t[:, 64 * h:64 * h + 1] for h in range(4)], axis=0)


def _dil_fwd(q, k, v, *, name):
    dil, length, _ = q.shape
    qb, win = _dil_blocks(length)

    def body(q_ref, k_ref, v_ref, o_ref, lse_ref):
        i = pl.program_id(1)
        s, _, _, vw, _, lane = _dil_scores(q_ref, k_ref, v_ref, i, qb, win, length)
        m = jnp.max(s, axis=-1, keepdims=True)
        lse = m + jnp.log(jnp.sum(jnp.exp(s - m), axis=-1, keepdims=True))
        p = jnp.exp(s - lse)
        o4 = jnp.dot(p.astype(BF16), vw, preferred_element_type=F32)
        o_ref[0] = _pick_heads(o4, lane, qb)
        lse_ref[0] = _pick_heads(jnp.broadcast_to(lse, (4 * qb, 256)), lane, qb)

    seq = pl.BlockSpec((1, length, 256), lambda j, i: (j, 0, 0))
    blk = pl.BlockSpec((1, qb, 256), lambda j, i: (j, i, 0))
    return pl.pallas_call(
        body, name=name, grid=(dil, length // qb),
        in_specs=[blk, seq, seq], out_specs=[blk, blk],
        out_shape=[_sds((dil, length, 256), F32)] * 2,
        compiler_params=_params("parallel", "parallel"),
    )(q, k, v)


def _dil_bwd(q, k, v, do, lse, cc, *, name):
    dil, length, _ = q.shape
    qb, win = _dil_blocks(length)

    def body(q_ref, k_ref, v_ref, do_ref, lse_ref, cc_ref, dq_ref, dk_ref, dv_ref):
        i = pl.program_id(1)

        @pl.when(i == 0)
        def _():
            dk_ref[...] = jnp.zeros_like(dk_ref)
            dv_ref[...] = jnp.zeros_like(dv_ref)

        s, qs, kw, vw, start, lane = _dil_scores(q_ref, k_ref, v_ref, i, qb, win, length)
        p = jnp.exp(s - _stack_head_cols(lse_ref[0], qb))
        dov = do_ref[0].astype(F32)
        dos = jnp.concatenate([jnp.where((lane >> 6) == h, dov, 0.0) for h in range(4)], axis=0).astype(BF16)
        dp = lax.dot_general(dos, vw, NT_DIMS, preferred_element_type=F32)
        ds = p * (dp + _stack_head_cols(cc_ref[0], qb))
        dsb = ds.astype(BF16)
        dq4 = jnp.dot(dsb, kw, preferred_element_type=F32)
        dq_ref[0] = _pick_heads(dq4, lane, qb) * QK_SCALE
        dk_ref[0, pl.ds(start, win), :] += lax.dot_general(dsb, qs, TN_DIMS, preferred_element_type=F32) * QK_SCALE
        dv_ref[0, pl.ds(start, win), :] += lax.dot_general(p.astype(BF16), dos, TN_DIMS, preferred_element_type=F32)

    seq = pl.BlockSpec((1, length, 256), lambda j, i: (j, 0, 0))
    blk = pl.BlockSpec((1, qb, 256), lambda j, i: (j, i, 0))
    return pl.pallas_call(
        body, name=name, grid=(dil, length // qb),
        in_specs=[blk, seq, seq, blk, blk, blk], out_specs=[blk, seq, seq],
        out_shape=[_sds((dil, length, 256), F32)] * 3,
        compiler_params=_params("parallel", "arbitrary"),
    )(q, k, v, do, lse, cc)


def _merge_weights(lses):
    m = jnp.maximum(jnp.maximum(lses[0], lses[1]), lses[2])
    es = [jnp.exp(t - m) for t in lses]
    inv = 1.0 / (es[0] + es[1] + es[2])
    return [e * inv for e in es]


def _dil_merge(outs, lses, *, tm, name):
    n = outs[0].shape[1]

    def body(*refs):
        o_in, l_in = refs[0:3], refs[3:6]
        y_ref, yb_ref, scr = refs[6:9]
        lv = [_load_token_order(l_in[g], scr, d, tm) for g, d in enumerate(DIL_DILATIONS)]
        ws = _merge_weights(lv)
        y = jnp.zeros((tm, 256), F32)
        for g, d in enumerate(DIL_DILATIONS):
            y = y + ws[g] * _load_token_order(o_in[g], scr, d, tm)
        y_ref[...] = y
        yb_ref[...] = y.astype(BF16)

    specs = [_dil_spec(d, tm) for d in DIL_DILATIONS]
    return pl.pallas_call(
        body, name=name, grid=(n // tm,), in_specs=specs + specs,
        out_specs=[_rows(tm, 256)] * 2, out_shape=[_sds((n, 256), F32), _sds((n, 256), BF16)],
        scratch_shapes=[_dil_scratch(tm)],
        compiler_params=_params("parallel"),
    )(*outs, *lses)


def _dil_merge_bwd(dy, y, lses, *, tm, name):
    n = dy.shape[0]

    def body(*refs):
        dy_ref, y_ref = refs[0:2]
        l_in = refs[2:5]
        do_out, cc_out = refs[5:8], refs[8:11]
        scr = refs[11]
        lv = [_load_token_order(l_in[g], scr, d, tm) for g, d in enumerate(DIL_DILATIONS)]
        ws = _merge_weights(lv)
        dyv = dy_ref[...]
        rr = lax.broadcasted_iota(jnp.int32, (256, 256), 0) >> 6
        cc = lax.broadcasted_iota(jnp.int32, (256, 256), 1) >> 6
        ones = jnp.where(rr == cc, 1.0, 0.0).astype(F32)
        tsum = jnp.dot(dyv * y_ref[...], ones, preferred_element_type=F32,
                       precision=lax.Precision.HIGHEST)
        for g, d in enumerate(DIL_DILATIONS):
            _store_dil_order(ws[g] * dyv, do_out[g], scr, d, tm)
            _store_dil_order(-ws[g] * tsum, cc_out[g], scr, d, tm)

    specs = [_dil_spec(d, tm) for d in DIL_DILATIONS]
    res = pl.pallas_call(
        body, name=name, grid=(n // tm,),
        in_specs=[_rows(tm, 256)] * 2 + specs,
        out_specs=specs + specs,
        out_shape=[_sds((d, n // d, 256), BF16) for d in DIL_DILATIONS]
                  + [_sds((d, n // d, 256), F32) for d in DIL_DILATIONS],
        scratch_shapes=[_dil_scratch(tm)],
        compiler_params=_params("parallel"),
    )(dy, y, *lses)
    return res[0:3], res[3:6]


_WEIGHTS = (("w_in", 1, 736), ("w_branch_na", 1, 128), ("w_branch_dil", 1, 128), ("w_out", 0, 128),
            ("w_up", 1, 512), ("w_down", 0, 512), ("w_ple_gate", 0, 128), ("w_ple_proj", 1, 128))
_W_IN, _W_BNA, _W_BD, _W_OUT, _W_UP, _W_DOWN, _W_PG, _W_PP = range(8)
_GATHER_EARLY = (_W_BNA, _W_BD, _W_OUT, _W_PG, _W_PP)
_GATHER_LATE = (_W_UP, _W_DOWN)


def _to_full(widx, gathered):
    if _WEIGHTS[widx][1] == 0:
        return gathered.reshape(-1, gathered.shape[2])
    return jnp.transpose(gathered, (1, 0, 2)).reshape(gathered.shape[1], -1)


def _to_chunks(widx, mat):
    _, axis, width = _WEIGHTS[widx]
    if axis == 0:
        return mat.reshape(N_DEV, width, mat.shape[1])
    return jnp.transpose(mat.reshape(mat.shape[0], N_DEV, width), (1, 0, 2))


def _local_step(x, p_bf16, positions, target, g_mix, g_mlp, g_ple, g_final, rpb2, w_in, rest, distributed):
    tm = 256
    half = HEAD_DIM // 2
    inv_freq = 10000.0 ** (-jnp.arange(half, dtype=F32) / half)
    ang = positions.astype(F32)[:, None] * inv_freq
    cos, sin = jnp.cos(ang), jnp.sin(ang)
    cos_t = jnp.tile(jnp.concatenate([cos, cos], axis=-1), (1, 4))
    sin_t = jnp.tile(jnp.concatenate([-sin, sin], axis=-1), (1, 4))
    rb = _rpb_table(rpb2)
    wts = [w_in] + [None] * 7
    if not distributed:
        wts[1:] = rest

    a = _rms_fwd(x, g_mix, tm=tm, name="rms_mix")
    early = _plan_all_gather([rest[i - 1] for i in _GATHER_EARLY]) if distributed else None
    proj = _matmul(a, w_in, out_dtype=F32, tm=512, tn=2944, tk=1024, name="mm_in", comm=early)
    if distributed:
        proj, got = proj
        for i, g in zip(_GATHER_EARLY, got):
            wts[i] = _to_full(i, g)
    na_qkv, dq_g, dk_g, dv_g, sn, sd = _split_proj(proj, cos_t, sin_t, tm=tm, name="split_proj")
    late = _plan_all_gather([rest[i - 1] for i in _GATHER_LATE]) if distributed else None
    y_na = _na_fwd(*na_qkv, rb, name="na_fwd", comm=late)
    if distributed:
        y_na, got = y_na
        for i, g in zip(_GATHER_LATE, got):
            wts[i] = _to_full(i, g)
    _, w_bna, w_bd, w_out, w_up, w_down, w_pg, w_pp = wts
    d_out, d_lse = [], []
    for g in range(3):
        o, lse = _dil_fwd(dq_g[g], dk_g[g], dv_g[g], name=f"dil_fwd{g}")
        d_out.append(o)
        d_lse.append(lse)
    y_dil, y_dil_b = _dil_merge(d_out, d_lse, tm=tm, name="dil_merge")
    bn = _matmul(y_na, w_bna, out_dtype=F32, tm=512, tn=1024, tk=512, name="mm_bna")
    bd = _matmul(y_dil_b, w_bd, out_dtype=F32, tm=512, tn=1024, tk=256, name="mm_bd")
    mixed = _gate_mix(sn, bn, sd, bd, tm=tm, name="gate_mix")
    mo = _matmul(mixed, w_out, out_dtype=F32, tm=512, tn=1024, tk=1024, name="mm_out")
    h1, c = _residual_rms(x, mo, g_mlp, tm=tm, name="res_rms_mlp")
    u = _matmul(c, w_up, out_dtype=F32, tm=512, tn=2048, tk=1024, name="mm_up")
    f = _relu_sq(u, tm=128, name="relu_sq")
    dn = _matmul(f, w_down, out_dtype=F32, tm=512, tn=1024, tk=2048, name="mm_down")
    h2, e = _residual_rms(h1, dn, g_ple, tm=tm, name="res_rms_ple")
    gt = _matmul(e, w_pg, out_dtype=F32, tm=512, tn=1024, tk=1024, name="mm_pg")
    pp = _matmul(p_bf16, w_pp, out_dtype=F32, tm=512, tn=1024, tk=256, name="mm_pp")

    gws = [None] * 8
    dh3, dpp, dgt, dg_final, loss = _tail(h2, gt, pp, target, g_final, tm=tm, name="tail")
    gws[_W_PP] = _matmul(p_bf16, dpp, ta=True, out_dtype=BF16, tm=256, tn=1024, tk=512, name="mm_gw_pp")
    gws[_W_PG] = _matmul(e, dgt, ta=True, out_dtype=BF16, tm=512, tn=1024, tk=512, name="mm_gw_pg")
    de = _matmul(dgt, w_pg, tb=True, out_dtype=F32, tm=512, tn=1024, tk=1024, name="mm_de")
    dh2, dh2_b, dg_ple = _rms_bwd(de, h2, g_ple, dh3, tm=tm, name="rms_bwd_ple")
    df = _matmul(dh2_b, w_down, tb=True, out_dtype=F32, tm=512, tn=2048, tk=1024, name="mm_df")
    gws[_W_DOWN] = _matmul(f, dh2_b, ta=True, out_dtype=BF16, tm=1024, tn=1024, tk=512, name="mm_gw_down")
    du = _relu_sq_bwd(df, u, tm=128, name="relu_sq_bwd")
    gws[_W_UP] = _matmul(c, du, ta=True, out_dtype=BF16, tm=512, tn=2048, tk=512, name="mm_gw_up")
    dc = _matmul(du, w_up, tb=True, out_dtype=F32, tm=512, tn=1024, tk=2048, name="mm_dc")
    dh1, dh1_b, dg_mlp = _rms_bwd(dc, h1, g_mlp, dh2, tm=tm, name="rms_bwd_mlp")
    dmixed = _matmul(dh1_b, w_out, tb=True, out_dtype=F32, tm=512, tn=1024, tk=1024, name="mm_dmixed")
    gws[_W_OUT] = _matmul(mixed, dh1_b, ta=True, out_dtype=BF16, tm=512, tn=1024, tk=512, name="mm_gw_out")
    dbn, dbd, dgn, dgd = _gate_bwd(dmixed, sn, bn, sd, bd, tm=tm, name="gate_bwd")
    gws[_W_BNA] = _matmul(y_na, dbn, ta=True, out_dtype=BF16, tm=512, tn=1024, tk=512, name="mm_gw_bna")
    dy_na = _matmul(dbn, w_bna, tb=True, out_dtype=BF16, tm=512, tn=512, tk=1024, name="mm_dy_na")
    gws[_W_BD] = _matmul(y_dil_b, dbd, ta=True, out_dtype=BF16, tm=256, tn=1024, tk=512, name="mm_gw_bd")
    dy_dil = _matmul(dbd, w_bd, tb=True, out_dtype=F32, tm=512, tn=256, tk=1024, name="mm_dy_dil")
    others = [i for i in range(8) if i != _W_IN]
    swap = _plan_exchange([_to_chunks(i, gws[i]) for i in others]) if distributed else None
    dna = _na_bwd(*na_qkv, dy_na, rb, name="na_bwd", comm=swap)
    if distributed:
        dna, got = dna
        for i, g in zip(others, got):
            gws[i] = g
    drpb = _rpb_grad(dna[3].reshape(8, -1), name="rpb_grad")
    do_g, cc_g = _dil_merge_bwd(dy_dil, y_dil, d_lse, tm=tm, name="dil_merge_bwd")
    ddq, ddk, ddv = [], [], []
    for g in range(3):
        r = _dil_bwd(dq_g[g], dk_g[g], dv_g[g], do_g[g], d_lse[g], cc_g[g], name=f"dil_bwd{g}")
        ddq.append(r[0])
        ddk.append(r[1])
        ddv.append(r[2])
    dproj = _assemble_dproj(dna[0:3], ddq, ddk, ddv, dgn, dgd, cos_t, sin_t, tm=tm, name="assemble_dproj")
    gw_in = _matmul(a, dproj, ta=True, out_dtype=BF16, tm=512, tn=2944, tk=512, name="mm_gw_in")
    swap_in = _plan_exchange([_to_chunks(_W_IN, gw_in)]) if distributed else None
    da = _matmul(dproj, w_in, tb=True, out_dtype=F32, tm=512, tn=1024, tk=2944, name="mm_da", comm=swap_in)
    if distributed:
        da, (gw_in,) = da
    gws[_W_IN] = gw_in
    dx, dg_mix = _rms_bwd(da, x, g_mix, dh1, tm=tm, name="rms_bwd_mix", want_bf16=False)
    return loss, dx, gws, (dg_mix, dg_mlp, dg_ple, dg_final), drpb


def _gather_w_in(shard):
    def body(w_ref, o_ref):
        o_ref[...] = w_ref[...].astype(BF16)

    rows, cols = shard.shape
    blk = pl.BlockSpec((256, cols), lambda i: (i, 0))
    shard_b = pl.pallas_call(body, name="cast_w_in", grid=(rows // 256,), in_specs=[blk], out_specs=blk,
                             out_shape=_sds(shard.shape, BF16), compiler_params=_params("parallel"))(shard)
    gathered, = _comm_only(_plan_all_gather([shard_b]), name="all_gather_w_in")
    return _to_full(_W_IN, gathered)


def _adamw(w, g, m, v):
    m = ADAM_B1 * m + (1.0 - ADAM_B1) * g
    v = ADAM_B2 * v + (1.0 - ADAM_B2) * (g * g)
    m_hat = m / (1.0 - ADAM_B1 ** ADAM_STEP)
    v_hat = v / (1.0 - ADAM_B2 ** ADAM_STEP)
    delta = -ADAM_LR * (m_hat / (jnp.sqrt(v_hat) + ADAM_EPS) + ADAM_WD * w)
    return delta, m, v


def _sum_adamw(parts, w, m, v, *, tr, name):
    rows, cols = w.shape

    def body(p_ref, w_ref, m_ref, v_ref, g_ref, d_ref, nm_ref, nv_ref):
        g = p_ref[0].astype(F32)
        for s in range(1, N_DEV):
            g = g + p_ref[s].astype(F32)
        g_ref[...] = g
        d_ref[...], nm_ref[...], nv_ref[...] = _adamw(w_ref[...], g, m_ref[...], v_ref[...])

    blk = pl.BlockSpec((tr, cols), lambda i: (i, 0))
    return pl.pallas_call(
        body, name=name, grid=(rows // tr,),
        in_specs=[pl.BlockSpec((N_DEV, tr, cols), lambda i: (0, i, 0)), blk, blk, blk],
        out_specs=[blk] * 4, out_shape=[_sds((rows, cols), F32)] * 4,
        compiler_params=_params("parallel"),
    )(parts, w, m, v)


_RPB_SIZE = 8 * 15 * 31


def _pack_small(g_mix, g_mlp, g_ple, g_final, rpb, loss_row):
    flat = jnp.concatenate([g_mix.reshape(-1), g_mlp.reshape(-1), g_ple.reshape(-1), g_final.reshape(-1),
                            rpb.reshape(-1), jnp.zeros((3840 - _RPB_SIZE,), F32), loss_row.reshape(-1),
                            jnp.zeros((128,), F32)])
    return flat.reshape(64, 128)


def _unpack_small(t):
    flat = t.reshape(-1)
    return (flat[0:1024].reshape(1, 1024), flat[4096:4096 + _RPB_SIZE].reshape(1, 8, 15, 31),
            flat[1024:2048].reshape(1, 1024), flat[2048:3072].reshape(1, 1024), flat[3072:4096])


def kernel(x, p, positions, g_mix, w_in, rpb, w_branch_na, w_branch_dil, w_out, g_mlp, w_up, w_down, g_ple, w_ple_gate, w_ple_proj, g_final, loss_target, m_g_mix, m_w_in, m_rpb, m_w_branch_na, m_w_branch_dil, m_w_out, m_g_mlp, m_w_up, m_w_down, m_g_ple, m_w_ple_gate, m_w_ple_proj, m_g_final, v_g_mix, v_w_in, v_rpb, v_w_branch_na, v_w_branch_dil, v_w_out, v_g_mlp, v_w_up, v_w_down, v_g_ple, v_w_ple_gate, v_w_ple_proj, v_g_final):
    sharded = dict(w_in=(w_in, m_w_in, v_w_in), w_branch_na=(w_branch_na, m_w_branch_na, v_w_branch_na),
                   w_branch_dil=(w_branch_dil, m_w_branch_dil, v_w_branch_dil), w_out=(w_out, m_w_out, v_w_out),
                   w_up=(w_up, m_w_up, v_w_up), w_down=(w_down, m_w_down, v_w_down),
                   w_ple_gate=(w_ple_gate, m_w_ple_gate, v_w_ple_gate),
                   w_ple_proj=(w_ple_proj, m_w_ple_proj, v_w_ple_proj))
    shards = {k: tuple(t[0] for t in val) for k, val in sharded.items()}

    w_in_full = _gather_w_in(shards["w_in"][0])
    rest = [shards[name][0].astype(BF16) for name, _, _ in _WEIGHTS[1:]]

    loss, dx, parts, dgs, drpb = _local_step(
        x[0], p[0, 0].astype(BF16), positions[0], loss_target[0],
        g_mix, g_mlp, g_ple, g_final.reshape(1, -1), rpb[0], w_in_full, rest, True)

    drpb3 = drpb.reshape(8, 16, 32)[:, :15, :31]
    small = _pack_small(dgs[0], dgs[1], dgs[2], dgs[3], drpb3, loss)
    small_all, = _comm_only(_plan_share(small), name="share_small")

    out = {}
    for (name, _, _), part in zip(_WEIGHTS, parts):
        w, m, v = shards[name]
        res = _sum_adamw(part, w, m, v, tr=min(128, w.shape[0]), name="adamw_" + name)
        out[name] = [t[None] for t in res]
    small_w = _pack_small(g_mix, g_mlp, g_ple, g_final, rpb, jnp.zeros((128,), F32))
    small_m = _pack_small(m_g_mix, m_g_mlp, m_g_ple, m_g_final, m_rpb, jnp.zeros((128,), F32))
    small_v = _pack_small(v_g_mix, v_g_mlp, v_g_ple, v_g_final, v_rpb, jnp.zeros((128,), F32))
    res = _sum_adamw(small_all, small_w, small_m, small_v, tr=64, name="adamw_small")
    unpacked = [_unpack_small(t) for t in res]
    for i, name in enumerate(("g_mix", "rpb", "g_mlp", "g_ple", "g_final")):
        out[name] = [u[i] for u in unpacked]
    loss_total = res[0][62, 0]

    order = ("g_mix", "w_in", "rpb", "w_branch_na", "w_branch_dil", "w_out", "g_mlp", "w_up", "w_down",
             "g_ple", "w_ple_gate", "w_ple_proj", "g_final")
    grads = [out[k][0] for k in order]
    deltas = [out[k][1] for k in order]
    new_m = [out[k][2] for k in order]
    new_v = [out[k][3] for k in order]
    return (loss_total, dx[None], *grads, *deltas, *new_m, *new_v)
```

```python
import functools

import numpy as np
import jax
import jax.numpy as jnp
from jax import lax
from jax.experimental import pallas as pl
from jax.experimental.pallas import tpu as pltpu

F32 = jnp.float32
BF16 = jnp.bfloat16

D_MODEL = 1024
HEAD_DIM = 64
GRID_W = 64
NA_WIDTH = 512
DIL_WIDTH = 768
DIL_OUT = 256
D_FF = 4096
IN_WIDTH = 5888
DIL_DILATIONS = (1, 4, 16)
DIL_RADIUS = 64
NA_WIN_ROWS = 8
RMS_EPS = 1e-6
NEG_INF = -1e30
QK_SCALE = HEAD_DIM ** -0.5

ADAM_LR = 0.001
ADAM_B1 = 0.9
ADAM_B2 = 0.999
ADAM_EPS = 1e-08
ADAM_WD = 0.01
ADAM_STEP = 10

N_DEV = 8
VMEM_LIMIT = 56 * 1024 * 1024
MESH = pl.DeviceIdType.MESH

NT_DIMS = (((1,), (1,)), ((), ()))
TN_DIMS = (((0,), (0,)), ((), ()))


def _sds(shape, dtype):
    return jax.ShapeDtypeStruct(shape, dtype)


def _params(*sem):
    return pltpu.CompilerParams(dimension_semantics=sem, vmem_limit_bytes=VMEM_LIMIT)


def _rows(tm, width, col=0):
    return pl.BlockSpec((tm, width), lambda i, c=col: (i, c))


def _const(shape):
    zeros = (0,) * len(shape)
    return pl.BlockSpec(shape, lambda i: zeros)


def _my_index():
    return 4 * lax.axis_index("x") + 2 * lax.axis_index("y") + lax.axis_index("c")


def _peer(k):
    x, y, c = lax.axis_index("x"), lax.axis_index("y"), lax.axis_index("c")
    px = 1 - x if k & 4 else x
    py = 1 - y if k & 2 else y
    pc = 1 - c if k & 1 else c
    return (px, py, pc), 4 * px + 2 * py + pc


class _CommPlan:
    def __init__(self, ins, out_shapes, n_remote, n_local, start, wait):
        self.ins, self.out_shapes = list(ins), list(out_shapes)
        self.n_remote, self.n_local = n_remote, n_local
        self.start, self.wait = start, wait


def _plan_all_gather(shards):
    n_w = len(shards)

    def remote(ins, outs, send, recv, w, k, slot):
        dev, idx = _peer(k)
        return pltpu.make_async_remote_copy(
            src_ref=ins[w], dst_ref=outs[w].at[idx if slot is None else slot],
            send_sem=send.at[w * 7 + k - 1], recv_sem=recv.at[w * 7 + k - 1],
            device_id=dev, device_id_type=MESH)

    def start(ins, outs, send, recv, local):
        me = _my_index()
        for w in range(n_w):
            pltpu.make_async_copy(ins[w], outs[w].at[me], local.at[w]).start()
            for k in range(1, N_DEV):
                remote(ins, outs, send, recv, w, k, me).start()

    def wait(ins, outs, send, recv, local):
        me = _my_index()
        for w in range(n_w):
            for k in range(1, N_DEV):
                remote(ins, outs, send, recv, w, k, None).wait()
            pltpu.make_async_copy(ins[w], outs[w].at[me], local.at[w]).wait()

    return _CommPlan(shards, [_sds((N_DEV,) + s.shape, s.dtype) for s in shards], n_w * 7, n_w, start, wait)


def _plan_exchange(chunked):
    n_w = len(chunked)

    def remote(ins, outs, send, recv, w, k):
        dev, idx = _peer(k)
        return pltpu.make_async_remote_copy(
            src_ref=ins[w].at[idx], dst_ref=outs[w].at[k],
            send_sem=send.at[w * 7 + k - 1], recv_sem=recv.at[w * 7 + k - 1],
            device_id=dev, device_id_type=MESH)

    def start(ins, outs, send, recv, local):
        me = _my_index()
        for w in range(n_w):
            pltpu.make_async_copy(ins[w].at[me], outs[w].at[0], local.at[w]).start()
        for k in range(1, N_DEV):
            for w in range(n_w):
                remote(ins, outs, send, recv, w, k).start()

    def wait(ins, outs, send, recv, local):
        me = _my_index()
        for k in range(1, N_DEV):
            for w in range(n_w):
                remote(ins, outs, send, recv, w, k).wait()
        for w in range(n_w):
            pltpu.make_async_copy(ins[w].at[me], outs[w].at[0], local.at[w]).wait()

    return _CommPlan(chunked, [_sds(t.shape, t.dtype) for t in chunked], n_w * 7, n_w, start, wait)


def _plan_share(block):
    def remote(ins, outs, send, recv, k, slot):
        dev, idx = _peer(k)
        return pltpu.make_async_remote_copy(
            src_ref=ins[0], dst_ref=outs[0].at[idx if slot is None else slot],
            send_sem=send.at[k - 1], recv_sem=recv.at[k - 1], device_id=dev, device_id_type=MESH)

    def start(ins, outs, send, recv, local):
        me = _my_index()
        pltpu.make_async_copy(ins[0], outs[0].at[me], local.at[0]).start()
        for k in range(1, N_DEV):
            remote(ins, outs, send, recv, k, me).start()

    def wait(ins, outs, send, recv, local):
        for k in range(1, N_DEV):
            remote(ins, outs, send, recv, k, None).wait()
        pltpu.make_async_copy(ins[0], outs[0].at[_my_index()], local.at[0]).wait()

    return _CommPlan([block], [_sds((N_DEV,) + block.shape, block.dtype)], 7, 1, start, wait)


def _hosted_call(body, plan, *, name, grid, in_specs, out_specs, out_shape, scratch_shapes, args):
    n_in, n_out, n_scr = len(in_specs), len(out_specs), len(scratch_shapes)
    sem = ("arbitrary",) * len(grid)
    if plan is None:
        def plain(*refs):
            body(refs[:n_in], refs[n_in:n_in + n_out], refs[n_in + n_out:])

        res = pl.pallas_call(plain, name=name, grid=grid, in_specs=in_specs, out_specs=out_specs,
                             out_shape=out_shape, scratch_shapes=scratch_shapes,
                             compiler_params=_params(*sem))(*args)
        return list(res), None

    n_ci, n_co = len(plan.ins), len(plan.out_shapes)
    hbm = pl.BlockSpec(memory_space=pl.ANY)

    def hosted(*refs):
        ins, refs = refs[:n_in], refs[n_in:]
        c_ins, refs = refs[:n_ci], refs[n_ci:]
        outs, refs = refs[:n_out], refs[n_out:]
        c_outs, refs = refs[:n_co], refs[n_co:]
        scr, sems = refs[:n_scr], refs[n_scr:]
        first = functools.reduce(jnp.logical_and, [pl.program_id(d) == 0 for d in range(len(grid))])
        last = functools.reduce(jnp.logical_and, [pl.program_id(d) == grid[d] - 1 for d in range(len(grid))])

        @pl.when(first)
        def _():
            plan.start(c_ins, c_outs, *sems)

        body(ins, outs, scr)

        @pl.when(last)
        def _():
            plan.wait(c_ins, c_outs, *sems)

    res = pl.pallas_call(
        hosted, name=name, grid=grid,
        in_specs=list(in_specs) + [hbm] * n_ci, out_specs=list(out_specs) + [hbm] * n_co,
        out_shape=list(out_shape) + plan.out_shapes,
        scratch_shapes=list(scratch_shapes) + [pltpu.SemaphoreType.DMA((plan.n_remote,)),
                                               pltpu.SemaphoreType.DMA((plan.n_remote,)),
                                               pltpu.SemaphoreType.DMA((plan.n_local,))],
        compiler_params=_params(*sem),
    )(*args, *plan.ins)
    return list(res[:n_out]), list(res[n_out:])


def _comm_only(plan, *, name):
    hbm = pl.BlockSpec(memory_space=pl.ANY)
    n_ci, n_co = len(plan.ins), len(plan.out_shapes)

    def body(*refs):
        c_ins, c_outs, sems = refs[:n_ci], refs[n_ci:n_ci + n_co], refs[n_ci + n_co:]
        plan.start(c_ins, c_outs, *sems)
        plan.wait(c_ins, c_outs, *sems)

    return pl.pallas_call(
        body, name=name, in_specs=[hbm] * n_ci, out_specs=[hbm] * n_co, out_shape=plan.out_shapes,
        scratch_shapes=[pltpu.SemaphoreType.DMA((plan.n_remote,)), pltpu.SemaphoreType.DMA((plan.n_remote,)),
                        pltpu.SemaphoreType.DMA((plan.n_local,))],
        compiler_params=pltpu.CompilerParams(vmem_limit_bytes=VMEM_LIMIT),
    )(*plan.ins)


_HBM_SPEC = pl.BlockSpec(memory_space=pltpu.HBM)
_SEM_SPEC = pl.BlockSpec(memory_space=pltpu.SEMAPHORE)
_SIDE_EFFECT = pltpu.SideEffectType.DATAFLOW_SIDE_EFFECTING


def _gather_copies(srcs, lands, send, recv, sending):
    me = _my_index()
    out = []
    for w in range(len(srcs)):
        for k in range(1, N_DEV):
            dev, idx = _peer(k)
            out.append(pltpu.make_async_remote_copy(
                src_ref=srcs[w], dst_ref=lands[w].at[me if sending else idx],
                send_sem=send.at[w * 7 + k - 1], recv_sem=recv.at[w * 7 + k - 1],
                device_id=dev, device_id_type=MESH))
    return out


def _exchange_copies(srcs, lands, send, recv, sending):
    out = []
    for w in range(len(srcs)):
        for k in range(1, N_DEV):
            dev, idx = _peer(k)
            out.append(pltpu.make_async_remote_copy(
                src_ref=srcs[w].at[idx], dst_ref=lands[w].at[k],
                send_sem=send.at[w * 7 + k - 1], recv_sem=recv.at[w * 7 + k - 1],
                device_id=dev, device_id_type=MESH))
    return out


def _start_copies(make, srcs, land_shapes, *, name):
    n = len(srcs)

    def body(*refs):
        src_refs, land_refs = refs[:n], refs[n:2 * n]
        send, recv = refs[2 * n], refs[2 * n + 1]
        token = refs[-1]
        for cp in make(src_refs, land_refs, send, recv, True):
            cp.start()
        token[...] = jnp.zeros_like(token)

    lands = [pltpu.with_memory_space_constraint(lax.empty(s.shape, s.dtype), pltpu.HBM) for s in land_shapes]
    res = pl.pallas_call(
        body, name=name,
        out_shape=(pltpu.SemaphoreType.DMA((7 * n,)), pltpu.SemaphoreType.DMA((7 * n,)),
                   *[pltpu.HBM(s.shape, s.dtype) for s in srcs],
                   *[pltpu.HBM(s.shape, s.dtype) for s in land_shapes],
                   _sds((8, 128), F32)),
        in_specs=[_HBM_SPEC] * (2 * n),
        out_specs=(_SEM_SPEC, _SEM_SPEC, *([_HBM_SPEC] * (2 * n)), pl.BlockSpec(memory_space=pltpu.VMEM)),
        input_output_aliases={i: 2 + i for i in range(2 * n)},
        compiler_params=pltpu.CompilerParams(has_side_effects=_SIDE_EFFECT),
    )(*[pltpu.with_memory_space_constraint(s, pltpu.HBM) for s in srcs], *lands)
    return (n, res[0], res[1], res[2:2 + n], res[2 + n:2 + 2 * n]), res[-1]


def _wait_copies(make, handle, after, *, name):
    n, send_sems, recv_sems, srcs, lands = handle

    def body(*refs):
        src_refs, land_refs = refs[:n], refs[n:2 * n]
        send, recv = refs[2 * n], refs[2 * n + 1]
        for cp in make(src_refs, land_refs, send, recv, False):
            cp.wait_send()
            cp.wait_recv()

    res = pl.pallas_call(
        body, name=name,
        out_shape=tuple(pltpu.HBM(s.shape, s.dtype) for s in (*srcs, *lands)),
        in_specs=[_HBM_SPEC] * (2 * n) + [_SEM_SPEC, _SEM_SPEC, pl.BlockSpec(memory_space=pl.ANY)],
        out_specs=tuple([_HBM_SPEC] * (2 * n)),
        input_output_aliases={i: i for i in range(2 * n)},
        compiler_params=pltpu.CompilerParams(has_side_effects=_SIDE_EFFECT),
    )(*srcs, *lands, send_sems, recv_sems, after)
    return list(res[n:])


def _matmul(a, b, *, ta=False, tb=False, out_dtype, tm, tn, tk, name, comm=None):
    m, k = (a.shape[1], a.shape[0]) if ta else a.shape
    n = b.shape[0] if tb else b.shape[1]
    tm, tn, tk = min(tm, m), min(tn, n), min(tk, k)
    nk = k // tk
    dims = (((0 if ta else 1,), (1 if tb else 0,)), ((), ()))

    def body(ins, outs, acc):
        a_ref, b_ref = ins
        o_ref, = outs
        part = lax.dot_general(a_ref[...], b_ref[...], dims, preferred_element_type=F32)
        if nk == 1:
            o_ref[...] = part.astype(o_ref.dtype)
            return
        acc_ref, = acc
        kk = pl.program_id(2)

        @pl.when(kk == 0)
        def _():
            acc_ref[...] = part

        @pl.when(kk > 0)
        def _():
            acc_ref[...] += part

        @pl.when(kk == nk - 1)
        def _():
            o_ref[...] = acc_ref[...].astype(o_ref.dtype)

    a_spec = (pl.BlockSpec((tk, tm), lambda j, i, kk: (kk, i)) if ta
              else pl.BlockSpec((tm, tk), lambda j, i, kk: (i, kk)))
    b_spec = (pl.BlockSpec((tn, tk), lambda j, i, kk: (j, kk)) if tb
              else pl.BlockSpec((tk, tn), lambda j, i, kk: (kk, j)))
    res, comm_res = _hosted_call(
        body, comm, name=name, grid=(n // tn, m // tm, nk),
        in_specs=[a_spec, b_spec],
        out_specs=[pl.BlockSpec((tm, tn), lambda j, i, kk: (i, j))],
        out_shape=[_sds((m, n), out_dtype)],
        scratch_shapes=[] if nk == 1 else [pltpu.VMEM((tm, tn), F32)],
        args=(a, b))
    return res[0] if comm is None else (res[0], comm_res)


def _rstd(h):
    return lax.rsqrt(jnp.mean(h * h, axis=-1, keepdims=True) + RMS_EPS)


def _sigmoid(z):
    return 1.0 / (1.0 + jnp.exp(-z))


def _rms_fwd(x, g, *, tm, name):
    n = x.shape[0]

    def body(x_ref, g_ref, o_ref):
        h = x_ref[...]
        o_ref[...] = (h * _rstd(h) * g_ref[...]).astype(BF16)

    return pl.pallas_call(
        body, name=name, grid=(n // tm,),
        in_specs=[_rows(tm, D_MODEL), _const((1, D_MODEL))],
        out_specs=_rows(tm, D_MODEL), out_shape=_sds((n, D_MODEL), BF16),
        compiler_params=_params("parallel"),
    )(x, g)


def _swap_halves(t):
    width = t.shape[1]
    lane = lax.broadcasted_iota(jnp.int32, t.shape, 1)
    return jnp.where((lane & 63) < 32, pltpu.roll(t, width - 32, 1), pltpu.roll(t, 32, 1))


def _dil_spec(dil, tm):
    return pl.BlockSpec((dil, tm // dil, 256), lambda i: (0, i, 0))


def _dil_scratch(tm):
    return pltpu.VMEM((2, tm, 128), F32)


def _load_token_order(src, scr, dil, tm):
    if dil == 1:
        return src[0]
    for j in range(dil):
        for c in range(2):
            scr[c, pl.ds(j, tm // dil, stride=dil), :] = src[j, :, c * 128:(c + 1) * 128]
    return jnp.concatenate([scr[0], scr[1]], axis=1)


def _store_dil_order(val, dst, scr, dil, tm):
    if dil == 1:
        dst[0] = val.astype(dst.dtype)
        return
    for c in range(2):
        scr[c] = val[:, c * 128:(c + 1) * 128]
    for j in range(dil):
        for c in range(2):
            dst[j, :, c * 128:(c + 1) * 128] = scr[c, pl.ds(j, tm // dil, stride=dil), :].astype(dst.dtype)


def _split_proj(proj, cos_t, sin_t, *, tm, name):
    n = proj.shape[0]
    n_dil = len(DIL_DILATIONS)

    def body(*refs):
        na_in = refs[0:3]
        dil_in = refs[3:3 + 3 * n_dil]
        gate_in = refs[12:20]
        cos_ref, sin_ref = refs[20:22]
        outs = refs[22:]
        na_out = outs[0:3]
        dil_out = outs[3:12]
        sn_ref, sd_ref = outs[12:14]
        scr = outs[14]
        for t in range(3):
            na_out[t][...] = na_in[t][...].astype(BF16)
        cosv, sinv = cos_ref[...], sin_ref[...]
        for t in range(3):
            for gi, dil in enumerate(DIL_DILATIONS):
                val = dil_in[t * n_dil + gi][...]
                if t < 2:
                    val = val * cosv + _swap_halves(val) * sinv
                _store_dil_order(val, dil_out[t * n_dil + gi], scr, dil, tm)
        for c in range(4):
            sn_ref[:, c * 256:(c + 1) * 256] = _sigmoid(gate_in[c][...])
            sd_ref[:, c * 256:(c + 1) * 256] = _sigmoid(gate_in[4 + c][...])

    in_specs = [_rows(tm, NA_WIDTH, c) for c in range(3)]
    in_specs += [_rows(tm, 256, 6 + c) for c in range(9)]
    in_specs += [_rows(tm, 256, 15 + c) for c in range(8)]
    in_specs += [_rows(tm, 256), _rows(tm, 256)]
    out_specs = [_rows(tm, NA_WIDTH)] * 3
    out_shape = [_sds((n, NA_WIDTH), BF16)] * 3
    for _ in range(3):
        for dil in DIL_DILATIONS:
            out_specs.append(pl.BlockSpec((dil, tm // dil, 256), lambda i: (0, i, 0)))
            out_shape.append(_sds((dil, n // dil, 256), BF16))
    out_specs += [_rows(tm, D_MODEL)] * 2
    out_shape += [_sds((n, D_MODEL), F32)] * 2
    res = pl.pallas_call(
        body, name=name, grid=(n // tm,),
        in_specs=in_specs, out_specs=out_specs, out_shape=out_shape,
        scratch_shapes=[_dil_scratch(tm)],
        compiler_params=_params("parallel"),
    )(*([proj] * 20), cos_t, sin_t)
    return res[0:3], res[3:6], res[6:9], res[9:12], res[12], res[13]


def _gate_mix(sn, bn, sd, bd, *, tm, name):
    n = sn.shape[0]

    def body(sn_ref, bn_ref, sd_ref, bd_ref, o_ref):
        o_ref[...] = (sn_ref[...] * bn_ref[...] + sd_ref[...] * bd_ref[...]).astype(BF16)

    return pl.pallas_call(
        body, name=name, grid=(n // tm,), in_specs=[_rows(tm, D_MODEL)] * 4,
        out_specs=_rows(tm, D_MODEL), out_shape=_sds((n, D_MODEL), BF16),
        compiler_params=_params("parallel"),
    )(sn, bn, sd, bd)


def _residual_rms(h, delta, g, *, tm, name):
    n = h.shape[0]

    def body(h_ref, d_ref, g_ref, hn_ref, z_ref):
        hn = h_ref[...] + d_ref[...]
        hn_ref[...] = hn
        z_ref[...] = (hn * _rstd(hn) * g_ref[...]).astype(BF16)

    return pl.pallas_call(
        body, name=name, grid=(n // tm,),
        in_specs=[_rows(tm, D_MODEL), _rows(tm, D_MODEL), _const((1, D_MODEL))],
        out_specs=[_rows(tm, D_MODEL)] * 2,
        out_shape=[_sds((n, D_MODEL), F32), _sds((n, D_MODEL), BF16)],
        compiler_params=_params("parallel"),
    )(h, delta, g)


def _relu_sq(u, *, tm, name):
    n, w = u.shape

    def body(u_ref, f_ref):
        r = jnp.maximum(u_ref[...], 0.0)
        f_ref[...] = (r * r).astype(BF16)

    return pl.pallas_call(
        body, name=name, grid=(n // tm,), in_specs=[_rows(tm, w)],
        out_specs=_rows(tm, w), out_shape=_sds((n, w), BF16),
        compiler_params=_params("parallel"),
    )(u)


def _relu_sq_bwd(df, u, *, tm, name):
    n, w = u.shape

    def body(df_ref, u_ref, o_ref):
        o_ref[...] = (df_ref[...] * (2.0 * jnp.maximum(u_ref[...], 0.0))).astype(BF16)

    return pl.pallas_call(
        body, name=name, grid=(n // tm,), in_specs=[_rows(tm, w)] * 2,
        out_specs=_rows(tm, w), out_shape=_sds((n, w), BF16),
        compiler_params=_params("parallel"),
    )(df, u)


def _tail(h2, gt, pp, target, g_final, *, tm, name):
    n = h2.shape[0]

    def body(h2_ref, gt_ref, pp_ref, t_ref, g_ref, dh3_ref, dpp_ref, dgt_ref, dg_ref, loss_ref):
        i = pl.program_id(0)

        @pl.when(i == 0)
        def _():
            dg_ref[...] = jnp.zeros_like(dg_ref)
            loss_ref[...] = jnp.zeros_like(loss_ref)

        sg = _sigmoid(gt_ref[...])
        pp_v = pp_ref[...]
        h3 = h2_ref[...] + sg * pp_v
        r3 = _rstd(h3)
        n3 = h3 * r3
        g = g_ref[...]
        err = n3 * g - t_ref[...]
        loss_ref[...] += 0.5 * jnp.sum(jnp.sum(err * err, axis=-1, keepdims=True) / D_MODEL)
        dy = err / D_MODEL
        dg_ref[...] += jnp.sum(dy * n3, axis=0, keepdims=True)
        dn = dy * g
        dh3 = r3 * (dn - n3 * jnp.mean(dn * n3, axis=-1, keepdims=True))
        dh3_ref[...] = dh3
        dpp_ref[...] = (dh3 * sg).astype(BF16)
        dgt_ref[...] = (dh3 * pp_v * sg * (1.0 - sg)).astype(BF16)

    return pl.pallas_call(
        body, name=name, grid=(n // tm,),
        in_specs=[_rows(tm, D_MODEL)] * 4 + [_const((1, D_MODEL))],
        out_specs=[_rows(tm, D_MODEL)] * 3 + [_const((1, D_MODEL)), _const((1, 128))],
        out_shape=[_sds((n, D_MODEL), F32), _sds((n, D_MODEL), BF16), _sds((n, D_MODEL), BF16),
                   _sds((1, D_MODEL), F32), _sds((1, 128), F32)],
        compiler_params=_params("arbitrary"),
    )(h2, gt, pp, target, g_final)


def _rms_bwd(dz, h, g, dres, *, tm, name, want_bf16=True):
    n = h.shape[0]

    def body(dz_ref, h_ref, g_ref, dres_ref, dh_ref, *rest):
        if want_bf16:
            dhb_ref, dg_ref = rest
        else:
            dg_ref, = rest
        i = pl.program_id(0)

        @pl.when(i == 0)
        def _():
            dg_ref[...] = jnp.zeros_like(dg_ref)

        hv = h_ref[...]
        r = _rstd(hv)
        nrm = hv * r
        dz_v = dz_ref[...]
        dg_ref[...] += jnp.sum(dz_v * nrm, axis=0, keepdims=True)
        dn = dz_v * g_ref[...]
        dh = dres_ref[...] + r * (dn - nrm * jnp.mean(dn * nrm, axis=-1, keepdims=True))
        dh_ref[...] = dh
        if want_bf16:
            dhb_ref[...] = dh.astype(BF16)

    out_specs = [_rows(tm, D_MODEL)]
    out_shape = [_sds((n, D_MODEL), F32)]
    if want_bf16:
        out_specs.append(_rows(tm, D_MODEL))
        out_shape.append(_sds((n, D_MODEL), BF16))
    out_specs.append(_const((1, D_MODEL)))
    out_shape.append(_sds((1, D_MODEL), F32))
    return pl.pallas_call(
        body, name=name, grid=(n // tm,),
        in_specs=[_rows(tm, D_MODEL), _rows(tm, D_MODEL), _const((1, D_MODEL)), _rows(tm, D_MODEL)],
        out_specs=out_specs, out_shape=out_shape,
        compiler_params=_params("arbitrary"),
    )(dz, h, g, dres)


def _gate_bwd(dmixed, sn, bn, sd, bd, *, tm, name):
    n = sn.shape[0]

    def body(dm_ref, sn_ref, bn_ref, sd_ref, bd_ref, dbn_ref, dbd_ref, dgn_ref, dgd_ref):
        dm = dm_ref[...]
        s1, s2 = sn_ref[...], sd_ref[...]
        dbn_ref[...] = (dm * s1).astype(BF16)
        dbd_ref[...] = (dm * s2).astype(BF16)
        dgn_ref[...] = (dm * bn_ref[...] * s1 * (1.0 - s1)).astype(BF16)
        dgd_ref[...] = (dm * bd_ref[...] * s2 * (1.0 - s2)).astype(BF16)

    return pl.pallas_call(
        body, name=name, grid=(n // tm,), in_specs=[_rows(tm, D_MODEL)] * 5,
        out_specs=[_rows(tm, D_MODEL)] * 4, out_shape=[_sds((n, D_MODEL), BF16)] * 4,
        compiler_params=_params("parallel"),
    )(dmixed, sn, bn, sd, bd)


def _assemble_dproj(dna, ddil_q, ddil_k, ddil_v, dgn, dgd, cos_t, sin_t, *, tm, name):
    n = dgn.shape[0]

    def body(*refs):
        dq_ref, dk_ref, dv_ref = refs[0:3]
        dil_in = refs[3:12]
        dgn_ref, dgd_ref, cos_ref, sin_ref, o_ref, scr = refs[12:18]
        o_ref[:, 0:512] = dq_ref[...]
        o_ref[:, 512:1024] = dk_ref[...].astype(BF16)
        o_ref[:, 1024:1536] = dv_ref[...].astype(BF16)
        cosv, sinv = cos_ref[...], sin_ref[...]
        for t in range(3):
            for gi, dil in enumerate(DIL_DILATIONS):
                val = _load_token_order(dil_in[t * 3 + gi], scr, dil, tm)
                if t < 2:
                    val = val * cosv + _swap_halves(val * sinv)
                c0 = 1536 + t * DIL_WIDTH + gi * 256
                o_ref[:, c0:c0 + 256] = val.astype(BF16)
        o_ref[:, 3840:4864] = dgn_ref[...]
        o_ref[:, 4864:5888] = dgd_ref[...]

    in_specs = [_rows(tm, NA_WIDTH)] * 3
    for _ in range(3):
        for dil in DIL_DILATIONS:
            in_specs.append(pl.BlockSpec((dil, tm // dil, 256), lambda i: (0, i, 0)))
    in_specs += [_rows(tm, D_MODEL)] * 2 + [_rows(tm, 256)] * 2
    return pl.pallas_call(
        body, name=name, grid=(n // tm,), in_specs=in_specs,
        out_specs=_rows(tm, IN_WIDTH), out_shape=_sds((n, IN_WIDTH), BF16),
        scratch_shapes=[_dil_scratch(tm)],
        compiler_params=_params("parallel"),
    )(*dna, *ddil_q, *ddil_k, *ddil_v, dgn, dgd, cos_t, sin_t)


N_ROW_OFF = 2 * NA_WIN_ROWS - 1
N_PAIRS = N_ROW_OFF - 1
RB_WIDTH = (N_ROW_OFF + 1) * GRID_W


def _na_bias(rb_ref, pair_scr):
    shape = (GRID_W, RB_WIDTH)
    qc = lax.broadcasted_iota(jnp.int32, shape, 0)
    qc2 = lax.broadcasted_iota(jnp.int32, (GRID_W, 128), 0)
    kc2 = lax.broadcasted_iota(jnp.int32, (GRID_W, 128), 1) & (GRID_W - 1)
    cs = jnp.clip(qc2 - 8, 0, GRID_W - 16)
    valid = (kc2 >= cs) & (kc2 < cs + 16)
    for hh in range(2):
        t = jnp.broadcast_to(rb_ref[hh], shape)
        t = pltpu.roll(t, RB_WIDTH - 15, 1)
        for b in range(6):
            t = jnp.where(((qc >> b) & 1) == 1, pltpu.roll(t, 1 << b, 1), t)
        t_odd = pltpu.roll(t, RB_WIDTH - GRID_W, 1)
        for ro in range(N_PAIRS):
            src = t if ro % 2 == 0 else t_odd
            base = (ro // 2) * 128
            pair_scr[hh, ro] = jnp.where(valid, src[:, base:base + 128], NEG_INF)


NA_GROUP_FWD = 4
NA_GROUP_BWD = 4


def _stack_heads(ref, r):
    lane = lax.broadcasted_iota(jnp.int32, (GRID_W, 128), 1)
    t = ref[pl.ds(pl.multiple_of(r * GRID_W, GRID_W), GRID_W), :].astype(F32)
    return jnp.concatenate([jnp.where(lane < 64, t, 0.0), jnp.where(lane >= 64, t, 0.0)], axis=0).astype(BF16)


def _unstack_heads(t2):
    lane = lax.broadcasted_iota(jnp.int32, (GRID_W, 128), 1)
    return jnp.where(lane < 64, t2[:GRID_W], t2[GRID_W:])


def _na_window(k_ref, v_ref, r, n_rows):
    rs = jnp.clip(r - NA_WIN_ROWS // 2, 0, n_rows - NA_WIN_ROWS)
    ro0 = (NA_WIN_ROWS - 1) - (r - rs)
    off = pl.multiple_of(rs * GRID_W, GRID_W)
    kw = k_ref[pl.ds(off, NA_WIN_ROWS * GRID_W), :]
    vw = v_ref[pl.ds(off, NA_WIN_ROWS * GRID_W), :]
    return kw, vw, off, ro0


def _na_probs(s_raw, pair_scr, ro0):
    bias = [jnp.concatenate([pair_scr[hh, ro0 + 2 * j] for j in range(NA_WIN_ROWS // 2)], axis=1)
            for hh in range(2)]
    s = s_raw * QK_SCALE + jnp.concatenate(bias, axis=0)
    m = jnp.max(s, axis=-1, keepdims=True)
    e = jnp.exp(s - m)
    return e * (1.0 / jnp.sum(e, axis=-1, keepdims=True))


def _na_fwd(q, k, v, rb, *, name, comm=None):
    n = q.shape[0]
    n_rows = n // GRID_W

    def body(ins, outs, scr):
        q_ref, k_ref, v_ref, rb_ref = ins
        o_ref, = outs
        pair_scr, = scr
        _na_bias(rb_ref, pair_scr)

        def group(g, carry):
            rows = [g * NA_GROUP_FWD + t for t in range(NA_GROUP_FWD)]
            wins = [_na_window(k_ref, v_ref, r, n_rows) for r in rows]
            raw = [lax.dot_general(_stack_heads(q_ref, r), w[0], NT_DIMS, preferred_element_type=F32)
                   for r, w in zip(rows, wins)]
            probs = [_na_probs(s, pair_scr, w[3]) for s, w in zip(raw, wins)]
            outs2 = [jnp.dot(p.astype(BF16), w[1], preferred_element_type=F32) for p, w in zip(probs, wins)]
            for r, o2 in zip(rows, outs2):
                o_ref[pl.ds(pl.multiple_of(r * GRID_W, GRID_W), GRID_W), :] = _unstack_heads(o2).astype(BF16)
            return carry

        lax.fori_loop(0, n_rows // NA_GROUP_FWD, group, 0)

    col = pl.BlockSpec((n, 128), lambda h: (0, h))
    res, comm_res = _hosted_call(
        body, comm, name=name, grid=(NA_WIDTH // 128,),
        in_specs=[col, col, col, pl.BlockSpec((2, 1, RB_WIDTH), lambda h: (h, 0, 0))],
        out_specs=[col], out_shape=[_sds((n, NA_WIDTH), BF16)],
        scratch_shapes=[pltpu.VMEM((2, N_PAIRS, GRID_W, 128), F32)],
        args=(q, k, v, rb))
    return res[0] if comm is None else (res[0], comm_res)


def _na_bwd(q, k, v, do, rb, *, name, comm=None):
    n = q.shape[0]
    n_rows = n // GRID_W
    win = NA_WIN_ROWS * GRID_W

    def body(ins, outs, scr):
        q_ref, k_ref, v_ref, do_ref, rb_ref = ins
        dq_ref, dk_ref, dv_ref, drb_ref = outs
        pair_scr, acc_scr = scr
        _na_bias(rb_ref, pair_scr)
        acc_scr[...] = jnp.zeros_like(acc_scr)
        dk_ref[...] = jnp.zeros_like(dk_ref)
        dv_ref[...] = jnp.zeros_like(dv_ref)

        def group(g, carry):
            rows = [g * NA_GROUP_BWD + t for t in range(NA_GROUP_BWD)]
            wins = [_na_window(k_ref, v_ref, r, n_rows) for r in rows]
            qss = [_stack_heads(q_ref, r) for r in rows]
            doss = [_stack_heads(do_ref, r) for r in rows]
            raw = [lax.dot_general(qs, w[0], NT_DIMS, preferred_element_type=F32) for qs, w in zip(qss, wins)]
            dps = [lax.dot_general(dos, w[1], NT_DIMS, preferred_element_type=F32) for dos, w in zip(doss, wins)]
            probs = [_na_probs(s, pair_scr, w[3]) for s, w in zip(raw, wins)]
            dss = [p * (dp - jnp.sum(p * dp, axis=-1, keepdims=True)) for p, dp in zip(probs, dps)]
            dsbs = [ds.astype(BF16) for ds in dss]
            dq2s = [jnp.dot(dsb, w[0], preferred_element_type=F32) for dsb, w in zip(dsbs, wins)]
            dkws = [lax.dot_general(dsb, qs, TN_DIMS, preferred_element_type=F32) for dsb, qs in zip(dsbs, qss)]
            dvws = [lax.dot_general(p.astype(BF16), dos, TN_DIMS, preferred_element_type=F32)
                    for p, dos in zip(probs, doss)]
            for t, r in enumerate(rows):
                _, _, off, ro0 = wins[t]
                for hh in range(2):
                    for j in range(NA_WIN_ROWS // 2):
                        acc_scr[hh, ro0 + 2 * j] += dss[t][hh * GRID_W:(hh + 1) * GRID_W, j * 128:(j + 1) * 128]
                dq_ref[pl.ds(pl.multiple_of(r * GRID_W, GRID_W), GRID_W), :] = (
                    _unstack_heads(dq2s[t]) * QK_SCALE).astype(BF16)
                dk_ref[pl.ds(off, win), :] += dkws[t] * QK_SCALE
                dv_ref[pl.ds(off, win), :] += dvws[t]
            return carry

        lax.fori_loop(0, n_rows // NA_GROUP_BWD, group, 0)

        qc = lax.broadcasted_iota(jnp.int32, (N_PAIRS * GRID_W, 128), 0)
        for hh in range(2):
            t = acc_scr[hh].reshape(N_PAIRS * GRID_W, 128)
            for b in range(6):
                t = jnp.where(((qc >> b) & 1) == 1, pltpu.roll(t, 128 - (1 << b), 1), t)
            t = pltpu.roll(t, 15, 1)
            drb_ref[hh] = jnp.sum(t.reshape(N_PAIRS, GRID_W, 128), axis=1)

    col = pl.BlockSpec((n, 128), lambda h: (0, h))
    res, comm_res = _hosted_call(
        body, comm, name=name, grid=(NA_WIDTH // 128,),
        in_specs=[col, col, col, col, pl.BlockSpec((2, 1, RB_WIDTH), lambda h: (h, 0, 0))],
        out_specs=[col, col, col, pl.BlockSpec((2, N_PAIRS, 128), lambda h: (h, 0, 0))],
        out_shape=[_sds((n, NA_WIDTH), BF16), _sds((n, NA_WIDTH), F32), _sds((n, NA_WIDTH), F32),
                   _sds((8, N_PAIRS, 128), F32)],
        scratch_shapes=[pltpu.VMEM((2, N_PAIRS, GRID_W, 128), F32),
                        pltpu.VMEM((2, N_PAIRS, GRID_W, 128), F32)],
        args=(q, k, v, do, rb))
    return res if comm is None else (res, comm_res)


def _rpb_table(rpb2):
    t = jnp.pad(rpb2, ((0, 0), (0, 1), (0, GRID_W - rpb2.shape[-1])))
    return t.reshape(8, 1, RB_WIDTH)


def _rpb_grad(drb, *, name):
    kdim = drb.shape[1]

    def body(x_ref, o_ref):
        kk = lax.broadcasted_iota(jnp.int32, (128, 512), 0)
        jj = lax.broadcasted_iota(jnp.int32, (128, 512), 1)
        half, co = kk >> 6, kk & 63
        acc = jnp.zeros((8, 512), F32)
        for ro in range(N_PAIRS):
            hit = ((ro + half) == (jj >> 5)) & (co == (jj & 31)) & (co < 31)
            onehot = jnp.where(hit, 1.0, 0.0).astype(F32)
            acc = acc + jnp.dot(x_ref[:, ro * 128:(ro + 1) * 128], onehot, preferred_element_type=F32,
                                precision=lax.Precision.HIGHEST)
        o_ref[...] = acc

    return pl.pallas_call(
        body, name=name, grid=(1,),
        in_specs=[_const((8, kdim))], out_specs=_const((8, 512)), out_shape=_sds((8, 512), F32),
        compiler_params=_params("arbitrary"),
    )(drb)


def _dil_blocks(length):
    qb = min(128, length)
    return qb, min(qb + 2 * DIL_RADIUS, length)


def _dil_scores(q_ref, k_ref, v_ref, i, qb, win, length):
    start = pl.multiple_of(jnp.clip(i * qb - DIL_RADIUS, 0, length - win), DIL_RADIUS)
    kw = k_ref[0, pl.ds(start, win), :]
    vw = v_ref[0, pl.ds(start, win), :]
    qv = q_ref[0].astype(F32)
    lane = lax.broadcasted_iota(jnp.int32, (qb, 256), 1)
    qs = jnp.concatenate([jnp.where((lane >> 6) == h, qv, 0.0) for h in range(4)], axis=0).astype(BF16)
    s = lax.dot_general(qs, kw, NT_DIMS, preferred_element_type=F32) * QK_SCALE
    qi = i * qb + (lax.broadcasted_iota(jnp.int32, (4 * qb, win), 0) & (qb - 1))
    kj = start + lax.broadcasted_iota(jnp.int32, (4 * qb, win), 1)
    s = jnp.where(jnp.abs(qi - kj) <= DIL_RADIUS, s, NEG_INF)
    return s, qs, kw, vw, start, lane


def _pick_heads(stacked, lane, qb):
    out = jnp.zeros((qb, 256), stacked.dtype)
    for h in range(4):
        out = jnp.where((lane >> 6) == h, stacked[h * qb:(h + 1) * qb], out)
    return out


def _stack_head_cols(t, qb):
    return jnp.concatenate([t[:, 64 * h:64 * h + 1] for h in range(4)], axis=0)


def _dil_fwd(q, k, v, *, name):
    dil, length, _ = q.shape
    qb, win = _dil_blocks(length)

    def body(q_ref, k_ref, v_ref, o_ref, lse_ref):
        i = pl.program_id(1)
        s, _, _, vw, _, lane = _dil_scores(q_ref, k_ref, v_ref, i, qb, win, length)
        m = jnp.max(s, axis=-1, keepdims=True)
        lse = m + jnp.log(jnp.sum(jnp.exp(s - m), axis=-1, keepdims=True))
        p = jnp.exp(s - lse)
        o4 = jnp.dot(p.astype(BF16), vw, preferred_element_type=F32)
        o_ref[0] = _pick_heads(o4, lane, qb)
        lse_ref[0] = _pick_heads(jnp.broadcast_to(lse, (4 * qb, 256)), lane, qb)

    seq = pl.BlockSpec((1, length, 256), lambda j, i: (j, 0, 0))
    blk = pl.BlockSpec((1, qb, 256), lambda j, i: (j, i, 0))
    return pl.pallas_call(
        body, name=name, grid=(dil, length // qb),
        in_specs=[blk, seq, seq], out_specs=[blk, blk],
        out_shape=[_sds((dil, length, 256), F32)] * 2,
        compiler_params=_params("parallel", "parallel"),
    )(q, k, v)


def _dil_bwd(q, k, v, do, lse, cc, *, name):
    dil, length, _ = q.shape
    qb, win = _dil_blocks(length)

    def body(q_ref, k_ref, v_ref, do_ref, lse_ref, cc_ref, dq_ref, dk_ref, dv_ref):
        i = pl.program_id(1)

        @pl.when(i == 0)
        def _():
            dk_ref[...] = jnp.zeros_like(dk_ref)
            dv_ref[...] = jnp.zeros_like(dv_ref)

        s, qs, kw, vw, start, lane = _dil_scores(q_ref, k_ref, v_ref, i, qb, win, length)
        p = jnp.exp(s - _stack_head_cols(lse_ref[0], qb))
        dov = do_ref[0].astype(F32)
        dos = jnp.concatenate([jnp.where((lane >> 6) == h, dov, 0.0) for h in range(4)], axis=0).astype(BF16)
        dp = lax.dot_general(dos, vw, NT_DIMS, preferred_element_type=F32)
        ds = p * (dp + _stack_head_cols(cc_ref[0], qb))
        dsb = ds.astype(BF16)
        dq4 = jnp.dot(dsb, kw, preferred_element_type=F32)
        dq_ref[0] = _pick_heads(dq4, lane, qb) * QK_SCALE
        dk_ref[0, pl.ds(start, win), :] += lax.dot_general(dsb, qs, TN_DIMS, preferred_element_type=F32) * QK_SCALE
        dv_ref[0, pl.ds(start, win), :] += lax.dot_general(p.astype(BF16), dos, TN_DIMS, preferred_element_type=F32)

    seq = pl.BlockSpec((1, length, 256), lambda j, i: (j, 0, 0))
    blk = pl.BlockSpec((1, qb, 256), lambda j, i: (j, i, 0))
    return pl.pallas_call(
        body, name=name, grid=(dil, length // qb),
        in_specs=[blk, seq, seq, blk, blk, blk], out_specs=[blk, seq, seq],
        out_shape=[_sds((dil, length, 256), F32)] * 3,
        compiler_params=_params("parallel", "arbitrary"),
    )(q, k, v, do, lse, cc)


def _merge_weights(lses):
    m = jnp.maximum(jnp.maximum(lses[0], lses[1]), lses[2])
    es = [jnp.exp(t - m) for t in lses]
    inv = 1.0 / (es[0] + es[1] + es[2])
    return [e * inv for e in es]


def _dil_merge(outs, lses, *, tm, name):
    n = outs[0].shape[1]

    def body(*refs):
        o_in, l_in = refs[0:3], refs[3:6]
        y_ref, yb_ref, scr = refs[6:9]
        lv = [_load_token_order(l_in[g], scr, d, tm) for g, d in enumerate(DIL_DILATIONS)]
        ws = _merge_weights(lv)
        y = jnp.zeros((tm, 256), F32)
        for g, d in enumerate(DIL_DILATIONS):
            y = y + ws[g] * _load_token_order(o_in[g], scr, d, tm)
        y_ref[...] = y
        yb_ref[...] = y.astype(BF16)

    specs = [_dil_spec(d, tm) for d in DIL_DILATIONS]
    return pl.pallas_call(
        body, name=name, grid=(n // tm,), in_specs=specs + specs,
        out_specs=[_rows(tm, 256)] * 2, out_shape=[_sds((n, 256), F32), _sds((n, 256), BF16)],
        scratch_shapes=[_dil_scratch(tm)],
        compiler_params=_params("parallel"),
    )(*outs, *lses)


def _dil_merge_bwd(dy, y, lses, *, tm, name):
    n = dy.shape[0]

    def body(*refs):
        dy_ref, y_ref = refs[0:2]
        l_in = refs[2:5]
        do_out, cc_out = refs[5:8], refs[8:11]
        scr = refs[11]
        lv = [_load_token_order(l_in[g], scr, d, tm) for g, d in enumerate(DIL_DILATIONS)]
        ws = _merge_weights(lv)
        dyv = dy_ref[...]
        rr = lax.broadcasted_iota(jnp.int32, (256, 256), 0) >> 6
        cc = lax.broadcasted_iota(jnp.int32, (256, 256), 1) >> 6
        ones = jnp.where(rr == cc, 1.0, 0.0).astype(F32)
        tsum = jnp.dot(dyv * y_ref[...], ones, preferred_element_type=F32,
                       precision=lax.Precision.HIGHEST)
        for g, d in enumerate(DIL_DILATIONS):
            _store_dil_order(ws[g] * dyv, do_out[g], scr, d, tm)
            _store_dil_order(-ws[g] * tsum, cc_out[g], scr, d, tm)

    specs = [_dil_spec(d, tm) for d in DIL_DILATIONS]
    res = pl.pallas_call(
        body, name=name, grid=(n // tm,),
        in_specs=[_rows(tm, 256)] * 2 + specs,
        out_specs=specs + specs,
        out_shape=[_sds((d, n // d, 256), BF16) for d in DIL_DILATIONS]
                  + [_sds((d, n // d, 256), F32) for d in DIL_DILATIONS],
        scratch_shapes=[_dil_scratch(tm)],
        compiler_params=_params("parallel"),
    )(dy, y, *lses)
    return res[0:3], res[3:6]


_WEIGHTS = (("w_in", 1, 736), ("w_branch_na", 1, 128), ("w_branch_dil", 1, 128), ("w_out", 0, 128),
            ("w_up", 1, 512), ("w_down", 0, 512), ("w_ple_gate", 0, 128), ("w_ple_proj", 1, 128))
_W_IN, _W_BNA, _W_BD, _W_OUT, _W_UP, _W_DOWN, _W_PG, _W_PP = range(8)
_GATHER_EARLY = (_W_BNA, _W_BD, _W_OUT, _W_PG, _W_PP)
_GATHER_LATE = (_W_UP, _W_DOWN)


def _to_full(widx, gathered):
    if _WEIGHTS[widx][1] == 0:
        return gathered.reshape(-1, gathered.shape[2])
    return jnp.transpose(gathered, (1, 0, 2)).reshape(gathered.shape[1], -1)


def _to_chunks(widx, mat):
    _, axis, width = _WEIGHTS[widx]
    if axis == 0:
        return mat.reshape(N_DEV, width, mat.shape[1])
    return jnp.transpose(mat.reshape(mat.shape[0], N_DEV, width), (1, 0, 2))


def _local_step(x, p_bf16, positions, target, g_mix, g_mlp, g_ple, g_final, rpb2, get_w_in, get_rest, send_grads):
    tm = 256
    half = HEAD_DIM // 2
    inv_freq = 10000.0 ** (-jnp.arange(half, dtype=F32) / half)
    ang = positions.astype(F32)[:, None] * inv_freq
    cos, sin = jnp.cos(ang), jnp.sin(ang)
    cos_t = jnp.tile(jnp.concatenate([cos, cos], axis=-1), (1, 4))
    sin_t = jnp.tile(jnp.concatenate([-sin, sin], axis=-1), (1, 4))
    rb = _rpb_table(rpb2)

    a = _rms_fwd(x, g_mix, tm=tm, name="rms_mix")
    w_in = get_w_in(a)
    proj = _matmul(a, w_in, out_dtype=F32, tm=512, tn=2944, tk=1024, name="mm_in")
    na_qkv, dq_g, dk_g, dv_g, sn, sd = _split_proj(proj, cos_t, sin_t, tm=tm, name="split_proj")
    y_na = _na_fwd(*na_qkv, rb, name="na_fwd")
    d_out, d_lse = [], []
    for g in range(3):
        o, lse = _dil_fwd(dq_g[g], dk_g[g], dv_g[g], name=f"dil_fwd{g}")
        d_out.append(o)
        d_lse.append(lse)
    y_dil, y_dil_b = _dil_merge(d_out, d_lse, tm=tm, name="dil_merge")
    w_bna, w_bd, w_out, w_up, w_down, w_pg, w_pp = get_rest(y_dil_b)
    bn = _matmul(y_na, w_bna, out_dtype=F32, tm=512, tn=1024, tk=512, name="mm_bna")
    bd = _matmul(y_dil_b, w_bd, out_dtype=F32, tm=512, tn=1024, tk=256, name="mm_bd")
    mixed = _gate_mix(sn, bn, sd, bd, tm=tm, name="gate_mix")
    mo = _matmul(mixed, w_out, out_dtype=F32, tm=512, tn=1024, tk=1024, name="mm_out")
    h1, c = _residual_rms(x, mo, g_mlp, tm=tm, name="res_rms_mlp")
    u = _matmul(c, w_up, out_dtype=F32, tm=512, tn=2048, tk=1024, name="mm_up")
    f = _relu_sq(u, tm=128, name="relu_sq")
    dn = _matmul(f, w_down, out_dtype=F32, tm=512, tn=1024, tk=2048, name="mm_down")
    h2, e = _residual_rms(h1, dn, g_ple, tm=tm, name="res_rms_ple")
    gt = _matmul(e, w_pg, out_dtype=F32, tm=512, tn=1024, tk=1024, name="mm_pg")
    pp = _matmul(p_bf16, w_pp, out_dtype=F32, tm=512, tn=1024, tk=256, name="mm_pp")

    dh3, dpp, dgt, dg_final, loss = _tail(h2, gt, pp, target, g_final, tm=tm, name="tail")
    gw_pp = _matmul(p_bf16, dpp, ta=True, out_dtype=BF16, tm=256, tn=1024, tk=512, name="mm_gw_pp")
    gw_pg = _matmul(e, dgt, ta=True, out_dtype=BF16, tm=512, tn=1024, tk=512, name="mm_gw_pg")
    de = _matmul(dgt, w_pg, tb=True, out_dtype=F32, tm=512, tn=1024, tk=1024, name="mm_de")
    dh2, dh2_b, dg_ple = _rms_bwd(de, h2, g_ple, dh3, tm=tm, name="rms_bwd_ple")
    df = _matmul(dh2_b, w_down, tb=True, out_dtype=F32, tm=512, tn=2048, tk=1024, name="mm_df")
    gw_down = _matmul(f, dh2_b, ta=True, out_dtype=BF16, tm=1024, tn=1024, tk=512, name="mm_gw_down")
    send_grads((_W_PP, _W_PG, _W_DOWN), (gw_pp, gw_pg, gw_down))
    du = _relu_sq_bwd(df, u, tm=128, name="relu_sq_bwd")
    gw_up = _matmul(c, du, ta=True, out_dtype=BF16, tm=512, tn=2048, tk=512, name="mm_gw_up")
    send_grads((_W_UP,), (gw_up,))
    dc = _matmul(du, w_up, tb=True, out_dtype=F32, tm=512, tn=1024, tk=2048, name="mm_dc")
    dh1, dh1_b, dg_mlp = _rms_bwd(dc, h1, g_mlp, dh2, tm=tm, name="rms_bwd_mlp")
    dmixed = _matmul(dh1_b, w_out, tb=True, out_dtype=F32, tm=512, tn=1024, tk=1024, name="mm_dmixed")
    gw_out = _matmul(mixed, dh1_b, ta=True, out_dtype=BF16, tm=512, tn=1024, tk=512, name="mm_gw_out")
    dbn, dbd, dgn, dgd = _gate_bwd(dmixed, sn, bn, sd, bd, tm=tm, name="gate_bwd")
    gw_bna = _matmul(y_na, dbn, ta=True, out_dtype=BF16, tm=512, tn=1024, tk=512, name="mm_gw_bna")
    dy_na = _matmul(dbn, w_bna, tb=True, out_dtype=BF16, tm=512, tn=512, tk=1024, name="mm_dy_na")
    gw_bd = _matmul(y_dil_b, dbd, ta=True, out_dtype=BF16, tm=256, tn=1024, tk=512, name="mm_gw_bd")
    send_grads((_W_OUT, _W_BNA, _W_BD), (gw_out, gw_bna, gw_bd))
    dy_dil = _matmul(dbd, w_bd, tb=True, out_dtype=F32, tm=512, tn=256, tk=1024, name="mm_dy_dil")
    dna = _na_bwd(*na_qkv, dy_na, rb, name="na_bwd")
    drpb = _rpb_grad(dna[3].reshape(8, -1), name="rpb_grad")
    do_g, cc_g = _dil_merge_bwd(dy_dil, y_dil, d_lse, tm=tm, name="dil_merge_bwd")
    ddq, ddk, ddv = [], [], []
    for g in range(3):
        r = _dil_bwd(dq_g[g], dk_g[g], dv_g[g], do_g[g], d_lse[g], cc_g[g], name=f"dil_bwd{g}")
        ddq.append(r[0])
        ddk.append(r[1])
        ddv.append(r[2])
    dproj = _assemble_dproj(dna[0:3], ddq, ddk, ddv, dgn, dgd, cos_t, sin_t, tm=tm, name="assemble_dproj")
    gw_in = _matmul(a, dproj, ta=True, out_dtype=BF16, tm=512, tn=2944, tk=512, name="mm_gw_in")
    send_grads((_W_IN,), (gw_in,))
    da = _matmul(dproj, w_in, tb=True, out_dtype=F32, tm=512, tn=1024, tk=2944, name="mm_da")
    dx, dg_mix = _rms_bwd(da, x, g_mix, dh1, tm=tm, name="rms_bwd_mix", want_bf16=False)
    return loss, dx, (dg_mix, dg_mlp, dg_ple, dg_final), drpb


def _cast_bf16(t, *, name):
    def body(t_ref, o_ref):
        o_ref[...] = t_ref[...].astype(BF16)

    rows, cols = t.shape
    tr = min(256, rows)
    blk = pl.BlockSpec((tr, cols), lambda i: (i, 0))
    return pl.pallas_call(body, name=name, grid=(rows // tr,), in_specs=[blk], out_specs=blk,
                          out_shape=_sds(t.shape, BF16), compiler_params=_params("parallel"))(t)


def _adamw(w, g, m, v):
    m = ADAM_B1 * m + (1.0 - ADAM_B1) * g
    v = ADAM_B2 * v + (1.0 - ADAM_B2) * (g * g)
    m_hat = m / (1.0 - ADAM_B1 ** ADAM_STEP)
    v_hat = v / (1.0 - ADAM_B2 ** ADAM_STEP)
    delta = -ADAM_LR * (m_hat / (jnp.sqrt(v_hat) + ADAM_EPS) + ADAM_WD * w)
    return delta, m, v


def _sum_adamw(parts, w, m, v, *, tr, name, own=None):
    rows, cols = w.shape

    def body(*refs):
        p_ref, w_ref, m_ref, v_ref = refs[:4]
        g_ref, d_ref, nm_ref, nv_ref = refs[-4:]
        g = (p_ref[0] if own is None else refs[4][...]).astype(F32)
        for s in range(1, N_DEV):
            g = g + p_ref[s].astype(F32)
        g_ref[...] = g
        d_ref[...], nm_ref[...], nv_ref[...] = _adamw(w_ref[...], g, m_ref[...], v_ref[...])

    blk = pl.BlockSpec((tr, cols), lambda i: (i, 0))
    extra = [] if own is None else [own]
    return pl.pallas_call(
        body, name=name, grid=(rows // tr,),
        in_specs=[pl.BlockSpec((N_DEV, tr, cols), lambda i: (0, i, 0)), blk, blk, blk] + [blk] * len(extra),
        out_specs=[blk] * 4, out_shape=[_sds((rows, cols), F32)] * 4,
        compiler_params=_params("parallel"),
    )(parts, w, m, v, *extra)


_RPB_SIZE = 8 * 15 * 31


def _pack_small(g_mix, g_mlp, g_ple, g_final, rpb, loss_row):
    flat = jnp.concatenate([g_mix.reshape(-1), g_mlp.reshape(-1), g_ple.reshape(-1), g_final.reshape(-1),
                            rpb.reshape(-1), jnp.zeros((3840 - _RPB_SIZE,), F32), loss_row.reshape(-1),
                            jnp.zeros((128,), F32)])
    return flat.reshape(64, 128)


def _unpack_small(t):
    flat = t.reshape(-1)
    return (flat[0:1024].reshape(1, 1024), flat[4096:4096 + _RPB_SIZE].reshape(1, 8, 15, 31),
            flat[1024:2048].reshape(1, 1024), flat[2048:3072].reshape(1, 1024), flat[3072:4096])


def kernel(x, p, positions, g_mix, w_in, rpb, w_branch_na, w_branch_dil, w_out, g_mlp, w_up, w_down, g_ple, w_ple_gate, w_ple_proj, g_final, loss_target, m_g_mix, m_w_in, m_rpb, m_w_branch_na, m_w_branch_dil, m_w_out, m_g_mlp, m_w_up, m_w_down, m_g_ple, m_w_ple_gate, m_w_ple_proj, m_g_final, v_g_mix, v_w_in, v_rpb, v_w_branch_na, v_w_branch_dil, v_w_out, v_g_mlp, v_w_up, v_w_down, v_g_ple, v_w_ple_gate, v_w_ple_proj, v_g_final):
    sharded = dict(w_in=(w_in, m_w_in, v_w_in), w_branch_na=(w_branch_na, m_w_branch_na, v_w_branch_na),
                   w_branch_dil=(w_branch_dil, m_w_branch_dil, v_w_branch_dil), w_out=(w_out, m_w_out, v_w_out),
                   w_up=(w_up, m_w_up, v_w_up), w_down=(w_down, m_w_down, v_w_down),
                   w_ple_gate=(w_ple_gate, m_w_ple_gate, v_w_ple_gate),
                   w_ple_proj=(w_ple_proj, m_w_ple_proj, v_w_ple_proj))
    shards = {k: tuple(t[0] for t in val) for k, val in sharded.items()}

    me = _my_index()

    w_in_b = _cast_bf16(shards["w_in"][0], name="cast_w_in")
    rest_b = [shards[name][0].astype(BF16) for name, _, _ in _WEIGHTS[1:]]
    gather_in, token_in = _start_copies(_gather_copies, [w_in_b], [_sds((N_DEV,) + w_in_b.shape, BF16)],
                                        name="start_gather_w_in")
    gather_rest, token_rest = _start_copies(_gather_copies, rest_b,
                                            [_sds((N_DEV,) + t.shape, BF16) for t in rest_b],
                                            name="start_gather_rest")

    def whole(widx, landed, mine):
        return _to_full(widx, lax.dynamic_update_index_in_dim(landed, mine, me, 0))

    def get_w_in(after):
        landed, = _wait_copies(_gather_copies, gather_in, after, name="wait_gather_w_in")
        return whole(_W_IN, landed, w_in_b)

    def get_rest(after):
        landed = _wait_copies(_gather_copies, gather_rest, after, name="wait_gather_rest")
        return [whole(i + 1, t, mine) for i, (t, mine) in enumerate(zip(landed, rest_b))]

    sent = []

    def send_grads(indices, grads):
        chunked = [_to_chunks(i, g) for i, g in zip(indices, grads)]
        handle, _ = _start_copies(_exchange_copies, chunked, [_sds(t.shape, BF16) for t in chunked],
                                  name="start_exchange_" + "_".join(_WEIGHTS[i][0] for i in indices))
        sent.append((indices, chunked, handle))

    g_mix_0 = g_mix + (token_in[0:1, 0:1] + token_rest[0:1, 0:1])
    loss, dx, dgs, drpb = _local_step(
        x[0], p[0, 0].astype(BF16), positions[0], loss_target[0],
        g_mix_0, g_mlp, g_ple, g_final.reshape(1, -1), rpb[0], get_w_in, get_rest, send_grads)

    drpb3 = drpb.reshape(8, 16, 32)[:, :15, :31]
    small = _pack_small(dgs[0], dgs[1], dgs[2], dgs[3], drpb3, loss)
    small_all, = _comm_only(_plan_share(small), name="share_small")

    out = {}
    for indices, chunked, handle in sent:
        landed = _wait_copies(_exchange_copies, handle, dx,
                              name="wait_exchange_" + "_".join(_WEIGHTS[i][0] for i in indices))
        for i, part, mine in zip(indices, landed, chunked):
            name = _WEIGHTS[i][0]
            w, m, v = shards[name]
            own = lax.dynamic_index_in_dim(mine, me, 0, keepdims=False)
            res = _sum_adamw(part, w, m, v, tr=min(128, w.shape[0]), name="adamw_" + name, own=own)
            out[name] = [t[None] for t in res]
    small_w = _pack_small(g_mix, g_mlp, g_ple, g_final, rpb, jnp.zeros((128,), F32))
    small_m = _pack_small(m_g_mix, m_g_mlp, m_g_ple, m_g_final, m_rpb, jnp.zeros((128,), F32))
    small_v = _pack_small(v_g_mix, v_g_mlp, v_g_ple, v_g_final, v_rpb, jnp.zeros((128,), F32))
    res = _sum_adamw(small_all, small_w, small_m, small_v, tr=64, name="adamw_small")
    unpacked = [_unpack_small(t) for t in res]
    for i, name in enumerate(("g_mix", "rpb", "g_mlp", "g_ple", "g_final")):
        out[name] = [u[i] for u in unpacked]
    loss_total = res[0][62, 0]

    order = ("g_mix", "w_in", "rpb", "w_branch_na", "w_branch_dil", "w_out", "g_mlp", "w_up", "w_down",
             "g_ple", "w_ple_gate", "w_ple_proj", "g_final")
    grads = [out[k][0] for k in order]
    deltas = [out[k][1] for k in order]
    new_m = [out[k][2] for k in order]
    new_v = [out[k][3] for k in order]
    return (loss_total, dx[None], *grads, *deltas, *new_m, *new_v)
```

```python
import functools

import numpy as np
import jax
import jax.numpy as jnp
from jax import lax
from jax.experimental import pallas as pl
from jax.experimental.pallas import tpu as pltpu

F32 = jnp.float32
BF16 = jnp.bfloat16

D_MODEL = 1024
HEAD_DIM = 64
GRID_W = 64
NA_WIDTH = 512
DIL_WIDTH = 768
DIL_OUT = 256
D_FF = 4096
IN_WIDTH = 5888
DIL_DILATIONS = (1, 4, 16)
DIL_RADIUS = 64
NA_WIN_ROWS = 8
RMS_EPS = 1e-6
NEG_INF = -1e30
QK_SCALE = HEAD_DIM ** -0.5

ADAM_LR = 0.001
ADAM_B1 = 0.9
ADAM_B2 = 0.999
ADAM_EPS = 1e-08
ADAM_WD = 0.01
ADAM_STEP = 10

N_DEV = 8
VMEM_LIMIT = 56 * 1024 * 1024
MESH = pl.DeviceIdType.MESH

NT_DIMS = (((1,), (1,)), ((), ()))
TN_DIMS = (((0,), (0,)), ((), ()))


def _sds(shape, dtype):
    return jax.ShapeDtypeStruct(shape, dtype)


def _params(*sem):
    return pltpu.CompilerParams(dimension_semantics=sem, vmem_limit_bytes=VMEM_LIMIT)


def _rows(tm, width, col=0):
    return pl.BlockSpec((tm, width), lambda i, c=col: (i, c))


def _const(shape):
    zeros = (0,) * len(shape)
    return pl.BlockSpec(shape, lambda i: zeros)


def _my_index():
    return 4 * lax.axis_index("x") + 2 * lax.axis_index("y") + lax.axis_index("c")


def _peer(k):
    x, y, c = lax.axis_index("x"), lax.axis_index("y"), lax.axis_index("c")
    px = 1 - x if k & 4 else x
    py = 1 - y if k & 2 else y
    pc = 1 - c if k & 1 else c
    return (px, py, pc), 4 * px + 2 * py + pc


class _CommPlan:
    def __init__(self, ins, out_shapes, n_remote, n_local, start, wait):
        self.ins, self.out_shapes = list(ins), list(out_shapes)
        self.n_remote, self.n_local = n_remote, n_local
        self.start, self.wait = start, wait


def _plan_all_gather(shards):
    n_w = len(shards)

    def remote(ins, outs, send, recv, w, k, slot):
        dev, idx = _peer(k)
        return pltpu.make_async_remote_copy(
            src_ref=ins[w], dst_ref=outs[w].at[idx if slot is None else slot],
            send_sem=send.at[w * 7 + k - 1], recv_sem=recv.at[w * 7 + k - 1],
            device_id=dev, device_id_type=MESH)

    def start(ins, outs, send, recv, local):
        me = _my_index()
        for w in range(n_w):
            pltpu.make_async_copy(ins[w], outs[w].at[me], local.at[w]).start()
            for k in range(1, N_DEV):
                remote(ins, outs, send, recv, w, k, me).start()

    def wait(ins, outs, send, recv, local):
        me = _my_index()
        for w in range(n_w):
            for k in range(1, N_DEV):
                remote(ins, outs, send, recv, w, k, None).wait()
            pltpu.make_async_copy(ins[w], outs[w].at[me], local.at[w]).wait()

    return _CommPlan(shards, [_sds((N_DEV,) + s.shape, s.dtype) for s in shards], n_w * 7, n_w, start, wait)


def _plan_exchange(chunked):
    n_w = len(chunked)

    def remote(ins, outs, send, recv, w, k):
        dev, idx = _peer(k)
        return pltpu.make_async_remote_copy(
            src_ref=ins[w].at[idx], dst_ref=outs[w].at[k],
            send_sem=send.at[w * 7 + k - 1], recv_sem=recv.at[w * 7 + k - 1],
            device_id=dev, device_id_type=MESH)

    def start(ins, outs, send, recv, local):
        me = _my_index()
        for w in range(n_w):
            pltpu.make_async_copy(ins[w].at[me], outs[w].at[0], local.at[w]).start()
        for k in range(1, N_DEV):
            for w in range(n_w):
                remote(ins, outs, send, recv, w, k).start()

    def wait(ins, outs, send, recv, local):
        me = _my_index()
        for k in range(1, N_DEV):
            for w in range(n_w):
                remote(ins, outs, send, recv, w, k).wait()
        for w in range(n_w):
            pltpu.make_async_copy(ins[w].at[me], outs[w].at[0], local.at[w]).wait()

    return _CommPlan(chunked, [_sds(t.shape, t.dtype) for t in chunked], n_w * 7, n_w, start, wait)


def _plan_share(block):
    def remote(ins, outs, send, recv, k, slot):
        dev, idx = _peer(k)
        return pltpu.make_async_remote_copy(
            src_ref=ins[0], dst_ref=outs[0].at[idx if slot is None else slot],
            send_sem=send.at[k - 1], recv_sem=recv.at[k - 1], device_id=dev, device_id_type=MESH)

    def start(ins, outs, send, recv, local):
        me = _my_index()
        pltpu.make_async_copy(ins[0], outs[0].at[me], local.at[0]).start()
        for k in range(1, N_DEV):
            remote(ins, outs, send, recv, k, me).start()

    def wait(ins, outs, send, recv, local):
        for k in range(1, N_DEV):
            remote(ins, outs, send, recv, k, None).wait()
        pltpu.make_async_copy(ins[0], outs[0].at[_my_index()], local.at[0]).wait()

    return _CommPlan([block], [_sds((N_DEV,) + block.shape, block.dtype)], 7, 1, start, wait)


def _hosted_call(body, plan, *, name, grid, in_specs, out_specs, out_shape, scratch_shapes, args, after=None):
    n_in, n_out, n_scr = len(in_specs), len(out_specs), len(scratch_shapes)
    sem = ("arbitrary",) * len(grid)
    if plan is None:
        extra = [] if after is None else [after]
        n_x = n_in + len(extra)

        def plain(*refs):
            body(refs[:n_in], refs[n_x:n_x + n_out], refs[n_x + n_out:])

        res = pl.pallas_call(plain, name=name, grid=grid,
                             in_specs=list(in_specs) + [pl.BlockSpec(memory_space=pl.ANY)] * len(extra),
                             out_specs=out_specs, out_shape=out_shape, scratch_shapes=scratch_shapes,
                             compiler_params=_params(*sem))(*args, *extra)
        return list(res), None

    n_ci, n_co = len(plan.ins), len(plan.out_shapes)
    hbm = pl.BlockSpec(memory_space=pl.ANY)

    def hosted(*refs):
        ins, refs = refs[:n_in], refs[n_in:]
        c_ins, refs = refs[:n_ci], refs[n_ci:]
        outs, refs = refs[:n_out], refs[n_out:]
        c_outs, refs = refs[:n_co], refs[n_co:]
        scr, sems = refs[:n_scr], refs[n_scr:]
        first = functools.reduce(jnp.logical_and, [pl.program_id(d) == 0 for d in range(len(grid))])
        last = functools.reduce(jnp.logical_and, [pl.program_id(d) == grid[d] - 1 for d in range(len(grid))])

        @pl.when(first)
        def _():
            plan.start(c_ins, c_outs, *sems)

        body(ins, outs, scr)

        @pl.when(last)
        def _():
            plan.wait(c_ins, c_outs, *sems)

    res = pl.pallas_call(
        hosted, name=name, grid=grid,
        in_specs=list(in_specs) + [hbm] * n_ci, out_specs=list(out_specs) + [hbm] * n_co,
        out_shape=list(out_shape) + plan.out_shapes,
        scratch_shapes=list(scratch_shapes) + [pltpu.SemaphoreType.DMA((plan.n_remote,)),
                                               pltpu.SemaphoreType.DMA((plan.n_remote,)),
                                               pltpu.SemaphoreType.DMA((plan.n_local,))],
        compiler_params=_params(*sem),
    )(*args, *plan.ins)
    return list(res[:n_out]), list(res[n_out:])


def _comm_only(plan, *, name):
    hbm = pl.BlockSpec(memory_space=pl.ANY)
    n_ci, n_co = len(plan.ins), len(plan.out_shapes)

    def body(*refs):
        c_ins, c_outs, sems = refs[:n_ci], refs[n_ci:n_ci + n_co], refs[n_ci + n_co:]
        plan.start(c_ins, c_outs, *sems)
        plan.wait(c_ins, c_outs, *sems)

    return pl.pallas_call(
        body, name=name, in_specs=[hbm] * n_ci, out_specs=[hbm] * n_co, out_shape=plan.out_shapes,
        scratch_shapes=[pltpu.SemaphoreType.DMA((plan.n_remote,)), pltpu.SemaphoreType.DMA((plan.n_remote,)),
                        pltpu.SemaphoreType.DMA((plan.n_local,))],
        compiler_params=pltpu.CompilerParams(vmem_limit_bytes=VMEM_LIMIT),
    )(*plan.ins)


_HBM_SPEC = pl.BlockSpec(memory_space=pltpu.HBM)
_SEM_SPEC = pl.BlockSpec(memory_space=pltpu.SEMAPHORE)
_SIDE_EFFECT = pltpu.SideEffectType.DATAFLOW_SIDE_EFFECTING


def _gather_copies(srcs, lands, send, recv, sending):
    me = _my_index()
    out = []
    for w in range(len(srcs)):
        for k in range(1, N_DEV):
            dev, idx = _peer(k)
            out.append(pltpu.make_async_remote_copy(
                src_ref=srcs[w], dst_ref=lands[w].at[me if sending else idx],
                send_sem=send.at[w * 7 + k - 1], recv_sem=recv.at[w * 7 + k - 1],
                device_id=dev, device_id_type=MESH))
    return out


def _exchange_copies(srcs, lands, send, recv, sending):
    out = []
    for w in range(len(srcs)):
        for k in range(1, N_DEV):
            dev, idx = _peer(k)
            out.append(pltpu.make_async_remote_copy(
                src_ref=srcs[w].at[idx], dst_ref=lands[w].at[k],
                send_sem=send.at[w * 7 + k - 1], recv_sem=recv.at[w * 7 + k - 1],
                device_id=dev, device_id_type=MESH))
    return out


def _start_copies(make, srcs, land_shapes, *, name, after=None):
    n = len(srcs)
    extra = [] if after is None else [after]

    def body(*refs):
        src_refs, land_refs = refs[:n], refs[n:2 * n]
        send, recv = refs[2 * n + len(extra)], refs[2 * n + len(extra) + 1]
        token = refs[-1]
        for cp in make(src_refs, land_refs, send, recv, True):
            cp.start()
        token[...] = jnp.zeros_like(token)

    lands = [pltpu.with_memory_space_constraint(lax.empty(s.shape, s.dtype), pltpu.HBM) for s in land_shapes]
    res = pl.pallas_call(
        body, name=name,
        out_shape=(pltpu.SemaphoreType.DMA((7 * n,)), pltpu.SemaphoreType.DMA((7 * n,)),
                   *[pltpu.HBM(s.shape, s.dtype) for s in srcs],
                   *[pltpu.HBM(s.shape, s.dtype) for s in land_shapes],
                   _sds((8, 128), F32)),
        in_specs=[_HBM_SPEC] * (2 * n) + [pl.BlockSpec(memory_space=pl.ANY)] * len(extra),
        out_specs=(_SEM_SPEC, _SEM_SPEC, *([_HBM_SPEC] * (2 * n)), pl.BlockSpec(memory_space=pltpu.VMEM)),
        input_output_aliases={i: 2 + i for i in range(2 * n)},
        compiler_params=pltpu.CompilerParams(has_side_effects=_SIDE_EFFECT),
    )(*[pltpu.with_memory_space_constraint(s, pltpu.HBM) for s in srcs], *lands, *extra)
    return (n, res[0], res[1], res[2:2 + n], res[2 + n:2 + 2 * n]), res[-1]


def _wait_copies(make, handle, after, *, name):
    n, send_sems, recv_sems, srcs, lands = handle

    def body(*refs):
        src_refs, land_refs = refs[:n], refs[n:2 * n]
        send, recv = refs[2 * n], refs[2 * n + 1]
        for cp in make(src_refs, land_refs, send, recv, False):
            cp.wait_send()
            cp.wait_recv()

    res = pl.pallas_call(
        body, name=name,
        out_shape=tuple(pltpu.HBM(s.shape, s.dtype) for s in (*srcs, *lands)),
        in_specs=[_HBM_SPEC] * (2 * n) + [_SEM_SPEC, _SEM_SPEC, pl.BlockSpec(memory_space=pl.ANY)],
        out_specs=tuple([_HBM_SPEC] * (2 * n)),
        input_output_aliases={i: i for i in range(2 * n)},
        compiler_params=pltpu.CompilerParams(has_side_effects=_SIDE_EFFECT),
    )(*srcs, *lands, send_sems, recv_sems, after)
    return list(res[n:])


def _matmul(a, b, *, ta=False, tb=False, out_dtype, tm, tn, tk, name, comm=None, after=None):
    m, k = (a.shape[1], a.shape[0]) if ta else a.shape
    n = b.shape[0] if tb else b.shape[1]
    tm, tn, tk = min(tm, m), min(tn, n), min(tk, k)
    nk = k // tk
    dims = (((0 if ta else 1,), (1 if tb else 0,)), ((), ()))

    def body(ins, outs, acc):
        a_ref, b_ref = ins
        o_ref, = outs
        part = lax.dot_general(a_ref[...], b_ref[...], dims, preferred_element_type=F32)
        if nk == 1:
            o_ref[...] = part.astype(o_ref.dtype)
            return
        acc_ref, = acc
        kk = pl.program_id(2)

        @pl.when(kk == 0)
        def _():
            acc_ref[...] = part

        @pl.when(kk > 0)
        def _():
            acc_ref[...] += part

        @pl.when(kk == nk - 1)
        def _():
            o_ref[...] = acc_ref[...].astype(o_ref.dtype)

    a_spec = (pl.BlockSpec((tk, tm), lambda j, i, kk: (kk, i)) if ta
              else pl.BlockSpec((tm, tk), lambda j, i, kk: (i, kk)))
    b_spec = (pl.BlockSpec((tn, tk), lambda j, i, kk: (j, kk)) if tb
              else pl.BlockSpec((tk, tn), lambda j, i, kk: (kk, j)))
    res, comm_res = _hosted_call(
        body, comm, name=name, grid=(n // tn, m // tm, nk),
        in_specs=[a_spec, b_spec],
        out_specs=[pl.BlockSpec((tm, tn), lambda j, i, kk: (i, j))],
        out_shape=[_sds((m, n), out_dtype)],
        scratch_shapes=[] if nk == 1 else [pltpu.VMEM((tm, tn), F32)],
        args=(a, b), after=after)
    return res[0] if comm is None else (res[0], comm_res)


def _rstd(h):
    return lax.rsqrt(jnp.mean(h * h, axis=-1, keepdims=True) + RMS_EPS)


def _sigmoid(z):
    return 1.0 / (1.0 + jnp.exp(-z))


def _rms_fwd(x, g, *, tm, name):
    n = x.shape[0]

    def body(x_ref, g_ref, o_ref):
        h = x_ref[...]
        o_ref[...] = (h * _rstd(h) * g_ref[...]).astype(BF16)

    return pl.pallas_call(
        body, name=name, grid=(n // tm,),
        in_specs=[_rows(tm, D_MODEL), _const((1, D_MODEL))],
        out_specs=_rows(tm, D_MODEL), out_shape=_sds((n, D_MODEL), BF16),
        compiler_params=_params("parallel"),
    )(x, g)


def _swap_halves(t):
    width = t.shape[1]
    lane = lax.broadcasted_iota(jnp.int32, t.shape, 1)
    return jnp.where((lane & 63) < 32, pltpu.roll(t, width - 32, 1), pltpu.roll(t, 32, 1))


def _dil_spec(dil, tm):
    return pl.BlockSpec((dil, tm // dil, 256), lambda i: (0, i, 0))


def _dil_scratch(tm):
    return pltpu.VMEM((2, tm, 128), F32)


def _load_token_order(src, scr, dil, tm):
    if dil == 1:
        return src[0]
    for j in range(dil):
        for c in range(2):
            scr[c, pl.ds(j, tm // dil, stride=dil), :] = src[j, :, c * 128:(c + 1) * 128]
    return jnp.concatenate([scr[0], scr[1]], axis=1)


def _store_dil_order(val, dst, scr, dil, tm):
    if dil == 1:
        dst[0] = val.astype(dst.dtype)
        return
    for c in range(2):
        scr[c] = val[:, c * 128:(c + 1) * 128]
    for j in range(dil):
        for c in range(2):
            dst[j, :, c * 128:(c + 1) * 128] = scr[c, pl.ds(j, tm // dil, stride=dil), :].astype(dst.dtype)


def _split_proj(proj, cos_t, sin_t, *, tm, name):
    n = proj.shape[0]
    n_dil = len(DIL_DILATIONS)

    def body(*refs):
        na_in = refs[0:3]
        dil_in = refs[3:3 + 3 * n_dil]
        gate_in = refs[12:20]
        cos_ref, sin_ref = refs[20:22]
        outs = refs[22:]
        na_out = outs[0:3]
        dil_out = outs[3:12]
        sn_ref, sd_ref = outs[12:14]
        scr = outs[14]
        for t in range(3):
            na_out[t][...] = na_in[t][...].astype(BF16)
        cosv, sinv = cos_ref[...], sin_ref[...]
        for t in range(3):
            for gi, dil in enumerate(DIL_DILATIONS):
                val = dil_in[t * n_dil + gi][...]
                if t < 2:
                    val = val * cosv + _swap_halves(val) * sinv
                _store_dil_order(val, dil_out[t * n_dil + gi], scr, dil, tm)
        for c in range(4):
            sn_ref[:, c * 256:(c + 1) * 256] = _sigmoid(gate_in[c][...])
            sd_ref[:, c * 256:(c + 1) * 256] = _sigmoid(gate_in[4 + c][...])

    in_specs = [_rows(tm, NA_WIDTH, c) for c in range(3)]
    in_specs += [_rows(tm, 256, 6 + c) for c in range(9)]
    in_specs += [_rows(tm, 256, 15 + c) for c in range(8)]
    in_specs += [_rows(tm, 256), _rows(tm, 256)]
    out_specs = [_rows(tm, NA_WIDTH)] * 3
    out_shape = [_sds((n, NA_WIDTH), BF16)] * 3
    for _ in range(3):
        for dil in DIL_DILATIONS:
            out_specs.append(pl.BlockSpec((dil, tm // dil, 256), lambda i: (0, i, 0)))
            out_shape.append(_sds((dil, n // dil, 256), BF16))
    out_specs += [_rows(tm, D_MODEL)] * 2
    out_shape += [_sds((n, D_MODEL), F32)] * 2
    res = pl.pallas_call(
        body, name=name, grid=(n // tm,),
        in_specs=in_specs, out_specs=out_specs, out_shape=out_shape,
        scratch_shapes=[_dil_scratch(tm)],
        compiler_params=_params("parallel"),
    )(*([proj] * 20), cos_t, sin_t)
    return res[0:3], res[3:6], res[6:9], res[9:12], res[12], res[13]


def _gate_mix(sn, bn, sd, bd, *, tm, name):
    n = sn.shape[0]

    def body(sn_ref, bn_ref, sd_ref, bd_ref, o_ref):
        o_ref[...] = (sn_ref[...] * bn_ref[...] + sd_ref[...] * bd_ref[...]).astype(BF16)

    return pl.pallas_call(
        body, name=name, grid=(n // tm,), in_specs=[_rows(tm, D_MODEL)] * 4,
        out_specs=_rows(tm, D_MODEL), out_shape=_sds((n, D_MODEL), BF16),
        compiler_params=_params("parallel"),
    )(sn, bn, sd, bd)


def _residual_rms(h, delta, g, *, tm, name):
    n = h.shape[0]

    def body(h_ref, d_ref, g_ref, hn_ref, z_ref):
        hn = h_ref[...] + d_ref[...]
        hn_ref[...] = hn
        z_ref[...] = (hn * _rstd(hn) * g_ref[...]).astype(BF16)

    return pl.pallas_call(
        body, name=name, grid=(n // tm,),
        in_specs=[_rows(tm, D_MODEL), _rows(tm, D_MODEL), _const((1, D_MODEL))],
        out_specs=[_rows(tm, D_MODEL)] * 2,
        out_shape=[_sds((n, D_MODEL), F32), _sds((n, D_MODEL), BF16)],
        compiler_params=_params("parallel"),
    )(h, delta, g)


def _relu_sq(u, *, tm, name):
    n, w = u.shape

    def body(u_ref, f_ref):
        r = jnp.maximum(u_ref[...], 0.0)
        f_ref[...] = (r * r).astype(BF16)

    return pl.pallas_call(
        body, name=name, grid=(n // tm,), in_specs=[_rows(tm, w)],
        out_specs=_rows(tm, w), out_shape=_sds((n, w), BF16),
        compiler_params=_params("parallel"),
    )(u)


def _relu_sq_bwd(df, u, *, tm, name, after=None):
    n, w = u.shape
    extra = [] if after is None else [after]

    def body(df_ref, u_ref, *rest):
        rest[-1][...] = (df_ref[...] * (2.0 * jnp.maximum(u_ref[...], 0.0))).astype(BF16)

    return pl.pallas_call(
        body, name=name, grid=(n // tm,),
        in_specs=[_rows(tm, w)] * 2 + [pl.BlockSpec(memory_space=pl.ANY)] * len(extra),
        out_specs=_rows(tm, w), out_shape=_sds((n, w), BF16),
        compiler_params=_params("parallel"),
    )(df, u, *extra)


def _tail(h2, gt, pp, target, g_final, *, tm, name):
    n = h2.shape[0]

    def body(h2_ref, gt_ref, pp_ref, t_ref, g_ref, dh3_ref, dpp_ref, dgt_ref, dg_ref, loss_ref):
        i = pl.program_id(0)

        @pl.when(i == 0)
        def _():
            dg_ref[...] = jnp.zeros_like(dg_ref)
            loss_ref[...] = jnp.zeros_like(loss_ref)

        sg = _sigmoid(gt_ref[...])
        pp_v = pp_ref[...]
        h3 = h2_ref[...] + sg * pp_v
        r3 = _rstd(h3)
        n3 = h3 * r3
        g = g_ref[...]
        err = n3 * g - t_ref[...]
        loss_ref[...] += 0.5 * jnp.sum(jnp.sum(err * err, axis=-1, keepdims=True) / D_MODEL)
        dy = err / D_MODEL
        dg_ref[...] += jnp.sum(dy * n3, axis=0, keepdims=True)
        dn = dy * g
        dh3 = r3 * (dn - n3 * jnp.mean(dn * n3, axis=-1, keepdims=True))
        dh3_ref[...] = dh3
        dpp_ref[...] = (dh3 * sg).astype(BF16)
        dgt_ref[...] = (dh3 * pp_v * sg * (1.0 - sg)).astype(BF16)

    return pl.pallas_call(
        body, name=name, grid=(n // tm,),
        in_specs=[_rows(tm, D_MODEL)] * 4 + [_const((1, D_MODEL))],
        out_specs=[_rows(tm, D_MODEL)] * 3 + [_const((1, D_MODEL)), _const((1, 128))],
        out_shape=[_sds((n, D_MODEL), F32), _sds((n, D_MODEL), BF16), _sds((n, D_MODEL), BF16),
                   _sds((1, D_MODEL), F32), _sds((1, 128), F32)],
        compiler_params=_params("arbitrary"),
    )(h2, gt, pp, target, g_final)


def _rms_bwd(dz, h, g, dres, *, tm, name, want_bf16=True):
    n = h.shape[0]

    def body(dz_ref, h_ref, g_ref, dres_ref, dh_ref, *rest):
        if want_bf16:
            dhb_ref, dg_ref = rest
        else:
            dg_ref, = rest
        i = pl.program_id(0)

        @pl.when(i == 0)
        def _():
            dg_ref[...] = jnp.zeros_like(dg_ref)

        hv = h_ref[...]
        r = _rstd(hv)
        nrm = hv * r
        dz_v = dz_ref[...]
        dg_ref[...] += jnp.sum(dz_v * nrm, axis=0, keepdims=True)
        dn = dz_v * g_ref[...]
        dh = dres_ref[...] + r * (dn - nrm * jnp.mean(dn * nrm, axis=-1, keepdims=True))
        dh_ref[...] = dh
        if want_bf16:
            dhb_ref[...] = dh.astype(BF16)

    out_specs = [_rows(tm, D_MODEL)]
    out_shape = [_sds((n, D_MODEL), F32)]
    if want_bf16:
        out_specs.append(_rows(tm, D_MODEL))
        out_shape.append(_sds((n, D_MODEL), BF16))
    out_specs.append(_const((1, D_MODEL)))
    out_shape.append(_sds((1, D_MODEL), F32))
    return pl.pallas_call(
        body, name=name, grid=(n // tm,),
        in_specs=[_rows(tm, D_MODEL), _rows(tm, D_MODEL), _const((1, D_MODEL)), _rows(tm, D_MODEL)],
        out_specs=out_specs, out_shape=out_shape,
        compiler_params=_params("arbitrary"),
    )(dz, h, g, dres)


def _gate_bwd(dmixed, sn, bn, sd, bd, *, tm, name):
    n = sn.shape[0]

    def body(dm_ref, sn_ref, bn_ref, sd_ref, bd_ref, dbn_ref, dbd_ref, dgn_ref, dgd_ref):
        dm = dm_ref[...]
        s1, s2 = sn_ref[...], sd_ref[...]
        dbn_ref[...] = (dm * s1).astype(BF16)
        dbd_ref[...] = (dm * s2).astype(BF16)
        dgn_ref[...] = (dm * bn_ref[...] * s1 * (1.0 - s1)).astype(BF16)
        dgd_ref[...] = (dm * bd_ref[...] * s2 * (1.0 - s2)).astype(BF16)

    return pl.pallas_call(
        body, name=name, grid=(n // tm,), in_specs=[_rows(tm, D_MODEL)] * 5,
        out_specs=[_rows(tm, D_MODEL)] * 4, out_shape=[_sds((n, D_MODEL), BF16)] * 4,
        compiler_params=_params("parallel"),
    )(dmixed, sn, bn, sd, bd)


def _assemble_dproj(dna, ddil_q, ddil_k, ddil_v, dgn, dgd, cos_t, sin_t, *, tm, name):
    n = dgn.shape[0]

    def body(*refs):
        dq_ref, dk_ref, dv_ref = refs[0:3]
        dil_in = refs[3:12]
        dgn_ref, dgd_ref, cos_ref, sin_ref, o_ref, scr = refs[12:18]
        o_ref[:, 0:512] = dq_ref[...]
        o_ref[:, 512:1024] = dk_ref[...].astype(BF16)
        o_ref[:, 1024:1536] = dv_ref[...].astype(BF16)
        cosv, sinv = cos_ref[...], sin_ref[...]
        for t in range(3):
            for gi, dil in enumerate(DIL_DILATIONS):
                val = _load_token_order(dil_in[t * 3 + gi], scr, dil, tm)
                if t < 2:
                    val = val * cosv + _swap_halves(val * sinv)
                c0 = 1536 + t * DIL_WIDTH + gi * 256
                o_ref[:, c0:c0 + 256] = val.astype(BF16)
        o_ref[:, 3840:4864] = dgn_ref[...]
        o_ref[:, 4864:5888] = dgd_ref[...]

    in_specs = [_rows(tm, NA_WIDTH)] * 3
    for _ in range(3):
        for dil in DIL_DILATIONS:
            in_specs.append(pl.BlockSpec((dil, tm // dil, 256), lambda i: (0, i, 0)))
    in_specs += [_rows(tm, D_MODEL)] * 2 + [_rows(tm, 256)] * 2
    return pl.pallas_call(
        body, name=name, grid=(n // tm,), in_specs=in_specs,
        out_specs=_rows(tm, IN_WIDTH), out_shape=_sds((n, IN_WIDTH), BF16),
        scratch_shapes=[_dil_scratch(tm)],
        compiler_params=_params("parallel"),
    )(*dna, *ddil_q, *ddil_k, *ddil_v, dgn, dgd, cos_t, sin_t)


N_ROW_OFF = 2 * NA_WIN_ROWS - 1
N_PAIRS = N_ROW_OFF - 1
RB_WIDTH = (N_ROW_OFF + 1) * GRID_W


def _na_bias(rb_ref, pair_scr):
    shape = (GRID_W, RB_WIDTH)
    qc = lax.broadcasted_iota(jnp.int32, shape, 0)
    qc2 = lax.broadcasted_iota(jnp.int32, (GRID_W, 128), 0)
    kc2 = lax.broadcasted_iota(jnp.int32, (GRID_W, 128), 1) & (GRID_W - 1)
    cs = jnp.clip(qc2 - 8, 0, GRID_W - 16)
    valid = (kc2 >= cs) & (kc2 < cs + 16)
    for hh in range(2):
        t = jnp.broadcast_to(rb_ref[hh], shape)
        t = pltpu.roll(t, RB_WIDTH - 15, 1)
        for b in range(6):
            t = jnp.where(((qc >> b) & 1) == 1, pltpu.roll(t, 1 << b, 1), t)
        t_odd = pltpu.roll(t, RB_WIDTH - GRID_W, 1)
        for ro in range(N_PAIRS):
            src = t if ro % 2 == 0 else t_odd
            base = (ro // 2) * 128
            pair_scr[hh, ro] = jnp.where(valid, src[:, base:base + 128], NEG_INF)


NA_GROUP_FWD = 4
NA_GROUP_BWD = 4


def _stack_heads(ref, r):
    lane = lax.broadcasted_iota(jnp.int32, (GRID_W, 128), 1)
    t = ref[pl.ds(pl.multiple_of(r * GRID_W, GRID_W), GRID_W), :].astype(F32)
    return jnp.concatenate([jnp.where(lane < 64, t, 0.0), jnp.where(lane >= 64, t, 0.0)], axis=0).astype(BF16)


def _unstack_heads(t2):
    lane = lax.broadcasted_iota(jnp.int32, (GRID_W, 128), 1)
    return jnp.where(lane < 64, t2[:GRID_W], t2[GRID_W:])


def _na_window(k_ref, v_ref, r, n_rows):
    rs = jnp.clip(r - NA_WIN_ROWS // 2, 0, n_rows - NA_WIN_ROWS)
    ro0 = (NA_WIN_ROWS - 1) - (r - rs)
    off = pl.multiple_of(rs * GRID_W, GRID_W)
    kw = k_ref[pl.ds(off, NA_WIN_ROWS * GRID_W), :]
    vw = v_ref[pl.ds(off, NA_WIN_ROWS * GRID_W), :]
    return kw, vw, off, ro0


def _na_probs(s_raw, pair_scr, ro0):
    bias = [jnp.concatenate([pair_scr[hh, ro0 + 2 * j] for j in range(NA_WIN_ROWS // 2)], axis=1)
            for hh in range(2)]
    s = s_raw * QK_SCALE + jnp.concatenate(bias, axis=0)
    m = jnp.max(s, axis=-1, keepdims=True)
    e = jnp.exp(s - m)
    return e * (1.0 / jnp.sum(e, axis=-1, keepdims=True))


def _na_fwd(q, k, v, rb, *, name, comm=None):
    n = q.shape[0]
    n_rows = n // GRID_W

    def body(ins, outs, scr):
        q_ref, k_ref, v_ref, rb_ref = ins
        o_ref, = outs
        pair_scr, = scr
        _na_bias(rb_ref, pair_scr)

        def group(g, carry):
            rows = [g * NA_GROUP_FWD + t for t in range(NA_GROUP_FWD)]
            wins = [_na_window(k_ref, v_ref, r, n_rows) for r in rows]
            raw = [lax.dot_general(_stack_heads(q_ref, r), w[0], NT_DIMS, preferred_element_type=F32)
                   for r, w in zip(rows, wins)]
            probs = [_na_probs(s, pair_scr, w[3]) for s, w in zip(raw, wins)]
            outs2 = [jnp.dot(p.astype(BF16), w[1], preferred_element_type=F32) for p, w in zip(probs, wins)]
            for r, o2 in zip(rows, outs2):
                o_ref[pl.ds(pl.multiple_of(r * GRID_W, GRID_W), GRID_W), :] = _unstack_heads(o2).astype(BF16)
            return carry

        lax.fori_loop(0, n_rows // NA_GROUP_FWD, group, 0)

    col = pl.BlockSpec((n, 128), lambda h: (0, h))
    res, comm_res = _hosted_call(
        body, comm, name=name, grid=(NA_WIDTH // 128,),
        in_specs=[col, col, col, pl.BlockSpec((2, 1, RB_WIDTH), lambda h: (h, 0, 0))],
        out_specs=[col], out_shape=[_sds((n, NA_WIDTH), BF16)],
        scratch_shapes=[pltpu.VMEM((2, N_PAIRS, GRID_W, 128), F32)],
        args=(q, k, v, rb))
    return res[0] if comm is None else (res[0], comm_res)


def _na_bwd(q, k, v, do, rb, *, name, comm=None):
    n = q.shape[0]
    n_rows = n // GRID_W
    win = NA_WIN_ROWS * GRID_W

    def body(ins, outs, scr):
        q_ref, k_ref, v_ref, do_ref, rb_ref = ins
        dq_ref, dk_ref, dv_ref, drb_ref = outs
        pair_scr, acc_scr = scr
        _na_bias(rb_ref, pair_scr)
        acc_scr[...] = jnp.zeros_like(acc_scr)
        dk_ref[...] = jnp.zeros_like(dk_ref)
        dv_ref[...] = jnp.zeros_like(dv_ref)

        def group(g, carry):
            rows = [g * NA_GROUP_BWD + t for t in range(NA_GROUP_BWD)]
            wins = [_na_window(k_ref, v_ref, r, n_rows) for r in rows]
            qss = [_stack_heads(q_ref, r) for r in rows]
            doss = [_stack_heads(do_ref, r) for r in rows]
            raw = [lax.dot_general(qs, w[0], NT_DIMS, preferred_element_type=F32) for qs, w in zip(qss, wins)]
            dps = [lax.dot_general(dos, w[1], NT_DIMS, preferred_element_type=F32) for dos, w in zip(doss, wins)]
            probs = [_na_probs(s, pair_scr, w[3]) for s, w in zip(raw, wins)]
            dss = [p * (dp - jnp.sum(p * dp, axis=-1, keepdims=True)) for p, dp in zip(probs, dps)]
            dsbs = [ds.astype(BF16) for ds in dss]
            dq2s = [jnp.dot(dsb, w[0], preferred_element_type=F32) for dsb, w in zip(dsbs, wins)]
            dkws = [lax.dot_general(dsb, qs, TN_DIMS, preferred_element_type=F32) for dsb, qs in zip(dsbs, qss)]
            dvws = [lax.dot_general(p.astype(BF16), dos, TN_DIMS, preferred_element_type=F32)
                    for p, dos in zip(probs, doss)]
            for t, r in enumerate(rows):
                _, _, off, ro0 = wins[t]
                for hh in range(2):
                    for j in range(NA_WIN_ROWS // 2):
                        acc_scr[hh, ro0 + 2 * j] += dss[t][hh * GRID_W:(hh + 1) * GRID_W, j * 128:(j + 1) * 128]
                dq_ref[pl.ds(pl.multiple_of(r * GRID_W, GRID_W), GRID_W), :] = (
                    _unstack_heads(dq2s[t]) * QK_SCALE).astype(BF16)
                dk_ref[pl.ds(off, win), :] += dkws[t] * QK_SCALE
                dv_ref[pl.ds(off, win), :] += dvws[t]
            return carry

        lax.fori_loop(0, n_rows // NA_GROUP_BWD, group, 0)

        qc = lax.broadcasted_iota(jnp.int32, (N_PAIRS * GRID_W, 128), 0)
        for hh in range(2):
            t = acc_scr[hh].reshape(N_PAIRS * GRID_W, 128)
            for b in range(6):
                t = jnp.where(((qc >> b) & 1) == 1, pltpu.roll(t, 128 - (1 << b), 1), t)
            t = pltpu.roll(t, 15, 1)
            drb_ref[hh] = jnp.sum(t.reshape(N_PAIRS, GRID_W, 128), axis=1)

    col = pl.BlockSpec((n, 128), lambda h: (0, h))
    res, comm_res = _hosted_call(
        body, comm, name=name, grid=(NA_WIDTH // 128,),
        in_specs=[col, col, col, col, pl.BlockSpec((2, 1, RB_WIDTH), lambda h: (h, 0, 0))],
        out_specs=[col, col, col, pl.BlockSpec((2, N_PAIRS, 128), lambda h: (h, 0, 0))],
        out_shape=[_sds((n, NA_WIDTH), BF16), _sds((n, NA_WIDTH), F32), _sds((n, NA_WIDTH), F32),
                   _sds((8, N_PAIRS, 128), F32)],
        scratch_shapes=[pltpu.VMEM((2, N_PAIRS, GRID_W, 128), F32),
                        pltpu.VMEM((2, N_PAIRS, GRID_W, 128), F32)],
        args=(q, k, v, do, rb))
    return res if comm is None else (res, comm_res)


def _rpb_table(rpb2):
    t = jnp.pad(rpb2, ((0, 0), (0, 1), (0, GRID_W - rpb2.shape[-1])))
    return t.reshape(8, 1, RB_WIDTH)


def _rpb_grad(drb, *, name):
    kdim = drb.shape[1]

    def body(x_ref, o_ref):
        kk = lax.broadcasted_iota(jnp.int32, (128, 512), 0)
        jj = lax.broadcasted_iota(jnp.int32, (128, 512), 1)
        half, co = kk >> 6, kk & 63
        acc = jnp.zeros((8, 512), F32)
        for ro in range(N_PAIRS):
            hit = ((ro + half) == (jj >> 5)) & (co == (jj & 31)) & (co < 31)
            onehot = jnp.where(hit, 1.0, 0.0).astype(F32)
            acc = acc + jnp.dot(x_ref[:, ro * 128:(ro + 1) * 128], onehot, preferred_element_type=F32,
                                precision=lax.Precision.HIGHEST)
        o_ref[...] = acc

    return pl.pallas_call(
        body, name=name, grid=(1,),
        in_specs=[_const((8, kdim))], out_specs=_const((8, 512)), out_shape=_sds((8, 512), F32),
        compiler_params=_params("arbitrary"),
    )(drb)


def _dil_blocks(length):
    qb = min(128, length)
    return qb, min(qb + 2 * DIL_RADIUS, length)


def _dil_scores(q_ref, k_ref, v_ref, i, qb, win, length):
    start = pl.multiple_of(jnp.clip(i * qb - DIL_RADIUS, 0, length - win), DIL_RADIUS)
    kw = k_ref[0, pl.ds(start, win), :]
    vw = v_ref[0, pl.ds(start, win), :]
    qv = q_ref[0].astype(F32)
    lane = lax.broadcasted_iota(jnp.int32, (qb, 256), 1)
    qs = jnp.concatenate([jnp.where((lane >> 6) == h, qv, 0.0) for h in range(4)], axis=0).astype(BF16)
    s = lax.dot_general(qs, kw, NT_DIMS, preferred_element_type=F32) * QK_SCALE
    qi = i * qb + (lax.broadcasted_iota(jnp.int32, (4 * qb, win), 0) & (qb - 1))
    kj = start + lax.broadcasted_iota(jnp.int32, (4 * qb, win), 1)
    s = jnp.where(jnp.abs(qi - kj) <= DIL_RADIUS, s, NEG_INF)
    return s, qs, kw, vw, start, lane


def _pick_heads(stacked, lane, qb):
    out = jnp.zeros((qb, 256), stacked.dtype)
    for h in range(4):
        out = jnp.where((lane >> 6) == h, stacked[h * qb:(h + 1) * qb], out)
    return out


def _stack_head_cols(t, qb):
    return jnp.concatenate([t[:, 64 * h:64 * h + 1] for h in range(4)], axis=0)


def _dil_fwd(q, k, v, *, name):
    dil, length, _ = q.shape
    qb, win = _dil_blocks(length)

    def body(q_ref, k_ref, v_ref, o_ref, lse_ref):
        i = pl.program_id(1)
        s, _, _, vw, _, lane = _dil_scores(q_ref, k_ref, v_ref, i, qb, win, length)
        m = jnp.max(s, axis=-1, keepdims=True)
        lse = m + jnp.log(jnp.sum(jnp.exp(s - m), axis=-1, keepdims=True))
        p = jnp.exp(s - lse)
        o4 = jnp.dot(p.astype(BF16), vw, preferred_element_type=F32)
        o_ref[0] = _pick_heads(o4, lane, qb)
        lse_ref[0] = _pick_heads(jnp.broadcast_to(lse, (4 * qb, 256)), lane, qb)

    seq = pl.BlockSpec((1, length, 256), lambda j, i: (j, 0, 0))
    blk = pl.BlockSpec((1, qb, 256), lambda j, i: (j, i, 0))
    return pl.pallas_call(
        body, name=name, grid=(dil, length // qb),
        in_specs=[blk, seq, seq], out_specs=[blk, blk],
        out_shape=[_sds((dil, length, 256), F32)] * 2,
        compiler_params=_params("parallel", "parallel"),
    )(q, k, v)


def _dil_bwd(q, k, v, do, lse, cc, *, name):
    dil, length, _ = q.shape
    qb, win = _dil_blocks(length)

    def body(q_ref, k_ref, v_ref, do_ref, lse_ref, cc_ref, dq_ref, dk_ref, dv_ref):
        i = pl.program_id(1)

        @pl.when(i == 0)
        def _():
            dk_ref[...] = jnp.zeros_like(dk_ref)
            dv_ref[...] = jnp.zeros_like(dv_ref)

        s, qs, kw, vw, start, lane = _dil_scores(q_ref, k_ref, v_ref, i, qb, win, length)
        p = jnp.exp(s - _stack_head_cols(lse_ref[0], qb))
        dov = do_ref[0].astype(F32)
        dos = jnp.concatenate([jnp.where((lane >> 6) == h, dov, 0.0) for h in range(4)], axis=0).astype(BF16)
        dp = lax.dot_general(dos, vw, NT_DIMS, preferred_element_type=F32)
        ds = p * (dp + _stack_head_cols(cc_ref[0], qb))
        dsb = ds.astype(BF16)
        dq4 = jnp.dot(dsb, kw, preferred_element_type=F32)
        dq_ref[0] = _pick_heads(dq4, lane, qb) * QK_SCALE
        dk_ref[0, pl.ds(start, win), :] += lax.dot_general(dsb, qs, TN_DIMS, preferred_element_type=F32) * QK_SCALE
        dv_ref[0, pl.ds(start, win), :] += lax.dot_general(p.astype(BF16), dos, TN_DIMS, preferred_element_type=F32)

    seq = pl.BlockSpec((1, length, 256), lambda j, i: (j, 0, 0))
    blk = pl.BlockSpec((1, qb, 256), lambda j, i: (j, i, 0))
    return pl.pallas_call(
        body, name=name, grid=(dil, length // qb),
        in_specs=[blk, seq, seq, blk, blk, blk], out_specs=[blk, seq, seq],
        out_shape=[_sds((dil, length, 256), F32)] * 3,
        compiler_params=_params("parallel", "arbitrary"),
    )(q, k, v, do, lse, cc)


def _merge_weights(lses):
    m = jnp.maximum(jnp.maximum(lses[0], lses[1]), lses[2])
    es = [jnp.exp(t - m) for t in lses]
    inv = 1.0 / (es[0] + es[1] + es[2])
    return [e * inv for e in es]


def _dil_merge(outs, lses, *, tm, name):
    n = outs[0].shape[1]

    def body(*refs):
        o_in, l_in = refs[0:3], refs[3:6]
        y_ref, yb_ref, scr = refs[6:9]
        lv = [_load_token_order(l_in[g], scr, d, tm) for g, d in enumerate(DIL_DILATIONS)]
        ws = _merge_weights(lv)
        y = jnp.zeros((tm, 256), F32)
        for g, d in enumerate(DIL_DILATIONS):
            y = y + ws[g] * _load_token_order(o_in[g], scr, d, tm)
        y_ref[...] = y
        yb_ref[...] = y.astype(BF16)

    specs = [_dil_spec(d, tm) for d in DIL_DILATIONS]
    return pl.pallas_call(
        body, name=name, grid=(n // tm,), in_specs=specs + specs,
        out_specs=[_rows(tm, 256)] * 2, out_shape=[_sds((n, 256), F32), _sds((n, 256), BF16)],
        scratch_shapes=[_dil_scratch(tm)],
        compiler_params=_params("parallel"),
    )(*outs, *lses)


def _dil_merge_bwd(dy, y, lses, *, tm, name):
    n = dy.shape[0]

    def body(*refs):
        dy_ref, y_ref = refs[0:2]
        l_in = refs[2:5]
        do_out, cc_out = refs[5:8], refs[8:11]
        scr = refs[11]
        lv = [_load_token_order(l_in[g], scr, d, tm) for g, d in enumerate(DIL_DILATIONS)]
        ws = _merge_weights(lv)
        dyv = dy_ref[...]
        rr = lax.broadcasted_iota(jnp.int32, (256, 256), 0) >> 6
        cc = lax.broadcasted_iota(jnp.int32, (256, 256), 1) >> 6
        ones = jnp.where(rr == cc, 1.0, 0.0).astype(F32)
        tsum = jnp.dot(dyv * y_ref[...], ones, preferred_element_type=F32,
                       precision=lax.Precision.HIGHEST)
        for g, d in enumerate(DIL_DILATIONS):
            _store_dil_order(ws[g] * dyv, do_out[g], scr, d, tm)
            _store_dil_order(-ws[g] * tsum, cc_out[g], scr, d, tm)

    specs = [_dil_spec(d, tm) for d in DIL_DILATIONS]
    res = pl.pallas_call(
        body, name=name, grid=(n // tm,),
        in_specs=[_rows(tm, 256)] * 2 + specs,
        out_specs=specs + specs,
        out_shape=[_sds((d, n // d, 256), BF16) for d in DIL_DILATIONS]
                  + [_sds((d, n // d, 256), F32) for d in DIL_DILATIONS],
        scratch_shapes=[_dil_scratch(tm)],
        compiler_params=_params("parallel"),
    )(dy, y, *lses)
    return res[0:3], res[3:6]


_WEIGHTS = (("w_in", 1, 736), ("w_branch_na", 1, 128), ("w_branch_dil", 1, 128), ("w_out", 0, 128),
            ("w_up", 1, 512), ("w_down", 0, 512), ("w_ple_gate", 0, 128), ("w_ple_proj", 1, 128))
_W_IN, _W_BNA, _W_BD, _W_OUT, _W_UP, _W_DOWN, _W_PG, _W_PP = range(8)
_GATHER_EARLY = (_W_BNA, _W_BD, _W_OUT, _W_PG, _W_PP)
_GATHER_LATE = (_W_UP, _W_DOWN)


def _to_full(widx, gathered):
    if _WEIGHTS[widx][1] == 0:
        return gathered.reshape(-1, gathered.shape[2])
    return jnp.transpose(gathered, (1, 0, 2)).reshape(gathered.shape[1], -1)


def _to_chunks(widx, mat):
    _, axis, width = _WEIGHTS[widx]
    if axis == 0:
        return mat.reshape(N_DEV, width, mat.shape[1])
    return jnp.transpose(mat.reshape(mat.shape[0], N_DEV, width), (1, 0, 2))


def _local_step(x, p_bf16, positions, target, g_mix, g_mlp, g_ple, g_final, rpb2, get_w_in, get_rest, send_grads):
    tm = 256
    half = HEAD_DIM // 2
    inv_freq = 10000.0 ** (-jnp.arange(half, dtype=F32) / half)
    ang = positions.astype(F32)[:, None] * inv_freq
    cos, sin = jnp.cos(ang), jnp.sin(ang)
    cos_t = jnp.tile(jnp.concatenate([cos, cos], axis=-1), (1, 4))
    sin_t = jnp.tile(jnp.concatenate([-sin, sin], axis=-1), (1, 4))
    rb = _rpb_table(rpb2)

    a = _rms_fwd(x, g_mix, tm=tm, name="rms_mix")
    w_in, token = get_w_in(a)
    proj = _matmul(a, w_in, out_dtype=F32, tm=512, tn=2944, tk=1024, name="mm_in", after=token)
    na_qkv, dq_g, dk_g, dv_g, sn, sd = _split_proj(proj, cos_t, sin_t, tm=tm, name="split_proj")
    y_na = _na_fwd(*na_qkv, rb, name="na_fwd")
    d_out, d_lse = [], []
    for g in range(3):
        o, lse = _dil_fwd(dq_g[g], dk_g[g], dv_g[g], name=f"dil_fwd{g}")
        d_out.append(o)
        d_lse.append(lse)
    y_dil, y_dil_b = _dil_merge(d_out, d_lse, tm=tm, name="dil_merge")
    w_bna, w_bd, w_out, w_up, w_down, w_pg, w_pp = get_rest(y_dil_b)
    bn = _matmul(y_na, w_bna, out_dtype=F32, tm=512, tn=1024, tk=512, name="mm_bna")
    bd = _matmul(y_dil_b, w_bd, out_dtype=F32, tm=512, tn=1024, tk=256, name="mm_bd")
    mixed = _gate_mix(sn, bn, sd, bd, tm=tm, name="gate_mix")
    mo = _matmul(mixed, w_out, out_dtype=F32, tm=512, tn=1024, tk=1024, name="mm_out")
    h1, c = _residual_rms(x, mo, g_mlp, tm=tm, name="res_rms_mlp")
    u = _matmul(c, w_up, out_dtype=F32, tm=512, tn=2048, tk=1024, name="mm_up")
    f = _relu_sq(u, tm=128, name="relu_sq")
    dn = _matmul(f, w_down, out_dtype=F32, tm=512, tn=1024, tk=2048, name="mm_down")
    h2, e = _residual_rms(h1, dn, g_ple, tm=tm, name="res_rms_ple")
    gt = _matmul(e, w_pg, out_dtype=F32, tm=512, tn=1024, tk=1024, name="mm_pg")
    pp = _matmul(p_bf16, w_pp, out_dtype=F32, tm=512, tn=1024, tk=256, name="mm_pp")

    dh3, dpp, dgt, dg_final, loss = _tail(h2, gt, pp, target, g_final, tm=tm, name="tail")
    gw_pp = _matmul(p_bf16, dpp, ta=True, out_dtype=BF16, tm=256, tn=1024, tk=512, name="mm_gw_pp")
    gw_pg = _matmul(e, dgt, ta=True, out_dtype=BF16, tm=512, tn=1024, tk=512, name="mm_gw_pg")
    de = _matmul(dgt, w_pg, tb=True, out_dtype=F32, tm=512, tn=1024, tk=1024, name="mm_de")
    dh2, dh2_b, dg_ple = _rms_bwd(de, h2, g_ple, dh3, tm=tm, name="rms_bwd_ple")
    df = _matmul(dh2_b, w_down, tb=True, out_dtype=F32, tm=512, tn=2048, tk=1024, name="mm_df")
    gw_down = _matmul(f, dh2_b, ta=True, out_dtype=BF16, tm=1024, tn=1024, tk=512, name="mm_gw_down")
    token = send_grads((_W_PP, _W_PG, _W_DOWN), (gw_pp, gw_pg, gw_down))
    du = _relu_sq_bwd(df, u, tm=128, name="relu_sq_bwd", after=token)
    gw_up = _matmul(c, du, ta=True, out_dtype=BF16, tm=512, tn=2048, tk=512, name="mm_gw_up")
    token = send_grads((_W_UP,), (gw_up,))
    dc = _matmul(du, w_up, tb=True, out_dtype=F32, tm=512, tn=1024, tk=2048, name="mm_dc", after=token)
    dh1, dh1_b, dg_mlp = _rms_bwd(dc, h1, g_mlp, dh2, tm=tm, name="rms_bwd_mlp")
    dmixed = _matmul(dh1_b, w_out, tb=True, out_dtype=F32, tm=512, tn=1024, tk=1024, name="mm_dmixed")
    gw_out = _matmul(mixed, dh1_b, ta=True, out_dtype=BF16, tm=512, tn=1024, tk=512, name="mm_gw_out")
    dbn, dbd, dgn, dgd = _gate_bwd(dmixed, sn, bn, sd, bd, tm=tm, name="gate_bwd")
    gw_bna = _matmul(y_na, dbn, ta=True, out_dtype=BF16, tm=512, tn=1024, tk=512, name="mm_gw_bna")
    dy_na = _matmul(dbn, w_bna, tb=True, out_dtype=BF16, tm=512, tn=512, tk=1024, name="mm_dy_na")
    gw_bd = _matmul(y_dil_b, dbd, ta=True, out_dtype=BF16, tm=256, tn=1024, tk=512, name="mm_gw_bd")
    token = send_grads((_W_OUT, _W_BNA, _W_BD), (gw_out, gw_bna, gw_bd))
    dy_dil = _matmul(dbd, w_bd, tb=True, out_dtype=F32, tm=512, tn=256, tk=1024, name="mm_dy_dil", after=token)
    dna = _na_bwd(*na_qkv, dy_na, rb, name="na_bwd")
    drpb = _rpb_grad(dna[3].reshape(8, -1), name="rpb_grad")
    do_g, cc_g = _dil_merge_bwd(dy_dil, y_dil, d_lse, tm=tm, name="dil_merge_bwd")
    ddq, ddk, ddv = [], [], []
    for g in range(3):
        r = _dil_bwd(dq_g[g], dk_g[g], dv_g[g], do_g[g], d_lse[g], cc_g[g], name=f"dil_bwd{g}")
        ddq.append(r[0])
        ddk.append(r[1])
        ddv.append(r[2])
    dproj = _assemble_dproj(dna[0:3], ddq, ddk, ddv, dgn, dgd, cos_t, sin_t, tm=tm, name="assemble_dproj")
    gw_in = _matmul(a, dproj, ta=True, out_dtype=BF16, tm=512, tn=2944, tk=512, name="mm_gw_in")
    token = send_grads((_W_IN,), (gw_in,))
    da = _matmul(dproj, w_in, tb=True, out_dtype=F32, tm=512, tn=1024, tk=2944, name="mm_da", after=token)
    dx, dg_mix = _rms_bwd(da, x, g_mix, dh1, tm=tm, name="rms_bwd_mix", want_bf16=False)
    return loss, dx, (dg_mix, dg_mlp, dg_ple, dg_final), drpb


def _cast_bf16(t, *, name):
    def body(t_ref, o_ref):
        o_ref[...] = t_ref[...].astype(BF16)

    rows, cols = t.shape
    tr = min(256, rows)
    blk = pl.BlockSpec((tr, cols), lambda i: (i, 0))
    return pl.pallas_call(body, name=name, grid=(rows // tr,), in_specs=[blk], out_specs=blk,
                          out_shape=_sds(t.shape, BF16), compiler_params=_params("parallel"))(t)


def _adamw(w, g, m, v):
    m = ADAM_B1 * m + (1.0 - ADAM_B1) * g
    v = ADAM_B2 * v + (1.0 - ADAM_B2) * (g * g)
    m_hat = m / (1.0 - ADAM_B1 ** ADAM_STEP)
    v_hat = v / (1.0 - ADAM_B2 ** ADAM_STEP)
    delta = -ADAM_LR * (m_hat / (jnp.sqrt(v_hat) + ADAM_EPS) + ADAM_WD * w)
    return delta, m, v


def _sum_adamw(parts, w, m, v, *, tr, name, own=None):
    rows, cols = w.shape

    def body(*refs):
        p_ref, w_ref, m_ref, v_ref = refs[:4]
        g_ref, d_ref, nm_ref, nv_ref = refs[-4:]
        g = (p_ref[0] if own is None else refs[4][...]).astype(F32)
        for s in range(1, N_DEV):
            g = g + p_ref[s].astype(F32)
        g_ref[...] = g
        d_ref[...], nm_ref[...], nv_ref[...] = _adamw(w_ref[...], g, m_ref[...], v_ref[...])

    blk = pl.BlockSpec((tr, cols), lambda i: (i, 0))
    extra = [] if own is None else [own]
    return pl.pallas_call(
        body, name=name, grid=(rows // tr,),
        in_specs=[pl.BlockSpec((N_DEV, tr, cols), lambda i: (0, i, 0)), blk, blk, blk] + [blk] * len(extra),
        out_specs=[blk] * 4, out_shape=[_sds((rows, cols), F32)] * 4,
        compiler_params=_params("parallel"),
    )(parts, w, m, v, *extra)


_RPB_SIZE = 8 * 15 * 31


def _pack_small(g_mix, g_mlp, g_ple, g_final, rpb, loss_row):
    flat = jnp.concatenate([g_mix.reshape(-1), g_mlp.reshape(-1), g_ple.reshape(-1), g_final.reshape(-1),
                            rpb.reshape(-1), jnp.zeros((3840 - _RPB_SIZE,), F32), loss_row.reshape(-1),
                            jnp.zeros((128,), F32)])
    return flat.reshape(64, 128)


def _unpack_small(t):
    flat = t.reshape(-1)
    return (flat[0:1024].reshape(1, 1024), flat[4096:4096 + _RPB_SIZE].reshape(1, 8, 15, 31),
            flat[1024:2048].reshape(1, 1024), flat[2048:3072].reshape(1, 1024), flat[3072:4096])


def kernel(x, p, positions, g_mix, w_in, rpb, w_branch_na, w_branch_dil, w_out, g_mlp, w_up, w_down, g_ple, w_ple_gate, w_ple_proj, g_final, loss_target, m_g_mix, m_w_in, m_rpb, m_w_branch_na, m_w_branch_dil, m_w_out, m_g_mlp, m_w_up, m_w_down, m_g_ple, m_w_ple_gate, m_w_ple_proj, m_g_final, v_g_mix, v_w_in, v_rpb, v_w_branch_na, v_w_branch_dil, v_w_out, v_g_mlp, v_w_up, v_w_down, v_g_ple, v_w_ple_gate, v_w_ple_proj, v_g_final):
    sharded = dict(w_in=(w_in, m_w_in, v_w_in), w_branch_na=(w_branch_na, m_w_branch_na, v_w_branch_na),
                   w_branch_dil=(w_branch_dil, m_w_branch_dil, v_w_branch_dil), w_out=(w_out, m_w_out, v_w_out),
                   w_up=(w_up, m_w_up, v_w_up), w_down=(w_down, m_w_down, v_w_down),
                   w_ple_gate=(w_ple_gate, m_w_ple_gate, v_w_ple_gate),
                   w_ple_proj=(w_ple_proj, m_w_ple_proj, v_w_ple_proj))
    shards = {k: tuple(t[0] for t in val) for k, val in sharded.items()}

    me = _my_index()

    w_in_b = _cast_bf16(shards["w_in"][0], name="cast_w_in")
    rest_b = [shards[name][0].astype(BF16) for name, _, _ in _WEIGHTS[1:]]
    gather_in, token_in = _start_copies(_gather_copies, [w_in_b], [_sds((N_DEV,) + w_in_b.shape, BF16)],
                                        name="start_gather_w_in")

    def whole(widx, landed, mine):
        return _to_full(widx, lax.dynamic_update_index_in_dim(landed, mine, me, 0))

    rest_handle = []

    def get_w_in(after):
        landed, = _wait_copies(_gather_copies, gather_in, after, name="wait_gather_w_in")
        handle, token = _start_copies(_gather_copies, rest_b, [_sds((N_DEV,) + t.shape, BF16) for t in rest_b],
                                      name="start_gather_rest", after=landed)
        rest_handle.append(handle)
        return whole(_W_IN, landed, w_in_b), token

    def get_rest(after):
        landed = _wait_copies(_gather_copies, rest_handle[0], after, name="wait_gather_rest")
        return [whole(i + 1, t, mine) for i, (t, mine) in enumerate(zip(landed, rest_b))]

    sent = []

    def send_grads(indices, grads):
        chunked = [_to_chunks(i, g) for i, g in zip(indices, grads)]
        handle, token = _start_copies(_exchange_copies, chunked, [_sds(t.shape, BF16) for t in chunked],
                                      name="start_exchange_" + "_".join(_WEIGHTS[i][0] for i in indices))
        sent.append((indices, chunked, handle))
        return token

    g_mix_0 = g_mix + token_in[0:1, 0:1]
    loss, dx, dgs, drpb = _local_step(
        x[0], p[0, 0].astype(BF16), positions[0], loss_target[0],
        g_mix_0, g_mlp, g_ple, g_final.reshape(1, -1), rpb[0], get_w_in, get_rest, send_grads)

    drpb3 = drpb.reshape(8, 16, 32)[:, :15, :31]
    small = _pack_small(dgs[0], dgs[1], dgs[2], dgs[3], drpb3, loss)
    small_all, = _comm_only(_plan_share(small), name="share_small")

    out = {}
    done = dx
    for indices, chunked, handle in sent:
        landed = _wait_copies(_exchange_copies, handle, done,
                              name="wait_exchange_" + "_".join(_WEIGHTS[i][0] for i in indices))
        for i, part, mine in zip(indices, landed, chunked):
            name = _WEIGHTS[i][0]
            w, m, v = shards[name]
            own = lax.dynamic_index_in_dim(mine, me, 0, keepdims=False)
            res = _sum_adamw(part, w, m, v, tr=min(128, w.shape[0]), name="adamw_" + name, own=own)
            out[name] = [t[None] for t in res]
            done = res[0]
    small_w = _pack_small(g_mix, g_mlp, g_ple, g_final, rpb, jnp.zeros((128,), F32))
    small_m = _pack_small(m_g_mix, m_g_mlp, m_g_ple, m_g_final, m_rpb, jnp.zeros((128,), F32))
    small_v = _pack_small(v_g_mix, v_g_mlp, v_g_ple, v_g_final, v_rpb, jnp.zeros((128,), F32))
    res = _sum_adamw(small_all, small_w, small_m, small_v, tr=64, name="adamw_small")
    unpacked = [_unpack_small(t) for t in res]
    for i, name in enumerate(("g_mix", "rpb", "g_mlp", "g_ple", "g_final")):
        out[name] = [u[i] for u in unpacked]
    loss_total = res[0][62, 0]

    order = ("g_mix", "w_in", "rpb", "w_branch_na", "w_branch_dil", "w_out", "g_mlp", "w_up", "w_down",
             "g_ple", "w_ple_gate", "w_ple_proj", "g_final")
    grads = [out[k][0] for k in order]
    deltas = [out[k][1] for k in order]
    new_m = [out[k][2] for k in order]
    new_v = [out[k][3] for k in order]
    return (loss_total, dx[None], *grads, *deltas, *new_m, *new_v)
```

```python
import functools

import numpy as np
import jax
import jax.numpy as jnp
from jax import lax
from jax.experimental import pallas as pl
from jax.experimental.pallas import tpu as pltpu

F32 = jnp.float32
BF16 = jnp.bfloat16

D_MODEL = 1024
HEAD_DIM = 64
GRID_W = 64
NA_WIDTH = 512
DIL_WIDTH = 768
DIL_OUT = 256
D_FF = 4096
IN_WIDTH = 5888
DIL_DILATIONS = (1, 4, 16)
DIL_RADIUS = 64
NA_WIN_ROWS = 8
RMS_EPS = 1e-6
NEG_INF = -1e30
QK_SCALE = HEAD_DIM ** -0.5

ADAM_LR = 0.001
ADAM_B1 = 0.9
ADAM_B2 = 0.999
ADAM_EPS = 1e-08
ADAM_WD = 0.01
ADAM_STEP = 10

N_DEV = 8
VMEM_LIMIT = 56 * 1024 * 1024
MESH = pl.DeviceIdType.MESH

NT_DIMS = (((1,), (1,)), ((), ()))
TN_DIMS = (((0,), (0,)), ((), ()))


def _sds(shape, dtype):
    return jax.ShapeDtypeStruct(shape, dtype)


def _params(*sem):
    return pltpu.CompilerParams(dimension_semantics=sem, vmem_limit_bytes=VMEM_LIMIT)


def _rows(tm, width, col=0):
    return pl.BlockSpec((tm, width), lambda i, c=col: (i, c))


def _const(shape):
    zeros = (0,) * len(shape)
    return pl.BlockSpec(shape, lambda i: zeros)


def _my_index():
    return 4 * lax.axis_index("x") + 2 * lax.axis_index("y") + lax.axis_index("c")


def _peer(k):
    x, y, c = lax.axis_index("x"), lax.axis_index("y"), lax.axis_index("c")
    px = 1 - x if k & 4 else x
    py = 1 - y if k & 2 else y
    pc = 1 - c if k & 1 else c
    return (px, py, pc), 4 * px + 2 * py + pc


class _CommPlan:
    def __init__(self, ins, out_shapes, n_remote, n_local, start, wait):
        self.ins, self.out_shapes = list(ins), list(out_shapes)
        self.n_remote, self.n_local = n_remote, n_local
        self.start, self.wait = start, wait


def _plan_all_gather(shards):
    n_w = len(shards)

    def remote(ins, outs, send, recv, w, k, slot):
        dev, idx = _peer(k)
        return pltpu.make_async_remote_copy(
            src_ref=ins[w], dst_ref=outs[w].at[idx if slot is None else slot],
            send_sem=send.at[w * 7 + k - 1], recv_sem=recv.at[w * 7 + k - 1],
            device_id=dev, device_id_type=MESH)

    def start(ins, outs, send, recv, local):
        me = _my_index()
        for w in range(n_w):
            pltpu.make_async_copy(ins[w], outs[w].at[me], local.at[w]).start()
            for k in range(1, N_DEV):
                remote(ins, outs, send, recv, w, k, me).start()

    def wait(ins, outs, send, recv, local):
        me = _my_index()
        for w in range(n_w):
            for k in range(1, N_DEV):
                remote(ins, outs, send, recv, w, k, None).wait()
            pltpu.make_async_copy(ins[w], outs[w].at[me], local.at[w]).wait()

    return _CommPlan(shards, [_sds((N_DEV,) + s.shape, s.dtype) for s in shards], n_w * 7, n_w, start, wait)


def _plan_exchange(chunked):
    n_w = len(chunked)

    def remote(ins, outs, send, recv, w, k):
        dev, idx = _peer(k)
        return pltpu.make_async_remote_copy(
            src_ref=ins[w].at[idx], dst_ref=outs[w].at[k],
            send_sem=send.at[w * 7 + k - 1], recv_sem=recv.at[w * 7 + k - 1],
            device_id=dev, device_id_type=MESH)

    def start(ins, outs, send, recv, local):
        me = _my_index()
        for w in range(n_w):
            pltpu.make_async_copy(ins[w].at[me], outs[w].at[0], local.at[w]).start()
        for k in range(1, N_DEV):
            for w in range(n_w):
                remote(ins, outs, send, recv, w, k).start()

    def wait(ins, outs, send, recv, local):
        me = _my_index()
        for k in range(1, N_DEV):
            for w in range(n_w):
                remote(ins, outs, send, recv, w, k).wait()
        for w in range(n_w):
            pltpu.make_async_copy(ins[w].at[me], outs[w].at[0], local.at[w]).wait()

    return _CommPlan(chunked, [_sds(t.shape, t.dtype) for t in chunked], n_w * 7, n_w, start, wait)


def _plan_share(block):
    def remote(ins, outs, send, recv, k, slot):
        dev, idx = _peer(k)
        return pltpu.make_async_remote_copy(
            src_ref=ins[0], dst_ref=outs[0].at[idx if slot is None else slot],
            send_sem=send.at[k - 1], recv_sem=recv.at[k - 1], device_id=dev, device_id_type=MESH)

    def start(ins, outs, send, recv, local):
        me = _my_index()
        pltpu.make_async_copy(ins[0], outs[0].at[me], local.at[0]).start()
        for k in range(1, N_DEV):
            remote(ins, outs, send, recv, k, me).start()

    def wait(ins, outs, send, recv, local):
        for k in range(1, N_DEV):
            remote(ins, outs, send, recv, k, None).wait()
        pltpu.make_async_copy(ins[0], outs[0].at[_my_index()], local.at[0]).wait()

    return _CommPlan([block], [_sds((N_DEV,) + block.shape, block.dtype)], 7, 1, start, wait)


def _hosted_call(body, plan, *, name, grid, in_specs, out_specs, out_shape, scratch_shapes, args, after=None):
    n_in, n_out, n_scr = len(in_specs), len(out_specs), len(scratch_shapes)
    sem = ("arbitrary",) * len(grid)
    if plan is None:
        extra = [] if after is None else [after]
        n_x = n_in + len(extra)

        def plain(*refs):
            body(refs[:n_in], refs[n_x:n_x + n_out], refs[n_x + n_out:])

        res = pl.pallas_call(plain, name=name, grid=grid,
                             in_specs=list(in_specs) + [pl.BlockSpec(memory_space=pl.ANY)] * len(extra),
                             out_specs=out_specs, out_shape=out_shape, scratch_shapes=scratch_shapes,
                             compiler_params=_params(*sem))(*args, *extra)
        return list(res), None

    n_ci, n_co = len(plan.ins), len(plan.out_shapes)
    hbm = pl.BlockSpec(memory_space=pl.ANY)

    def hosted(*refs):
        ins, refs = refs[:n_in], refs[n_in:]
        c_ins, refs = refs[:n_ci], refs[n_ci:]
        outs, refs = refs[:n_out], refs[n_out:]
        c_outs, refs = refs[:n_co], refs[n_co:]
        scr, sems = refs[:n_scr], refs[n_scr:]
        first = functools.reduce(jnp.logical_and, [pl.program_id(d) == 0 for d in range(len(grid))])
        last = functools.reduce(jnp.logical_and, [pl.program_id(d) == grid[d] - 1 for d in range(len(grid))])

        @pl.when(first)
        def _():
            plan.start(c_ins, c_outs, *sems)

        body(ins, outs, scr)

        @pl.when(last)
        def _():
            plan.wait(c_ins, c_outs, *sems)

    res = pl.pallas_call(
        hosted, name=name, grid=grid,
        in_specs=list(in_specs) + [hbm] * n_ci, out_specs=list(out_specs) + [hbm] * n_co,
        out_shape=list(out_shape) + plan.out_shapes,
        scratch_shapes=list(scratch_shapes) + [pltpu.SemaphoreType.DMA((plan.n_remote,)),
                                               pltpu.SemaphoreType.DMA((plan.n_remote,)),
                                               pltpu.SemaphoreType.DMA((plan.n_local,))],
        compiler_params=_params(*sem),
    )(*args, *plan.ins)
    return list(res[:n_out]), list(res[n_out:])


def _comm_only(plan, *, name):
    hbm = pl.BlockSpec(memory_space=pl.ANY)
    n_ci, n_co = len(plan.ins), len(plan.out_shapes)

    def body(*refs):
        c_ins, c_outs, sems = refs[:n_ci], refs[n_ci:n_ci + n_co], refs[n_ci + n_co:]
        plan.start(c_ins, c_outs, *sems)
        plan.wait(c_ins, c_outs, *sems)

    return pl.pallas_call(
        body, name=name, in_specs=[hbm] * n_ci, out_specs=[hbm] * n_co, out_shape=plan.out_shapes,
        scratch_shapes=[pltpu.SemaphoreType.DMA((plan.n_remote,)), pltpu.SemaphoreType.DMA((plan.n_remote,)),
                        pltpu.SemaphoreType.DMA((plan.n_local,))],
        compiler_params=pltpu.CompilerParams(vmem_limit_bytes=VMEM_LIMIT),
    )(*plan.ins)


_HBM_SPEC = pl.BlockSpec(memory_space=pltpu.HBM)
_SEM_SPEC = pl.BlockSpec(memory_space=pltpu.SEMAPHORE)
_SIDE_EFFECT = pltpu.SideEffectType.DATAFLOW_SIDE_EFFECTING


def _gather_copies(srcs, lands, send, recv, sending):
    me = _my_index()
    out = []
    for w in range(len(srcs)):
        for k in range(1, N_DEV):
            dev, idx = _peer(k)
            out.append(pltpu.make_async_remote_copy(
                src_ref=srcs[w], dst_ref=lands[w].at[me if sending else idx],
                send_sem=send.at[w * 7 + k - 1], recv_sem=recv.at[w * 7 + k - 1],
                device_id=dev, device_id_type=MESH))
    return out


def _exchange_copies(srcs, lands, send, recv, sending):
    out = []
    for w in range(len(srcs)):
        for k in range(1, N_DEV):
            dev, idx = _peer(k)
            out.append(pltpu.make_async_remote_copy(
                src_ref=srcs[w].at[idx], dst_ref=lands[w].at[k],
                send_sem=send.at[w * 7 + k - 1], recv_sem=recv.at[w * 7 + k - 1],
                device_id=dev, device_id_type=MESH))
    return out


def _start_copies(make, srcs, land_shapes, *, name, after=None):
    n = len(srcs)
    extra = [] if after is None else [after]

    def body(*refs):
        src_refs, land_refs = refs[:n], refs[n:2 * n]
        send, recv = refs[2 * n + len(extra)], refs[2 * n + len(extra) + 1]
        token = refs[-1]
        for cp in make(src_refs, land_refs, send, recv, True):
            cp.start()
        token[...] = jnp.zeros_like(token)

    lands = [pltpu.with_memory_space_constraint(lax.empty(s.shape, s.dtype), pltpu.HBM) for s in land_shapes]
    res = pl.pallas_call(
        body, name=name,
        out_shape=(pltpu.SemaphoreType.DMA((7 * n,)), pltpu.SemaphoreType.DMA((7 * n,)),
                   *[pltpu.HBM(s.shape, s.dtype) for s in srcs],
                   *[pltpu.HBM(s.shape, s.dtype) for s in land_shapes],
                   _sds((8, 128), F32)),
        in_specs=[_HBM_SPEC] * (2 * n) + [pl.BlockSpec(memory_space=pl.ANY)] * len(extra),
        out_specs=(_SEM_SPEC, _SEM_SPEC, *([_HBM_SPEC] * (2 * n)), pl.BlockSpec(memory_space=pltpu.VMEM)),
        input_output_aliases={i: 2 + i for i in range(2 * n)},
        compiler_params=pltpu.CompilerParams(has_side_effects=_SIDE_EFFECT),
    )(*[pltpu.with_memory_space_constraint(s, pltpu.HBM) for s in srcs], *lands, *extra)
    return (n, res[0], res[1], res[2:2 + n], res[2 + n:2 + 2 * n]), res[-1]


def _wait_copies(make, handle, after, *, name):
    n, send_sems, recv_sems, srcs, lands = handle

    def body(*refs):
        src_refs, land_refs = refs[:n], refs[n:2 * n]
        send, recv = refs[2 * n], refs[2 * n + 1]
        for cp in make(src_refs, land_refs, send, recv, False):
            cp.wait_send()
            cp.wait_recv()

    res = pl.pallas_call(
        body, name=name,
        out_shape=tuple(pltpu.HBM(s.shape, s.dtype) for s in (*srcs, *lands)),
        in_specs=[_HBM_SPEC] * (2 * n) + [_SEM_SPEC, _SEM_SPEC, pl.BlockSpec(memory_space=pl.ANY)],
        out_specs=tuple([_HBM_SPEC] * (2 * n)),
        input_output_aliases={i: i for i in range(2 * n)},
        compiler_params=pltpu.CompilerParams(has_side_effects=_SIDE_EFFECT),
    )(*srcs, *lands, send_sems, recv_sems, after)
    return list(res[n:])


def _matmul(a, b, *, ta=False, tb=False, out_dtype, tm, tn, tk, name, after=None, extra=(), epilogue=None):
    m, k = (a.shape[1], a.shape[0]) if ta else a.shape
    n = b.shape[0] if tb else b.shape[1]
    tm, tn, tk = min(tm, m), min(tn, n), min(tk, k)
    nk = k // tk
    dims = (((0 if ta else 1,), (1 if tb else 0,)), ((), ()))
    out_dtypes = out_dtype if isinstance(out_dtype, tuple) else (out_dtype,)

    def finish(acc, x_refs, o_refs):
        vals = (acc,) if epilogue is None else epilogue(acc, *[r[...] for r in x_refs])
        for o_ref, val in zip(o_refs, vals):
            o_ref[...] = val.astype(o_ref.dtype)

    def body(ins, outs, acc):
        a_ref, b_ref = ins[:2]
        part = lax.dot_general(a_ref[...], b_ref[...], dims, preferred_element_type=F32)
        if nk == 1:
            finish(part, ins[2:], outs)
            return
        acc_ref, = acc
        kk = pl.program_id(2)

        @pl.when(kk == 0)
        def _():
            acc_ref[...] = part

        @pl.when(kk > 0)
        def _():
            acc_ref[...] += part

        @pl.when(kk == nk - 1)
        def _():
            finish(acc_ref[...], ins[2:], outs)

    a_spec = (pl.BlockSpec((tk, tm), lambda j, i, kk: (kk, i)) if ta
              else pl.BlockSpec((tm, tk), lambda j, i, kk: (i, kk)))
    b_spec = (pl.BlockSpec((tn, tk), lambda j, i, kk: (j, kk)) if tb
              else pl.BlockSpec((tk, tn), lambda j, i, kk: (kk, j)))
    tile = pl.BlockSpec((tm, tn), lambda j, i, kk: (i, j))
    row = pl.BlockSpec((1, tn), lambda j, i, kk: (0, j))
    res, _ = _hosted_call(
        body, None, name=name, grid=(n // tn, m // tm, nk),
        in_specs=[a_spec, b_spec] + [row if t.shape[0] == 1 else tile for t in extra],
        out_specs=[tile] * len(out_dtypes),
        out_shape=[_sds((m, n), dt) for dt in out_dtypes],
        scratch_shapes=[] if nk == 1 else [pltpu.VMEM((tm, tn), F32)],
        args=(a, b, *extra), after=after)
    return res if isinstance(out_dtype, tuple) else res[0]


def _rstd(h):
    return lax.rsqrt(jnp.mean(h * h, axis=-1, keepdims=True) + RMS_EPS)


def _sigmoid(z):
    return 1.0 / (1.0 + jnp.exp(-z))


def _rms_fwd(x, g, *, tm, name):
    n = x.shape[0]

    def body(x_ref, g_ref, o_ref):
        h = x_ref[...]
        o_ref[...] = (h * _rstd(h) * g_ref[...]).astype(BF16)

    return pl.pallas_call(
        body, name=name, grid=(n // tm,),
        in_specs=[_rows(tm, D_MODEL), _const((1, D_MODEL))],
        out_specs=_rows(tm, D_MODEL), out_shape=_sds((n, D_MODEL), BF16),
        compiler_params=_params("parallel"),
    )(x, g)


def _swap_halves(t):
    width = t.shape[1]
    lane = lax.broadcasted_iota(jnp.int32, t.shape, 1)
    return jnp.where((lane & 63) < 32, pltpu.roll(t, width - 32, 1), pltpu.roll(t, 32, 1))


def _dil_spec(dil, tm):
    return pl.BlockSpec((dil, tm // dil, 256), lambda i: (0, i, 0))


def _dil_scratch(tm):
    return pltpu.VMEM((2, tm, 128), F32)


def _load_token_order(src, scr, dil, tm):
    if dil == 1:
        return src[0]
    for j in range(dil):
        for c in range(2):
            scr[c, pl.ds(j, tm // dil, stride=dil), :] = src[j, :, c * 128:(c + 1) * 128]
    return jnp.concatenate([scr[0], scr[1]], axis=1)


def _store_dil_order(val, dst, scr, dil, tm):
    if dil == 1:
        dst[0] = val.astype(dst.dtype)
        return
    for c in range(2):
        scr[c] = val[:, c * 128:(c + 1) * 128]
    for j in range(dil):
        for c in range(2):
            dst[j, :, c * 128:(c + 1) * 128] = scr[c, pl.ds(j, tm // dil, stride=dil), :].astype(dst.dtype)


def _split_proj(proj, cos_t, sin_t, *, tm, name):
    n = proj.shape[0]
    n_dil = len(DIL_DILATIONS)

    def body(*refs):
        na_in = refs[0:3]
        dil_in = refs[3:3 + 3 * n_dil]
        gate_in = refs[12:20]
        cos_ref, sin_ref = refs[20:22]
        outs = refs[22:]
        na_out = outs[0:3]
        dil_out = outs[3:12]
        sn_ref, sd_ref = outs[12:14]
        scr = outs[14]
        for t in range(3):
            na_out[t][...] = na_in[t][...].astype(BF16)
        cosv, sinv = cos_ref[...], sin_ref[...]
        for t in range(3):
            for gi, dil in enumerate(DIL_DILATIONS):
                val = dil_in[t * n_dil + gi][...]
                if t < 2:
                    val = val * cosv + _swap_halves(val) * sinv
                _store_dil_order(val, dil_out[t * n_dil + gi], scr, dil, tm)
        for c in range(4):
            sn_ref[:, c * 256:(c + 1) * 256] = _sigmoid(gate_in[c][...])
            sd_ref[:, c * 256:(c + 1) * 256] = _sigmoid(gate_in[4 + c][...])

    in_specs = [_rows(tm, NA_WIDTH, c) for c in range(3)]
    in_specs += [_rows(tm, 256, 6 + c) for c in range(9)]
    in_specs += [_rows(tm, 256, 15 + c) for c in range(8)]
    in_specs += [_rows(tm, 256), _rows(tm, 256)]
    out_specs = [_rows(tm, NA_WIDTH)] * 3
    out_shape = [_sds((n, NA_WIDTH), BF16)] * 3
    for _ in range(3):
        for dil in DIL_DILATIONS:
            out_specs.append(pl.BlockSpec((dil, tm // dil, 256), lambda i: (0, i, 0)))
            out_shape.append(_sds((dil, n // dil, 256), BF16))
    out_specs += [_rows(tm, D_MODEL)] * 2
    out_shape += [_sds((n, D_MODEL), F32)] * 2
    res = pl.pallas_call(
        body, name=name, grid=(n // tm,),
        in_specs=in_specs, out_specs=out_specs, out_shape=out_shape,
        scratch_shapes=[_dil_scratch(tm)],
        compiler_params=_params("parallel"),
    )(*([proj] * 20), cos_t, sin_t)
    return res[0:3], res[3:6], res[6:9], res[9:12], res[12], res[13]


def _residual_rms_tile(delta, h, g):
    hn = h + delta
    return hn, hn * _rstd(hn) * g


def _gate_bwd_tile(dm, s1, b1, s2, b2):
    return dm * s1, dm * s2, dm * b1 * s1 * (1.0 - s1), dm * b2 * s2 * (1.0 - s2)


def _tail(h2, gt, pp, target, g_final, *, tm, name):
    n = h2.shape[0]

    def body(h2_ref, gt_ref, pp_ref, t_ref, g_ref, dh3_ref, dpp_ref, dgt_ref, dg_ref, loss_ref):
        i = pl.program_id(0)

        @pl.when(i == 0)
        def _():
            dg_ref[...] = jnp.zeros_like(dg_ref)
            loss_ref[...] = jnp.zeros_like(loss_ref)

        sg = _sigmoid(gt_ref[...])
        pp_v = pp_ref[...]
        h3 = h2_ref[...] + sg * pp_v
        r3 = _rstd(h3)
        n3 = h3 * r3
        g = g_ref[...]
        err = n3 * g - t_ref[...]
        loss_ref[...] += 0.5 * jnp.sum(jnp.sum(err * err, axis=-1, keepdims=True) / D_MODEL)
        dy = err / D_MODEL
        dg_ref[...] += jnp.sum(dy * n3, axis=0, keepdims=True)
        dn = dy * g
        dh3 = r3 * (dn - n3 * jnp.mean(dn * n3, axis=-1, keepdims=True))
        dh3_ref[...] = dh3
        dpp_ref[...] = (dh3 * sg).astype(BF16)
        dgt_ref[...] = (dh3 * pp_v * sg * (1.0 - sg)).astype(BF16)

    return pl.pallas_call(
        body, name=name, grid=(n // tm,),
        in_specs=[_rows(tm, D_MODEL)] * 4 + [_const((1, D_MODEL))],
        out_specs=[_rows(tm, D_MODEL)] * 3 + [_const((1, D_MODEL)), _const((1, 128))],
        out_shape=[_sds((n, D_MODEL), F32), _sds((n, D_MODEL), BF16), _sds((n, D_MODEL), BF16),
                   _sds((1, D_MODEL), F32), _sds((1, 128), F32)],
        compiler_params=_params("arbitrary"),
    )(h2, gt, pp, target, g_final)


def _rms_bwd(dz, h, g, dres, *, tm, name, want_bf16=True):
    n = h.shape[0]

    def body(dz_ref, h_ref, g_ref, dres_ref, dh_ref, *rest):
        if want_bf16:
            dhb_ref, dg_ref = rest
        else:
            dg_ref, = rest
        i = pl.program_id(0)

        @pl.when(i == 0)
        def _():
            dg_ref[...] = jnp.zeros_like(dg_ref)

        hv = h_ref[...]
        r = _rstd(hv)
        nrm = hv * r
        dz_v = dz_ref[...]
        dg_ref[...] += jnp.sum(dz_v * nrm, axis=0, keepdims=True)
        dn = dz_v * g_ref[...]
        dh = dres_ref[...] + r * (dn - nrm * jnp.mean(dn * nrm, axis=-1, keepdims=True))
        dh_ref[...] = dh
        if want_bf16:
            dhb_ref[...] = dh.astype(BF16)

    out_specs = [_rows(tm, D_MODEL)]
    out_shape = [_sds((n, D_MODEL), F32)]
    if want_bf16:
        out_specs.append(_rows(tm, D_MODEL))
        out_shape.append(_sds((n, D_MODEL), BF16))
    out_specs.append(_const((1, D_MODEL)))
    out_shape.append(_sds((1, D_MODEL), F32))
    return pl.pallas_call(
        body, name=name, grid=(n // tm,),
        in_specs=[_rows(tm, D_MODEL), _rows(tm, D_MODEL), _const((1, D_MODEL)), _rows(tm, D_MODEL)],
        out_specs=out_specs, out_shape=out_shape,
        compiler_params=_params("arbitrary"),
    )(dz, h, g, dres)


def _assemble_dproj(dna, ddil_q, ddil_k, ddil_v, dgn, dgd, cos_t, sin_t, *, tm, name):
    n = dgn.shape[0]

    def body(*refs):
        dq_ref, dk_ref, dv_ref = refs[0:3]
        dil_in = refs[3:12]
        dgn_ref, dgd_ref, cos_ref, sin_ref, o_ref, scr = refs[12:18]
        o_ref[:, 0:512] = dq_ref[...]
        o_ref[:, 512:1024] = dk_ref[...].astype(BF16)
        o_ref[:, 1024:1536] = dv_ref[...].astype(BF16)
        cosv, sinv = cos_ref[...], sin_ref[...]
        for t in range(3):
            for gi, dil in enumerate(DIL_DILATIONS):
                val = _load_token_order(dil_in[t * 3 + gi], scr, dil, tm)
                if t < 2:
                    val = val * cosv + _swap_halves(val * sinv)
                c0 = 1536 + t * DIL_WIDTH + gi * 256
                o_ref[:, c0:c0 + 256] = val.astype(BF16)
        o_ref[:, 3840:4864] = dgn_ref[...]
        o_ref[:, 4864:5888] = dgd_ref[...]

    in_specs = [_rows(tm, NA_WIDTH)] * 3
    for _ in range(3):
        for dil in DIL_DILATIONS:
            in_specs.append(pl.BlockSpec((dil, tm // dil, 256), lambda i: (0, i, 0)))
    in_specs += [_rows(tm, D_MODEL)] * 2 + [_rows(tm, 256)] * 2
    return pl.pallas_call(
        body, name=name, grid=(n // tm,), in_specs=in_specs,
        out_specs=_rows(tm, IN_WIDTH), out_shape=_sds((n, IN_WIDTH), BF16),
        scratch_shapes=[_dil_scratch(tm)],
        compiler_params=_params("parallel"),
    )(*dna, *ddil_q, *ddil_k, *ddil_v, dgn, dgd, cos_t, sin_t)


N_ROW_OFF = 2 * NA_WIN_ROWS - 1
N_PAIRS = N_ROW_OFF - 1
RB_WIDTH = (N_ROW_OFF + 1) * GRID_W


def _na_bias(rb_ref, pair_scr):
    shape = (GRID_W, RB_WIDTH)
    qc = lax.broadcasted_iota(jnp.int32, shape, 0)
    qc2 = lax.broadcasted_iota(jnp.int32, (GRID_W, 128), 0)
    kc2 = lax.broadcasted_iota(jnp.int32, (GRID_W, 128), 1) & (GRID_W - 1)
    cs = jnp.clip(qc2 - 8, 0, GRID_W - 16)
    valid = (kc2 >= cs) & (kc2 < cs + 16)
    for hh in range(2):
        t = jnp.broadcast_to(rb_ref[hh], shape)
        t = pltpu.roll(t, RB_WIDTH - 15, 1)
        for b in range(6):
            t = jnp.where(((qc >> b) & 1) == 1, pltpu.roll(t, 1 << b, 1), t)
        t_odd = pltpu.roll(t, RB_WIDTH - GRID_W, 1)
        for ro in range(N_PAIRS):
            src = t if ro % 2 == 0 else t_odd
            base = (ro // 2) * 128
            pair_scr[hh, ro] = jnp.where(valid, src[:, base:base + 128], NEG_INF)


NA_GROUP_FWD = 4
NA_GROUP_BWD = 4


def _stack_heads(ref, r):
    lane = lax.broadcasted_iota(jnp.int32, (GRID_W, 128), 1)
    t = ref[pl.ds(pl.multiple_of(r * GRID_W, GRID_W), GRID_W), :].astype(F32)
    return jnp.concatenate([jnp.where(lane < 64, t, 0.0), jnp.where(lane >= 64, t, 0.0)], axis=0).astype(BF16)


def _unstack_heads(t2):
    lane = lax.broadcasted_iota(jnp.int32, (GRID_W, 128), 1)
    return jnp.where(lane < 64, t2[:GRID_W], t2[GRID_W:])


def _na_window(k_ref, v_ref, r, n_rows):
    rs = jnp.clip(r - NA_WIN_ROWS // 2, 0, n_rows - NA_WIN_ROWS)
    ro0 = (NA_WIN_ROWS - 1) - (r - rs)
    off = pl.multiple_of(rs * GRID_W, GRID_W)
    kw = k_ref[pl.ds(off, NA_WIN_ROWS * GRID_W), :]
    vw = v_ref[pl.ds(off, NA_WIN_ROWS * GRID_W), :]
    return kw, vw, off, ro0


def _na_probs(s_raw, pair_scr, ro0):
    bias = [jnp.concatenate([pair_scr[hh, ro0 + 2 * j] for j in range(NA_WIN_ROWS // 2)], axis=1)
            for hh in range(2)]
    s = s_raw * QK_SCALE + jnp.concatenate(bias, axis=0)
    m = jnp.max(s, axis=-1, keepdims=True)
    e = jnp.exp(s - m)
    return e * (1.0 / jnp.sum(e, axis=-1, keepdims=True))


def _na_fwd(q, k, v, rb, *, name, comm=None):
    n = q.shape[0]
    n_rows = n // GRID_W

    def body(ins, outs, scr):
        q_ref, k_ref, v_ref, rb_ref = ins
        o_ref, = outs
        pair_scr, = scr
        _na_bias(rb_ref, pair_scr)

        def group(g, carry):
            rows = [g * NA_GROUP_FWD + t for t in range(NA_GROUP_FWD)]
            wins = [_na_window(k_ref, v_ref, r, n_rows) for r in rows]
            raw = [lax.dot_general(_stack_heads(q_ref, r), w[0], NT_DIMS, preferred_element_type=F32)
                   for r, w in zip(rows, wins)]
            probs = [_na_probs(s, pair_scr, w[3]) for s, w in zip(raw, wins)]
            outs2 = [jnp.dot(p.astype(BF16), w[1], preferred_element_type=F32) for p, w in zip(probs, wins)]
            for r, o2 in zip(rows, outs2):
                o_ref[pl.ds(pl.multiple_of(r * GRID_W, GRID_W), GRID_W), :] = _unstack_heads(o2).astype(BF16)
            return carry

        lax.fori_loop(0, n_rows // NA_GROUP_FWD, group, 0)

    col = pl.BlockSpec((n, 128), lambda h: (0, h))
    res, comm_res = _hosted_call(
        body, comm, name=name, grid=(NA_WIDTH // 128,),
        in_specs=[col, col, col, pl.BlockSpec((2, 1, RB_WIDTH), lambda h: (h, 0, 0))],
        out_specs=[col], out_shape=[_sds((n, NA_WIDTH), BF16)],
        scratch_shapes=[pltpu.VMEM((2, N_PAIRS, GRID_W, 128), F32)],
        args=(q, k, v, rb))
    return res[0] if comm is None else (res[0], comm_res)


def _na_bwd(q, k, v, do, rb, *, name, comm=None):
    n = q.shape[0]
    n_rows = n // GRID_W
    win = NA_WIN_ROWS * GRID_W

    def body(ins, outs, scr):
        q_ref, k_ref, v_ref, do_ref, rb_ref = ins
        dq_ref, dk_ref, dv_ref, drb_ref = outs
        pair_scr, acc_scr = scr
        _na_bias(rb_ref, pair_scr)
        acc_scr[...] = jnp.zeros_like(acc_scr)
        dk_ref[...] = jnp.zeros_like(dk_ref)
        dv_ref[...] = jnp.zeros_like(dv_ref)

        def group(g, carry):
            rows = [g * NA_GROUP_BWD + t for t in range(NA_GROUP_BWD)]
            wins = [_na_window(k_ref, v_ref, r, n_rows) for r in rows]
            qss = [_stack_heads(q_ref, r) for r in rows]
            doss = [_stack_heads(do_ref, r) for r in rows]
            raw = [lax.dot_general(qs, w[0], NT_DIMS, preferred_element_type=F32) for qs, w in zip(qss, wins)]
            dps = [lax.dot_general(dos, w[1], NT_DIMS, preferred_element_type=F32) for dos, w in zip(doss, wins)]
            probs = [_na_probs(s, pair_scr, w[3]) for s, w in zip(raw, wins)]
            dss = [p * (dp - jnp.sum(p * dp, axis=-1, keepdims=True)) for p, dp in zip(probs, dps)]
            dsbs = [ds.astype(BF16) for ds in dss]
            dq2s = [jnp.dot(dsb, w[0], preferred_element_type=F32) for dsb, w in zip(dsbs, wins)]
            dkws = [lax.dot_general(dsb, qs, TN_DIMS, preferred_element_type=F32) for dsb, qs in zip(dsbs, qss)]
            dvws = [lax.dot_general(p.astype(BF16), dos, TN_DIMS, preferred_element_type=F32)
                    for p, dos in zip(probs, doss)]
            for t, r in enumerate(rows):
                _, _, off, ro0 = wins[t]
                for hh in range(2):
                    for j in range(NA_WIN_ROWS // 2):
                        acc_scr[hh, ro0 + 2 * j] += dss[t][hh * GRID_W:(hh + 1) * GRID_W, j * 128:(j + 1) * 128]
                dq_ref[pl.ds(pl.multiple_of(r * GRID_W, GRID_W), GRID_W), :] = (
                    _unstack_heads(dq2s[t]) * QK_SCALE).astype(BF16)
                dk_ref[pl.ds(off, win), :] += dkws[t] * QK_SCALE
                dv_ref[pl.ds(off, win), :] += dvws[t]
            return carry

        lax.fori_loop(0, n_rows // NA_GROUP_BWD, group, 0)

        qc = lax.broadcasted_iota(jnp.int32, (N_PAIRS * GRID_W, 128), 0)
        for hh in range(2):
            t = acc_scr[hh].reshape(N_PAIRS * GRID_W, 128)
            for b in range(6):
                t = jnp.where(((qc >> b) & 1) == 1, pltpu.roll(t, 128 - (1 << b), 1), t)
            t = pltpu.roll(t, 15, 1)
            drb_ref[hh] = jnp.sum(t.reshape(N_PAIRS, GRID_W, 128), axis=1)

    col = pl.BlockSpec((n, 128), lambda h: (0, h))
    res, comm_res = _hosted_call(
        body, comm, name=name, grid=(NA_WIDTH // 128,),
        in_specs=[col, col, col, col, pl.BlockSpec((2, 1, RB_WIDTH), lambda h: (h, 0, 0))],
        out_specs=[col, col, col, pl.BlockSpec((2, N_PAIRS, 128), lambda h: (h, 0, 0))],
        out_shape=[_sds((n, NA_WIDTH), BF16), _sds((n, NA_WIDTH), F32), _sds((n, NA_WIDTH), F32),
                   _sds((8, N_PAIRS, 128), F32)],
        scratch_shapes=[pltpu.VMEM((2, N_PAIRS, GRID_W, 128), F32),
                        pltpu.VMEM((2, N_PAIRS, GRID_W, 128), F32)],
        args=(q, k, v, do, rb))
    return res if comm is None else (res, comm_res)


def _rpb_table(rpb2):
    t = jnp.pad(rpb2, ((0, 0), (0, 1), (0, GRID_W - rpb2.shape[-1])))
    return t.reshape(8, 1, RB_WIDTH)


def _rpb_grad(drb, *, name):
    kdim = drb.shape[1]

    def body(x_ref, o_ref):
        kk = lax.broadcasted_iota(jnp.int32, (128, 512), 0)
        jj = lax.broadcasted_iota(jnp.int32, (128, 512), 1)
        half, co = kk >> 6, kk & 63
        acc = jnp.zeros((8, 512), F32)
        for ro in range(N_PAIRS):
            hit = ((ro + half) == (jj >> 5)) & (co == (jj & 31)) & (co < 31)
            onehot = jnp.where(hit, 1.0, 0.0).astype(F32)
            acc = acc + jnp.dot(x_ref[:, ro * 128:(ro + 1) * 128], onehot, preferred_element_type=F32,
                                precision=lax.Precision.HIGHEST)
        o_ref[...] = acc

    return pl.pallas_call(
        body, name=name, grid=(1,),
        in_specs=[_const((8, kdim))], out_specs=_const((8, 512)), out_shape=_sds((8, 512), F32),
        compiler_params=_params("arbitrary"),
    )(drb)


def _dil_blocks(length):
    qb = min(128, length)
    return qb, min(qb + 2 * DIL_RADIUS, length)


def _dil_scores(q_ref, k_ref, v_ref, i, qb, win, length):
    start = pl.multiple_of(jnp.clip(i * qb - DIL_RADIUS, 0, length - win), DIL_RADIUS)
    kw = k_ref[0, pl.ds(start, win), :]
    vw = v_ref[0, pl.ds(start, win), :]
    qv = q_ref[0].astype(F32)
    lane = lax.broadcasted_iota(jnp.int32, (qb, 256), 1)
    qs = jnp.concatenate([jnp.where((lane >> 6) == h, qv, 0.0) for h in range(4)], axis=0).astype(BF16)
    s = lax.dot_general(qs, kw, NT_DIMS, preferred_element_type=F32) * QK_SCALE
    qi = i * qb + (lax.broadcasted_iota(jnp.int32, (4 * qb, win), 0) & (qb - 1))
    kj = start + lax.broadcasted_iota(jnp.int32, (4 * qb, win), 1)
    s = jnp.where(jnp.abs(qi - kj) <= DIL_RADIUS, s, NEG_INF)
    return s, qs, kw, vw, start, lane


def _pick_heads(stacked, lane, qb):
    out = jnp.zeros((qb, 256), stacked.dtype)
    for h in range(4):
        out = jnp.where((lane >> 6) == h, stacked[h * qb:(h + 1) * qb], out)
    return out


def _stack_head_cols(t, qb):
    return jnp.concatenate([t[:, 64 * h:64 * h + 1] for h in range(4)], axis=0)


def _dil_fwd(q, k, v, *, name):
    dil, length, _ = q.shape
    qb, win = _dil_blocks(length)

    def body(q_ref, k_ref, v_ref, o_ref, lse_ref):
        i = pl.program_id(1)
        s, _, _, vw, _, lane = _dil_scores(q_ref, k_ref, v_ref, i, qb, win, length)
        m = jnp.max(s, axis=-1, keepdims=True)
        lse = m + jnp.log(jnp.sum(jnp.exp(s - m), axis=-1, keepdims=True))
        p = jnp.exp(s - lse)
        o4 = jnp.dot(p.astype(BF16), vw, preferred_element_type=F32)
        o_ref[0] = _pick_heads(o4, lane, qb)
        lse_ref[0] = _pick_heads(jnp.broadcast_to(lse, (4 * qb, 256)), lane, qb)

    seq = pl.BlockSpec((1, length, 256), lambda j, i: (j, 0, 0))
    blk = pl.BlockSpec((1, qb, 256), lambda j, i: (j, i, 0))
    return pl.pallas_call(
        body, name=name, grid=(dil, length // qb),
        in_specs=[blk, seq, seq], out_specs=[blk, blk],
        out_shape=[_sds((dil, length, 256), F32)] * 2,
        compiler_params=_params("parallel", "parallel"),
    )(q, k, v)


def _dil_bwd(q, k, v, do, lse, cc, *, name):
    dil, length, _ = q.shape
    qb, win = _dil_blocks(length)

    def body(q_ref, k_ref, v_ref, do_ref, lse_ref, cc_ref, dq_ref, dk_ref, dv_ref):
        i = pl.program_id(1)

        @pl.when(i == 0)
        def _():
            dk_ref[...] = jnp.zeros_like(dk_ref)
            dv_ref[...] = jnp.zeros_like(dv_ref)

        s, qs, kw, vw, start, lane = _dil_scores(q_ref, k_ref, v_ref, i, qb, win, length)
        p = jnp.exp(s - _stack_head_cols(lse_ref[0], qb))
        dov = do_ref[0].astype(F32)
        dos = jnp.concatenate([jnp.where((lane >> 6) == h, dov, 0.0) for h in range(4)], axis=0).astype(BF16)
        dp = lax.dot_general(dos, vw, NT_DIMS, preferred_element_type=F32)
        ds = p * (dp + _stack_head_cols(cc_ref[0], qb))
        dsb = ds.astype(BF16)
        dq4 = jnp.dot(dsb, kw, preferred_element_type=F32)
        dq_ref[0] = _pick_heads(dq4, lane, qb) * QK_SCALE
        dk_ref[0, pl.ds(start, win), :] += lax.dot_general(dsb, qs, TN_DIMS, preferred_element_type=F32) * QK_SCALE
        dv_ref[0, pl.ds(start, win), :] += lax.dot_general(p.astype(BF16), dos, TN_DIMS, preferred_element_type=F32)

    seq = pl.BlockSpec((1, length, 256), lambda j, i: (j, 0, 0))
    blk = pl.BlockSpec((1, qb, 256), lambda j, i: (j, i, 0))
    return pl.pallas_call(
        body, name=name, grid=(dil, length // qb),
        in_specs=[blk, seq, seq, blk, blk, blk], out_specs=[blk, seq, seq],
        out_shape=[_sds((dil, length, 256), F32)] * 3,
        compiler_params=_params("parallel", "arbitrary"),
    )(q, k, v, do, lse, cc)


def _merge_weights(lses):
    m = jnp.maximum(jnp.maximum(lses[0], lses[1]), lses[2])
    es = [jnp.exp(t - m) for t in lses]
    inv = 1.0 / (es[0] + es[1] + es[2])
    return [e * inv for e in es]


def _dil_merge(outs, lses, *, tm, name):
    n = outs[0].shape[1]

    def body(*refs):
        o_in, l_in = refs[0:3], refs[3:6]
        y_ref, yb_ref, scr = refs[6:9]
        lv = [_load_token_order(l_in[g], scr, d, tm) for g, d in enumerate(DIL_DILATIONS)]
        ws = _merge_weights(lv)
        y = jnp.zeros((tm, 256), F32)
        for g, d in enumerate(DIL_DILATIONS):
            y = y + ws[g] * _load_token_order(o_in[g], scr, d, tm)
        y_ref[...] = y
        yb_ref[...] = y.astype(BF16)

    specs = [_dil_spec(d, tm) for d in DIL_DILATIONS]
    return pl.pallas_call(
        body, name=name, grid=(n // tm,), in_specs=specs + specs,
        out_specs=[_rows(tm, 256)] * 2, out_shape=[_sds((n, 256), F32), _sds((n, 256), BF16)],
        scratch_shapes=[_dil_scratch(tm)],
        compiler_params=_params("parallel"),
    )(*outs, *lses)


def _dil_merge_bwd(dy, y, lses, *, tm, name):
    n = dy.shape[0]

    def body(*refs):
        dy_ref, y_ref = refs[0:2]
        l_in = refs[2:5]
        do_out, cc_out = refs[5:8], refs[8:11]
        scr = refs[11]
        lv = [_load_token_order(l_in[g], scr, d, tm) for g, d in enumerate(DIL_DILATIONS)]
        ws = _merge_weights(lv)
        dyv = dy_ref[...]
        rr = lax.broadcasted_iota(jnp.int32, (256, 256), 0) >> 6
        cc = lax.broadcasted_iota(jnp.int32, (256, 256), 1) >> 6
        ones = jnp.where(rr == cc, 1.0, 0.0).astype(F32)
        tsum = jnp.dot(dyv * y_ref[...], ones, preferred_element_type=F32,
                       precision=lax.Precision.HIGHEST)
        for g, d in enumerate(DIL_DILATIONS):
            _store_dil_order(ws[g] * dyv, do_out[g], scr, d, tm)
            _store_dil_order(-ws[g] * tsum, cc_out[g], scr, d, tm)

    specs = [_dil_spec(d, tm) for d in DIL_DILATIONS]
    res = pl.pallas_call(
        body, name=name, grid=(n // tm,),
        in_specs=[_rows(tm, 256)] * 2 + specs,
        out_specs=specs + specs,
        out_shape=[_sds((d, n // d, 256), BF16) for d in DIL_DILATIONS]
                  + [_sds((d, n // d, 256), F32) for d in DIL_DILATIONS],
        scratch_shapes=[_dil_scratch(tm)],
        compiler_params=_params("parallel"),
    )(dy, y, *lses)
    return res[0:3], res[3:6]


_WEIGHTS = (("w_in", 1, 736), ("w_branch_na", 1, 128), ("w_branch_dil", 1, 128), ("w_out", 0, 128),
            ("w_up", 1, 512), ("w_down", 0, 512), ("w_ple_gate", 0, 128), ("w_ple_proj", 1, 128))
_W_IN, _W_BNA, _W_BD, _W_OUT, _W_UP, _W_DOWN, _W_PG, _W_PP = range(8)
_GATHER_EARLY = (_W_BNA, _W_BD, _W_OUT, _W_PG, _W_PP)
_GATHER_LATE = (_W_UP, _W_DOWN)


def _to_full(widx, gathered):
    if _WEIGHTS[widx][1] == 0:
        return gathered.reshape(-1, gathered.shape[2])
    return jnp.transpose(gathered, (1, 0, 2)).reshape(gathered.shape[1], -1)


def _to_chunks(widx, mat):
    _, axis, width = _WEIGHTS[widx]
    if axis == 0:
        return mat.reshape(N_DEV, width, mat.shape[1])
    return jnp.transpose(mat.reshape(mat.shape[0], N_DEV, width), (1, 0, 2))


def _local_step(x, p_bf16, positions, target, g_mix, g_mlp, g_ple, g_final, rpb2, get_w_in, get_rest, send_grads):
    tm = 256
    half = HEAD_DIM // 2
    inv_freq = 10000.0 ** (-jnp.arange(half, dtype=F32) / half)
    ang = positions.astype(F32)[:, None] * inv_freq
    cos, sin = jnp.cos(ang), jnp.sin(ang)
    cos_t = jnp.tile(jnp.concatenate([cos, cos], axis=-1), (1, 4))
    sin_t = jnp.tile(jnp.concatenate([-sin, sin], axis=-1), (1, 4))
    rb = _rpb_table(rpb2)

    a = _rms_fwd(x, g_mix, tm=tm, name="rms_mix")
    w_in, token = get_w_in(a)
    proj = _matmul(a, w_in, out_dtype=F32, tm=512, tn=2944, tk=1024, name="mm_in", after=token)
    na_qkv, dq_g, dk_g, dv_g, sn, sd = _split_proj(proj, cos_t, sin_t, tm=tm, name="split_proj")
    y_na = _na_fwd(*na_qkv, rb, name="na_fwd")
    d_out, d_lse = [], []
    for g in range(3):
        o, lse = _dil_fwd(dq_g[g], dk_g[g], dv_g[g], name=f"dil_fwd{g}")
        d_out.append(o)
        d_lse.append(lse)
    y_dil, y_dil_b = _dil_merge(d_out, d_lse, tm=tm, name="dil_merge")
    w_bna, w_bd, w_out, w_up, w_down, w_pg, w_pp = get_rest(y_dil_b)
    bn = _matmul(y_na, w_bna, out_dtype=F32, tm=512, tn=1024, tk=512, name="mm_bna")
    bd, mixed = _matmul(y_dil_b, w_bd, out_dtype=(F32, BF16), tm=512, tn=1024, tk=256, name="mm_bd",
                        extra=(sn, bn, sd), epilogue=lambda acc, s1, b1, s2: (acc, s1 * b1 + s2 * acc))
    h1, c = _matmul(mixed, w_out, out_dtype=(F32, BF16), tm=512, tn=1024, tk=1024, name="mm_out",
                    extra=(x, g_mlp), epilogue=_residual_rms_tile)
    u, f = _matmul(c, w_up, out_dtype=(F32, BF16), tm=512, tn=2048, tk=1024, name="mm_up",
                   epilogue=lambda acc: (acc, jnp.square(jnp.maximum(acc, 0.0))))
    h2, e = _matmul(f, w_down, out_dtype=(F32, BF16), tm=512, tn=1024, tk=2048, name="mm_down",
                    extra=(h1, g_ple), epilogue=_residual_rms_tile)
    gt = _matmul(e, w_pg, out_dtype=F32, tm=512, tn=1024, tk=1024, name="mm_pg")
    pp = _matmul(p_bf16, w_pp, out_dtype=F32, tm=512, tn=1024, tk=256, name="mm_pp")

    dh3, dpp, dgt, dg_final, loss = _tail(h2, gt, pp, target, g_final, tm=tm, name="tail")
    gw_pp = _matmul(p_bf16, dpp, ta=True, out_dtype=BF16, tm=256, tn=1024, tk=512, name="mm_gw_pp")
    gw_pg = _matmul(e, dgt, ta=True, out_dtype=BF16, tm=512, tn=1024, tk=512, name="mm_gw_pg")
    de = _matmul(dgt, w_pg, tb=True, out_dtype=F32, tm=512, tn=1024, tk=1024, name="mm_de")
    dh2, dh2_b, dg_ple = _rms_bwd(de, h2, g_ple, dh3, tm=tm, name="rms_bwd_ple")
    du = _matmul(dh2_b, w_down, tb=True, out_dtype=BF16, tm=512, tn=2048, tk=1024, name="mm_du",
                 extra=(u,), epilogue=lambda acc, uv: (acc * (2.0 * jnp.maximum(uv, 0.0)),))
    gw_down = _matmul(f, dh2_b, ta=True, out_dtype=BF16, tm=1024, tn=1024, tk=512, name="mm_gw_down")
    token = send_grads((_W_PP, _W_PG, _W_DOWN), (gw_pp, gw_pg, gw_down))
    gw_up = _matmul(c, du, ta=True, out_dtype=BF16, tm=512, tn=2048, tk=512, name="mm_gw_up", after=token)
    token = send_grads((_W_UP,), (gw_up,))
    dc = _matmul(du, w_up, tb=True, out_dtype=F32, tm=512, tn=1024, tk=2048, name="mm_dc", after=token)
    dh1, dh1_b, dg_mlp = _rms_bwd(dc, h1, g_mlp, dh2, tm=tm, name="rms_bwd_mlp")
    dbn, dbd, dgn, dgd = _matmul(dh1_b, w_out, tb=True, out_dtype=(BF16,) * 4, tm=512, tn=1024, tk=1024,
                                 name="mm_dmixed", extra=(sn, bn, sd, bd), epilogue=_gate_bwd_tile)
    gw_out = _matmul(mixed, dh1_b, ta=True, out_dtype=BF16, tm=512, tn=1024, tk=512, name="mm_gw_out")
    gw_bna = _matmul(y_na, dbn, ta=True, out_dtype=BF16, tm=512, tn=1024, tk=512, name="mm_gw_bna")
    dy_na = _matmul(dbn, w_bna, tb=True, out_dtype=BF16, tm=512, tn=512, tk=1024, name="mm_dy_na")
    gw_bd = _matmul(y_dil_b, dbd, ta=True, out_dtype=BF16, tm=256, tn=1024, tk=512, name="mm_gw_bd")
    token = send_grads((_W_OUT, _W_BNA, _W_BD), (gw_out, gw_bna, gw_bd))
    dy_dil = _matmul(dbd, w_bd, tb=True, out_dtype=F32, tm=512, tn=256, tk=1024, name="mm_dy_dil", after=token)
    dna = _na_bwd(*na_qkv, dy_na, rb, name="na_bwd")
    drpb = _rpb_grad(dna[3].reshape(8, -1), name="rpb_grad")
    do_g, cc_g = _dil_merge_bwd(dy_dil, y_dil, d_lse, tm=tm, name="dil_merge_bwd")
    ddq, ddk, ddv = [], [], []
    for g in range(3):
        r = _dil_bwd(dq_g[g], dk_g[g], dv_g[g], do_g[g], d_lse[g], cc_g[g], name=f"dil_bwd{g}")
        ddq.append(r[0])
        ddk.append(r[1])
        ddv.append(r[2])
    dproj = _assemble_dproj(dna[0:3], ddq, ddk, ddv, dgn, dgd, cos_t, sin_t, tm=tm, name="assemble_dproj")
    gw_in = _matmul(a, dproj, ta=True, out_dtype=BF16, tm=512, tn=2944, tk=512, name="mm_gw_in")
    token = send_grads((_W_IN,), (gw_in,))
    da = _matmul(dproj, w_in, tb=True, out_dtype=F32, tm=512, tn=1024, tk=2944, name="mm_da", after=token)
    dx, dg_mix = _rms_bwd(da, x, g_mix, dh1, tm=tm, name="rms_bwd_mix", want_bf16=False)
    return loss, dx, (dg_mix, dg_mlp, dg_ple, dg_final), drpb


def _cast_bf16(t, *, name):
    def body(t_ref, o_ref):
        o_ref[...] = t_ref[...].astype(BF16)

    rows, cols = t.shape
    tr = min(256, rows)
    blk = pl.BlockSpec((tr, cols), lambda i: (i, 0))
    return pl.pallas_call(body, name=name, grid=(rows // tr,), in_specs=[blk], out_specs=blk,
                          out_shape=_sds(t.shape, BF16), compiler_params=_params("parallel"))(t)


def _adamw(w, g, m, v):
    m = ADAM_B1 * m + (1.0 - ADAM_B1) * g
    v = ADAM_B2 * v + (1.0 - ADAM_B2) * (g * g)
    m_hat = m / (1.0 - ADAM_B1 ** ADAM_STEP)
    v_hat = v / (1.0 - ADAM_B2 ** ADAM_STEP)
    delta = -ADAM_LR * (m_hat / (jnp.sqrt(v_hat) + ADAM_EPS) + ADAM_WD * w)
    return delta, m, v


def _sum_adamw(parts, w, m, v, *, tr, name, own=None):
    rows, cols = w.shape

    def body(*refs):
        p_ref, w_ref, m_ref, v_ref = refs[:4]
        g_ref, d_ref, nm_ref, nv_ref = refs[-4:]
        g = (p_ref[0] if own is None else refs[4][...]).astype(F32)
        for s in range(1, N_DEV):
            g = g + p_ref[s].astype(F32)
        g_ref[...] = g
        d_ref[...], nm_ref[...], nv_ref[...] = _adamw(w_ref[...], g, m_ref[...], v_ref[...])

    blk = pl.BlockSpec((tr, cols), lambda i: (i, 0))
    extra = [] if own is None else [own]
    return pl.pallas_call(
        body, name=name, grid=(rows // tr,),
        in_specs=[pl.BlockSpec((N_DEV, tr, cols), lambda i: (0, i, 0)), blk, blk, blk] + [blk] * len(extra),
        out_specs=[blk] * 4, out_shape=[_sds((rows, cols), F32)] * 4,
        compiler_params=_params("parallel"),
    )(parts, w, m, v, *extra)


_RPB_SIZE = 8 * 15 * 31


def _pack_small(g_mix, g_mlp, g_ple, g_final, rpb, loss_row):
    flat = jnp.concatenate([g_mix.reshape(-1), g_mlp.reshape(-1), g_ple.reshape(-1), g_final.reshape(-1),
                            rpb.reshape(-1), jnp.zeros((3840 - _RPB_SIZE,), F32), loss_row.reshape(-1),
                            jnp.zeros((128,), F32)])
    return flat.reshape(64, 128)


def _unpack_small(t):
    flat = t.reshape(-1)
    return (flat[0:1024].reshape(1, 1024), flat[4096:4096 + _RPB_SIZE].reshape(1, 8, 15, 31),
            flat[1024:2048].reshape(1, 1024), flat[2048:3072].reshape(1, 1024), flat[3072:4096])


def kernel(x, p, positions, g_mix, w_in, rpb, w_branch_na, w_branch_dil, w_out, g_mlp, w_up, w_down, g_ple, w_ple_gate, w_ple_proj, g_final, loss_target, m_g_mix, m_w_in, m_rpb, m_w_branch_na, m_w_branch_dil, m_w_out, m_g_mlp, m_w_up, m_w_down, m_g_ple, m_w_ple_gate, m_w_ple_proj, m_g_final, v_g_mix, v_w_in, v_rpb, v_w_branch_na, v_w_branch_dil, v_w_out, v_g_mlp, v_w_up, v_w_down, v_g_ple, v_w_ple_gate, v_w_ple_proj, v_g_final):
    sharded = dict(w_in=(w_in, m_w_in, v_w_in), w_branch_na=(w_branch_na, m_w_branch_na, v_w_branch_na),
                   w_branch_dil=(w_branch_dil, m_w_branch_dil, v_w_branch_dil), w_out=(w_out, m_w_out, v_w_out),
                   w_up=(w_up, m_w_up, v_w_up), w_down=(w_down, m_w_down, v_w_down),
                   w_ple_gate=(w_ple_gate, m_w_ple_gate, v_w_ple_gate),
                   w_ple_proj=(w_ple_proj, m_w_ple_proj, v_w_ple_proj))
    shards = {k: tuple(t[0] for t in val) for k, val in sharded.items()}

    me = _my_index()

    w_in_b = _cast_bf16(shards["w_in"][0], name="cast_w_in")
    rest_b = [shards[name][0].astype(BF16) for name, _, _ in _WEIGHTS[1:]]
    gather_in, token_in = _start_copies(_gather_copies, [w_in_b], [_sds((N_DEV,) + w_in_b.shape, BF16)],
                                        name="start_gather_w_in")

    def whole(widx, landed, mine):
        return _to_full(widx, lax.dynamic_update_index_in_dim(landed, mine, me, 0))

    rest_handle = []

    def get_w_in(after):
        landed, = _wait_copies(_gather_copies, gather_in, after, name="wait_gather_w_in")
        handle, token = _start_copies(_gather_copies, rest_b, [_sds((N_DEV,) + t.shape, BF16) for t in rest_b],
                                      name="start_gather_rest", after=landed)
        rest_handle.append(handle)
        return whole(_W_IN, landed, w_in_b), token

    def get_rest(after):
        landed = _wait_copies(_gather_copies, rest_handle[0], after, name="wait_gather_rest")
        return [whole(i + 1, t, mine) for i, (t, mine) in enumerate(zip(landed, rest_b))]

    sent = []

    def send_grads(indices, grads):
        chunked = [_to_chunks(i, g) for i, g in zip(indices, grads)]
        handle, token = _start_copies(_exchange_copies, chunked, [_sds(t.shape, BF16) for t in chunked],
                                      name="start_exchange_" + "_".join(_WEIGHTS[i][0] for i in indices))
        sent.append((indices, chunked, handle))
        return token

    g_mix_0 = g_mix + token_in[0:1, 0:1]
    loss, dx, dgs, drpb = _local_step(
        x[0], p[0, 0].astype(BF16), positions[0], loss_target[0],
        g_mix_0, g_mlp, g_ple, g_final.reshape(1, -1), rpb[0], get_w_in, get_rest, send_grads)

    drpb3 = drpb.reshape(8, 16, 32)[:, :15, :31]
    small = _pack_small(dgs[0], dgs[1], dgs[2], dgs[3], drpb3, loss)
    share, done = _start_copies(_gather_copies, [small], [_sds((N_DEV,) + small.shape, F32)],
                                name="start_share_small")

    out = {}
    for indices, chunked, handle in sent:
        landed = _wait_copies(_exchange_copies, handle, done,
                              name="wait_exchange_" + "_".join(_WEIGHTS[i][0] for i in indices))
        for i, part, mine in zip(indices, landed, chunked):
            name = _WEIGHTS[i][0]
            w, m, v = shards[name]
            own = lax.dynamic_index_in_dim(mine, me, 0, keepdims=False)
            res = _sum_adamw(part, w, m, v, tr=min(128, w.shape[0]), name="adamw_" + name, own=own)
            out[name] = [t[None] for t in res]
            done = res[0]
    small_landed, = _wait_copies(_gather_copies, share, done, name="wait_share_small")
    small_all = lax.dynamic_update_index_in_dim(small_landed, small, me, 0)
    small_w = _pack_small(g_mix, g_mlp, g_ple, g_final, rpb, jnp.zeros((128,), F32))
    small_m = _pack_small(m_g_mix, m_g_mlp, m_g_ple, m_g_final, m_rpb, jnp.zeros((128,), F32))
    small_v = _pack_small(v_g_mix, v_g_mlp, v_g_ple, v_g_final, v_rpb, jnp.zeros((128,), F32))
    res = _sum_adamw(small_all, small_w, small_m, small_v, tr=64, name="adamw_small")
    unpacked = [_unpack_small(t) for t in res]
    for i, name in enumerate(("g_mix", "rpb", "g_mlp", "g_ple", "g_final")):
        out[name] = [u[i] for u in unpacked]
    loss_total = res[0][62, 0]

    order = ("g_mix", "w_in", "rpb", "w_branch_na", "w_branch_dil", "w_out", "g_mlp", "w_up", "w_down",
             "g_ple", "w_ple_gate", "w_ple_proj", "g_final")
    grads = [out[k][0] for k in order]
    deltas = [out[k][1] for k in order]
    new_m = [out[k][2] for k in order]
    new_v = [out[k][3] for k in order]
    return (loss_total, dx[None], *grads, *deltas, *new_m, *new_v)
```

```python
import jax
import jax.numpy as jnp
from jax import lax
from jax.experimental import pallas as pl
from jax.experimental.pallas import tpu as pltpu

F32 = jnp.float32
BF16 = jnp.bfloat16

D_MODEL = 1024
HEAD_DIM = 64
GRID_W = 64
NA_WIDTH = 512
DIL_WIDTH = 768
IN_WIDTH = 5888
DIL_DILATIONS = (1, 4, 16)
DIL_RADIUS = 64
NA_WIN_ROWS = 8
RMS_EPS = 1e-6
NEG_INF = -1e30
QK_SCALE = HEAD_DIM ** -0.5

ADAM_LR = 0.001
ADAM_B1 = 0.9
ADAM_B2 = 0.999
ADAM_EPS = 1e-08
ADAM_WD = 0.01
ADAM_STEP = 10

N_DEV = 8
VMEM_LIMIT = 56 * 1024 * 1024
MESH = pl.DeviceIdType.MESH

NT_DIMS = (((1,), (1,)), ((), ()))
TN_DIMS = (((0,), (0,)), ((), ()))


def _sds(shape, dtype):
    return jax.ShapeDtypeStruct(shape, dtype)


def _params(*sem):
    return pltpu.CompilerParams(dimension_semantics=sem, vmem_limit_bytes=VMEM_LIMIT)


def _rows(tm, width, col=0):
    return pl.BlockSpec((tm, width), lambda i, c=col: (i, c))


def _const(shape):
    zeros = (0,) * len(shape)
    return pl.BlockSpec(shape, lambda i: zeros)


def _my_index():
    return 4 * lax.axis_index("x") + 2 * lax.axis_index("y") + lax.axis_index("c")


def _peer(k):
    x, y, c = lax.axis_index("x"), lax.axis_index("y"), lax.axis_index("c")
    px = 1 - x if k & 4 else x
    py = 1 - y if k & 2 else y
    pc = 1 - c if k & 1 else c
    return (px, py, pc), 4 * px + 2 * py + pc


def _call(body, *, name, grid, in_specs, out_specs, out_shape, scratch_shapes, args, after=None):
    n_in, n_out = len(in_specs), len(out_specs)
    extra = [] if after is None else [after]
    n_x = n_in + len(extra)

    def plain(*refs):
        body(refs[:n_in], refs[n_x:n_x + n_out], refs[n_x + n_out:])

    res = pl.pallas_call(plain, name=name, grid=grid,
                         in_specs=list(in_specs) + [pl.BlockSpec(memory_space=pl.ANY)] * len(extra),
                         out_specs=out_specs, out_shape=out_shape, scratch_shapes=scratch_shapes,
                         compiler_params=_params(*(("arbitrary",) * len(grid))))(*args, *extra)
    return list(res)


_HBM_SPEC = pl.BlockSpec(memory_space=pltpu.HBM)
_SEM_SPEC = pl.BlockSpec(memory_space=pltpu.SEMAPHORE)
_SIDE_EFFECT = pltpu.SideEffectType.DATAFLOW_SIDE_EFFECTING


def _gather_copies(srcs, lands, send, recv, sending):
    me = _my_index()
    out = []
    for w in range(len(srcs)):
        for k in range(1, N_DEV):
            dev, idx = _peer(k)
            out.append(pltpu.make_async_remote_copy(
                src_ref=srcs[w], dst_ref=lands[w].at[me if sending else idx],
                send_sem=send.at[w * 7 + k - 1], recv_sem=recv.at[w * 7 + k - 1],
                device_id=dev, device_id_type=MESH))
    return out


def _exchange_copies(srcs, lands, send, recv, sending):
    out = []
    for w in range(len(srcs)):
        for k in range(1, N_DEV):
            dev, idx = _peer(k)
            out.append(pltpu.make_async_remote_copy(
                src_ref=srcs[w].at[idx], dst_ref=lands[w].at[k],
                send_sem=send.at[w * 7 + k - 1], recv_sem=recv.at[w * 7 + k - 1],
                device_id=dev, device_id_type=MESH))
    return out


def _start_copies(make, srcs, land_shapes, *, name, after=None):
    n = len(srcs)
    extra = [] if after is None else [after]

    def body(*refs):
        src_refs, land_refs = refs[:n], refs[n:2 * n]
        send, recv = refs[2 * n + len(extra)], refs[2 * n + len(extra) + 1]
        token = refs[-1]
        for cp in make(src_refs, land_refs, send, recv, True):
            cp.start()
        token[...] = jnp.zeros_like(token)

    lands = [pltpu.with_memory_space_constraint(lax.empty(s.shape, s.dtype), pltpu.HBM) for s in land_shapes]
    res = pl.pallas_call(
        body, name=name,
        out_shape=(pltpu.SemaphoreType.DMA((7 * n,)), pltpu.SemaphoreType.DMA((7 * n,)),
                   *[pltpu.HBM(s.shape, s.dtype) for s in srcs],
                   *[pltpu.HBM(s.shape, s.dtype) for s in land_shapes],
                   _sds((8, 128), F32)),
        in_specs=[_HBM_SPEC] * (2 * n) + [pl.BlockSpec(memory_space=pl.ANY)] * len(extra),
        out_specs=(_SEM_SPEC, _SEM_SPEC, *([_HBM_SPEC] * (2 * n)), pl.BlockSpec(memory_space=pltpu.VMEM)),
        input_output_aliases={i: 2 + i for i in range(2 * n)},
        compiler_params=pltpu.CompilerParams(has_side_effects=_SIDE_EFFECT),
    )(*[pltpu.with_memory_space_constraint(s, pltpu.HBM) for s in srcs], *lands, *extra)
    return (n, res[0], res[1], res[2:2 + n], res[2 + n:2 + 2 * n]), res[-1]


def _wait_copies(make, handle, after, *, name):
    n, send_sems, recv_sems, srcs, lands = handle

    def body(*refs):
        src_refs, land_refs = refs[:n], refs[n:2 * n]
        send, recv = refs[2 * n], refs[2 * n + 1]
        for cp in make(src_refs, land_refs, send, recv, False):
            cp.wait_send()
            cp.wait_recv()

    res = pl.pallas_call(
        body, name=name,
        out_shape=tuple(pltpu.HBM(s.shape, s.dtype) for s in (*srcs, *lands)),
        in_specs=[_HBM_SPEC] * (2 * n) + [_SEM_SPEC, _SEM_SPEC, pl.BlockSpec(memory_space=pl.ANY)],
        out_specs=tuple([_HBM_SPEC] * (2 * n)),
        input_output_aliases={i: i for i in range(2 * n)},
        compiler_params=pltpu.CompilerParams(has_side_effects=_SIDE_EFFECT),
    )(*srcs, *lands, send_sems, recv_sems, after)
    return list(res[n:])


def _matmul(a, b, *, ta=False, tb=False, out_dtype, tm, tn, tk, name, after=None, extra=(), epilogue=None,
            n_colsum=0):
    m, k = (a.shape[1], a.shape[0]) if ta else a.shape
    n = b.shape[0] if tb else b.shape[1]
    tm, tn, tk = min(tm, m), min(tn, n), min(tk, k)
    nk = k // tk
    dims = (((0 if ta else 1,), (1 if tb else 0,)), ((), ()))
    out_dtypes = out_dtype if isinstance(out_dtype, tuple) else (out_dtype,)
    n_tiles = len(out_dtypes)

    def finish(acc, x_refs, o_refs):
        vals = (acc,) if epilogue is None else epilogue(acc, *[r[...] for r in x_refs])
        for o_ref, val in zip(o_refs[:n_tiles], vals[:n_tiles]):
            o_ref[...] = val.astype(o_ref.dtype)
        i = pl.program_id(1)
        for s_ref, val in zip(o_refs[n_tiles:], vals[n_tiles:]):
            @pl.when(i == 0)
            def _(s_ref=s_ref, val=val):
                s_ref[...] = val

            @pl.when(i > 0)
            def _(s_ref=s_ref, val=val):
                s_ref[...] += val

    def body(ins, outs, acc):
        a_ref, b_ref = ins[:2]
        part = lax.dot_general(a_ref[...], b_ref[...], dims, preferred_element_type=F32)
        if nk == 1:
            finish(part, ins[2:], outs)
            return
        acc_ref, = acc
        kk = pl.program_id(2)

        @pl.when(kk == 0)
        def _():
            acc_ref[...] = part

        @pl.when(kk > 0)
        def _():
            acc_ref[...] += part

        @pl.when(kk == nk - 1)
        def _():
            finish(acc_ref[...], ins[2:], outs)

    a_spec = (pl.BlockSpec((tk, tm), lambda j, i, kk: (kk, i)) if ta
              else pl.BlockSpec((tm, tk), lambda j, i, kk: (i, kk)))
    b_spec = (pl.BlockSpec((tn, tk), lambda j, i, kk: (j, kk)) if tb
              else pl.BlockSpec((tk, tn), lambda j, i, kk: (kk, j)))
    tile = pl.BlockSpec((tm, tn), lambda j, i, kk: (i, j))
    row = pl.BlockSpec((1, tn), lambda j, i, kk: (0, j))
    res = _call(
        body, name=name, grid=(n // tn, m // tm, nk),
        in_specs=[a_spec, b_spec] + [row if t.shape[0] == 1 else tile for t in extra],
        out_specs=[tile] * n_tiles + [row] * n_colsum,
        out_shape=[_sds((m, n), dt) for dt in out_dtypes] + [_sds((1, n), F32)] * n_colsum,
        scratch_shapes=[] if nk == 1 else [pltpu.VMEM((tm, tn), F32)],
        args=(a, b, *extra), after=after)
    return res if isinstance(out_dtype, tuple) or n_colsum else res[0]


def _rstd(h):
    return lax.rsqrt(jnp.mean(h * h, axis=-1, keepdims=True) + RMS_EPS)


def _sigmoid(z):
    return 1.0 / (1.0 + jnp.exp(-z))


def _rms_fwd(x, g, *, tm, name):
    n = x.shape[0]

    def body(x_ref, g_ref, o_ref):
        h = x_ref[...]
        o_ref[...] = (h * _rstd(h) * g_ref[...]).astype(BF16)

    return pl.pallas_call(
        body, name=name, grid=(n // tm,),
        in_specs=[_rows(tm, D_MODEL), _const((1, D_MODEL))],
        out_specs=_rows(tm, D_MODEL), out_shape=_sds((n, D_MODEL), BF16),
        compiler_params=_params("parallel"),
    )(x, g)


def _swap_halves(t):
    width = t.shape[1]
    lane = lax.broadcasted_iota(jnp.int32, t.shape, 1)
    return jnp.where((lane & 63) < 32, pltpu.roll(t, width - 32, 1), pltpu.roll(t, 32, 1))


def _dil_spec(dil, tm):
    return pl.BlockSpec((dil, tm // dil, 256), lambda i: (0, i, 0))


def _dil_scratch(tm):
    return pltpu.VMEM((2, tm, 128), F32)


def _load_token_order(src, scr, dil, tm):
    if dil == 1:
        return src[0]
    for j in range(dil):
        for c in range(2):
            scr[c, pl.ds(j, tm // dil, stride=dil), :] = src[j, :, c * 128:(c + 1) * 128]
    return jnp.concatenate([scr[0], scr[1]], axis=1)


def _store_dil_order(val, dst, scr, dil, tm):
    if dil == 1:
        dst[0] = val.astype(dst.dtype)
        return
    for c in range(2):
        scr[c] = val[:, c * 128:(c + 1) * 128]
    for j in range(dil):
        for c in range(2):
            dst[j, :, c * 128:(c + 1) * 128] = scr[c, pl.ds(j, tm // dil, stride=dil), :].astype(dst.dtype)


def _split_proj(proj, cos_t, sin_t, *, tm, name):
    n = proj.shape[0]
    n_dil = len(DIL_DILATIONS)

    def body(*refs):
        na_in = refs[0:3]
        dil_in = refs[3:3 + 3 * n_dil]
        gate_in = refs[12:20]
        cos_ref, sin_ref = refs[20:22]
        outs = refs[22:]
        na_out = outs[0:3]
        dil_out = outs[3:12]
        sn_ref, sd_ref = outs[12:14]
        scr = outs[14]
        for t in range(3):
            na_out[t][...] = na_in[t][...].astype(BF16)
        cosv, sinv = cos_ref[...], sin_ref[...]
        for t in range(3):
            for gi, dil in enumerate(DIL_DILATIONS):
                val = dil_in[t * n_dil + gi][...]
                if t < 2:
                    val = val * cosv + _swap_halves(val) * sinv
                _store_dil_order(val, dil_out[t * n_dil + gi], scr, dil, tm)
        for c in range(4):
            sn_ref[:, c * 256:(c + 1) * 256] = _sigmoid(gate_in[c][...])
            sd_ref[:, c * 256:(c + 1) * 256] = _sigmoid(gate_in[4 + c][...])

    in_specs = [_rows(tm, NA_WIDTH, c) for c in range(3)]
    in_specs += [_rows(tm, 256, 6 + c) for c in range(9)]
    in_specs += [_rows(tm, 256, 15 + c) for c in range(8)]
    in_specs += [_rows(tm, 256), _rows(tm, 256)]
    out_specs = [_rows(tm, NA_WIDTH)] * 3
    out_shape = [_sds((n, NA_WIDTH), BF16)] * 3
    for _ in range(3):
        for dil in DIL_DILATIONS:
            out_specs.append(pl.BlockSpec((dil, tm // dil, 256), lambda i: (0, i, 0)))
            out_shape.append(_sds((dil, n // dil, 256), BF16))
    out_specs += [_rows(tm, D_MODEL)] * 2
    out_shape += [_sds((n, D_MODEL), F32)] * 2
    res = pl.pallas_call(
        body, name=name, grid=(n // tm,),
        in_specs=in_specs, out_specs=out_specs, out_shape=out_shape,
        scratch_shapes=[_dil_scratch(tm)],
        compiler_params=_params("parallel"),
    )(*([proj] * 20), cos_t, sin_t)
    return res[0:3], res[3:6], res[6:9], res[9:12], res[12], res[13]


def _residual_rms_tile(delta, h, g):
    hn = h + delta
    return hn, hn * _rstd(hn) * g


def _gate_bwd_tile(dm, s1, b1, s2, b2):
    return dm * s1, dm * s2, dm * b1 * s1 * (1.0 - s1), dm * b2 * s2 * (1.0 - s2)


def _tail_tile(gt, pp, h2, target, g):
    sg = _sigmoid(gt)
    h3 = h2 + sg * pp
    r3 = _rstd(h3)
    n3 = h3 * r3
    err = n3 * g - target
    loss = 0.5 * jnp.sum(jnp.sum(err * err, axis=-1, keepdims=True) / D_MODEL)
    dy = err / D_MODEL
    dn = dy * g
    dh3 = r3 * (dn - n3 * jnp.mean(dn * n3, axis=-1, keepdims=True))
    return (dh3, dh3 * sg, dh3 * pp * sg * (1.0 - sg),
            jnp.sum(dy * n3, axis=0, keepdims=True), jnp.full((1, gt.shape[1]), loss, F32))


def _rms_bwd_tile(dz, h, g, dres):
    r = _rstd(h)
    nrm = h * r
    dn = dz * g
    dh = dres + r * (dn - nrm * jnp.mean(dn * nrm, axis=-1, keepdims=True))
    return dh, jnp.sum(dz * nrm, axis=0, keepdims=True)


def _rms_bwd_twice(dz, h, g, dres):
    dh, dg = _rms_bwd_tile(dz, h, g, dres)
    return dh, dh, dg


def _assemble_dproj(dna, ddil_q, ddil_k, ddil_v, dgn, dgd, cos_t, sin_t, *, tm, name):
    n = dgn.shape[0]

    def body(*refs):
        dq_ref, dk_ref, dv_ref = refs[0:3]
        dil_in = refs[3:12]
        dgn_ref, dgd_ref, cos_ref, sin_ref, o_ref, scr = refs[12:18]
        o_ref[:, 0:512] = dq_ref[...]
        o_ref[:, 512:1024] = dk_ref[...].astype(BF16)
        o_ref[:, 1024:1536] = dv_ref[...].astype(BF16)
        cosv, sinv = cos_ref[...], sin_ref[...]
        for t in range(3):
            for gi, dil in enumerate(DIL_DILATIONS):
                val = _load_token_order(dil_in[t * 3 + gi], scr, dil, tm)
                if t < 2:
                    val = val * cosv + _swap_halves(val * sinv)
                c0 = 1536 + t * DIL_WIDTH + gi * 256
                o_ref[:, c0:c0 + 256] = val.astype(BF16)
        o_ref[:, 3840:4864] = dgn_ref[...]
        o_ref[:, 4864:5888] = dgd_ref[...]

    in_specs = [_rows(tm, NA_WIDTH)] * 3
    for _ in range(3):
        for dil in DIL_DILATIONS:
            in_specs.append(pl.BlockSpec((dil, tm // dil, 256), lambda i: (0, i, 0)))
    in_specs += [_rows(tm, D_MODEL)] * 2 + [_rows(tm, 256)] * 2
    return pl.pallas_call(
        body, name=name, grid=(n // tm,), in_specs=in_specs,
        out_specs=_rows(tm, IN_WIDTH), out_shape=_sds((n, IN_WIDTH), BF16),
        scratch_shapes=[_dil_scratch(tm)],
        compiler_params=_params("parallel"),
    )(*dna, *ddil_q, *ddil_k, *ddil_v, dgn, dgd, cos_t, sin_t)


N_ROW_OFF = 2 * NA_WIN_ROWS - 1
N_PAIRS = N_ROW_OFF - 1
RB_WIDTH = (N_ROW_OFF + 1) * GRID_W


def _na_bias(rb_ref, pair_scr):
    shape = (GRID_W, RB_WIDTH)
    qc = lax.broadcasted_iota(jnp.int32, shape, 0)
    qc2 = lax.broadcasted_iota(jnp.int32, (GRID_W, 128), 0)
    kc2 = lax.broadcasted_iota(jnp.int32, (GRID_W, 128), 1) & (GRID_W - 1)
    cs = jnp.clip(qc2 - 8, 0, GRID_W - 16)
    valid = (kc2 >= cs) & (kc2 < cs + 16)
    for hh in range(2):
        t = jnp.broadcast_to(rb_ref[hh], shape)
        t = pltpu.roll(t, RB_WIDTH - 15, 1)
        for b in range(6):
            t = jnp.where(((qc >> b) & 1) == 1, pltpu.roll(t, 1 << b, 1), t)
        t_odd = pltpu.roll(t, RB_WIDTH - GRID_W, 1)
        for ro in range(N_PAIRS):
            src = t if ro % 2 == 0 else t_odd
            base = (ro // 2) * 128
            pair_scr[hh, ro] = jnp.where(valid, src[:, base:base + 128], NEG_INF)


NA_GROUP_FWD = 4
NA_GROUP_BWD = 4


def _stack_heads(ref, r, scale=1.0):
    lane = lax.broadcasted_iota(jnp.int32, (GRID_W, 128), 1)
    t = ref[pl.ds(pl.multiple_of(r * GRID_W, GRID_W), GRID_W), :].astype(F32) * scale
    return jnp.concatenate([jnp.where(lane < 64, t, 0.0), jnp.where(lane >= 64, t, 0.0)], axis=0).astype(BF16)


def _unstack_heads(t2):
    lane = lax.broadcasted_iota(jnp.int32, (GRID_W, 128), 1)
    return jnp.where(lane < 64, t2[:GRID_W], t2[GRID_W:])


def _na_window(k_ref, v_ref, r, n_rows):
    rs = jnp.clip(r - NA_WIN_ROWS // 2, 0, n_rows - NA_WIN_ROWS)
    ro0 = (NA_WIN_ROWS - 1) - (r - rs)
    off = pl.multiple_of(rs * GRID_W, GRID_W)
    kw = k_ref[pl.ds(off, NA_WIN_ROWS * GRID_W), :]
    vw = v_ref[pl.ds(off, NA_WIN_ROWS * GRID_W), :]
    return kw, vw, off, ro0


def _na_probs(s_raw, pair_scr, ro0):
    bias = [jnp.concatenate([pair_scr[hh, ro0 + 2 * j] for j in range(NA_WIN_ROWS // 2)], axis=1)
            for hh in range(2)]
    s = s_raw + jnp.concatenate(bias, axis=0)
    m = jnp.max(s, axis=-1, keepdims=True)
    e = jnp.exp(s - m)
    return e * (1.0 / jnp.sum(e, axis=-1, keepdims=True))


def _na_fwd(q, k, v, rb, *, name):
    n = q.shape[0]
    n_rows = n // GRID_W

    def body(ins, outs, scr):
        q_ref, k_ref, v_ref, rb_ref = ins
        o_ref, = outs
        pair_scr, = scr
        _na_bias(rb_ref, pair_scr)

        def group(g, carry):
            rows = [g * NA_GROUP_FWD + t for t in range(NA_GROUP_FWD)]
            wins = [_na_window(k_ref, v_ref, r, n_rows) for r in rows]
            raw = [lax.dot_general(_stack_heads(q_ref, r, QK_SCALE), w[0], NT_DIMS, preferred_element_type=F32)
                   for r, w in zip(rows, wins)]
            probs = [_na_probs(s, pair_scr, w[3]) for s, w in zip(raw, wins)]
            outs2 = [jnp.dot(p.astype(BF16), w[1], preferred_element_type=F32) for p, w in zip(probs, wins)]
            for r, o2 in zip(rows, outs2):
                o_ref[pl.ds(pl.multiple_of(r * GRID_W, GRID_W), GRID_W), :] = _unstack_heads(o2).astype(BF16)
            return carry

        lax.fori_loop(0, n_rows // NA_GROUP_FWD, group, 0)

    col = pl.BlockSpec((n, 128), lambda h: (0, h))
    return _call(
        body, name=name, grid=(NA_WIDTH // 128,),
        in_specs=[col, col, col, pl.BlockSpec((2, 1, RB_WIDTH), lambda h: (h, 0, 0))],
        out_specs=[col], out_shape=[_sds((n, NA_WIDTH), BF16)],
        scratch_shapes=[pltpu.VMEM((2, N_PAIRS, GRID_W, 128), F32)],
        args=(q, k, v, rb))[0]


def _na_bwd(q, k, v, do, rb, *, name):
    n = q.shape[0]
    n_rows = n // GRID_W
    win = NA_WIN_ROWS * GRID_W

    def body(ins, outs, scr):
        q_ref, k_ref, v_ref, do_ref, rb_ref = ins
        dq_ref, dk_ref, dv_ref, drb_ref = outs
        pair_scr, acc_scr = scr
        _na_bias(rb_ref, pair_scr)
        acc_scr[...] = jnp.zeros_like(acc_scr)
        dk_ref[...] = jnp.zeros_like(dk_ref)
        dv_ref[...] = jnp.zeros_like(dv_ref)

        def group(g, carry):
            rows = [g * NA_GROUP_BWD + t for t in range(NA_GROUP_BWD)]
            wins = [_na_window(k_ref, v_ref, r, n_rows) for r in rows]
            qss = [_stack_heads(q_ref, r, QK_SCALE) for r in rows]
            doss = [_stack_heads(do_ref, r) for r in rows]
            raw = [lax.dot_general(qs, w[0], NT_DIMS, preferred_element_type=F32) for qs, w in zip(qss, wins)]
            dps = [lax.dot_general(dos, w[1], NT_DIMS, preferred_element_type=F32) for dos, w in zip(doss, wins)]
            probs = [_na_probs(s, pair_scr, w[3]) for s, w in zip(raw, wins)]
            dss = [p * (dp - jnp.sum(p * dp, axis=-1, keepdims=True)) for p, dp in zip(probs, dps)]
            dsbs = [ds.astype(BF16) for ds in dss]
            dq2s = [jnp.dot(dsb, w[0], preferred_element_type=F32) for dsb, w in zip(dsbs, wins)]
            dkws = [lax.dot_general(dsb, qs, TN_DIMS, preferred_element_type=F32) for dsb, qs in zip(dsbs, qss)]
            dvws = [lax.dot_general(p.astype(BF16), dos, TN_DIMS, preferred_element_type=F32)
                    for p, dos in zip(probs, doss)]
            for t, r in enumerate(rows):
                _, _, off, ro0 = wins[t]
                for hh in range(2):
                    for j in range(NA_WIN_ROWS // 2):
                        acc_scr[hh, ro0 + 2 * j] += dss[t][hh * GRID_W:(hh + 1) * GRID_W, j * 128:(j + 1) * 128]
                dq_ref[pl.ds(pl.multiple_of(r * GRID_W, GRID_W), GRID_W), :] = (
                    _unstack_heads(dq2s[t]) * QK_SCALE).astype(BF16)
                dk_ref[pl.ds(off, win), :] += dkws[t]
                dv_ref[pl.ds(off, win), :] += dvws[t]
            return carry

        lax.fori_loop(0, n_rows // NA_GROUP_BWD, group, 0)

        qc = lax.broadcasted_iota(jnp.int32, (N_PAIRS * GRID_W, 128), 0)
        for hh in range(2):
            t = acc_scr[hh].reshape(N_PAIRS * GRID_W, 128)
            for b in range(6):
                t = jnp.where(((qc >> b) & 1) == 1, pltpu.roll(t, 128 - (1 << b), 1), t)
            t = pltpu.roll(t, 15, 1)
            drb_ref[hh] = jnp.sum(t.reshape(N_PAIRS, GRID_W, 128), axis=1)

    col = pl.BlockSpec((n, 128), lambda h: (0, h))
    return _call(
        body, name=name, grid=(NA_WIDTH // 128,),
        in_specs=[col, col, col, col, pl.BlockSpec((2, 1, RB_WIDTH), lambda h: (h, 0, 0))],
        out_specs=[col, col, col, pl.BlockSpec((2, N_PAIRS, 128), lambda h: (h, 0, 0))],
        out_shape=[_sds((n, NA_WIDTH), BF16), _sds((n, NA_WIDTH), F32), _sds((n, NA_WIDTH), F32),
                   _sds((8, N_PAIRS, 128), F32)],
        scratch_shapes=[pltpu.VMEM((2, N_PAIRS, GRID_W, 128), F32),
                        pltpu.VMEM((2, N_PAIRS, GRID_W, 128), F32)],
        args=(q, k, v, do, rb))


def _rpb_table(rpb2):
    t = jnp.pad(rpb2, ((0, 0), (0, 1), (0, GRID_W - rpb2.shape[-1])))
    return t.reshape(8, 1, RB_WIDTH)


def _rpb_grad(drb, *, name):
    kdim = drb.shape[1]

    def body(x_ref, o_ref):
        kk = lax.broadcasted_iota(jnp.int32, (128, 512), 0)
        jj = lax.broadcasted_iota(jnp.int32, (128, 512), 1)
        half, co = kk >> 6, kk & 63
        acc = jnp.zeros((8, 512), F32)
        for ro in range(N_PAIRS):
            hit = ((ro + half) == (jj >> 5)) & (co == (jj & 31)) & (co < 31)
            onehot = jnp.where(hit, 1.0, 0.0).astype(F32)
            acc = acc + jnp.dot(x_ref[:, ro * 128:(ro + 1) * 128], onehot, preferred_element_type=F32,
                                precision=lax.Precision.HIGHEST)
        o_ref[...] = acc

    return pl.pallas_call(
        body, name=name, grid=(1,),
        in_specs=[_const((8, kdim))], out_specs=_const((8, 512)), out_shape=_sds((8, 512), F32),
        compiler_params=_params("arbitrary"),
    )(drb)


def _dil_blocks(length):
    qb = min(128, length)
    return qb, min(qb + 2 * DIL_RADIUS, length)


def _dil_scores(q_ref, k_ref, v_ref, i, qb, win, length):
    start = pl.multiple_of(jnp.clip(i * qb - DIL_RADIUS, 0, length - win), DIL_RADIUS)
    kw = k_ref[0, pl.ds(start, win), :]
    vw = v_ref[0, pl.ds(start, win), :]
    qv = q_ref[0].astype(F32) * QK_SCALE
    lane = lax.broadcasted_iota(jnp.int32, (qb, 256), 1)
    qs = jnp.concatenate([jnp.where((lane >> 6) == h, qv, 0.0) for h in range(4)], axis=0).astype(BF16)
    s = lax.dot_general(qs, kw, NT_DIMS, preferred_element_type=F32)
    gap = ((lax.broadcasted_iota(jnp.int32, (4 * qb, win), 0) & (qb - 1))
           - lax.broadcasted_iota(jnp.int32, (4 * qb, win), 1)) + (i * qb - start)
    s = jnp.where(jnp.abs(gap) <= DIL_RADIUS, s, NEG_INF)
    return s, qs, kw, vw, start, lane


def _pick_heads(stacked, lane, qb):
    out = jnp.zeros((qb, 256), stacked.dtype)
    for h in range(4):
        out = jnp.where((lane >> 6) == h, stacked[h * qb:(h + 1) * qb], out)
    return out


def _stack_head_cols(t, qb):
    return jnp.concatenate([t[:, 64 * h:64 * h + 1] for h in range(4)], axis=0)


def _dil_fwd(q, k, v, *, name):
    dil, length, _ = q.shape
    qb, win = _dil_blocks(length)

    def body(q_ref, k_ref, v_ref, o_ref, lse_ref):
        i = pl.program_id(1)
        s, _, _, vw, _, lane = _dil_scores(q_ref, k_ref, v_ref, i, qb, win, length)
        m = jnp.max(s, axis=-1, keepdims=True)
        e = jnp.exp(s - m)
        norm = jnp.sum(e, axis=-1, keepdims=True)
        lse = m + jnp.log(norm)
        p = e * (1.0 / norm)
        o4 = jnp.dot(p.astype(BF16), vw, preferred_element_type=F32)
        o_ref[0] = _pick_heads(o4, lane, qb)
        lse_ref[0] = _pick_heads(jnp.broadcast_to(lse, (4 * qb, 256)), lane, qb)

    seq = pl.BlockSpec((1, length, 256), lambda j, i: (j, 0, 0))
    blk = pl.BlockSpec((1, qb, 256), lambda j, i: (j, i, 0))
    return pl.pallas_call(
        body, name=name, grid=(dil, length // qb),
        in_specs=[blk, seq, seq], out_specs=[blk, blk],
        out_shape=[_sds((dil, length, 256), F32)] * 2,
        compiler_params=_params("parallel", "parallel"),
    )(q, k, v)


def _dil_bwd(q, k, v, do, lse, cc, *, name):
    dil, length, _ = q.shape
    qb, win = _dil_blocks(length)

    def body(q_ref, k_ref, v_ref, do_ref, lse_ref, cc_ref, dq_ref, dk_ref, dv_ref):
        i = pl.program_id(1)

        @pl.when(i == 0)
        def _():
            dk_ref[...] = jnp.zeros_like(dk_ref)
            dv_ref[...] = jnp.zeros_like(dv_ref)

        s, qs, kw, vw, start, lane = _dil_scores(q_ref, k_ref, v_ref, i, qb, win, length)
        p = jnp.exp(s - _stack_head_cols(lse_ref[0], qb))
        dov = do_ref[0].astype(F32)
        dos = jnp.concatenate([jnp.where((lane >> 6) == h, dov, 0.0) for h in range(4)], axis=0).astype(BF16)
        dp = lax.dot_general(dos, vw, NT_DIMS, preferred_element_type=F32)
        ds = p * (dp + _stack_head_cols(cc_ref[0], qb))
        dsb = ds.astype(BF16)
        dq4 = jnp.dot(dsb, kw, preferred_element_type=F32)
        dq_ref[0] = _pick_heads(dq4, lane, qb) * QK_SCALE
        dk_ref[0, pl.ds(start, win), :] += lax.dot_general(dsb, qs, TN_DIMS, preferred_element_type=F32)
        dv_ref[0, pl.ds(start, win), :] += lax.dot_general(p.astype(BF16), dos, TN_DIMS, preferred_element_type=F32)

    seq = pl.BlockSpec((1, length, 256), lambda j, i: (j, 0, 0))
    blk = pl.BlockSpec((1, qb, 256), lambda j, i: (j, i, 0))
    return pl.pallas_call(
        body, name=name, grid=(dil, length // qb),
        in_specs=[blk, seq, seq, blk, blk, blk], out_specs=[blk, seq, seq],
        out_shape=[_sds((dil, length, 256), F32)] * 3,
        compiler_params=_params("parallel", "arbitrary"),
    )(q, k, v, do, lse, cc)


def _merge_weights(lses):
    m = jnp.maximum(jnp.maximum(lses[0], lses[1]), lses[2])
    es = [jnp.exp(t - m) for t in lses]
    inv = 1.0 / (es[0] + es[1] + es[2])
    return [e * inv for e in es]


def _dil_merge(outs, lses, *, tm, name):
    n = outs[0].shape[1]

    def body(*refs):
        o_in, l_in = refs[0:3], refs[3:6]
        y_ref, yb_ref, scr = refs[6:9]
        lv = [_load_token_order(l_in[g], scr, d, tm) for g, d in enumerate(DIL_DILATIONS)]
        ws = _merge_weights(lv)
        y = jnp.zeros((tm, 256), F32)
        for g, d in enumerate(DIL_DILATIONS):
            y = y + ws[g] * _load_token_order(o_in[g], scr, d, tm)
        y_ref[...] = y
        yb_ref[...] = y.astype(BF16)

    specs = [_dil_spec(d, tm) for d in DIL_DILATIONS]
    return pl.pallas_call(
        body, name=name, grid=(n // tm,), in_specs=specs + specs,
        out_specs=[_rows(tm, 256)] * 2, out_shape=[_sds((n, 256), F32), _sds((n, 256), BF16)],
        scratch_shapes=[_dil_scratch(tm)],
        compiler_params=_params("parallel"),
    )(*outs, *lses)


def _dil_merge_bwd(dy, y, lses, *, tm, name):
    n = dy.shape[0]

    def body(*refs):
        dy_ref, y_ref = refs[0:2]
        l_in = refs[2:5]
        do_out, cc_out = refs[5:8], refs[8:11]
        scr = refs[11]
        lv = [_load_token_order(l_in[g], scr, d, tm) for g, d in enumerate(DIL_DILATIONS)]
        ws = _merge_weights(lv)
        dyv = dy_ref[...]
        rr = lax.broadcasted_iota(jnp.int32, (256, 256), 0) >> 6
        cc = lax.broadcasted_iota(jnp.int32, (256, 256), 1) >> 6
        ones = jnp.where(rr == cc, 1.0, 0.0).astype(F32)
        tsum = jnp.dot(dyv * y_ref[...], ones, preferred_element_type=F32,
                       precision=lax.Precision.HIGHEST)
        for g, d in enumerate(DIL_DILATIONS):
            _store_dil_order(ws[g] * dyv, do_out[g], scr, d, tm)
            _store_dil_order(-ws[g] * tsum, cc_out[g], scr, d, tm)

    specs = [_dil_spec(d, tm) for d in DIL_DILATIONS]
    res = pl.pallas_call(
        body, name=name, grid=(n // tm,),
        in_specs=[_rows(tm, 256)] * 2 + specs,
        out_specs=specs + specs,
        out_shape=[_sds((d, n // d, 256), BF16) for d in DIL_DILATIONS]
                  + [_sds((d, n // d, 256), F32) for d in DIL_DILATIONS],
        scratch_shapes=[_dil_scratch(tm)],
        compiler_params=_params("parallel"),
    )(dy, y, *lses)
    return res[0:3], res[3:6]


_WEIGHTS = (("w_in", 1, 736), ("w_branch_na", 1, 128), ("w_branch_dil", 1, 128), ("w_out", 0, 128),
            ("w_up", 1, 512), ("w_down", 0, 512), ("w_ple_gate", 0, 128), ("w_ple_proj", 1, 128))
_W_IN, _W_BNA, _W_BD, _W_OUT, _W_UP, _W_DOWN, _W_PG, _W_PP = range(8)


def _to_full(widx, gathered):
    if _WEIGHTS[widx][1] == 0:
        return gathered.reshape(-1, gathered.shape[2])
    return jnp.transpose(gathered, (1, 0, 2)).reshape(gathered.shape[1], -1)


def _to_chunks(widx, mat):
    _, axis, width = _WEIGHTS[widx]
    if axis == 0:
        return mat.reshape(N_DEV, width, mat.shape[1])
    return jnp.transpose(mat.reshape(mat.shape[0], N_DEV, width), (1, 0, 2))


def _local_step(x, p_bf16, positions, target, g_mix, g_mlp, g_ple, g_final, rpb2, get_w_in, get_rest, send_grads):
    tm = 256
    half = HEAD_DIM // 2
    inv_freq = 10000.0 ** (-jnp.arange(half, dtype=F32) / half)
    ang = positions.astype(F32)[:, None] * inv_freq
    cos, sin = jnp.cos(ang), jnp.sin(ang)
    cos_t = jnp.tile(jnp.concatenate([cos, cos], axis=-1), (1, 4))
    sin_t = jnp.tile(jnp.concatenate([-sin, sin], axis=-1), (1, 4))
    rb = _rpb_table(rpb2)

    a = _rms_fwd(x, g_mix, tm=tm, name="rms_mix")
    w_in, token = get_w_in(a)
    proj = _matmul(a, w_in, out_dtype=F32, tm=512, tn=2944, tk=1024, name="mm_in", after=token)
    na_qkv, dq_g, dk_g, dv_g, sn, sd = _split_proj(proj, cos_t, sin_t, tm=tm, name="split_proj")
    y_na = _na_fwd(*na_qkv, rb, name="na_fwd")
    d_out, d_lse = [], []
    for g in range(3):
        o, lse = _dil_fwd(dq_g[g], dk_g[g], dv_g[g], name=f"dil_fwd{g}")
        d_out.append(o)
        d_lse.append(lse)
    y_dil, y_dil_b = _dil_merge(d_out, d_lse, tm=tm, name="dil_merge")
    w_bna, w_bd, w_out, w_up, w_down, w_pg, w_pp = get_rest(y_dil_b)
    bn = _matmul(y_na, w_bna, out_dtype=F32, tm=512, tn=1024, tk=512, name="mm_bna")
    bd, mixed = _matmul(y_dil_b, w_bd, out_dtype=(F32, BF16), tm=512, tn=1024, tk=256, name="mm_bd",
                        extra=(sn, bn, sd), epilogue=lambda acc, s1, b1, s2: (acc, s1 * b1 + s2 * acc))
    h1, c = _matmul(mixed, w_out, out_dtype=(F32, BF16), tm=512, tn=1024, tk=1024, name="mm_out",
                    extra=(x, g_mlp), epilogue=_residual_rms_tile)
    u, f = _matmul(c, w_up, out_dtype=(F32, BF16), tm=512, tn=2048, tk=1024, name="mm_up",
                   epilogue=lambda acc: (acc, jnp.square(jnp.maximum(acc, 0.0))))
    h2, e = _matmul(f, w_down, out_dtype=(F32, BF16), tm=512, tn=1024, tk=2048, name="mm_down",
                    extra=(h1, g_ple), epilogue=_residual_rms_tile)
    pp = _matmul(p_bf16, w_pp, out_dtype=F32, tm=512, tn=1024, tk=256, name="mm_pp")

    dh3, dpp, dgt, dg_final, loss = _matmul(
        e, w_pg, out_dtype=(F32, BF16, BF16), tm=512, tn=1024, tk=1024, name="mm_pg_tail",
        extra=(pp, h2, target, g_final), epilogue=_tail_tile, n_colsum=2)
    loss = loss[:, :128]
    gw_pp = _matmul(p_bf16, dpp, ta=True, out_dtype=BF16, tm=256, tn=1024, tk=512, name="mm_gw_pp")
    gw_pg = _matmul(e, dgt, ta=True, out_dtype=BF16, tm=512, tn=1024, tk=512, name="mm_gw_pg")
    dh2, dh2_b, dg_ple = _matmul(
        dgt, w_pg, tb=True, out_dtype=(F32, BF16), tm=512, tn=1024, tk=1024, name="mm_de",
        extra=(h2, g_ple, dh3), epilogue=_rms_bwd_twice, n_colsum=1)
    du = _matmul(dh2_b, w_down, tb=True, out_dtype=BF16, tm=512, tn=2048, tk=1024, name="mm_du",
                 extra=(u,), epilogue=lambda acc, uv: (acc * (2.0 * jnp.maximum(uv, 0.0)),))
    gw_down = _matmul(f, dh2_b, ta=True, out_dtype=BF16, tm=1024, tn=1024, tk=512, name="mm_gw_down")
    token = send_grads((_W_PP, _W_PG, _W_DOWN), (gw_pp, gw_pg, gw_down))
    gw_up = _matmul(c, du, ta=True, out_dtype=BF16, tm=512, tn=2048, tk=512, name="mm_gw_up", after=token)
    token = send_grads((_W_UP,), (gw_up,))
    dh1, dh1_b, dg_mlp = _matmul(
        du, w_up, tb=True, out_dtype=(F32, BF16), tm=512, tn=1024, tk=2048, name="mm_dc", after=token,
        extra=(h1, g_mlp, dh2), epilogue=_rms_bwd_twice, n_colsum=1)
    dbn, dbd, dgn, dgd = _matmul(dh1_b, w_out, tb=True, out_dtype=(BF16,) * 4, tm=512, tn=1024, tk=1024,
                                 name="mm_dmixed", extra=(sn, bn, sd, bd), epilogue=_gate_bwd_tile)
    gw_out = _matmul(mixed, dh1_b, ta=True, out_dtype=BF16, tm=512, tn=1024, tk=512, name="mm_gw_out")
    gw_bna = _matmul(y_na, dbn, ta=True, out_dtype=BF16, tm=512, tn=1024, tk=512, name="mm_gw_bna")
    dy_na = _matmul(dbn, w_bna, tb=True, out_dtype=BF16, tm=512, tn=512, tk=1024, name="mm_dy_na")
    gw_bd = _matmul(y_dil_b, dbd, ta=True, out_dtype=BF16, tm=256, tn=1024, tk=512, name="mm_gw_bd")
    token = send_grads((_W_OUT, _W_BNA, _W_BD), (gw_out, gw_bna, gw_bd))
    dy_dil = _matmul(dbd, w_bd, tb=True, out_dtype=F32, tm=512, tn=256, tk=1024, name="mm_dy_dil", after=token)
    dna = _na_bwd(*na_qkv, dy_na, rb, name="na_bwd")
    drpb = _rpb_grad(dna[3].reshape(8, -1), name="rpb_grad")
    do_g, cc_g = _dil_merge_bwd(dy_dil, y_dil, d_lse, tm=tm, name="dil_merge_bwd")
    ddq, ddk, ddv = [], [], []
    for g in range(3):
        r = _dil_bwd(dq_g[g], dk_g[g], dv_g[g], do_g[g], d_lse[g], cc_g[g], name=f"dil_bwd{g}")
        ddq.append(r[0])
        ddk.append(r[1])
        ddv.append(r[2])
    dproj = _assemble_dproj(dna[0:3], ddq, ddk, ddv, dgn, dgd, cos_t, sin_t, tm=tm, name="assemble_dproj")
    gw_in = _matmul(a, dproj, ta=True, out_dtype=BF16, tm=512, tn=2944, tk=512, name="mm_gw_in")
    token = send_grads((_W_IN,), (gw_in,))
    dx, dg_mix = _matmul(
        dproj, w_in, tb=True, out_dtype=(F32,), tm=512, tn=1024, tk=2944, name="mm_da", after=token,
        extra=(x, g_mix, dh1), epilogue=_rms_bwd_tile, n_colsum=1)
    return loss, dx, (dg_mix, dg_mlp, dg_ple, dg_final), drpb


def _cast_bf16(t, *, name):
    def body(t_ref, o_ref):
        o_ref[...] = t_ref[...].astype(BF16)

    rows, cols = t.shape
    tr = min(256, rows)
    blk = pl.BlockSpec((tr, cols), lambda i: (i, 0))
    return pl.pallas_call(body, name=name, grid=(rows // tr,), in_specs=[blk], out_specs=blk,
                          out_shape=_sds(t.shape, BF16), compiler_params=_params("parallel"))(t)


def _adamw(w, g, m, v):
    m = ADAM_B1 * m + (1.0 - ADAM_B1) * g
    v = ADAM_B2 * v + (1.0 - ADAM_B2) * (g * g)
    m_hat = m / (1.0 - ADAM_B1 ** ADAM_STEP)
    v_hat = v / (1.0 - ADAM_B2 ** ADAM_STEP)
    delta = -ADAM_LR * (m_hat / (jnp.sqrt(v_hat) + ADAM_EPS) + ADAM_WD * w)
    return delta, m, v


def _sum_adamw(parts, w, m, v, *, tr, name, own=None):
    rows, cols = w.shape

    def body(*refs):
        p_ref, w_ref, m_ref, v_ref = refs[:4]
        g_ref, d_ref, nm_ref, nv_ref = refs[-4:]
        g = (p_ref[0] if own is None else refs[4][...]).astype(F32)
        for s in range(1, N_DEV):
            g = g + p_ref[s].astype(F32)
        g_ref[...] = g
        d_ref[...], nm_ref[...], nv_ref[...] = _adamw(w_ref[...], g, m_ref[...], v_ref[...])

    blk = pl.BlockSpec((tr, cols), lambda i: (i, 0))
    extra = [] if own is None else [own]
    return pl.pallas_call(
        body, name=name, grid=(rows // tr,),
        in_specs=[pl.BlockSpec((N_DEV, tr, cols), lambda i: (0, i, 0)), blk, blk, blk] + [blk] * len(extra),
        out_specs=[blk] * 4, out_shape=[_sds((rows, cols), F32)] * 4,
        compiler_params=_params("parallel"),
    )(parts, w, m, v, *extra)


_RPB_SIZE = 8 * 15 * 31


def _pack_small(g_mix, g_mlp, g_ple, g_final, rpb, loss_row):
    flat = jnp.concatenate([g_mix.reshape(-1), g_mlp.reshape(-1), g_ple.reshape(-1), g_final.reshape(-1),
                            rpb.reshape(-1), jnp.zeros((3840 - _RPB_SIZE,), F32), loss_row.reshape(-1),
                            jnp.zeros((128,), F32)])
    return flat.reshape(64, 128)


def _unpack_small(t):
    flat = t.reshape(-1)
    return (flat[0:1024].reshape(1, 1024), flat[4096:4096 + _RPB_SIZE].reshape(1, 8, 15, 31),
            flat[1024:2048].reshape(1, 1024), flat[2048:3072].reshape(1, 1024), flat[3072:4096])


def kernel(x, p, positions, g_mix, w_in, rpb, w_branch_na, w_branch_dil, w_out, g_mlp, w_up, w_down, g_ple, w_ple_gate, w_ple_proj, g_final, loss_target, m_g_mix, m_w_in, m_rpb, m_w_branch_na, m_w_branch_dil, m_w_out, m_g_mlp, m_w_up, m_w_down, m_g_ple, m_w_ple_gate, m_w_ple_proj, m_g_final, v_g_mix, v_w_in, v_rpb, v_w_branch_na, v_w_branch_dil, v_w_out, v_g_mlp, v_w_up, v_w_down, v_g_ple, v_w_ple_gate, v_w_ple_proj, v_g_final):
    sharded = dict(w_in=(w_in, m_w_in, v_w_in), w_branch_na=(w_branch_na, m_w_branch_na, v_w_branch_na),
                   w_branch_dil=(w_branch_dil, m_w_branch_dil, v_w_branch_dil), w_out=(w_out, m_w_out, v_w_out),
                   w_up=(w_up, m_w_up, v_w_up), w_down=(w_down, m_w_down, v_w_down),
                   w_ple_gate=(w_ple_gate, m_w_ple_gate, v_w_ple_gate),
                   w_ple_proj=(w_ple_proj, m_w_ple_proj, v_w_ple_proj))
    shards = {k: tuple(t[0] for t in val) for k, val in sharded.items()}

    me = _my_index()

    w_in_b = _cast_bf16(shards["w_in"][0], name="cast_w_in")
    rest_b = [shards[name][0].astype(BF16) for name, _, _ in _WEIGHTS[1:]]
    gather_in, token_in = _start_copies(_gather_copies, [w_in_b], [_sds((N_DEV,) + w_in_b.shape, BF16)],
                                        name="start_gather_w_in")

    def whole(widx, landed, mine):
        return _to_full(widx, lax.dynamic_update_index_in_dim(landed, mine, me, 0))

    rest_handle = []

    def get_w_in(after):
        landed, = _wait_copies(_gather_copies, gather_in, after, name="wait_gather_w_in")
        handle, token = _start_copies(_gather_copies, rest_b, [_sds((N_DEV,) + t.shape, BF16) for t in rest_b],
                                      name="start_gather_rest", after=landed)
        rest_handle.append(handle)
        return whole(_W_IN, landed, w_in_b), token

    def get_rest(after):
        landed = _wait_copies(_gather_copies, rest_handle[0], after, name="wait_gather_rest")
        return [whole(i + 1, t, mine) for i, (t, mine) in enumerate(zip(landed, rest_b))]

    sent = []

    def send_grads(indices, grads):
        chunked = [_to_chunks(i, g) for i, g in zip(indices, grads)]
        handle, token = _start_copies(_exchange_copies, chunked, [_sds(t.shape, BF16) for t in chunked],
                                      name="start_exchange_" + "_".join(_WEIGHTS[i][0] for i in indices))
        sent.append((indices, chunked, handle))
        return token

    g_mix_0 = g_mix + token_in[0:1, 0:1]
    loss, dx, dgs, drpb = _local_step(
        x[0], p[0, 0].astype(BF16), positions[0], loss_target[0],
        g_mix_0, g_mlp, g_ple, g_final.reshape(1, -1), rpb[0], get_w_in, get_rest, send_grads)

    drpb3 = drpb.reshape(8, 16, 32)[:, :15, :31]
    small = _pack_small(dgs[0], dgs[1], dgs[2], dgs[3], drpb3, loss)
    share, done = _start_copies(_gather_copies, [small], [_sds((N_DEV,) + small.shape, F32)],
                                name="start_share_small")

    out = {}
    for indices, chunked, handle in sent:
        landed = _wait_copies(_exchange_copies, handle, done,
                              name="wait_exchange_" + "_".join(_WEIGHTS[i][0] for i in indices))
        for i, part, mine in zip(indices, landed, chunked):
            name = _WEIGHTS[i][0]
            w, m, v = shards[name]
            own = lax.dynamic_index_in_dim(mine, me, 0, keepdims=False)
            res = _sum_adamw(part, w, m, v, tr=min(128, w.shape[0]), name="adamw_" + name, own=own)
            out[name] = [t[None] for t in res]
            done = res[0]
    small_landed, = _wait_copies(_gather_copies, share, done, name="wait_share_small")
    small_all = lax.dynamic_update_index_in_dim(small_landed, small, me, 0)
    small_w = _pack_small(g_mix, g_mlp, g_ple, g_final, rpb, jnp.zeros((128,), F32))
    small_m = _pack_small(m_g_mix, m_g_mlp, m_g_ple, m_g_final, m_rpb, jnp.zeros((128,), F32))
    small_v = _pack_small(v_g_mix, v_g_mlp, v_g_ple, v_g_final, v_rpb, jnp.zeros((128,), F32))
    res = _sum_adamw(small_all, small_w, small_m, small_v, tr=64, name="adamw_small")
    unpacked = [_unpack_small(t) for t in res]
    for i, name in enumerate(("g_mix", "rpb", "g_mlp", "g_ple", "g_final")):
        out[name] = [u[i] for u in unpacked]
    loss_total = res[0][62, 0]

    order = ("g_mix", "w_in", "rpb", "w_branch_na", "w_branch_dil", "w_out", "g_mlp", "w_up", "w_down",
             "g_ple", "w_ple_gate", "w_ple_proj", "g_final")
    grads = [out[k][0] for k in order]
    deltas = [out[k][1] for k in order]
    new_m = [out[k][2] for k in order]
    new_v = [out[k][3] for k in order]
    return (loss_total, dx[None], *grads, *deltas, *new_m, *new_v)
```

```python
import jax
import jax.numpy as jnp
from jax import lax
from jax.experimental import pallas as pl
from jax.experimental.pallas import tpu as pltpu

F32 = jnp.float32
BF16 = jnp.bfloat16

D_MODEL = 1024
HEAD_DIM = 64
GRID_W = 64
NA_WIDTH = 512
DIL_WIDTH = 768
IN_WIDTH = 5888
DIL_DILATIONS = (1, 4, 16)
DIL_RADIUS = 64
NA_WIN_ROWS = 8
RMS_EPS = 1e-6
NEG_INF = -1e30
QK_SCALE = HEAD_DIM ** -0.5

ADAM_LR = 0.001
ADAM_B1 = 0.9
ADAM_B2 = 0.999
ADAM_EPS = 1e-08
ADAM_WD = 0.01
ADAM_STEP = 10

N_DEV = 8
VMEM_LIMIT = 56 * 1024 * 1024
EPILOGUE_ROWS = 256
MESH = pl.DeviceIdType.MESH

NT_DIMS = (((1,), (1,)), ((), ()))
TN_DIMS = (((0,), (0,)), ((), ()))


def _sds(shape, dtype):
    return jax.ShapeDtypeStruct(shape, dtype)


def _params(*sem):
    return pltpu.CompilerParams(dimension_semantics=sem, vmem_limit_bytes=VMEM_LIMIT)


def _rows(tm, width, col=0):
    return pl.BlockSpec((tm, width), lambda i, c=col: (i, c))


def _const(shape):
    zeros = (0,) * len(shape)
    return pl.BlockSpec(shape, lambda i: zeros)


def _my_index():
    return 4 * lax.axis_index("x") + 2 * lax.axis_index("y") + lax.axis_index("c")


def _peer(k):
    x, y, c = lax.axis_index("x"), lax.axis_index("y"), lax.axis_index("c")
    px = 1 - x if k & 4 else x
    py = 1 - y if k & 2 else y
    pc = 1 - c if k & 1 else c
    return (px, py, pc), 4 * px + 2 * py + pc


def _call(body, *, name, grid, in_specs, out_specs, out_shape, scratch_shapes, args, after=None):
    n_in, n_out = len(in_specs), len(out_specs)
    extra = [] if after is None else [after]
    n_x = n_in + len(extra)

    def plain(*refs):
        body(refs[:n_in], refs[n_x:n_x + n_out], refs[n_x + n_out:])

    res = pl.pallas_call(plain, name=name, grid=grid,
                         in_specs=list(in_specs) + [pl.BlockSpec(memory_space=pl.ANY)] * len(extra),
                         out_specs=out_specs, out_shape=out_shape, scratch_shapes=scratch_shapes,
                         compiler_params=_params(*(("arbitrary",) * len(grid))))(*args, *extra)
    return list(res)


_HBM_SPEC = pl.BlockSpec(memory_space=pltpu.HBM)
_SEM_SPEC = pl.BlockSpec(memory_space=pltpu.SEMAPHORE)
_SIDE_EFFECT = pltpu.SideEffectType.DATAFLOW_SIDE_EFFECTING


def _gather_copies(srcs, lands, send, recv, sending):
    me = _my_index()
    out = []
    for w in range(len(srcs)):
        for k in range(1, N_DEV):
            dev, idx = _peer(k)
            out.append(pltpu.make_async_remote_copy(
                src_ref=srcs[w], dst_ref=lands[w].at[me if sending else idx],
                send_sem=send.at[w * 7 + k - 1], recv_sem=recv.at[w * 7 + k - 1],
                device_id=dev, device_id_type=MESH))
    return out


def _exchange_copies(srcs, lands, send, recv, sending):
    out = []
    for w in range(len(srcs)):
        for k in range(1, N_DEV):
            dev, idx = _peer(k)
            out.append(pltpu.make_async_remote_copy(
                src_ref=srcs[w].at[idx], dst_ref=lands[w].at[k],
                send_sem=send.at[w * 7 + k - 1], recv_sem=recv.at[w * 7 + k - 1],
                device_id=dev, device_id_type=MESH))
    return out


def _start_copies(make, srcs, land_shapes, *, name, after=None):
    n = len(srcs)
    extra = [] if after is None else [after]

    def body(*refs):
        src_refs, land_refs = refs[:n], refs[n:2 * n]
        send, recv = refs[2 * n + len(extra)], refs[2 * n + len(extra) + 1]
        token = refs[-1]
        for cp in make(src_refs, land_refs, send, recv, True):
            cp.start()
        token[...] = jnp.zeros_like(token)

    lands = [pltpu.with_memory_space_constraint(lax.empty(s.shape, s.dtype), pltpu.HBM) for s in land_shapes]
    res = pl.pallas_call(
        body, name=name,
        out_shape=(pltpu.SemaphoreType.DMA((7 * n,)), pltpu.SemaphoreType.DMA((7 * n,)),
                   *[pltpu.HBM(s.shape, s.dtype) for s in srcs],
                   *[pltpu.HBM(s.shape, s.dtype) for s in land_shapes],
                   _sds((8, 128), F32)),
        in_specs=[_HBM_SPEC] * (2 * n) + [pl.BlockSpec(memory_space=pl.ANY)] * len(extra),
        out_specs=(_SEM_SPEC, _SEM_SPEC, *([_HBM_SPEC] * (2 * n)), pl.BlockSpec(memory_space=pltpu.VMEM)),
        input_output_aliases={i: 2 + i for i in range(2 * n)},
        compiler_params=pltpu.CompilerParams(has_side_effects=_SIDE_EFFECT),
    )(*[pltpu.with_memory_space_constraint(s, pltpu.HBM) for s in srcs], *lands, *extra)
    return (n, res[0], res[1], res[2:2 + n], res[2 + n:2 + 2 * n]), res[-1]


def _wait_copies(make, handle, after, *, name):
    n, send_sems, recv_sems, srcs, lands = handle

    def body(*refs):
        src_refs, land_refs = refs[:n], refs[n:2 * n]
        send, recv = refs[2 * n], refs[2 * n + 1]
        for cp in make(src_refs, land_refs, send, recv, False):
            cp.wait_send()
            cp.wait_recv()

    res = pl.pallas_call(
        body, name=name,
        out_shape=tuple(pltpu.HBM(s.shape, s.dtype) for s in (*srcs, *lands)),
        in_specs=[_HBM_SPEC] * (2 * n) + [_SEM_SPEC, _SEM_SPEC, pl.BlockSpec(memory_space=pl.ANY)],
        out_specs=tuple([_HBM_SPEC] * (2 * n)),
        input_output_aliases={i: i for i in range(2 * n)},
        compiler_params=pltpu.CompilerParams(has_side_effects=_SIDE_EFFECT),
    )(*srcs, *lands, send_sems, recv_sems, after)
    return list(res[n:])


def _matmul(a, b, *, ta=False, tb=False, out_dtype, tm, tn, tk, name, after=None, extra=(), epilogue=None,
            n_colsum=0):
    m, k = (a.shape[1], a.shape[0]) if ta else a.shape
    n = b.shape[0] if tb else b.shape[1]
    tm, tn, tk = min(tm, m), min(tn, n), min(tk, k)
    nk = k // tk
    dims = (((0 if ta else 1,), (1 if tb else 0,)), ((), ()))
    out_dtypes = out_dtype if isinstance(out_dtype, tuple) else (out_dtype,)
    n_tiles = len(out_dtypes)

    def add_colsums(o_refs, sums):
        i = pl.program_id(1)
        for s_ref, val in zip(o_refs[n_tiles:], sums):
            @pl.when(i == 0)
            def _(s_ref=s_ref, val=val):
                s_ref[...] = val

            @pl.when(i > 0)
            def _(s_ref=s_ref, val=val):
                s_ref[...] += val

    def finish(acc, x_refs, o_refs):
        vals = (acc,) if epilogue is None else epilogue(acc, *[r[...] for r in x_refs])
        for o_ref, val in zip(o_refs[:n_tiles], vals[:n_tiles]):
            o_ref[...] = val.astype(o_ref.dtype)
        add_colsums(o_refs, vals[n_tiles:])

    chunk = EPILOGUE_ROWS if (nk == 1 and epilogue is not None and not ta and tm % EPILOGUE_ROWS == 0) else None

    def body(ins, outs, acc):
        a_ref, b_ref = ins[:2]
        if chunk is not None:
            sums = None
            for r0 in range(0, tm, chunk):
                part = lax.dot_general(a_ref[r0:r0 + chunk, :], b_ref[...], dims, preferred_element_type=F32)
                vals = epilogue(part, *[r[...] if r.shape[0] == 1 else r[r0:r0 + chunk, :] for r in ins[2:]])
                for o_ref, val in zip(outs[:n_tiles], vals[:n_tiles]):
                    o_ref[r0:r0 + chunk, :] = val.astype(o_ref.dtype)
                sums = vals[n_tiles:] if sums is None else [s + v for s, v in zip(sums, vals[n_tiles:])]
            add_colsums(outs, sums)
            return
        part = lax.dot_general(a_ref[...], b_ref[...], dims, preferred_element_type=F32)
        if nk == 1:
            finish(part, ins[2:], outs)
            return
        acc_ref, = acc
        kk = pl.program_id(2)

        @pl.when(kk == 0)
        def _():
            acc_ref[...] = part

        @pl.when(kk > 0)
        def _():
            acc_ref[...] += part

        @pl.when(kk == nk - 1)
        def _():
            finish(acc_ref[...], ins[2:], outs)

    a_spec = (pl.BlockSpec((tk, tm), lambda j, i, kk: (kk, i)) if ta
              else pl.BlockSpec((tm, tk), lambda j, i, kk: (i, kk)))
    b_spec = (pl.BlockSpec((tn, tk), lambda j, i, kk: (j, kk)) if tb
              else pl.BlockSpec((tk, tn), lambda j, i, kk: (kk, j)))
    tile = pl.BlockSpec((tm, tn), lambda j, i, kk: (i, j))
    row = pl.BlockSpec((1, tn), lambda j, i, kk: (0, j))
    res = _call(
        body, name=name, grid=(n // tn, m // tm, nk),
        in_specs=[a_spec, b_spec] + [row if t.shape[0] == 1 else tile for t in extra],
        out_specs=[tile] * n_tiles + [row] * n_colsum,
        out_shape=[_sds((m, n), dt) for dt in out_dtypes] + [_sds((1, n), F32)] * n_colsum,
        scratch_shapes=[] if nk == 1 else [pltpu.VMEM((tm, tn), F32)],
        args=(a, b, *extra), after=after)
    return res if isinstance(out_dtype, tuple) or n_colsum else res[0]


def _rstd(h):
    return lax.rsqrt(jnp.mean(h * h, axis=-1, keepdims=True) + RMS_EPS)


def _sigmoid(z):
    return 1.0 / (1.0 + jnp.exp(-z))


def _rms_fwd(x, g, *, tm, name):
    n = x.shape[0]

    def body(x_ref, g_ref, o_ref):
        h = x_ref[...]
        o_ref[...] = (h * _rstd(h) * g_ref[...]).astype(BF16)

    return pl.pallas_call(
        body, name=name, grid=(n // tm,),
        in_specs=[_rows(tm, D_MODEL), _const((1, D_MODEL))],
        out_specs=_rows(tm, D_MODEL), out_shape=_sds((n, D_MODEL), BF16),
        compiler_params=_params("parallel"),
    )(x, g)


def _swap_halves(t):
    width = t.shape[1]
    lane = lax.broadcasted_iota(jnp.int32, t.shape, 1)
    return jnp.where((lane & 63) < 32, pltpu.roll(t, width - 32, 1), pltpu.roll(t, 32, 1))


def _dil_spec(dil, tm):
    return pl.BlockSpec((dil, tm // dil, 256), lambda i: (0, i, 0))


def _dil_scratch(tm):
    return pltpu.VMEM((2, tm, 128), F32)


def _load_token_order(src, scr, dil, tm):
    if dil == 1:
        return src[0]
    for j in range(dil):
        for c in range(2):
            scr[c, pl.ds(j, tm // dil, stride=dil), :] = src[j, :, c * 128:(c + 1) * 128]
    return jnp.concatenate([scr[0], scr[1]], axis=1)


def _store_dil_order(val, dst, scr, dil, tm):
    if dil == 1:
        dst[0] = val.astype(dst.dtype)
        return
    for c in range(2):
        scr[c] = val[:, c * 128:(c + 1) * 128]
    for j in range(dil):
        for c in range(2):
            dst[j, :, c * 128:(c + 1) * 128] = scr[c, pl.ds(j, tm // dil, stride=dil), :].astype(dst.dtype)


def _split_proj(proj, cos_t, sin_t, *, tm, name):
    n = proj.shape[0]
    n_dil = len(DIL_DILATIONS)

    def body(*refs):
        na_in = refs[0:3]
        dil_in = refs[3:3 + 3 * n_dil]
        gate_in = refs[12:20]
        cos_ref, sin_ref = refs[20:22]
        outs = refs[22:]
        na_out = outs[0:3]
        dil_out = outs[3:12]
        sn_ref, sd_ref = outs[12:14]
        scr = outs[14]
        for t in range(3):
            na_out[t][...] = na_in[t][...].astype(BF16)
        cosv, sinv = cos_ref[...], sin_ref[...]
        for t in range(3):
            for gi, dil in enumerate(DIL_DILATIONS):
                val = dil_in[t * n_dil + gi][...]
                if t < 2:
                    val = val * cosv + _swap_halves(val) * sinv
                _store_dil_order(val, dil_out[t * n_dil + gi], scr, dil, tm)
        for c in range(4):
            sn_ref[:, c * 256:(c + 1) * 256] = _sigmoid(gate_in[c][...])
            sd_ref[:, c * 256:(c + 1) * 256] = _sigmoid(gate_in[4 + c][...])

    in_specs = [_rows(tm, NA_WIDTH, c) for c in range(3)]
    in_specs += [_rows(tm, 256, 6 + c) for c in range(9)]
    in_specs += [_rows(tm, 256, 15 + c) for c in range(8)]
    in_specs += [_rows(tm, 256), _rows(tm, 256)]
    out_specs = [_rows(tm, NA_WIDTH)] * 3
    out_shape = [_sds((n, NA_WIDTH), BF16)] * 3
    for _ in range(3):
        for dil in DIL_DILATIONS:
            out_specs.append(pl.BlockSpec((dil, tm // dil, 256), lambda i: (0, i, 0)))
            out_shape.append(_sds((dil, n // dil, 256), BF16))
    out_specs += [_rows(tm, D_MODEL)] * 2
    out_shape += [_sds((n, D_MODEL), F32)] * 2
    res = pl.pallas_call(
        body, name=name, grid=(n // tm,),
        in_specs=in_specs, out_specs=out_specs, out_shape=out_shape,
        scratch_shapes=[_dil_scratch(tm)],
        compiler_params=_params("parallel"),
    )(*([proj] * 20), cos_t, sin_t)
    return res[0:3], res[3:6], res[6:9], res[9:12], res[12], res[13]


def _residual_rms_tile(delta, h, g):
    hn = h + delta
    return hn, hn * _rstd(hn) * g


def _gate_bwd_tile(dm, s1, b1, s2, b2):
    return dm * s1, dm * s2, dm * b1 * s1 * (1.0 - s1), dm * b2 * s2 * (1.0 - s2)


def _tail_tile(gt, pp, h2, target, g):
    sg = _sigmoid(gt)
    h3 = h2 + sg * pp
    r3 = _rstd(h3)
    n3 = h3 * r3
    err = n3 * g - target
    loss = 0.5 * jnp.sum(jnp.sum(err * err, axis=-1, keepdims=True) / D_MODEL)
    dy = err / D_MODEL
    dn = dy * g
    dh3 = r3 * (dn - n3 * jnp.mean(dn * n3, axis=-1, keepdims=True))
    return (dh3, dh3 * sg, dh3 * pp * sg * (1.0 - sg),
            jnp.sum(dy * n3, axis=0, keepdims=True), jnp.full((1, gt.shape[1]), loss, F32))


def _rms_bwd_tile(dz, h, g, dres):
    r = _rstd(h)
    nrm = h * r
    dn = dz * g
    dh = dres + r * (dn - nrm * jnp.mean(dn * nrm, axis=-1, keepdims=True))
    return dh, jnp.sum(dz * nrm, axis=0, keepdims=True)


def _rms_bwd_twice(dz, h, g, dres):
    dh, dg = _rms_bwd_tile(dz, h, g, dres)
    return dh, dh, dg


def _assemble_dproj(dna, ddil_q, ddil_k, ddil_v, dgn, dgd, cos_t, sin_t, *, tm, name):
    n = dgn.shape[0]

    def body(*refs):
        dq_ref, dk_ref, dv_ref = refs[0:3]
        dil_in = refs[3:12]
        dgn_ref, dgd_ref, cos_ref, sin_ref, o_ref, scr = refs[12:18]
        o_ref[:, 0:512] = dq_ref[...]
        o_ref[:, 512:1024] = dk_ref[...].astype(BF16)
        o_ref[:, 1024:1536] = dv_ref[...].astype(BF16)
        cosv, sinv = cos_ref[...], sin_ref[...]
        for t in range(3):
            for gi, dil in enumerate(DIL_DILATIONS):
                val = _load_token_order(dil_in[t * 3 + gi], scr, dil, tm)
                if t < 2:
                    val = val * cosv + _swap_halves(val * sinv)
                c0 = 1536 + t * DIL_WIDTH + gi * 256
                o_ref[:, c0:c0 + 256] = val.astype(BF16)
        o_ref[:, 3840:4864] = dgn_ref[...]
        o_ref[:, 4864:5888] = dgd_ref[...]

    in_specs = [_rows(tm, NA_WIDTH)] * 3
    for _ in range(3):
        for dil in DIL_DILATIONS:
            in_specs.append(pl.BlockSpec((dil, tm // dil, 256), lambda i: (0, i, 0)))
    in_specs += [_rows(tm, D_MODEL)] * 2 + [_rows(tm, 256)] * 2
    return pl.pallas_call(
        body, name=name, grid=(n // tm,), in_specs=in_specs,
        out_specs=_rows(tm, IN_WIDTH), out_shape=_sds((n, IN_WIDTH), BF16),
        scratch_shapes=[_dil_scratch(tm)],
        compiler_params=_params("parallel"),
    )(*dna, *ddil_q, *ddil_k, *ddil_v, dgn, dgd, cos_t, sin_t)


N_ROW_OFF = 2 * NA_WIN_ROWS - 1
N_PAIRS = N_ROW_OFF - 1
RB_WIDTH = (N_ROW_OFF + 1) * GRID_W


def _na_bias(rb_ref, pair_scr):
    shape = (GRID_W, RB_WIDTH)
    qc = lax.broadcasted_iota(jnp.int32, shape, 0)
    qc2 = lax.broadcasted_iota(jnp.int32, (GRID_W, 128), 0)
    kc2 = lax.broadcasted_iota(jnp.int32, (GRID_W, 128), 1) & (GRID_W - 1)
    cs = jnp.clip(qc2 - 8, 0, GRID_W - 16)
    valid = (kc2 >= cs) & (kc2 < cs + 16)
    for hh in range(2):
        t = jnp.broadcast_to(rb_ref[hh], shape)
        t = pltpu.roll(t, RB_WIDTH - 15, 1)
        for b in range(6):
            t = jnp.where(((qc >> b) & 1) == 1, pltpu.roll(t, 1 << b, 1), t)
        t_odd = pltpu.roll(t, RB_WIDTH - GRID_W, 1)
        for ro in range(N_PAIRS):
            src = t if ro % 2 == 0 else t_odd
            base = (ro // 2) * 128
            pair_scr[hh, ro] = jnp.where(valid, src[:, base:base + 128], NEG_INF)


NA_GROUP_FWD = 4
NA_GROUP_BWD = 4


def _stack_heads(ref, r, scale=1.0):
    lane = lax.broadcasted_iota(jnp.int32, (GRID_W, 128), 1)
    t = ref[pl.ds(pl.multiple_of(r * GRID_W, GRID_W), GRID_W), :].astype(F32) * scale
    return jnp.concatenate([jnp.where(lane < 64, t, 0.0), jnp.where(lane >= 64, t, 0.0)], axis=0).astype(BF16)


def _unstack_heads(t2):
    lane = lax.broadcasted_iota(jnp.int32, (GRID_W, 128), 1)
    return jnp.where(lane < 64, t2[:GRID_W], t2[GRID_W:])


def _na_window(k_ref, v_ref, r, n_rows):
    rs = jnp.clip(r - NA_WIN_ROWS // 2, 0, n_rows - NA_WIN_ROWS)
    ro0 = (NA_WIN_ROWS - 1) - (r - rs)
    off = pl.multiple_of(rs * GRID_W, GRID_W)
    kw = k_ref[pl.ds(off, NA_WIN_ROWS * GRID_W), :]
    vw = v_ref[pl.ds(off, NA_WIN_ROWS * GRID_W), :]
    return kw, vw, off, ro0


def _na_probs(s_raw, pair_scr, ro0):
    bias = [jnp.concatenate([pair_scr[hh, ro0 + 2 * j] for j in range(NA_WIN_ROWS // 2)], axis=1)
            for hh in range(2)]
    s = s_raw + jnp.concatenate(bias, axis=0)
    m = jnp.max(s, axis=-1, keepdims=True)
    e = jnp.exp(s - m)
    return e * (1.0 / jnp.sum(e, axis=-1, keepdims=True))


def _na_fwd(q, k, v, rb, *, name):
    n = q.shape[0]
    n_rows = n // GRID_W

    def body(ins, outs, scr):
        q_ref, k_ref, v_ref, rb_ref = ins
        o_ref, = outs
        pair_scr, = scr
        _na_bias(rb_ref, pair_scr)

        def group(g, carry):
            rows = [g * NA_GROUP_FWD + t for t in range(NA_GROUP_FWD)]
            wins = [_na_window(k_ref, v_ref, r, n_rows) for r in rows]
            raw = [lax.dot_general(_stack_heads(q_ref, r, QK_SCALE), w[0], NT_DIMS, preferred_element_type=F32)
                   for r, w in zip(rows, wins)]
            probs = [_na_probs(s, pair_scr, w[3]) for s, w in zip(raw, wins)]
            outs2 = [jnp.dot(p.astype(BF16), w[1], preferred_element_type=F32) for p, w in zip(probs, wins)]
            for r, o2 in zip(rows, outs2):
                o_ref[pl.ds(pl.multiple_of(r * GRID_W, GRID_W), GRID_W), :] = _unstack_heads(o2).astype(BF16)
            return carry

        lax.fori_loop(0, n_rows // NA_GROUP_FWD, group, 0)

    col = pl.BlockSpec((n, 128), lambda h: (0, h))
    return _call(
        body, name=name, grid=(NA_WIDTH // 128,),
        in_specs=[col, col, col, pl.BlockSpec((2, 1, RB_WIDTH), lambda h: (h, 0, 0))],
        out_specs=[col], out_shape=[_sds((n, NA_WIDTH), BF16)],
        scratch_shapes=[pltpu.VMEM((2, N_PAIRS, GRID_W, 128), F32)],
        args=(q, k, v, rb))[0]


def _na_bwd(q, k, v, do, rb, *, name):
    n = q.shape[0]
    n_rows = n // GRID_W
    win = NA_WIN_ROWS * GRID_W

    def body(ins, outs, scr):
        q_ref, k_ref, v_ref, do_ref, rb_ref = ins
        dq_ref, dk_ref, dv_ref, drb_ref = outs
        pair_scr, acc_scr = scr
        _na_bias(rb_ref, pair_scr)
        acc_scr[...] = jnp.zeros_like(acc_scr)
        dk_ref[...] = jnp.zeros_like(dk_ref)
        dv_ref[...] = jnp.zeros_like(dv_ref)

        def group(g, carry):
            rows = [g * NA_GROUP_BWD + t for t in range(NA_GROUP_BWD)]
            wins = [_na_window(k_ref, v_ref, r, n_rows) for r in rows]
            qss = [_stack_heads(q_ref, r, QK_SCALE) for r in rows]
            doss = [_stack_heads(do_ref, r) for r in rows]
            raw = [lax.dot_general(qs, w[0], NT_DIMS, preferred_element_type=F32) for qs, w in zip(qss, wins)]
            dps = [lax.dot_general(dos, w[1], NT_DIMS, preferred_element_type=F32) for dos, w in zip(doss, wins)]
            probs = [_na_probs(s, pair_scr, w[3]) for s, w in zip(raw, wins)]
            dss = [p * (dp - jnp.sum(p * dp, axis=-1, keepdims=True)) for p, dp in zip(probs, dps)]
            dsbs = [ds.astype(BF16) for ds in dss]
            dq2s = [jnp.dot(dsb, w[0], preferred_element_type=F32) for dsb, w in zip(dsbs, wins)]
            dkws = [lax.dot_general(dsb, qs, TN_DIMS, preferred_element_type=F32) for dsb, qs in zip(dsbs, qss)]
            dvws = [lax.dot_general(p.astype(BF16), dos, TN_DIMS, preferred_element_type=F32)
                    for p, dos in zip(probs, doss)]
            for t, r in enumerate(rows):
                _, _, off, ro0 = wins[t]
                for hh in range(2):
                    for j in range(NA_WIN_ROWS // 2):
                        acc_scr[hh, ro0 + 2 * j] += dss[t][hh * GRID_W:(hh + 1) * GRID_W, j * 128:(j + 1) * 128]
                dq_ref[pl.ds(pl.multiple_of(r * GRID_W, GRID_W), GRID_W), :] = (
                    _unstack_heads(dq2s[t]) * QK_SCALE).astype(BF16)
                dk_ref[pl.ds(off, win), :] += dkws[t]
                dv_ref[pl.ds(off, win), :] += dvws[t]
            return carry

        lax.fori_loop(0, n_rows // NA_GROUP_BWD, group, 0)

        qc = lax.broadcasted_iota(jnp.int32, (N_PAIRS * GRID_W, 128), 0)
        for hh in range(2):
            t = acc_scr[hh].reshape(N_PAIRS * GRID_W, 128)
            for b in range(6):
                t = jnp.where(((qc >> b) & 1) == 1, pltpu.roll(t, 128 - (1 << b), 1), t)
            t = pltpu.roll(t, 15, 1)
            drb_ref[hh] = jnp.sum(t.reshape(N_PAIRS, GRID_W, 128), axis=1)

    col = pl.BlockSpec((n, 128), lambda h: (0, h))
    return _call(
        body, name=name, grid=(NA_WIDTH // 128,),
        in_specs=[col, col, col, col, pl.BlockSpec((2, 1, RB_WIDTH), lambda h: (h, 0, 0))],
        out_specs=[col, col, col, pl.BlockSpec((2, N_PAIRS, 128), lambda h: (h, 0, 0))],
        out_shape=[_sds((n, NA_WIDTH), BF16), _sds((n, NA_WIDTH), F32), _sds((n, NA_WIDTH), F32),
                   _sds((8, N_PAIRS, 128), F32)],
        scratch_shapes=[pltpu.VMEM((2, N_PAIRS, GRID_W, 128), F32),
                        pltpu.VMEM((2, N_PAIRS, GRID_W, 128), F32)],
        args=(q, k, v, do, rb))


def _rpb_table(rpb2):
    t = jnp.pad(rpb2, ((0, 0), (0, 1), (0, GRID_W - rpb2.shape[-1])))
    return t.reshape(8, 1, RB_WIDTH)


def _rpb_grad(drb, *, name):
    kdim = drb.shape[1]

    def body(x_ref, o_ref):
        kk = lax.broadcasted_iota(jnp.int32, (128, 512), 0)
        jj = lax.broadcasted_iota(jnp.int32, (128, 512), 1)
        half, co = kk >> 6, kk & 63
        acc = jnp.zeros((8, 512), F32)
        for ro in range(N_PAIRS):
            hit = ((ro + half) == (jj >> 5)) & (co == (jj & 31)) & (co < 31)
            onehot = jnp.where(hit, 1.0, 0.0).astype(F32)
            acc = acc + jnp.dot(x_ref[:, ro * 128:(ro + 1) * 128], onehot, preferred_element_type=F32,
                                precision=lax.Precision.HIGHEST)
        o_ref[...] = acc

    return pl.pallas_call(
        body, name=name, grid=(1,),
        in_specs=[_const((8, kdim))], out_specs=_const((8, 512)), out_shape=_sds((8, 512), F32),
        compiler_params=_params("arbitrary"),
    )(drb)


def _dil_blocks(length):
    qb = min(128, length)
    return qb, min(qb + 2 * DIL_RADIUS, length)


def _dil_scores(q_ref, k_ref, v_ref, i, qb, win, length):
    start = pl.multiple_of(jnp.clip(i * qb - DIL_RADIUS, 0, length - win), DIL_RADIUS)
    kw = k_ref[0, pl.ds(start, win), :]
    vw = v_ref[0, pl.ds(start, win), :]
    qv = q_ref[0].astype(F32) * QK_SCALE
    lane = lax.broadcasted_iota(jnp.int32, (qb, 256), 1)
    qs = jnp.concatenate([jnp.where((lane >> 6) == h, qv, 0.0) for h in range(4)], axis=0).astype(BF16)
    s = lax.dot_general(qs, kw, NT_DIMS, preferred_element_type=F32)
    gap = ((lax.broadcasted_iota(jnp.int32, (4 * qb, win), 0) & (qb - 1))
           - lax.broadcasted_iota(jnp.int32, (4 * qb, win), 1)) + (i * qb - start)
    s = jnp.where(jnp.abs(gap) <= DIL_RADIUS, s, NEG_INF)
    return s, qs, kw, vw, start, lane


def _pick_heads(stacked, lane, qb):
    out = jnp.zeros((qb, 256), stacked.dtype)
    for h in range(4):
        out = jnp.where((lane >> 6) == h, stacked[h * qb:(h + 1) * qb], out)
    return out


def _stack_head_cols(t, qb):
    return jnp.concatenate([t[:, 64 * h:64 * h + 1] for h in range(4)], axis=0)


def _dil_fwd(q, k, v, *, name):
    dil, length, _ = q.shape
    qb, win = _dil_blocks(length)

    def body(q_ref, k_ref, v_ref, o_ref, lse_ref):
        i = pl.program_id(1)
        s, _, _, vw, _, lane = _dil_scores(q_ref, k_ref, v_ref, i, qb, win, length)
        m = jnp.max(s, axis=-1, keepdims=True)
        e = jnp.exp(s - m)
        norm = jnp.sum(e, axis=-1, keepdims=True)
        lse = m + jnp.log(norm)
        p = e * (1.0 / norm)
        o4 = jnp.dot(p.astype(BF16), vw, preferred_element_type=F32)
        o_ref[0] = _pick_heads(o4, lane, qb)
        lse_ref[0] = _pick_heads(jnp.broadcast_to(lse, (4 * qb, 256)), lane, qb)

    seq = pl.BlockSpec((1, length, 256), lambda j, i: (j, 0, 0))
    blk = pl.BlockSpec((1, qb, 256), lambda j, i: (j, i, 0))
    return pl.pallas_call(
        body, name=name, grid=(dil, length // qb),
        in_specs=[blk, seq, seq], out_specs=[blk, blk],
        out_shape=[_sds((dil, length, 256), F32)] * 2,
        compiler_params=_params("parallel", "parallel"),
    )(q, k, v)


def _dil_bwd(q, k, v, do, lse, cc, *, name):
    dil, length, _ = q.shape
    qb, win = _dil_blocks(length)

    def body(q_ref, k_ref, v_ref, do_ref, lse_ref, cc_ref, dq_ref, dk_ref, dv_ref):
        i = pl.program_id(1)

        @pl.when(i == 0)
        def _():
            dk_ref[...] = jnp.zeros_like(dk_ref)
            dv_ref[...] = jnp.zeros_like(dv_ref)

        s, qs, kw, vw, start, lane = _dil_scores(q_ref, k_ref, v_ref, i, qb, win, length)
        p = jnp.exp(s - _stack_head_cols(lse_ref[0], qb))
        dov = do_ref[0].astype(F32)
        dos = jnp.concatenate([jnp.where((lane >> 6) == h, dov, 0.0) for h in range(4)], axis=0).astype(BF16)
        dp = lax.dot_general(dos, vw, NT_DIMS, preferred_element_type=F32)
        ds = p * (dp + _stack_head_cols(cc_ref[0], qb))
        dsb = ds.astype(BF16)
        dq4 = jnp.dot(dsb, kw, preferred_element_type=F32)
        dq_ref[0] = _pick_heads(dq4, lane, qb) * QK_SCALE
        dk_ref[0, pl.ds(start, win), :] += lax.dot_general(dsb, qs, TN_DIMS, preferred_element_type=F32)
        dv_ref[0, pl.ds(start, win), :] += lax.dot_general(p.astype(BF16), dos, TN_DIMS, preferred_element_type=F32)

    seq = pl.BlockSpec((1, length, 256), lambda j, i: (j, 0, 0))
    blk = pl.BlockSpec((1, qb, 256), lambda j, i: (j, i, 0))
    return pl.pallas_call(
        body, name=name, grid=(dil, length // qb),
        in_specs=[blk, seq, seq, blk, blk, blk], out_specs=[blk, seq, seq],
        out_shape=[_sds((dil, length, 256), F32)] * 3,
        compiler_params=_params("parallel", "arbitrary"),
    )(q, k, v, do, lse, cc)


def _merge_weights(lses):
    m = jnp.maximum(jnp.maximum(lses[0], lses[1]), lses[2])
    es = [jnp.exp(t - m) for t in lses]
    inv = 1.0 / (es[0] + es[1] + es[2])
    return [e * inv for e in es]


def _dil_merge(outs, lses, *, tm, name):
    n = outs[0].shape[1]

    def body(*refs):
        o_in, l_in = refs[0:3], refs[3:6]
        y_ref, yb_ref, scr = refs[6:9]
        lv = [_load_token_order(l_in[g], scr, d, tm) for g, d in enumerate(DIL_DILATIONS)]
        ws = _merge_weights(lv)
        y = jnp.zeros((tm, 256), F32)
        for g, d in enumerate(DIL_DILATIONS):
            y = y + ws[g] * _load_token_order(o_in[g], scr, d, tm)
        y_ref[...] = y
        yb_ref[...] = y.astype(BF16)

    specs = [_dil_spec(d, tm) for d in DIL_DILATIONS]
    return pl.pallas_call(
        body, name=name, grid=(n // tm,), in_specs=specs + specs,
        out_specs=[_rows(tm, 256)] * 2, out_shape=[_sds((n, 256), F32), _sds((n, 256), BF16)],
        scratch_shapes=[_dil_scratch(tm)],
        compiler_params=_params("parallel"),
    )(*outs, *lses)


def _dil_merge_bwd(dy, y, lses, *, tm, name):
    n = dy.shape[0]

    def body(*refs):
        dy_ref, y_ref = refs[0:2]
        l_in = refs[2:5]
        do_out, cc_out = refs[5:8], refs[8:11]
        scr = refs[11]
        lv = [_load_token_order(l_in[g], scr, d, tm) for g, d in enumerate(DIL_DILATIONS)]
        ws = _merge_weights(lv)
        dyv = dy_ref[...]
        rr = lax.broadcasted_iota(jnp.int32, (256, 256), 0) >> 6
        cc = lax.broadcasted_iota(jnp.int32, (256, 256), 1) >> 6
        ones = jnp.where(rr == cc, 1.0, 0.0).astype(F32)
        tsum = jnp.dot(dyv * y_ref[...], ones, preferred_element_type=F32,
                       precision=lax.Precision.HIGHEST)
        for g, d in enumerate(DIL_DILATIONS):
            _store_dil_order(ws[g] * dyv, do_out[g], scr, d, tm)
            _store_dil_order(-ws[g] * tsum, cc_out[g], scr, d, tm)

    specs = [_dil_spec(d, tm) for d in DIL_DILATIONS]
    res = pl.pallas_call(
        body, name=name, grid=(n // tm,),
        in_specs=[_rows(tm, 256)] * 2 + specs,
        out_specs=specs + specs,
        out_shape=[_sds((d, n // d, 256), BF16) for d in DIL_DILATIONS]
                  + [_sds((d, n // d, 256), F32) for d in DIL_DILATIONS],
        scratch_shapes=[_dil_scratch(tm)],
        compiler_params=_params("parallel"),
    )(dy, y, *lses)
    return res[0:3], res[3:6]


_WEIGHTS = (("w_in", 1, 736), ("w_branch_na", 1, 128), ("w_branch_dil", 1, 128), ("w_out", 0, 128),
            ("w_up", 1, 512), ("w_down", 0, 512), ("w_ple_gate", 0, 128), ("w_ple_proj", 1, 128))
_W_IN, _W_BNA, _W_BD, _W_OUT, _W_UP, _W_DOWN, _W_PG, _W_PP = range(8)


def _to_full(widx, gathered):
    if _WEIGHTS[widx][1] == 0:
        return gathered.reshape(-1, gathered.shape[2])
    return jnp.transpose(gathered, (1, 0, 2)).reshape(gathered.shape[1], -1)


def _to_chunks(widx, mat):
    _, axis, width = _WEIGHTS[widx]
    if axis == 0:
        return mat.reshape(N_DEV, width, mat.shape[1])
    return jnp.transpose(mat.reshape(mat.shape[0], N_DEV, width), (1, 0, 2))


def _local_step(x, p_bf16, positions, target, g_mix, g_mlp, g_ple, g_final, rpb2, get_w_in, get_rest, send_grads):
    tm = 256
    half = HEAD_DIM // 2
    inv_freq = 10000.0 ** (-jnp.arange(half, dtype=F32) / half)
    ang = positions.astype(F32)[:, None] * inv_freq
    cos, sin = jnp.cos(ang), jnp.sin(ang)
    cos_t = jnp.tile(jnp.concatenate([cos, cos], axis=-1), (1, 4))
    sin_t = jnp.tile(jnp.concatenate([-sin, sin], axis=-1), (1, 4))
    rb = _rpb_table(rpb2)

    a = _rms_fwd(x, g_mix, tm=tm, name="rms_mix")
    w_in, token = get_w_in(a)
    proj = _matmul(a, w_in, out_dtype=F32, tm=512, tn=2944, tk=1024, name="mm_in", after=token)
    na_qkv, dq_g, dk_g, dv_g, sn, sd = _split_proj(proj, cos_t, sin_t, tm=tm, name="split_proj")
    y_na = _na_fwd(*na_qkv, rb, name="na_fwd")
    d_out, d_lse = [], []
    for g in range(3):
        o, lse = _dil_fwd(dq_g[g], dk_g[g], dv_g[g], name=f"dil_fwd{g}")
        d_out.append(o)
        d_lse.append(lse)
    y_dil, y_dil_b = _dil_merge(d_out, d_lse, tm=tm, name="dil_merge")
    w_bna, w_bd, w_out, w_up, w_down, w_pg, w_pp = get_rest(y_dil_b)
    bn = _matmul(y_na, w_bna, out_dtype=F32, tm=512, tn=1024, tk=512, name="mm_bna")
    bd, mixed = _matmul(y_dil_b, w_bd, out_dtype=(F32, BF16), tm=512, tn=1024, tk=256, name="mm_bd",
                        extra=(sn, bn, sd), epilogue=lambda acc, s1, b1, s2: (acc, s1 * b1 + s2 * acc))
    h1, c = _matmul(mixed, w_out, out_dtype=(F32, BF16), tm=512, tn=1024, tk=1024, name="mm_out",
                    extra=(x, g_mlp), epilogue=_residual_rms_tile)
    u, f = _matmul(c, w_up, out_dtype=(F32, BF16), tm=512, tn=2048, tk=1024, name="mm_up",
                   epilogue=lambda acc: (acc, jnp.square(jnp.maximum(acc, 0.0))))
    h2, e = _matmul(f, w_down, out_dtype=(F32, BF16), tm=512, tn=1024, tk=4096, name="mm_down",
                    extra=(h1, g_ple), epilogue=_residual_rms_tile)
    pp = _matmul(p_bf16, w_pp, out_dtype=F32, tm=512, tn=1024, tk=256, name="mm_pp")

    dh3, dpp, dgt, dg_final, loss = _matmul(
        e, w_pg, out_dtype=(F32, BF16, BF16), tm=512, tn=1024, tk=1024, name="mm_pg_tail",
        extra=(pp, h2, target, g_final), epilogue=_tail_tile, n_colsum=2)
    loss = loss[:, :128]
    gw_pp = _matmul(p_bf16, dpp, ta=True, out_dtype=BF16, tm=256, tn=1024, tk=512, name="mm_gw_pp")
    gw_pg = _matmul(e, dgt, ta=True, out_dtype=BF16, tm=512, tn=1024, tk=2048, name="mm_gw_pg")
    dh2, dh2_b, dg_ple = _matmul(
        dgt, w_pg, tb=True, out_dtype=(F32, BF16), tm=512, tn=1024, tk=1024, name="mm_de",
        extra=(h2, g_ple, dh3), epilogue=_rms_bwd_twice, n_colsum=1)
    du = _matmul(dh2_b, w_down, tb=True, out_dtype=BF16, tm=512, tn=2048, tk=1024, name="mm_du",
                 extra=(u,), epilogue=lambda acc, uv: (acc * (2.0 * jnp.maximum(uv, 0.0)),))
    gw_down = _matmul(f, dh2_b, ta=True, out_dtype=BF16, tm=1024, tn=1024, tk=2048, name="mm_gw_down")
    token = send_grads((_W_PP, _W_PG, _W_DOWN), (gw_pp, gw_pg, gw_down))
    gw_up = _matmul(c, du, ta=True, out_dtype=BF16, tm=512, tn=2048, tk=2048, name="mm_gw_up", after=token)
    token = send_grads((_W_UP,), (gw_up,))
    dh1, dh1_b, dg_mlp = _matmul(
        du, w_up, tb=True, out_dtype=(F32, BF16), tm=512, tn=1024, tk=4096, name="mm_dc", after=token,
        extra=(h1, g_mlp, dh2), epilogue=_rms_bwd_twice, n_colsum=1)
    dbn, dbd, dgn, dgd = _matmul(dh1_b, w_out, tb=True, out_dtype=(BF16,) * 4, tm=512, tn=1024, tk=1024,
                                 name="mm_dmixed", extra=(sn, bn, sd, bd), epilogue=_gate_bwd_tile)
    gw_out = _matmul(mixed, dh1_b, ta=True, out_dtype=BF16, tm=512, tn=1024, tk=2048, name="mm_gw_out")
    gw_bna = _matmul(y_na, dbn, ta=True, out_dtype=BF16, tm=512, tn=1024, tk=512, name="mm_gw_bna")
    dy_na = _matmul(dbn, w_bna, tb=True, out_dtype=BF16, tm=512, tn=512, tk=1024, name="mm_dy_na")
    gw_bd = _matmul(y_dil_b, dbd, ta=True, out_dtype=BF16, tm=256, tn=1024, tk=512, name="mm_gw_bd")
    token = send_grads((_W_OUT, _W_BNA, _W_BD), (gw_out, gw_bna, gw_bd))
    dy_dil = _matmul(dbd, w_bd, tb=True, out_dtype=F32, tm=512, tn=256, tk=1024, name="mm_dy_dil", after=token)
    dna = _na_bwd(*na_qkv, dy_na, rb, name="na_bwd")
    drpb = _rpb_grad(dna[3].reshape(8, -1), name="rpb_grad")
    do_g, cc_g = _dil_merge_bwd(dy_dil, y_dil, d_lse, tm=tm, name="dil_merge_bwd")
    ddq, ddk, ddv = [], [], []
    for g in range(3):
        r = _dil_bwd(dq_g[g], dk_g[g], dv_g[g], do_g[g], d_lse[g], cc_g[g], name=f"dil_bwd{g}")
        ddq.append(r[0])
        ddk.append(r[1])
        ddv.append(r[2])
    dproj = _assemble_dproj(dna[0:3], ddq, ddk, ddv, dgn, dgd, cos_t, sin_t, tm=tm, name="assemble_dproj")
    gw_in = _matmul(a, dproj, ta=True, out_dtype=BF16, tm=512, tn=2944, tk=2048, name="mm_gw_in")
    token = send_grads((_W_IN,), (gw_in,))
    dx, dg_mix = _matmul(
        dproj, w_in, tb=True, out_dtype=(F32,), tm=512, tn=1024, tk=5888, name="mm_da", after=token,
        extra=(x, g_mix, dh1), epilogue=_rms_bwd_tile, n_colsum=1)
    return loss, dx, (dg_mix, dg_mlp, dg_ple, dg_final), drpb


def _cast_bf16(t, *, name):
    def body(t_ref, o_ref):
        o_ref[...] = t_ref[...].astype(BF16)

    rows, cols = t.shape
    tr = min(256, rows)
    blk = pl.BlockSpec((tr, cols), lambda i: (i, 0))
    return pl.pallas_call(body, name=name, grid=(rows // tr,), in_specs=[blk], out_specs=blk,
                          out_shape=_sds(t.shape, BF16), compiler_params=_params("parallel"))(t)


def _adamw(w, g, m, v):
    m = ADAM_B1 * m + (1.0 - ADAM_B1) * g
    v = ADAM_B2 * v + (1.0 - ADAM_B2) * (g * g)
    m_hat = m / (1.0 - ADAM_B1 ** ADAM_STEP)
    v_hat = v / (1.0 - ADAM_B2 ** ADAM_STEP)
    delta = -ADAM_LR * (m_hat / (jnp.sqrt(v_hat) + ADAM_EPS) + ADAM_WD * w)
    return delta, m, v


def _sum_adamw(parts, w, m, v, *, tr, name, own=None):
    rows, cols = w.shape

    def body(*refs):
        p_ref, w_ref, m_ref, v_ref = refs[:4]
        g_ref, d_ref, nm_ref, nv_ref = refs[-4:]
        g = (p_ref[0] if own is None else refs[4][...]).astype(F32)
        for s in range(1, N_DEV):
            g = g + p_ref[s].astype(F32)
        g_ref[...] = g
        d_ref[...], nm_ref[...], nv_ref[...] = _adamw(w_ref[...], g, m_ref[...], v_ref[...])

    blk = pl.BlockSpec((tr, cols), lambda i: (i, 0))
    extra = [] if own is None else [own]
    return pl.pallas_call(
        body, name=name, grid=(rows // tr,),
        in_specs=[pl.BlockSpec((N_DEV, tr, cols), lambda i: (0, i, 0)), blk, blk, blk] + [blk] * len(extra),
        out_specs=[blk] * 4, out_shape=[_sds((rows, cols), F32)] * 4,
        compiler_params=_params("parallel"),
    )(parts, w, m, v, *extra)


_RPB_SIZE = 8 * 15 * 31


def _pack_small(g_mix, g_mlp, g_ple, g_final, rpb, loss_row):
    flat = jnp.concatenate([g_mix.reshape(-1), g_mlp.reshape(-1), g_ple.reshape(-1), g_final.reshape(-1),
                            rpb.reshape(-1), jnp.zeros((3840 - _RPB_SIZE,), F32), loss_row.reshape(-1),
                            jnp.zeros((128,), F32)])
    return flat.reshape(64, 128)


def _unpack_small(t):
    flat = t.reshape(-1)
    return (flat[0:1024].reshape(1, 1024), flat[4096:4096 + _RPB_SIZE].reshape(1, 8, 15, 31),
            flat[1024:2048].reshape(1, 1024), flat[2048:3072].reshape(1, 1024), flat[3072:4096])


def kernel(x, p, positions, g_mix, w_in, rpb, w_branch_na, w_branch_dil, w_out, g_mlp, w_up, w_down, g_ple, w_ple_gate, w_ple_proj, g_final, loss_target, m_g_mix, m_w_in, m_rpb, m_w_branch_na, m_w_branch_dil, m_w_out, m_g_mlp, m_w_up, m_w_down, m_g_ple, m_w_ple_gate, m_w_ple_proj, m_g_final, v_g_mix, v_w_in, v_rpb, v_w_branch_na, v_w_branch_dil, v_w_out, v_g_mlp, v_w_up, v_w_down, v_g_ple, v_w_ple_gate, v_w_ple_proj, v_g_final):
    sharded = dict(w_in=(w_in, m_w_in, v_w_in), w_branch_na=(w_branch_na, m_w_branch_na, v_w_branch_na),
                   w_branch_dil=(w_branch_dil, m_w_branch_dil, v_w_branch_dil), w_out=(w_out, m_w_out, v_w_out),
                   w_up=(w_up, m_w_up, v_w_up), w_down=(w_down, m_w_down, v_w_down),
                   w_ple_gate=(w_ple_gate, m_w_ple_gate, v_w_ple_gate),
                   w_ple_proj=(w_ple_proj, m_w_ple_proj, v_w_ple_proj))
    shards = {k: tuple(t[0] for t in val) for k, val in sharded.items()}

    me = _my_index()

    w_in_b = _cast_bf16(shards["w_in"][0], name="cast_w_in")
    rest_b = [shards[name][0].astype(BF16) for name, _, _ in _WEIGHTS[1:]]
    gather_in, token_in = _start_copies(_gather_copies, [w_in_b], [_sds((N_DEV,) + w_in_b.shape, BF16)],
                                        name="start_gather_w_in")

    def whole(widx, landed, mine):
        return _to_full(widx, lax.dynamic_update_index_in_dim(landed, mine, me, 0))

    rest_handle = []

    def get_w_in(after):
        landed, = _wait_copies(_gather_copies, gather_in, after, name="wait_gather_w_in")
        handle, token = _start_copies(_gather_copies, rest_b, [_sds((N_DEV,) + t.shape, BF16) for t in rest_b],
                                      name="start_gather_rest", after=landed)
        rest_handle.append(handle)
        return whole(_W_IN, landed, w_in_b), token

    def get_rest(after):
        landed = _wait_copies(_gather_copies, rest_handle[0], after, name="wait_gather_rest")
        return [whole(i + 1, t, mine) for i, (t, mine) in enumerate(zip(landed, rest_b))]

    sent = []

    def send_grads(indices, grads):
        chunked = [_to_chunks(i, g) for i, g in zip(indices, grads)]
        handle, token = _start_copies(_exchange_copies, chunked, [_sds(t.shape, BF16) for t in chunked],
                                      name="start_exchange_" + "_".join(_WEIGHTS[i][0] for i in indices))
        sent.append((indices, chunked, handle))
        return token

    g_mix_0 = g_mix + token_in[0:1, 0:1]
    loss, dx, dgs, drpb = _local_step(
        x[0], p[0, 0].astype(BF16), positions[0], loss_target[0],
        g_mix_0, g_mlp, g_ple, g_final.reshape(1, -1), rpb[0], get_w_in, get_rest, send_grads)

    drpb3 = drpb.reshape(8, 16, 32)[:, :15, :31]
    small = _pack_small(dgs[0], dgs[1], dgs[2], dgs[3], drpb3, loss)
    share, done = _start_copies(_gather_copies, [small], [_sds((N_DEV,) + small.shape, F32)],
                                name="start_share_small")

    out = {}
    for indices, chunked, handle in sent:
        landed = _wait_copies(_exchange_copies, handle, done,
                              name="wait_exchange_" + "_".join(_WEIGHTS[i][0] for i in indices))
        for i, part, mine in zip(indices, landed, chunked):
            name = _WEIGHTS[i][0]
            w, m, v = shards[name]
            own = lax.dynamic_index_in_dim(mine, me, 0, keepdims=False)
            res = _sum_adamw(part, w, m, v, tr=min(128, w.shape[0]), name="adamw_" + name, own=own)
            out[name] = [t[None] for t in res]
            done = res[0]
    small_landed, = _wait_copies(_gather_copies, share, done, name="wait_share_small")
    small_all = lax.dynamic_update_index_in_dim(small_landed, small, me, 0)
    small_w = _pack_small(g_mix, g_mlp, g_ple, g_final, rpb, jnp.zeros((128,), F32))
    small_m = _pack_small(m_g_mix, m_g_mlp, m_g_ple, m_g_final, m_rpb, jnp.zeros((128,), F32))
    small_v = _pack_small(v_g_mix, v_g_mlp, v_g_ple, v_g_final, v_rpb, jnp.zeros((128,), F32))
    res = _sum_adamw(small_all, small_w, small_m, small_v, tr=64, name="adamw_small")
    unpacked = [_unpack_small(t) for t in res]
    for i, name in enumerate(("g_mix", "rpb", "g_mlp", "g_ple", "g_final")):
        out[name] = [u[i] for u in unpacked]
    loss_total = res[0][62, 0]

    order = ("g_mix", "w_in", "rpb", "w_branch_na", "w_branch_dil", "w_out", "g_mlp", "w_up", "w_down",
             "g_ple", "w_ple_gate", "w_ple_proj", "g_final")
    grads = [out[k][0] for k in order]
    deltas = [out[k][1] for k in order]
    new_m = [out[k][2] for k in order]
    new_v = [out[k][3] for k in order]
    return (loss_total, dx[None], *grads, *deltas, *new_m, *new_v)
```

```python
import jax
import jax.numpy as jnp
from jax import lax
from jax.experimental import pallas as pl
from jax.experimental.pallas import tpu as pltpu

F32 = jnp.float32
BF16 = jnp.bfloat16

D_MODEL = 1024
HEAD_DIM = 64
GRID_W = 64
NA_WIDTH = 512
DIL_WIDTH = 768
IN_WIDTH = 5888
DIL_DILATIONS = (1, 4, 16)
DIL_RADIUS = 64
NA_WIN_ROWS = 8
RMS_EPS = 1e-6
NEG_INF = -1e30
QK_SCALE = HEAD_DIM ** -0.5

ADAM_LR = 0.001
ADAM_B1 = 0.9
ADAM_B2 = 0.999
ADAM_EPS = 1e-08
ADAM_WD = 0.01
ADAM_STEP = 10

N_DEV = 8
VMEM_LIMIT = 56 * 1024 * 1024
EPILOGUE_ROWS = 256
MESH = pl.DeviceIdType.MESH

NT_DIMS = (((1,), (1,)), ((), ()))
TN_DIMS = (((0,), (0,)), ((), ()))


def _sds(shape, dtype):
    return jax.ShapeDtypeStruct(shape, dtype)


def _params(*sem):
    return pltpu.CompilerParams(dimension_semantics=sem, vmem_limit_bytes=VMEM_LIMIT)


def _rows(tm, width, col=0):
    return pl.BlockSpec((tm, width), lambda i, c=col: (i, c))


def _const(shape):
    zeros = (0,) * len(shape)
    return pl.BlockSpec(shape, lambda i: zeros)


def _my_index():
    return 4 * lax.axis_index("x") + 2 * lax.axis_index("y") + lax.axis_index("c")


def _peer(k):
    x, y, c = lax.axis_index("x"), lax.axis_index("y"), lax.axis_index("c")
    px = 1 - x if k & 4 else x
    py = 1 - y if k & 2 else y
    pc = 1 - c if k & 1 else c
    return (px, py, pc), 4 * px + 2 * py + pc


def _call(body, *, name, grid, in_specs, out_specs, out_shape, scratch_shapes, args, after=None):
    n_in, n_out = len(in_specs), len(out_specs)
    extra = [] if after is None else [after]
    n_x = n_in + len(extra)

    def plain(*refs):
        body(refs[:n_in], refs[n_x:n_x + n_out], refs[n_x + n_out:])

    res = pl.pallas_call(plain, name=name, grid=grid,
                         in_specs=list(in_specs) + [pl.BlockSpec(memory_space=pl.ANY)] * len(extra),
                         out_specs=out_specs, out_shape=out_shape, scratch_shapes=scratch_shapes,
                         compiler_params=_params(*(("arbitrary",) * len(grid))))(*args, *extra)
    return list(res)


_HBM_SPEC = pl.BlockSpec(memory_space=pltpu.HBM)
_SEM_SPEC = pl.BlockSpec(memory_space=pltpu.SEMAPHORE)
_SIDE_EFFECT = pltpu.SideEffectType.DATAFLOW_SIDE_EFFECTING


def _gather_copies(srcs, lands, send, recv, sending):
    me = _my_index()
    out = []
    for w in range(len(srcs)):
        for k in range(1, N_DEV):
            dev, idx = _peer(k)
            out.append(pltpu.make_async_remote_copy(
                src_ref=srcs[w], dst_ref=lands[w].at[me if sending else idx],
                send_sem=send.at[w * 7 + k - 1], recv_sem=recv.at[w * 7 + k - 1],
                device_id=dev, device_id_type=MESH))
    return out


def _exchange_copies(srcs, lands, send, recv, sending):
    out = []
    for w in range(len(srcs)):
        for k in range(1, N_DEV):
            dev, idx = _peer(k)
            out.append(pltpu.make_async_remote_copy(
                src_ref=srcs[w].at[idx], dst_ref=lands[w].at[k],
                send_sem=send.at[w * 7 + k - 1], recv_sem=recv.at[w * 7 + k - 1],
                device_id=dev, device_id_type=MESH))
    return out


def _start_copies(make, srcs, land_shapes, *, name, after=None):
    n = len(srcs)
    extra = [] if after is None else [after]

    def body(*refs):
        src_refs, land_refs = refs[:n], refs[n:2 * n]
        send, recv = refs[2 * n + len(extra)], refs[2 * n + len(extra) + 1]
        token = refs[-1]
        for cp in make(src_refs, land_refs, send, recv, True):
            cp.start()
        token[...] = jnp.zeros_like(token)

    lands = [pltpu.with_memory_space_constraint(lax.empty(s.shape, s.dtype), pltpu.HBM) for s in land_shapes]
    res = pl.pallas_call(
        body, name=name,
        out_shape=(pltpu.SemaphoreType.DMA((7 * n,)), pltpu.SemaphoreType.DMA((7 * n,)),
                   *[pltpu.HBM(s.shape, s.dtype) for s in srcs],
                   *[pltpu.HBM(s.shape, s.dtype) for s in land_shapes],
                   _sds((8, 128), F32)),
        in_specs=[_HBM_SPEC] * (2 * n) + [pl.BlockSpec(memory_space=pl.ANY)] * len(extra),
        out_specs=(_SEM_SPEC, _SEM_SPEC, *([_HBM_SPEC] * (2 * n)), pl.BlockSpec(memory_space=pltpu.VMEM)),
        input_output_aliases={i: 2 + i for i in range(2 * n)},
        compiler_params=pltpu.CompilerParams(has_side_effects=_SIDE_EFFECT),
    )(*[pltpu.with_memory_space_constraint(s, pltpu.HBM) for s in srcs], *lands, *extra)
    return (n, res[0], res[1], res[2:2 + n], res[2 + n:2 + 2 * n]), res[-1]


def _wait_copies(make, handle, after, *, name):
    n, send_sems, recv_sems, srcs, lands = handle

    def body(*refs):
        src_refs, land_refs = refs[:n], refs[n:2 * n]
        send, recv = refs[2 * n], refs[2 * n + 1]
        for cp in make(src_refs, land_refs, send, recv, False):
            cp.wait_send()
            cp.wait_recv()

    res = pl.pallas_call(
        body, name=name,
        out_shape=tuple(pltpu.HBM(s.shape, s.dtype) for s in (*srcs, *lands)),
        in_specs=[_HBM_SPEC] * (2 * n) + [_SEM_SPEC, _SEM_SPEC, pl.BlockSpec(memory_space=pl.ANY)],
        out_specs=tuple([_HBM_SPEC] * (2 * n)),
        input_output_aliases={i: i for i in range(2 * n)},
        compiler_params=pltpu.CompilerParams(has_side_effects=_SIDE_EFFECT),
    )(*srcs, *lands, send_sems, recv_sems, after)
    return list(res[:n]), list(res[n:])


def _matmul(a, b, *, ta=False, tb=False, out_dtype, tm, tn, tk, name, after=None, extra=(), epilogue=None,
            n_colsum=0, transpose_out=False):
    m, k = (a.shape[1], a.shape[0]) if ta else a.shape
    n = b.shape[0] if tb else b.shape[1]
    tm, tn, tk = min(tm, m), min(tn, n), min(tk, k)
    nk = k // tk
    dims = (((0 if ta else 1,), (1 if tb else 0,)), ((), ()))
    out_dtypes = out_dtype if isinstance(out_dtype, tuple) else (out_dtype,)
    n_tiles = len(out_dtypes)

    def add_colsums(o_refs, sums):
        i = pl.program_id(1)
        for s_ref, val in zip(o_refs[n_tiles:], sums):
            @pl.when(i == 0)
            def _(s_ref=s_ref, val=val):
                s_ref[...] = val

            @pl.when(i > 0)
            def _(s_ref=s_ref, val=val):
                s_ref[...] += val

    def finish(acc, x_refs, o_refs):
        vals = (acc,) if epilogue is None else epilogue(acc, *[r[...] for r in x_refs])
        for o_ref, val in zip(o_refs[:n_tiles], vals[:n_tiles]):
            o_ref[...] = (val.T if transpose_out else val).astype(o_ref.dtype)
        add_colsums(o_refs, vals[n_tiles:])

    chunk = EPILOGUE_ROWS if (nk == 1 and epilogue is not None and not ta and tm % EPILOGUE_ROWS == 0) else None

    def body(ins, outs, acc):
        a_ref, b_ref = ins[:2]
        if chunk is not None:
            sums = None
            for r0 in range(0, tm, chunk):
                part = lax.dot_general(a_ref[r0:r0 + chunk, :], b_ref[...], dims, preferred_element_type=F32)
                vals = epilogue(part, *[r[...] if r.shape[0] == 1 else r[r0:r0 + chunk, :] for r in ins[2:]])
                for o_ref, val in zip(outs[:n_tiles], vals[:n_tiles]):
                    o_ref[r0:r0 + chunk, :] = val.astype(o_ref.dtype)
                sums = vals[n_tiles:] if sums is None else [s + v for s, v in zip(sums, vals[n_tiles:])]
            add_colsums(outs, sums)
            return
        part = lax.dot_general(a_ref[...], b_ref[...], dims, preferred_element_type=F32)
        if nk == 1:
            finish(part, ins[2:], outs)
            return
        acc_ref, = acc
        kk = pl.program_id(2)

        @pl.when(kk == 0)
        def _():
            acc_ref[...] = part

        @pl.when(kk > 0)
        def _():
            acc_ref[...] += part

        @pl.when(kk == nk - 1)
        def _():
            finish(acc_ref[...], ins[2:], outs)

    a_spec = (pl.BlockSpec((tk, tm), lambda j, i, kk: (kk, i)) if ta
              else pl.BlockSpec((tm, tk), lambda j, i, kk: (i, kk)))
    b_spec = (pl.BlockSpec((tn, tk), lambda j, i, kk: (j, kk)) if tb
              else pl.BlockSpec((tk, tn), lambda j, i, kk: (kk, j)))
    tile = pl.BlockSpec((tm, tn), lambda j, i, kk: (i, j))
    row = pl.BlockSpec((1, tn), lambda j, i, kk: (0, j))
    out_tile, out_dims = (pl.BlockSpec((tn, tm), lambda j, i, kk: (j, i)), (n, m)) if transpose_out else (tile, (m, n))
    res = _call(
        body, name=name, grid=(n // tn, m // tm, nk),
        in_specs=[a_spec, b_spec] + [row if t.shape[0] == 1 else tile for t in extra],
        out_specs=[out_tile] * n_tiles + [row] * n_colsum,
        out_shape=[_sds(out_dims, dt) for dt in out_dtypes] + [_sds((1, n), F32)] * n_colsum,
        scratch_shapes=[] if nk == 1 else [pltpu.VMEM((tm, tn), F32)],
        args=(a, b, *extra), after=after)
    return res if isinstance(out_dtype, tuple) or n_colsum else res[0]


def _rstd(h):
    return lax.rsqrt(jnp.mean(h * h, axis=-1, keepdims=True) + RMS_EPS)


def _sigmoid(z):
    return 1.0 / (1.0 + jnp.exp(-z))


def _rms_fwd(x, g, *, tm, name):
    n = x.shape[0]

    def body(x_ref, g_ref, o_ref):
        h = x_ref[...]
        o_ref[...] = (h * _rstd(h) * g_ref[...]).astype(BF16)

    return pl.pallas_call(
        body, name=name, grid=(n // tm,),
        in_specs=[_rows(tm, D_MODEL), _const((1, D_MODEL))],
        out_specs=_rows(tm, D_MODEL), out_shape=_sds((n, D_MODEL), BF16),
        compiler_params=_params("parallel"),
    )(x, g)


def _swap_halves(t):
    width = t.shape[1]
    lane = lax.broadcasted_iota(jnp.int32, t.shape, 1)
    return jnp.where((lane & 63) < 32, pltpu.roll(t, width - 32, 1), pltpu.roll(t, 32, 1))


def _dil_spec(dil, tm):
    return pl.BlockSpec((dil, tm // dil, 256), lambda i: (0, i, 0))


def _dil_scratch(tm):
    return pltpu.VMEM((2, tm, 128), F32)


def _load_token_order(src, scr, dil, tm):
    if dil == 1:
        return src[0]
    for j in range(dil):
        for c in range(2):
            scr[c, pl.ds(j, tm // dil, stride=dil), :] = src[j, :, c * 128:(c + 1) * 128]
    return jnp.concatenate([scr[0], scr[1]], axis=1)


def _store_dil_order(val, dst, scr, dil, tm):
    if dil == 1:
        dst[0] = val.astype(dst.dtype)
        return
    for c in range(2):
        scr[c] = val[:, c * 128:(c + 1) * 128]
    for j in range(dil):
        for c in range(2):
            dst[j, :, c * 128:(c + 1) * 128] = scr[c, pl.ds(j, tm // dil, stride=dil), :].astype(dst.dtype)


def _split_proj(proj, cos_t, sin_t, *, tm, name):
    n = proj.shape[0]
    n_dil = len(DIL_DILATIONS)

    def body(*refs):
        na_in = refs[0:3]
        dil_in = refs[3:3 + 3 * n_dil]
        gate_in = refs[12:20]
        cos_ref, sin_ref = refs[20:22]
        outs = refs[22:]
        na_out = outs[0:3]
        dil_out = outs[3:12]
        sn_ref, sd_ref = outs[12:14]
        scr = outs[14]
        for t in range(3):
            na_out[t][...] = na_in[t][...].astype(BF16)
        cosv, sinv = cos_ref[...], sin_ref[...]
        for t in range(3):
            for gi, dil in enumerate(DIL_DILATIONS):
                val = dil_in[t * n_dil + gi][...]
                if t < 2:
                    val = val * cosv + _swap_halves(val) * sinv
                _store_dil_order(val, dil_out[t * n_dil + gi], scr, dil, tm)
        for c in range(4):
            sn_ref[:, c * 256:(c + 1) * 256] = _sigmoid(gate_in[c][...])
            sd_ref[:, c * 256:(c + 1) * 256] = _sigmoid(gate_in[4 + c][...])

    in_specs = [_rows(tm, NA_WIDTH, c) for c in range(3)]
    in_specs += [_rows(tm, 256, 6 + c) for c in range(9)]
    in_specs += [_rows(tm, 256, 15 + c) for c in range(8)]
    in_specs += [_rows(tm, 256), _rows(tm, 256)]
    out_specs = [_rows(tm, NA_WIDTH)] * 3
    out_shape = [_sds((n, NA_WIDTH), BF16)] * 3
    for _ in range(3):
        for dil in DIL_DILATIONS:
            out_specs.append(pl.BlockSpec((dil, tm // dil, 256), lambda i: (0, i, 0)))
            out_shape.append(_sds((dil, n // dil, 256), BF16))
    out_specs += [_rows(tm, D_MODEL)] * 2
    out_shape += [_sds((n, D_MODEL), F32)] * 2
    res = pl.pallas_call(
        body, name=name, grid=(n // tm,),
        in_specs=in_specs, out_specs=out_specs, out_shape=out_shape,
        scratch_shapes=[_dil_scratch(tm)],
        compiler_params=_params("parallel"),
    )(*([proj] * 20), cos_t, sin_t)
    return res[0:3], res[3:6], res[6:9], res[9:12], res[12], res[13]


def _residual_rms_tile(delta, h, g):
    hn = h + delta
    return hn, hn * _rstd(hn) * g


def _gate_bwd_tile(dm, s1, b1, s2, b2):
    return dm * s1, dm * s2, dm * b1 * s1 * (1.0 - s1), dm * b2 * s2 * (1.0 - s2)


def _tail_tile(gt, pp, h2, target, g):
    sg = _sigmoid(gt)
    h3 = h2 + sg * pp
    r3 = _rstd(h3)
    n3 = h3 * r3
    err = n3 * g - target
    loss = 0.5 * jnp.sum(jnp.sum(err * err, axis=-1, keepdims=True) / D_MODEL)
    dy = err / D_MODEL
    dn = dy * g
    dh3 = r3 * (dn - n3 * jnp.mean(dn * n3, axis=-1, keepdims=True))
    return (dh3, dh3 * sg, dh3 * pp * sg * (1.0 - sg),
            jnp.sum(dy * n3, axis=0, keepdims=True), jnp.full((1, gt.shape[1]), loss, F32))


def _rms_bwd_tile(dz, h, g, dres):
    r = _rstd(h)
    nrm = h * r
    dn = dz * g
    dh = dres + r * (dn - nrm * jnp.mean(dn * nrm, axis=-1, keepdims=True))
    return dh, jnp.sum(dz * nrm, axis=0, keepdims=True)


def _rms_bwd_twice(dz, h, g, dres):
    dh, dg = _rms_bwd_tile(dz, h, g, dres)
    return dh, dh, dg


def _assemble_dproj(dna, ddil_q, ddil_k, ddil_v, dgn, dgd, cos_t, sin_t, *, tm, name):
    n = dgn.shape[0]

    def body(*refs):
        dq_ref, dk_ref, dv_ref = refs[0:3]
        dil_in = refs[3:12]
        dgn_ref, dgd_ref, cos_ref, sin_ref, o_ref, scr = refs[12:18]
        o_ref[:, 0:512] = dq_ref[...]
        o_ref[:, 512:1024] = dk_ref[...].astype(BF16)
        o_ref[:, 1024:1536] = dv_ref[...].astype(BF16)
        cosv, sinv = cos_ref[...], sin_ref[...]
        for t in range(3):
            for gi, dil in enumerate(DIL_DILATIONS):
                val = _load_token_order(dil_in[t * 3 + gi], scr, dil, tm)
                if t < 2:
                    val = val * cosv + _swap_halves(val * sinv)
                c0 = 1536 + t * DIL_WIDTH + gi * 256
                o_ref[:, c0:c0 + 256] = val.astype(BF16)
        o_ref[:, 3840:4864] = dgn_ref[...]
        o_ref[:, 4864:5888] = dgd_ref[...]

    in_specs = [_rows(tm, NA_WIDTH)] * 3
    for _ in range(3):
        for dil in DIL_DILATIONS:
            in_specs.append(pl.BlockSpec((dil, tm // dil, 256), lambda i: (0, i, 0)))
    in_specs += [_rows(tm, D_MODEL)] * 2 + [_rows(tm, 256)] * 2
    return pl.pallas_call(
        body, name=name, grid=(n // tm,), in_specs=in_specs,
        out_specs=_rows(tm, IN_WIDTH), out_shape=_sds((n, IN_WIDTH), BF16),
        scratch_shapes=[_dil_scratch(tm)],
        compiler_params=_params("parallel"),
    )(*dna, *ddil_q, *ddil_k, *ddil_v, dgn, dgd, cos_t, sin_t)


N_ROW_OFF = 2 * NA_WIN_ROWS - 1
N_PAIRS = N_ROW_OFF - 1
RB_WIDTH = (N_ROW_OFF + 1) * GRID_W


def _na_bias(rb_ref, pair_scr):
    shape = (GRID_W, RB_WIDTH)
    qc = lax.broadcasted_iota(jnp.int32, shape, 0)
    qc2 = lax.broadcasted_iota(jnp.int32, (GRID_W, 128), 0)
    kc2 = lax.broadcasted_iota(jnp.int32, (GRID_W, 128), 1) & (GRID_W - 1)
    cs = jnp.clip(qc2 - 8, 0, GRID_W - 16)
    valid = (kc2 >= cs) & (kc2 < cs + 16)
    for hh in range(2):
        t = jnp.broadcast_to(rb_ref[hh], shape)
        t = pltpu.roll(t, RB_WIDTH - 15, 1)
        for b in range(6):
            t = jnp.where(((qc >> b) & 1) == 1, pltpu.roll(t, 1 << b, 1), t)
        t_odd = pltpu.roll(t, RB_WIDTH - GRID_W, 1)
        for ro in range(N_PAIRS):
            src = t if ro % 2 == 0 else t_odd
            base = (ro // 2) * 128
            pair_scr[hh, ro] = jnp.where(valid, src[:, base:base + 128], NEG_INF)


NA_GROUP_FWD = 4
NA_GROUP_BWD = 4


def _stack_heads(ref, r, scale=1.0):
    lane = lax.broadcasted_iota(jnp.int32, (GRID_W, 128), 1)
    t = ref[pl.ds(pl.multiple_of(r * GRID_W, GRID_W), GRID_W), :].astype(F32) * scale
    return jnp.concatenate([jnp.where(lane < 64, t, 0.0), jnp.where(lane >= 64, t, 0.0)], axis=0).astype(BF16)


def _unstack_heads(t2):
    lane = lax.broadcasted_iota(jnp.int32, (GRID_W, 128), 1)
    return jnp.where(lane < 64, t2[:GRID_W], t2[GRID_W:])


def _na_window(k_ref, v_ref, r, n_rows):
    rs = jnp.clip(r - NA_WIN_ROWS // 2, 0, n_rows - NA_WIN_ROWS)
    ro0 = (NA_WIN_ROWS - 1) - (r - rs)
    off = pl.multiple_of(rs * GRID_W, GRID_W)
    kw = k_ref[pl.ds(off, NA_WIN_ROWS * GRID_W), :]
    vw = v_ref[pl.ds(off, NA_WIN_ROWS * GRID_W), :]
    return kw, vw, off, ro0


def _na_probs(s_raw, pair_scr, ro0):
    bias = [jnp.concatenate([pair_scr[hh, ro0 + 2 * j] for j in range(NA_WIN_ROWS // 2)], axis=1)
            for hh in range(2)]
    s = s_raw + jnp.concatenate(bias, axis=0)
    m = jnp.max(s, axis=-1, keepdims=True)
    e = jnp.exp(s - m)
    return e * (1.0 / jnp.sum(e, axis=-1, keepdims=True))


def _na_fwd(q, k, v, rb, *, name):
    n = q.shape[0]
    n_rows = n // GRID_W

    def body(ins, outs, scr):
        q_ref, k_ref, v_ref, rb_ref = ins
        o_ref, = outs
        pair_scr, = scr
        _na_bias(rb_ref, pair_scr)

        def group(g, carry):
            rows = [g * NA_GROUP_FWD + t for t in range(NA_GROUP_FWD)]
            wins = [_na_window(k_ref, v_ref, r, n_rows) for r in rows]
            raw = [lax.dot_general(_stack_heads(q_ref, r, QK_SCALE), w[0], NT_DIMS, preferred_element_type=F32)
                   for r, w in zip(rows, wins)]
            probs = [_na_probs(s, pair_scr, w[3]) for s, w in zip(raw, wins)]
            outs2 = [jnp.dot(p.astype(BF16), w[1], preferred_element_type=F32) for p, w in zip(probs, wins)]
            for r, o2 in zip(rows, outs2):
                o_ref[pl.ds(pl.multiple_of(r * GRID_W, GRID_W), GRID_W), :] = _unstack_heads(o2).astype(BF16)
            return carry

        lax.fori_loop(0, n_rows // NA_GROUP_FWD, group, 0)

    col = pl.BlockSpec((n, 128), lambda h: (0, h))
    return _call(
        body, name=name, grid=(NA_WIDTH // 128,),
        in_specs=[col, col, col, pl.BlockSpec((2, 1, RB_WIDTH), lambda h: (h, 0, 0))],
        out_specs=[col], out_shape=[_sds((n, NA_WIDTH), BF16)],
        scratch_shapes=[pltpu.VMEM((2, N_PAIRS, GRID_W, 128), F32)],
        args=(q, k, v, rb))[0]


def _na_bwd(q, k, v, do, rb, *, name):
    n = q.shape[0]
    n_rows = n // GRID_W
    win = NA_WIN_ROWS * GRID_W

    def body(ins, outs, scr):
        q_ref, k_ref, v_ref, do_ref, rb_ref = ins
        dq_ref, dk_ref, dv_ref, drb_ref = outs
        pair_scr, acc_scr = scr
        _na_bias(rb_ref, pair_scr)
        acc_scr[...] = jnp.zeros_like(acc_scr)
        dk_ref[...] = jnp.zeros_like(dk_ref)
        dv_ref[...] = jnp.zeros_like(dv_ref)

        def group(g, carry):
            rows = [g * NA_GROUP_BWD + t for t in range(NA_GROUP_BWD)]
            wins = [_na_window(k_ref, v_ref, r, n_rows) for r in rows]
            qss = [_stack_heads(q_ref, r, QK_SCALE) for r in rows]
            doss = [_stack_heads(do_ref, r) for r in rows]
            raw = [lax.dot_general(qs, w[0], NT_DIMS, preferred_element_type=F32) for qs, w in zip(qss, wins)]
            dps = [lax.dot_general(dos, w[1], NT_DIMS, preferred_element_type=F32) for dos, w in zip(doss, wins)]
            probs = [_na_probs(s, pair_scr, w[3]) for s, w in zip(raw, wins)]
            dss = [p * (dp - jnp.sum(p * dp, axis=-1, keepdims=True)) for p, dp in zip(probs, dps)]
            dsbs = [ds.astype(BF16) for ds in dss]
            dq2s = [jnp.dot(dsb, w[0], preferred_element_type=F32) for dsb, w in zip(dsbs, wins)]
            dkws = [lax.dot_general(dsb, qs, TN_DIMS, preferred_element_type=F32) for dsb, qs in zip(dsbs, qss)]
            dvws = [lax.dot_general(p.astype(BF16), dos, TN_DIMS, preferred_element_type=F32)
                    for p, dos in zip(probs, doss)]
            for t, r in enumerate(rows):
                _, _, off, ro0 = wins[t]
                for hh in range(2):
                    for j in range(NA_WIN_ROWS // 2):
                        acc_scr[hh, ro0 + 2 * j] += dss[t][hh * GRID_W:(hh + 1) * GRID_W, j * 128:(j + 1) * 128]
                dq_ref[pl.ds(pl.multiple_of(r * GRID_W, GRID_W), GRID_W), :] = (
                    _unstack_heads(dq2s[t]) * QK_SCALE).astype(BF16)
                dk_ref[pl.ds(off, win), :] += dkws[t]
                dv_ref[pl.ds(off, win), :] += dvws[t]
            return carry

        lax.fori_loop(0, n_rows // NA_GROUP_BWD, group, 0)

        qc = lax.broadcasted_iota(jnp.int32, (N_PAIRS * GRID_W, 128), 0)
        for hh in range(2):
            t = acc_scr[hh].reshape(N_PAIRS * GRID_W, 128)
            for b in range(6):
                t = jnp.where(((qc >> b) & 1) == 1, pltpu.roll(t, 128 - (1 << b), 1), t)
            t = pltpu.roll(t, 15, 1)
            drb_ref[hh] = jnp.sum(t.reshape(N_PAIRS, GRID_W, 128), axis=1)

    col = pl.BlockSpec((n, 128), lambda h: (0, h))
    return _call(
        body, name=name, grid=(NA_WIDTH // 128,),
        in_specs=[col, col, col, col, pl.BlockSpec((2, 1, RB_WIDTH), lambda h: (h, 0, 0))],
        out_specs=[col, col, col, pl.BlockSpec((2, N_PAIRS, 128), lambda h: (h, 0, 0))],
        out_shape=[_sds((n, NA_WIDTH), BF16), _sds((n, NA_WIDTH), F32), _sds((n, NA_WIDTH), F32),
                   _sds((8, N_PAIRS, 128), F32)],
        scratch_shapes=[pltpu.VMEM((2, N_PAIRS, GRID_W, 128), F32),
                        pltpu.VMEM((2, N_PAIRS, GRID_W, 128), F32)],
        args=(q, k, v, do, rb))


def _rpb_table(rpb2):
    t = jnp.pad(rpb2, ((0, 0), (0, 1), (0, GRID_W - rpb2.shape[-1])))
    return t.reshape(8, 1, RB_WIDTH)


def _rpb_grad(drb, *, name):
    kdim = drb.shape[1]

    def body(x_ref, o_ref):
        kk = lax.broadcasted_iota(jnp.int32, (128, 512), 0)
        jj = lax.broadcasted_iota(jnp.int32, (128, 512), 1)
        half, co = kk >> 6, kk & 63
        acc = jnp.zeros((8, 512), F32)
        for ro in range(N_PAIRS):
            hit = ((ro + half) == (jj >> 5)) & (co == (jj & 31)) & (co < 31)
            onehot = jnp.where(hit, 1.0, 0.0).astype(F32)
            acc = acc + jnp.dot(x_ref[:, ro * 128:(ro + 1) * 128], onehot, preferred_element_type=F32,
                                precision=lax.Precision.HIGHEST)
        o_ref[...] = acc

    return pl.pallas_call(
        body, name=name, grid=(1,),
        in_specs=[_const((8, kdim))], out_specs=_const((8, 512)), out_shape=_sds((8, 512), F32),
        compiler_params=_params("arbitrary"),
    )(drb)


def _dil_blocks(length):
    qb = min(128, length)
    return qb, min(qb + 2 * DIL_RADIUS, length)


def _dil_scores(q_ref, k_ref, v_ref, i, qb, win, length):
    start = pl.multiple_of(jnp.clip(i * qb - DIL_RADIUS, 0, length - win), DIL_RADIUS)
    kw = k_ref[0, pl.ds(start, win), :]
    vw = v_ref[0, pl.ds(start, win), :]
    qv = q_ref[0].astype(F32) * QK_SCALE
    lane = lax.broadcasted_iota(jnp.int32, (qb, 256), 1)
    qs = jnp.concatenate([jnp.where((lane >> 6) == h, qv, 0.0) for h in range(4)], axis=0).astype(BF16)
    s = lax.dot_general(qs, kw, NT_DIMS, preferred_element_type=F32)
    gap = ((lax.broadcasted_iota(jnp.int32, (4 * qb, win), 0) & (qb - 1))
           - lax.broadcasted_iota(jnp.int32, (4 * qb, win), 1)) + (i * qb - start)
    s = jnp.where(jnp.abs(gap) <= DIL_RADIUS, s, NEG_INF)
    return s, qs, kw, vw, start, lane


def _pick_heads(stacked, lane, qb):
    out = jnp.zeros((qb, 256), stacked.dtype)
    for h in range(4):
        out = jnp.where((lane >> 6) == h, stacked[h * qb:(h + 1) * qb], out)
    return out


def _stack_head_cols(t, qb):
    return jnp.concatenate([t[:, 64 * h:64 * h + 1] for h in range(4)], axis=0)


def _dil_fwd(q, k, v, *, name):
    dil, length, _ = q.shape
    qb, win = _dil_blocks(length)

    def body(q_ref, k_ref, v_ref, o_ref, lse_ref):
        i = pl.program_id(1)
        s, _, _, vw, _, lane = _dil_scores(q_ref, k_ref, v_ref, i, qb, win, length)
        m = jnp.max(s, axis=-1, keepdims=True)
        e = jnp.exp(s - m)
        norm = jnp.sum(e, axis=-1, keepdims=True)
        lse = m + jnp.log(norm)
        p = e * (1.0 / norm)
        o4 = jnp.dot(p.astype(BF16), vw, preferred_element_type=F32)
        o_ref[0] = _pick_heads(o4, lane, qb)
        lse_ref[0] = _pick_heads(jnp.broadcast_to(lse, (4 * qb, 256)), lane, qb)

    seq = pl.BlockSpec((1, length, 256), lambda j, i: (j, 0, 0))
    blk = pl.BlockSpec((1, qb, 256), lambda j, i: (j, i, 0))
    return pl.pallas_call(
        body, name=name, grid=(dil, length // qb),
        in_specs=[blk, seq, seq], out_specs=[blk, blk],
        out_shape=[_sds((dil, length, 256), F32)] * 2,
        compiler_params=_params("parallel", "parallel"),
    )(q, k, v)


def _dil_bwd(q, k, v, do, lse, cc, *, name):
    dil, length, _ = q.shape
    qb, win = _dil_blocks(length)

    def body(q_ref, k_ref, v_ref, do_ref, lse_ref, cc_ref, dq_ref, dk_ref, dv_ref):
        i = pl.program_id(1)

        @pl.when(i == 0)
        def _():
            dk_ref[...] = jnp.zeros_like(dk_ref)
            dv_ref[...] = jnp.zeros_like(dv_ref)

        s, qs, kw, vw, start, lane = _dil_scores(q_ref, k_ref, v_ref, i, qb, win, length)
        p = jnp.exp(s - _stack_head_cols(lse_ref[0], qb))
        dov = do_ref[0].astype(F32)
        dos = jnp.concatenate([jnp.where((lane >> 6) == h, dov, 0.0) for h in range(4)], axis=0).astype(BF16)
        dp = lax.dot_general(dos, vw, NT_DIMS, preferred_element_type=F32)
        ds = p * (dp + _stack_head_cols(cc_ref[0], qb))
        dsb = ds.astype(BF16)
        dq4 = jnp.dot(dsb, kw, preferred_element_type=F32)
        dq_ref[0] = _pick_heads(dq4, lane, qb) * QK_SCALE
        dk_ref[0, pl.ds(start, win), :] += lax.dot_general(dsb, qs, TN_DIMS, preferred_element_type=F32)
        dv_ref[0, pl.ds(start, win), :] += lax.dot_general(p.astype(BF16), dos, TN_DIMS, preferred_element_type=F32)

    seq = pl.BlockSpec((1, length, 256), lambda j, i: (j, 0, 0))
    blk = pl.BlockSpec((1, qb, 256), lambda j, i: (j, i, 0))
    return pl.pallas_call(
        body, name=name, grid=(dil, length // qb),
        in_specs=[blk, seq, seq, blk, blk, blk], out_specs=[blk, seq, seq],
        out_shape=[_sds((dil, length, 256), F32)] * 3,
        compiler_params=_params("parallel", "arbitrary"),
    )(q, k, v, do, lse, cc)


def _merge_weights(lses):
    m = jnp.maximum(jnp.maximum(lses[0], lses[1]), lses[2])
    es = [jnp.exp(t - m) for t in lses]
    inv = 1.0 / (es[0] + es[1] + es[2])
    return [e * inv for e in es]


def _dil_merge(outs, lses, *, tm, name):
    n = outs[0].shape[1]

    def body(*refs):
        o_in, l_in = refs[0:3], refs[3:6]
        y_ref, yb_ref, scr = refs[6:9]
        lv = [_load_token_order(l_in[g], scr, d, tm) for g, d in enumerate(DIL_DILATIONS)]
        ws = _merge_weights(lv)
        y = jnp.zeros((tm, 256), F32)
        for g, d in enumerate(DIL_DILATIONS):
            y = y + ws[g] * _load_token_order(o_in[g], scr, d, tm)
        y_ref[...] = y
        yb_ref[...] = y.astype(BF16)

    specs = [_dil_spec(d, tm) for d in DIL_DILATIONS]
    return pl.pallas_call(
        body, name=name, grid=(n // tm,), in_specs=specs + specs,
        out_specs=[_rows(tm, 256)] * 2, out_shape=[_sds((n, 256), F32), _sds((n, 256), BF16)],
        scratch_shapes=[_dil_scratch(tm)],
        compiler_params=_params("parallel"),
    )(*outs, *lses)


def _dil_merge_bwd(dy, y, lses, *, tm, name):
    n = dy.shape[0]

    def body(*refs):
        dy_ref, y_ref = refs[0:2]
        l_in = refs[2:5]
        do_out, cc_out = refs[5:8], refs[8:11]
        scr = refs[11]
        lv = [_load_token_order(l_in[g], scr, d, tm) for g, d in enumerate(DIL_DILATIONS)]
        ws = _merge_weights(lv)
        dyv = dy_ref[...]
        rr = lax.broadcasted_iota(jnp.int32, (256, 256), 0) >> 6
        cc = lax.broadcasted_iota(jnp.int32, (256, 256), 1) >> 6
        ones = jnp.where(rr == cc, 1.0, 0.0).astype(F32)
        tsum = jnp.dot(dyv * y_ref[...], ones, preferred_element_type=F32,
                       precision=lax.Precision.HIGHEST)
        for g, d in enumerate(DIL_DILATIONS):
            _store_dil_order(ws[g] * dyv, do_out[g], scr, d, tm)
            _store_dil_order(-ws[g] * tsum, cc_out[g], scr, d, tm)

    specs = [_dil_spec(d, tm) for d in DIL_DILATIONS]
    res = pl.pallas_call(
        body, name=name, grid=(n // tm,),
        in_specs=[_rows(tm, 256)] * 2 + specs,
        out_specs=specs + specs,
        out_shape=[_sds((d, n // d, 256), BF16) for d in DIL_DILATIONS]
                  + [_sds((d, n // d, 256), F32) for d in DIL_DILATIONS],
        scratch_shapes=[_dil_scratch(tm)],
        compiler_params=_params("parallel"),
    )(dy, y, *lses)
    return res[0:3], res[3:6]


_WEIGHTS = (("w_in", 1, 736), ("w_branch_na", 1, 128), ("w_branch_dil", 1, 128), ("w_out", 0, 128),
            ("w_up", 1, 512), ("w_down", 0, 512), ("w_ple_gate", 0, 128), ("w_ple_proj", 1, 128))
_W_IN, _W_BNA, _W_BD, _W_OUT, _W_UP, _W_DOWN, _W_PG, _W_PP = range(8)


def _to_full(gathered):
    return gathered.reshape(-1, gathered.shape[2])


def _to_chunks(widx, mat):
    return mat.reshape(N_DEV, _WEIGHTS[widx][2], mat.shape[1])


def _local_step(x, p_bf16, positions, target, g_mix, g_mlp, g_ple, g_final, rpb2, get_w_in, get_rest, send_grads):
    tm = 256
    half = HEAD_DIM // 2
    inv_freq = 10000.0 ** (-jnp.arange(half, dtype=F32) / half)
    ang = positions.astype(F32)[:, None] * inv_freq
    cos, sin = jnp.cos(ang), jnp.sin(ang)
    cos_t = jnp.tile(jnp.concatenate([cos, cos], axis=-1), (1, 4))
    sin_t = jnp.tile(jnp.concatenate([-sin, sin], axis=-1), (1, 4))
    rb = _rpb_table(rpb2)

    a = _rms_fwd(x, g_mix, tm=tm, name="rms_mix")
    w_in, token = get_w_in(a)
    proj = _matmul(a, w_in, tb=True, out_dtype=F32, tm=512, tn=2944, tk=1024, name="mm_in", after=token)
    na_qkv, dq_g, dk_g, dv_g, sn, sd = _split_proj(proj, cos_t, sin_t, tm=tm, name="split_proj")
    y_na = _na_fwd(*na_qkv, rb, name="na_fwd")
    d_out, d_lse = [], []
    for g in range(3):
        o, lse = _dil_fwd(dq_g[g], dk_g[g], dv_g[g], name=f"dil_fwd{g}")
        d_out.append(o)
        d_lse.append(lse)
    y_dil, y_dil_b = _dil_merge(d_out, d_lse, tm=tm, name="dil_merge")
    w_bna, w_bd, w_out, w_up, w_down, w_pg, w_pp = get_rest(y_dil_b)
    bn = _matmul(y_na, w_bna, tb=True, out_dtype=F32, tm=512, tn=1024, tk=512, name="mm_bna")
    bd, mixed = _matmul(y_dil_b, w_bd, tb=True, out_dtype=(F32, BF16), tm=512, tn=1024, tk=256, name="mm_bd",
                        extra=(sn, bn, sd), epilogue=lambda acc, s1, b1, s2: (acc, s1 * b1 + s2 * acc))
    h1, c = _matmul(mixed, w_out, out_dtype=(F32, BF16), tm=512, tn=1024, tk=1024, name="mm_out",
                    extra=(x, g_mlp), epilogue=_residual_rms_tile)
    u, f = _matmul(c, w_up, tb=True, out_dtype=(F32, BF16), tm=512, tn=2048, tk=1024, name="mm_up",
                   epilogue=lambda acc: (acc, jnp.square(jnp.maximum(acc, 0.0))))
    h2, e = _matmul(f, w_down, out_dtype=(F32, BF16), tm=512, tn=1024, tk=4096, name="mm_down",
                    extra=(h1, g_ple), epilogue=_residual_rms_tile)
    pp = _matmul(p_bf16, w_pp, tb=True, out_dtype=F32, tm=512, tn=1024, tk=256, name="mm_pp")

    dh3, dpp, dgt, dg_final, loss = _matmul(
        e, w_pg, out_dtype=(F32, BF16, BF16), tm=512, tn=1024, tk=1024, name="mm_pg_tail",
        extra=(pp, h2, target, g_final), epilogue=_tail_tile, n_colsum=2)
    loss = loss[:, :128]
    gw_pp = _matmul(p_bf16, dpp, ta=True, transpose_out=True, out_dtype=BF16, tm=256, tn=1024, tk=2048,
                    name="mm_gw_pp")
    gw_pg = _matmul(e, dgt, ta=True, out_dtype=BF16, tm=512, tn=1024, tk=2048, name="mm_gw_pg")
    dh2, dh2_b, dg_ple = _matmul(
        dgt, w_pg, tb=True, out_dtype=(F32, BF16), tm=512, tn=1024, tk=1024, name="mm_de",
        extra=(h2, g_ple, dh3), epilogue=_rms_bwd_twice, n_colsum=1)
    du = _matmul(dh2_b, w_down, tb=True, out_dtype=BF16, tm=512, tn=2048, tk=1024, name="mm_du",
                 extra=(u,), epilogue=lambda acc, uv: (acc * (2.0 * jnp.maximum(uv, 0.0)),))
    gw_down = _matmul(f, dh2_b, ta=True, out_dtype=BF16, tm=1024, tn=1024, tk=2048, name="mm_gw_down")
    token = send_grads((_W_PP, _W_PG, _W_DOWN), (gw_pp, gw_pg, gw_down))
    gw_up = _matmul(c, du, ta=True, transpose_out=True, out_dtype=BF16, tm=512, tn=2048, tk=2048, name="mm_gw_up",
                    after=token)
    token = send_grads((_W_UP,), (gw_up,))
    dh1, dh1_b, dg_mlp = _matmul(
        du, w_up, out_dtype=(F32, BF16), tm=512, tn=1024, tk=4096, name="mm_dc", after=token,
        extra=(h1, g_mlp, dh2), epilogue=_rms_bwd_twice, n_colsum=1)
    dbn, dbd, dgn, dgd = _matmul(dh1_b, w_out, tb=True, out_dtype=(BF16,) * 4, tm=512, tn=1024, tk=1024,
                                 name="mm_dmixed", extra=(sn, bn, sd, bd), epilogue=_gate_bwd_tile)
    gw_out = _matmul(mixed, dh1_b, ta=True, out_dtype=BF16, tm=512, tn=1024, tk=2048, name="mm_gw_out")
    gw_bna = _matmul(y_na, dbn, ta=True, transpose_out=True, out_dtype=BF16, tm=512, tn=1024, tk=2048,
                     name="mm_gw_bna")
    dy_na = _matmul(dbn, w_bna, out_dtype=BF16, tm=512, tn=512, tk=1024, name="mm_dy_na")
    gw_bd = _matmul(y_dil_b, dbd, ta=True, transpose_out=True, out_dtype=BF16, tm=256, tn=1024, tk=2048,
                    name="mm_gw_bd")
    token = send_grads((_W_OUT, _W_BNA, _W_BD), (gw_out, gw_bna, gw_bd))
    dy_dil = _matmul(dbd, w_bd, out_dtype=F32, tm=512, tn=256, tk=1024, name="mm_dy_dil", after=token)
    dna = _na_bwd(*na_qkv, dy_na, rb, name="na_bwd")
    drpb = _rpb_grad(dna[3].reshape(8, -1), name="rpb_grad")
    do_g, cc_g = _dil_merge_bwd(dy_dil, y_dil, d_lse, tm=tm, name="dil_merge_bwd")
    ddq, ddk, ddv = [], [], []
    for g in range(3):
        r = _dil_bwd(dq_g[g], dk_g[g], dv_g[g], do_g[g], d_lse[g], cc_g[g], name=f"dil_bwd{g}")
        ddq.append(r[0])
        ddk.append(r[1])
        ddv.append(r[2])
    dproj = _assemble_dproj(dna[0:3], ddq, ddk, ddv, dgn, dgd, cos_t, sin_t, tm=tm, name="assemble_dproj")
    gw_in = _matmul(a, dproj, ta=True, transpose_out=True, out_dtype=BF16, tm=512, tn=2944, tk=2048, name="mm_gw_in")
    token = send_grads((_W_IN,), (gw_in,))
    dx, dg_mix = _matmul(
        dproj, w_in, out_dtype=(F32,), tm=512, tn=1024, tk=5888, name="mm_da", after=token,
        extra=(x, g_mix, dh1), epilogue=_rms_bwd_tile, n_colsum=1)
    return loss, dx, (dg_mix, dg_mlp, dg_ple, dg_final), drpb


def _cast_bf16(t, *, name):
    def body(t_ref, o_ref):
        o_ref[...] = t_ref[...].astype(BF16)

    rows, cols = t.shape
    tr = 256 if rows % 256 == 0 else rows
    blk = pl.BlockSpec((tr, cols), lambda i: (i, 0))
    return pl.pallas_call(body, name=name, grid=(rows // tr,), in_specs=[blk], out_specs=blk,
                          out_shape=_sds(t.shape, BF16), compiler_params=_params("parallel"))(t)


def _adamw(w, g, m, v):
    m = ADAM_B1 * m + (1.0 - ADAM_B1) * g
    v = ADAM_B2 * v + (1.0 - ADAM_B2) * (g * g)
    m_hat = m / (1.0 - ADAM_B1 ** ADAM_STEP)
    v_hat = v / (1.0 - ADAM_B2 ** ADAM_STEP)
    delta = -ADAM_LR * (m_hat / (jnp.sqrt(v_hat) + ADAM_EPS) + ADAM_WD * w)
    return delta, m, v


def _sum_adamw(parts, w, m, v, *, tr, name, own=None, transposed=False):
    rows, cols = w.shape

    def body(*refs):
        p_ref, w_ref, m_ref, v_ref = refs[:4]
        g_ref, d_ref, nm_ref, nv_ref = refs[-4:]
        g = (p_ref[0] if own is None else refs[4][...]).astype(F32)
        for s in range(1, N_DEV):
            g = g + p_ref[s].astype(F32)
        if transposed:
            g = g.T
        g_ref[...] = g
        d_ref[...], nm_ref[...], nv_ref[...] = _adamw(w_ref[...], g, m_ref[...], v_ref[...])

    extra = [] if own is None else [own]
    if transposed:
        blk = pl.BlockSpec((rows, tr), lambda i: (0, i))
        g_blk, p_blk, steps = pl.BlockSpec((tr, rows), lambda i: (i, 0)), (N_DEV, tr, rows), cols // tr
    else:
        blk = pl.BlockSpec((tr, cols), lambda i: (i, 0))
        g_blk, p_blk, steps = blk, (N_DEV, tr, cols), rows // tr
    return pl.pallas_call(
        body, name=name, grid=(steps,),
        in_specs=[pl.BlockSpec(p_blk, lambda i: (0, i, 0)), blk, blk, blk] + [g_blk] * len(extra),
        out_specs=[blk] * 4, out_shape=[_sds((rows, cols), F32)] * 4,
        compiler_params=_params("parallel"),
    )(parts, w, m, v, *extra)


_RPB_SIZE = 8 * 15 * 31


def _pack_small(g_mix, g_mlp, g_ple, g_final, rpb, loss_row):
    flat = jnp.concatenate([g_mix.reshape(-1), g_mlp.reshape(-1), g_ple.reshape(-1), g_final.reshape(-1),
                            rpb.reshape(-1), jnp.zeros((3840 - _RPB_SIZE,), F32), loss_row.reshape(-1),
                            jnp.zeros((128,), F32)])
    return flat.reshape(64, 128)


def _unpack_small(t):
    flat = t.reshape(-1)
    return (flat[0:1024].reshape(1, 1024), flat[4096:4096 + _RPB_SIZE].reshape(1, 8, 15, 31),
            flat[1024:2048].reshape(1, 1024), flat[2048:3072].reshape(1, 1024), flat[3072:4096])


def kernel(x, p, positions, g_mix, w_in, rpb, w_branch_na, w_branch_dil, w_out, g_mlp, w_up, w_down, g_ple, w_ple_gate, w_ple_proj, g_final, loss_target, m_g_mix, m_w_in, m_rpb, m_w_branch_na, m_w_branch_dil, m_w_out, m_g_mlp, m_w_up, m_w_down, m_g_ple, m_w_ple_gate, m_w_ple_proj, m_g_final, v_g_mix, v_w_in, v_rpb, v_w_branch_na, v_w_branch_dil, v_w_out, v_g_mlp, v_w_up, v_w_down, v_g_ple, v_w_ple_gate, v_w_ple_proj, v_g_final):
    sharded = dict(w_in=(w_in, m_w_in, v_w_in), w_branch_na=(w_branch_na, m_w_branch_na, v_w_branch_na),
                   w_branch_dil=(w_branch_dil, m_w_branch_dil, v_w_branch_dil), w_out=(w_out, m_w_out, v_w_out),
                   w_up=(w_up, m_w_up, v_w_up), w_down=(w_down, m_w_down, v_w_down),
                   w_ple_gate=(w_ple_gate, m_w_ple_gate, v_w_ple_gate),
                   w_ple_proj=(w_ple_proj, m_w_ple_proj, v_w_ple_proj))
    shards = {k: tuple(t[0] for t in val) for k, val in sharded.items()}

    me = _my_index()

    shards["w_in"] = tuple(t.T for t in shards["w_in"])

    w_in_b = _cast_bf16(shards["w_in"][0], name="cast_w_in")
    rest_b = [shards[name][0].astype(BF16).T if axis == 1 else shards[name][0].astype(BF16)
              for name, axis, _ in _WEIGHTS[1:]]
    gather_in, token_in = _start_copies(_gather_copies, [w_in_b], [_sds((N_DEV,) + w_in_b.shape, BF16)],
                                        name="start_gather_w_in")

    def whole(landed, mine):
        return _to_full(lax.dynamic_update_index_in_dim(landed, mine, me, 0))

    rest_handle = []

    def get_w_in(after):
        (mine,), (landed,) = _wait_copies(_gather_copies, gather_in, after, name="wait_gather_w_in")
        handle, token = _start_copies(_gather_copies, rest_b, [_sds((N_DEV,) + t.shape, BF16) for t in rest_b],
                                      name="start_gather_rest", after=landed)
        rest_handle.append(handle)
        return whole(landed, mine), token

    def get_rest(after):
        mine, landed = _wait_copies(_gather_copies, rest_handle[0], after, name="wait_gather_rest")
        return [whole(t, own) for t, own in zip(landed, mine)]

    sent = []

    def send_grads(indices, grads):
        chunked = [_to_chunks(i, g) for i, g in zip(indices, grads)]
        handle, token = _start_copies(_exchange_copies, chunked, [_sds(t.shape, BF16) for t in chunked],
                                      name="start_exchange_" + "_".join(_WEIGHTS[i][0] for i in indices))
        sent.append((indices, handle))
        return token

    g_mix_0 = g_mix + token_in[0:1, 0:1]
    loss, dx, dgs, drpb = _local_step(
        x[0], p[0, 0].astype(BF16), positions[0], loss_target[0],
        g_mix_0, g_mlp, g_ple, g_final.reshape(1, -1), rpb[0], get_w_in, get_rest, send_grads)

    drpb3 = drpb.reshape(8, 16, 32)[:, :15, :31]
    small = _pack_small(dgs[0], dgs[1], dgs[2], dgs[3], drpb3, loss)
    share, done = _start_copies(_gather_copies, [small], [_sds((N_DEV,) + small.shape, F32)],
                                name="start_share_small")

    out = {}
    for indices, handle in sent:
        chunked, landed = _wait_copies(_exchange_copies, handle, done,
                                       name="wait_exchange_" + "_".join(_WEIGHTS[i][0] for i in indices))
        for i, part, mine in zip(indices, landed, chunked):
            name = _WEIGHTS[i][0]
            w, m, v = shards[name]
            own = lax.dynamic_index_in_dim(mine, me, 0, keepdims=False)
            turned = _WEIGHTS[i][1] == 1 and i != _W_IN
            res = _sum_adamw(part, w, m, v, tr=368 if i == _W_IN else 128, name="adamw_" + name, own=own,
                             transposed=turned)
            out[name] = [(t.T if i == _W_IN else t)[None] for t in res]
            done = res[0]
    (small,), (small_landed,) = _wait_copies(_gather_copies, share, done, name="wait_share_small")
    small_all = lax.dynamic_update_index_in_dim(small_landed, small, me, 0)
    small_w = _pack_small(g_mix, g_mlp, g_ple, g_final, rpb, jnp.zeros((128,), F32))
    small_m = _pack_small(m_g_mix, m_g_mlp, m_g_ple, m_g_final, m_rpb, jnp.zeros((128,), F32))
    small_v = _pack_small(v_g_mix, v_g_mlp, v_g_ple, v_g_final, v_rpb, jnp.zeros((128,), F32))
    res = _sum_adamw(small_all, small_w, small_m, small_v, tr=64, name="adamw_small")
    unpacked = [_unpack_small(t) for t in res]
    for i, name in enumerate(("g_mix", "rpb", "g_mlp", "g_ple", "g_final")):
        out[name] = [u[i] for u in unpacked]
    loss_total = res[0][62, 0]

    order = ("g_mix", "w_in", "rpb", "w_branch_na", "w_branch_dil", "w_out", "g_mlp", "w_up", "w_down",
             "g_ple", "w_ple_gate", "w_ple_proj", "g_final")
    grads = [out[k][0] for k in order]
    deltas = [out[k][1] for k in order]
    new_m = [out[k][2] for k in order]
    new_v = [out[k][3] for k in order]
    return (loss_total, dx[None], *grads, *deltas, *new_m, *new_v)
```

```python
import jax
import jax.numpy as jnp
from jax import lax
from jax.experimental import pallas as pl
from jax.experimental.pallas import tpu as pltpu

F32 = jnp.float32
BF16 = jnp.bfloat16

D_MODEL = 1024
HEAD_DIM = 64
GRID_W = 64
NA_WIDTH = 512
DIL_WIDTH = 768
IN_WIDTH = 5888
DIL_DILATIONS = (1, 4, 16)
DIL_RADIUS = 64
NA_WIN_ROWS = 8
RMS_EPS = 1e-6
NEG_INF = -1e30
QK_SCALE = HEAD_DIM ** -0.5

ADAM_LR = 0.001
ADAM_B1 = 0.9
ADAM_B2 = 0.999
ADAM_EPS = 1e-08
ADAM_WD = 0.01
ADAM_STEP = 10

N_DEV = 8
VMEM_LIMIT = 56 * 1024 * 1024
EPILOGUE_ROWS = 256
MESH = pl.DeviceIdType.MESH

NT_DIMS = (((1,), (1,)), ((), ()))
TN_DIMS = (((0,), (0,)), ((), ()))


def _sds(shape, dtype):
    return jax.ShapeDtypeStruct(shape, dtype)


def _params(*sem):
    return pltpu.CompilerParams(dimension_semantics=sem, vmem_limit_bytes=VMEM_LIMIT)


def _rows(tm, width, col=0):
    return pl.BlockSpec((tm, width), lambda i, c=col: (i, c))


def _const(shape):
    zeros = (0,) * len(shape)
    return pl.BlockSpec(shape, lambda i: zeros)


def _my_index():
    return 4 * lax.axis_index("x") + 2 * lax.axis_index("y") + lax.axis_index("c")


def _peer(k):
    x, y, c = lax.axis_index("x"), lax.axis_index("y"), lax.axis_index("c")
    px = 1 - x if k & 4 else x
    py = 1 - y if k & 2 else y
    pc = 1 - c if k & 1 else c
    return (px, py, pc), 4 * px + 2 * py + pc


def _call(body, *, name, grid, in_specs, out_specs, out_shape, scratch_shapes, args, after=None):
    n_in, n_out = len(in_specs), len(out_specs)
    extra = [] if after is None else [after]
    n_x = n_in + len(extra)

    def plain(*refs):
        body(refs[:n_in], refs[n_x:n_x + n_out], refs[n_x + n_out:])

    res = pl.pallas_call(plain, name=name, grid=grid,
                         in_specs=list(in_specs) + [pl.BlockSpec(memory_space=pl.ANY)] * len(extra),
                         out_specs=out_specs, out_shape=out_shape, scratch_shapes=scratch_shapes,
                         compiler_params=_params(*(("arbitrary",) * len(grid))))(*args, *extra)
    return list(res)


_HBM_SPEC = pl.BlockSpec(memory_space=pltpu.HBM)
_SEM_SPEC = pl.BlockSpec(memory_space=pltpu.SEMAPHORE)
_SIDE_EFFECT = pltpu.SideEffectType.DATAFLOW_SIDE_EFFECTING


def _gather_copies(srcs, lands, send, recv, sending):
    me = _my_index()
    out = []
    for w in range(len(srcs)):
        for k in range(1, N_DEV):
            dev, idx = _peer(k)
            out.append(pltpu.make_async_remote_copy(
                src_ref=srcs[w], dst_ref=lands[w].at[me if sending else idx],
                send_sem=send.at[w * 7 + k - 1], recv_sem=recv.at[w * 7 + k - 1],
                device_id=dev, device_id_type=MESH))
    return out


def _exchange_copies(srcs, lands, send, recv, sending):
    out = []
    for w in range(len(srcs)):
        for k in range(1, N_DEV):
            dev, idx = _peer(k)
            out.append(pltpu.make_async_remote_copy(
                src_ref=srcs[w].at[idx], dst_ref=lands[w].at[k],
                send_sem=send.at[w * 7 + k - 1], recv_sem=recv.at[w * 7 + k - 1],
                device_id=dev, device_id_type=MESH))
    return out


def _start_copies(make, srcs, land_shapes, *, name, after=None):
    n = len(srcs)
    extra = [] if after is None else [after]

    def body(*refs):
        src_refs, land_refs = refs[:n], refs[n:2 * n]
        send, recv = refs[2 * n + len(extra)], refs[2 * n + len(extra) + 1]
        token = refs[-1]
        for cp in make(src_refs, land_refs, send, recv, True):
            cp.start()
        token[...] = jnp.zeros_like(token)

    lands = [pltpu.with_memory_space_constraint(lax.empty(s.shape, s.dtype), pltpu.HBM) for s in land_shapes]
    res = pl.pallas_call(
        body, name=name,
        out_shape=(pltpu.SemaphoreType.DMA((7 * n,)), pltpu.SemaphoreType.DMA((7 * n,)),
                   *[pltpu.HBM(s.shape, s.dtype) for s in srcs],
                   *[pltpu.HBM(s.shape, s.dtype) for s in land_shapes],
                   _sds((8, 128), F32)),
        in_specs=[_HBM_SPEC] * (2 * n) + [pl.BlockSpec(memory_space=pl.ANY)] * len(extra),
        out_specs=(_SEM_SPEC, _SEM_SPEC, *([_HBM_SPEC] * (2 * n)), pl.BlockSpec(memory_space=pltpu.VMEM)),
        input_output_aliases={i: 2 + i for i in range(2 * n)},
        compiler_params=pltpu.CompilerParams(has_side_effects=_SIDE_EFFECT),
    )(*[pltpu.with_memory_space_constraint(s, pltpu.HBM) for s in srcs], *lands, *extra)
    return (n, res[0], res[1], res[2:2 + n], res[2 + n:2 + 2 * n]), res[-1]


def _wait_copies(make, handle, after, *, name):
    n, send_sems, recv_sems, srcs, lands = handle

    def body(*refs):
        src_refs, land_refs = refs[:n], refs[n:2 * n]
        send, recv = refs[2 * n], refs[2 * n + 1]
        for cp in make(src_refs, land_refs, send, recv, False):
            cp.wait_send()
            cp.wait_recv()

    res = pl.pallas_call(
        body, name=name,
        out_shape=tuple(pltpu.HBM(s.shape, s.dtype) for s in (*srcs, *lands)),
        in_specs=[_HBM_SPEC] * (2 * n) + [_SEM_SPEC, _SEM_SPEC, pl.BlockSpec(memory_space=pl.ANY)],
        out_specs=tuple([_HBM_SPEC] * (2 * n)),
        input_output_aliases={i: i for i in range(2 * n)},
        compiler_params=pltpu.CompilerParams(has_side_effects=_SIDE_EFFECT),
    )(*srcs, *lands, send_sems, recv_sems, after)
    return list(res[:n]), list(res[n:])


def _matmul(a, b, *, ta=False, tb=False, out_dtype, tm, tn, tk, name, after=None, extra=(), epilogue=None,
            n_colsum=0, transpose_out=False):
    m, k = (a.shape[1], a.shape[0]) if ta else a.shape
    n = b.shape[0] if tb else b.shape[1]
    tm, tn, tk = min(tm, m), min(tn, n), min(tk, k)
    nk = k // tk
    dims = (((0 if ta else 1,), (1 if tb else 0,)), ((), ()))
    out_dtypes = out_dtype if isinstance(out_dtype, tuple) else (out_dtype,)
    n_tiles = len(out_dtypes)

    def add_colsums(o_refs, sums):
        i = pl.program_id(1)
        for s_ref, val in zip(o_refs[n_tiles:], sums):
            @pl.when(i == 0)
            def _(s_ref=s_ref, val=val):
                s_ref[...] = val

            @pl.when(i > 0)
            def _(s_ref=s_ref, val=val):
                s_ref[...] += val

    def finish(acc, x_refs, o_refs):
        vals = (acc,) if epilogue is None else epilogue(acc, *[r[...] for r in x_refs])
        for o_ref, val in zip(o_refs[:n_tiles], vals[:n_tiles]):
            o_ref[...] = (val.T if transpose_out else val).astype(o_ref.dtype)
        add_colsums(o_refs, vals[n_tiles:])

    chunk = EPILOGUE_ROWS if (nk == 1 and epilogue is not None and not ta and tm % EPILOGUE_ROWS == 0) else None

    def body(ins, outs, acc):
        a_ref, b_ref = ins[:2]
        if chunk is not None:
            sums = None
            for r0 in range(0, tm, chunk):
                part = lax.dot_general(a_ref[r0:r0 + chunk, :], b_ref[...], dims, preferred_element_type=F32)
                vals = epilogue(part, *[r[...] if r.shape[0] == 1 else r[r0:r0 + chunk, :] for r in ins[2:]])
                for o_ref, val in zip(outs[:n_tiles], vals[:n_tiles]):
                    o_ref[r0:r0 + chunk, :] = val.astype(o_ref.dtype)
                sums = vals[n_tiles:] if sums is None else [s + v for s, v in zip(sums, vals[n_tiles:])]
            add_colsums(outs, sums)
            return
        part = lax.dot_general(a_ref[...], b_ref[...], dims, preferred_element_type=F32)
        if nk == 1:
            finish(part, ins[2:], outs)
            return
        acc_ref, = acc
        kk = pl.program_id(2)

        @pl.when(kk == 0)
        def _():
            acc_ref[...] = part

        @pl.when(kk > 0)
        def _():
            acc_ref[...] += part

        @pl.when(kk == nk - 1)
        def _():
            finish(acc_ref[...], ins[2:], outs)

    a_spec = (pl.BlockSpec((tk, tm), lambda j, i, kk: (kk, i)) if ta
              else pl.BlockSpec((tm, tk), lambda j, i, kk: (i, kk)))
    b_spec = (pl.BlockSpec((tn, tk), lambda j, i, kk: (j, kk)) if tb
              else pl.BlockSpec((tk, tn), lambda j, i, kk: (kk, j)))
    tile = pl.BlockSpec((tm, tn), lambda j, i, kk: (i, j))
    row = pl.BlockSpec((1, tn), lambda j, i, kk: (0, j))
    out_tile, out_dims = (pl.BlockSpec((tn, tm), lambda j, i, kk: (j, i)), (n, m)) if transpose_out else (tile, (m, n))
    res = _call(
        body, name=name, grid=(n // tn, m // tm, nk),
        in_specs=[a_spec, b_spec] + [row if t.shape[0] == 1 else tile for t in extra],
        out_specs=[out_tile] * n_tiles + [row] * n_colsum,
        out_shape=[_sds(out_dims, dt) for dt in out_dtypes] + [_sds((1, n), F32)] * n_colsum,
        scratch_shapes=[] if nk == 1 else [pltpu.VMEM((tm, tn), F32)],
        args=(a, b, *extra), after=after)
    return res if isinstance(out_dtype, tuple) or n_colsum else res[0]


def _rstd(h):
    return lax.rsqrt(jnp.mean(h * h, axis=-1, keepdims=True) + RMS_EPS)


def _sigmoid(z):
    return 1.0 / (1.0 + jnp.exp(-z))


def _rms_fwd(x, g, *, tm, name):
    n = x.shape[0]

    def body(x_ref, g_ref, o_ref):
        h = x_ref[...]
        o_ref[...] = (h * _rstd(h) * g_ref[...]).astype(BF16)

    return pl.pallas_call(
        body, name=name, grid=(n // tm,),
        in_specs=[_rows(tm, D_MODEL), _const((1, D_MODEL))],
        out_specs=_rows(tm, D_MODEL), out_shape=_sds((n, D_MODEL), BF16),
        compiler_params=_params("parallel"),
    )(x, g)


def _swap_halves(t):
    width = t.shape[1]
    lane = lax.broadcasted_iota(jnp.int32, t.shape, 1)
    return jnp.where((lane & 63) < 32, pltpu.roll(t, width - 32, 1), pltpu.roll(t, 32, 1))


def _dil_spec(dil, tm):
    return pl.BlockSpec((dil, tm // dil, 256), lambda i: (0, i, 0))


def _dil_scratch(tm):
    return pltpu.VMEM((2, tm, 128), F32)


def _load_token_order(src, scr, dil, tm):
    if dil == 1:
        return src[0]
    for j in range(dil):
        for c in range(2):
            scr[c, pl.ds(j, tm // dil, stride=dil), :] = src[j, :, c * 128:(c + 1) * 128]
    return jnp.concatenate([scr[0], scr[1]], axis=1)


def _store_dil_order(val, dst, scr, dil, tm):
    if dil == 1:
        dst[0] = val.astype(dst.dtype)
        return
    for c in range(2):
        scr[c] = val[:, c * 128:(c + 1) * 128]
    for j in range(dil):
        for c in range(2):
            dst[j, :, c * 128:(c + 1) * 128] = scr[c, pl.ds(j, tm // dil, stride=dil), :].astype(dst.dtype)


def _split_proj(proj, cos_t, sin_t, *, tm, name):
    n = proj.shape[0]
    n_dil = len(DIL_DILATIONS)

    def body(*refs):
        na_in = refs[0:3]
        dil_in = refs[3:3 + 3 * n_dil]
        gate_in = refs[12:20]
        cos_ref, sin_ref = refs[20:22]
        outs = refs[22:]
        na_out = outs[0:3]
        dil_out = outs[3:12]
        sn_ref, sd_ref = outs[12:14]
        scr = outs[14]
        for t in range(3):
            na_out[t][...] = na_in[t][...].astype(BF16)
        cosv, sinv = cos_ref[...], sin_ref[...]
        for t in range(3):
            for gi, dil in enumerate(DIL_DILATIONS):
                val = dil_in[t * n_dil + gi][...]
                if t < 2:
                    val = val * cosv + _swap_halves(val) * sinv
                _store_dil_order(val, dil_out[t * n_dil + gi], scr, dil, tm)
        for c in range(4):
            sn_ref[:, c * 256:(c + 1) * 256] = _sigmoid(gate_in[c][...])
            sd_ref[:, c * 256:(c + 1) * 256] = _sigmoid(gate_in[4 + c][...])

    in_specs = [_rows(tm, NA_WIDTH, c) for c in range(3)]
    in_specs += [_rows(tm, 256, 6 + c) for c in range(9)]
    in_specs += [_rows(tm, 256, 15 + c) for c in range(8)]
    in_specs += [_rows(tm, 256), _rows(tm, 256)]
    out_specs = [_rows(tm, NA_WIDTH)] * 3
    out_shape = [_sds((n, NA_WIDTH), BF16)] * 3
    for _ in range(3):
        for dil in DIL_DILATIONS:
            out_specs.append(pl.BlockSpec((dil, tm // dil, 256), lambda i: (0, i, 0)))
            out_shape.append(_sds((dil, n // dil, 256), BF16))
    out_specs += [_rows(tm, D_MODEL)] * 2
    out_shape += [_sds((n, D_MODEL), F32)] * 2
    res = pl.pallas_call(
        body, name=name, grid=(n // tm,),
        in_specs=in_specs, out_specs=out_specs, out_shape=out_shape,
        scratch_shapes=[_dil_scratch(tm)],
        compiler_params=_params("parallel"),
    )(*([proj] * 20), cos_t, sin_t)
    return res[0:3], res[3:6], res[6:9], res[9:12], res[12], res[13]


def _residual_rms_tile(delta, h, g):
    hn = h + delta
    return hn, hn * _rstd(hn) * g


def _gate_bwd_tile(dm, s1, b1, s2, b2):
    return dm * s1, dm * s2, dm * b1 * s1 * (1.0 - s1), dm * b2 * s2 * (1.0 - s2)


def _tail_tile(gt, pp, h2, target, g):
    sg = _sigmoid(gt)
    h3 = h2 + sg * pp
    r3 = _rstd(h3)
    n3 = h3 * r3
    err = n3 * g - target
    loss = 0.5 * jnp.sum(jnp.sum(err * err, axis=-1, keepdims=True) / D_MODEL)
    dy = err / D_MODEL
    dn = dy * g
    dh3 = r3 * (dn - n3 * jnp.mean(dn * n3, axis=-1, keepdims=True))
    return (dh3, dh3 * sg, dh3 * pp * sg * (1.0 - sg),
            jnp.sum(dy * n3, axis=0, keepdims=True), jnp.full((1, gt.shape[1]), loss, F32))


def _rms_bwd_tile(dz, h, g, dres):
    r = _rstd(h)
    nrm = h * r
    dn = dz * g
    dh = dres + r * (dn - nrm * jnp.mean(dn * nrm, axis=-1, keepdims=True))
    return dh, jnp.sum(dz * nrm, axis=0, keepdims=True)


def _rms_bwd_twice(dz, h, g, dres):
    dh, dg = _rms_bwd_tile(dz, h, g, dres)
    return dh, dh, dg


def _assemble_dproj(dna, ddil_q, ddil_k, ddil_v, dgn, dgd, cos_t, sin_t, *, tm, name):
    n = dgn.shape[0]

    def body(*refs):
        dq_ref, dk_ref, dv_ref = refs[0:3]
        dil_in = refs[3:12]
        dgn_ref, dgd_ref, cos_ref, sin_ref, o_ref, scr = refs[12:18]
        o_ref[:, 0:512] = dq_ref[...]
        o_ref[:, 512:1024] = dk_ref[...].astype(BF16)
        o_ref[:, 1024:1536] = dv_ref[...].astype(BF16)
        cosv, sinv = cos_ref[...], sin_ref[...]
        for t in range(3):
            for gi, dil in enumerate(DIL_DILATIONS):
                val = _load_token_order(dil_in[t * 3 + gi], scr, dil, tm)
                if t < 2:
                    val = val * cosv + _swap_halves(val * sinv)
                c0 = 1536 + t * DIL_WIDTH + gi * 256
                o_ref[:, c0:c0 + 256] = val.astype(BF16)
        o_ref[:, 3840:4864] = dgn_ref[...]
        o_ref[:, 4864:5888] = dgd_ref[...]

    in_specs = [_rows(tm, NA_WIDTH)] * 3
    for _ in range(3):
        for dil in DIL_DILATIONS:
            in_specs.append(pl.BlockSpec((dil, tm // dil, 256), lambda i: (0, i, 0)))
    in_specs += [_rows(tm, D_MODEL)] * 2 + [_rows(tm, 256)] * 2
    return pl.pallas_call(
        body, name=name, grid=(n // tm,), in_specs=in_specs,
        out_specs=_rows(tm, IN_WIDTH), out_shape=_sds((n, IN_WIDTH), BF16),
        scratch_shapes=[_dil_scratch(tm)],
        compiler_params=_params("parallel"),
    )(*dna, *ddil_q, *ddil_k, *ddil_v, dgn, dgd, cos_t, sin_t)


N_ROW_OFF = 2 * NA_WIN_ROWS - 1
N_PAIRS = N_ROW_OFF - 1
RB_WIDTH = (N_ROW_OFF + 1) * GRID_W


def _na_bias(rb_ref, pair_scr):
    shape = (GRID_W, RB_WIDTH)
    qc = lax.broadcasted_iota(jnp.int32, shape, 0)
    qc2 = lax.broadcasted_iota(jnp.int32, (GRID_W, 128), 0)
    kc2 = lax.broadcasted_iota(jnp.int32, (GRID_W, 128), 1) & (GRID_W - 1)
    cs = jnp.clip(qc2 - 8, 0, GRID_W - 16)
    valid = (kc2 >= cs) & (kc2 < cs + 16)
    for hh in range(2):
        t = jnp.broadcast_to(rb_ref[hh], shape)
        t = pltpu.roll(t, RB_WIDTH - 15, 1)
        for b in range(6):
            t = jnp.where(((qc >> b) & 1) == 1, pltpu.roll(t, 1 << b, 1), t)
        t_odd = pltpu.roll(t, RB_WIDTH - GRID_W, 1)
        for ro in range(N_PAIRS):
            src = t if ro % 2 == 0 else t_odd
            base = (ro // 2) * 128
            pair_scr[hh, ro] = jnp.where(valid, src[:, base:base + 128], NEG_INF)


NA_GROUP_FWD = 4
NA_GROUP_BWD = 4


def _stack_heads(ref, r, scale=1.0):
    lane = lax.broadcasted_iota(jnp.int32, (GRID_W, 128), 1)
    t = ref[pl.ds(pl.multiple_of(r * GRID_W, GRID_W), GRID_W), :].astype(F32) * scale
    return jnp.concatenate([jnp.where(lane < 64, t, 0.0), jnp.where(lane >= 64, t, 0.0)], axis=0).astype(BF16)


def _unstack_heads(t2):
    lane = lax.broadcasted_iota(jnp.int32, (GRID_W, 128), 1)
    return jnp.where(lane < 64, t2[:GRID_W], t2[GRID_W:])


def _na_window(k_ref, v_ref, r, n_rows):
    rs = jnp.clip(r - NA_WIN_ROWS // 2, 0, n_rows - NA_WIN_ROWS)
    ro0 = (NA_WIN_ROWS - 1) - (r - rs)
    off = pl.multiple_of(rs * GRID_W, GRID_W)
    kw = k_ref[pl.ds(off, NA_WIN_ROWS * GRID_W), :]
    vw = v_ref[pl.ds(off, NA_WIN_ROWS * GRID_W), :]
    return kw, vw, off, ro0


def _na_probs(s_raw, pair_scr, ro0):
    bias = [jnp.concatenate([pair_scr[hh, ro0 + 2 * j] for j in range(NA_WIN_ROWS // 2)], axis=1)
            for hh in range(2)]
    s = s_raw + jnp.concatenate(bias, axis=0)
    m = jnp.max(s, axis=-1, keepdims=True)
    e = jnp.exp(s - m)
    return e * (1.0 / jnp.sum(e, axis=-1, keepdims=True))


def _na_fwd(q, k, v, rb, *, name):
    n = q.shape[0]
    n_rows = n // GRID_W

    def body(ins, outs, scr):
        q_ref, k_ref, v_ref, rb_ref = ins
        o_ref, = outs
        pair_scr, = scr
        _na_bias(rb_ref, pair_scr)

        def group(g, carry):
            rows = [g * NA_GROUP_FWD + t for t in range(NA_GROUP_FWD)]
            wins = [_na_window(k_ref, v_ref, r, n_rows) for r in rows]
            raw = [lax.dot_general(_stack_heads(q_ref, r, QK_SCALE), w[0], NT_DIMS, preferred_element_type=F32)
                   for r, w in zip(rows, wins)]
            probs = [_na_probs(s, pair_scr, w[3]) for s, w in zip(raw, wins)]
            outs2 = [jnp.dot(p.astype(BF16), w[1], preferred_element_type=F32) for p, w in zip(probs, wins)]
            for r, o2 in zip(rows, outs2):
                o_ref[pl.ds(pl.multiple_of(r * GRID_W, GRID_W), GRID_W), :] = _unstack_heads(o2).astype(BF16)
            return carry

        lax.fori_loop(0, n_rows // NA_GROUP_FWD, group, 0)

    col = pl.BlockSpec((n, 128), lambda h: (0, h))
    return _call(
        body, name=name, grid=(NA_WIDTH // 128,),
        in_specs=[col, col, col, pl.BlockSpec((2, 1, RB_WIDTH), lambda h: (h, 0, 0))],
        out_specs=[col], out_shape=[_sds((n, NA_WIDTH), BF16)],
        scratch_shapes=[pltpu.VMEM((2, N_PAIRS, GRID_W, 128), F32)],
        args=(q, k, v, rb))[0]


def _na_bwd(q, k, v, do, rb, *, name):
    n = q.shape[0]
    n_rows = n // GRID_W
    win = NA_WIN_ROWS * GRID_W

    def body(ins, outs, scr):
        q_ref, k_ref, v_ref, do_ref, rb_ref = ins
        dq_ref, dk_ref, dv_ref, drb_ref = outs
        pair_scr, acc_scr = scr
        _na_bias(rb_ref, pair_scr)
        acc_scr[...] = jnp.zeros_like(acc_scr)
        dk_ref[...] = jnp.zeros_like(dk_ref)
        dv_ref[...] = jnp.zeros_like(dv_ref)

        def group(g, carry):
            rows = [g * NA_GROUP_BWD + t for t in range(NA_GROUP_BWD)]
            wins = [_na_window(k_ref, v_ref, r, n_rows) for r in rows]
            qss = [_stack_heads(q_ref, r, QK_SCALE) for r in rows]
            doss = [_stack_heads(do_ref, r) for r in rows]
            raw = [lax.dot_general(qs, w[0], NT_DIMS, preferred_element_type=F32) for qs, w in zip(qss, wins)]
            dps = [lax.dot_general(dos, w[1], NT_DIMS, preferred_element_type=F32) for dos, w in zip(doss, wins)]
            probs = [_na_probs(s, pair_scr, w[3]) for s, w in zip(raw, wins)]
            dss = [p * (dp - jnp.sum(p * dp, axis=-1, keepdims=True)) for p, dp in zip(probs, dps)]
            dsbs = [ds.astype(BF16) for ds in dss]
            dq2s = [jnp.dot(dsb, w[0], preferred_element_type=F32) for dsb, w in zip(dsbs, wins)]
            dkws = [lax.dot_general(dsb, qs, TN_DIMS, preferred_element_type=F32) for dsb, qs in zip(dsbs, qss)]
            dvws = [lax.dot_general(p.astype(BF16), dos, TN_DIMS, preferred_element_type=F32)
                    for p, dos in zip(probs, doss)]
            for t, r in enumerate(rows):
                _, _, off, ro0 = wins[t]
                for hh in range(2):
                    for j in range(NA_WIN_ROWS // 2):
                        acc_scr[hh, ro0 + 2 * j] += dss[t][hh * GRID_W:(hh + 1) * GRID_W, j * 128:(j + 1) * 128]
                dq_ref[pl.ds(pl.multiple_of(r * GRID_W, GRID_W), GRID_W), :] = (
                    _unstack_heads(dq2s[t]) * QK_SCALE).astype(BF16)
                dk_ref[pl.ds(off, win), :] += dkws[t]
                dv_ref[pl.ds(off, win), :] += dvws[t]
            return carry

        lax.fori_loop(0, n_rows // NA_GROUP_BWD, group, 0)

        qc = lax.broadcasted_iota(jnp.int32, (N_PAIRS * GRID_W, 128), 0)
        for hh in range(2):
            t = acc_scr[hh].reshape(N_PAIRS * GRID_W, 128)
            for b in range(6):
                t = jnp.where(((qc >> b) & 1) == 1, pltpu.roll(t, 128 - (1 << b), 1), t)
            t = pltpu.roll(t, 15, 1)
            drb_ref[hh] = jnp.sum(t.reshape(N_PAIRS, GRID_W, 128), axis=1)

    col = pl.BlockSpec((n, 128), lambda h: (0, h))
    return _call(
        body, name=name, grid=(NA_WIDTH // 128,),
        in_specs=[col, col, col, col, pl.BlockSpec((2, 1, RB_WIDTH), lambda h: (h, 0, 0))],
        out_specs=[col, col, col, pl.BlockSpec((2, N_PAIRS, 128), lambda h: (h, 0, 0))],
        out_shape=[_sds((n, NA_WIDTH), BF16), _sds((n, NA_WIDTH), F32), _sds((n, NA_WIDTH), F32),
                   _sds((8, N_PAIRS, 128), F32)],
        scratch_shapes=[pltpu.VMEM((2, N_PAIRS, GRID_W, 128), F32),
                        pltpu.VMEM((2, N_PAIRS, GRID_W, 128), F32)],
        args=(q, k, v, do, rb))


def _rpb_table(rpb2):
    t = jnp.pad(rpb2, ((0, 0), (0, 1), (0, GRID_W - rpb2.shape[-1])))
    return t.reshape(8, 1, RB_WIDTH)


def _rpb_grad(drb, *, name):
    kdim = drb.shape[1]

    def body(x_ref, o_ref):
        kk = lax.broadcasted_iota(jnp.int32, (128, 512), 0)
        jj = lax.broadcasted_iota(jnp.int32, (128, 512), 1)
        half, co = kk >> 6, kk & 63
        acc = jnp.zeros((8, 512), F32)
        for ro in range(N_PAIRS):
            hit = ((ro + half) == (jj >> 5)) & (co == (jj & 31)) & (co < 31)
            onehot = jnp.where(hit, 1.0, 0.0).astype(F32)
            acc = acc + jnp.dot(x_ref[:, ro * 128:(ro + 1) * 128], onehot, preferred_element_type=F32,
                                precision=lax.Precision.HIGHEST)
        o_ref[...] = acc

    return pl.pallas_call(
        body, name=name, grid=(1,),
        in_specs=[_const((8, kdim))], out_specs=_const((8, 512)), out_shape=_sds((8, 512), F32),
        compiler_params=_params("arbitrary"),
    )(drb)


DIL_GROUP = 2


def _dil_blocks(length):
    qb = min(128, length)
    return qb, min(qb + 2 * DIL_RADIUS, length), min(DIL_GROUP, length // qb)


def _stack_lanes(ref, t, qb, scale=1.0):
    lane = lax.broadcasted_iota(jnp.int32, (qb, 256), 1)
    val = ref[0, t * qb:(t + 1) * qb, :].astype(F32) * scale
    return jnp.concatenate([jnp.where((lane >> 6) == h, val, 0.0) for h in range(4)], axis=0).astype(BF16)


def _dil_window(k_ref, v_ref, blk, qb, win, length):
    start = pl.multiple_of(jnp.clip(blk * qb - DIL_RADIUS, 0, length - win), DIL_RADIUS)
    return k_ref[0, pl.ds(start, win), :], v_ref[0, pl.ds(start, win), :], start


def _dil_mask(s, blk, start, qb, win):
    gap = ((lax.broadcasted_iota(jnp.int32, (4 * qb, win), 0) & (qb - 1))
           - lax.broadcasted_iota(jnp.int32, (4 * qb, win), 1)) + (blk * qb - start)
    return jnp.where(jnp.abs(gap) <= DIL_RADIUS, s, NEG_INF)


def _pick_heads(stacked, qb):
    lane = lax.broadcasted_iota(jnp.int32, (qb, 256), 1)
    out = jnp.zeros((qb, 256), stacked.dtype)
    for h in range(4):
        out = jnp.where((lane >> 6) == h, stacked[h * qb:(h + 1) * qb], out)
    return out


def _stack_head_cols(ref, t, qb):
    return jnp.concatenate([ref[0, t * qb:(t + 1) * qb, 64 * h:64 * h + 1] for h in range(4)], axis=0)


def _dil_fwd(q, k, v, *, name):
    dil, length, _ = q.shape
    qb, win, grp = _dil_blocks(length)

    def body(q_ref, k_ref, v_ref, o_ref, lse_ref):
        blks = [pl.program_id(1) * grp + t for t in range(grp)]
        wins = [_dil_window(k_ref, v_ref, b, qb, win, length) for b in blks]
        raw = [lax.dot_general(_stack_lanes(q_ref, t, qb, QK_SCALE), w[0], NT_DIMS, preferred_element_type=F32)
               for t, w in enumerate(wins)]
        lses, outs = [], []
        for t, (s, w) in enumerate(zip(raw, wins)):
            s = _dil_mask(s, blks[t], w[2], qb, win)
            m = jnp.max(s, axis=-1, keepdims=True)
            e = jnp.exp(s - m)
            norm = jnp.sum(e, axis=-1, keepdims=True)
            lses.append(m + jnp.log(norm))
            outs.append(jnp.dot((e * (1.0 / norm)).astype(BF16), w[1], preferred_element_type=F32))
        for t in range(grp):
            o_ref[0, t * qb:(t + 1) * qb, :] = _pick_heads(outs[t], qb)
            lse_ref[0, t * qb:(t + 1) * qb, :] = _pick_heads(jnp.broadcast_to(lses[t], (4 * qb, 256)), qb)

    seq = pl.BlockSpec((1, length, 256), lambda j, i: (j, 0, 0))
    blk = pl.BlockSpec((1, grp * qb, 256), lambda j, i: (j, i, 0))
    return pl.pallas_call(
        body, name=name, grid=(dil, length // (grp * qb)),
        in_specs=[blk, seq, seq], out_specs=[blk, blk],
        out_shape=[_sds((dil, length, 256), F32)] * 2,
        compiler_params=_params("parallel", "parallel"),
    )(q, k, v)


def _dil_bwd(q, k, v, do, lse, cc, *, name):
    dil, length, _ = q.shape
    qb, win, grp = _dil_blocks(length)

    def body(q_ref, k_ref, v_ref, do_ref, lse_ref, cc_ref, dq_ref, dk_ref, dv_ref):
        @pl.when(pl.program_id(1) == 0)
        def _():
            dk_ref[...] = jnp.zeros_like(dk_ref)
            dv_ref[...] = jnp.zeros_like(dv_ref)

        blks = [pl.program_id(1) * grp + t for t in range(grp)]
        wins = [_dil_window(k_ref, v_ref, b, qb, win, length) for b in blks]
        qss = [_stack_lanes(q_ref, t, qb, QK_SCALE) for t in range(grp)]
        doss = [_stack_lanes(do_ref, t, qb) for t in range(grp)]
        raw = [lax.dot_general(qs, w[0], NT_DIMS, preferred_element_type=F32) for qs, w in zip(qss, wins)]
        dps = [lax.dot_general(dos, w[1], NT_DIMS, preferred_element_type=F32) for dos, w in zip(doss, wins)]
        probs = [jnp.exp(_dil_mask(s, blks[t], wins[t][2], qb, win) - _stack_head_cols(lse_ref, t, qb))
                 for t, s in enumerate(raw)]
        dsbs = [(p * (dp + _stack_head_cols(cc_ref, t, qb))).astype(BF16)
                for t, (p, dp) in enumerate(zip(probs, dps))]
        dq4s = [jnp.dot(dsb, w[0], preferred_element_type=F32) for dsb, w in zip(dsbs, wins)]
        dkws = [lax.dot_general(dsb, qs, TN_DIMS, preferred_element_type=F32) for dsb, qs in zip(dsbs, qss)]
        dvws = [lax.dot_general(p.astype(BF16), dos, TN_DIMS, preferred_element_type=F32)
                for p, dos in zip(probs, doss)]
        for t in range(grp):
            dq_ref[0, t * qb:(t + 1) * qb, :] = _pick_heads(dq4s[t], qb) * QK_SCALE
            dk_ref[0, pl.ds(wins[t][2], win), :] += dkws[t]
            dv_ref[0, pl.ds(wins[t][2], win), :] += dvws[t]

    seq = pl.BlockSpec((1, length, 256), lambda j, i: (j, 0, 0))
    blk = pl.BlockSpec((1, grp * qb, 256), lambda j, i: (j, i, 0))
    return pl.pallas_call(
        body, name=name, grid=(dil, length // (grp * qb)),
        in_specs=[blk, seq, seq, blk, blk, blk], out_specs=[blk, seq, seq],
        out_shape=[_sds((dil, length, 256), F32)] * 3,
        compiler_params=_params("parallel", "arbitrary"),
    )(q, k, v, do, lse, cc)


def _merge_weights(lses):
    m = jnp.maximum(jnp.maximum(lses[0], lses[1]), lses[2])
    es = [jnp.exp(t - m) for t in lses]
    inv = 1.0 / (es[0] + es[1] + es[2])
    return [e * inv for e in es]


def _dil_merge(outs, lses, *, tm, name):
    n = outs[0].shape[1]

    def body(*refs):
        o_in, l_in = refs[0:3], refs[3:6]
        y_ref, yb_ref, scr = refs[6:9]
        lv = [_load_token_order(l_in[g], scr, d, tm) for g, d in enumerate(DIL_DILATIONS)]
        ws = _merge_weights(lv)
        y = jnp.zeros((tm, 256), F32)
        for g, d in enumerate(DIL_DILATIONS):
            y = y + ws[g] * _load_token_order(o_in[g], scr, d, tm)
        y_ref[...] = y
        yb_ref[...] = y.astype(BF16)

    specs = [_dil_spec(d, tm) for d in DIL_DILATIONS]
    return pl.pallas_call(
        body, name=name, grid=(n // tm,), in_specs=specs + specs,
        out_specs=[_rows(tm, 256)] * 2, out_shape=[_sds((n, 256), F32), _sds((n, 256), BF16)],
        scratch_shapes=[_dil_scratch(tm)],
        compiler_params=_params("parallel"),
    )(*outs, *lses)


def _dil_merge_bwd(dy, y, lses, *, tm, name):
    n = dy.shape[0]

    def body(*refs):
        dy_ref, y_ref = refs[0:2]
        l_in = refs[2:5]
        do_out, cc_out = refs[5:8], refs[8:11]
        scr = refs[11]
        lv = [_load_token_order(l_in[g], scr, d, tm) for g, d in enumerate(DIL_DILATIONS)]
        ws = _merge_weights(lv)
        dyv = dy_ref[...]
        rr = lax.broadcasted_iota(jnp.int32, (256, 256), 0) >> 6
        cc = lax.broadcasted_iota(jnp.int32, (256, 256), 1) >> 6
        ones = jnp.where(rr == cc, 1.0, 0.0).astype(F32)
        tsum = jnp.dot(dyv * y_ref[...], ones, preferred_element_type=F32,
                       precision=lax.Precision.HIGHEST)
        for g, d in enumerate(DIL_DILATIONS):
            _store_dil_order(ws[g] * dyv, do_out[g], scr, d, tm)
            _store_dil_order(-ws[g] * tsum, cc_out[g], scr, d, tm)

    specs = [_dil_spec(d, tm) for d in DIL_DILATIONS]
    res = pl.pallas_call(
        body, name=name, grid=(n // tm,),
        in_specs=[_rows(tm, 256)] * 2 + specs,
        out_specs=specs + specs,
        out_shape=[_sds((d, n // d, 256), BF16) for d in DIL_DILATIONS]
                  + [_sds((d, n // d, 256), F32) for d in DIL_DILATIONS],
        scratch_shapes=[_dil_scratch(tm)],
        compiler_params=_params("parallel"),
    )(dy, y, *lses)
    return res[0:3], res[3:6]


_WEIGHTS = (("w_in", 1, 736), ("w_branch_na", 1, 128), ("w_branch_dil", 1, 128), ("w_out", 0, 128),
            ("w_up", 1, 512), ("w_down", 0, 512), ("w_ple_gate", 0, 128), ("w_ple_proj", 1, 128))
_W_IN, _W_BNA, _W_BD, _W_OUT, _W_UP, _W_DOWN, _W_PG, _W_PP = range(8)


def _to_full(gathered):
    return gathered.reshape(-1, gathered.shape[2])


def _to_chunks(widx, mat):
    return mat.reshape(N_DEV, _WEIGHTS[widx][2], mat.shape[1])


def _local_step(x, p_bf16, positions, target, g_mix, g_mlp, g_ple, g_final, rpb2, get_w_in, get_rest, send_grads):
    tm = 256
    half = HEAD_DIM // 2
    inv_freq = 10000.0 ** (-jnp.arange(half, dtype=F32) / half)
    ang = positions.astype(F32)[:, None] * inv_freq
    cos, sin = jnp.cos(ang), jnp.sin(ang)
    cos_t = jnp.tile(jnp.concatenate([cos, cos], axis=-1), (1, 4))
    sin_t = jnp.tile(jnp.concatenate([-sin, sin], axis=-1), (1, 4))
    rb = _rpb_table(rpb2)

    a = _rms_fwd(x, g_mix, tm=tm, name="rms_mix")
    w_in, token = get_w_in(a)
    proj = _matmul(a, w_in, tb=True, out_dtype=F32, tm=512, tn=2944, tk=1024, name="mm_in", after=token)
    na_qkv, dq_g, dk_g, dv_g, sn, sd = _split_proj(proj, cos_t, sin_t, tm=tm, name="split_proj")
    y_na = _na_fwd(*na_qkv, rb, name="na_fwd")
    d_out, d_lse = [], []
    for g in range(3):
        o, lse = _dil_fwd(dq_g[g], dk_g[g], dv_g[g], name=f"dil_fwd{g}")
        d_out.append(o)
        d_lse.append(lse)
    y_dil, y_dil_b = _dil_merge(d_out, d_lse, tm=tm, name="dil_merge")
    w_bna, w_bd, w_out, w_up, w_down, w_pg, w_pp = get_rest(y_dil_b)
    bn = _matmul(y_na, w_bna, tb=True, out_dtype=F32, tm=512, tn=1024, tk=512, name="mm_bna")
    bd, mixed = _matmul(y_dil_b, w_bd, tb=True, out_dtype=(F32, BF16), tm=512, tn=1024, tk=256, name="mm_bd",
                        extra=(sn, bn, sd), epilogue=lambda acc, s1, b1, s2: (acc, s1 * b1 + s2 * acc))
    h1, c = _matmul(mixed, w_out, out_dtype=(F32, BF16), tm=512, tn=1024, tk=1024, name="mm_out",
                    extra=(x, g_mlp), epilogue=_residual_rms_tile)
    u, f = _matmul(c, w_up, tb=True, out_dtype=(F32, BF16), tm=512, tn=2048, tk=1024, name="mm_up",
                   epilogue=lambda acc: (acc, jnp.square(jnp.maximum(acc, 0.0))))
    h2, e = _matmul(f, w_down, out_dtype=(F32, BF16), tm=512, tn=1024, tk=4096, name="mm_down",
                    extra=(h1, g_ple), epilogue=_residual_rms_tile)
    pp = _matmul(p_bf16, w_pp, tb=True, out_dtype=F32, tm=512, tn=1024, tk=256, name="mm_pp")

    dh3, dpp, dgt, dg_final, loss = _matmul(
        e, w_pg, out_dtype=(F32, BF16, BF16), tm=512, tn=1024, tk=1024, name="mm_pg_tail",
        extra=(pp, h2, target, g_final), epilogue=_tail_tile, n_colsum=2)
    loss = loss[:, :128]
    gw_pp = _matmul(p_bf16, dpp, ta=True, transpose_out=True, out_dtype=BF16, tm=256, tn=1024, tk=2048,
                    name="mm_gw_pp")
    gw_pg = _matmul(e, dgt, ta=True, out_dtype=BF16, tm=512, tn=1024, tk=2048, name="mm_gw_pg")
    dh2, dh2_b, dg_ple = _matmul(
        dgt, w_pg, tb=True, out_dtype=(F32, BF16), tm=512, tn=1024, tk=1024, name="mm_de",
        extra=(h2, g_ple, dh3), epilogue=_rms_bwd_twice, n_colsum=1)
    du = _matmul(dh2_b, w_down, tb=True, out_dtype=BF16, tm=512, tn=2048, tk=1024, name="mm_du",
                 extra=(u,), epilogue=lambda acc, uv: (acc * (2.0 * jnp.maximum(uv, 0.0)),))
    gw_down = _matmul(f, dh2_b, ta=True, out_dtype=BF16, tm=1024, tn=1024, tk=2048, name="mm_gw_down")
    token = send_grads((_W_PP, _W_PG, _W_DOWN), (gw_pp, gw_pg, gw_down))
    gw_up = _matmul(c, du, ta=True, transpose_out=True, out_dtype=BF16, tm=512, tn=2048, tk=2048, name="mm_gw_up",
                    after=token)
    token = send_grads((_W_UP,), (gw_up,))
    dh1, dh1_b, dg_mlp = _matmul(
        du, w_up, out_dtype=(F32, BF16), tm=512, tn=1024, tk=4096, name="mm_dc", after=token,
        extra=(h1, g_mlp, dh2), epilogue=_rms_bwd_twice, n_colsum=1)
    dbn, dbd, dgn, dgd = _matmul(dh1_b, w_out, tb=True, out_dtype=(BF16,) * 4, tm=512, tn=1024, tk=1024,
                                 name="mm_dmixed", extra=(sn, bn, sd, bd), epilogue=_gate_bwd_tile)
    gw_out = _matmul(mixed, dh1_b, ta=True, out_dtype=BF16, tm=512, tn=1024, tk=2048, name="mm_gw_out")
    gw_bna = _matmul(y_na, dbn, ta=True, transpose_out=True, out_dtype=BF16, tm=512, tn=1024, tk=2048,
                     name="mm_gw_bna")
    dy_na = _matmul(dbn, w_bna, out_dtype=BF16, tm=512, tn=512, tk=1024, name="mm_dy_na")
    gw_bd = _matmul(y_dil_b, dbd, ta=True, transpose_out=True, out_dtype=BF16, tm=256, tn=1024, tk=2048,
                    name="mm_gw_bd")
    token = send_grads((_W_OUT, _W_BNA, _W_BD), (gw_out, gw_bna, gw_bd))
    dy_dil = _matmul(dbd, w_bd, out_dtype=F32, tm=512, tn=256, tk=1024, name="mm_dy_dil", after=token)
    dna = _na_bwd(*na_qkv, dy_na, rb, name="na_bwd")
    drpb = _rpb_grad(dna[3].reshape(8, -1), name="rpb_grad")
    do_g, cc_g = _dil_merge_bwd(dy_dil, y_dil, d_lse, tm=tm, name="dil_merge_bwd")
    ddq, ddk, ddv = [], [], []
    for g in range(3):
        r = _dil_bwd(dq_g[g], dk_g[g], dv_g[g], do_g[g], d_lse[g], cc_g[g], name=f"dil_bwd{g}")
        ddq.append(r[0])
        ddk.append(r[1])
        ddv.append(r[2])
    dproj = _assemble_dproj(dna[0:3], ddq, ddk, ddv, dgn, dgd, cos_t, sin_t, tm=tm, name="assemble_dproj")
    gw_in = _matmul(a, dproj, ta=True, transpose_out=True, out_dtype=BF16, tm=512, tn=2944, tk=2048, name="mm_gw_in")
    token = send_grads((_W_IN,), (gw_in,))
    dx, dg_mix = _matmul(
        dproj, w_in, out_dtype=(F32,), tm=512, tn=1024, tk=5888, name="mm_da", after=token,
        extra=(x, g_mix, dh1), epilogue=_rms_bwd_tile, n_colsum=1)
    return loss, dx, (dg_mix, dg_mlp, dg_ple, dg_final), drpb


def _cast_bf16(t, *, name):
    def body(t_ref, o_ref):
        o_ref[...] = t_ref[...].astype(BF16)

    rows, cols = t.shape
    tr = 256 if rows % 256 == 0 else rows
    blk = pl.BlockSpec((tr, cols), lambda i: (i, 0))
    return pl.pallas_call(body, name=name, grid=(rows // tr,), in_specs=[blk], out_specs=blk,
                          out_shape=_sds(t.shape, BF16), compiler_params=_params("parallel"))(t)


def _adamw(w, g, m, v):
    m = ADAM_B1 * m + (1.0 - ADAM_B1) * g
    v = ADAM_B2 * v + (1.0 - ADAM_B2) * (g * g)
    m_hat = m / (1.0 - ADAM_B1 ** ADAM_STEP)
    v_hat = v / (1.0 - ADAM_B2 ** ADAM_STEP)
    delta = -ADAM_LR * (m_hat / (jnp.sqrt(v_hat) + ADAM_EPS) + ADAM_WD * w)
    return delta, m, v


def _sum_adamw(parts, w, m, v, *, tr, name, own=None, transposed=False):
    rows, cols = w.shape

    def body(*refs):
        p_ref, w_ref, m_ref, v_ref = refs[:4]
        g_ref, d_ref, nm_ref, nv_ref = refs[-4:]
        g = (p_ref[0] if own is None else refs[4][...]).astype(F32)
        for s in range(1, N_DEV):
            g = g + p_ref[s].astype(F32)
        if transposed:
            g = g.T
        g_ref[...] = g
        d_ref[...], nm_ref[...], nv_ref[...] = _adamw(w_ref[...], g, m_ref[...], v_ref[...])

    extra = [] if own is None else [own]
    if transposed:
        blk = pl.BlockSpec((rows, tr), lambda i: (0, i))
        g_blk, p_blk, steps = pl.BlockSpec((tr, rows), lambda i: (i, 0)), (N_DEV, tr, rows), cols // tr
    else:
        blk = pl.BlockSpec((tr, cols), lambda i: (i, 0))
        g_blk, p_blk, steps = blk, (N_DEV, tr, cols), rows // tr
    return pl.pallas_call(
        body, name=name, grid=(steps,),
        in_specs=[pl.BlockSpec(p_blk, lambda i: (0, i, 0)), blk, blk, blk] + [g_blk] * len(extra),
        out_specs=[blk] * 4, out_shape=[_sds((rows, cols), F32)] * 4,
        compiler_params=_params("parallel"),
    )(parts, w, m, v, *extra)


_RPB_SIZE = 8 * 15 * 31


def _pack_small(g_mix, g_mlp, g_ple, g_final, rpb, loss_row):
    flat = jnp.concatenate([g_mix.reshape(-1), g_mlp.reshape(-1), g_ple.reshape(-1), g_final.reshape(-1),
                            rpb.reshape(-1), jnp.zeros((3840 - _RPB_SIZE,), F32), loss_row.reshape(-1),
                            jnp.zeros((128,), F32)])
    return flat.reshape(64, 128)


def _unpack_small(t):
    flat = t.reshape(-1)
    return (flat[0:1024].reshape(1, 1024), flat[4096:4096 + _RPB_SIZE].reshape(1, 8, 15, 31),
            flat[1024:2048].reshape(1, 1024), flat[2048:3072].reshape(1, 1024), flat[3072:4096])


def kernel(x, p, positions, g_mix, w_in, rpb, w_branch_na, w_branch_dil, w_out, g_mlp, w_up, w_down, g_ple, w_ple_gate, w_ple_proj, g_final, loss_target, m_g_mix, m_w_in, m_rpb, m_w_branch_na, m_w_branch_dil, m_w_out, m_g_mlp, m_w_up, m_w_down, m_g_ple, m_w_ple_gate, m_w_ple_proj, m_g_final, v_g_mix, v_w_in, v_rpb, v_w_branch_na, v_w_branch_dil, v_w_out, v_g_mlp, v_w_up, v_w_down, v_g_ple, v_w_ple_gate, v_w_ple_proj, v_g_final):
    sharded = dict(w_in=(w_in, m_w_in, v_w_in), w_branch_na=(w_branch_na, m_w_branch_na, v_w_branch_na),
                   w_branch_dil=(w_branch_dil, m_w_branch_dil, v_w_branch_dil), w_out=(w_out, m_w_out, v_w_out),
                   w_up=(w_up, m_w_up, v_w_up), w_down=(w_down, m_w_down, v_w_down),
                   w_ple_gate=(w_ple_gate, m_w_ple_gate, v_w_ple_gate),
                   w_ple_proj=(w_ple_proj, m_w_ple_proj, v_w_ple_proj))
    shards = {k: tuple(t[0] for t in val) for k, val in sharded.items()}

    me = _my_index()

    shards["w_in"] = tuple(t.T for t in shards["w_in"])

    w_in_b = _cast_bf16(shards["w_in"][0], name="cast_w_in")
    rest_b = [shards[name][0].astype(BF16).T if axis == 1 else shards[name][0].astype(BF16)
              for name, axis, _ in _WEIGHTS[1:]]
    gather_in, token_in = _start_copies(_gather_copies, [w_in_b], [_sds((N_DEV,) + w_in_b.shape, BF16)],
                                        name="start_gather_w_in")

    def whole(landed, mine):
        return _to_full(lax.dynamic_update_index_in_dim(landed, mine, me, 0))

    rest_handle = []

    def get_w_in(after):
        (mine,), (landed,) = _wait_copies(_gather_copies, gather_in, after, name="wait_gather_w_in")
        handle, token = _start_copies(_gather_copies, rest_b, [_sds((N_DEV,) + t.shape, BF16) for t in rest_b],
                                      name="start_gather_rest", after=landed)
        rest_handle.append(handle)
        return whole(landed, mine), token

    def get_rest(after):
        mine, landed = _wait_copies(_gather_copies, rest_handle[0], after, name="wait_gather_rest")
        return [whole(t, own) for t, own in zip(landed, mine)]

    sent = []

    def send_grads(indices, grads):
        chunked = [_to_chunks(i, g) for i, g in zip(indices, grads)]
        handle, token = _start_copies(_exchange_copies, chunked, [_sds(t.shape, BF16) for t in chunked],
                                      name="start_exchange_" + "_".join(_WEIGHTS[i][0] for i in indices))
        sent.append((indices, handle))
        return token

    g_mix_0 = g_mix + token_in[0:1, 0:1]
    loss, dx, dgs, drpb = _local_step(
        x[0], p[0, 0].astype(BF16), positions[0], loss_target[0],
        g_mix_0, g_mlp, g_ple, g_final.reshape(1, -1), rpb[0], get_w_in, get_rest, send_grads)

    drpb3 = drpb.reshape(8, 16, 32)[:, :15, :31]
    small = _pack_small(dgs[0], dgs[1], dgs[2], dgs[3], drpb3, loss)
    share, done = _start_copies(_gather_copies, [small], [_sds((N_DEV,) + small.shape, F32)],
                                name="start_share_small")

    out = {}
    for indices, handle in sent:
        chunked, landed = _wait_copies(_exchange_copies, handle, done,
                                       name="wait_exchange_" + "_".join(_WEIGHTS[i][0] for i in indices))
        for i, part, mine in zip(indices, landed, chunked):
            name = _WEIGHTS[i][0]
            w, m, v = shards[name]
            own = lax.dynamic_index_in_dim(mine, me, 0, keepdims=False)
            turned = _WEIGHTS[i][1] == 1 and i != _W_IN
            res = _sum_adamw(part, w, m, v, tr=368 if i == _W_IN else 128, name="adamw_" + name, own=own,
                             transposed=turned)
            out[name] = [(t.T if i == _W_IN else t)[None] for t in res]
            done = res[0]
    (small,), (small_landed,) = _wait_copies(_gather_copies, share, done, name="wait_share_small")
    small_all = lax.dynamic_update_index_in_dim(small_landed, small, me, 0)
    small_w = _pack_small(g_mix, g_mlp, g_ple, g_final, rpb, jnp.zeros((128,), F32))
    small_m = _pack_small(m_g_mix, m_g_mlp, m_g_ple, m_g_final, m_rpb, jnp.zeros((128,), F32))
    small_v = _pack_small(v_g_mix, v_g_mlp, v_g_ple, v_g_final, v_rpb, jnp.zeros((128,), F32))
    res = _sum_adamw(small_all, small_w, small_m, small_v, tr=64, name="adamw_small")
    unpacked = [_unpack_small(t) for t in res]
    for i, name in enumerate(("g_mix", "rpb", "g_mlp", "g_ple", "g_final")):
        out[name] = [u[i] for u in unpacked]
    loss_total = res[0][62, 0]

    order = ("g_mix", "w_in", "rpb", "w_branch_na", "w_branch_dil", "w_out", "g_mlp", "w_up", "w_down",
             "g_ple", "w_ple_gate", "w_ple_proj", "g_final")
    grads = [out[k][0] for k in order]
    deltas = [out[k][1] for k in order]
    new_m = [out[k][2] for k in order]
    new_v = [out[k][3] for k in order]
    return (loss_total, dx[None], *grads, *deltas, *new_m, *new_v)
```

```python
import jax
import jax.numpy as jnp
from jax import lax
from jax.experimental import pallas as pl
from jax.experimental.pallas import tpu as pltpu

F32 = jnp.float32
BF16 = jnp.bfloat16

D_MODEL = 1024
HEAD_DIM = 64
GRID_W = 64
NA_WIDTH = 512
DIL_WIDTH = 768
IN_WIDTH = 5888
DIL_DILATIONS = (1, 4, 16)
DIL_RADIUS = 64
NA_WIN_ROWS = 8
RMS_EPS = 1e-6
NEG_INF = -1e30
QK_SCALE = HEAD_DIM ** -0.5

ADAM_LR = 0.001
ADAM_B1 = 0.9
ADAM_B2 = 0.999
ADAM_EPS = 1e-08
ADAM_WD = 0.01
ADAM_STEP = 10

N_DEV = 8
VMEM_LIMIT = 56 * 1024 * 1024
EPILOGUE_ROWS = 256
MESH = pl.DeviceIdType.MESH

NT_DIMS = (((1,), (1,)), ((), ()))
TN_DIMS = (((0,), (0,)), ((), ()))


def _sds(shape, dtype):
    return jax.ShapeDtypeStruct(shape, dtype)


def _params(*sem):
    return pltpu.CompilerParams(dimension_semantics=sem, vmem_limit_bytes=VMEM_LIMIT)


def _rows(tm, width, col=0):
    return pl.BlockSpec((tm, width), lambda i, c=col: (i, c))


def _const(shape):
    zeros = (0,) * len(shape)
    return pl.BlockSpec(shape, lambda i: zeros)


def _my_index():
    return 4 * lax.axis_index("x") + 2 * lax.axis_index("y") + lax.axis_index("c")


def _peer(k):
    x, y, c = lax.axis_index("x"), lax.axis_index("y"), lax.axis_index("c")
    px = 1 - x if k & 4 else x
    py = 1 - y if k & 2 else y
    pc = 1 - c if k & 1 else c
    return (px, py, pc), 4 * px + 2 * py + pc


def _call(body, *, name, grid, in_specs, out_specs, out_shape, scratch_shapes, args, after=None):
    n_in, n_out = len(in_specs), len(out_specs)
    extra = [] if after is None else [after]
    n_x = n_in + len(extra)

    def plain(*refs):
        body(refs[:n_in], refs[n_x:n_x + n_out], refs[n_x + n_out:])

    res = pl.pallas_call(plain, name=name, grid=grid,
                         in_specs=list(in_specs) + [pl.BlockSpec(memory_space=pl.ANY)] * len(extra),
                         out_specs=out_specs, out_shape=out_shape, scratch_shapes=scratch_shapes,
                         compiler_params=_params(*(("arbitrary",) * len(grid))))(*args, *extra)
    return list(res)


_HBM_SPEC = pl.BlockSpec(memory_space=pltpu.HBM)
_SEM_SPEC = pl.BlockSpec(memory_space=pltpu.SEMAPHORE)
_SIDE_EFFECT = pltpu.SideEffectType.DATAFLOW_SIDE_EFFECTING


_FIRST_LEG = (1, 2, 4, 6)
_SECOND_LEG = (2, 4, 6)


def _gather_copies(srcs, lands, send, recv, sending):
    me = _my_index()
    out = []
    for w in range(len(srcs)):
        for k in range(1, N_DEV):
            dev, idx = _peer(k)
            out.append(pltpu.make_async_remote_copy(
                src_ref=srcs[w], dst_ref=lands[w].at[me if sending else idx],
                send_sem=send.at[w * 7 + k - 1], recv_sem=recv.at[w * 7 + k - 1],
                device_id=dev, device_id_type=MESH))
    return out


def _first_leg_copies(srcs, lands, send, recv, sending):
    me = _my_index()
    out = []
    for w in range(len(srcs)):
        for j, k in enumerate(_FIRST_LEG):
            dev, idx = _peer(k)
            out.append(pltpu.make_async_remote_copy(
                src_ref=srcs[w], dst_ref=lands[w].at[me if sending else idx],
                send_sem=send.at[w * 4 + j], recv_sem=recv.at[w * 4 + j],
                device_id=dev, device_id_type=MESH))
    return out


def _second_leg_copies(srcs, lands, send, recv, sending):
    sibling, _ = _peer(1)
    out = []
    for w in range(len(lands)):
        for j, k in enumerate(_SECOND_LEG):
            slot = _peer(k if sending else k ^ 1)[1]
            out.append(pltpu.make_async_remote_copy(
                src_ref=lands[w].at[slot], dst_ref=lands[w].at[slot],
                send_sem=send.at[w * 3 + j], recv_sem=recv.at[w * 3 + j],
                device_id=sibling, device_id_type=MESH))
    return out


def _exchange_copies(srcs, lands, send, recv, sending):
    out = []
    for w in range(len(srcs)):
        for k in range(1, N_DEV):
            dev, idx = _peer(k)
            out.append(pltpu.make_async_remote_copy(
                src_ref=srcs[w].at[idx], dst_ref=lands[w].at[k],
                send_sem=send.at[w * 7 + k - 1], recv_sem=recv.at[w * 7 + k - 1],
                device_id=dev, device_id_type=MESH))
    return out


def _start_copies(make, srcs, lands, n_copies, *, name, after=None):
    n_src, n_buf = len(srcs), len(srcs) + len(lands)
    extra = [] if after is None else [after]

    def body(*refs):
        send, recv = refs[n_buf + len(extra)], refs[n_buf + len(extra) + 1]
        for cp in make(refs[:n_src], refs[n_src:n_buf], send, recv, True):
            cp.start()
        refs[-1][...] = jnp.zeros_like(refs[-1])

    bufs = list(srcs) + [lax.empty(t.shape, t.dtype) if isinstance(t, jax.ShapeDtypeStruct) else t for t in lands]
    res = pl.pallas_call(
        body, name=name,
        out_shape=(pltpu.SemaphoreType.DMA((n_copies,)), pltpu.SemaphoreType.DMA((n_copies,)),
                   *[pltpu.HBM(t.shape, t.dtype) for t in bufs], _sds((8, 128), F32)),
        in_specs=[_HBM_SPEC] * n_buf + [pl.BlockSpec(memory_space=pl.ANY)] * len(extra),
        out_specs=(_SEM_SPEC, _SEM_SPEC, *([_HBM_SPEC] * n_buf), pl.BlockSpec(memory_space=pltpu.VMEM)),
        input_output_aliases={i: 2 + i for i in range(n_buf)},
        compiler_params=pltpu.CompilerParams(has_side_effects=_SIDE_EFFECT),
    )(*[pltpu.with_memory_space_constraint(t, pltpu.HBM) for t in bufs], *extra)
    return (n_src, res[0], res[1], res[2:2 + n_buf]), res[-1]


def _wait_copies(make, handle, after, *, name):
    n_src, send_sems, recv_sems, bufs = handle
    n_buf = len(bufs)

    def body(*refs):
        for cp in make(refs[:n_src], refs[n_src:n_buf], refs[n_buf], refs[n_buf + 1], False):
            cp.wait_send()
            cp.wait_recv()

    res = pl.pallas_call(
        body, name=name,
        out_shape=tuple(pltpu.HBM(t.shape, t.dtype) for t in bufs),
        in_specs=[_HBM_SPEC] * n_buf + [_SEM_SPEC, _SEM_SPEC, pl.BlockSpec(memory_space=pl.ANY)],
        out_specs=tuple([_HBM_SPEC] * n_buf),
        input_output_aliases={i: i for i in range(n_buf)},
        compiler_params=pltpu.CompilerParams(has_side_effects=_SIDE_EFFECT),
    )(*bufs, send_sems, recv_sems, after)
    return list(res[:n_src]), list(res[n_src:])


def _matmul(a, b, *, ta=False, tb=False, out_dtype, tm, tn, tk, name, after=None, extra=(), epilogue=None,
            n_colsum=0, transpose_out=False):
    m, k = (a.shape[1], a.shape[0]) if ta else a.shape
    n = b.shape[0] if tb else b.shape[1]
    tm, tn, tk = min(tm, m), min(tn, n), min(tk, k)
    nk = k // tk
    dims = (((0 if ta else 1,), (1 if tb else 0,)), ((), ()))
    out_dtypes = out_dtype if isinstance(out_dtype, tuple) else (out_dtype,)
    n_tiles = len(out_dtypes)

    def add_colsums(o_refs, sums):
        i = pl.program_id(1)
        for s_ref, val in zip(o_refs[n_tiles:], sums):
            @pl.when(i == 0)
            def _(s_ref=s_ref, val=val):
                s_ref[...] = val

            @pl.when(i > 0)
            def _(s_ref=s_ref, val=val):
                s_ref[...] += val

    def finish(acc, x_refs, o_refs):
        vals = (acc,) if epilogue is None else epilogue(acc, *[r[...] for r in x_refs])
        for o_ref, val in zip(o_refs[:n_tiles], vals[:n_tiles]):
            o_ref[...] = (val.T if transpose_out else val).astype(o_ref.dtype)
        add_colsums(o_refs, vals[n_tiles:])

    chunk = EPILOGUE_ROWS if (nk == 1 and epilogue is not None and not ta and tm % EPILOGUE_ROWS == 0) else None

    def body(ins, outs, acc):
        a_ref, b_ref = ins[:2]
        if chunk is not None:
            sums = None
            for r0 in range(0, tm, chunk):
                part = lax.dot_general(a_ref[r0:r0 + chunk, :], b_ref[...], dims, preferred_element_type=F32)
                vals = epilogue(part, *[r[...] if r.shape[0] == 1 else r[r0:r0 + chunk, :] for r in ins[2:]])
                for o_ref, val in zip(outs[:n_tiles], vals[:n_tiles]):
                    o_ref[r0:r0 + chunk, :] = val.astype(o_ref.dtype)
                sums = vals[n_tiles:] if sums is None else [s + v for s, v in zip(sums, vals[n_tiles:])]
            add_colsums(outs, sums)
            return
        part = lax.dot_general(a_ref[...], b_ref[...], dims, preferred_element_type=F32)
        if nk == 1:
            finish(part, ins[2:], outs)
            return
        acc_ref, = acc
        kk = pl.program_id(2)

        @pl.when(kk == 0)
        def _():
            acc_ref[...] = part

        @pl.when(kk > 0)
        def _():
            acc_ref[...] += part

        @pl.when(kk == nk - 1)
        def _():
            finish(acc_ref[...], ins[2:], outs)

    a_spec = (pl.BlockSpec((tk, tm), lambda j, i, kk: (kk, i)) if ta
              else pl.BlockSpec((tm, tk), lambda j, i, kk: (i, kk)))
    b_spec = (pl.BlockSpec((tn, tk), lambda j, i, kk: (j, kk)) if tb
              else pl.BlockSpec((tk, tn), lambda j, i, kk: (kk, j)))
    tile = pl.BlockSpec((tm, tn), lambda j, i, kk: (i, j))
    row = pl.BlockSpec((1, tn), lambda j, i, kk: (0, j))
    out_tile, out_dims = (pl.BlockSpec((tn, tm), lambda j, i, kk: (j, i)), (n, m)) if transpose_out else (tile, (m, n))
    res = _call(
        body, name=name, grid=(n // tn, m // tm, nk),
        in_specs=[a_spec, b_spec] + [row if t.shape[0] == 1 else tile for t in extra],
        out_specs=[out_tile] * n_tiles + [row] * n_colsum,
        out_shape=[_sds(out_dims, dt) for dt in out_dtypes] + [_sds((1, n), F32)] * n_colsum,
        scratch_shapes=[] if nk == 1 else [pltpu.VMEM((tm, tn), F32)],
        args=(a, b, *extra), after=after)
    return res if isinstance(out_dtype, tuple) or n_colsum else res[0]


def _rstd(h):
    return lax.rsqrt(jnp.mean(h * h, axis=-1, keepdims=True) + RMS_EPS)


def _sigmoid(z):
    return 1.0 / (1.0 + jnp.exp(-z))


def _rms_fwd(x, g, *, tm, name):
    n = x.shape[0]

    def body(x_ref, g_ref, o_ref):
        h = x_ref[...]
        o_ref[...] = (h * _rstd(h) * g_ref[...]).astype(BF16)

    return pl.pallas_call(
        body, name=name, grid=(n // tm,),
        in_specs=[_rows(tm, D_MODEL), _const((1, D_MODEL))],
        out_specs=_rows(tm, D_MODEL), out_shape=_sds((n, D_MODEL), BF16),
        compiler_params=_params("parallel"),
    )(x, g)


def _swap_halves(t):
    width = t.shape[1]
    lane = lax.broadcasted_iota(jnp.int32, t.shape, 1)
    return jnp.where((lane & 63) < 32, pltpu.roll(t, width - 32, 1), pltpu.roll(t, 32, 1))


def _dil_spec(dil, tm):
    return pl.BlockSpec((dil, tm // dil, 256), lambda i: (0, i, 0))


def _dil_scratch(tm):
    return pltpu.VMEM((2, tm, 128), F32)


def _load_token_order(src, scr, dil, tm):
    if dil == 1:
        return src[0]
    for j in range(dil):
        for c in range(2):
            scr[c, pl.ds(j, tm // dil, stride=dil), :] = src[j, :, c * 128:(c + 1) * 128]
    return jnp.concatenate([scr[0], scr[1]], axis=1)


def _store_dil_order(val, dst, scr, dil, tm):
    if dil == 1:
        dst[0] = val.astype(dst.dtype)
        return
    for c in range(2):
        scr[c] = val[:, c * 128:(c + 1) * 128]
    for j in range(dil):
        for c in range(2):
            dst[j, :, c * 128:(c + 1) * 128] = scr[c, pl.ds(j, tm // dil, stride=dil), :].astype(dst.dtype)


def _split_proj(proj, cos_t, sin_t, *, tm, name):
    n = proj.shape[0]
    n_dil = len(DIL_DILATIONS)

    def body(*refs):
        na_in = refs[0:3]
        dil_in = refs[3:3 + 3 * n_dil]
        gate_in = refs[12:20]
        cos_ref, sin_ref = refs[20:22]
        outs = refs[22:]
        na_out = outs[0:3]
        dil_out = outs[3:12]
        sn_ref, sd_ref = outs[12:14]
        scr = outs[14]
        for t in range(3):
            na_out[t][...] = na_in[t][...].astype(BF16)
        cosv, sinv = cos_ref[...], sin_ref[...]
        for t in range(3):
            for gi, dil in enumerate(DIL_DILATIONS):
                val = dil_in[t * n_dil + gi][...]
                if t < 2:
                    val = val * cosv + _swap_halves(val) * sinv
                _store_dil_order(val, dil_out[t * n_dil + gi], scr, dil, tm)
        for c in range(4):
            sn_ref[:, c * 256:(c + 1) * 256] = _sigmoid(gate_in[c][...])
            sd_ref[:, c * 256:(c + 1) * 256] = _sigmoid(gate_in[4 + c][...])

    in_specs = [_rows(tm, NA_WIDTH, c) for c in range(3)]
    in_specs += [_rows(tm, 256, 6 + c) for c in range(9)]
    in_specs += [_rows(tm, 256, 15 + c) for c in range(8)]
    in_specs += [_rows(tm, 256), _rows(tm, 256)]
    out_specs = [_rows(tm, NA_WIDTH)] * 3
    out_shape = [_sds((n, NA_WIDTH), BF16)] * 3
    for _ in range(3):
        for dil in DIL_DILATIONS:
            out_specs.append(pl.BlockSpec((dil, tm // dil, 256), lambda i: (0, i, 0)))
            out_shape.append(_sds((dil, n // dil, 256), BF16))
    out_specs += [_rows(tm, D_MODEL)] * 2
    out_shape += [_sds((n, D_MODEL), F32)] * 2
    res = pl.pallas_call(
        body, name=name, grid=(n // tm,),
        in_specs=in_specs, out_specs=out_specs, out_shape=out_shape,
        scratch_shapes=[_dil_scratch(tm)],
        compiler_params=_params("parallel"),
    )(*([proj] * 20), cos_t, sin_t)
    return res[0:3], res[3:6], res[6:9], res[9:12], res[12], res[13]


def _residual_rms_tile(delta, h, g):
    hn = h + delta
    return hn, hn * _rstd(hn) * g


def _gate_bwd_tile(dm, s1, b1, s2, b2):
    return dm * s1, dm * s2, dm * b1 * s1 * (1.0 - s1), dm * b2 * s2 * (1.0 - s2)


def _tail_tile(gt, pp, h2, target, g):
    sg = _sigmoid(gt)
    h3 = h2 + sg * pp
    r3 = _rstd(h3)
    n3 = h3 * r3
    err = n3 * g - target
    loss = 0.5 * jnp.sum(jnp.sum(err * err, axis=-1, keepdims=True) / D_MODEL)
    dy = err / D_MODEL
    dn = dy * g
    dh3 = r3 * (dn - n3 * jnp.mean(dn * n3, axis=-1, keepdims=True))
    return (dh3, dh3 * sg, dh3 * pp * sg * (1.0 - sg),
            jnp.sum(dy * n3, axis=0, keepdims=True), jnp.full((1, gt.shape[1]), loss, F32))


def _rms_bwd_tile(dz, h, g, dres):
    r = _rstd(h)
    nrm = h * r
    dn = dz * g
    dh = dres + r * (dn - nrm * jnp.mean(dn * nrm, axis=-1, keepdims=True))
    return dh, jnp.sum(dz * nrm, axis=0, keepdims=True)


def _rms_bwd_twice(dz, h, g, dres):
    dh, dg = _rms_bwd_tile(dz, h, g, dres)
    return dh, dh, dg


def _assemble_dproj(dna, ddil_q, ddil_k, ddil_v, dgn, dgd, cos_t, sin_t, *, tm, name):
    n = dgn.shape[0]

    def body(*refs):
        dq_ref, dk_ref, dv_ref = refs[0:3]
        dil_in = refs[3:12]
        dgn_ref, dgd_ref, cos_ref, sin_ref, o_ref, scr = refs[12:18]
        o_ref[:, 0:512] = dq_ref[...]
        o_ref[:, 512:1024] = dk_ref[...].astype(BF16)
        o_ref[:, 1024:1536] = dv_ref[...].astype(BF16)
        cosv, sinv = cos_ref[...], sin_ref[...]
        for t in range(3):
            for gi, dil in enumerate(DIL_DILATIONS):
                val = _load_token_order(dil_in[t * 3 + gi], scr, dil, tm)
                if t < 2:
                    val = val * cosv + _swap_halves(val * sinv)
                c0 = 1536 + t * DIL_WIDTH + gi * 256
                o_ref[:, c0:c0 + 256] = val.astype(BF16)
        o_ref[:, 3840:4864] = dgn_ref[...]
        o_ref[:, 4864:5888] = dgd_ref[...]

    in_specs = [_rows(tm, NA_WIDTH)] * 3
    for _ in range(3):
        for dil in DIL_DILATIONS:
            in_specs.append(pl.BlockSpec((dil, tm // dil, 256), lambda i: (0, i, 0)))
    in_specs += [_rows(tm, D_MODEL)] * 2 + [_rows(tm, 256)] * 2
    return pl.pallas_call(
        body, name=name, grid=(n // tm,), in_specs=in_specs,
        out_specs=_rows(tm, IN_WIDTH), out_shape=_sds((n, IN_WIDTH), BF16),
        scratch_shapes=[_dil_scratch(tm)],
        compiler_params=_params("parallel"),
    )(*dna, *ddil_q, *ddil_k, *ddil_v, dgn, dgd, cos_t, sin_t)


N_ROW_OFF = 2 * NA_WIN_ROWS - 1
N_PAIRS = N_ROW_OFF - 1
RB_WIDTH = (N_ROW_OFF + 1) * GRID_W


def _na_bias(rb_ref, pair_scr):
    shape = (GRID_W, RB_WIDTH)
    qc = lax.broadcasted_iota(jnp.int32, shape, 0)
    qc2 = lax.broadcasted_iota(jnp.int32, (GRID_W, 128), 0)
    kc2 = lax.broadcasted_iota(jnp.int32, (GRID_W, 128), 1) & (GRID_W - 1)
    cs = jnp.clip(qc2 - 8, 0, GRID_W - 16)
    valid = (kc2 >= cs) & (kc2 < cs + 16)
    for hh in range(2):
        t = jnp.broadcast_to(rb_ref[hh], shape)
        t = pltpu.roll(t, RB_WIDTH - 15, 1)
        for b in range(6):
            t = jnp.where(((qc >> b) & 1) == 1, pltpu.roll(t, 1 << b, 1), t)
        t_odd = pltpu.roll(t, RB_WIDTH - GRID_W, 1)
        for ro in range(N_PAIRS):
            src = t if ro % 2 == 0 else t_odd
            base = (ro // 2) * 128
            pair_scr[hh, ro] = jnp.where(valid, src[:, base:base + 128], NEG_INF)


NA_GROUP_FWD = 4
NA_GROUP_BWD = 4


def _stack_heads(ref, r, scale=1.0):
    lane = lax.broadcasted_iota(jnp.int32, (GRID_W, 128), 1)
    t = ref[pl.ds(pl.multiple_of(r * GRID_W, GRID_W), GRID_W), :].astype(F32) * scale
    return jnp.concatenate([jnp.where(lane < 64, t, 0.0), jnp.where(lane >= 64, t, 0.0)], axis=0).astype(BF16)


def _unstack_heads(t2):
    lane = lax.broadcasted_iota(jnp.int32, (GRID_W, 128), 1)
    return jnp.where(lane < 64, t2[:GRID_W], t2[GRID_W:])


def _na_window(k_ref, v_ref, r, n_rows):
    rs = jnp.clip(r - NA_WIN_ROWS // 2, 0, n_rows - NA_WIN_ROWS)
    ro0 = (NA_WIN_ROWS - 1) - (r - rs)
    off = pl.multiple_of(rs * GRID_W, GRID_W)
    kw = k_ref[pl.ds(off, NA_WIN_ROWS * GRID_W), :]
    vw = v_ref[pl.ds(off, NA_WIN_ROWS * GRID_W), :]
    return kw, vw, off, ro0


def _na_probs(s_raw, pair_scr, ro0):
    bias = [jnp.concatenate([pair_scr[hh, ro0 + 2 * j] for j in range(NA_WIN_ROWS // 2)], axis=1)
            for hh in range(2)]
    s = s_raw + jnp.concatenate(bias, axis=0)
    m = jnp.max(s, axis=-1, keepdims=True)
    e = jnp.exp(s - m)
    return e * (1.0 / jnp.sum(e, axis=-1, keepdims=True))


def _na_fwd(q, k, v, rb, *, name, after=None):
    n = q.shape[0]
    n_rows = n // GRID_W

    def body(ins, outs, scr):
        q_ref, k_ref, v_ref, rb_ref = ins
        o_ref, = outs
        pair_scr, = scr
        _na_bias(rb_ref, pair_scr)

        def group(g, carry):
            rows = [g * NA_GROUP_FWD + t for t in range(NA_GROUP_FWD)]
            wins = [_na_window(k_ref, v_ref, r, n_rows) for r in rows]
            raw = [lax.dot_general(_stack_heads(q_ref, r, QK_SCALE), w[0], NT_DIMS, preferred_element_type=F32)
                   for r, w in zip(rows, wins)]
            probs = [_na_probs(s, pair_scr, w[3]) for s, w in zip(raw, wins)]
            outs2 = [jnp.dot(p.astype(BF16), w[1], preferred_element_type=F32) for p, w in zip(probs, wins)]
            for r, o2 in zip(rows, outs2):
                o_ref[pl.ds(pl.multiple_of(r * GRID_W, GRID_W), GRID_W), :] = _unstack_heads(o2).astype(BF16)
            return carry

        lax.fori_loop(0, n_rows // NA_GROUP_FWD, group, 0)

    col = pl.BlockSpec((n, 128), lambda h: (0, h))
    return _call(
        body, name=name, grid=(NA_WIDTH // 128,),
        in_specs=[col, col, col, pl.BlockSpec((2, 1, RB_WIDTH), lambda h: (h, 0, 0))],
        out_specs=[col], out_shape=[_sds((n, NA_WIDTH), BF16)],
        scratch_shapes=[pltpu.VMEM((2, N_PAIRS, GRID_W, 128), F32)],
        args=(q, k, v, rb), after=after)[0]


def _na_bwd(q, k, v, do, rb, *, name):
    n = q.shape[0]
    n_rows = n // GRID_W
    win = NA_WIN_ROWS * GRID_W

    def body(ins, outs, scr):
        q_ref, k_ref, v_ref, do_ref, rb_ref = ins
        dq_ref, dk_ref, dv_ref, drb_ref = outs
        pair_scr, acc_scr = scr
        _na_bias(rb_ref, pair_scr)
        acc_scr[...] = jnp.zeros_like(acc_scr)
        dk_ref[...] = jnp.zeros_like(dk_ref)
        dv_ref[...] = jnp.zeros_like(dv_ref)

        def group(g, carry):
            rows = [g * NA_GROUP_BWD + t for t in range(NA_GROUP_BWD)]
            wins = [_na_window(k_ref, v_ref, r, n_rows) for r in rows]
            qss = [_stack_heads(q_ref, r, QK_SCALE) for r in rows]
            doss = [_stack_heads(do_ref, r) for r in rows]
            raw = [lax.dot_general(qs, w[0], NT_DIMS, preferred_element_type=F32) for qs, w in zip(qss, wins)]
            dps = [lax.dot_general(dos, w[1], NT_DIMS, preferred_element_type=F32) for dos, w in zip(doss, wins)]
            probs = [_na_probs(s, pair_scr, w[3]) for s, w in zip(raw, wins)]
            dss = [p * (dp - jnp.sum(p * dp, axis=-1, keepdims=True)) for p, dp in zip(probs, dps)]
            dsbs = [ds.astype(BF16) for ds in dss]
            dq2s = [jnp.dot(dsb, w[0], preferred_element_type=F32) for dsb, w in zip(dsbs, wins)]
            dkws = [lax.dot_general(dsb, qs, TN_DIMS, preferred_element_type=F32) for dsb, qs in zip(dsbs, qss)]
            dvws = [lax.dot_general(p.astype(BF16), dos, TN_DIMS, preferred_element_type=F32)
                    for p, dos in zip(probs, doss)]
            for t, r in enumerate(rows):
                _, _, off, ro0 = wins[t]
                for hh in range(2):
                    for j in range(NA_WIN_ROWS // 2):
                        acc_scr[hh, ro0 + 2 * j] += dss[t][hh * GRID_W:(hh + 1) * GRID_W, j * 128:(j + 1) * 128]
                dq_ref[pl.ds(pl.multiple_of(r * GRID_W, GRID_W), GRID_W), :] = (
                    _unstack_heads(dq2s[t]) * QK_SCALE).astype(BF16)
                dk_ref[pl.ds(off, win), :] += dkws[t]
                dv_ref[pl.ds(off, win), :] += dvws[t]
            return carry

        lax.fori_loop(0, n_rows // NA_GROUP_BWD, group, 0)

        qc = lax.broadcasted_iota(jnp.int32, (N_PAIRS * GRID_W, 128), 0)
        for hh in range(2):
            t = acc_scr[hh].reshape(N_PAIRS * GRID_W, 128)
            for b in range(6):
                t = jnp.where(((qc >> b) & 1) == 1, pltpu.roll(t, 128 - (1 << b), 1), t)
            t = pltpu.roll(t, 15, 1)
            drb_ref[hh] = jnp.sum(t.reshape(N_PAIRS, GRID_W, 128), axis=1)

    col = pl.BlockSpec((n, 128), lambda h: (0, h))
    return _call(
        body, name=name, grid=(NA_WIDTH // 128,),
        in_specs=[col, col, col, col, pl.BlockSpec((2, 1, RB_WIDTH), lambda h: (h, 0, 0))],
        out_specs=[col, col, col, pl.BlockSpec((2, N_PAIRS, 128), lambda h: (h, 0, 0))],
        out_shape=[_sds((n, NA_WIDTH), BF16), _sds((n, NA_WIDTH), F32), _sds((n, NA_WIDTH), F32),
                   _sds((8, N_PAIRS, 128), F32)],
        scratch_shapes=[pltpu.VMEM((2, N_PAIRS, GRID_W, 128), F32),
                        pltpu.VMEM((2, N_PAIRS, GRID_W, 128), F32)],
        args=(q, k, v, do, rb))


def _rpb_table(rpb2):
    t = jnp.pad(rpb2, ((0, 0), (0, 1), (0, GRID_W - rpb2.shape[-1])))
    return t.reshape(8, 1, RB_WIDTH)


def _rpb_grad(drb, *, name):
    kdim = drb.shape[1]

    def body(x_ref, o_ref):
        kk = lax.broadcasted_iota(jnp.int32, (128, 512), 0)
        jj = lax.broadcasted_iota(jnp.int32, (128, 512), 1)
        half, co = kk >> 6, kk & 63
        acc = jnp.zeros((8, 512), F32)
        for ro in range(N_PAIRS):
            hit = ((ro + half) == (jj >> 5)) & (co == (jj & 31)) & (co < 31)
            onehot = jnp.where(hit, 1.0, 0.0).astype(F32)
            acc = acc + jnp.dot(x_ref[:, ro * 128:(ro + 1) * 128], onehot, preferred_element_type=F32,
                                precision=lax.Precision.HIGHEST)
        o_ref[...] = acc

    return pl.pallas_call(
        body, name=name, grid=(1,),
        in_specs=[_const((8, kdim))], out_specs=_const((8, 512)), out_shape=_sds((8, 512), F32),
        compiler_params=_params("arbitrary"),
    )(drb)


DIL_GROUP = 2


def _dil_blocks(length):
    qb = min(128, length)
    return qb, min(qb + 2 * DIL_RADIUS, length), min(DIL_GROUP, length // qb)


def _stack_lanes(ref, t, qb, scale=1.0):
    lane = lax.broadcasted_iota(jnp.int32, (qb, 256), 1)
    val = ref[0, t * qb:(t + 1) * qb, :].astype(F32) * scale
    return jnp.concatenate([jnp.where((lane >> 6) == h, val, 0.0) for h in range(4)], axis=0).astype(BF16)


def _dil_window(k_ref, v_ref, blk, qb, win, length):
    start = pl.multiple_of(jnp.clip(blk * qb - DIL_RADIUS, 0, length - win), DIL_RADIUS)
    return k_ref[0, pl.ds(start, win), :], v_ref[0, pl.ds(start, win), :], start


def _dil_mask(s, blk, start, qb, win):
    gap = ((lax.broadcasted_iota(jnp.int32, (4 * qb, win), 0) & (qb - 1))
           - lax.broadcasted_iota(jnp.int32, (4 * qb, win), 1)) + (blk * qb - start)
    return jnp.where(jnp.abs(gap) <= DIL_RADIUS, s, NEG_INF)


def _pick_heads(stacked, qb):
    lane = lax.broadcasted_iota(jnp.int32, (qb, 256), 1)
    out = jnp.zeros((qb, 256), stacked.dtype)
    for h in range(4):
        out = jnp.where((lane >> 6) == h, stacked[h * qb:(h + 1) * qb], out)
    return out


def _stack_head_cols(ref, t, qb):
    return jnp.concatenate([ref[0, t * qb:(t + 1) * qb, 64 * h:64 * h + 1] for h in range(4)], axis=0)


def _dil_fwd(q, k, v, *, name):
    dil, length, _ = q.shape
    qb, win, grp = _dil_blocks(length)

    def body(q_ref, k_ref, v_ref, o_ref, lse_ref):
        blks = [pl.program_id(1) * grp + t for t in range(grp)]
        wins = [_dil_window(k_ref, v_ref, b, qb, win, length) for b in blks]
        raw = [lax.dot_general(_stack_lanes(q_ref, t, qb, QK_SCALE), w[0], NT_DIMS, preferred_element_type=F32)
               for t, w in enumerate(wins)]
        lses, outs = [], []
        for t, (s, w) in enumerate(zip(raw, wins)):
            s = _dil_mask(s, blks[t], w[2], qb, win)
            m = jnp.max(s, axis=-1, keepdims=True)
            e = jnp.exp(s - m)
            norm = jnp.sum(e, axis=-1, keepdims=True)
            lses.append(m + jnp.log(norm))
            outs.append(jnp.dot((e * (1.0 / norm)).astype(BF16), w[1], preferred_element_type=F32))
        for t in range(grp):
            o_ref[0, t * qb:(t + 1) * qb, :] = _pick_heads(outs[t], qb)
            lse_ref[0, t * qb:(t + 1) * qb, :] = _pick_heads(jnp.broadcast_to(lses[t], (4 * qb, 256)), qb)

    seq = pl.BlockSpec((1, length, 256), lambda j, i: (j, 0, 0))
    blk = pl.BlockSpec((1, grp * qb, 256), lambda j, i: (j, i, 0))
    return pl.pallas_call(
        body, name=name, grid=(dil, length // (grp * qb)),
        in_specs=[blk, seq, seq], out_specs=[blk, blk],
        out_shape=[_sds((dil, length, 256), F32)] * 2,
        compiler_params=_params("parallel", "parallel"),
    )(q, k, v)


def _dil_bwd(q, k, v, do, lse, cc, *, name):
    dil, length, _ = q.shape
    qb, win, grp = _dil_blocks(length)

    def body(q_ref, k_ref, v_ref, do_ref, lse_ref, cc_ref, dq_ref, dk_ref, dv_ref):
        @pl.when(pl.program_id(1) == 0)
        def _():
            dk_ref[...] = jnp.zeros_like(dk_ref)
            dv_ref[...] = jnp.zeros_like(dv_ref)

        blks = [pl.program_id(1) * grp + t for t in range(grp)]
        wins = [_dil_window(k_ref, v_ref, b, qb, win, length) for b in blks]
        qss = [_stack_lanes(q_ref, t, qb, QK_SCALE) for t in range(grp)]
        doss = [_stack_lanes(do_ref, t, qb) for t in range(grp)]
        raw = [lax.dot_general(qs, w[0], NT_DIMS, preferred_element_type=F32) for qs, w in zip(qss, wins)]
        dps = [lax.dot_general(dos, w[1], NT_DIMS, preferred_element_type=F32) for dos, w in zip(doss, wins)]
        probs = [jnp.exp(_dil_mask(s, blks[t], wins[t][2], qb, win) - _stack_head_cols(lse_ref, t, qb))
                 for t, s in enumerate(raw)]
        dsbs = [(p * (dp + _stack_head_cols(cc_ref, t, qb))).astype(BF16)
                for t, (p, dp) in enumerate(zip(probs, dps))]
        dq4s = [jnp.dot(dsb, w[0], preferred_element_type=F32) for dsb, w in zip(dsbs, wins)]
        dkws = [lax.dot_general(dsb, qs, TN_DIMS, preferred_element_type=F32) for dsb, qs in zip(dsbs, qss)]
        dvws = [lax.dot_general(p.astype(BF16), dos, TN_DIMS, preferred_element_type=F32)
                for p, dos in zip(probs, doss)]
        for t in range(grp):
            dq_ref[0, t * qb:(t + 1) * qb, :] = _pick_heads(dq4s[t], qb) * QK_SCALE
            dk_ref[0, pl.ds(wins[t][2], win), :] += dkws[t]
            dv_ref[0, pl.ds(wins[t][2], win), :] += dvws[t]

    seq = pl.BlockSpec((1, length, 256), lambda j, i: (j, 0, 0))
    blk = pl.BlockSpec((1, grp * qb, 256), lambda j, i: (j, i, 0))
    return pl.pallas_call(
        body, name=name, grid=(dil, length // (grp * qb)),
        in_specs=[blk, seq, seq, blk, blk, blk], out_specs=[blk, seq, seq],
        out_shape=[_sds((dil, length, 256), F32)] * 3,
        compiler_params=_params("parallel", "arbitrary"),
    )(q, k, v, do, lse, cc)


def _merge_weights(lses):
    m = jnp.maximum(jnp.maximum(lses[0], lses[1]), lses[2])
    es = [jnp.exp(t - m) for t in lses]
    inv = 1.0 / (es[0] + es[1] + es[2])
    return [e * inv for e in es]


def _dil_merge(outs, lses, *, tm, name):
    n = outs[0].shape[1]

    def body(*refs):
        o_in, l_in = refs[0:3], refs[3:6]
        y_ref, yb_ref, scr = refs[6:9]
        lv = [_load_token_order(l_in[g], scr, d, tm) for g, d in enumerate(DIL_DILATIONS)]
        ws = _merge_weights(lv)
        y = jnp.zeros((tm, 256), F32)
        for g, d in enumerate(DIL_DILATIONS):
            y = y + ws[g] * _load_token_order(o_in[g], scr, d, tm)
        y_ref[...] = y
        yb_ref[...] = y.astype(BF16)

    specs = [_dil_spec(d, tm) for d in DIL_DILATIONS]
    return pl.pallas_call(
        body, name=name, grid=(n // tm,), in_specs=specs + specs,
        out_specs=[_rows(tm, 256)] * 2, out_shape=[_sds((n, 256), F32), _sds((n, 256), BF16)],
        scratch_shapes=[_dil_scratch(tm)],
        compiler_params=_params("parallel"),
    )(*outs, *lses)


def _dil_merge_bwd(dy, y, lses, *, tm, name):
    n = dy.shape[0]

    def body(*refs):
        dy_ref, y_ref = refs[0:2]
        l_in = refs[2:5]
        do_out, cc_out = refs[5:8], refs[8:11]
        scr = refs[11]
        lv = [_load_token_order(l_in[g], scr, d, tm) for g, d in enumerate(DIL_DILATIONS)]
        ws = _merge_weights(lv)
        dyv = dy_ref[...]
        rr = lax.broadcasted_iota(jnp.int32, (256, 256), 0) >> 6
        cc = lax.broadcasted_iota(jnp.int32, (256, 256), 1) >> 6
        ones = jnp.where(rr == cc, 1.0, 0.0).astype(F32)
        tsum = jnp.dot(dyv * y_ref[...], ones, preferred_element_type=F32,
                       precision=lax.Precision.HIGHEST)
        for g, d in enumerate(DIL_DILATIONS):
            _store_dil_order(ws[g] * dyv, do_out[g], scr, d, tm)
            _store_dil_order(-ws[g] * tsum, cc_out[g], scr, d, tm)

    specs = [_dil_spec(d, tm) for d in DIL_DILATIONS]
    res = pl.pallas_call(
        body, name=name, grid=(n // tm,),
        in_specs=[_rows(tm, 256)] * 2 + specs,
        out_specs=specs + specs,
        out_shape=[_sds((d, n // d, 256), BF16) for d in DIL_DILATIONS]
                  + [_sds((d, n // d, 256), F32) for d in DIL_DILATIONS],
        scratch_shapes=[_dil_scratch(tm)],
        compiler_params=_params("parallel"),
    )(dy, y, *lses)
    return res[0:3], res[3:6]


_WEIGHTS = (("w_in", 1, 736), ("w_branch_na", 1, 128), ("w_branch_dil", 1, 128), ("w_out", 0, 128),
            ("w_up", 1, 512), ("w_down", 0, 512), ("w_ple_gate", 0, 128), ("w_ple_proj", 1, 128))
_W_IN, _W_BNA, _W_BD, _W_OUT, _W_UP, _W_DOWN, _W_PG, _W_PP = range(8)


def _to_full(gathered):
    return gathered.reshape(-1, gathered.shape[2])


def _to_chunks(widx, mat):
    return mat.reshape(N_DEV, _WEIGHTS[widx][2], mat.shape[1])


def _local_step(x, p_bf16, positions, target, g_mix, g_mlp, g_ple, g_final, rpb2,
                get_w_in, relay_rest, get_rest, send_grads):
    tm = 256
    half = HEAD_DIM // 2
    inv_freq = 10000.0 ** (-jnp.arange(half, dtype=F32) / half)
    ang = positions.astype(F32)[:, None] * inv_freq
    cos, sin = jnp.cos(ang), jnp.sin(ang)
    cos_t = jnp.tile(jnp.concatenate([cos, cos], axis=-1), (1, 4))
    sin_t = jnp.tile(jnp.concatenate([-sin, sin], axis=-1), (1, 4))
    rb = _rpb_table(rpb2)

    a = _rms_fwd(x, g_mix, tm=tm, name="rms_mix")
    w_in, token = get_w_in(a)
    proj = _matmul(a, w_in, tb=True, out_dtype=F32, tm=512, tn=2944, tk=1024, name="mm_in", after=token)
    na_qkv, dq_g, dk_g, dv_g, sn, sd = _split_proj(proj, cos_t, sin_t, tm=tm, name="split_proj")
    y_na = _na_fwd(*na_qkv, rb, name="na_fwd", after=relay_rest(sd))
    d_out, d_lse = [], []
    for g in range(3):
        o, lse = _dil_fwd(dq_g[g], dk_g[g], dv_g[g], name=f"dil_fwd{g}")
        d_out.append(o)
        d_lse.append(lse)
    y_dil, y_dil_b = _dil_merge(d_out, d_lse, tm=tm, name="dil_merge")
    w_bna, w_bd, w_out, w_up, w_down, w_pg, w_pp = get_rest(y_dil_b)
    bn = _matmul(y_na, w_bna, tb=True, out_dtype=F32, tm=512, tn=1024, tk=512, name="mm_bna")
    bd, mixed = _matmul(y_dil_b, w_bd, tb=True, out_dtype=(F32, BF16), tm=512, tn=1024, tk=256, name="mm_bd",
                        extra=(sn, bn, sd), epilogue=lambda acc, s1, b1, s2: (acc, s1 * b1 + s2 * acc))
    h1, c = _matmul(mixed, w_out, out_dtype=(F32, BF16), tm=512, tn=1024, tk=1024, name="mm_out",
                    extra=(x, g_mlp), epilogue=_residual_rms_tile)
    u, f = _matmul(c, w_up, tb=True, out_dtype=(F32, BF16), tm=512, tn=2048, tk=1024, name="mm_up",
                   epilogue=lambda acc: (acc, jnp.square(jnp.maximum(acc, 0.0))))
    h2, e = _matmul(f, w_down, out_dtype=(F32, BF16), tm=512, tn=1024, tk=4096, name="mm_down",
                    extra=(h1, g_ple), epilogue=_residual_rms_tile)
    pp = _matmul(p_bf16, w_pp, tb=True, out_dtype=F32, tm=512, tn=1024, tk=256, name="mm_pp")

    dh3, dpp, dgt, dg_final, loss = _matmul(
        e, w_pg, out_dtype=(F32, BF16, BF16), tm=512, tn=1024, tk=1024, name="mm_pg_tail",
        extra=(pp, h2, target, g_final), epilogue=_tail_tile, n_colsum=2)
    loss = loss[:, :128]
    gw_pp = _matmul(p_bf16, dpp, ta=True, transpose_out=True, out_dtype=BF16, tm=256, tn=1024, tk=2048,
                    name="mm_gw_pp")
    gw_pg = _matmul(e, dgt, ta=True, out_dtype=BF16, tm=512, tn=1024, tk=2048, name="mm_gw_pg")
    dh2, dh2_b, dg_ple = _matmul(
        dgt, w_pg, tb=True, out_dtype=(F32, BF16), tm=512, tn=1024, tk=1024, name="mm_de",
        extra=(h2, g_ple, dh3), epilogue=_rms_bwd_twice, n_colsum=1)
    du = _matmul(dh2_b, w_down, tb=True, out_dtype=BF16, tm=512, tn=2048, tk=1024, name="mm_du",
                 extra=(u,), epilogue=lambda acc, uv: (acc * (2.0 * jnp.maximum(uv, 0.0)),))
    gw_down = _matmul(f, dh2_b, ta=True, out_dtype=BF16, tm=1024, tn=1024, tk=2048, name="mm_gw_down")
    token = send_grads((_W_PP, _W_PG, _W_DOWN), (gw_pp, gw_pg, gw_down))
    gw_up = _matmul(c, du, ta=True, transpose_out=True, out_dtype=BF16, tm=512, tn=2048, tk=2048, name="mm_gw_up",
                    after=token)
    token = send_grads((_W_UP,), (gw_up,))
    dh1, dh1_b, dg_mlp = _matmul(
        du, w_up, out_dtype=(F32, BF16), tm=512, tn=1024, tk=4096, name="mm_dc", after=token,
        extra=(h1, g_mlp, dh2), epilogue=_rms_bwd_twice, n_colsum=1)
    dbn, dbd, dgn, dgd = _matmul(dh1_b, w_out, tb=True, out_dtype=(BF16,) * 4, tm=512, tn=1024, tk=1024,
                                 name="mm_dmixed", extra=(sn, bn, sd, bd), epilogue=_gate_bwd_tile)
    gw_out = _matmul(mixed, dh1_b, ta=True, out_dtype=BF16, tm=512, tn=1024, tk=2048, name="mm_gw_out")
    gw_bna = _matmul(y_na, dbn, ta=True, transpose_out=True, out_dtype=BF16, tm=512, tn=1024, tk=2048,
                     name="mm_gw_bna")
    dy_na = _matmul(dbn, w_bna, out_dtype=BF16, tm=512, tn=512, tk=1024, name="mm_dy_na")
    gw_bd = _matmul(y_dil_b, dbd, ta=True, transpose_out=True, out_dtype=BF16, tm=256, tn=1024, tk=2048,
                    name="mm_gw_bd")
    token = send_grads((_W_OUT, _W_BNA, _W_BD), (gw_out, gw_bna, gw_bd))
    dy_dil = _matmul(dbd, w_bd, out_dtype=F32, tm=512, tn=256, tk=1024, name="mm_dy_dil", after=token)
    dna = _na_bwd(*na_qkv, dy_na, rb, name="na_bwd")
    drpb = _rpb_grad(dna[3].reshape(8, -1), name="rpb_grad")
    do_g, cc_g = _dil_merge_bwd(dy_dil, y_dil, d_lse, tm=tm, name="dil_merge_bwd")
    ddq, ddk, ddv = [], [], []
    for g in range(3):
        r = _dil_bwd(dq_g[g], dk_g[g], dv_g[g], do_g[g], d_lse[g], cc_g[g], name=f"dil_bwd{g}")
        ddq.append(r[0])
        ddk.append(r[1])
        ddv.append(r[2])
    dproj = _assemble_dproj(dna[0:3], ddq, ddk, ddv, dgn, dgd, cos_t, sin_t, tm=tm, name="assemble_dproj")
    gw_in = _matmul(a, dproj, ta=True, transpose_out=True, out_dtype=BF16, tm=512, tn=2944, tk=2048, name="mm_gw_in")
    token = send_grads((_W_IN,), (gw_in,))
    dx, dg_mix = _matmul(
        dproj, w_in, out_dtype=(F32,), tm=512, tn=1024, tk=5888, name="mm_da", after=token,
        extra=(x, g_mix, dh1), epilogue=_rms_bwd_tile, n_colsum=1)
    return loss, dx, (dg_mix, dg_mlp, dg_ple, dg_final), drpb


def _cast_bf16(t, *, name):
    def body(t_ref, o_ref):
        o_ref[...] = t_ref[...].astype(BF16)

    rows, cols = t.shape
    tr = 256 if rows % 256 == 0 else rows
    blk = pl.BlockSpec((tr, cols), lambda i: (i, 0))
    return pl.pallas_call(body, name=name, grid=(rows // tr,), in_specs=[blk], out_specs=blk,
                          out_shape=_sds(t.shape, BF16), compiler_params=_params("parallel"))(t)


def _adamw(w, g, m, v):
    m = ADAM_B1 * m + (1.0 - ADAM_B1) * g
    v = ADAM_B2 * v + (1.0 - ADAM_B2) * (g * g)
    m_hat = m / (1.0 - ADAM_B1 ** ADAM_STEP)
    v_hat = v / (1.0 - ADAM_B2 ** ADAM_STEP)
    delta = -ADAM_LR * (m_hat / (jnp.sqrt(v_hat) + ADAM_EPS) + ADAM_WD * w)
    return delta, m, v


def _sum_adamw(parts, w, m, v, *, tr, name, own=None, transposed=False):
    rows, cols = w.shape

    def body(*refs):
        p_ref, w_ref, m_ref, v_ref = refs[:4]
        g_ref, d_ref, nm_ref, nv_ref = refs[-4:]
        g = (p_ref[0] if own is None else refs[4][...]).astype(F32)
        for s in range(1, N_DEV):
            g = g + p_ref[s].astype(F32)
        if transposed:
            g = g.T
        g_ref[...] = g
        d_ref[...], nm_ref[...], nv_ref[...] = _adamw(w_ref[...], g, m_ref[...], v_ref[...])

    extra = [] if own is None else [own]
    if transposed:
        blk = pl.BlockSpec((rows, tr), lambda i: (0, i))
        g_blk, p_blk, steps = pl.BlockSpec((tr, rows), lambda i: (i, 0)), (N_DEV, tr, rows), cols // tr
    else:
        blk = pl.BlockSpec((tr, cols), lambda i: (i, 0))
        g_blk, p_blk, steps = blk, (N_DEV, tr, cols), rows // tr
    return pl.pallas_call(
        body, name=name, grid=(steps,),
        in_specs=[pl.BlockSpec(p_blk, lambda i: (0, i, 0)), blk, blk, blk] + [g_blk] * len(extra),
        out_specs=[blk] * 4, out_shape=[_sds((rows, cols), F32)] * 4,
        compiler_params=_params("parallel"),
    )(parts, w, m, v, *extra)


_RPB_SIZE = 8 * 15 * 31


def _pack_small(g_mix, g_mlp, g_ple, g_final, rpb, loss_row):
    flat = jnp.concatenate([g_mix.reshape(-1), g_mlp.reshape(-1), g_ple.reshape(-1), g_final.reshape(-1),
                            rpb.reshape(-1), jnp.zeros((3840 - _RPB_SIZE,), F32), loss_row.reshape(-1),
                            jnp.zeros((128,), F32)])
    return flat.reshape(64, 128)


def _unpack_small(t):
    flat = t.reshape(-1)
    return (flat[0:1024].reshape(1, 1024), flat[4096:4096 + _RPB_SIZE].reshape(1, 8, 15, 31),
            flat[1024:2048].reshape(1, 1024), flat[2048:3072].reshape(1, 1024), flat[3072:4096])


def kernel(x, p, positions, g_mix, w_in, rpb, w_branch_na, w_branch_dil, w_out, g_mlp, w_up, w_down, g_ple, w_ple_gate, w_ple_proj, g_final, loss_target, m_g_mix, m_w_in, m_rpb, m_w_branch_na, m_w_branch_dil, m_w_out, m_g_mlp, m_w_up, m_w_down, m_g_ple, m_w_ple_gate, m_w_ple_proj, m_g_final, v_g_mix, v_w_in, v_rpb, v_w_branch_na, v_w_branch_dil, v_w_out, v_g_mlp, v_w_up, v_w_down, v_g_ple, v_w_ple_gate, v_w_ple_proj, v_g_final):
    sharded = dict(w_in=(w_in, m_w_in, v_w_in), w_branch_na=(w_branch_na, m_w_branch_na, v_w_branch_na),
                   w_branch_dil=(w_branch_dil, m_w_branch_dil, v_w_branch_dil), w_out=(w_out, m_w_out, v_w_out),
                   w_up=(w_up, m_w_up, v_w_up), w_down=(w_down, m_w_down, v_w_down),
                   w_ple_gate=(w_ple_gate, m_w_ple_gate, v_w_ple_gate),
                   w_ple_proj=(w_ple_proj, m_w_ple_proj, v_w_ple_proj))
    shards = {k: tuple(t[0] for t in val) for k, val in sharded.items()}

    me = _my_index()

    shards["w_in"] = tuple(t.T for t in shards["w_in"])

    w_in_b = _cast_bf16(shards["w_in"][0], name="cast_w_in")
    rest_b = [shards[name][0].astype(BF16).T if axis == 1 else shards[name][0].astype(BF16)
              for name, axis, _ in _WEIGHTS[1:]]
    first_in, token_in = _start_copies(_first_leg_copies, [w_in_b], [_sds((N_DEV,) + w_in_b.shape, BF16)], 4,
                                       name="start_gather_w_in")

    def whole(landed, mine):
        return _to_full(lax.dynamic_update_index_in_dim(landed, mine, me, 0))

    rest = {}

    def get_w_in(after):
        (mine,), landed = _wait_copies(_first_leg_copies, first_in, after, name="wait_gather_w_in")
        second, token = _start_copies(_second_leg_copies, [], landed, 3, name="start_forward_w_in")
        _, (landed,) = _wait_copies(_second_leg_copies, second, token, name="wait_forward_w_in")
        rest["first"], token = _start_copies(_first_leg_copies, rest_b,
                                             [_sds((N_DEV,) + t.shape, BF16) for t in rest_b], 4 * len(rest_b),
                                             name="start_gather_rest", after=landed)
        return whole(landed, mine), token

    def relay_rest(after):
        rest["mine"], landed = _wait_copies(_first_leg_copies, rest["first"], after, name="wait_gather_rest")
        rest["second"], token = _start_copies(_second_leg_copies, [], landed, 3 * len(rest_b),
                                              name="start_forward_rest")
        return token

    def get_rest(after):
        _, landed = _wait_copies(_second_leg_copies, rest["second"], after, name="wait_forward_rest")
        return [whole(t, own) for t, own in zip(landed, rest["mine"])]

    sent = []

    def send_grads(indices, grads):
        chunked = [_to_chunks(i, g) for i, g in zip(indices, grads)]
        handle, token = _start_copies(_exchange_copies, chunked, [_sds(t.shape, BF16) for t in chunked],
                                      7 * len(chunked),
                                      name="start_exchange_" + "_".join(_WEIGHTS[i][0] for i in indices))
        sent.append((indices, handle))
        return token

    g_mix_0 = g_mix + token_in[0:1, 0:1]
    loss, dx, dgs, drpb = _local_step(
        x[0], p[0, 0].astype(BF16), positions[0], loss_target[0],
        g_mix_0, g_mlp, g_ple, g_final.reshape(1, -1), rpb[0], get_w_in, relay_rest, get_rest, send_grads)

    drpb3 = drpb.reshape(8, 16, 32)[:, :15, :31]
    small = _pack_small(dgs[0], dgs[1], dgs[2], dgs[3], drpb3, loss)
    share, done = _start_copies(_gather_copies, [small], [_sds((N_DEV,) + small.shape, F32)], 7,
                                name="start_share_small")

    out = {}
    for indices, handle in sent:
        chunked, landed = _wait_copies(_exchange_copies, handle, done,
                                       name="wait_exchange_" + "_".join(_WEIGHTS[i][0] for i in indices))
        for i, part, mine in zip(indices, landed, chunked):
            name = _WEIGHTS[i][0]
            w, m, v = shards[name]
            own = lax.dynamic_index_in_dim(mine, me, 0, keepdims=False)
            turned = _WEIGHTS[i][1] == 1 and i != _W_IN
            res = _sum_adamw(part, w, m, v, tr=368 if i == _W_IN else 128, name="adamw_" + name, own=own,
                             transposed=turned)
            out[name] = [(t.T if i == _W_IN else t)[None] for t in res]
            done = res[0]
    (small,), (small_landed,) = _wait_copies(_gather_copies, share, done, name="wait_share_small")
    small_all = lax.dynamic_update_index_in_dim(small_landed, small, me, 0)
    small_w = _pack_small(g_mix, g_mlp, g_ple, g_final, rpb, jnp.zeros((128,), F32))
    small_m = _pack_small(m_g_mix, m_g_mlp, m_g_ple, m_g_final, m_rpb, jnp.zeros((128,), F32))
    small_v = _pack_small(v_g_mix, v_g_mlp, v_g_ple, v_g_final, v_rpb, jnp.zeros((128,), F32))
    res = _sum_adamw(small_all, small_w, small_m, small_v, tr=64, name="adamw_small")
    unpacked = [_unpack_small(t) for t in res]
    for i, name in enumerate(("g_mix", "rpb", "g_mlp", "g_ple", "g_final")):
        out[name] = [u[i] for u in unpacked]
    loss_total = res[0][62, 0]

    order = ("g_mix", "w_in", "rpb", "w_branch_na", "w_branch_dil", "w_out", "g_mlp", "w_up", "w_down",
             "g_ple", "w_ple_gate", "w_ple_proj", "g_final")
    grads = [out[k][0] for k in order]
    deltas = [out[k][1] for k in order]
    new_m = [out[k][2] for k in order]
    new_v = [out[k][3] for k in order]
    return (loss_total, dx[None], *grads, *deltas, *new_m, *new_v)
```

```python
import jax
import jax.numpy as jnp
from jax import lax
from jax.experimental import pallas as pl
from jax.experimental.pallas import tpu as pltpu

F32 = jnp.float32
BF16 = jnp.bfloat16

D_MODEL = 1024
HEAD_DIM = 64
GRID_W = 64
NA_WIDTH = 512
DIL_WIDTH = 768
IN_WIDTH = 5888
DIL_DILATIONS = (1, 4, 16)
DIL_RADIUS = 64
NA_WIN_ROWS = 8
RMS_EPS = 1e-6
NEG_INF = -1e30
QK_SCALE = HEAD_DIM ** -0.5

ADAM_LR = 0.001
ADAM_B1 = 0.9
ADAM_B2 = 0.999
ADAM_EPS = 1e-08
ADAM_WD = 0.01
ADAM_STEP = 10

N_DEV = 8
VMEM_LIMIT = 56 * 1024 * 1024
EPILOGUE_ROWS = 256
MESH = pl.DeviceIdType.MESH

NT_DIMS = (((1,), (1,)), ((), ()))
TN_DIMS = (((0,), (0,)), ((), ()))


def _sds(shape, dtype):
    return jax.ShapeDtypeStruct(shape, dtype)


def _params(*sem):
    return pltpu.CompilerParams(dimension_semantics=sem, vmem_limit_bytes=VMEM_LIMIT)


def _rows(tm, width, col=0):
    return pl.BlockSpec((tm, width), lambda i, c=col: (i, c))


def _const(shape):
    zeros = (0,) * len(shape)
    return pl.BlockSpec(shape, lambda i: zeros)


def _my_index():
    return 4 * lax.axis_index("x") + 2 * lax.axis_index("y") + lax.axis_index("c")


def _peer(k):
    x, y, c = lax.axis_index("x"), lax.axis_index("y"), lax.axis_index("c")
    px = 1 - x if k & 4 else x
    py = 1 - y if k & 2 else y
    pc = 1 - c if k & 1 else c
    return (px, py, pc), 4 * px + 2 * py + pc


def _call(body, *, name, grid, in_specs, out_specs, out_shape, scratch_shapes, args, after=None):
    n_in, n_out = len(in_specs), len(out_specs)
    extra = [] if after is None else [after]
    n_x = n_in + len(extra)

    def plain(*refs):
        body(refs[:n_in], refs[n_x:n_x + n_out], refs[n_x + n_out:])

    res = pl.pallas_call(plain, name=name, grid=grid,
                         in_specs=list(in_specs) + [pl.BlockSpec(memory_space=pl.ANY)] * len(extra),
                         out_specs=out_specs, out_shape=out_shape, scratch_shapes=scratch_shapes,
                         compiler_params=_params(*(("arbitrary",) * len(grid))))(*args, *extra)
    return list(res)


_HBM_SPEC = pl.BlockSpec(memory_space=pltpu.HBM)
_SEM_SPEC = pl.BlockSpec(memory_space=pltpu.SEMAPHORE)
_SIDE_EFFECT = pltpu.SideEffectType.DATAFLOW_SIDE_EFFECTING


_FIRST_LEG = (1, 2, 4, 6)
_SECOND_LEG = (2, 4, 6)


def _gather_copies(srcs, lands, send, recv, sending):
    me = _my_index()
    out = []
    for w in range(len(srcs)):
        for k in range(1, N_DEV):
            dev, idx = _peer(k)
            out.append(pltpu.make_async_remote_copy(
                src_ref=srcs[w], dst_ref=lands[w].at[me if sending else idx],
                send_sem=send.at[w * 7 + k - 1], recv_sem=recv.at[w * 7 + k - 1],
                device_id=dev, device_id_type=MESH))
    return out


def _first_leg_copies(srcs, lands, send, recv, sending):
    me = _my_index()
    out = []
    for w in range(len(srcs)):
        for j, k in enumerate(_FIRST_LEG):
            dev, idx = _peer(k)
            out.append(pltpu.make_async_remote_copy(
                src_ref=srcs[w], dst_ref=lands[w].at[me if sending else idx],
                send_sem=send.at[w * 4 + j], recv_sem=recv.at[w * 4 + j],
                device_id=dev, device_id_type=MESH))
    return out


def _second_leg_copies(srcs, lands, send, recv, sending):
    sibling, _ = _peer(1)
    out = []
    for w in range(len(lands)):
        for j, k in enumerate(_SECOND_LEG):
            slot = _peer(k if sending else k ^ 1)[1]
            out.append(pltpu.make_async_remote_copy(
                src_ref=lands[w].at[slot], dst_ref=lands[w].at[slot],
                send_sem=send.at[w * 3 + j], recv_sem=recv.at[w * 3 + j],
                device_id=sibling, device_id_type=MESH))
    return out


def _exchange_copies(srcs, lands, send, recv, sending):
    out = []
    for w in range(len(srcs)):
        for k in range(1, N_DEV):
            dev, idx = _peer(k)
            out.append(pltpu.make_async_remote_copy(
                src_ref=srcs[w].at[idx], dst_ref=lands[w].at[k],
                send_sem=send.at[w * 7 + k - 1], recv_sem=recv.at[w * 7 + k - 1],
                device_id=dev, device_id_type=MESH))
    return out


def _start_copies(make, srcs, lands, n_copies, *, name, after=None):
    n_src, n_buf = len(srcs), len(srcs) + len(lands)
    extra = [] if after is None else [after]

    def body(*refs):
        send, recv = refs[n_buf + len(extra)], refs[n_buf + len(extra) + 1]
        for cp in make(refs[:n_src], refs[n_src:n_buf], send, recv, True):
            cp.start()
        refs[-1][...] = jnp.zeros_like(refs[-1])

    bufs = list(srcs) + [lax.empty(t.shape, t.dtype) if isinstance(t, jax.ShapeDtypeStruct) else t for t in lands]
    res = pl.pallas_call(
        body, name=name,
        out_shape=(pltpu.SemaphoreType.DMA((n_copies,)), pltpu.SemaphoreType.DMA((n_copies,)),
                   *[pltpu.HBM(t.shape, t.dtype) for t in bufs], _sds((8, 128), F32)),
        in_specs=[_HBM_SPEC] * n_buf + [pl.BlockSpec(memory_space=pl.ANY)] * len(extra),
        out_specs=(_SEM_SPEC, _SEM_SPEC, *([_HBM_SPEC] * n_buf), pl.BlockSpec(memory_space=pltpu.VMEM)),
        input_output_aliases={i: 2 + i for i in range(n_buf)},
        compiler_params=pltpu.CompilerParams(has_side_effects=_SIDE_EFFECT),
    )(*[pltpu.with_memory_space_constraint(t, pltpu.HBM) for t in bufs], *extra)
    return (n_src, res[0], res[1], res[2:2 + n_buf]), res[-1]


def _wait_copies(make, handle, after, *, name):
    n_src, send_sems, recv_sems, bufs = handle
    n_buf = len(bufs)

    def body(*refs):
        for cp in make(refs[:n_src], refs[n_src:n_buf], refs[n_buf], refs[n_buf + 1], False):
            cp.wait_send()
            cp.wait_recv()

    res = pl.pallas_call(
        body, name=name,
        out_shape=tuple(pltpu.HBM(t.shape, t.dtype) for t in bufs),
        in_specs=[_HBM_SPEC] * n_buf + [_SEM_SPEC, _SEM_SPEC, pl.BlockSpec(memory_space=pl.ANY)],
        out_specs=tuple([_HBM_SPEC] * n_buf),
        input_output_aliases={i: i for i in range(n_buf)},
        compiler_params=pltpu.CompilerParams(has_side_effects=_SIDE_EFFECT),
    )(*bufs, send_sems, recv_sems, after)
    return list(res[:n_src]), list(res[n_src:])


def _matmul(a, b, *, ta=False, tb=False, out_dtype, tm, tn, tk, name, after=None, extra=(), epilogue=None,
            n_colsum=0, transpose_out=False, n_limit=None, stack_cols=False):
    m, k = (a.shape[1], a.shape[0]) if ta else a.shape
    n = n_limit or (b.shape[0] if tb else b.shape[1])
    tm, tn, tk = min(tm, m), min(tn, n), min(tk, k)
    nk = k // tk
    dims = (((0 if ta else 1,), (1 if tb else 0,)), ((), ()))
    out_dtypes = out_dtype if isinstance(out_dtype, tuple) else (out_dtype,)
    n_tiles = len(out_dtypes)

    def add_colsums(o_refs, sums):
        i = pl.program_id(1)
        for s_ref, val in zip(o_refs[n_tiles:], sums):
            @pl.when(i == 0)
            def _(s_ref=s_ref, val=val):
                s_ref[...] = val

            @pl.when(i > 0)
            def _(s_ref=s_ref, val=val):
                s_ref[...] += val

    def finish(acc, x_refs, o_refs):
        vals = (acc,) if epilogue is None else epilogue(acc, *[r[...] for r in x_refs])
        for o_ref, val in zip(o_refs[:n_tiles], vals[:n_tiles]):
            o_ref[...] = (val.T if transpose_out else val).astype(o_ref.dtype)
        add_colsums(o_refs, vals[n_tiles:])

    chunk = EPILOGUE_ROWS if (nk == 1 and epilogue is not None and not ta and tm % EPILOGUE_ROWS == 0) else None

    def body(ins, outs, acc):
        a_ref, b_ref = ins[:2]
        if chunk is not None:
            sums = None
            for r0 in range(0, tm, chunk):
                part = lax.dot_general(a_ref[r0:r0 + chunk, :], b_ref[...], dims, preferred_element_type=F32)
                vals = epilogue(part, *[r[...] if r.shape[0] == 1 else r[r0:r0 + chunk, :] for r in ins[2:]])
                for o_ref, val in zip(outs[:n_tiles], vals[:n_tiles]):
                    o_ref[r0:r0 + chunk, :] = val.astype(o_ref.dtype)
                sums = vals[n_tiles:] if sums is None else [s + v for s, v in zip(sums, vals[n_tiles:])]
            add_colsums(outs, sums)
            return
        part = lax.dot_general(a_ref[...], b_ref[...], dims, preferred_element_type=F32)
        if nk == 1:
            finish(part, ins[2:], outs)
            return
        acc_ref, = acc
        kk = pl.program_id(2)

        @pl.when(kk == 0)
        def _():
            acc_ref[...] = part

        @pl.when(kk > 0)
        def _():
            acc_ref[...] += part

        @pl.when(kk == nk - 1)
        def _():
            finish(acc_ref[...], ins[2:], outs)

    a_spec = (pl.BlockSpec((tk, tm), lambda j, i, kk: (kk, i)) if ta
              else pl.BlockSpec((tm, tk), lambda j, i, kk: (i, kk)))
    b_spec = (pl.BlockSpec((tn, tk), lambda j, i, kk: (j, kk)) if tb
              else pl.BlockSpec((tk, tn), lambda j, i, kk: (kk, j)))
    tile = pl.BlockSpec((tm, tn), lambda j, i, kk: (i, j))
    row = pl.BlockSpec((1, tn), lambda j, i, kk: (0, j))

    def x_spec(t):
        if isinstance(t, tuple):
            return pl.BlockSpec((None, tm, tn), lambda j, i, kk, lead=t[1]: (lead, i, j))
        return row if t.shape[0] == 1 else tile

    out_tile, out_dims = (pl.BlockSpec((tn, tm), lambda j, i, kk: (j, i)), (n, m)) if transpose_out else (tile, (m, n))
    if stack_cols:
        out_tile, out_dims = pl.BlockSpec((None, tm, tn), lambda j, i, kk: (j, i, 0)), (n // tn, m, tn)
    res = _call(
        body, name=name, grid=(n // tn, m // tm, nk),
        in_specs=[a_spec, b_spec] + [x_spec(t) for t in extra],
        out_specs=[out_tile] * n_tiles + [row] * n_colsum,
        out_shape=[_sds(out_dims, dt) for dt in out_dtypes] + [_sds((1, n), F32)] * n_colsum,
        scratch_shapes=[] if nk == 1 else [pltpu.VMEM((tm, tn), F32)],
        args=(a, b, *[t[0] if isinstance(t, tuple) else t for t in extra]), after=after)
    return res if isinstance(out_dtype, tuple) or n_colsum else res[0]


def _rstd(h):
    return lax.rsqrt(jnp.mean(h * h, axis=-1, keepdims=True) + RMS_EPS)


def _sigmoid(z):
    return 1.0 / (1.0 + jnp.exp(-z))


def _rms_fwd(x, g, *, tm, name):
    n = x.shape[0]

    def body(x_ref, g_ref, o_ref):
        h = x_ref[...]
        o_ref[...] = (h * _rstd(h) * g_ref[...]).astype(BF16)

    return pl.pallas_call(
        body, name=name, grid=(n // tm,),
        in_specs=[_rows(tm, D_MODEL), _const((1, D_MODEL))],
        out_specs=_rows(tm, D_MODEL), out_shape=_sds((n, D_MODEL), BF16),
        compiler_params=_params("parallel"),
    )(x, g)


def _swap_halves(t):
    width = t.shape[1]
    lane = lax.broadcasted_iota(jnp.int32, t.shape, 1)
    return jnp.where((lane & 63) < 32, pltpu.roll(t, width - 32, 1), pltpu.roll(t, 32, 1))


def _dil_spec(dil, tm):
    return pl.BlockSpec((dil, tm // dil, 256), lambda i: (0, i, 0))


def _dil_scratch(tm):
    return pltpu.VMEM((2, tm, 128), F32)


def _load_token_order(src, scr, dil, tm):
    if dil == 1:
        return src[0]
    for j in range(dil):
        for c in range(2):
            scr[c, pl.ds(j, tm // dil, stride=dil), :] = src[j, :, c * 128:(c + 1) * 128]
    return jnp.concatenate([scr[0], scr[1]], axis=1)


def _store_dil_order(val, dst, scr, dil, tm):
    if dil == 1:
        dst[0] = val.astype(dst.dtype)
        return
    for c in range(2):
        scr[c] = val[:, c * 128:(c + 1) * 128]
    for j in range(dil):
        for c in range(2):
            dst[j, :, c * 128:(c + 1) * 128] = scr[c, pl.ds(j, tm // dil, stride=dil), :].astype(dst.dtype)


def _split_proj(proj, cos_t, sin_t, *, tm, name):
    n = proj.shape[0]
    n_dil = len(DIL_DILATIONS)

    def body(*refs):
        na_in = refs[0:3]
        dil_in = refs[3:3 + 3 * n_dil]
        cos_ref, sin_ref = refs[12:14]
        outs = refs[14:]
        na_out = outs[0:3]
        dil_out = outs[3:12]
        scr = outs[12]
        for t in range(3):
            na_out[t][...] = na_in[t][...].astype(BF16)
        cosv, sinv = cos_ref[...], sin_ref[...]
        for t in range(3):
            for gi, dil in enumerate(DIL_DILATIONS):
                val = dil_in[t * n_dil + gi][...]
                if t < 2:
                    val = val * cosv + _swap_halves(val) * sinv
                _store_dil_order(val, dil_out[t * n_dil + gi], scr, dil, tm)

    in_specs = [_rows(tm, NA_WIDTH, c) for c in range(3)]
    in_specs += [_rows(tm, 256, 6 + c) for c in range(9)]
    in_specs += [_rows(tm, 256), _rows(tm, 256)]
    out_specs = [_rows(tm, NA_WIDTH)] * 3
    out_shape = [_sds((n, NA_WIDTH), BF16)] * 3
    for _ in range(3):
        for dil in DIL_DILATIONS:
            out_specs.append(pl.BlockSpec((dil, tm // dil, 256), lambda i: (0, i, 0)))
            out_shape.append(_sds((dil, n // dil, 256), BF16))
    res = pl.pallas_call(
        body, name=name, grid=(n // tm,),
        in_specs=in_specs, out_specs=out_specs, out_shape=out_shape,
        scratch_shapes=[_dil_scratch(tm)],
        compiler_params=_params("parallel"),
    )(*([proj] * 12), cos_t, sin_t)
    return res[0:3], res[3:6], res[6:9], res[9:12]


def _residual_rms_tile(delta, h, g):
    hn = h + delta
    return hn, hn * _rstd(hn) * g


def _gate_bwd_tile(dm, s1, b1, s2, b2):
    return dm * s1, dm * s2, dm * b1 * s1 * (1.0 - s1), dm * b2 * s2 * (1.0 - s2)


def _tail_tile(gt, pp, h2, target, g):
    sg = _sigmoid(gt)
    h3 = h2 + sg * pp
    r3 = _rstd(h3)
    n3 = h3 * r3
    err = n3 * g - target
    loss = 0.5 * jnp.sum(jnp.sum(err * err, axis=-1, keepdims=True) / D_MODEL)
    dy = err / D_MODEL
    dn = dy * g
    dh3 = r3 * (dn - n3 * jnp.mean(dn * n3, axis=-1, keepdims=True))
    return (dh3, dh3 * sg, dh3 * pp * sg * (1.0 - sg),
            jnp.sum(dy * n3, axis=0, keepdims=True), jnp.full((1, gt.shape[1]), loss, F32))


def _rms_bwd_tile(dz, h, g, dres):
    r = _rstd(h)
    nrm = h * r
    dn = dz * g
    dh = dres + r * (dn - nrm * jnp.mean(dn * nrm, axis=-1, keepdims=True))
    return dh, jnp.sum(dz * nrm, axis=0, keepdims=True)


def _rms_bwd_twice(dz, h, g, dres):
    dh, dg = _rms_bwd_tile(dz, h, g, dres)
    return dh, dh, dg


def _assemble_dproj(dna, ddil_q, ddil_k, ddil_v, dgn, dgd, cos_t, sin_t, *, tm, name):
    n = dgn.shape[0]

    def body(*refs):
        dq_ref, dk_ref, dv_ref = refs[0:3]
        dil_in = refs[3:12]
        dgn_ref, dgd_ref, cos_ref, sin_ref, o_ref, scr = refs[12:18]
        o_ref[:, 0:512] = dq_ref[...]
        o_ref[:, 512:1024] = dk_ref[...].astype(BF16)
        o_ref[:, 1024:1536] = dv_ref[...].astype(BF16)
        cosv, sinv = cos_ref[...], sin_ref[...]
        for t in range(3):
            for gi, dil in enumerate(DIL_DILATIONS):
                val = _load_token_order(dil_in[t * 3 + gi], scr, dil, tm)
                if t < 2:
                    val = val * cosv + _swap_halves(val * sinv)
                c0 = 1536 + t * DIL_WIDTH + gi * 256
                o_ref[:, c0:c0 + 256] = val.astype(BF16)
        o_ref[:, 3840:4864] = dgn_ref[...]
        o_ref[:, 4864:5888] = dgd_ref[...]

    in_specs = [_rows(tm, NA_WIDTH)] * 3
    for _ in range(3):
        for dil in DIL_DILATIONS:
            in_specs.append(pl.BlockSpec((dil, tm // dil, 256), lambda i: (0, i, 0)))
    in_specs += [_rows(tm, D_MODEL)] * 2 + [_rows(tm, 256)] * 2
    return pl.pallas_call(
        body, name=name, grid=(n // tm,), in_specs=in_specs,
        out_specs=_rows(tm, IN_WIDTH), out_shape=_sds((n, IN_WIDTH), BF16),
        scratch_shapes=[_dil_scratch(tm)],
        compiler_params=_params("parallel"),
    )(*dna, *ddil_q, *ddil_k, *ddil_v, dgn, dgd, cos_t, sin_t)


N_ROW_OFF = 2 * NA_WIN_ROWS - 1
N_PAIRS = N_ROW_OFF - 1
RB_WIDTH = (N_ROW_OFF + 1) * GRID_W


def _na_bias(rb_ref, pair_scr):
    shape = (GRID_W, RB_WIDTH)
    qc = lax.broadcasted_iota(jnp.int32, shape, 0)
    qc2 = lax.broadcasted_iota(jnp.int32, (GRID_W, 128), 0)
    kc2 = lax.broadcasted_iota(jnp.int32, (GRID_W, 128), 1) & (GRID_W - 1)
    cs = jnp.clip(qc2 - 8, 0, GRID_W - 16)
    valid = (kc2 >= cs) & (kc2 < cs + 16)
    for hh in range(2):
        t = jnp.broadcast_to(rb_ref[hh], shape)
        t = pltpu.roll(t, RB_WIDTH - 15, 1)
        for b in range(6):
            t = jnp.where(((qc >> b) & 1) == 1, pltpu.roll(t, 1 << b, 1), t)
        t_odd = pltpu.roll(t, RB_WIDTH - GRID_W, 1)
        for ro in range(N_PAIRS):
            src = t if ro % 2 == 0 else t_odd
            base = (ro // 2) * 128
            pair_scr[hh, ro] = jnp.where(valid, src[:, base:base + 128], NEG_INF)


NA_GROUP_FWD = 4
NA_GROUP_BWD = 4


def _stack_heads(ref, r, scale=1.0):
    lane = lax.broadcasted_iota(jnp.int32, (GRID_W, 128), 1)
    t = ref[pl.ds(pl.multiple_of(r * GRID_W, GRID_W), GRID_W), :].astype(F32) * scale
    return jnp.concatenate([jnp.where(lane < 64, t, 0.0), jnp.where(lane >= 64, t, 0.0)], axis=0).astype(BF16)


def _unstack_heads(t2):
    lane = lax.broadcasted_iota(jnp.int32, (GRID_W, 128), 1)
    return jnp.where(lane < 64, t2[:GRID_W], t2[GRID_W:])


def _na_window(k_ref, v_ref, r, n_rows):
    rs = jnp.clip(r - NA_WIN_ROWS // 2, 0, n_rows - NA_WIN_ROWS)
    ro0 = (NA_WIN_ROWS - 1) - (r - rs)
    off = pl.multiple_of(rs * GRID_W, GRID_W)
    kw = k_ref[pl.ds(off, NA_WIN_ROWS * GRID_W), :]
    vw = v_ref[pl.ds(off, NA_WIN_ROWS * GRID_W), :]
    return kw, vw, off, ro0


def _na_probs(s_raw, pair_scr, ro0):
    bias = [jnp.concatenate([pair_scr[hh, ro0 + 2 * j] for j in range(NA_WIN_ROWS // 2)], axis=1)
            for hh in range(2)]
    s = s_raw + jnp.concatenate(bias, axis=0)
    m = jnp.max(s, axis=-1, keepdims=True)
    e = jnp.exp(s - m)
    return e * (1.0 / jnp.sum(e, axis=-1, keepdims=True))


def _na_fwd(q, k, v, rb, *, name, after=None):
    n = q.shape[0]
    n_rows = n // GRID_W

    def body(ins, outs, scr):
        q_ref, k_ref, v_ref, rb_ref = ins
        o_ref, = outs
        pair_scr, = scr
        _na_bias(rb_ref, pair_scr)

        def group(g, carry):
            rows = [g * NA_GROUP_FWD + t for t in range(NA_GROUP_FWD)]
            wins = [_na_window(k_ref, v_ref, r, n_rows) for r in rows]
            raw = [lax.dot_general(_stack_heads(q_ref, r, QK_SCALE), w[0], NT_DIMS, preferred_element_type=F32)
                   for r, w in zip(rows, wins)]
            probs = [_na_probs(s, pair_scr, w[3]) for s, w in zip(raw, wins)]
            outs2 = [jnp.dot(p.astype(BF16), w[1], preferred_element_type=F32) for p, w in zip(probs, wins)]
            for r, o2 in zip(rows, outs2):
                o_ref[pl.ds(pl.multiple_of(r * GRID_W, GRID_W), GRID_W), :] = _unstack_heads(o2).astype(BF16)
            return carry

        lax.fori_loop(0, n_rows // NA_GROUP_FWD, group, 0)

    col = pl.BlockSpec((n, 128), lambda h: (0, h))
    return _call(
        body, name=name, grid=(NA_WIDTH // 128,),
        in_specs=[col, col, col, pl.BlockSpec((2, 1, RB_WIDTH), lambda h: (h, 0, 0))],
        out_specs=[col], out_shape=[_sds((n, NA_WIDTH), BF16)],
        scratch_shapes=[pltpu.VMEM((2, N_PAIRS, GRID_W, 128), F32)],
        args=(q, k, v, rb), after=after)[0]


def _na_bwd(q, k, v, do, rb, *, name):
    n = q.shape[0]
    n_rows = n // GRID_W
    win = NA_WIN_ROWS * GRID_W

    def body(ins, outs, scr):
        q_ref, k_ref, v_ref, do_ref, rb_ref = ins
        dq_ref, dk_ref, dv_ref, drb_ref = outs
        pair_scr, acc_scr = scr
        _na_bias(rb_ref, pair_scr)
        acc_scr[...] = jnp.zeros_like(acc_scr)
        dk_ref[...] = jnp.zeros_like(dk_ref)
        dv_ref[...] = jnp.zeros_like(dv_ref)

        def group(g, carry):
            rows = [g * NA_GROUP_BWD + t for t in range(NA_GROUP_BWD)]
            wins = [_na_window(k_ref, v_ref, r, n_rows) for r in rows]
            qss = [_stack_heads(q_ref, r, QK_SCALE) for r in rows]
            doss = [_stack_heads(do_ref, r) for r in rows]
            raw = [lax.dot_general(qs, w[0], NT_DIMS, preferred_element_type=F32) for qs, w in zip(qss, wins)]
            dps = [lax.dot_general(dos, w[1], NT_DIMS, preferred_element_type=F32) for dos, w in zip(doss, wins)]
            probs = [_na_probs(s, pair_scr, w[3]) for s, w in zip(raw, wins)]
            dss = [p * (dp - jnp.sum(p * dp, axis=-1, keepdims=True)) for p, dp in zip(probs, dps)]
            dsbs = [ds.astype(BF16) for ds in dss]
            dq2s = [jnp.dot(dsb, w[0], preferred_element_type=F32) for dsb, w in zip(dsbs, wins)]
            dkws = [lax.dot_general(dsb, qs, TN_DIMS, preferred_element_type=F32) for dsb, qs in zip(dsbs, qss)]
            dvws = [lax.dot_general(p.astype(BF16), dos, TN_DIMS, preferred_element_type=F32)
                    for p, dos in zip(probs, doss)]
            for t, r in enumerate(rows):
                _, _, off, ro0 = wins[t]
                for hh in range(2):
                    for j in range(NA_WIN_ROWS // 2):
                        acc_scr[hh, ro0 + 2 * j] += dss[t][hh * GRID_W:(hh + 1) * GRID_W, j * 128:(j + 1) * 128]
                dq_ref[pl.ds(pl.multiple_of(r * GRID_W, GRID_W), GRID_W), :] = (
                    _unstack_heads(dq2s[t]) * QK_SCALE).astype(BF16)
                dk_ref[pl.ds(off, win), :] += dkws[t]
                dv_ref[pl.ds(off, win), :] += dvws[t]
            return carry

        lax.fori_loop(0, n_rows // NA_GROUP_BWD, group, 0)

        qc = lax.broadcasted_iota(jnp.int32, (N_PAIRS * GRID_W, 128), 0)
        for hh in range(2):
            t = acc_scr[hh].reshape(N_PAIRS * GRID_W, 128)
            for b in range(6):
                t = jnp.where(((qc >> b) & 1) == 1, pltpu.roll(t, 128 - (1 << b), 1), t)
            t = pltpu.roll(t, 15, 1)
            drb_ref[hh] = jnp.sum(t.reshape(N_PAIRS, GRID_W, 128), axis=1)

    col = pl.BlockSpec((n, 128), lambda h: (0, h))
    return _call(
        body, name=name, grid=(NA_WIDTH // 128,),
        in_specs=[col, col, col, col, pl.BlockSpec((2, 1, RB_WIDTH), lambda h: (h, 0, 0))],
        out_specs=[col, col, col, pl.BlockSpec((2, N_PAIRS, 128), lambda h: (h, 0, 0))],
        out_shape=[_sds((n, NA_WIDTH), BF16), _sds((n, NA_WIDTH), F32), _sds((n, NA_WIDTH), F32),
                   _sds((8, N_PAIRS, 128), F32)],
        scratch_shapes=[pltpu.VMEM((2, N_PAIRS, GRID_W, 128), F32),
                        pltpu.VMEM((2, N_PAIRS, GRID_W, 128), F32)],
        args=(q, k, v, do, rb))


def _rpb_table(rpb2):
    t = jnp.pad(rpb2, ((0, 0), (0, 1), (0, GRID_W - rpb2.shape[-1])))
    return t.reshape(8, 1, RB_WIDTH)


def _rpb_grad(drb, *, name):
    kdim = drb.shape[1]

    def body(x_ref, o_ref):
        kk = lax.broadcasted_iota(jnp.int32, (128, 512), 0)
        jj = lax.broadcasted_iota(jnp.int32, (128, 512), 1)
        half, co = kk >> 6, kk & 63
        acc = jnp.zeros((8, 512), F32)
        for ro in range(N_PAIRS):
            hit = ((ro + half) == (jj >> 5)) & (co == (jj & 31)) & (co < 31)
            onehot = jnp.where(hit, 1.0, 0.0).astype(F32)
            acc = acc + jnp.dot(x_ref[:, ro * 128:(ro + 1) * 128], onehot, preferred_element_type=F32,
                                precision=lax.Precision.HIGHEST)
        o_ref[...] = acc

    return pl.pallas_call(
        body, name=name, grid=(1,),
        in_specs=[_const((8, kdim))], out_specs=_const((8, 512)), out_shape=_sds((8, 512), F32),
        compiler_params=_params("arbitrary"),
    )(drb)


DIL_GROUP = 2


def _dil_blocks(length):
    qb = min(128, length)
    return qb, min(qb + 2 * DIL_RADIUS, length), min(DIL_GROUP, length // qb)


def _stack_lanes(ref, t, qb, scale=1.0):
    lane = lax.broadcasted_iota(jnp.int32, (qb, 256), 1)
    val = ref[0, t * qb:(t + 1) * qb, :].astype(F32) * scale
    return jnp.concatenate([jnp.where((lane >> 6) == h, val, 0.0) for h in range(4)], axis=0).astype(BF16)


def _dil_window(k_ref, v_ref, blk, qb, win, length):
    start = pl.multiple_of(jnp.clip(blk * qb - DIL_RADIUS, 0, length - win), DIL_RADIUS)
    return k_ref[0, pl.ds(start, win), :], v_ref[0, pl.ds(start, win), :], start


def _dil_mask(s, blk, start, qb, win):
    gap = ((lax.broadcasted_iota(jnp.int32, (4 * qb, win), 0) & (qb - 1))
           - lax.broadcasted_iota(jnp.int32, (4 * qb, win), 1)) + (blk * qb - start)
    return jnp.where(jnp.abs(gap) <= DIL_RADIUS, s, NEG_INF)


def _pick_heads(stacked, qb):
    lane = lax.broadcasted_iota(jnp.int32, (qb, 256), 1)
    out = jnp.zeros((qb, 256), stacked.dtype)
    for h in range(4):
        out = jnp.where((lane >> 6) == h, stacked[h * qb:(h + 1) * qb], out)
    return out


def _stack_head_cols(ref, t, qb):
    return jnp.concatenate([ref[0, t * qb:(t + 1) * qb, 64 * h:64 * h + 1] for h in range(4)], axis=0)


def _dil_fwd(q, k, v, *, name):
    dil, length, _ = q.shape
    qb, win, grp = _dil_blocks(length)

    def body(q_ref, k_ref, v_ref, o_ref, lse_ref):
        blks = [pl.program_id(1) * grp + t for t in range(grp)]
        wins = [_dil_window(k_ref, v_ref, b, qb, win, length) for b in blks]
        raw = [lax.dot_general(_stack_lanes(q_ref, t, qb, QK_SCALE), w[0], NT_DIMS, preferred_element_type=F32)
               for t, w in enumerate(wins)]
        lses, outs = [], []
        for t, (s, w) in enumerate(zip(raw, wins)):
            s = _dil_mask(s, blks[t], w[2], qb, win)
            m = jnp.max(s, axis=-1, keepdims=True)
            e = jnp.exp(s - m)
            norm = jnp.sum(e, axis=-1, keepdims=True)
            lses.append(m + jnp.log(norm))
            outs.append(jnp.dot((e * (1.0 / norm)).astype(BF16), w[1], preferred_element_type=F32))
        for t in range(grp):
            o_ref[0, t * qb:(t + 1) * qb, :] = _pick_heads(outs[t], qb)
            lse_ref[0, t * qb:(t + 1) * qb, :] = _pick_heads(jnp.broadcast_to(lses[t], (4 * qb, 256)), qb)

    seq = pl.BlockSpec((1, length, 256), lambda j, i: (j, 0, 0))
    blk = pl.BlockSpec((1, grp * qb, 256), lambda j, i: (j, i, 0))
    return pl.pallas_call(
        body, name=name, grid=(dil, length // (grp * qb)),
        in_specs=[blk, seq, seq], out_specs=[blk, blk],
        out_shape=[_sds((dil, length, 256), F32)] * 2,
        compiler_params=_params("parallel", "parallel"),
    )(q, k, v)


def _dil_bwd(q, k, v, do, lse, cc, *, name):
    dil, length, _ = q.shape
    qb, win, grp = _dil_blocks(length)

    def body(q_ref, k_ref, v_ref, do_ref, lse_ref, cc_ref, dq_ref, dk_ref, dv_ref):
        @pl.when(pl.program_id(1) == 0)
        def _():
            dk_ref[...] = jnp.zeros_like(dk_ref)
            dv_ref[...] = jnp.zeros_like(dv_ref)

        blks = [pl.program_id(1) * grp + t for t in range(grp)]
        wins = [_dil_window(k_ref, v_ref, b, qb, win, length) for b in blks]
        qss = [_stack_lanes(q_ref, t, qb, QK_SCALE) for t in range(grp)]
        doss = [_stack_lanes(do_ref, t, qb) for t in range(grp)]
        raw = [lax.dot_general(qs, w[0], NT_DIMS, preferred_element_type=F32) for qs, w in zip(qss, wins)]
        dps = [lax.dot_general(dos, w[1], NT_DIMS, preferred_element_type=F32) for dos, w in zip(doss, wins)]
        probs = [jnp.exp(_dil_mask(s, blks[t], wins[t][2], qb, win) - _stack_head_cols(lse_ref, t, qb))
                 for t, s in enumerate(raw)]
        dsbs = [(p * (dp + _stack_head_cols(cc_ref, t, qb))).astype(BF16)
                for t, (p, dp) in enumerate(zip(probs, dps))]
        dq4s = [jnp.dot(dsb, w[0], preferred_element_type=F32) for dsb, w in zip(dsbs, wins)]
        dkws = [lax.dot_general(dsb, qs, TN_DIMS, preferred_element_type=F32) for dsb, qs in zip(dsbs, qss)]
        dvws = [lax.dot_general(p.astype(BF16), dos, TN_DIMS, preferred_element_type=F32)
                for p, dos in zip(probs, doss)]
        for t in range(grp):
            dq_ref[0, t * qb:(t + 1) * qb, :] = _pick_heads(dq4s[t], qb) * QK_SCALE
            dk_ref[0, pl.ds(wins[t][2], win), :] += dkws[t]
            dv_ref[0, pl.ds(wins[t][2], win), :] += dvws[t]

    seq = pl.BlockSpec((1, length, 256), lambda j, i: (j, 0, 0))
    blk = pl.BlockSpec((1, grp * qb, 256), lambda j, i: (j, i, 0))
    return pl.pallas_call(
        body, name=name, grid=(dil, length // (grp * qb)),
        in_specs=[blk, seq, seq, blk, blk, blk], out_specs=[blk, seq, seq],
        out_shape=[_sds((dil, length, 256), F32)] * 3,
        compiler_params=_params("parallel", "arbitrary"),
    )(q, k, v, do, lse, cc)


def _merge_weights(lses):
    m = jnp.maximum(jnp.maximum(lses[0], lses[1]), lses[2])
    es = [jnp.exp(t - m) for t in lses]
    inv = 1.0 / (es[0] + es[1] + es[2])
    return [e * inv for e in es]


def _dil_merge(outs, lses, *, tm, name):
    n = outs[0].shape[1]

    def body(*refs):
        o_in, l_in = refs[0:3], refs[3:6]
        y_ref, yb_ref, scr = refs[6:9]
        lv = [_load_token_order(l_in[g], scr, d, tm) for g, d in enumerate(DIL_DILATIONS)]
        ws = _merge_weights(lv)
        y = jnp.zeros((tm, 256), F32)
        for g, d in enumerate(DIL_DILATIONS):
            y = y + ws[g] * _load_token_order(o_in[g], scr, d, tm)
        y_ref[...] = y
        yb_ref[...] = y.astype(BF16)

    specs = [_dil_spec(d, tm) for d in DIL_DILATIONS]
    return pl.pallas_call(
        body, name=name, grid=(n // tm,), in_specs=specs + specs,
        out_specs=[_rows(tm, 256)] * 2, out_shape=[_sds((n, 256), F32), _sds((n, 256), BF16)],
        scratch_shapes=[_dil_scratch(tm)],
        compiler_params=_params("parallel"),
    )(*outs, *lses)


def _dil_merge_bwd(dy, y, lses, *, tm, name):
    n = dy.shape[0]

    def body(*refs):
        dy_ref, y_ref = refs[0:2]
        l_in = refs[2:5]
        do_out, cc_out = refs[5:8], refs[8:11]
        scr = refs[11]
        lv = [_load_token_order(l_in[g], scr, d, tm) for g, d in enumerate(DIL_DILATIONS)]
        ws = _merge_weights(lv)
        dyv = dy_ref[...]
        rr = lax.broadcasted_iota(jnp.int32, (256, 256), 0) >> 6
        cc = lax.broadcasted_iota(jnp.int32, (256, 256), 1) >> 6
        ones = jnp.where(rr == cc, 1.0, 0.0).astype(F32)
        tsum = jnp.dot(dyv * y_ref[...], ones, preferred_element_type=F32,
                       precision=lax.Precision.HIGHEST)
        for g, d in enumerate(DIL_DILATIONS):
            _store_dil_order(ws[g] * dyv, do_out[g], scr, d, tm)
            _store_dil_order(-ws[g] * tsum, cc_out[g], scr, d, tm)

    specs = [_dil_spec(d, tm) for d in DIL_DILATIONS]
    res = pl.pallas_call(
        body, name=name, grid=(n // tm,),
        in_specs=[_rows(tm, 256)] * 2 + specs,
        out_specs=specs + specs,
        out_shape=[_sds((d, n // d, 256), BF16) for d in DIL_DILATIONS]
                  + [_sds((d, n // d, 256), F32) for d in DIL_DILATIONS],
        scratch_shapes=[_dil_scratch(tm)],
        compiler_params=_params("parallel"),
    )(dy, y, *lses)
    return res[0:3], res[3:6]


_WEIGHTS = (("w_in", 1, 736), ("w_branch_na", 1, 128), ("w_branch_dil", 1, 128), ("w_out", 0, 128),
            ("w_up", 1, 512), ("w_down", 0, 512), ("w_ple_gate", 0, 128), ("w_ple_proj", 1, 128))
_W_IN, _W_BNA, _W_BD, _W_OUT, _W_UP, _W_DOWN, _W_PG, _W_PP = range(8)


def _to_full(gathered):
    return gathered.reshape(-1, gathered.shape[2])


def _to_chunks(widx, mat):
    return mat.reshape(N_DEV, _WEIGHTS[widx][2], mat.shape[1])


def _local_step(x, p_bf16, positions, target, g_mix, g_mlp, g_ple, g_final, rpb2,
                get_w_in, relay_rest, get_rest, send_grads):
    tm = 256
    half = HEAD_DIM // 2
    inv_freq = 10000.0 ** (-jnp.arange(half, dtype=F32) / half)
    ang = positions.astype(F32)[:, None] * inv_freq
    cos, sin = jnp.cos(ang), jnp.sin(ang)
    cos_t = jnp.tile(jnp.concatenate([cos, cos], axis=-1), (1, 4))
    sin_t = jnp.tile(jnp.concatenate([-sin, sin], axis=-1), (1, 4))
    rb = _rpb_table(rpb2)

    a = _rms_fwd(x, g_mix, tm=tm, name="rms_mix")
    w_in, token = get_w_in(a)
    qkv_width = 3 * NA_WIDTH + 3 * DIL_WIDTH
    proj = _matmul(a, w_in, tb=True, n_limit=qkv_width, out_dtype=F32, tm=512, tn=qkv_width // 2, tk=1024,
                   name="mm_in_qkv", after=token)
    gates = _matmul(a, w_in[qkv_width:], tb=True, stack_cols=True, out_dtype=F32, tm=512, tn=D_MODEL, tk=1024,
                    name="mm_in_gates", epilogue=lambda acc: (_sigmoid(acc),))
    sn, sd = (gates, 0), (gates, 1)
    na_qkv, dq_g, dk_g, dv_g = _split_proj(proj, cos_t, sin_t, tm=tm, name="split_proj")
    y_na = _na_fwd(*na_qkv, rb, name="na_fwd", after=relay_rest(na_qkv[0]))
    d_out, d_lse = [], []
    for g in range(3):
        o, lse = _dil_fwd(dq_g[g], dk_g[g], dv_g[g], name=f"dil_fwd{g}")
        d_out.append(o)
        d_lse.append(lse)
    y_dil, y_dil_b = _dil_merge(d_out, d_lse, tm=tm, name="dil_merge")
    w_bna, w_bd, w_out, w_up, w_down, w_pg, w_pp = get_rest(y_dil_b)
    bn = _matmul(y_na, w_bna, tb=True, out_dtype=F32, tm=512, tn=1024, tk=512, name="mm_bna")
    bd, mixed = _matmul(y_dil_b, w_bd, tb=True, out_dtype=(F32, BF16), tm=512, tn=1024, tk=256, name="mm_bd",
                        extra=(sn, bn, sd), epilogue=lambda acc, s1, b1, s2: (acc, s1 * b1 + s2 * acc))
    h1, c = _matmul(mixed, w_out, out_dtype=(F32, BF16), tm=512, tn=1024, tk=1024, name="mm_out",
                    extra=(x, g_mlp), epilogue=_residual_rms_tile)
    u, f = _matmul(c, w_up, tb=True, out_dtype=(F32, BF16), tm=512, tn=2048, tk=1024, name="mm_up",
                   epilogue=lambda acc: (acc, jnp.square(jnp.maximum(acc, 0.0))))
    h2, e = _matmul(f, w_down, out_dtype=(F32, BF16), tm=512, tn=1024, tk=4096, name="mm_down",
                    extra=(h1, g_ple), epilogue=_residual_rms_tile)
    pp = _matmul(p_bf16, w_pp, tb=True, out_dtype=F32, tm=512, tn=1024, tk=256, name="mm_pp")

    dh3, dpp, dgt, dg_final, loss = _matmul(
        e, w_pg, out_dtype=(F32, BF16, BF16), tm=512, tn=1024, tk=1024, name="mm_pg_tail",
        extra=(pp, h2, target, g_final), epilogue=_tail_tile, n_colsum=2)
    loss = loss[:, :128]
    gw_pp = _matmul(p_bf16, dpp, ta=True, transpose_out=True, out_dtype=BF16, tm=256, tn=1024, tk=2048,
                    name="mm_gw_pp")
    gw_pg = _matmul(e, dgt, ta=True, out_dtype=BF16, tm=512, tn=1024, tk=2048, name="mm_gw_pg")
    dh2, dh2_b, dg_ple = _matmul(
        dgt, w_pg, tb=True, out_dtype=(F32, BF16), tm=512, tn=1024, tk=1024, name="mm_de",
        extra=(h2, g_ple, dh3), epilogue=_rms_bwd_twice, n_colsum=1)
    du = _matmul(dh2_b, w_down, tb=True, out_dtype=BF16, tm=512, tn=2048, tk=1024, name="mm_du",
                 extra=(u,), epilogue=lambda acc, uv: (acc * (2.0 * jnp.maximum(uv, 0.0)),))
    gw_down = _matmul(f, dh2_b, ta=True, out_dtype=BF16, tm=1024, tn=1024, tk=2048, name="mm_gw_down")
    gw_up = _matmul(c, du, ta=True, transpose_out=True, out_dtype=BF16, tm=512, tn=2048, tk=2048, name="mm_gw_up")
    dh1, dh1_b, dg_mlp = _matmul(
        du, w_up, out_dtype=(F32, BF16), tm=512, tn=1024, tk=4096, name="mm_dc",
        extra=(h1, g_mlp, dh2), epilogue=_rms_bwd_twice, n_colsum=1)
    dbn, dbd, dgn, dgd = _matmul(dh1_b, w_out, tb=True, out_dtype=(BF16,) * 4, tm=512, tn=1024, tk=1024,
                                 name="mm_dmixed", extra=(sn, bn, sd, bd), epilogue=_gate_bwd_tile)
    gw_out = _matmul(mixed, dh1_b, ta=True, out_dtype=BF16, tm=512, tn=1024, tk=2048, name="mm_gw_out")
    gw_bna = _matmul(y_na, dbn, ta=True, transpose_out=True, out_dtype=BF16, tm=512, tn=1024, tk=2048,
                     name="mm_gw_bna")
    dy_na = _matmul(dbn, w_bna, out_dtype=BF16, tm=512, tn=512, tk=1024, name="mm_dy_na")
    gw_bd = _matmul(y_dil_b, dbd, ta=True, transpose_out=True, out_dtype=BF16, tm=256, tn=1024, tk=2048,
                    name="mm_gw_bd")
    token = send_grads((_W_PP, _W_PG, _W_DOWN, _W_UP, _W_OUT, _W_BNA, _W_BD),
                       (gw_pp, gw_pg, gw_down, gw_up, gw_out, gw_bna, gw_bd))
    dy_dil = _matmul(dbd, w_bd, out_dtype=F32, tm=512, tn=256, tk=1024, name="mm_dy_dil", after=token)
    dna = _na_bwd(*na_qkv, dy_na, rb, name="na_bwd")
    drpb = _rpb_grad(dna[3].reshape(8, -1), name="rpb_grad")
    do_g, cc_g = _dil_merge_bwd(dy_dil, y_dil, d_lse, tm=tm, name="dil_merge_bwd")
    ddq, ddk, ddv = [], [], []
    for g in range(3):
        r = _dil_bwd(dq_g[g], dk_g[g], dv_g[g], do_g[g], d_lse[g], cc_g[g], name=f"dil_bwd{g}")
        ddq.append(r[0])
        ddk.append(r[1])
        ddv.append(r[2])
    dproj = _assemble_dproj(dna[0:3], ddq, ddk, ddv, dgn, dgd, cos_t, sin_t, tm=tm, name="assemble_dproj")
    gw_in = _matmul(a, dproj, ta=True, transpose_out=True, out_dtype=BF16, tm=512, tn=2944, tk=2048, name="mm_gw_in")
    token = send_grads((_W_IN,), (gw_in,))
    dx, dg_mix = _matmul(
        dproj, w_in, out_dtype=(F32,), tm=512, tn=1024, tk=5888, name="mm_da", after=token,
        extra=(x, g_mix, dh1), epilogue=_rms_bwd_tile, n_colsum=1)
    return loss, dx, (dg_mix, dg_mlp, dg_ple, dg_final), drpb


def _cast_bf16(t, *, name):
    def body(t_ref, o_ref):
        o_ref[...] = t_ref[...].astype(BF16)

    rows, cols = t.shape
    tr = 256 if rows % 256 == 0 else rows
    blk = pl.BlockSpec((tr, cols), lambda i: (i, 0))
    return pl.pallas_call(body, name=name, grid=(rows // tr,), in_specs=[blk], out_specs=blk,
                          out_shape=_sds(t.shape, BF16), compiler_params=_params("parallel"))(t)


def _adamw(w, g, m, v):
    m = ADAM_B1 * m + (1.0 - ADAM_B1) * g
    v = ADAM_B2 * v + (1.0 - ADAM_B2) * (g * g)
    m_hat = m / (1.0 - ADAM_B1 ** ADAM_STEP)
    v_hat = v / (1.0 - ADAM_B2 ** ADAM_STEP)
    delta = -ADAM_LR * (m_hat / (jnp.sqrt(v_hat) + ADAM_EPS) + ADAM_WD * w)
    return delta, m, v


def _sum_adamw(parts, w, m, v, *, tr, name, own=None, transposed=False):
    rows, cols = w.shape

    def body(*refs):
        p_ref, w_ref, m_ref, v_ref = refs[:4]
        g_ref, d_ref, nm_ref, nv_ref = refs[-4:]
        g = (p_ref[0] if own is None else refs[4][...]).astype(F32)
        for s in range(1, N_DEV):
            g = g + p_ref[s].astype(F32)
        if transposed:
            g = g.T
        g_ref[...] = g
        d_ref[...], nm_ref[...], nv_ref[...] = _adamw(w_ref[...], g, m_ref[...], v_ref[...])

    extra = [] if own is None else [own]
    if transposed:
        blk = pl.BlockSpec((rows, tr), lambda i: (0, i))
        g_blk, p_blk, steps = pl.BlockSpec((tr, rows), lambda i: (i, 0)), (N_DEV, tr, rows), cols // tr
    else:
        blk = pl.BlockSpec((tr, cols), lambda i: (i, 0))
        g_blk, p_blk, steps = blk, (N_DEV, tr, cols), rows // tr
    return pl.pallas_call(
        body, name=name, grid=(steps,),
        in_specs=[pl.BlockSpec(p_blk, lambda i: (0, i, 0)), blk, blk, blk] + [g_blk] * len(extra),
        out_specs=[blk] * 4, out_shape=[_sds((rows, cols), F32)] * 4,
        compiler_params=_params("parallel"),
    )(parts, w, m, v, *extra)


_RPB_SIZE = 8 * 15 * 31


def _pack_small(g_mix, g_mlp, g_ple, g_final, rpb, loss_row):
    flat = jnp.concatenate([g_mix.reshape(-1), g_mlp.reshape(-1), g_ple.reshape(-1), g_final.reshape(-1),
                            rpb.reshape(-1), jnp.zeros((3840 - _RPB_SIZE,), F32), loss_row.reshape(-1),
                            jnp.zeros((128,), F32)])
    return flat.reshape(64, 128)


def _unpack_small(t):
    flat = t.reshape(-1)
    return (flat[0:1024].reshape(1, 1024), flat[4096:4096 + _RPB_SIZE].reshape(1, 8, 15, 31),
            flat[1024:2048].reshape(1, 1024), flat[2048:3072].reshape(1, 1024), flat[3072:4096])


def kernel(x, p, positions, g_mix, w_in, rpb, w_branch_na, w_branch_dil, w_out, g_mlp, w_up, w_down, g_ple, w_ple_gate, w_ple_proj, g_final, loss_target, m_g_mix, m_w_in, m_rpb, m_w_branch_na, m_w_branch_dil, m_w_out, m_g_mlp, m_w_up, m_w_down, m_g_ple, m_w_ple_gate, m_w_ple_proj, m_g_final, v_g_mix, v_w_in, v_rpb, v_w_branch_na, v_w_branch_dil, v_w_out, v_g_mlp, v_w_up, v_w_down, v_g_ple, v_w_ple_gate, v_w_ple_proj, v_g_final):
    sharded = dict(w_in=(w_in, m_w_in, v_w_in), w_branch_na=(w_branch_na, m_w_branch_na, v_w_branch_na),
                   w_branch_dil=(w_branch_dil, m_w_branch_dil, v_w_branch_dil), w_out=(w_out, m_w_out, v_w_out),
                   w_up=(w_up, m_w_up, v_w_up), w_down=(w_down, m_w_down, v_w_down),
                   w_ple_gate=(w_ple_gate, m_w_ple_gate, v_w_ple_gate),
                   w_ple_proj=(w_ple_proj, m_w_ple_proj, v_w_ple_proj))
    shards = {k: tuple(t[0] for t in val) for k, val in sharded.items()}

    me = _my_index()

    shards["w_in"] = tuple(t.T for t in shards["w_in"])

    w_in_b = _cast_bf16(shards["w_in"][0], name="cast_w_in")
    rest_b = [shards[name][0].astype(BF16).T if axis == 1 else shards[name][0].astype(BF16)
              for name, axis, _ in _WEIGHTS[1:]]
    first_in, token_in = _start_copies(_first_leg_copies, [w_in_b], [_sds((N_DEV,) + w_in_b.shape, BF16)], 4,
                                       name="start_gather_w_in")

    def whole(landed, mine):
        return _to_full(lax.dynamic_update_index_in_dim(landed, mine, me, 0))

    rest = {}

    def get_w_in(after):
        (mine,), landed = _wait_copies(_first_leg_copies, first_in, after, name="wait_gather_w_in")
        second, token = _start_copies(_second_leg_copies, [], landed, 3, name="start_forward_w_in")
        _, (landed,) = _wait_copies(_second_leg_copies, second, token, name="wait_forward_w_in")
        rest["first"], token = _start_copies(_first_leg_copies, rest_b,
                                             [_sds((N_DEV,) + t.shape, BF16) for t in rest_b], 4 * len(rest_b),
                                             name="start_gather_rest", after=landed)
        return whole(landed, mine), token

    def relay_rest(after):
        rest["mine"], landed = _wait_copies(_first_leg_copies, rest["first"], after, name="wait_gather_rest")
        rest["second"], token = _start_copies(_second_leg_copies, [], landed, 3 * len(rest_b),
                                              name="start_forward_rest")
        return token

    def get_rest(after):
        _, landed = _wait_copies(_second_leg_copies, rest["second"], after, name="wait_forward_rest")
        return [whole(t, own) for t, own in zip(landed, rest["mine"])]

    sent = []

    def send_grads(indices, grads):
        chunked = [_to_chunks(i, g) for i, g in zip(indices, grads)]
        handle, token = _start_copies(_exchange_copies, chunked, [_sds(t.shape, BF16) for t in chunked],
                                      7 * len(chunked),
                                      name="start_exchange_" + ("w_in" if indices == (_W_IN,) else "rest"))
        sent.append((indices, handle))
        return token

    g_mix_0 = g_mix + token_in[0:1, 0:1]
    loss, dx, dgs, drpb = _local_step(
        x[0], p[0, 0].astype(BF16), positions[0], loss_target[0],
        g_mix_0, g_mlp, g_ple, g_final.reshape(1, -1), rpb[0], get_w_in, relay_rest, get_rest, send_grads)

    drpb3 = drpb.reshape(8, 16, 32)[:, :15, :31]
    small = _pack_small(dgs[0], dgs[1], dgs[2], dgs[3], drpb3, loss)
    share, done = _start_copies(_gather_copies, [small], [_sds((N_DEV,) + small.shape, F32)], 7,
                                name="start_share_small")

    out = {}
    for indices, handle in sent:
        chunked, landed = _wait_copies(_exchange_copies, handle, done,
                                       name="wait_exchange_" + ("w_in" if indices == (_W_IN,) else "rest"))
        for i, part, mine in zip(indices, landed, chunked):
            name = _WEIGHTS[i][0]
            w, m, v = shards[name]
            own = lax.dynamic_index_in_dim(mine, me, 0, keepdims=False)
            turned = _WEIGHTS[i][1] == 1 and i != _W_IN
            res = _sum_adamw(part, w, m, v, tr=368 if i == _W_IN else 128, name="adamw_" + name, own=own,
                             transposed=turned)
            out[name] = [(t.T if i == _W_IN else t)[None] for t in res]
            done = res[0]
    (small,), (small_landed,) = _wait_copies(_gather_copies, share, done, name="wait_share_small")
    small_all = lax.dynamic_update_index_in_dim(small_landed, small, me, 0)
    small_w = _pack_small(g_mix, g_mlp, g_ple, g_final, rpb, jnp.zeros((128,), F32))
    small_m = _pack_small(m_g_mix, m_g_mlp, m_g_ple, m_g_final, m_rpb, jnp.zeros((128,), F32))
    small_v = _pack_small(v_g_mix, v_g_mlp, v_g_ple, v_g_final, v_rpb, jnp.zeros((128,), F32))
    res = _sum_adamw(small_all, small_w, small_m, small_v, tr=64, name="adamw_small")
    unpacked = [_unpack_small(t) for t in res]
    for i, name in enumerate(("g_mix", "rpb", "g_mlp", "g_ple", "g_final")):
        out[name] = [u[i] for u in unpacked]
    loss_total = res[0][62, 0]

    order = ("g_mix", "w_in", "rpb", "w_branch_na", "w_branch_dil", "w_out", "g_mlp", "w_up", "w_down",
             "g_ple", "w_ple_gate", "w_ple_proj", "g_final")
    grads = [out[k][0] for k in order]
    deltas = [out[k][1] for k in order]
    new_m = [out[k][2] for k in order]
    new_v = [out[k][3] for k in order]
    return (loss_total, dx[None], *grads, *deltas, *new_m, *new_v)
```

```python
import jax
import jax.numpy as jnp
from jax import lax
from jax.experimental import pallas as pl
from jax.experimental.pallas import tpu as pltpu

F32 = jnp.float32
BF16 = jnp.bfloat16

D_MODEL = 1024
HEAD_DIM = 64
GRID_W = 64
NA_WIDTH = 512
DIL_WIDTH = 768
IN_WIDTH = 5888
DIL_DILATIONS = (1, 4, 16)
DIL_RADIUS = 64
NA_WIN_ROWS = 8
RMS_EPS = 1e-6
NEG_INF = -1e30
QK_SCALE = HEAD_DIM ** -0.5

ADAM_LR = 0.001
ADAM_B1 = 0.9
ADAM_B2 = 0.999
ADAM_EPS = 1e-08
ADAM_WD = 0.01
ADAM_STEP = 10

N_DEV = 8
VMEM_LIMIT = 56 * 1024 * 1024
EPILOGUE_ROWS = 256
MESH = pl.DeviceIdType.MESH

NT_DIMS = (((1,), (1,)), ((), ()))
TN_DIMS = (((0,), (0,)), ((), ()))


def _sds(shape, dtype):
    return jax.ShapeDtypeStruct(shape, dtype)


def _params(*sem):
    return pltpu.CompilerParams(dimension_semantics=sem, vmem_limit_bytes=VMEM_LIMIT)


def _rows(tm, width, col=0):
    return pl.BlockSpec((tm, width), lambda i, c=col: (i, c))


def _const(shape):
    zeros = (0,) * len(shape)
    return pl.BlockSpec(shape, lambda i: zeros)


def _my_index():
    return 4 * lax.axis_index("x") + 2 * lax.axis_index("y") + lax.axis_index("c")


def _peer(k):
    x, y, c = lax.axis_index("x"), lax.axis_index("y"), lax.axis_index("c")
    px = 1 - x if k & 4 else x
    py = 1 - y if k & 2 else y
    pc = 1 - c if k & 1 else c
    return (px, py, pc), 4 * px + 2 * py + pc


def _call(body, *, name, grid, in_specs, out_specs, out_shape, scratch_shapes, args, after=None):
    n_in, n_out = len(in_specs), len(out_specs)
    extra = [] if after is None else [after]
    n_x = n_in + len(extra)

    def plain(*refs):
        body(refs[:n_in], refs[n_x:n_x + n_out], refs[n_x + n_out:])

    res = pl.pallas_call(plain, name=name, grid=grid,
                         in_specs=list(in_specs) + [pl.BlockSpec(memory_space=pl.ANY)] * len(extra),
                         out_specs=out_specs, out_shape=out_shape, scratch_shapes=scratch_shapes,
                         compiler_params=_params(*(("arbitrary",) * len(grid))))(*args, *extra)
    return list(res)


_HBM_SPEC = pl.BlockSpec(memory_space=pltpu.HBM)
_SEM_SPEC = pl.BlockSpec(memory_space=pltpu.SEMAPHORE)
_SIDE_EFFECT = pltpu.SideEffectType.DATAFLOW_SIDE_EFFECTING


_FIRST_LEG = (1, 2, 4, 6)
_SECOND_LEG = (2, 4, 6)


def _gather_copies(srcs, lands, send, recv, sending):
    me = _my_index()
    out = []
    for w in range(len(srcs)):
        for k in range(1, N_DEV):
            dev, idx = _peer(k)
            out.append(pltpu.make_async_remote_copy(
                src_ref=srcs[w], dst_ref=lands[w].at[me if sending else idx],
                send_sem=send.at[w * 7 + k - 1], recv_sem=recv.at[w * 7 + k - 1],
                device_id=dev, device_id_type=MESH))
    return out


def _first_leg_copies(srcs, lands, send, recv, sending):
    me = _my_index()
    out = []
    for w in range(len(srcs)):
        for j, k in enumerate(_FIRST_LEG):
            dev, idx = _peer(k)
            out.append(pltpu.make_async_remote_copy(
                src_ref=srcs[w], dst_ref=lands[w].at[me if sending else idx],
                send_sem=send.at[w * 4 + j], recv_sem=recv.at[w * 4 + j],
                device_id=dev, device_id_type=MESH))
    return out


def _second_leg_copies(srcs, lands, send, recv, sending):
    sibling, _ = _peer(1)
    out = []
    for w in range(len(lands)):
        for j, k in enumerate(_SECOND_LEG):
            slot = _peer(k if sending else k ^ 1)[1]
            out.append(pltpu.make_async_remote_copy(
                src_ref=lands[w].at[slot], dst_ref=lands[w].at[slot],
                send_sem=send.at[w * 3 + j], recv_sem=recv.at[w * 3 + j],
                device_id=sibling, device_id_type=MESH))
    return out


def _exchange_copies(srcs, lands, send, recv, sending):
    out = []
    for w in range(len(srcs)):
        for k in range(1, N_DEV):
            dev, idx = _peer(k)
            out.append(pltpu.make_async_remote_copy(
                src_ref=srcs[w].at[idx], dst_ref=lands[w].at[k],
                send_sem=send.at[w * 7 + k - 1], recv_sem=recv.at[w * 7 + k - 1],
                device_id=dev, device_id_type=MESH))
    return out


def _start_copies(make, srcs, lands, n_copies, *, name, after=None):
    n_src, n_buf = len(srcs), len(srcs) + len(lands)
    extra = [] if after is None else [after]

    def body(*refs):
        send, recv = refs[n_buf + len(extra)], refs[n_buf + len(extra) + 1]
        for cp in make(refs[:n_src], refs[n_src:n_buf], send, recv, True):
            cp.start()
        refs[-1][...] = jnp.zeros_like(refs[-1])

    bufs = list(srcs) + [lax.empty(t.shape, t.dtype) if isinstance(t, jax.ShapeDtypeStruct) else t for t in lands]
    res = pl.pallas_call(
        body, name=name,
        out_shape=(pltpu.SemaphoreType.DMA((n_copies,)), pltpu.SemaphoreType.DMA((n_copies,)),
                   *[pltpu.HBM(t.shape, t.dtype) for t in bufs], _sds((8, 128), F32)),
        in_specs=[_HBM_SPEC] * n_buf + [pl.BlockSpec(memory_space=pl.ANY)] * len(extra),
        out_specs=(_SEM_SPEC, _SEM_SPEC, *([_HBM_SPEC] * n_buf), pl.BlockSpec(memory_space=pltpu.VMEM)),
        input_output_aliases={i: 2 + i for i in range(n_buf)},
        compiler_params=pltpu.CompilerParams(has_side_effects=_SIDE_EFFECT),
    )(*[pltpu.with_memory_space_constraint(t, pltpu.HBM) for t in bufs], *extra)
    return (n_src, res[0], res[1], res[2:2 + n_buf]), res[-1]


def _wait_copies(make, handle, after, *, name):
    n_src, send_sems, recv_sems, bufs = handle
    n_buf = len(bufs)

    def body(*refs):
        for cp in make(refs[:n_src], refs[n_src:n_buf], refs[n_buf], refs[n_buf + 1], False):
            cp.wait_send()
            cp.wait_recv()

    res = pl.pallas_call(
        body, name=name,
        out_shape=tuple(pltpu.HBM(t.shape, t.dtype) for t in bufs),
        in_specs=[_HBM_SPEC] * n_buf + [_SEM_SPEC, _SEM_SPEC, pl.BlockSpec(memory_space=pl.ANY)],
        out_specs=tuple([_HBM_SPEC] * n_buf),
        input_output_aliases={i: i for i in range(n_buf)},
        compiler_params=pltpu.CompilerParams(has_side_effects=_SIDE_EFFECT),
    )(*bufs, send_sems, recv_sems, after)
    return list(res[:n_src]), list(res[n_src:])


def _matmul(a, b, *, ta=False, tb=False, out_dtype, tm, tn, tk, name, after=None, extra=(), epilogue=None,
            n_colsum=0, transpose_out=False, n_limit=None, stack_cols=False):
    m, k = (a.shape[1], a.shape[0]) if ta else a.shape
    n = n_limit or (b.shape[0] if tb else b.shape[1])
    tm, tn, tk = min(tm, m), min(tn, n), min(tk, k)
    nk = k // tk
    dims = (((0 if ta else 1,), (1 if tb else 0,)), ((), ()))
    out_dtypes = out_dtype if isinstance(out_dtype, tuple) else (out_dtype,)
    n_tiles = len(out_dtypes)

    def add_colsums(o_refs, sums):
        i = pl.program_id(1)
        for s_ref, val in zip(o_refs[n_tiles:], sums):
            @pl.when(i == 0)
            def _(s_ref=s_ref, val=val):
                s_ref[...] = val

            @pl.when(i > 0)
            def _(s_ref=s_ref, val=val):
                s_ref[...] += val

    def finish(acc, x_refs, o_refs):
        vals = (acc,) if epilogue is None else epilogue(acc, *[r[...] for r in x_refs])
        for o_ref, val in zip(o_refs[:n_tiles], vals[:n_tiles]):
            o_ref[...] = (val.T if transpose_out else val).astype(o_ref.dtype)
        add_colsums(o_refs, vals[n_tiles:])

    chunk = EPILOGUE_ROWS if (nk == 1 and epilogue is not None and not ta and tm % EPILOGUE_ROWS == 0) else None

    def body(ins, outs, acc):
        a_ref, b_ref = ins[:2]
        if chunk is not None:
            sums = None
            for r0 in range(0, tm, chunk):
                part = lax.dot_general(a_ref[r0:r0 + chunk, :], b_ref[...], dims, preferred_element_type=F32)
                vals = epilogue(part, *[r[...] if r.shape[0] == 1 else r[r0:r0 + chunk, :] for r in ins[2:]])
                for o_ref, val in zip(outs[:n_tiles], vals[:n_tiles]):
                    o_ref[r0:r0 + chunk, :] = val.astype(o_ref.dtype)
                sums = vals[n_tiles:] if sums is None else [s + v for s, v in zip(sums, vals[n_tiles:])]
            add_colsums(outs, sums)
            return
        part = lax.dot_general(a_ref[...], b_ref[...], dims, preferred_element_type=F32)
        if nk == 1:
            finish(part, ins[2:], outs)
            return
        acc_ref, = acc
        kk = pl.program_id(2)

        @pl.when(kk == 0)
        def _():
            acc_ref[...] = part

        @pl.when(kk > 0)
        def _():
            acc_ref[...] += part

        @pl.when(kk == nk - 1)
        def _():
            finish(acc_ref[...], ins[2:], outs)

    a_spec = (pl.BlockSpec((tk, tm), lambda j, i, kk: (kk, i)) if ta
              else pl.BlockSpec((tm, tk), lambda j, i, kk: (i, kk)))
    b_spec = (pl.BlockSpec((tn, tk), lambda j, i, kk: (j, kk)) if tb
              else pl.BlockSpec((tk, tn), lambda j, i, kk: (kk, j)))
    tile = pl.BlockSpec((tm, tn), lambda j, i, kk: (i, j))
    row = pl.BlockSpec((1, tn), lambda j, i, kk: (0, j))

    def x_spec(t):
        if isinstance(t, tuple):
            return pl.BlockSpec((None, tm, tn), lambda j, i, kk, lead=t[1]: (lead, i, j))
        return row if t.shape[0] == 1 else tile

    out_tile, out_dims = (pl.BlockSpec((tn, tm), lambda j, i, kk: (j, i)), (n, m)) if transpose_out else (tile, (m, n))
    if stack_cols:
        out_tile, out_dims = pl.BlockSpec((None, tm, tn), lambda j, i, kk: (j, i, 0)), (n // tn, m, tn)
    res = _call(
        body, name=name, grid=(n // tn, m // tm, nk),
        in_specs=[a_spec, b_spec] + [x_spec(t) for t in extra],
        out_specs=[out_tile] * n_tiles + [row] * n_colsum,
        out_shape=[_sds(out_dims, dt) for dt in out_dtypes] + [_sds((1, n), F32)] * n_colsum,
        scratch_shapes=[] if nk == 1 else [pltpu.VMEM((tm, tn), F32)],
        args=(a, b, *[t[0] if isinstance(t, tuple) else t for t in extra]), after=after)
    return res if isinstance(out_dtype, tuple) or n_colsum else res[0]


def _rstd(h):
    return lax.rsqrt(jnp.mean(h * h, axis=-1, keepdims=True) + RMS_EPS)


def _sigmoid(z):
    return 1.0 / (1.0 + jnp.exp(-z))


def _rms_fwd(x, g, *, tm, name):
    n = x.shape[0]

    def body(x_ref, g_ref, o_ref):
        h = x_ref[...]
        o_ref[...] = (h * _rstd(h) * g_ref[...]).astype(BF16)

    return pl.pallas_call(
        body, name=name, grid=(n // tm,),
        in_specs=[_rows(tm, D_MODEL), _const((1, D_MODEL))],
        out_specs=_rows(tm, D_MODEL), out_shape=_sds((n, D_MODEL), BF16),
        compiler_params=_params("parallel"),
    )(x, g)


def _swap_halves(t):
    width = t.shape[1]
    lane = lax.broadcasted_iota(jnp.int32, t.shape, 1)
    return jnp.where((lane & 63) < 32, pltpu.roll(t, width - 32, 1), pltpu.roll(t, 32, 1))


def _dil_spec(dil, tm):
    return pl.BlockSpec((dil, tm // dil, 256), lambda i: (0, i, 0))


def _dil_scratch(tm):
    return pltpu.VMEM((2, tm, 128), F32)


def _load_token_order(src, scr, dil, tm):
    if dil == 1:
        return src[0]
    for j in range(dil):
        for c in range(2):
            scr[c, pl.ds(j, tm // dil, stride=dil), :] = src[j, :, c * 128:(c + 1) * 128]
    return jnp.concatenate([scr[0], scr[1]], axis=1)


def _store_dil_order(val, dst, scr, dil, tm):
    if dil == 1:
        dst[0] = val.astype(dst.dtype)
        return
    for c in range(2):
        scr[c] = val[:, c * 128:(c + 1) * 128]
    for j in range(dil):
        for c in range(2):
            dst[j, :, c * 128:(c + 1) * 128] = scr[c, pl.ds(j, tm // dil, stride=dil), :].astype(dst.dtype)


def _split_proj(proj, cos_t, sin_t, *, tm, name):
    n = proj.shape[0]
    n_dil = len(DIL_DILATIONS)

    def body(*refs):
        na_in = refs[0:3]
        dil_in = refs[3:3 + 3 * n_dil]
        cos_ref, sin_ref = refs[12:14]
        outs = refs[14:]
        na_out = outs[0:3]
        dil_out = outs[3:12]
        scr = outs[12]
        for t in range(3):
            na_out[t][...] = na_in[t][...].astype(BF16)
        cosv, sinv = cos_ref[...], sin_ref[...]
        for t in range(3):
            for gi, dil in enumerate(DIL_DILATIONS):
                val = dil_in[t * n_dil + gi][...]
                if t < 2:
                    val = val * cosv + _swap_halves(val) * sinv
                _store_dil_order(val, dil_out[t * n_dil + gi], scr, dil, tm)

    in_specs = [_rows(tm, NA_WIDTH, c) for c in range(3)]
    in_specs += [_rows(tm, 256, 6 + c) for c in range(9)]
    in_specs += [_rows(tm, 256), _rows(tm, 256)]
    out_specs = [_rows(tm, NA_WIDTH)] * 3
    out_shape = [_sds((n, NA_WIDTH), BF16)] * 3
    for _ in range(3):
        for dil in DIL_DILATIONS:
            out_specs.append(pl.BlockSpec((dil, tm // dil, 256), lambda i: (0, i, 0)))
            out_shape.append(_sds((dil, n // dil, 256), BF16))
    res = pl.pallas_call(
        body, name=name, grid=(n // tm,),
        in_specs=in_specs, out_specs=out_specs, out_shape=out_shape,
        scratch_shapes=[_dil_scratch(tm)],
        compiler_params=_params("parallel"),
    )(*([proj] * 12), cos_t, sin_t)
    return res[0:3], res[3:6], res[6:9], res[9:12]


def _residual_rms_tile(delta, h, g):
    hn = h + delta
    return hn, hn * _rstd(hn) * g


def _gate_bwd_tile(dm, s1, b1, s2, b2):
    return dm * s1, dm * s2, dm * b1 * s1 * (1.0 - s1), dm * b2 * s2 * (1.0 - s2)


def _tail_tile(gt, pp, h2, target, g):
    sg = _sigmoid(gt)
    h3 = h2 + sg * pp
    r3 = _rstd(h3)
    n3 = h3 * r3
    err = n3 * g - target
    loss = 0.5 * jnp.sum(jnp.sum(err * err, axis=-1, keepdims=True) / D_MODEL)
    dy = err / D_MODEL
    dn = dy * g
    dh3 = r3 * (dn - n3 * jnp.mean(dn * n3, axis=-1, keepdims=True))
    return (dh3, dh3 * sg, dh3 * pp * sg * (1.0 - sg),
            jnp.sum(dy * n3, axis=0, keepdims=True), jnp.full((1, gt.shape[1]), loss, F32))


def _rms_bwd_tile(dz, h, g, dres):
    r = _rstd(h)
    nrm = h * r
    dn = dz * g
    dh = dres + r * (dn - nrm * jnp.mean(dn * nrm, axis=-1, keepdims=True))
    return dh, jnp.sum(dz * nrm, axis=0, keepdims=True)


def _rms_bwd_twice(dz, h, g, dres):
    dh, dg = _rms_bwd_tile(dz, h, g, dres)
    return dh, dh, dg


def _assemble_dproj(dna, ddil_q, ddil_k, ddil_v, dgn, dgd, cos_t, sin_t, *, tm, name):
    n = dgn.shape[0]

    def body(*refs):
        dq_ref, dk_ref, dv_ref = refs[0:3]
        dil_in = refs[3:12]
        dgn_ref, dgd_ref, cos_ref, sin_ref, o_ref, scr = refs[12:18]
        o_ref[:, 0:512] = dq_ref[...]
        o_ref[:, 512:1024] = dk_ref[...].astype(BF16)
        o_ref[:, 1024:1536] = dv_ref[...].astype(BF16)
        cosv, sinv = cos_ref[...], sin_ref[...]
        for t in range(3):
            for gi, dil in enumerate(DIL_DILATIONS):
                val = _load_token_order(dil_in[t * 3 + gi], scr, dil, tm)
                if t < 2:
                    val = val * cosv + _swap_halves(val * sinv)
                c0 = 1536 + t * DIL_WIDTH + gi * 256
                o_ref[:, c0:c0 + 256] = val.astype(BF16)
        o_ref[:, 3840:4864] = dgn_ref[...]
        o_ref[:, 4864:5888] = dgd_ref[...]

    in_specs = [_rows(tm, NA_WIDTH)] * 3
    for _ in range(3):
        for dil in DIL_DILATIONS:
            in_specs.append(pl.BlockSpec((dil, tm // dil, 256), lambda i: (0, i, 0)))
    in_specs += [_rows(tm, D_MODEL)] * 2 + [_rows(tm, 256)] * 2
    return pl.pallas_call(
        body, name=name, grid=(n // tm,), in_specs=in_specs,
        out_specs=_rows(tm, IN_WIDTH), out_shape=_sds((n, IN_WIDTH), BF16),
        scratch_shapes=[_dil_scratch(tm)],
        compiler_params=_params("parallel"),
    )(*dna, *ddil_q, *ddil_k, *ddil_v, dgn, dgd, cos_t, sin_t)


N_ROW_OFF = 2 * NA_WIN_ROWS - 1
N_PAIRS = N_ROW_OFF - 1
RB_WIDTH = (N_ROW_OFF + 1) * GRID_W


def _na_bias(rb_ref, pair_scr):
    shape = (GRID_W, RB_WIDTH)
    qc = lax.broadcasted_iota(jnp.int32, shape, 0)
    qc2 = lax.broadcasted_iota(jnp.int32, (GRID_W, 128), 0)
    kc2 = lax.broadcasted_iota(jnp.int32, (GRID_W, 128), 1) & (GRID_W - 1)
    cs = jnp.clip(qc2 - 8, 0, GRID_W - 16)
    valid = (kc2 >= cs) & (kc2 < cs + 16)
    for hh in range(2):
        t = jnp.broadcast_to(rb_ref[hh], shape)
        t = pltpu.roll(t, RB_WIDTH - 15, 1)
        for b in range(6):
            t = jnp.where(((qc >> b) & 1) == 1, pltpu.roll(t, 1 << b, 1), t)
        t_odd = pltpu.roll(t, RB_WIDTH - GRID_W, 1)
        for ro in range(N_PAIRS):
            src = t if ro % 2 == 0 else t_odd
            base = (ro // 2) * 128
            pair_scr[hh, ro] = jnp.where(valid, src[:, base:base + 128], NEG_INF)


NA_GROUP_FWD = 4
NA_GROUP_BWD = 4


def _stack_heads(ref, r, scale=1.0):
    lane = lax.broadcasted_iota(jnp.int32, (GRID_W, 128), 1)
    t = ref[pl.ds(pl.multiple_of(r * GRID_W, GRID_W), GRID_W), :].astype(F32) * scale
    return jnp.concatenate([jnp.where(lane < 64, t, 0.0), jnp.where(lane >= 64, t, 0.0)], axis=0).astype(BF16)


def _unstack_heads(t2):
    lane = lax.broadcasted_iota(jnp.int32, (GRID_W, 128), 1)
    return jnp.where(lane < 64, t2[:GRID_W], t2[GRID_W:])


def _na_window(k_ref, v_ref, r, n_rows):
    rs = jnp.clip(r - NA_WIN_ROWS // 2, 0, n_rows - NA_WIN_ROWS)
    ro0 = (NA_WIN_ROWS - 1) - (r - rs)
    off = pl.multiple_of(rs * GRID_W, GRID_W)
    kw = k_ref[pl.ds(off, NA_WIN_ROWS * GRID_W), :]
    vw = v_ref[pl.ds(off, NA_WIN_ROWS * GRID_W), :]
    return kw, vw, off, ro0


def _na_probs(s_raw, pair_scr, ro0):
    bias = [jnp.concatenate([pair_scr[hh, ro0 + 2 * j] for j in range(NA_WIN_ROWS // 2)], axis=1)
            for hh in range(2)]
    s = s_raw + jnp.concatenate(bias, axis=0)
    m = jnp.max(s, axis=-1, keepdims=True)
    e = jnp.exp(s - m)
    return e * (1.0 / jnp.sum(e, axis=-1, keepdims=True))


def _na_fwd(q, k, v, rb, *, name, after=None):
    n = q.shape[0]
    n_rows = n // GRID_W

    def body(ins, outs, scr):
        q_ref, k_ref, v_ref, rb_ref = ins
        o_ref, = outs
        pair_scr, = scr
        _na_bias(rb_ref, pair_scr)

        def group(g, carry):
            rows = [g * NA_GROUP_FWD + t for t in range(NA_GROUP_FWD)]
            wins = [_na_window(k_ref, v_ref, r, n_rows) for r in rows]
            raw = [lax.dot_general(_stack_heads(q_ref, r, QK_SCALE), w[0], NT_DIMS, preferred_element_type=F32)
                   for r, w in zip(rows, wins)]
            probs = [_na_probs(s, pair_scr, w[3]) for s, w in zip(raw, wins)]
            outs2 = [jnp.dot(p.astype(BF16), w[1], preferred_element_type=F32) for p, w in zip(probs, wins)]
            for r, o2 in zip(rows, outs2):
                o_ref[pl.ds(pl.multiple_of(r * GRID_W, GRID_W), GRID_W), :] = _unstack_heads(o2).astype(BF16)
            return carry

        lax.fori_loop(0, n_rows // NA_GROUP_FWD, group, 0)

    col = pl.BlockSpec((n, 128), lambda h: (0, h))
    return _call(
        body, name=name, grid=(NA_WIDTH // 128,),
        in_specs=[col, col, col, pl.BlockSpec((2, 1, RB_WIDTH), lambda h: (h, 0, 0))],
        out_specs=[col], out_shape=[_sds((n, NA_WIDTH), BF16)],
        scratch_shapes=[pltpu.VMEM((2, N_PAIRS, GRID_W, 128), F32)],
        args=(q, k, v, rb), after=after)[0]


def _na_bwd(q, k, v, do, rb, *, name):
    n = q.shape[0]
    n_rows = n // GRID_W
    win = NA_WIN_ROWS * GRID_W

    def body(ins, outs, scr):
        q_ref, k_ref, v_ref, do_ref, rb_ref = ins
        dq_ref, dk_ref, dv_ref, drb_ref = outs
        pair_scr, acc_scr = scr
        _na_bias(rb_ref, pair_scr)
        acc_scr[...] = jnp.zeros_like(acc_scr)
        dk_ref[...] = jnp.zeros_like(dk_ref)
        dv_ref[...] = jnp.zeros_like(dv_ref)

        def group(g, carry):
            rows = [g * NA_GROUP_BWD + t for t in range(NA_GROUP_BWD)]
            wins = [_na_window(k_ref, v_ref, r, n_rows) for r in rows]
            qss = [_stack_heads(q_ref, r, QK_SCALE) for r in rows]
            doss = [_stack_heads(do_ref, r) for r in rows]
            raw = [lax.dot_general(qs, w[0], NT_DIMS, preferred_element_type=F32) for qs, w in zip(qss, wins)]
            dps = [lax.dot_general(dos, w[1], NT_DIMS, preferred_element_type=F32) for dos, w in zip(doss, wins)]
            probs = [_na_probs(s, pair_scr, w[3]) for s, w in zip(raw, wins)]
            dss = [p * (dp - jnp.sum(p * dp, axis=-1, keepdims=True)) for p, dp in zip(probs, dps)]
            dsbs = [ds.astype(BF16) for ds in dss]
            dq2s = [jnp.dot(dsb, w[0], preferred_element_type=F32) for dsb, w in zip(dsbs, wins)]
            dkws = [lax.dot_general(dsb, qs, TN_DIMS, preferred_element_type=F32) for dsb, qs in zip(dsbs, qss)]
            dvws = [lax.dot_general(p.astype(BF16), dos, TN_DIMS, preferred_element_type=F32)
                    for p, dos in zip(probs, doss)]
            for t, r in enumerate(rows):
                _, _, off, ro0 = wins[t]
                for hh in range(2):
                    for j in range(NA_WIN_ROWS // 2):
                        acc_scr[hh, ro0 + 2 * j] += dss[t][hh * GRID_W:(hh + 1) * GRID_W, j * 128:(j + 1) * 128]
                dq_ref[pl.ds(pl.multiple_of(r * GRID_W, GRID_W), GRID_W), :] = (
                    _unstack_heads(dq2s[t]) * QK_SCALE).astype(BF16)
                dk_ref[pl.ds(off, win), :] += dkws[t]
                dv_ref[pl.ds(off, win), :] += dvws[t]
            return carry

        lax.fori_loop(0, n_rows // NA_GROUP_BWD, group, 0)

        qc = lax.broadcasted_iota(jnp.int32, (N_PAIRS * GRID_W, 128), 0)
        for hh in range(2):
            t = acc_scr[hh].reshape(N_PAIRS * GRID_W, 128)
            for b in range(6):
                t = jnp.where(((qc >> b) & 1) == 1, pltpu.roll(t, 128 - (1 << b), 1), t)
            t = pltpu.roll(t, 15, 1)
            drb_ref[hh] = jnp.sum(t.reshape(N_PAIRS, GRID_W, 128), axis=1)

    col = pl.BlockSpec((n, 128), lambda h: (0, h))
    return _call(
        body, name=name, grid=(NA_WIDTH // 128,),
        in_specs=[col, col, col, col, pl.BlockSpec((2, 1, RB_WIDTH), lambda h: (h, 0, 0))],
        out_specs=[col, col, col, pl.BlockSpec((2, N_PAIRS, 128), lambda h: (h, 0, 0))],
        out_shape=[_sds((n, NA_WIDTH), BF16), _sds((n, NA_WIDTH), F32), _sds((n, NA_WIDTH), F32),
                   _sds((8, N_PAIRS, 128), F32)],
        scratch_shapes=[pltpu.VMEM((2, N_PAIRS, GRID_W, 128), F32),
                        pltpu.VMEM((2, N_PAIRS, GRID_W, 128), F32)],
        args=(q, k, v, do, rb))


def _rpb_table(rpb2):
    t = jnp.pad(rpb2, ((0, 0), (0, 1), (0, GRID_W - rpb2.shape[-1])))
    return t.reshape(8, 1, RB_WIDTH)


def _rpb_grad(drb, *, name):
    kdim = drb.shape[1]

    def body(x_ref, o_ref):
        kk = lax.broadcasted_iota(jnp.int32, (128, 512), 0)
        jj = lax.broadcasted_iota(jnp.int32, (128, 512), 1)
        half, co = kk >> 6, kk & 63
        acc = jnp.zeros((8, 512), F32)
        for ro in range(N_PAIRS):
            hit = ((ro + half) == (jj >> 5)) & (co == (jj & 31)) & (co < 31)
            onehot = jnp.where(hit, 1.0, 0.0).astype(F32)
            acc = acc + jnp.dot(x_ref[:, ro * 128:(ro + 1) * 128], onehot, preferred_element_type=F32,
                                precision=lax.Precision.HIGHEST)
        o_ref[...] = acc

    return pl.pallas_call(
        body, name=name, grid=(1,),
        in_specs=[_const((8, kdim))], out_specs=_const((8, 512)), out_shape=_sds((8, 512), F32),
        compiler_params=_params("arbitrary"),
    )(drb)


DIL_GROUP = 2


def _dil_blocks(length):
    qb = min(128, length)
    return qb, min(qb + 2 * DIL_RADIUS, length), min(DIL_GROUP, length // qb)


def _stack_lanes(ref, t, qb, scale=1.0):
    lane = lax.broadcasted_iota(jnp.int32, (qb, 256), 1)
    val = ref[0, t * qb:(t + 1) * qb, :].astype(F32) * scale
    return jnp.concatenate([jnp.where((lane >> 6) == h, val, 0.0) for h in range(4)], axis=0).astype(BF16)


def _dil_window(k_ref, v_ref, blk, qb, win, length):
    start = pl.multiple_of(jnp.clip(blk * qb - DIL_RADIUS, 0, length - win), DIL_RADIUS)
    return k_ref[0, pl.ds(start, win), :], v_ref[0, pl.ds(start, win), :], start


def _dil_mask(s, blk, start, qb, win):
    gap = ((lax.broadcasted_iota(jnp.int32, (4 * qb, win), 0) & (qb - 1))
           - lax.broadcasted_iota(jnp.int32, (4 * qb, win), 1)) + (blk * qb - start)
    return jnp.where(jnp.abs(gap) <= DIL_RADIUS, s, NEG_INF)


def _pick_heads(stacked, qb):
    lane = lax.broadcasted_iota(jnp.int32, (qb, 256), 1)
    out = jnp.zeros((qb, 256), stacked.dtype)
    for h in range(4):
        out = jnp.where((lane >> 6) == h, stacked[h * qb:(h + 1) * qb], out)
    return out


def _stack_head_cols(ref, t, qb):
    return jnp.concatenate([ref[0, t * qb:(t + 1) * qb, 64 * h:64 * h + 1] for h in range(4)], axis=0)


def _dil_fwd(q, k, v, *, name, after=None):
    dil, length, _ = q.shape
    qb, win, grp = _dil_blocks(length)
    extra = [] if after is None else [after]

    def body(q_ref, k_ref, v_ref, *rest):
        o_ref, lse_ref = rest[-2:]
        blks = [pl.program_id(1) * grp + t for t in range(grp)]
        wins = [_dil_window(k_ref, v_ref, b, qb, win, length) for b in blks]
        raw = [lax.dot_general(_stack_lanes(q_ref, t, qb, QK_SCALE), w[0], NT_DIMS, preferred_element_type=F32)
               for t, w in enumerate(wins)]
        lses, outs = [], []
        for t, (s, w) in enumerate(zip(raw, wins)):
            s = _dil_mask(s, blks[t], w[2], qb, win)
            m = jnp.max(s, axis=-1, keepdims=True)
            e = jnp.exp(s - m)
            norm = jnp.sum(e, axis=-1, keepdims=True)
            lses.append(m + jnp.log(norm))
            outs.append(jnp.dot((e * (1.0 / norm)).astype(BF16), w[1], preferred_element_type=F32))
        for t in range(grp):
            o_ref[0, t * qb:(t + 1) * qb, :] = _pick_heads(outs[t], qb)
            lse_ref[0, t * qb:(t + 1) * qb, :] = _pick_heads(jnp.broadcast_to(lses[t], (4 * qb, 256)), qb)

    seq = pl.BlockSpec((1, length, 256), lambda j, i: (j, 0, 0))
    blk = pl.BlockSpec((1, grp * qb, 256), lambda j, i: (j, i, 0))
    return pl.pallas_call(
        body, name=name, grid=(dil, length // (grp * qb)),
        in_specs=[blk, seq, seq] + [pl.BlockSpec(memory_space=pl.ANY)] * len(extra), out_specs=[blk, blk],
        out_shape=[_sds((dil, length, 256), F32)] * 2,
        compiler_params=_params("parallel", "parallel"),
    )(q, k, v, *extra)


def _dil_bwd(q, k, v, do, lse, cc, *, name):
    dil, length, _ = q.shape
    qb, win, grp = _dil_blocks(length)

    def body(q_ref, k_ref, v_ref, do_ref, lse_ref, cc_ref, dq_ref, dk_ref, dv_ref):
        @pl.when(pl.program_id(1) == 0)
        def _():
            dk_ref[...] = jnp.zeros_like(dk_ref)
            dv_ref[...] = jnp.zeros_like(dv_ref)

        blks = [pl.program_id(1) * grp + t for t in range(grp)]
        wins = [_dil_window(k_ref, v_ref, b, qb, win, length) for b in blks]
        qss = [_stack_lanes(q_ref, t, qb, QK_SCALE) for t in range(grp)]
        doss = [_stack_lanes(do_ref, t, qb) for t in range(grp)]
        raw = [lax.dot_general(qs, w[0], NT_DIMS, preferred_element_type=F32) for qs, w in zip(qss, wins)]
        dps = [lax.dot_general(dos, w[1], NT_DIMS, preferred_element_type=F32) for dos, w in zip(doss, wins)]
        probs = [jnp.exp(_dil_mask(s, blks[t], wins[t][2], qb, win) - _stack_head_cols(lse_ref, t, qb))
                 for t, s in enumerate(raw)]
        dsbs = [(p * (dp + _stack_head_cols(cc_ref, t, qb))).astype(BF16)
                for t, (p, dp) in enumerate(zip(probs, dps))]
        dq4s = [jnp.dot(dsb, w[0], preferred_element_type=F32) for dsb, w in zip(dsbs, wins)]
        dkws = [lax.dot_general(dsb, qs, TN_DIMS, preferred_element_type=F32) for dsb, qs in zip(dsbs, qss)]
        dvws = [lax.dot_general(p.astype(BF16), dos, TN_DIMS, preferred_element_type=F32)
                for p, dos in zip(probs, doss)]
        for t in range(grp):
            dq_ref[0, t * qb:(t + 1) * qb, :] = _pick_heads(dq4s[t], qb) * QK_SCALE
            dk_ref[0, pl.ds(wins[t][2], win), :] += dkws[t]
            dv_ref[0, pl.ds(wins[t][2], win), :] += dvws[t]

    seq = pl.BlockSpec((1, length, 256), lambda j, i: (j, 0, 0))
    blk = pl.BlockSpec((1, grp * qb, 256), lambda j, i: (j, i, 0))
    return pl.pallas_call(
        body, name=name, grid=(dil, length // (grp * qb)),
        in_specs=[blk, seq, seq, blk, blk, blk], out_specs=[blk, seq, seq],
        out_shape=[_sds((dil, length, 256), F32)] * 3,
        compiler_params=_params("parallel", "arbitrary"),
    )(q, k, v, do, lse, cc)


def _merge_weights(lses):
    m = jnp.maximum(jnp.maximum(lses[0], lses[1]), lses[2])
    es = [jnp.exp(t - m) for t in lses]
    inv = 1.0 / (es[0] + es[1] + es[2])
    return [e * inv for e in es]


def _dil_merge(outs, lses, *, tm, name):
    n = outs[0].shape[1]

    def body(*refs):
        o_in, l_in = refs[0:3], refs[3:6]
        y_ref, yb_ref, scr = refs[6:9]
        lv = [_load_token_order(l_in[g], scr, d, tm) for g, d in enumerate(DIL_DILATIONS)]
        ws = _merge_weights(lv)
        y = jnp.zeros((tm, 256), F32)
        for g, d in enumerate(DIL_DILATIONS):
            y = y + ws[g] * _load_token_order(o_in[g], scr, d, tm)
        y_ref[...] = y
        yb_ref[...] = y.astype(BF16)

    specs = [_dil_spec(d, tm) for d in DIL_DILATIONS]
    return pl.pallas_call(
        body, name=name, grid=(n // tm,), in_specs=specs + specs,
        out_specs=[_rows(tm, 256)] * 2, out_shape=[_sds((n, 256), F32), _sds((n, 256), BF16)],
        scratch_shapes=[_dil_scratch(tm)],
        compiler_params=_params("parallel"),
    )(*outs, *lses)


def _dil_merge_bwd(dy, y, lses, *, tm, name):
    n = dy.shape[0]

    def body(*refs):
        dy_ref, y_ref = refs[0:2]
        l_in = refs[2:5]
        do_out, cc_out = refs[5:8], refs[8:11]
        scr = refs[11]
        lv = [_load_token_order(l_in[g], scr, d, tm) for g, d in enumerate(DIL_DILATIONS)]
        ws = _merge_weights(lv)
        dyv = dy_ref[...]
        rr = lax.broadcasted_iota(jnp.int32, (256, 256), 0) >> 6
        cc = lax.broadcasted_iota(jnp.int32, (256, 256), 1) >> 6
        ones = jnp.where(rr == cc, 1.0, 0.0).astype(F32)
        tsum = jnp.dot(dyv * y_ref[...], ones, preferred_element_type=F32,
                       precision=lax.Precision.HIGHEST)
        for g, d in enumerate(DIL_DILATIONS):
            _store_dil_order(ws[g] * dyv, do_out[g], scr, d, tm)
            _store_dil_order(-ws[g] * tsum, cc_out[g], scr, d, tm)

    specs = [_dil_spec(d, tm) for d in DIL_DILATIONS]
    res = pl.pallas_call(
        body, name=name, grid=(n // tm,),
        in_specs=[_rows(tm, 256)] * 2 + specs,
        out_specs=specs + specs,
        out_shape=[_sds((d, n // d, 256), BF16) for d in DIL_DILATIONS]
                  + [_sds((d, n // d, 256), F32) for d in DIL_DILATIONS],
        scratch_shapes=[_dil_scratch(tm)],
        compiler_params=_params("parallel"),
    )(dy, y, *lses)
    return res[0:3], res[3:6]


_WEIGHTS = (("w_in", 1, 736), ("w_branch_na", 1, 128), ("w_branch_dil", 1, 128), ("w_out", 0, 128),
            ("w_up", 1, 512), ("w_down", 0, 512), ("w_ple_gate", 0, 128), ("w_ple_proj", 1, 128))
_W_IN, _W_BNA, _W_BD, _W_OUT, _W_UP, _W_DOWN, _W_PG, _W_PP = range(8)


def _to_full(gathered):
    return gathered.reshape(-1, gathered.shape[2])


def _to_chunks(widx, mat):
    return mat.reshape(N_DEV, _WEIGHTS[widx][2], mat.shape[1])


def _local_step(x, p_bf16, positions, target, g_mix, g_mlp, g_ple, g_final, rpb2,
                get_w_in, relay_rest, get_rest, send_grads):
    tm = 256
    half = HEAD_DIM // 2
    inv_freq = 10000.0 ** (-jnp.arange(half, dtype=F32) / half)
    ang = positions.astype(F32)[:, None] * inv_freq
    cos, sin = jnp.cos(ang), jnp.sin(ang)
    cos_t = jnp.tile(jnp.concatenate([cos, cos], axis=-1), (1, 4))
    sin_t = jnp.tile(jnp.concatenate([-sin, sin], axis=-1), (1, 4))
    rb = _rpb_table(rpb2)

    a = _rms_fwd(x, g_mix, tm=tm, name="rms_mix")
    w_in, token = get_w_in(a)
    qkv_width = 3 * NA_WIDTH + 3 * DIL_WIDTH
    proj = _matmul(a, w_in, tb=True, n_limit=qkv_width, out_dtype=F32, tm=512, tn=qkv_width // 2, tk=1024,
                   name="mm_in_qkv", after=token)
    gates = _matmul(a, w_in[qkv_width:], tb=True, stack_cols=True, out_dtype=F32, tm=512, tn=D_MODEL, tk=1024,
                    name="mm_in_gates", epilogue=lambda acc: (_sigmoid(acc),))
    sn, sd = (gates, 0), (gates, 1)
    na_qkv, dq_g, dk_g, dv_g = _split_proj(proj, cos_t, sin_t, tm=tm, name="split_proj")
    y_na = _na_fwd(*na_qkv, rb, name="na_fwd")
    token = relay_rest(y_na)
    d_out, d_lse = [], []
    for g in range(3):
        o, lse = _dil_fwd(dq_g[g], dk_g[g], dv_g[g], name=f"dil_fwd{g}", after=token if g == 0 else None)
        d_out.append(o)
        d_lse.append(lse)
    y_dil, y_dil_b = _dil_merge(d_out, d_lse, tm=tm, name="dil_merge")
    w_bna, w_bd, w_out, w_up, w_down, w_pg, w_pp = get_rest(y_dil_b)
    bn = _matmul(y_na, w_bna, tb=True, out_dtype=F32, tm=512, tn=1024, tk=512, name="mm_bna")
    bd, mixed = _matmul(y_dil_b, w_bd, tb=True, out_dtype=(F32, BF16), tm=512, tn=1024, tk=256, name="mm_bd",
                        extra=(sn, bn, sd), epilogue=lambda acc, s1, b1, s2: (acc, s1 * b1 + s2 * acc))
    h1, c = _matmul(mixed, w_out, out_dtype=(F32, BF16), tm=512, tn=1024, tk=1024, name="mm_out",
                    extra=(x, g_mlp), epilogue=_residual_rms_tile)
    u, f = _matmul(c, w_up, tb=True, out_dtype=(F32, BF16), tm=512, tn=2048, tk=1024, name="mm_up",
                   epilogue=lambda acc: (acc, jnp.square(jnp.maximum(acc, 0.0))))
    h2, e = _matmul(f, w_down, out_dtype=(F32, BF16), tm=512, tn=1024, tk=4096, name="mm_down",
                    extra=(h1, g_ple), epilogue=_residual_rms_tile)
    pp = _matmul(p_bf16, w_pp, tb=True, out_dtype=F32, tm=512, tn=1024, tk=256, name="mm_pp")

    dh3, dpp, dgt, dg_final, loss = _matmul(
        e, w_pg, out_dtype=(F32, BF16, BF16), tm=512, tn=1024, tk=1024, name="mm_pg_tail",
        extra=(pp, h2, target, g_final), epilogue=_tail_tile, n_colsum=2)
    loss = loss[:, :128]
    gw_pp = _matmul(p_bf16, dpp, ta=True, transpose_out=True, out_dtype=BF16, tm=256, tn=1024, tk=2048,
                    name="mm_gw_pp")
    gw_pg = _matmul(e, dgt, ta=True, out_dtype=BF16, tm=512, tn=1024, tk=2048, name="mm_gw_pg")
    dh2, dh2_b, dg_ple = _matmul(
        dgt, w_pg, tb=True, out_dtype=(F32, BF16), tm=512, tn=1024, tk=1024, name="mm_de",
        extra=(h2, g_ple, dh3), epilogue=_rms_bwd_twice, n_colsum=1)
    du = _matmul(dh2_b, w_down, tb=True, out_dtype=BF16, tm=512, tn=2048, tk=1024, name="mm_du",
                 extra=(u,), epilogue=lambda acc, uv: (acc * (2.0 * jnp.maximum(uv, 0.0)),))
    gw_down = _matmul(f, dh2_b, ta=True, out_dtype=BF16, tm=1024, tn=1024, tk=2048, name="mm_gw_down")
    gw_up = _matmul(c, du, ta=True, transpose_out=True, out_dtype=BF16, tm=512, tn=2048, tk=2048, name="mm_gw_up")
    dh1, dh1_b, dg_mlp = _matmul(
        du, w_up, out_dtype=(F32, BF16), tm=512, tn=1024, tk=4096, name="mm_dc",
        extra=(h1, g_mlp, dh2), epilogue=_rms_bwd_twice, n_colsum=1)
    dbn, dbd, dgn, dgd = _matmul(dh1_b, w_out, tb=True, out_dtype=(BF16,) * 4, tm=512, tn=1024, tk=1024,
                                 name="mm_dmixed", extra=(sn, bn, sd, bd), epilogue=_gate_bwd_tile)
    gw_out = _matmul(mixed, dh1_b, ta=True, out_dtype=BF16, tm=512, tn=1024, tk=2048, name="mm_gw_out")
    gw_bna = _matmul(y_na, dbn, ta=True, transpose_out=True, out_dtype=BF16, tm=512, tn=1024, tk=2048,
                     name="mm_gw_bna")
    dy_na = _matmul(dbn, w_bna, out_dtype=BF16, tm=512, tn=512, tk=1024, name="mm_dy_na")
    gw_bd = _matmul(y_dil_b, dbd, ta=True, transpose_out=True, out_dtype=BF16, tm=256, tn=1024, tk=2048,
                    name="mm_gw_bd")
    token = send_grads((_W_PP, _W_PG, _W_DOWN, _W_UP, _W_OUT, _W_BNA, _W_BD),
                       (gw_pp, gw_pg, gw_down, gw_up, gw_out, gw_bna, gw_bd))
    dy_dil = _matmul(dbd, w_bd, out_dtype=F32, tm=512, tn=256, tk=1024, name="mm_dy_dil", after=token)
    dna = _na_bwd(*na_qkv, dy_na, rb, name="na_bwd")
    drpb = _rpb_grad(dna[3].reshape(8, -1), name="rpb_grad")
    do_g, cc_g = _dil_merge_bwd(dy_dil, y_dil, d_lse, tm=tm, name="dil_merge_bwd")
    ddq, ddk, ddv = [], [], []
    for g in range(3):
        r = _dil_bwd(dq_g[g], dk_g[g], dv_g[g], do_g[g], d_lse[g], cc_g[g], name=f"dil_bwd{g}")
        ddq.append(r[0])
        ddk.append(r[1])
        ddv.append(r[2])
    dproj = _assemble_dproj(dna[0:3], ddq, ddk, ddv, dgn, dgd, cos_t, sin_t, tm=tm, name="assemble_dproj")
    gw_in = _matmul(a, dproj, ta=True, transpose_out=True, out_dtype=BF16, tm=512, tn=2944, tk=2048, name="mm_gw_in")
    token = send_grads((_W_IN,), (gw_in,))
    dx, dg_mix = _matmul(
        dproj, w_in, out_dtype=(F32,), tm=512, tn=1024, tk=5888, name="mm_da", after=token,
        extra=(x, g_mix, dh1), epilogue=_rms_bwd_tile, n_colsum=1)
    return loss, dx, (dg_mix, dg_mlp, dg_ple, dg_final), drpb


def _cast_bf16(t, *, name):
    def body(t_ref, o_ref):
        o_ref[...] = t_ref[...].astype(BF16)

    rows, cols = t.shape
    tr = 256 if rows % 256 == 0 else rows
    blk = pl.BlockSpec((tr, cols), lambda i: (i, 0))
    return pl.pallas_call(body, name=name, grid=(rows // tr,), in_specs=[blk], out_specs=blk,
                          out_shape=_sds(t.shape, BF16), compiler_params=_params("parallel"))(t)


def _adamw(w, g, m, v):
    m = ADAM_B1 * m + (1.0 - ADAM_B1) * g
    v = ADAM_B2 * v + (1.0 - ADAM_B2) * (g * g)
    m_hat = m / (1.0 - ADAM_B1 ** ADAM_STEP)
    v_hat = v / (1.0 - ADAM_B2 ** ADAM_STEP)
    delta = -ADAM_LR * (m_hat / (jnp.sqrt(v_hat) + ADAM_EPS) + ADAM_WD * w)
    return delta, m, v


def _sum_adamw(parts, w, m, v, *, tr, name, own=None, transposed=False):
    rows, cols = w.shape

    def body(*refs):
        p_ref, w_ref, m_ref, v_ref = refs[:4]
        g_ref, d_ref, nm_ref, nv_ref = refs[-4:]
        g = (p_ref[0] if own is None else refs[4][...]).astype(F32)
        for s in range(1, N_DEV):
            g = g + p_ref[s].astype(F32)
        if transposed:
            g = g.T
        g_ref[...] = g
        d_ref[...], nm_ref[...], nv_ref[...] = _adamw(w_ref[...], g, m_ref[...], v_ref[...])

    extra = [] if own is None else [own]
    if transposed:
        blk = pl.BlockSpec((rows, tr), lambda i: (0, i))
        g_blk, p_blk, steps = pl.BlockSpec((tr, rows), lambda i: (i, 0)), (N_DEV, tr, rows), cols // tr
    else:
        blk = pl.BlockSpec((tr, cols), lambda i: (i, 0))
        g_blk, p_blk, steps = blk, (N_DEV, tr, cols), rows // tr
    return pl.pallas_call(
        body, name=name, grid=(steps,),
        in_specs=[pl.BlockSpec(p_blk, lambda i: (0, i, 0)), blk, blk, blk] + [g_blk] * len(extra),
        out_specs=[blk] * 4, out_shape=[_sds((rows, cols), F32)] * 4,
        compiler_params=_params("parallel"),
    )(parts, w, m, v, *extra)


_RPB_SIZE = 8 * 15 * 31


def _pack_small(g_mix, g_mlp, g_ple, g_final, rpb, loss_row):
    flat = jnp.concatenate([g_mix.reshape(-1), g_mlp.reshape(-1), g_ple.reshape(-1), g_final.reshape(-1),
                            rpb.reshape(-1), jnp.zeros((3840 - _RPB_SIZE,), F32), loss_row.reshape(-1),
                            jnp.zeros((128,), F32)])
    return flat.reshape(64, 128)


def _unpack_small(t):
    flat = t.reshape(-1)
    return (flat[0:1024].reshape(1, 1024), flat[4096:4096 + _RPB_SIZE].reshape(1, 8, 15, 31),
            flat[1024:2048].reshape(1, 1024), flat[2048:3072].reshape(1, 1024), flat[3072:4096])


def kernel(x, p, positions, g_mix, w_in, rpb, w_branch_na, w_branch_dil, w_out, g_mlp, w_up, w_down, g_ple, w_ple_gate, w_ple_proj, g_final, loss_target, m_g_mix, m_w_in, m_rpb, m_w_branch_na, m_w_branch_dil, m_w_out, m_g_mlp, m_w_up, m_w_down, m_g_ple, m_w_ple_gate, m_w_ple_proj, m_g_final, v_g_mix, v_w_in, v_rpb, v_w_branch_na, v_w_branch_dil, v_w_out, v_g_mlp, v_w_up, v_w_down, v_g_ple, v_w_ple_gate, v_w_ple_proj, v_g_final):
    sharded = dict(w_in=(w_in, m_w_in, v_w_in), w_branch_na=(w_branch_na, m_w_branch_na, v_w_branch_na),
                   w_branch_dil=(w_branch_dil, m_w_branch_dil, v_w_branch_dil), w_out=(w_out, m_w_out, v_w_out),
                   w_up=(w_up, m_w_up, v_w_up), w_down=(w_down, m_w_down, v_w_down),
                   w_ple_gate=(w_ple_gate, m_w_ple_gate, v_w_ple_gate),
                   w_ple_proj=(w_ple_proj, m_w_ple_proj, v_w_ple_proj))
    shards = {k: tuple(t[0] for t in val) for k, val in sharded.items()}

    me = _my_index()

    shards["w_in"] = tuple(t.T for t in shards["w_in"])

    w_in_b = _cast_bf16(shards["w_in"][0], name="cast_w_in")
    rest_b = [shards[name][0].astype(BF16).T if axis == 1 else shards[name][0].astype(BF16)
              for name, axis, _ in _WEIGHTS[1:]]
    first_in, token_in = _start_copies(_first_leg_copies, [w_in_b], [_sds((N_DEV,) + w_in_b.shape, BF16)], 4,
                                       name="start_gather_w_in")

    def whole(landed, mine):
        return _to_full(lax.dynamic_update_index_in_dim(landed, mine, me, 0))

    rest = {}

    def get_w_in(after):
        (mine,), landed = _wait_copies(_first_leg_copies, first_in, after, name="wait_gather_w_in")
        second, token = _start_copies(_second_leg_copies, [], landed, 3, name="start_forward_w_in")
        _, (landed,) = _wait_copies(_second_leg_copies, second, token, name="wait_forward_w_in")
        rest["first"], token = _start_copies(_first_leg_copies, rest_b,
                                             [_sds((N_DEV,) + t.shape, BF16) for t in rest_b], 4 * len(rest_b),
                                             name="start_gather_rest", after=landed)
        return whole(landed, mine), token

    def relay_rest(after):
        rest["mine"], landed = _wait_copies(_first_leg_copies, rest["first"], after, name="wait_gather_rest")
        rest["second"], token = _start_copies(_second_leg_copies, [], landed, 3 * len(rest_b),
                                              name="start_forward_rest")
        return token

    def get_rest(after):
        _, landed = _wait_copies(_second_leg_copies, rest["second"], after, name="wait_forward_rest")
        return [whole(t, own) for t, own in zip(landed, rest["mine"])]

    sent = []

    def send_grads(indices, grads):
        chunked = [_to_chunks(i, g) for i, g in zip(indices, grads)]
        handle, token = _start_copies(_exchange_copies, chunked, [_sds(t.shape, BF16) for t in chunked],
                                      7 * len(chunked),
                                      name="start_exchange_" + ("w_in" if indices == (_W_IN,) else "rest"))
        sent.append((indices, handle))
        return token

    g_mix_0 = g_mix + token_in[0:1, 0:1]
    loss, dx, dgs, drpb = _local_step(
        x[0], p[0, 0].astype(BF16), positions[0], loss_target[0],
        g_mix_0, g_mlp, g_ple, g_final.reshape(1, -1), rpb[0], get_w_in, relay_rest, get_rest, send_grads)

    drpb3 = drpb.reshape(8, 16, 32)[:, :15, :31]
    small = _pack_small(dgs[0], dgs[1], dgs[2], dgs[3], drpb3, loss)
    share, done = _start_copies(_gather_copies, [small], [_sds((N_DEV,) + small.shape, F32)], 7,
                                name="start_share_small")

    out = {}
    for indices, handle in sent:
        chunked, landed = _wait_copies(_exchange_copies, handle, done,
                                       name="wait_exchange_" + ("w_in" if indices == (_W_IN,) else "rest"))
        for i, part, mine in zip(indices, landed, chunked):
            name = _WEIGHTS[i][0]
            w, m, v = shards[name]
            own = lax.dynamic_index_in_dim(mine, me, 0, keepdims=False)
            turned = _WEIGHTS[i][1] == 1 and i != _W_IN
            res = _sum_adamw(part, w, m, v, tr=368 if i == _W_IN else 128, name="adamw_" + name, own=own,
                             transposed=turned)
            out[name] = [(t.T if i == _W_IN else t)[None] for t in res]
            done = res[0]
    (small,), (small_landed,) = _wait_copies(_gather_copies, share, done, name="wait_share_small")
    small_all = lax.dynamic_update_index_in_dim(small_landed, small, me, 0)
    small_w = _pack_small(g_mix, g_mlp, g_ple, g_final, rpb, jnp.zeros((128,), F32))
    small_m = _pack_small(m_g_mix, m_g_mlp, m_g_ple, m_g_final, m_rpb, jnp.zeros((128,), F32))
    small_v = _pack_small(v_g_mix, v_g_mlp, v_g_ple, v_g_final, v_rpb, jnp.zeros((128,), F32))
    res = _sum_adamw(small_all, small_w, small_m, small_v, tr=64, name="adamw_small")
    unpacked = [_unpack_small(t) for t in res]
    for i, name in enumerate(("g_mix", "rpb", "g_mlp", "g_ple", "g_final")):
        out[name] = [u[i] for u in unpacked]
    loss_total = res[0][62, 0]

    order = ("g_mix", "w_in", "rpb", "w_branch_na", "w_branch_dil", "w_out", "g_mlp", "w_up", "w_down",
             "g_ple", "w_ple_gate", "w_ple_proj", "g_final")
    grads = [out[k][0] for k in order]
    deltas = [out[k][1] for k in order]
    new_m = [out[k][2] for k in order]
    new_v = [out[k][3] for k in order]
    return (loss_total, dx[None], *grads, *deltas, *new_m, *new_v)
```

```python
import jax
import jax.numpy as jnp
from jax import lax
from jax.experimental import pallas as pl
from jax.experimental.pallas import tpu as pltpu

F32 = jnp.float32
BF16 = jnp.bfloat16

D_MODEL = 1024
HEAD_DIM = 64
GRID_W = 64
NA_WIDTH = 512
DIL_WIDTH = 768
IN_WIDTH = 5888
DIL_DILATIONS = (1, 4, 16)
DIL_RADIUS = 64
NA_WIN_ROWS = 8
RMS_EPS = 1e-6
NEG_INF = -1e30
QK_SCALE = HEAD_DIM ** -0.5

ADAM_LR = 0.001
ADAM_B1 = 0.9
ADAM_B2 = 0.999
ADAM_EPS = 1e-08
ADAM_WD = 0.01
ADAM_STEP = 10

N_DEV = 8
VMEM_LIMIT = 56 * 1024 * 1024
EPILOGUE_ROWS = 256
MESH = pl.DeviceIdType.MESH

NT_DIMS = (((1,), (1,)), ((), ()))
TN_DIMS = (((0,), (0,)), ((), ()))


def _sds(shape, dtype):
    return jax.ShapeDtypeStruct(shape, dtype)


def _params(*sem):
    return pltpu.CompilerParams(dimension_semantics=sem, vmem_limit_bytes=VMEM_LIMIT)


def _rows(tm, width, col=0):
    return pl.BlockSpec((tm, width), lambda i, c=col: (i, c))


def _const(shape):
    zeros = (0,) * len(shape)
    return pl.BlockSpec(shape, lambda i: zeros)


def _my_index():
    return 4 * lax.axis_index("x") + 2 * lax.axis_index("y") + lax.axis_index("c")


def _peer(k):
    x, y, c = lax.axis_index("x"), lax.axis_index("y"), lax.axis_index("c")
    px = 1 - x if k & 4 else x
    py = 1 - y if k & 2 else y
    pc = 1 - c if k & 1 else c
    return (px, py, pc), 4 * px + 2 * py + pc


def _call(body, *, name, grid, in_specs, out_specs, out_shape, scratch_shapes, args, after=None):
    n_in, n_out = len(in_specs), len(out_specs)
    extra = [] if after is None else [after]
    n_x = n_in + len(extra)

    def plain(*refs):
        body(refs[:n_in], refs[n_x:n_x + n_out], refs[n_x + n_out:])

    res = pl.pallas_call(plain, name=name, grid=grid,
                         in_specs=list(in_specs) + [pl.BlockSpec(memory_space=pl.ANY)] * len(extra),
                         out_specs=out_specs, out_shape=out_shape, scratch_shapes=scratch_shapes,
                         compiler_params=_params(*(("arbitrary",) * len(grid))))(*args, *extra)
    return list(res)


_HBM_SPEC = pl.BlockSpec(memory_space=pltpu.HBM)
_SEM_SPEC = pl.BlockSpec(memory_space=pltpu.SEMAPHORE)
_SIDE_EFFECT = pltpu.SideEffectType.DATAFLOW_SIDE_EFFECTING


_FIRST_LEG = (1, 2, 4, 6)
_SECOND_LEG = (2, 4, 6)


def _gather_copies(srcs, lands, send, recv, sending):
    me = _my_index()
    out = []
    for w in range(len(srcs)):
        for k in range(1, N_DEV):
            dev, idx = _peer(k)
            out.append(pltpu.make_async_remote_copy(
                src_ref=srcs[w], dst_ref=lands[w].at[me if sending else idx],
                send_sem=send.at[w * 7 + k - 1], recv_sem=recv.at[w * 7 + k - 1],
                device_id=dev, device_id_type=MESH))
    return out


def _first_leg_copies(srcs, lands, send, recv, sending):
    me = _my_index()
    out = []
    for w in range(len(srcs)):
        for j, k in enumerate(_FIRST_LEG):
            dev, idx = _peer(k)
            out.append(pltpu.make_async_remote_copy(
                src_ref=srcs[w], dst_ref=lands[w].at[me if sending else idx],
                send_sem=send.at[w * 4 + j], recv_sem=recv.at[w * 4 + j],
                device_id=dev, device_id_type=MESH))
    return out


def _second_leg_copies(srcs, lands, send, recv, sending):
    sibling, _ = _peer(1)
    out = []
    for w in range(len(lands)):
        for j, k in enumerate(_SECOND_LEG):
            slot = _peer(k if sending else k ^ 1)[1]
            out.append(pltpu.make_async_remote_copy(
                src_ref=lands[w].at[slot], dst_ref=lands[w].at[slot],
                send_sem=send.at[w * 3 + j], recv_sem=recv.at[w * 3 + j],
                device_id=sibling, device_id_type=MESH))
    return out


def _exchange_copies(srcs, lands, send, recv, sending):
    out = []
    for w in range(len(srcs)):
        for k in range(1, N_DEV):
            dev, idx = _peer(k)
            out.append(pltpu.make_async_remote_copy(
                src_ref=srcs[w].at[idx], dst_ref=lands[w].at[k],
                send_sem=send.at[w * 7 + k - 1], recv_sem=recv.at[w * 7 + k - 1],
                device_id=dev, device_id_type=MESH))
    return out


def _start_copies(make, srcs, lands, n_copies, *, name, after=None):
    n_src, n_buf = len(srcs), len(srcs) + len(lands)
    extra = [] if after is None else [after]

    def body(*refs):
        send, recv = refs[n_buf + len(extra)], refs[n_buf + len(extra) + 1]
        for cp in make(refs[:n_src], refs[n_src:n_buf], send, recv, True):
            cp.start()
        refs[-1][...] = jnp.zeros_like(refs[-1])

    bufs = list(srcs) + [lax.empty(t.shape, t.dtype) if isinstance(t, jax.ShapeDtypeStruct) else t for t in lands]
    res = pl.pallas_call(
        body, name=name,
        out_shape=(pltpu.SemaphoreType.DMA((n_copies,)), pltpu.SemaphoreType.DMA((n_copies,)),
                   *[pltpu.HBM(t.shape, t.dtype) for t in bufs], _sds((8, 128), F32)),
        in_specs=[_HBM_SPEC] * n_buf + [pl.BlockSpec(memory_space=pl.ANY)] * len(extra),
        out_specs=(_SEM_SPEC, _SEM_SPEC, *([_HBM_SPEC] * n_buf), pl.BlockSpec(memory_space=pltpu.VMEM)),
        input_output_aliases={i: 2 + i for i in range(n_buf)},
        compiler_params=pltpu.CompilerParams(has_side_effects=_SIDE_EFFECT),
    )(*[pltpu.with_memory_space_constraint(t, pltpu.HBM) for t in bufs], *extra)
    return (n_src, res[0], res[1], res[2:2 + n_buf]), res[-1]


def _wait_copies(make, handle, after, *, name):
    n_src, send_sems, recv_sems, bufs = handle
    n_buf = len(bufs)

    def body(*refs):
        for cp in make(refs[:n_src], refs[n_src:n_buf], refs[n_buf], refs[n_buf + 1], False):
            cp.wait_send()
            cp.wait_recv()

    res = pl.pallas_call(
        body, name=name,
        out_shape=tuple(pltpu.HBM(t.shape, t.dtype) for t in bufs),
        in_specs=[_HBM_SPEC] * n_buf + [_SEM_SPEC, _SEM_SPEC, pl.BlockSpec(memory_space=pl.ANY)],
        out_specs=tuple([_HBM_SPEC] * n_buf),
        input_output_aliases={i: i for i in range(n_buf)},
        compiler_params=pltpu.CompilerParams(has_side_effects=_SIDE_EFFECT),
    )(*bufs, send_sems, recv_sems, after)
    return list(res[:n_src]), list(res[n_src:])


def _matmul(a, b, *, ta=False, tb=False, out_dtype, tm, tn, tk, name, after=None, extra=(), epilogue=None,
            n_colsum=0, transpose_out=False, n_limit=None, stack_cols=False):
    m, k = (a.shape[1], a.shape[0]) if ta else a.shape
    n = n_limit or (b.shape[0] if tb else b.shape[1])
    tm, tn, tk = min(tm, m), min(tn, n), min(tk, k)
    nk = k // tk
    dims = (((0 if ta else 1,), (1 if tb else 0,)), ((), ()))
    out_dtypes = out_dtype if isinstance(out_dtype, tuple) else (out_dtype,)
    n_tiles = len(out_dtypes)

    def add_colsums(o_refs, sums):
        i = pl.program_id(1)
        for s_ref, val in zip(o_refs[n_tiles:], sums):
            @pl.when(i == 0)
            def _(s_ref=s_ref, val=val):
                s_ref[...] = val

            @pl.when(i > 0)
            def _(s_ref=s_ref, val=val):
                s_ref[...] += val

    def finish(acc, x_refs, o_refs):
        vals = (acc,) if epilogue is None else epilogue(acc, *[r[...] for r in x_refs])
        for o_ref, val in zip(o_refs[:n_tiles], vals[:n_tiles]):
            o_ref[...] = (val.T if transpose_out else val).astype(o_ref.dtype)
        add_colsums(o_refs, vals[n_tiles:])

    chunk = EPILOGUE_ROWS if (nk == 1 and epilogue is not None and not ta and tm % EPILOGUE_ROWS == 0) else None

    def body(ins, outs, acc):
        a_ref, b_ref = ins[:2]
        if chunk is not None:
            sums = None
            for r0 in range(0, tm, chunk):
                part = lax.dot_general(a_ref[r0:r0 + chunk, :], b_ref[...], dims, preferred_element_type=F32)
                vals = epilogue(part, *[r[...] if r.shape[0] == 1 else r[r0:r0 + chunk, :] for r in ins[2:]])
                for o_ref, val in zip(outs[:n_tiles], vals[:n_tiles]):
                    o_ref[r0:r0 + chunk, :] = val.astype(o_ref.dtype)
                sums = vals[n_tiles:] if sums is None else [s + v for s, v in zip(sums, vals[n_tiles:])]
            add_colsums(outs, sums)
            return
        part = lax.dot_general(a_ref[...], b_ref[...], dims, preferred_element_type=F32)
        if nk == 1:
            finish(part, ins[2:], outs)
            return
        acc_ref, = acc
        kk = pl.program_id(2)

        @pl.when(kk == 0)
        def _():
            acc_ref[...] = part

        @pl.when(kk > 0)
        def _():
            acc_ref[...] += part

        @pl.when(kk == nk - 1)
        def _():
            finish(acc_ref[...], ins[2:], outs)

    a_spec = (pl.BlockSpec((tk, tm), lambda j, i, kk: (kk, i)) if ta
              else pl.BlockSpec((tm, tk), lambda j, i, kk: (i, kk)))
    b_spec = (pl.BlockSpec((tn, tk), lambda j, i, kk: (j, kk)) if tb
              else pl.BlockSpec((tk, tn), lambda j, i, kk: (kk, j)))
    tile = pl.BlockSpec((tm, tn), lambda j, i, kk: (i, j))
    row = pl.BlockSpec((1, tn), lambda j, i, kk: (0, j))

    def x_spec(t):
        if isinstance(t, tuple):
            return pl.BlockSpec((None, tm, tn), lambda j, i, kk, lead=t[1]: (lead, i, j))
        return row if t.shape[0] == 1 else tile

    out_tile, out_dims = (pl.BlockSpec((tn, tm), lambda j, i, kk: (j, i)), (n, m)) if transpose_out else (tile, (m, n))
    if stack_cols:
        out_tile, out_dims = pl.BlockSpec((None, tm, tn), lambda j, i, kk: (j, i, 0)), (n // tn, m, tn)
    res = _call(
        body, name=name, grid=(n // tn, m // tm, nk),
        in_specs=[a_spec, b_spec] + [x_spec(t) for t in extra],
        out_specs=[out_tile] * n_tiles + [row] * n_colsum,
        out_shape=[_sds(out_dims, dt) for dt in out_dtypes] + [_sds((1, n), F32)] * n_colsum,
        scratch_shapes=[] if nk == 1 else [pltpu.VMEM((tm, tn), F32)],
        args=(a, b, *[t[0] if isinstance(t, tuple) else t for t in extra]), after=after)
    return res if isinstance(out_dtype, tuple) or n_colsum else res[0]


def _rstd(h):
    return lax.rsqrt(jnp.mean(h * h, axis=-1, keepdims=True) + RMS_EPS)


def _sigmoid(z):
    return 1.0 / (1.0 + jnp.exp(-z))


def _rms_fwd(x, g, *, tm, name):
    n = x.shape[0]

    def body(x_ref, g_ref, o_ref):
        h = x_ref[...]
        o_ref[...] = (h * _rstd(h) * g_ref[...]).astype(BF16)

    return pl.pallas_call(
        body, name=name, grid=(n // tm,),
        in_specs=[_rows(tm, D_MODEL), _const((1, D_MODEL))],
        out_specs=_rows(tm, D_MODEL), out_shape=_sds((n, D_MODEL), BF16),
        compiler_params=_params("parallel"),
    )(x, g)


def _swap_halves(t):
    width = t.shape[1]
    lane = lax.broadcasted_iota(jnp.int32, t.shape, 1)
    return jnp.where((lane & 63) < 32, pltpu.roll(t, width - 32, 1), pltpu.roll(t, 32, 1))


def _dil_spec(dil, tm):
    return pl.BlockSpec((dil, tm // dil, 256), lambda i: (0, i, 0))


def _dil_scratch(tm):
    return pltpu.VMEM((2, tm, 128), F32)


def _load_token_order(src, scr, dil, tm):
    if dil == 1:
        return src[0]
    for j in range(dil):
        for c in range(2):
            scr[c, pl.ds(j, tm // dil, stride=dil), :] = src[j, :, c * 128:(c + 1) * 128]
    return jnp.concatenate([scr[0], scr[1]], axis=1)


def _store_dil_order(val, dst, scr, dil, tm):
    if dil == 1:
        dst[0] = val.astype(dst.dtype)
        return
    for c in range(2):
        scr[c] = val[:, c * 128:(c + 1) * 128]
    for j in range(dil):
        for c in range(2):
            dst[j, :, c * 128:(c + 1) * 128] = scr[c, pl.ds(j, tm // dil, stride=dil), :].astype(dst.dtype)


def _split_proj(proj, cos_t, sin_t, *, tm, name):
    n = proj.shape[0]
    n_dil = len(DIL_DILATIONS)

    def body(*refs):
        na_in = refs[0:3]
        dil_in = refs[3:3 + 3 * n_dil]
        cos_ref, sin_ref = refs[12:14]
        outs = refs[14:]
        na_out = outs[0:3]
        dil_out = outs[3:12]
        scr = outs[12]
        for t in range(3):
            na_out[t][...] = na_in[t][...].astype(BF16)
        cosv, sinv = cos_ref[...], sin_ref[...]
        for t in range(3):
            for gi, dil in enumerate(DIL_DILATIONS):
                val = dil_in[t * n_dil + gi][...]
                if t < 2:
                    val = val * cosv + _swap_halves(val) * sinv
                _store_dil_order(val, dil_out[t * n_dil + gi], scr, dil, tm)

    in_specs = [_rows(tm, NA_WIDTH, c) for c in range(3)]
    in_specs += [_rows(tm, 256, 6 + c) for c in range(9)]
    in_specs += [_rows(tm, 256), _rows(tm, 256)]
    out_specs = [_rows(tm, NA_WIDTH)] * 3
    out_shape = [_sds((n, NA_WIDTH), BF16)] * 3
    for _ in range(3):
        for dil in DIL_DILATIONS:
            out_specs.append(pl.BlockSpec((dil, tm // dil, 256), lambda i: (0, i, 0)))
            out_shape.append(_sds((dil, n // dil, 256), BF16))
    res = pl.pallas_call(
        body, name=name, grid=(n // tm,),
        in_specs=in_specs, out_specs=out_specs, out_shape=out_shape,
        scratch_shapes=[_dil_scratch(tm)],
        compiler_params=_params("parallel"),
    )(*([proj] * 12), cos_t, sin_t)
    return res[0:3], res[3:6], res[6:9], res[9:12]


def _residual_rms_tile(delta, h, g):
    hn = h + delta
    return hn, hn * _rstd(hn) * g


def _gate_bwd_tile(dm, s1, b1, s2, b2):
    return dm * s1, dm * s2, dm * b1 * s1 * (1.0 - s1), dm * b2 * s2 * (1.0 - s2)


def _tail_tile(gt, pp, h2, target, g):
    sg = _sigmoid(gt)
    h3 = h2 + sg * pp
    r3 = _rstd(h3)
    n3 = h3 * r3
    err = n3 * g - target
    loss = 0.5 * jnp.sum(jnp.sum(err * err, axis=-1, keepdims=True) / D_MODEL)
    dy = err / D_MODEL
    dn = dy * g
    dh3 = r3 * (dn - n3 * jnp.mean(dn * n3, axis=-1, keepdims=True))
    return (dh3, dh3 * sg, dh3 * pp * sg * (1.0 - sg),
            jnp.sum(dy * n3, axis=0, keepdims=True), jnp.full((1, gt.shape[1]), loss, F32))


def _rms_bwd_tile(dz, h, g, dres):
    r = _rstd(h)
    nrm = h * r
    dn = dz * g
    dh = dres + r * (dn - nrm * jnp.mean(dn * nrm, axis=-1, keepdims=True))
    return dh, jnp.sum(dz * nrm, axis=0, keepdims=True)


def _rms_bwd_twice(dz, h, g, dres):
    dh, dg = _rms_bwd_tile(dz, h, g, dres)
    return dh, dh, dg


def _assemble_dproj(dna, ddil_q, ddil_k, ddil_v, dgn, dgd, cos_t, sin_t, *, tm, name):
    n = dgn.shape[0]

    def body(*refs):
        dq_ref, dk_ref, dv_ref = refs[0:3]
        dil_in = refs[3:12]
        dgn_ref, dgd_ref, cos_ref, sin_ref, o_ref, scr = refs[12:18]
        o_ref[:, 0:512] = dq_ref[...]
        o_ref[:, 512:1024] = dk_ref[...].astype(BF16)
        o_ref[:, 1024:1536] = dv_ref[...].astype(BF16)
        cosv, sinv = cos_ref[...], sin_ref[...]
        for t in range(3):
            for gi, dil in enumerate(DIL_DILATIONS):
                val = _load_token_order(dil_in[t * 3 + gi], scr, dil, tm)
                if t < 2:
                    val = val * cosv + _swap_halves(val * sinv)
                c0 = 1536 + t * DIL_WIDTH + gi * 256
                o_ref[:, c0:c0 + 256] = val.astype(BF16)
        o_ref[:, 3840:4864] = dgn_ref[...]
        o_ref[:, 4864:5888] = dgd_ref[...]

    in_specs = [_rows(tm, NA_WIDTH)] * 3
    for _ in range(3):
        for dil in DIL_DILATIONS:
            in_specs.append(pl.BlockSpec((dil, tm // dil, 256), lambda i: (0, i, 0)))
    in_specs += [_rows(tm, D_MODEL)] * 2 + [_rows(tm, 256)] * 2
    return pl.pallas_call(
        body, name=name, grid=(n // tm,), in_specs=in_specs,
        out_specs=_rows(tm, IN_WIDTH), out_shape=_sds((n, IN_WIDTH), BF16),
        scratch_shapes=[_dil_scratch(tm)],
        compiler_params=_params("parallel"),
    )(*dna, *ddil_q, *ddil_k, *ddil_v, dgn, dgd, cos_t, sin_t)


N_ROW_OFF = 2 * NA_WIN_ROWS - 1
N_PAIRS = N_ROW_OFF - 1
RB_WIDTH = (N_ROW_OFF + 1) * GRID_W


def _na_bias(rb_ref, pair_scr):
    shape = (GRID_W, RB_WIDTH)
    qc = lax.broadcasted_iota(jnp.int32, shape, 0)
    qc2 = lax.broadcasted_iota(jnp.int32, (GRID_W, 128), 0)
    kc2 = lax.broadcasted_iota(jnp.int32, (GRID_W, 128), 1) & (GRID_W - 1)
    cs = jnp.clip(qc2 - 8, 0, GRID_W - 16)
    valid = (kc2 >= cs) & (kc2 < cs + 16)
    for hh in range(2):
        t = jnp.broadcast_to(rb_ref[hh], shape)
        t = pltpu.roll(t, RB_WIDTH - 15, 1)
        for b in range(6):
            t = jnp.where(((qc >> b) & 1) == 1, pltpu.roll(t, 1 << b, 1), t)
        t_odd = pltpu.roll(t, RB_WIDTH - GRID_W, 1)
        for ro in range(N_PAIRS):
            src = t if ro % 2 == 0 else t_odd
            base = (ro // 2) * 128
            pair_scr[hh, ro] = jnp.where(valid, src[:, base:base + 128], NEG_INF)


NA_GROUP_FWD = 8
NA_GROUP_BWD = 4


def _stack_heads(ref, r, scale=1.0):
    lane = lax.broadcasted_iota(jnp.int32, (GRID_W, 128), 1)
    t = ref[pl.ds(pl.multiple_of(r * GRID_W, GRID_W), GRID_W), :].astype(F32) * scale
    return jnp.concatenate([jnp.where(lane < 64, t, 0.0), jnp.where(lane >= 64, t, 0.0)], axis=0).astype(BF16)


def _unstack_heads(t2):
    lane = lax.broadcasted_iota(jnp.int32, (GRID_W, 128), 1)
    return jnp.where(lane < 64, t2[:GRID_W], t2[GRID_W:])


def _na_window(k_ref, v_ref, r, n_rows):
    rs = jnp.clip(r - NA_WIN_ROWS // 2, 0, n_rows - NA_WIN_ROWS)
    ro0 = (NA_WIN_ROWS - 1) - (r - rs)
    off = pl.multiple_of(rs * GRID_W, GRID_W)
    kw = k_ref[pl.ds(off, NA_WIN_ROWS * GRID_W), :]
    vw = v_ref[pl.ds(off, NA_WIN_ROWS * GRID_W), :]
    return kw, vw, off, ro0


def _na_probs(s_raw, pair_scr, ro0):
    bias = [jnp.concatenate([pair_scr[hh, ro0 + 2 * j] for j in range(NA_WIN_ROWS // 2)], axis=1)
            for hh in range(2)]
    s = s_raw + jnp.concatenate(bias, axis=0)
    m = jnp.max(s, axis=-1, keepdims=True)
    e = jnp.exp(s - m)
    return e * (1.0 / jnp.sum(e, axis=-1, keepdims=True))


def _na_fwd(q, k, v, rb, *, name):
    n = q.shape[0]
    n_rows = n // GRID_W

    def body(ins, outs, scr):
        q_ref, k_ref, v_ref, rb_ref = ins
        o_ref, = outs
        pair_scr, = scr
        _na_bias(rb_ref, pair_scr)

        def group(g, carry):
            rows = [g * NA_GROUP_FWD + t for t in range(NA_GROUP_FWD)]
            wins = [_na_window(k_ref, v_ref, r, n_rows) for r in rows]
            raw = [lax.dot_general(_stack_heads(q_ref, r, QK_SCALE), w[0], NT_DIMS, preferred_element_type=F32)
                   for r, w in zip(rows, wins)]
            probs = [_na_probs(s, pair_scr, w[3]) for s, w in zip(raw, wins)]
            outs2 = [jnp.dot(p.astype(BF16), w[1], preferred_element_type=F32) for p, w in zip(probs, wins)]
            for r, o2 in zip(rows, outs2):
                o_ref[pl.ds(pl.multiple_of(r * GRID_W, GRID_W), GRID_W), :] = _unstack_heads(o2).astype(BF16)
            return carry

        lax.fori_loop(0, n_rows // NA_GROUP_FWD, group, 0)

    col = pl.BlockSpec((n, 128), lambda h: (0, h))
    return _call(
        body, name=name, grid=(NA_WIDTH // 128,),
        in_specs=[col, col, col, pl.BlockSpec((2, 1, RB_WIDTH), lambda h: (h, 0, 0))],
        out_specs=[col], out_shape=[_sds((n, NA_WIDTH), BF16)],
        scratch_shapes=[pltpu.VMEM((2, N_PAIRS, GRID_W, 128), F32)],
        args=(q, k, v, rb))[0]


def _na_bwd(q, k, v, do, rb, *, name):
    n = q.shape[0]
    n_rows = n // GRID_W
    win = NA_WIN_ROWS * GRID_W

    def body(ins, outs, scr):
        q_ref, k_ref, v_ref, do_ref, rb_ref = ins
        dq_ref, dk_ref, dv_ref, drb_ref = outs
        pair_scr, acc_scr = scr
        _na_bias(rb_ref, pair_scr)
        acc_scr[...] = jnp.zeros_like(acc_scr)
        dk_ref[...] = jnp.zeros_like(dk_ref)
        dv_ref[...] = jnp.zeros_like(dv_ref)

        def group(g, carry):
            rows = [g * NA_GROUP_BWD + t for t in range(NA_GROUP_BWD)]
            wins = [_na_window(k_ref, v_ref, r, n_rows) for r in rows]
            qss = [_stack_heads(q_ref, r, QK_SCALE) for r in rows]
            doss = [_stack_heads(do_ref, r) for r in rows]
            raw = [lax.dot_general(qs, w[0], NT_DIMS, preferred_element_type=F32) for qs, w in zip(qss, wins)]
            dps = [lax.dot_general(dos, w[1], NT_DIMS, preferred_element_type=F32) for dos, w in zip(doss, wins)]
            probs = [_na_probs(s, pair_scr, w[3]) for s, w in zip(raw, wins)]
            dss = [p * (dp - jnp.sum(p * dp, axis=-1, keepdims=True)) for p, dp in zip(probs, dps)]
            dsbs = [ds.astype(BF16) for ds in dss]
            dq2s = [jnp.dot(dsb, w[0], preferred_element_type=F32) for dsb, w in zip(dsbs, wins)]
            dkws = [lax.dot_general(dsb, qs, TN_DIMS, preferred_element_type=F32) for dsb, qs in zip(dsbs, qss)]
            dvws = [lax.dot_general(p.astype(BF16), dos, TN_DIMS, preferred_element_type=F32)
                    for p, dos in zip(probs, doss)]
            for t, r in enumerate(rows):
                _, _, off, ro0 = wins[t]
                for hh in range(2):
                    for j in range(NA_WIN_ROWS // 2):
                        acc_scr[hh, ro0 + 2 * j] += dss[t][hh * GRID_W:(hh + 1) * GRID_W, j * 128:(j + 1) * 128]
                dq_ref[pl.ds(pl.multiple_of(r * GRID_W, GRID_W), GRID_W), :] = (
                    _unstack_heads(dq2s[t]) * QK_SCALE).astype(BF16)
                dk_ref[pl.ds(off, win), :] += dkws[t]
                dv_ref[pl.ds(off, win), :] += dvws[t]
            return carry

        lax.fori_loop(0, n_rows // NA_GROUP_BWD, group, 0)

        qc = lax.broadcasted_iota(jnp.int32, (N_PAIRS * GRID_W, 128), 0)
        for hh in range(2):
            t = acc_scr[hh].reshape(N_PAIRS * GRID_W, 128)
            for b in range(6):
                t = jnp.where(((qc >> b) & 1) == 1, pltpu.roll(t, 128 - (1 << b), 1), t)
            t = pltpu.roll(t, 15, 1)
            drb_ref[hh] = jnp.sum(t.reshape(N_PAIRS, GRID_W, 128), axis=1)

    col = pl.BlockSpec((n, 128), lambda h: (0, h))
    return _call(
        body, name=name, grid=(NA_WIDTH // 128,),
        in_specs=[col, col, col, col, pl.BlockSpec((2, 1, RB_WIDTH), lambda h: (h, 0, 0))],
        out_specs=[col, col, col, pl.BlockSpec((2, N_PAIRS, 128), lambda h: (h, 0, 0))],
        out_shape=[_sds((n, NA_WIDTH), BF16), _sds((n, NA_WIDTH), F32), _sds((n, NA_WIDTH), F32),
                   _sds((8, N_PAIRS, 128), F32)],
        scratch_shapes=[pltpu.VMEM((2, N_PAIRS, GRID_W, 128), F32),
                        pltpu.VMEM((2, N_PAIRS, GRID_W, 128), F32)],
        args=(q, k, v, do, rb))


def _rpb_table(rpb2):
    t = jnp.pad(rpb2, ((0, 0), (0, 1), (0, GRID_W - rpb2.shape[-1])))
    return t.reshape(8, 1, RB_WIDTH)


def _rpb_grad(drb, *, name):
    kdim = drb.shape[1]

    def body(x_ref, o_ref):
        kk = lax.broadcasted_iota(jnp.int32, (128, 512), 0)
        jj = lax.broadcasted_iota(jnp.int32, (128, 512), 1)
        half, co = kk >> 6, kk & 63
        acc = jnp.zeros((8, 512), F32)
        for ro in range(N_PAIRS):
            hit = ((ro + half) == (jj >> 5)) & (co == (jj & 31)) & (co < 31)
            onehot = jnp.where(hit, 1.0, 0.0).astype(F32)
            acc = acc + jnp.dot(x_ref[:, ro * 128:(ro + 1) * 128], onehot, preferred_element_type=F32,
                                precision=lax.Precision.HIGHEST)
        o_ref[...] = acc

    return pl.pallas_call(
        body, name=name, grid=(1,),
        in_specs=[_const((8, kdim))], out_specs=_const((8, 512)), out_shape=_sds((8, 512), F32),
        compiler_params=_params("arbitrary"),
    )(drb)


DIL_GROUP = 2


def _dil_blocks(length):
    qb = min(128, length)
    return qb, min(qb + 2 * DIL_RADIUS, length), min(DIL_GROUP, length // qb)


def _stack_lanes(ref, t, qb, scale=1.0):
    lane = lax.broadcasted_iota(jnp.int32, (qb, 256), 1)
    val = ref[0, t * qb:(t + 1) * qb, :].astype(F32) * scale
    return jnp.concatenate([jnp.where((lane >> 6) == h, val, 0.0) for h in range(4)], axis=0).astype(BF16)


def _dil_window(k_ref, v_ref, blk, qb, win, length):
    start = pl.multiple_of(jnp.clip(blk * qb - DIL_RADIUS, 0, length - win), DIL_RADIUS)
    return k_ref[0, pl.ds(start, win), :], v_ref[0, pl.ds(start, win), :], start


def _dil_mask(s, blk, start, qb, win):
    gap = ((lax.broadcasted_iota(jnp.int32, (4 * qb, win), 0) & (qb - 1))
           - lax.broadcasted_iota(jnp.int32, (4 * qb, win), 1)) + (blk * qb - start)
    return jnp.where(jnp.abs(gap) <= DIL_RADIUS, s, NEG_INF)


def _pick_heads(stacked, qb):
    lane = lax.broadcasted_iota(jnp.int32, (qb, 256), 1)
    out = jnp.zeros((qb, 256), stacked.dtype)
    for h in range(4):
        out = jnp.where((lane >> 6) == h, stacked[h * qb:(h + 1) * qb], out)
    return out


def _stack_head_cols(ref, t, qb):
    return jnp.concatenate([ref[0, t * qb:(t + 1) * qb, 64 * h:64 * h + 1] for h in range(4)], axis=0)


def _dil_fwd(q, k, v, *, name, after=None):
    dil, length, _ = q.shape
    qb, win, grp = _dil_blocks(length)
    extra = [] if after is None else [after]

    def body(q_ref, k_ref, v_ref, *rest):
        o_ref, lse_ref = rest[-2:]
        blks = [pl.program_id(1) * grp + t for t in range(grp)]
        wins = [_dil_window(k_ref, v_ref, b, qb, win, length) for b in blks]
        raw = [lax.dot_general(_stack_lanes(q_ref, t, qb, QK_SCALE), w[0], NT_DIMS, preferred_element_type=F32)
               for t, w in enumerate(wins)]
        lses, outs = [], []
        for t, (s, w) in enumerate(zip(raw, wins)):
            s = _dil_mask(s, blks[t], w[2], qb, win)
            m = jnp.max(s, axis=-1, keepdims=True)
            e = jnp.exp(s - m)
            norm = jnp.sum(e, axis=-1, keepdims=True)
            lses.append(m + jnp.log(norm))
            outs.append(jnp.dot((e * (1.0 / norm)).astype(BF16), w[1], preferred_element_type=F32))
        for t in range(grp):
            o_ref[0, t * qb:(t + 1) * qb, :] = _pick_heads(outs[t], qb)
            lse_ref[0, t * qb:(t + 1) * qb, :] = _pick_heads(jnp.broadcast_to(lses[t], (4 * qb, 256)), qb)

    seq = pl.BlockSpec((1, length, 256), lambda j, i: (j, 0, 0))
    blk = pl.BlockSpec((1, grp * qb, 256), lambda j, i: (j, i, 0))
    return pl.pallas_call(
        body, name=name, grid=(dil, length // (grp * qb)),
        in_specs=[blk, seq, seq] + [pl.BlockSpec(memory_space=pl.ANY)] * len(extra), out_specs=[blk, blk],
        out_shape=[_sds((dil, length, 256), F32)] * 2,
        compiler_params=_params("parallel", "parallel"),
    )(q, k, v, *extra)


def _dil_bwd(q, k, v, do, lse, cc, *, name):
    dil, length, _ = q.shape
    qb, win, grp = _dil_blocks(length)

    def body(q_ref, k_ref, v_ref, do_ref, lse_ref, cc_ref, dq_ref, dk_ref, dv_ref):
        @pl.when(pl.program_id(1) == 0)
        def _():
            dk_ref[...] = jnp.zeros_like(dk_ref)
            dv_ref[...] = jnp.zeros_like(dv_ref)

        blks = [pl.program_id(1) * grp + t for t in range(grp)]
        wins = [_dil_window(k_ref, v_ref, b, qb, win, length) for b in blks]
        qss = [_stack_lanes(q_ref, t, qb, QK_SCALE) for t in range(grp)]
        doss = [_stack_lanes(do_ref, t, qb) for t in range(grp)]
        raw = [lax.dot_general(qs, w[0], NT_DIMS, preferred_element_type=F32) for qs, w in zip(qss, wins)]
        dps = [lax.dot_general(dos, w[1], NT_DIMS, preferred_element_type=F32) for dos, w in zip(doss, wins)]
        probs = [jnp.exp(_dil_mask(s, blks[t], wins[t][2], qb, win) - _stack_head_cols(lse_ref, t, qb))
                 for t, s in enumerate(raw)]
        dsbs = [(p * (dp + _stack_head_cols(cc_ref, t, qb))).astype(BF16)
                for t, (p, dp) in enumerate(zip(probs, dps))]
        dq4s = [jnp.dot(dsb, w[0], preferred_element_type=F32) for dsb, w in zip(dsbs, wins)]
        dkws = [lax.dot_general(dsb, qs, TN_DIMS, preferred_element_type=F32) for dsb, qs in zip(dsbs, qss)]
        dvws = [lax.dot_general(p.astype(BF16), dos, TN_DIMS, preferred_element_type=F32)
                for p, dos in zip(probs, doss)]
        for t in range(grp):
            dq_ref[0, t * qb:(t + 1) * qb, :] = _pick_heads(dq4s[t], qb) * QK_SCALE
            dk_ref[0, pl.ds(wins[t][2], win), :] += dkws[t]
            dv_ref[0, pl.ds(wins[t][2], win), :] += dvws[t]

    seq = pl.BlockSpec((1, length, 256), lambda j, i: (j, 0, 0))
    blk = pl.BlockSpec((1, grp * qb, 256), lambda j, i: (j, i, 0))
    return pl.pallas_call(
        body, name=name, grid=(dil, length // (grp * qb)),
        in_specs=[blk, seq, seq, blk, blk, blk], out_specs=[blk, seq, seq],
        out_shape=[_sds((dil, length, 256), F32)] * 3,
        compiler_params=_params("parallel", "arbitrary"),
    )(q, k, v, do, lse, cc)


def _merge_weights(lses):
    m = jnp.maximum(jnp.maximum(lses[0], lses[1]), lses[2])
    es = [jnp.exp(t - m) for t in lses]
    inv = 1.0 / (es[0] + es[1] + es[2])
    return [e * inv for e in es]


def _dil_merge(outs, lses, *, tm, name):
    n = outs[0].shape[1]

    def body(*refs):
        o_in, l_in = refs[0:3], refs[3:6]
        y_ref, yb_ref, scr = refs[6:9]
        lv = [_load_token_order(l_in[g], scr, d, tm) for g, d in enumerate(DIL_DILATIONS)]
        ws = _merge_weights(lv)
        y = jnp.zeros((tm, 256), F32)
        for g, d in enumerate(DIL_DILATIONS):
            y = y + ws[g] * _load_token_order(o_in[g], scr, d, tm)
        y_ref[...] = y
        yb_ref[...] = y.astype(BF16)

    specs = [_dil_spec(d, tm) for d in DIL_DILATIONS]
    return pl.pallas_call(
        body, name=name, grid=(n // tm,), in_specs=specs + specs,
        out_specs=[_rows(tm, 256)] * 2, out_shape=[_sds((n, 256), F32), _sds((n, 256), BF16)],
        scratch_shapes=[_dil_scratch(tm)],
        compiler_params=_params("parallel"),
    )(*outs, *lses)


def _dil_merge_bwd(dy, y, lses, *, tm, name):
    n = dy.shape[0]

    def body(*refs):
        dy_ref, y_ref = refs[0:2]
        l_in = refs[2:5]
        do_out, cc_out = refs[5:8], refs[8:11]
        scr = refs[11]
        lv = [_load_token_order(l_in[g], scr, d, tm) for g, d in enumerate(DIL_DILATIONS)]
        ws = _merge_weights(lv)
        dyv = dy_ref[...]
        rr = lax.broadcasted_iota(jnp.int32, (256, 256), 0) >> 6
        cc = lax.broadcasted_iota(jnp.int32, (256, 256), 1) >> 6
        ones = jnp.where(rr == cc, 1.0, 0.0).astype(F32)
        tsum = jnp.dot(dyv * y_ref[...], ones, preferred_element_type=F32,
                       precision=lax.Precision.HIGHEST)
        for g, d in enumerate(DIL_DILATIONS):
            _store_dil_order(ws[g] * dyv, do_out[g], scr, d, tm)
            _store_dil_order(-ws[g] * tsum, cc_out[g], scr, d, tm)

    specs = [_dil_spec(d, tm) for d in DIL_DILATIONS]
    res = pl.pallas_call(
        body, name=name, grid=(n // tm,),
        in_specs=[_rows(tm, 256)] * 2 + specs,
        out_specs=specs + specs,
        out_shape=[_sds((d, n // d, 256), BF16) for d in DIL_DILATIONS]
                  + [_sds((d, n // d, 256), F32) for d in DIL_DILATIONS],
        scratch_shapes=[_dil_scratch(tm)],
        compiler_params=_params("parallel"),
    )(dy, y, *lses)
    return res[0:3], res[3:6]


_WEIGHTS = (("w_in", 1, 736), ("w_branch_na", 1, 128), ("w_branch_dil", 1, 128), ("w_out", 0, 128),
            ("w_up", 1, 512), ("w_down", 0, 512), ("w_ple_gate", 0, 128), ("w_ple_proj", 1, 128))
_W_IN, _W_BNA, _W_BD, _W_OUT, _W_UP, _W_DOWN, _W_PG, _W_PP = range(8)


def _to_full(gathered):
    return gathered.reshape(-1, gathered.shape[2])


def _to_chunks(widx, mat):
    return mat.reshape(N_DEV, _WEIGHTS[widx][2], mat.shape[1])


def _local_step(x, p_bf16, positions, target, g_mix, g_mlp, g_ple, g_final, rpb2,
                get_w_in, relay_rest, get_rest, send_grads):
    tm = 256
    half = HEAD_DIM // 2
    inv_freq = 10000.0 ** (-jnp.arange(half, dtype=F32) / half)
    ang = positions.astype(F32)[:, None] * inv_freq
    cos, sin = jnp.cos(ang), jnp.sin(ang)
    cos_t = jnp.tile(jnp.concatenate([cos, cos], axis=-1), (1, 4))
    sin_t = jnp.tile(jnp.concatenate([-sin, sin], axis=-1), (1, 4))
    rb = _rpb_table(rpb2)

    a = _rms_fwd(x, g_mix, tm=tm, name="rms_mix")
    w_in, token = get_w_in(a)
    qkv_width = 3 * NA_WIDTH + 3 * DIL_WIDTH
    proj = _matmul(a, w_in, tb=True, n_limit=qkv_width, out_dtype=F32, tm=512, tn=qkv_width // 2, tk=1024,
                   name="mm_in_qkv", after=token)
    gates = _matmul(a, w_in[qkv_width:], tb=True, stack_cols=True, out_dtype=F32, tm=512, tn=D_MODEL, tk=1024,
                    name="mm_in_gates", epilogue=lambda acc: (_sigmoid(acc),))
    sn, sd = (gates, 0), (gates, 1)
    na_qkv, dq_g, dk_g, dv_g = _split_proj(proj, cos_t, sin_t, tm=tm, name="split_proj")
    y_na = _na_fwd(*na_qkv, rb, name="na_fwd")
    token = relay_rest(y_na)
    d_out, d_lse = [], []
    for g in range(3):
        o, lse = _dil_fwd(dq_g[g], dk_g[g], dv_g[g], name=f"dil_fwd{g}", after=token if g == 0 else None)
        d_out.append(o)
        d_lse.append(lse)
    y_dil, y_dil_b = _dil_merge(d_out, d_lse, tm=tm, name="dil_merge")
    w_bna, w_bd, w_out, w_up, w_down, w_pg, w_pp = get_rest(y_dil_b)
    bn = _matmul(y_na, w_bna, tb=True, out_dtype=F32, tm=512, tn=1024, tk=512, name="mm_bna")
    bd, mixed = _matmul(y_dil_b, w_bd, tb=True, out_dtype=(F32, BF16), tm=512, tn=1024, tk=256, name="mm_bd",
                        extra=(sn, bn, sd), epilogue=lambda acc, s1, b1, s2: (acc, s1 * b1 + s2 * acc))
    h1, c = _matmul(mixed, w_out, out_dtype=(F32, BF16), tm=512, tn=1024, tk=1024, name="mm_out",
                    extra=(x, g_mlp), epilogue=_residual_rms_tile)
    u, f = _matmul(c, w_up, tb=True, out_dtype=(F32, BF16), tm=512, tn=2048, tk=1024, name="mm_up",
                   epilogue=lambda acc: (acc, jnp.square(jnp.maximum(acc, 0.0))))
    h2, e = _matmul(f, w_down, out_dtype=(F32, BF16), tm=512, tn=1024, tk=4096, name="mm_down",
                    extra=(h1, g_ple), epilogue=_residual_rms_tile)
    pp = _matmul(p_bf16, w_pp, tb=True, out_dtype=F32, tm=512, tn=1024, tk=256, name="mm_pp")

    dh3, dpp, dgt, dg_final, loss = _matmul(
        e, w_pg, out_dtype=(F32, BF16, BF16), tm=512, tn=1024, tk=1024, name="mm_pg_tail",
        extra=(pp, h2, target, g_final), epilogue=_tail_tile, n_colsum=2)
    loss = loss[:, :128]
    gw_pp = _matmul(p_bf16, dpp, ta=True, transpose_out=True, out_dtype=BF16, tm=256, tn=1024, tk=2048,
                    name="mm_gw_pp")
    gw_pg = _matmul(e, dgt, ta=True, out_dtype=BF16, tm=512, tn=1024, tk=2048, name="mm_gw_pg")
    dh2, dh2_b, dg_ple = _matmul(
        dgt, w_pg, tb=True, out_dtype=(F32, BF16), tm=512, tn=1024, tk=1024, name="mm_de",
        extra=(h2, g_ple, dh3), epilogue=_rms_bwd_twice, n_colsum=1)
    du = _matmul(dh2_b, w_down, tb=True, out_dtype=BF16, tm=512, tn=2048, tk=1024, name="mm_du",
                 extra=(u,), epilogue=lambda acc, uv: (acc * (2.0 * jnp.maximum(uv, 0.0)),))
    gw_down = _matmul(f, dh2_b, ta=True, out_dtype=BF16, tm=1024, tn=1024, tk=2048, name="mm_gw_down")
    gw_up = _matmul(c, du, ta=True, transpose_out=True, out_dtype=BF16, tm=512, tn=2048, tk=2048, name="mm_gw_up")
    dh1, dh1_b, dg_mlp = _matmul(
        du, w_up, out_dtype=(F32, BF16), tm=512, tn=1024, tk=4096, name="mm_dc",
        extra=(h1, g_mlp, dh2), epilogue=_rms_bwd_twice, n_colsum=1)
    dbn, dbd, dgn, dgd = _matmul(dh1_b, w_out, tb=True, out_dtype=(BF16,) * 4, tm=512, tn=1024, tk=1024,
                                 name="mm_dmixed", extra=(sn, bn, sd, bd), epilogue=_gate_bwd_tile)
    gw_out = _matmul(mixed, dh1_b, ta=True, out_dtype=BF16, tm=512, tn=1024, tk=2048, name="mm_gw_out")
    gw_bna = _matmul(y_na, dbn, ta=True, transpose_out=True, out_dtype=BF16, tm=512, tn=1024, tk=2048,
                     name="mm_gw_bna")
    dy_na = _matmul(dbn, w_bna, out_dtype=BF16, tm=512, tn=512, tk=1024, name="mm_dy_na")
    gw_bd = _matmul(y_dil_b, dbd, ta=True, transpose_out=True, out_dtype=BF16, tm=256, tn=1024, tk=2048,
                    name="mm_gw_bd")
    token = send_grads((_W_PP, _W_PG, _W_DOWN, _W_UP, _W_OUT, _W_BNA, _W_BD),
                       (gw_pp, gw_pg, gw_down, gw_up, gw_out, gw_bna, gw_bd))
    dy_dil = _matmul(dbd, w_bd, out_dtype=F32, tm=512, tn=256, tk=1024, name="mm_dy_dil", after=token)
    dna = _na_bwd(*na_qkv, dy_na, rb, name="na_bwd")
    drpb = _rpb_grad(dna[3].reshape(8, -1), name="rpb_grad")
    do_g, cc_g = _dil_merge_bwd(dy_dil, y_dil, d_lse, tm=tm, name="dil_merge_bwd")
    ddq, ddk, ddv = [], [], []
    for g in range(3):
        r = _dil_bwd(dq_g[g], dk_g[g], dv_g[g], do_g[g], d_lse[g], cc_g[g], name=f"dil_bwd{g}")
        ddq.append(r[0])
        ddk.append(r[1])
        ddv.append(r[2])
    dproj = _assemble_dproj(dna[0:3], ddq, ddk, ddv, dgn, dgd, cos_t, sin_t, tm=tm, name="assemble_dproj")
    gw_in = _matmul(a, dproj, ta=True, transpose_out=True, out_dtype=BF16, tm=512, tn=2944, tk=2048, name="mm_gw_in")
    token = send_grads((_W_IN,), (gw_in,))
    dx, dg_mix = _matmul(
        dproj, w_in, out_dtype=(F32,), tm=512, tn=1024, tk=5888, name="mm_da", after=token,
        extra=(x, g_mix, dh1), epilogue=_rms_bwd_tile, n_colsum=1)
    return loss, dx, (dg_mix, dg_mlp, dg_ple, dg_final), drpb


def _cast_bf16(t, *, name):
    def body(t_ref, o_ref):
        o_ref[...] = t_ref[...].astype(BF16)

    rows, cols = t.shape
    tr = 256 if rows % 256 == 0 else rows
    blk = pl.BlockSpec((tr, cols), lambda i: (i, 0))
    return pl.pallas_call(body, name=name, grid=(rows // tr,), in_specs=[blk], out_specs=blk,
                          out_shape=_sds(t.shape, BF16), compiler_params=_params("parallel"))(t)


def _adamw(w, g, m, v):
    m = ADAM_B1 * m + (1.0 - ADAM_B1) * g
    v = ADAM_B2 * v + (1.0 - ADAM_B2) * (g * g)
    m_hat = m / (1.0 - ADAM_B1 ** ADAM_STEP)
    v_hat = v / (1.0 - ADAM_B2 ** ADAM_STEP)
    delta = -ADAM_LR * (m_hat / (jnp.sqrt(v_hat) + ADAM_EPS) + ADAM_WD * w)
    return delta, m, v


def _sum_adamw(parts, w, m, v, *, tr, name, own=None, transposed=False):
    rows, cols = w.shape
    n_pre = 0 if own is None else 1

    def body(*refs):
        p_ref, w_ref, m_ref, v_ref = refs[n_pre:n_pre + 4]
        g_ref, d_ref, nm_ref, nv_ref = refs[-4:]
        g = (p_ref[0] if own is None else refs[n_pre + 4][...]).astype(F32)
        for s in range(1, N_DEV):
            g = g + p_ref[s].astype(F32)
        if transposed:
            g = g.T
        g_ref[...] = g
        d_ref[...], nm_ref[...], nv_ref[...] = _adamw(w_ref[...], g, m_ref[...], v_ref[...])

    if transposed:
        blk = pl.BlockSpec((rows, tr), lambda i, *_: (0, i))
        g_rows, steps = rows, cols // tr
    else:
        blk = pl.BlockSpec((tr, cols), lambda i, *_: (i, 0))
        g_rows, steps = cols, rows // tr
    in_specs = [pl.BlockSpec((N_DEV, tr, g_rows), lambda i, *_: (0, i, 0)), blk, blk, blk]
    args = [parts, w, m, v]
    if own is not None:
        in_specs.append(pl.BlockSpec((None, tr, g_rows), lambda i, idx: (idx[0], i, 0)))
        args = [own[1]] + args + [own[0]]
    return pl.pallas_call(
        body, name=name,
        grid_spec=pltpu.PrefetchScalarGridSpec(num_scalar_prefetch=n_pre, grid=(steps,), in_specs=in_specs,
                                               out_specs=[blk] * 4),
        out_shape=[_sds((rows, cols), F32)] * 4,
        compiler_params=_params("parallel"),
    )(*args)


_RPB_SIZE = 8 * 15 * 31


def _pack_small(g_mix, g_mlp, g_ple, g_final, rpb, loss_row):
    flat = jnp.concatenate([g_mix.reshape(-1), g_mlp.reshape(-1), g_ple.reshape(-1), g_final.reshape(-1),
                            rpb.reshape(-1), jnp.zeros((3840 - _RPB_SIZE,), F32), loss_row.reshape(-1),
                            jnp.zeros((128,), F32)])
    return flat.reshape(64, 128)


def _unpack_small(t):
    flat = t.reshape(-1)
    return (flat[0:1024].reshape(1, 1024), flat[4096:4096 + _RPB_SIZE].reshape(1, 8, 15, 31),
            flat[1024:2048].reshape(1, 1024), flat[2048:3072].reshape(1, 1024), flat[3072:4096])


def kernel(x, p, positions, g_mix, w_in, rpb, w_branch_na, w_branch_dil, w_out, g_mlp, w_up, w_down, g_ple, w_ple_gate, w_ple_proj, g_final, loss_target, m_g_mix, m_w_in, m_rpb, m_w_branch_na, m_w_branch_dil, m_w_out, m_g_mlp, m_w_up, m_w_down, m_g_ple, m_w_ple_gate, m_w_ple_proj, m_g_final, v_g_mix, v_w_in, v_rpb, v_w_branch_na, v_w_branch_dil, v_w_out, v_g_mlp, v_w_up, v_w_down, v_g_ple, v_w_ple_gate, v_w_ple_proj, v_g_final):
    sharded = dict(w_in=(w_in, m_w_in, v_w_in), w_branch_na=(w_branch_na, m_w_branch_na, v_w_branch_na),
                   w_branch_dil=(w_branch_dil, m_w_branch_dil, v_w_branch_dil), w_out=(w_out, m_w_out, v_w_out),
                   w_up=(w_up, m_w_up, v_w_up), w_down=(w_down, m_w_down, v_w_down),
                   w_ple_gate=(w_ple_gate, m_w_ple_gate, v_w_ple_gate),
                   w_ple_proj=(w_ple_proj, m_w_ple_proj, v_w_ple_proj))
    shards = {k: tuple(t[0] for t in val) for k, val in sharded.items()}

    me = _my_index()

    shards["w_in"] = tuple(t.T for t in shards["w_in"])

    w_in_b = _cast_bf16(shards["w_in"][0], name="cast_w_in")
    rest_b = [shards[name][0].astype(BF16).T if axis == 1 else shards[name][0].astype(BF16)
              for name, axis, _ in _WEIGHTS[1:]]
    first_in, token_in = _start_copies(_first_leg_copies, [w_in_b], [_sds((N_DEV,) + w_in_b.shape, BF16)], 4,
                                       name="start_gather_w_in")

    def whole(landed, mine):
        return _to_full(lax.dynamic_update_index_in_dim(landed, mine, me, 0))

    rest = {}

    def get_w_in(after):
        (mine,), landed = _wait_copies(_first_leg_copies, first_in, after, name="wait_gather_w_in")
        second, token = _start_copies(_second_leg_copies, [], landed, 3, name="start_forward_w_in")
        _, (landed,) = _wait_copies(_second_leg_copies, second, token, name="wait_forward_w_in")
        rest["first"], token = _start_copies(_first_leg_copies, rest_b,
                                             [_sds((N_DEV,) + t.shape, BF16) for t in rest_b], 4 * len(rest_b),
                                             name="start_gather_rest", after=landed)
        return whole(landed, mine), token

    def relay_rest(after):
        rest["mine"], landed = _wait_copies(_first_leg_copies, rest["first"], after, name="wait_gather_rest")
        rest["second"], token = _start_copies(_second_leg_copies, [], landed, 3 * len(rest_b),
                                              name="start_forward_rest")
        return token

    def get_rest(after):
        _, landed = _wait_copies(_second_leg_copies, rest["second"], after, name="wait_forward_rest")
        return [whole(t, own) for t, own in zip(landed, rest["mine"])]

    sent = []

    def send_grads(indices, grads):
        chunked = [_to_chunks(i, g) for i, g in zip(indices, grads)]
        handle, token = _start_copies(_exchange_copies, chunked, [_sds(t.shape, BF16) for t in chunked],
                                      7 * len(chunked),
                                      name="start_exchange_" + ("w_in" if indices == (_W_IN,) else "rest"))
        sent.append((indices, handle))
        return token

    g_mix_0 = g_mix + token_in[0:1, 0:1]
    loss, dx, dgs, drpb = _local_step(
        x[0], p[0, 0].astype(BF16), positions[0], loss_target[0],
        g_mix_0, g_mlp, g_ple, g_final.reshape(1, -1), rpb[0], get_w_in, relay_rest, get_rest, send_grads)

    drpb3 = drpb.reshape(8, 16, 32)[:, :15, :31]
    small = _pack_small(dgs[0], dgs[1], dgs[2], dgs[3], drpb3, loss)
    share, done = _start_copies(_gather_copies, [small], [_sds((N_DEV,) + small.shape, F32)], 7,
                                name="start_share_small")

    out = {}
    for indices, handle in sent:
        chunked, landed = _wait_copies(_exchange_copies, handle, done,
                                       name="wait_exchange_" + ("w_in" if indices == (_W_IN,) else "rest"))
        for i, part, mine in zip(indices, landed, chunked):
            name = _WEIGHTS[i][0]
            w, m, v = shards[name]
            turned = _WEIGHTS[i][1] == 1 and i != _W_IN
            res = _sum_adamw(part, w, m, v, tr=368 if i == _W_IN else 128, name="adamw_" + name,
                             own=(mine, me.reshape(1).astype(jnp.int32)), transposed=turned)
            out[name] = [(t.T if i == _W_IN else t)[None] for t in res]
            done = res[0]
    (small,), (small_landed,) = _wait_copies(_gather_copies, share, done, name="wait_share_small")
    small_all = lax.dynamic_update_index_in_dim(small_landed, small, me, 0)
    small_w = _pack_small(g_mix, g_mlp, g_ple, g_final, rpb, jnp.zeros((128,), F32))
    small_m = _pack_small(m_g_mix, m_g_mlp, m_g_ple, m_g_final, m_rpb, jnp.zeros((128,), F32))
    small_v = _pack_small(v_g_mix, v_g_mlp, v_g_ple, v_g_final, v_rpb, jnp.zeros((128,), F32))
    res = _sum_adamw(small_all, small_w, small_m, small_v, tr=64, name="adamw_small")
    unpacked = [_unpack_small(t) for t in res]
    for i, name in enumerate(("g_mix", "rpb", "g_mlp", "g_ple", "g_final")):
        out[name] = [u[i] for u in unpacked]
    loss_total = res[0][62, 0]

    order = ("g_mix", "w_in", "rpb", "w_branch_na", "w_branch_dil", "w_out", "g_mlp", "w_up", "w_down",
             "g_ple", "w_ple_gate", "w_ple_proj", "g_final")
    grads = [out[k][0] for k in order]
    deltas = [out[k][1] for k in order]
    new_m = [out[k][2] for k in order]
    new_v = [out[k][3] for k in order]
    return (loss_total, dx[None], *grads, *deltas, *new_m, *new_v)
```

```python
import jax
import jax.numpy as jnp
from jax import lax
from jax.experimental import pallas as pl
from jax.experimental.pallas import tpu as pltpu

F32 = jnp.float32
BF16 = jnp.bfloat16

D_MODEL = 1024
HEAD_DIM = 64
GRID_W = 64
NA_WIDTH = 512
DIL_WIDTH = 768
IN_WIDTH = 5888
DIL_DILATIONS = (1, 4, 16)
DIL_RADIUS = 64
NA_WIN_ROWS = 8
RMS_EPS = 1e-6
NEG_INF = -1e30
QK_SCALE = HEAD_DIM ** -0.5

ADAM_LR = 0.001
ADAM_B1 = 0.9
ADAM_B2 = 0.999
ADAM_EPS = 1e-08
ADAM_WD = 0.01
ADAM_STEP = 10

N_DEV = 8
VMEM_LIMIT = 56 * 1024 * 1024
EPILOGUE_ROWS = 256
MESH = pl.DeviceIdType.MESH

NT_DIMS = (((1,), (1,)), ((), ()))
TN_DIMS = (((0,), (0,)), ((), ()))


def _sds(shape, dtype):
    return jax.ShapeDtypeStruct(shape, dtype)


def _params(*sem):
    return pltpu.CompilerParams(dimension_semantics=sem, vmem_limit_bytes=VMEM_LIMIT)


def _rows(tm, width, col=0):
    return pl.BlockSpec((tm, width), lambda i, c=col: (i, c))


def _const(shape):
    zeros = (0,) * len(shape)
    return pl.BlockSpec(shape, lambda i: zeros)


def _my_index():
    return 4 * lax.axis_index("x") + 2 * lax.axis_index("y") + lax.axis_index("c")


def _peer(k):
    x, y, c = lax.axis_index("x"), lax.axis_index("y"), lax.axis_index("c")
    px = 1 - x if k & 4 else x
    py = 1 - y if k & 2 else y
    pc = 1 - c if k & 1 else c
    return (px, py, pc), 4 * px + 2 * py + pc


def _call(body, *, name, grid, in_specs, out_specs, out_shape, scratch_shapes, args, after=None):
    n_in, n_out = len(in_specs), len(out_specs)
    extra = [] if after is None else [after]
    n_x = n_in + len(extra)

    def plain(*refs):
        body(refs[:n_in], refs[n_x:n_x + n_out], refs[n_x + n_out:])

    res = pl.pallas_call(plain, name=name, grid=grid,
                         in_specs=list(in_specs) + [pl.BlockSpec(memory_space=pl.ANY)] * len(extra),
                         out_specs=out_specs, out_shape=out_shape, scratch_shapes=scratch_shapes,
                         compiler_params=_params(*(("arbitrary",) * len(grid))))(*args, *extra)
    return list(res)


_HBM_SPEC = pl.BlockSpec(memory_space=pltpu.HBM)
_SEM_SPEC = pl.BlockSpec(memory_space=pltpu.SEMAPHORE)
_SIDE_EFFECT = pltpu.SideEffectType.DATAFLOW_SIDE_EFFECTING


_FIRST_LEG = (1, 2, 4, 6)
_SECOND_LEG = (2, 4, 6)


def _gather_copies(srcs, lands, send, recv, sending):
    me = _my_index()
    out = []
    for w in range(len(srcs)):
        for k in range(1, N_DEV):
            dev, idx = _peer(k)
            out.append(pltpu.make_async_remote_copy(
                src_ref=srcs[w], dst_ref=lands[w].at[me if sending else idx],
                send_sem=send.at[w * 7 + k - 1], recv_sem=recv.at[w * 7 + k - 1],
                device_id=dev, device_id_type=MESH))
    return out


def _first_leg_copies(srcs, lands, send, recv, sending):
    me = _my_index()
    out = []
    for w in range(len(srcs)):
        for j, k in enumerate(_FIRST_LEG):
            dev, idx = _peer(k)
            out.append(pltpu.make_async_remote_copy(
                src_ref=srcs[w], dst_ref=lands[w].at[me if sending else idx],
                send_sem=send.at[w * 4 + j], recv_sem=recv.at[w * 4 + j],
                device_id=dev, device_id_type=MESH))
    return out


def _second_leg_copies(srcs, lands, send, recv, sending):
    sibling, _ = _peer(1)
    out = []
    for w in range(len(lands)):
        for j, k in enumerate(_SECOND_LEG):
            slot = _peer(k if sending else k ^ 1)[1]
            out.append(pltpu.make_async_remote_copy(
                src_ref=lands[w].at[slot], dst_ref=lands[w].at[slot],
                send_sem=send.at[w * 3 + j], recv_sem=recv.at[w * 3 + j],
                device_id=sibling, device_id_type=MESH))
    return out


def _exchange_copies(srcs, lands, send, recv, sending):
    out = []
    for w in range(len(srcs)):
        for k in range(1, N_DEV):
            dev, idx = _peer(k)
            out.append(pltpu.make_async_remote_copy(
                src_ref=srcs[w].at[idx], dst_ref=lands[w].at[k],
                send_sem=send.at[w * 7 + k - 1], recv_sem=recv.at[w * 7 + k - 1],
                device_id=dev, device_id_type=MESH))
    return out


def _start_copies(make, srcs, lands, n_copies, *, name, after=None):
    n_src, n_buf = len(srcs), len(srcs) + len(lands)
    extra = [] if after is None else [after]

    def body(*refs):
        send, recv = refs[n_buf + len(extra)], refs[n_buf + len(extra) + 1]
        for cp in make(refs[:n_src], refs[n_src:n_buf], send, recv, True):
            cp.start()
        refs[-1][...] = jnp.zeros_like(refs[-1])

    bufs = list(srcs) + [lax.empty(t.shape, t.dtype) if isinstance(t, jax.ShapeDtypeStruct) else t for t in lands]
    res = pl.pallas_call(
        body, name=name,
        out_shape=(pltpu.SemaphoreType.DMA((n_copies,)), pltpu.SemaphoreType.DMA((n_copies,)),
                   *[pltpu.HBM(t.shape, t.dtype) for t in bufs], _sds((8, 128), F32)),
        in_specs=[_HBM_SPEC] * n_buf + [pl.BlockSpec(memory_space=pl.ANY)] * len(extra),
        out_specs=(_SEM_SPEC, _SEM_SPEC, *([_HBM_SPEC] * n_buf), pl.BlockSpec(memory_space=pltpu.VMEM)),
        input_output_aliases={i: 2 + i for i in range(n_buf)},
        compiler_params=pltpu.CompilerParams(has_side_effects=_SIDE_EFFECT),
    )(*[pltpu.with_memory_space_constraint(t, pltpu.HBM) for t in bufs], *extra)
    return (n_src, res[0], res[1], res[2:2 + n_buf]), res[-1]


def _wait_copies(make, handle, after, *, name):
    n_src, send_sems, recv_sems, bufs = handle
    n_buf = len(bufs)

    def body(*refs):
        for cp in make(refs[:n_src], refs[n_src:n_buf], refs[n_buf], refs[n_buf + 1], False):
            cp.wait_send()
            cp.wait_recv()

    res = pl.pallas_call(
        body, name=name,
        out_shape=tuple(pltpu.HBM(t.shape, t.dtype) for t in bufs),
        in_specs=[_HBM_SPEC] * n_buf + [_SEM_SPEC, _SEM_SPEC, pl.BlockSpec(memory_space=pl.ANY)],
        out_specs=tuple([_HBM_SPEC] * n_buf),
        input_output_aliases={i: i for i in range(n_buf)},
        compiler_params=pltpu.CompilerParams(has_side_effects=_SIDE_EFFECT),
    )(*bufs, send_sems, recv_sems, after)
    return list(res[:n_src]), list(res[n_src:])


def _matmul(a, b, *, ta=False, tb=False, out_dtype, tm, tn, tk, name, after=None, extra=(), epilogue=None,
            n_colsum=0, transpose_out=False, n_limit=None, stack_cols=False):
    m, k = (a.shape[1], a.shape[0]) if ta else a.shape
    n = n_limit or (b.shape[0] if tb else b.shape[1])
    tm, tn, tk = min(tm, m), min(tn, n), min(tk, k)
    nk = k // tk
    dims = (((0 if ta else 1,), (1 if tb else 0,)), ((), ()))
    out_dtypes = out_dtype if isinstance(out_dtype, tuple) else (out_dtype,)
    n_tiles = len(out_dtypes)

    def add_colsums(o_refs, sums):
        i = pl.program_id(1)
        for s_ref, val in zip(o_refs[n_tiles:], sums):
            @pl.when(i == 0)
            def _(s_ref=s_ref, val=val):
                s_ref[...] = val

            @pl.when(i > 0)
            def _(s_ref=s_ref, val=val):
                s_ref[...] += val

    def finish(acc, x_refs, o_refs):
        vals = (acc,) if epilogue is None else epilogue(acc, *[r[...] for r in x_refs])
        for o_ref, val in zip(o_refs[:n_tiles], vals[:n_tiles]):
            o_ref[...] = (val.T if transpose_out else val).astype(o_ref.dtype)
        add_colsums(o_refs, vals[n_tiles:])

    chunk = EPILOGUE_ROWS if (nk == 1 and epilogue is not None and not ta and tm % EPILOGUE_ROWS == 0) else None

    def body(ins, outs, acc):
        a_ref, b_ref = ins[:2]
        if chunk is not None:
            sums = None
            for r0 in range(0, tm, chunk):
                part = lax.dot_general(a_ref[r0:r0 + chunk, :], b_ref[...], dims, preferred_element_type=F32)
                vals = epilogue(part, *[r[...] if r.shape[0] == 1 else r[r0:r0 + chunk, :] for r in ins[2:]])
                for o_ref, val in zip(outs[:n_tiles], vals[:n_tiles]):
                    o_ref[r0:r0 + chunk, :] = val.astype(o_ref.dtype)
                sums = vals[n_tiles:] if sums is None else [s + v for s, v in zip(sums, vals[n_tiles:])]
            add_colsums(outs, sums)
            return
        part = lax.dot_general(a_ref[...], b_ref[...], dims, preferred_element_type=F32)
        if nk == 1:
            finish(part, ins[2:], outs)
            return
        acc_ref, = acc
        kk = pl.program_id(2)

        @pl.when(kk == 0)
        def _():
            acc_ref[...] = part

        @pl.when(kk > 0)
        def _():
            acc_ref[...] += part

        @pl.when(kk == nk - 1)
        def _():
            finish(acc_ref[...], ins[2:], outs)

    a_spec = (pl.BlockSpec((tk, tm), lambda j, i, kk: (kk, i)) if ta
              else pl.BlockSpec((tm, tk), lambda j, i, kk: (i, kk)))
    b_spec = (pl.BlockSpec((tn, tk), lambda j, i, kk: (j, kk)) if tb
              else pl.BlockSpec((tk, tn), lambda j, i, kk: (kk, j)))
    tile = pl.BlockSpec((tm, tn), lambda j, i, kk: (i, j))
    row = pl.BlockSpec((1, tn), lambda j, i, kk: (0, j))

    def x_spec(t):
        if isinstance(t, tuple):
            return pl.BlockSpec((None, tm, tn), lambda j, i, kk, lead=t[1]: (lead, i, j))
        return row if t.shape[0] == 1 else tile

    out_tile, out_dims = (pl.BlockSpec((tn, tm), lambda j, i, kk: (j, i)), (n, m)) if transpose_out else (tile, (m, n))
    if stack_cols:
        out_tile, out_dims = pl.BlockSpec((None, tm, tn), lambda j, i, kk: (j, i, 0)), (n // tn, m, tn)
    res = _call(
        body, name=name, grid=(n // tn, m // tm, nk),
        in_specs=[a_spec, b_spec] + [x_spec(t) for t in extra],
        out_specs=[out_tile] * n_tiles + [row] * n_colsum,
        out_shape=[_sds(out_dims, dt) for dt in out_dtypes] + [_sds((1, n), F32)] * n_colsum,
        scratch_shapes=[] if nk == 1 else [pltpu.VMEM((tm, tn), F32)],
        args=(a, b, *[t[0] if isinstance(t, tuple) else t for t in extra]), after=after)
    return res if isinstance(out_dtype, tuple) or n_colsum else res[0]


def _rstd(h):
    return lax.rsqrt(jnp.mean(h * h, axis=-1, keepdims=True) + RMS_EPS)


def _sigmoid(z):
    return 1.0 / (1.0 + jnp.exp(-z))


def _rms_fwd(x, g, *, tm, name):
    n = x.shape[0]

    def body(x_ref, g_ref, o_ref):
        h = x_ref[...]
        o_ref[...] = (h * _rstd(h) * g_ref[...]).astype(BF16)

    return pl.pallas_call(
        body, name=name, grid=(n // tm,),
        in_specs=[_rows(tm, D_MODEL), _const((1, D_MODEL))],
        out_specs=_rows(tm, D_MODEL), out_shape=_sds((n, D_MODEL), BF16),
        compiler_params=_params("parallel"),
    )(x, g)


def _swap_halves(t):
    lane = lax.broadcasted_iota(jnp.int32, (t.shape[0], 128), 1)
    pieces = [t[:, c:c + 128] for c in range(0, t.shape[1], 128)]
    return jnp.concatenate([jnp.where((lane & 63) < 32, pltpu.roll(h, 96, 1), pltpu.roll(h, 32, 1))
                            for h in pieces], axis=1)


def _dil_spec(dil, tm):
    return pl.BlockSpec((dil, tm // dil, 256), lambda i: (0, i, 0))


def _dil_scratch(tm):
    return pltpu.VMEM((2, tm, 128), F32)


def _load_token_order(src, scr, dil, tm):
    if dil == 1:
        return src[0]
    for j in range(dil):
        for c in range(2):
            scr[c, pl.ds(j, tm // dil, stride=dil), :] = src[j, :, c * 128:(c + 1) * 128]
    return jnp.concatenate([scr[0], scr[1]], axis=1)


def _store_dil_order(val, dst, scr, dil, tm):
    if dil == 1:
        dst[0] = val.astype(dst.dtype)
        return
    for c in range(2):
        scr[c] = val[:, c * 128:(c + 1) * 128]
    for j in range(dil):
        for c in range(2):
            dst[j, :, c * 128:(c + 1) * 128] = scr[c, pl.ds(j, tm // dil, stride=dil), :].astype(dst.dtype)


def _split_proj(proj, cos_t, sin_t, *, tm, name):
    n = proj.shape[0]
    n_dil = len(DIL_DILATIONS)

    def body(*refs):
        na_in = refs[0:3]
        dil_in = refs[3:3 + 3 * n_dil]
        cos_ref, sin_ref = refs[12:14]
        outs = refs[14:]
        na_out = outs[0:3]
        dil_out = outs[3:12]
        scr = outs[12]
        for t in range(3):
            na_out[t][...] = na_in[t][...].astype(BF16)
        cosv, sinv = cos_ref[...], sin_ref[...]
        for t in range(3):
            for gi, dil in enumerate(DIL_DILATIONS):
                val = dil_in[t * n_dil + gi][...]
                if t < 2:
                    val = val * cosv + _swap_halves(val) * sinv
                _store_dil_order(val, dil_out[t * n_dil + gi], scr, dil, tm)

    in_specs = [_rows(tm, NA_WIDTH, c) for c in range(3)]
    in_specs += [_rows(tm, 256, 6 + c) for c in range(9)]
    in_specs += [_rows(tm, 256), _rows(tm, 256)]
    out_specs = [_rows(tm, NA_WIDTH)] * 3
    out_shape = [_sds((n, NA_WIDTH), BF16)] * 3
    for _ in range(3):
        for dil in DIL_DILATIONS:
            out_specs.append(pl.BlockSpec((dil, tm // dil, 256), lambda i: (0, i, 0)))
            out_shape.append(_sds((dil, n // dil, 256), BF16))
    res = pl.pallas_call(
        body, name=name, grid=(n // tm,),
        in_specs=in_specs, out_specs=out_specs, out_shape=out_shape,
        scratch_shapes=[_dil_scratch(tm)],
        compiler_params=_params("parallel"),
    )(*([proj] * 12), cos_t, sin_t)
    return res[0:3], res[3:6], res[6:9], res[9:12]


def _residual_rms_tile(delta, h, g):
    hn = h + delta
    return hn, hn * _rstd(hn) * g


def _gate_bwd_tile(dm, s1, b1, s2, b2):
    return dm * s1, dm * s2, dm * b1 * s1 * (1.0 - s1), dm * b2 * s2 * (1.0 - s2)


def _tail_tile(gt, pp, h2, target, g):
    sg = _sigmoid(gt)
    h3 = h2 + sg * pp
    r3 = _rstd(h3)
    n3 = h3 * r3
    err = n3 * g - target
    loss = 0.5 * jnp.sum(jnp.sum(err * err, axis=-1, keepdims=True) / D_MODEL)
    dy = err / D_MODEL
    dn = dy * g
    dh3 = r3 * (dn - n3 * jnp.mean(dn * n3, axis=-1, keepdims=True))
    return (dh3, dh3 * sg, dh3 * pp * sg * (1.0 - sg),
            jnp.sum(dy * n3, axis=0, keepdims=True), jnp.full((1, gt.shape[1]), loss, F32))


def _rms_bwd_tile(dz, h, g, dres):
    r = _rstd(h)
    nrm = h * r
    dn = dz * g
    dh = dres + r * (dn - nrm * jnp.mean(dn * nrm, axis=-1, keepdims=True))
    return dh, jnp.sum(dz * nrm, axis=0, keepdims=True)


def _rms_bwd_twice(dz, h, g, dres):
    dh, dg = _rms_bwd_tile(dz, h, g, dres)
    return dh, dh, dg


def _assemble_dproj(dna, ddil_q, ddil_k, ddil_v, dgn, dgd, cos_t, sin_t, *, tm, name):
    n = dgn.shape[0]

    def body(*refs):
        dq_ref, dk_ref, dv_ref = refs[0:3]
        dil_in = refs[3:12]
        dgn_ref, dgd_ref, cos_ref, sin_ref, o_ref, scr = refs[12:18]
        o_ref[:, 0:512] = dq_ref[...]
        o_ref[:, 512:1024] = dk_ref[...].astype(BF16)
        o_ref[:, 1024:1536] = dv_ref[...].astype(BF16)
        cosv, sinv = cos_ref[...], sin_ref[...]
        for t in range(3):
            for gi, dil in enumerate(DIL_DILATIONS):
                val = _load_token_order(dil_in[t * 3 + gi], scr, dil, tm)
                if t < 2:
                    val = val * cosv + _swap_halves(val * sinv)
                c0 = 1536 + t * DIL_WIDTH + gi * 256
                o_ref[:, c0:c0 + 256] = val.astype(BF16)
        o_ref[:, 3840:4864] = dgn_ref[...]
        o_ref[:, 4864:5888] = dgd_ref[...]

    in_specs = [_rows(tm, NA_WIDTH)] * 3
    for _ in range(3):
        for dil in DIL_DILATIONS:
            in_specs.append(pl.BlockSpec((dil, tm // dil, 256), lambda i: (0, i, 0)))
    in_specs += [_rows(tm, D_MODEL)] * 2 + [_rows(tm, 256)] * 2
    return pl.pallas_call(
        body, name=name, grid=(n // tm,), in_specs=in_specs,
        out_specs=_rows(tm, IN_WIDTH), out_shape=_sds((n, IN_WIDTH), BF16),
        scratch_shapes=[_dil_scratch(tm)],
        compiler_params=_params("parallel"),
    )(*dna, *ddil_q, *ddil_k, *ddil_v, dgn, dgd, cos_t, sin_t)


N_ROW_OFF = 2 * NA_WIN_ROWS - 1
N_PAIRS = N_ROW_OFF - 1
RB_WIDTH = (N_ROW_OFF + 1) * GRID_W


def _na_bias(rb_ref, pair_scr):
    shape = (GRID_W, RB_WIDTH)
    qc = lax.broadcasted_iota(jnp.int32, shape, 0)
    qc2 = lax.broadcasted_iota(jnp.int32, (GRID_W, 128), 0)
    kc2 = lax.broadcasted_iota(jnp.int32, (GRID_W, 128), 1) & (GRID_W - 1)
    cs = jnp.clip(qc2 - 8, 0, GRID_W - 16)
    valid = (kc2 >= cs) & (kc2 < cs + 16)
    for hh in range(2):
        t = jnp.broadcast_to(rb_ref[hh], shape)
        t = pltpu.roll(t, RB_WIDTH - 15, 1)
        for b in range(6):
            t = jnp.where(((qc >> b) & 1) == 1, pltpu.roll(t, 1 << b, 1), t)
        t_odd = pltpu.roll(t, RB_WIDTH - GRID_W, 1)
        for ro in range(N_PAIRS):
            src = t if ro % 2 == 0 else t_odd
            base = (ro // 2) * 128
            pair_scr[hh, ro] = jnp.where(valid, src[:, base:base + 128], NEG_INF)


NA_GROUP_FWD = 8
NA_GROUP_BWD = 4


def _stack_heads(ref, r, scale=1.0):
    lane = lax.broadcasted_iota(jnp.int32, (GRID_W, 128), 1)
    t = ref[pl.ds(pl.multiple_of(r * GRID_W, GRID_W), GRID_W), :].astype(F32) * scale
    return jnp.concatenate([jnp.where(lane < 64, t, 0.0), jnp.where(lane >= 64, t, 0.0)], axis=0).astype(BF16)


def _unstack_heads(t2):
    lane = lax.broadcasted_iota(jnp.int32, (GRID_W, 128), 1)
    return jnp.where(lane < 64, t2[:GRID_W], t2[GRID_W:])


def _na_window(k_ref, v_ref, r, n_rows):
    rs = jnp.clip(r - NA_WIN_ROWS // 2, 0, n_rows - NA_WIN_ROWS)
    ro0 = (NA_WIN_ROWS - 1) - (r - rs)
    off = pl.multiple_of(rs * GRID_W, GRID_W)
    kw = k_ref[pl.ds(off, NA_WIN_ROWS * GRID_W), :]
    vw = v_ref[pl.ds(off, NA_WIN_ROWS * GRID_W), :]
    return kw, vw, off, ro0


def _na_probs(s_raw, pair_scr, ro0):
    bias = [jnp.concatenate([pair_scr[hh, ro0 + 2 * j] for j in range(NA_WIN_ROWS // 2)], axis=1)
            for hh in range(2)]
    s = s_raw + jnp.concatenate(bias, axis=0)
    m = jnp.max(s, axis=-1, keepdims=True)
    e = jnp.exp(s - m)
    return e * (1.0 / jnp.sum(e, axis=-1, keepdims=True))


def _na_fwd(q, k, v, rb, *, name):
    n = q.shape[0]
    n_rows = n // GRID_W

    def body(ins, outs, scr):
        q_ref, k_ref, v_ref, rb_ref = ins
        o_ref, = outs
        pair_scr, = scr
        _na_bias(rb_ref, pair_scr)

        def group(g, carry):
            rows = [g * NA_GROUP_FWD + t for t in range(NA_GROUP_FWD)]
            wins = [_na_window(k_ref, v_ref, r, n_rows) for r in rows]
            raw = [lax.dot_general(_stack_heads(q_ref, r, QK_SCALE), w[0], NT_DIMS, preferred_element_type=F32)
                   for r, w in zip(rows, wins)]
            probs = [_na_probs(s, pair_scr, w[3]) for s, w in zip(raw, wins)]
            outs2 = [jnp.dot(p.astype(BF16), w[1], preferred_element_type=F32) for p, w in zip(probs, wins)]
            for r, o2 in zip(rows, outs2):
                o_ref[pl.ds(pl.multiple_of(r * GRID_W, GRID_W), GRID_W), :] = _unstack_heads(o2).astype(BF16)
            return carry

        lax.fori_loop(0, n_rows // NA_GROUP_FWD, group, 0)

    col = pl.BlockSpec((n, 128), lambda h: (0, h))
    return _call(
        body, name=name, grid=(NA_WIDTH // 128,),
        in_specs=[col, col, col, pl.BlockSpec((2, 1, RB_WIDTH), lambda h: (h, 0, 0))],
        out_specs=[col], out_shape=[_sds((n, NA_WIDTH), BF16)],
        scratch_shapes=[pltpu.VMEM((2, N_PAIRS, GRID_W, 128), F32)],
        args=(q, k, v, rb))[0]


def _na_bwd(q, k, v, do, rb, *, name):
    n = q.shape[0]
    n_rows = n // GRID_W
    win = NA_WIN_ROWS * GRID_W

    def body(ins, outs, scr):
        q_ref, k_ref, v_ref, do_ref, rb_ref = ins
        dq_ref, dk_ref, dv_ref, drb_ref = outs
        pair_scr, acc_scr = scr
        _na_bias(rb_ref, pair_scr)
        acc_scr[...] = jnp.zeros_like(acc_scr)
        dk_ref[...] = jnp.zeros_like(dk_ref)
        dv_ref[...] = jnp.zeros_like(dv_ref)

        def group(g, carry):
            rows = [g * NA_GROUP_BWD + t for t in range(NA_GROUP_BWD)]
            wins = [_na_window(k_ref, v_ref, r, n_rows) for r in rows]
            qss = [_stack_heads(q_ref, r, QK_SCALE) for r in rows]
            doss = [_stack_heads(do_ref, r) for r in rows]
            raw = [lax.dot_general(qs, w[0], NT_DIMS, preferred_element_type=F32) for qs, w in zip(qss, wins)]
            dps = [lax.dot_general(dos, w[1], NT_DIMS, preferred_element_type=F32) for dos, w in zip(doss, wins)]
            probs = [_na_probs(s, pair_scr, w[3]) for s, w in zip(raw, wins)]
            dss = [p * (dp - jnp.sum(p * dp, axis=-1, keepdims=True)) for p, dp in zip(probs, dps)]
            dsbs = [ds.astype(BF16) for ds in dss]
            dq2s = [jnp.dot(dsb, w[0], preferred_element_type=F32) for dsb, w in zip(dsbs, wins)]
            dkws = [lax.dot_general(dsb, qs, TN_DIMS, preferred_element_type=F32) for dsb, qs in zip(dsbs, qss)]
            dvws = [lax.dot_general(p.astype(BF16), dos, TN_DIMS, preferred_element_type=F32)
                    for p, dos in zip(probs, doss)]
            for t, r in enumerate(rows):
                _, _, off, ro0 = wins[t]
                for hh in range(2):
                    for j in range(NA_WIN_ROWS // 2):
                        acc_scr[hh, ro0 + 2 * j] += dss[t][hh * GRID_W:(hh + 1) * GRID_W, j * 128:(j + 1) * 128]
                dq_ref[pl.ds(pl.multiple_of(r * GRID_W, GRID_W), GRID_W), :] = (
                    _unstack_heads(dq2s[t]) * QK_SCALE).astype(BF16)
                dk_ref[pl.ds(off, win), :] += dkws[t]
                dv_ref[pl.ds(off, win), :] += dvws[t]
            return carry

        lax.fori_loop(0, n_rows // NA_GROUP_BWD, group, 0)

        qc = lax.broadcasted_iota(jnp.int32, (N_PAIRS * GRID_W, 128), 0)
        for hh in range(2):
            t = acc_scr[hh].reshape(N_PAIRS * GRID_W, 128)
            for b in range(6):
                t = jnp.where(((qc >> b) & 1) == 1, pltpu.roll(t, 128 - (1 << b), 1), t)
            t = pltpu.roll(t, 15, 1)
            drb_ref[hh] = jnp.sum(t.reshape(N_PAIRS, GRID_W, 128), axis=1)

    col = pl.BlockSpec((n, 128), lambda h: (0, h))
    return _call(
        body, name=name, grid=(NA_WIDTH // 128,),
        in_specs=[col, col, col, col, pl.BlockSpec((2, 1, RB_WIDTH), lambda h: (h, 0, 0))],
        out_specs=[col, col, col, pl.BlockSpec((2, N_PAIRS, 128), lambda h: (h, 0, 0))],
        out_shape=[_sds((n, NA_WIDTH), BF16), _sds((n, NA_WIDTH), F32), _sds((n, NA_WIDTH), F32),
                   _sds((8, N_PAIRS, 128), F32)],
        scratch_shapes=[pltpu.VMEM((2, N_PAIRS, GRID_W, 128), F32),
                        pltpu.VMEM((2, N_PAIRS, GRID_W, 128), F32)],
        args=(q, k, v, do, rb))


def _rpb_table(rpb2):
    t = jnp.pad(rpb2, ((0, 0), (0, 1), (0, GRID_W - rpb2.shape[-1])))
    return t.reshape(8, 1, RB_WIDTH)


def _rpb_grad(drb, *, name):
    kdim = drb.shape[1]

    def body(x_ref, o_ref):
        kk = lax.broadcasted_iota(jnp.int32, (128, 512), 0)
        jj = lax.broadcasted_iota(jnp.int32, (128, 512), 1)
        half, co = kk >> 6, kk & 63
        acc = jnp.zeros((8, 512), F32)
        for ro in range(N_PAIRS):
            hit = ((ro + half) == (jj >> 5)) & (co == (jj & 31)) & (co < 31)
            onehot = jnp.where(hit, 1.0, 0.0).astype(F32)
            acc = acc + jnp.dot(x_ref[:, ro * 128:(ro + 1) * 128], onehot, preferred_element_type=F32,
                                precision=lax.Precision.HIGHEST)
        o_ref[...] = acc

    return pl.pallas_call(
        body, name=name, grid=(1,),
        in_specs=[_const((8, kdim))], out_specs=_const((8, 512)), out_shape=_sds((8, 512), F32),
        compiler_params=_params("arbitrary"),
    )(drb)


DIL_GROUP = 2


def _dil_blocks(length):
    qb = min(128, length)
    return qb, min(qb + 2 * DIL_RADIUS, length), min(DIL_GROUP, length // qb)


def _stack_lanes(ref, t, qb, scale=1.0):
    lane = lax.broadcasted_iota(jnp.int32, (qb, 256), 1)
    val = ref[0, t * qb:(t + 1) * qb, :].astype(F32) * scale
    return jnp.concatenate([jnp.where((lane >> 6) == h, val, 0.0) for h in range(4)], axis=0).astype(BF16)


def _dil_window(k_ref, v_ref, blk, qb, win, length):
    start = pl.multiple_of(jnp.clip(blk * qb - DIL_RADIUS, 0, length - win), DIL_RADIUS)
    return k_ref[0, pl.ds(start, win), :], v_ref[0, pl.ds(start, win), :], start


def _dil_mask(s, blk, start, qb, win):
    gap = ((lax.broadcasted_iota(jnp.int32, (4 * qb, win), 0) & (qb - 1))
           - lax.broadcasted_iota(jnp.int32, (4 * qb, win), 1)) + (blk * qb - start)
    return jnp.where(jnp.abs(gap) <= DIL_RADIUS, s, NEG_INF)


def _pick_heads(stacked, qb):
    lane = lax.broadcasted_iota(jnp.int32, (qb, 256), 1)
    out = jnp.zeros((qb, 256), stacked.dtype)
    for h in range(4):
        out = jnp.where((lane >> 6) == h, stacked[h * qb:(h + 1) * qb], out)
    return out


def _stack_head_cols(ref, t, qb):
    return jnp.concatenate([ref[0, t * qb:(t + 1) * qb, 64 * h:64 * h + 1] for h in range(4)], axis=0)


def _dil_fwd(q, k, v, *, name, after=None):
    dil, length, _ = q.shape
    qb, win, grp = _dil_blocks(length)
    extra = [] if after is None else [after]

    def body(q_ref, k_ref, v_ref, *rest):
        o_ref, lse_ref = rest[-2:]
        blks = [pl.program_id(1) * grp + t for t in range(grp)]
        wins = [_dil_window(k_ref, v_ref, b, qb, win, length) for b in blks]
        raw = [lax.dot_general(_stack_lanes(q_ref, t, qb, QK_SCALE), w[0], NT_DIMS, preferred_element_type=F32)
               for t, w in enumerate(wins)]
        lses, outs = [], []
        for t, (s, w) in enumerate(zip(raw, wins)):
            s = _dil_mask(s, blks[t], w[2], qb, win)
            m = jnp.max(s, axis=-1, keepdims=True)
            e = jnp.exp(s - m)
            norm = jnp.sum(e, axis=-1, keepdims=True)
            lses.append(m + jnp.log(norm))
            outs.append(jnp.dot((e * (1.0 / norm)).astype(BF16), w[1], preferred_element_type=F32))
        for t in range(grp):
            o_ref[0, t * qb:(t + 1) * qb, :] = _pick_heads(outs[t], qb)
            lse_ref[0, t * qb:(t + 1) * qb, :] = _pick_heads(jnp.broadcast_to(lses[t], (4 * qb, 256)), qb)

    seq = pl.BlockSpec((1, length, 256), lambda j, i: (j, 0, 0))
    blk = pl.BlockSpec((1, grp * qb, 256), lambda j, i: (j, i, 0))
    return pl.pallas_call(
        body, name=name, grid=(dil, length // (grp * qb)),
        in_specs=[blk, seq, seq] + [pl.BlockSpec(memory_space=pl.ANY)] * len(extra), out_specs=[blk, blk],
        out_shape=[_sds((dil, length, 256), F32)] * 2,
        compiler_params=_params("parallel", "parallel"),
    )(q, k, v, *extra)


def _dil_bwd(q, k, v, do, lse, cc, *, name):
    dil, length, _ = q.shape
    qb, win, grp = _dil_blocks(length)

    def body(q_ref, k_ref, v_ref, do_ref, lse_ref, cc_ref, dq_ref, dk_ref, dv_ref):
        @pl.when(pl.program_id(1) == 0)
        def _():
            dk_ref[...] = jnp.zeros_like(dk_ref)
            dv_ref[...] = jnp.zeros_like(dv_ref)

        blks = [pl.program_id(1) * grp + t for t in range(grp)]
        wins = [_dil_window(k_ref, v_ref, b, qb, win, length) for b in blks]
        qss = [_stack_lanes(q_ref, t, qb, QK_SCALE) for t in range(grp)]
        doss = [_stack_lanes(do_ref, t, qb) for t in range(grp)]
        raw = [lax.dot_general(qs, w[0], NT_DIMS, preferred_element_type=F32) for qs, w in zip(qss, wins)]
        dps = [lax.dot_general(dos, w[1], NT_DIMS, preferred_element_type=F32) for dos, w in zip(doss, wins)]
        probs = [jnp.exp(_dil_mask(s, blks[t], wins[t][2], qb, win) - _stack_head_cols(lse_ref, t, qb))
                 for t, s in enumerate(raw)]
        dsbs = [(p * (dp + _stack_head_cols(cc_ref, t, qb))).astype(BF16)
                for t, (p, dp) in enumerate(zip(probs, dps))]
        dq4s = [jnp.dot(dsb, w[0], preferred_element_type=F32) for dsb, w in zip(dsbs, wins)]
        dkws = [lax.dot_general(dsb, qs, TN_DIMS, preferred_element_type=F32) for dsb, qs in zip(dsbs, qss)]
        dvws = [lax.dot_general(p.astype(BF16), dos, TN_DIMS, preferred_element_type=F32)
                for p, dos in zip(probs, doss)]
        for t in range(grp):
            dq_ref[0, t * qb:(t + 1) * qb, :] = _pick_heads(dq4s[t], qb) * QK_SCALE
            dk_ref[0, pl.ds(wins[t][2], win), :] += dkws[t]
            dv_ref[0, pl.ds(wins[t][2], win), :] += dvws[t]

    seq = pl.BlockSpec((1, length, 256), lambda j, i: (j, 0, 0))
    blk = pl.BlockSpec((1, grp * qb, 256), lambda j, i: (j, i, 0))
    return pl.pallas_call(
        body, name=name, grid=(dil, length // (grp * qb)),
        in_specs=[blk, seq, seq, blk, blk, blk], out_specs=[blk, seq, seq],
        out_shape=[_sds((dil, length, 256), F32)] * 3,
        compiler_params=_params("parallel", "arbitrary"),
    )(q, k, v, do, lse, cc)


def _merge_weights(lses):
    m = jnp.maximum(jnp.maximum(lses[0], lses[1]), lses[2])
    es = [jnp.exp(t - m) for t in lses]
    inv = 1.0 / (es[0] + es[1] + es[2])
    return [e * inv for e in es]


def _dil_merge(outs, lses, *, tm, name):
    n = outs[0].shape[1]

    def body(*refs):
        o_in, l_in = refs[0:3], refs[3:6]
        y_ref, yb_ref, scr = refs[6:9]
        lv = [_load_token_order(l_in[g], scr, d, tm) for g, d in enumerate(DIL_DILATIONS)]
        ws = _merge_weights(lv)
        y = jnp.zeros((tm, 256), F32)
        for g, d in enumerate(DIL_DILATIONS):
            y = y + ws[g] * _load_token_order(o_in[g], scr, d, tm)
        y_ref[...] = y
        yb_ref[...] = y.astype(BF16)

    specs = [_dil_spec(d, tm) for d in DIL_DILATIONS]
    return pl.pallas_call(
        body, name=name, grid=(n // tm,), in_specs=specs + specs,
        out_specs=[_rows(tm, 256)] * 2, out_shape=[_sds((n, 256), F32), _sds((n, 256), BF16)],
        scratch_shapes=[_dil_scratch(tm)],
        compiler_params=_params("parallel"),
    )(*outs, *lses)


def _dil_merge_bwd(dy, y, lses, *, tm, name):
    n = dy.shape[0]

    def body(*refs):
        dy_ref, y_ref = refs[0:2]
        l_in = refs[2:5]
        do_out, cc_out = refs[5:8], refs[8:11]
        scr = refs[11]
        lv = [_load_token_order(l_in[g], scr, d, tm) for g, d in enumerate(DIL_DILATIONS)]
        ws = _merge_weights(lv)
        dyv = dy_ref[...]
        rr = lax.broadcasted_iota(jnp.int32, (256, 256), 0) >> 6
        cc = lax.broadcasted_iota(jnp.int32, (256, 256), 1) >> 6
        ones = jnp.where(rr == cc, 1.0, 0.0).astype(F32)
        tsum = jnp.dot(dyv * y_ref[...], ones, preferred_element_type=F32,
                       precision=lax.Precision.HIGHEST)
        for g, d in enumerate(DIL_DILATIONS):
            _store_dil_order(ws[g] * dyv, do_out[g], scr, d, tm)
            _store_dil_order(-ws[g] * tsum, cc_out[g], scr, d, tm)

    specs = [_dil_spec(d, tm) for d in DIL_DILATIONS]
    res = pl.pallas_call(
        body, name=name, grid=(n // tm,),
        in_specs=[_rows(tm, 256)] * 2 + specs,
        out_specs=specs + specs,
        out_shape=[_sds((d, n // d, 256), BF16) for d in DIL_DILATIONS]
                  + [_sds((d, n // d, 256), F32) for d in DIL_DILATIONS],
        scratch_shapes=[_dil_scratch(tm)],
        compiler_params=_params("parallel"),
    )(dy, y, *lses)
    return res[0:3], res[3:6]


_WEIGHTS = (("w_in", 1, 736), ("w_branch_na", 1, 128), ("w_branch_dil", 1, 128), ("w_out", 0, 128),
            ("w_up", 1, 512), ("w_down", 0, 512), ("w_ple_gate", 0, 128), ("w_ple_proj", 1, 128))
_W_IN, _W_BNA, _W_BD, _W_OUT, _W_UP, _W_DOWN, _W_PG, _W_PP = range(8)


def _to_full(gathered):
    return gathered.reshape(-1, gathered.shape[2])


def _to_chunks(widx, mat):
    return mat.reshape(N_DEV, _WEIGHTS[widx][2], mat.shape[1])


def _local_step(x, p_bf16, positions, target, g_mix, g_mlp, g_ple, g_final, rpb2,
                get_w_in, relay_rest, get_rest, send_grads):
    tm = 256
    half = HEAD_DIM // 2
    inv_freq = 10000.0 ** (-jnp.arange(half, dtype=F32) / half)
    ang = positions.astype(F32)[:, None] * inv_freq
    cos, sin = jnp.cos(ang), jnp.sin(ang)
    cos_t = jnp.tile(jnp.concatenate([cos, cos], axis=-1), (1, 4))
    sin_t = jnp.tile(jnp.concatenate([-sin, sin], axis=-1), (1, 4))
    rb = _rpb_table(rpb2)

    a = _rms_fwd(x, g_mix, tm=tm, name="rms_mix")
    w_in, token = get_w_in(a)
    qkv_width = 3 * NA_WIDTH + 3 * DIL_WIDTH
    proj = _matmul(a, w_in, tb=True, n_limit=qkv_width, out_dtype=F32, tm=512, tn=qkv_width // 2, tk=1024,
                   name="mm_in_qkv", after=token)
    gates = _matmul(a, w_in[qkv_width:], tb=True, stack_cols=True, out_dtype=F32, tm=512, tn=D_MODEL, tk=1024,
                    name="mm_in_gates", epilogue=lambda acc: (_sigmoid(acc),))
    sn, sd = (gates, 0), (gates, 1)
    na_qkv, dq_g, dk_g, dv_g = _split_proj(proj, cos_t, sin_t, tm=tm, name="split_proj")
    y_na = _na_fwd(*na_qkv, rb, name="na_fwd")
    token = relay_rest(y_na)
    d_out, d_lse = [], []
    for g in range(3):
        o, lse = _dil_fwd(dq_g[g], dk_g[g], dv_g[g], name=f"dil_fwd{g}", after=token if g == 0 else None)
        d_out.append(o)
        d_lse.append(lse)
    y_dil, y_dil_b = _dil_merge(d_out, d_lse, tm=tm, name="dil_merge")
    w_bna, w_bd, w_out, w_up, w_down, w_pg, w_pp = get_rest(y_dil_b)
    bn = _matmul(y_na, w_bna, tb=True, out_dtype=F32, tm=512, tn=1024, tk=512, name="mm_bna")
    bd, mixed = _matmul(y_dil_b, w_bd, tb=True, out_dtype=(F32, BF16), tm=512, tn=1024, tk=256, name="mm_bd",
                        extra=(sn, bn, sd), epilogue=lambda acc, s1, b1, s2: (acc, s1 * b1 + s2 * acc))
    h1, c = _matmul(mixed, w_out, out_dtype=(F32, BF16), tm=512, tn=1024, tk=1024, name="mm_out",
                    extra=(x, g_mlp), epilogue=_residual_rms_tile)
    u, f = _matmul(c, w_up, tb=True, out_dtype=(F32, BF16), tm=512, tn=2048, tk=1024, name="mm_up",
                   epilogue=lambda acc: (acc, jnp.square(jnp.maximum(acc, 0.0))))
    h2, e = _matmul(f, w_down, out_dtype=(F32, BF16), tm=512, tn=1024, tk=4096, name="mm_down",
                    extra=(h1, g_ple), epilogue=_residual_rms_tile)
    pp = _matmul(p_bf16, w_pp, tb=True, out_dtype=F32, tm=512, tn=1024, tk=256, name="mm_pp")

    dh3, dpp, dgt, dg_final, loss = _matmul(
        e, w_pg, out_dtype=(F32, BF16, BF16), tm=512, tn=1024, tk=1024, name="mm_pg_tail",
        extra=(pp, h2, target, g_final), epilogue=_tail_tile, n_colsum=2)
    loss = loss[:, :128]
    gw_pp = _matmul(p_bf16, dpp, ta=True, transpose_out=True, out_dtype=BF16, tm=256, tn=1024, tk=2048,
                    name="mm_gw_pp")
    gw_pg = _matmul(e, dgt, ta=True, out_dtype=BF16, tm=512, tn=1024, tk=2048, name="mm_gw_pg")
    dh2, dh2_b, dg_ple = _matmul(
        dgt, w_pg, tb=True, out_dtype=(F32, BF16), tm=512, tn=1024, tk=1024, name="mm_de",
        extra=(h2, g_ple, dh3), epilogue=_rms_bwd_twice, n_colsum=1)
    du = _matmul(dh2_b, w_down, tb=True, out_dtype=BF16, tm=512, tn=2048, tk=1024, name="mm_du",
                 extra=(u,), epilogue=lambda acc, uv: (acc * (2.0 * jnp.maximum(uv, 0.0)),))
    gw_down = _matmul(f, dh2_b, ta=True, out_dtype=BF16, tm=1024, tn=1024, tk=2048, name="mm_gw_down")
    gw_up = _matmul(c, du, ta=True, transpose_out=True, out_dtype=BF16, tm=512, tn=2048, tk=2048, name="mm_gw_up")
    dh1, dh1_b, dg_mlp = _matmul(
        du, w_up, out_dtype=(F32, BF16), tm=512, tn=1024, tk=4096, name="mm_dc",
        extra=(h1, g_mlp, dh2), epilogue=_rms_bwd_twice, n_colsum=1)
    dbn, dbd, dgn, dgd = _matmul(dh1_b, w_out, tb=True, out_dtype=(BF16,) * 4, tm=512, tn=1024, tk=1024,
                                 name="mm_dmixed", extra=(sn, bn, sd, bd), epilogue=_gate_bwd_tile)
    gw_out = _matmul(mixed, dh1_b, ta=True, out_dtype=BF16, tm=512, tn=1024, tk=2048, name="mm_gw_out")
    gw_bna = _matmul(y_na, dbn, ta=True, transpose_out=True, out_dtype=BF16, tm=512, tn=1024, tk=2048,
                     name="mm_gw_bna")
    dy_na = _matmul(dbn, w_bna, out_dtype=BF16, tm=512, tn=512, tk=1024, name="mm_dy_na")
    gw_bd = _matmul(y_dil_b, dbd, ta=True, transpose_out=True, out_dtype=BF16, tm=256, tn=1024, tk=2048,
                    name="mm_gw_bd")
    token = send_grads((_W_PP, _W_PG, _W_DOWN, _W_UP, _W_OUT, _W_BNA, _W_BD),
                       (gw_pp, gw_pg, gw_down, gw_up, gw_out, gw_bna, gw_bd))
    dy_dil = _matmul(dbd, w_bd, out_dtype=F32, tm=512, tn=256, tk=1024, name="mm_dy_dil", after=token)
    dna = _na_bwd(*na_qkv, dy_na, rb, name="na_bwd")
    drpb = _rpb_grad(dna[3].reshape(8, -1), name="rpb_grad")
    do_g, cc_g = _dil_merge_bwd(dy_dil, y_dil, d_lse, tm=tm, name="dil_merge_bwd")
    ddq, ddk, ddv = [], [], []
    for g in range(3):
        r = _dil_bwd(dq_g[g], dk_g[g], dv_g[g], do_g[g], d_lse[g], cc_g[g], name=f"dil_bwd{g}")
        ddq.append(r[0])
        ddk.append(r[1])
        ddv.append(r[2])
    dproj = _assemble_dproj(dna[0:3], ddq, ddk, ddv, dgn, dgd, cos_t, sin_t, tm=tm, name="assemble_dproj")
    gw_in = _matmul(a, dproj, ta=True, transpose_out=True, out_dtype=BF16, tm=512, tn=2944, tk=2048, name="mm_gw_in")
    token = send_grads((_W_IN,), (gw_in,))
    dx, dg_mix = _matmul(
        dproj, w_in, out_dtype=(F32,), tm=512, tn=1024, tk=5888, name="mm_da", after=token,
        extra=(x, g_mix, dh1), epilogue=_rms_bwd_tile, n_colsum=1)
    return loss, dx, (dg_mix, dg_mlp, dg_ple, dg_final), drpb


def _cast_bf16(t, *, name):
    def body(t_ref, o_ref):
        o_ref[...] = t_ref[...].astype(BF16)

    rows, cols = t.shape
    tr = 256 if rows % 256 == 0 else rows
    blk = pl.BlockSpec((tr, cols), lambda i: (i, 0))
    return pl.pallas_call(body, name=name, grid=(rows // tr,), in_specs=[blk], out_specs=blk,
                          out_shape=_sds(t.shape, BF16), compiler_params=_params("parallel"))(t)


def _adamw(w, g, m, v):
    m = ADAM_B1 * m + (1.0 - ADAM_B1) * g
    v = ADAM_B2 * v + (1.0 - ADAM_B2) * (g * g)
    m_hat = m / (1.0 - ADAM_B1 ** ADAM_STEP)
    v_hat = v / (1.0 - ADAM_B2 ** ADAM_STEP)
    delta = -ADAM_LR * (m_hat / (jnp.sqrt(v_hat) + ADAM_EPS) + ADAM_WD * w)
    return delta, m, v


def _sum_adamw(parts, w, m, v, *, tr, name, own=None, transposed=False):
    rows, cols = w.shape
    n_pre = 0 if own is None else 1

    def body(*refs):
        p_ref, w_ref, m_ref, v_ref = refs[n_pre:n_pre + 4]
        g_ref, d_ref, nm_ref, nv_ref = refs[-4:]
        g = (p_ref[0] if own is None else refs[n_pre + 4][...]).astype(F32)
        for s in range(1, N_DEV):
            g = g + p_ref[s].astype(F32)
        if transposed:
            g = g.T
        g_ref[...] = g
        d_ref[...], nm_ref[...], nv_ref[...] = _adamw(w_ref[...], g, m_ref[...], v_ref[...])

    if transposed:
        blk = pl.BlockSpec((rows, tr), lambda i, *_: (0, i))
        g_rows, steps = rows, cols // tr
    else:
        blk = pl.BlockSpec((tr, cols), lambda i, *_: (i, 0))
        g_rows, steps = cols, rows // tr
    in_specs = [pl.BlockSpec((N_DEV, tr, g_rows), lambda i, *_: (0, i, 0)), blk, blk, blk]
    args = [parts, w, m, v]
    if own is not None:
        in_specs.append(pl.BlockSpec((None, tr, g_rows), lambda i, idx: (idx[0], i, 0)))
        args = [own[1]] + args + [own[0]]
    return pl.pallas_call(
        body, name=name,
        grid_spec=pltpu.PrefetchScalarGridSpec(num_scalar_prefetch=n_pre, grid=(steps,), in_specs=in_specs,
                                               out_specs=[blk] * 4),
        out_shape=[_sds((rows, cols), F32)] * 4,
        compiler_params=_params("parallel"),
    )(*args)


_RPB_SIZE = 8 * 15 * 31


def _pack_small(g_mix, g_mlp, g_ple, g_final, rpb, loss_row):
    flat = jnp.concatenate([g_mix.reshape(-1), g_mlp.reshape(-1), g_ple.reshape(-1), g_final.reshape(-1),
                            rpb.reshape(-1), jnp.zeros((3840 - _RPB_SIZE,), F32), loss_row.reshape(-1),
                            jnp.zeros((128,), F32)])
    return flat.reshape(64, 128)


def _unpack_small(t):
    flat = t.reshape(-1)
    return (flat[0:1024].reshape(1, 1024), flat[4096:4096 + _RPB_SIZE].reshape(1, 8, 15, 31),
            flat[1024:2048].reshape(1, 1024), flat[2048:3072].reshape(1, 1024), flat[3072:4096])


def kernel(x, p, positions, g_mix, w_in, rpb, w_branch_na, w_branch_dil, w_out, g_mlp, w_up, w_down, g_ple, w_ple_gate, w_ple_proj, g_final, loss_target, m_g_mix, m_w_in, m_rpb, m_w_branch_na, m_w_branch_dil, m_w_out, m_g_mlp, m_w_up, m_w_down, m_g_ple, m_w_ple_gate, m_w_ple_proj, m_g_final, v_g_mix, v_w_in, v_rpb, v_w_branch_na, v_w_branch_dil, v_w_out, v_g_mlp, v_w_up, v_w_down, v_g_ple, v_w_ple_gate, v_w_ple_proj, v_g_final):
    sharded = dict(w_in=(w_in, m_w_in, v_w_in), w_branch_na=(w_branch_na, m_w_branch_na, v_w_branch_na),
                   w_branch_dil=(w_branch_dil, m_w_branch_dil, v_w_branch_dil), w_out=(w_out, m_w_out, v_w_out),
                   w_up=(w_up, m_w_up, v_w_up), w_down=(w_down, m_w_down, v_w_down),
                   w_ple_gate=(w_ple_gate, m_w_ple_gate, v_w_ple_gate),
                   w_ple_proj=(w_ple_proj, m_w_ple_proj, v_w_ple_proj))
    shards = {k: tuple(t[0] for t in val) for k, val in sharded.items()}

    me = _my_index()

    shards["w_in"] = tuple(t.T for t in shards["w_in"])

    w_in_b = _cast_bf16(shards["w_in"][0], name="cast_w_in")
    rest_b = [shards[name][0].astype(BF16).T if axis == 1 else shards[name][0].astype(BF16)
              for name, axis, _ in _WEIGHTS[1:]]
    first_in, token_in = _start_copies(_first_leg_copies, [w_in_b], [_sds((N_DEV,) + w_in_b.shape, BF16)], 4,
                                       name="start_gather_w_in")

    def whole(landed, mine):
        return _to_full(lax.dynamic_update_index_in_dim(landed, mine, me, 0))

    rest = {}

    def get_w_in(after):
        (mine,), landed = _wait_copies(_first_leg_copies, first_in, after, name="wait_gather_w_in")
        second, token = _start_copies(_second_leg_copies, [], landed, 3, name="start_forward_w_in")
        _, (landed,) = _wait_copies(_second_leg_copies, second, token, name="wait_forward_w_in")
        rest["first"], token = _start_copies(_first_leg_copies, rest_b,
                                             [_sds((N_DEV,) + t.shape, BF16) for t in rest_b], 4 * len(rest_b),
                                             name="start_gather_rest", after=landed)
        return whole(landed, mine), token

    def relay_rest(after):
        rest["mine"], landed = _wait_copies(_first_leg_copies, rest["first"], after, name="wait_gather_rest")
        rest["second"], token = _start_copies(_second_leg_copies, [], landed, 3 * len(rest_b),
                                              name="start_forward_rest")
        return token

    def get_rest(after):
        _, landed = _wait_copies(_second_leg_copies, rest["second"], after, name="wait_forward_rest")
        return [whole(t, own) for t, own in zip(landed, rest["mine"])]

    sent = []

    def send_grads(indices, grads):
        chunked = [_to_chunks(i, g) for i, g in zip(indices, grads)]
        handle, token = _start_copies(_exchange_copies, chunked, [_sds(t.shape, BF16) for t in chunked],
                                      7 * len(chunked),
                                      name="start_exchange_" + ("w_in" if indices == (_W_IN,) else "rest"))
        sent.append((indices, handle))
        return token

    g_mix_0 = g_mix + token_in[0:1, 0:1]
    loss, dx, dgs, drpb = _local_step(
        x[0], p[0, 0].astype(BF16), positions[0], loss_target[0],
        g_mix_0, g_mlp, g_ple, g_final.reshape(1, -1), rpb[0], get_w_in, relay_rest, get_rest, send_grads)

    drpb3 = drpb.reshape(8, 16, 32)[:, :15, :31]
    small = _pack_small(dgs[0], dgs[1], dgs[2], dgs[3], drpb3, loss)
    share, done = _start_copies(_gather_copies, [small], [_sds((N_DEV,) + small.shape, F32)], 7,
                                name="start_share_small")

    out = {}
    for indices, handle in sent:
        chunked, landed = _wait_copies(_exchange_copies, handle, done,
                                       name="wait_exchange_" + ("w_in" if indices == (_W_IN,) else "rest"))
        for i, part, mine in zip(indices, landed, chunked):
            name = _WEIGHTS[i][0]
            w, m, v = shards[name]
            turned = _WEIGHTS[i][1] == 1 and i != _W_IN
            res = _sum_adamw(part, w, m, v, tr=368 if i == _W_IN else 128, name="adamw_" + name,
                             own=(mine, me.reshape(1).astype(jnp.int32)), transposed=turned)
            out[name] = [(t.T if i == _W_IN else t)[None] for t in res]
            done = res[0]
    (small,), (small_landed,) = _wait_copies(_gather_copies, share, done, name="wait_share_small")
    small_all = lax.dynamic_update_index_in_dim(small_landed, small, me, 0)
    small_w = _pack_small(g_mix, g_mlp, g_ple, g_final, rpb, jnp.zeros((128,), F32))
    small_m = _pack_small(m_g_mix, m_g_mlp, m_g_ple, m_g_final, m_rpb, jnp.zeros((128,), F32))
    small_v = _pack_small(v_g_mix, v_g_mlp, v_g_ple, v_g_final, v_rpb, jnp.zeros((128,), F32))
    res = _sum_adamw(small_all, small_w, small_m, small_v, tr=64, name="adamw_small")
    unpacked = [_unpack_small(t) for t in res]
    for i, name in enumerate(("g_mix", "rpb", "g_mlp", "g_ple", "g_final")):
        out[name] = [u[i] for u in unpacked]
    loss_total = res[0][62, 0]

    order = ("g_mix", "w_in", "rpb", "w_branch_na", "w_branch_dil", "w_out", "g_mlp", "w_up", "w_down",
             "g_ple", "w_ple_gate", "w_ple_proj", "g_final")
    grads = [out[k][0] for k in order]
    deltas = [out[k][1] for k in order]
    new_m = [out[k][2] for k in order]
    new_v = [out[k][3] for k in order]
    return (loss_total, dx[None], *grads, *deltas, *new_m, *new_v)
```

```python
import jax
import jax.numpy as jnp
from jax import lax
from jax.experimental import pallas as pl
from jax.experimental.pallas import tpu as pltpu

F32 = jnp.float32
BF16 = jnp.bfloat16

D_MODEL = 1024
HEAD_DIM = 64
GRID_W = 64
NA_WIDTH = 512
DIL_WIDTH = 768
IN_WIDTH = 5888
DIL_DILATIONS = (1, 4, 16)
DIL_RADIUS = 64
NA_WIN_ROWS = 8
RMS_EPS = 1e-6
NEG_INF = -1e30
QK_SCALE = HEAD_DIM ** -0.5

ADAM_LR = 0.001
ADAM_B1 = 0.9
ADAM_B2 = 0.999
ADAM_EPS = 1e-08
ADAM_WD = 0.01
ADAM_STEP = 10

N_DEV = 8
VMEM_LIMIT = 56 * 1024 * 1024
EPILOGUE_ROWS = 256
MESH = pl.DeviceIdType.MESH

NT_DIMS = (((1,), (1,)), ((), ()))
TN_DIMS = (((0,), (0,)), ((), ()))


def _sds(shape, dtype):
    return jax.ShapeDtypeStruct(shape, dtype)


def _params(*sem):
    return pltpu.CompilerParams(dimension_semantics=sem, vmem_limit_bytes=VMEM_LIMIT)


def _rows(tm, width, col=0):
    return pl.BlockSpec((tm, width), lambda i, c=col: (i, c))


def _const(shape):
    zeros = (0,) * len(shape)
    return pl.BlockSpec(shape, lambda i: zeros)


def _my_index():
    return 4 * lax.axis_index("x") + 2 * lax.axis_index("y") + lax.axis_index("c")


def _peer(k):
    x, y, c = lax.axis_index("x"), lax.axis_index("y"), lax.axis_index("c")
    px = 1 - x if k & 4 else x
    py = 1 - y if k & 2 else y
    pc = 1 - c if k & 1 else c
    return (px, py, pc), 4 * px + 2 * py + pc


def _call(body, *, name, grid, in_specs, out_specs, out_shape, scratch_shapes, args, after=None):
    n_in, n_out = len(in_specs), len(out_specs)
    extra = [] if after is None else [after]
    n_x = n_in + len(extra)

    def plain(*refs):
        body(refs[:n_in], refs[n_x:n_x + n_out], refs[n_x + n_out:])

    res = pl.pallas_call(plain, name=name, grid=grid,
                         in_specs=list(in_specs) + [pl.BlockSpec(memory_space=pl.ANY)] * len(extra),
                         out_specs=out_specs, out_shape=out_shape, scratch_shapes=scratch_shapes,
                         compiler_params=_params(*(("arbitrary",) * len(grid))))(*args, *extra)
    return list(res)


_HBM_SPEC = pl.BlockSpec(memory_space=pltpu.HBM)
_SEM_SPEC = pl.BlockSpec(memory_space=pltpu.SEMAPHORE)
_SIDE_EFFECT = pltpu.SideEffectType.DATAFLOW_SIDE_EFFECTING


_FIRST_LEG = (1, 2, 4, 6)
_SECOND_LEG = (2, 4, 6)


def _gather_copies(srcs, lands, send, recv, sending):
    me = _my_index()
    out = []
    for w in range(len(srcs)):
        for k in range(1, N_DEV):
            dev, idx = _peer(k)
            out.append(pltpu.make_async_remote_copy(
                src_ref=srcs[w], dst_ref=lands[w].at[me if sending else idx],
                send_sem=send.at[w * 7 + k - 1], recv_sem=recv.at[w * 7 + k - 1],
                device_id=dev, device_id_type=MESH))
    return out


def _first_leg_copies(srcs, lands, send, recv, sending):
    me = _my_index()
    out = []
    for w in range(len(srcs)):
        for j, k in enumerate(_FIRST_LEG):
            dev, idx = _peer(k)
            out.append(pltpu.make_async_remote_copy(
                src_ref=srcs[w], dst_ref=lands[w].at[me if sending else idx],
                send_sem=send.at[w * 4 + j], recv_sem=recv.at[w * 4 + j],
                device_id=dev, device_id_type=MESH))
    return out


def _second_leg_copies(srcs, lands, send, recv, sending):
    sibling, _ = _peer(1)
    out = []
    for w in range(len(lands)):
        for j, k in enumerate(_SECOND_LEG):
            slot = _peer(k if sending else k ^ 1)[1]
            out.append(pltpu.make_async_remote_copy(
                src_ref=lands[w].at[slot], dst_ref=lands[w].at[slot],
                send_sem=send.at[w * 3 + j], recv_sem=recv.at[w * 3 + j],
                device_id=sibling, device_id_type=MESH))
    return out


def _exchange_copies(srcs, lands, send, recv, sending):
    out = []
    for w in range(len(srcs)):
        for k in range(1, N_DEV):
            dev, idx = _peer(k)
            out.append(pltpu.make_async_remote_copy(
                src_ref=srcs[w].at[idx], dst_ref=lands[w].at[k],
                send_sem=send.at[w * 7 + k - 1], recv_sem=recv.at[w * 7 + k - 1],
                device_id=dev, device_id_type=MESH))
    return out


def _start_copies(make, srcs, lands, n_copies, *, name, after=None):
    n_src, n_buf = len(srcs), len(srcs) + len(lands)
    extra = [] if after is None else [after]

    def body(*refs):
        send, recv = refs[n_buf + len(extra)], refs[n_buf + len(extra) + 1]
        for cp in make(refs[:n_src], refs[n_src:n_buf], send, recv, True):
            cp.start()
        refs[-1][...] = jnp.zeros_like(refs[-1])

    bufs = list(srcs) + [lax.empty(t.shape, t.dtype) if isinstance(t, jax.ShapeDtypeStruct) else t for t in lands]
    res = pl.pallas_call(
        body, name=name,
        out_shape=(pltpu.SemaphoreType.DMA((n_copies,)), pltpu.SemaphoreType.DMA((n_copies,)),
                   *[pltpu.HBM(t.shape, t.dtype) for t in bufs], _sds((8, 128), F32)),
        in_specs=[_HBM_SPEC] * n_buf + [pl.BlockSpec(memory_space=pl.ANY)] * len(extra),
        out_specs=(_SEM_SPEC, _SEM_SPEC, *([_HBM_SPEC] * n_buf), pl.BlockSpec(memory_space=pltpu.VMEM)),
        input_output_aliases={i: 2 + i for i in range(n_buf)},
        compiler_params=pltpu.CompilerParams(has_side_effects=_SIDE_EFFECT),
    )(*[pltpu.with_memory_space_constraint(t, pltpu.HBM) for t in bufs], *extra)
    return (n_src, res[0], res[1], res[2:2 + n_buf]), res[-1]


def _wait_copies(make, handle, after, *, name):
    n_src, send_sems, recv_sems, bufs = handle
    n_buf = len(bufs)

    def body(*refs):
        for cp in make(refs[:n_src], refs[n_src:n_buf], refs[n_buf], refs[n_buf + 1], False):
            cp.wait_send()
            cp.wait_recv()

    res = pl.pallas_call(
        body, name=name,
        out_shape=tuple(pltpu.HBM(t.shape, t.dtype) for t in bufs),
        in_specs=[_HBM_SPEC] * n_buf + [_SEM_SPEC, _SEM_SPEC, pl.BlockSpec(memory_space=pl.ANY)],
        out_specs=tuple([_HBM_SPEC] * n_buf),
        input_output_aliases={i: i for i in range(n_buf)},
        compiler_params=pltpu.CompilerParams(has_side_effects=_SIDE_EFFECT),
    )(*bufs, send_sems, recv_sems, after)
    return list(res[:n_src]), list(res[n_src:])


def _matmul(a, b, *, ta=False, tb=False, out_dtype, tm, tn, tk, name, after=None, extra=(), epilogue=None,
            n_colsum=0, transpose_out=False, n_limit=None):
    m, k = (a.shape[1], a.shape[0]) if ta else a.shape
    n = n_limit or (b.shape[0] if tb else b.shape[1])
    tm, tn, tk = min(tm, m), min(tn, n), min(tk, k)
    nk = k // tk
    dims = (((0 if ta else 1,), (1 if tb else 0,)), ((), ()))
    out_dtypes = out_dtype if isinstance(out_dtype, tuple) else (out_dtype,)
    n_tiles = len(out_dtypes)

    def add_colsums(o_refs, sums):
        i = pl.program_id(1)
        for s_ref, val in zip(o_refs[n_tiles:], sums):
            @pl.when(i == 0)
            def _(s_ref=s_ref, val=val):
                s_ref[...] = val

            @pl.when(i > 0)
            def _(s_ref=s_ref, val=val):
                s_ref[...] += val

    def finish(acc, x_refs, o_refs):
        vals = (acc,) if epilogue is None else epilogue(acc, *[r[...] for r in x_refs])
        for o_ref, val in zip(o_refs[:n_tiles], vals[:n_tiles]):
            o_ref[...] = (val.T if transpose_out else val).astype(o_ref.dtype)
        add_colsums(o_refs, vals[n_tiles:])

    chunk = EPILOGUE_ROWS if (nk == 1 and epilogue is not None and not ta and tm % EPILOGUE_ROWS == 0) else None

    def body(ins, outs, acc):
        a_ref, b_ref = ins[:2]
        if chunk is not None:
            sums = None
            for r0 in range(0, tm, chunk):
                part = lax.dot_general(a_ref[r0:r0 + chunk, :], b_ref[...], dims, preferred_element_type=F32)
                vals = epilogue(part, *[r[...] if r.shape[0] == 1 else r[r0:r0 + chunk, :] for r in ins[2:]])
                for o_ref, val in zip(outs[:n_tiles], vals[:n_tiles]):
                    o_ref[r0:r0 + chunk, :] = val.astype(o_ref.dtype)
                sums = vals[n_tiles:] if sums is None else [s + v for s, v in zip(sums, vals[n_tiles:])]
            add_colsums(outs, sums)
            return
        part = lax.dot_general(a_ref[...], b_ref[...], dims, preferred_element_type=F32)
        if nk == 1:
            finish(part, ins[2:], outs)
            return
        acc_ref, = acc
        kk = pl.program_id(2)

        @pl.when(kk == 0)
        def _():
            acc_ref[...] = part

        @pl.when(kk > 0)
        def _():
            acc_ref[...] += part

        @pl.when(kk == nk - 1)
        def _():
            finish(acc_ref[...], ins[2:], outs)

    a_spec = (pl.BlockSpec((tk, tm), lambda j, i, kk: (kk, i)) if ta
              else pl.BlockSpec((tm, tk), lambda j, i, kk: (i, kk)))
    b_spec = (pl.BlockSpec((tn, tk), lambda j, i, kk: (j, kk)) if tb
              else pl.BlockSpec((tk, tn), lambda j, i, kk: (kk, j)))
    tile = pl.BlockSpec((tm, tn), lambda j, i, kk: (i, j))
    row = pl.BlockSpec((1, tn), lambda j, i, kk: (0, j))

    def x_spec(t):
        if isinstance(t, tuple):
            return pl.BlockSpec((tm, tn), lambda j, i, kk, first=t[1] * (n // tn): (i, first + j))
        return row if t.shape[0] == 1 else tile

    out_tile, out_dims = (pl.BlockSpec((tn, tm), lambda j, i, kk: (j, i)), (n, m)) if transpose_out else (tile, (m, n))
    res = _call(
        body, name=name, grid=(n // tn, m // tm, nk),
        in_specs=[a_spec, b_spec] + [x_spec(t) for t in extra],
        out_specs=[out_tile] * n_tiles + [row] * n_colsum,
        out_shape=[_sds(out_dims, dt) for dt in out_dtypes] + [_sds((1, n), F32)] * n_colsum,
        scratch_shapes=[] if nk == 1 else [pltpu.VMEM((tm, tn), F32)],
        args=(a, b, *[t[0] if isinstance(t, tuple) else t for t in extra]), after=after)
    return res if isinstance(out_dtype, tuple) or n_colsum else res[0]


def _rstd(h):
    return lax.rsqrt(jnp.mean(h * h, axis=-1, keepdims=True) + RMS_EPS)


def _sigmoid(z):
    return 1.0 / (1.0 + jnp.exp(-z))


def _rms_fwd(x, g, *, tm, name):
    n = x.shape[0]

    def body(x_ref, g_ref, o_ref):
        h = x_ref[...]
        o_ref[...] = (h * _rstd(h) * g_ref[...]).astype(BF16)

    return pl.pallas_call(
        body, name=name, grid=(n // tm,),
        in_specs=[_rows(tm, D_MODEL), _const((1, D_MODEL))],
        out_specs=_rows(tm, D_MODEL), out_shape=_sds((n, D_MODEL), BF16),
        compiler_params=_params("parallel"),
    )(x, g)


def _swap_halves(t):
    lane = lax.broadcasted_iota(jnp.int32, (t.shape[0], 128), 1)
    pieces = [t[:, c:c + 128] for c in range(0, t.shape[1], 128)]
    return jnp.concatenate([jnp.where((lane & 63) < 32, pltpu.roll(h, 96, 1), pltpu.roll(h, 32, 1))
                            for h in pieces], axis=1)


def _dil_spec(dil, tm):
    return pl.BlockSpec((dil, tm // dil, 256), lambda i: (0, i, 0))


def _dil_scratch(tm):
    return pltpu.VMEM((2, tm, 128), F32)


def _load_token_order(src, scr, dil, tm):
    if dil == 1:
        return src[0]
    for j in range(dil):
        for c in range(2):
            scr[c, pl.ds(j, tm // dil, stride=dil), :] = src[j, :, c * 128:(c + 1) * 128]
    return jnp.concatenate([scr[0], scr[1]], axis=1)


def _store_dil_order(val, dst, scr, dil, tm):
    if dil == 1:
        dst[0] = val.astype(dst.dtype)
        return
    for c in range(2):
        scr[c] = val[:, c * 128:(c + 1) * 128]
    for j in range(dil):
        for c in range(2):
            dst[j, :, c * 128:(c + 1) * 128] = scr[c, pl.ds(j, tm // dil, stride=dil), :].astype(dst.dtype)


def _split_proj(proj, cos_t, sin_t, *, tm, name):
    n = proj.shape[0]
    n_dil = len(DIL_DILATIONS)

    def body(*refs):
        na_in = refs[0:3]
        dil_in = refs[3:3 + 3 * n_dil]
        cos_ref, sin_ref = refs[12:14]
        outs = refs[14:]
        na_out = outs[0:3]
        dil_out = outs[3:12]
        scr = outs[12]
        for t in range(3):
            na_out[t][...] = na_in[t][...].astype(BF16)
        cosv, sinv = cos_ref[...], sin_ref[...]
        for t in range(3):
            for gi, dil in enumerate(DIL_DILATIONS):
                val = dil_in[t * n_dil + gi][...]
                if t < 2:
                    val = val * cosv + _swap_halves(val) * sinv
                _store_dil_order(val, dil_out[t * n_dil + gi], scr, dil, tm)

    in_specs = [_rows(tm, NA_WIDTH, c) for c in range(3)]
    in_specs += [_rows(tm, 256, 6 + c) for c in range(9)]
    in_specs += [_rows(tm, 256), _rows(tm, 256)]
    out_specs = [_rows(tm, NA_WIDTH)] * 3
    out_shape = [_sds((n, NA_WIDTH), BF16)] * 3
    for _ in range(3):
        for dil in DIL_DILATIONS:
            out_specs.append(pl.BlockSpec((dil, tm // dil, 256), lambda i: (0, i, 0)))
            out_shape.append(_sds((dil, n // dil, 256), BF16))
    res = pl.pallas_call(
        body, name=name, grid=(n // tm,),
        in_specs=in_specs, out_specs=out_specs, out_shape=out_shape,
        scratch_shapes=[_dil_scratch(tm)],
        compiler_params=_params("parallel"),
    )(*([proj] * 12), cos_t, sin_t)
    return res[0:3], res[3:6], res[6:9], res[9:12]


def _residual_rms_tile(delta, h, g):
    hn = h + delta
    return hn, hn * _rstd(hn) * g


def _gate_bwd_tile(dm, s1, b1, s2, b2):
    return dm * s1, dm * s2, dm * b1 * s1 * (1.0 - s1), dm * b2 * s2 * (1.0 - s2)


def _tail_tile(gt, pp, h2, target, g):
    sg = _sigmoid(gt)
    h3 = h2 + sg * pp
    r3 = _rstd(h3)
    n3 = h3 * r3
    err = n3 * g - target
    loss = 0.5 * jnp.sum(jnp.sum(err * err, axis=-1, keepdims=True) / D_MODEL)
    dy = err / D_MODEL
    dn = dy * g
    dh3 = r3 * (dn - n3 * jnp.mean(dn * n3, axis=-1, keepdims=True))
    return (dh3, dh3 * sg, dh3 * pp * sg * (1.0 - sg),
            jnp.sum(dy * n3, axis=0, keepdims=True), jnp.full((1, gt.shape[1]), loss, F32))


def _rms_bwd_tile(dz, h, g, dres):
    r = _rstd(h)
    nrm = h * r
    dn = dz * g
    dh = dres + r * (dn - nrm * jnp.mean(dn * nrm, axis=-1, keepdims=True))
    return dh, jnp.sum(dz * nrm, axis=0, keepdims=True)


def _rms_bwd_twice(dz, h, g, dres):
    dh, dg = _rms_bwd_tile(dz, h, g, dres)
    return dh, dh, dg


def _assemble_dproj(dna, ddil_q, ddil_k, ddil_v, dgn, dgd, cos_t, sin_t, *, tm, name):
    n = dgn.shape[0]

    def body(*refs):
        dq_ref, dk_ref, dv_ref = refs[0:3]
        dil_in = refs[3:12]
        dgn_ref, dgd_ref, cos_ref, sin_ref, o_ref, scr = refs[12:18]
        o_ref[:, 0:512] = dq_ref[...]
        o_ref[:, 512:1024] = dk_ref[...].astype(BF16)
        o_ref[:, 1024:1536] = dv_ref[...].astype(BF16)
        cosv, sinv = cos_ref[...], sin_ref[...]
        for t in range(3):
            for gi, dil in enumerate(DIL_DILATIONS):
                val = _load_token_order(dil_in[t * 3 + gi], scr, dil, tm)
                if t < 2:
                    val = val * cosv + _swap_halves(val * sinv)
                c0 = 1536 + t * DIL_WIDTH + gi * 256
                o_ref[:, c0:c0 + 256] = val.astype(BF16)
        o_ref[:, 3840:4864] = dgn_ref[...]
        o_ref[:, 4864:5888] = dgd_ref[...]

    in_specs = [_rows(tm, NA_WIDTH)] * 3
    for _ in range(3):
        for dil in DIL_DILATIONS:
            in_specs.append(pl.BlockSpec((dil, tm // dil, 256), lambda i: (0, i, 0)))
    in_specs += [_rows(tm, D_MODEL)] * 2 + [_rows(tm, 256)] * 2
    return pl.pallas_call(
        body, name=name, grid=(n // tm,), in_specs=in_specs,
        out_specs=_rows(tm, IN_WIDTH), out_shape=_sds((n, IN_WIDTH), BF16),
        scratch_shapes=[_dil_scratch(tm)],
        compiler_params=_params("parallel"),
    )(*dna, *ddil_q, *ddil_k, *ddil_v, dgn, dgd, cos_t, sin_t)


N_ROW_OFF = 2 * NA_WIN_ROWS - 1
N_PAIRS = N_ROW_OFF - 1
RB_WIDTH = (N_ROW_OFF + 1) * GRID_W


def _na_bias(rb_ref, pair_scr):
    shape = (GRID_W, RB_WIDTH)
    qc = lax.broadcasted_iota(jnp.int32, shape, 0)
    qc2 = lax.broadcasted_iota(jnp.int32, (GRID_W, 128), 0)
    kc2 = lax.broadcasted_iota(jnp.int32, (GRID_W, 128), 1) & (GRID_W - 1)
    cs = jnp.clip(qc2 - 8, 0, GRID_W - 16)
    valid = (kc2 >= cs) & (kc2 < cs + 16)
    for hh in range(2):
        t = jnp.broadcast_to(rb_ref[hh], shape)
        t = pltpu.roll(t, RB_WIDTH - 15, 1)
        for b in range(6):
            t = jnp.where(((qc >> b) & 1) == 1, pltpu.roll(t, 1 << b, 1), t)
        t_odd = pltpu.roll(t, RB_WIDTH - GRID_W, 1)
        for ro in range(N_PAIRS):
            src = t if ro % 2 == 0 else t_odd
            base = (ro // 2) * 128
            pair_scr[hh, ro] = jnp.where(valid, src[:, base:base + 128], NEG_INF)


NA_GROUP_FWD = 8
NA_GROUP_BWD = 4


def _stack_heads(ref, r, scale=1.0):
    lane = lax.broadcasted_iota(jnp.int32, (GRID_W, 128), 1)
    t = ref[pl.ds(pl.multiple_of(r * GRID_W, GRID_W), GRID_W), :].astype(F32) * scale
    return jnp.concatenate([jnp.where(lane < 64, t, 0.0), jnp.where(lane >= 64, t, 0.0)], axis=0).astype(BF16)


def _unstack_heads(t2):
    lane = lax.broadcasted_iota(jnp.int32, (GRID_W, 128), 1)
    return jnp.where(lane < 64, t2[:GRID_W], t2[GRID_W:])


def _na_window(k_ref, v_ref, r, n_rows):
    rs = jnp.clip(r - NA_WIN_ROWS // 2, 0, n_rows - NA_WIN_ROWS)
    ro0 = (NA_WIN_ROWS - 1) - (r - rs)
    off = pl.multiple_of(rs * GRID_W, GRID_W)
    kw = k_ref[pl.ds(off, NA_WIN_ROWS * GRID_W), :]
    vw = v_ref[pl.ds(off, NA_WIN_ROWS * GRID_W), :]
    return kw, vw, off, ro0


def _na_probs(s_raw, pair_scr, ro0):
    bias = [jnp.concatenate([pair_scr[hh, ro0 + 2 * j] for j in range(NA_WIN_ROWS // 2)], axis=1)
            for hh in range(2)]
    s = s_raw + jnp.concatenate(bias, axis=0)
    m = jnp.max(s, axis=-1, keepdims=True)
    e = jnp.exp(s - m)
    return e * (1.0 / jnp.sum(e, axis=-1, keepdims=True))


def _na_fwd(q, k, v, rb, *, name):
    n = q.shape[0]
    n_rows = n // GRID_W

    def body(ins, outs, scr):
        q_ref, k_ref, v_ref, rb_ref = ins
        o_ref, = outs
        pair_scr, = scr
        _na_bias(rb_ref, pair_scr)

        def group(g, carry):
            rows = [g * NA_GROUP_FWD + t for t in range(NA_GROUP_FWD)]
            wins = [_na_window(k_ref, v_ref, r, n_rows) for r in rows]
            raw = [lax.dot_general(_stack_heads(q_ref, r, QK_SCALE), w[0], NT_DIMS, preferred_element_type=F32)
                   for r, w in zip(rows, wins)]
            probs = [_na_probs(s, pair_scr, w[3]) for s, w in zip(raw, wins)]
            outs2 = [jnp.dot(p.astype(BF16), w[1], preferred_element_type=F32) for p, w in zip(probs, wins)]
            for r, o2 in zip(rows, outs2):
                o_ref[pl.ds(pl.multiple_of(r * GRID_W, GRID_W), GRID_W), :] = _unstack_heads(o2).astype(BF16)
            return carry

        lax.fori_loop(0, n_rows // NA_GROUP_FWD, group, 0)

    col = pl.BlockSpec((n, 128), lambda h: (0, h))
    return _call(
        body, name=name, grid=(NA_WIDTH // 128,),
        in_specs=[col, col, col, pl.BlockSpec((2, 1, RB_WIDTH), lambda h: (h, 0, 0))],
        out_specs=[col], out_shape=[_sds((n, NA_WIDTH), BF16)],
        scratch_shapes=[pltpu.VMEM((2, N_PAIRS, GRID_W, 128), F32)],
        args=(q, k, v, rb))[0]


def _na_bwd(q, k, v, do, rb, *, name):
    n = q.shape[0]
    n_rows = n // GRID_W
    win = NA_WIN_ROWS * GRID_W

    def body(ins, outs, scr):
        q_ref, k_ref, v_ref, do_ref, rb_ref = ins
        dq_ref, dk_ref, dv_ref, drb_ref = outs
        pair_scr, acc_scr = scr
        _na_bias(rb_ref, pair_scr)
        acc_scr[...] = jnp.zeros_like(acc_scr)
        dk_ref[...] = jnp.zeros_like(dk_ref)
        dv_ref[...] = jnp.zeros_like(dv_ref)

        def group(g, carry):
            rows = [g * NA_GROUP_BWD + t for t in range(NA_GROUP_BWD)]
            wins = [_na_window(k_ref, v_ref, r, n_rows) for r in rows]
            qss = [_stack_heads(q_ref, r, QK_SCALE) for r in rows]
            doss = [_stack_heads(do_ref, r) for r in rows]
            raw = [lax.dot_general(qs, w[0], NT_DIMS, preferred_element_type=F32) for qs, w in zip(qss, wins)]
            dps = [lax.dot_general(dos, w[1], NT_DIMS, preferred_element_type=F32) for dos, w in zip(doss, wins)]
            probs = [_na_probs(s, pair_scr, w[3]) for s, w in zip(raw, wins)]
            dss = [p * (dp - jnp.sum(p * dp, axis=-1, keepdims=True)) for p, dp in zip(probs, dps)]
            dsbs = [ds.astype(BF16) for ds in dss]
            dq2s = [jnp.dot(dsb, w[0], preferred_element_type=F32) for dsb, w in zip(dsbs, wins)]
            dkws = [lax.dot_general(dsb, qs, TN_DIMS, preferred_element_type=F32) for dsb, qs in zip(dsbs, qss)]
            dvws = [lax.dot_general(p.astype(BF16), dos, TN_DIMS, preferred_element_type=F32)
                    for p, dos in zip(probs, doss)]
            for t, r in enumerate(rows):
                _, _, off, ro0 = wins[t]
                for hh in range(2):
                    for j in range(NA_WIN_ROWS // 2):
                        acc_scr[hh, ro0 + 2 * j] += dss[t][hh * GRID_W:(hh + 1) * GRID_W, j * 128:(j + 1) * 128]
                dq_ref[pl.ds(pl.multiple_of(r * GRID_W, GRID_W), GRID_W), :] = (
                    _unstack_heads(dq2s[t]) * QK_SCALE).astype(BF16)
                dk_ref[pl.ds(off, win), :] += dkws[t]
                dv_ref[pl.ds(off, win), :] += dvws[t]
            return carry

        lax.fori_loop(0, n_rows // NA_GROUP_BWD, group, 0)

        qc = lax.broadcasted_iota(jnp.int32, (N_PAIRS * GRID_W, 128), 0)
        for hh in range(2):
            t = acc_scr[hh].reshape(N_PAIRS * GRID_W, 128)
            for b in range(6):
                t = jnp.where(((qc >> b) & 1) == 1, pltpu.roll(t, 128 - (1 << b), 1), t)
            t = pltpu.roll(t, 15, 1)
            drb_ref[hh] = jnp.sum(t.reshape(N_PAIRS, GRID_W, 128), axis=1)

    col = pl.BlockSpec((n, 128), lambda h: (0, h))
    return _call(
        body, name=name, grid=(NA_WIDTH // 128,),
        in_specs=[col, col, col, col, pl.BlockSpec((2, 1, RB_WIDTH), lambda h: (h, 0, 0))],
        out_specs=[col, col, col, pl.BlockSpec((2, N_PAIRS, 128), lambda h: (h, 0, 0))],
        out_shape=[_sds((n, NA_WIDTH), BF16), _sds((n, NA_WIDTH), F32), _sds((n, NA_WIDTH), F32),
                   _sds((8, N_PAIRS, 128), F32)],
        scratch_shapes=[pltpu.VMEM((2, N_PAIRS, GRID_W, 128), F32),
                        pltpu.VMEM((2, N_PAIRS, GRID_W, 128), F32)],
        args=(q, k, v, do, rb))


def _rpb_table(rpb2):
    t = jnp.pad(rpb2, ((0, 0), (0, 1), (0, GRID_W - rpb2.shape[-1])))
    return t.reshape(8, 1, RB_WIDTH)


def _rpb_grad(drb, *, name):
    kdim = drb.shape[1]

    def body(x_ref, o_ref):
        kk = lax.broadcasted_iota(jnp.int32, (128, 512), 0)
        jj = lax.broadcasted_iota(jnp.int32, (128, 512), 1)
        half, co = kk >> 6, kk & 63
        acc = jnp.zeros((8, 512), F32)
        for ro in range(N_PAIRS):
            hit = ((ro + half) == (jj >> 5)) & (co == (jj & 31)) & (co < 31)
            onehot = jnp.where(hit, 1.0, 0.0).astype(F32)
            acc = acc + jnp.dot(x_ref[:, ro * 128:(ro + 1) * 128], onehot, preferred_element_type=F32,
                                precision=lax.Precision.HIGHEST)
        o_ref[...] = acc

    return pl.pallas_call(
        body, name=name, grid=(1,),
        in_specs=[_const((8, kdim))], out_specs=_const((8, 512)), out_shape=_sds((8, 512), F32),
        compiler_params=_params("arbitrary"),
    )(drb)


DIL_GROUP = 2


def _dil_blocks(length):
    qb = min(128, length)
    return qb, min(qb + 2 * DIL_RADIUS, length), min(DIL_GROUP, length // qb)


def _stack_lanes(ref, t, qb, scale=1.0):
    lane = lax.broadcasted_iota(jnp.int32, (qb, 256), 1)
    val = ref[0, t * qb:(t + 1) * qb, :].astype(F32) * scale
    return jnp.concatenate([jnp.where((lane >> 6) == h, val, 0.0) for h in range(4)], axis=0).astype(BF16)


def _dil_window(k_ref, v_ref, blk, qb, win, length):
    start = pl.multiple_of(jnp.clip(blk * qb - DIL_RADIUS, 0, length - win), DIL_RADIUS)
    return k_ref[0, pl.ds(start, win), :], v_ref[0, pl.ds(start, win), :], start


def _dil_mask(s, blk, start, qb, win):
    gap = ((lax.broadcasted_iota(jnp.int32, (4 * qb, win), 0) & (qb - 1))
           - lax.broadcasted_iota(jnp.int32, (4 * qb, win), 1)) + (blk * qb - start)
    return jnp.where(jnp.abs(gap) <= DIL_RADIUS, s, NEG_INF)


def _pick_heads(stacked, qb):
    lane = lax.broadcasted_iota(jnp.int32, (qb, 256), 1)
    out = jnp.zeros((qb, 256), stacked.dtype)
    for h in range(4):
        out = jnp.where((lane >> 6) == h, stacked[h * qb:(h + 1) * qb], out)
    return out


def _stack_head_cols(ref, t, qb):
    return jnp.concatenate([ref[0, t * qb:(t + 1) * qb, 64 * h:64 * h + 1] for h in range(4)], axis=0)


def _dil_fwd(q, k, v, *, name, after=None):
    dil, length, _ = q.shape
    qb, win, grp = _dil_blocks(length)
    extra = [] if after is None else [after]

    def body(q_ref, k_ref, v_ref, *rest):
        o_ref, lse_ref = rest[-2:]
        blks = [pl.program_id(1) * grp + t for t in range(grp)]
        wins = [_dil_window(k_ref, v_ref, b, qb, win, length) for b in blks]
        raw = [lax.dot_general(_stack_lanes(q_ref, t, qb, QK_SCALE), w[0], NT_DIMS, preferred_element_type=F32)
               for t, w in enumerate(wins)]
        lses, outs = [], []
        for t, (s, w) in enumerate(zip(raw, wins)):
            s = _dil_mask(s, blks[t], w[2], qb, win)
            m = jnp.max(s, axis=-1, keepdims=True)
            e = jnp.exp(s - m)
            norm = jnp.sum(e, axis=-1, keepdims=True)
            lses.append(m + jnp.log(norm))
            outs.append(jnp.dot((e * (1.0 / norm)).astype(BF16), w[1], preferred_element_type=F32))
        for t in range(grp):
            o_ref[0, t * qb:(t + 1) * qb, :] = _pick_heads(outs[t], qb)
            lse_ref[0, t * qb:(t + 1) * qb, :] = _pick_heads(jnp.broadcast_to(lses[t], (4 * qb, 256)), qb)

    seq = pl.BlockSpec((1, length, 256), lambda j, i: (j, 0, 0))
    blk = pl.BlockSpec((1, grp * qb, 256), lambda j, i: (j, i, 0))
    return pl.pallas_call(
        body, name=name, grid=(dil, length // (grp * qb)),
        in_specs=[blk, seq, seq] + [pl.BlockSpec(memory_space=pl.ANY)] * len(extra), out_specs=[blk, blk],
        out_shape=[_sds((dil, length, 256), F32)] * 2,
        compiler_params=_params("parallel", "parallel"),
    )(q, k, v, *extra)


def _dil_bwd(q, k, v, do, lse, cc, *, name):
    dil, length, _ = q.shape
    qb, win, grp = _dil_blocks(length)

    def body(q_ref, k_ref, v_ref, do_ref, lse_ref, cc_ref, dq_ref, dk_ref, dv_ref):
        @pl.when(pl.program_id(1) == 0)
        def _():
            dk_ref[...] = jnp.zeros_like(dk_ref)
            dv_ref[...] = jnp.zeros_like(dv_ref)

        blks = [pl.program_id(1) * grp + t for t in range(grp)]
        wins = [_dil_window(k_ref, v_ref, b, qb, win, length) for b in blks]
        qss = [_stack_lanes(q_ref, t, qb, QK_SCALE) for t in range(grp)]
        doss = [_stack_lanes(do_ref, t, qb) for t in range(grp)]
        raw = [lax.dot_general(qs, w[0], NT_DIMS, preferred_element_type=F32) for qs, w in zip(qss, wins)]
        dps = [lax.dot_general(dos, w[1], NT_DIMS, preferred_element_type=F32) for dos, w in zip(doss, wins)]
        probs = [jnp.exp(_dil_mask(s, blks[t], wins[t][2], qb, win) - _stack_head_cols(lse_ref, t, qb))
                 for t, s in enumerate(raw)]
        dsbs = [(p * (dp + _stack_head_cols(cc_ref, t, qb))).astype(BF16)
                for t, (p, dp) in enumerate(zip(probs, dps))]
        dq4s = [jnp.dot(dsb, w[0], preferred_element_type=F32) for dsb, w in zip(dsbs, wins)]
        dkws = [lax.dot_general(dsb, qs, TN_DIMS, preferred_element_type=F32) for dsb, qs in zip(dsbs, qss)]
        dvws = [lax.dot_general(p.astype(BF16), dos, TN_DIMS, preferred_element_type=F32)
                for p, dos in zip(probs, doss)]
        for t in range(grp):
            dq_ref[0, t * qb:(t + 1) * qb, :] = _pick_heads(dq4s[t], qb) * QK_SCALE
            dk_ref[0, pl.ds(wins[t][2], win), :] += dkws[t]
            dv_ref[0, pl.ds(wins[t][2], win), :] += dvws[t]

    seq = pl.BlockSpec((1, length, 256), lambda j, i: (j, 0, 0))
    blk = pl.BlockSpec((1, grp * qb, 256), lambda j, i: (j, i, 0))
    return pl.pallas_call(
        body, name=name, grid=(dil, length // (grp * qb)),
        in_specs=[blk, seq, seq, blk, blk, blk], out_specs=[blk, seq, seq],
        out_shape=[_sds((dil, length, 256), F32)] * 3,
        compiler_params=_params("parallel", "arbitrary"),
    )(q, k, v, do, lse, cc)


def _merge_weights(lses):
    m = jnp.maximum(jnp.maximum(lses[0], lses[1]), lses[2])
    es = [jnp.exp(t - m) for t in lses]
    inv = 1.0 / (es[0] + es[1] + es[2])
    return [e * inv for e in es]


def _dil_merge(outs, lses, *, tm, name):
    n = outs[0].shape[1]

    def body(*refs):
        o_in, l_in = refs[0:3], refs[3:6]
        y_ref, yb_ref, scr = refs[6:9]
        lv = [_load_token_order(l_in[g], scr, d, tm) for g, d in enumerate(DIL_DILATIONS)]
        ws = _merge_weights(lv)
        y = jnp.zeros((tm, 256), F32)
        for g, d in enumerate(DIL_DILATIONS):
            y = y + ws[g] * _load_token_order(o_in[g], scr, d, tm)
        y_ref[...] = y
        yb_ref[...] = y.astype(BF16)

    specs = [_dil_spec(d, tm) for d in DIL_DILATIONS]
    return pl.pallas_call(
        body, name=name, grid=(n // tm,), in_specs=specs + specs,
        out_specs=[_rows(tm, 256)] * 2, out_shape=[_sds((n, 256), F32), _sds((n, 256), BF16)],
        scratch_shapes=[_dil_scratch(tm)],
        compiler_params=_params("parallel"),
    )(*outs, *lses)


def _dil_merge_bwd(dy, y, lses, *, tm, name):
    n = dy.shape[0]

    def body(*refs):
        dy_ref, y_ref = refs[0:2]
        l_in = refs[2:5]
        do_out, cc_out = refs[5:8], refs[8:11]
        scr = refs[11]
        lv = [_load_token_order(l_in[g], scr, d, tm) for g, d in enumerate(DIL_DILATIONS)]
        ws = _merge_weights(lv)
        dyv = dy_ref[...]
        rr = lax.broadcasted_iota(jnp.int32, (256, 256), 0) >> 6
        cc = lax.broadcasted_iota(jnp.int32, (256, 256), 1) >> 6
        ones = jnp.where(rr == cc, 1.0, 0.0).astype(F32)
        tsum = jnp.dot(dyv * y_ref[...], ones, preferred_element_type=F32,
                       precision=lax.Precision.HIGHEST)
        for g, d in enumerate(DIL_DILATIONS):
            _store_dil_order(ws[g] * dyv, do_out[g], scr, d, tm)
            _store_dil_order(-ws[g] * tsum, cc_out[g], scr, d, tm)

    specs = [_dil_spec(d, tm) for d in DIL_DILATIONS]
    res = pl.pallas_call(
        body, name=name, grid=(n // tm,),
        in_specs=[_rows(tm, 256)] * 2 + specs,
        out_specs=specs + specs,
        out_shape=[_sds((d, n // d, 256), BF16) for d in DIL_DILATIONS]
                  + [_sds((d, n // d, 256), F32) for d in DIL_DILATIONS],
        scratch_shapes=[_dil_scratch(tm)],
        compiler_params=_params("parallel"),
    )(dy, y, *lses)
    return res[0:3], res[3:6]


_WEIGHTS = (("w_in", 1, 736), ("w_branch_na", 1, 128), ("w_branch_dil", 1, 128), ("w_out", 0, 128),
            ("w_up", 1, 512), ("w_down", 0, 512), ("w_ple_gate", 0, 128), ("w_ple_proj", 1, 128))
_W_IN, _W_BNA, _W_BD, _W_OUT, _W_UP, _W_DOWN, _W_PG, _W_PP = range(8)


def _to_full(gathered):
    return gathered.reshape(-1, gathered.shape[2])


def _to_chunks(widx, mat):
    return mat.reshape(N_DEV, _WEIGHTS[widx][2], mat.shape[1])


def _local_step(x, p_bf16, positions, target, g_mix, g_mlp, g_ple, g_final, rpb2,
                get_w_in, relay_rest, get_rest, send_grads):
    tm = 256
    half = HEAD_DIM // 2
    inv_freq = 10000.0 ** (-jnp.arange(half, dtype=F32) / half)
    ang = positions.astype(F32)[:, None] * inv_freq
    cos, sin = jnp.cos(ang), jnp.sin(ang)
    cos_t = jnp.tile(jnp.concatenate([cos, cos], axis=-1), (1, 4))
    sin_t = jnp.tile(jnp.concatenate([-sin, sin], axis=-1), (1, 4))
    rb = _rpb_table(rpb2)

    a = _rms_fwd(x, g_mix, tm=tm, name="rms_mix")
    w_in, token = get_w_in(a)
    qkv_width = 3 * NA_WIDTH + 3 * DIL_WIDTH
    proj = _matmul(a, w_in, tb=True, n_limit=qkv_width, out_dtype=F32, tm=512, tn=qkv_width // 2, tk=1024,
                   name="mm_in_qkv", after=token)
    gates = _matmul(a, w_in[qkv_width:], tb=True, out_dtype=F32, tm=512, tn=2 * D_MODEL, tk=1024,
                    name="mm_in_gates", epilogue=lambda acc: (_sigmoid(acc),))
    sn, sd = (gates, 0), (gates, 1)
    na_qkv, dq_g, dk_g, dv_g = _split_proj(proj, cos_t, sin_t, tm=tm, name="split_proj")
    y_na = _na_fwd(*na_qkv, rb, name="na_fwd")
    token = relay_rest(y_na)
    d_out, d_lse = [], []
    for g in range(3):
        o, lse = _dil_fwd(dq_g[g], dk_g[g], dv_g[g], name=f"dil_fwd{g}", after=token if g == 0 else None)
        d_out.append(o)
        d_lse.append(lse)
    y_dil, y_dil_b = _dil_merge(d_out, d_lse, tm=tm, name="dil_merge")
    w_bna, w_bd, w_out, w_up, w_down, w_pg, w_pp = get_rest(y_dil_b)
    bn = _matmul(y_na, w_bna, tb=True, out_dtype=F32, tm=512, tn=1024, tk=512, name="mm_bna")
    bd, mixed = _matmul(y_dil_b, w_bd, tb=True, out_dtype=(F32, BF16), tm=512, tn=1024, tk=256, name="mm_bd",
                        extra=(sn, bn, sd), epilogue=lambda acc, s1, b1, s2: (acc, s1 * b1 + s2 * acc))
    h1, c = _matmul(mixed, w_out, out_dtype=(F32, BF16), tm=512, tn=1024, tk=1024, name="mm_out",
                    extra=(x, g_mlp), epilogue=_residual_rms_tile)
    u, f = _matmul(c, w_up, tb=True, out_dtype=(BF16, BF16), tm=512, tn=2048, tk=1024, name="mm_up",
                   epilogue=lambda acc: (acc, jnp.square(jnp.maximum(acc, 0.0))))
    h2, e = _matmul(f, w_down, out_dtype=(F32, BF16), tm=512, tn=1024, tk=4096, name="mm_down",
                    extra=(h1, g_ple), epilogue=_residual_rms_tile)
    pp = _matmul(p_bf16, w_pp, tb=True, out_dtype=F32, tm=512, tn=1024, tk=256, name="mm_pp")

    dh3, dpp, dgt, dg_final, loss = _matmul(
        e, w_pg, out_dtype=(F32, BF16, BF16), tm=512, tn=1024, tk=1024, name="mm_pg_tail",
        extra=(pp, h2, target, g_final), epilogue=_tail_tile, n_colsum=2)
    loss = loss[:, :128]
    gw_pp = _matmul(p_bf16, dpp, ta=True, transpose_out=True, out_dtype=BF16, tm=256, tn=1024, tk=2048,
                    name="mm_gw_pp")
    gw_pg = _matmul(e, dgt, ta=True, out_dtype=BF16, tm=512, tn=1024, tk=2048, name="mm_gw_pg")
    dh2, dh2_b, dg_ple = _matmul(
        dgt, w_pg, tb=True, out_dtype=(F32, BF16), tm=512, tn=1024, tk=1024, name="mm_de",
        extra=(h2, g_ple, dh3), epilogue=_rms_bwd_twice, n_colsum=1)
    du = _matmul(dh2_b, w_down, tb=True, out_dtype=BF16, tm=512, tn=2048, tk=1024, name="mm_du",
                 extra=(u,), epilogue=lambda acc, uv: (acc * (2.0 * jnp.maximum(uv.astype(F32), 0.0)),))
    gw_down = _matmul(f, dh2_b, ta=True, out_dtype=BF16, tm=1024, tn=1024, tk=2048, name="mm_gw_down")
    gw_up = _matmul(c, du, ta=True, transpose_out=True, out_dtype=BF16, tm=512, tn=2048, tk=2048, name="mm_gw_up")
    dh1, dh1_b, dg_mlp = _matmul(
        du, w_up, out_dtype=(F32, BF16), tm=512, tn=1024, tk=4096, name="mm_dc",
        extra=(h1, g_mlp, dh2), epilogue=_rms_bwd_twice, n_colsum=1)
    dbn, dbd, dgn, dgd = _matmul(dh1_b, w_out, tb=True, out_dtype=(BF16,) * 4, tm=512, tn=1024, tk=1024,
                                 name="mm_dmixed", extra=(sn, bn, sd, bd), epilogue=_gate_bwd_tile)
    gw_out = _matmul(mixed, dh1_b, ta=True, out_dtype=BF16, tm=512, tn=1024, tk=2048, name="mm_gw_out")
    gw_bna = _matmul(y_na, dbn, ta=True, transpose_out=True, out_dtype=BF16, tm=512, tn=1024, tk=2048,
                     name="mm_gw_bna")
    dy_na = _matmul(dbn, w_bna, out_dtype=BF16, tm=512, tn=512, tk=1024, name="mm_dy_na")
    gw_bd = _matmul(y_dil_b, dbd, ta=True, transpose_out=True, out_dtype=BF16, tm=256, tn=1024, tk=2048,
                    name="mm_gw_bd")
    token = send_grads((_W_PP, _W_PG, _W_DOWN, _W_UP, _W_OUT, _W_BNA, _W_BD),
                       (gw_pp, gw_pg, gw_down, gw_up, gw_out, gw_bna, gw_bd))
    dy_dil = _matmul(dbd, w_bd, out_dtype=F32, tm=512, tn=256, tk=1024, name="mm_dy_dil", after=token)
    dna = _na_bwd(*na_qkv, dy_na, rb, name="na_bwd")
    drpb = _rpb_grad(dna[3].reshape(8, -1), name="rpb_grad")
    do_g, cc_g = _dil_merge_bwd(dy_dil, y_dil, d_lse, tm=tm, name="dil_merge_bwd")
    ddq, ddk, ddv = [], [], []
    for g in range(3):
        r = _dil_bwd(dq_g[g], dk_g[g], dv_g[g], do_g[g], d_lse[g], cc_g[g], name=f"dil_bwd{g}")
        ddq.append(r[0])
        ddk.append(r[1])
        ddv.append(r[2])
    dproj = _assemble_dproj(dna[0:3], ddq, ddk, ddv, dgn, dgd, cos_t, sin_t, tm=tm, name="assemble_dproj")
    gw_in = _matmul(a, dproj, ta=True, transpose_out=True, out_dtype=BF16, tm=512, tn=2944, tk=2048, name="mm_gw_in")
    token = send_grads((_W_IN,), (gw_in,))
    dx, dg_mix = _matmul(
        dproj, w_in, out_dtype=(F32,), tm=512, tn=1024, tk=5888, name="mm_da", after=token,
        extra=(x, g_mix, dh1), epilogue=_rms_bwd_tile, n_colsum=1)
    return loss, dx, (dg_mix, dg_mlp, dg_ple, dg_final), drpb


def _cast_bf16(t, *, name):
    def body(t_ref, o_ref):
        o_ref[...] = t_ref[...].astype(BF16)

    rows, cols = t.shape
    tr = 256 if rows % 256 == 0 else rows
    blk = pl.BlockSpec((tr, cols), lambda i: (i, 0))
    return pl.pallas_call(body, name=name, grid=(rows // tr,), in_specs=[blk], out_specs=blk,
                          out_shape=_sds(t.shape, BF16), compiler_params=_params("parallel"))(t)


def _adamw(w, g, m, v):
    m = ADAM_B1 * m + (1.0 - ADAM_B1) * g
    v = ADAM_B2 * v + (1.0 - ADAM_B2) * (g * g)
    m_hat = m / (1.0 - ADAM_B1 ** ADAM_STEP)
    v_hat = v / (1.0 - ADAM_B2 ** ADAM_STEP)
    delta = -ADAM_LR * (m_hat / (jnp.sqrt(v_hat) + ADAM_EPS) + ADAM_WD * w)
    return delta, m, v


def _sum_adamw(parts, w, m, v, *, tr, name, own=None, transposed=False):
    rows, cols = w.shape
    n_pre = 0 if own is None else 1

    def body(*refs):
        p_ref, w_ref, m_ref, v_ref = refs[n_pre:n_pre + 4]
        g_ref, d_ref, nm_ref, nv_ref = refs[-4:]
        g = (p_ref[0] if own is None else refs[n_pre + 4][...]).astype(F32)
        for s in range(1, N_DEV):
            g = g + p_ref[s].astype(F32)
        if transposed:
            g = g.T
        g_ref[...] = g
        d_ref[...], nm_ref[...], nv_ref[...] = _adamw(w_ref[...], g, m_ref[...], v_ref[...])

    if transposed:
        blk = pl.BlockSpec((rows, tr), lambda i, *_: (0, i))
        g_rows, steps = rows, cols // tr
    else:
        blk = pl.BlockSpec((tr, cols), lambda i, *_: (i, 0))
        g_rows, steps = cols, rows // tr
    in_specs = [pl.BlockSpec((N_DEV, tr, g_rows), lambda i, *_: (0, i, 0)), blk, blk, blk]
    args = [parts, w, m, v]
    if own is not None:
        in_specs.append(pl.BlockSpec((None, tr, g_rows), lambda i, idx: (idx[0], i, 0)))
        args = [own[1]] + args + [own[0]]
    return pl.pallas_call(
        body, name=name,
        grid_spec=pltpu.PrefetchScalarGridSpec(num_scalar_prefetch=n_pre, grid=(steps,), in_specs=in_specs,
                                               out_specs=[blk] * 4),
        out_shape=[_sds((rows, cols), F32)] * 4,
        compiler_params=_params("parallel"),
    )(*args)


_RPB_SIZE = 8 * 15 * 31


def _pack_small(g_mix, g_mlp, g_ple, g_final, rpb, loss_row):
    flat = jnp.concatenate([g_mix.reshape(-1), g_mlp.reshape(-1), g_ple.reshape(-1), g_final.reshape(-1),
                            rpb.reshape(-1), jnp.zeros((3840 - _RPB_SIZE,), F32), loss_row.reshape(-1),
                            jnp.zeros((128,), F32)])
    return flat.reshape(64, 128)


def _unpack_small(t):
    flat = t.reshape(-1)
    return (flat[0:1024].reshape(1, 1024), flat[4096:4096 + _RPB_SIZE].reshape(1, 8, 15, 31),
            flat[1024:2048].reshape(1, 1024), flat[2048:3072].reshape(1, 1024), flat[3072:4096])


def kernel(x, p, positions, g_mix, w_in, rpb, w_branch_na, w_branch_dil, w_out, g_mlp, w_up, w_down, g_ple, w_ple_gate, w_ple_proj, g_final, loss_target, m_g_mix, m_w_in, m_rpb, m_w_branch_na, m_w_branch_dil, m_w_out, m_g_mlp, m_w_up, m_w_down, m_g_ple, m_w_ple_gate, m_w_ple_proj, m_g_final, v_g_mix, v_w_in, v_rpb, v_w_branch_na, v_w_branch_dil, v_w_out, v_g_mlp, v_w_up, v_w_down, v_g_ple, v_w_ple_gate, v_w_ple_proj, v_g_final):
    sharded = dict(w_in=(w_in, m_w_in, v_w_in), w_branch_na=(w_branch_na, m_w_branch_na, v_w_branch_na),
                   w_branch_dil=(w_branch_dil, m_w_branch_dil, v_w_branch_dil), w_out=(w_out, m_w_out, v_w_out),
                   w_up=(w_up, m_w_up, v_w_up), w_down=(w_down, m_w_down, v_w_down),
                   w_ple_gate=(w_ple_gate, m_w_ple_gate, v_w_ple_gate),
                   w_ple_proj=(w_ple_proj, m_w_ple_proj, v_w_ple_proj))
    shards = {k: tuple(t[0] for t in val) for k, val in sharded.items()}

    me = _my_index()

    shards["w_in"] = tuple(t.T for t in shards["w_in"])

    w_in_b = _cast_bf16(shards["w_in"][0], name="cast_w_in")
    rest_b = [shards[name][0].astype(BF16).T if axis == 1 else shards[name][0].astype(BF16)
              for name, axis, _ in _WEIGHTS[1:]]
    first_in, token_in = _start_copies(_first_leg_copies, [w_in_b], [_sds((N_DEV,) + w_in_b.shape, BF16)], 4,
                                       name="start_gather_w_in")

    def whole(landed, mine):
        return _to_full(lax.dynamic_update_index_in_dim(landed, mine, me, 0))

    rest = {}

    def get_w_in(after):
        (mine,), landed = _wait_copies(_first_leg_copies, first_in, after, name="wait_gather_w_in")
        second, token = _start_copies(_second_leg_copies, [], landed, 3, name="start_forward_w_in")
        _, (landed,) = _wait_copies(_second_leg_copies, second, token, name="wait_forward_w_in")
        rest["first"], token = _start_copies(_first_leg_copies, rest_b,
                                             [_sds((N_DEV,) + t.shape, BF16) for t in rest_b], 4 * len(rest_b),
                                             name="start_gather_rest", after=landed)
        return whole(landed, mine), token

    def relay_rest(after):
        rest["mine"], landed = _wait_copies(_first_leg_copies, rest["first"], after, name="wait_gather_rest")
        rest["second"], token = _start_copies(_second_leg_copies, [], landed, 3 * len(rest_b),
                                              name="start_forward_rest")
        return token

    def get_rest(after):
        _, landed = _wait_copies(_second_leg_copies, rest["second"], after, name="wait_forward_rest")
        return [whole(t, own) for t, own in zip(landed, rest["mine"])]

    sent = []

    def send_grads(indices, grads):
        chunked = [_to_chunks(i, g) for i, g in zip(indices, grads)]
        handle, token = _start_copies(_exchange_copies, chunked, [_sds(t.shape, BF16) for t in chunked],
                                      7 * len(chunked),
                                      name="start_exchange_" + ("w_in" if indices == (_W_IN,) else "rest"))
        sent.append((indices, handle))
        return token

    g_mix_0 = g_mix + token_in[0:1, 0:1]
    loss, dx, dgs, drpb = _local_step(
        x[0], p[0, 0].astype(BF16), positions[0], loss_target[0],
        g_mix_0, g_mlp, g_ple, g_final.reshape(1, -1), rpb[0], get_w_in, relay_rest, get_rest, send_grads)

    drpb3 = drpb.reshape(8, 16, 32)[:, :15, :31]
    small = _pack_small(dgs[0], dgs[1], dgs[2], dgs[3], drpb3, loss)
    share, done = _start_copies(_gather_copies, [small], [_sds((N_DEV,) + small.shape, F32)], 7,
                                name="start_share_small")

    out = {}
    for indices, handle in sent:
        chunked, landed = _wait_copies(_exchange_copies, handle, done,
                                       name="wait_exchange_" + ("w_in" if indices == (_W_IN,) else "rest"))
        for i, part, mine in zip(indices, landed, chunked):
            name = _WEIGHTS[i][0]
            w, m, v = shards[name]
            turned = _WEIGHTS[i][1] == 1 and i != _W_IN
            res = _sum_adamw(part, w, m, v, tr=368 if i == _W_IN else 128, name="adamw_" + name,
                             own=(mine, me.reshape(1).astype(jnp.int32)), transposed=turned)
            out[name] = [(t.T if i == _W_IN else t)[None] for t in res]
            done = res[0]
    (small,), (small_landed,) = _wait_copies(_gather_copies, share, done, name="wait_share_small")
    small_all = lax.dynamic_update_index_in_dim(small_landed, small, me, 0)
    small_w = _pack_small(g_mix, g_mlp, g_ple, g_final, rpb, jnp.zeros((128,), F32))
    small_m = _pack_small(m_g_mix, m_g_mlp, m_g_ple, m_g_final, m_rpb, jnp.zeros((128,), F32))
    small_v = _pack_small(v_g_mix, v_g_mlp, v_g_ple, v_g_final, v_rpb, jnp.zeros((128,), F32))
    res = _sum_adamw(small_all, small_w, small_m, small_v, tr=64, name="adamw_small")
    unpacked = [_unpack_small(t) for t in res]
    for i, name in enumerate(("g_mix", "rpb", "g_mlp", "g_ple", "g_final")):
        out[name] = [u[i] for u in unpacked]
    loss_total = res[0][62, 0]

    order = ("g_mix", "w_in", "rpb", "w_branch_na", "w_branch_dil", "w_out", "g_mlp", "w_up", "w_down",
             "g_ple", "w_ple_gate", "w_ple_proj", "g_final")
    grads = [out[k][0] for k in order]
    deltas = [out[k][1] for k in order]
    new_m = [out[k][2] for k in order]
    new_v = [out[k][3] for k in order]
    return (loss_total, dx[None], *grads, *deltas, *new_m, *new_v)
```

```python
import jax
import jax.numpy as jnp
from jax import lax
from jax.experimental import pallas as pl
from jax.experimental.pallas import tpu as pltpu

F32 = jnp.float32
BF16 = jnp.bfloat16

D_MODEL = 1024
HEAD_DIM = 64
GRID_W = 64
NA_WIDTH = 512
DIL_WIDTH = 768
IN_WIDTH = 5888
DIL_DILATIONS = (1, 4, 16)
DIL_RADIUS = 64
NA_WIN_ROWS = 8
RMS_EPS = 1e-6
NEG_INF = -1e30
QK_SCALE = HEAD_DIM ** -0.5

ADAM_LR = 0.001
ADAM_B1 = 0.9
ADAM_B2 = 0.999
ADAM_EPS = 1e-08
ADAM_WD = 0.01
ADAM_STEP = 10

N_DEV = 8
VMEM_LIMIT = 56 * 1024 * 1024
EPILOGUE_ROWS = 256
MESH = pl.DeviceIdType.MESH

NT_DIMS = (((1,), (1,)), ((), ()))
TN_DIMS = (((0,), (0,)), ((), ()))


def _sds(shape, dtype):
    return jax.ShapeDtypeStruct(shape, dtype)


def _params(*sem):
    return pltpu.CompilerParams(dimension_semantics=sem, vmem_limit_bytes=VMEM_LIMIT)


def _rows(tm, width, col=0):
    return pl.BlockSpec((tm, width), lambda i, c=col: (i, c))


def _const(shape):
    zeros = (0,) * len(shape)
    return pl.BlockSpec(shape, lambda i: zeros)


def _my_index():
    return 4 * lax.axis_index("x") + 2 * lax.axis_index("y") + lax.axis_index("c")


def _peer(k):
    x, y, c = lax.axis_index("x"), lax.axis_index("y"), lax.axis_index("c")
    px = 1 - x if k & 4 else x
    py = 1 - y if k & 2 else y
    pc = 1 - c if k & 1 else c
    return (px, py, pc), 4 * px + 2 * py + pc


def _call(body, *, name, grid, in_specs, out_specs, out_shape, scratch_shapes, args, after=None):
    n_in, n_out = len(in_specs), len(out_specs)
    extra = [] if after is None else [after]
    n_x = n_in + len(extra)

    def plain(*refs):
        body(refs[:n_in], refs[n_x:n_x + n_out], refs[n_x + n_out:])

    res = pl.pallas_call(plain, name=name, grid=grid,
                         in_specs=list(in_specs) + [pl.BlockSpec(memory_space=pl.ANY)] * len(extra),
                         out_specs=out_specs, out_shape=out_shape, scratch_shapes=scratch_shapes,
                         compiler_params=_params(*(("arbitrary",) * len(grid))))(*args, *extra)
    return list(res)


_HBM_SPEC = pl.BlockSpec(memory_space=pltpu.HBM)
_SEM_SPEC = pl.BlockSpec(memory_space=pltpu.SEMAPHORE)
_SIDE_EFFECT = pltpu.SideEffectType.DATAFLOW_SIDE_EFFECTING


_FIRST_LEG = (1, 2, 4, 6)
_SECOND_LEG = (2, 4, 6)


def _gather_copies(srcs, lands, send, recv, sending):
    me = _my_index()
    out = []
    for w in range(len(srcs)):
        for k in range(1, N_DEV):
            dev, idx = _peer(k)
            out.append(pltpu.make_async_remote_copy(
                src_ref=srcs[w], dst_ref=lands[w].at[me if sending else idx],
                send_sem=send.at[w * 7 + k - 1], recv_sem=recv.at[w * 7 + k - 1],
                device_id=dev, device_id_type=MESH))
    return out


def _first_leg_copies(srcs, lands, send, recv, sending):
    me = _my_index()
    out = []
    for w in range(len(srcs)):
        for j, k in enumerate(_FIRST_LEG):
            dev, idx = _peer(k)
            out.append(pltpu.make_async_remote_copy(
                src_ref=srcs[w], dst_ref=lands[w].at[me if sending else idx],
                send_sem=send.at[w * 4 + j], recv_sem=recv.at[w * 4 + j],
                device_id=dev, device_id_type=MESH))
    return out


def _second_leg_copies(srcs, lands, send, recv, sending):
    sibling, _ = _peer(1)
    out = []
    for w in range(len(lands)):
        for j, k in enumerate(_SECOND_LEG):
            slot = _peer(k if sending else k ^ 1)[1]
            out.append(pltpu.make_async_remote_copy(
                src_ref=lands[w].at[slot], dst_ref=lands[w].at[slot],
                send_sem=send.at[w * 3 + j], recv_sem=recv.at[w * 3 + j],
                device_id=sibling, device_id_type=MESH))
    return out


def _exchange_copies(srcs, lands, send, recv, sending):
    out = []
    for w in range(len(srcs)):
        for k in range(1, N_DEV):
            dev, idx = _peer(k)
            out.append(pltpu.make_async_remote_copy(
                src_ref=srcs[w].at[idx], dst_ref=lands[w].at[k],
                send_sem=send.at[w * 7 + k - 1], recv_sem=recv.at[w * 7 + k - 1],
                device_id=dev, device_id_type=MESH))
    return out


def _start_copies(make, srcs, lands, n_copies, *, name, after=None):
    n_src, n_buf = len(srcs), len(srcs) + len(lands)
    extra = [] if after is None else [after]

    def body(*refs):
        send, recv = refs[n_buf + len(extra)], refs[n_buf + len(extra) + 1]
        for cp in make(refs[:n_src], refs[n_src:n_buf], send, recv, True):
            cp.start()
        refs[-1][...] = jnp.zeros_like(refs[-1])

    bufs = list(srcs) + [lax.empty(t.shape, t.dtype) if isinstance(t, jax.ShapeDtypeStruct) else t for t in lands]
    res = pl.pallas_call(
        body, name=name,
        out_shape=(pltpu.SemaphoreType.DMA((n_copies,)), pltpu.SemaphoreType.DMA((n_copies,)),
                   *[pltpu.HBM(t.shape, t.dtype) for t in bufs], _sds((8, 128), F32)),
        in_specs=[_HBM_SPEC] * n_buf + [pl.BlockSpec(memory_space=pl.ANY)] * len(extra),
        out_specs=(_SEM_SPEC, _SEM_SPEC, *([_HBM_SPEC] * n_buf), pl.BlockSpec(memory_space=pltpu.VMEM)),
        input_output_aliases={i: 2 + i for i in range(n_buf)},
        compiler_params=pltpu.CompilerParams(has_side_effects=_SIDE_EFFECT),
    )(*[pltpu.with_memory_space_constraint(t, pltpu.HBM) for t in bufs], *extra)
    return (n_src, res[0], res[1], res[2:2 + n_buf]), res[-1]


def _wait_copies(make, handle, after, *, name):
    n_src, send_sems, recv_sems, bufs = handle
    n_buf = len(bufs)

    def body(*refs):
        for cp in make(refs[:n_src], refs[n_src:n_buf], refs[n_buf], refs[n_buf + 1], False):
            cp.wait_send()
            cp.wait_recv()

    res = pl.pallas_call(
        body, name=name,
        out_shape=tuple(pltpu.HBM(t.shape, t.dtype) for t in bufs),
        in_specs=[_HBM_SPEC] * n_buf + [_SEM_SPEC, _SEM_SPEC, pl.BlockSpec(memory_space=pl.ANY)],
        out_specs=tuple([_HBM_SPEC] * n_buf),
        input_output_aliases={i: i for i in range(n_buf)},
        compiler_params=pltpu.CompilerParams(has_side_effects=_SIDE_EFFECT),
    )(*bufs, send_sems, recv_sems, after)
    return list(res[:n_src]), list(res[n_src:])


def _matmul(a, b, *, ta=False, tb=False, out_dtype, tm, tn, tk, name, after=None, extra=(), epilogue=None,
            n_colsum=0, transpose_out=False, n_limit=None):
    m, k = (a.shape[1], a.shape[0]) if ta else a.shape
    n = n_limit or (b.shape[0] if tb else b.shape[1])
    tm, tn, tk = min(tm, m), min(tn, n), min(tk, k)
    nk = k // tk
    dims = (((0 if ta else 1,), (1 if tb else 0,)), ((), ()))
    out_dtypes = out_dtype if isinstance(out_dtype, tuple) else (out_dtype,)
    n_tiles = len(out_dtypes)

    def add_colsums(o_refs, sums):
        i = pl.program_id(1)
        for s_ref, val in zip(o_refs[n_tiles:], sums):
            @pl.when(i == 0)
            def _(s_ref=s_ref, val=val):
                s_ref[...] = val

            @pl.when(i > 0)
            def _(s_ref=s_ref, val=val):
                s_ref[...] += val

    def finish(acc, x_refs, o_refs):
        vals = (acc,) if epilogue is None else epilogue(acc, *[r[...] for r in x_refs])
        for o_ref, val in zip(o_refs[:n_tiles], vals[:n_tiles]):
            o_ref[...] = (val.T if transpose_out else val).astype(o_ref.dtype)
        add_colsums(o_refs, vals[n_tiles:])

    chunk = EPILOGUE_ROWS if (nk == 1 and epilogue is not None and not ta and tm % EPILOGUE_ROWS == 0) else None

    def body(ins, outs, acc):
        a_ref, b_ref = ins[:2]
        if chunk is not None:
            sums = None
            for r0 in range(0, tm, chunk):
                part = lax.dot_general(a_ref[r0:r0 + chunk, :], b_ref[...], dims, preferred_element_type=F32)
                vals = epilogue(part, *[r[...] if r.shape[0] == 1 else r[r0:r0 + chunk, :] for r in ins[2:]])
                for o_ref, val in zip(outs[:n_tiles], vals[:n_tiles]):
                    o_ref[r0:r0 + chunk, :] = val.astype(o_ref.dtype)
                sums = vals[n_tiles:] if sums is None else [s + v for s, v in zip(sums, vals[n_tiles:])]
            add_colsums(outs, sums)
            return
        part = lax.dot_general(a_ref[...], b_ref[...], dims, preferred_element_type=F32)
        if nk == 1:
            finish(part, ins[2:], outs)
            return
        acc_ref, = acc
        kk = pl.program_id(2)

        @pl.when(kk == 0)
        def _():
            acc_ref[...] = part

        @pl.when(kk > 0)
        def _():
            acc_ref[...] += part

        @pl.when(kk == nk - 1)
        def _():
            finish(acc_ref[...], ins[2:], outs)

    a_spec = (pl.BlockSpec((tk, tm), lambda j, i, kk: (kk, i)) if ta
              else pl.BlockSpec((tm, tk), lambda j, i, kk: (i, kk)))
    b_spec = (pl.BlockSpec((tn, tk), lambda j, i, kk: (j, kk)) if tb
              else pl.BlockSpec((tk, tn), lambda j, i, kk: (kk, j)))
    tile = pl.BlockSpec((tm, tn), lambda j, i, kk: (i, j))
    row = pl.BlockSpec((1, tn), lambda j, i, kk: (0, j))

    def x_spec(t):
        if isinstance(t, tuple):
            return pl.BlockSpec((tm, tn), lambda j, i, kk, first=t[1] * (n // tn): (i, first + j))
        return row if t.shape[0] == 1 else tile

    out_tile, out_dims = (pl.BlockSpec((tn, tm), lambda j, i, kk: (j, i)), (n, m)) if transpose_out else (tile, (m, n))
    res = _call(
        body, name=name, grid=(n // tn, m // tm, nk),
        in_specs=[a_spec, b_spec] + [x_spec(t) for t in extra],
        out_specs=[out_tile] * n_tiles + [row] * n_colsum,
        out_shape=[_sds(out_dims, dt) for dt in out_dtypes] + [_sds((1, n), F32)] * n_colsum,
        scratch_shapes=[] if nk == 1 else [pltpu.VMEM((tm, tn), F32)],
        args=(a, b, *[t[0] if isinstance(t, tuple) else t for t in extra]), after=after)
    return res if isinstance(out_dtype, tuple) or n_colsum else res[0]


def _rstd(h):
    return lax.rsqrt(jnp.mean(h * h, axis=-1, keepdims=True) + RMS_EPS)


def _sigmoid(z):
    return 1.0 / (1.0 + jnp.exp(-z))


def _rms_fwd(x, g, *, tm, name):
    n = x.shape[0]

    def body(x_ref, g_ref, o_ref):
        h = x_ref[...]
        o_ref[...] = (h * _rstd(h) * g_ref[...]).astype(BF16)

    return pl.pallas_call(
        body, name=name, grid=(n // tm,),
        in_specs=[_rows(tm, D_MODEL), _const((1, D_MODEL))],
        out_specs=_rows(tm, D_MODEL), out_shape=_sds((n, D_MODEL), BF16),
        compiler_params=_params("parallel"),
    )(x, g)


def _swap_halves(t):
    lane = lax.broadcasted_iota(jnp.int32, (t.shape[0], 128), 1)
    pieces = [t[:, c:c + 128] for c in range(0, t.shape[1], 128)]
    return jnp.concatenate([jnp.where((lane & 63) < 32, pltpu.roll(h, 96, 1), pltpu.roll(h, 32, 1))
                            for h in pieces], axis=1)


def _dil_spec(dil, tm):
    return pl.BlockSpec((dil, tm // dil, 256), lambda i: (0, i, 0))


def _dil_scratch(tm):
    return pltpu.VMEM((2, tm, 128), F32)


def _load_token_order(src, scr, dil, tm):
    if dil == 1:
        return src[0]
    for j in range(dil):
        for c in range(2):
            scr[c, pl.ds(j, tm // dil, stride=dil), :] = src[j, :, c * 128:(c + 1) * 128]
    return jnp.concatenate([scr[0], scr[1]], axis=1)


def _store_dil_order(val, dst, scr, dil, tm):
    if dil == 1:
        dst[0] = val.astype(dst.dtype)
        return
    for c in range(2):
        scr[c] = val[:, c * 128:(c + 1) * 128]
    for j in range(dil):
        for c in range(2):
            dst[j, :, c * 128:(c + 1) * 128] = scr[c, pl.ds(j, tm // dil, stride=dil), :].astype(dst.dtype)


def _split_proj(proj, cos_t, sin_t, *, tm, name):
    n = proj.shape[0]
    n_dil = len(DIL_DILATIONS)

    def body(*refs):
        na_in = refs[0:3]
        dil_in = refs[3:3 + 3 * n_dil]
        cos_ref, sin_ref = refs[12:14]
        outs = refs[14:]
        na_out = outs[0:3]
        dil_out = outs[3:12]
        scr = outs[12]
        for t in range(3):
            na_out[t][...] = na_in[t][...].astype(BF16)
        cosv, sinv = cos_ref[...], sin_ref[...]
        for t in range(3):
            for gi, dil in enumerate(DIL_DILATIONS):
                val = dil_in[t * n_dil + gi][...]
                if t < 2:
                    val = val * cosv + _swap_halves(val) * sinv
                _store_dil_order(val, dil_out[t * n_dil + gi], scr, dil, tm)

    in_specs = [_rows(tm, NA_WIDTH, c) for c in range(3)]
    in_specs += [_rows(tm, 256, 6 + c) for c in range(9)]
    in_specs += [_rows(tm, 256), _rows(tm, 256)]
    out_specs = [_rows(tm, NA_WIDTH)] * 3
    out_shape = [_sds((n, NA_WIDTH), BF16)] * 3
    for _ in range(3):
        for dil in DIL_DILATIONS:
            out_specs.append(pl.BlockSpec((dil, tm // dil, 256), lambda i: (0, i, 0)))
            out_shape.append(_sds((dil, n // dil, 256), BF16))
    res = pl.pallas_call(
        body, name=name, grid=(n // tm,),
        in_specs=in_specs, out_specs=out_specs, out_shape=out_shape,
        scratch_shapes=[_dil_scratch(tm)],
        compiler_params=_params("parallel"),
    )(*([proj] * 12), cos_t, sin_t)
    return res[0:3], res[3:6], res[6:9], res[9:12]


def _residual_rms_tile(delta, h, g):
    hn = h + delta
    return hn, hn * _rstd(hn) * g


def _gate_mix_tile(b2, s1, b1, s2):
    return b2, s1.astype(F32) * b1.astype(F32) + s2.astype(F32) * b2


def _gate_bwd_tile(dm, s1, b1, s2, b2):
    s1, b1, s2, b2 = (t.astype(F32) for t in (s1, b1, s2, b2))
    return dm * s1, dm * s2, dm * b1 * s1 * (1.0 - s1), dm * b2 * s2 * (1.0 - s2)


def _tail_tile(gt, pp, h2, target, g):
    sg = _sigmoid(gt)
    h3 = h2 + sg * pp
    r3 = _rstd(h3)
    n3 = h3 * r3
    err = n3 * g - target
    loss = 0.5 * jnp.sum(jnp.sum(err * err, axis=-1, keepdims=True) / D_MODEL)
    dy = err / D_MODEL
    dn = dy * g
    dh3 = r3 * (dn - n3 * jnp.mean(dn * n3, axis=-1, keepdims=True))
    return (dh3, dh3 * sg, dh3 * pp * sg * (1.0 - sg),
            jnp.sum(dy * n3, axis=0, keepdims=True), jnp.full((1, gt.shape[1]), loss, F32))


def _rms_bwd_tile(dz, h, g, dres):
    r = _rstd(h)
    nrm = h * r
    dn = dz * g
    dh = dres + r * (dn - nrm * jnp.mean(dn * nrm, axis=-1, keepdims=True))
    return dh, jnp.sum(dz * nrm, axis=0, keepdims=True)


def _rms_bwd_twice(dz, h, g, dres):
    dh, dg = _rms_bwd_tile(dz, h, g, dres)
    return dh, dh, dg


def _assemble_dproj(dna, ddil_q, ddil_k, ddil_v, dgn, dgd, cos_t, sin_t, *, tm, name):
    n = dgn.shape[0]

    def body(*refs):
        dq_ref, dk_ref, dv_ref = refs[0:3]
        dil_in = refs[3:12]
        dgn_ref, dgd_ref, cos_ref, sin_ref, o_ref, scr = refs[12:18]
        o_ref[:, 0:512] = dq_ref[...]
        o_ref[:, 512:1024] = dk_ref[...].astype(BF16)
        o_ref[:, 1024:1536] = dv_ref[...].astype(BF16)
        cosv, sinv = cos_ref[...], sin_ref[...]
        for t in range(3):
            for gi, dil in enumerate(DIL_DILATIONS):
                val = _load_token_order(dil_in[t * 3 + gi], scr, dil, tm)
                if t < 2:
                    val = val * cosv + _swap_halves(val * sinv)
                c0 = 1536 + t * DIL_WIDTH + gi * 256
                o_ref[:, c0:c0 + 256] = val.astype(BF16)
        o_ref[:, 3840:4864] = dgn_ref[...]
        o_ref[:, 4864:5888] = dgd_ref[...]

    in_specs = [_rows(tm, NA_WIDTH)] * 3
    for _ in range(3):
        for dil in DIL_DILATIONS:
            in_specs.append(pl.BlockSpec((dil, tm // dil, 256), lambda i: (0, i, 0)))
    in_specs += [_rows(tm, D_MODEL)] * 2 + [_rows(tm, 256)] * 2
    return pl.pallas_call(
        body, name=name, grid=(n // tm,), in_specs=in_specs,
        out_specs=_rows(tm, IN_WIDTH), out_shape=_sds((n, IN_WIDTH), BF16),
        scratch_shapes=[_dil_scratch(tm)],
        compiler_params=_params("parallel"),
    )(*dna, *ddil_q, *ddil_k, *ddil_v, dgn, dgd, cos_t, sin_t)


N_ROW_OFF = 2 * NA_WIN_ROWS - 1
N_PAIRS = N_ROW_OFF - 1
RB_WIDTH = (N_ROW_OFF + 1) * GRID_W


def _na_bias(rb_ref, pair_scr):
    shape = (GRID_W, RB_WIDTH)
    qc = lax.broadcasted_iota(jnp.int32, shape, 0)
    qc2 = lax.broadcasted_iota(jnp.int32, (GRID_W, 128), 0)
    kc2 = lax.broadcasted_iota(jnp.int32, (GRID_W, 128), 1) & (GRID_W - 1)
    cs = jnp.clip(qc2 - 8, 0, GRID_W - 16)
    valid = (kc2 >= cs) & (kc2 < cs + 16)
    for hh in range(2):
        t = jnp.broadcast_to(rb_ref[hh], shape)
        t = pltpu.roll(t, RB_WIDTH - 15, 1)
        for b in range(6):
            t = jnp.where(((qc >> b) & 1) == 1, pltpu.roll(t, 1 << b, 1), t)
        t_odd = pltpu.roll(t, RB_WIDTH - GRID_W, 1)
        for ro in range(N_PAIRS):
            src = t if ro % 2 == 0 else t_odd
            base = (ro // 2) * 128
            pair_scr[hh, ro] = jnp.where(valid, src[:, base:base + 128], NEG_INF)


NA_GROUP_FWD = 8
NA_GROUP_BWD = 4


def _stack_heads(ref, r, scale=1.0):
    lane = lax.broadcasted_iota(jnp.int32, (GRID_W, 128), 1)
    t = ref[pl.ds(pl.multiple_of(r * GRID_W, GRID_W), GRID_W), :].astype(F32) * scale
    return jnp.concatenate([jnp.where(lane < 64, t, 0.0), jnp.where(lane >= 64, t, 0.0)], axis=0).astype(BF16)


def _unstack_heads(t2):
    lane = lax.broadcasted_iota(jnp.int32, (GRID_W, 128), 1)
    return jnp.where(lane < 64, t2[:GRID_W], t2[GRID_W:])


def _na_window(k_ref, v_ref, r, n_rows):
    rs = jnp.clip(r - NA_WIN_ROWS // 2, 0, n_rows - NA_WIN_ROWS)
    ro0 = (NA_WIN_ROWS - 1) - (r - rs)
    off = pl.multiple_of(rs * GRID_W, GRID_W)
    kw = k_ref[pl.ds(off, NA_WIN_ROWS * GRID_W), :]
    vw = v_ref[pl.ds(off, NA_WIN_ROWS * GRID_W), :]
    return kw, vw, off, ro0


def _na_probs(s_raw, pair_scr, ro0):
    bias = [jnp.concatenate([pair_scr[hh, ro0 + 2 * j] for j in range(NA_WIN_ROWS // 2)], axis=1)
            for hh in range(2)]
    s = s_raw + jnp.concatenate(bias, axis=0)
    m = jnp.max(s, axis=-1, keepdims=True)
    e = jnp.exp(s - m)
    return e * (1.0 / jnp.sum(e, axis=-1, keepdims=True))


def _na_fwd(q, k, v, rb, *, name):
    n = q.shape[0]
    n_rows = n // GRID_W

    def body(ins, outs, scr):
        q_ref, k_ref, v_ref, rb_ref = ins
        o_ref, = outs
        pair_scr, = scr
        _na_bias(rb_ref, pair_scr)

        def group(g, carry):
            rows = [g * NA_GROUP_FWD + t for t in range(NA_GROUP_FWD)]
            wins = [_na_window(k_ref, v_ref, r, n_rows) for r in rows]
            raw = [lax.dot_general(_stack_heads(q_ref, r, QK_SCALE), w[0], NT_DIMS, preferred_element_type=F32)
                   for r, w in zip(rows, wins)]
            probs = [_na_probs(s, pair_scr, w[3]) for s, w in zip(raw, wins)]
            outs2 = [jnp.dot(p.astype(BF16), w[1], preferred_element_type=F32) for p, w in zip(probs, wins)]
            for r, o2 in zip(rows, outs2):
                o_ref[pl.ds(pl.multiple_of(r * GRID_W, GRID_W), GRID_W), :] = _unstack_heads(o2).astype(BF16)
            return carry

        lax.fori_loop(0, n_rows // NA_GROUP_FWD, group, 0)

    col = pl.BlockSpec((n, 128), lambda h: (0, h))
    return _call(
        body, name=name, grid=(NA_WIDTH // 128,),
        in_specs=[col, col, col, pl.BlockSpec((2, 1, RB_WIDTH), lambda h: (h, 0, 0))],
        out_specs=[col], out_shape=[_sds((n, NA_WIDTH), BF16)],
        scratch_shapes=[pltpu.VMEM((2, N_PAIRS, GRID_W, 128), F32)],
        args=(q, k, v, rb))[0]


def _na_bwd(q, k, v, do, rb, *, name):
    n = q.shape[0]
    n_rows = n // GRID_W
    win = NA_WIN_ROWS * GRID_W

    def body(ins, outs, scr):
        q_ref, k_ref, v_ref, do_ref, rb_ref = ins
        dq_ref, dk_ref, dv_ref, drb_ref = outs
        pair_scr, acc_scr = scr
        _na_bias(rb_ref, pair_scr)
        acc_scr[...] = jnp.zeros_like(acc_scr)
        dk_ref[...] = jnp.zeros_like(dk_ref)
        dv_ref[...] = jnp.zeros_like(dv_ref)

        def group(g, carry):
            rows = [g * NA_GROUP_BWD + t for t in range(NA_GROUP_BWD)]
            wins = [_na_window(k_ref, v_ref, r, n_rows) for r in rows]
            qss = [_stack_heads(q_ref, r, QK_SCALE) for r in rows]
            doss = [_stack_heads(do_ref, r) for r in rows]
            raw = [lax.dot_general(qs, w[0], NT_DIMS, preferred_element_type=F32) for qs, w in zip(qss, wins)]
            dps = [lax.dot_general(dos, w[1], NT_DIMS, preferred_element_type=F32) for dos, w in zip(doss, wins)]
            probs = [_na_probs(s, pair_scr, w[3]) for s, w in zip(raw, wins)]
            dss = [p * (dp - jnp.sum(p * dp, axis=-1, keepdims=True)) for p, dp in zip(probs, dps)]
            dsbs = [ds.astype(BF16) for ds in dss]
            dq2s = [jnp.dot(dsb, w[0], preferred_element_type=F32) for dsb, w in zip(dsbs, wins)]
            dkws = [lax.dot_general(dsb, qs, TN_DIMS, preferred_element_type=F32) for dsb, qs in zip(dsbs, qss)]
            dvws = [lax.dot_general(p.astype(BF16), dos, TN_DIMS, preferred_element_type=F32)
                    for p, dos in zip(probs, doss)]
            for t, r in enumerate(rows):
                _, _, off, ro0 = wins[t]
                for hh in range(2):
                    for j in range(NA_WIN_ROWS // 2):
                        acc_scr[hh, ro0 + 2 * j] += dss[t][hh * GRID_W:(hh + 1) * GRID_W, j * 128:(j + 1) * 128]
                dq_ref[pl.ds(pl.multiple_of(r * GRID_W, GRID_W), GRID_W), :] = (
                    _unstack_heads(dq2s[t]) * QK_SCALE).astype(BF16)
                dk_ref[pl.ds(off, win), :] += dkws[t]
                dv_ref[pl.ds(off, win), :] += dvws[t]
            return carry

        lax.fori_loop(0, n_rows // NA_GROUP_BWD, group, 0)

        qc = lax.broadcasted_iota(jnp.int32, (N_PAIRS * GRID_W, 128), 0)
        for hh in range(2):
            t = acc_scr[hh].reshape(N_PAIRS * GRID_W, 128)
            for b in range(6):
                t = jnp.where(((qc >> b) & 1) == 1, pltpu.roll(t, 128 - (1 << b), 1), t)
            t = pltpu.roll(t, 15, 1)
            drb_ref[hh] = jnp.sum(t.reshape(N_PAIRS, GRID_W, 128), axis=1)

    col = pl.BlockSpec((n, 128), lambda h: (0, h))
    return _call(
        body, name=name, grid=(NA_WIDTH // 128,),
        in_specs=[col, col, col, col, pl.BlockSpec((2, 1, RB_WIDTH), lambda h: (h, 0, 0))],
        out_specs=[col, col, col, pl.BlockSpec((2, N_PAIRS, 128), lambda h: (h, 0, 0))],
        out_shape=[_sds((n, NA_WIDTH), BF16), _sds((n, NA_WIDTH), F32), _sds((n, NA_WIDTH), F32),
                   _sds((8, N_PAIRS, 128), F32)],
        scratch_shapes=[pltpu.VMEM((2, N_PAIRS, GRID_W, 128), F32),
                        pltpu.VMEM((2, N_PAIRS, GRID_W, 128), F32)],
        args=(q, k, v, do, rb))


def _rpb_table(rpb2):
    t = jnp.pad(rpb2, ((0, 0), (0, 1), (0, GRID_W - rpb2.shape[-1])))
    return t.reshape(8, 1, RB_WIDTH)


def _rpb_grad(drb, *, name):
    kdim = drb.shape[1]

    def body(x_ref, o_ref):
        kk = lax.broadcasted_iota(jnp.int32, (128, 512), 0)
        jj = lax.broadcasted_iota(jnp.int32, (128, 512), 1)
        half, co = kk >> 6, kk & 63
        acc = jnp.zeros((8, 512), F32)
        for ro in range(N_PAIRS):
            hit = ((ro + half) == (jj >> 5)) & (co == (jj & 31)) & (co < 31)
            onehot = jnp.where(hit, 1.0, 0.0).astype(F32)
            acc = acc + jnp.dot(x_ref[:, ro * 128:(ro + 1) * 128], onehot, preferred_element_type=F32,
                                precision=lax.Precision.HIGHEST)
        o_ref[...] = acc

    return pl.pallas_call(
        body, name=name, grid=(1,),
        in_specs=[_const((8, kdim))], out_specs=_const((8, 512)), out_shape=_sds((8, 512), F32),
        compiler_params=_params("arbitrary"),
    )(drb)


DIL_GROUP = 2


def _dil_blocks(length):
    qb = min(128, length)
    return qb, min(qb + 2 * DIL_RADIUS, length), min(DIL_GROUP, length // qb)


def _stack_lanes(ref, t, qb, scale=1.0):
    lane = lax.broadcasted_iota(jnp.int32, (qb, 256), 1)
    val = ref[0, t * qb:(t + 1) * qb, :].astype(F32) * scale
    return jnp.concatenate([jnp.where((lane >> 6) == h, val, 0.0) for h in range(4)], axis=0).astype(BF16)


def _dil_window(k_ref, v_ref, blk, qb, win, length):
    start = pl.multiple_of(jnp.clip(blk * qb - DIL_RADIUS, 0, length - win), DIL_RADIUS)
    return k_ref[0, pl.ds(start, win), :], v_ref[0, pl.ds(start, win), :], start


def _dil_mask(s, blk, start, qb, win):
    gap = ((lax.broadcasted_iota(jnp.int32, (4 * qb, win), 0) & (qb - 1))
           - lax.broadcasted_iota(jnp.int32, (4 * qb, win), 1)) + (blk * qb - start)
    return jnp.where(jnp.abs(gap) <= DIL_RADIUS, s, NEG_INF)


def _pick_heads(stacked, qb):
    lane = lax.broadcasted_iota(jnp.int32, (qb, 256), 1)
    out = jnp.zeros((qb, 256), stacked.dtype)
    for h in range(4):
        out = jnp.where((lane >> 6) == h, stacked[h * qb:(h + 1) * qb], out)
    return out


def _stack_head_cols(ref, t, qb):
    return jnp.concatenate([ref[0, t * qb:(t + 1) * qb, 64 * h:64 * h + 1] for h in range(4)], axis=0)


def _dil_fwd(q, k, v, *, name, after=None):
    dil, length, _ = q.shape
    qb, win, grp = _dil_blocks(length)
    extra = [] if after is None else [after]

    def body(q_ref, k_ref, v_ref, *rest):
        o_ref, lse_ref = rest[-2:]
        blks = [pl.program_id(1) * grp + t for t in range(grp)]
        wins = [_dil_window(k_ref, v_ref, b, qb, win, length) for b in blks]
        raw = [lax.dot_general(_stack_lanes(q_ref, t, qb, QK_SCALE), w[0], NT_DIMS, preferred_element_type=F32)
               for t, w in enumerate(wins)]
        lses, outs = [], []
        for t, (s, w) in enumerate(zip(raw, wins)):
            s = _dil_mask(s, blks[t], w[2], qb, win)
            m = jnp.max(s, axis=-1, keepdims=True)
            e = jnp.exp(s - m)
            norm = jnp.sum(e, axis=-1, keepdims=True)
            lses.append(m + jnp.log(norm))
            outs.append(jnp.dot((e * (1.0 / norm)).astype(BF16), w[1], preferred_element_type=F32))
        for t in range(grp):
            o_ref[0, t * qb:(t + 1) * qb, :] = _pick_heads(outs[t], qb)
            lse_ref[0, t * qb:(t + 1) * qb, :] = _pick_heads(jnp.broadcast_to(lses[t], (4 * qb, 256)), qb)

    seq = pl.BlockSpec((1, length, 256), lambda j, i: (j, 0, 0))
    blk = pl.BlockSpec((1, grp * qb, 256), lambda j, i: (j, i, 0))
    return pl.pallas_call(
        body, name=name, grid=(dil, length // (grp * qb)),
        in_specs=[blk, seq, seq] + [pl.BlockSpec(memory_space=pl.ANY)] * len(extra), out_specs=[blk, blk],
        out_shape=[_sds((dil, length, 256), F32)] * 2,
        compiler_params=_params("parallel", "parallel"),
    )(q, k, v, *extra)


def _dil_bwd(q, k, v, do, lse, cc, *, name):
    dil, length, _ = q.shape
    qb, win, grp = _dil_blocks(length)

    def body(q_ref, k_ref, v_ref, do_ref, lse_ref, cc_ref, dq_ref, dk_ref, dv_ref):
        @pl.when(pl.program_id(1) == 0)
        def _():
            dk_ref[...] = jnp.zeros_like(dk_ref)
            dv_ref[...] = jnp.zeros_like(dv_ref)

        blks = [pl.program_id(1) * grp + t for t in range(grp)]
        wins = [_dil_window(k_ref, v_ref, b, qb, win, length) for b in blks]
        qss = [_stack_lanes(q_ref, t, qb, QK_SCALE) for t in range(grp)]
        doss = [_stack_lanes(do_ref, t, qb) for t in range(grp)]
        raw = [lax.dot_general(qs, w[0], NT_DIMS, preferred_element_type=F32) for qs, w in zip(qss, wins)]
        dps = [lax.dot_general(dos, w[1], NT_DIMS, preferred_element_type=F32) for dos, w in zip(doss, wins)]
        probs = [jnp.exp(_dil_mask(s, blks[t], wins[t][2], qb, win) - _stack_head_cols(lse_ref, t, qb))
                 for t, s in enumerate(raw)]
        dsbs = [(p * (dp + _stack_head_cols(cc_ref, t, qb))).astype(BF16)
                for t, (p, dp) in enumerate(zip(probs, dps))]
        dq4s = [jnp.dot(dsb, w[0], preferred_element_type=F32) for dsb, w in zip(dsbs, wins)]
        dkws = [lax.dot_general(dsb, qs, TN_DIMS, preferred_element_type=F32) for dsb, qs in zip(dsbs, qss)]
        dvws = [lax.dot_general(p.astype(BF16), dos, TN_DIMS, preferred_element_type=F32)
                for p, dos in zip(probs, doss)]
        for t in range(grp):
            dq_ref[0, t * qb:(t + 1) * qb, :] = _pick_heads(dq4s[t], qb) * QK_SCALE
            dk_ref[0, pl.ds(wins[t][2], win), :] += dkws[t]
            dv_ref[0, pl.ds(wins[t][2], win), :] += dvws[t]

    seq = pl.BlockSpec((1, length, 256), lambda j, i: (j, 0, 0))
    blk = pl.BlockSpec((1, grp * qb, 256), lambda j, i: (j, i, 0))
    return pl.pallas_call(
        body, name=name, grid=(dil, length // (grp * qb)),
        in_specs=[blk, seq, seq, blk, blk, blk], out_specs=[blk, seq, seq],
        out_shape=[_sds((dil, length, 256), F32)] * 3,
        compiler_params=_params("parallel", "arbitrary"),
    )(q, k, v, do, lse, cc)


def _merge_weights(lses):
    m = jnp.maximum(jnp.maximum(lses[0], lses[1]), lses[2])
    es = [jnp.exp(t - m) for t in lses]
    inv = 1.0 / (es[0] + es[1] + es[2])
    return [e * inv for e in es]


def _dil_merge(outs, lses, *, tm, name):
    n = outs[0].shape[1]

    def body(*refs):
        o_in, l_in = refs[0:3], refs[3:6]
        y_ref, yb_ref, scr = refs[6:9]
        lv = [_load_token_order(l_in[g], scr, d, tm) for g, d in enumerate(DIL_DILATIONS)]
        ws = _merge_weights(lv)
        y = jnp.zeros((tm, 256), F32)
        for g, d in enumerate(DIL_DILATIONS):
            y = y + ws[g] * _load_token_order(o_in[g], scr, d, tm)
        y_ref[...] = y
        yb_ref[...] = y.astype(BF16)

    specs = [_dil_spec(d, tm) for d in DIL_DILATIONS]
    return pl.pallas_call(
        body, name=name, grid=(n // tm,), in_specs=specs + specs,
        out_specs=[_rows(tm, 256)] * 2, out_shape=[_sds((n, 256), F32), _sds((n, 256), BF16)],
        scratch_shapes=[_dil_scratch(tm)],
        compiler_params=_params("parallel"),
    )(*outs, *lses)


def _dil_merge_bwd(dy, y, lses, *, tm, name):
    n = dy.shape[0]

    def body(*refs):
        dy_ref, y_ref = refs[0:2]
        l_in = refs[2:5]
        do_out, cc_out = refs[5:8], refs[8:11]
        scr = refs[11]
        lv = [_load_token_order(l_in[g], scr, d, tm) for g, d in enumerate(DIL_DILATIONS)]
        ws = _merge_weights(lv)
        dyv = dy_ref[...]
        rr = lax.broadcasted_iota(jnp.int32, (256, 256), 0) >> 6
        cc = lax.broadcasted_iota(jnp.int32, (256, 256), 1) >> 6
        ones = jnp.where(rr == cc, 1.0, 0.0).astype(F32)
        tsum = jnp.dot(dyv * y_ref[...], ones, preferred_element_type=F32,
                       precision=lax.Precision.HIGHEST)
        for g, d in enumerate(DIL_DILATIONS):
            _store_dil_order(ws[g] * dyv, do_out[g], scr, d, tm)
            _store_dil_order(-ws[g] * tsum, cc_out[g], scr, d, tm)

    specs = [_dil_spec(d, tm) for d in DIL_DILATIONS]
    res = pl.pallas_call(
        body, name=name, grid=(n // tm,),
        in_specs=[_rows(tm, 256)] * 2 + specs,
        out_specs=specs + specs,
        out_shape=[_sds((d, n // d, 256), BF16) for d in DIL_DILATIONS]
                  + [_sds((d, n // d, 256), F32) for d in DIL_DILATIONS],
        scratch_shapes=[_dil_scratch(tm)],
        compiler_params=_params("parallel"),
    )(dy, y, *lses)
    return res[0:3], res[3:6]


_WEIGHTS = (("w_in", 1, 736), ("w_branch_na", 1, 128), ("w_branch_dil", 1, 128), ("w_out", 0, 128),
            ("w_up", 1, 512), ("w_down", 0, 512), ("w_ple_gate", 0, 128), ("w_ple_proj", 1, 128))
_W_IN, _W_BNA, _W_BD, _W_OUT, _W_UP, _W_DOWN, _W_PG, _W_PP = range(8)


def _to_full(gathered):
    return gathered.reshape(-1, gathered.shape[2])


def _to_chunks(widx, mat):
    return mat.reshape(N_DEV, _WEIGHTS[widx][2], mat.shape[1])


def _local_step(x, p_bf16, positions, target, g_mix, g_mlp, g_ple, g_final, rpb2,
                get_w_in, relay_rest, get_rest, send_grads):
    tm = 256
    half = HEAD_DIM // 2
    inv_freq = 10000.0 ** (-jnp.arange(half, dtype=F32) / half)
    ang = positions.astype(F32)[:, None] * inv_freq
    cos, sin = jnp.cos(ang), jnp.sin(ang)
    cos_t = jnp.tile(jnp.concatenate([cos, cos], axis=-1), (1, 4))
    sin_t = jnp.tile(jnp.concatenate([-sin, sin], axis=-1), (1, 4))
    rb = _rpb_table(rpb2)

    a = _rms_fwd(x, g_mix, tm=tm, name="rms_mix")
    w_in, token = get_w_in(a)
    qkv_width = 3 * NA_WIDTH + 3 * DIL_WIDTH
    proj = _matmul(a, w_in, tb=True, n_limit=qkv_width, out_dtype=F32, tm=512, tn=qkv_width // 2, tk=1024,
                   name="mm_in_qkv", after=token)
    gates = _matmul(a, w_in[qkv_width:], tb=True, out_dtype=BF16, tm=512, tn=2 * D_MODEL, tk=1024,
                    name="mm_in_gates", epilogue=lambda acc: (_sigmoid(acc),))
    sn, sd = (gates, 0), (gates, 1)
    na_qkv, dq_g, dk_g, dv_g = _split_proj(proj, cos_t, sin_t, tm=tm, name="split_proj")
    y_na = _na_fwd(*na_qkv, rb, name="na_fwd")
    token = relay_rest(y_na)
    d_out, d_lse = [], []
    for g in range(3):
        o, lse = _dil_fwd(dq_g[g], dk_g[g], dv_g[g], name=f"dil_fwd{g}", after=token if g == 0 else None)
        d_out.append(o)
        d_lse.append(lse)
    y_dil, y_dil_b = _dil_merge(d_out, d_lse, tm=tm, name="dil_merge")
    w_bna, w_bd, w_out, w_up, w_down, w_pg, w_pp = get_rest(y_dil_b)
    bn = _matmul(y_na, w_bna, tb=True, out_dtype=BF16, tm=512, tn=1024, tk=512, name="mm_bna")
    bd, mixed = _matmul(y_dil_b, w_bd, tb=True, out_dtype=(BF16, BF16), tm=512, tn=1024, tk=256, name="mm_bd",
                        extra=(sn, bn, sd), epilogue=_gate_mix_tile)
    h1, c = _matmul(mixed, w_out, out_dtype=(F32, BF16), tm=512, tn=1024, tk=1024, name="mm_out",
                    extra=(x, g_mlp), epilogue=_residual_rms_tile)
    u, f = _matmul(c, w_up, tb=True, out_dtype=(BF16, BF16), tm=512, tn=2048, tk=1024, name="mm_up",
                   epilogue=lambda acc: (acc, jnp.square(jnp.maximum(acc, 0.0))))
    h2, e = _matmul(f, w_down, out_dtype=(F32, BF16), tm=512, tn=1024, tk=4096, name="mm_down",
                    extra=(h1, g_ple), epilogue=_residual_rms_tile)
    pp = _matmul(p_bf16, w_pp, tb=True, out_dtype=F32, tm=512, tn=1024, tk=256, name="mm_pp")

    dh3, dpp, dgt, dg_final, loss = _matmul(
        e, w_pg, out_dtype=(F32, BF16, BF16), tm=512, tn=1024, tk=1024, name="mm_pg_tail",
        extra=(pp, h2, target, g_final), epilogue=_tail_tile, n_colsum=2)
    loss = loss[:, :128]
    gw_pp = _matmul(p_bf16, dpp, ta=True, transpose_out=True, out_dtype=BF16, tm=256, tn=1024, tk=2048,
                    name="mm_gw_pp")
    gw_pg = _matmul(e, dgt, ta=True, out_dtype=BF16, tm=512, tn=1024, tk=2048, name="mm_gw_pg")
    dh2, dh2_b, dg_ple = _matmul(
        dgt, w_pg, tb=True, out_dtype=(F32, BF16), tm=512, tn=1024, tk=1024, name="mm_de",
        extra=(h2, g_ple, dh3), epilogue=_rms_bwd_twice, n_colsum=1)
    du = _matmul(dh2_b, w_down, tb=True, out_dtype=BF16, tm=512, tn=2048, tk=1024, name="mm_du",
                 extra=(u,), epilogue=lambda acc, uv: (acc * (2.0 * jnp.maximum(uv.astype(F32), 0.0)),))
    gw_down = _matmul(f, dh2_b, ta=True, out_dtype=BF16, tm=1024, tn=1024, tk=2048, name="mm_gw_down")
    gw_up = _matmul(c, du, ta=True, transpose_out=True, out_dtype=BF16, tm=512, tn=2048, tk=2048, name="mm_gw_up")
    dh1, dh1_b, dg_mlp = _matmul(
        du, w_up, out_dtype=(F32, BF16), tm=512, tn=1024, tk=4096, name="mm_dc",
        extra=(h1, g_mlp, dh2), epilogue=_rms_bwd_twice, n_colsum=1)
    dbn, dbd, dgn, dgd = _matmul(dh1_b, w_out, tb=True, out_dtype=(BF16,) * 4, tm=512, tn=1024, tk=1024,
                                 name="mm_dmixed", extra=(sn, bn, sd, bd), epilogue=_gate_bwd_tile)
    gw_out = _matmul(mixed, dh1_b, ta=True, out_dtype=BF16, tm=512, tn=1024, tk=2048, name="mm_gw_out")
    gw_bna = _matmul(y_na, dbn, ta=True, transpose_out=True, out_dtype=BF16, tm=512, tn=1024, tk=2048,
                     name="mm_gw_bna")
    dy_na = _matmul(dbn, w_bna, out_dtype=BF16, tm=512, tn=512, tk=1024, name="mm_dy_na")
    gw_bd = _matmul(y_dil_b, dbd, ta=True, transpose_out=True, out_dtype=BF16, tm=256, tn=1024, tk=2048,
                    name="mm_gw_bd")
    token = send_grads((_W_PP, _W_PG, _W_DOWN, _W_UP, _W_OUT, _W_BNA, _W_BD),
                       (gw_pp, gw_pg, gw_down, gw_up, gw_out, gw_bna, gw_bd))
    dy_dil = _matmul(dbd, w_bd, out_dtype=F32, tm=512, tn=256, tk=1024, name="mm_dy_dil", after=token)
    dna = _na_bwd(*na_qkv, dy_na, rb, name="na_bwd")
    drpb = _rpb_grad(dna[3].reshape(8, -1), name="rpb_grad")
    do_g, cc_g = _dil_merge_bwd(dy_dil, y_dil, d_lse, tm=tm, name="dil_merge_bwd")
    ddq, ddk, ddv = [], [], []
    for g in range(3):
        r = _dil_bwd(dq_g[g], dk_g[g], dv_g[g], do_g[g], d_lse[g], cc_g[g], name=f"dil_bwd{g}")
        ddq.append(r[0])
        ddk.append(r[1])
        ddv.append(r[2])
    dproj = _assemble_dproj(dna[0:3], ddq, ddk, ddv, dgn, dgd, cos_t, sin_t, tm=tm, name="assemble_dproj")
    gw_in = _matmul(a, dproj, ta=True, transpose_out=True, out_dtype=BF16, tm=512, tn=2944, tk=2048, name="mm_gw_in")
    token = send_grads((_W_IN,), (gw_in,))
    dx, dg_mix = _matmul(
        dproj, w_in, out_dtype=(F32,), tm=512, tn=1024, tk=5888, name="mm_da", after=token,
        extra=(x, g_mix, dh1), epilogue=_rms_bwd_tile, n_colsum=1)
    return loss, dx, (dg_mix, dg_mlp, dg_ple, dg_final), drpb


def _cast_bf16(t, *, name):
    def body(t_ref, o_ref):
        o_ref[...] = t_ref[...].astype(BF16)

    rows, cols = t.shape
    tr = 256 if rows % 256 == 0 else rows
    blk = pl.BlockSpec((tr, cols), lambda i: (i, 0))
    return pl.pallas_call(body, name=name, grid=(rows // tr,), in_specs=[blk], out_specs=blk,
                          out_shape=_sds(t.shape, BF16), compiler_params=_params("parallel"))(t)


def _adamw(w, g, m, v):
    m = ADAM_B1 * m + (1.0 - ADAM_B1) * g
    v = ADAM_B2 * v + (1.0 - ADAM_B2) * (g * g)
    m_hat = m / (1.0 - ADAM_B1 ** ADAM_STEP)
    v_hat = v / (1.0 - ADAM_B2 ** ADAM_STEP)
    delta = -ADAM_LR * (m_hat / (jnp.sqrt(v_hat) + ADAM_EPS) + ADAM_WD * w)
    return delta, m, v


def _sum_adamw(parts, w, m, v, *, tr, name, own=None, transposed=False):
    rows, cols = w.shape
    n_pre = 0 if own is None else 1

    def body(*refs):
        p_ref, w_ref, m_ref, v_ref = refs[n_pre:n_pre + 4]
        g_ref, d_ref, nm_ref, nv_ref = refs[-4:]
        g = (p_ref[0] if own is None else refs[n_pre + 4][...]).astype(F32)
        for s in range(1, N_DEV):
            g = g + p_ref[s].astype(F32)
        if transposed:
            g = g.T
        g_ref[...] = g
        d_ref[...], nm_ref[...], nv_ref[...] = _adamw(w_ref[...], g, m_ref[...], v_ref[...])

    if transposed:
        blk = pl.BlockSpec((rows, tr), lambda i, *_: (0, i))
        g_rows, steps = rows, cols // tr
    else:
        blk = pl.BlockSpec((tr, cols), lambda i, *_: (i, 0))
        g_rows, steps = cols, rows // tr
    in_specs = [pl.BlockSpec((N_DEV, tr, g_rows), lambda i, *_: (0, i, 0)), blk, blk, blk]
    args = [parts, w, m, v]
    if own is not None:
        in_specs.append(pl.BlockSpec((None, tr, g_rows), lambda i, idx: (idx[0], i, 0)))
        args = [own[1]] + args + [own[0]]
    return pl.pallas_call(
        body, name=name,
        grid_spec=pltpu.PrefetchScalarGridSpec(num_scalar_prefetch=n_pre, grid=(steps,), in_specs=in_specs,
                                               out_specs=[blk] * 4),
        out_shape=[_sds((rows, cols), F32)] * 4,
        compiler_params=_params("parallel"),
    )(*args)


_RPB_SIZE = 8 * 15 * 31


def _pack_small(g_mix, g_mlp, g_ple, g_final, rpb, loss_row):
    flat = jnp.concatenate([g_mix.reshape(-1), g_mlp.reshape(-1), g_ple.reshape(-1), g_final.reshape(-1),
                            rpb.reshape(-1), jnp.zeros((3840 - _RPB_SIZE,), F32), loss_row.reshape(-1),
                            jnp.zeros((128,), F32)])
    return flat.reshape(64, 128)


def _unpack_small(t):
    flat = t.reshape(-1)
    return (flat[0:1024].reshape(1, 1024), flat[4096:4096 + _RPB_SIZE].reshape(1, 8, 15, 31),
            flat[1024:2048].reshape(1, 1024), flat[2048:3072].reshape(1, 1024), flat[3072:4096])


def kernel(x, p, positions, g_mix, w_in, rpb, w_branch_na, w_branch_dil, w_out, g_mlp, w_up, w_down, g_ple, w_ple_gate, w_ple_proj, g_final, loss_target, m_g_mix, m_w_in, m_rpb, m_w_branch_na, m_w_branch_dil, m_w_out, m_g_mlp, m_w_up, m_w_down, m_g_ple, m_w_ple_gate, m_w_ple_proj, m_g_final, v_g_mix, v_w_in, v_rpb, v_w_branch_na, v_w_branch_dil, v_w_out, v_g_mlp, v_w_up, v_w_down, v_g_ple, v_w_ple_gate, v_w_ple_proj, v_g_final):
    sharded = dict(w_in=(w_in, m_w_in, v_w_in), w_branch_na=(w_branch_na, m_w_branch_na, v_w_branch_na),
                   w_branch_dil=(w_branch_dil, m_w_branch_dil, v_w_branch_dil), w_out=(w_out, m_w_out, v_w_out),
                   w_up=(w_up, m_w_up, v_w_up), w_down=(w_down, m_w_down, v_w_down),
                   w_ple_gate=(w_ple_gate, m_w_ple_gate, v_w_ple_gate),
                   w_ple_proj=(w_ple_proj, m_w_ple_proj, v_w_ple_proj))
    shards = {k: tuple(t[0] for t in val) for k, val in sharded.items()}

    me = _my_index()

    shards["w_in"] = tuple(t.T for t in shards["w_in"])

    w_in_b = _cast_bf16(shards["w_in"][0], name="cast_w_in")
    rest_b = [shards[name][0].astype(BF16).T if axis == 1 else shards[name][0].astype(BF16)
              for name, axis, _ in _WEIGHTS[1:]]
    first_in, token_in = _start_copies(_first_leg_copies, [w_in_b], [_sds((N_DEV,) + w_in_b.shape, BF16)], 4,
                                       name="start_gather_w_in")

    def whole(landed, mine):
        return _to_full(lax.dynamic_update_index_in_dim(landed, mine, me, 0))

    rest = {}

    def get_w_in(after):
        (mine,), landed = _wait_copies(_first_leg_copies, first_in, after, name="wait_gather_w_in")
        second, token = _start_copies(_second_leg_copies, [], landed, 3, name="start_forward_w_in")
        _, (landed,) = _wait_copies(_second_leg_copies, second, token, name="wait_forward_w_in")
        rest["first"], token = _start_copies(_first_leg_copies, rest_b,
                                             [_sds((N_DEV,) + t.shape, BF16) for t in rest_b], 4 * len(rest_b),
                                             name="start_gather_rest", after=landed)
        return whole(landed, mine), token

    def relay_rest(after):
        rest["mine"], landed = _wait_copies(_first_leg_copies, rest["first"], after, name="wait_gather_rest")
        rest["second"], token = _start_copies(_second_leg_copies, [], landed, 3 * len(rest_b),
                                              name="start_forward_rest")
        return token

    def get_rest(after):
        _, landed = _wait_copies(_second_leg_copies, rest["second"], after, name="wait_forward_rest")
        return [whole(t, own) for t, own in zip(landed, rest["mine"])]

    sent = []

    def send_grads(indices, grads):
        chunked = [_to_chunks(i, g) for i, g in zip(indices, grads)]
        handle, token = _start_copies(_exchange_copies, chunked, [_sds(t.shape, BF16) for t in chunked],
                                      7 * len(chunked),
                                      name="start_exchange_" + ("w_in" if indices == (_W_IN,) else "rest"))
        sent.append((indices, handle))
        return token

    g_mix_0 = g_mix + token_in[0:1, 0:1]
    loss, dx, dgs, drpb = _local_step(
        x[0], p[0, 0].astype(BF16), positions[0], loss_target[0],
        g_mix_0, g_mlp, g_ple, g_final.reshape(1, -1), rpb[0], get_w_in, relay_rest, get_rest, send_grads)

    drpb3 = drpb.reshape(8, 16, 32)[:, :15, :31]
    small = _pack_small(dgs[0], dgs[1], dgs[2], dgs[3], drpb3, loss)
    share, done = _start_copies(_gather_copies, [small], [_sds((N_DEV,) + small.shape, F32)], 7,
                                name="start_share_small")

    out = {}
    for indices, handle in sent:
        chunked, landed = _wait_copies(_exchange_copies, handle, done,
                                       name="wait_exchange_" + ("w_in" if indices == (_W_IN,) else "rest"))
        for i, part, mine in zip(indices, landed, chunked):
            name = _WEIGHTS[i][0]
            w, m, v = shards[name]
            turned = _WEIGHTS[i][1] == 1 and i != _W_IN
            res = _sum_adamw(part, w, m, v, tr=368 if i == _W_IN else 128, name="adamw_" + name,
                             own=(mine, me.reshape(1).astype(jnp.int32)), transposed=turned)
            out[name] = [(t.T if i == _W_IN else t)[None] for t in res]
            done = res[0]
    (small,), (small_landed,) = _wait_copies(_gather_copies, share, done, name="wait_share_small")
    small_all = lax.dynamic_update_index_in_dim(small_landed, small, me, 0)
    small_w = _pack_small(g_mix, g_mlp, g_ple, g_final, rpb, jnp.zeros((128,), F32))
    small_m = _pack_small(m_g_mix, m_g_mlp, m_g_ple, m_g_final, m_rpb, jnp.zeros((128,), F32))
    small_v = _pack_small(v_g_mix, v_g_mlp, v_g_ple, v_g_final, v_rpb, jnp.zeros((128,), F32))
    res = _sum_adamw(small_all, small_w, small_m, small_v, tr=64, name="adamw_small")
    unpacked = [_unpack_small(t) for t in res]
    for i, name in enumerate(("g_mix", "rpb", "g_mlp", "g_ple", "g_final")):
        out[name] = [u[i] for u in unpacked]
    loss_total = res[0][62, 0]

    order = ("g_mix", "w_in", "rpb", "w_branch_na", "w_branch_dil", "w_out", "g_mlp", "w_up", "w_down",
             "g_ple", "w_ple_gate", "w_ple_proj", "g_final")
    grads = [out[k][0] for k in order]
    deltas = [out[k][1] for k in order]
    new_m = [out[k][2] for k in order]
    new_v = [out[k][3] for k in order]
    return (loss_total, dx[None], *grads, *deltas, *new_m, *new_v)
```

```python
import jax
import jax.numpy as jnp
from jax import lax
from jax.experimental import pallas as pl
from jax.experimental.pallas import tpu as pltpu

F32 = jnp.float32
BF16 = jnp.bfloat16

D_MODEL = 1024
HEAD_DIM = 64
GRID_W = 64
NA_WIDTH = 512
DIL_WIDTH = 768
IN_WIDTH = 5888
DIL_DILATIONS = (1, 4, 16)
DIL_RADIUS = 64
NA_WIN_ROWS = 8
RMS_EPS = 1e-6
NEG_INF = -1e30
QK_SCALE = HEAD_DIM ** -0.5

ADAM_LR = 0.001
ADAM_B1 = 0.9
ADAM_B2 = 0.999
ADAM_EPS = 1e-08
ADAM_WD = 0.01
ADAM_STEP = 10

N_DEV = 8
VMEM_LIMIT = 56 * 1024 * 1024
EPILOGUE_ROWS = 256
MESH = pl.DeviceIdType.MESH

NT_DIMS = (((1,), (1,)), ((), ()))
TN_DIMS = (((0,), (0,)), ((), ()))


def _sds(shape, dtype):
    return jax.ShapeDtypeStruct(shape, dtype)


def _params(*sem):
    return pltpu.CompilerParams(dimension_semantics=sem, vmem_limit_bytes=VMEM_LIMIT)


def _rows(tm, width, col=0):
    return pl.BlockSpec((tm, width), lambda i, c=col: (i, c))


def _const(shape):
    zeros = (0,) * len(shape)
    return pl.BlockSpec(shape, lambda i: zeros)


def _my_index():
    return 4 * lax.axis_index("x") + 2 * lax.axis_index("y") + lax.axis_index("c")


def _peer(k):
    x, y, c = lax.axis_index("x"), lax.axis_index("y"), lax.axis_index("c")
    px = 1 - x if k & 4 else x
    py = 1 - y if k & 2 else y
    pc = 1 - c if k & 1 else c
    return (px, py, pc), 4 * px + 2 * py + pc


def _call(body, *, name, grid, in_specs, out_specs, out_shape, scratch_shapes, args, after=None):
    n_in, n_out = len(in_specs), len(out_specs)
    extra = [] if after is None else [after]
    n_x = n_in + len(extra)

    def plain(*refs):
        body(refs[:n_in], refs[n_x:n_x + n_out], refs[n_x + n_out:])

    res = pl.pallas_call(plain, name=name, grid=grid,
                         in_specs=list(in_specs) + [pl.BlockSpec(memory_space=pl.ANY)] * len(extra),
                         out_specs=out_specs, out_shape=out_shape, scratch_shapes=scratch_shapes,
                         compiler_params=_params(*(("arbitrary",) * len(grid))))(*args, *extra)
    return list(res)


_HBM_SPEC = pl.BlockSpec(memory_space=pltpu.HBM)
_SEM_SPEC = pl.BlockSpec(memory_space=pltpu.SEMAPHORE)
_SIDE_EFFECT = pltpu.SideEffectType.DATAFLOW_SIDE_EFFECTING


_FIRST_LEG = (1, 2, 4, 6)
_SECOND_LEG = (2, 4, 6)


def _gather_copies(srcs, lands, send, recv, sending):
    me = _my_index()
    out = []
    for w in range(len(srcs)):
        for k in range(1, N_DEV):
            dev, idx = _peer(k)
            out.append(pltpu.make_async_remote_copy(
                src_ref=srcs[w], dst_ref=lands[w].at[me if sending else idx],
                send_sem=send.at[w * 7 + k - 1], recv_sem=recv.at[w * 7 + k - 1],
                device_id=dev, device_id_type=MESH))
    return out


def _first_leg_copies(srcs, lands, send, recv, sending):
    me = _my_index()
    out = []
    for w in range(len(srcs)):
        for j, k in enumerate(_FIRST_LEG):
            dev, idx = _peer(k)
            out.append(pltpu.make_async_remote_copy(
                src_ref=srcs[w], dst_ref=lands[w].at[me if sending else idx],
                send_sem=send.at[w * 4 + j], recv_sem=recv.at[w * 4 + j],
                device_id=dev, device_id_type=MESH))
    return out


def _second_leg_copies(srcs, lands, send, recv, sending):
    sibling, _ = _peer(1)
    out = []
    for w in range(len(lands)):
        for j, k in enumerate(_SECOND_LEG):
            slot = _peer(k if sending else k ^ 1)[1]
            out.append(pltpu.make_async_remote_copy(
                src_ref=lands[w].at[slot], dst_ref=lands[w].at[slot],
                send_sem=send.at[w * 3 + j], recv_sem=recv.at[w * 3 + j],
                device_id=sibling, device_id_type=MESH))
    return out


def _exchange_copies(srcs, lands, send, recv, sending):
    out = []
    for w in range(len(srcs)):
        for k in range(1, N_DEV):
            dev, idx = _peer(k)
            out.append(pltpu.make_async_remote_copy(
                src_ref=srcs[w].at[idx], dst_ref=lands[w].at[k],
                send_sem=send.at[w * 7 + k - 1], recv_sem=recv.at[w * 7 + k - 1],
                device_id=dev, device_id_type=MESH))
    return out


def _start_copies(make, srcs, lands, n_copies, *, name, after=None):
    n_src, n_buf = len(srcs), len(srcs) + len(lands)
    extra = [] if after is None else [after]

    def body(*refs):
        send, recv = refs[n_buf + len(extra)], refs[n_buf + len(extra) + 1]
        for cp in make(refs[:n_src], refs[n_src:n_buf], send, recv, True):
            cp.start()
        refs[-1][...] = jnp.zeros_like(refs[-1])

    bufs = list(srcs) + [lax.empty(t.shape, t.dtype) if isinstance(t, jax.ShapeDtypeStruct) else t for t in lands]
    res = pl.pallas_call(
        body, name=name,
        out_shape=(pltpu.SemaphoreType.DMA((n_copies,)), pltpu.SemaphoreType.DMA((n_copies,)),
                   *[pltpu.HBM(t.shape, t.dtype) for t in bufs], _sds((8, 128), F32)),
        in_specs=[_HBM_SPEC] * n_buf + [pl.BlockSpec(memory_space=pl.ANY)] * len(extra),
        out_specs=(_SEM_SPEC, _SEM_SPEC, *([_HBM_SPEC] * n_buf), pl.BlockSpec(memory_space=pltpu.VMEM)),
        input_output_aliases={i: 2 + i for i in range(n_buf)},
        compiler_params=pltpu.CompilerParams(has_side_effects=_SIDE_EFFECT),
    )(*[pltpu.with_memory_space_constraint(t, pltpu.HBM) for t in bufs], *extra)
    return (n_src, res[0], res[1], res[2:2 + n_buf]), res[-1]


def _wait_copies(make, handle, after, *, name):
    n_src, send_sems, recv_sems, bufs = handle
    n_buf = len(bufs)

    def body(*refs):
        for cp in make(refs[:n_src], refs[n_src:n_buf], refs[n_buf], refs[n_buf + 1], False):
            cp.wait_send()
            cp.wait_recv()

    res = pl.pallas_call(
        body, name=name,
        out_shape=tuple(pltpu.HBM(t.shape, t.dtype) for t in bufs),
        in_specs=[_HBM_SPEC] * n_buf + [_SEM_SPEC, _SEM_SPEC, pl.BlockSpec(memory_space=pl.ANY)],
        out_specs=tuple([_HBM_SPEC] * n_buf),
        input_output_aliases={i: i for i in range(n_buf)},
        compiler_params=pltpu.CompilerParams(has_side_effects=_SIDE_EFFECT),
    )(*bufs, send_sems, recv_sems, after)
    return list(res[:n_src]), list(res[n_src:])


def _matmul(a, b, *, ta=False, tb=False, out_dtype, tm, tn, tk, name, after=None, extra=(), epilogue=None,
            n_colsum=0, transpose_out=False, n_limit=None):
    m, k = (a.shape[1], a.shape[0]) if ta else a.shape
    n = n_limit or (b.shape[0] if tb else b.shape[1])
    tm, tn, tk = min(tm, m), min(tn, n), min(tk, k)
    nk = k // tk
    dims = (((0 if ta else 1,), (1 if tb else 0,)), ((), ()))
    out_dtypes = out_dtype if isinstance(out_dtype, tuple) else (out_dtype,)
    n_tiles = len(out_dtypes)

    def add_colsums(o_refs, sums):
        i = pl.program_id(1)
        for s_ref, val in zip(o_refs[n_tiles:], sums):
            @pl.when(i == 0)
            def _(s_ref=s_ref, val=val):
                s_ref[...] = val

            @pl.when(i > 0)
            def _(s_ref=s_ref, val=val):
                s_ref[...] += val

    def finish(acc, x_refs, o_refs):
        vals = (acc,) if epilogue is None else epilogue(acc, *[r[...] for r in x_refs])
        for o_ref, val in zip(o_refs[:n_tiles], vals[:n_tiles]):
            o_ref[...] = (val.T if transpose_out else val).astype(o_ref.dtype)
        add_colsums(o_refs, vals[n_tiles:])

    chunk = EPILOGUE_ROWS if (nk == 1 and epilogue is not None and not ta and tm % EPILOGUE_ROWS == 0) else None

    def body(ins, outs, acc):
        a_ref, b_ref = ins[:2]
        if chunk is not None:
            sums = None
            for r0 in range(0, tm, chunk):
                part = lax.dot_general(a_ref[r0:r0 + chunk, :], b_ref[...], dims, preferred_element_type=F32)
                vals = epilogue(part, *[r[...] if r.shape[0] == 1 else r[r0:r0 + chunk, :] for r in ins[2:]])
                for o_ref, val in zip(outs[:n_tiles], vals[:n_tiles]):
                    o_ref[r0:r0 + chunk, :] = val.astype(o_ref.dtype)
                sums = vals[n_tiles:] if sums is None else [s + v for s, v in zip(sums, vals[n_tiles:])]
            add_colsums(outs, sums)
            return
        part = lax.dot_general(a_ref[...], b_ref[...], dims, preferred_element_type=F32)
        if nk == 1:
            finish(part, ins[2:], outs)
            return
        acc_ref, = acc
        kk = pl.program_id(2)

        @pl.when(kk == 0)
        def _():
            acc_ref[...] = part

        @pl.when(kk > 0)
        def _():
            acc_ref[...] += part

        @pl.when(kk == nk - 1)
        def _():
            finish(acc_ref[...], ins[2:], outs)

    a_spec = (pl.BlockSpec((tk, tm), lambda j, i, kk: (kk, i)) if ta
              else pl.BlockSpec((tm, tk), lambda j, i, kk: (i, kk)))
    b_spec = (pl.BlockSpec((tn, tk), lambda j, i, kk: (j, kk)) if tb
              else pl.BlockSpec((tk, tn), lambda j, i, kk: (kk, j)))
    tile = pl.BlockSpec((tm, tn), lambda j, i, kk: (i, j))
    row = pl.BlockSpec((1, tn), lambda j, i, kk: (0, j))

    def x_spec(t):
        if isinstance(t, tuple):
            return pl.BlockSpec((tm, tn), lambda j, i, kk, first=t[1] * (n // tn): (i, first + j))
        return row if t.shape[0] == 1 else tile

    out_tile, out_dims = (pl.BlockSpec((tn, tm), lambda j, i, kk: (j, i)), (n, m)) if transpose_out else (tile, (m, n))
    res = _call(
        body, name=name, grid=(n // tn, m // tm, nk),
        in_specs=[a_spec, b_spec] + [x_spec(t) for t in extra],
        out_specs=[out_tile] * n_tiles + [row] * n_colsum,
        out_shape=[_sds(out_dims, dt) for dt in out_dtypes] + [_sds((1, n), F32)] * n_colsum,
        scratch_shapes=[] if nk == 1 else [pltpu.VMEM((tm, tn), F32)],
        args=(a, b, *[t[0] if isinstance(t, tuple) else t for t in extra]), after=after)
    return res if isinstance(out_dtype, tuple) or n_colsum else res[0]


def _rstd(h):
    return lax.rsqrt(jnp.mean(h * h, axis=-1, keepdims=True) + RMS_EPS)


def _sigmoid(z):
    return 1.0 / (1.0 + jnp.exp(-z))


def _rms_fwd(x, g, *, tm, name):
    n = x.shape[0]

    def body(x_ref, g_ref, o_ref):
        h = x_ref[...]
        o_ref[...] = (h * _rstd(h) * g_ref[...]).astype(BF16)

    return pl.pallas_call(
        body, name=name, grid=(n // tm,),
        in_specs=[_rows(tm, D_MODEL), _const((1, D_MODEL))],
        out_specs=_rows(tm, D_MODEL), out_shape=_sds((n, D_MODEL), BF16),
        compiler_params=_params("parallel"),
    )(x, g)


def _swap_halves(t):
    lane = lax.broadcasted_iota(jnp.int32, (t.shape[0], 128), 1)
    pieces = [t[:, c:c + 128] for c in range(0, t.shape[1], 128)]
    return jnp.concatenate([jnp.where((lane & 63) < 32, pltpu.roll(h, 96, 1), pltpu.roll(h, 32, 1))
                            for h in pieces], axis=1)


def _dil_spec(dil, tm):
    return pl.BlockSpec((dil, tm // dil, 256), lambda i: (0, i, 0))


def _dil_scratch(tm):
    return pltpu.VMEM((2, tm, 128), F32)


def _load_token_order(src, scr, dil, tm):
    if dil == 1:
        return src[0]
    for j in range(dil):
        for c in range(2):
            scr[c, pl.ds(j, tm // dil, stride=dil), :] = src[j, :, c * 128:(c + 1) * 128]
    return jnp.concatenate([scr[0], scr[1]], axis=1)


def _store_dil_order(val, dst, scr, dil, tm):
    if dil == 1:
        dst[0] = val.astype(dst.dtype)
        return
    for c in range(2):
        scr[c] = val[:, c * 128:(c + 1) * 128]
    for j in range(dil):
        for c in range(2):
            dst[j, :, c * 128:(c + 1) * 128] = scr[c, pl.ds(j, tm // dil, stride=dil), :].astype(dst.dtype)


def _split_proj(proj, cos_t, sin_t, *, tm, name):
    n = proj.shape[0]
    n_dil = len(DIL_DILATIONS)

    def body(*refs):
        dil_in = refs[0:3 * n_dil]
        cos_ref, sin_ref = refs[9:11]
        dil_out = refs[11:20]
        scr = refs[20]
        cosv, sinv = cos_ref[...], sin_ref[...]
        for t in range(3):
            for gi, dil in enumerate(DIL_DILATIONS):
                val = dil_in[t * n_dil + gi][...]
                if t < 2:
                    val = val * cosv + _swap_halves(val) * sinv
                _store_dil_order(val, dil_out[t * n_dil + gi], scr, dil, tm)

    in_specs = [_rows(tm, 256, c) for c in range(9)] + [_rows(tm, 256), _rows(tm, 256)]
    out_specs, out_shape = [], []
    for _ in range(3):
        for dil in DIL_DILATIONS:
            out_specs.append(pl.BlockSpec((dil, tm // dil, 256), lambda i: (0, i, 0)))
            out_shape.append(_sds((dil, n // dil, 256), BF16))
    res = pl.pallas_call(
        body, name=name, grid=(n // tm,),
        in_specs=in_specs, out_specs=out_specs, out_shape=out_shape,
        scratch_shapes=[_dil_scratch(tm)],
        compiler_params=_params("parallel"),
    )(*([proj] * 9), cos_t, sin_t)
    return res[0:3], res[3:6], res[6:9]


def _residual_rms_tile(delta, h, g):
    hn = h + delta
    return hn, hn * _rstd(hn) * g


def _gate_mix_tile(b2, s1, b1, s2):
    return b2, s1.astype(F32) * b1.astype(F32) + s2.astype(F32) * b2


def _gate_bwd_tile(dm, s1, b1, s2, b2):
    s1, b1, s2, b2 = (t.astype(F32) for t in (s1, b1, s2, b2))
    return dm * s1, dm * s2, dm * b1 * s1 * (1.0 - s1), dm * b2 * s2 * (1.0 - s2)


def _tail_tile(gt, pp, h2, target, g):
    sg = _sigmoid(gt)
    h3 = h2 + sg * pp
    r3 = _rstd(h3)
    n3 = h3 * r3
    err = n3 * g - target
    loss = 0.5 * jnp.sum(jnp.sum(err * err, axis=-1, keepdims=True) / D_MODEL)
    dy = err / D_MODEL
    dn = dy * g
    dh3 = r3 * (dn - n3 * jnp.mean(dn * n3, axis=-1, keepdims=True))
    return (dh3, dh3 * sg, dh3 * pp * sg * (1.0 - sg),
            jnp.sum(dy * n3, axis=0, keepdims=True), jnp.full((1, gt.shape[1]), loss, F32))


def _rms_bwd_tile(dz, h, g, dres):
    r = _rstd(h)
    nrm = h * r
    dn = dz * g
    dh = dres + r * (dn - nrm * jnp.mean(dn * nrm, axis=-1, keepdims=True))
    return dh, jnp.sum(dz * nrm, axis=0, keepdims=True)


def _rms_bwd_twice(dz, h, g, dres):
    dh, dg = _rms_bwd_tile(dz, h, g, dres)
    return dh, dh, dg


def _assemble_dproj(dna, ddil_q, ddil_k, ddil_v, dgn, dgd, cos_t, sin_t, *, tm, name):
    n = dgn.shape[0]

    def body(*refs):
        dq_ref, dk_ref, dv_ref = refs[0:3]
        dil_in = refs[3:12]
        dgn_ref, dgd_ref, cos_ref, sin_ref, o_ref, scr = refs[12:18]
        o_ref[:, 0:512] = dq_ref[...]
        o_ref[:, 512:1024] = dk_ref[...].astype(BF16)
        o_ref[:, 1024:1536] = dv_ref[...].astype(BF16)
        cosv, sinv = cos_ref[...], sin_ref[...]
        for t in range(3):
            for gi, dil in enumerate(DIL_DILATIONS):
                val = _load_token_order(dil_in[t * 3 + gi], scr, dil, tm)
                if t < 2:
                    val = val * cosv + _swap_halves(val * sinv)
                c0 = 1536 + t * DIL_WIDTH + gi * 256
                o_ref[:, c0:c0 + 256] = val.astype(BF16)
        o_ref[:, 3840:4864] = dgn_ref[...]
        o_ref[:, 4864:5888] = dgd_ref[...]

    in_specs = [_rows(tm, NA_WIDTH)] * 3
    for _ in range(3):
        for dil in DIL_DILATIONS:
            in_specs.append(pl.BlockSpec((dil, tm // dil, 256), lambda i: (0, i, 0)))
    in_specs += [_rows(tm, D_MODEL)] * 2 + [_rows(tm, 256)] * 2
    return pl.pallas_call(
        body, name=name, grid=(n // tm,), in_specs=in_specs,
        out_specs=_rows(tm, IN_WIDTH), out_shape=_sds((n, IN_WIDTH), BF16),
        scratch_shapes=[_dil_scratch(tm)],
        compiler_params=_params("parallel"),
    )(*dna, *ddil_q, *ddil_k, *ddil_v, dgn, dgd, cos_t, sin_t)


N_ROW_OFF = 2 * NA_WIN_ROWS - 1
N_PAIRS = N_ROW_OFF - 1
RB_WIDTH = (N_ROW_OFF + 1) * GRID_W


def _na_bias(rb_ref, pair_scr):
    shape = (GRID_W, RB_WIDTH)
    qc = lax.broadcasted_iota(jnp.int32, shape, 0)
    qc2 = lax.broadcasted_iota(jnp.int32, (GRID_W, 128), 0)
    kc2 = lax.broadcasted_iota(jnp.int32, (GRID_W, 128), 1) & (GRID_W - 1)
    cs = jnp.clip(qc2 - 8, 0, GRID_W - 16)
    valid = (kc2 >= cs) & (kc2 < cs + 16)
    for hh in range(2):
        t = jnp.broadcast_to(rb_ref[hh], shape)
        t = pltpu.roll(t, RB_WIDTH - 15, 1)
        for b in range(6):
            t = jnp.where(((qc >> b) & 1) == 1, pltpu.roll(t, 1 << b, 1), t)
        t_odd = pltpu.roll(t, RB_WIDTH - GRID_W, 1)
        for ro in range(N_PAIRS):
            src = t if ro % 2 == 0 else t_odd
            base = (ro // 2) * 128
            pair_scr[hh, ro] = jnp.where(valid, src[:, base:base + 128], NEG_INF)


NA_GROUP_FWD = 8
NA_GROUP_BWD = 4


def _stack_heads(ref, r, scale=1.0):
    lane = lax.broadcasted_iota(jnp.int32, (GRID_W, 128), 1)
    t = ref[pl.ds(pl.multiple_of(r * GRID_W, GRID_W), GRID_W), :].astype(F32) * scale
    return jnp.concatenate([jnp.where(lane < 64, t, 0.0), jnp.where(lane >= 64, t, 0.0)], axis=0).astype(BF16)


def _unstack_heads(t2):
    lane = lax.broadcasted_iota(jnp.int32, (GRID_W, 128), 1)
    return jnp.where(lane < 64, t2[:GRID_W], t2[GRID_W:])


def _na_window(k_ref, v_ref, r, n_rows):
    rs = jnp.clip(r - NA_WIN_ROWS // 2, 0, n_rows - NA_WIN_ROWS)
    ro0 = (NA_WIN_ROWS - 1) - (r - rs)
    off = pl.multiple_of(rs * GRID_W, GRID_W)
    kw = k_ref[pl.ds(off, NA_WIN_ROWS * GRID_W), :]
    vw = v_ref[pl.ds(off, NA_WIN_ROWS * GRID_W), :]
    return kw, vw, off, ro0


def _na_probs(s_raw, pair_scr, ro0):
    bias = [jnp.concatenate([pair_scr[hh, ro0 + 2 * j] for j in range(NA_WIN_ROWS // 2)], axis=1)
            for hh in range(2)]
    s = s_raw + jnp.concatenate(bias, axis=0)
    m = jnp.max(s, axis=-1, keepdims=True)
    e = jnp.exp(s - m)
    return e * (1.0 / jnp.sum(e, axis=-1, keepdims=True))


def _na_qkv_specs(n):
    pairs = NA_WIDTH // 128
    return [pl.BlockSpec((n, 128), lambda h, first=t * pairs: (0, first + h)) for t in range(3)]


def _na_fwd(qkv, rb, *, name):
    n = qkv.shape[0]
    n_rows = n // GRID_W

    def body(ins, outs, scr):
        q_ref, k_ref, v_ref, rb_ref = ins
        o_ref, = outs
        pair_scr, = scr
        _na_bias(rb_ref, pair_scr)

        def group(g, carry):
            rows = [g * NA_GROUP_FWD + t for t in range(NA_GROUP_FWD)]
            wins = [_na_window(k_ref, v_ref, r, n_rows) for r in rows]
            raw = [lax.dot_general(_stack_heads(q_ref, r, QK_SCALE), w[0], NT_DIMS, preferred_element_type=F32)
                   for r, w in zip(rows, wins)]
            probs = [_na_probs(s, pair_scr, w[3]) for s, w in zip(raw, wins)]
            outs2 = [jnp.dot(p.astype(BF16), w[1], preferred_element_type=F32) for p, w in zip(probs, wins)]
            for r, o2 in zip(rows, outs2):
                o_ref[pl.ds(pl.multiple_of(r * GRID_W, GRID_W), GRID_W), :] = _unstack_heads(o2).astype(BF16)
            return carry

        lax.fori_loop(0, n_rows // NA_GROUP_FWD, group, 0)

    col = pl.BlockSpec((n, 128), lambda h: (0, h))
    return _call(
        body, name=name, grid=(NA_WIDTH // 128,),
        in_specs=_na_qkv_specs(n) + [pl.BlockSpec((2, 1, RB_WIDTH), lambda h: (h, 0, 0))],
        out_specs=[col], out_shape=[_sds((n, NA_WIDTH), BF16)],
        scratch_shapes=[pltpu.VMEM((2, N_PAIRS, GRID_W, 128), F32)],
        args=(qkv, qkv, qkv, rb))[0]


def _na_bwd(qkv, do, rb, *, name):
    n = qkv.shape[0]
    n_rows = n // GRID_W
    win = NA_WIN_ROWS * GRID_W

    def body(ins, outs, scr):
        q_ref, k_ref, v_ref, do_ref, rb_ref = ins
        dq_ref, dk_ref, dv_ref, drb_ref = outs
        pair_scr, acc_scr = scr
        _na_bias(rb_ref, pair_scr)
        acc_scr[...] = jnp.zeros_like(acc_scr)
        dk_ref[...] = jnp.zeros_like(dk_ref)
        dv_ref[...] = jnp.zeros_like(dv_ref)

        def group(g, carry):
            rows = [g * NA_GROUP_BWD + t for t in range(NA_GROUP_BWD)]
            wins = [_na_window(k_ref, v_ref, r, n_rows) for r in rows]
            qss = [_stack_heads(q_ref, r, QK_SCALE) for r in rows]
            doss = [_stack_heads(do_ref, r) for r in rows]
            raw = [lax.dot_general(qs, w[0], NT_DIMS, preferred_element_type=F32) for qs, w in zip(qss, wins)]
            dps = [lax.dot_general(dos, w[1], NT_DIMS, preferred_element_type=F32) for dos, w in zip(doss, wins)]
            probs = [_na_probs(s, pair_scr, w[3]) for s, w in zip(raw, wins)]
            dss = [p * (dp - jnp.sum(p * dp, axis=-1, keepdims=True)) for p, dp in zip(probs, dps)]
            dsbs = [ds.astype(BF16) for ds in dss]
            dq2s = [jnp.dot(dsb, w[0], preferred_element_type=F32) for dsb, w in zip(dsbs, wins)]
            dkws = [lax.dot_general(dsb, qs, TN_DIMS, preferred_element_type=F32) for dsb, qs in zip(dsbs, qss)]
            dvws = [lax.dot_general(p.astype(BF16), dos, TN_DIMS, preferred_element_type=F32)
                    for p, dos in zip(probs, doss)]
            for t, r in enumerate(rows):
                _, _, off, ro0 = wins[t]
                for hh in range(2):
                    for j in range(NA_WIN_ROWS // 2):
                        acc_scr[hh, ro0 + 2 * j] += dss[t][hh * GRID_W:(hh + 1) * GRID_W, j * 128:(j + 1) * 128]
                dq_ref[pl.ds(pl.multiple_of(r * GRID_W, GRID_W), GRID_W), :] = (
                    _unstack_heads(dq2s[t]) * QK_SCALE).astype(BF16)
                dk_ref[pl.ds(off, win), :] += dkws[t]
                dv_ref[pl.ds(off, win), :] += dvws[t]
            return carry

        lax.fori_loop(0, n_rows // NA_GROUP_BWD, group, 0)

        qc = lax.broadcasted_iota(jnp.int32, (N_PAIRS * GRID_W, 128), 0)
        for hh in range(2):
            t = acc_scr[hh].reshape(N_PAIRS * GRID_W, 128)
            for b in range(6):
                t = jnp.where(((qc >> b) & 1) == 1, pltpu.roll(t, 128 - (1 << b), 1), t)
            t = pltpu.roll(t, 15, 1)
            drb_ref[hh] = jnp.sum(t.reshape(N_PAIRS, GRID_W, 128), axis=1)

    col = pl.BlockSpec((n, 128), lambda h: (0, h))
    return _call(
        body, name=name, grid=(NA_WIDTH // 128,),
        in_specs=_na_qkv_specs(n) + [col, pl.BlockSpec((2, 1, RB_WIDTH), lambda h: (h, 0, 0))],
        out_specs=[col, col, col, pl.BlockSpec((2, N_PAIRS, 128), lambda h: (h, 0, 0))],
        out_shape=[_sds((n, NA_WIDTH), BF16), _sds((n, NA_WIDTH), F32), _sds((n, NA_WIDTH), F32),
                   _sds((8, N_PAIRS, 128), F32)],
        scratch_shapes=[pltpu.VMEM((2, N_PAIRS, GRID_W, 128), F32),
                        pltpu.VMEM((2, N_PAIRS, GRID_W, 128), F32)],
        args=(qkv, qkv, qkv, do, rb))


def _rpb_table(rpb2):
    t = jnp.pad(rpb2, ((0, 0), (0, 1), (0, GRID_W - rpb2.shape[-1])))
    return t.reshape(8, 1, RB_WIDTH)


def _rpb_grad(drb, *, name):
    kdim = drb.shape[1]

    def body(x_ref, o_ref):
        kk = lax.broadcasted_iota(jnp.int32, (128, 512), 0)
        jj = lax.broadcasted_iota(jnp.int32, (128, 512), 1)
        half, co = kk >> 6, kk & 63
        acc = jnp.zeros((8, 512), F32)
        for ro in range(N_PAIRS):
            hit = ((ro + half) == (jj >> 5)) & (co == (jj & 31)) & (co < 31)
            onehot = jnp.where(hit, 1.0, 0.0).astype(F32)
            acc = acc + jnp.dot(x_ref[:, ro * 128:(ro + 1) * 128], onehot, preferred_element_type=F32,
                                precision=lax.Precision.HIGHEST)
        o_ref[...] = acc

    return pl.pallas_call(
        body, name=name, grid=(1,),
        in_specs=[_const((8, kdim))], out_specs=_const((8, 512)), out_shape=_sds((8, 512), F32),
        compiler_params=_params("arbitrary"),
    )(drb)


DIL_GROUP = 2


def _dil_blocks(length):
    qb = min(128, length)
    return qb, min(qb + 2 * DIL_RADIUS, length), min(DIL_GROUP, length // qb)


def _stack_lanes(ref, t, qb, scale=1.0):
    lane = lax.broadcasted_iota(jnp.int32, (qb, 256), 1)
    val = ref[0, t * qb:(t + 1) * qb, :].astype(F32) * scale
    return jnp.concatenate([jnp.where((lane >> 6) == h, val, 0.0) for h in range(4)], axis=0).astype(BF16)


def _dil_window(k_ref, v_ref, blk, qb, win, length):
    start = pl.multiple_of(jnp.clip(blk * qb - DIL_RADIUS, 0, length - win), DIL_RADIUS)
    return k_ref[0, pl.ds(start, win), :], v_ref[0, pl.ds(start, win), :], start


def _dil_mask(s, blk, start, qb, win):
    gap = ((lax.broadcasted_iota(jnp.int32, (4 * qb, win), 0) & (qb - 1))
           - lax.broadcasted_iota(jnp.int32, (4 * qb, win), 1)) + (blk * qb - start)
    return jnp.where(jnp.abs(gap) <= DIL_RADIUS, s, NEG_INF)


def _pick_heads(stacked, qb):
    lane = lax.broadcasted_iota(jnp.int32, (qb, 256), 1)
    out = jnp.zeros((qb, 256), stacked.dtype)
    for h in range(4):
        out = jnp.where((lane >> 6) == h, stacked[h * qb:(h + 1) * qb], out)
    return out


def _stack_head_cols(ref, t, qb):
    return jnp.concatenate([ref[0, t * qb:(t + 1) * qb, 64 * h:64 * h + 1] for h in range(4)], axis=0)


def _dil_fwd(q, k, v, *, name, after=None):
    dil, length, _ = q.shape
    qb, win, grp = _dil_blocks(length)
    extra = [] if after is None else [after]

    def body(q_ref, k_ref, v_ref, *rest):
        o_ref, lse_ref = rest[-2:]
        blks = [pl.program_id(1) * grp + t for t in range(grp)]
        wins = [_dil_window(k_ref, v_ref, b, qb, win, length) for b in blks]
        raw = [lax.dot_general(_stack_lanes(q_ref, t, qb, QK_SCALE), w[0], NT_DIMS, preferred_element_type=F32)
               for t, w in enumerate(wins)]
        lses, outs = [], []
        for t, (s, w) in enumerate(zip(raw, wins)):
            s = _dil_mask(s, blks[t], w[2], qb, win)
            m = jnp.max(s, axis=-1, keepdims=True)
            e = jnp.exp(s - m)
            norm = jnp.sum(e, axis=-1, keepdims=True)
            lses.append(m + jnp.log(norm))
            outs.append(jnp.dot((e * (1.0 / norm)).astype(BF16), w[1], preferred_element_type=F32))
        for t in range(grp):
            o_ref[0, t * qb:(t + 1) * qb, :] = _pick_heads(outs[t], qb)
            lse_ref[0, t * qb:(t + 1) * qb, :] = _pick_heads(jnp.broadcast_to(lses[t], (4 * qb, 256)), qb)

    seq = pl.BlockSpec((1, length, 256), lambda j, i: (j, 0, 0))
    blk = pl.BlockSpec((1, grp * qb, 256), lambda j, i: (j, i, 0))
    return pl.pallas_call(
        body, name=name, grid=(dil, length // (grp * qb)),
        in_specs=[blk, seq, seq] + [pl.BlockSpec(memory_space=pl.ANY)] * len(extra), out_specs=[blk, blk],
        out_shape=[_sds((dil, length, 256), F32)] * 2,
        compiler_params=_params("parallel", "parallel"),
    )(q, k, v, *extra)


def _dil_bwd(q, k, v, do, lse, cc, *, name):
    dil, length, _ = q.shape
    qb, win, grp = _dil_blocks(length)

    def body(q_ref, k_ref, v_ref, do_ref, lse_ref, cc_ref, dq_ref, dk_ref, dv_ref):
        @pl.when(pl.program_id(1) == 0)
        def _():
            dk_ref[...] = jnp.zeros_like(dk_ref)
            dv_ref[...] = jnp.zeros_like(dv_ref)

        blks = [pl.program_id(1) * grp + t for t in range(grp)]
        wins = [_dil_window(k_ref, v_ref, b, qb, win, length) for b in blks]
        qss = [_stack_lanes(q_ref, t, qb, QK_SCALE) for t in range(grp)]
        doss = [_stack_lanes(do_ref, t, qb) for t in range(grp)]
        raw = [lax.dot_general(qs, w[0], NT_DIMS, preferred_element_type=F32) for qs, w in zip(qss, wins)]
        dps = [lax.dot_general(dos, w[1], NT_DIMS, preferred_element_type=F32) for dos, w in zip(doss, wins)]
        probs = [jnp.exp(_dil_mask(s, blks[t], wins[t][2], qb, win) - _stack_head_cols(lse_ref, t, qb))
                 for t, s in enumerate(raw)]
        dsbs = [(p * (dp + _stack_head_cols(cc_ref, t, qb))).astype(BF16)
                for t, (p, dp) in enumerate(zip(probs, dps))]
        dq4s = [jnp.dot(dsb, w[0], preferred_element_type=F32) for dsb, w in zip(dsbs, wins)]
        dkws = [lax.dot_general(dsb, qs, TN_DIMS, preferred_element_type=F32) for dsb, qs in zip(dsbs, qss)]
        dvws = [lax.dot_general(p.astype(BF16), dos, TN_DIMS, preferred_element_type=F32)
                for p, dos in zip(probs, doss)]
        for t in range(grp):
            dq_ref[0, t * qb:(t + 1) * qb, :] = _pick_heads(dq4s[t], qb) * QK_SCALE
            dk_ref[0, pl.ds(wins[t][2], win), :] += dkws[t]
            dv_ref[0, pl.ds(wins[t][2], win), :] += dvws[t]

    seq = pl.BlockSpec((1, length, 256), lambda j, i: (j, 0, 0))
    blk = pl.BlockSpec((1, grp * qb, 256), lambda j, i: (j, i, 0))
    return pl.pallas_call(
        body, name=name, grid=(dil, length // (grp * qb)),
        in_specs=[blk, seq, seq, blk, blk, blk], out_specs=[blk, seq, seq],
        out_shape=[_sds((dil, length, 256), F32)] * 3,
        compiler_params=_params("parallel", "arbitrary"),
    )(q, k, v, do, lse, cc)


def _merge_weights(lses):
    m = jnp.maximum(jnp.maximum(lses[0], lses[1]), lses[2])
    es = [jnp.exp(t - m) for t in lses]
    inv = 1.0 / (es[0] + es[1] + es[2])
    return [e * inv for e in es]


def _dil_merge(outs, lses, *, tm, name):
    n = outs[0].shape[1]

    def body(*refs):
        o_in, l_in = refs[0:3], refs[3:6]
        y_ref, yb_ref, scr = refs[6:9]
        lv = [_load_token_order(l_in[g], scr, d, tm) for g, d in enumerate(DIL_DILATIONS)]
        ws = _merge_weights(lv)
        y = jnp.zeros((tm, 256), F32)
        for g, d in enumerate(DIL_DILATIONS):
            y = y + ws[g] * _load_token_order(o_in[g], scr, d, tm)
        y_ref[...] = y
        yb_ref[...] = y.astype(BF16)

    specs = [_dil_spec(d, tm) for d in DIL_DILATIONS]
    return pl.pallas_call(
        body, name=name, grid=(n // tm,), in_specs=specs + specs,
        out_specs=[_rows(tm, 256)] * 2, out_shape=[_sds((n, 256), F32), _sds((n, 256), BF16)],
        scratch_shapes=[_dil_scratch(tm)],
        compiler_params=_params("parallel"),
    )(*outs, *lses)


def _dil_merge_bwd(dy, y, lses, *, tm, name):
    n = dy.shape[0]

    def body(*refs):
        dy_ref, y_ref = refs[0:2]
        l_in = refs[2:5]
        do_out, cc_out = refs[5:8], refs[8:11]
        scr = refs[11]
        lv = [_load_token_order(l_in[g], scr, d, tm) for g, d in enumerate(DIL_DILATIONS)]
        ws = _merge_weights(lv)
        dyv = dy_ref[...]
        rr = lax.broadcasted_iota(jnp.int32, (256, 256), 0) >> 6
        cc = lax.broadcasted_iota(jnp.int32, (256, 256), 1) >> 6
        ones = jnp.where(rr == cc, 1.0, 0.0).astype(F32)
        tsum = jnp.dot(dyv * y_ref[...], ones, preferred_element_type=F32,
                       precision=lax.Precision.HIGHEST)
        for g, d in enumerate(DIL_DILATIONS):
            _store_dil_order(ws[g] * dyv, do_out[g], scr, d, tm)
            _store_dil_order(-ws[g] * tsum, cc_out[g], scr, d, tm)

    specs = [_dil_spec(d, tm) for d in DIL_DILATIONS]
    res = pl.pallas_call(
        body, name=name, grid=(n // tm,),
        in_specs=[_rows(tm, 256)] * 2 + specs,
        out_specs=specs + specs,
        out_shape=[_sds((d, n // d, 256), BF16) for d in DIL_DILATIONS]
                  + [_sds((d, n // d, 256), F32) for d in DIL_DILATIONS],
        scratch_shapes=[_dil_scratch(tm)],
        compiler_params=_params("parallel"),
    )(dy, y, *lses)
    return res[0:3], res[3:6]


_WEIGHTS = (("w_in", 1, 736), ("w_branch_na", 1, 128), ("w_branch_dil", 1, 128), ("w_out", 0, 128),
            ("w_up", 1, 512), ("w_down", 0, 512), ("w_ple_gate", 0, 128), ("w_ple_proj", 1, 128))
_W_IN, _W_BNA, _W_BD, _W_OUT, _W_UP, _W_DOWN, _W_PG, _W_PP = range(8)


def _to_full(gathered):
    return gathered.reshape(-1, gathered.shape[2])


def _to_chunks(widx, mat):
    return mat.reshape(N_DEV, _WEIGHTS[widx][2], mat.shape[1])


def _local_step(x, p_bf16, positions, target, g_mix, g_mlp, g_ple, g_final, rpb2,
                get_w_in, relay_rest, get_rest, send_grads):
    tm = 512
    half = HEAD_DIM // 2
    inv_freq = 10000.0 ** (-jnp.arange(half, dtype=F32) / half)
    ang = positions.astype(F32)[:, None] * inv_freq
    cos, sin = jnp.cos(ang), jnp.sin(ang)
    cos_t = jnp.tile(jnp.concatenate([cos, cos], axis=-1), (1, 4))
    sin_t = jnp.tile(jnp.concatenate([-sin, sin], axis=-1), (1, 4))
    rb = _rpb_table(rpb2)

    a = _rms_fwd(x, g_mix, tm=tm, name="rms_mix")
    w_in, token = get_w_in(a)
    na_width, qkv_width = 3 * NA_WIDTH, 3 * NA_WIDTH + 3 * DIL_WIDTH
    na_qkv = _matmul(a, w_in, tb=True, n_limit=na_width, out_dtype=BF16, tm=512, tn=na_width, tk=1024,
                     name="mm_in_na", after=token)
    proj = _matmul(a, w_in[na_width:qkv_width], tb=True, out_dtype=F32, tm=512, tn=3 * DIL_WIDTH, tk=1024,
                   name="mm_in_dil")
    gates = _matmul(a, w_in[qkv_width:], tb=True, out_dtype=BF16, tm=512, tn=2 * D_MODEL, tk=1024,
                    name="mm_in_gates", epilogue=lambda acc: (_sigmoid(acc),))
    sn, sd = (gates, 0), (gates, 1)
    dq_g, dk_g, dv_g = _split_proj(proj, cos_t, sin_t, tm=tm, name="split_proj")
    y_na = _na_fwd(na_qkv, rb, name="na_fwd")
    token = relay_rest(y_na)
    d_out, d_lse = [], []
    for g in range(3):
        o, lse = _dil_fwd(dq_g[g], dk_g[g], dv_g[g], name=f"dil_fwd{g}", after=token if g == 0 else None)
        d_out.append(o)
        d_lse.append(lse)
    y_dil, y_dil_b = _dil_merge(d_out, d_lse, tm=tm, name="dil_merge")
    w_bna, w_bd, w_out, w_up, w_down, w_pg, w_pp = get_rest(y_dil_b)
    bn = _matmul(y_na, w_bna, tb=True, out_dtype=BF16, tm=512, tn=1024, tk=512, name="mm_bna")
    bd, mixed = _matmul(y_dil_b, w_bd, tb=True, out_dtype=(BF16, BF16), tm=512, tn=1024, tk=256, name="mm_bd",
                        extra=(sn, bn, sd), epilogue=_gate_mix_tile)
    h1, c = _matmul(mixed, w_out, out_dtype=(F32, BF16), tm=512, tn=1024, tk=1024, name="mm_out",
                    extra=(x, g_mlp), epilogue=_residual_rms_tile)
    u, f = _matmul(c, w_up, tb=True, out_dtype=(BF16, BF16), tm=512, tn=2048, tk=1024, name="mm_up",
                   epilogue=lambda acc: (acc, jnp.square(jnp.maximum(acc, 0.0))))
    h2, e = _matmul(f, w_down, out_dtype=(F32, BF16), tm=512, tn=1024, tk=4096, name="mm_down",
                    extra=(h1, g_ple), epilogue=_residual_rms_tile)
    pp = _matmul(p_bf16, w_pp, tb=True, out_dtype=F32, tm=512, tn=1024, tk=256, name="mm_pp")

    dh3, dpp, dgt, dg_final, loss = _matmul(
        e, w_pg, out_dtype=(F32, BF16, BF16), tm=512, tn=1024, tk=1024, name="mm_pg_tail",
        extra=(pp, h2, target, g_final), epilogue=_tail_tile, n_colsum=2)
    loss = loss[:, :128]
    gw_pp = _matmul(p_bf16, dpp, ta=True, transpose_out=True, out_dtype=BF16, tm=256, tn=1024, tk=2048,
                    name="mm_gw_pp")
    gw_pg = _matmul(e, dgt, ta=True, out_dtype=BF16, tm=512, tn=1024, tk=2048, name="mm_gw_pg")
    dh2, dh2_b, dg_ple = _matmul(
        dgt, w_pg, tb=True, out_dtype=(F32, BF16), tm=512, tn=1024, tk=1024, name="mm_de",
        extra=(h2, g_ple, dh3), epilogue=_rms_bwd_twice, n_colsum=1)
    du = _matmul(dh2_b, w_down, tb=True, out_dtype=BF16, tm=512, tn=2048, tk=1024, name="mm_du",
                 extra=(u,), epilogue=lambda acc, uv: (acc * (2.0 * jnp.maximum(uv.astype(F32), 0.0)),))
    gw_down = _matmul(f, dh2_b, ta=True, out_dtype=BF16, tm=1024, tn=1024, tk=2048, name="mm_gw_down")
    gw_up = _matmul(c, du, ta=True, transpose_out=True, out_dtype=BF16, tm=512, tn=2048, tk=2048, name="mm_gw_up")
    dh1, dh1_b, dg_mlp = _matmul(
        du, w_up, out_dtype=(F32, BF16), tm=512, tn=1024, tk=4096, name="mm_dc",
        extra=(h1, g_mlp, dh2), epilogue=_rms_bwd_twice, n_colsum=1)
    dbn, dbd, dgn, dgd = _matmul(dh1_b, w_out, tb=True, out_dtype=(BF16,) * 4, tm=512, tn=1024, tk=1024,
                                 name="mm_dmixed", extra=(sn, bn, sd, bd), epilogue=_gate_bwd_tile)
    gw_out = _matmul(mixed, dh1_b, ta=True, out_dtype=BF16, tm=512, tn=1024, tk=2048, name="mm_gw_out")
    gw_bna = _matmul(y_na, dbn, ta=True, transpose_out=True, out_dtype=BF16, tm=512, tn=1024, tk=2048,
                     name="mm_gw_bna")
    dy_na = _matmul(dbn, w_bna, out_dtype=BF16, tm=512, tn=512, tk=1024, name="mm_dy_na")
    gw_bd = _matmul(y_dil_b, dbd, ta=True, transpose_out=True, out_dtype=BF16, tm=256, tn=1024, tk=2048,
                    name="mm_gw_bd")
    token = send_grads((_W_PP, _W_PG, _W_DOWN, _W_UP, _W_OUT, _W_BNA, _W_BD),
                       (gw_pp, gw_pg, gw_down, gw_up, gw_out, gw_bna, gw_bd))
    dy_dil = _matmul(dbd, w_bd, out_dtype=F32, tm=512, tn=256, tk=1024, name="mm_dy_dil", after=token)
    dna = _na_bwd(na_qkv, dy_na, rb, name="na_bwd")
    drpb = _rpb_grad(dna[3].reshape(8, -1), name="rpb_grad")
    do_g, cc_g = _dil_merge_bwd(dy_dil, y_dil, d_lse, tm=tm, name="dil_merge_bwd")
    ddq, ddk, ddv = [], [], []
    for g in range(3):
        r = _dil_bwd(dq_g[g], dk_g[g], dv_g[g], do_g[g], d_lse[g], cc_g[g], name=f"dil_bwd{g}")
        ddq.append(r[0])
        ddk.append(r[1])
        ddv.append(r[2])
    dproj = _assemble_dproj(dna[0:3], ddq, ddk, ddv, dgn, dgd, cos_t, sin_t, tm=tm, name="assemble_dproj")
    gw_in = _matmul(a, dproj, ta=True, transpose_out=True, out_dtype=BF16, tm=512, tn=2944, tk=2048, name="mm_gw_in")
    token = send_grads((_W_IN,), (gw_in,))
    dx, dg_mix = _matmul(
        dproj, w_in, out_dtype=(F32,), tm=512, tn=1024, tk=5888, name="mm_da", after=token,
        extra=(x, g_mix, dh1), epilogue=_rms_bwd_tile, n_colsum=1)
    return loss, dx, (dg_mix, dg_mlp, dg_ple, dg_final), drpb


def _cast_bf16(t, *, name):
    def body(t_ref, o_ref):
        o_ref[...] = t_ref[...].astype(BF16)

    rows, cols = t.shape
    tr = 256 if rows % 256 == 0 else rows
    blk = pl.BlockSpec((tr, cols), lambda i: (i, 0))
    return pl.pallas_call(body, name=name, grid=(rows // tr,), in_specs=[blk], out_specs=blk,
                          out_shape=_sds(t.shape, BF16), compiler_params=_params("parallel"))(t)


def _adamw(w, g, m, v):
    m = ADAM_B1 * m + (1.0 - ADAM_B1) * g
    v = ADAM_B2 * v + (1.0 - ADAM_B2) * (g * g)
    m_hat = m / (1.0 - ADAM_B1 ** ADAM_STEP)
    v_hat = v / (1.0 - ADAM_B2 ** ADAM_STEP)
    delta = -ADAM_LR * (m_hat / (jnp.sqrt(v_hat) + ADAM_EPS) + ADAM_WD * w)
    return delta, m, v


def _sum_adamw(parts, w, m, v, *, tr, name, own=None, transposed=False):
    rows, cols = w.shape
    n_pre = 0 if own is None else 1

    def body(*refs):
        p_ref, w_ref, m_ref, v_ref = refs[n_pre:n_pre + 4]
        g_ref, d_ref, nm_ref, nv_ref = refs[-4:]
        g = (p_ref[0] if own is None else refs[n_pre + 4][...]).astype(F32)
        for s in range(1, N_DEV):
            g = g + p_ref[s].astype(F32)
        if transposed:
            g = g.T
        g_ref[...] = g
        d_ref[...], nm_ref[...], nv_ref[...] = _adamw(w_ref[...], g, m_ref[...], v_ref[...])

    if transposed:
        blk = pl.BlockSpec((rows, tr), lambda i, *_: (0, i))
        g_rows, steps = rows, cols // tr
    else:
        blk = pl.BlockSpec((tr, cols), lambda i, *_: (i, 0))
        g_rows, steps = cols, rows // tr
    in_specs = [pl.BlockSpec((N_DEV, tr, g_rows), lambda i, *_: (0, i, 0)), blk, blk, blk]
    args = [parts, w, m, v]
    if own is not None:
        in_specs.append(pl.BlockSpec((None, tr, g_rows), lambda i, idx: (idx[0], i, 0)))
        args = [own[1]] + args + [own[0]]
    return pl.pallas_call(
        body, name=name,
        grid_spec=pltpu.PrefetchScalarGridSpec(num_scalar_prefetch=n_pre, grid=(steps,), in_specs=in_specs,
                                               out_specs=[blk] * 4),
        out_shape=[_sds((rows, cols), F32)] * 4,
        compiler_params=_params("parallel"),
    )(*args)


_RPB_SIZE = 8 * 15 * 31


def _pack_small(g_mix, g_mlp, g_ple, g_final, rpb, loss_row):
    flat = jnp.concatenate([g_mix.reshape(-1), g_mlp.reshape(-1), g_ple.reshape(-1), g_final.reshape(-1),
                            rpb.reshape(-1), jnp.zeros((3840 - _RPB_SIZE,), F32), loss_row.reshape(-1),
                            jnp.zeros((128,), F32)])
    return flat.reshape(64, 128)


def _unpack_small(t):
    flat = t.reshape(-1)
    return (flat[0:1024].reshape(1, 1024), flat[4096:4096 + _RPB_SIZE].reshape(1, 8, 15, 31),
            flat[1024:2048].reshape(1, 1024), flat[2048:3072].reshape(1, 1024), flat[3072:4096])


def kernel(x, p, positions, g_mix, w_in, rpb, w_branch_na, w_branch_dil, w_out, g_mlp, w_up, w_down, g_ple, w_ple_gate, w_ple_proj, g_final, loss_target, m_g_mix, m_w_in, m_rpb, m_w_branch_na, m_w_branch_dil, m_w_out, m_g_mlp, m_w_up, m_w_down, m_g_ple, m_w_ple_gate, m_w_ple_proj, m_g_final, v_g_mix, v_w_in, v_rpb, v_w_branch_na, v_w_branch_dil, v_w_out, v_g_mlp, v_w_up, v_w_down, v_g_ple, v_w_ple_gate, v_w_ple_proj, v_g_final):
    sharded = dict(w_in=(w_in, m_w_in, v_w_in), w_branch_na=(w_branch_na, m_w_branch_na, v_w_branch_na),
                   w_branch_dil=(w_branch_dil, m_w_branch_dil, v_w_branch_dil), w_out=(w_out, m_w_out, v_w_out),
                   w_up=(w_up, m_w_up, v_w_up), w_down=(w_down, m_w_down, v_w_down),
                   w_ple_gate=(w_ple_gate, m_w_ple_gate, v_w_ple_gate),
                   w_ple_proj=(w_ple_proj, m_w_ple_proj, v_w_ple_proj))
    shards = {k: tuple(t[0] for t in val) for k, val in sharded.items()}

    me = _my_index()

    shards["w_in"] = tuple(t.T for t in shards["w_in"])

    w_in_b = _cast_bf16(shards["w_in"][0], name="cast_w_in")
    rest_b = [shards[name][0].astype(BF16).T if axis == 1 else shards[name][0].astype(BF16)
              for name, axis, _ in _WEIGHTS[1:]]
    first_in, token_in = _start_copies(_first_leg_copies, [w_in_b], [_sds((N_DEV,) + w_in_b.shape, BF16)], 4,
                                       name="start_gather_w_in")

    def whole(landed, mine):
        return _to_full(lax.dynamic_update_index_in_dim(landed, mine, me, 0))

    rest = {}

    def get_w_in(after):
        (mine,), landed = _wait_copies(_first_leg_copies, first_in, after, name="wait_gather_w_in")
        second, token = _start_copies(_second_leg_copies, [], landed, 3, name="start_forward_w_in")
        _, (landed,) = _wait_copies(_second_leg_copies, second, token, name="wait_forward_w_in")
        rest["first"], token = _start_copies(_first_leg_copies, rest_b,
                                             [_sds((N_DEV,) + t.shape, BF16) for t in rest_b], 4 * len(rest_b),
                                             name="start_gather_rest", after=landed)
        return whole(landed, mine), token

    def relay_rest(after):
        rest["mine"], landed = _wait_copies(_first_leg_copies, rest["first"], after, name="wait_gather_rest")
        rest["second"], token = _start_copies(_second_leg_copies, [], landed, 3 * len(rest_b),
                                              name="start_forward_rest")
        return token

    def get_rest(after):
        _, landed = _wait_copies(_second_leg_copies, rest["second"], after, name="wait_forward_rest")
        return [whole(t, own) for t, own in zip(landed, rest["mine"])]

    sent = []

    def send_grads(indices, grads):
        chunked = [_to_chunks(i, g) for i, g in zip(indices, grads)]
        handle, token = _start_copies(_exchange_copies, chunked, [_sds(t.shape, BF16) for t in chunked],
                                      7 * len(chunked),
                                      name="start_exchange_" + ("w_in" if indices == (_W_IN,) else "rest"))
        sent.append((indices, handle))
        return token

    g_mix_0 = g_mix + token_in[0:1, 0:1]
    loss, dx, dgs, drpb = _local_step(
        x[0], p[0, 0].astype(BF16), positions[0], loss_target[0],
        g_mix_0, g_mlp, g_ple, g_final.reshape(1, -1), rpb[0], get_w_in, relay_rest, get_rest, send_grads)

    drpb3 = drpb.reshape(8, 16, 32)[:, :15, :31]
    small = _pack_small(dgs[0], dgs[1], dgs[2], dgs[3], drpb3, loss)
    share, done = _start_copies(_gather_copies, [small], [_sds((N_DEV,) + small.shape, F32)], 7,
                                name="start_share_small")

    out = {}
    for indices, handle in sent:
        chunked, landed = _wait_copies(_exchange_copies, handle, done,
                                       name="wait_exchange_" + ("w_in" if indices == (_W_IN,) else "rest"))
        for i, part, mine in zip(indices, landed, chunked):
            name = _WEIGHTS[i][0]
            w, m, v = shards[name]
            turned = _WEIGHTS[i][1] == 1 and i != _W_IN
            res = _sum_adamw(part, w, m, v, tr=368 if i == _W_IN else 128, name="adamw_" + name,
                             own=(mine, me.reshape(1).astype(jnp.int32)), transposed=turned)
            out[name] = [(t.T if i == _W_IN else t)[None] for t in res]
            done = res[0]
    (small,), (small_landed,) = _wait_copies(_gather_copies, share, done, name="wait_share_small")
    small_all = lax.dynamic_update_index_in_dim(small_landed, small, me, 0)
    small_w = _pack_small(g_mix, g_mlp, g_ple, g_final, rpb, jnp.zeros((128,), F32))
    small_m = _pack_small(m_g_mix, m_g_mlp, m_g_ple, m_g_final, m_rpb, jnp.zeros((128,), F32))
    small_v = _pack_small(v_g_mix, v_g_mlp, v_g_ple, v_g_final, v_rpb, jnp.zeros((128,), F32))
    res = _sum_adamw(small_all, small_w, small_m, small_v, tr=64, name="adamw_small")
    unpacked = [_unpack_small(t) for t in res]
    for i, name in enumerate(("g_mix", "rpb", "g_mlp", "g_ple", "g_final")):
        out[name] = [u[i] for u in unpacked]
    loss_total = res[0][62, 0]

    order = ("g_mix", "w_in", "rpb", "w_branch_na", "w_branch_dil", "w_out", "g_mlp", "w_up", "w_down",
             "g_ple", "w_ple_gate", "w_ple_proj", "g_final")
    grads = [out[k][0] for k in order]
    deltas = [out[k][1] for k in order]
    new_m = [out[k][2] for k in order]
    new_v = [out[k][3] for k in order]
    return (loss_total, dx[None], *grads, *deltas, *new_m, *new_v)
```

```python
import jax
import jax.numpy as jnp
from jax import lax
from jax.experimental import pallas as pl
from jax.experimental.pallas import tpu as pltpu

F32 = jnp.float32
BF16 = jnp.bfloat16

D_MODEL = 1024
HEAD_DIM = 64
GRID_W = 64
NA_WIDTH = 512
DIL_WIDTH = 768
IN_WIDTH = 5888
DIL_DILATIONS = (1, 4, 16)
DIL_RADIUS = 64
NA_WIN_ROWS = 8
RMS_EPS = 1e-6
NEG_INF = -1e30
QK_SCALE = HEAD_DIM ** -0.5

ADAM_LR = 0.001
ADAM_B1 = 0.9
ADAM_B2 = 0.999
ADAM_EPS = 1e-08
ADAM_WD = 0.01
ADAM_STEP = 10

N_DEV = 8
VMEM_LIMIT = 56 * 1024 * 1024
EPILOGUE_ROWS = 256
MESH = pl.DeviceIdType.MESH

NT_DIMS = (((1,), (1,)), ((), ()))
TN_DIMS = (((0,), (0,)), ((), ()))


def _sds(shape, dtype):
    return jax.ShapeDtypeStruct(shape, dtype)


def _params(*sem):
    return pltpu.CompilerParams(dimension_semantics=sem, vmem_limit_bytes=VMEM_LIMIT)


def _rows(tm, width, col=0):
    return pl.BlockSpec((tm, width), lambda i, c=col: (i, c))


def _const(shape):
    zeros = (0,) * len(shape)
    return pl.BlockSpec(shape, lambda i: zeros)


def _my_index():
    return 4 * lax.axis_index("x") + 2 * lax.axis_index("y") + lax.axis_index("c")


def _peer(k):
    x, y, c = lax.axis_index("x"), lax.axis_index("y"), lax.axis_index("c")
    px = 1 - x if k & 4 else x
    py = 1 - y if k & 2 else y
    pc = 1 - c if k & 1 else c
    return (px, py, pc), 4 * px + 2 * py + pc


def _call(body, *, name, grid, in_specs, out_specs, out_shape, scratch_shapes, args, after=None):
    n_in, n_out = len(in_specs), len(out_specs)
    extra = [] if after is None else [after]
    n_x = n_in + len(extra)

    def plain(*refs):
        body(refs[:n_in], refs[n_x:n_x + n_out], refs[n_x + n_out:])

    res = pl.pallas_call(plain, name=name, grid=grid,
                         in_specs=list(in_specs) + [pl.BlockSpec(memory_space=pl.ANY)] * len(extra),
                         out_specs=out_specs, out_shape=out_shape, scratch_shapes=scratch_shapes,
                         compiler_params=_params(*(("arbitrary",) * len(grid))))(*args, *extra)
    return list(res)


_HBM_SPEC = pl.BlockSpec(memory_space=pltpu.HBM)
_SEM_SPEC = pl.BlockSpec(memory_space=pltpu.SEMAPHORE)
_SIDE_EFFECT = pltpu.SideEffectType.DATAFLOW_SIDE_EFFECTING


_FIRST_LEG = (1, 2, 4, 6)
_SECOND_LEG = (2, 4, 6)


def _gather_copies(srcs, lands, send, recv, sending):
    me = _my_index()
    out = []
    for w in range(len(srcs)):
        for k in range(1, N_DEV):
            dev, idx = _peer(k)
            out.append(pltpu.make_async_remote_copy(
                src_ref=srcs[w], dst_ref=lands[w].at[me if sending else idx],
                send_sem=send.at[w * 7 + k - 1], recv_sem=recv.at[w * 7 + k - 1],
                device_id=dev, device_id_type=MESH))
    return out


def _first_leg_copies(srcs, lands, send, recv, sending):
    me = _my_index()
    out = []
    for w in range(len(srcs)):
        for j, k in enumerate(_FIRST_LEG):
            dev, idx = _peer(k)
            out.append(pltpu.make_async_remote_copy(
                src_ref=srcs[w], dst_ref=lands[w].at[me if sending else idx],
                send_sem=send.at[w * 4 + j], recv_sem=recv.at[w * 4 + j],
                device_id=dev, device_id_type=MESH))
    return out


def _second_leg_copies(srcs, lands, send, recv, sending):
    sibling, _ = _peer(1)
    out = []
    for w in range(len(lands)):
        for j, k in enumerate(_SECOND_LEG):
            slot = _peer(k if sending else k ^ 1)[1]
            out.append(pltpu.make_async_remote_copy(
                src_ref=lands[w].at[slot], dst_ref=lands[w].at[slot],
                send_sem=send.at[w * 3 + j], recv_sem=recv.at[w * 3 + j],
                device_id=sibling, device_id_type=MESH))
    return out


def _exchange_copies(srcs, lands, send, recv, sending):
    out = []
    for w in range(len(srcs)):
        for k in range(1, N_DEV):
            dev, idx = _peer(k)
            out.append(pltpu.make_async_remote_copy(
                src_ref=srcs[w].at[idx], dst_ref=lands[w].at[k],
                send_sem=send.at[w * 7 + k - 1], recv_sem=recv.at[w * 7 + k - 1],
                device_id=dev, device_id_type=MESH))
    return out


def _start_copies(make, srcs, lands, n_copies, *, name, after=None):
    n_src, n_buf = len(srcs), len(srcs) + len(lands)
    extra = [] if after is None else [after]

    def body(*refs):
        send, recv = refs[n_buf + len(extra)], refs[n_buf + len(extra) + 1]
        for cp in make(refs[:n_src], refs[n_src:n_buf], send, recv, True):
            cp.start()
        refs[-1][...] = jnp.zeros_like(refs[-1])

    bufs = list(srcs) + [lax.empty(t.shape, t.dtype) if isinstance(t, jax.ShapeDtypeStruct) else t for t in lands]
    res = pl.pallas_call(
        body, name=name,
        out_shape=(pltpu.SemaphoreType.DMA((n_copies,)), pltpu.SemaphoreType.DMA((n_copies,)),
                   *[pltpu.HBM(t.shape, t.dtype) for t in bufs], _sds((8, 128), F32)),
        in_specs=[_HBM_SPEC] * n_buf + [pl.BlockSpec(memory_space=pl.ANY)] * len(extra),
        out_specs=(_SEM_SPEC, _SEM_SPEC, *([_HBM_SPEC] * n_buf), pl.BlockSpec(memory_space=pltpu.VMEM)),
        input_output_aliases={i: 2 + i for i in range(n_buf)},
        compiler_params=pltpu.CompilerParams(has_side_effects=_SIDE_EFFECT),
    )(*[pltpu.with_memory_space_constraint(t, pltpu.HBM) for t in bufs], *extra)
    return (n_src, res[0], res[1], res[2:2 + n_buf]), res[-1]


def _wait_copies(make, handle, after, *, name):
    n_src, send_sems, recv_sems, bufs = handle
    n_buf = len(bufs)

    def body(*refs):
        for cp in make(refs[:n_src], refs[n_src:n_buf], refs[n_buf], refs[n_buf + 1], False):
            cp.wait_send()
            cp.wait_recv()

    res = pl.pallas_call(
        body, name=name,
        out_shape=tuple(pltpu.HBM(t.shape, t.dtype) for t in bufs),
        in_specs=[_HBM_SPEC] * n_buf + [_SEM_SPEC, _SEM_SPEC, pl.BlockSpec(memory_space=pl.ANY)],
        out_specs=tuple([_HBM_SPEC] * n_buf),
        input_output_aliases={i: i for i in range(n_buf)},
        compiler_params=pltpu.CompilerParams(has_side_effects=_SIDE_EFFECT),
    )(*bufs, send_sems, recv_sems, after)
    return list(res[:n_src]), list(res[n_src:])


def _matmul(a, b, *, ta=False, tb=False, out_dtype, tm, tn, tk, name, after=None, extra=(), epilogue=None,
            n_colsum=0, transpose_out=False, n_limit=None):
    m, k = (a.shape[1], a.shape[0]) if ta else a.shape
    n = n_limit or (b.shape[0] if tb else b.shape[1])
    tm, tn, tk = min(tm, m), min(tn, n), min(tk, k)
    nk = k // tk
    dims = (((0 if ta else 1,), (1 if tb else 0,)), ((), ()))
    out_dtypes = out_dtype if isinstance(out_dtype, tuple) else (out_dtype,)
    n_tiles = len(out_dtypes)

    def add_colsums(o_refs, sums):
        i = pl.program_id(1)
        for s_ref, val in zip(o_refs[n_tiles:], sums):
            @pl.when(i == 0)
            def _(s_ref=s_ref, val=val):
                s_ref[...] = val

            @pl.when(i > 0)
            def _(s_ref=s_ref, val=val):
                s_ref[...] += val

    def finish(acc, x_refs, o_refs):
        vals = (acc,) if epilogue is None else epilogue(acc, *[r[...] for r in x_refs])
        for o_ref, val in zip(o_refs[:n_tiles], vals[:n_tiles]):
            o_ref[...] = (val.T if transpose_out else val).astype(o_ref.dtype)
        add_colsums(o_refs, vals[n_tiles:])

    chunk = EPILOGUE_ROWS if (nk == 1 and epilogue is not None and not ta and tm % EPILOGUE_ROWS == 0) else None

    def body(ins, outs, acc):
        a_ref, b_ref = ins[:2]
        if chunk is not None:
            sums = None
            for r0 in range(0, tm, chunk):
                part = lax.dot_general(a_ref[r0:r0 + chunk, :], b_ref[...], dims, preferred_element_type=F32)
                vals = epilogue(part, *[r[...] if r.shape[0] == 1 else r[r0:r0 + chunk, :] for r in ins[2:]])
                for o_ref, val in zip(outs[:n_tiles], vals[:n_tiles]):
                    o_ref[r0:r0 + chunk, :] = val.astype(o_ref.dtype)
                sums = vals[n_tiles:] if sums is None else [s + v for s, v in zip(sums, vals[n_tiles:])]
            add_colsums(outs, sums)
            return
        part = lax.dot_general(a_ref[...], b_ref[...], dims, preferred_element_type=F32)
        if nk == 1:
            finish(part, ins[2:], outs)
            return
        acc_ref, = acc
        kk = pl.program_id(2)

        @pl.when(kk == 0)
        def _():
            acc_ref[...] = part

        @pl.when(kk > 0)
        def _():
            acc_ref[...] += part

        @pl.when(kk == nk - 1)
        def _():
            finish(acc_ref[...], ins[2:], outs)

    a_spec = (pl.BlockSpec((tk, tm), lambda j, i, kk: (kk, i)) if ta
              else pl.BlockSpec((tm, tk), lambda j, i, kk: (i, kk)))
    b_spec = (pl.BlockSpec((tn, tk), lambda j, i, kk: (j, kk)) if tb
              else pl.BlockSpec((tk, tn), lambda j, i, kk: (kk, j)))
    tile = pl.BlockSpec((tm, tn), lambda j, i, kk: (i, j))
    row = pl.BlockSpec((1, tn), lambda j, i, kk: (0, j))

    def x_spec(t):
        if isinstance(t, tuple):
            return pl.BlockSpec((tm, tn), lambda j, i, kk, first=t[1] * (n // tn): (i, first + j))
        return row if t.shape[0] == 1 else tile

    out_tile, out_dims = (pl.BlockSpec((tn, tm), lambda j, i, kk: (j, i)), (n, m)) if transpose_out else (tile, (m, n))
    res = _call(
        body, name=name, grid=(n // tn, m // tm, nk),
        in_specs=[a_spec, b_spec] + [x_spec(t) for t in extra],
        out_specs=[out_tile] * n_tiles + [row] * n_colsum,
        out_shape=[_sds(out_dims, dt) for dt in out_dtypes] + [_sds((1, n), F32)] * n_colsum,
        scratch_shapes=[] if nk == 1 else [pltpu.VMEM((tm, tn), F32)],
        args=(a, b, *[t[0] if isinstance(t, tuple) else t for t in extra]), after=after)
    return res if isinstance(out_dtype, tuple) or n_colsum else res[0]


def _rstd(h):
    return lax.rsqrt(jnp.mean(h * h, axis=-1, keepdims=True) + RMS_EPS)


def _sigmoid(z):
    return 1.0 / (1.0 + jnp.exp(-z))


def _rms_fwd(x, g, *, tm, name):
    n = x.shape[0]

    def body(x_ref, g_ref, o_ref):
        h = x_ref[...]
        o_ref[...] = (h * _rstd(h) * g_ref[...]).astype(BF16)

    return pl.pallas_call(
        body, name=name, grid=(n // tm,),
        in_specs=[_rows(tm, D_MODEL), _const((1, D_MODEL))],
        out_specs=_rows(tm, D_MODEL), out_shape=_sds((n, D_MODEL), BF16),
        compiler_params=_params("parallel"),
    )(x, g)


def _swap_halves(t):
    lane = lax.broadcasted_iota(jnp.int32, (t.shape[0], 128), 1)
    pieces = [t[:, c:c + 128] for c in range(0, t.shape[1], 128)]
    return jnp.concatenate([jnp.where((lane & 63) < 32, pltpu.roll(h, 96, 1), pltpu.roll(h, 32, 1))
                            for h in pieces], axis=1)


def _dil_spec(dil, tm):
    return pl.BlockSpec((dil, tm // dil, 256), lambda i: (0, i, 0))


def _dil_scratch(tm):
    return pltpu.VMEM((2, tm, 128), F32)


def _load_token_order(src, scr, dil, tm):
    if dil == 1:
        return src[0]
    for j in range(dil):
        for c in range(2):
            scr[c, pl.ds(j, tm // dil, stride=dil), :] = src[j, :, c * 128:(c + 1) * 128]
    return jnp.concatenate([scr[0], scr[1]], axis=1)


def _store_dil_order(val, dst, scr, dil, tm):
    if dil == 1:
        dst[0] = val.astype(dst.dtype)
        return
    for c in range(2):
        scr[c] = val[:, c * 128:(c + 1) * 128]
    for j in range(dil):
        for c in range(2):
            dst[j, :, c * 128:(c + 1) * 128] = scr[c, pl.ds(j, tm // dil, stride=dil), :].astype(dst.dtype)


def _split_proj(proj, cos_t, sin_t, *, tm, name):
    n = proj.shape[0]
    n_dil = len(DIL_DILATIONS)

    def body(*refs):
        dil_in = refs[0:3 * n_dil]
        cos_ref, sin_ref = refs[9:11]
        dil_out = refs[11:20]
        scr = refs[20]
        cosv, sinv = cos_ref[...], sin_ref[...]
        for t in range(3):
            for gi, dil in enumerate(DIL_DILATIONS):
                val = dil_in[t * n_dil + gi][...]
                if t < 2:
                    val = val * cosv + _swap_halves(val) * sinv
                _store_dil_order(val, dil_out[t * n_dil + gi], scr, dil, tm)

    in_specs = [_rows(tm, 256, c) for c in range(9)] + [_rows(tm, 256), _rows(tm, 256)]
    out_specs, out_shape = [], []
    for _ in range(3):
        for dil in DIL_DILATIONS:
            out_specs.append(pl.BlockSpec((dil, tm // dil, 256), lambda i: (0, i, 0)))
            out_shape.append(_sds((dil, n // dil, 256), BF16))
    res = pl.pallas_call(
        body, name=name, grid=(n // tm,),
        in_specs=in_specs, out_specs=out_specs, out_shape=out_shape,
        scratch_shapes=[_dil_scratch(tm)],
        compiler_params=_params("parallel"),
    )(*([proj] * 9), cos_t, sin_t)
    return res[0:3], res[3:6], res[6:9]


def _residual_rms_tile(delta, h, g):
    hn = h + delta
    return hn, hn * _rstd(hn) * g


def _gate_mix_tile(b2, s1, b1, s2):
    return b2, s1.astype(F32) * b1.astype(F32) + s2.astype(F32) * b2


def _gate_bwd_tile(dm, s1, b1, s2, b2):
    s1, b1, s2, b2 = (t.astype(F32) for t in (s1, b1, s2, b2))
    return dm * s1, dm * s2, dm * b1 * s1 * (1.0 - s1), dm * b2 * s2 * (1.0 - s2)


def _tail_tile(gt, pp, h2, target, g):
    sg = _sigmoid(gt)
    h3 = h2 + sg * pp
    r3 = _rstd(h3)
    n3 = h3 * r3
    err = n3 * g - target
    loss = 0.5 * jnp.sum(jnp.sum(err * err, axis=-1, keepdims=True) / D_MODEL)
    dy = err / D_MODEL
    dn = dy * g
    dh3 = r3 * (dn - n3 * jnp.mean(dn * n3, axis=-1, keepdims=True))
    return (dh3, dh3 * sg, dh3 * pp * sg * (1.0 - sg),
            jnp.sum(dy * n3, axis=0, keepdims=True), jnp.full((1, gt.shape[1]), loss, F32))


def _rms_bwd_tile(dz, h, g, dres):
    r = _rstd(h)
    nrm = h * r
    dn = dz * g
    dh = dres + r * (dn - nrm * jnp.mean(dn * nrm, axis=-1, keepdims=True))
    return dh, jnp.sum(dz * nrm, axis=0, keepdims=True)


def _rms_bwd_twice(dz, h, g, dres):
    dh, dg = _rms_bwd_tile(dz, h, g, dres)
    return dh, dh, dg


def _assemble_dproj(dna, ddil_q, ddil_k, ddil_v, dgn, dgd, cos_t, sin_t, *, tm, name):
    n = dgn.shape[0]

    def body(*refs):
        dq_ref, dk_ref, dv_ref = refs[0:3]
        dil_in = refs[3:12]
        dgn_ref, dgd_ref, cos_ref, sin_ref, o_ref, scr = refs[12:18]
        o_ref[:, 0:512] = dq_ref[...]
        o_ref[:, 512:1024] = dk_ref[...].astype(BF16)
        o_ref[:, 1024:1536] = dv_ref[...].astype(BF16)
        cosv, sinv = cos_ref[...], sin_ref[...]
        for t in range(3):
            for gi, dil in enumerate(DIL_DILATIONS):
                val = _load_token_order(dil_in[t * 3 + gi], scr, dil, tm)
                if t < 2:
                    val = val * cosv + _swap_halves(val * sinv)
                c0 = 1536 + t * DIL_WIDTH + gi * 256
                o_ref[:, c0:c0 + 256] = val.astype(BF16)
        o_ref[:, 3840:4864] = dgn_ref[...]
        o_ref[:, 4864:5888] = dgd_ref[...]

    in_specs = [_rows(tm, NA_WIDTH)] * 3
    for _ in range(3):
        for dil in DIL_DILATIONS:
            in_specs.append(pl.BlockSpec((dil, tm // dil, 256), lambda i: (0, i, 0)))
    in_specs += [_rows(tm, D_MODEL)] * 2 + [_rows(tm, 256)] * 2
    return pl.pallas_call(
        body, name=name, grid=(n // tm,), in_specs=in_specs,
        out_specs=_rows(tm, IN_WIDTH), out_shape=_sds((n, IN_WIDTH), BF16),
        scratch_shapes=[_dil_scratch(tm)],
        compiler_params=_params("parallel"),
    )(*dna, *ddil_q, *ddil_k, *ddil_v, dgn, dgd, cos_t, sin_t)


N_ROW_OFF = 2 * NA_WIN_ROWS - 1
N_PAIRS = N_ROW_OFF - 1
RB_WIDTH = (N_ROW_OFF + 1) * GRID_W


def _na_bias(rb_ref, pair_scr):
    shape = (GRID_W, RB_WIDTH)
    qc = lax.broadcasted_iota(jnp.int32, shape, 0)
    qc2 = lax.broadcasted_iota(jnp.int32, (GRID_W, 128), 0)
    kc2 = lax.broadcasted_iota(jnp.int32, (GRID_W, 128), 1) & (GRID_W - 1)
    cs = jnp.clip(qc2 - 8, 0, GRID_W - 16)
    valid = (kc2 >= cs) & (kc2 < cs + 16)
    for hh in range(2):
        t = jnp.broadcast_to(rb_ref[hh], shape)
        t = pltpu.roll(t, RB_WIDTH - 15, 1)
        for b in range(6):
            t = jnp.where(((qc >> b) & 1) == 1, pltpu.roll(t, 1 << b, 1), t)
        t_odd = pltpu.roll(t, RB_WIDTH - GRID_W, 1)
        for ro in range(N_PAIRS):
            src = t if ro % 2 == 0 else t_odd
            base = (ro // 2) * 128
            pair_scr[hh, ro] = jnp.where(valid, src[:, base:base + 128], NEG_INF)


NA_GROUP_FWD = 8
NA_GROUP_BWD = 4


def _stack_heads(ref, r, scale=1.0):
    lane = lax.broadcasted_iota(jnp.int32, (GRID_W, 128), 1)
    t = ref[pl.ds(pl.multiple_of(r * GRID_W, GRID_W), GRID_W), :].astype(F32) * scale
    return jnp.concatenate([jnp.where(lane < 64, t, 0.0), jnp.where(lane >= 64, t, 0.0)], axis=0).astype(BF16)


def _unstack_heads(t2):
    lane = lax.broadcasted_iota(jnp.int32, (GRID_W, 128), 1)
    return jnp.where(lane < 64, t2[:GRID_W], t2[GRID_W:])


def _na_window(k_ref, v_ref, r, n_rows):
    rs = jnp.clip(r - NA_WIN_ROWS // 2, 0, n_rows - NA_WIN_ROWS)
    ro0 = (NA_WIN_ROWS - 1) - (r - rs)
    off = pl.multiple_of(rs * GRID_W, GRID_W)
    kw = k_ref[pl.ds(off, NA_WIN_ROWS * GRID_W), :]
    vw = v_ref[pl.ds(off, NA_WIN_ROWS * GRID_W), :]
    return kw, vw, off, ro0


def _na_probs(s_raw, pair_scr, ro0):
    bias = [jnp.concatenate([pair_scr[hh, ro0 + 2 * j] for j in range(NA_WIN_ROWS // 2)], axis=1)
            for hh in range(2)]
    s = s_raw + jnp.concatenate(bias, axis=0)
    m = jnp.max(s, axis=-1, keepdims=True)
    e = jnp.exp(s - m)
    return e * (1.0 / jnp.sum(e, axis=-1, keepdims=True))


def _na_qkv_specs(n):
    pairs = NA_WIDTH // 128
    return [pl.BlockSpec((n, 128), lambda h, first=t * pairs: (0, first + h)) for t in range(3)]


def _na_fwd(qkv, rb, *, name):
    n = qkv.shape[0]
    n_rows = n // GRID_W

    def body(ins, outs, scr):
        q_ref, k_ref, v_ref, rb_ref = ins
        o_ref, = outs
        pair_scr, = scr
        _na_bias(rb_ref, pair_scr)

        def group(g, carry):
            rows = [g * NA_GROUP_FWD + t for t in range(NA_GROUP_FWD)]
            wins = [_na_window(k_ref, v_ref, r, n_rows) for r in rows]
            raw = [lax.dot_general(_stack_heads(q_ref, r, QK_SCALE), w[0], NT_DIMS, preferred_element_type=F32)
                   for r, w in zip(rows, wins)]
            probs = [_na_probs(s, pair_scr, w[3]) for s, w in zip(raw, wins)]
            outs2 = [jnp.dot(p.astype(BF16), w[1], preferred_element_type=F32) for p, w in zip(probs, wins)]
            for r, o2 in zip(rows, outs2):
                o_ref[pl.ds(pl.multiple_of(r * GRID_W, GRID_W), GRID_W), :] = _unstack_heads(o2).astype(BF16)
            return carry

        lax.fori_loop(0, n_rows // NA_GROUP_FWD, group, 0)

    col = pl.BlockSpec((n, 128), lambda h: (0, h))
    return _call(
        body, name=name, grid=(NA_WIDTH // 128,),
        in_specs=_na_qkv_specs(n) + [pl.BlockSpec((2, 1, RB_WIDTH), lambda h: (h, 0, 0))],
        out_specs=[col], out_shape=[_sds((n, NA_WIDTH), BF16)],
        scratch_shapes=[pltpu.VMEM((2, N_PAIRS, GRID_W, 128), F32)],
        args=(qkv, qkv, qkv, rb))[0]


def _na_bwd(qkv, do, rb, *, name):
    n = qkv.shape[0]
    n_rows = n // GRID_W
    win = NA_WIN_ROWS * GRID_W

    def body(ins, outs, scr):
        q_ref, k_ref, v_ref, do_ref, rb_ref = ins
        dq_ref, dk_ref, dv_ref, drb_ref = outs
        pair_scr, acc_scr = scr
        _na_bias(rb_ref, pair_scr)
        acc_scr[...] = jnp.zeros_like(acc_scr)
        dk_ref[...] = jnp.zeros_like(dk_ref)
        dv_ref[...] = jnp.zeros_like(dv_ref)

        def group(g, carry):
            rows = [g * NA_GROUP_BWD + t for t in range(NA_GROUP_BWD)]
            wins = [_na_window(k_ref, v_ref, r, n_rows) for r in rows]
            qss = [_stack_heads(q_ref, r, QK_SCALE) for r in rows]
            doss = [_stack_heads(do_ref, r) for r in rows]
            raw = [lax.dot_general(qs, w[0], NT_DIMS, preferred_element_type=F32) for qs, w in zip(qss, wins)]
            dps = [lax.dot_general(dos, w[1], NT_DIMS, preferred_element_type=F32) for dos, w in zip(doss, wins)]
            probs = [_na_probs(s, pair_scr, w[3]) for s, w in zip(raw, wins)]
            dss = [p * (dp - jnp.sum(p * dp, axis=-1, keepdims=True)) for p, dp in zip(probs, dps)]
            dsbs = [ds.astype(BF16) for ds in dss]
            dq2s = [jnp.dot(dsb, w[0], preferred_element_type=F32) for dsb, w in zip(dsbs, wins)]
            dkws = [lax.dot_general(dsb, qs, TN_DIMS, preferred_element_type=F32) for dsb, qs in zip(dsbs, qss)]
            dvws = [lax.dot_general(p.astype(BF16), dos, TN_DIMS, preferred_element_type=F32)
                    for p, dos in zip(probs, doss)]
            for t, r in enumerate(rows):
                _, _, off, ro0 = wins[t]
                for hh in range(2):
                    for j in range(NA_WIN_ROWS // 2):
                        acc_scr[hh, ro0 + 2 * j] += dss[t][hh * GRID_W:(hh + 1) * GRID_W, j * 128:(j + 1) * 128]
                dq_ref[pl.ds(pl.multiple_of(r * GRID_W, GRID_W), GRID_W), :] = (
                    _unstack_heads(dq2s[t]) * QK_SCALE).astype(BF16)
                dk_ref[pl.ds(off, win), :] += dkws[t]
                dv_ref[pl.ds(off, win), :] += dvws[t]
            return carry

        lax.fori_loop(0, n_rows // NA_GROUP_BWD, group, 0)

        qc = lax.broadcasted_iota(jnp.int32, (N_PAIRS * GRID_W, 128), 0)
        for hh in range(2):
            t = acc_scr[hh].reshape(N_PAIRS * GRID_W, 128)
            for b in range(6):
                t = jnp.where(((qc >> b) & 1) == 1, pltpu.roll(t, 128 - (1 << b), 1), t)
            t = pltpu.roll(t, 15, 1)
            drb_ref[hh] = jnp.sum(t.reshape(N_PAIRS, GRID_W, 128), axis=1)

    col = pl.BlockSpec((n, 128), lambda h: (0, h))
    return _call(
        body, name=name, grid=(NA_WIDTH // 128,),
        in_specs=_na_qkv_specs(n) + [col, pl.BlockSpec((2, 1, RB_WIDTH), lambda h: (h, 0, 0))],
        out_specs=[col, col, col, pl.BlockSpec((2, N_PAIRS, 128), lambda h: (h, 0, 0))],
        out_shape=[_sds((n, NA_WIDTH), BF16), _sds((n, NA_WIDTH), F32), _sds((n, NA_WIDTH), F32),
                   _sds((8, N_PAIRS, 128), F32)],
        scratch_shapes=[pltpu.VMEM((2, N_PAIRS, GRID_W, 128), F32),
                        pltpu.VMEM((2, N_PAIRS, GRID_W, 128), F32)],
        args=(qkv, qkv, qkv, do, rb))


def _rpb_table(rpb2):
    t = jnp.pad(rpb2, ((0, 0), (0, 1), (0, GRID_W - rpb2.shape[-1])))
    return t.reshape(8, 1, RB_WIDTH)


def _rpb_grad(drb, *, name):
    kdim = drb.shape[1]

    def body(x_ref, o_ref):
        kk = lax.broadcasted_iota(jnp.int32, (128, 512), 0)
        jj = lax.broadcasted_iota(jnp.int32, (128, 512), 1)
        half, co = kk >> 6, kk & 63
        acc = jnp.zeros((8, 512), F32)
        for ro in range(N_PAIRS):
            hit = ((ro + half) == (jj >> 5)) & (co == (jj & 31)) & (co < 31)
            onehot = jnp.where(hit, 1.0, 0.0).astype(F32)
            acc = acc + jnp.dot(x_ref[:, ro * 128:(ro + 1) * 128], onehot, preferred_element_type=F32,
                                precision=lax.Precision.HIGHEST)
        o_ref[...] = acc

    return pl.pallas_call(
        body, name=name, grid=(1,),
        in_specs=[_const((8, kdim))], out_specs=_const((8, 512)), out_shape=_sds((8, 512), F32),
        compiler_params=_params("arbitrary"),
    )(drb)


DIL_GROUP = 2


def _dil_blocks(length):
    qb = min(128, length)
    return qb, min(qb + 2 * DIL_RADIUS, length), min(DIL_GROUP, length // qb)


def _stack_lanes(ref, t, qb, scale=1.0):
    lane = lax.broadcasted_iota(jnp.int32, (qb, 256), 1)
    val = ref[0, t * qb:(t + 1) * qb, :].astype(F32) * scale
    return jnp.concatenate([jnp.where((lane >> 6) == h, val, 0.0) for h in range(4)], axis=0).astype(BF16)


def _dil_window(k_ref, v_ref, blk, qb, win, length):
    start = pl.multiple_of(jnp.clip(blk * qb - DIL_RADIUS, 0, length - win), DIL_RADIUS)
    return k_ref[0, pl.ds(start, win), :], v_ref[0, pl.ds(start, win), :], start


def _dil_caps_init(caps_scr, qb, win):
    @pl.when((pl.program_id(0) == 0) & (pl.program_id(1) == 0))
    def _():
        gap = ((lax.broadcasted_iota(jnp.int32, (4 * qb, win), 0) & (qb - 1))
               - lax.broadcasted_iota(jnp.int32, (4 * qb, win), 1))
        for v in range(3):
            caps_scr[v] = jnp.where(jnp.abs(gap + v * DIL_RADIUS) <= DIL_RADIUS, jnp.inf, NEG_INF)


def _dil_mask(s, blk, start, qb, caps_scr):
    return jnp.minimum(s, caps_scr[(blk * qb - start) // DIL_RADIUS])


def _pick_heads(stacked, qb):
    lane = lax.broadcasted_iota(jnp.int32, (qb, 256), 1)
    out = jnp.zeros((qb, 256), stacked.dtype)
    for h in range(4):
        out = jnp.where((lane >> 6) == h, stacked[h * qb:(h + 1) * qb], out)
    return out


def _stack_head_cols(ref, t, qb):
    return jnp.concatenate([ref[0, t * qb:(t + 1) * qb, 64 * h:64 * h + 1] for h in range(4)], axis=0)


def _dil_fwd(q, k, v, *, name, after=None):
    dil, length, _ = q.shape
    qb, win, grp = _dil_blocks(length)
    extra = [] if after is None else [after]

    def body(q_ref, k_ref, v_ref, *rest):
        o_ref, lse_ref, caps_scr = rest[-3:]
        _dil_caps_init(caps_scr, qb, win)
        blks = [pl.program_id(1) * grp + t for t in range(grp)]
        wins = [_dil_window(k_ref, v_ref, b, qb, win, length) for b in blks]
        raw = [lax.dot_general(_stack_lanes(q_ref, t, qb, QK_SCALE), w[0], NT_DIMS, preferred_element_type=F32)
               for t, w in enumerate(wins)]
        lses, outs = [], []
        for t, (s, w) in enumerate(zip(raw, wins)):
            s = _dil_mask(s, blks[t], w[2], qb, caps_scr)
            m = jnp.max(s, axis=-1, keepdims=True)
            e = jnp.exp(s - m)
            norm = jnp.sum(e, axis=-1, keepdims=True)
            lses.append(m + jnp.log(norm))
            outs.append(jnp.dot((e * (1.0 / norm)).astype(BF16), w[1], preferred_element_type=F32))
        for t in range(grp):
            o_ref[0, t * qb:(t + 1) * qb, :] = _pick_heads(outs[t], qb)
            lse_ref[0, t * qb:(t + 1) * qb, :] = _pick_heads(jnp.broadcast_to(lses[t], (4 * qb, 256)), qb)

    seq = pl.BlockSpec((1, length, 256), lambda j, i: (j, 0, 0))
    blk = pl.BlockSpec((1, grp * qb, 256), lambda j, i: (j, i, 0))
    return pl.pallas_call(
        body, name=name, grid=(dil, length // (grp * qb)),
        in_specs=[blk, seq, seq] + [pl.BlockSpec(memory_space=pl.ANY)] * len(extra), out_specs=[blk, blk],
        out_shape=[_sds((dil, length, 256), F32)] * 2,
        scratch_shapes=[pltpu.VMEM((3, 4 * qb, win), F32)],
        compiler_params=_params("arbitrary", "arbitrary"),
    )(q, k, v, *extra)


def _dil_bwd(q, k, v, do, lse, cc, *, name):
    dil, length, _ = q.shape
    qb, win, grp = _dil_blocks(length)

    def body(q_ref, k_ref, v_ref, do_ref, lse_ref, cc_ref, dq_ref, dk_ref, dv_ref, caps_scr):
        _dil_caps_init(caps_scr, qb, win)

        @pl.when(pl.program_id(1) == 0)
        def _():
            dk_ref[...] = jnp.zeros_like(dk_ref)
            dv_ref[...] = jnp.zeros_like(dv_ref)

        blks = [pl.program_id(1) * grp + t for t in range(grp)]
        wins = [_dil_window(k_ref, v_ref, b, qb, win, length) for b in blks]
        qss = [_stack_lanes(q_ref, t, qb, QK_SCALE) for t in range(grp)]
        doss = [_stack_lanes(do_ref, t, qb) for t in range(grp)]
        raw = [lax.dot_general(qs, w[0], NT_DIMS, preferred_element_type=F32) for qs, w in zip(qss, wins)]
        dps = [lax.dot_general(dos, w[1], NT_DIMS, preferred_element_type=F32) for dos, w in zip(doss, wins)]
        probs = [jnp.exp(_dil_mask(s, blks[t], wins[t][2], qb, caps_scr) - _stack_head_cols(lse_ref, t, qb))
                 for t, s in enumerate(raw)]
        dsbs = [(p * (dp + _stack_head_cols(cc_ref, t, qb))).astype(BF16)
                for t, (p, dp) in enumerate(zip(probs, dps))]
        dq4s = [jnp.dot(dsb, w[0], preferred_element_type=F32) for dsb, w in zip(dsbs, wins)]
        dkws = [lax.dot_general(dsb, qs, TN_DIMS, preferred_element_type=F32) for dsb, qs in zip(dsbs, qss)]
        dvws = [lax.dot_general(p.astype(BF16), dos, TN_DIMS, preferred_element_type=F32)
                for p, dos in zip(probs, doss)]
        for t in range(grp):
            dq_ref[0, t * qb:(t + 1) * qb, :] = _pick_heads(dq4s[t], qb) * QK_SCALE
            dk_ref[0, pl.ds(wins[t][2], win), :] += dkws[t]
            dv_ref[0, pl.ds(wins[t][2], win), :] += dvws[t]

    seq = pl.BlockSpec((1, length, 256), lambda j, i: (j, 0, 0))
    blk = pl.BlockSpec((1, grp * qb, 256), lambda j, i: (j, i, 0))
    return pl.pallas_call(
        body, name=name, grid=(dil, length // (grp * qb)),
        in_specs=[blk, seq, seq, blk, blk, blk], out_specs=[blk, seq, seq],
        out_shape=[_sds((dil, length, 256), F32)] * 3,
        scratch_shapes=[pltpu.VMEM((3, 4 * qb, win), F32)],
        compiler_params=_params("arbitrary", "arbitrary"),
    )(q, k, v, do, lse, cc)


def _merge_weights(lses):
    m = jnp.maximum(jnp.maximum(lses[0], lses[1]), lses[2])
    es = [jnp.exp(t - m) for t in lses]
    inv = 1.0 / (es[0] + es[1] + es[2])
    return [e * inv for e in es]


def _dil_merge(outs, lses, *, tm, name):
    n = outs[0].shape[1]

    def body(*refs):
        o_in, l_in = refs[0:3], refs[3:6]
        y_ref, yb_ref, scr = refs[6:9]
        lv = [_load_token_order(l_in[g], scr, d, tm) for g, d in enumerate(DIL_DILATIONS)]
        ws = _merge_weights(lv)
        y = jnp.zeros((tm, 256), F32)
        for g, d in enumerate(DIL_DILATIONS):
            y = y + ws[g] * _load_token_order(o_in[g], scr, d, tm)
        y_ref[...] = y
        yb_ref[...] = y.astype(BF16)

    specs = [_dil_spec(d, tm) for d in DIL_DILATIONS]
    return pl.pallas_call(
        body, name=name, grid=(n // tm,), in_specs=specs + specs,
        out_specs=[_rows(tm, 256)] * 2, out_shape=[_sds((n, 256), F32), _sds((n, 256), BF16)],
        scratch_shapes=[_dil_scratch(tm)],
        compiler_params=_params("parallel"),
    )(*outs, *lses)


def _dil_merge_bwd(dy, y, lses, *, tm, name):
    n = dy.shape[0]

    def body(*refs):
        dy_ref, y_ref = refs[0:2]
        l_in = refs[2:5]
        do_out, cc_out = refs[5:8], refs[8:11]
        scr = refs[11]
        lv = [_load_token_order(l_in[g], scr, d, tm) for g, d in enumerate(DIL_DILATIONS)]
        ws = _merge_weights(lv)
        dyv = dy_ref[...]
        rr = lax.broadcasted_iota(jnp.int32, (256, 256), 0) >> 6
        cc = lax.broadcasted_iota(jnp.int32, (256, 256), 1) >> 6
        ones = jnp.where(rr == cc, 1.0, 0.0).astype(F32)
        tsum = jnp.dot(dyv * y_ref[...], ones, preferred_element_type=F32,
                       precision=lax.Precision.HIGHEST)
        for g, d in enumerate(DIL_DILATIONS):
            _store_dil_order(ws[g] * dyv, do_out[g], scr, d, tm)
            _store_dil_order(-ws[g] * tsum, cc_out[g], scr, d, tm)

    specs = [_dil_spec(d, tm) for d in DIL_DILATIONS]
    res = pl.pallas_call(
        body, name=name, grid=(n // tm,),
        in_specs=[_rows(tm, 256)] * 2 + specs,
        out_specs=specs + specs,
        out_shape=[_sds((d, n // d, 256), BF16) for d in DIL_DILATIONS]
                  + [_sds((d, n // d, 256), F32) for d in DIL_DILATIONS],
        scratch_shapes=[_dil_scratch(tm)],
        compiler_params=_params("parallel"),
    )(dy, y, *lses)
    return res[0:3], res[3:6]


_WEIGHTS = (("w_in", 1, 736), ("w_branch_na", 1, 128), ("w_branch_dil", 1, 128), ("w_out", 0, 128),
            ("w_up", 1, 512), ("w_down", 0, 512), ("w_ple_gate", 0, 128), ("w_ple_proj", 1, 128))
_W_IN, _W_BNA, _W_BD, _W_OUT, _W_UP, _W_DOWN, _W_PG, _W_PP = range(8)


def _to_full(gathered):
    return gathered.reshape(-1, gathered.shape[2])


def _to_chunks(widx, mat):
    return mat.reshape(N_DEV, _WEIGHTS[widx][2], mat.shape[1])


def _local_step(x, p_bf16, positions, target, g_mix, g_mlp, g_ple, g_final, rpb2,
                get_w_in, relay_rest, get_rest, send_grads):
    tm = 512
    half = HEAD_DIM // 2
    inv_freq = 10000.0 ** (-jnp.arange(half, dtype=F32) / half)
    ang = positions.astype(F32)[:, None] * inv_freq
    cos, sin = jnp.cos(ang), jnp.sin(ang)
    cos_t = jnp.tile(jnp.concatenate([cos, cos], axis=-1), (1, 4))
    sin_t = jnp.tile(jnp.concatenate([-sin, sin], axis=-1), (1, 4))
    rb = _rpb_table(rpb2)

    a = _rms_fwd(x, g_mix, tm=tm, name="rms_mix")
    w_in, token = get_w_in(a)
    na_width, qkv_width = 3 * NA_WIDTH, 3 * NA_WIDTH + 3 * DIL_WIDTH
    na_qkv = _matmul(a, w_in, tb=True, n_limit=na_width, out_dtype=BF16, tm=512, tn=na_width, tk=1024,
                     name="mm_in_na", after=token)
    proj = _matmul(a, w_in[na_width:qkv_width], tb=True, out_dtype=F32, tm=512, tn=3 * DIL_WIDTH, tk=1024,
                   name="mm_in_dil")
    gates = _matmul(a, w_in[qkv_width:], tb=True, out_dtype=BF16, tm=512, tn=2 * D_MODEL, tk=1024,
                    name="mm_in_gates", epilogue=lambda acc: (_sigmoid(acc),))
    sn, sd = (gates, 0), (gates, 1)
    dq_g, dk_g, dv_g = _split_proj(proj, cos_t, sin_t, tm=tm, name="split_proj")
    y_na = _na_fwd(na_qkv, rb, name="na_fwd")
    token = relay_rest(y_na)
    d_out, d_lse = [], []
    for g in range(3):
        o, lse = _dil_fwd(dq_g[g], dk_g[g], dv_g[g], name=f"dil_fwd{g}", after=token if g == 0 else None)
        d_out.append(o)
        d_lse.append(lse)
    y_dil, y_dil_b = _dil_merge(d_out, d_lse, tm=tm, name="dil_merge")
    w_bna, w_bd, w_out, w_up, w_down, w_pg, w_pp = get_rest(y_dil_b)
    bn = _matmul(y_na, w_bna, tb=True, out_dtype=BF16, tm=512, tn=1024, tk=512, name="mm_bna")
    bd, mixed = _matmul(y_dil_b, w_bd, tb=True, out_dtype=(BF16, BF16), tm=512, tn=1024, tk=256, name="mm_bd",
                        extra=(sn, bn, sd), epilogue=_gate_mix_tile)
    h1, c = _matmul(mixed, w_out, out_dtype=(F32, BF16), tm=512, tn=1024, tk=1024, name="mm_out",
                    extra=(x, g_mlp), epilogue=_residual_rms_tile)
    u, f = _matmul(c, w_up, tb=True, out_dtype=(BF16, BF16), tm=512, tn=2048, tk=1024, name="mm_up",
                   epilogue=lambda acc: (acc, jnp.square(jnp.maximum(acc, 0.0))))
    h2, e = _matmul(f, w_down, out_dtype=(F32, BF16), tm=512, tn=1024, tk=4096, name="mm_down",
                    extra=(h1, g_ple), epilogue=_residual_rms_tile)
    pp = _matmul(p_bf16, w_pp, tb=True, out_dtype=F32, tm=512, tn=1024, tk=256, name="mm_pp")

    dh3, dpp, dgt, dg_final, loss = _matmul(
        e, w_pg, out_dtype=(F32, BF16, BF16), tm=512, tn=1024, tk=1024, name="mm_pg_tail",
        extra=(pp, h2, target, g_final), epilogue=_tail_tile, n_colsum=2)
    loss = loss[:, :128]
    gw_pp = _matmul(p_bf16, dpp, ta=True, transpose_out=True, out_dtype=BF16, tm=256, tn=1024, tk=2048,
                    name="mm_gw_pp")
    gw_pg = _matmul(e, dgt, ta=True, out_dtype=BF16, tm=512, tn=1024, tk=2048, name="mm_gw_pg")
    dh2, dh2_b, dg_ple = _matmul(
        dgt, w_pg, tb=True, out_dtype=(F32, BF16), tm=512, tn=1024, tk=1024, name="mm_de",
        extra=(h2, g_ple, dh3), epilogue=_rms_bwd_twice, n_colsum=1)
    du = _matmul(dh2_b, w_down, tb=True, out_dtype=BF16, tm=512, tn=2048, tk=1024, name="mm_du",
                 extra=(u,), epilogue=lambda acc, uv: (acc * (2.0 * jnp.maximum(uv.astype(F32), 0.0)),))
    gw_down = _matmul(f, dh2_b, ta=True, out_dtype=BF16, tm=1024, tn=1024, tk=2048, name="mm_gw_down")
    gw_up = _matmul(c, du, ta=True, transpose_out=True, out_dtype=BF16, tm=512, tn=2048, tk=2048, name="mm_gw_up")
    dh1, dh1_b, dg_mlp = _matmul(
        du, w_up, out_dtype=(F32, BF16), tm=512, tn=1024, tk=4096, name="mm_dc",
        extra=(h1, g_mlp, dh2), epilogue=_rms_bwd_twice, n_colsum=1)
    dbn, dbd, dgn, dgd = _matmul(dh1_b, w_out, tb=True, out_dtype=(BF16,) * 4, tm=512, tn=1024, tk=1024,
                                 name="mm_dmixed", extra=(sn, bn, sd, bd), epilogue=_gate_bwd_tile)
    gw_out = _matmul(mixed, dh1_b, ta=True, out_dtype=BF16, tm=512, tn=1024, tk=2048, name="mm_gw_out")
    gw_bna = _matmul(y_na, dbn, ta=True, transpose_out=True, out_dtype=BF16, tm=512, tn=1024, tk=2048,
                     name="mm_gw_bna")
    dy_na = _matmul(dbn, w_bna, out_dtype=BF16, tm=512, tn=512, tk=1024, name="mm_dy_na")
    gw_bd = _matmul(y_dil_b, dbd, ta=True, transpose_out=True, out_dtype=BF16, tm=256, tn=1024, tk=2048,
                    name="mm_gw_bd")
    token = send_grads((_W_PP, _W_PG, _W_DOWN, _W_UP, _W_OUT, _W_BNA, _W_BD),
                       (gw_pp, gw_pg, gw_down, gw_up, gw_out, gw_bna, gw_bd))
    dy_dil = _matmul(dbd, w_bd, out_dtype=F32, tm=512, tn=256, tk=1024, name="mm_dy_dil", after=token)
    dna = _na_bwd(na_qkv, dy_na, rb, name="na_bwd")
    drpb = _rpb_grad(dna[3].reshape(8, -1), name="rpb_grad")
    do_g, cc_g = _dil_merge_bwd(dy_dil, y_dil, d_lse, tm=tm, name="dil_merge_bwd")
    ddq, ddk, ddv = [], [], []
    for g in range(3):
        r = _dil_bwd(dq_g[g], dk_g[g], dv_g[g], do_g[g], d_lse[g], cc_g[g], name=f"dil_bwd{g}")
        ddq.append(r[0])
        ddk.append(r[1])
        ddv.append(r[2])
    dproj = _assemble_dproj(dna[0:3], ddq, ddk, ddv, dgn, dgd, cos_t, sin_t, tm=tm, name="assemble_dproj")
    gw_in = _matmul(a, dproj, ta=True, transpose_out=True, out_dtype=BF16, tm=512, tn=2944, tk=2048, name="mm_gw_in")
    token = send_grads((_W_IN,), (gw_in,))
    dx, dg_mix = _matmul(
        dproj, w_in, out_dtype=(F32,), tm=512, tn=1024, tk=5888, name="mm_da", after=token,
        extra=(x, g_mix, dh1), epilogue=_rms_bwd_tile, n_colsum=1)
    return loss, dx, (dg_mix, dg_mlp, dg_ple, dg_final), drpb


def _cast_bf16(t, *, name):
    def body(t_ref, o_ref):
        o_ref[...] = t_ref[...].astype(BF16)

    rows, cols = t.shape
    tr = 256 if rows % 256 == 0 else rows
    blk = pl.BlockSpec((tr, cols), lambda i: (i, 0))
    return pl.pallas_call(body, name=name, grid=(rows // tr,), in_specs=[blk], out_specs=blk,
                          out_shape=_sds(t.shape, BF16), compiler_params=_params("parallel"))(t)


def _adamw(w, g, m, v):
    m = ADAM_B1 * m + (1.0 - ADAM_B1) * g
    v = ADAM_B2 * v + (1.0 - ADAM_B2) * (g * g)
    m_hat = m / (1.0 - ADAM_B1 ** ADAM_STEP)
    v_hat = v / (1.0 - ADAM_B2 ** ADAM_STEP)
    delta = -ADAM_LR * (m_hat / (jnp.sqrt(v_hat) + ADAM_EPS) + ADAM_WD * w)
    return delta, m, v


def _sum_adamw(parts, w, m, v, *, tr, name, own=None, transposed=False):
    rows, cols = w.shape
    n_pre = 0 if own is None else 1

    def body(*refs):
        p_ref, w_ref, m_ref, v_ref = refs[n_pre:n_pre + 4]
        g_ref, d_ref, nm_ref, nv_ref = refs[-4:]
        g = (p_ref[0] if own is None else refs[n_pre + 4][...]).astype(F32)
        for s in range(1, N_DEV):
            g = g + p_ref[s].astype(F32)
        if transposed:
            g = g.T
        g_ref[...] = g
        d_ref[...], nm_ref[...], nv_ref[...] = _adamw(w_ref[...], g, m_ref[...], v_ref[...])

    if transposed:
        blk = pl.BlockSpec((rows, tr), lambda i, *_: (0, i))
        g_rows, steps = rows, cols // tr
    else:
        blk = pl.BlockSpec((tr, cols), lambda i, *_: (i, 0))
        g_rows, steps = cols, rows // tr
    in_specs = [pl.BlockSpec((N_DEV, tr, g_rows), lambda i, *_: (0, i, 0)), blk, blk, blk]
    args = [parts, w, m, v]
    if own is not None:
        in_specs.append(pl.BlockSpec((None, tr, g_rows), lambda i, idx: (idx[0], i, 0)))
        args = [own[1]] + args + [own[0]]
    return pl.pallas_call(
        body, name=name,
        grid_spec=pltpu.PrefetchScalarGridSpec(num_scalar_prefetch=n_pre, grid=(steps,), in_specs=in_specs,
                                               out_specs=[blk] * 4),
        out_shape=[_sds((rows, cols), F32)] * 4,
        compiler_params=_params("parallel"),
    )(*args)


_RPB_SIZE = 8 * 15 * 31


def _pack_small(g_mix, g_mlp, g_ple, g_final, rpb, loss_row):
    flat = jnp.concatenate([g_mix.reshape(-1), g_mlp.reshape(-1), g_ple.reshape(-1), g_final.reshape(-1),
                            rpb.reshape(-1), jnp.zeros((3840 - _RPB_SIZE,), F32), loss_row.reshape(-1),
                            jnp.zeros((128,), F32)])
    return flat.reshape(64, 128)


def _unpack_small(t):
    flat = t.reshape(-1)
    return (flat[0:1024].reshape(1, 1024), flat[4096:4096 + _RPB_SIZE].reshape(1, 8, 15, 31),
            flat[1024:2048].reshape(1, 1024), flat[2048:3072].reshape(1, 1024), flat[3072:4096])


def kernel(x, p, positions, g_mix, w_in, rpb, w_branch_na, w_branch_dil, w_out, g_mlp, w_up, w_down, g_ple, w_ple_gate, w_ple_proj, g_final, loss_target, m_g_mix, m_w_in, m_rpb, m_w_branch_na, m_w_branch_dil, m_w_out, m_g_mlp, m_w_up, m_w_down, m_g_ple, m_w_ple_gate, m_w_ple_proj, m_g_final, v_g_mix, v_w_in, v_rpb, v_w_branch_na, v_w_branch_dil, v_w_out, v_g_mlp, v_w_up, v_w_down, v_g_ple, v_w_ple_gate, v_w_ple_proj, v_g_final):
    sharded = dict(w_in=(w_in, m_w_in, v_w_in), w_branch_na=(w_branch_na, m_w_branch_na, v_w_branch_na),
                   w_branch_dil=(w_branch_dil, m_w_branch_dil, v_w_branch_dil), w_out=(w_out, m_w_out, v_w_out),
                   w_up=(w_up, m_w_up, v_w_up), w_down=(w_down, m_w_down, v_w_down),
                   w_ple_gate=(w_ple_gate, m_w_ple_gate, v_w_ple_gate),
                   w_ple_proj=(w_ple_proj, m_w_ple_proj, v_w_ple_proj))
    shards = {k: tuple(t[0] for t in val) for k, val in sharded.items()}

    me = _my_index()

    shards["w_in"] = tuple(t.T for t in shards["w_in"])

    w_in_b = _cast_bf16(shards["w_in"][0], name="cast_w_in")
    rest_b = [shards[name][0].astype(BF16).T if axis == 1 else shards[name][0].astype(BF16)
              for name, axis, _ in _WEIGHTS[1:]]
    first_in, token_in = _start_copies(_first_leg_copies, [w_in_b], [_sds((N_DEV,) + w_in_b.shape, BF16)], 4,
                                       name="start_gather_w_in")

    def whole(landed, mine):
        return _to_full(lax.dynamic_update_index_in_dim(landed, mine, me, 0))

    rest = {}

    def get_w_in(after):
        (mine,), landed = _wait_copies(_first_leg_copies, first_in, after, name="wait_gather_w_in")
        second, token = _start_copies(_second_leg_copies, [], landed, 3, name="start_forward_w_in")
        _, (landed,) = _wait_copies(_second_leg_copies, second, token, name="wait_forward_w_in")
        rest["first"], token = _start_copies(_first_leg_copies, rest_b,
                                             [_sds((N_DEV,) + t.shape, BF16) for t in rest_b], 4 * len(rest_b),
                                             name="start_gather_rest", after=landed)
        return whole(landed, mine), token

    def relay_rest(after):
        rest["mine"], landed = _wait_copies(_first_leg_copies, rest["first"], after, name="wait_gather_rest")
        rest["second"], token = _start_copies(_second_leg_copies, [], landed, 3 * len(rest_b),
                                              name="start_forward_rest")
        return token

    def get_rest(after):
        _, landed = _wait_copies(_second_leg_copies, rest["second"], after, name="wait_forward_rest")
        return [whole(t, own) for t, own in zip(landed, rest["mine"])]

    sent = []

    def send_grads(indices, grads):
        chunked = [_to_chunks(i, g) for i, g in zip(indices, grads)]
        handle, token = _start_copies(_exchange_copies, chunked, [_sds(t.shape, BF16) for t in chunked],
                                      7 * len(chunked),
                                      name="start_exchange_" + ("w_in" if indices == (_W_IN,) else "rest"))
        sent.append((indices, handle))
        return token

    g_mix_0 = g_mix + token_in[0:1, 0:1]
    loss, dx, dgs, drpb = _local_step(
        x[0], p[0, 0].astype(BF16), positions[0], loss_target[0],
        g_mix_0, g_mlp, g_ple, g_final.reshape(1, -1), rpb[0], get_w_in, relay_rest, get_rest, send_grads)

    drpb3 = drpb.reshape(8, 16, 32)[:, :15, :31]
    small = _pack_small(dgs[0], dgs[1], dgs[2], dgs[3], drpb3, loss)
    share, done = _start_copies(_gather_copies, [small], [_sds((N_DEV,) + small.shape, F32)], 7,
                                name="start_share_small")

    out = {}
    for indices, handle in sent:
        chunked, landed = _wait_copies(_exchange_copies, handle, done,
                                       name="wait_exchange_" + ("w_in" if indices == (_W_IN,) else "rest"))
        for i, part, mine in zip(indices, landed, chunked):
            name = _WEIGHTS[i][0]
            w, m, v = shards[name]
            turned = _WEIGHTS[i][1] == 1 and i != _W_IN
            res = _sum_adamw(part, w, m, v, tr=368 if i == _W_IN else 128, name="adamw_" + name,
                             own=(mine, me.reshape(1).astype(jnp.int32)), transposed=turned)
            out[name] = [(t.T if i == _W_IN else t)[None] for t in res]
            done = res[0]
    (small,), (small_landed,) = _wait_copies(_gather_copies, share, done, name="wait_share_small")
    small_all = lax.dynamic_update_index_in_dim(small_landed, small, me, 0)
    small_w = _pack_small(g_mix, g_mlp, g_ple, g_final, rpb, jnp.zeros((128,), F32))
    small_m = _pack_small(m_g_mix, m_g_mlp, m_g_ple, m_g_final, m_rpb, jnp.zeros((128,), F32))
    small_v = _pack_small(v_g_mix, v_g_mlp, v_g_ple, v_g_final, v_rpb, jnp.zeros((128,), F32))
    res = _sum_adamw(small_all, small_w, small_m, small_v, tr=64, name="adamw_small")
    unpacked = [_unpack_small(t) for t in res]
    for i, name in enumerate(("g_mix", "rpb", "g_mlp", "g_ple", "g_final")):
        out[name] = [u[i] for u in unpacked]
    loss_total = res[0][62, 0]

    order = ("g_mix", "w_in", "rpb", "w_branch_na", "w_branch_dil", "w_out", "g_mlp", "w_up", "w_down",
             "g_ple", "w_ple_gate", "w_ple_proj", "g_final")
    grads = [out[k][0] for k in order]
    deltas = [out[k][1] for k in order]
    new_m = [out[k][2] for k in order]
    new_v = [out[k][3] for k in order]
    return (loss_total, dx[None], *grads, *deltas, *new_m, *new_v)
```

```python
import jax
import jax.numpy as jnp
from jax import lax
from jax.experimental import pallas as pl
from jax.experimental.pallas import tpu as pltpu

F32 = jnp.float32
BF16 = jnp.bfloat16

D_MODEL = 1024
HEAD_DIM = 64
GRID_W = 64
NA_WIDTH = 512
DIL_WIDTH = 768
IN_WIDTH = 5888
DIL_DILATIONS = (1, 4, 16)
DIL_RADIUS = 64
NA_WIN_ROWS = 8
RMS_EPS = 1e-6
NEG_INF = -1e30
QK_SCALE = HEAD_DIM ** -0.5

ADAM_LR = 0.001
ADAM_B1 = 0.9
ADAM_B2 = 0.999
ADAM_EPS = 1e-08
ADAM_WD = 0.01
ADAM_STEP = 10

N_DEV = 8
VMEM_LIMIT = 56 * 1024 * 1024
EPILOGUE_ROWS = 256
MESH = pl.DeviceIdType.MESH

NT_DIMS = (((1,), (1,)), ((), ()))
TN_DIMS = (((0,), (0,)), ((), ()))


def _sds(shape, dtype):
    return jax.ShapeDtypeStruct(shape, dtype)


def _params(*sem):
    return pltpu.CompilerParams(dimension_semantics=sem, vmem_limit_bytes=VMEM_LIMIT)


def _rows(tm, width, col=0):
    return pl.BlockSpec((tm, width), lambda i, c=col: (i, c))


def _const(shape):
    zeros = (0,) * len(shape)
    return pl.BlockSpec(shape, lambda i: zeros)


def _my_index():
    return 4 * lax.axis_index("x") + 2 * lax.axis_index("y") + lax.axis_index("c")


def _peer(k):
    x, y, c = lax.axis_index("x"), lax.axis_index("y"), lax.axis_index("c")
    px = 1 - x if k & 4 else x
    py = 1 - y if k & 2 else y
    pc = 1 - c if k & 1 else c
    return (px, py, pc), 4 * px + 2 * py + pc


def _call(body, *, name, grid, in_specs, out_specs, out_shape, scratch_shapes, args, after=None):
    n_in, n_out = len(in_specs), len(out_specs)
    extra = [] if after is None else [after]
    n_x = n_in + len(extra)

    def plain(*refs):
        body(refs[:n_in], refs[n_x:n_x + n_out], refs[n_x + n_out:])

    res = pl.pallas_call(plain, name=name, grid=grid,
                         in_specs=list(in_specs) + [pl.BlockSpec(memory_space=pl.ANY)] * len(extra),
                         out_specs=out_specs, out_shape=out_shape, scratch_shapes=scratch_shapes,
                         compiler_params=_params(*(("arbitrary",) * len(grid))))(*args, *extra)
    return list(res)


_HBM_SPEC = pl.BlockSpec(memory_space=pltpu.HBM)
_SEM_SPEC = pl.BlockSpec(memory_space=pltpu.SEMAPHORE)
_SIDE_EFFECT = pltpu.SideEffectType.DATAFLOW_SIDE_EFFECTING


_FIRST_LEG = (1, 2, 4, 6)
_SECOND_LEG = (2, 4, 6)


def _gather_copies(srcs, lands, send, recv, sending):
    me = _my_index()
    out = []
    for w in range(len(srcs)):
        for k in range(1, N_DEV):
            dev, idx = _peer(k)
            out.append(pltpu.make_async_remote_copy(
                src_ref=srcs[w], dst_ref=lands[w].at[me if sending else idx],
                send_sem=send.at[w * 7 + k - 1], recv_sem=recv.at[w * 7 + k - 1],
                device_id=dev, device_id_type=MESH))
    return out


def _first_leg_copies(srcs, lands, send, recv, sending):
    me = _my_index()
    out = []
    for w in range(len(srcs)):
        for j, k in enumerate(_FIRST_LEG):
            dev, idx = _peer(k)
            out.append(pltpu.make_async_remote_copy(
                src_ref=srcs[w], dst_ref=lands[w].at[me if sending else idx],
                send_sem=send.at[w * 4 + j], recv_sem=recv.at[w * 4 + j],
                device_id=dev, device_id_type=MESH))
    return out


def _second_leg_copies(srcs, lands, send, recv, sending):
    sibling, _ = _peer(1)
    out = []
    for w in range(len(lands)):
        for j, k in enumerate(_SECOND_LEG):
            slot = _peer(k if sending else k ^ 1)[1]
            out.append(pltpu.make_async_remote_copy(
                src_ref=lands[w].at[slot], dst_ref=lands[w].at[slot],
                send_sem=send.at[w * 3 + j], recv_sem=recv.at[w * 3 + j],
                device_id=sibling, device_id_type=MESH))
    return out


def _exchange_copies(srcs, lands, send, recv, sending):
    out = []
    for w in range(len(srcs)):
        for k in range(1, N_DEV):
            dev, idx = _peer(k)
            out.append(pltpu.make_async_remote_copy(
                src_ref=srcs[w].at[idx], dst_ref=lands[w].at[k],
                send_sem=send.at[w * 7 + k - 1], recv_sem=recv.at[w * 7 + k - 1],
                device_id=dev, device_id_type=MESH))
    return out


def _start_copies(make, srcs, lands, n_copies, *, name, after=None):
    n_src, n_buf = len(srcs), len(srcs) + len(lands)
    extra = [] if after is None else [after]

    def body(*refs):
        send, recv = refs[n_buf + len(extra)], refs[n_buf + len(extra) + 1]
        for cp in make(refs[:n_src], refs[n_src:n_buf], send, recv, True):
            cp.start()
        refs[-1][...] = jnp.zeros_like(refs[-1])

    bufs = list(srcs) + [lax.empty(t.shape, t.dtype) if isinstance(t, jax.ShapeDtypeStruct) else t for t in lands]
    res = pl.pallas_call(
        body, name=name,
        out_shape=(pltpu.SemaphoreType.DMA((n_copies,)), pltpu.SemaphoreType.DMA((n_copies,)),
                   *[pltpu.HBM(t.shape, t.dtype) for t in bufs], _sds((8, 128), F32)),
        in_specs=[_HBM_SPEC] * n_buf + [pl.BlockSpec(memory_space=pl.ANY)] * len(extra),
        out_specs=(_SEM_SPEC, _SEM_SPEC, *([_HBM_SPEC] * n_buf), pl.BlockSpec(memory_space=pltpu.VMEM)),
        input_output_aliases={i: 2 + i for i in range(n_buf)},
        compiler_params=pltpu.CompilerParams(has_side_effects=_SIDE_EFFECT),
    )(*[pltpu.with_memory_space_constraint(t, pltpu.HBM) for t in bufs], *extra)
    return (n_src, res[0], res[1], res[2:2 + n_buf]), res[-1]


def _wait_copies(make, handle, after, *, name):
    n_src, send_sems, recv_sems, bufs = handle
    n_buf = len(bufs)

    def body(*refs):
        for cp in make(refs[:n_src], refs[n_src:n_buf], refs[n_buf], refs[n_buf + 1], False):
            cp.wait_send()
            cp.wait_recv()

    res = pl.pallas_call(
        body, name=name,
        out_shape=tuple(pltpu.HBM(t.shape, t.dtype) for t in bufs),
        in_specs=[_HBM_SPEC] * n_buf + [_SEM_SPEC, _SEM_SPEC, pl.BlockSpec(memory_space=pl.ANY)],
        out_specs=tuple([_HBM_SPEC] * n_buf),
        input_output_aliases={i: i for i in range(n_buf)},
        compiler_params=pltpu.CompilerParams(has_side_effects=_SIDE_EFFECT),
    )(*bufs, send_sems, recv_sems, after)
    return list(res[:n_src]), list(res[n_src:])


def _matmul(a, b, *, ta=False, tb=False, out_dtype, tm, tn, tk, name, after=None, extra=(), epilogue=None,
            n_colsum=0, transpose_out=False, n_limit=None):
    m, k = (a.shape[1], a.shape[0]) if ta else a.shape
    n = n_limit or (b.shape[0] if tb else b.shape[1])
    tm, tn, tk = min(tm, m), min(tn, n), min(tk, k)
    nk = k // tk
    dims = (((0 if ta else 1,), (1 if tb else 0,)), ((), ()))
    out_dtypes = out_dtype if isinstance(out_dtype, tuple) else (out_dtype,)
    n_tiles = len(out_dtypes)

    def add_colsums(o_refs, sums):
        i = pl.program_id(1)
        for s_ref, val in zip(o_refs[n_tiles:], sums):
            @pl.when(i == 0)
            def _(s_ref=s_ref, val=val):
                s_ref[...] = val

            @pl.when(i > 0)
            def _(s_ref=s_ref, val=val):
                s_ref[...] += val

    def finish(acc, x_refs, o_refs):
        vals = (acc,) if epilogue is None else epilogue(acc, *[r[...] for r in x_refs])
        for o_ref, val in zip(o_refs[:n_tiles], vals[:n_tiles]):
            o_ref[...] = (val.T if transpose_out else val).astype(o_ref.dtype)
        add_colsums(o_refs, vals[n_tiles:])

    chunk = EPILOGUE_ROWS if (nk == 1 and epilogue is not None and not ta and tm % EPILOGUE_ROWS == 0) else None

    def body(ins, outs, acc):
        a_ref, b_ref = ins[:2]
        if chunk is not None:
            sums = None
            for r0 in range(0, tm, chunk):
                part = lax.dot_general(a_ref[r0:r0 + chunk, :], b_ref[...], dims, preferred_element_type=F32)
                vals = epilogue(part, *[r[...] if r.shape[0] == 1 else r[r0:r0 + chunk, :] for r in ins[2:]])
                for o_ref, val in zip(outs[:n_tiles], vals[:n_tiles]):
                    o_ref[r0:r0 + chunk, :] = val.astype(o_ref.dtype)
                sums = vals[n_tiles:] if sums is None else [s + v for s, v in zip(sums, vals[n_tiles:])]
            add_colsums(outs, sums)
            return
        part = lax.dot_general(a_ref[...], b_ref[...], dims, preferred_element_type=F32)
        if nk == 1:
            finish(part, ins[2:], outs)
            return
        acc_ref, = acc
        kk = pl.program_id(2)

        @pl.when(kk == 0)
        def _():
            acc_ref[...] = part

        @pl.when(kk > 0)
        def _():
            acc_ref[...] += part

        @pl.when(kk == nk - 1)
        def _():
            finish(acc_ref[...], ins[2:], outs)

    a_spec = (pl.BlockSpec((tk, tm), lambda j, i, kk: (kk, i)) if ta
              else pl.BlockSpec((tm, tk), lambda j, i, kk: (i, kk)))
    b_spec = (pl.BlockSpec((tn, tk), lambda j, i, kk: (j, kk)) if tb
              else pl.BlockSpec((tk, tn), lambda j, i, kk: (kk, j)))
    tile = pl.BlockSpec((tm, tn), lambda j, i, kk: (i, j))
    row = pl.BlockSpec((1, tn), lambda j, i, kk: (0, j))

    def x_spec(t):
        if isinstance(t, tuple):
            return pl.BlockSpec((tm, tn), lambda j, i, kk, first=t[1] * (n // tn): (i, first + j))
        return row if t.shape[0] == 1 else tile

    out_tile, out_dims = (pl.BlockSpec((tn, tm), lambda j, i, kk: (j, i)), (n, m)) if transpose_out else (tile, (m, n))
    res = _call(
        body, name=name, grid=(n // tn, m // tm, nk),
        in_specs=[a_spec, b_spec] + [x_spec(t) for t in extra],
        out_specs=[out_tile] * n_tiles + [row] * n_colsum,
        out_shape=[_sds(out_dims, dt) for dt in out_dtypes] + [_sds((1, n), F32)] * n_colsum,
        scratch_shapes=[] if nk == 1 else [pltpu.VMEM((tm, tn), F32)],
        args=(a, b, *[t[0] if isinstance(t, tuple) else t for t in extra]), after=after)
    return res if isinstance(out_dtype, tuple) or n_colsum else res[0]


def _rstd(h):
    return lax.rsqrt(jnp.mean(h * h, axis=-1, keepdims=True) + RMS_EPS)


def _sigmoid(z):
    return 1.0 / (1.0 + jnp.exp(-z))


def _rms_fwd(x, g, *, tm, name):
    n = x.shape[0]

    def body(x_ref, g_ref, o_ref):
        h = x_ref[...]
        o_ref[...] = (h * _rstd(h) * g_ref[...]).astype(BF16)

    return pl.pallas_call(
        body, name=name, grid=(n // tm,),
        in_specs=[_rows(tm, D_MODEL), _const((1, D_MODEL))],
        out_specs=_rows(tm, D_MODEL), out_shape=_sds((n, D_MODEL), BF16),
        compiler_params=_params("parallel"),
    )(x, g)


def _swap_halves(t):
    lane = lax.broadcasted_iota(jnp.int32, (t.shape[0], 128), 1)
    pieces = [t[:, c:c + 128] for c in range(0, t.shape[1], 128)]
    return jnp.concatenate([jnp.where((lane & 63) < 32, pltpu.roll(h, 96, 1), pltpu.roll(h, 32, 1))
                            for h in pieces], axis=1)


def _dil_spec(dil, tm):
    return pl.BlockSpec((dil, tm // dil, 256), lambda i: (0, i, 0))


def _dil_scratch(tm):
    return pltpu.VMEM((2, tm, 128), F32)


def _load_token_order(src, scr, dil, tm):
    if dil == 1:
        return src[0]
    for j in range(dil):
        for c in range(2):
            scr[c, pl.ds(j, tm // dil, stride=dil), :] = src[j, :, c * 128:(c + 1) * 128]
    return jnp.concatenate([scr[0], scr[1]], axis=1)


def _store_dil_order(val, dst, scr, dil, tm):
    if dil == 1:
        dst[0] = val.astype(dst.dtype)
        return
    for c in range(2):
        scr[c] = val[:, c * 128:(c + 1) * 128]
    for j in range(dil):
        for c in range(2):
            dst[j, :, c * 128:(c + 1) * 128] = scr[c, pl.ds(j, tm // dil, stride=dil), :].astype(dst.dtype)


def _split_proj(proj, cos_t, sin_t, *, tm, name):
    n = proj.shape[0]
    n_dil = len(DIL_DILATIONS)

    def body(*refs):
        dil_in = refs[0:3 * n_dil]
        cos_ref, sin_ref = refs[9:11]
        dil_out = refs[11:20]
        scr = refs[20]
        cosv, sinv = cos_ref[...], sin_ref[...]
        for t in range(3):
            for gi, dil in enumerate(DIL_DILATIONS):
                val = dil_in[t * n_dil + gi][...]
                if t < 2:
                    val = val * cosv + _swap_halves(val) * sinv
                _store_dil_order(val, dil_out[t * n_dil + gi], scr, dil, tm)

    in_specs = [_rows(tm, 256, c) for c in range(9)] + [_rows(tm, 256), _rows(tm, 256)]
    out_specs, out_shape = [], []
    for _ in range(3):
        for dil in DIL_DILATIONS:
            out_specs.append(pl.BlockSpec((dil, tm // dil, 256), lambda i: (0, i, 0)))
            out_shape.append(_sds((dil, n // dil, 256), BF16))
    res = pl.pallas_call(
        body, name=name, grid=(n // tm,),
        in_specs=in_specs, out_specs=out_specs, out_shape=out_shape,
        scratch_shapes=[_dil_scratch(tm)],
        compiler_params=_params("parallel"),
    )(*([proj] * 9), cos_t, sin_t)
    return res[0:3], res[3:6], res[6:9]


def _residual_rms_tile(delta, h, g):
    hn = h + delta
    return hn, hn * _rstd(hn) * g


def _gate_mix_tile(b2, s1, b1, s2):
    return b2, s1.astype(F32) * b1.astype(F32) + s2.astype(F32) * b2


def _gate_bwd_tile(dm, s1, b1, s2, b2):
    s1, b1, s2, b2 = (t.astype(F32) for t in (s1, b1, s2, b2))
    return dm * s1, dm * s2, dm * b1 * s1 * (1.0 - s1), dm * b2 * s2 * (1.0 - s2)


def _tail_tile(gt, pp, h2, target, g):
    sg = _sigmoid(gt)
    h3 = h2 + sg * pp
    r3 = _rstd(h3)
    n3 = h3 * r3
    err = n3 * g - target
    loss = 0.5 * jnp.sum(jnp.sum(err * err, axis=-1, keepdims=True) / D_MODEL)
    dy = err / D_MODEL
    dn = dy * g
    dh3 = r3 * (dn - n3 * jnp.mean(dn * n3, axis=-1, keepdims=True))
    return (dh3, dh3 * sg, dh3 * pp * sg * (1.0 - sg),
            jnp.sum(dy * n3, axis=0, keepdims=True), jnp.full((1, gt.shape[1]), loss, F32))


def _rms_bwd_tile(dz, h, g, dres):
    r = _rstd(h)
    nrm = h * r
    dn = dz * g
    dh = dres + r * (dn - nrm * jnp.mean(dn * nrm, axis=-1, keepdims=True))
    return dh, jnp.sum(dz * nrm, axis=0, keepdims=True)


def _rms_bwd_twice(dz, h, g, dres):
    dh, dg = _rms_bwd_tile(dz, h, g, dres)
    return dh, dh, dg


def _assemble_dproj(dna, ddil_q, ddil_k, ddil_v, dgn, dgd, cos_t, sin_t, *, tm, name):
    n = dgn.shape[0]

    def body(*refs):
        dq_ref, dk_ref, dv_ref = refs[0:3]
        dil_in = refs[3:12]
        dgn_ref, dgd_ref, cos_ref, sin_ref, o_ref, scr = refs[12:18]
        o_ref[:, 0:512] = dq_ref[...]
        o_ref[:, 512:1024] = dk_ref[...].astype(BF16)
        o_ref[:, 1024:1536] = dv_ref[...].astype(BF16)
        cosv, sinv = cos_ref[...], sin_ref[...]
        for t in range(3):
            for gi, dil in enumerate(DIL_DILATIONS):
                val = _load_token_order(dil_in[t * 3 + gi], scr, dil, tm)
                if t < 2:
                    val = val * cosv + _swap_halves(val * sinv)
                c0 = 1536 + t * DIL_WIDTH + gi * 256
                o_ref[:, c0:c0 + 256] = val.astype(BF16)
        o_ref[:, 3840:4864] = dgn_ref[...]
        o_ref[:, 4864:5888] = dgd_ref[...]

    in_specs = [_rows(tm, NA_WIDTH)] * 3
    for _ in range(3):
        for dil in DIL_DILATIONS:
            in_specs.append(pl.BlockSpec((dil, tm // dil, 256), lambda i: (0, i, 0)))
    in_specs += [_rows(tm, D_MODEL)] * 2 + [_rows(tm, 256)] * 2
    return pl.pallas_call(
        body, name=name, grid=(n // tm,), in_specs=in_specs,
        out_specs=_rows(tm, IN_WIDTH), out_shape=_sds((n, IN_WIDTH), BF16),
        scratch_shapes=[_dil_scratch(tm)],
        compiler_params=_params("parallel"),
    )(*dna, *ddil_q, *ddil_k, *ddil_v, dgn, dgd, cos_t, sin_t)


N_ROW_OFF = 2 * NA_WIN_ROWS - 1
N_PAIRS = N_ROW_OFF - 1
RB_WIDTH = (N_ROW_OFF + 1) * GRID_W


def _na_bias(rb_ref, pair_scr):
    shape = (GRID_W, RB_WIDTH)
    qc = lax.broadcasted_iota(jnp.int32, shape, 0)
    qc2 = lax.broadcasted_iota(jnp.int32, (GRID_W, 128), 0)
    kc2 = lax.broadcasted_iota(jnp.int32, (GRID_W, 128), 1) & (GRID_W - 1)
    cs = jnp.clip(qc2 - 8, 0, GRID_W - 16)
    valid = (kc2 >= cs) & (kc2 < cs + 16)
    for hh in range(2):
        t = jnp.broadcast_to(rb_ref[hh], shape)
        t = pltpu.roll(t, RB_WIDTH - 15, 1)
        for b in range(6):
            t = jnp.where(((qc >> b) & 1) == 1, pltpu.roll(t, 1 << b, 1), t)
        t_odd = pltpu.roll(t, RB_WIDTH - GRID_W, 1)
        for ro in range(N_PAIRS):
            src = t if ro % 2 == 0 else t_odd
            base = (ro // 2) * 128
            pair_scr[hh, ro] = jnp.where(valid, src[:, base:base + 128], NEG_INF)


NA_GROUP_FWD = 8
NA_GROUP_BWD = 4


def _stack_heads(ref, r, scale=1.0):
    lane = lax.broadcasted_iota(jnp.int32, (GRID_W, 128), 1)
    t = ref[pl.ds(pl.multiple_of(r * GRID_W, GRID_W), GRID_W), :].astype(F32) * scale
    return jnp.concatenate([jnp.where(lane < 64, t, 0.0), jnp.where(lane >= 64, t, 0.0)], axis=0).astype(BF16)


def _unstack_heads(t2):
    lane = lax.broadcasted_iota(jnp.int32, (GRID_W, 128), 1)
    return jnp.where(lane < 64, t2[:GRID_W], t2[GRID_W:])


def _na_window(k_ref, v_ref, r, n_rows):
    rs = jnp.clip(r - NA_WIN_ROWS // 2, 0, n_rows - NA_WIN_ROWS)
    ro0 = (NA_WIN_ROWS - 1) - (r - rs)
    off = pl.multiple_of(rs * GRID_W, GRID_W)
    kw = k_ref[pl.ds(off, NA_WIN_ROWS * GRID_W), :]
    vw = v_ref[pl.ds(off, NA_WIN_ROWS * GRID_W), :]
    return kw, vw, off, ro0


def _na_probs(s_raw, pair_scr, ro0):
    bias = [jnp.concatenate([pair_scr[hh, ro0 + 2 * j] for j in range(NA_WIN_ROWS // 2)], axis=1)
            for hh in range(2)]
    s = s_raw + jnp.concatenate(bias, axis=0)
    m = jnp.max(s, axis=-1, keepdims=True)
    e = jnp.exp(s - m)
    return e * (1.0 / jnp.sum(e, axis=-1, keepdims=True))


def _na_qkv_specs(n):
    pairs = NA_WIDTH // 128
    return [pl.BlockSpec((n, 128), lambda h, first=t * pairs: (0, first + h)) for t in range(3)]


def _na_fwd(qkv, rb, *, name):
    n = qkv.shape[0]
    n_rows = n // GRID_W

    def body(ins, outs, scr):
        q_ref, k_ref, v_ref, rb_ref = ins
        o_ref, = outs
        pair_scr, = scr
        _na_bias(rb_ref, pair_scr)

        def group(g, carry):
            rows = [g * NA_GROUP_FWD + t for t in range(NA_GROUP_FWD)]
            wins = [_na_window(k_ref, v_ref, r, n_rows) for r in rows]
            raw = [lax.dot_general(_stack_heads(q_ref, r, QK_SCALE), w[0], NT_DIMS, preferred_element_type=F32)
                   for r, w in zip(rows, wins)]
            probs = [_na_probs(s, pair_scr, w[3]) for s, w in zip(raw, wins)]
            outs2 = [jnp.dot(p.astype(BF16), w[1], preferred_element_type=F32) for p, w in zip(probs, wins)]
            for r, o2 in zip(rows, outs2):
                o_ref[pl.ds(pl.multiple_of(r * GRID_W, GRID_W), GRID_W), :] = _unstack_heads(o2).astype(BF16)
            return carry

        lax.fori_loop(0, n_rows // NA_GROUP_FWD, group, 0)

    col = pl.BlockSpec((n, 128), lambda h: (0, h))
    return _call(
        body, name=name, grid=(NA_WIDTH // 128,),
        in_specs=_na_qkv_specs(n) + [pl.BlockSpec((2, 1, RB_WIDTH), lambda h: (h, 0, 0))],
        out_specs=[col], out_shape=[_sds((n, NA_WIDTH), BF16)],
        scratch_shapes=[pltpu.VMEM((2, N_PAIRS, GRID_W, 128), F32)],
        args=(qkv, qkv, qkv, rb))[0]


def _na_bwd(qkv, do, rb, *, name):
    n = qkv.shape[0]
    n_rows = n // GRID_W
    win = NA_WIN_ROWS * GRID_W

    def body(ins, outs, scr):
        q_ref, k_ref, v_ref, do_ref, rb_ref = ins
        dq_ref, dk_ref, dv_ref, drb_ref = outs
        pair_scr, acc_scr = scr
        _na_bias(rb_ref, pair_scr)
        acc_scr[...] = jnp.zeros_like(acc_scr)
        dk_ref[...] = jnp.zeros_like(dk_ref)
        dv_ref[...] = jnp.zeros_like(dv_ref)

        def group(g, carry):
            rows = [g * NA_GROUP_BWD + t for t in range(NA_GROUP_BWD)]
            wins = [_na_window(k_ref, v_ref, r, n_rows) for r in rows]
            qss = [_stack_heads(q_ref, r, QK_SCALE) for r in rows]
            doss = [_stack_heads(do_ref, r) for r in rows]
            raw = [lax.dot_general(qs, w[0], NT_DIMS, preferred_element_type=F32) for qs, w in zip(qss, wins)]
            dps = [lax.dot_general(dos, w[1], NT_DIMS, preferred_element_type=F32) for dos, w in zip(doss, wins)]
            probs = [_na_probs(s, pair_scr, w[3]) for s, w in zip(raw, wins)]
            dss = [p * (dp - jnp.sum(p * dp, axis=-1, keepdims=True)) for p, dp in zip(probs, dps)]
            dsbs = [ds.astype(BF16) for ds in dss]
            dq2s = [jnp.dot(dsb, w[0], preferred_element_type=F32) for dsb, w in zip(dsbs, wins)]
            dkws = [lax.dot_general(dsb, qs, TN_DIMS, preferred_element_type=F32) for dsb, qs in zip(dsbs, qss)]
            dvws = [lax.dot_general(p.astype(BF16), dos, TN_DIMS, preferred_element_type=F32)
                    for p, dos in zip(probs, doss)]
            for t, r in enumerate(rows):
                _, _, off, ro0 = wins[t]
                for hh in range(2):
                    for j in range(NA_WIN_ROWS // 2):
                        acc_scr[hh, ro0 + 2 * j] += dss[t][hh * GRID_W:(hh + 1) * GRID_W, j * 128:(j + 1) * 128]
                dq_ref[pl.ds(pl.multiple_of(r * GRID_W, GRID_W), GRID_W), :] = (
                    _unstack_heads(dq2s[t]) * QK_SCALE).astype(BF16)
                dk_ref[pl.ds(off, win), :] += dkws[t]
                dv_ref[pl.ds(off, win), :] += dvws[t]
            return carry

        lax.fori_loop(0, n_rows // NA_GROUP_BWD, group, 0)

        qc = lax.broadcasted_iota(jnp.int32, (N_PAIRS * GRID_W, 128), 0)
        for hh in range(2):
            t = acc_scr[hh].reshape(N_PAIRS * GRID_W, 128)
            for b in range(6):
                t = jnp.where(((qc >> b) & 1) == 1, pltpu.roll(t, 128 - (1 << b), 1), t)
            t = pltpu.roll(t, 15, 1)
            drb_ref[hh] = jnp.sum(t.reshape(N_PAIRS, GRID_W, 128), axis=1)

    col = pl.BlockSpec((n, 128), lambda h: (0, h))
    return _call(
        body, name=name, grid=(NA_WIDTH // 128,),
        in_specs=_na_qkv_specs(n) + [col, pl.BlockSpec((2, 1, RB_WIDTH), lambda h: (h, 0, 0))],
        out_specs=[col, col, col, pl.BlockSpec((2, N_PAIRS, 128), lambda h: (h, 0, 0))],
        out_shape=[_sds((n, NA_WIDTH), BF16), _sds((n, NA_WIDTH), F32), _sds((n, NA_WIDTH), F32),
                   _sds((8, N_PAIRS, 128), F32)],
        scratch_shapes=[pltpu.VMEM((2, N_PAIRS, GRID_W, 128), F32),
                        pltpu.VMEM((2, N_PAIRS, GRID_W, 128), F32)],
        args=(qkv, qkv, qkv, do, rb))


def _rpb_table(rpb2):
    t = jnp.pad(rpb2, ((0, 0), (0, 1), (0, GRID_W - rpb2.shape[-1])))
    return t.reshape(8, 1, RB_WIDTH)


def _rpb_grad(drb, *, name):
    kdim = drb.shape[1]

    def body(x_ref, o_ref):
        kk = lax.broadcasted_iota(jnp.int32, (128, 512), 0)
        jj = lax.broadcasted_iota(jnp.int32, (128, 512), 1)
        half, co = kk >> 6, kk & 63
        acc = jnp.zeros((8, 512), F32)
        for ro in range(N_PAIRS):
            hit = ((ro + half) == (jj >> 5)) & (co == (jj & 31)) & (co < 31)
            onehot = jnp.where(hit, 1.0, 0.0).astype(F32)
            acc = acc + jnp.dot(x_ref[:, ro * 128:(ro + 1) * 128], onehot, preferred_element_type=F32,
                                precision=lax.Precision.HIGHEST)
        o_ref[...] = acc

    return pl.pallas_call(
        body, name=name, grid=(1,),
        in_specs=[_const((8, kdim))], out_specs=_const((8, 512)), out_shape=_sds((8, 512), F32),
        compiler_params=_params("arbitrary"),
    )(drb)


DIL_GROUP = 2


def _dil_blocks(length):
    qb = min(128, length)
    return qb, min(qb + 2 * DIL_RADIUS, length), min(DIL_GROUP, length // qb)


def _stack_lanes(ref, t, qb, scale=1.0):
    lane = lax.broadcasted_iota(jnp.int32, (qb, 256), 1)
    val = ref[0, t * qb:(t + 1) * qb, :].astype(F32) * scale
    return jnp.concatenate([jnp.where((lane >> 6) == h, val, 0.0) for h in range(4)], axis=0).astype(BF16)


def _dil_window(k_ref, v_ref, blk, qb, win, length):
    start = pl.multiple_of(jnp.clip(blk * qb - DIL_RADIUS, 0, length - win), DIL_RADIUS)
    return k_ref[0, pl.ds(start, win), :], v_ref[0, pl.ds(start, win), :], start


def _dil_caps_init(caps_scr, qb, win):
    @pl.when((pl.program_id(0) == 0) & (pl.program_id(1) == 0))
    def _():
        gap = ((lax.broadcasted_iota(jnp.int32, (4 * qb, win), 0) & (qb - 1))
               - lax.broadcasted_iota(jnp.int32, (4 * qb, win), 1))
        for v in range(3):
            caps_scr[v] = jnp.where(jnp.abs(gap + v * DIL_RADIUS) <= DIL_RADIUS, jnp.inf, NEG_INF)


def _dil_mask(s, blk, start, qb, caps_scr):
    return jnp.minimum(s, caps_scr[(blk * qb - start) // DIL_RADIUS])


def _pick_heads(stacked, qb):
    lane = lax.broadcasted_iota(jnp.int32, (qb, 256), 1)
    out = jnp.zeros((qb, 256), stacked.dtype)
    for h in range(4):
        out = jnp.where((lane >> 6) == h, stacked[h * qb:(h + 1) * qb], out)
    return out


def _stack_head_cols(ref, t, qb):
    return jnp.concatenate([ref[0, t * qb:(t + 1) * qb, 64 * h:64 * h + 1] for h in range(4)], axis=0)


def _dil_fwd(q, k, v, *, name, after=None):
    dil, length, _ = q.shape
    qb, win, grp = _dil_blocks(length)
    extra = [] if after is None else [after]

    def body(q_ref, k_ref, v_ref, *rest):
        o_ref, lse_ref, caps_scr = rest[-3:]
        _dil_caps_init(caps_scr, qb, win)
        blks = [pl.program_id(1) * grp + t for t in range(grp)]
        wins = [_dil_window(k_ref, v_ref, b, qb, win, length) for b in blks]
        raw = [lax.dot_general(_stack_lanes(q_ref, t, qb, QK_SCALE), w[0], NT_DIMS, preferred_element_type=F32)
               for t, w in enumerate(wins)]
        lses, outs = [], []
        for t, (s, w) in enumerate(zip(raw, wins)):
            s = _dil_mask(s, blks[t], w[2], qb, caps_scr)
            m = jnp.max(s, axis=-1, keepdims=True)
            e = jnp.exp(s - m)
            norm = jnp.sum(e, axis=-1, keepdims=True)
            lses.append(m + jnp.log(norm))
            outs.append(jnp.dot((e * (1.0 / norm)).astype(BF16), w[1], preferred_element_type=F32))
        for t in range(grp):
            o_ref[0, t * qb:(t + 1) * qb, :] = _pick_heads(outs[t], qb)
            lse_ref[0, t * qb:(t + 1) * qb, :] = _pick_heads(jnp.broadcast_to(lses[t], (4 * qb, 256)), qb)

    seq = pl.BlockSpec((1, length, 256), lambda j, i: (j, 0, 0))
    blk = pl.BlockSpec((1, grp * qb, 256), lambda j, i: (j, i, 0))
    return pl.pallas_call(
        body, name=name, grid=(dil, length // (grp * qb)),
        in_specs=[blk, seq, seq] + [pl.BlockSpec(memory_space=pl.ANY)] * len(extra), out_specs=[blk, blk],
        out_shape=[_sds((dil, length, 256), F32)] * 2,
        scratch_shapes=[pltpu.VMEM((3, 4 * qb, win), F32)],
        compiler_params=_params("arbitrary", "arbitrary"),
    )(q, k, v, *extra)


def _dil_bwd(q, k, v, do, lse, cc, *, name):
    dil, length, _ = q.shape
    qb, win, grp = _dil_blocks(length)

    def body(q_ref, k_ref, v_ref, do_ref, lse_ref, cc_ref, dq_ref, dk_ref, dv_ref, caps_scr):
        _dil_caps_init(caps_scr, qb, win)

        @pl.when(pl.program_id(1) == 0)
        def _():
            dk_ref[...] = jnp.zeros_like(dk_ref)
            dv_ref[...] = jnp.zeros_like(dv_ref)

        blks = [pl.program_id(1) * grp + t for t in range(grp)]
        wins = [_dil_window(k_ref, v_ref, b, qb, win, length) for b in blks]
        qss = [_stack_lanes(q_ref, t, qb, QK_SCALE) for t in range(grp)]
        doss = [_stack_lanes(do_ref, t, qb) for t in range(grp)]
        raw = [lax.dot_general(qs, w[0], NT_DIMS, preferred_element_type=F32) for qs, w in zip(qss, wins)]
        dps = [lax.dot_general(dos, w[1], NT_DIMS, preferred_element_type=F32) for dos, w in zip(doss, wins)]
        probs = [jnp.exp(_dil_mask(s, blks[t], wins[t][2], qb, caps_scr) - _stack_head_cols(lse_ref, t, qb))
                 for t, s in enumerate(raw)]
        dsbs = [(p * (dp + _stack_head_cols(cc_ref, t, qb))).astype(BF16)
                for t, (p, dp) in enumerate(zip(probs, dps))]
        dq4s = [jnp.dot(dsb, w[0], preferred_element_type=F32) for dsb, w in zip(dsbs, wins)]
        dkws = [lax.dot_general(dsb, qs, TN_DIMS, preferred_element_type=F32) for dsb, qs in zip(dsbs, qss)]
        dvws = [lax.dot_general(p.astype(BF16), dos, TN_DIMS, preferred_element_type=F32)
                for p, dos in zip(probs, doss)]
        for t in range(grp):
            dq_ref[0, t * qb:(t + 1) * qb, :] = _pick_heads(dq4s[t], qb) * QK_SCALE
            dk_ref[0, pl.ds(wins[t][2], win), :] += dkws[t]
            dv_ref[0, pl.ds(wins[t][2], win), :] += dvws[t]

    seq = pl.BlockSpec((1, length, 256), lambda j, i: (j, 0, 0))
    blk = pl.BlockSpec((1, grp * qb, 256), lambda j, i: (j, i, 0))
    return pl.pallas_call(
        body, name=name, grid=(dil, length // (grp * qb)),
        in_specs=[blk, seq, seq, blk, blk, blk], out_specs=[blk, seq, seq],
        out_shape=[_sds((dil, length, 256), F32)] * 3,
        scratch_shapes=[pltpu.VMEM((3, 4 * qb, win), F32)],
        compiler_params=_params("arbitrary", "arbitrary"),
    )(q, k, v, do, lse, cc)


def _merge_weights(lses):
    m = jnp.maximum(jnp.maximum(lses[0], lses[1]), lses[2])
    es = [jnp.exp(t - m) for t in lses]
    inv = 1.0 / (es[0] + es[1] + es[2])
    return [e * inv for e in es]


def _dil_merge(outs, lses, *, tm, name):
    n = outs[0].shape[1]

    def body(*refs):
        o_in, l_in = refs[0:3], refs[3:6]
        y_ref, yb_ref, scr = refs[6:9]
        lv = [_load_token_order(l_in[g], scr, d, tm) for g, d in enumerate(DIL_DILATIONS)]
        ws = _merge_weights(lv)
        y = jnp.zeros((tm, 256), F32)
        for g, d in enumerate(DIL_DILATIONS):
            y = y + ws[g] * _load_token_order(o_in[g], scr, d, tm)
        y_ref[...] = y
        yb_ref[...] = y.astype(BF16)

    specs = [_dil_spec(d, tm) for d in DIL_DILATIONS]
    return pl.pallas_call(
        body, name=name, grid=(n // tm,), in_specs=specs + specs,
        out_specs=[_rows(tm, 256)] * 2, out_shape=[_sds((n, 256), F32), _sds((n, 256), BF16)],
        scratch_shapes=[_dil_scratch(tm)],
        compiler_params=_params("parallel"),
    )(*outs, *lses)


def _dil_merge_bwd(dy, y, lses, *, tm, name):
    n = dy.shape[0]

    def body(*refs):
        dy_ref, y_ref = refs[0:2]
        l_in = refs[2:5]
        do_out, cc_out = refs[5:8], refs[8:11]
        scr = refs[11]
        lv = [_load_token_order(l_in[g], scr, d, tm) for g, d in enumerate(DIL_DILATIONS)]
        ws = _merge_weights(lv)
        dyv = dy_ref[...]
        rr = lax.broadcasted_iota(jnp.int32, (256, 256), 0) >> 6
        cc = lax.broadcasted_iota(jnp.int32, (256, 256), 1) >> 6
        ones = jnp.where(rr == cc, 1.0, 0.0).astype(F32)
        tsum = jnp.dot(dyv * y_ref[...], ones, preferred_element_type=F32,
                       precision=lax.Precision.HIGHEST)
        for g, d in enumerate(DIL_DILATIONS):
            _store_dil_order(ws[g] * dyv, do_out[g], scr, d, tm)
            _store_dil_order(-ws[g] * tsum, cc_out[g], scr, d, tm)

    specs = [_dil_spec(d, tm) for d in DIL_DILATIONS]
    res = pl.pallas_call(
        body, name=name, grid=(n // tm,),
        in_specs=[_rows(tm, 256)] * 2 + specs,
        out_specs=specs + specs,
        out_shape=[_sds((d, n // d, 256), BF16) for d in DIL_DILATIONS]
                  + [_sds((d, n // d, 256), F32) for d in DIL_DILATIONS],
        scratch_shapes=[_dil_scratch(tm)],
        compiler_params=_params("parallel"),
    )(dy, y, *lses)
    return res[0:3], res[3:6]


_WEIGHTS = (("w_in", 1, 736), ("w_branch_na", 1, 128), ("w_branch_dil", 1, 128), ("w_out", 0, 128),
            ("w_up", 1, 512), ("w_down", 0, 512), ("w_ple_gate", 0, 128), ("w_ple_proj", 1, 128))
_W_IN, _W_BNA, _W_BD, _W_OUT, _W_UP, _W_DOWN, _W_PG, _W_PP = range(8)


def _to_full(gathered):
    return gathered.reshape(-1, gathered.shape[2])


def _to_chunks(widx, mat):
    return mat.reshape(N_DEV, _WEIGHTS[widx][2], mat.shape[1])


def _local_step(x, p_bf16, positions, target, g_mix, g_mlp, g_ple, g_final, rpb2,
                get_w_in, relay_rest, get_rest, send_grads):
    tm = 512
    half = HEAD_DIM // 2
    inv_freq = 10000.0 ** (-jnp.arange(half, dtype=F32) / half)
    ang = positions.astype(F32)[:, None] * inv_freq
    cos, sin = jnp.cos(ang), jnp.sin(ang)
    cos_t = jnp.tile(jnp.concatenate([cos, cos], axis=-1), (1, 4))
    sin_t = jnp.tile(jnp.concatenate([-sin, sin], axis=-1), (1, 4))
    rb = _rpb_table(rpb2)

    a = _rms_fwd(x, g_mix, tm=tm, name="rms_mix")
    w_in, token = get_w_in(a)
    na_width, qkv_width = 3 * NA_WIDTH, 3 * NA_WIDTH + 3 * DIL_WIDTH
    na_qkv = _matmul(a, w_in, tb=True, n_limit=na_width, out_dtype=BF16, tm=512, tn=na_width, tk=1024,
                     name="mm_in_na", after=token)
    proj = _matmul(a, w_in[na_width:qkv_width], tb=True, out_dtype=F32, tm=512, tn=3 * DIL_WIDTH, tk=1024,
                   name="mm_in_dil")
    gates = _matmul(a, w_in[qkv_width:], tb=True, out_dtype=BF16, tm=512, tn=2 * D_MODEL, tk=1024,
                    name="mm_in_gates", epilogue=lambda acc: (_sigmoid(acc),))
    sn, sd = (gates, 0), (gates, 1)
    dq_g, dk_g, dv_g = _split_proj(proj, cos_t, sin_t, tm=tm, name="split_proj")
    y_na = _na_fwd(na_qkv, rb, name="na_fwd")
    token = relay_rest(y_na)
    d_out, d_lse = [], []
    for g in range(3):
        o, lse = _dil_fwd(dq_g[g], dk_g[g], dv_g[g], name=f"dil_fwd{g}", after=token if g == 0 else None)
        d_out.append(o)
        d_lse.append(lse)
    y_dil, y_dil_b = _dil_merge(d_out, d_lse, tm=tm, name="dil_merge")
    w_bna, w_bd, w_out, w_up, w_down, w_pg, w_pp = get_rest(y_dil_b)
    bn = _matmul(y_na, w_bna, tb=True, out_dtype=BF16, tm=512, tn=1024, tk=512, name="mm_bna")
    bd, mixed = _matmul(y_dil_b, w_bd, tb=True, out_dtype=(BF16, BF16), tm=512, tn=1024, tk=256, name="mm_bd",
                        extra=(sn, bn, sd), epilogue=_gate_mix_tile)
    h1, c = _matmul(mixed, w_out, out_dtype=(F32, BF16), tm=512, tn=1024, tk=1024, name="mm_out",
                    extra=(x, g_mlp), epilogue=_residual_rms_tile)
    u, f = _matmul(c, w_up, tb=True, out_dtype=(BF16, BF16), tm=512, tn=2048, tk=1024, name="mm_up",
                   epilogue=lambda acc: (acc, jnp.square(jnp.maximum(acc, 0.0))))
    h2, e = _matmul(f, w_down, out_dtype=(F32, BF16), tm=512, tn=1024, tk=4096, name="mm_down",
                    extra=(h1, g_ple), epilogue=_residual_rms_tile)
    pp = _matmul(p_bf16, w_pp, tb=True, out_dtype=F32, tm=512, tn=1024, tk=256, name="mm_pp")

    dh3, dpp, dgt, dg_final, loss = _matmul(
        e, w_pg, out_dtype=(F32, BF16, BF16), tm=512, tn=1024, tk=1024, name="mm_pg_tail",
        extra=(pp, h2, target, g_final), epilogue=_tail_tile, n_colsum=2)
    loss = loss[:, :128]
    gw_pp = _matmul(p_bf16, dpp, ta=True, transpose_out=True, out_dtype=BF16, tm=256, tn=1024, tk=2048,
                    name="mm_gw_pp")
    gw_pg = _matmul(e, dgt, ta=True, out_dtype=BF16, tm=512, tn=1024, tk=2048, name="mm_gw_pg")
    dh2, dh2_b, dg_ple = _matmul(
        dgt, w_pg, tb=True, out_dtype=(F32, BF16), tm=512, tn=1024, tk=1024, name="mm_de",
        extra=(h2, g_ple, dh3), epilogue=_rms_bwd_twice, n_colsum=1)
    du = _matmul(dh2_b, w_down, tb=True, out_dtype=BF16, tm=512, tn=2048, tk=1024, name="mm_du",
                 extra=(u,), epilogue=lambda acc, uv: (acc * (2.0 * jnp.maximum(uv.astype(F32), 0.0)),))
    gw_down = _matmul(f, dh2_b, ta=True, out_dtype=BF16, tm=1024, tn=1024, tk=2048, name="mm_gw_down")
    gw_up = _matmul(c, du, ta=True, transpose_out=True, out_dtype=BF16, tm=512, tn=2048, tk=2048, name="mm_gw_up")
    dh1, dh1_b, dg_mlp = _matmul(
        du, w_up, out_dtype=(F32, BF16), tm=512, tn=1024, tk=4096, name="mm_dc",
        extra=(h1, g_mlp, dh2), epilogue=_rms_bwd_twice, n_colsum=1)
    dbn, dbd, dgn, dgd = _matmul(dh1_b, w_out, tb=True, out_dtype=(BF16,) * 4, tm=512, tn=1024, tk=1024,
                                 name="mm_dmixed", extra=(sn, bn, sd, bd), epilogue=_gate_bwd_tile)
    gw_out = _matmul(mixed, dh1_b, ta=True, out_dtype=BF16, tm=512, tn=1024, tk=2048, name="mm_gw_out")
    gw_bna = _matmul(y_na, dbn, ta=True, transpose_out=True, out_dtype=BF16, tm=512, tn=1024, tk=2048,
                     name="mm_gw_bna")
    dy_na = _matmul(dbn, w_bna, out_dtype=BF16, tm=512, tn=512, tk=1024, name="mm_dy_na")
    gw_bd = _matmul(y_dil_b, dbd, ta=True, transpose_out=True, out_dtype=BF16, tm=256, tn=1024, tk=2048,
                    name="mm_gw_bd")
    token = send_grads((_W_PP, _W_PG, _W_DOWN, _W_UP, _W_OUT, _W_BNA, _W_BD),
                       (gw_pp, gw_pg, gw_down, gw_up, gw_out, gw_bna, gw_bd))
    dy_dil = _matmul(dbd, w_bd, out_dtype=F32, tm=512, tn=256, tk=1024, name="mm_dy_dil", after=token)
    dna = _na_bwd(na_qkv, dy_na, rb, name="na_bwd")
    drpb = _rpb_grad(dna[3].reshape(8, -1), name="rpb_grad")
    do_g, cc_g = _dil_merge_bwd(dy_dil, y_dil, d_lse, tm=tm, name="dil_merge_bwd")
    ddq, ddk, ddv = [], [], []
    for g in range(3):
        r = _dil_bwd(dq_g[g], dk_g[g], dv_g[g], do_g[g], d_lse[g], cc_g[g], name=f"dil_bwd{g}")
        ddq.append(r[0])
        ddk.append(r[1])
        ddv.append(r[2])
    dproj = _assemble_dproj(dna[0:3], ddq, ddk, ddv, dgn, dgd, cos_t, sin_t, tm=tm, name="assemble_dproj")
    gw_in = _matmul(a, dproj, ta=True, transpose_out=True, out_dtype=BF16, tm=512, tn=2944, tk=2048, name="mm_gw_in")
    token = send_grads((_W_IN,), (gw_in,))
    dx, dg_mix = _matmul(
        dproj, w_in, out_dtype=(F32,), tm=512, tn=1024, tk=5888, name="mm_da", after=token,
        extra=(x, g_mix, dh1), epilogue=_rms_bwd_tile, n_colsum=1)
    return loss, dx, (dg_mix, dg_mlp, dg_ple, dg_final), drpb


def _cast_bf16(t, *, name):
    def body(t_ref, o_ref):
        o_ref[...] = t_ref[...].astype(BF16)

    rows, cols = t.shape
    tr = 256 if rows % 256 == 0 else rows
    blk = pl.BlockSpec((tr, cols), lambda i: (i, 0))
    return pl.pallas_call(body, name=name, grid=(rows // tr,), in_specs=[blk], out_specs=blk,
                          out_shape=_sds(t.shape, BF16), compiler_params=_params("parallel"))(t)


def _adamw(w, g, m, v):
    m = ADAM_B1 * m + (1.0 - ADAM_B1) * g
    v = ADAM_B2 * v + (1.0 - ADAM_B2) * (g * g)
    m_hat = m / (1.0 - ADAM_B1 ** ADAM_STEP)
    v_hat = v / (1.0 - ADAM_B2 ** ADAM_STEP)
    delta = -ADAM_LR * (m_hat / (jnp.sqrt(v_hat) + ADAM_EPS) + ADAM_WD * w)
    return delta, m, v


def _sum_adamw(parts, w, m, v, *, tr, name, own=None, transposed=False):
    rows, cols = w.shape
    n_pre = 0 if own is None else 1

    def body(*refs):
        p_ref, w_ref, m_ref, v_ref = refs[n_pre:n_pre + 4]
        g_ref, d_ref, nm_ref, nv_ref = refs[-4:]
        g = (p_ref[0] if own is None else refs[n_pre + 4][...]).astype(F32)
        for s in range(1, N_DEV):
            g = g + p_ref[s].astype(F32)
        if transposed:
            g = g.T
        g_ref[...] = g
        d_ref[...], nm_ref[...], nv_ref[...] = _adamw(w_ref[...], g, m_ref[...], v_ref[...])

    if transposed:
        blk = pl.BlockSpec((rows, tr), lambda i, *_: (0, i))
        g_rows, steps = rows, cols // tr
    else:
        blk = pl.BlockSpec((tr, cols), lambda i, *_: (i, 0))
        g_rows, steps = cols, rows // tr
    in_specs = [pl.BlockSpec((N_DEV, tr, g_rows), lambda i, *_: (0, i, 0)), blk, blk, blk]
    args = [parts, w, m, v]
    if own is not None:
        in_specs.append(pl.BlockSpec((None, tr, g_rows), lambda i, idx: (idx[0], i, 0)))
        args = [own[1]] + args + [own[0]]
    return pl.pallas_call(
        body, name=name,
        grid_spec=pltpu.PrefetchScalarGridSpec(num_scalar_prefetch=n_pre, grid=(steps,), in_specs=in_specs,
                                               out_specs=[blk] * 4),
        out_shape=[_sds((rows, cols), F32)] * 4,
        compiler_params=_params("parallel"),
    )(*args)


_RPB_SIZE = 8 * 15 * 31


def _pack_small(g_mix, g_mlp, g_ple, g_final, rpb, loss_row):
    flat = jnp.concatenate([g_mix.reshape(-1), g_mlp.reshape(-1), g_ple.reshape(-1), g_final.reshape(-1),
                            rpb.reshape(-1), jnp.zeros((3840 - _RPB_SIZE,), F32), loss_row.reshape(-1),
                            jnp.zeros((128,), F32)])
    return flat.reshape(64, 128)


def _unpack_small(t):
    flat = t.reshape(-1)
    return (flat[0:1024].reshape(1, 1024), flat[4096:4096 + _RPB_SIZE].reshape(1, 8, 15, 31),
            flat[1024:2048].reshape(1, 1024), flat[2048:3072].reshape(1, 1024), flat[3072:4096])


def kernel(x, p, positions, g_mix, w_in, rpb, w_branch_na, w_branch_dil, w_out, g_mlp, w_up, w_down, g_ple, w_ple_gate, w_ple_proj, g_final, loss_target, m_g_mix, m_w_in, m_rpb, m_w_branch_na, m_w_branch_dil, m_w_out, m_g_mlp, m_w_up, m_w_down, m_g_ple, m_w_ple_gate, m_w_ple_proj, m_g_final, v_g_mix, v_w_in, v_rpb, v_w_branch_na, v_w_branch_dil, v_w_out, v_g_mlp, v_w_up, v_w_down, v_g_ple, v_w_ple_gate, v_w_ple_proj, v_g_final):
    sharded = dict(w_in=(w_in, m_w_in, v_w_in), w_branch_na=(w_branch_na, m_w_branch_na, v_w_branch_na),
                   w_branch_dil=(w_branch_dil, m_w_branch_dil, v_w_branch_dil), w_out=(w_out, m_w_out, v_w_out),
                   w_up=(w_up, m_w_up, v_w_up), w_down=(w_down, m_w_down, v_w_down),
                   w_ple_gate=(w_ple_gate, m_w_ple_gate, v_w_ple_gate),
                   w_ple_proj=(w_ple_proj, m_w_ple_proj, v_w_ple_proj))
    shards = {k: tuple(t[0] for t in val) for k, val in sharded.items()}

    me = _my_index()

    shards["w_in"] = tuple(t.T for t in shards["w_in"])

    w_in_b = _cast_bf16(shards["w_in"][0], name="cast_w_in")
    rest_b = [shards[name][0].astype(BF16).T if axis == 1 else shards[name][0].astype(BF16)
              for name, axis, _ in _WEIGHTS[1:]]
    first_in, token_in = _start_copies(_first_leg_copies, [w_in_b], [_sds((N_DEV,) + w_in_b.shape, BF16)], 4,
                                       name="start_gather_w_in")

    def whole(landed, mine):
        return _to_full(lax.dynamic_update_index_in_dim(landed, mine, me, 0))

    rest = {}

    def get_w_in(after):
        (mine,), landed = _wait_copies(_first_leg_copies, first_in, after, name="wait_gather_w_in")
        second, token = _start_copies(_second_leg_copies, [], landed, 3, name="start_forward_w_in")
        _, (landed,) = _wait_copies(_second_leg_copies, second, token, name="wait_forward_w_in")
        rest["first"], token = _start_copies(_first_leg_copies, rest_b,
                                             [_sds((N_DEV,) + t.shape, BF16) for t in rest_b], 4 * len(rest_b),
                                             name="start_gather_rest", after=landed)
        return whole(landed, mine), token

    def relay_rest(after):
        rest["mine"], landed = _wait_copies(_first_leg_copies, rest["first"], after, name="wait_gather_rest")
        rest["second"], token = _start_copies(_second_leg_copies, [], landed, 3 * len(rest_b),
                                              name="start_forward_rest")
        return token

    def get_rest(after):
        _, landed = _wait_copies(_second_leg_copies, rest["second"], after, name="wait_forward_rest")
        return [whole(t, own) for t, own in zip(landed, rest["mine"])]

    sent = []

    def send_grads(indices, grads):
        chunked = [_to_chunks(i, g) for i, g in zip(indices, grads)]
        handle, token = _start_copies(_exchange_copies, chunked, [_sds(t.shape, BF16) for t in chunked],
                                      7 * len(chunked),
                                      name="start_exchange_" + ("w_in" if indices == (_W_IN,) else "rest"))
        sent.append((indices, handle))
        return token

    g_mix_0 = g_mix + token_in[0:1, 0:1]
    loss, dx, dgs, drpb = _local_step(
        x[0], p[0, 0].astype(BF16), positions[0], loss_target[0],
        g_mix_0, g_mlp, g_ple, g_final.reshape(1, -1), rpb[0], get_w_in, relay_rest, get_rest, send_grads)

    drpb3 = drpb.reshape(8, 16, 32)[:, :15, :31]
    small = _pack_small(dgs[0], dgs[1], dgs[2], dgs[3], drpb3, loss)
    share, done = _start_copies(_gather_copies, [small], [_sds((N_DEV,) + small.shape, F32)], 7,
                                name="start_share_small")

    out = {}

    def update_shards(indices, handle, after):
        chunked, landed = _wait_copies(_exchange_copies, handle, after,
                                       name="wait_exchange_" + ("w_in" if indices == (_W_IN,) else "rest"))
        for i, part, mine in zip(indices, landed, chunked):
            name = _WEIGHTS[i][0]
            w, m, v = shards[name]
            turned = _WEIGHTS[i][1] == 1 and i != _W_IN
            res = _sum_adamw(part, w, m, v, tr=368 if i == _W_IN else 128, name="adamw_" + name,
                             own=(mine, me.reshape(1).astype(jnp.int32)), transposed=turned)
            out[name] = [(t.T if i == _W_IN else t)[None] for t in res]
        return res[0]

    done = update_shards(*sent[0], done)
    (small,), (small_landed,) = _wait_copies(_gather_copies, share, done, name="wait_share_small")
    small_all = lax.dynamic_update_index_in_dim(small_landed, small, me, 0)
    small_w = _pack_small(g_mix, g_mlp, g_ple, g_final, rpb, jnp.zeros((128,), F32))
    small_m = _pack_small(m_g_mix, m_g_mlp, m_g_ple, m_g_final, m_rpb, jnp.zeros((128,), F32))
    small_v = _pack_small(v_g_mix, v_g_mlp, v_g_ple, v_g_final, v_rpb, jnp.zeros((128,), F32))
    res = _sum_adamw(small_all, small_w, small_m, small_v, tr=64, name="adamw_small")
    unpacked = [_unpack_small(t) for t in res]
    for i, name in enumerate(("g_mix", "rpb", "g_mlp", "g_ple", "g_final")):
        out[name] = [u[i] for u in unpacked]
    loss_total = res[0][62, 0]
    update_shards(*sent[1], res[0])

    order = ("g_mix", "w_in", "rpb", "w_branch_na", "w_branch_dil", "w_out", "g_mlp", "w_up", "w_down",
             "g_ple", "w_ple_gate", "w_ple_proj", "g_final")
    grads = [out[k][0] for k in order]
    deltas = [out[k][1] for k in order]
    new_m = [out[k][2] for k in order]
    new_v = [out[k][3] for k in order]
    return (loss_total, dx[None], *grads, *deltas, *new_m, *new_v)
```

```python
import jax
import jax.numpy as jnp
from jax import lax
from jax.experimental import pallas as pl
from jax.experimental.pallas import tpu as pltpu

F32 = jnp.float32
BF16 = jnp.bfloat16

D_MODEL = 1024
HEAD_DIM = 64
GRID_W = 64
NA_WIDTH = 512
DIL_WIDTH = 768
IN_WIDTH = 5888
DIL_DILATIONS = (1, 4, 16)
DIL_RADIUS = 64
NA_WIN_ROWS = 8
RMS_EPS = 1e-6
NEG_INF = -1e30
QK_SCALE = HEAD_DIM ** -0.5

ADAM_LR = 0.001
ADAM_B1 = 0.9
ADAM_B2 = 0.999
ADAM_EPS = 1e-08
ADAM_WD = 0.01
ADAM_STEP = 10

N_DEV = 8
VMEM_LIMIT = 56 * 1024 * 1024
EPILOGUE_ROWS = 256
MESH = pl.DeviceIdType.MESH

NT_DIMS = (((1,), (1,)), ((), ()))
TN_DIMS = (((0,), (0,)), ((), ()))


def _sds(shape, dtype):
    return jax.ShapeDtypeStruct(shape, dtype)


def _params(*sem):
    return pltpu.CompilerParams(dimension_semantics=sem, vmem_limit_bytes=VMEM_LIMIT)


def _rows(tm, width, col=0):
    return pl.BlockSpec((tm, width), lambda i, c=col: (i, c))


def _const(shape):
    zeros = (0,) * len(shape)
    return pl.BlockSpec(shape, lambda i: zeros)


def _my_index():
    return 4 * lax.axis_index("x") + 2 * lax.axis_index("y") + lax.axis_index("c")


def _peer(k):
    x, y, c = lax.axis_index("x"), lax.axis_index("y"), lax.axis_index("c")
    px = 1 - x if k & 4 else x
    py = 1 - y if k & 2 else y
    pc = 1 - c if k & 1 else c
    return (px, py, pc), 4 * px + 2 * py + pc


def _call(body, *, name, grid, in_specs, out_specs, out_shape, scratch_shapes, args, after=None):
    n_in, n_out = len(in_specs), len(out_specs)
    extra = [] if after is None else [after]
    n_x = n_in + len(extra)

    def plain(*refs):
        body(refs[:n_in], refs[n_x:n_x + n_out], refs[n_x + n_out:])

    res = pl.pallas_call(plain, name=name, grid=grid,
                         in_specs=list(in_specs) + [pl.BlockSpec(memory_space=pl.ANY)] * len(extra),
                         out_specs=out_specs, out_shape=out_shape, scratch_shapes=scratch_shapes,
                         compiler_params=_params(*(("arbitrary",) * len(grid))))(*args, *extra)
    return list(res)


_HBM_SPEC = pl.BlockSpec(memory_space=pltpu.HBM)
_SEM_SPEC = pl.BlockSpec(memory_space=pltpu.SEMAPHORE)
_SIDE_EFFECT = pltpu.SideEffectType.DATAFLOW_SIDE_EFFECTING


_FIRST_LEG = (1, 2, 4, 6)
_SECOND_LEG = (2, 4, 6)


def _gather_copies(srcs, lands, send, recv, sending):
    me = _my_index()
    out = []
    for w in range(len(srcs)):
        for k in range(1, N_DEV):
            dev, idx = _peer(k)
            out.append(pltpu.make_async_remote_copy(
                src_ref=srcs[w], dst_ref=lands[w].at[me if sending else idx],
                send_sem=send.at[w * 7 + k - 1], recv_sem=recv.at[w * 7 + k - 1],
                device_id=dev, device_id_type=MESH))
    return out


def _first_leg_copies(srcs, lands, send, recv, sending):
    me = _my_index()
    out = []
    for w in range(len(srcs)):
        for j, k in enumerate(_FIRST_LEG):
            dev, idx = _peer(k)
            out.append(pltpu.make_async_remote_copy(
                src_ref=srcs[w], dst_ref=lands[w].at[me if sending else idx],
                send_sem=send.at[w * 4 + j], recv_sem=recv.at[w * 4 + j],
                device_id=dev, device_id_type=MESH))
    return out


def _second_leg_copies(srcs, lands, send, recv, sending):
    sibling, _ = _peer(1)
    out = []
    for w in range(len(lands)):
        for j, k in enumerate(_SECOND_LEG):
            slot = _peer(k if sending else k ^ 1)[1]
            out.append(pltpu.make_async_remote_copy(
                src_ref=lands[w].at[slot], dst_ref=lands[w].at[slot],
                send_sem=send.at[w * 3 + j], recv_sem=recv.at[w * 3 + j],
                device_id=sibling, device_id_type=MESH))
    return out


def _exchange_copies(srcs, lands, send, recv, sending):
    out = []
    for w in range(len(srcs)):
        for k in range(1, N_DEV):
            dev, idx = _peer(k)
            out.append(pltpu.make_async_remote_copy(
                src_ref=srcs[w].at[idx], dst_ref=lands[w].at[k],
                send_sem=send.at[w * 7 + k - 1], recv_sem=recv.at[w * 7 + k - 1],
                device_id=dev, device_id_type=MESH))
    return out


def _start_copies(make, srcs, lands, n_copies, *, name, after=None):
    n_src, n_buf = len(srcs), len(srcs) + len(lands)
    extra = [] if after is None else [after]

    def body(*refs):
        send, recv = refs[n_buf + len(extra)], refs[n_buf + len(extra) + 1]
        for cp in make(refs[:n_src], refs[n_src:n_buf], send, recv, True):
            cp.start()
        refs[-1][...] = jnp.zeros_like(refs[-1])

    bufs = list(srcs) + [lax.empty(t.shape, t.dtype) if isinstance(t, jax.ShapeDtypeStruct) else t for t in lands]
    res = pl.pallas_call(
        body, name=name,
        out_shape=(pltpu.SemaphoreType.DMA((n_copies,)), pltpu.SemaphoreType.DMA((n_copies,)),
                   *[pltpu.HBM(t.shape, t.dtype) for t in bufs], _sds((8, 128), F32)),
        in_specs=[_HBM_SPEC] * n_buf + [pl.BlockSpec(memory_space=pl.ANY)] * len(extra),
        out_specs=(_SEM_SPEC, _SEM_SPEC, *([_HBM_SPEC] * n_buf), pl.BlockSpec(memory_space=pltpu.VMEM)),
        input_output_aliases={i: 2 + i for i in range(n_buf)},
        compiler_params=pltpu.CompilerParams(has_side_effects=_SIDE_EFFECT),
    )(*[pltpu.with_memory_space_constraint(t, pltpu.HBM) for t in bufs], *extra)
    return (n_src, res[0], res[1], res[2:2 + n_buf]), res[-1]


def _wait_copies(make, handle, after, *, name):
    n_src, send_sems, recv_sems, bufs = handle
    n_buf = len(bufs)

    def body(*refs):
        for cp in make(refs[:n_src], refs[n_src:n_buf], refs[n_buf], refs[n_buf + 1], False):
            cp.wait_send()
            cp.wait_recv()

    res = pl.pallas_call(
        body, name=name,
        out_shape=tuple(pltpu.HBM(t.shape, t.dtype) for t in bufs),
        in_specs=[_HBM_SPEC] * n_buf + [_SEM_SPEC, _SEM_SPEC, pl.BlockSpec(memory_space=pl.ANY)],
        out_specs=tuple([_HBM_SPEC] * n_buf),
        input_output_aliases={i: i for i in range(n_buf)},
        compiler_params=pltpu.CompilerParams(has_side_effects=_SIDE_EFFECT),
    )(*bufs, send_sems, recv_sems, after)
    return list(res[:n_src]), list(res[n_src:])


def _matmul(a, b, *, ta=False, tb=False, out_dtype, tm, tn, tk, name, after=None, extra=(), epilogue=None,
            n_colsum=0, transpose_out=False, n_limit=None):
    m, k = (a.shape[1], a.shape[0]) if ta else a.shape
    n = n_limit or (b.shape[0] if tb else b.shape[1])
    tm, tn, tk = min(tm, m), min(tn, n), min(tk, k)
    nk = k // tk
    dims = (((0 if ta else 1,), (1 if tb else 0,)), ((), ()))
    out_dtypes = out_dtype if isinstance(out_dtype, tuple) else (out_dtype,)
    n_tiles = len(out_dtypes)

    def add_colsums(o_refs, sums):
        i = pl.program_id(1)
        for s_ref, val in zip(o_refs[n_tiles:], sums):
            @pl.when(i == 0)
            def _(s_ref=s_ref, val=val):
                s_ref[...] = val

            @pl.when(i > 0)
            def _(s_ref=s_ref, val=val):
                s_ref[...] += val

    def finish(acc, x_refs, o_refs):
        vals = (acc,) if epilogue is None else epilogue(acc, *[r[...] for r in x_refs])
        for o_ref, val in zip(o_refs[:n_tiles], vals[:n_tiles]):
            o_ref[...] = (val.T if transpose_out else val).astype(o_ref.dtype)
        add_colsums(o_refs, vals[n_tiles:])

    chunk = EPILOGUE_ROWS if (nk == 1 and epilogue is not None and not ta and tm % EPILOGUE_ROWS == 0) else None

    def body(ins, outs, acc):
        a_ref, b_ref = ins[:2]
        if chunk is not None:
            sums = None
            for r0 in range(0, tm, chunk):
                part = lax.dot_general(a_ref[r0:r0 + chunk, :], b_ref[...], dims, preferred_element_type=F32)
                vals = epilogue(part, *[r[...] if r.shape[0] == 1 else r[r0:r0 + chunk, :] for r in ins[2:]])
                for o_ref, val in zip(outs[:n_tiles], vals[:n_tiles]):
                    o_ref[r0:r0 + chunk, :] = val.astype(o_ref.dtype)
                sums = vals[n_tiles:] if sums is None else [s + v for s, v in zip(sums, vals[n_tiles:])]
            add_colsums(outs, sums)
            return
        part = lax.dot_general(a_ref[...], b_ref[...], dims, preferred_element_type=F32)
        if nk == 1:
            finish(part, ins[2:], outs)
            return
        acc_ref, = acc
        kk = pl.program_id(2)

        @pl.when(kk == 0)
        def _():
            acc_ref[...] = part

        @pl.when(kk > 0)
        def _():
            acc_ref[...] += part

        @pl.when(kk == nk - 1)
        def _():
            finish(acc_ref[...], ins[2:], outs)

    a_spec = (pl.BlockSpec((tk, tm), lambda j, i, kk: (kk, i)) if ta
              else pl.BlockSpec((tm, tk), lambda j, i, kk: (i, kk)))
    b_spec = (pl.BlockSpec((tn, tk), lambda j, i, kk: (j, kk)) if tb
              else pl.BlockSpec((tk, tn), lambda j, i, kk: (kk, j)))
    tile = pl.BlockSpec((tm, tn), lambda j, i, kk: (i, j))
    row = pl.BlockSpec((1, tn), lambda j, i, kk: (0, j))

    def x_spec(t):
        if isinstance(t, tuple):
            return pl.BlockSpec((tm, tn), lambda j, i, kk, first=t[1] * (n // tn): (i, first + j))
        return row if t.shape[0] == 1 else tile

    out_tile, out_dims = (pl.BlockSpec((tn, tm), lambda j, i, kk: (j, i)), (n, m)) if transpose_out else (tile, (m, n))
    res = _call(
        body, name=name, grid=(n // tn, m // tm, nk),
        in_specs=[a_spec, b_spec] + [x_spec(t) for t in extra],
        out_specs=[out_tile] * n_tiles + [row] * n_colsum,
        out_shape=[_sds(out_dims, dt) for dt in out_dtypes] + [_sds((1, n), F32)] * n_colsum,
        scratch_shapes=[] if nk == 1 else [pltpu.VMEM((tm, tn), F32)],
        args=(a, b, *[t[0] if isinstance(t, tuple) else t for t in extra]), after=after)
    return res if isinstance(out_dtype, tuple) or n_colsum else res[0]


def _rstd(h):
    return lax.rsqrt(jnp.mean(h * h, axis=-1, keepdims=True) + RMS_EPS)


def _sigmoid(z):
    return 1.0 / (1.0 + jnp.exp(-z))


def _rms_fwd(x, g, *, tm, name):
    n = x.shape[0]

    def body(x_ref, g_ref, o_ref):
        h = x_ref[...]
        o_ref[...] = (h * _rstd(h) * g_ref[...]).astype(BF16)

    return pl.pallas_call(
        body, name=name, grid=(n // tm,),
        in_specs=[_rows(tm, D_MODEL), _const((1, D_MODEL))],
        out_specs=_rows(tm, D_MODEL), out_shape=_sds((n, D_MODEL), BF16),
        compiler_params=_params("parallel"),
    )(x, g)


def _swap_halves(t):
    lane = lax.broadcasted_iota(jnp.int32, (t.shape[0], 128), 1)
    pieces = [t[:, c:c + 128] for c in range(0, t.shape[1], 128)]
    return jnp.concatenate([jnp.where((lane & 63) < 32, pltpu.roll(h, 96, 1), pltpu.roll(h, 32, 1))
                            for h in pieces], axis=1)


def _dil_spec(dil, tm):
    return pl.BlockSpec((dil, tm // dil, 256), lambda i: (0, i, 0))


def _dil_scratch(tm):
    return pltpu.VMEM((2, tm, 128), F32)


def _load_token_order(src, scr, dil, tm):
    if dil == 1:
        return src[0]
    for j in range(dil):
        for c in range(2):
            scr[c, pl.ds(j, tm // dil, stride=dil), :] = src[j, :, c * 128:(c + 1) * 128]
    return jnp.concatenate([scr[0], scr[1]], axis=1)


def _store_dil_order(val, dst, scr, dil, row0=0):
    rows = val.shape[0]
    if dil == 1:
        dst[0, row0:row0 + rows, :] = val.astype(dst.dtype)
        return
    for c in range(2):
        scr[c] = val[:, c * 128:(c + 1) * 128]
    for j in range(dil):
        for c in range(2):
            dst[j, row0 // dil:(row0 + rows) // dil, c * 128:(c + 1) * 128] = (
                scr[c, pl.ds(j, rows // dil, stride=dil), :].astype(dst.dtype))


def _project_dil(a, w_t, cos_t, sin_t, *, tm, name):
    n = a.shape[0]
    n_dil = len(DIL_DILATIONS)
    chunk = min(EPILOGUE_ROWS, tm)

    def body(a_ref, w_ref, cos_ref, sin_ref, *rest):
        outs, scr = rest[:3 * n_dil], rest[3 * n_dil]
        for r0 in range(0, tm, chunk):
            part = lax.dot_general(a_ref[r0:r0 + chunk, :], w_ref[...], NT_DIMS, preferred_element_type=F32)
            cosv, sinv = cos_ref[r0:r0 + chunk, :], sin_ref[r0:r0 + chunk, :]
            for t in range(3):
                for gi, dil in enumerate(DIL_DILATIONS):
                    c0 = (t * n_dil + gi) * 256
                    val = part[:, c0:c0 + 256]
                    if t < 2:
                        val = val * cosv + _swap_halves(val) * sinv
                    _store_dil_order(val, outs[t * n_dil + gi], scr, dil, r0)

    out_specs, out_shape = [], []
    for _ in range(3):
        for dil in DIL_DILATIONS:
            out_specs.append(pl.BlockSpec((dil, tm // dil, 256), lambda i: (0, i, 0)))
            out_shape.append(_sds((dil, n // dil, 256), BF16))
    res = pl.pallas_call(
        body, name=name, grid=(n // tm,),
        in_specs=[_rows(tm, D_MODEL), _const(w_t.shape), _rows(tm, 256), _rows(tm, 256)],
        out_specs=out_specs, out_shape=out_shape,
        scratch_shapes=[pltpu.VMEM((2, chunk, 128), F32)],
        compiler_params=_params("parallel"),
    )(a, w_t, cos_t, sin_t)
    return res[0:3], res[3:6], res[6:9]


def _residual_rms_tile(delta, h, g):
    hn = h + delta
    return hn, hn * _rstd(hn) * g


def _gate_mix_tile(b2, s1, b1, s2):
    return b2, s1.astype(F32) * b1.astype(F32) + s2.astype(F32) * b2


def _gate_bwd_tile(dm, s1, b1, s2, b2):
    s1, b1, s2, b2 = (t.astype(F32) for t in (s1, b1, s2, b2))
    return dm * s1, dm * s2, dm * b1 * s1 * (1.0 - s1), dm * b2 * s2 * (1.0 - s2)


def _tail_tile(gt, pp, h2, target, g):
    sg = _sigmoid(gt)
    h3 = h2 + sg * pp
    r3 = _rstd(h3)
    n3 = h3 * r3
    err = n3 * g - target
    loss = 0.5 * jnp.sum(jnp.sum(err * err, axis=-1, keepdims=True) / D_MODEL)
    dy = err / D_MODEL
    dn = dy * g
    dh3 = r3 * (dn - n3 * jnp.mean(dn * n3, axis=-1, keepdims=True))
    return (dh3, dh3 * sg, dh3 * pp * sg * (1.0 - sg),
            jnp.sum(dy * n3, axis=0, keepdims=True), jnp.full((1, gt.shape[1]), loss, F32))


def _rms_bwd_tile(dz, h, g, dres):
    r = _rstd(h)
    nrm = h * r
    dn = dz * g
    dh = dres + r * (dn - nrm * jnp.mean(dn * nrm, axis=-1, keepdims=True))
    return dh, jnp.sum(dz * nrm, axis=0, keepdims=True)


def _rms_bwd_twice(dz, h, g, dres):
    dh, dg = _rms_bwd_tile(dz, h, g, dres)
    return dh, dh, dg


def _assemble_dproj(dna, ddil_q, ddil_k, ddil_v, dgn, dgd, cos_t, sin_t, *, tm, name):
    n = dgn.shape[0]

    def body(*refs):
        dq_ref, dk_ref, dv_ref = refs[0:3]
        dil_in = refs[3:12]
        dgn_ref, dgd_ref, cos_ref, sin_ref, o_ref, scr = refs[12:18]
        o_ref[:, 0:512] = dq_ref[...]
        o_ref[:, 512:1024] = dk_ref[...].astype(BF16)
        o_ref[:, 1024:1536] = dv_ref[...].astype(BF16)
        cosv, sinv = cos_ref[...], sin_ref[...]
        for t in range(3):
            for gi, dil in enumerate(DIL_DILATIONS):
                val = _load_token_order(dil_in[t * 3 + gi], scr, dil, tm)
                if t < 2:
                    val = val * cosv + _swap_halves(val * sinv)
                c0 = 1536 + t * DIL_WIDTH + gi * 256
                o_ref[:, c0:c0 + 256] = val.astype(BF16)
        o_ref[:, 3840:4864] = dgn_ref[...]
        o_ref[:, 4864:5888] = dgd_ref[...]

    in_specs = [_rows(tm, NA_WIDTH)] * 3
    for _ in range(3):
        for dil in DIL_DILATIONS:
            in_specs.append(pl.BlockSpec((dil, tm // dil, 256), lambda i: (0, i, 0)))
    in_specs += [_rows(tm, D_MODEL)] * 2 + [_rows(tm, 256)] * 2
    return pl.pallas_call(
        body, name=name, grid=(n // tm,), in_specs=in_specs,
        out_specs=_rows(tm, IN_WIDTH), out_shape=_sds((n, IN_WIDTH), BF16),
        scratch_shapes=[_dil_scratch(tm)],
        compiler_params=_params("parallel"),
    )(*dna, *ddil_q, *ddil_k, *ddil_v, dgn, dgd, cos_t, sin_t)


N_ROW_OFF = 2 * NA_WIN_ROWS - 1
N_PAIRS = N_ROW_OFF - 1
RB_WIDTH = (N_ROW_OFF + 1) * GRID_W


def _na_bias(rb_ref, pair_scr):
    shape = (GRID_W, RB_WIDTH)
    qc = lax.broadcasted_iota(jnp.int32, shape, 0)
    qc2 = lax.broadcasted_iota(jnp.int32, (GRID_W, 128), 0)
    kc2 = lax.broadcasted_iota(jnp.int32, (GRID_W, 128), 1) & (GRID_W - 1)
    cs = jnp.clip(qc2 - 8, 0, GRID_W - 16)
    valid = (kc2 >= cs) & (kc2 < cs + 16)
    for hh in range(2):
        t = jnp.broadcast_to(rb_ref[hh], shape)
        t = pltpu.roll(t, RB_WIDTH - 15, 1)
        for b in range(6):
            t = jnp.where(((qc >> b) & 1) == 1, pltpu.roll(t, 1 << b, 1), t)
        t_odd = pltpu.roll(t, RB_WIDTH - GRID_W, 1)
        for ro in range(N_PAIRS):
            src = t if ro % 2 == 0 else t_odd
            base = (ro // 2) * 128
            pair_scr[hh, ro] = jnp.where(valid, src[:, base:base + 128], NEG_INF)


NA_GROUP_FWD = 8
NA_GROUP_BWD = 4


def _stack_heads(ref, r, scale=1.0):
    lane = lax.broadcasted_iota(jnp.int32, (GRID_W, 128), 1)
    t = ref[pl.ds(pl.multiple_of(r * GRID_W, GRID_W), GRID_W), :].astype(F32) * scale
    return jnp.concatenate([jnp.where(lane < 64, t, 0.0), jnp.where(lane >= 64, t, 0.0)], axis=0).astype(BF16)


def _unstack_heads(t2):
    lane = lax.broadcasted_iota(jnp.int32, (GRID_W, 128), 1)
    return jnp.where(lane < 64, t2[:GRID_W], t2[GRID_W:])


def _na_window(k_ref, v_ref, r, n_rows):
    rs = jnp.clip(r - NA_WIN_ROWS // 2, 0, n_rows - NA_WIN_ROWS)
    ro0 = (NA_WIN_ROWS - 1) - (r - rs)
    off = pl.multiple_of(rs * GRID_W, GRID_W)
    kw = k_ref[pl.ds(off, NA_WIN_ROWS * GRID_W), :]
    vw = v_ref[pl.ds(off, NA_WIN_ROWS * GRID_W), :]
    return kw, vw, off, ro0


def _na_probs(s_raw, pair_scr, ro0):
    bias = [jnp.concatenate([pair_scr[hh, ro0 + 2 * j] for j in range(NA_WIN_ROWS // 2)], axis=1)
            for hh in range(2)]
    s = s_raw + jnp.concatenate(bias, axis=0)
    m = jnp.max(s, axis=-1, keepdims=True)
    e = jnp.exp(s - m)
    return e * (1.0 / jnp.sum(e, axis=-1, keepdims=True))


def _na_qkv_specs(n):
    pairs = NA_WIDTH // 128
    return [pl.BlockSpec((n, 128), lambda h, first=t * pairs: (0, first + h)) for t in range(3)]


def _na_fwd(qkv, rb, *, name):
    n = qkv.shape[0]
    n_rows = n // GRID_W

    def body(ins, outs, scr):
        q_ref, k_ref, v_ref, rb_ref = ins
        o_ref, = outs
        pair_scr, = scr
        _na_bias(rb_ref, pair_scr)

        def group(g, carry):
            rows = [g * NA_GROUP_FWD + t for t in range(NA_GROUP_FWD)]
            wins = [_na_window(k_ref, v_ref, r, n_rows) for r in rows]
            raw = [lax.dot_general(_stack_heads(q_ref, r, QK_SCALE), w[0], NT_DIMS, preferred_element_type=F32)
                   for r, w in zip(rows, wins)]
            probs = [_na_probs(s, pair_scr, w[3]) for s, w in zip(raw, wins)]
            outs2 = [jnp.dot(p.astype(BF16), w[1], preferred_element_type=F32) for p, w in zip(probs, wins)]
            for r, o2 in zip(rows, outs2):
                o_ref[pl.ds(pl.multiple_of(r * GRID_W, GRID_W), GRID_W), :] = _unstack_heads(o2).astype(BF16)
            return carry

        lax.fori_loop(0, n_rows // NA_GROUP_FWD, group, 0)

    col = pl.BlockSpec((n, 128), lambda h: (0, h))
    return _call(
        body, name=name, grid=(NA_WIDTH // 128,),
        in_specs=_na_qkv_specs(n) + [pl.BlockSpec((2, 1, RB_WIDTH), lambda h: (h, 0, 0))],
        out_specs=[col], out_shape=[_sds((n, NA_WIDTH), BF16)],
        scratch_shapes=[pltpu.VMEM((2, N_PAIRS, GRID_W, 128), F32)],
        args=(qkv, qkv, qkv, rb))[0]


def _na_bwd(qkv, do, rb, *, name):
    n = qkv.shape[0]
    n_rows = n // GRID_W
    win = NA_WIN_ROWS * GRID_W

    def body(ins, outs, scr):
        q_ref, k_ref, v_ref, do_ref, rb_ref = ins
        dq_ref, dk_ref, dv_ref, drb_ref = outs
        pair_scr, acc_scr = scr
        _na_bias(rb_ref, pair_scr)
        acc_scr[...] = jnp.zeros_like(acc_scr)
        dk_ref[...] = jnp.zeros_like(dk_ref)
        dv_ref[...] = jnp.zeros_like(dv_ref)

        def group(g, carry):
            rows = [g * NA_GROUP_BWD + t for t in range(NA_GROUP_BWD)]
            wins = [_na_window(k_ref, v_ref, r, n_rows) for r in rows]
            qss = [_stack_heads(q_ref, r, QK_SCALE) for r in rows]
            doss = [_stack_heads(do_ref, r) for r in rows]
            raw = [lax.dot_general(qs, w[0], NT_DIMS, preferred_element_type=F32) for qs, w in zip(qss, wins)]
            dps = [lax.dot_general(dos, w[1], NT_DIMS, preferred_element_type=F32) for dos, w in zip(doss, wins)]
            probs = [_na_probs(s, pair_scr, w[3]) for s, w in zip(raw, wins)]
            dss = [p * (dp - jnp.sum(p * dp, axis=-1, keepdims=True)) for p, dp in zip(probs, dps)]
            dsbs = [ds.astype(BF16) for ds in dss]
            dq2s = [jnp.dot(dsb, w[0], preferred_element_type=F32) for dsb, w in zip(dsbs, wins)]
            dkws = [lax.dot_general(dsb, qs, TN_DIMS, preferred_element_type=F32) for dsb, qs in zip(dsbs, qss)]
            dvws = [lax.dot_general(p.astype(BF16), dos, TN_DIMS, preferred_element_type=F32)
                    for p, dos in zip(probs, doss)]
            for t, r in enumerate(rows):
                _, _, off, ro0 = wins[t]
                for hh in range(2):
                    for j in range(NA_WIN_ROWS // 2):
                        acc_scr[hh, ro0 + 2 * j] += dss[t][hh * GRID_W:(hh + 1) * GRID_W, j * 128:(j + 1) * 128]
                dq_ref[pl.ds(pl.multiple_of(r * GRID_W, GRID_W), GRID_W), :] = (
                    _unstack_heads(dq2s[t]) * QK_SCALE).astype(BF16)
                dk_ref[pl.ds(off, win), :] += dkws[t]
                dv_ref[pl.ds(off, win), :] += dvws[t]
            return carry

        lax.fori_loop(0, n_rows // NA_GROUP_BWD, group, 0)

        qc = lax.broadcasted_iota(jnp.int32, (N_PAIRS * GRID_W, 128), 0)
        for hh in range(2):
            t = acc_scr[hh].reshape(N_PAIRS * GRID_W, 128)
            for b in range(6):
                t = jnp.where(((qc >> b) & 1) == 1, pltpu.roll(t, 128 - (1 << b), 1), t)
            t = pltpu.roll(t, 15, 1)
            drb_ref[hh] = jnp.sum(t.reshape(N_PAIRS, GRID_W, 128), axis=1)

    col = pl.BlockSpec((n, 128), lambda h: (0, h))
    return _call(
        body, name=name, grid=(NA_WIDTH // 128,),
        in_specs=_na_qkv_specs(n) + [col, pl.BlockSpec((2, 1, RB_WIDTH), lambda h: (h, 0, 0))],
        out_specs=[col, col, col, pl.BlockSpec((2, N_PAIRS, 128), lambda h: (h, 0, 0))],
        out_shape=[_sds((n, NA_WIDTH), BF16), _sds((n, NA_WIDTH), F32), _sds((n, NA_WIDTH), F32),
                   _sds((8, N_PAIRS, 128), F32)],
        scratch_shapes=[pltpu.VMEM((2, N_PAIRS, GRID_W, 128), F32),
                        pltpu.VMEM((2, N_PAIRS, GRID_W, 128), F32)],
        args=(qkv, qkv, qkv, do, rb))


def _rpb_table(rpb2):
    t = jnp.pad(rpb2, ((0, 0), (0, 1), (0, GRID_W - rpb2.shape[-1])))
    return t.reshape(8, 1, RB_WIDTH)


def _rpb_grad(drb, *, name):
    kdim = drb.shape[1]

    def body(x_ref, o_ref):
        kk = lax.broadcasted_iota(jnp.int32, (128, 512), 0)
        jj = lax.broadcasted_iota(jnp.int32, (128, 512), 1)
        half, co = kk >> 6, kk & 63
        acc = jnp.zeros((8, 512), F32)
        for ro in range(N_PAIRS):
            hit = ((ro + half) == (jj >> 5)) & (co == (jj & 31)) & (co < 31)
            onehot = jnp.where(hit, 1.0, 0.0).astype(F32)
            acc = acc + jnp.dot(x_ref[:, ro * 128:(ro + 1) * 128], onehot, preferred_element_type=F32,
                                precision=lax.Precision.HIGHEST)
        o_ref[...] = acc

    return pl.pallas_call(
        body, name=name, grid=(1,),
        in_specs=[_const((8, kdim))], out_specs=_const((8, 512)), out_shape=_sds((8, 512), F32),
        compiler_params=_params("arbitrary"),
    )(drb)


DIL_GROUP = 2


def _dil_blocks(length):
    qb = min(128, length)
    return qb, min(qb + 2 * DIL_RADIUS, length), min(DIL_GROUP, length // qb)


def _stack_lanes(ref, t, qb, scale=1.0):
    lane = lax.broadcasted_iota(jnp.int32, (qb, 256), 1)
    val = ref[0, t * qb:(t + 1) * qb, :].astype(F32) * scale
    return jnp.concatenate([jnp.where((lane >> 6) == h, val, 0.0) for h in range(4)], axis=0).astype(BF16)


def _dil_window(k_ref, v_ref, blk, qb, win, length):
    start = pl.multiple_of(jnp.clip(blk * qb - DIL_RADIUS, 0, length - win), DIL_RADIUS)
    return k_ref[0, pl.ds(start, win), :], v_ref[0, pl.ds(start, win), :], start


def _dil_caps_init(caps_scr, qb, win):
    @pl.when((pl.program_id(0) == 0) & (pl.program_id(1) == 0))
    def _():
        gap = ((lax.broadcasted_iota(jnp.int32, (4 * qb, win), 0) & (qb - 1))
               - lax.broadcasted_iota(jnp.int32, (4 * qb, win), 1))
        for v in range(3):
            caps_scr[v] = jnp.where(jnp.abs(gap + v * DIL_RADIUS) <= DIL_RADIUS, jnp.inf, NEG_INF)


def _dil_mask(s, blk, start, qb, caps_scr):
    return jnp.minimum(s, caps_scr[(blk * qb - start) // DIL_RADIUS])


def _pick_heads(stacked, qb):
    lane = lax.broadcasted_iota(jnp.int32, (qb, 256), 1)
    out = jnp.zeros((qb, 256), stacked.dtype)
    for h in range(4):
        out = jnp.where((lane >> 6) == h, stacked[h * qb:(h + 1) * qb], out)
    return out


def _stack_head_cols(ref, t, qb):
    return jnp.concatenate([ref[0, t * qb:(t + 1) * qb, 64 * h:64 * h + 1] for h in range(4)], axis=0)


def _dil_fwd(q, k, v, *, name, after=None):
    dil, length, _ = q.shape
    qb, win, grp = _dil_blocks(length)
    extra = [] if after is None else [after]

    def body(q_ref, k_ref, v_ref, *rest):
        o_ref, lse_ref, caps_scr = rest[-3:]
        _dil_caps_init(caps_scr, qb, win)
        blks = [pl.program_id(1) * grp + t for t in range(grp)]
        wins = [_dil_window(k_ref, v_ref, b, qb, win, length) for b in blks]
        raw = [lax.dot_general(_stack_lanes(q_ref, t, qb, QK_SCALE), w[0], NT_DIMS, preferred_element_type=F32)
               for t, w in enumerate(wins)]
        lses, outs = [], []
        for t, (s, w) in enumerate(zip(raw, wins)):
            s = _dil_mask(s, blks[t], w[2], qb, caps_scr)
            m = jnp.max(s, axis=-1, keepdims=True)
            e = jnp.exp(s - m)
            norm = jnp.sum(e, axis=-1, keepdims=True)
            lses.append(m + jnp.log(norm))
            outs.append(jnp.dot((e * (1.0 / norm)).astype(BF16), w[1], preferred_element_type=F32))
        for t in range(grp):
            o_ref[0, t * qb:(t + 1) * qb, :] = _pick_heads(outs[t], qb)
            lse_ref[0, t * qb:(t + 1) * qb, :] = _pick_heads(jnp.broadcast_to(lses[t], (4 * qb, 256)), qb)

    seq = pl.BlockSpec((1, length, 256), lambda j, i: (j, 0, 0))
    blk = pl.BlockSpec((1, grp * qb, 256), lambda j, i: (j, i, 0))
    return pl.pallas_call(
        body, name=name, grid=(dil, length // (grp * qb)),
        in_specs=[blk, seq, seq] + [pl.BlockSpec(memory_space=pl.ANY)] * len(extra), out_specs=[blk, blk],
        out_shape=[_sds((dil, length, 256), F32)] * 2,
        scratch_shapes=[pltpu.VMEM((3, 4 * qb, win), F32)],
        compiler_params=_params("arbitrary", "arbitrary"),
    )(q, k, v, *extra)


def _dil_bwd(q, k, v, do, lse, cc, *, name):
    dil, length, _ = q.shape
    qb, win, grp = _dil_blocks(length)

    def body(q_ref, k_ref, v_ref, do_ref, lse_ref, cc_ref, dq_ref, dk_ref, dv_ref, caps_scr):
        _dil_caps_init(caps_scr, qb, win)

        @pl.when(pl.program_id(1) == 0)
        def _():
            dk_ref[...] = jnp.zeros_like(dk_ref)
            dv_ref[...] = jnp.zeros_like(dv_ref)

        blks = [pl.program_id(1) * grp + t for t in range(grp)]
        wins = [_dil_window(k_ref, v_ref, b, qb, win, length) for b in blks]
        qss = [_stack_lanes(q_ref, t, qb, QK_SCALE) for t in range(grp)]
        doss = [_stack_lanes(do_ref, t, qb) for t in range(grp)]
        raw = [lax.dot_general(qs, w[0], NT_DIMS, preferred_element_type=F32) for qs, w in zip(qss, wins)]
        dps = [lax.dot_general(dos, w[1], NT_DIMS, preferred_element_type=F32) for dos, w in zip(doss, wins)]
        probs = [jnp.exp(_dil_mask(s, blks[t], wins[t][2], qb, caps_scr) - _stack_head_cols(lse_ref, t, qb))
                 for t, s in enumerate(raw)]
        dsbs = [(p * (dp + _stack_head_cols(cc_ref, t, qb))).astype(BF16)
                for t, (p, dp) in enumerate(zip(probs, dps))]
        dq4s = [jnp.dot(dsb, w[0], preferred_element_type=F32) for dsb, w in zip(dsbs, wins)]
        dkws = [lax.dot_general(dsb, qs, TN_DIMS, preferred_element_type=F32) for dsb, qs in zip(dsbs, qss)]
        dvws = [lax.dot_general(p.astype(BF16), dos, TN_DIMS, preferred_element_type=F32)
                for p, dos in zip(probs, doss)]
        for t in range(grp):
            dq_ref[0, t * qb:(t + 1) * qb, :] = _pick_heads(dq4s[t], qb) * QK_SCALE
            dk_ref[0, pl.ds(wins[t][2], win), :] += dkws[t]
            dv_ref[0, pl.ds(wins[t][2], win), :] += dvws[t]

    seq = pl.BlockSpec((1, length, 256), lambda j, i: (j, 0, 0))
    blk = pl.BlockSpec((1, grp * qb, 256), lambda j, i: (j, i, 0))
    return pl.pallas_call(
        body, name=name, grid=(dil, length // (grp * qb)),
        in_specs=[blk, seq, seq, blk, blk, blk], out_specs=[blk, seq, seq],
        out_shape=[_sds((dil, length, 256), F32)] * 3,
        scratch_shapes=[pltpu.VMEM((3, 4 * qb, win), F32)],
        compiler_params=_params("arbitrary", "arbitrary"),
    )(q, k, v, do, lse, cc)


def _merge_weights(lses):
    m = jnp.maximum(jnp.maximum(lses[0], lses[1]), lses[2])
    es = [jnp.exp(t - m) for t in lses]
    inv = 1.0 / (es[0] + es[1] + es[2])
    return [e * inv for e in es]


def _dil_merge(outs, lses, *, tm, name):
    n = outs[0].shape[1]

    def body(*refs):
        o_in, l_in = refs[0:3], refs[3:6]
        y_ref, yb_ref, scr = refs[6:9]
        lv = [_load_token_order(l_in[g], scr, d, tm) for g, d in enumerate(DIL_DILATIONS)]
        ws = _merge_weights(lv)
        y = jnp.zeros((tm, 256), F32)
        for g, d in enumerate(DIL_DILATIONS):
            y = y + ws[g] * _load_token_order(o_in[g], scr, d, tm)
        y_ref[...] = y
        yb_ref[...] = y.astype(BF16)

    specs = [_dil_spec(d, tm) for d in DIL_DILATIONS]
    return pl.pallas_call(
        body, name=name, grid=(n // tm,), in_specs=specs + specs,
        out_specs=[_rows(tm, 256)] * 2, out_shape=[_sds((n, 256), F32), _sds((n, 256), BF16)],
        scratch_shapes=[_dil_scratch(tm)],
        compiler_params=_params("parallel"),
    )(*outs, *lses)


def _dil_merge_bwd(dy, y, lses, *, tm, name):
    n = dy.shape[0]

    def body(*refs):
        dy_ref, y_ref = refs[0:2]
        l_in = refs[2:5]
        do_out, cc_out = refs[5:8], refs[8:11]
        scr = refs[11]
        lv = [_load_token_order(l_in[g], scr, d, tm) for g, d in enumerate(DIL_DILATIONS)]
        ws = _merge_weights(lv)
        dyv = dy_ref[...]
        rr = lax.broadcasted_iota(jnp.int32, (256, 256), 0) >> 6
        cc = lax.broadcasted_iota(jnp.int32, (256, 256), 1) >> 6
        ones = jnp.where(rr == cc, 1.0, 0.0).astype(F32)
        tsum = jnp.dot(dyv * y_ref[...], ones, preferred_element_type=F32,
                       precision=lax.Precision.HIGHEST)
        for g, d in enumerate(DIL_DILATIONS):
            _store_dil_order(ws[g] * dyv, do_out[g], scr, d)
            _store_dil_order(-ws[g] * tsum, cc_out[g], scr, d)

    specs = [_dil_spec(d, tm) for d in DIL_DILATIONS]
    res = pl.pallas_call(
        body, name=name, grid=(n // tm,),
        in_specs=[_rows(tm, 256)] * 2 + specs,
        out_specs=specs + specs,
        out_shape=[_sds((d, n // d, 256), BF16) for d in DIL_DILATIONS]
                  + [_sds((d, n // d, 256), F32) for d in DIL_DILATIONS],
        scratch_shapes=[_dil_scratch(tm)],
        compiler_params=_params("parallel"),
    )(dy, y, *lses)
    return res[0:3], res[3:6]


_WEIGHTS = (("w_in", 1, 736), ("w_branch_na", 1, 128), ("w_branch_dil", 1, 128), ("w_out", 0, 128),
            ("w_up", 1, 512), ("w_down", 0, 512), ("w_ple_gate", 0, 128), ("w_ple_proj", 1, 128))
_W_IN, _W_BNA, _W_BD, _W_OUT, _W_UP, _W_DOWN, _W_PG, _W_PP = range(8)


def _to_full(gathered):
    return gathered.reshape(-1, gathered.shape[2])


def _to_chunks(widx, mat):
    return mat.reshape(N_DEV, _WEIGHTS[widx][2], mat.shape[1])


def _local_step(x, p_bf16, positions, target, g_mix, g_mlp, g_ple, g_final, rpb2,
                get_w_in, relay_rest, get_rest, send_grads):
    tm = 512
    half = HEAD_DIM // 2
    inv_freq = 10000.0 ** (-jnp.arange(half, dtype=F32) / half)
    ang = positions.astype(F32)[:, None] * inv_freq
    cos, sin = jnp.cos(ang), jnp.sin(ang)
    cos_t = jnp.tile(jnp.concatenate([cos, cos], axis=-1), (1, 4))
    sin_t = jnp.tile(jnp.concatenate([-sin, sin], axis=-1), (1, 4))
    rb = _rpb_table(rpb2)

    a = _rms_fwd(x, g_mix, tm=tm, name="rms_mix")
    w_in, token = get_w_in(a)
    na_width, qkv_width = 3 * NA_WIDTH, 3 * NA_WIDTH + 3 * DIL_WIDTH
    na_qkv = _matmul(a, w_in, tb=True, n_limit=na_width, out_dtype=BF16, tm=512, tn=na_width, tk=1024,
                     name="mm_in_na", after=token)
    dq_g, dk_g, dv_g = _project_dil(a, w_in[na_width:qkv_width], cos_t, sin_t, tm=512, name="mm_in_dil")
    gates = _matmul(a, w_in[qkv_width:], tb=True, out_dtype=BF16, tm=512, tn=2 * D_MODEL, tk=1024,
                    name="mm_in_gates", epilogue=lambda acc: (_sigmoid(acc),))
    sn, sd = (gates, 0), (gates, 1)
    y_na = _na_fwd(na_qkv, rb, name="na_fwd")
    token = relay_rest(y_na)
    d_out, d_lse = [], []
    for g in range(3):
        o, lse = _dil_fwd(dq_g[g], dk_g[g], dv_g[g], name=f"dil_fwd{g}", after=token if g == 0 else None)
        d_out.append(o)
        d_lse.append(lse)
    y_dil, y_dil_b = _dil_merge(d_out, d_lse, tm=tm, name="dil_merge")
    w_bna, w_bd, w_out, w_up, w_down, w_pg, w_pp = get_rest(y_dil_b)
    bn = _matmul(y_na, w_bna, tb=True, out_dtype=BF16, tm=512, tn=1024, tk=512, name="mm_bna")
    bd, mixed = _matmul(y_dil_b, w_bd, tb=True, out_dtype=(BF16, BF16), tm=512, tn=1024, tk=256, name="mm_bd",
                        extra=(sn, bn, sd), epilogue=_gate_mix_tile)
    h1, c = _matmul(mixed, w_out, out_dtype=(F32, BF16), tm=512, tn=1024, tk=1024, name="mm_out",
                    extra=(x, g_mlp), epilogue=_residual_rms_tile)
    u, f = _matmul(c, w_up, tb=True, out_dtype=(BF16, BF16), tm=512, tn=2048, tk=1024, name="mm_up",
                   epilogue=lambda acc: (acc, jnp.square(jnp.maximum(acc, 0.0))))
    h2, e = _matmul(f, w_down, out_dtype=(F32, BF16), tm=512, tn=1024, tk=4096, name="mm_down",
                    extra=(h1, g_ple), epilogue=_residual_rms_tile)
    pp = _matmul(p_bf16, w_pp, tb=True, out_dtype=F32, tm=512, tn=1024, tk=256, name="mm_pp")

    dh3, dpp, dgt, dg_final, loss = _matmul(
        e, w_pg, out_dtype=(F32, BF16, BF16), tm=512, tn=1024, tk=1024, name="mm_pg_tail",
        extra=(pp, h2, target, g_final), epilogue=_tail_tile, n_colsum=2)
    loss = loss[:, :128]
    gw_pp = _matmul(p_bf16, dpp, ta=True, transpose_out=True, out_dtype=BF16, tm=256, tn=1024, tk=2048,
                    name="mm_gw_pp")
    gw_pg = _matmul(e, dgt, ta=True, out_dtype=BF16, tm=512, tn=1024, tk=2048, name="mm_gw_pg")
    dh2, dh2_b, dg_ple = _matmul(
        dgt, w_pg, tb=True, out_dtype=(F32, BF16), tm=512, tn=1024, tk=1024, name="mm_de",
        extra=(h2, g_ple, dh3), epilogue=_rms_bwd_twice, n_colsum=1)
    du = _matmul(dh2_b, w_down, tb=True, out_dtype=BF16, tm=512, tn=2048, tk=1024, name="mm_du",
                 extra=(u,), epilogue=lambda acc, uv: (acc * (2.0 * jnp.maximum(uv.astype(F32), 0.0)),))
    gw_down = _matmul(f, dh2_b, ta=True, out_dtype=BF16, tm=1024, tn=1024, tk=2048, name="mm_gw_down")
    gw_up = _matmul(c, du, ta=True, transpose_out=True, out_dtype=BF16, tm=512, tn=2048, tk=2048, name="mm_gw_up")
    dh1, dh1_b, dg_mlp = _matmul(
        du, w_up, out_dtype=(F32, BF16), tm=512, tn=1024, tk=4096, name="mm_dc",
        extra=(h1, g_mlp, dh2), epilogue=_rms_bwd_twice, n_colsum=1)
    dbn, dbd, dgn, dgd = _matmul(dh1_b, w_out, tb=True, out_dtype=(BF16,) * 4, tm=512, tn=1024, tk=1024,
                                 name="mm_dmixed", extra=(sn, bn, sd, bd), epilogue=_gate_bwd_tile)
    gw_out = _matmul(mixed, dh1_b, ta=True, out_dtype=BF16, tm=512, tn=1024, tk=2048, name="mm_gw_out")
    gw_bna = _matmul(y_na, dbn, ta=True, transpose_out=True, out_dtype=BF16, tm=512, tn=1024, tk=2048,
                     name="mm_gw_bna")
    dy_na = _matmul(dbn, w_bna, out_dtype=BF16, tm=512, tn=512, tk=1024, name="mm_dy_na")
    gw_bd = _matmul(y_dil_b, dbd, ta=True, transpose_out=True, out_dtype=BF16, tm=256, tn=1024, tk=2048,
                    name="mm_gw_bd")
    token = send_grads((_W_PP, _W_PG, _W_DOWN, _W_UP, _W_OUT, _W_BNA, _W_BD),
                       (gw_pp, gw_pg, gw_down, gw_up, gw_out, gw_bna, gw_bd))
    dy_dil = _matmul(dbd, w_bd, out_dtype=F32, tm=512, tn=256, tk=1024, name="mm_dy_dil", after=token)
    dna = _na_bwd(na_qkv, dy_na, rb, name="na_bwd")
    drpb = _rpb_grad(dna[3].reshape(8, -1), name="rpb_grad")
    do_g, cc_g = _dil_merge_bwd(dy_dil, y_dil, d_lse, tm=tm, name="dil_merge_bwd")
    ddq, ddk, ddv = [], [], []
    for g in range(3):
        r = _dil_bwd(dq_g[g], dk_g[g], dv_g[g], do_g[g], d_lse[g], cc_g[g], name=f"dil_bwd{g}")
        ddq.append(r[0])
        ddk.append(r[1])
        ddv.append(r[2])
    dproj = _assemble_dproj(dna[0:3], ddq, ddk, ddv, dgn, dgd, cos_t, sin_t, tm=tm, name="assemble_dproj")
    gw_in = _matmul(a, dproj, ta=True, transpose_out=True, out_dtype=BF16, tm=512, tn=2944, tk=2048, name="mm_gw_in")
    token = send_grads((_W_IN,), (gw_in,))
    dx, dg_mix = _matmul(
        dproj, w_in, out_dtype=(F32,), tm=512, tn=1024, tk=5888, name="mm_da", after=token,
        extra=(x, g_mix, dh1), epilogue=_rms_bwd_tile, n_colsum=1)
    return loss, dx, (dg_mix, dg_mlp, dg_ple, dg_final), drpb


def _cast_bf16(t, *, name):
    def body(t_ref, o_ref):
        o_ref[...] = t_ref[...].astype(BF16)

    rows, cols = t.shape
    tr = 256 if rows % 256 == 0 else rows
    blk = pl.BlockSpec((tr, cols), lambda i: (i, 0))
    return pl.pallas_call(body, name=name, grid=(rows // tr,), in_specs=[blk], out_specs=blk,
                          out_shape=_sds(t.shape, BF16), compiler_params=_params("parallel"))(t)


def _adamw(w, g, m, v):
    m = ADAM_B1 * m + (1.0 - ADAM_B1) * g
    v = ADAM_B2 * v + (1.0 - ADAM_B2) * (g * g)
    m_hat = m / (1.0 - ADAM_B1 ** ADAM_STEP)
    v_hat = v / (1.0 - ADAM_B2 ** ADAM_STEP)
    delta = -ADAM_LR * (m_hat / (jnp.sqrt(v_hat) + ADAM_EPS) + ADAM_WD * w)
    return delta, m, v


def _sum_adamw(parts, w, m, v, *, tr, name, own=None, transposed=False):
    rows, cols = w.shape
    n_pre = 0 if own is None else 1

    def body(*refs):
        p_ref, w_ref, m_ref, v_ref = refs[n_pre:n_pre + 4]
        g_ref, d_ref, nm_ref, nv_ref = refs[-4:]
        g = (p_ref[0] if own is None else refs[n_pre + 4][...]).astype(F32)
        for s in range(1, N_DEV):
            g = g + p_ref[s].astype(F32)
        if transposed:
            g = g.T
        g_ref[...] = g
        d_ref[...], nm_ref[...], nv_ref[...] = _adamw(w_ref[...], g, m_ref[...], v_ref[...])

    if transposed:
        blk = pl.BlockSpec((rows, tr), lambda i, *_: (0, i))
        g_rows, steps = rows, cols // tr
    else:
        blk = pl.BlockSpec((tr, cols), lambda i, *_: (i, 0))
        g_rows, steps = cols, rows // tr
    in_specs = [pl.BlockSpec((N_DEV, tr, g_rows), lambda i, *_: (0, i, 0)), blk, blk, blk]
    args = [parts, w, m, v]
    if own is not None:
        in_specs.append(pl.BlockSpec((None, tr, g_rows), lambda i, idx: (idx[0], i, 0)))
        args = [own[1]] + args + [own[0]]
    return pl.pallas_call(
        body, name=name,
        grid_spec=pltpu.PrefetchScalarGridSpec(num_scalar_prefetch=n_pre, grid=(steps,), in_specs=in_specs,
                                               out_specs=[blk] * 4),
        out_shape=[_sds((rows, cols), F32)] * 4,
        compiler_params=_params("parallel"),
    )(*args)


_RPB_SIZE = 8 * 15 * 31


def _pack_small(g_mix, g_mlp, g_ple, g_final, rpb, loss_row):
    flat = jnp.concatenate([g_mix.reshape(-1), g_mlp.reshape(-1), g_ple.reshape(-1), g_final.reshape(-1),
                            rpb.reshape(-1), jnp.zeros((3840 - _RPB_SIZE,), F32), loss_row.reshape(-1),
                            jnp.zeros((128,), F32)])
    return flat.reshape(64, 128)


def _unpack_small(t):
    flat = t.reshape(-1)
    return (flat[0:1024].reshape(1, 1024), flat[4096:4096 + _RPB_SIZE].reshape(1, 8, 15, 31),
            flat[1024:2048].reshape(1, 1024), flat[2048:3072].reshape(1, 1024), flat[3072:4096])


def kernel(x, p, positions, g_mix, w_in, rpb, w_branch_na, w_branch_dil, w_out, g_mlp, w_up, w_down, g_ple, w_ple_gate, w_ple_proj, g_final, loss_target, m_g_mix, m_w_in, m_rpb, m_w_branch_na, m_w_branch_dil, m_w_out, m_g_mlp, m_w_up, m_w_down, m_g_ple, m_w_ple_gate, m_w_ple_proj, m_g_final, v_g_mix, v_w_in, v_rpb, v_w_branch_na, v_w_branch_dil, v_w_out, v_g_mlp, v_w_up, v_w_down, v_g_ple, v_w_ple_gate, v_w_ple_proj, v_g_final):
    sharded = dict(w_in=(w_in, m_w_in, v_w_in), w_branch_na=(w_branch_na, m_w_branch_na, v_w_branch_na),
                   w_branch_dil=(w_branch_dil, m_w_branch_dil, v_w_branch_dil), w_out=(w_out, m_w_out, v_w_out),
                   w_up=(w_up, m_w_up, v_w_up), w_down=(w_down, m_w_down, v_w_down),
                   w_ple_gate=(w_ple_gate, m_w_ple_gate, v_w_ple_gate),
                   w_ple_proj=(w_ple_proj, m_w_ple_proj, v_w_ple_proj))
    shards = {k: tuple(t[0] for t in val) for k, val in sharded.items()}

    me = _my_index()

    shards["w_in"] = tuple(t.T for t in shards["w_in"])

    w_in_b = _cast_bf16(shards["w_in"][0], name="cast_w_in")
    rest_b = [shards[name][0].astype(BF16).T if axis == 1 else shards[name][0].astype(BF16)
              for name, axis, _ in _WEIGHTS[1:]]
    first_in, token_in = _start_copies(_first_leg_copies, [w_in_b], [_sds((N_DEV,) + w_in_b.shape, BF16)], 4,
                                       name="start_gather_w_in")

    def whole(landed, mine):
        return _to_full(lax.dynamic_update_index_in_dim(landed, mine, me, 0))

    rest = {}

    def get_w_in(after):
        (mine,), landed = _wait_copies(_first_leg_copies, first_in, after, name="wait_gather_w_in")
        second, token = _start_copies(_second_leg_copies, [], landed, 3, name="start_forward_w_in")
        _, (landed,) = _wait_copies(_second_leg_copies, second, token, name="wait_forward_w_in")
        rest["first"], token = _start_copies(_first_leg_copies, rest_b,
                                             [_sds((N_DEV,) + t.shape, BF16) for t in rest_b], 4 * len(rest_b),
                                             name="start_gather_rest", after=landed)
        return whole(landed, mine), token

    def relay_rest(after):
        rest["mine"], landed = _wait_copies(_first_leg_copies, rest["first"], after, name="wait_gather_rest")
        rest["second"], token = _start_copies(_second_leg_copies, [], landed, 3 * len(rest_b),
                                              name="start_forward_rest")
        return token

    def get_rest(after):
        _, landed = _wait_copies(_second_leg_copies, rest["second"], after, name="wait_forward_rest")
        return [whole(t, own) for t, own in zip(landed, rest["mine"])]

    sent = []

    def send_grads(indices, grads):
        chunked = [_to_chunks(i, g) for i, g in zip(indices, grads)]
        handle, token = _start_copies(_exchange_copies, chunked, [_sds(t.shape, BF16) for t in chunked],
                                      7 * len(chunked),
                                      name="start_exchange_" + ("w_in" if indices == (_W_IN,) else "rest"))
        sent.append((indices, handle))
        return token

    g_mix_0 = g_mix + token_in[0:1, 0:1]
    loss, dx, dgs, drpb = _local_step(
        x[0], p[0, 0].astype(BF16), positions[0], loss_target[0],
        g_mix_0, g_mlp, g_ple, g_final.reshape(1, -1), rpb[0], get_w_in, relay_rest, get_rest, send_grads)

    drpb3 = drpb.reshape(8, 16, 32)[:, :15, :31]
    small = _pack_small(dgs[0], dgs[1], dgs[2], dgs[3], drpb3, loss)
    share, done = _start_copies(_gather_copies, [small], [_sds((N_DEV,) + small.shape, F32)], 7,
                                name="start_share_small")

    out = {}
    for indices, handle in sent:
        chunked, landed = _wait_copies(_exchange_copies, handle, done,
                                       name="wait_exchange_" + ("w_in" if indices == (_W_IN,) else "rest"))
        for i, part, mine in zip(indices, landed, chunked):
            name = _WEIGHTS[i][0]
            w, m, v = shards[name]
            turned = _WEIGHTS[i][1] == 1 and i != _W_IN
            res = _sum_adamw(part, w, m, v, tr=368 if i == _W_IN else 128, name="adamw_" + name,
                             own=(mine, me.reshape(1).astype(jnp.int32)), transposed=turned)
            out[name] = [(t.T if i == _W_IN else t)[None] for t in res]
            done = res[0]
    (small,), (small_landed,) = _wait_copies(_gather_copies, share, done, name="wait_share_small")
    small_all = lax.dynamic_update_index_in_dim(small_landed, small, me, 0)
    small_w = _pack_small(g_mix, g_mlp, g_ple, g_final, rpb, jnp.zeros((128,), F32))
    small_m = _pack_small(m_g_mix, m_g_mlp, m_g_ple, m_g_final, m_rpb, jnp.zeros((128,), F32))
    small_v = _pack_small(v_g_mix, v_g_mlp, v_g_ple, v_g_final, v_rpb, jnp.zeros((128,), F32))
    res = _sum_adamw(small_all, small_w, small_m, small_v, tr=64, name="adamw_small")
    unpacked = [_unpack_small(t) for t in res]
    for i, name in enumerate(("g_mix", "rpb", "g_mlp", "g_ple", "g_final")):
        out[name] = [u[i] for u in unpacked]
    loss_total = res[0][62, 0]

    order = ("g_mix", "w_in", "rpb", "w_branch_na", "w_branch_dil", "w_out", "g_mlp", "w_up", "w_down",
             "g_ple", "w_ple_gate", "w_ple_proj", "g_final")
    grads = [out[k][0] for k in order]
    deltas = [out[k][1] for k in order]
    new_m = [out[k][2] for k in order]
    new_v = [out[k][3] for k in order]
    return (loss_total, dx[None], *grads, *deltas, *new_m, *new_v)
```

```python
import jax
import jax.numpy as jnp
from jax import lax
from jax.experimental import pallas as pl
from jax.experimental.pallas import tpu as pltpu

F32 = jnp.float32
BF16 = jnp.bfloat16

D_MODEL = 1024
HEAD_DIM = 64
GRID_W = 64
NA_WIDTH = 512
DIL_WIDTH = 768
IN_WIDTH = 5888
DIL_DILATIONS = (1, 4, 16)
DIL_RADIUS = 64
NA_WIN_ROWS = 8
RMS_EPS = 1e-6
NEG_INF = -1e30
QK_SCALE = HEAD_DIM ** -0.5

ADAM_LR = 0.001
ADAM_B1 = 0.9
ADAM_B2 = 0.999
ADAM_EPS = 1e-08
ADAM_WD = 0.01
ADAM_STEP = 10

N_DEV = 8
VMEM_LIMIT = 56 * 1024 * 1024
EPILOGUE_ROWS = 256
MESH = pl.DeviceIdType.MESH

NT_DIMS = (((1,), (1,)), ((), ()))
TN_DIMS = (((0,), (0,)), ((), ()))


def _sds(shape, dtype):
    return jax.ShapeDtypeStruct(shape, dtype)


def _params(*sem):
    return pltpu.CompilerParams(dimension_semantics=sem, vmem_limit_bytes=VMEM_LIMIT)


def _rows(tm, width, col=0):
    return pl.BlockSpec((tm, width), lambda i, c=col: (i, c))


def _const(shape):
    zeros = (0,) * len(shape)
    return pl.BlockSpec(shape, lambda i: zeros)


def _my_index():
    return 4 * lax.axis_index("x") + 2 * lax.axis_index("y") + lax.axis_index("c")


def _peer(k):
    x, y, c = lax.axis_index("x"), lax.axis_index("y"), lax.axis_index("c")
    px = 1 - x if k & 4 else x
    py = 1 - y if k & 2 else y
    pc = 1 - c if k & 1 else c
    return (px, py, pc), 4 * px + 2 * py + pc


def _call(body, *, name, grid, in_specs, out_specs, out_shape, scratch_shapes, args, after=None):
    n_in, n_out = len(in_specs), len(out_specs)
    extra = [] if after is None else [after]
    n_x = n_in + len(extra)

    def plain(*refs):
        body(refs[:n_in], refs[n_x:n_x + n_out], refs[n_x + n_out:])

    res = pl.pallas_call(plain, name=name, grid=grid,
                         in_specs=list(in_specs) + [pl.BlockSpec(memory_space=pl.ANY)] * len(extra),
                         out_specs=out_specs, out_shape=out_shape, scratch_shapes=scratch_shapes,
                         compiler_params=_params(*(("arbitrary",) * len(grid))))(*args, *extra)
    return list(res)


_HBM_SPEC = pl.BlockSpec(memory_space=pltpu.HBM)
_SEM_SPEC = pl.BlockSpec(memory_space=pltpu.SEMAPHORE)
_SIDE_EFFECT = pltpu.SideEffectType.DATAFLOW_SIDE_EFFECTING


_FIRST_LEG = (1, 2, 4, 6)
_SECOND_LEG = (2, 4, 6)


def _gather_copies(srcs, lands, send, recv, sending):
    me = _my_index()
    out = []
    for w in range(len(srcs)):
        for k in range(1, N_DEV):
            dev, idx = _peer(k)
            out.append(pltpu.make_async_remote_copy(
                src_ref=srcs[w], dst_ref=lands[w].at[me if sending else idx],
                send_sem=send.at[w * 7 + k - 1], recv_sem=recv.at[w * 7 + k - 1],
                device_id=dev, device_id_type=MESH))
    return out


def _first_leg_copies(srcs, lands, send, recv, sending):
    me = _my_index()
    out = []
    for w in range(len(srcs)):
        for j, k in enumerate(_FIRST_LEG):
            dev, idx = _peer(k)
            out.append(pltpu.make_async_remote_copy(
                src_ref=srcs[w], dst_ref=lands[w].at[me if sending else idx],
                send_sem=send.at[w * 4 + j], recv_sem=recv.at[w * 4 + j],
                device_id=dev, device_id_type=MESH))
    return out


def _second_leg_copies(srcs, lands, send, recv, sending):
    sibling, _ = _peer(1)
    out = []
    for w in range(len(lands)):
        for j, k in enumerate(_SECOND_LEG):
            slot = _peer(k if sending else k ^ 1)[1]
            out.append(pltpu.make_async_remote_copy(
                src_ref=lands[w].at[slot], dst_ref=lands[w].at[slot],
                send_sem=send.at[w * 3 + j], recv_sem=recv.at[w * 3 + j],
                device_id=sibling, device_id_type=MESH))
    return out


def _exchange_copies(srcs, lands, send, recv, sending):
    out = []
    for w in range(len(srcs)):
        for k in range(1, N_DEV):
            dev, idx = _peer(k)
            out.append(pltpu.make_async_remote_copy(
                src_ref=srcs[w].at[idx], dst_ref=lands[w].at[k],
                send_sem=send.at[w * 7 + k - 1], recv_sem=recv.at[w * 7 + k - 1],
                device_id=dev, device_id_type=MESH))
    return out


def _start_copies(make, srcs, lands, n_copies, *, name, after=None):
    n_src, n_buf = len(srcs), len(srcs) + len(lands)
    extra = [] if after is None else [after]

    def body(*refs):
        send, recv = refs[n_buf + len(extra)], refs[n_buf + len(extra) + 1]
        for cp in make(refs[:n_src], refs[n_src:n_buf], send, recv, True):
            cp.start()
        refs[-1][...] = jnp.zeros_like(refs[-1])

    bufs = list(srcs) + [lax.empty(t.shape, t.dtype) if isinstance(t, jax.ShapeDtypeStruct) else t for t in lands]
    res = pl.pallas_call(
        body, name=name,
        out_shape=(pltpu.SemaphoreType.DMA((n_copies,)), pltpu.SemaphoreType.DMA((n_copies,)),
                   *[pltpu.HBM(t.shape, t.dtype) for t in bufs], _sds((8, 128), F32)),
        in_specs=[_HBM_SPEC] * n_buf + [pl.BlockSpec(memory_space=pl.ANY)] * len(extra),
        out_specs=(_SEM_SPEC, _SEM_SPEC, *([_HBM_SPEC] * n_buf), pl.BlockSpec(memory_space=pltpu.VMEM)),
        input_output_aliases={i: 2 + i for i in range(n_buf)},
        compiler_params=pltpu.CompilerParams(has_side_effects=_SIDE_EFFECT),
    )(*[pltpu.with_memory_space_constraint(t, pltpu.HBM) for t in bufs], *extra)
    return (n_src, res[0], res[1], res[2:2 + n_buf]), res[-1]


def _wait_copies(make, handle, after, *, name):
    n_src, send_sems, recv_sems, bufs = handle
    n_buf = len(bufs)

    def body(*refs):
        for cp in make(refs[:n_src], refs[n_src:n_buf], refs[n_buf], refs[n_buf + 1], False):
            cp.wait_send()
            cp.wait_recv()

    res = pl.pallas_call(
        body, name=name,
        out_shape=tuple(pltpu.HBM(t.shape, t.dtype) for t in bufs),
        in_specs=[_HBM_SPEC] * n_buf + [_SEM_SPEC, _SEM_SPEC, pl.BlockSpec(memory_space=pl.ANY)],
        out_specs=tuple([_HBM_SPEC] * n_buf),
        input_output_aliases={i: i for i in range(n_buf)},
        compiler_params=pltpu.CompilerParams(has_side_effects=_SIDE_EFFECT),
    )(*bufs, send_sems, recv_sems, after)
    return list(res[:n_src]), list(res[n_src:])


def _matmul(a, b, *, ta=False, tb=False, out_dtype, tm, tn, tk, name, after=None, extra=(), epilogue=None,
            n_colsum=0, transpose_out=False):
    m, k = (a.shape[1], a.shape[0]) if ta else a.shape
    n = b.shape[0] if tb else b.shape[1]
    tm, tn, tk = min(tm, m), min(tn, n), min(tk, k)
    nk = k // tk
    dims = (((0 if ta else 1,), (1 if tb else 0,)), ((), ()))
    out_dtypes = out_dtype if isinstance(out_dtype, tuple) else (out_dtype,)
    n_tiles = len(out_dtypes)

    def add_colsums(o_refs, sums):
        i = pl.program_id(1)
        for s_ref, val in zip(o_refs[n_tiles:], sums):
            @pl.when(i == 0)
            def _(s_ref=s_ref, val=val):
                s_ref[...] = val

            @pl.when(i > 0)
            def _(s_ref=s_ref, val=val):
                s_ref[...] += val

    def finish(acc, x_refs, o_refs):
        vals = (acc,) if epilogue is None else epilogue(acc, *[r[...] for r in x_refs])
        for o_ref, val in zip(o_refs[:n_tiles], vals[:n_tiles]):
            o_ref[...] = (val.T if transpose_out else val).astype(o_ref.dtype)
        add_colsums(o_refs, vals[n_tiles:])

    chunk = EPILOGUE_ROWS if (nk == 1 and epilogue is not None and not ta and tm % EPILOGUE_ROWS == 0) else None

    def body(ins, outs, acc):
        a_ref, b_ref = ins[:2]
        if chunk is not None:
            sums = None
            for r0 in range(0, tm, chunk):
                part = lax.dot_general(a_ref[r0:r0 + chunk, :], b_ref[...], dims, preferred_element_type=F32)
                vals = epilogue(part, *[r[...] if r.shape[0] == 1 else r[r0:r0 + chunk, :] for r in ins[2:]])
                for o_ref, val in zip(outs[:n_tiles], vals[:n_tiles]):
                    o_ref[r0:r0 + chunk, :] = val.astype(o_ref.dtype)
                sums = vals[n_tiles:] if sums is None else [s + v for s, v in zip(sums, vals[n_tiles:])]
            add_colsums(outs, sums)
            return
        part = lax.dot_general(a_ref[...], b_ref[...], dims, preferred_element_type=F32)
        if nk == 1:
            finish(part, ins[2:], outs)
            return
        acc_ref, = acc
        kk = pl.program_id(2)

        @pl.when(kk == 0)
        def _():
            acc_ref[...] = part

        @pl.when(kk > 0)
        def _():
            acc_ref[...] += part

        @pl.when(kk == nk - 1)
        def _():
            finish(acc_ref[...], ins[2:], outs)

    a_spec = (pl.BlockSpec((tk, tm), lambda j, i, kk: (kk, i)) if ta
              else pl.BlockSpec((tm, tk), lambda j, i, kk: (i, kk)))
    b_spec = (pl.BlockSpec((tn, tk), lambda j, i, kk: (j, kk)) if tb
              else pl.BlockSpec((tk, tn), lambda j, i, kk: (kk, j)))
    tile = pl.BlockSpec((tm, tn), lambda j, i, kk: (i, j))
    row = pl.BlockSpec((1, tn), lambda j, i, kk: (0, j))

    def x_spec(t):
        if isinstance(t, tuple):
            return pl.BlockSpec((tm, tn), lambda j, i, kk, first=t[1] * (n // tn): (i, first + j))
        return row if t.shape[0] == 1 else tile

    out_tile, out_dims = (pl.BlockSpec((tn, tm), lambda j, i, kk: (j, i)), (n, m)) if transpose_out else (tile, (m, n))
    res = _call(
        body, name=name, grid=(n // tn, m // tm, nk),
        in_specs=[a_spec, b_spec] + [x_spec(t) for t in extra],
        out_specs=[out_tile] * n_tiles + [row] * n_colsum,
        out_shape=[_sds(out_dims, dt) for dt in out_dtypes] + [_sds((1, n), F32)] * n_colsum,
        scratch_shapes=[] if nk == 1 else [pltpu.VMEM((tm, tn), F32)],
        args=(a, b, *[t[0] if isinstance(t, tuple) else t for t in extra]), after=after)
    return res if isinstance(out_dtype, tuple) or n_colsum else res[0]


def _rstd(h):
    return lax.rsqrt(jnp.mean(h * h, axis=-1, keepdims=True) + RMS_EPS)


def _sigmoid(z):
    return 1.0 / (1.0 + jnp.exp(-z))


def _rms_fwd(x, g, *, tm, name):
    n = x.shape[0]

    def body(x_ref, g_ref, o_ref):
        h = x_ref[...]
        o_ref[...] = (h * _rstd(h) * g_ref[...]).astype(BF16)

    return pl.pallas_call(
        body, name=name, grid=(n // tm,),
        in_specs=[_rows(tm, D_MODEL), _const((1, D_MODEL))],
        out_specs=_rows(tm, D_MODEL), out_shape=_sds((n, D_MODEL), BF16),
        compiler_params=_params("parallel"),
    )(x, g)


def _swap_halves(t):
    lane = lax.broadcasted_iota(jnp.int32, (t.shape[0], 128), 1)
    pieces = [t[:, c:c + 128] for c in range(0, t.shape[1], 128)]
    return jnp.concatenate([jnp.where((lane & 63) < 32, pltpu.roll(h, 96, 1), pltpu.roll(h, 32, 1))
                            for h in pieces], axis=1)


def _dil_spec(dil, tm):
    return pl.BlockSpec((dil, tm // dil, 256), lambda i: (0, i, 0))


def _dil_scratch(tm):
    return pltpu.VMEM((2, tm, 128), F32)


def _load_token_order(src, scr, dil, tm):
    if dil == 1:
        return src[0]
    for j in range(dil):
        for c in range(2):
            scr[c, pl.ds(j, tm // dil, stride=dil), :] = src[j, :, c * 128:(c + 1) * 128]
    return jnp.concatenate([scr[0], scr[1]], axis=1)


def _store_dil_order(val, dst, scr, dil, row0=0):
    rows = val.shape[0]
    if dil == 1:
        dst[0, row0:row0 + rows, :] = val.astype(dst.dtype)
        return
    for c in range(2):
        scr[c] = val[:, c * 128:(c + 1) * 128]
    for j in range(dil):
        for c in range(2):
            dst[j, row0 // dil:(row0 + rows) // dil, c * 128:(c + 1) * 128] = (
                scr[c, pl.ds(j, rows // dil, stride=dil), :].astype(dst.dtype))


def _project_in(a, w_t, cos_t, sin_t, *, tm, name, after=None):
    n = a.shape[0]
    n_dil = len(DIL_DILATIONS)
    na_w, dil_w = 3 * NA_WIDTH, 3 * DIL_WIDTH
    chunk = min(EPILOGUE_ROWS, tm)
    extra = [] if after is None else [after]

    def body(a_ref, w_ref, cos_ref, sin_ref, *rest):
        na_ref, gate_ref = rest[len(extra):len(extra) + 2]
        outs, scr = rest[len(extra) + 2:len(extra) + 2 + 3 * n_dil], rest[-1]

        def part(r0, first, width):
            return lax.dot_general(a_ref[r0:r0 + chunk, :], w_ref[first:first + width, :], NT_DIMS,
                                   preferred_element_type=F32)

        for r0 in range(0, tm, chunk):
            na_ref[r0:r0 + chunk, :] = part(r0, 0, na_w).astype(BF16)
            dil_part = part(r0, na_w, dil_w)
            cosv, sinv = cos_ref[r0:r0 + chunk, :], sin_ref[r0:r0 + chunk, :]
            for t in range(3):
                for gi, dil in enumerate(DIL_DILATIONS):
                    c0 = (t * n_dil + gi) * 256
                    val = dil_part[:, c0:c0 + 256]
                    if t < 2:
                        val = val * cosv + _swap_halves(val) * sinv
                    _store_dil_order(val, outs[t * n_dil + gi], scr, dil, r0)
            gate_ref[r0:r0 + chunk, :] = _sigmoid(part(r0, na_w + dil_w, 2 * D_MODEL)).astype(BF16)

    out_specs = [_rows(tm, na_w), _rows(tm, 2 * D_MODEL)]
    out_shape = [_sds((n, na_w), BF16), _sds((n, 2 * D_MODEL), BF16)]
    for _ in range(3):
        for dil in DIL_DILATIONS:
            out_specs.append(pl.BlockSpec((dil, tm // dil, 256), lambda i: (0, i, 0)))
            out_shape.append(_sds((dil, n // dil, 256), BF16))
    res = pl.pallas_call(
        body, name=name, grid=(n // tm,),
        in_specs=[_rows(tm, D_MODEL), _const(w_t.shape), _rows(tm, 256), _rows(tm, 256)]
                 + [pl.BlockSpec(memory_space=pl.ANY)] * len(extra),
        out_specs=out_specs, out_shape=out_shape,
        scratch_shapes=[pltpu.VMEM((2, chunk, 128), F32)],
        compiler_params=_params("parallel"),
    )(a, w_t, cos_t, sin_t, *extra)
    return res[0], res[1], res[2:5], res[5:8], res[8:11]


def _residual_rms_tile(delta, h, g):
    hn = h + delta
    return hn, hn * _rstd(hn) * g


def _gate_mix_tile(b2, s1, b1, s2):
    return b2, s1.astype(F32) * b1.astype(F32) + s2.astype(F32) * b2


def _gate_bwd_tile(dm, s1, b1, s2, b2):
    s1, b1, s2, b2 = (t.astype(F32) for t in (s1, b1, s2, b2))
    return dm * s1, dm * s2, dm * b1 * s1 * (1.0 - s1), dm * b2 * s2 * (1.0 - s2)


def _tail_tile(gt, pp, h2, target, g):
    sg = _sigmoid(gt)
    h3 = h2 + sg * pp
    r3 = _rstd(h3)
    n3 = h3 * r3
    err = n3 * g - target
    loss = 0.5 * jnp.sum(jnp.sum(err * err, axis=-1, keepdims=True) / D_MODEL)
    dy = err / D_MODEL
    dn = dy * g
    dh3 = r3 * (dn - n3 * jnp.mean(dn * n3, axis=-1, keepdims=True))
    return (dh3, dh3 * sg, dh3 * pp * sg * (1.0 - sg),
            jnp.sum(dy * n3, axis=0, keepdims=True), jnp.full((1, gt.shape[1]), loss, F32))


def _rms_bwd_tile(dz, h, g, dres):
    r = _rstd(h)
    nrm = h * r
    dn = dz * g
    dh = dres + r * (dn - nrm * jnp.mean(dn * nrm, axis=-1, keepdims=True))
    return dh, jnp.sum(dz * nrm, axis=0, keepdims=True)


def _rms_bwd_twice(dz, h, g, dres):
    dh, dg = _rms_bwd_tile(dz, h, g, dres)
    return dh, dh, dg


def _assemble_dproj(dna, ddil_q, ddil_k, ddil_v, dgn, dgd, cos_t, sin_t, *, tm, name):
    n = dgn.shape[0]

    def body(*refs):
        dq_ref, dk_ref, dv_ref = refs[0:3]
        dil_in = refs[3:12]
        dgn_ref, dgd_ref, cos_ref, sin_ref, o_ref, scr = refs[12:18]
        o_ref[:, 0:512] = dq_ref[...]
        o_ref[:, 512:1024] = dk_ref[...].astype(BF16)
        o_ref[:, 1024:1536] = dv_ref[...].astype(BF16)
        cosv, sinv = cos_ref[...], sin_ref[...]
        for t in range(3):
            for gi, dil in enumerate(DIL_DILATIONS):
                val = _load_token_order(dil_in[t * 3 + gi], scr, dil, tm)
                if t < 2:
                    val = val * cosv + _swap_halves(val * sinv)
                c0 = 1536 + t * DIL_WIDTH + gi * 256
                o_ref[:, c0:c0 + 256] = val.astype(BF16)
        o_ref[:, 3840:4864] = dgn_ref[...]
        o_ref[:, 4864:5888] = dgd_ref[...]

    in_specs = [_rows(tm, NA_WIDTH)] * 3
    for _ in range(3):
        for dil in DIL_DILATIONS:
            in_specs.append(pl.BlockSpec((dil, tm // dil, 256), lambda i: (0, i, 0)))
    in_specs += [_rows(tm, D_MODEL)] * 2 + [_rows(tm, 256)] * 2
    return pl.pallas_call(
        body, name=name, grid=(n // tm,), in_specs=in_specs,
        out_specs=_rows(tm, IN_WIDTH), out_shape=_sds((n, IN_WIDTH), BF16),
        scratch_shapes=[_dil_scratch(tm)],
        compiler_params=_params("parallel"),
    )(*dna, *ddil_q, *ddil_k, *ddil_v, dgn, dgd, cos_t, sin_t)


N_ROW_OFF = 2 * NA_WIN_ROWS - 1
N_PAIRS = N_ROW_OFF - 1
RB_WIDTH = (N_ROW_OFF + 1) * GRID_W


def _na_bias(rb_ref, pair_scr):
    shape = (GRID_W, RB_WIDTH)
    qc = lax.broadcasted_iota(jnp.int32, shape, 0)
    qc2 = lax.broadcasted_iota(jnp.int32, (GRID_W, 128), 0)
    kc2 = lax.broadcasted_iota(jnp.int32, (GRID_W, 128), 1) & (GRID_W - 1)
    cs = jnp.clip(qc2 - 8, 0, GRID_W - 16)
    valid = (kc2 >= cs) & (kc2 < cs + 16)
    for hh in range(2):
        t = jnp.broadcast_to(rb_ref[hh], shape)
        t = pltpu.roll(t, RB_WIDTH - 15, 1)
        for b in range(6):
            t = jnp.where(((qc >> b) & 1) == 1, pltpu.roll(t, 1 << b, 1), t)
        t_odd = pltpu.roll(t, RB_WIDTH - GRID_W, 1)
        for ro in range(N_PAIRS):
            src = t if ro % 2 == 0 else t_odd
            base = (ro // 2) * 128
            pair_scr[hh, ro] = jnp.where(valid, src[:, base:base + 128], NEG_INF)


NA_GROUP_FWD = 8
NA_GROUP_BWD = 4


def _stack_heads(ref, r, scale=1.0):
    lane = lax.broadcasted_iota(jnp.int32, (GRID_W, 128), 1)
    t = ref[pl.ds(pl.multiple_of(r * GRID_W, GRID_W), GRID_W), :].astype(F32) * scale
    return jnp.concatenate([jnp.where(lane < 64, t, 0.0), jnp.where(lane >= 64, t, 0.0)], axis=0).astype(BF16)


def _unstack_heads(t2):
    lane = lax.broadcasted_iota(jnp.int32, (GRID_W, 128), 1)
    return jnp.where(lane < 64, t2[:GRID_W], t2[GRID_W:])


def _na_window(k_ref, v_ref, r, n_rows):
    rs = jnp.clip(r - NA_WIN_ROWS // 2, 0, n_rows - NA_WIN_ROWS)
    ro0 = (NA_WIN_ROWS - 1) - (r - rs)
    off = pl.multiple_of(rs * GRID_W, GRID_W)
    kw = k_ref[pl.ds(off, NA_WIN_ROWS * GRID_W), :]
    vw = v_ref[pl.ds(off, NA_WIN_ROWS * GRID_W), :]
    return kw, vw, off, ro0


def _na_probs(s_raw, pair_scr, ro0):
    bias = [jnp.concatenate([pair_scr[hh, ro0 + 2 * j] for j in range(NA_WIN_ROWS // 2)], axis=1)
            for hh in range(2)]
    s = s_raw + jnp.concatenate(bias, axis=0)
    m = jnp.max(s, axis=-1, keepdims=True)
    e = jnp.exp(s - m)
    return e * (1.0 / jnp.sum(e, axis=-1, keepdims=True))


def _na_qkv_specs(n):
    pairs = NA_WIDTH // 128
    return [pl.BlockSpec((n, 128), lambda h, first=t * pairs: (0, first + h)) for t in range(3)]


def _na_fwd(qkv, rb, *, name):
    n = qkv.shape[0]
    n_rows = n // GRID_W

    def body(ins, outs, scr):
        q_ref, k_ref, v_ref, rb_ref = ins
        o_ref, = outs
        pair_scr, = scr
        _na_bias(rb_ref, pair_scr)

        def group(g, carry):
            rows = [g * NA_GROUP_FWD + t for t in range(NA_GROUP_FWD)]
            wins = [_na_window(k_ref, v_ref, r, n_rows) for r in rows]
            raw = [lax.dot_general(_stack_heads(q_ref, r, QK_SCALE), w[0], NT_DIMS, preferred_element_type=F32)
                   for r, w in zip(rows, wins)]
            probs = [_na_probs(s, pair_scr, w[3]) for s, w in zip(raw, wins)]
            outs2 = [jnp.dot(p.astype(BF16), w[1], preferred_element_type=F32) for p, w in zip(probs, wins)]
            for r, o2 in zip(rows, outs2):
                o_ref[pl.ds(pl.multiple_of(r * GRID_W, GRID_W), GRID_W), :] = _unstack_heads(o2).astype(BF16)
            return carry

        lax.fori_loop(0, n_rows // NA_GROUP_FWD, group, 0)

    col = pl.BlockSpec((n, 128), lambda h: (0, h))
    return _call(
        body, name=name, grid=(NA_WIDTH // 128,),
        in_specs=_na_qkv_specs(n) + [pl.BlockSpec((2, 1, RB_WIDTH), lambda h: (h, 0, 0))],
        out_specs=[col], out_shape=[_sds((n, NA_WIDTH), BF16)],
        scratch_shapes=[pltpu.VMEM((2, N_PAIRS, GRID_W, 128), F32)],
        args=(qkv, qkv, qkv, rb))[0]


def _na_bwd(qkv, do, rb, *, name):
    n = qkv.shape[0]
    n_rows = n // GRID_W
    win = NA_WIN_ROWS * GRID_W

    def body(ins, outs, scr):
        q_ref, k_ref, v_ref, do_ref, rb_ref = ins
        dq_ref, dk_ref, dv_ref, drb_ref = outs
        pair_scr, acc_scr = scr
        _na_bias(rb_ref, pair_scr)
        acc_scr[...] = jnp.zeros_like(acc_scr)
        dk_ref[...] = jnp.zeros_like(dk_ref)
        dv_ref[...] = jnp.zeros_like(dv_ref)

        def group(g, carry):
            rows = [g * NA_GROUP_BWD + t for t in range(NA_GROUP_BWD)]
            wins = [_na_window(k_ref, v_ref, r, n_rows) for r in rows]
            qss = [_stack_heads(q_ref, r, QK_SCALE) for r in rows]
            doss = [_stack_heads(do_ref, r) for r in rows]
            raw = [lax.dot_general(qs, w[0], NT_DIMS, preferred_element_type=F32) for qs, w in zip(qss, wins)]
            dps = [lax.dot_general(dos, w[1], NT_DIMS, preferred_element_type=F32) for dos, w in zip(doss, wins)]
            probs = [_na_probs(s, pair_scr, w[3]) for s, w in zip(raw, wins)]
            dss = [p * (dp - jnp.sum(p * dp, axis=-1, keepdims=True)) for p, dp in zip(probs, dps)]
            dsbs = [ds.astype(BF16) for ds in dss]
            dq2s = [jnp.dot(dsb, w[0], preferred_element_type=F32) for dsb, w in zip(dsbs, wins)]
            dkws = [lax.dot_general(dsb, qs, TN_DIMS, preferred_element_type=F32) for dsb, qs in zip(dsbs, qss)]
            dvws = [lax.dot_general(p.astype(BF16), dos, TN_DIMS, preferred_element_type=F32)
                    for p, dos in zip(probs, doss)]
            for t, r in enumerate(rows):
                _, _, off, ro0 = wins[t]
                for hh in range(2):
                    for j in range(NA_WIN_ROWS // 2):
                        acc_scr[hh, ro0 + 2 * j] += dss[t][hh * GRID_W:(hh + 1) * GRID_W, j * 128:(j + 1) * 128]
                dq_ref[pl.ds(pl.multiple_of(r * GRID_W, GRID_W), GRID_W), :] = (
                    _unstack_heads(dq2s[t]) * QK_SCALE).astype(BF16)
                dk_ref[pl.ds(off, win), :] += dkws[t]
                dv_ref[pl.ds(off, win), :] += dvws[t]
            return carry

        lax.fori_loop(0, n_rows // NA_GROUP_BWD, group, 0)

        qc = lax.broadcasted_iota(jnp.int32, (N_PAIRS * GRID_W, 128), 0)
        for hh in range(2):
            t = acc_scr[hh].reshape(N_PAIRS * GRID_W, 128)
            for b in range(6):
                t = jnp.where(((qc >> b) & 1) == 1, pltpu.roll(t, 128 - (1 << b), 1), t)
            t = pltpu.roll(t, 15, 1)
            drb_ref[hh] = jnp.sum(t.reshape(N_PAIRS, GRID_W, 128), axis=1)

    col = pl.BlockSpec((n, 128), lambda h: (0, h))
    return _call(
        body, name=name, grid=(NA_WIDTH // 128,),
        in_specs=_na_qkv_specs(n) + [col, pl.BlockSpec((2, 1, RB_WIDTH), lambda h: (h, 0, 0))],
        out_specs=[col, col, col, pl.BlockSpec((2, N_PAIRS, 128), lambda h: (h, 0, 0))],
        out_shape=[_sds((n, NA_WIDTH), BF16), _sds((n, NA_WIDTH), F32), _sds((n, NA_WIDTH), F32),
                   _sds((8, N_PAIRS, 128), F32)],
        scratch_shapes=[pltpu.VMEM((2, N_PAIRS, GRID_W, 128), F32),
                        pltpu.VMEM((2, N_PAIRS, GRID_W, 128), F32)],
        args=(qkv, qkv, qkv, do, rb))


def _rpb_table(rpb2):
    t = jnp.pad(rpb2, ((0, 0), (0, 1), (0, GRID_W - rpb2.shape[-1])))
    return t.reshape(8, 1, RB_WIDTH)


def _rpb_grad(drb, *, name):
    kdim = drb.shape[1]

    def body(x_ref, o_ref):
        kk = lax.broadcasted_iota(jnp.int32, (128, 512), 0)
        jj = lax.broadcasted_iota(jnp.int32, (128, 512), 1)
        half, co = kk >> 6, kk & 63
        acc = jnp.zeros((8, 512), F32)
        for ro in range(N_PAIRS):
            hit = ((ro + half) == (jj >> 5)) & (co == (jj & 31)) & (co < 31)
            onehot = jnp.where(hit, 1.0, 0.0).astype(F32)
            acc = acc + jnp.dot(x_ref[:, ro * 128:(ro + 1) * 128], onehot, preferred_element_type=F32,
                                precision=lax.Precision.HIGHEST)
        o_ref[...] = acc

    return pl.pallas_call(
        body, name=name, grid=(1,),
        in_specs=[_const((8, kdim))], out_specs=_const((8, 512)), out_shape=_sds((8, 512), F32),
        compiler_params=_params("arbitrary"),
    )(drb)


DIL_GROUP = 2


def _dil_blocks(length):
    qb = min(128, length)
    return qb, min(qb + 2 * DIL_RADIUS, length), min(DIL_GROUP, length // qb)


def _stack_lanes(ref, t, qb, scale=1.0):
    lane = lax.broadcasted_iota(jnp.int32, (qb, 256), 1)
    val = ref[0, t * qb:(t + 1) * qb, :].astype(F32) * scale
    return jnp.concatenate([jnp.where((lane >> 6) == h, val, 0.0) for h in range(4)], axis=0).astype(BF16)


def _dil_window(k_ref, v_ref, blk, qb, win, length):
    start = pl.multiple_of(jnp.clip(blk * qb - DIL_RADIUS, 0, length - win), DIL_RADIUS)
    return k_ref[0, pl.ds(start, win), :], v_ref[0, pl.ds(start, win), :], start


def _dil_caps_init(caps_scr, qb, win):
    @pl.when((pl.program_id(0) == 0) & (pl.program_id(1) == 0))
    def _():
        gap = ((lax.broadcasted_iota(jnp.int32, (4 * qb, win), 0) & (qb - 1))
               - lax.broadcasted_iota(jnp.int32, (4 * qb, win), 1))
        for v in range(3):
            caps_scr[v] = jnp.where(jnp.abs(gap + v * DIL_RADIUS) <= DIL_RADIUS, jnp.inf, NEG_INF)


def _dil_mask(s, blk, start, qb, caps_scr):
    return jnp.minimum(s, caps_scr[(blk * qb - start) // DIL_RADIUS])


def _pick_heads(stacked, qb):
    lane = lax.broadcasted_iota(jnp.int32, (qb, 256), 1)
    out = jnp.zeros((qb, 256), stacked.dtype)
    for h in range(4):
        out = jnp.where((lane >> 6) == h, stacked[h * qb:(h + 1) * qb], out)
    return out


def _stack_head_cols(ref, t, qb):
    return jnp.concatenate([ref[0, t * qb:(t + 1) * qb, 64 * h:64 * h + 1] for h in range(4)], axis=0)


def _dil_fwd(q, k, v, *, name, after=None):
    dil, length, _ = q.shape
    qb, win, grp = _dil_blocks(length)
    extra = [] if after is None else [after]

    def body(q_ref, k_ref, v_ref, *rest):
        o_ref, lse_ref, caps_scr = rest[-3:]
        _dil_caps_init(caps_scr, qb, win)
        blks = [pl.program_id(1) * grp + t for t in range(grp)]
        wins = [_dil_window(k_ref, v_ref, b, qb, win, length) for b in blks]
        raw = [lax.dot_general(_stack_lanes(q_ref, t, qb, QK_SCALE), w[0], NT_DIMS, preferred_element_type=F32)
               for t, w in enumerate(wins)]
        lses, outs = [], []
        for t, (s, w) in enumerate(zip(raw, wins)):
            s = _dil_mask(s, blks[t], w[2], qb, caps_scr)
            m = jnp.max(s, axis=-1, keepdims=True)
            e = jnp.exp(s - m)
            norm = jnp.sum(e, axis=-1, keepdims=True)
            lses.append(m + jnp.log(norm))
            outs.append(jnp.dot((e * (1.0 / norm)).astype(BF16), w[1], preferred_element_type=F32))
        for t in range(grp):
            o_ref[0, t * qb:(t + 1) * qb, :] = _pick_heads(outs[t], qb)
            lse_ref[0, t * qb:(t + 1) * qb, :] = _pick_heads(jnp.broadcast_to(lses[t], (4 * qb, 256)), qb)

    seq = pl.BlockSpec((1, length, 256), lambda j, i: (j, 0, 0))
    blk = pl.BlockSpec((1, grp * qb, 256), lambda j, i: (j, i, 0))
    return pl.pallas_call(
        body, name=name, grid=(dil, length // (grp * qb)),
        in_specs=[blk, seq, seq] + [pl.BlockSpec(memory_space=pl.ANY)] * len(extra), out_specs=[blk, blk],
        out_shape=[_sds((dil, length, 256), F32)] * 2,
        scratch_shapes=[pltpu.VMEM((3, 4 * qb, win), F32)],
        compiler_params=_params("arbitrary", "arbitrary"),
    )(q, k, v, *extra)


def _dil_bwd(q, k, v, do, lse, cc, *, name):
    dil, length, _ = q.shape
    qb, win, grp = _dil_blocks(length)

    def body(q_ref, k_ref, v_ref, do_ref, lse_ref, cc_ref, dq_ref, dk_ref, dv_ref, caps_scr):
        _dil_caps_init(caps_scr, qb, win)

        @pl.when(pl.program_id(1) == 0)
        def _():
            dk_ref[...] = jnp.zeros_like(dk_ref)
            dv_ref[...] = jnp.zeros_like(dv_ref)

        blks = [pl.program_id(1) * grp + t for t in range(grp)]
        wins = [_dil_window(k_ref, v_ref, b, qb, win, length) for b in blks]
        qss = [_stack_lanes(q_ref, t, qb, QK_SCALE) for t in range(grp)]
        doss = [_stack_lanes(do_ref, t, qb) for t in range(grp)]
        raw = [lax.dot_general(qs, w[0], NT_DIMS, preferred_element_type=F32) for qs, w in zip(qss, wins)]
        dps = [lax.dot_general(dos, w[1], NT_DIMS, preferred_element_type=F32) for dos, w in zip(doss, wins)]
        probs = [jnp.exp(_dil_mask(s, blks[t], wins[t][2], qb, caps_scr) - _stack_head_cols(lse_ref, t, qb))
                 for t, s in enumerate(raw)]
        dsbs = [(p * (dp + _stack_head_cols(cc_ref, t, qb))).astype(BF16)
                for t, (p, dp) in enumerate(zip(probs, dps))]
        dq4s = [jnp.dot(dsb, w[0], preferred_element_type=F32) for dsb, w in zip(dsbs, wins)]
        dkws = [lax.dot_general(dsb, qs, TN_DIMS, preferred_element_type=F32) for dsb, qs in zip(dsbs, qss)]
        dvws = [lax.dot_general(p.astype(BF16), dos, TN_DIMS, preferred_element_type=F32)
                for p, dos in zip(probs, doss)]
        for t in range(grp):
            dq_ref[0, t * qb:(t + 1) * qb, :] = _pick_heads(dq4s[t], qb) * QK_SCALE
            dk_ref[0, pl.ds(wins[t][2], win), :] += dkws[t]
            dv_ref[0, pl.ds(wins[t][2], win), :] += dvws[t]

    seq = pl.BlockSpec((1, length, 256), lambda j, i: (j, 0, 0))
    blk = pl.BlockSpec((1, grp * qb, 256), lambda j, i: (j, i, 0))
    return pl.pallas_call(
        body, name=name, grid=(dil, length // (grp * qb)),
        in_specs=[blk, seq, seq, blk, blk, blk], out_specs=[blk, seq, seq],
        out_shape=[_sds((dil, length, 256), F32)] * 3,
        scratch_shapes=[pltpu.VMEM((3, 4 * qb, win), F32)],
        compiler_params=_params("arbitrary", "arbitrary"),
    )(q, k, v, do, lse, cc)


def _merge_weights(lses):
    m = jnp.maximum(jnp.maximum(lses[0], lses[1]), lses[2])
    es = [jnp.exp(t - m) for t in lses]
    inv = 1.0 / (es[0] + es[1] + es[2])
    return [e * inv for e in es]


def _dil_merge(outs, lses, *, tm, name):
    n = outs[0].shape[1]

    def body(*refs):
        o_in, l_in = refs[0:3], refs[3:6]
        y_ref, yb_ref, scr = refs[6:9]
        lv = [_load_token_order(l_in[g], scr, d, tm) for g, d in enumerate(DIL_DILATIONS)]
        ws = _merge_weights(lv)
        y = jnp.zeros((tm, 256), F32)
        for g, d in enumerate(DIL_DILATIONS):
            y = y + ws[g] * _load_token_order(o_in[g], scr, d, tm)
        y_ref[...] = y
        yb_ref[...] = y.astype(BF16)

    specs = [_dil_spec(d, tm) for d in DIL_DILATIONS]
    return pl.pallas_call(
        body, name=name, grid=(n // tm,), in_specs=specs + specs,
        out_specs=[_rows(tm, 256)] * 2, out_shape=[_sds((n, 256), F32), _sds((n, 256), BF16)],
        scratch_shapes=[_dil_scratch(tm)],
        compiler_params=_params("parallel"),
    )(*outs, *lses)


def _dil_merge_bwd(dy, y, lses, *, tm, name):
    n = dy.shape[0]

    def body(*refs):
        dy_ref, y_ref = refs[0:2]
        l_in = refs[2:5]
        do_out, cc_out = refs[5:8], refs[8:11]
        scr = refs[11]
        lv = [_load_token_order(l_in[g], scr, d, tm) for g, d in enumerate(DIL_DILATIONS)]
        ws = _merge_weights(lv)
        dyv = dy_ref[...]
        rr = lax.broadcasted_iota(jnp.int32, (256, 256), 0) >> 6
        cc = lax.broadcasted_iota(jnp.int32, (256, 256), 1) >> 6
        ones = jnp.where(rr == cc, 1.0, 0.0).astype(F32)
        tsum = jnp.dot(dyv * y_ref[...], ones, preferred_element_type=F32,
                       precision=lax.Precision.HIGHEST)
        for g, d in enumerate(DIL_DILATIONS):
            _store_dil_order(ws[g] * dyv, do_out[g], scr, d)
            _store_dil_order(-ws[g] * tsum, cc_out[g], scr, d)

    specs = [_dil_spec(d, tm) for d in DIL_DILATIONS]
    res = pl.pallas_call(
        body, name=name, grid=(n // tm,),
        in_specs=[_rows(tm, 256)] * 2 + specs,
        out_specs=specs + specs,
        out_shape=[_sds((d, n // d, 256), BF16) for d in DIL_DILATIONS]
                  + [_sds((d, n // d, 256), F32) for d in DIL_DILATIONS],
        scratch_shapes=[_dil_scratch(tm)],
        compiler_params=_params("parallel"),
    )(dy, y, *lses)
    return res[0:3], res[3:6]


_WEIGHTS = (("w_in", 1, 736), ("w_branch_na", 1, 128), ("w_branch_dil", 1, 128), ("w_out", 0, 128),
            ("w_up", 1, 512), ("w_down", 0, 512), ("w_ple_gate", 0, 128), ("w_ple_proj", 1, 128))
_W_IN, _W_BNA, _W_BD, _W_OUT, _W_UP, _W_DOWN, _W_PG, _W_PP = range(8)


def _to_full(gathered):
    return gathered.reshape(-1, gathered.shape[2])


def _to_chunks(widx, mat):
    return mat.reshape(N_DEV, _WEIGHTS[widx][2], mat.shape[1])


def _local_step(x, p_bf16, positions, target, g_mix, g_mlp, g_ple, g_final, rpb2,
                get_w_in, relay_rest, get_rest, send_grads):
    tm = 512
    half = HEAD_DIM // 2
    inv_freq = 10000.0 ** (-jnp.arange(half, dtype=F32) / half)
    ang = positions.astype(F32)[:, None] * inv_freq
    cos, sin = jnp.cos(ang), jnp.sin(ang)
    cos_t = jnp.tile(jnp.concatenate([cos, cos], axis=-1), (1, 4))
    sin_t = jnp.tile(jnp.concatenate([-sin, sin], axis=-1), (1, 4))
    rb = _rpb_table(rpb2)

    a = _rms_fwd(x, g_mix, tm=tm, name="rms_mix")
    w_in, token = get_w_in(a)
    na_qkv, gates, dq_g, dk_g, dv_g = _project_in(a, w_in, cos_t, sin_t, tm=512, name="mm_in", after=token)
    sn, sd = (gates, 0), (gates, 1)
    y_na = _na_fwd(na_qkv, rb, name="na_fwd")
    token = relay_rest(y_na)
    d_out, d_lse = [], []
    for g in range(3):
        o, lse = _dil_fwd(dq_g[g], dk_g[g], dv_g[g], name=f"dil_fwd{g}", after=token if g == 0 else None)
        d_out.append(o)
        d_lse.append(lse)
    y_dil, y_dil_b = _dil_merge(d_out, d_lse, tm=tm, name="dil_merge")
    w_bna, w_bd, w_out, w_up, w_down, w_pg, w_pp = get_rest(y_dil_b)
    bn = _matmul(y_na, w_bna, tb=True, out_dtype=BF16, tm=512, tn=1024, tk=512, name="mm_bna")
    bd, mixed = _matmul(y_dil_b, w_bd, tb=True, out_dtype=(BF16, BF16), tm=512, tn=1024, tk=256, name="mm_bd",
                        extra=(sn, bn, sd), epilogue=_gate_mix_tile)
    h1, c = _matmul(mixed, w_out, out_dtype=(F32, BF16), tm=512, tn=1024, tk=1024, name="mm_out",
                    extra=(x, g_mlp), epilogue=_residual_rms_tile)
    u, f = _matmul(c, w_up, tb=True, out_dtype=(BF16, BF16), tm=512, tn=2048, tk=1024, name="mm_up",
                   epilogue=lambda acc: (acc, jnp.square(jnp.maximum(acc, 0.0))))
    h2, e = _matmul(f, w_down, out_dtype=(F32, BF16), tm=512, tn=1024, tk=4096, name="mm_down",
                    extra=(h1, g_ple), epilogue=_residual_rms_tile)
    pp = _matmul(p_bf16, w_pp, tb=True, out_dtype=F32, tm=512, tn=1024, tk=256, name="mm_pp")

    dh3, dpp, dgt, dg_final, loss = _matmul(
        e, w_pg, out_dtype=(F32, BF16, BF16), tm=512, tn=1024, tk=1024, name="mm_pg_tail",
        extra=(pp, h2, target, g_final), epilogue=_tail_tile, n_colsum=2)
    loss = loss[:, :128]
    gw_pp = _matmul(p_bf16, dpp, ta=True, transpose_out=True, out_dtype=BF16, tm=256, tn=1024, tk=2048,
                    name="mm_gw_pp")
    gw_pg = _matmul(e, dgt, ta=True, out_dtype=BF16, tm=512, tn=1024, tk=2048, name="mm_gw_pg")
    dh2, dh2_b, dg_ple = _matmul(
        dgt, w_pg, tb=True, out_dtype=(F32, BF16), tm=512, tn=1024, tk=1024, name="mm_de",
        extra=(h2, g_ple, dh3), epilogue=_rms_bwd_twice, n_colsum=1)
    du = _matmul(dh2_b, w_down, tb=True, out_dtype=BF16, tm=512, tn=2048, tk=1024, name="mm_du",
                 extra=(u,), epilogue=lambda acc, uv: (acc * (2.0 * jnp.maximum(uv.astype(F32), 0.0)),))
    gw_down = _matmul(f, dh2_b, ta=True, out_dtype=BF16, tm=1024, tn=1024, tk=2048, name="mm_gw_down")
    gw_up = _matmul(c, du, ta=True, transpose_out=True, out_dtype=BF16, tm=512, tn=2048, tk=2048, name="mm_gw_up")
    dh1, dh1_b, dg_mlp = _matmul(
        du, w_up, out_dtype=(F32, BF16), tm=512, tn=1024, tk=4096, name="mm_dc",
        extra=(h1, g_mlp, dh2), epilogue=_rms_bwd_twice, n_colsum=1)
    dbn, dbd, dgn, dgd = _matmul(dh1_b, w_out, tb=True, out_dtype=(BF16,) * 4, tm=512, tn=1024, tk=1024,
                                 name="mm_dmixed", extra=(sn, bn, sd, bd), epilogue=_gate_bwd_tile)
    gw_out = _matmul(mixed, dh1_b, ta=True, out_dtype=BF16, tm=512, tn=1024, tk=2048, name="mm_gw_out")
    gw_bna = _matmul(y_na, dbn, ta=True, transpose_out=True, out_dtype=BF16, tm=512, tn=1024, tk=2048,
                     name="mm_gw_bna")
    dy_na = _matmul(dbn, w_bna, out_dtype=BF16, tm=512, tn=512, tk=1024, name="mm_dy_na")
    gw_bd = _matmul(y_dil_b, dbd, ta=True, transpose_out=True, out_dtype=BF16, tm=256, tn=1024, tk=2048,
                    name="mm_gw_bd")
    token = send_grads((_W_PP, _W_PG, _W_DOWN, _W_UP, _W_OUT, _W_BNA, _W_BD),
                       (gw_pp, gw_pg, gw_down, gw_up, gw_out, gw_bna, gw_bd))
    dy_dil = _matmul(dbd, w_bd, out_dtype=F32, tm=512, tn=256, tk=1024, name="mm_dy_dil", after=token)
    dna = _na_bwd(na_qkv, dy_na, rb, name="na_bwd")
    drpb = _rpb_grad(dna[3].reshape(8, -1), name="rpb_grad")
    do_g, cc_g = _dil_merge_bwd(dy_dil, y_dil, d_lse, tm=tm, name="dil_merge_bwd")
    ddq, ddk, ddv = [], [], []
    for g in range(3):
        r = _dil_bwd(dq_g[g], dk_g[g], dv_g[g], do_g[g], d_lse[g], cc_g[g], name=f"dil_bwd{g}")
        ddq.append(r[0])
        ddk.append(r[1])
        ddv.append(r[2])
    dproj = _assemble_dproj(dna[0:3], ddq, ddk, ddv, dgn, dgd, cos_t, sin_t, tm=tm, name="assemble_dproj")
    gw_in = _matmul(a, dproj, ta=True, transpose_out=True, out_dtype=BF16, tm=512, tn=2944, tk=2048, name="mm_gw_in")
    token = send_grads((_W_IN,), (gw_in,))
    dx, dg_mix = _matmul(
        dproj, w_in, out_dtype=(F32,), tm=512, tn=1024, tk=5888, name="mm_da", after=token,
        extra=(x, g_mix, dh1), epilogue=_rms_bwd_tile, n_colsum=1)
    return loss, dx, (dg_mix, dg_mlp, dg_ple, dg_final), drpb


def _cast_bf16(t, *, name):
    def body(t_ref, o_ref):
        o_ref[...] = t_ref[...].astype(BF16)

    rows, cols = t.shape
    tr = 256 if rows % 256 == 0 else rows
    blk = pl.BlockSpec((tr, cols), lambda i: (i, 0))
    return pl.pallas_call(body, name=name, grid=(rows // tr,), in_specs=[blk], out_specs=blk,
                          out_shape=_sds(t.shape, BF16), compiler_params=_params("parallel"))(t)


def _adamw(w, g, m, v):
    m = ADAM_B1 * m + (1.0 - ADAM_B1) * g
    v = ADAM_B2 * v + (1.0 - ADAM_B2) * (g * g)
    m_hat = m / (1.0 - ADAM_B1 ** ADAM_STEP)
    v_hat = v / (1.0 - ADAM_B2 ** ADAM_STEP)
    delta = -ADAM_LR * (m_hat / (jnp.sqrt(v_hat) + ADAM_EPS) + ADAM_WD * w)
    return delta, m, v


def _sum_adamw(parts, w, m, v, *, tr, name, own=None, transposed=False):
    rows, cols = w.shape
    n_pre = 0 if own is None else 1

    def body(*refs):
        p_ref, w_ref, m_ref, v_ref = refs[n_pre:n_pre + 4]
        g_ref, d_ref, nm_ref, nv_ref = refs[-4:]
        g = (p_ref[0] if own is None else refs[n_pre + 4][...]).astype(F32)
        for s in range(1, N_DEV):
            g = g + p_ref[s].astype(F32)
        if transposed:
            g = g.T
        g_ref[...] = g
        d_ref[...], nm_ref[...], nv_ref[...] = _adamw(w_ref[...], g, m_ref[...], v_ref[...])

    if transposed:
        blk = pl.BlockSpec((rows, tr), lambda i, *_: (0, i))
        g_rows, steps = rows, cols // tr
    else:
        blk = pl.BlockSpec((tr, cols), lambda i, *_: (i, 0))
        g_rows, steps = cols, rows // tr
    in_specs = [pl.BlockSpec((N_DEV, tr, g_rows), lambda i, *_: (0, i, 0)), blk, blk, blk]
    args = [parts, w, m, v]
    if own is not None:
        in_specs.append(pl.BlockSpec((None, tr, g_rows), lambda i, idx: (idx[0], i, 0)))
        args = [own[1]] + args + [own[0]]
    return pl.pallas_call(
        body, name=name,
        grid_spec=pltpu.PrefetchScalarGridSpec(num_scalar_prefetch=n_pre, grid=(steps,), in_specs=in_specs,
                                               out_specs=[blk] * 4),
        out_shape=[_sds((rows, cols), F32)] * 4,
        compiler_params=_params("parallel"),
    )(*args)


_RPB_SIZE = 8 * 15 * 31


def _pack_small(g_mix, g_mlp, g_ple, g_final, rpb, loss_row):
    flat = jnp.concatenate([g_mix.reshape(-1), g_mlp.reshape(-1), g_ple.reshape(-1), g_final.reshape(-1),
                            rpb.reshape(-1), jnp.zeros((3840 - _RPB_SIZE,), F32), loss_row.reshape(-1),
                            jnp.zeros((128,), F32)])
    return flat.reshape(64, 128)


def _unpack_small(t):
    flat = t.reshape(-1)
    return (flat[0:1024].reshape(1, 1024), flat[4096:4096 + _RPB_SIZE].reshape(1, 8, 15, 31),
            flat[1024:2048].reshape(1, 1024), flat[2048:3072].reshape(1, 1024), flat[3072:4096])


def kernel(x, p, positions, g_mix, w_in, rpb, w_branch_na, w_branch_dil, w_out, g_mlp, w_up, w_down, g_ple, w_ple_gate, w_ple_proj, g_final, loss_target, m_g_mix, m_w_in, m_rpb, m_w_branch_na, m_w_branch_dil, m_w_out, m_g_mlp, m_w_up, m_w_down, m_g_ple, m_w_ple_gate, m_w_ple_proj, m_g_final, v_g_mix, v_w_in, v_rpb, v_w_branch_na, v_w_branch_dil, v_w_out, v_g_mlp, v_w_up, v_w_down, v_g_ple, v_w_ple_gate, v_w_ple_proj, v_g_final):
    sharded = dict(w_in=(w_in, m_w_in, v_w_in), w_branch_na=(w_branch_na, m_w_branch_na, v_w_branch_na),
                   w_branch_dil=(w_branch_dil, m_w_branch_dil, v_w_branch_dil), w_out=(w_out, m_w_out, v_w_out),
                   w_up=(w_up, m_w_up, v_w_up), w_down=(w_down, m_w_down, v_w_down),
                   w_ple_gate=(w_ple_gate, m_w_ple_gate, v_w_ple_gate),
                   w_ple_proj=(w_ple_proj, m_w_ple_proj, v_w_ple_proj))
    shards = {k: tuple(t[0] for t in val) for k, val in sharded.items()}

    me = _my_index()

    shards["w_in"] = tuple(t.T for t in shards["w_in"])

    w_in_b = _cast_bf16(shards["w_in"][0], name="cast_w_in")
    rest_b = [shards[name][0].astype(BF16).T if axis == 1 else shards[name][0].astype(BF16)
              for name, axis, _ in _WEIGHTS[1:]]
    first_in, token_in = _start_copies(_first_leg_copies, [w_in_b], [_sds((N_DEV,) + w_in_b.shape, BF16)], 4,
                                       name="start_gather_w_in")

    def whole(landed, mine):
        return _to_full(lax.dynamic_update_index_in_dim(landed, mine, me, 0))

    rest = {}

    def get_w_in(after):
        (mine,), landed = _wait_copies(_first_leg_copies, first_in, after, name="wait_gather_w_in")
        second, token = _start_copies(_second_leg_copies, [], landed, 3, name="start_forward_w_in")
        _, (landed,) = _wait_copies(_second_leg_copies, second, token, name="wait_forward_w_in")
        rest["first"], token = _start_copies(_first_leg_copies, rest_b,
                                             [_sds((N_DEV,) + t.shape, BF16) for t in rest_b], 4 * len(rest_b),
                                             name="start_gather_rest", after=landed)
        return whole(landed, mine), token

    def relay_rest(after):
        rest["mine"], landed = _wait_copies(_first_leg_copies, rest["first"], after, name="wait_gather_rest")
        rest["second"], token = _start_copies(_second_leg_copies, [], landed, 3 * len(rest_b),
                                              name="start_forward_rest")
        return token

    def get_rest(after):
        _, landed = _wait_copies(_second_leg_copies, rest["second"], after, name="wait_forward_rest")
        return [whole(t, own) for t, own in zip(landed, rest["mine"])]

    sent = []

    def send_grads(indices, grads):
        chunked = [_to_chunks(i, g) for i, g in zip(indices, grads)]
        handle, token = _start_copies(_exchange_copies, chunked, [_sds(t.shape, BF16) for t in chunked],
                                      7 * len(chunked),
                                      name="start_exchange_" + ("w_in" if indices == (_W_IN,) else "rest"))
        sent.append((indices, handle))
        return token

    g_mix_0 = g_mix + token_in[0:1, 0:1]
    loss, dx, dgs, drpb = _local_step(
        x[0], p[0, 0].astype(BF16), positions[0], loss_target[0],
        g_mix_0, g_mlp, g_ple, g_final.reshape(1, -1), rpb[0], get_w_in, relay_rest, get_rest, send_grads)

    drpb3 = drpb.reshape(8, 16, 32)[:, :15, :31]
    small = _pack_small(dgs[0], dgs[1], dgs[2], dgs[3], drpb3, loss)
    share, done = _start_copies(_gather_copies, [small], [_sds((N_DEV,) + small.shape, F32)], 7,
                                name="start_share_small")

    out = {}
    for indices, handle in sent:
        chunked, landed = _wait_copies(_exchange_copies, handle, done,
                                       name="wait_exchange_" + ("w_in" if indices == (_W_IN,) else "rest"))
        for i, part, mine in zip(indices, landed, chunked):
            name = _WEIGHTS[i][0]
            w, m, v = shards[name]
            turned = _WEIGHTS[i][1] == 1 and i != _W_IN
            res = _sum_adamw(part, w, m, v, tr=368 if i == _W_IN else 128, name="adamw_" + name,
                             own=(mine, me.reshape(1).astype(jnp.int32)), transposed=turned)
            out[name] = [(t.T if i == _W_IN else t)[None] for t in res]
            done = res[0]
    (small,), (small_landed,) = _wait_copies(_gather_copies, share, done, name="wait_share_small")
    small_all = lax.dynamic_update_index_in_dim(small_landed, small, me, 0)
    small_w = _pack_small(g_mix, g_mlp, g_ple, g_final, rpb, jnp.zeros((128,), F32))
    small_m = _pack_small(m_g_mix, m_g_mlp, m_g_ple, m_g_final, m_rpb, jnp.zeros((128,), F32))
    small_v = _pack_small(v_g_mix, v_g_mlp, v_g_ple, v_g_final, v_rpb, jnp.zeros((128,), F32))
    res = _sum_adamw(small_all, small_w, small_m, small_v, tr=64, name="adamw_small")
    unpacked = [_unpack_small(t) for t in res]
    for i, name in enumerate(("g_mix", "rpb", "g_mlp", "g_ple", "g_final")):
        out[name] = [u[i] for u in unpacked]
    loss_total = res[0][62, 0]

    order = ("g_mix", "w_in", "rpb", "w_branch_na", "w_branch_dil", "w_out", "g_mlp", "w_up", "w_down",
             "g_ple", "w_ple_gate", "w_ple_proj", "g_final")
    grads = [out[k][0] for k in order]
    deltas = [out[k][1] for k in order]
    new_m = [out[k][2] for k in order]
    new_v = [out[k][3] for k in order]
    return (loss_total, dx[None], *grads, *deltas, *new_m, *new_v)
```

```python
import jax
import jax.numpy as jnp
from jax import lax
from jax.experimental import pallas as pl
from jax.experimental.pallas import tpu as pltpu

F32 = jnp.float32
BF16 = jnp.bfloat16

D_MODEL = 1024
HEAD_DIM = 64
GRID_W = 64
NA_WIDTH = 512
DIL_WIDTH = 768
IN_WIDTH = 5888
DIL_DILATIONS = (1, 4, 16)
DIL_RADIUS = 64
NA_WIN_ROWS = 8
RMS_EPS = 1e-6
NEG_INF = -1e30
QK_SCALE = HEAD_DIM ** -0.5

ADAM_LR = 0.001
ADAM_B1 = 0.9
ADAM_B2 = 0.999
ADAM_EPS = 1e-08
ADAM_WD = 0.01
ADAM_STEP = 10

N_DEV = 8
VMEM_LIMIT = 56 * 1024 * 1024
EPILOGUE_ROWS = 256
MESH = pl.DeviceIdType.MESH

NT_DIMS = (((1,), (1,)), ((), ()))
TN_DIMS = (((0,), (0,)), ((), ()))


def _sds(shape, dtype):
    return jax.ShapeDtypeStruct(shape, dtype)


def _params(*sem):
    return pltpu.CompilerParams(dimension_semantics=sem, vmem_limit_bytes=VMEM_LIMIT)


def _rows(tm, width, col=0):
    return pl.BlockSpec((tm, width), lambda i, c=col: (i, c))


def _const(shape):
    zeros = (0,) * len(shape)
    return pl.BlockSpec(shape, lambda i: zeros)


def _my_index():
    return 4 * lax.axis_index("x") + 2 * lax.axis_index("y") + lax.axis_index("c")


def _peer(k):
    x, y, c = lax.axis_index("x"), lax.axis_index("y"), lax.axis_index("c")
    px = 1 - x if k & 4 else x
    py = 1 - y if k & 2 else y
    pc = 1 - c if k & 1 else c
    return (px, py, pc), 4 * px + 2 * py + pc


def _call(body, *, name, grid, in_specs, out_specs, out_shape, scratch_shapes, args, after=None):
    n_in, n_out = len(in_specs), len(out_specs)
    extra = [] if after is None else [after]
    n_x = n_in + len(extra)

    def plain(*refs):
        body(refs[:n_in], refs[n_x:n_x + n_out], refs[n_x + n_out:])

    res = pl.pallas_call(plain, name=name, grid=grid,
                         in_specs=list(in_specs) + [pl.BlockSpec(memory_space=pl.ANY)] * len(extra),
                         out_specs=out_specs, out_shape=out_shape, scratch_shapes=scratch_shapes,
                         compiler_params=_params(*(("arbitrary",) * len(grid))))(*args, *extra)
    return list(res)


_HBM_SPEC = pl.BlockSpec(memory_space=pltpu.HBM)
_SEM_SPEC = pl.BlockSpec(memory_space=pltpu.SEMAPHORE)
_SIDE_EFFECT = pltpu.SideEffectType.DATAFLOW_SIDE_EFFECTING


_FIRST_LEG = (1, 2, 4, 6)
_SECOND_LEG = (2, 4, 6)


def _gather_copies(srcs, lands, send, recv, sending):
    me = _my_index()
    out = []
    for w in range(len(srcs)):
        for k in range(1, N_DEV):
            dev, idx = _peer(k)
            out.append(pltpu.make_async_remote_copy(
                src_ref=srcs[w], dst_ref=lands[w].at[me if sending else idx],
                send_sem=send.at[w * 7 + k - 1], recv_sem=recv.at[w * 7 + k - 1],
                device_id=dev, device_id_type=MESH))
    return out


def _first_leg_copies(srcs, lands, send, recv, sending):
    me = _my_index()
    out = []
    for w in range(len(srcs)):
        for j, k in enumerate(_FIRST_LEG):
            dev, idx = _peer(k)
            out.append(pltpu.make_async_remote_copy(
                src_ref=srcs[w], dst_ref=lands[w].at[me if sending else idx],
                send_sem=send.at[w * 4 + j], recv_sem=recv.at[w * 4 + j],
                device_id=dev, device_id_type=MESH))
    return out


def _second_leg_copies(srcs, lands, send, recv, sending):
    sibling, _ = _peer(1)
    out = []
    for w in range(len(lands)):
        for j, k in enumerate(_SECOND_LEG):
            slot = _peer(k if sending else k ^ 1)[1]
            out.append(pltpu.make_async_remote_copy(
                src_ref=lands[w].at[slot], dst_ref=lands[w].at[slot],
                send_sem=send.at[w * 3 + j], recv_sem=recv.at[w * 3 + j],
                device_id=sibling, device_id_type=MESH))
    return out


def _exchange_copies(srcs, lands, send, recv, sending):
    out = []
    for w in range(len(srcs)):
        for k in range(1, N_DEV):
            dev, idx = _peer(k)
            out.append(pltpu.make_async_remote_copy(
                src_ref=srcs[w].at[idx], dst_ref=lands[w].at[k],
                send_sem=send.at[w * 7 + k - 1], recv_sem=recv.at[w * 7 + k - 1],
                device_id=dev, device_id_type=MESH))
    return out


def _start_copies(make, srcs, lands, n_copies, *, name, after=None):
    n_src, n_buf = len(srcs), len(srcs) + len(lands)
    extra = [] if after is None else [after]

    def body(*refs):
        send, recv = refs[n_buf + len(extra)], refs[n_buf + len(extra) + 1]
        for cp in make(refs[:n_src], refs[n_src:n_buf], send, recv, True):
            cp.start()
        refs[-1][...] = jnp.zeros_like(refs[-1])

    bufs = list(srcs) + [lax.empty(t.shape, t.dtype) if isinstance(t, jax.ShapeDtypeStruct) else t for t in lands]
    res = pl.pallas_call(
        body, name=name,
        out_shape=(pltpu.SemaphoreType.DMA((n_copies,)), pltpu.SemaphoreType.DMA((n_copies,)),
                   *[pltpu.HBM(t.shape, t.dtype) for t in bufs], _sds((8, 128), F32)),
        in_specs=[_HBM_SPEC] * n_buf + [pl.BlockSpec(memory_space=pl.ANY)] * len(extra),
        out_specs=(_SEM_SPEC, _SEM_SPEC, *([_HBM_SPEC] * n_buf), pl.BlockSpec(memory_space=pltpu.VMEM)),
        input_output_aliases={i: 2 + i for i in range(n_buf)},
        compiler_params=pltpu.CompilerParams(has_side_effects=_SIDE_EFFECT),
    )(*[pltpu.with_memory_space_constraint(t, pltpu.HBM) for t in bufs], *extra)
    return (n_src, res[0], res[1], res[2:2 + n_buf]), res[-1]


def _wait_copies(make, handle, after, *, name):
    n_src, send_sems, recv_sems, bufs = handle
    n_buf = len(bufs)

    def body(*refs):
        for cp in make(refs[:n_src], refs[n_src:n_buf], refs[n_buf], refs[n_buf + 1], False):
            cp.wait_send()
            cp.wait_recv()

    res = pl.pallas_call(
        body, name=name,
        out_shape=tuple(pltpu.HBM(t.shape, t.dtype) for t in bufs),
        in_specs=[_HBM_SPEC] * n_buf + [_SEM_SPEC, _SEM_SPEC, pl.BlockSpec(memory_space=pl.ANY)],
        out_specs=tuple([_HBM_SPEC] * n_buf),
        input_output_aliases={i: i for i in range(n_buf)},
        compiler_params=pltpu.CompilerParams(has_side_effects=_SIDE_EFFECT),
    )(*bufs, send_sems, recv_sems, after)
    return list(res[:n_src]), list(res[n_src:])


def _matmul(a, b, *, ta=False, tb=False, out_dtype, tm, tn, tk, name, after=None, extra=(), epilogue=None,
            n_colsum=0, transpose_out=False):
    m, k = (a.shape[1], a.shape[0]) if ta else a.shape
    n = b.shape[0] if tb else b.shape[1]
    tm, tn, tk = min(tm, m), min(tn, n), min(tk, k)
    nk = k // tk
    dims = (((0 if ta else 1,), (1 if tb else 0,)), ((), ()))
    out_dtypes = out_dtype if isinstance(out_dtype, tuple) else (out_dtype,)
    n_tiles = len(out_dtypes)

    def add_colsums(o_refs, sums):
        i = pl.program_id(1)
        for s_ref, val in zip(o_refs[n_tiles:], sums):
            @pl.when(i == 0)
            def _(s_ref=s_ref, val=val):
                s_ref[...] = val

            @pl.when(i > 0)
            def _(s_ref=s_ref, val=val):
                s_ref[...] += val

    def finish(acc, x_refs, o_refs):
        vals = (acc,) if epilogue is None else epilogue(acc, *[r[...] for r in x_refs])
        for o_ref, val in zip(o_refs[:n_tiles], vals[:n_tiles]):
            o_ref[...] = (val.T if transpose_out else val).astype(o_ref.dtype)
        add_colsums(o_refs, vals[n_tiles:])

    chunk = EPILOGUE_ROWS if (nk == 1 and epilogue is not None and not ta and tm % EPILOGUE_ROWS == 0) else None

    def body(ins, outs, acc):
        a_ref, b_ref = ins[:2]
        if chunk is not None:
            sums = None
            for r0 in range(0, tm, chunk):
                part = lax.dot_general(a_ref[r0:r0 + chunk, :], b_ref[...], dims, preferred_element_type=F32)
                vals = epilogue(part, *[r[...] if r.shape[0] == 1 else r[r0:r0 + chunk, :] for r in ins[2:]])
                for o_ref, val in zip(outs[:n_tiles], vals[:n_tiles]):
                    o_ref[r0:r0 + chunk, :] = val.astype(o_ref.dtype)
                sums = vals[n_tiles:] if sums is None else [s + v for s, v in zip(sums, vals[n_tiles:])]
            add_colsums(outs, sums)
            return
        part = lax.dot_general(a_ref[...], b_ref[...], dims, preferred_element_type=F32)
        if nk == 1:
            finish(part, ins[2:], outs)
            return
        acc_ref, = acc
        kk = pl.program_id(2)

        @pl.when(kk == 0)
        def _():
            acc_ref[...] = part

        @pl.when(kk > 0)
        def _():
            acc_ref[...] += part

        @pl.when(kk == nk - 1)
        def _():
            finish(acc_ref[...], ins[2:], outs)

    a_spec = (pl.BlockSpec((tk, tm), lambda j, i, kk: (kk, i)) if ta
              else pl.BlockSpec((tm, tk), lambda j, i, kk: (i, kk)))
    b_spec = (pl.BlockSpec((tn, tk), lambda j, i, kk: (j, kk)) if tb
              else pl.BlockSpec((tk, tn), lambda j, i, kk: (kk, j)))
    tile = pl.BlockSpec((tm, tn), lambda j, i, kk: (i, j))
    row = pl.BlockSpec((1, tn), lambda j, i, kk: (0, j))

    def x_spec(t):
        if isinstance(t, tuple):
            return pl.BlockSpec((tm, tn), lambda j, i, kk, first=t[1] * (n // tn): (i, first + j))
        return row if t.shape[0] == 1 else tile

    out_tile, out_dims = (pl.BlockSpec((tn, tm), lambda j, i, kk: (j, i)), (n, m)) if transpose_out else (tile, (m, n))
    res = _call(
        body, name=name, grid=(n // tn, m // tm, nk),
        in_specs=[a_spec, b_spec] + [x_spec(t) for t in extra],
        out_specs=[out_tile] * n_tiles + [row] * n_colsum,
        out_shape=[_sds(out_dims, dt) for dt in out_dtypes] + [_sds((1, n), F32)] * n_colsum,
        scratch_shapes=[] if nk == 1 else [pltpu.VMEM((tm, tn), F32)],
        args=(a, b, *[t[0] if isinstance(t, tuple) else t for t in extra]), after=after)
    return res if isinstance(out_dtype, tuple) or n_colsum else res[0]


def _rstd(h):
    return lax.rsqrt(jnp.mean(h * h, axis=-1, keepdims=True) + RMS_EPS)


def _sigmoid(z):
    return 1.0 / (1.0 + jnp.exp(-z))


def _rms_fwd(x, g, *, tm, name):
    n = x.shape[0]

    def body(x_ref, g_ref, o_ref):
        h = x_ref[...]
        o_ref[...] = (h * _rstd(h) * g_ref[...]).astype(BF16)

    return pl.pallas_call(
        body, name=name, grid=(n // tm,),
        in_specs=[_rows(tm, D_MODEL), _const((1, D_MODEL))],
        out_specs=_rows(tm, D_MODEL), out_shape=_sds((n, D_MODEL), BF16),
        compiler_params=_params("parallel"),
    )(x, g)


def _swap_halves(t):
    lane = lax.broadcasted_iota(jnp.int32, (t.shape[0], 128), 1)
    pieces = [t[:, c:c + 128] for c in range(0, t.shape[1], 128)]
    return jnp.concatenate([jnp.where((lane & 63) < 32, pltpu.roll(h, 96, 1), pltpu.roll(h, 32, 1))
                            for h in pieces], axis=1)


def _dil_spec(dil, tm):
    return pl.BlockSpec((dil, tm // dil, 256), lambda i: (0, i, 0))


def _dil_scratch(tm):
    return pltpu.VMEM((2, tm, 128), F32)


def _load_token_order(src, scr, dil, rows, row0=0):
    if dil == 1:
        return src[0, row0:row0 + rows, :]
    for j in range(dil):
        for c in range(2):
            scr[c, pl.ds(j, rows // dil, stride=dil), :] = (
                src[j, row0 // dil:(row0 + rows) // dil, c * 128:(c + 1) * 128])
    return jnp.concatenate([scr[0, 0:rows, :], scr[1, 0:rows, :]], axis=1)


def _store_dil_order(val, dst, scr, dil, row0=0):
    rows = val.shape[0]
    if dil == 1:
        dst[0, row0:row0 + rows, :] = val.astype(dst.dtype)
        return
    for c in range(2):
        scr[c] = val[:, c * 128:(c + 1) * 128]
    for j in range(dil):
        for c in range(2):
            dst[j, row0 // dil:(row0 + rows) // dil, c * 128:(c + 1) * 128] = (
                scr[c, pl.ds(j, rows // dil, stride=dil), :].astype(dst.dtype))


def _project_in(a, w_t, cos_t, sin_t, *, tm, name, after=None):
    n = a.shape[0]
    n_dil = len(DIL_DILATIONS)
    na_w, dil_w = 3 * NA_WIDTH, 3 * DIL_WIDTH
    chunk = min(EPILOGUE_ROWS, tm)
    extra = [] if after is None else [after]

    def body(a_ref, w_ref, cos_ref, sin_ref, *rest):
        na_ref, gate_ref = rest[len(extra):len(extra) + 2]
        outs, scr = rest[len(extra) + 2:len(extra) + 2 + 3 * n_dil], rest[-1]

        def part(r0, first, width):
            return lax.dot_general(a_ref[r0:r0 + chunk, :], w_ref[first:first + width, :], NT_DIMS,
                                   preferred_element_type=F32)

        for r0 in range(0, tm, chunk):
            na_ref[r0:r0 + chunk, :] = part(r0, 0, na_w).astype(BF16)
            dil_part = part(r0, na_w, dil_w)
            cosv, sinv = cos_ref[r0:r0 + chunk, :], sin_ref[r0:r0 + chunk, :]
            for t in range(3):
                for gi, dil in enumerate(DIL_DILATIONS):
                    c0 = (t * n_dil + gi) * 256
                    val = dil_part[:, c0:c0 + 256]
                    if t < 2:
                        val = val * cosv + _swap_halves(val) * sinv
                    _store_dil_order(val, outs[t * n_dil + gi], scr, dil, r0)
            gate_ref[r0:r0 + chunk, :] = _sigmoid(part(r0, na_w + dil_w, 2 * D_MODEL)).astype(BF16)

    out_specs = [_rows(tm, na_w), _rows(tm, 2 * D_MODEL)]
    out_shape = [_sds((n, na_w), BF16), _sds((n, 2 * D_MODEL), BF16)]
    for _ in range(3):
        for dil in DIL_DILATIONS:
            out_specs.append(pl.BlockSpec((dil, tm // dil, 256), lambda i: (0, i, 0)))
            out_shape.append(_sds((dil, n // dil, 256), BF16))
    res = pl.pallas_call(
        body, name=name, grid=(n // tm,),
        in_specs=[_rows(tm, D_MODEL), _const(w_t.shape), _rows(tm, 256), _rows(tm, 256)]
                 + [pl.BlockSpec(memory_space=pl.ANY)] * len(extra),
        out_specs=out_specs, out_shape=out_shape,
        scratch_shapes=[pltpu.VMEM((2, chunk, 128), F32)],
        compiler_params=_params("parallel"),
    )(a, w_t, cos_t, sin_t, *extra)
    return res[0], res[1], res[2:5], res[5:8], res[8:11]


def _residual_rms_tile(delta, h, g):
    hn = h + delta
    return hn, hn * _rstd(hn) * g


def _gate_mix_tile(b2, s1, b1, s2):
    return b2, s1.astype(F32) * b1.astype(F32) + s2.astype(F32) * b2


def _gate_bwd_tile(dm, s1, b1, s2, b2):
    s1, b1, s2, b2 = (t.astype(F32) for t in (s1, b1, s2, b2))
    return dm * s1, dm * s2, dm * b1 * s1 * (1.0 - s1), dm * b2 * s2 * (1.0 - s2)


def _tail_tile(gt, pp, h2, target, g):
    sg = _sigmoid(gt)
    h3 = h2 + sg * pp
    r3 = _rstd(h3)
    n3 = h3 * r3
    err = n3 * g - target
    loss = 0.5 * jnp.sum(jnp.sum(err * err, axis=-1, keepdims=True) / D_MODEL)
    dy = err / D_MODEL
    dn = dy * g
    dh3 = r3 * (dn - n3 * jnp.mean(dn * n3, axis=-1, keepdims=True))
    return (dh3, dh3 * sg, dh3 * pp * sg * (1.0 - sg),
            jnp.sum(dy * n3, axis=0, keepdims=True), jnp.full((1, gt.shape[1]), loss, F32))


def _rms_bwd_tile(dz, h, g, dres):
    r = _rstd(h)
    nrm = h * r
    dn = dz * g
    dh = dres + r * (dn - nrm * jnp.mean(dn * nrm, axis=-1, keepdims=True))
    return dh, jnp.sum(dz * nrm, axis=0, keepdims=True)


def _rms_bwd_twice(dz, h, g, dres):
    dh, dg = _rms_bwd_tile(dz, h, g, dres)
    return dh, dh, dg


def _assemble_dproj(dna, ddil_q, ddil_k, ddil_v, dgn, dgd, cos_t, sin_t, *, tm, name):
    n = dgn.shape[0]

    def body(*refs):
        dq_ref, dk_ref, dv_ref = refs[0:3]
        dil_in = refs[3:12]
        dgn_ref, dgd_ref, cos_ref, sin_ref, o_ref, scr = refs[12:18]
        o_ref[:, 0:512] = dq_ref[...]
        o_ref[:, 512:1024] = dk_ref[...].astype(BF16)
        o_ref[:, 1024:1536] = dv_ref[...].astype(BF16)
        cosv, sinv = cos_ref[...], sin_ref[...]
        for t in range(3):
            for gi, dil in enumerate(DIL_DILATIONS):
                val = _load_token_order(dil_in[t * 3 + gi], scr, dil, tm)
                if t < 2:
                    val = val * cosv + _swap_halves(val * sinv)
                c0 = 1536 + t * DIL_WIDTH + gi * 256
                o_ref[:, c0:c0 + 256] = val.astype(BF16)
        o_ref[:, 3840:4864] = dgn_ref[...]
        o_ref[:, 4864:5888] = dgd_ref[...]

    in_specs = [_rows(tm, NA_WIDTH)] * 3
    for _ in range(3):
        for dil in DIL_DILATIONS:
            in_specs.append(pl.BlockSpec((dil, tm // dil, 256), lambda i: (0, i, 0)))
    in_specs += [_rows(tm, D_MODEL)] * 2 + [_rows(tm, 256)] * 2
    return pl.pallas_call(
        body, name=name, grid=(n // tm,), in_specs=in_specs,
        out_specs=_rows(tm, IN_WIDTH), out_shape=_sds((n, IN_WIDTH), BF16),
        scratch_shapes=[_dil_scratch(tm)],
        compiler_params=_params("parallel"),
    )(*dna, *ddil_q, *ddil_k, *ddil_v, dgn, dgd, cos_t, sin_t)


N_ROW_OFF = 2 * NA_WIN_ROWS - 1
N_PAIRS = N_ROW_OFF - 1
RB_WIDTH = (N_ROW_OFF + 1) * GRID_W


def _na_bias(rb_ref, pair_scr):
    shape = (GRID_W, RB_WIDTH)
    qc = lax.broadcasted_iota(jnp.int32, shape, 0)
    qc2 = lax.broadcasted_iota(jnp.int32, (GRID_W, 128), 0)
    kc2 = lax.broadcasted_iota(jnp.int32, (GRID_W, 128), 1) & (GRID_W - 1)
    cs = jnp.clip(qc2 - 8, 0, GRID_W - 16)
    valid = (kc2 >= cs) & (kc2 < cs + 16)
    for hh in range(2):
        t = jnp.broadcast_to(rb_ref[hh], shape)
        t = pltpu.roll(t, RB_WIDTH - 15, 1)
        for b in range(6):
            t = jnp.where(((qc >> b) & 1) == 1, pltpu.roll(t, 1 << b, 1), t)
        t_odd = pltpu.roll(t, RB_WIDTH - GRID_W, 1)
        for ro in range(N_PAIRS):
            src = t if ro % 2 == 0 else t_odd
            base = (ro // 2) * 128
            pair_scr[hh, ro] = jnp.where(valid, src[:, base:base + 128], NEG_INF)


NA_GROUP_FWD = 8
NA_GROUP_BWD = 4


def _stack_heads(ref, r, scale=1.0):
    lane = lax.broadcasted_iota(jnp.int32, (GRID_W, 128), 1)
    t = ref[pl.ds(pl.multiple_of(r * GRID_W, GRID_W), GRID_W), :].astype(F32) * scale
    return jnp.concatenate([jnp.where(lane < 64, t, 0.0), jnp.where(lane >= 64, t, 0.0)], axis=0).astype(BF16)


def _unstack_heads(t2):
    lane = lax.broadcasted_iota(jnp.int32, (GRID_W, 128), 1)
    return jnp.where(lane < 64, t2[:GRID_W], t2[GRID_W:])


def _na_window(k_ref, v_ref, r, n_rows):
    rs = jnp.clip(r - NA_WIN_ROWS // 2, 0, n_rows - NA_WIN_ROWS)
    ro0 = (NA_WIN_ROWS - 1) - (r - rs)
    off = pl.multiple_of(rs * GRID_W, GRID_W)
    kw = k_ref[pl.ds(off, NA_WIN_ROWS * GRID_W), :]
    vw = v_ref[pl.ds(off, NA_WIN_ROWS * GRID_W), :]
    return kw, vw, off, ro0


def _na_probs(s_raw, pair_scr, ro0):
    bias = [jnp.concatenate([pair_scr[hh, ro0 + 2 * j] for j in range(NA_WIN_ROWS // 2)], axis=1)
            for hh in range(2)]
    s = s_raw + jnp.concatenate(bias, axis=0)
    m = jnp.max(s, axis=-1, keepdims=True)
    e = jnp.exp(s - m)
    return e * (1.0 / jnp.sum(e, axis=-1, keepdims=True))


def _na_qkv_specs(n):
    pairs = NA_WIDTH // 128
    return [pl.BlockSpec((n, 128), lambda h, first=t * pairs: (0, first + h)) for t in range(3)]


def _na_fwd(qkv, rb, *, name):
    n = qkv.shape[0]
    n_rows = n // GRID_W

    def body(ins, outs, scr):
        q_ref, k_ref, v_ref, rb_ref = ins
        o_ref, = outs
        pair_scr, = scr
        _na_bias(rb_ref, pair_scr)

        def group(g, carry):
            rows = [g * NA_GROUP_FWD + t for t in range(NA_GROUP_FWD)]
            wins = [_na_window(k_ref, v_ref, r, n_rows) for r in rows]
            raw = [lax.dot_general(_stack_heads(q_ref, r, QK_SCALE), w[0], NT_DIMS, preferred_element_type=F32)
                   for r, w in zip(rows, wins)]
            probs = [_na_probs(s, pair_scr, w[3]) for s, w in zip(raw, wins)]
            outs2 = [jnp.dot(p.astype(BF16), w[1], preferred_element_type=F32) for p, w in zip(probs, wins)]
            for r, o2 in zip(rows, outs2):
                o_ref[pl.ds(pl.multiple_of(r * GRID_W, GRID_W), GRID_W), :] = _unstack_heads(o2).astype(BF16)
            return carry

        lax.fori_loop(0, n_rows // NA_GROUP_FWD, group, 0)

    col = pl.BlockSpec((n, 128), lambda h: (0, h))
    return _call(
        body, name=name, grid=(NA_WIDTH // 128,),
        in_specs=_na_qkv_specs(n) + [pl.BlockSpec((2, 1, RB_WIDTH), lambda h: (h, 0, 0))],
        out_specs=[col], out_shape=[_sds((n, NA_WIDTH), BF16)],
        scratch_shapes=[pltpu.VMEM((2, N_PAIRS, GRID_W, 128), F32)],
        args=(qkv, qkv, qkv, rb))[0]


def _na_bwd(qkv, do, rb, *, name):
    n = qkv.shape[0]
    n_rows = n // GRID_W
    win = NA_WIN_ROWS * GRID_W

    def body(ins, outs, scr):
        q_ref, k_ref, v_ref, do_ref, rb_ref = ins
        dq_ref, dk_ref, dv_ref, drb_ref = outs
        pair_scr, acc_scr = scr
        _na_bias(rb_ref, pair_scr)
        acc_scr[...] = jnp.zeros_like(acc_scr)
        dk_ref[...] = jnp.zeros_like(dk_ref)
        dv_ref[...] = jnp.zeros_like(dv_ref)

        def group(g, carry):
            rows = [g * NA_GROUP_BWD + t for t in range(NA_GROUP_BWD)]
            wins = [_na_window(k_ref, v_ref, r, n_rows) for r in rows]
            qss = [_stack_heads(q_ref, r, QK_SCALE) for r in rows]
            doss = [_stack_heads(do_ref, r) for r in rows]
            raw = [lax.dot_general(qs, w[0], NT_DIMS, preferred_element_type=F32) for qs, w in zip(qss, wins)]
            dps = [lax.dot_general(dos, w[1], NT_DIMS, preferred_element_type=F32) for dos, w in zip(doss, wins)]
            probs = [_na_probs(s, pair_scr, w[3]) for s, w in zip(raw, wins)]
            dss = [p * (dp - jnp.sum(p * dp, axis=-1, keepdims=True)) for p, dp in zip(probs, dps)]
            dsbs = [ds.astype(BF16) for ds in dss]
            dq2s = [jnp.dot(dsb, w[0], preferred_element_type=F32) for dsb, w in zip(dsbs, wins)]
            dkws = [lax.dot_general(dsb, qs, TN_DIMS, preferred_element_type=F32) for dsb, qs in zip(dsbs, qss)]
            dvws = [lax.dot_general(p.astype(BF16), dos, TN_DIMS, preferred_element_type=F32)
                    for p, dos in zip(probs, doss)]
            for t, r in enumerate(rows):
                _, _, off, ro0 = wins[t]
                for hh in range(2):
                    for j in range(NA_WIN_ROWS // 2):
                        acc_scr[hh, ro0 + 2 * j] += dss[t][hh * GRID_W:(hh + 1) * GRID_W, j * 128:(j + 1) * 128]
                dq_ref[pl.ds(pl.multiple_of(r * GRID_W, GRID_W), GRID_W), :] = (
                    _unstack_heads(dq2s[t]) * QK_SCALE).astype(BF16)
                dk_ref[pl.ds(off, win), :] += dkws[t]
                dv_ref[pl.ds(off, win), :] += dvws[t]
            return carry

        lax.fori_loop(0, n_rows // NA_GROUP_BWD, group, 0)

        qc = lax.broadcasted_iota(jnp.int32, (N_PAIRS * GRID_W, 128), 0)
        for hh in range(2):
            t = acc_scr[hh].reshape(N_PAIRS * GRID_W, 128)
            for b in range(6):
                t = jnp.where(((qc >> b) & 1) == 1, pltpu.roll(t, 128 - (1 << b), 1), t)
            t = pltpu.roll(t, 15, 1)
            drb_ref[hh] = jnp.sum(t.reshape(N_PAIRS, GRID_W, 128), axis=1)

    col = pl.BlockSpec((n, 128), lambda h: (0, h))
    return _call(
        body, name=name, grid=(NA_WIDTH // 128,),
        in_specs=_na_qkv_specs(n) + [col, pl.BlockSpec((2, 1, RB_WIDTH), lambda h: (h, 0, 0))],
        out_specs=[col, col, col, pl.BlockSpec((2, N_PAIRS, 128), lambda h: (h, 0, 0))],
        out_shape=[_sds((n, NA_WIDTH), BF16), _sds((n, NA_WIDTH), F32), _sds((n, NA_WIDTH), F32),
                   _sds((8, N_PAIRS, 128), F32)],
        scratch_shapes=[pltpu.VMEM((2, N_PAIRS, GRID_W, 128), F32),
                        pltpu.VMEM((2, N_PAIRS, GRID_W, 128), F32)],
        args=(qkv, qkv, qkv, do, rb))


def _rpb_table(rpb2):
    t = jnp.pad(rpb2, ((0, 0), (0, 1), (0, GRID_W - rpb2.shape[-1])))
    return t.reshape(8, 1, RB_WIDTH)


def _rpb_grad(drb, *, name):
    kdim = drb.shape[1]

    def body(x_ref, o_ref):
        kk = lax.broadcasted_iota(jnp.int32, (128, 512), 0)
        jj = lax.broadcasted_iota(jnp.int32, (128, 512), 1)
        half, co = kk >> 6, kk & 63
        acc = jnp.zeros((8, 512), F32)
        for ro in range(N_PAIRS):
            hit = ((ro + half) == (jj >> 5)) & (co == (jj & 31)) & (co < 31)
            onehot = jnp.where(hit, 1.0, 0.0).astype(F32)
            acc = acc + jnp.dot(x_ref[:, ro * 128:(ro + 1) * 128], onehot, preferred_element_type=F32,
                                precision=lax.Precision.HIGHEST)
        o_ref[...] = acc

    return pl.pallas_call(
        body, name=name, grid=(1,),
        in_specs=[_const((8, kdim))], out_specs=_const((8, 512)), out_shape=_sds((8, 512), F32),
        compiler_params=_params("arbitrary"),
    )(drb)


DIL_GROUP = 2


def _dil_blocks(length):
    qb = min(128, length)
    return qb, min(qb + 2 * DIL_RADIUS, length), min(DIL_GROUP, length // qb)


def _stack_lanes(ref, t, qb, scale=1.0):
    lane = lax.broadcasted_iota(jnp.int32, (qb, 256), 1)
    val = ref[0, t * qb:(t + 1) * qb, :].astype(F32) * scale
    return jnp.concatenate([jnp.where((lane >> 6) == h, val, 0.0) for h in range(4)], axis=0).astype(BF16)


def _dil_window(k_ref, v_ref, blk, qb, win, length):
    start = pl.multiple_of(jnp.clip(blk * qb - DIL_RADIUS, 0, length - win), DIL_RADIUS)
    return k_ref[0, pl.ds(start, win), :], v_ref[0, pl.ds(start, win), :], start


def _dil_caps_init(caps_scr, qb, win):
    @pl.when((pl.program_id(0) == 0) & (pl.program_id(1) == 0))
    def _():
        gap = ((lax.broadcasted_iota(jnp.int32, (4 * qb, win), 0) & (qb - 1))
               - lax.broadcasted_iota(jnp.int32, (4 * qb, win), 1))
        for v in range(3):
            caps_scr[v] = jnp.where(jnp.abs(gap + v * DIL_RADIUS) <= DIL_RADIUS, jnp.inf, NEG_INF)


def _dil_mask(s, blk, start, qb, caps_scr):
    return jnp.minimum(s, caps_scr[(blk * qb - start) // DIL_RADIUS])


def _pick_heads(stacked, qb):
    lane = lax.broadcasted_iota(jnp.int32, (qb, 256), 1)
    out = jnp.zeros((qb, 256), stacked.dtype)
    for h in range(4):
        out = jnp.where((lane >> 6) == h, stacked[h * qb:(h + 1) * qb], out)
    return out


def _stack_head_cols(ref, t, qb):
    return jnp.concatenate([ref[0, t * qb:(t + 1) * qb, 64 * h:64 * h + 1] for h in range(4)], axis=0)


def _dil_fwd(q, k, v, *, name, after=None):
    dil, length, _ = q.shape
    qb, win, grp = _dil_blocks(length)
    extra = [] if after is None else [after]

    def body(q_ref, k_ref, v_ref, *rest):
        o_ref, lse_ref, caps_scr = rest[-3:]
        _dil_caps_init(caps_scr, qb, win)
        blks = [pl.program_id(1) * grp + t for t in range(grp)]
        wins = [_dil_window(k_ref, v_ref, b, qb, win, length) for b in blks]
        raw = [lax.dot_general(_stack_lanes(q_ref, t, qb, QK_SCALE), w[0], NT_DIMS, preferred_element_type=F32)
               for t, w in enumerate(wins)]
        lses, outs = [], []
        for t, (s, w) in enumerate(zip(raw, wins)):
            s = _dil_mask(s, blks[t], w[2], qb, caps_scr)
            m = jnp.max(s, axis=-1, keepdims=True)
            e = jnp.exp(s - m)
            norm = jnp.sum(e, axis=-1, keepdims=True)
            lses.append(m + jnp.log(norm))
            outs.append(jnp.dot((e * (1.0 / norm)).astype(BF16), w[1], preferred_element_type=F32))
        for t in range(grp):
            o_ref[0, t * qb:(t + 1) * qb, :] = _pick_heads(outs[t], qb)
            lse_ref[0, t * qb:(t + 1) * qb, :] = _pick_heads(jnp.broadcast_to(lses[t], (4 * qb, 256)), qb)

    seq = pl.BlockSpec((1, length, 256), lambda j, i: (j, 0, 0))
    blk = pl.BlockSpec((1, grp * qb, 256), lambda j, i: (j, i, 0))
    return pl.pallas_call(
        body, name=name, grid=(dil, length // (grp * qb)),
        in_specs=[blk, seq, seq] + [pl.BlockSpec(memory_space=pl.ANY)] * len(extra), out_specs=[blk, blk],
        out_shape=[_sds((dil, length, 256), F32)] * 2,
        scratch_shapes=[pltpu.VMEM((3, 4 * qb, win), F32)],
        compiler_params=_params("arbitrary", "arbitrary"),
    )(q, k, v, *extra)


def _dil_bwd(q, k, v, do, lse, cc, *, name):
    dil, length, _ = q.shape
    qb, win, grp = _dil_blocks(length)

    def body(q_ref, k_ref, v_ref, do_ref, lse_ref, cc_ref, dq_ref, dk_ref, dv_ref, caps_scr):
        _dil_caps_init(caps_scr, qb, win)

        @pl.when(pl.program_id(1) == 0)
        def _():
            dk_ref[...] = jnp.zeros_like(dk_ref)
            dv_ref[...] = jnp.zeros_like(dv_ref)

        blks = [pl.program_id(1) * grp + t for t in range(grp)]
        wins = [_dil_window(k_ref, v_ref, b, qb, win, length) for b in blks]
        qss = [_stack_lanes(q_ref, t, qb, QK_SCALE) for t in range(grp)]
        doss = [_stack_lanes(do_ref, t, qb) for t in range(grp)]
        raw = [lax.dot_general(qs, w[0], NT_DIMS, preferred_element_type=F32) for qs, w in zip(qss, wins)]
        dps = [lax.dot_general(dos, w[1], NT_DIMS, preferred_element_type=F32) for dos, w in zip(doss, wins)]
        probs = [jnp.exp(_dil_mask(s, blks[t], wins[t][2], qb, caps_scr) - _stack_head_cols(lse_ref, t, qb))
                 for t, s in enumerate(raw)]
        dsbs = [(p * (dp + _stack_head_cols(cc_ref, t, qb))).astype(BF16)
                for t, (p, dp) in enumerate(zip(probs, dps))]
        dq4s = [jnp.dot(dsb, w[0], preferred_element_type=F32) for dsb, w in zip(dsbs, wins)]
        dkws = [lax.dot_general(dsb, qs, TN_DIMS, preferred_element_type=F32) for dsb, qs in zip(dsbs, qss)]
        dvws = [lax.dot_general(p.astype(BF16), dos, TN_DIMS, preferred_element_type=F32)
                for p, dos in zip(probs, doss)]
        for t in range(grp):
            dq_ref[0, t * qb:(t + 1) * qb, :] = _pick_heads(dq4s[t], qb) * QK_SCALE
            dk_ref[0, pl.ds(wins[t][2], win), :] += dkws[t]
            dv_ref[0, pl.ds(wins[t][2], win), :] += dvws[t]

    seq = pl.BlockSpec((1, length, 256), lambda j, i: (j, 0, 0))
    blk = pl.BlockSpec((1, grp * qb, 256), lambda j, i: (j, i, 0))
    return pl.pallas_call(
        body, name=name, grid=(dil, length // (grp * qb)),
        in_specs=[blk, seq, seq, blk, blk, blk], out_specs=[blk, seq, seq],
        out_shape=[_sds((dil, length, 256), F32)] * 3,
        scratch_shapes=[pltpu.VMEM((3, 4 * qb, win), F32)],
        compiler_params=_params("arbitrary", "arbitrary"),
    )(q, k, v, do, lse, cc)


def _merge_weights(lses):
    m = jnp.maximum(jnp.maximum(lses[0], lses[1]), lses[2])
    es = [jnp.exp(t - m) for t in lses]
    inv = 1.0 / (es[0] + es[1] + es[2])
    return [e * inv for e in es]


def _branch_mix(y_na, w_bna, outs, lses, w_bd, gates, *, tm, name):
    n = y_na.shape[0]
    chunk = min(EPILOGUE_ROWS, tm)

    def body(yna_ref, wn_ref, *rest):
        o_in, l_in = rest[0:3], rest[3:6]
        wd_ref, sn_ref, sd_ref = rest[6:9]
        y_ref, yb_ref, bn_ref, bd_ref, mix_ref, scr = rest[9:15]
        for r0 in range(0, tm, chunk):
            rows = slice(r0, r0 + chunk)
            lv = [_load_token_order(l_in[g], scr, d, chunk, r0) for g, d in enumerate(DIL_DILATIONS)]
            ws = _merge_weights(lv)
            y = jnp.zeros((chunk, 256), F32)
            for g, d in enumerate(DIL_DILATIONS):
                y = y + ws[g] * _load_token_order(o_in[g], scr, d, chunk, r0)
            yb = y.astype(BF16)
            y_ref[rows, :] = y
            yb_ref[rows, :] = yb
            bn = lax.dot_general(yna_ref[rows, :], wn_ref[...], NT_DIMS, preferred_element_type=F32).astype(BF16)
            bd = lax.dot_general(yb, wd_ref[...], NT_DIMS, preferred_element_type=F32)
            bn_ref[rows, :] = bn
            bd, mixed = _gate_mix_tile(bd, sn_ref[rows, :], bn, sd_ref[rows, :])
            bd_ref[rows, :] = bd.astype(BF16)
            mix_ref[rows, :] = mixed.astype(BF16)

    specs = [_dil_spec(d, tm) for d in DIL_DILATIONS]
    return pl.pallas_call(
        body, name=name, grid=(n // tm,),
        in_specs=[_rows(tm, NA_WIDTH), _const(w_bna.shape)] + specs + specs
                 + [_const(w_bd.shape), _rows(tm, D_MODEL, 0), _rows(tm, D_MODEL, 1)],
        out_specs=[_rows(tm, 256)] * 2 + [_rows(tm, D_MODEL)] * 3,
        out_shape=[_sds((n, 256), F32), _sds((n, 256), BF16)] + [_sds((n, D_MODEL), BF16)] * 3,
        scratch_shapes=[_dil_scratch(chunk)],
        compiler_params=_params("parallel"),
    )(y_na, w_bna, *outs, *lses, w_bd, gates, gates)


def _dil_merge_bwd(dy, y, lses, *, tm, name):
    n = dy.shape[0]

    def body(*refs):
        dy_ref, y_ref = refs[0:2]
        l_in = refs[2:5]
        do_out, cc_out = refs[5:8], refs[8:11]
        scr = refs[11]
        lv = [_load_token_order(l_in[g], scr, d, tm) for g, d in enumerate(DIL_DILATIONS)]
        ws = _merge_weights(lv)
        dyv = dy_ref[...]
        rr = lax.broadcasted_iota(jnp.int32, (256, 256), 0) >> 6
        cc = lax.broadcasted_iota(jnp.int32, (256, 256), 1) >> 6
        ones = jnp.where(rr == cc, 1.0, 0.0).astype(F32)
        tsum = jnp.dot(dyv * y_ref[...], ones, preferred_element_type=F32,
                       precision=lax.Precision.HIGHEST)
        for g, d in enumerate(DIL_DILATIONS):
            _store_dil_order(ws[g] * dyv, do_out[g], scr, d)
            _store_dil_order(-ws[g] * tsum, cc_out[g], scr, d)

    specs = [_dil_spec(d, tm) for d in DIL_DILATIONS]
    res = pl.pallas_call(
        body, name=name, grid=(n // tm,),
        in_specs=[_rows(tm, 256)] * 2 + specs,
        out_specs=specs + specs,
        out_shape=[_sds((d, n // d, 256), BF16) for d in DIL_DILATIONS]
                  + [_sds((d, n // d, 256), F32) for d in DIL_DILATIONS],
        scratch_shapes=[_dil_scratch(tm)],
        compiler_params=_params("parallel"),
    )(dy, y, *lses)
    return res[0:3], res[3:6]


_WEIGHTS = (("w_in", 1, 736), ("w_branch_na", 1, 128), ("w_branch_dil", 1, 128), ("w_out", 0, 128),
            ("w_up", 1, 512), ("w_down", 0, 512), ("w_ple_gate", 0, 128), ("w_ple_proj", 1, 128))
_W_IN, _W_BNA, _W_BD, _W_OUT, _W_UP, _W_DOWN, _W_PG, _W_PP = range(8)


def _to_full(gathered):
    return gathered.reshape(-1, gathered.shape[2])


def _to_chunks(widx, mat):
    return mat.reshape(N_DEV, _WEIGHTS[widx][2], mat.shape[1])


def _local_step(x, p_bf16, positions, target, g_mix, g_mlp, g_ple, g_final, rpb2,
                get_w_in, relay_rest, get_rest, send_grads):
    tm = 512
    half = HEAD_DIM // 2
    inv_freq = 10000.0 ** (-jnp.arange(half, dtype=F32) / half)
    ang = positions.astype(F32)[:, None] * inv_freq
    cos, sin = jnp.cos(ang), jnp.sin(ang)
    cos_t = jnp.tile(jnp.concatenate([cos, cos], axis=-1), (1, 4))
    sin_t = jnp.tile(jnp.concatenate([-sin, sin], axis=-1), (1, 4))
    rb = _rpb_table(rpb2)

    a = _rms_fwd(x, g_mix, tm=tm, name="rms_mix")
    w_in, token = get_w_in(a)
    na_qkv, gates, dq_g, dk_g, dv_g = _project_in(a, w_in, cos_t, sin_t, tm=512, name="mm_in", after=token)
    sn, sd = (gates, 0), (gates, 1)
    y_na = _na_fwd(na_qkv, rb, name="na_fwd")
    token = relay_rest(y_na)
    d_out, d_lse = [], []
    for g in range(3):
        o, lse = _dil_fwd(dq_g[g], dk_g[g], dv_g[g], name=f"dil_fwd{g}", after=token if g == 0 else None)
        d_out.append(o)
        d_lse.append(lse)
    w_bna, w_bd, w_out, w_up, w_down, w_pg, w_pp = get_rest(d_out[2])
    y_dil, y_dil_b, bn, bd, mixed = _branch_mix(y_na, w_bna, d_out, d_lse, w_bd, gates, tm=tm, name="branch_mix")
    h1, c = _matmul(mixed, w_out, out_dtype=(F32, BF16), tm=512, tn=1024, tk=1024, name="mm_out",
                    extra=(x, g_mlp), epilogue=_residual_rms_tile)
    u, f = _matmul(c, w_up, tb=True, out_dtype=(BF16, BF16), tm=512, tn=2048, tk=1024, name="mm_up",
                   epilogue=lambda acc: (acc, jnp.square(jnp.maximum(acc, 0.0))))
    h2, e = _matmul(f, w_down, out_dtype=(F32, BF16), tm=512, tn=1024, tk=4096, name="mm_down",
                    extra=(h1, g_ple), epilogue=_residual_rms_tile)
    pp = _matmul(p_bf16, w_pp, tb=True, out_dtype=F32, tm=512, tn=1024, tk=256, name="mm_pp")

    dh3, dpp, dgt, dg_final, loss = _matmul(
        e, w_pg, out_dtype=(F32, BF16, BF16), tm=512, tn=1024, tk=1024, name="mm_pg_tail",
        extra=(pp, h2, target, g_final), epilogue=_tail_tile, n_colsum=2)
    loss = loss[:, :128]
    gw_pp = _matmul(p_bf16, dpp, ta=True, transpose_out=True, out_dtype=BF16, tm=256, tn=1024, tk=2048,
                    name="mm_gw_pp")
    gw_pg = _matmul(e, dgt, ta=True, out_dtype=BF16, tm=512, tn=1024, tk=2048, name="mm_gw_pg")
    dh2, dh2_b, dg_ple = _matmul(
        dgt, w_pg, tb=True, out_dtype=(F32, BF16), tm=512, tn=1024, tk=1024, name="mm_de",
        extra=(h2, g_ple, dh3), epilogue=_rms_bwd_twice, n_colsum=1)
    du = _matmul(dh2_b, w_down, tb=True, out_dtype=BF16, tm=512, tn=2048, tk=1024, name="mm_du",
                 extra=(u,), epilogue=lambda acc, uv: (acc * (2.0 * jnp.maximum(uv.astype(F32), 0.0)),))
    gw_down = _matmul(f, dh2_b, ta=True, out_dtype=BF16, tm=1024, tn=1024, tk=2048, name="mm_gw_down")
    gw_up = _matmul(c, du, ta=True, transpose_out=True, out_dtype=BF16, tm=512, tn=2048, tk=2048, name="mm_gw_up")
    dh1, dh1_b, dg_mlp = _matmul(
        du, w_up, out_dtype=(F32, BF16), tm=512, tn=1024, tk=4096, name="mm_dc",
        extra=(h1, g_mlp, dh2), epilogue=_rms_bwd_twice, n_colsum=1)
    dbn, dbd, dgn, dgd = _matmul(dh1_b, w_out, tb=True, out_dtype=(BF16,) * 4, tm=512, tn=1024, tk=1024,
                                 name="mm_dmixed", extra=(sn, bn, sd, bd), epilogue=_gate_bwd_tile)
    gw_out = _matmul(mixed, dh1_b, ta=True, out_dtype=BF16, tm=512, tn=1024, tk=2048, name="mm_gw_out")
    gw_bna = _matmul(y_na, dbn, ta=True, transpose_out=True, out_dtype=BF16, tm=512, tn=1024, tk=2048,
                     name="mm_gw_bna")
    dy_na = _matmul(dbn, w_bna, out_dtype=BF16, tm=512, tn=512, tk=1024, name="mm_dy_na")
    gw_bd = _matmul(y_dil_b, dbd, ta=True, transpose_out=True, out_dtype=BF16, tm=256, tn=1024, tk=2048,
                    name="mm_gw_bd")
    token = send_grads((_W_PP, _W_PG, _W_DOWN, _W_UP, _W_OUT, _W_BNA, _W_BD),
                       (gw_pp, gw_pg, gw_down, gw_up, gw_out, gw_bna, gw_bd))
    dy_dil = _matmul(dbd, w_bd, out_dtype=F32, tm=512, tn=256, tk=1024, name="mm_dy_dil", after=token)
    dna = _na_bwd(na_qkv, dy_na, rb, name="na_bwd")
    drpb = _rpb_grad(dna[3].reshape(8, -1), name="rpb_grad")
    do_g, cc_g = _dil_merge_bwd(dy_dil, y_dil, d_lse, tm=tm, name="dil_merge_bwd")
    ddq, ddk, ddv = [], [], []
    for g in range(3):
        r = _dil_bwd(dq_g[g], dk_g[g], dv_g[g], do_g[g], d_lse[g], cc_g[g], name=f"dil_bwd{g}")
        ddq.append(r[0])
        ddk.append(r[1])
        ddv.append(r[2])
    dproj = _assemble_dproj(dna[0:3], ddq, ddk, ddv, dgn, dgd, cos_t, sin_t, tm=tm, name="assemble_dproj")
    gw_in = _matmul(a, dproj, ta=True, transpose_out=True, out_dtype=BF16, tm=512, tn=2944, tk=2048, name="mm_gw_in")
    token = send_grads((_W_IN,), (gw_in,))
    dx, dg_mix = _matmul(
        dproj, w_in, out_dtype=(F32,), tm=512, tn=1024, tk=5888, name="mm_da", after=token,
        extra=(x, g_mix, dh1), epilogue=_rms_bwd_tile, n_colsum=1)
    return loss, dx, (dg_mix, dg_mlp, dg_ple, dg_final), drpb


def _cast_bf16(t, *, name):
    def body(t_ref, o_ref):
        o_ref[...] = t_ref[...].astype(BF16)

    rows, cols = t.shape
    tr = 256 if rows % 256 == 0 else rows
    blk = pl.BlockSpec((tr, cols), lambda i: (i, 0))
    return pl.pallas_call(body, name=name, grid=(rows // tr,), in_specs=[blk], out_specs=blk,
                          out_shape=_sds(t.shape, BF16), compiler_params=_params("parallel"))(t)


def _adamw(w, g, m, v):
    m = ADAM_B1 * m + (1.0 - ADAM_B1) * g
    v = ADAM_B2 * v + (1.0 - ADAM_B2) * (g * g)
    m_hat = m / (1.0 - ADAM_B1 ** ADAM_STEP)
    v_hat = v / (1.0 - ADAM_B2 ** ADAM_STEP)
    delta = -ADAM_LR * (m_hat / (jnp.sqrt(v_hat) + ADAM_EPS) + ADAM_WD * w)
    return delta, m, v


def _sum_adamw(parts, w, m, v, *, tr, name, own=None, transposed=False):
    rows, cols = w.shape
    n_pre = 0 if own is None else 1

    def body(*refs):
        p_ref, w_ref, m_ref, v_ref = refs[n_pre:n_pre + 4]
        g_ref, d_ref, nm_ref, nv_ref = refs[-4:]
        g = (p_ref[0] if own is None else refs[n_pre + 4][...]).astype(F32)
        for s in range(1, N_DEV):
            g = g + p_ref[s].astype(F32)
        if transposed:
            g = g.T
        g_ref[...] = g
        d_ref[...], nm_ref[...], nv_ref[...] = _adamw(w_ref[...], g, m_ref[...], v_ref[...])

    if transposed:
        blk = pl.BlockSpec((rows, tr), lambda i, *_: (0, i))
        g_rows, steps = rows, cols // tr
    else:
        blk = pl.BlockSpec((tr, cols), lambda i, *_: (i, 0))
        g_rows, steps = cols, rows // tr
    in_specs = [pl.BlockSpec((N_DEV, tr, g_rows), lambda i, *_: (0, i, 0)), blk, blk, blk]
    args = [parts, w, m, v]
    if own is not None:
        in_specs.append(pl.BlockSpec((None, tr, g_rows), lambda i, idx: (idx[0], i, 0)))
        args = [own[1]] + args + [own[0]]
    return pl.pallas_call(
        body, name=name,
        grid_spec=pltpu.PrefetchScalarGridSpec(num_scalar_prefetch=n_pre, grid=(steps,), in_specs=in_specs,
                                               out_specs=[blk] * 4),
        out_shape=[_sds((rows, cols), F32)] * 4,
        compiler_params=_params("parallel"),
    )(*args)


_RPB_SIZE = 8 * 15 * 31


def _pack_small(g_mix, g_mlp, g_ple, g_final, rpb, loss_row):
    flat = jnp.concatenate([g_mix.reshape(-1), g_mlp.reshape(-1), g_ple.reshape(-1), g_final.reshape(-1),
                            rpb.reshape(-1), jnp.zeros((3840 - _RPB_SIZE,), F32), loss_row.reshape(-1),
                            jnp.zeros((128,), F32)])
    return flat.reshape(64, 128)


def _unpack_small(t):
    flat = t.reshape(-1)
    return (flat[0:1024].reshape(1, 1024), flat[4096:4096 + _RPB_SIZE].reshape(1, 8, 15, 31),
            flat[1024:2048].reshape(1, 1024), flat[2048:3072].reshape(1, 1024), flat[3072:4096])


def kernel(x, p, positions, g_mix, w_in, rpb, w_branch_na, w_branch_dil, w_out, g_mlp, w_up, w_down, g_ple, w_ple_gate, w_ple_proj, g_final, loss_target, m_g_mix, m_w_in, m_rpb, m_w_branch_na, m_w_branch_dil, m_w_out, m_g_mlp, m_w_up, m_w_down, m_g_ple, m_w_ple_gate, m_w_ple_proj, m_g_final, v_g_mix, v_w_in, v_rpb, v_w_branch_na, v_w_branch_dil, v_w_out, v_g_mlp, v_w_up, v_w_down, v_g_ple, v_w_ple_gate, v_w_ple_proj, v_g_final):
    sharded = dict(w_in=(w_in, m_w_in, v_w_in), w_branch_na=(w_branch_na, m_w_branch_na, v_w_branch_na),
                   w_branch_dil=(w_branch_dil, m_w_branch_dil, v_w_branch_dil), w_out=(w_out, m_w_out, v_w_out),
                   w_up=(w_up, m_w_up, v_w_up), w_down=(w_down, m_w_down, v_w_down),
                   w_ple_gate=(w_ple_gate, m_w_ple_gate, v_w_ple_gate),
                   w_ple_proj=(w_ple_proj, m_w_ple_proj, v_w_ple_proj))
    shards = {k: tuple(t[0] for t in val) for k, val in sharded.items()}

    me = _my_index()

    shards["w_in"] = tuple(t.T for t in shards["w_in"])

    w_in_b = _cast_bf16(shards["w_in"][0], name="cast_w_in")
    rest_b = [shards[name][0].astype(BF16).T if axis == 1 else shards[name][0].astype(BF16)
              for name, axis, _ in _WEIGHTS[1:]]
    first_in, token_in = _start_copies(_first_leg_copies, [w_in_b], [_sds((N_DEV,) + w_in_b.shape, BF16)], 4,
                                       name="start_gather_w_in")

    def whole(landed, mine):
        return _to_full(lax.dynamic_update_index_in_dim(landed, mine, me, 0))

    rest = {}

    def get_w_in(after):
        (mine,), landed = _wait_copies(_first_leg_copies, first_in, after, name="wait_gather_w_in")
        second, token = _start_copies(_second_leg_copies, [], landed, 3, name="start_forward_w_in")
        _, (landed,) = _wait_copies(_second_leg_copies, second, token, name="wait_forward_w_in")
        rest["first"], token = _start_copies(_first_leg_copies, rest_b,
                                             [_sds((N_DEV,) + t.shape, BF16) for t in rest_b], 4 * len(rest_b),
                                             name="start_gather_rest", after=landed)
        return whole(landed, mine), token

    def relay_rest(after):
        rest["mine"], landed = _wait_copies(_first_leg_copies, rest["first"], after, name="wait_gather_rest")
        rest["second"], token = _start_copies(_second_leg_copies, [], landed, 3 * len(rest_b),
                                              name="start_forward_rest")
        return token

    def get_rest(after):
        _, landed = _wait_copies(_second_leg_copies, rest["second"], after, name="wait_forward_rest")
        return [whole(t, own) for t, own in zip(landed, rest["mine"])]

    sent = []

    def send_grads(indices, grads):
        chunked = [_to_chunks(i, g) for i, g in zip(indices, grads)]
        handle, token = _start_copies(_exchange_copies, chunked, [_sds(t.shape, BF16) for t in chunked],
                                      7 * len(chunked),
                                      name="start_exchange_" + ("w_in" if indices == (_W_IN,) else "rest"))
        sent.append((indices, handle))
        return token

    g_mix_0 = g_mix + token_in[0:1, 0:1]
    loss, dx, dgs, drpb = _local_step(
        x[0], p[0, 0].astype(BF16), positions[0], loss_target[0],
        g_mix_0, g_mlp, g_ple, g_final.reshape(1, -1), rpb[0], get_w_in, relay_rest, get_rest, send_grads)

    drpb3 = drpb.reshape(8, 16, 32)[:, :15, :31]
    small = _pack_small(dgs[0], dgs[1], dgs[2], dgs[3], drpb3, loss)
    share, done = _start_copies(_gather_copies, [small], [_sds((N_DEV,) + small.shape, F32)], 7,
                                name="start_share_small")

    out = {}
    for indices, handle in sent:
        chunked, landed = _wait_copies(_exchange_copies, handle, done,
                                       name="wait_exchange_" + ("w_in" if indices == (_W_IN,) else "rest"))
        for i, part, mine in zip(indices, landed, chunked):
            name = _WEIGHTS[i][0]
            w, m, v = shards[name]
            turned = _WEIGHTS[i][1] == 1 and i != _W_IN
            res = _sum_adamw(part, w, m, v, tr=368 if i == _W_IN else 128, name="adamw_" + name,
                             own=(mine, me.reshape(1).astype(jnp.int32)), transposed=turned)
            out[name] = [(t.T if i == _W_IN else t)[None] for t in res]
            done = res[0]
    (small,), (small_landed,) = _wait_copies(_gather_copies, share, done, name="wait_share_small")
    small_all = lax.dynamic_update_index_in_dim(small_landed, small, me, 0)
    small_w = _pack_small(g_mix, g_mlp, g_ple, g_final, rpb, jnp.zeros((128,), F32))
    small_m = _pack_small(m_g_mix, m_g_mlp, m_g_ple, m_g_final, m_rpb, jnp.zeros((128,), F32))
    small_v = _pack_small(v_g_mix, v_g_mlp, v_g_ple, v_g_final, v_rpb, jnp.zeros((128,), F32))
    res = _sum_adamw(small_all, small_w, small_m, small_v, tr=64, name="adamw_small")
    unpacked = [_unpack_small(t) for t in res]
    for i, name in enumerate(("g_mix", "rpb", "g_mlp", "g_ple", "g_final")):
        out[name] = [u[i] for u in unpacked]
    loss_total = res[0][62, 0]

    order = ("g_mix", "w_in", "rpb", "w_branch_na", "w_branch_dil", "w_out", "g_mlp", "w_up", "w_down",
             "g_ple", "w_ple_gate", "w_ple_proj", "g_final")
    grads = [out[k][0] for k in order]
    deltas = [out[k][1] for k in order]
    new_m = [out[k][2] for k in order]
    new_v = [out[k][3] for k in order]
    return (loss_total, dx[None], *grads, *deltas, *new_m, *new_v)
```

```python
import functools

import jax
import jax.numpy as jnp
from jax import lax
from jax.experimental import pallas as pl
from jax.experimental.pallas import tpu as pltpu

F32 = jnp.float32
BF16 = jnp.bfloat16

D_MODEL = 1024
HEAD_DIM = 64
GRID_W = 64
NA_WIDTH = 512
DIL_WIDTH = 768
IN_WIDTH = 5888
DIL_DILATIONS = (1, 4, 16)
DIL_RADIUS = 64
NA_WIN_ROWS = 8
RMS_EPS = 1e-6
NEG_INF = -1e30
QK_SCALE = HEAD_DIM ** -0.5

ADAM_LR = 0.001
ADAM_B1 = 0.9
ADAM_B2 = 0.999
ADAM_EPS = 1e-08
ADAM_WD = 0.01
ADAM_STEP = 10

N_DEV = 8
VMEM_LIMIT = 56 * 1024 * 1024
EPILOGUE_ROWS = 256
MESH = pl.DeviceIdType.MESH

NT_DIMS = (((1,), (1,)), ((), ()))
TN_DIMS = (((0,), (0,)), ((), ()))


def _sds(shape, dtype):
    return jax.ShapeDtypeStruct(shape, dtype)


def _params(*sem):
    return pltpu.CompilerParams(dimension_semantics=sem, vmem_limit_bytes=VMEM_LIMIT)


def _rows(tm, width, col=0):
    return pl.BlockSpec((tm, width), lambda i, c=col: (i, c))


def _const(shape):
    zeros = (0,) * len(shape)
    return pl.BlockSpec(shape, lambda i: zeros)


def _my_index():
    return 4 * lax.axis_index("x") + 2 * lax.axis_index("y") + lax.axis_index("c")


def _peer(k):
    x, y, c = lax.axis_index("x"), lax.axis_index("y"), lax.axis_index("c")
    px = 1 - x if k & 4 else x
    py = 1 - y if k & 2 else y
    pc = 1 - c if k & 1 else c
    return (px, py, pc), 4 * px + 2 * py + pc


def _call(body, *, name, grid, in_specs, out_specs, out_shape, scratch_shapes, args, after=None):
    n_in, n_out = len(in_specs), len(out_specs)
    extra = [] if after is None else [after]
    n_x = n_in + len(extra)

    def plain(*refs):
        body(refs[:n_in], refs[n_x:n_x + n_out], refs[n_x + n_out:])

    res = pl.pallas_call(plain, name=name, grid=grid,
                         in_specs=list(in_specs) + [pl.BlockSpec(memory_space=pl.ANY)] * len(extra),
                         out_specs=out_specs, out_shape=out_shape, scratch_shapes=scratch_shapes,
                         compiler_params=_params(*(("arbitrary",) * len(grid))))(*args, *extra)
    return list(res)


_HBM_SPEC = pl.BlockSpec(memory_space=pltpu.HBM)
_SEM_SPEC = pl.BlockSpec(memory_space=pltpu.SEMAPHORE)
_SIDE_EFFECT = pltpu.SideEffectType.DATAFLOW_SIDE_EFFECTING


_FIRST_LEG = (1, 2, 4, 6)
_SECOND_LEG = (2, 4, 6)


def _gather_copies(srcs, lands, send, recv, sending):
    me = _my_index()
    out = []
    for w in range(len(srcs)):
        for k in range(1, N_DEV):
            dev, idx = _peer(k)
            out.append(pltpu.make_async_remote_copy(
                src_ref=srcs[w], dst_ref=lands[w].at[me if sending else idx],
                send_sem=send.at[w * 7 + k - 1], recv_sem=recv.at[w * 7 + k - 1],
                device_id=dev, device_id_type=MESH))
    return out


def _first_leg_copies(srcs, lands, send, recv, sending):
    me = _my_index()
    out = []
    for w in range(len(srcs)):
        for j, k in enumerate(_FIRST_LEG):
            dev, idx = _peer(k)
            out.append(pltpu.make_async_remote_copy(
                src_ref=srcs[w], dst_ref=lands[w].at[me if sending else idx],
                send_sem=send.at[w * 4 + j], recv_sem=recv.at[w * 4 + j],
                device_id=dev, device_id_type=MESH))
    return out


def _second_leg_copies(srcs, lands, send, recv, sending, first=0):
    sibling, _ = _peer(1)
    out = []
    for w in range(len(lands)):
        for j, k in enumerate(_SECOND_LEG):
            slot = _peer(k if sending else k ^ 1)[1]
            sem = (first + w) * 3 + j
            out.append(pltpu.make_async_remote_copy(
                src_ref=lands[w].at[slot], dst_ref=lands[w].at[slot],
                send_sem=send.at[sem], recv_sem=recv.at[sem],
                device_id=sibling, device_id_type=MESH))
    return out


def _exchange_copies(srcs, lands, send, recv, sending):
    out = []
    for w in range(len(srcs)):
        for k in range(1, N_DEV):
            dev, idx = _peer(k)
            out.append(pltpu.make_async_remote_copy(
                src_ref=srcs[w].at[idx], dst_ref=lands[w].at[k],
                send_sem=send.at[w * 7 + k - 1], recv_sem=recv.at[w * 7 + k - 1],
                device_id=dev, device_id_type=MESH))
    return out


def _start_copies(make, srcs, lands, n_copies, *, name, after=None):
    n_src, n_buf = len(srcs), len(srcs) + len(lands)
    extra = [] if after is None else [after]

    def body(*refs):
        send, recv = refs[n_buf + len(extra)], refs[n_buf + len(extra) + 1]
        for cp in make(refs[:n_src], refs[n_src:n_buf], send, recv, True):
            cp.start()
        refs[-1][...] = jnp.zeros_like(refs[-1])

    bufs = list(srcs) + [lax.empty(t.shape, t.dtype) if isinstance(t, jax.ShapeDtypeStruct) else t for t in lands]
    res = pl.pallas_call(
        body, name=name,
        out_shape=(pltpu.SemaphoreType.DMA((n_copies,)), pltpu.SemaphoreType.DMA((n_copies,)),
                   *[pltpu.HBM(t.shape, t.dtype) for t in bufs], _sds((8, 128), F32)),
        in_specs=[_HBM_SPEC] * n_buf + [pl.BlockSpec(memory_space=pl.ANY)] * len(extra),
        out_specs=(_SEM_SPEC, _SEM_SPEC, *([_HBM_SPEC] * n_buf), pl.BlockSpec(memory_space=pltpu.VMEM)),
        input_output_aliases={i: 2 + i for i in range(n_buf)},
        compiler_params=pltpu.CompilerParams(has_side_effects=_SIDE_EFFECT),
    )(*[pltpu.with_memory_space_constraint(t, pltpu.HBM) for t in bufs], *extra)
    return (n_src, res[0], res[1], res[2:2 + n_buf]), res[-1]


def _wait_copies(make, handle, after, *, name):
    n_src, send_sems, recv_sems, bufs = handle
    n_buf = len(bufs)

    def body(*refs):
        for cp in make(refs[:n_src], refs[n_src:n_buf], refs[n_buf], refs[n_buf + 1], False):
            cp.wait_send()
            cp.wait_recv()

    res = pl.pallas_call(
        body, name=name,
        out_shape=tuple(pltpu.HBM(t.shape, t.dtype) for t in bufs),
        in_specs=[_HBM_SPEC] * n_buf + [_SEM_SPEC, _SEM_SPEC, pl.BlockSpec(memory_space=pl.ANY)],
        out_specs=tuple([_HBM_SPEC] * n_buf),
        input_output_aliases={i: i for i in range(n_buf)},
        compiler_params=pltpu.CompilerParams(has_side_effects=_SIDE_EFFECT),
    )(*bufs, send_sems, recv_sems, after)
    return list(res[:n_src]), list(res[n_src:])


def _matmul(a, b, *, ta=False, tb=False, out_dtype, tm, tn, tk, name, after=None, extra=(), epilogue=None,
            n_colsum=0, transpose_out=False):
    m, k = (a.shape[1], a.shape[0]) if ta else a.shape
    n = b.shape[0] if tb else b.shape[1]
    tm, tn, tk = min(tm, m), min(tn, n), min(tk, k)
    nk = k // tk
    dims = (((0 if ta else 1,), (1 if tb else 0,)), ((), ()))
    out_dtypes = out_dtype if isinstance(out_dtype, tuple) else (out_dtype,)
    n_tiles = len(out_dtypes)

    def add_colsums(o_refs, sums):
        i = pl.program_id(1)
        for s_ref, val in zip(o_refs[n_tiles:], sums):
            @pl.when(i == 0)
            def _(s_ref=s_ref, val=val):
                s_ref[...] = val

            @pl.when(i > 0)
            def _(s_ref=s_ref, val=val):
                s_ref[...] += val

    def finish(acc, x_refs, o_refs):
        vals = (acc,) if epilogue is None else epilogue(acc, *[r[...] for r in x_refs])
        for o_ref, val in zip(o_refs[:n_tiles], vals[:n_tiles]):
            o_ref[...] = (val.T if transpose_out else val).astype(o_ref.dtype)
        add_colsums(o_refs, vals[n_tiles:])

    chunk = EPILOGUE_ROWS if (nk == 1 and epilogue is not None and not ta and tm % EPILOGUE_ROWS == 0) else None

    def body(ins, outs, acc):
        a_ref, b_ref = ins[:2]
        if chunk is not None:
            sums = None
            for r0 in range(0, tm, chunk):
                part = lax.dot_general(a_ref[r0:r0 + chunk, :], b_ref[...], dims, preferred_element_type=F32)
                vals = epilogue(part, *[r[...] if r.shape[0] == 1 else r[r0:r0 + chunk, :] for r in ins[2:]])
                for o_ref, val in zip(outs[:n_tiles], vals[:n_tiles]):
                    o_ref[r0:r0 + chunk, :] = val.astype(o_ref.dtype)
                sums = vals[n_tiles:] if sums is None else [s + v for s, v in zip(sums, vals[n_tiles:])]
            add_colsums(outs, sums)
            return
        part = lax.dot_general(a_ref[...], b_ref[...], dims, preferred_element_type=F32)
        if nk == 1:
            finish(part, ins[2:], outs)
            return
        acc_ref, = acc
        kk = pl.program_id(2)

        @pl.when(kk == 0)
        def _():
            acc_ref[...] = part

        @pl.when(kk > 0)
        def _():
            acc_ref[...] += part

        @pl.when(kk == nk - 1)
        def _():
            finish(acc_ref[...], ins[2:], outs)

    a_spec = (pl.BlockSpec((tk, tm), lambda j, i, kk: (kk, i)) if ta
              else pl.BlockSpec((tm, tk), lambda j, i, kk: (i, kk)))
    b_spec = (pl.BlockSpec((tn, tk), lambda j, i, kk: (j, kk)) if tb
              else pl.BlockSpec((tk, tn), lambda j, i, kk: (kk, j)))
    tile = pl.BlockSpec((tm, tn), lambda j, i, kk: (i, j))
    row = pl.BlockSpec((1, tn), lambda j, i, kk: (0, j))

    def x_spec(t):
        if isinstance(t, tuple):
            return pl.BlockSpec((tm, tn), lambda j, i, kk, first=t[1] * (n // tn): (i, first + j))
        return row if t.shape[0] == 1 else tile

    out_tile, out_dims = (pl.BlockSpec((tn, tm), lambda j, i, kk: (j, i)), (n, m)) if transpose_out else (tile, (m, n))
    res = _call(
        body, name=name, grid=(n // tn, m // tm, nk),
        in_specs=[a_spec, b_spec] + [x_spec(t) for t in extra],
        out_specs=[out_tile] * n_tiles + [row] * n_colsum,
        out_shape=[_sds(out_dims, dt) for dt in out_dtypes] + [_sds((1, n), F32)] * n_colsum,
        scratch_shapes=[] if nk == 1 else [pltpu.VMEM((tm, tn), F32)],
        args=(a, b, *[t[0] if isinstance(t, tuple) else t for t in extra]), after=after)
    return res if isinstance(out_dtype, tuple) or n_colsum else res[0]


def _rstd(h):
    return lax.rsqrt(jnp.mean(h * h, axis=-1, keepdims=True) + RMS_EPS)


def _sigmoid(z):
    return 1.0 / (1.0 + jnp.exp(-z))


def _rms_fwd(x, g, *, tm, name):
    n = x.shape[0]

    def body(x_ref, g_ref, o_ref):
        h = x_ref[...]
        o_ref[...] = (h * _rstd(h) * g_ref[...]).astype(BF16)

    return pl.pallas_call(
        body, name=name, grid=(n // tm,),
        in_specs=[_rows(tm, D_MODEL), _const((1, D_MODEL))],
        out_specs=_rows(tm, D_MODEL), out_shape=_sds((n, D_MODEL), BF16),
        compiler_params=_params("parallel"),
    )(x, g)


def _swap_halves(t):
    lane = lax.broadcasted_iota(jnp.int32, (t.shape[0], 128), 1)
    pieces = [t[:, c:c + 128] for c in range(0, t.shape[1], 128)]
    return jnp.concatenate([jnp.where((lane & 63) < 32, pltpu.roll(h, 96, 1), pltpu.roll(h, 32, 1))
                            for h in pieces], axis=1)


def _dil_spec(dil, tm):
    return pl.BlockSpec((dil, tm // dil, 256), lambda i: (0, i, 0))


def _dil_scratch(tm):
    return pltpu.VMEM((2, tm, 128), F32)


def _load_token_order(src, scr, dil, rows, row0=0):
    if dil == 1:
        return src[0, row0:row0 + rows, :]
    for j in range(dil):
        for c in range(2):
            scr[c, pl.ds(j, rows // dil, stride=dil), :] = (
                src[j, row0 // dil:(row0 + rows) // dil, c * 128:(c + 1) * 128])
    return jnp.concatenate([scr[0, 0:rows, :], scr[1, 0:rows, :]], axis=1)


def _store_dil_order(val, dst, scr, dil, row0=0):
    rows = val.shape[0]
    if dil == 1:
        dst[0, row0:row0 + rows, :] = val.astype(dst.dtype)
        return
    for c in range(2):
        scr[c] = val[:, c * 128:(c + 1) * 128]
    for j in range(dil):
        for c in range(2):
            dst[j, row0 // dil:(row0 + rows) // dil, c * 128:(c + 1) * 128] = (
                scr[c, pl.ds(j, rows // dil, stride=dil), :].astype(dst.dtype))


def _project_in(a, w_t, cos_t, sin_t, *, tm, name, after=None):
    n = a.shape[0]
    n_dil = len(DIL_DILATIONS)
    na_w, dil_w = 3 * NA_WIDTH, 3 * DIL_WIDTH
    chunk = min(EPILOGUE_ROWS, tm)
    extra = [] if after is None else [after]

    def body(a_ref, w_ref, cos_ref, sin_ref, *rest):
        na_ref, gate_ref = rest[len(extra):len(extra) + 2]
        outs, scr = rest[len(extra) + 2:len(extra) + 2 + 3 * n_dil], rest[-1]

        def part(r0, first, width):
            return lax.dot_general(a_ref[r0:r0 + chunk, :], w_ref[first:first + width, :], NT_DIMS,
                                   preferred_element_type=F32)

        for r0 in range(0, tm, chunk):
            na_ref[r0:r0 + chunk, :] = part(r0, 0, na_w).astype(BF16)
            dil_part = part(r0, na_w, dil_w)
            cosv, sinv = cos_ref[r0:r0 + chunk, :], sin_ref[r0:r0 + chunk, :]
            for t in range(3):
                for gi, dil in enumerate(DIL_DILATIONS):
                    c0 = (t * n_dil + gi) * 256
                    val = dil_part[:, c0:c0 + 256]
                    if t < 2:
                        val = val * cosv + _swap_halves(val) * sinv
                    _store_dil_order(val, outs[t * n_dil + gi], scr, dil, r0)
            gate_ref[r0:r0 + chunk, :] = _sigmoid(part(r0, na_w + dil_w, 2 * D_MODEL)).astype(BF16)

    out_specs = [_rows(tm, na_w), _rows(tm, 2 * D_MODEL)]
    out_shape = [_sds((n, na_w), BF16), _sds((n, 2 * D_MODEL), BF16)]
    for _ in range(3):
        for dil in DIL_DILATIONS:
            out_specs.append(pl.BlockSpec((dil, tm // dil, 256), lambda i: (0, i, 0)))
            out_shape.append(_sds((dil, n // dil, 256), BF16))
    res = pl.pallas_call(
        body, name=name, grid=(n // tm,),
        in_specs=[_rows(tm, D_MODEL), _const(w_t.shape), _rows(tm, 256), _rows(tm, 256)]
                 + [pl.BlockSpec(memory_space=pl.ANY)] * len(extra),
        out_specs=out_specs, out_shape=out_shape,
        scratch_shapes=[pltpu.VMEM((2, chunk, 128), F32)],
        compiler_params=_params("parallel"),
    )(a, w_t, cos_t, sin_t, *extra)
    return res[0], res[1], res[2:5], res[5:8], res[8:11]


def _residual_rms_tile(delta, h, g):
    hn = h + delta
    return hn, hn * _rstd(hn) * g


def _gate_mix_tile(b2, s1, b1, s2):
    return b2, s1.astype(F32) * b1.astype(F32) + s2.astype(F32) * b2


def _gate_bwd_tile(dm, s1, b1, s2, b2):
    s1, b1, s2, b2 = (t.astype(F32) for t in (s1, b1, s2, b2))
    return dm * s1, dm * s2, dm * b1 * s1 * (1.0 - s1), dm * b2 * s2 * (1.0 - s2)


def _tail_tile(gt, pp, h2, target, g):
    sg = _sigmoid(gt)
    h3 = h2 + sg * pp
    r3 = _rstd(h3)
    n3 = h3 * r3
    err = n3 * g - target
    loss = 0.5 * jnp.sum(jnp.sum(err * err, axis=-1, keepdims=True) / D_MODEL)
    dy = err / D_MODEL
    dn = dy * g
    dh3 = r3 * (dn - n3 * jnp.mean(dn * n3, axis=-1, keepdims=True))
    return (dh3, dh3 * sg, dh3 * pp * sg * (1.0 - sg),
            jnp.sum(dy * n3, axis=0, keepdims=True), jnp.full((1, gt.shape[1]), loss, F32))


def _rms_bwd_tile(dz, h, g, dres):
    r = _rstd(h)
    nrm = h * r
    dn = dz * g
    dh = dres + r * (dn - nrm * jnp.mean(dn * nrm, axis=-1, keepdims=True))
    return dh, jnp.sum(dz * nrm, axis=0, keepdims=True)


def _rms_bwd_twice(dz, h, g, dres):
    dh, dg = _rms_bwd_tile(dz, h, g, dres)
    return dh, dh, dg


def _assemble_dproj(dna, ddil_q, ddil_k, ddil_v, dgn, dgd, cos_t, sin_t, *, tm, name):
    n = dgn.shape[0]

    def body(*refs):
        dq_ref, dk_ref, dv_ref = refs[0:3]
        dil_in = refs[3:12]
        dgn_ref, dgd_ref, cos_ref, sin_ref, o_ref, scr = refs[12:18]
        o_ref[:, 0:512] = dq_ref[...]
        o_ref[:, 512:1024] = dk_ref[...].astype(BF16)
        o_ref[:, 1024:1536] = dv_ref[...].astype(BF16)
        cosv, sinv = cos_ref[...], sin_ref[...]
        for t in range(3):
            for gi, dil in enumerate(DIL_DILATIONS):
                val = _load_token_order(dil_in[t * 3 + gi], scr, dil, tm)
                if t < 2:
                    val = val * cosv + _swap_halves(val * sinv)
                c0 = 1536 + t * DIL_WIDTH + gi * 256
                o_ref[:, c0:c0 + 256] = val.astype(BF16)
        o_ref[:, 3840:4864] = dgn_ref[...]
        o_ref[:, 4864:5888] = dgd_ref[...]

    in_specs = [_rows(tm, NA_WIDTH)] * 3
    for _ in range(3):
        for dil in DIL_DILATIONS:
            in_specs.append(pl.BlockSpec((dil, tm // dil, 256), lambda i: (0, i, 0)))
    in_specs += [_rows(tm, D_MODEL)] * 2 + [_rows(tm, 256)] * 2
    return pl.pallas_call(
        body, name=name, grid=(n // tm,), in_specs=in_specs,
        out_specs=_rows(tm, IN_WIDTH), out_shape=_sds((n, IN_WIDTH), BF16),
        scratch_shapes=[_dil_scratch(tm)],
        compiler_params=_params("parallel"),
    )(*dna, *ddil_q, *ddil_k, *ddil_v, dgn, dgd, cos_t, sin_t)


N_ROW_OFF = 2 * NA_WIN_ROWS - 1
N_PAIRS = N_ROW_OFF - 1
RB_WIDTH = (N_ROW_OFF + 1) * GRID_W


def _na_bias(rb_ref, pair_scr):
    shape = (GRID_W, RB_WIDTH)
    qc = lax.broadcasted_iota(jnp.int32, shape, 0)
    qc2 = lax.broadcasted_iota(jnp.int32, (GRID_W, 128), 0)
    kc2 = lax.broadcasted_iota(jnp.int32, (GRID_W, 128), 1) & (GRID_W - 1)
    cs = jnp.clip(qc2 - 8, 0, GRID_W - 16)
    valid = (kc2 >= cs) & (kc2 < cs + 16)
    for hh in range(2):
        t = jnp.broadcast_to(rb_ref[hh], shape)
        t = pltpu.roll(t, RB_WIDTH - 15, 1)
        for b in range(6):
            t = jnp.where(((qc >> b) & 1) == 1, pltpu.roll(t, 1 << b, 1), t)
        t_odd = pltpu.roll(t, RB_WIDTH - GRID_W, 1)
        for ro in range(N_PAIRS):
            src = t if ro % 2 == 0 else t_odd
            base = (ro // 2) * 128
            pair_scr[hh, ro] = jnp.where(valid, src[:, base:base + 128], NEG_INF)


NA_GROUP_FWD = 8
NA_GROUP_BWD = 4


def _stack_heads(ref, r, scale=1.0):
    lane = lax.broadcasted_iota(jnp.int32, (GRID_W, 128), 1)
    t = ref[pl.ds(pl.multiple_of(r * GRID_W, GRID_W), GRID_W), :].astype(F32) * scale
    return jnp.concatenate([jnp.where(lane < 64, t, 0.0), jnp.where(lane >= 64, t, 0.0)], axis=0).astype(BF16)


def _unstack_heads(t2):
    lane = lax.broadcasted_iota(jnp.int32, (GRID_W, 128), 1)
    return jnp.where(lane < 64, t2[:GRID_W], t2[GRID_W:])


def _na_window(k_ref, v_ref, r, n_rows):
    rs = jnp.clip(r - NA_WIN_ROWS // 2, 0, n_rows - NA_WIN_ROWS)
    ro0 = (NA_WIN_ROWS - 1) - (r - rs)
    off = pl.multiple_of(rs * GRID_W, GRID_W)
    kw = k_ref[pl.ds(off, NA_WIN_ROWS * GRID_W), :]
    vw = v_ref[pl.ds(off, NA_WIN_ROWS * GRID_W), :]
    return kw, vw, off, ro0


def _na_probs(s_raw, pair_scr, ro0):
    bias = [jnp.concatenate([pair_scr[hh, ro0 + 2 * j] for j in range(NA_WIN_ROWS // 2)], axis=1)
            for hh in range(2)]
    s = s_raw + jnp.concatenate(bias, axis=0)
    m = jnp.max(s, axis=-1, keepdims=True)
    e = jnp.exp(s - m)
    return e * (1.0 / jnp.sum(e, axis=-1, keepdims=True))


def _na_qkv_specs(n):
    pairs = NA_WIDTH // 128
    return [pl.BlockSpec((n, 128), lambda h, first=t * pairs: (0, first + h)) for t in range(3)]


def _na_fwd(qkv, rb, *, name):
    n = qkv.shape[0]
    n_rows = n // GRID_W

    def body(ins, outs, scr):
        q_ref, k_ref, v_ref, rb_ref = ins
        o_ref, = outs
        pair_scr, = scr
        _na_bias(rb_ref, pair_scr)

        def group(g, carry):
            rows = [g * NA_GROUP_FWD + t for t in range(NA_GROUP_FWD)]
            wins = [_na_window(k_ref, v_ref, r, n_rows) for r in rows]
            raw = [lax.dot_general(_stack_heads(q_ref, r, QK_SCALE), w[0], NT_DIMS, preferred_element_type=F32)
                   for r, w in zip(rows, wins)]
            probs = [_na_probs(s, pair_scr, w[3]) for s, w in zip(raw, wins)]
            outs2 = [jnp.dot(p.astype(BF16), w[1], preferred_element_type=F32) for p, w in zip(probs, wins)]
            for r, o2 in zip(rows, outs2):
                o_ref[pl.ds(pl.multiple_of(r * GRID_W, GRID_W), GRID_W), :] = _unstack_heads(o2).astype(BF16)
            return carry

        lax.fori_loop(0, n_rows // NA_GROUP_FWD, group, 0)

    col = pl.BlockSpec((n, 128), lambda h: (0, h))
    return _call(
        body, name=name, grid=(NA_WIDTH // 128,),
        in_specs=_na_qkv_specs(n) + [pl.BlockSpec((2, 1, RB_WIDTH), lambda h: (h, 0, 0))],
        out_specs=[col], out_shape=[_sds((n, NA_WIDTH), BF16)],
        scratch_shapes=[pltpu.VMEM((2, N_PAIRS, GRID_W, 128), F32)],
        args=(qkv, qkv, qkv, rb))[0]


def _na_bwd(qkv, do, rb, *, name):
    n = qkv.shape[0]
    n_rows = n // GRID_W
    win = NA_WIN_ROWS * GRID_W

    def body(ins, outs, scr):
        q_ref, k_ref, v_ref, do_ref, rb_ref = ins
        dq_ref, dk_ref, dv_ref, drb_ref = outs
        pair_scr, acc_scr = scr
        _na_bias(rb_ref, pair_scr)
        acc_scr[...] = jnp.zeros_like(acc_scr)
        dk_ref[...] = jnp.zeros_like(dk_ref)
        dv_ref[...] = jnp.zeros_like(dv_ref)

        def group(g, carry):
            rows = [g * NA_GROUP_BWD + t for t in range(NA_GROUP_BWD)]
            wins = [_na_window(k_ref, v_ref, r, n_rows) for r in rows]
            qss = [_stack_heads(q_ref, r, QK_SCALE) for r in rows]
            doss = [_stack_heads(do_ref, r) for r in rows]
            raw = [lax.dot_general(qs, w[0], NT_DIMS, preferred_element_type=F32) for qs, w in zip(qss, wins)]
            dps = [lax.dot_general(dos, w[1], NT_DIMS, preferred_element_type=F32) for dos, w in zip(doss, wins)]
            probs = [_na_probs(s, pair_scr, w[3]) for s, w in zip(raw, wins)]
            dss = [p * (dp - jnp.sum(p * dp, axis=-1, keepdims=True)) for p, dp in zip(probs, dps)]
            dsbs = [ds.astype(BF16) for ds in dss]
            dq2s = [jnp.dot(dsb, w[0], preferred_element_type=F32) for dsb, w in zip(dsbs, wins)]
            dkws = [lax.dot_general(dsb, qs, TN_DIMS, preferred_element_type=F32) for dsb, qs in zip(dsbs, qss)]
            dvws = [lax.dot_general(p.astype(BF16), dos, TN_DIMS, preferred_element_type=F32)
                    for p, dos in zip(probs, doss)]
            for t, r in enumerate(rows):
                _, _, off, ro0 = wins[t]
                for hh in range(2):
                    for j in range(NA_WIN_ROWS // 2):
                        acc_scr[hh, ro0 + 2 * j] += dss[t][hh * GRID_W:(hh + 1) * GRID_W, j * 128:(j + 1) * 128]
                dq_ref[pl.ds(pl.multiple_of(r * GRID_W, GRID_W), GRID_W), :] = (
                    _unstack_heads(dq2s[t]) * QK_SCALE).astype(BF16)
                dk_ref[pl.ds(off, win), :] += dkws[t]
                dv_ref[pl.ds(off, win), :] += dvws[t]
            return carry

        lax.fori_loop(0, n_rows // NA_GROUP_BWD, group, 0)

        qc = lax.broadcasted_iota(jnp.int32, (N_PAIRS * GRID_W, 128), 0)
        for hh in range(2):
            t = acc_scr[hh].reshape(N_PAIRS * GRID_W, 128)
            for b in range(6):
                t = jnp.where(((qc >> b) & 1) == 1, pltpu.roll(t, 128 - (1 << b), 1), t)
            t = pltpu.roll(t, 15, 1)
            drb_ref[hh] = jnp.sum(t.reshape(N_PAIRS, GRID_W, 128), axis=1)

    col = pl.BlockSpec((n, 128), lambda h: (0, h))
    return _call(
        body, name=name, grid=(NA_WIDTH // 128,),
        in_specs=_na_qkv_specs(n) + [col, pl.BlockSpec((2, 1, RB_WIDTH), lambda h: (h, 0, 0))],
        out_specs=[col, col, col, pl.BlockSpec((2, N_PAIRS, 128), lambda h: (h, 0, 0))],
        out_shape=[_sds((n, NA_WIDTH), BF16), _sds((n, NA_WIDTH), F32), _sds((n, NA_WIDTH), F32),
                   _sds((8, N_PAIRS, 128), F32)],
        scratch_shapes=[pltpu.VMEM((2, N_PAIRS, GRID_W, 128), F32),
                        pltpu.VMEM((2, N_PAIRS, GRID_W, 128), F32)],
        args=(qkv, qkv, qkv, do, rb))


def _rpb_table(rpb2):
    t = jnp.pad(rpb2, ((0, 0), (0, 1), (0, GRID_W - rpb2.shape[-1])))
    return t.reshape(8, 1, RB_WIDTH)


def _rpb_grad(drb, *, name):
    kdim = drb.shape[1]

    def body(x_ref, o_ref):
        kk = lax.broadcasted_iota(jnp.int32, (128, 512), 0)
        jj = lax.broadcasted_iota(jnp.int32, (128, 512), 1)
        half, co = kk >> 6, kk & 63
        acc = jnp.zeros((8, 512), F32)
        for ro in range(N_PAIRS):
            hit = ((ro + half) == (jj >> 5)) & (co == (jj & 31)) & (co < 31)
            onehot = jnp.where(hit, 1.0, 0.0).astype(F32)
            acc = acc + jnp.dot(x_ref[:, ro * 128:(ro + 1) * 128], onehot, preferred_element_type=F32,
                                precision=lax.Precision.HIGHEST)
        o_ref[...] = acc

    return pl.pallas_call(
        body, name=name, grid=(1,),
        in_specs=[_const((8, kdim))], out_specs=_const((8, 512)), out_shape=_sds((8, 512), F32),
        compiler_params=_params("arbitrary"),
    )(drb)


DIL_GROUP = 2


def _dil_blocks(length):
    qb = min(128, length)
    return qb, min(qb + 2 * DIL_RADIUS, length), min(DIL_GROUP, length // qb)


def _stack_lanes(ref, t, qb, scale=1.0):
    lane = lax.broadcasted_iota(jnp.int32, (qb, 256), 1)
    val = ref[0, t * qb:(t + 1) * qb, :].astype(F32) * scale
    return jnp.concatenate([jnp.where((lane >> 6) == h, val, 0.0) for h in range(4)], axis=0).astype(BF16)


def _dil_window(k_ref, v_ref, blk, qb, win, length):
    start = pl.multiple_of(jnp.clip(blk * qb - DIL_RADIUS, 0, length - win), DIL_RADIUS)
    return k_ref[0, pl.ds(start, win), :], v_ref[0, pl.ds(start, win), :], start


def _dil_caps_init(caps_scr, qb, win):
    @pl.when((pl.program_id(0) == 0) & (pl.program_id(1) == 0))
    def _():
        gap = ((lax.broadcasted_iota(jnp.int32, (4 * qb, win), 0) & (qb - 1))
               - lax.broadcasted_iota(jnp.int32, (4 * qb, win), 1))
        for v in range(3):
            caps_scr[v] = jnp.where(jnp.abs(gap + v * DIL_RADIUS) <= DIL_RADIUS, jnp.inf, NEG_INF)


def _dil_mask(s, blk, start, qb, caps_scr):
    return jnp.minimum(s, caps_scr[(blk * qb - start) // DIL_RADIUS])


def _pick_heads(stacked, qb):
    lane = lax.broadcasted_iota(jnp.int32, (qb, 256), 1)
    out = jnp.zeros((qb, 256), stacked.dtype)
    for h in range(4):
        out = jnp.where((lane >> 6) == h, stacked[h * qb:(h + 1) * qb], out)
    return out


def _stack_head_cols(ref, t, qb):
    return jnp.concatenate([ref[0, t * qb:(t + 1) * qb, 64 * h:64 * h + 1] for h in range(4)], axis=0)


def _dil_fwd(q, k, v, *, name, after=None):
    dil, length, _ = q.shape
    qb, win, grp = _dil_blocks(length)
    extra = [] if after is None else [after]

    def body(q_ref, k_ref, v_ref, *rest):
        o_ref, lse_ref, caps_scr = rest[-3:]
        _dil_caps_init(caps_scr, qb, win)
        blks = [pl.program_id(1) * grp + t for t in range(grp)]
        wins = [_dil_window(k_ref, v_ref, b, qb, win, length) for b in blks]
        raw = [lax.dot_general(_stack_lanes(q_ref, t, qb, QK_SCALE), w[0], NT_DIMS, preferred_element_type=F32)
               for t, w in enumerate(wins)]
        lses, outs = [], []
        for t, (s, w) in enumerate(zip(raw, wins)):
            s = _dil_mask(s, blks[t], w[2], qb, caps_scr)
            m = jnp.max(s, axis=-1, keepdims=True)
            e = jnp.exp(s - m)
            norm = jnp.sum(e, axis=-1, keepdims=True)
            lses.append(m + jnp.log(norm))
            outs.append(jnp.dot((e * (1.0 / norm)).astype(BF16), w[1], preferred_element_type=F32))
        for t in range(grp):
            o_ref[0, t * qb:(t + 1) * qb, :] = _pick_heads(outs[t], qb)
            lse_ref[0, t * qb:(t + 1) * qb, :] = _pick_heads(jnp.broadcast_to(lses[t], (4 * qb, 256)), qb)

    seq = pl.BlockSpec((1, length, 256), lambda j, i: (j, 0, 0))
    blk = pl.BlockSpec((1, grp * qb, 256), lambda j, i: (j, i, 0))
    return pl.pallas_call(
        body, name=name, grid=(dil, length // (grp * qb)),
        in_specs=[blk, seq, seq] + [pl.BlockSpec(memory_space=pl.ANY)] * len(extra), out_specs=[blk, blk],
        out_shape=[_sds((dil, length, 256), F32)] * 2,
        scratch_shapes=[pltpu.VMEM((3, 4 * qb, win), F32)],
        compiler_params=_params("arbitrary", "arbitrary"),
    )(q, k, v, *extra)


def _dil_bwd(q, k, v, do, lse, cc, *, name):
    dil, length, _ = q.shape
    qb, win, grp = _dil_blocks(length)

    def body(q_ref, k_ref, v_ref, do_ref, lse_ref, cc_ref, dq_ref, dk_ref, dv_ref, caps_scr):
        _dil_caps_init(caps_scr, qb, win)

        @pl.when(pl.program_id(1) == 0)
        def _():
            dk_ref[...] = jnp.zeros_like(dk_ref)
            dv_ref[...] = jnp.zeros_like(dv_ref)

        blks = [pl.program_id(1) * grp + t for t in range(grp)]
        wins = [_dil_window(k_ref, v_ref, b, qb, win, length) for b in blks]
        qss = [_stack_lanes(q_ref, t, qb, QK_SCALE) for t in range(grp)]
        doss = [_stack_lanes(do_ref, t, qb) for t in range(grp)]
        raw = [lax.dot_general(qs, w[0], NT_DIMS, preferred_element_type=F32) for qs, w in zip(qss, wins)]
        dps = [lax.dot_general(dos, w[1], NT_DIMS, preferred_element_type=F32) for dos, w in zip(doss, wins)]
        probs = [jnp.exp(_dil_mask(s, blks[t], wins[t][2], qb, caps_scr) - _stack_head_cols(lse_ref, t, qb))
                 for t, s in enumerate(raw)]
        dsbs = [(p * (dp + _stack_head_cols(cc_ref, t, qb))).astype(BF16)
                for t, (p, dp) in enumerate(zip(probs, dps))]
        dq4s = [jnp.dot(dsb, w[0], preferred_element_type=F32) for dsb, w in zip(dsbs, wins)]
        dkws = [lax.dot_general(dsb, qs, TN_DIMS, preferred_element_type=F32) for dsb, qs in zip(dsbs, qss)]
        dvws = [lax.dot_general(p.astype(BF16), dos, TN_DIMS, preferred_element_type=F32)
                for p, dos in zip(probs, doss)]
        for t in range(grp):
            dq_ref[0, t * qb:(t + 1) * qb, :] = _pick_heads(dq4s[t], qb) * QK_SCALE
            dk_ref[0, pl.ds(wins[t][2], win), :] += dkws[t]
            dv_ref[0, pl.ds(wins[t][2], win), :] += dvws[t]

    seq = pl.BlockSpec((1, length, 256), lambda j, i: (j, 0, 0))
    blk = pl.BlockSpec((1, grp * qb, 256), lambda j, i: (j, i, 0))
    return pl.pallas_call(
        body, name=name, grid=(dil, length // (grp * qb)),
        in_specs=[blk, seq, seq, blk, blk, blk], out_specs=[blk, seq, seq],
        out_shape=[_sds((dil, length, 256), F32)] * 3,
        scratch_shapes=[pltpu.VMEM((3, 4 * qb, win), F32)],
        compiler_params=_params("arbitrary", "arbitrary"),
    )(q, k, v, do, lse, cc)


def _merge_weights(lses):
    m = jnp.maximum(jnp.maximum(lses[0], lses[1]), lses[2])
    es = [jnp.exp(t - m) for t in lses]
    inv = 1.0 / (es[0] + es[1] + es[2])
    return [e * inv for e in es]


def _branch_mix(y_na, w_bna, outs, lses, w_bd, gates, *, tm, name):
    n = y_na.shape[0]
    chunk = min(EPILOGUE_ROWS, tm)

    def body(yna_ref, wn_ref, *rest):
        o_in, l_in = rest[0:3], rest[3:6]
        wd_ref, sn_ref, sd_ref = rest[6:9]
        y_ref, yb_ref, bn_ref, bd_ref, mix_ref, scr = rest[9:15]
        for r0 in range(0, tm, chunk):
            rows = slice(r0, r0 + chunk)
            lv = [_load_token_order(l_in[g], scr, d, chunk, r0) for g, d in enumerate(DIL_DILATIONS)]
            ws = _merge_weights(lv)
            y = jnp.zeros((chunk, 256), F32)
            for g, d in enumerate(DIL_DILATIONS):
                y = y + ws[g] * _load_token_order(o_in[g], scr, d, chunk, r0)
            yb = y.astype(BF16)
            y_ref[rows, :] = y
            yb_ref[rows, :] = yb
            bn = lax.dot_general(yna_ref[rows, :], wn_ref[...], NT_DIMS, preferred_element_type=F32).astype(BF16)
            bd = lax.dot_general(yb, wd_ref[...], NT_DIMS, preferred_element_type=F32)
            bn_ref[rows, :] = bn
            bd, mixed = _gate_mix_tile(bd, sn_ref[rows, :], bn, sd_ref[rows, :])
            bd_ref[rows, :] = bd.astype(BF16)
            mix_ref[rows, :] = mixed.astype(BF16)

    specs = [_dil_spec(d, tm) for d in DIL_DILATIONS]
    return pl.pallas_call(
        body, name=name, grid=(n // tm,),
        in_specs=[_rows(tm, NA_WIDTH), _const(w_bna.shape)] + specs + specs
                 + [_const(w_bd.shape), _rows(tm, D_MODEL, 0), _rows(tm, D_MODEL, 1)],
        out_specs=[_rows(tm, 256)] * 2 + [_rows(tm, D_MODEL)] * 3,
        out_shape=[_sds((n, 256), F32), _sds((n, 256), BF16)] + [_sds((n, D_MODEL), BF16)] * 3,
        scratch_shapes=[_dil_scratch(chunk)],
        compiler_params=_params("parallel"),
    )(y_na, w_bna, *outs, *lses, w_bd, gates, gates)


def _dil_merge_bwd(dy, y, lses, *, tm, name):
    n = dy.shape[0]

    def body(*refs):
        dy_ref, y_ref = refs[0:2]
        l_in = refs[2:5]
        do_out, cc_out = refs[5:8], refs[8:11]
        scr = refs[11]
        lv = [_load_token_order(l_in[g], scr, d, tm) for g, d in enumerate(DIL_DILATIONS)]
        ws = _merge_weights(lv)
        dyv = dy_ref[...]
        rr = lax.broadcasted_iota(jnp.int32, (256, 256), 0) >> 6
        cc = lax.broadcasted_iota(jnp.int32, (256, 256), 1) >> 6
        ones = jnp.where(rr == cc, 1.0, 0.0).astype(F32)
        tsum = jnp.dot(dyv * y_ref[...], ones, preferred_element_type=F32,
                       precision=lax.Precision.HIGHEST)
        for g, d in enumerate(DIL_DILATIONS):
            _store_dil_order(ws[g] * dyv, do_out[g], scr, d)
            _store_dil_order(-ws[g] * tsum, cc_out[g], scr, d)

    specs = [_dil_spec(d, tm) for d in DIL_DILATIONS]
    res = pl.pallas_call(
        body, name=name, grid=(n // tm,),
        in_specs=[_rows(tm, 256)] * 2 + specs,
        out_specs=specs + specs,
        out_shape=[_sds((d, n // d, 256), BF16) for d in DIL_DILATIONS]
                  + [_sds((d, n // d, 256), F32) for d in DIL_DILATIONS],
        scratch_shapes=[_dil_scratch(tm)],
        compiler_params=_params("parallel"),
    )(dy, y, *lses)
    return res[0:3], res[3:6]


_WEIGHTS = (("w_in", 1, 736), ("w_branch_na", 1, 128), ("w_branch_dil", 1, 128), ("w_out", 0, 128),
            ("w_up", 1, 512), ("w_down", 0, 512), ("w_ple_gate", 0, 128), ("w_ple_proj", 1, 128))
_W_IN, _W_BNA, _W_BD, _W_OUT, _W_UP, _W_DOWN, _W_PG, _W_PP = range(8)


def _to_full(gathered):
    return gathered.reshape(-1, gathered.shape[2])


def _to_chunks(widx, mat):
    return mat.reshape(N_DEV, _WEIGHTS[widx][2], mat.shape[1])


def _local_step(x, p_bf16, positions, target, g_mix, g_mlp, g_ple, g_final, rpb2,
                get_w_in, relay_rest, get_rest, send_grads):
    tm = 512
    half = HEAD_DIM // 2
    inv_freq = 10000.0 ** (-jnp.arange(half, dtype=F32) / half)
    ang = positions.astype(F32)[:, None] * inv_freq
    cos, sin = jnp.cos(ang), jnp.sin(ang)
    cos_t = jnp.tile(jnp.concatenate([cos, cos], axis=-1), (1, 4))
    sin_t = jnp.tile(jnp.concatenate([-sin, sin], axis=-1), (1, 4))
    rb = _rpb_table(rpb2)

    a = _rms_fwd(x, g_mix, tm=tm, name="rms_mix")
    w_in, token = get_w_in(a)
    na_qkv, gates, dq_g, dk_g, dv_g = _project_in(a, w_in, cos_t, sin_t, tm=512, name="mm_in", after=token)
    sn, sd = (gates, 0), (gates, 1)
    y_na = _na_fwd(na_qkv, rb, name="na_fwd")
    token = relay_rest(y_na)
    d_out, d_lse = [], []
    for g in range(3):
        o, lse = _dil_fwd(dq_g[g], dk_g[g], dv_g[g], name=f"dil_fwd{g}", after=token if g == 0 else None)
        d_out.append(o)
        d_lse.append(lse)
    w_bna, w_bd = get_rest(d_out[2], 0)
    y_dil, y_dil_b, bn, bd, mixed = _branch_mix(y_na, w_bna, d_out, d_lse, w_bd, gates, tm=tm, name="branch_mix")
    w_out, w_up, w_down, w_pg, w_pp = get_rest(mixed, 1)
    h1, c = _matmul(mixed, w_out, out_dtype=(F32, BF16), tm=512, tn=1024, tk=1024, name="mm_out",
                    extra=(x, g_mlp), epilogue=_residual_rms_tile)
    u, f = _matmul(c, w_up, tb=True, out_dtype=(BF16, BF16), tm=512, tn=2048, tk=1024, name="mm_up",
                   epilogue=lambda acc: (acc, jnp.square(jnp.maximum(acc, 0.0))))
    h2, e = _matmul(f, w_down, out_dtype=(F32, BF16), tm=512, tn=1024, tk=4096, name="mm_down",
                    extra=(h1, g_ple), epilogue=_residual_rms_tile)
    pp = _matmul(p_bf16, w_pp, tb=True, out_dtype=F32, tm=512, tn=1024, tk=256, name="mm_pp")

    dh3, dpp, dgt, dg_final, loss = _matmul(
        e, w_pg, out_dtype=(F32, BF16, BF16), tm=512, tn=1024, tk=1024, name="mm_pg_tail",
        extra=(pp, h2, target, g_final), epilogue=_tail_tile, n_colsum=2)
    loss = loss[:, :128]
    gw_pp = _matmul(p_bf16, dpp, ta=True, transpose_out=True, out_dtype=BF16, tm=256, tn=1024, tk=2048,
                    name="mm_gw_pp")
    gw_pg = _matmul(e, dgt, ta=True, out_dtype=BF16, tm=512, tn=1024, tk=2048, name="mm_gw_pg")
    dh2, dh2_b, dg_ple = _matmul(
        dgt, w_pg, tb=True, out_dtype=(F32, BF16), tm=512, tn=1024, tk=1024, name="mm_de",
        extra=(h2, g_ple, dh3), epilogue=_rms_bwd_twice, n_colsum=1)
    du = _matmul(dh2_b, w_down, tb=True, out_dtype=BF16, tm=512, tn=2048, tk=1024, name="mm_du",
                 extra=(u,), epilogue=lambda acc, uv: (acc * (2.0 * jnp.maximum(uv.astype(F32), 0.0)),))
    gw_down = _matmul(f, dh2_b, ta=True, out_dtype=BF16, tm=1024, tn=1024, tk=2048, name="mm_gw_down")
    gw_up = _matmul(c, du, ta=True, transpose_out=True, out_dtype=BF16, tm=512, tn=2048, tk=2048, name="mm_gw_up")
    dh1, dh1_b, dg_mlp = _matmul(
        du, w_up, out_dtype=(F32, BF16), tm=512, tn=1024, tk=4096, name="mm_dc",
        extra=(h1, g_mlp, dh2), epilogue=_rms_bwd_twice, n_colsum=1)
    dbn, dbd, dgn, dgd = _matmul(dh1_b, w_out, tb=True, out_dtype=(BF16,) * 4, tm=512, tn=1024, tk=1024,
                                 name="mm_dmixed", extra=(sn, bn, sd, bd), epilogue=_gate_bwd_tile)
    gw_out = _matmul(mixed, dh1_b, ta=True, out_dtype=BF16, tm=512, tn=1024, tk=2048, name="mm_gw_out")
    gw_bna = _matmul(y_na, dbn, ta=True, transpose_out=True, out_dtype=BF16, tm=512, tn=1024, tk=2048,
                     name="mm_gw_bna")
    dy_na = _matmul(dbn, w_bna, out_dtype=BF16, tm=512, tn=512, tk=1024, name="mm_dy_na")
    gw_bd = _matmul(y_dil_b, dbd, ta=True, transpose_out=True, out_dtype=BF16, tm=256, tn=1024, tk=2048,
                    name="mm_gw_bd")
    token = send_grads((_W_PP, _W_PG, _W_DOWN, _W_UP, _W_OUT, _W_BNA, _W_BD),
                       (gw_pp, gw_pg, gw_down, gw_up, gw_out, gw_bna, gw_bd))
    dy_dil = _matmul(dbd, w_bd, out_dtype=F32, tm=512, tn=256, tk=1024, name="mm_dy_dil", after=token)
    dna = _na_bwd(na_qkv, dy_na, rb, name="na_bwd")
    drpb = _rpb_grad(dna[3].reshape(8, -1), name="rpb_grad")
    do_g, cc_g = _dil_merge_bwd(dy_dil, y_dil, d_lse, tm=tm, name="dil_merge_bwd")
    ddq, ddk, ddv = [], [], []
    for g in range(3):
        r = _dil_bwd(dq_g[g], dk_g[g], dv_g[g], do_g[g], d_lse[g], cc_g[g], name=f"dil_bwd{g}")
        ddq.append(r[0])
        ddk.append(r[1])
        ddv.append(r[2])
    dproj = _assemble_dproj(dna[0:3], ddq, ddk, ddv, dgn, dgd, cos_t, sin_t, tm=tm, name="assemble_dproj")
    gw_in = _matmul(a, dproj, ta=True, transpose_out=True, out_dtype=BF16, tm=512, tn=2944, tk=2048, name="mm_gw_in")
    token = send_grads((_W_IN,), (gw_in,))
    dx, dg_mix = _matmul(
        dproj, w_in, out_dtype=(F32,), tm=512, tn=1024, tk=5888, name="mm_da", after=token,
        extra=(x, g_mix, dh1), epilogue=_rms_bwd_tile, n_colsum=1)
    return loss, dx, (dg_mix, dg_mlp, dg_ple, dg_final), drpb


def _cast_bf16(t, *, name):
    def body(t_ref, o_ref):
        o_ref[...] = t_ref[...].astype(BF16)

    rows, cols = t.shape
    tr = 256 if rows % 256 == 0 else rows
    blk = pl.BlockSpec((tr, cols), lambda i: (i, 0))
    return pl.pallas_call(body, name=name, grid=(rows // tr,), in_specs=[blk], out_specs=blk,
                          out_shape=_sds(t.shape, BF16), compiler_params=_params("parallel"))(t)


def _adamw(w, g, m, v):
    m = ADAM_B1 * m + (1.0 - ADAM_B1) * g
    v = ADAM_B2 * v + (1.0 - ADAM_B2) * (g * g)
    m_hat = m / (1.0 - ADAM_B1 ** ADAM_STEP)
    v_hat = v / (1.0 - ADAM_B2 ** ADAM_STEP)
    delta = -ADAM_LR * (m_hat / (jnp.sqrt(v_hat) + ADAM_EPS) + ADAM_WD * w)
    return delta, m, v


def _sum_adamw(parts, w, m, v, *, tr, name, own=None, transposed=False):
    rows, cols = w.shape
    n_pre = 0 if own is None else 1

    def body(*refs):
        p_ref, w_ref, m_ref, v_ref = refs[n_pre:n_pre + 4]
        g_ref, d_ref, nm_ref, nv_ref = refs[-4:]
        g = (p_ref[0] if own is None else refs[n_pre + 4][...]).astype(F32)
        for s in range(1, N_DEV):
            g = g + p_ref[s].astype(F32)
        if transposed:
            g = g.T
        g_ref[...] = g
        d_ref[...], nm_ref[...], nv_ref[...] = _adamw(w_ref[...], g, m_ref[...], v_ref[...])

    if transposed:
        blk = pl.BlockSpec((rows, tr), lambda i, *_: (0, i))
        g_rows, steps = rows, cols // tr
    else:
        blk = pl.BlockSpec((tr, cols), lambda i, *_: (i, 0))
        g_rows, steps = cols, rows // tr
    in_specs = [pl.BlockSpec((N_DEV, tr, g_rows), lambda i, *_: (0, i, 0)), blk, blk, blk]
    args = [parts, w, m, v]
    if own is not None:
        in_specs.append(pl.BlockSpec((None, tr, g_rows), lambda i, idx: (idx[0], i, 0)))
        args = [own[1]] + args + [own[0]]
    return pl.pallas_call(
        body, name=name,
        grid_spec=pltpu.PrefetchScalarGridSpec(num_scalar_prefetch=n_pre, grid=(steps,), in_specs=in_specs,
                                               out_specs=[blk] * 4),
        out_shape=[_sds((rows, cols), F32)] * 4,
        compiler_params=_params("parallel"),
    )(*args)


_RPB_SIZE = 8 * 15 * 31


def _pack_small(g_mix, g_mlp, g_ple, g_final, rpb, loss_row):
    flat = jnp.concatenate([g_mix.reshape(-1), g_mlp.reshape(-1), g_ple.reshape(-1), g_final.reshape(-1),
                            rpb.reshape(-1), jnp.zeros((3840 - _RPB_SIZE,), F32), loss_row.reshape(-1),
                            jnp.zeros((128,), F32)])
    return flat.reshape(64, 128)


def _unpack_small(t):
    flat = t.reshape(-1)
    return (flat[0:1024].reshape(1, 1024), flat[4096:4096 + _RPB_SIZE].reshape(1, 8, 15, 31),
            flat[1024:2048].reshape(1, 1024), flat[2048:3072].reshape(1, 1024), flat[3072:4096])


def kernel(x, p, positions, g_mix, w_in, rpb, w_branch_na, w_branch_dil, w_out, g_mlp, w_up, w_down, g_ple, w_ple_gate, w_ple_proj, g_final, loss_target, m_g_mix, m_w_in, m_rpb, m_w_branch_na, m_w_branch_dil, m_w_out, m_g_mlp, m_w_up, m_w_down, m_g_ple, m_w_ple_gate, m_w_ple_proj, m_g_final, v_g_mix, v_w_in, v_rpb, v_w_branch_na, v_w_branch_dil, v_w_out, v_g_mlp, v_w_up, v_w_down, v_g_ple, v_w_ple_gate, v_w_ple_proj, v_g_final):
    sharded = dict(w_in=(w_in, m_w_in, v_w_in), w_branch_na=(w_branch_na, m_w_branch_na, v_w_branch_na),
                   w_branch_dil=(w_branch_dil, m_w_branch_dil, v_w_branch_dil), w_out=(w_out, m_w_out, v_w_out),
                   w_up=(w_up, m_w_up, v_w_up), w_down=(w_down, m_w_down, v_w_down),
                   w_ple_gate=(w_ple_gate, m_w_ple_gate, v_w_ple_gate),
                   w_ple_proj=(w_ple_proj, m_w_ple_proj, v_w_ple_proj))
    shards = {k: tuple(t[0] for t in val) for k, val in sharded.items()}

    me = _my_index()

    shards["w_in"] = tuple(t.T for t in shards["w_in"])

    w_in_b = _cast_bf16(shards["w_in"][0], name="cast_w_in")
    rest_b = [shards[name][0].astype(BF16).T if axis == 1 else shards[name][0].astype(BF16)
              for name, axis, _ in _WEIGHTS[1:]]
    first_in, token_in = _start_copies(_first_leg_copies, [w_in_b], [_sds((N_DEV,) + w_in_b.shape, BF16)], 4,
                                       name="start_gather_w_in")

    def whole(landed, mine):
        return _to_full(lax.dynamic_update_index_in_dim(landed, mine, me, 0))

    rest = {}

    def get_w_in(after):
        (mine,), landed = _wait_copies(_first_leg_copies, first_in, after, name="wait_gather_w_in")
        second, token = _start_copies(_second_leg_copies, [], landed, 3, name="start_forward_w_in")
        _, (landed,) = _wait_copies(_second_leg_copies, second, token, name="wait_forward_w_in")
        rest["first"], token = _start_copies(_first_leg_copies, rest_b,
                                             [_sds((N_DEV,) + t.shape, BF16) for t in rest_b], 4 * len(rest_b),
                                             name="start_gather_rest", after=landed)
        return whole(landed, mine), token

    def relay_rest(after):
        rest["mine"], landed = _wait_copies(_first_leg_copies, rest["first"], after, name="wait_gather_rest")
        rest["second"], token = _start_copies(_second_leg_copies, [], landed, 3 * len(rest_b),
                                              name="start_forward_rest")
        return token

    def get_rest(after, stage):
        n_src, send_sems, recv_sems, bufs = rest["second"]
        part = slice(0, 2) if stage == 0 else slice(2, len(rest_b))
        _, landed = _wait_copies(functools.partial(_second_leg_copies, first=part.start),
                                 (n_src, send_sems, recv_sems, bufs[part]), after,
                                 name=f"wait_forward_rest{stage}")
        return [whole(t, own) for t, own in zip(landed, rest["mine"][part])]

    sent = []

    def send_grads(indices, grads):
        chunked = [_to_chunks(i, g) for i, g in zip(indices, grads)]
        handle, token = _start_copies(_exchange_copies, chunked, [_sds(t.shape, BF16) for t in chunked],
                                      7 * len(chunked),
                                      name="start_exchange_" + ("w_in" if indices == (_W_IN,) else "rest"))
        sent.append((indices, handle))
        return token

    g_mix_0 = g_mix + token_in[0:1, 0:1]
    loss, dx, dgs, drpb = _local_step(
        x[0], p[0, 0].astype(BF16), positions[0], loss_target[0],
        g_mix_0, g_mlp, g_ple, g_final.reshape(1, -1), rpb[0], get_w_in, relay_rest, get_rest, send_grads)

    drpb3 = drpb.reshape(8, 16, 32)[:, :15, :31]
    small = _pack_small(dgs[0], dgs[1], dgs[2], dgs[3], drpb3, loss)
    share, done = _start_copies(_gather_copies, [small], [_sds((N_DEV,) + small.shape, F32)], 7,
                                name="start_share_small")

    out = {}
    for indices, handle in sent:
        chunked, landed = _wait_copies(_exchange_copies, handle, done,
                                       name="wait_exchange_" + ("w_in" if indices == (_W_IN,) else "rest"))
        for i, part, mine in zip(indices, landed, chunked):
            name = _WEIGHTS[i][0]
            w, m, v = shards[name]
            turned = _WEIGHTS[i][1] == 1 and i != _W_IN
            res = _sum_adamw(part, w, m, v, tr=368 if i == _W_IN else 128, name="adamw_" + name,
                             own=(mine, me.reshape(1).astype(jnp.int32)), transposed=turned)
            out[name] = [(t.T if i == _W_IN else t)[None] for t in res]
            done = res[0]
    (small,), (small_landed,) = _wait_copies(_gather_copies, share, done, name="wait_share_small")
    small_all = lax.dynamic_update_index_in_dim(small_landed, small, me, 0)
    small_w = _pack_small(g_mix, g_mlp, g_ple, g_final, rpb, jnp.zeros((128,), F32))
    small_m = _pack_small(m_g_mix, m_g_mlp, m_g_ple, m_g_final, m_rpb, jnp.zeros((128,), F32))
    small_v = _pack_small(v_g_mix, v_g_mlp, v_g_ple, v_g_final, v_rpb, jnp.zeros((128,), F32))
    res = _sum_adamw(small_all, small_w, small_m, small_v, tr=64, name="adamw_small")
    unpacked = [_unpack_small(t) for t in res]
    for i, name in enumerate(("g_mix", "rpb", "g_mlp", "g_ple", "g_final")):
        out[name] = [u[i] for u in unpacked]
    loss_total = res[0][62, 0]

    order = ("g_mix", "w_in", "rpb", "w_branch_na", "w_branch_dil", "w_out", "g_mlp", "w_up", "w_down",
             "g_ple", "w_ple_gate", "w_ple_proj", "g_final")
    grads = [out[k][0] for k in order]
    deltas = [out[k][1] for k in order]
    new_m = [out[k][2] for k in order]
    new_v = [out[k][3] for k in order]
    return (loss_total, dx[None], *grads, *deltas, *new_m, *new_v)
```

```python
import functools

import jax
import jax.numpy as jnp
from jax import lax
from jax.experimental import pallas as pl
from jax.experimental.pallas import tpu as pltpu

F32 = jnp.float32
BF16 = jnp.bfloat16

D_MODEL = 1024
HEAD_DIM = 64
GRID_W = 64
NA_WIDTH = 512
DIL_WIDTH = 768
IN_WIDTH = 5888
DIL_DILATIONS = (1, 4, 16)
DIL_RADIUS = 64
NA_WIN_ROWS = 8
RMS_EPS = 1e-6
NEG_INF = -1e30
QK_SCALE = HEAD_DIM ** -0.5

ADAM_LR = 0.001
ADAM_B1 = 0.9
ADAM_B2 = 0.999
ADAM_EPS = 1e-08
ADAM_WD = 0.01
ADAM_STEP = 10

N_DEV = 8
VMEM_LIMIT = 56 * 1024 * 1024
EPILOGUE_ROWS = 256
MESH = pl.DeviceIdType.MESH

NT_DIMS = (((1,), (1,)), ((), ()))
TN_DIMS = (((0,), (0,)), ((), ()))


def _sds(shape, dtype):
    return jax.ShapeDtypeStruct(shape, dtype)


def _params(*sem):
    return pltpu.CompilerParams(dimension_semantics=sem, vmem_limit_bytes=VMEM_LIMIT)


def _rows(tm, width, col=0):
    return pl.BlockSpec((tm, width), lambda i, c=col: (i, c))


def _const(shape):
    zeros = (0,) * len(shape)
    return pl.BlockSpec(shape, lambda i: zeros)


def _my_index():
    return 4 * lax.axis_index("x") + 2 * lax.axis_index("y") + lax.axis_index("c")


def _peer(k):
    x, y, c = lax.axis_index("x"), lax.axis_index("y"), lax.axis_index("c")
    px = 1 - x if k & 4 else x
    py = 1 - y if k & 2 else y
    pc = 1 - c if k & 1 else c
    return (px, py, pc), 4 * px + 2 * py + pc


def _call(body, *, name, grid, in_specs, out_specs, out_shape, scratch_shapes, args, after=None):
    n_in, n_out = len(in_specs), len(out_specs)
    extra = [] if after is None else [after]
    n_x = n_in + len(extra)

    def plain(*refs):
        body(refs[:n_in], refs[n_x:n_x + n_out], refs[n_x + n_out:])

    res = pl.pallas_call(plain, name=name, grid=grid,
                         in_specs=list(in_specs) + [pl.BlockSpec(memory_space=pl.ANY)] * len(extra),
                         out_specs=out_specs, out_shape=out_shape, scratch_shapes=scratch_shapes,
                         compiler_params=_params(*(("arbitrary",) * len(grid))))(*args, *extra)
    return list(res)


_HBM_SPEC = pl.BlockSpec(memory_space=pltpu.HBM)
_SEM_SPEC = pl.BlockSpec(memory_space=pltpu.SEMAPHORE)
_SIDE_EFFECT = pltpu.SideEffectType.DATAFLOW_SIDE_EFFECTING


_FIRST_LEG = (1, 2, 4, 6)
_SECOND_LEG = (2, 4, 6)


def _gather_copies(srcs, lands, send, recv, sending):
    me = _my_index()
    out = []
    for w in range(len(srcs)):
        for k in range(1, N_DEV):
            dev, idx = _peer(k)
            out.append(pltpu.make_async_remote_copy(
                src_ref=srcs[w], dst_ref=lands[w].at[me if sending else idx],
                send_sem=send.at[w * 7 + k - 1], recv_sem=recv.at[w * 7 + k - 1],
                device_id=dev, device_id_type=MESH))
    return out


def _first_leg_copies(srcs, lands, send, recv, sending):
    me = _my_index()
    out = []
    for w in range(len(srcs)):
        for j, k in enumerate(_FIRST_LEG):
            dev, idx = _peer(k)
            out.append(pltpu.make_async_remote_copy(
                src_ref=srcs[w], dst_ref=lands[w].at[me if sending else idx],
                send_sem=send.at[w * 4 + j], recv_sem=recv.at[w * 4 + j],
                device_id=dev, device_id_type=MESH))
    return out


def _second_leg_copies(srcs, lands, send, recv, sending, first=0):
    sibling, _ = _peer(1)
    out = []
    for w in range(len(lands)):
        for j, k in enumerate(_SECOND_LEG):
            slot = _peer(k if sending else k ^ 1)[1]
            sem = (first + w) * 3 + j
            out.append(pltpu.make_async_remote_copy(
                src_ref=lands[w].at[slot], dst_ref=lands[w].at[slot],
                send_sem=send.at[sem], recv_sem=recv.at[sem],
                device_id=sibling, device_id_type=MESH))
    return out


def _exchange_copies(srcs, lands, send, recv, sending):
    out = []
    for w in range(len(srcs)):
        for k in range(1, N_DEV):
            dev, idx = _peer(k)
            out.append(pltpu.make_async_remote_copy(
                src_ref=srcs[w].at[idx], dst_ref=lands[w].at[k],
                send_sem=send.at[w * 7 + k - 1], recv_sem=recv.at[w * 7 + k - 1],
                device_id=dev, device_id_type=MESH))
    return out


def _start_copies(make, srcs, lands, n_copies, *, name, after=None):
    n_src, n_buf = len(srcs), len(srcs) + len(lands)
    extra = [] if after is None else [after]

    def body(*refs):
        send, recv = refs[n_buf + len(extra)], refs[n_buf + len(extra) + 1]
        for cp in make(refs[:n_src], refs[n_src:n_buf], send, recv, True):
            cp.start()
        refs[-1][...] = jnp.zeros_like(refs[-1])

    bufs = list(srcs) + [lax.empty(t.shape, t.dtype) if isinstance(t, jax.ShapeDtypeStruct) else t for t in lands]
    res = pl.pallas_call(
        body, name=name,
        out_shape=(pltpu.SemaphoreType.DMA((n_copies,)), pltpu.SemaphoreType.DMA((n_copies,)),
                   *[pltpu.HBM(t.shape, t.dtype) for t in bufs], _sds((8, 128), F32)),
        in_specs=[_HBM_SPEC] * n_buf + [pl.BlockSpec(memory_space=pl.ANY)] * len(extra),
        out_specs=(_SEM_SPEC, _SEM_SPEC, *([_HBM_SPEC] * n_buf), pl.BlockSpec(memory_space=pltpu.VMEM)),
        input_output_aliases={i: 2 + i for i in range(n_buf)},
        compiler_params=pltpu.CompilerParams(has_side_effects=_SIDE_EFFECT),
    )(*[pltpu.with_memory_space_constraint(t, pltpu.HBM) for t in bufs], *extra)
    return (n_src, res[0], res[1], res[2:2 + n_buf]), res[-1]


def _wait_copies(make, handle, after, *, name):
    n_src, send_sems, recv_sems, bufs = handle
    n_buf = len(bufs)

    def body(*refs):
        for cp in make(refs[:n_src], refs[n_src:n_buf], refs[n_buf], refs[n_buf + 1], False):
            cp.wait_send()
            cp.wait_recv()

    res = pl.pallas_call(
        body, name=name,
        out_shape=tuple(pltpu.HBM(t.shape, t.dtype) for t in bufs),
        in_specs=[_HBM_SPEC] * n_buf + [_SEM_SPEC, _SEM_SPEC, pl.BlockSpec(memory_space=pl.ANY)],
        out_specs=tuple([_HBM_SPEC] * n_buf),
        input_output_aliases={i: i for i in range(n_buf)},
        compiler_params=pltpu.CompilerParams(has_side_effects=_SIDE_EFFECT),
    )(*bufs, send_sems, recv_sems, after)
    return list(res[:n_src]), list(res[n_src:])


def _matmul(a, b, *, ta=False, tb=False, out_dtype, tm, tn, tk, name, after=None, extra=(), epilogue=None,
            n_colsum=0, transpose_out=False):
    m, k = (a.shape[1], a.shape[0]) if ta else a.shape
    n = b.shape[0] if tb else b.shape[1]
    tm, tn, tk = min(tm, m), min(tn, n), min(tk, k)
    nk = k // tk
    dims = (((0 if ta else 1,), (1 if tb else 0,)), ((), ()))
    out_dtypes = out_dtype if isinstance(out_dtype, tuple) else (out_dtype,)
    n_tiles = len(out_dtypes)

    def add_colsums(o_refs, sums):
        i = pl.program_id(1)
        for s_ref, val in zip(o_refs[n_tiles:], sums):
            @pl.when(i == 0)
            def _(s_ref=s_ref, val=val):
                s_ref[...] = val

            @pl.when(i > 0)
            def _(s_ref=s_ref, val=val):
                s_ref[...] += val

    def finish(acc, x_refs, o_refs):
        vals = (acc,) if epilogue is None else epilogue(acc, *[r[...] for r in x_refs])
        for o_ref, val in zip(o_refs[:n_tiles], vals[:n_tiles]):
            o_ref[...] = (val.T if transpose_out else val).astype(o_ref.dtype)
        add_colsums(o_refs, vals[n_tiles:])

    chunk = EPILOGUE_ROWS if (nk == 1 and epilogue is not None and not ta and tm % EPILOGUE_ROWS == 0) else None

    def body(ins, outs, acc):
        a_ref, b_ref = ins[:2]
        if chunk is not None:
            sums = None
            for r0 in range(0, tm, chunk):
                part = lax.dot_general(a_ref[r0:r0 + chunk, :], b_ref[...], dims, preferred_element_type=F32)
                vals = epilogue(part, *[r[...] if r.shape[0] == 1 else r[r0:r0 + chunk, :] for r in ins[2:]])
                for o_ref, val in zip(outs[:n_tiles], vals[:n_tiles]):
                    o_ref[r0:r0 + chunk, :] = val.astype(o_ref.dtype)
                sums = vals[n_tiles:] if sums is None else [s + v for s, v in zip(sums, vals[n_tiles:])]
            add_colsums(outs, sums)
            return
        part = lax.dot_general(a_ref[...], b_ref[...], dims, preferred_element_type=F32)
        if nk == 1:
            finish(part, ins[2:], outs)
            return
        acc_ref, = acc
        kk = pl.program_id(2)

        @pl.when(kk == 0)
        def _():
            acc_ref[...] = part

        @pl.when(kk > 0)
        def _():
            acc_ref[...] += part

        @pl.when(kk == nk - 1)
        def _():
            finish(acc_ref[...], ins[2:], outs)

    a_spec = (pl.BlockSpec((tk, tm), lambda j, i, kk: (kk, i)) if ta
              else pl.BlockSpec((tm, tk), lambda j, i, kk: (i, kk)))
    b_spec = (pl.BlockSpec((tn, tk), lambda j, i, kk: (j, kk)) if tb
              else pl.BlockSpec((tk, tn), lambda j, i, kk: (kk, j)))
    tile = pl.BlockSpec((tm, tn), lambda j, i, kk: (i, j))
    row = pl.BlockSpec((1, tn), lambda j, i, kk: (0, j))

    out_tile, out_dims = (pl.BlockSpec((tn, tm), lambda j, i, kk: (j, i)), (n, m)) if transpose_out else (tile, (m, n))
    res = _call(
        body, name=name, grid=(n // tn, m // tm, nk),
        in_specs=[a_spec, b_spec] + [row if t.shape[0] == 1 else tile for t in extra],
        out_specs=[out_tile] * n_tiles + [row] * n_colsum,
        out_shape=[_sds(out_dims, dt) for dt in out_dtypes] + [_sds((1, n), F32)] * n_colsum,
        scratch_shapes=[] if nk == 1 else [pltpu.VMEM((tm, tn), F32)],
        args=(a, b, *extra), after=after)
    return res if isinstance(out_dtype, tuple) or n_colsum else res[0]


def _rstd(h):
    return lax.rsqrt(jnp.mean(h * h, axis=-1, keepdims=True) + RMS_EPS)


def _sigmoid(z):
    return 1.0 / (1.0 + jnp.exp(-z))


def _rms_fwd(x, g, *, tm, name):
    n = x.shape[0]

    def body(x_ref, g_ref, o_ref):
        h = x_ref[...]
        o_ref[...] = (h * _rstd(h) * g_ref[...]).astype(BF16)

    return pl.pallas_call(
        body, name=name, grid=(n // tm,),
        in_specs=[_rows(tm, D_MODEL), _const((1, D_MODEL))],
        out_specs=_rows(tm, D_MODEL), out_shape=_sds((n, D_MODEL), BF16),
        compiler_params=_params("parallel"),
    )(x, g)


def _swap_halves(t):
    lane = lax.broadcasted_iota(jnp.int32, (t.shape[0], 128), 1)
    pieces = [t[:, c:c + 128] for c in range(0, t.shape[1], 128)]
    return jnp.concatenate([jnp.where((lane & 63) < 32, pltpu.roll(h, 96, 1), pltpu.roll(h, 32, 1))
                            for h in pieces], axis=1)


def _dil_spec(dil, tm):
    return pl.BlockSpec((dil, tm // dil, 256), lambda i: (0, i, 0))


def _dil_scratch(tm):
    return pltpu.VMEM((2, tm, 128), F32)


def _load_token_order(src, scr, dil, rows, row0=0):
    if dil == 1:
        return src[0, row0:row0 + rows, :]
    for j in range(dil):
        for c in range(2):
            scr[c, pl.ds(j, rows // dil, stride=dil), :] = (
                src[j, row0 // dil:(row0 + rows) // dil, c * 128:(c + 1) * 128])
    return jnp.concatenate([scr[0, 0:rows, :], scr[1, 0:rows, :]], axis=1)


def _store_dil_order(val, dst, scr, dil, row0=0):
    rows = val.shape[0]
    if dil == 1:
        dst[0, row0:row0 + rows, :] = val.astype(dst.dtype)
        return
    for c in range(2):
        scr[c] = val[:, c * 128:(c + 1) * 128]
    for j in range(dil):
        for c in range(2):
            dst[j, row0 // dil:(row0 + rows) // dil, c * 128:(c + 1) * 128] = (
                scr[c, pl.ds(j, rows // dil, stride=dil), :].astype(dst.dtype))


def _project_in(a, w_t, cos_t, sin_t, *, tm, name, after=None):
    n = a.shape[0]
    n_dil = len(DIL_DILATIONS)
    na_w, dil_w = 3 * NA_WIDTH, 3 * DIL_WIDTH
    chunk = min(EPILOGUE_ROWS, tm)
    extra = [] if after is None else [after]

    def body(a_ref, w_ref, cos_ref, sin_ref, *rest):
        na_ref, gate_ref = rest[len(extra):len(extra) + 2]
        outs, scr = rest[len(extra) + 2:len(extra) + 2 + 3 * n_dil], rest[-1]

        def part(r0, first, width):
            return lax.dot_general(a_ref[r0:r0 + chunk, :], w_ref[first:first + width, :], NT_DIMS,
                                   preferred_element_type=F32)

        for r0 in range(0, tm, chunk):
            na_ref[r0:r0 + chunk, :] = part(r0, 0, na_w).astype(BF16)
            dil_part = part(r0, na_w, dil_w)
            cosv, sinv = cos_ref[r0:r0 + chunk, :], sin_ref[r0:r0 + chunk, :]
            for t in range(3):
                for gi, dil in enumerate(DIL_DILATIONS):
                    c0 = (t * n_dil + gi) * 256
                    val = dil_part[:, c0:c0 + 256]
                    if t < 2:
                        val = val * cosv + _swap_halves(val) * sinv
                    _store_dil_order(val, outs[t * n_dil + gi], scr, dil, r0)
            gate_ref[r0:r0 + chunk, :] = _sigmoid(part(r0, na_w + dil_w, 2 * D_MODEL)).astype(BF16)

    out_specs = [_rows(tm, na_w), _rows(tm, 2 * D_MODEL)]
    out_shape = [_sds((n, na_w), BF16), _sds((n, 2 * D_MODEL), BF16)]
    for _ in range(3):
        for dil in DIL_DILATIONS:
            out_specs.append(pl.BlockSpec((dil, tm // dil, 256), lambda i: (0, i, 0)))
            out_shape.append(_sds((dil, n // dil, 256), BF16))
    res = pl.pallas_call(
        body, name=name, grid=(n // tm,),
        in_specs=[_rows(tm, D_MODEL), _const(w_t.shape), _rows(tm, 256), _rows(tm, 256)]
                 + [pl.BlockSpec(memory_space=pl.ANY)] * len(extra),
        out_specs=out_specs, out_shape=out_shape,
        scratch_shapes=[pltpu.VMEM((2, chunk, 128), F32)],
        compiler_params=_params("parallel"),
    )(a, w_t, cos_t, sin_t, *extra)
    return res[0], res[1], res[2:5], res[5:8], res[8:11]


def _residual_rms_tile(delta, h, g):
    hn = h + delta
    return hn, hn * _rstd(hn) * g


def _gate_mix_tile(b2, s1, b1, s2):
    return b2, s1.astype(F32) * b1.astype(F32) + s2.astype(F32) * b2


def _gate_bwd_tile(dm, s1, b1, s2, b2):
    s1, b1, s2, b2 = (t.astype(F32) for t in (s1, b1, s2, b2))
    return dm * s1, dm * s2, dm * b1 * s1 * (1.0 - s1), dm * b2 * s2 * (1.0 - s2)


def _tail_tile(gt, pp, h2, target, g):
    sg = _sigmoid(gt)
    h3 = h2 + sg * pp
    r3 = _rstd(h3)
    n3 = h3 * r3
    err = n3 * g - target
    loss = 0.5 * jnp.sum(jnp.sum(err * err, axis=-1, keepdims=True) / D_MODEL)
    dy = err / D_MODEL
    dn = dy * g
    dh3 = r3 * (dn - n3 * jnp.mean(dn * n3, axis=-1, keepdims=True))
    return (dh3, dh3 * sg, dh3 * pp * sg * (1.0 - sg),
            jnp.sum(dy * n3, axis=0, keepdims=True), jnp.full((1, gt.shape[1]), loss, F32))


def _rms_bwd_tile(dz, h, g, dres):
    r = _rstd(h)
    nrm = h * r
    dn = dz * g
    dh = dres + r * (dn - nrm * jnp.mean(dn * nrm, axis=-1, keepdims=True))
    return dh, jnp.sum(dz * nrm, axis=0, keepdims=True)


def _rms_bwd_twice(dz, h, g, dres):
    dh, dg = _rms_bwd_tile(dz, h, g, dres)
    return dh, dh, dg


def _assemble_dproj(dna, ddil_q, ddil_k, ddil_v, dgn, dgd, cos_t, sin_t, *, tm, name):
    n = dgn.shape[0]

    def body(*refs):
        dq_ref, dk_ref, dv_ref = refs[0:3]
        dil_in = refs[3:12]
        dgn_ref, dgd_ref, cos_ref, sin_ref, o_ref, scr = refs[12:18]
        o_ref[:, 0:512] = dq_ref[...]
        o_ref[:, 512:1024] = dk_ref[...].astype(BF16)
        o_ref[:, 1024:1536] = dv_ref[...].astype(BF16)
        cosv, sinv = cos_ref[...], sin_ref[...]
        for t in range(3):
            for gi, dil in enumerate(DIL_DILATIONS):
                val = _load_token_order(dil_in[t * 3 + gi], scr, dil, tm)
                if t < 2:
                    val = val * cosv + _swap_halves(val * sinv)
                c0 = 1536 + t * DIL_WIDTH + gi * 256
                o_ref[:, c0:c0 + 256] = val.astype(BF16)
        o_ref[:, 3840:4864] = dgn_ref[...]
        o_ref[:, 4864:5888] = dgd_ref[...]

    in_specs = [_rows(tm, NA_WIDTH)] * 3
    for _ in range(3):
        for dil in DIL_DILATIONS:
            in_specs.append(pl.BlockSpec((dil, tm // dil, 256), lambda i: (0, i, 0)))
    in_specs += [_rows(tm, D_MODEL)] * 2 + [_rows(tm, 256)] * 2
    return pl.pallas_call(
        body, name=name, grid=(n // tm,), in_specs=in_specs,
        out_specs=_rows(tm, IN_WIDTH), out_shape=_sds((n, IN_WIDTH), BF16),
        scratch_shapes=[_dil_scratch(tm)],
        compiler_params=_params("parallel"),
    )(*dna, *ddil_q, *ddil_k, *ddil_v, dgn, dgd, cos_t, sin_t)


N_ROW_OFF = 2 * NA_WIN_ROWS - 1
N_PAIRS = N_ROW_OFF - 1
RB_WIDTH = (N_ROW_OFF + 1) * GRID_W


def _na_bias(rb_ref, pair_scr):
    shape = (GRID_W, RB_WIDTH)
    qc = lax.broadcasted_iota(jnp.int32, shape, 0)
    qc2 = lax.broadcasted_iota(jnp.int32, (GRID_W, 128), 0)
    kc2 = lax.broadcasted_iota(jnp.int32, (GRID_W, 128), 1) & (GRID_W - 1)
    cs = jnp.clip(qc2 - 8, 0, GRID_W - 16)
    valid = (kc2 >= cs) & (kc2 < cs + 16)
    for hh in range(2):
        t = jnp.broadcast_to(rb_ref[hh], shape)
        t = pltpu.roll(t, RB_WIDTH - 15, 1)
        for b in range(6):
            t = jnp.where(((qc >> b) & 1) == 1, pltpu.roll(t, 1 << b, 1), t)
        t_odd = pltpu.roll(t, RB_WIDTH - GRID_W, 1)
        for ro in range(N_PAIRS):
            src = t if ro % 2 == 0 else t_odd
            base = (ro // 2) * 128
            pair_scr[hh, ro] = jnp.where(valid, src[:, base:base + 128], NEG_INF)


NA_GROUP_FWD = 8
NA_GROUP_BWD = 4


def _stack_heads(ref, r, scale=1.0):
    lane = lax.broadcasted_iota(jnp.int32, (GRID_W, 128), 1)
    t = ref[pl.ds(pl.multiple_of(r * GRID_W, GRID_W), GRID_W), :].astype(F32) * scale
    return jnp.concatenate([jnp.where(lane < 64, t, 0.0), jnp.where(lane >= 64, t, 0.0)], axis=0).astype(BF16)


def _unstack_heads(t2):
    lane = lax.broadcasted_iota(jnp.int32, (GRID_W, 128), 1)
    return jnp.where(lane < 64, t2[:GRID_W], t2[GRID_W:])


def _na_window(k_ref, v_ref, r, n_rows):
    rs = jnp.clip(r - NA_WIN_ROWS // 2, 0, n_rows - NA_WIN_ROWS)
    ro0 = (NA_WIN_ROWS - 1) - (r - rs)
    off = pl.multiple_of(rs * GRID_W, GRID_W)
    kw = k_ref[pl.ds(off, NA_WIN_ROWS * GRID_W), :]
    vw = v_ref[pl.ds(off, NA_WIN_ROWS * GRID_W), :]
    return kw, vw, off, ro0


def _na_probs(s_raw, pair_scr, ro0):
    bias = [jnp.concatenate([pair_scr[hh, ro0 + 2 * j] for j in range(NA_WIN_ROWS // 2)], axis=1)
            for hh in range(2)]
    s = s_raw + jnp.concatenate(bias, axis=0)
    m = jnp.max(s, axis=-1, keepdims=True)
    e = jnp.exp(s - m)
    return e * (1.0 / jnp.sum(e, axis=-1, keepdims=True))


def _na_qkv_specs(n):
    pairs = NA_WIDTH // 128
    return [pl.BlockSpec((n, 128), lambda h, first=t * pairs: (0, first + h)) for t in range(3)]


def _na_fwd(qkv, rb, *, name):
    n = qkv.shape[0]
    n_rows = n // GRID_W

    def body(ins, outs, scr):
        q_ref, k_ref, v_ref, rb_ref = ins
        o_ref, = outs
        pair_scr, = scr
        _na_bias(rb_ref, pair_scr)

        def group(g, carry):
            rows = [g * NA_GROUP_FWD + t for t in range(NA_GROUP_FWD)]
            wins = [_na_window(k_ref, v_ref, r, n_rows) for r in rows]
            raw = [lax.dot_general(_stack_heads(q_ref, r, QK_SCALE), w[0], NT_DIMS, preferred_element_type=F32)
                   for r, w in zip(rows, wins)]
            probs = [_na_probs(s, pair_scr, w[3]) for s, w in zip(raw, wins)]
            outs2 = [jnp.dot(p.astype(BF16), w[1], preferred_element_type=F32) for p, w in zip(probs, wins)]
            for r, o2 in zip(rows, outs2):
                o_ref[pl.ds(pl.multiple_of(r * GRID_W, GRID_W), GRID_W), :] = _unstack_heads(o2).astype(BF16)
            return carry

        lax.fori_loop(0, n_rows // NA_GROUP_FWD, group, 0)

    col = pl.BlockSpec((n, 128), lambda h: (0, h))
    return _call(
        body, name=name, grid=(NA_WIDTH // 128,),
        in_specs=_na_qkv_specs(n) + [pl.BlockSpec((2, 1, RB_WIDTH), lambda h: (h, 0, 0))],
        out_specs=[col], out_shape=[_sds((n, NA_WIDTH), BF16)],
        scratch_shapes=[pltpu.VMEM((2, N_PAIRS, GRID_W, 128), F32)],
        args=(qkv, qkv, qkv, rb))[0]


def _na_bwd(qkv, do, rb, *, name, after=None):
    n = qkv.shape[0]
    n_rows = n // GRID_W
    win = NA_WIN_ROWS * GRID_W

    def body(ins, outs, scr):
        q_ref, k_ref, v_ref, do_ref, rb_ref = ins
        dq_ref, dk_ref, dv_ref, drb_ref = outs
        pair_scr, acc_scr = scr
        _na_bias(rb_ref, pair_scr)
        acc_scr[...] = jnp.zeros_like(acc_scr)
        dk_ref[...] = jnp.zeros_like(dk_ref)
        dv_ref[...] = jnp.zeros_like(dv_ref)

        def group(g, carry):
            rows = [g * NA_GROUP_BWD + t for t in range(NA_GROUP_BWD)]
            wins = [_na_window(k_ref, v_ref, r, n_rows) for r in rows]
            qss = [_stack_heads(q_ref, r, QK_SCALE) for r in rows]
            doss = [_stack_heads(do_ref, r) for r in rows]
            raw = [lax.dot_general(qs, w[0], NT_DIMS, preferred_element_type=F32) for qs, w in zip(qss, wins)]
            dps = [lax.dot_general(dos, w[1], NT_DIMS, preferred_element_type=F32) for dos, w in zip(doss, wins)]
            probs = [_na_probs(s, pair_scr, w[3]) for s, w in zip(raw, wins)]
            dss = [p * (dp - jnp.sum(p * dp, axis=-1, keepdims=True)) for p, dp in zip(probs, dps)]
            dsbs = [ds.astype(BF16) for ds in dss]
            dq2s = [jnp.dot(dsb, w[0], preferred_element_type=F32) for dsb, w in zip(dsbs, wins)]
            dkws = [lax.dot_general(dsb, qs, TN_DIMS, preferred_element_type=F32) for dsb, qs in zip(dsbs, qss)]
            dvws = [lax.dot_general(p.astype(BF16), dos, TN_DIMS, preferred_element_type=F32)
                    for p, dos in zip(probs, doss)]
            for t, r in enumerate(rows):
                _, _, off, ro0 = wins[t]
                for hh in range(2):
                    for j in range(NA_WIN_ROWS // 2):
                        acc_scr[hh, ro0 + 2 * j] += dss[t][hh * GRID_W:(hh + 1) * GRID_W, j * 128:(j + 1) * 128]
                dq_ref[pl.ds(pl.multiple_of(r * GRID_W, GRID_W), GRID_W), :] = (
                    _unstack_heads(dq2s[t]) * QK_SCALE).astype(BF16)
                dk_ref[pl.ds(off, win), :] += dkws[t]
                dv_ref[pl.ds(off, win), :] += dvws[t]
            return carry

        lax.fori_loop(0, n_rows // NA_GROUP_BWD, group, 0)

        qc = lax.broadcasted_iota(jnp.int32, (N_PAIRS * GRID_W, 128), 0)
        for hh in range(2):
            t = acc_scr[hh].reshape(N_PAIRS * GRID_W, 128)
            for b in range(6):
                t = jnp.where(((qc >> b) & 1) == 1, pltpu.roll(t, 128 - (1 << b), 1), t)
            t = pltpu.roll(t, 15, 1)
            drb_ref[hh] = jnp.sum(t.reshape(N_PAIRS, GRID_W, 128), axis=1)

    col = pl.BlockSpec((n, 128), lambda h: (0, h))
    return _call(
        body, name=name, grid=(NA_WIDTH // 128,),
        in_specs=_na_qkv_specs(n) + [col, pl.BlockSpec((2, 1, RB_WIDTH), lambda h: (h, 0, 0))],
        out_specs=[col, col, col, pl.BlockSpec((2, N_PAIRS, 128), lambda h: (h, 0, 0))],
        out_shape=[_sds((n, NA_WIDTH), BF16), _sds((n, NA_WIDTH), F32), _sds((n, NA_WIDTH), F32),
                   _sds((8, N_PAIRS, 128), F32)],
        scratch_shapes=[pltpu.VMEM((2, N_PAIRS, GRID_W, 128), F32),
                        pltpu.VMEM((2, N_PAIRS, GRID_W, 128), F32)],
        args=(qkv, qkv, qkv, do, rb), after=after)


def _rpb_table(rpb2):
    t = jnp.pad(rpb2, ((0, 0), (0, 1), (0, GRID_W - rpb2.shape[-1])))
    return t.reshape(8, 1, RB_WIDTH)


def _rpb_grad(drb, *, name):
    kdim = drb.shape[1]

    def body(x_ref, o_ref):
        kk = lax.broadcasted_iota(jnp.int32, (128, 512), 0)
        jj = lax.broadcasted_iota(jnp.int32, (128, 512), 1)
        half, co = kk >> 6, kk & 63
        acc = jnp.zeros((8, 512), F32)
        for ro in range(N_PAIRS):
            hit = ((ro + half) == (jj >> 5)) & (co == (jj & 31)) & (co < 31)
            onehot = jnp.where(hit, 1.0, 0.0).astype(F32)
            acc = acc + jnp.dot(x_ref[:, ro * 128:(ro + 1) * 128], onehot, preferred_element_type=F32,
                                precision=lax.Precision.HIGHEST)
        o_ref[...] = acc

    return pl.pallas_call(
        body, name=name, grid=(1,),
        in_specs=[_const((8, kdim))], out_specs=_const((8, 512)), out_shape=_sds((8, 512), F32),
        compiler_params=_params("arbitrary"),
    )(drb)


DIL_GROUP = 2


def _dil_blocks(length):
    qb = min(128, length)
    return qb, min(qb + 2 * DIL_RADIUS, length), min(DIL_GROUP, length // qb)


def _stack_lanes(ref, t, qb, scale=1.0):
    lane = lax.broadcasted_iota(jnp.int32, (qb, 256), 1)
    val = ref[0, t * qb:(t + 1) * qb, :].astype(F32) * scale
    return jnp.concatenate([jnp.where((lane >> 6) == h, val, 0.0) for h in range(4)], axis=0).astype(BF16)


def _dil_window(k_ref, v_ref, blk, qb, win, length):
    start = pl.multiple_of(jnp.clip(blk * qb - DIL_RADIUS, 0, length - win), DIL_RADIUS)
    return k_ref[0, pl.ds(start, win), :], v_ref[0, pl.ds(start, win), :], start


def _dil_caps_init(caps_scr, qb, win):
    @pl.when((pl.program_id(0) == 0) & (pl.program_id(1) == 0))
    def _():
        gap = ((lax.broadcasted_iota(jnp.int32, (4 * qb, win), 0) & (qb - 1))
               - lax.broadcasted_iota(jnp.int32, (4 * qb, win), 1))
        for v in range(3):
            caps_scr[v] = jnp.where(jnp.abs(gap + v * DIL_RADIUS) <= DIL_RADIUS, jnp.inf, NEG_INF)


def _dil_mask(s, blk, start, qb, caps_scr):
    return jnp.minimum(s, caps_scr[(blk * qb - start) // DIL_RADIUS])


def _pick_heads(stacked, qb):
    lane = lax.broadcasted_iota(jnp.int32, (qb, 256), 1)
    out = jnp.zeros((qb, 256), stacked.dtype)
    for h in range(4):
        out = jnp.where((lane >> 6) == h, stacked[h * qb:(h + 1) * qb], out)
    return out


def _stack_head_cols(ref, t, qb):
    return jnp.concatenate([ref[0, t * qb:(t + 1) * qb, 64 * h:64 * h + 1] for h in range(4)], axis=0)


def _dil_fwd(q, k, v, *, name, after=None):
    dil, length, _ = q.shape
    qb, win, grp = _dil_blocks(length)
    extra = [] if after is None else [after]

    def body(q_ref, k_ref, v_ref, *rest):
        o_ref, lse_ref, caps_scr = rest[-3:]
        _dil_caps_init(caps_scr, qb, win)
        blks = [pl.program_id(1) * grp + t for t in range(grp)]
        wins = [_dil_window(k_ref, v_ref, b, qb, win, length) for b in blks]
        raw = [lax.dot_general(_stack_lanes(q_ref, t, qb, QK_SCALE), w[0], NT_DIMS, preferred_element_type=F32)
               for t, w in enumerate(wins)]
        lses, outs = [], []
        for t, (s, w) in enumerate(zip(raw, wins)):
            s = _dil_mask(s, blks[t], w[2], qb, caps_scr)
            m = jnp.max(s, axis=-1, keepdims=True)
            e = jnp.exp(s - m)
            norm = jnp.sum(e, axis=-1, keepdims=True)
            lses.append(m + jnp.log(norm))
            outs.append(jnp.dot((e * (1.0 / norm)).astype(BF16), w[1], preferred_element_type=F32))
        for t in range(grp):
            o_ref[0, t * qb:(t + 1) * qb, :] = _pick_heads(outs[t], qb)
            lse_ref[0, t * qb:(t + 1) * qb, :] = _pick_heads(jnp.broadcast_to(lses[t], (4 * qb, 256)), qb)

    seq = pl.BlockSpec((1, length, 256), lambda j, i: (j, 0, 0))
    blk = pl.BlockSpec((1, grp * qb, 256), lambda j, i: (j, i, 0))
    return pl.pallas_call(
        body, name=name, grid=(dil, length // (grp * qb)),
        in_specs=[blk, seq, seq] + [pl.BlockSpec(memory_space=pl.ANY)] * len(extra), out_specs=[blk, blk],
        out_shape=[_sds((dil, length, 256), F32)] * 2,
        scratch_shapes=[pltpu.VMEM((3, 4 * qb, win), F32)],
        compiler_params=_params("arbitrary", "arbitrary"),
    )(q, k, v, *extra)


def _dil_bwd(q, k, v, do, lse, cc, *, name):
    dil, length, _ = q.shape
    qb, win, grp = _dil_blocks(length)

    def body(q_ref, k_ref, v_ref, do_ref, lse_ref, cc_ref, dq_ref, dk_ref, dv_ref, caps_scr):
        _dil_caps_init(caps_scr, qb, win)

        @pl.when(pl.program_id(1) == 0)
        def _():
            dk_ref[...] = jnp.zeros_like(dk_ref)
            dv_ref[...] = jnp.zeros_like(dv_ref)

        blks = [pl.program_id(1) * grp + t for t in range(grp)]
        wins = [_dil_window(k_ref, v_ref, b, qb, win, length) for b in blks]
        qss = [_stack_lanes(q_ref, t, qb, QK_SCALE) for t in range(grp)]
        doss = [_stack_lanes(do_ref, t, qb) for t in range(grp)]
        raw = [lax.dot_general(qs, w[0], NT_DIMS, preferred_element_type=F32) for qs, w in zip(qss, wins)]
        dps = [lax.dot_general(dos, w[1], NT_DIMS, preferred_element_type=F32) for dos, w in zip(doss, wins)]
        probs = [jnp.exp(_dil_mask(s, blks[t], wins[t][2], qb, caps_scr) - _stack_head_cols(lse_ref, t, qb))
                 for t, s in enumerate(raw)]
        dsbs = [(p * (dp + _stack_head_cols(cc_ref, t, qb))).astype(BF16)
                for t, (p, dp) in enumerate(zip(probs, dps))]
        dq4s = [jnp.dot(dsb, w[0], preferred_element_type=F32) for dsb, w in zip(dsbs, wins)]
        dkws = [lax.dot_general(dsb, qs, TN_DIMS, preferred_element_type=F32) for dsb, qs in zip(dsbs, qss)]
        dvws = [lax.dot_general(p.astype(BF16), dos, TN_DIMS, preferred_element_type=F32)
                for p, dos in zip(probs, doss)]
        for t in range(grp):
            dq_ref[0, t * qb:(t + 1) * qb, :] = _pick_heads(dq4s[t], qb) * QK_SCALE
            dk_ref[0, pl.ds(wins[t][2], win), :] += dkws[t]
            dv_ref[0, pl.ds(wins[t][2], win), :] += dvws[t]

    seq = pl.BlockSpec((1, length, 256), lambda j, i: (j, 0, 0))
    blk = pl.BlockSpec((1, grp * qb, 256), lambda j, i: (j, i, 0))
    return pl.pallas_call(
        body, name=name, grid=(dil, length // (grp * qb)),
        in_specs=[blk, seq, seq, blk, blk, blk], out_specs=[blk, seq, seq],
        out_shape=[_sds((dil, length, 256), F32)] * 3,
        scratch_shapes=[pltpu.VMEM((3, 4 * qb, win), F32)],
        compiler_params=_params("arbitrary", "arbitrary"),
    )(q, k, v, do, lse, cc)


def _merge_weights(lses):
    m = jnp.maximum(jnp.maximum(lses[0], lses[1]), lses[2])
    es = [jnp.exp(t - m) for t in lses]
    inv = 1.0 / (es[0] + es[1] + es[2])
    return [e * inv for e in es]


def _branch_mix(y_na, w_bna, outs, lses, w_bd, gates, *, tm, name):
    n = y_na.shape[0]
    chunk = min(EPILOGUE_ROWS, tm)

    def body(yna_ref, wn_ref, *rest):
        o_in, l_in = rest[0:3], rest[3:6]
        wd_ref, sn_ref, sd_ref = rest[6:9]
        y_ref, yb_ref, bn_ref, bd_ref, mix_ref, scr = rest[9:15]
        for r0 in range(0, tm, chunk):
            rows = slice(r0, r0 + chunk)
            lv = [_load_token_order(l_in[g], scr, d, chunk, r0) for g, d in enumerate(DIL_DILATIONS)]
            ws = _merge_weights(lv)
            y = jnp.zeros((chunk, 256), F32)
            for g, d in enumerate(DIL_DILATIONS):
                y = y + ws[g] * _load_token_order(o_in[g], scr, d, chunk, r0)
            yb = y.astype(BF16)
            y_ref[rows, :] = y
            yb_ref[rows, :] = yb
            bn = lax.dot_general(yna_ref[rows, :], wn_ref[...], NT_DIMS, preferred_element_type=F32).astype(BF16)
            bd = lax.dot_general(yb, wd_ref[...], NT_DIMS, preferred_element_type=F32)
            bn_ref[rows, :] = bn
            bd, mixed = _gate_mix_tile(bd, sn_ref[rows, :], bn, sd_ref[rows, :])
            bd_ref[rows, :] = bd.astype(BF16)
            mix_ref[rows, :] = mixed.astype(BF16)

    specs = [_dil_spec(d, tm) for d in DIL_DILATIONS]
    return pl.pallas_call(
        body, name=name, grid=(n // tm,),
        in_specs=[_rows(tm, NA_WIDTH), _const(w_bna.shape)] + specs + specs
                 + [_const(w_bd.shape), _rows(tm, D_MODEL, 0), _rows(tm, D_MODEL, 1)],
        out_specs=[_rows(tm, 256)] * 2 + [_rows(tm, D_MODEL)] * 3,
        out_shape=[_sds((n, 256), F32), _sds((n, 256), BF16)] + [_sds((n, D_MODEL), BF16)] * 3,
        scratch_shapes=[_dil_scratch(chunk)],
        compiler_params=_params("parallel"),
    )(y_na, w_bna, *outs, *lses, w_bd, gates, gates)


def _branch_bwd(dh, w_out, gates, bn, bd, w_bna, w_bd, y, lses, *, tm, name):
    n = dh.shape[0]
    chunk = min(EPILOGUE_ROWS, tm)

    def body(dh_ref, wo_ref, sn_ref, sd_ref, bn_ref, bd_ref, wn_ref, wd_ref, y_ref, *rest):
        l_in = rest[0:3]
        dbn_ref, dbd_ref, dgn_ref, dgd_ref, dyna_ref = rest[3:8]
        do_out, cc_out, scr = rest[8:11], rest[11:14], rest[14]
        rr = lax.broadcasted_iota(jnp.int32, (256, 256), 0) >> 6
        cc = lax.broadcasted_iota(jnp.int32, (256, 256), 1) >> 6
        ones = jnp.where(rr == cc, 1.0, 0.0).astype(F32)
        for r0 in range(0, tm, chunk):
            rows = slice(r0, r0 + chunk)
            dm = lax.dot_general(dh_ref[rows, :], wo_ref[...], NT_DIMS, preferred_element_type=F32)
            dbn, dbd, dgn, dgd = (t.astype(BF16) for t in _gate_bwd_tile(
                dm, sn_ref[rows, :], bn_ref[rows, :], sd_ref[rows, :], bd_ref[rows, :]))
            dbn_ref[rows, :] = dbn
            dbd_ref[rows, :] = dbd
            dgn_ref[rows, :] = dgn
            dgd_ref[rows, :] = dgd
            dyna_ref[rows, :] = jnp.dot(dbn, wn_ref[...], preferred_element_type=F32).astype(BF16)
            dyv = jnp.dot(dbd, wd_ref[...], preferred_element_type=F32)
            lv = [_load_token_order(l_in[g], scr, d, chunk, r0) for g, d in enumerate(DIL_DILATIONS)]
            ws = _merge_weights(lv)
            tsum = jnp.dot(dyv * y_ref[rows, :], ones, preferred_element_type=F32,
                           precision=lax.Precision.HIGHEST)
            for g, d in enumerate(DIL_DILATIONS):
                _store_dil_order(ws[g] * dyv, do_out[g], scr, d, r0)
                _store_dil_order(-ws[g] * tsum, cc_out[g], scr, d, r0)

    specs = [_dil_spec(d, tm) for d in DIL_DILATIONS]
    wide = _rows(tm, D_MODEL)
    res = pl.pallas_call(
        body, name=name, grid=(n // tm,),
        in_specs=[wide, _const(w_out.shape), _rows(tm, D_MODEL, 0), _rows(tm, D_MODEL, 1), wide, wide,
                  _const(w_bna.shape), _const(w_bd.shape), _rows(tm, 256)] + specs,
        out_specs=[wide] * 4 + [_rows(tm, NA_WIDTH)] + specs + specs,
        out_shape=[_sds((n, D_MODEL), BF16)] * 4 + [_sds((n, NA_WIDTH), BF16)]
                  + [_sds((d, n // d, 256), BF16) for d in DIL_DILATIONS]
                  + [_sds((d, n // d, 256), F32) for d in DIL_DILATIONS],
        scratch_shapes=[_dil_scratch(chunk)],
        compiler_params=_params("parallel"),
    )(dh, w_out, gates, gates, bn, bd, w_bna, w_bd, y, *lses)
    return res[0], res[1], res[2], res[3], res[4], res[5:8], res[8:11]


_WEIGHTS = (("w_in", 1, 736), ("w_branch_na", 1, 128), ("w_branch_dil", 1, 128), ("w_out", 0, 128),
            ("w_up", 1, 512), ("w_down", 0, 512), ("w_ple_gate", 0, 128), ("w_ple_proj", 1, 128))
_W_IN, _W_BNA, _W_BD, _W_OUT, _W_UP, _W_DOWN, _W_PG, _W_PP = range(8)


def _to_full(gathered):
    return gathered.reshape(-1, gathered.shape[2])


def _to_chunks(widx, mat):
    return mat.reshape(N_DEV, _WEIGHTS[widx][2], mat.shape[1])


def _local_step(x, p_bf16, positions, target, g_mix, g_mlp, g_ple, g_final, rpb2,
                get_w_in, relay_rest, get_rest, send_grads):
    tm = 512
    half = HEAD_DIM // 2
    inv_freq = 10000.0 ** (-jnp.arange(half, dtype=F32) / half)
    ang = positions.astype(F32)[:, None] * inv_freq
    cos, sin = jnp.cos(ang), jnp.sin(ang)
    cos_t = jnp.tile(jnp.concatenate([cos, cos], axis=-1), (1, 4))
    sin_t = jnp.tile(jnp.concatenate([-sin, sin], axis=-1), (1, 4))
    rb = _rpb_table(rpb2)

    a = _rms_fwd(x, g_mix, tm=tm, name="rms_mix")
    w_in, token = get_w_in(a)
    na_qkv, gates, dq_g, dk_g, dv_g = _project_in(a, w_in, cos_t, sin_t, tm=512, name="mm_in", after=token)
    y_na = _na_fwd(na_qkv, rb, name="na_fwd")
    token = relay_rest(y_na)
    d_out, d_lse = [], []
    for g in range(3):
        o, lse = _dil_fwd(dq_g[g], dk_g[g], dv_g[g], name=f"dil_fwd{g}", after=token if g == 0 else None)
        d_out.append(o)
        d_lse.append(lse)
    w_bna, w_bd = get_rest(d_out[2], 0)
    y_dil, y_dil_b, bn, bd, mixed = _branch_mix(y_na, w_bna, d_out, d_lse, w_bd, gates, tm=tm, name="branch_mix")
    w_out, w_up, w_down, w_pg, w_pp = get_rest(mixed, 1)
    h1, c = _matmul(mixed, w_out, out_dtype=(F32, BF16), tm=512, tn=1024, tk=1024, name="mm_out",
                    extra=(x, g_mlp), epilogue=_residual_rms_tile)
    u, f = _matmul(c, w_up, tb=True, out_dtype=(BF16, BF16), tm=512, tn=2048, tk=1024, name="mm_up",
                   epilogue=lambda acc: (acc, jnp.square(jnp.maximum(acc, 0.0))))
    h2, e = _matmul(f, w_down, out_dtype=(F32, BF16), tm=512, tn=1024, tk=4096, name="mm_down",
                    extra=(h1, g_ple), epilogue=_residual_rms_tile)
    pp = _matmul(p_bf16, w_pp, tb=True, out_dtype=F32, tm=512, tn=1024, tk=256, name="mm_pp")

    dh3, dpp, dgt, dg_final, loss = _matmul(
        e, w_pg, out_dtype=(F32, BF16, BF16), tm=512, tn=1024, tk=1024, name="mm_pg_tail",
        extra=(pp, h2, target, g_final), epilogue=_tail_tile, n_colsum=2)
    loss = loss[:, :128]
    gw_pp = _matmul(p_bf16, dpp, ta=True, transpose_out=True, out_dtype=BF16, tm=256, tn=1024, tk=2048,
                    name="mm_gw_pp")
    gw_pg = _matmul(e, dgt, ta=True, out_dtype=BF16, tm=512, tn=1024, tk=2048, name="mm_gw_pg")
    dh2, dh2_b, dg_ple = _matmul(
        dgt, w_pg, tb=True, out_dtype=(F32, BF16), tm=512, tn=1024, tk=1024, name="mm_de",
        extra=(h2, g_ple, dh3), epilogue=_rms_bwd_twice, n_colsum=1)
    du = _matmul(dh2_b, w_down, tb=True, out_dtype=BF16, tm=512, tn=2048, tk=1024, name="mm_du",
                 extra=(u,), epilogue=lambda acc, uv: (acc * (2.0 * jnp.maximum(uv.astype(F32), 0.0)),))
    gw_down = _matmul(f, dh2_b, ta=True, out_dtype=BF16, tm=1024, tn=1024, tk=2048, name="mm_gw_down")
    gw_up = _matmul(c, du, ta=True, transpose_out=True, out_dtype=BF16, tm=512, tn=2048, tk=2048, name="mm_gw_up")
    dh1, dh1_b, dg_mlp = _matmul(
        du, w_up, out_dtype=(F32, BF16), tm=512, tn=1024, tk=4096, name="mm_dc",
        extra=(h1, g_mlp, dh2), epilogue=_rms_bwd_twice, n_colsum=1)
    dbn, dbd, dgn, dgd, dy_na, do_g, cc_g = _branch_bwd(dh1_b, w_out, gates, bn, bd, w_bna, w_bd, y_dil, d_lse,
                                                        tm=tm, name="branch_bwd")
    gw_out = _matmul(mixed, dh1_b, ta=True, out_dtype=BF16, tm=512, tn=1024, tk=2048, name="mm_gw_out")
    gw_bna = _matmul(y_na, dbn, ta=True, transpose_out=True, out_dtype=BF16, tm=512, tn=1024, tk=2048,
                     name="mm_gw_bna")
    gw_bd = _matmul(y_dil_b, dbd, ta=True, transpose_out=True, out_dtype=BF16, tm=256, tn=1024, tk=2048,
                    name="mm_gw_bd")
    token = send_grads((_W_PP, _W_PG, _W_DOWN, _W_UP, _W_OUT, _W_BNA, _W_BD),
                       (gw_pp, gw_pg, gw_down, gw_up, gw_out, gw_bna, gw_bd))
    dna = _na_bwd(na_qkv, dy_na, rb, name="na_bwd", after=token)
    drpb = _rpb_grad(dna[3].reshape(8, -1), name="rpb_grad")
    ddq, ddk, ddv = [], [], []
    for g in range(3):
        r = _dil_bwd(dq_g[g], dk_g[g], dv_g[g], do_g[g], d_lse[g], cc_g[g], name=f"dil_bwd{g}")
        ddq.append(r[0])
        ddk.append(r[1])
        ddv.append(r[2])
    dproj = _assemble_dproj(dna[0:3], ddq, ddk, ddv, dgn, dgd, cos_t, sin_t, tm=tm, name="assemble_dproj")
    gw_in = _matmul(a, dproj, ta=True, transpose_out=True, out_dtype=BF16, tm=512, tn=2944, tk=2048, name="mm_gw_in")
    token = send_grads((_W_IN,), (gw_in,))
    dx, dg_mix = _matmul(
        dproj, w_in, out_dtype=(F32,), tm=512, tn=1024, tk=5888, name="mm_da", after=token,
        extra=(x, g_mix, dh1), epilogue=_rms_bwd_tile, n_colsum=1)
    return loss, dx, (dg_mix, dg_mlp, dg_ple, dg_final), drpb


def _cast_bf16(t, *, name):
    def body(t_ref, o_ref):
        o_ref[...] = t_ref[...].astype(BF16)

    rows, cols = t.shape
    tr = 256 if rows % 256 == 0 else rows
    blk = pl.BlockSpec((tr, cols), lambda i: (i, 0))
    return pl.pallas_call(body, name=name, grid=(rows // tr,), in_specs=[blk], out_specs=blk,
                          out_shape=_sds(t.shape, BF16), compiler_params=_params("parallel"))(t)


def _adamw(w, g, m, v):
    m = ADAM_B1 * m + (1.0 - ADAM_B1) * g
    v = ADAM_B2 * v + (1.0 - ADAM_B2) * (g * g)
    m_hat = m / (1.0 - ADAM_B1 ** ADAM_STEP)
    v_hat = v / (1.0 - ADAM_B2 ** ADAM_STEP)
    delta = -ADAM_LR * (m_hat / (jnp.sqrt(v_hat) + ADAM_EPS) + ADAM_WD * w)
    return delta, m, v


def _sum_adamw(parts, w, m, v, *, tr, name, own=None, transposed=False):
    rows, cols = w.shape
    n_pre = 0 if own is None else 1

    def body(*refs):
        p_ref, w_ref, m_ref, v_ref = refs[n_pre:n_pre + 4]
        g_ref, d_ref, nm_ref, nv_ref = refs[-4:]
        g = (p_ref[0] if own is None else refs[n_pre + 4][...]).astype(F32)
        for s in range(1, N_DEV):
            g = g + p_ref[s].astype(F32)
        if transposed:
            g = g.T
        g_ref[...] = g
        d_ref[...], nm_ref[...], nv_ref[...] = _adamw(w_ref[...], g, m_ref[...], v_ref[...])

    if transposed:
        blk = pl.BlockSpec((rows, tr), lambda i, *_: (0, i))
        g_rows, steps = rows, cols // tr
    else:
        blk = pl.BlockSpec((tr, cols), lambda i, *_: (i, 0))
        g_rows, steps = cols, rows // tr
    in_specs = [pl.BlockSpec((N_DEV, tr, g_rows), lambda i, *_: (0, i, 0)), blk, blk, blk]
    args = [parts, w, m, v]
    if own is not None:
        in_specs.append(pl.BlockSpec((None, tr, g_rows), lambda i, idx: (idx[0], i, 0)))
        args = [own[1]] + args + [own[0]]
    return pl.pallas_call(
        body, name=name,
        grid_spec=pltpu.PrefetchScalarGridSpec(num_scalar_prefetch=n_pre, grid=(steps,), in_specs=in_specs,
                                               out_specs=[blk] * 4),
        out_shape=[_sds((rows, cols), F32)] * 4,
        compiler_params=_params("parallel"),
    )(*args)


_RPB_SIZE = 8 * 15 * 31


def _pack_small(g_mix, g_mlp, g_ple, g_final, rpb, loss_row):
    flat = jnp.concatenate([g_mix.reshape(-1), g_mlp.reshape(-1), g_ple.reshape(-1), g_final.reshape(-1),
                            rpb.reshape(-1), jnp.zeros((3840 - _RPB_SIZE,), F32), loss_row.reshape(-1),
                            jnp.zeros((128,), F32)])
    return flat.reshape(64, 128)


def _unpack_small(t):
    flat = t.reshape(-1)
    return (flat[0:1024].reshape(1, 1024), flat[4096:4096 + _RPB_SIZE].reshape(1, 8, 15, 31),
            flat[1024:2048].reshape(1, 1024), flat[2048:3072].reshape(1, 1024), flat[3072:4096])


def kernel(x, p, positions, g_mix, w_in, rpb, w_branch_na, w_branch_dil, w_out, g_mlp, w_up, w_down, g_ple, w_ple_gate, w_ple_proj, g_final, loss_target, m_g_mix, m_w_in, m_rpb, m_w_branch_na, m_w_branch_dil, m_w_out, m_g_mlp, m_w_up, m_w_down, m_g_ple, m_w_ple_gate, m_w_ple_proj, m_g_final, v_g_mix, v_w_in, v_rpb, v_w_branch_na, v_w_branch_dil, v_w_out, v_g_mlp, v_w_up, v_w_down, v_g_ple, v_w_ple_gate, v_w_ple_proj, v_g_final):
    sharded = dict(w_in=(w_in, m_w_in, v_w_in), w_branch_na=(w_branch_na, m_w_branch_na, v_w_branch_na),
                   w_branch_dil=(w_branch_dil, m_w_branch_dil, v_w_branch_dil), w_out=(w_out, m_w_out, v_w_out),
                   w_up=(w_up, m_w_up, v_w_up), w_down=(w_down, m_w_down, v_w_down),
                   w_ple_gate=(w_ple_gate, m_w_ple_gate, v_w_ple_gate),
                   w_ple_proj=(w_ple_proj, m_w_ple_proj, v_w_ple_proj))
    shards = {k: tuple(t[0] for t in val) for k, val in sharded.items()}

    me = _my_index()

    shards["w_in"] = tuple(t.T for t in shards["w_in"])

    w_in_b = _cast_bf16(shards["w_in"][0], name="cast_w_in")
    rest_b = [shards[name][0].astype(BF16).T if axis == 1 else shards[name][0].astype(BF16)
              for name, axis, _ in _WEIGHTS[1:]]
    first_in, token_in = _start_copies(_first_leg_copies, [w_in_b], [_sds((N_DEV,) + w_in_b.shape, BF16)], 4,
                                       name="start_gather_w_in")

    def whole(landed, mine):
        return _to_full(lax.dynamic_update_index_in_dim(landed, mine, me, 0))

    rest = {}

    def get_w_in(after):
        (mine,), landed = _wait_copies(_first_leg_copies, first_in, after, name="wait_gather_w_in")
        second, token = _start_copies(_second_leg_copies, [], landed, 3, name="start_forward_w_in")
        _, (landed,) = _wait_copies(_second_leg_copies, second, token, name="wait_forward_w_in")
        rest["first"], token = _start_copies(_first_leg_copies, rest_b,
                                             [_sds((N_DEV,) + t.shape, BF16) for t in rest_b], 4 * len(rest_b),
                                             name="start_gather_rest", after=landed)
        return whole(landed, mine), token

    def relay_rest(after):
        rest["mine"], landed = _wait_copies(_first_leg_copies, rest["first"], after, name="wait_gather_rest")
        rest["second"], token = _start_copies(_second_leg_copies, [], landed, 3 * len(rest_b),
                                              name="start_forward_rest")
        return token

    def get_rest(after, stage):
        n_src, send_sems, recv_sems, bufs = rest["second"]
        part = slice(0, 2) if stage == 0 else slice(2, len(rest_b))
        _, landed = _wait_copies(functools.partial(_second_leg_copies, first=part.start),
                                 (n_src, send_sems, recv_sems, bufs[part]), after,
                                 name=f"wait_forward_rest{stage}")
        return [whole(t, own) for t, own in zip(landed, rest["mine"][part])]

    sent = []

    def send_grads(indices, grads):
        chunked = [_to_chunks(i, g) for i, g in zip(indices, grads)]
        handle, token = _start_copies(_exchange_copies, chunked, [_sds(t.shape, BF16) for t in chunked],
                                      7 * len(chunked),
                                      name="start_exchange_" + ("w_in" if indices == (_W_IN,) else "rest"))
        sent.append((indices, handle))
        return token

    g_mix_0 = g_mix + token_in[0:1, 0:1]
    loss, dx, dgs, drpb = _local_step(
        x[0], p[0, 0].astype(BF16), positions[0], loss_target[0],
        g_mix_0, g_mlp, g_ple, g_final.reshape(1, -1), rpb[0], get_w_in, relay_rest, get_rest, send_grads)

    drpb3 = drpb.reshape(8, 16, 32)[:, :15, :31]
    small = _pack_small(dgs[0], dgs[1], dgs[2], dgs[3], drpb3, loss)
    share, done = _start_copies(_gather_copies, [small], [_sds((N_DEV,) + small.shape, F32)], 7,
                                name="start_share_small")

    out = {}
    for indices, handle in sent:
        chunked, landed = _wait_copies(_exchange_copies, handle, done,
                                       name="wait_exchange_" + ("w_in" if indices == (_W_IN,) else "rest"))
        for i, part, mine in zip(indices, landed, chunked):
            name = _WEIGHTS[i][0]
            w, m, v = shards[name]
            turned = _WEIGHTS[i][1] == 1 and i != _W_IN
            res = _sum_adamw(part, w, m, v, tr=368 if i == _W_IN else 128, name="adamw_" + name,
                             own=(mine, me.reshape(1).astype(jnp.int32)), transposed=turned)
            out[name] = [(t.T if i == _W_IN else t)[None] for t in res]
            done = res[0]
    (small,), (small_landed,) = _wait_copies(_gather_copies, share, done, name="wait_share_small")
    small_all = lax.dynamic_update_index_in_dim(small_landed, small, me, 0)
    small_w = _pack_small(g_mix, g_mlp, g_ple, g_final, rpb, jnp.zeros((128,), F32))
    small_m = _pack_small(m_g_mix, m_g_mlp, m_g_ple, m_g_final, m_rpb, jnp.zeros((128,), F32))
    small_v = _pack_small(v_g_mix, v_g_mlp, v_g_ple, v_g_final, v_rpb, jnp.zeros((128,), F32))
    res = _sum_adamw(small_all, small_w, small_m, small_v, tr=64, name="adamw_small")
    unpacked = [_unpack_small(t) for t in res]
    for i, name in enumerate(("g_mix", "rpb", "g_mlp", "g_ple", "g_final")):
        out[name] = [u[i] for u in unpacked]
    loss_total = res[0][62, 0]

    order = ("g_mix", "w_in", "rpb", "w_branch_na", "w_branch_dil", "w_out", "g_mlp", "w_up", "w_down",
             "g_ple", "w_ple_gate", "w_ple_proj", "g_final")
    grads = [out[k][0] for k in order]
    deltas = [out[k][1] for k in order]
    new_m = [out[k][2] for k in order]
    new_v = [out[k][3] for k in order]
    return (loss_total, dx[None], *grads, *deltas, *new_m, *new_v)
```

```python
import functools

import jax
import jax.numpy as jnp
from jax import lax
from jax.experimental import pallas as pl
from jax.experimental.pallas import tpu as pltpu

F32 = jnp.float32
BF16 = jnp.bfloat16

D_MODEL = 1024
HEAD_DIM = 64
GRID_W = 64
NA_WIDTH = 512
DIL_WIDTH = 768
IN_WIDTH = 5888
DIL_DILATIONS = (1, 4, 16)
DIL_RADIUS = 64
NA_WIN_ROWS = 8
RMS_EPS = 1e-6
NEG_INF = -1e30
QK_SCALE = HEAD_DIM ** -0.5

ADAM_LR = 0.001
ADAM_B1 = 0.9
ADAM_B2 = 0.999
ADAM_EPS = 1e-08
ADAM_WD = 0.01
ADAM_STEP = 10

N_DEV = 8
VMEM_LIMIT = 56 * 1024 * 1024
EPILOGUE_ROWS = 256
MESH = pl.DeviceIdType.MESH

NT_DIMS = (((1,), (1,)), ((), ()))
TN_DIMS = (((0,), (0,)), ((), ()))


def _sds(shape, dtype):
    return jax.ShapeDtypeStruct(shape, dtype)


def _params(*sem):
    return pltpu.CompilerParams(dimension_semantics=sem, vmem_limit_bytes=VMEM_LIMIT)


def _rows(tm, width, col=0):
    return pl.BlockSpec((tm, width), lambda i, c=col: (i, c))


def _const(shape):
    zeros = (0,) * len(shape)
    return pl.BlockSpec(shape, lambda i: zeros)


def _my_index():
    return 4 * lax.axis_index("x") + 2 * lax.axis_index("y") + lax.axis_index("c")


def _peer(k):
    x, y, c = lax.axis_index("x"), lax.axis_index("y"), lax.axis_index("c")
    px = 1 - x if k & 4 else x
    py = 1 - y if k & 2 else y
    pc = 1 - c if k & 1 else c
    return (px, py, pc), 4 * px + 2 * py + pc


def _call(body, *, name, grid, in_specs, out_specs, out_shape, scratch_shapes, args, after=None):
    n_in, n_out = len(in_specs), len(out_specs)
    extra = [] if after is None else [after]
    n_x = n_in + len(extra)

    def plain(*refs):
        body(refs[:n_in], refs[n_x:n_x + n_out], refs[n_x + n_out:])

    res = pl.pallas_call(plain, name=name, grid=grid,
                         in_specs=list(in_specs) + [pl.BlockSpec(memory_space=pl.ANY)] * len(extra),
                         out_specs=out_specs, out_shape=out_shape, scratch_shapes=scratch_shapes,
                         compiler_params=_params(*(("arbitrary",) * len(grid))))(*args, *extra)
    return list(res)


_HBM_SPEC = pl.BlockSpec(memory_space=pltpu.HBM)
_SEM_SPEC = pl.BlockSpec(memory_space=pltpu.SEMAPHORE)
_SIDE_EFFECT = pltpu.SideEffectType.DATAFLOW_SIDE_EFFECTING


_FIRST_LEG = (1, 2, 4, 6)
_SECOND_LEG = (2, 4, 6)


def _gather_copies(srcs, lands, send, recv, sending):
    me = _my_index()
    out = []
    for w in range(len(srcs)):
        for k in range(1, N_DEV):
            dev, idx = _peer(k)
            out.append(pltpu.make_async_remote_copy(
                src_ref=srcs[w], dst_ref=lands[w].at[me if sending else idx],
                send_sem=send.at[w * 7 + k - 1], recv_sem=recv.at[w * 7 + k - 1],
                device_id=dev, device_id_type=MESH))
    return out


def _first_leg_copies(srcs, lands, send, recv, sending):
    me = _my_index()
    out = []
    for w in range(len(srcs)):
        for j, k in enumerate(_FIRST_LEG):
            dev, idx = _peer(k)
            out.append(pltpu.make_async_remote_copy(
                src_ref=srcs[w], dst_ref=lands[w].at[me if sending else idx],
                send_sem=send.at[w * 4 + j], recv_sem=recv.at[w * 4 + j],
                device_id=dev, device_id_type=MESH))
    return out


def _second_leg_copies(srcs, lands, send, recv, sending, first=0):
    sibling, _ = _peer(1)
    out = []
    for w in range(len(lands)):
        for j, k in enumerate(_SECOND_LEG):
            slot = _peer(k if sending else k ^ 1)[1]
            sem = (first + w) * 3 + j
            out.append(pltpu.make_async_remote_copy(
                src_ref=lands[w].at[slot], dst_ref=lands[w].at[slot],
                send_sem=send.at[sem], recv_sem=recv.at[sem],
                device_id=sibling, device_id_type=MESH))
    return out


def _exchange_copies(srcs, lands, send, recv, sending):
    out = []
    for w in range(len(srcs)):
        for k in range(1, N_DEV):
            dev, idx = _peer(k)
            out.append(pltpu.make_async_remote_copy(
                src_ref=srcs[w].at[idx], dst_ref=lands[w].at[k],
                send_sem=send.at[w * 7 + k - 1], recv_sem=recv.at[w * 7 + k - 1],
                device_id=dev, device_id_type=MESH))
    return out


def _start_copies(make, srcs, lands, n_copies, *, name, after=None):
    n_src, n_buf = len(srcs), len(srcs) + len(lands)
    extra = [] if after is None else [after]

    def body(*refs):
        send, recv = refs[n_buf + len(extra)], refs[n_buf + len(extra) + 1]
        for cp in make(refs[:n_src], refs[n_src:n_buf], send, recv, True):
            cp.start()
        refs[-1][...] = jnp.zeros_like(refs[-1])

    bufs = list(srcs) + [lax.empty(t.shape, t.dtype) if isinstance(t, jax.ShapeDtypeStruct) else t for t in lands]
    res = pl.pallas_call(
        body, name=name,
        out_shape=(pltpu.SemaphoreType.DMA((n_copies,)), pltpu.SemaphoreType.DMA((n_copies,)),
                   *[pltpu.HBM(t.shape, t.dtype) for t in bufs], _sds((8, 128), F32)),
        in_specs=[_HBM_SPEC] * n_buf + [pl.BlockSpec(memory_space=pl.ANY)] * len(extra),
        out_specs=(_SEM_SPEC, _SEM_SPEC, *([_HBM_SPEC] * n_buf), pl.BlockSpec(memory_space=pltpu.VMEM)),
        input_output_aliases={i: 2 + i for i in range(n_buf)},
        compiler_params=pltpu.CompilerParams(has_side_effects=_SIDE_EFFECT),
    )(*[pltpu.with_memory_space_constraint(t, pltpu.HBM) for t in bufs], *extra)
    return (n_src, res[0], res[1], res[2:2 + n_buf]), res[-1]


def _wait_copies(make, handle, after, *, name):
    n_src, send_sems, recv_sems, bufs = handle
    n_buf = len(bufs)

    def body(*refs):
        for cp in make(refs[:n_src], refs[n_src:n_buf], refs[n_buf], refs[n_buf + 1], False):
            cp.wait_send()
            cp.wait_recv()

    res = pl.pallas_call(
        body, name=name,
        out_shape=tuple(pltpu.HBM(t.shape, t.dtype) for t in bufs),
        in_specs=[_HBM_SPEC] * n_buf + [_SEM_SPEC, _SEM_SPEC, pl.BlockSpec(memory_space=pl.ANY)],
        out_specs=tuple([_HBM_SPEC] * n_buf),
        input_output_aliases={i: i for i in range(n_buf)},
        compiler_params=pltpu.CompilerParams(has_side_effects=_SIDE_EFFECT),
    )(*bufs, send_sems, recv_sems, after)
    return list(res[:n_src]), list(res[n_src:])


def _add_colsums(s_refs, sums, step):
    for s_ref, val in zip(s_refs, sums):
        @pl.when(step == 0)
        def _(s_ref=s_ref, val=val):
            s_ref[...] = val

        @pl.when(step > 0)
        def _(s_ref=s_ref, val=val):
            s_ref[...] += val


def _matmul(a, b, *, ta=False, tb=False, out_dtype, tm, tn, tk, name, after=None, extra=(), epilogue=None,
            n_colsum=0, transpose_out=False):
    m, k = (a.shape[1], a.shape[0]) if ta else a.shape
    n = b.shape[0] if tb else b.shape[1]
    tm, tn, tk = min(tm, m), min(tn, n), min(tk, k)
    nk = k // tk
    dims = (((0 if ta else 1,), (1 if tb else 0,)), ((), ()))
    out_dtypes = out_dtype if isinstance(out_dtype, tuple) else (out_dtype,)
    n_tiles = len(out_dtypes)

    def add_colsums(o_refs, sums):
        _add_colsums(o_refs[n_tiles:], sums, pl.program_id(1))

    def finish(acc, x_refs, o_refs):
        vals = (acc,) if epilogue is None else epilogue(acc, *[r[...] for r in x_refs])
        for o_ref, val in zip(o_refs[:n_tiles], vals[:n_tiles]):
            o_ref[...] = (val.T if transpose_out else val).astype(o_ref.dtype)
        add_colsums(o_refs, vals[n_tiles:])

    chunk = EPILOGUE_ROWS if (nk == 1 and epilogue is not None and not ta and tm % EPILOGUE_ROWS == 0) else None

    def body(ins, outs, acc):
        a_ref, b_ref = ins[:2]
        if chunk is not None:
            sums = None
            for r0 in range(0, tm, chunk):
                part = lax.dot_general(a_ref[r0:r0 + chunk, :], b_ref[...], dims, preferred_element_type=F32)
                vals = epilogue(part, *[r[...] if r.shape[0] == 1 else r[r0:r0 + chunk, :] for r in ins[2:]])
                for o_ref, val in zip(outs[:n_tiles], vals[:n_tiles]):
                    o_ref[r0:r0 + chunk, :] = val.astype(o_ref.dtype)
                sums = vals[n_tiles:] if sums is None else [s + v for s, v in zip(sums, vals[n_tiles:])]
            add_colsums(outs, sums)
            return
        part = lax.dot_general(a_ref[...], b_ref[...], dims, preferred_element_type=F32)
        if nk == 1:
            finish(part, ins[2:], outs)
            return
        acc_ref, = acc
        kk = pl.program_id(2)

        @pl.when(kk == 0)
        def _():
            acc_ref[...] = part

        @pl.when(kk > 0)
        def _():
            acc_ref[...] += part

        @pl.when(kk == nk - 1)
        def _():
            finish(acc_ref[...], ins[2:], outs)

    a_spec = (pl.BlockSpec((tk, tm), lambda j, i, kk: (kk, i)) if ta
              else pl.BlockSpec((tm, tk), lambda j, i, kk: (i, kk)))
    b_spec = (pl.BlockSpec((tn, tk), lambda j, i, kk: (j, kk)) if tb
              else pl.BlockSpec((tk, tn), lambda j, i, kk: (kk, j)))
    tile = pl.BlockSpec((tm, tn), lambda j, i, kk: (i, j))
    row = pl.BlockSpec((1, tn), lambda j, i, kk: (0, j))

    out_tile, out_dims = (pl.BlockSpec((tn, tm), lambda j, i, kk: (j, i)), (n, m)) if transpose_out else (tile, (m, n))
    res = _call(
        body, name=name, grid=(n // tn, m // tm, nk),
        in_specs=[a_spec, b_spec] + [row if t.shape[0] == 1 else tile for t in extra],
        out_specs=[out_tile] * n_tiles + [row] * n_colsum,
        out_shape=[_sds(out_dims, dt) for dt in out_dtypes] + [_sds((1, n), F32)] * n_colsum,
        scratch_shapes=[] if nk == 1 else [pltpu.VMEM((tm, tn), F32)],
        args=(a, b, *extra), after=after)
    return res if isinstance(out_dtype, tuple) or n_colsum else res[0]


def _rstd(h):
    return lax.rsqrt(jnp.mean(h * h, axis=-1, keepdims=True) + RMS_EPS)


def _sigmoid(z):
    return 1.0 / (1.0 + jnp.exp(-z))


def _rms_fwd(x, g, *, tm, name):
    n = x.shape[0]

    def body(x_ref, g_ref, o_ref):
        h = x_ref[...]
        o_ref[...] = (h * _rstd(h) * g_ref[...]).astype(BF16)

    return pl.pallas_call(
        body, name=name, grid=(n // tm,),
        in_specs=[_rows(tm, D_MODEL), _const((1, D_MODEL))],
        out_specs=_rows(tm, D_MODEL), out_shape=_sds((n, D_MODEL), BF16),
        compiler_params=_params("parallel"),
    )(x, g)


def _swap_halves(t):
    lane = lax.broadcasted_iota(jnp.int32, (t.shape[0], 128), 1)
    pieces = [t[:, c:c + 128] for c in range(0, t.shape[1], 128)]
    return jnp.concatenate([jnp.where((lane & 63) < 32, pltpu.roll(h, 96, 1), pltpu.roll(h, 32, 1))
                            for h in pieces], axis=1)


def _dil_spec(dil, tm):
    return pl.BlockSpec((dil, tm // dil, 256), lambda i: (0, i, 0))


def _dil_scratch(tm):
    return pltpu.VMEM((2, tm, 128), F32)


def _load_token_order(src, scr, dil, rows, row0=0):
    if dil == 1:
        return src[0, row0:row0 + rows, :]
    for j in range(dil):
        for c in range(2):
            scr[c, pl.ds(j, rows // dil, stride=dil), :] = (
                src[j, row0 // dil:(row0 + rows) // dil, c * 128:(c + 1) * 128])
    return jnp.concatenate([scr[0, 0:rows, :], scr[1, 0:rows, :]], axis=1)


def _store_dil_order(val, dst, scr, dil, row0=0):
    rows = val.shape[0]
    if dil == 1:
        dst[0, row0:row0 + rows, :] = val.astype(dst.dtype)
        return
    for c in range(2):
        scr[c] = val[:, c * 128:(c + 1) * 128]
    for j in range(dil):
        for c in range(2):
            dst[j, row0 // dil:(row0 + rows) // dil, c * 128:(c + 1) * 128] = (
                scr[c, pl.ds(j, rows // dil, stride=dil), :].astype(dst.dtype))


def _project_in(a, w_t, cos_t, sin_t, *, tm, name, after=None):
    n = a.shape[0]
    n_dil = len(DIL_DILATIONS)
    na_w, dil_w = 3 * NA_WIDTH, 3 * DIL_WIDTH
    chunk = min(EPILOGUE_ROWS, tm)
    extra = [] if after is None else [after]

    def body(a_ref, w_ref, cos_ref, sin_ref, *rest):
        na_ref, gate_ref = rest[len(extra):len(extra) + 2]
        outs, scr = rest[len(extra) + 2:len(extra) + 2 + 3 * n_dil], rest[-1]

        def part(r0, first, width):
            return lax.dot_general(a_ref[r0:r0 + chunk, :], w_ref[first:first + width, :], NT_DIMS,
                                   preferred_element_type=F32)

        for r0 in range(0, tm, chunk):
            na_ref[r0:r0 + chunk, :] = part(r0, 0, na_w).astype(BF16)
            dil_part = part(r0, na_w, dil_w)
            cosv, sinv = cos_ref[r0:r0 + chunk, :], sin_ref[r0:r0 + chunk, :]
            for t in range(3):
                for gi, dil in enumerate(DIL_DILATIONS):
                    c0 = (t * n_dil + gi) * 256
                    val = dil_part[:, c0:c0 + 256]
                    if t < 2:
                        val = val * cosv + _swap_halves(val) * sinv
                    _store_dil_order(val, outs[t * n_dil + gi], scr, dil, r0)
            gate_ref[r0:r0 + chunk, :] = _sigmoid(part(r0, na_w + dil_w, 2 * D_MODEL)).astype(BF16)

    out_specs = [_rows(tm, na_w), _rows(tm, 2 * D_MODEL)]
    out_shape = [_sds((n, na_w), BF16), _sds((n, 2 * D_MODEL), BF16)]
    for _ in range(3):
        for dil in DIL_DILATIONS:
            out_specs.append(pl.BlockSpec((dil, tm // dil, 256), lambda i: (0, i, 0)))
            out_shape.append(_sds((dil, n // dil, 256), BF16))
    res = pl.pallas_call(
        body, name=name, grid=(n // tm,),
        in_specs=[_rows(tm, D_MODEL), _const(w_t.shape), _rows(tm, 256), _rows(tm, 256)]
                 + [pl.BlockSpec(memory_space=pl.ANY)] * len(extra),
        out_specs=out_specs, out_shape=out_shape,
        scratch_shapes=[pltpu.VMEM((2, chunk, 128), F32)],
        compiler_params=_params("parallel"),
    )(a, w_t, cos_t, sin_t, *extra)
    return res[0], res[1], res[2:5], res[5:8], res[8:11]


def _residual_rms_tile(delta, h, g):
    hn = h + delta
    return hn, hn * _rstd(hn) * g


def _gate_mix_tile(b2, s1, b1, s2):
    return b2, s1.astype(F32) * b1.astype(F32) + s2.astype(F32) * b2


def _gate_bwd_tile(dm, s1, b1, s2, b2):
    s1, b1, s2, b2 = (t.astype(F32) for t in (s1, b1, s2, b2))
    return dm * s1, dm * s2, dm * b1 * s1 * (1.0 - s1), dm * b2 * s2 * (1.0 - s2)


def _tail_tile(gt, pp, h2, target, g):
    sg = _sigmoid(gt)
    h3 = h2 + sg * pp
    r3 = _rstd(h3)
    n3 = h3 * r3
    err = n3 * g - target
    loss = 0.5 * jnp.sum(jnp.sum(err * err, axis=-1, keepdims=True) / D_MODEL)
    dy = err / D_MODEL
    dn = dy * g
    dh3 = r3 * (dn - n3 * jnp.mean(dn * n3, axis=-1, keepdims=True))
    return (dh3, dh3 * sg, dh3 * pp * sg * (1.0 - sg),
            jnp.sum(dy * n3, axis=0, keepdims=True), jnp.full((1, gt.shape[1]), loss, F32))


def _rms_bwd_tile(dz, h, g, dres):
    r = _rstd(h)
    nrm = h * r
    dn = dz * g
    dh = dres + r * (dn - nrm * jnp.mean(dn * nrm, axis=-1, keepdims=True))
    return dh, jnp.sum(dz * nrm, axis=0, keepdims=True)


def _rms_bwd_twice(dz, h, g, dres):
    dh, dg = _rms_bwd_tile(dz, h, g, dres)
    return dh, dh, dg


def _tail_step(e, w_pg, p, w_pp, h2, target, g_final, g_ple, *, tm, name):
    n = e.shape[0]
    chunk = min(EPILOGUE_ROWS, tm)

    def body(e_ref, wg_ref, p_ref, wp_ref, h2_ref, t_ref, gf_ref, gp_ref,
             dpp_ref, dgt_ref, dh2_ref, dh2b_ref, dgf_ref, loss_ref, dgp_ref):
        sums = None
        for r0 in range(0, tm, chunk):
            rows = slice(r0, r0 + chunk)
            gt = jnp.dot(e_ref[rows, :], wg_ref[...], preferred_element_type=F32)
            pp = lax.dot_general(p_ref[rows, :], wp_ref[...], NT_DIMS, preferred_element_type=F32)
            h2 = h2_ref[rows, :]
            dh3, dpp, dgt, dgf, loss = _tail_tile(gt, pp, h2, t_ref[rows, :], gf_ref[...])
            dgt = dgt.astype(BF16)
            dpp_ref[rows, :] = dpp.astype(BF16)
            dgt_ref[rows, :] = dgt
            dz = lax.dot_general(dgt, wg_ref[...], NT_DIMS, preferred_element_type=F32)
            dh2, dgp = _rms_bwd_tile(dz, h2, gp_ref[...], dh3)
            dh2_ref[rows, :] = dh2
            dh2b_ref[rows, :] = dh2.astype(BF16)
            vals = (dgf, loss, dgp)
            sums = vals if sums is None else [s + v for s, v in zip(sums, vals)]
        _add_colsums((dgf_ref, loss_ref, dgp_ref), sums, pl.program_id(0))

    wide, gain = _rows(tm, D_MODEL), _const((1, D_MODEL))
    return pl.pallas_call(
        body, name=name, grid=(n // tm,),
        in_specs=[wide, _const(w_pg.shape), _rows(tm, p.shape[1]), _const(w_pp.shape), wide, wide, gain, gain],
        out_specs=[wide] * 4 + [gain] * 3,
        out_shape=[_sds((n, D_MODEL), dt) for dt in (BF16, BF16, F32, BF16)] + [_sds((1, D_MODEL), F32)] * 3,
        compiler_params=_params("arbitrary"),
    )(e, w_pg, p, w_pp, h2, target, g_final, g_ple)


def _assemble_dproj(dna, ddil_q, ddil_k, ddil_v, dgn, dgd, cos_t, sin_t, *, tm, name):
    n = dgn.shape[0]

    def body(*refs):
        dq_ref, dk_ref, dv_ref = refs[0:3]
        dil_in = refs[3:12]
        dgn_ref, dgd_ref, cos_ref, sin_ref, o_ref, scr = refs[12:18]
        o_ref[:, 0:512] = dq_ref[...]
        o_ref[:, 512:1024] = dk_ref[...].astype(BF16)
        o_ref[:, 1024:1536] = dv_ref[...].astype(BF16)
        cosv, sinv = cos_ref[...], sin_ref[...]
        for t in range(3):
            for gi, dil in enumerate(DIL_DILATIONS):
                val = _load_token_order(dil_in[t * 3 + gi], scr, dil, tm)
                if t < 2:
                    val = val * cosv + _swap_halves(val * sinv)
                c0 = 1536 + t * DIL_WIDTH + gi * 256
                o_ref[:, c0:c0 + 256] = val.astype(BF16)
        o_ref[:, 3840:4864] = dgn_ref[...]
        o_ref[:, 4864:5888] = dgd_ref[...]

    in_specs = [_rows(tm, NA_WIDTH)] * 3
    for _ in range(3):
        for dil in DIL_DILATIONS:
            in_specs.append(pl.BlockSpec((dil, tm // dil, 256), lambda i: (0, i, 0)))
    in_specs += [_rows(tm, D_MODEL)] * 2 + [_rows(tm, 256)] * 2
    return pl.pallas_call(
        body, name=name, grid=(n // tm,), in_specs=in_specs,
        out_specs=_rows(tm, IN_WIDTH), out_shape=_sds((n, IN_WIDTH), BF16),
        scratch_shapes=[_dil_scratch(tm)],
        compiler_params=_params("parallel"),
    )(*dna, *ddil_q, *ddil_k, *ddil_v, dgn, dgd, cos_t, sin_t)


N_ROW_OFF = 2 * NA_WIN_ROWS - 1
N_PAIRS = N_ROW_OFF - 1
RB_WIDTH = (N_ROW_OFF + 1) * GRID_W


def _na_bias(rb_ref, pair_scr):
    shape = (GRID_W, RB_WIDTH)
    qc = lax.broadcasted_iota(jnp.int32, shape, 0)
    qc2 = lax.broadcasted_iota(jnp.int32, (GRID_W, 128), 0)
    kc2 = lax.broadcasted_iota(jnp.int32, (GRID_W, 128), 1) & (GRID_W - 1)
    cs = jnp.clip(qc2 - 8, 0, GRID_W - 16)
    valid = (kc2 >= cs) & (kc2 < cs + 16)
    for hh in range(2):
        t = jnp.broadcast_to(rb_ref[hh], shape)
        t = pltpu.roll(t, RB_WIDTH - 15, 1)
        for b in range(6):
            t = jnp.where(((qc >> b) & 1) == 1, pltpu.roll(t, 1 << b, 1), t)
        t_odd = pltpu.roll(t, RB_WIDTH - GRID_W, 1)
        for ro in range(N_PAIRS):
            src = t if ro % 2 == 0 else t_odd
            base = (ro // 2) * 128
            pair_scr[hh, ro] = jnp.where(valid, src[:, base:base + 128], NEG_INF)


NA_GROUP_FWD = 8
NA_GROUP_BWD = 4


def _stack_heads(ref, r, scale=1.0):
    lane = lax.broadcasted_iota(jnp.int32, (GRID_W, 128), 1)
    t = ref[pl.ds(pl.multiple_of(r * GRID_W, GRID_W), GRID_W), :].astype(F32) * scale
    return jnp.concatenate([jnp.where(lane < 64, t, 0.0), jnp.where(lane >= 64, t, 0.0)], axis=0).astype(BF16)


def _unstack_heads(t2):
    lane = lax.broadcasted_iota(jnp.int32, (GRID_W, 128), 1)
    return jnp.where(lane < 64, t2[:GRID_W], t2[GRID_W:])


def _na_window(k_ref, v_ref, r, n_rows):
    rs = jnp.clip(r - NA_WIN_ROWS // 2, 0, n_rows - NA_WIN_ROWS)
    ro0 = (NA_WIN_ROWS - 1) - (r - rs)
    off = pl.multiple_of(rs * GRID_W, GRID_W)
    kw = k_ref[pl.ds(off, NA_WIN_ROWS * GRID_W), :]
    vw = v_ref[pl.ds(off, NA_WIN_ROWS * GRID_W), :]
    return kw, vw, off, ro0


def _na_probs(s_raw, pair_scr, ro0):
    bias = [jnp.concatenate([pair_scr[hh, ro0 + 2 * j] for j in range(NA_WIN_ROWS // 2)], axis=1)
            for hh in range(2)]
    s = s_raw + jnp.concatenate(bias, axis=0)
    m = jnp.max(s, axis=-1, keepdims=True)
    e = jnp.exp(s - m)
    return e * (1.0 / jnp.sum(e, axis=-1, keepdims=True))


def _na_qkv_specs(n):
    pairs = NA_WIDTH // 128
    return [pl.BlockSpec((n, 128), lambda h, first=t * pairs: (0, first + h)) for t in range(3)]


def _na_fwd(qkv, rb, *, name):
    n = qkv.shape[0]
    n_rows = n // GRID_W

    def body(ins, outs, scr):
        q_ref, k_ref, v_ref, rb_ref = ins
        o_ref, = outs
        pair_scr, = scr
        _na_bias(rb_ref, pair_scr)

        def group(g, carry):
            rows = [g * NA_GROUP_FWD + t for t in range(NA_GROUP_FWD)]
            wins = [_na_window(k_ref, v_ref, r, n_rows) for r in rows]
            raw = [lax.dot_general(_stack_heads(q_ref, r, QK_SCALE), w[0], NT_DIMS, preferred_element_type=F32)
                   for r, w in zip(rows, wins)]
            probs = [_na_probs(s, pair_scr, w[3]) for s, w in zip(raw, wins)]
            outs2 = [jnp.dot(p.astype(BF16), w[1], preferred_element_type=F32) for p, w in zip(probs, wins)]
            for r, o2 in zip(rows, outs2):
                o_ref[pl.ds(pl.multiple_of(r * GRID_W, GRID_W), GRID_W), :] = _unstack_heads(o2).astype(BF16)
            return carry

        lax.fori_loop(0, n_rows // NA_GROUP_FWD, group, 0)

    col = pl.BlockSpec((n, 128), lambda h: (0, h))
    return _call(
        body, name=name, grid=(NA_WIDTH // 128,),
        in_specs=_na_qkv_specs(n) + [pl.BlockSpec((2, 1, RB_WIDTH), lambda h: (h, 0, 0))],
        out_specs=[col], out_shape=[_sds((n, NA_WIDTH), BF16)],
        scratch_shapes=[pltpu.VMEM((2, N_PAIRS, GRID_W, 128), F32)],
        args=(qkv, qkv, qkv, rb))[0]


def _na_bwd(qkv, do, rb, *, name, after=None):
    n = qkv.shape[0]
    n_rows = n // GRID_W
    win = NA_WIN_ROWS * GRID_W

    def body(ins, outs, scr):
        q_ref, k_ref, v_ref, do_ref, rb_ref = ins
        dq_ref, dk_ref, dv_ref, drb_ref = outs
        pair_scr, acc_scr = scr
        _na_bias(rb_ref, pair_scr)
        acc_scr[...] = jnp.zeros_like(acc_scr)
        dk_ref[...] = jnp.zeros_like(dk_ref)
        dv_ref[...] = jnp.zeros_like(dv_ref)

        def group(g, carry):
            rows = [g * NA_GROUP_BWD + t for t in range(NA_GROUP_BWD)]
            wins = [_na_window(k_ref, v_ref, r, n_rows) for r in rows]
            qss = [_stack_heads(q_ref, r, QK_SCALE) for r in rows]
            doss = [_stack_heads(do_ref, r) for r in rows]
            raw = [lax.dot_general(qs, w[0], NT_DIMS, preferred_element_type=F32) for qs, w in zip(qss, wins)]
            dps = [lax.dot_general(dos, w[1], NT_DIMS, preferred_element_type=F32) for dos, w in zip(doss, wins)]
            probs = [_na_probs(s, pair_scr, w[3]) for s, w in zip(raw, wins)]
            dss = [p * (dp - jnp.sum(p * dp, axis=-1, keepdims=True)) for p, dp in zip(probs, dps)]
            dsbs = [ds.astype(BF16) for ds in dss]
            dq2s = [jnp.dot(dsb, w[0], preferred_element_type=F32) for dsb, w in zip(dsbs, wins)]
            dkws = [lax.dot_general(dsb, qs, TN_DIMS, preferred_element_type=F32) for dsb, qs in zip(dsbs, qss)]
            dvws = [lax.dot_general(p.astype(BF16), dos, TN_DIMS, preferred_element_type=F32)
                    for p, dos in zip(probs, doss)]
            for t, r in enumerate(rows):
                _, _, off, ro0 = wins[t]
                for hh in range(2):
                    for j in range(NA_WIN_ROWS // 2):
                        acc_scr[hh, ro0 + 2 * j] += dss[t][hh * GRID_W:(hh + 1) * GRID_W, j * 128:(j + 1) * 128]
                dq_ref[pl.ds(pl.multiple_of(r * GRID_W, GRID_W), GRID_W), :] = (
                    _unstack_heads(dq2s[t]) * QK_SCALE).astype(BF16)
                dk_ref[pl.ds(off, win), :] += dkws[t]
                dv_ref[pl.ds(off, win), :] += dvws[t]
            return carry

        lax.fori_loop(0, n_rows // NA_GROUP_BWD, group, 0)

        qc = lax.broadcasted_iota(jnp.int32, (N_PAIRS * GRID_W, 128), 0)
        for hh in range(2):
            t = acc_scr[hh].reshape(N_PAIRS * GRID_W, 128)
            for b in range(6):
                t = jnp.where(((qc >> b) & 1) == 1, pltpu.roll(t, 128 - (1 << b), 1), t)
            t = pltpu.roll(t, 15, 1)
            drb_ref[hh] = jnp.sum(t.reshape(N_PAIRS, GRID_W, 128), axis=1)

    col = pl.BlockSpec((n, 128), lambda h: (0, h))
    return _call(
        body, name=name, grid=(NA_WIDTH // 128,),
        in_specs=_na_qkv_specs(n) + [col, pl.BlockSpec((2, 1, RB_WIDTH), lambda h: (h, 0, 0))],
        out_specs=[col, col, col, pl.BlockSpec((2, N_PAIRS, 128), lambda h: (h, 0, 0))],
        out_shape=[_sds((n, NA_WIDTH), BF16), _sds((n, NA_WIDTH), F32), _sds((n, NA_WIDTH), F32),
                   _sds((8, N_PAIRS, 128), F32)],
        scratch_shapes=[pltpu.VMEM((2, N_PAIRS, GRID_W, 128), F32),
                        pltpu.VMEM((2, N_PAIRS, GRID_W, 128), F32)],
        args=(qkv, qkv, qkv, do, rb), after=after)


def _rpb_table(rpb2):
    t = jnp.pad(rpb2, ((0, 0), (0, 1), (0, GRID_W - rpb2.shape[-1])))
    return t.reshape(8, 1, RB_WIDTH)


def _rpb_grad(drb, *, name):
    kdim = drb.shape[1]

    def body(x_ref, o_ref):
        kk = lax.broadcasted_iota(jnp.int32, (128, 512), 0)
        jj = lax.broadcasted_iota(jnp.int32, (128, 512), 1)
        half, co = kk >> 6, kk & 63
        acc = jnp.zeros((8, 512), F32)
        for ro in range(N_PAIRS):
            hit = ((ro + half) == (jj >> 5)) & (co == (jj & 31)) & (co < 31)
            onehot = jnp.where(hit, 1.0, 0.0).astype(F32)
            acc = acc + jnp.dot(x_ref[:, ro * 128:(ro + 1) * 128], onehot, preferred_element_type=F32,
                                precision=lax.Precision.HIGHEST)
        o_ref[...] = acc

    return pl.pallas_call(
        body, name=name, grid=(1,),
        in_specs=[_const((8, kdim))], out_specs=_const((8, 512)), out_shape=_sds((8, 512), F32),
        compiler_params=_params("arbitrary"),
    )(drb)


DIL_GROUP = 2


def _dil_blocks(length):
    qb = min(128, length)
    return qb, min(qb + 2 * DIL_RADIUS, length), min(DIL_GROUP, length // qb)


def _stack_lanes(ref, t, qb, scale=1.0):
    lane = lax.broadcasted_iota(jnp.int32, (qb, 256), 1)
    val = ref[0, t * qb:(t + 1) * qb, :].astype(F32) * scale
    return jnp.concatenate([jnp.where((lane >> 6) == h, val, 0.0) for h in range(4)], axis=0).astype(BF16)


def _dil_window(k_ref, v_ref, blk, qb, win, length):
    start = pl.multiple_of(jnp.clip(blk * qb - DIL_RADIUS, 0, length - win), DIL_RADIUS)
    return k_ref[0, pl.ds(start, win), :], v_ref[0, pl.ds(start, win), :], start


def _dil_caps_init(caps_scr, qb, win):
    @pl.when((pl.program_id(0) == 0) & (pl.program_id(1) == 0))
    def _():
        gap = ((lax.broadcasted_iota(jnp.int32, (4 * qb, win), 0) & (qb - 1))
               - lax.broadcasted_iota(jnp.int32, (4 * qb, win), 1))
        for v in range(3):
            caps_scr[v] = jnp.where(jnp.abs(gap + v * DIL_RADIUS) <= DIL_RADIUS, jnp.inf, NEG_INF)


def _dil_mask(s, blk, start, qb, caps_scr):
    return jnp.minimum(s, caps_scr[(blk * qb - start) // DIL_RADIUS])


def _pick_heads(stacked, qb):
    lane = lax.broadcasted_iota(jnp.int32, (qb, 256), 1)
    out = jnp.zeros((qb, 256), stacked.dtype)
    for h in range(4):
        out = jnp.where((lane >> 6) == h, stacked[h * qb:(h + 1) * qb], out)
    return out


def _stack_head_cols(ref, t, qb):
    return jnp.concatenate([ref[0, t * qb:(t + 1) * qb, 64 * h:64 * h + 1] for h in range(4)], axis=0)


def _dil_fwd(q, k, v, *, name, after=None):
    dil, length, _ = q.shape
    qb, win, grp = _dil_blocks(length)
    extra = [] if after is None else [after]

    def body(q_ref, k_ref, v_ref, *rest):
        o_ref, lse_ref, caps_scr = rest[-3:]
        _dil_caps_init(caps_scr, qb, win)
        blks = [pl.program_id(1) * grp + t for t in range(grp)]
        wins = [_dil_window(k_ref, v_ref, b, qb, win, length) for b in blks]
        raw = [lax.dot_general(_stack_lanes(q_ref, t, qb, QK_SCALE), w[0], NT_DIMS, preferred_element_type=F32)
               for t, w in enumerate(wins)]
        lses, outs = [], []
        for t, (s, w) in enumerate(zip(raw, wins)):
            s = _dil_mask(s, blks[t], w[2], qb, caps_scr)
            m = jnp.max(s, axis=-1, keepdims=True)
            e = jnp.exp(s - m)
            norm = jnp.sum(e, axis=-1, keepdims=True)
            lses.append(m + jnp.log(norm))
            outs.append(jnp.dot((e * (1.0 / norm)).astype(BF16), w[1], preferred_element_type=F32))
        for t in range(grp):
            o_ref[0, t * qb:(t + 1) * qb, :] = _pick_heads(outs[t], qb)
            lse_ref[0, t * qb:(t + 1) * qb, :] = _pick_heads(jnp.broadcast_to(lses[t], (4 * qb, 256)), qb)

    seq = pl.BlockSpec((1, length, 256), lambda j, i: (j, 0, 0))
    blk = pl.BlockSpec((1, grp * qb, 256), lambda j, i: (j, i, 0))
    return pl.pallas_call(
        body, name=name, grid=(dil, length // (grp * qb)),
        in_specs=[blk, seq, seq] + [pl.BlockSpec(memory_space=pl.ANY)] * len(extra), out_specs=[blk, blk],
        out_shape=[_sds((dil, length, 256), F32)] * 2,
        scratch_shapes=[pltpu.VMEM((3, 4 * qb, win), F32)],
        compiler_params=_params("arbitrary", "arbitrary"),
    )(q, k, v, *extra)


def _dil_bwd(q, k, v, do, lse, cc, *, name):
    dil, length, _ = q.shape
    qb, win, grp = _dil_blocks(length)

    def body(q_ref, k_ref, v_ref, do_ref, lse_ref, cc_ref, dq_ref, dk_ref, dv_ref, caps_scr):
        _dil_caps_init(caps_scr, qb, win)

        @pl.when(pl.program_id(1) == 0)
        def _():
            dk_ref[...] = jnp.zeros_like(dk_ref)
            dv_ref[...] = jnp.zeros_like(dv_ref)

        blks = [pl.program_id(1) * grp + t for t in range(grp)]
        wins = [_dil_window(k_ref, v_ref, b, qb, win, length) for b in blks]
        qss = [_stack_lanes(q_ref, t, qb, QK_SCALE) for t in range(grp)]
        doss = [_stack_lanes(do_ref, t, qb) for t in range(grp)]
        raw = [lax.dot_general(qs, w[0], NT_DIMS, preferred_element_type=F32) for qs, w in zip(qss, wins)]
        dps = [lax.dot_general(dos, w[1], NT_DIMS, preferred_element_type=F32) for dos, w in zip(doss, wins)]
        probs = [jnp.exp(_dil_mask(s, blks[t], wins[t][2], qb, caps_scr) - _stack_head_cols(lse_ref, t, qb))
                 for t, s in enumerate(raw)]
        dsbs = [(p * (dp + _stack_head_cols(cc_ref, t, qb))).astype(BF16)
                for t, (p, dp) in enumerate(zip(probs, dps))]
        dq4s = [jnp.dot(dsb, w[0], preferred_element_type=F32) for dsb, w in zip(dsbs, wins)]
        dkws = [lax.dot_general(dsb, qs, TN_DIMS, preferred_element_type=F32) for dsb, qs in zip(dsbs, qss)]
        dvws = [lax.dot_general(p.astype(BF16), dos, TN_DIMS, preferred_element_type=F32)
                for p, dos in zip(probs, doss)]
        for t in range(grp):
            dq_ref[0, t * qb:(t + 1) * qb, :] = _pick_heads(dq4s[t], qb) * QK_SCALE
            dk_ref[0, pl.ds(wins[t][2], win), :] += dkws[t]
            dv_ref[0, pl.ds(wins[t][2], win), :] += dvws[t]

    seq = pl.BlockSpec((1, length, 256), lambda j, i: (j, 0, 0))
    blk = pl.BlockSpec((1, grp * qb, 256), lambda j, i: (j, i, 0))
    return pl.pallas_call(
        body, name=name, grid=(dil, length // (grp * qb)),
        in_specs=[blk, seq, seq, blk, blk, blk], out_specs=[blk, seq, seq],
        out_shape=[_sds((dil, length, 256), F32)] * 3,
        scratch_shapes=[pltpu.VMEM((3, 4 * qb, win), F32)],
        compiler_params=_params("arbitrary", "arbitrary"),
    )(q, k, v, do, lse, cc)


def _merge_weights(lses):
    m = jnp.maximum(jnp.maximum(lses[0], lses[1]), lses[2])
    es = [jnp.exp(t - m) for t in lses]
    inv = 1.0 / (es[0] + es[1] + es[2])
    return [e * inv for e in es]


def _branch_mix(y_na, w_bna, outs, lses, w_bd, gates, *, tm, name):
    n = y_na.shape[0]
    chunk = min(EPILOGUE_ROWS, tm)

    def body(yna_ref, wn_ref, *rest):
        o_in, l_in = rest[0:3], rest[3:6]
        wd_ref, sn_ref, sd_ref = rest[6:9]
        y_ref, yb_ref, bn_ref, bd_ref, mix_ref, scr = rest[9:15]
        for r0 in range(0, tm, chunk):
            rows = slice(r0, r0 + chunk)
            lv = [_load_token_order(l_in[g], scr, d, chunk, r0) for g, d in enumerate(DIL_DILATIONS)]
            ws = _merge_weights(lv)
            y = jnp.zeros((chunk, 256), F32)
            for g, d in enumerate(DIL_DILATIONS):
                y = y + ws[g] * _load_token_order(o_in[g], scr, d, chunk, r0)
            yb = y.astype(BF16)
            y_ref[rows, :] = y
            yb_ref[rows, :] = yb
            bn = lax.dot_general(yna_ref[rows, :], wn_ref[...], NT_DIMS, preferred_element_type=F32).astype(BF16)
            bd = lax.dot_general(yb, wd_ref[...], NT_DIMS, preferred_element_type=F32)
            bn_ref[rows, :] = bn
            bd, mixed = _gate_mix_tile(bd, sn_ref[rows, :], bn, sd_ref[rows, :])
            bd_ref[rows, :] = bd.astype(BF16)
            mix_ref[rows, :] = mixed.astype(BF16)

    specs = [_dil_spec(d, tm) for d in DIL_DILATIONS]
    return pl.pallas_call(
        body, name=name, grid=(n // tm,),
        in_specs=[_rows(tm, NA_WIDTH), _const(w_bna.shape)] + specs + specs
                 + [_const(w_bd.shape), _rows(tm, D_MODEL, 0), _rows(tm, D_MODEL, 1)],
        out_specs=[_rows(tm, 256)] * 2 + [_rows(tm, D_MODEL)] * 3,
        out_shape=[_sds((n, 256), F32), _sds((n, 256), BF16)] + [_sds((n, D_MODEL), BF16)] * 3,
        scratch_shapes=[_dil_scratch(chunk)],
        compiler_params=_params("parallel"),
    )(y_na, w_bna, *outs, *lses, w_bd, gates, gates)


def _branch_bwd(dh, w_out, gates, bn, bd, w_bna, w_bd, y, lses, *, tm, name):
    n = dh.shape[0]
    chunk = min(EPILOGUE_ROWS, tm)

    def body(dh_ref, wo_ref, sn_ref, sd_ref, bn_ref, bd_ref, wn_ref, wd_ref, y_ref, *rest):
        l_in = rest[0:3]
        dbn_ref, dbd_ref, dgn_ref, dgd_ref, dyna_ref = rest[3:8]
        do_out, cc_out, scr = rest[8:11], rest[11:14], rest[14]
        rr = lax.broadcasted_iota(jnp.int32, (256, 256), 0) >> 6
        cc = lax.broadcasted_iota(jnp.int32, (256, 256), 1) >> 6
        ones = jnp.where(rr == cc, 1.0, 0.0).astype(F32)
        for r0 in range(0, tm, chunk):
            rows = slice(r0, r0 + chunk)
            dm = lax.dot_general(dh_ref[rows, :], wo_ref[...], NT_DIMS, preferred_element_type=F32)
            dbn, dbd, dgn, dgd = (t.astype(BF16) for t in _gate_bwd_tile(
                dm, sn_ref[rows, :], bn_ref[rows, :], sd_ref[rows, :], bd_ref[rows, :]))
            dbn_ref[rows, :] = dbn
            dbd_ref[rows, :] = dbd
            dgn_ref[rows, :] = dgn
            dgd_ref[rows, :] = dgd
            dyna_ref[rows, :] = jnp.dot(dbn, wn_ref[...], preferred_element_type=F32).astype(BF16)
            dyv = jnp.dot(dbd, wd_ref[...], preferred_element_type=F32)
            lv = [_load_token_order(l_in[g], scr, d, chunk, r0) for g, d in enumerate(DIL_DILATIONS)]
            ws = _merge_weights(lv)
            tsum = jnp.dot(dyv * y_ref[rows, :], ones, preferred_element_type=F32,
                           precision=lax.Precision.HIGHEST)
            for g, d in enumerate(DIL_DILATIONS):
                _store_dil_order(ws[g] * dyv, do_out[g], scr, d, r0)
                _store_dil_order(-ws[g] * tsum, cc_out[g], scr, d, r0)

    specs = [_dil_spec(d, tm) for d in DIL_DILATIONS]
    wide = _rows(tm, D_MODEL)
    res = pl.pallas_call(
        body, name=name, grid=(n // tm,),
        in_specs=[wide, _const(w_out.shape), _rows(tm, D_MODEL, 0), _rows(tm, D_MODEL, 1), wide, wide,
                  _const(w_bna.shape), _const(w_bd.shape), _rows(tm, 256)] + specs,
        out_specs=[wide] * 4 + [_rows(tm, NA_WIDTH)] + specs + specs,
        out_shape=[_sds((n, D_MODEL), BF16)] * 4 + [_sds((n, NA_WIDTH), BF16)]
                  + [_sds((d, n // d, 256), BF16) for d in DIL_DILATIONS]
                  + [_sds((d, n // d, 256), F32) for d in DIL_DILATIONS],
        scratch_shapes=[_dil_scratch(chunk)],
        compiler_params=_params("parallel"),
    )(dh, w_out, gates, gates, bn, bd, w_bna, w_bd, y, *lses)
    return res[0], res[1], res[2], res[3], res[4], res[5:8], res[8:11]


_WEIGHTS = (("w_in", 1, 736), ("w_branch_na", 1, 128), ("w_branch_dil", 1, 128), ("w_out", 0, 128),
            ("w_up", 1, 512), ("w_down", 0, 512), ("w_ple_gate", 0, 128), ("w_ple_proj", 1, 128))
_W_IN, _W_BNA, _W_BD, _W_OUT, _W_UP, _W_DOWN, _W_PG, _W_PP = range(8)


def _to_full(gathered):
    return gathered.reshape(-1, gathered.shape[2])


def _to_chunks(widx, mat):
    return mat.reshape(N_DEV, _WEIGHTS[widx][2], mat.shape[1])


def _local_step(x, p_bf16, positions, target, g_mix, g_mlp, g_ple, g_final, rpb2,
                get_w_in, relay_rest, get_rest, send_grads):
    tm = 512
    half = HEAD_DIM // 2
    inv_freq = 10000.0 ** (-jnp.arange(half, dtype=F32) / half)
    ang = positions.astype(F32)[:, None] * inv_freq
    cos, sin = jnp.cos(ang), jnp.sin(ang)
    cos_t = jnp.tile(jnp.concatenate([cos, cos], axis=-1), (1, 4))
    sin_t = jnp.tile(jnp.concatenate([-sin, sin], axis=-1), (1, 4))
    rb = _rpb_table(rpb2)

    a = _rms_fwd(x, g_mix, tm=tm, name="rms_mix")
    w_in, token = get_w_in(a)
    na_qkv, gates, dq_g, dk_g, dv_g = _project_in(a, w_in, cos_t, sin_t, tm=512, name="mm_in", after=token)
    y_na = _na_fwd(na_qkv, rb, name="na_fwd")
    token = relay_rest(y_na)
    d_out, d_lse = [], []
    for g in range(3):
        o, lse = _dil_fwd(dq_g[g], dk_g[g], dv_g[g], name=f"dil_fwd{g}", after=token if g == 0 else None)
        d_out.append(o)
        d_lse.append(lse)
    w_bna, w_bd = get_rest(d_out[2], 0)
    y_dil, y_dil_b, bn, bd, mixed = _branch_mix(y_na, w_bna, d_out, d_lse, w_bd, gates, tm=tm, name="branch_mix")
    w_out, w_up, w_down, w_pg, w_pp = get_rest(mixed, 1)
    h1, c = _matmul(mixed, w_out, out_dtype=(F32, BF16), tm=512, tn=1024, tk=1024, name="mm_out",
                    extra=(x, g_mlp), epilogue=_residual_rms_tile)
    u, f = _matmul(c, w_up, tb=True, out_dtype=(BF16, BF16), tm=512, tn=2048, tk=1024, name="mm_up",
                   epilogue=lambda acc: (acc, jnp.square(jnp.maximum(acc, 0.0))))
    h2, e = _matmul(f, w_down, out_dtype=(F32, BF16), tm=512, tn=1024, tk=4096, name="mm_down",
                    extra=(h1, g_ple), epilogue=_residual_rms_tile)

    dpp, dgt, dh2, dh2_b, dg_final, loss, dg_ple = _tail_step(e, w_pg, p_bf16, w_pp, h2, target, g_final, g_ple,
                                                              tm=tm, name="tail_step")
    loss = loss[:, :128]
    gw_pp = _matmul(p_bf16, dpp, ta=True, transpose_out=True, out_dtype=BF16, tm=256, tn=1024, tk=2048,
                    name="mm_gw_pp")
    gw_pg = _matmul(e, dgt, ta=True, out_dtype=BF16, tm=512, tn=1024, tk=2048, name="mm_gw_pg")
    du = _matmul(dh2_b, w_down, tb=True, out_dtype=BF16, tm=512, tn=2048, tk=1024, name="mm_du",
                 extra=(u,), epilogue=lambda acc, uv: (acc * (2.0 * jnp.maximum(uv.astype(F32), 0.0)),))
    gw_down = _matmul(f, dh2_b, ta=True, out_dtype=BF16, tm=1024, tn=1024, tk=2048, name="mm_gw_down")
    gw_up = _matmul(c, du, ta=True, transpose_out=True, out_dtype=BF16, tm=512, tn=2048, tk=2048, name="mm_gw_up")
    dh1, dh1_b, dg_mlp = _matmul(
        du, w_up, out_dtype=(F32, BF16), tm=512, tn=1024, tk=4096, name="mm_dc",
        extra=(h1, g_mlp, dh2), epilogue=_rms_bwd_twice, n_colsum=1)
    dbn, dbd, dgn, dgd, dy_na, do_g, cc_g = _branch_bwd(dh1_b, w_out, gates, bn, bd, w_bna, w_bd, y_dil, d_lse,
                                                        tm=tm, name="branch_bwd")
    gw_out = _matmul(mixed, dh1_b, ta=True, out_dtype=BF16, tm=512, tn=1024, tk=2048, name="mm_gw_out")
    gw_bna = _matmul(y_na, dbn, ta=True, transpose_out=True, out_dtype=BF16, tm=512, tn=1024, tk=2048,
                     name="mm_gw_bna")
    gw_bd = _matmul(y_dil_b, dbd, ta=True, transpose_out=True, out_dtype=BF16, tm=256, tn=1024, tk=2048,
                    name="mm_gw_bd")
    token = send_grads((_W_PP, _W_PG, _W_DOWN, _W_UP, _W_OUT, _W_BNA, _W_BD),
                       (gw_pp, gw_pg, gw_down, gw_up, gw_out, gw_bna, gw_bd))
    dna = _na_bwd(na_qkv, dy_na, rb, name="na_bwd", after=token)
    drpb = _rpb_grad(dna[3].reshape(8, -1), name="rpb_grad")
    ddq, ddk, ddv = [], [], []
    for g in range(3):
        r = _dil_bwd(dq_g[g], dk_g[g], dv_g[g], do_g[g], d_lse[g], cc_g[g], name=f"dil_bwd{g}")
        ddq.append(r[0])
        ddk.append(r[1])
        ddv.append(r[2])
    dproj = _assemble_dproj(dna[0:3], ddq, ddk, ddv, dgn, dgd, cos_t, sin_t, tm=tm, name="assemble_dproj")
    gw_in = _matmul(a, dproj, ta=True, transpose_out=True, out_dtype=BF16, tm=512, tn=2944, tk=2048, name="mm_gw_in")
    token = send_grads((_W_IN,), (gw_in,))
    dx, dg_mix = _matmul(
        dproj, w_in, out_dtype=(F32,), tm=512, tn=1024, tk=5888, name="mm_da", after=token,
        extra=(x, g_mix, dh1), epilogue=_rms_bwd_tile, n_colsum=1)
    return loss, dx, (dg_mix, dg_mlp, dg_ple, dg_final), drpb


def _cast_bf16(t, *, name):
    def body(t_ref, o_ref):
        o_ref[...] = t_ref[...].astype(BF16)

    rows, cols = t.shape
    tr = 256 if rows % 256 == 0 else rows
    blk = pl.BlockSpec((tr, cols), lambda i: (i, 0))
    return pl.pallas_call(body, name=name, grid=(rows // tr,), in_specs=[blk], out_specs=blk,
                          out_shape=_sds(t.shape, BF16), compiler_params=_params("parallel"))(t)


def _adamw(w, g, m, v):
    m = ADAM_B1 * m + (1.0 - ADAM_B1) * g
    v = ADAM_B2 * v + (1.0 - ADAM_B2) * (g * g)
    m_hat = m / (1.0 - ADAM_B1 ** ADAM_STEP)
    v_hat = v / (1.0 - ADAM_B2 ** ADAM_STEP)
    delta = -ADAM_LR * (m_hat / (jnp.sqrt(v_hat) + ADAM_EPS) + ADAM_WD * w)
    return delta, m, v


def _sum_adamw(parts, w, m, v, *, tr, name, own=None, transposed=False):
    rows, cols = w.shape
    n_pre = 0 if own is None else 1

    def body(*refs):
        p_ref, w_ref, m_ref, v_ref = refs[n_pre:n_pre + 4]
        g_ref, d_ref, nm_ref, nv_ref = refs[-4:]
        g = (p_ref[0] if own is None else refs[n_pre + 4][...]).astype(F32)
        for s in range(1, N_DEV):
            g = g + p_ref[s].astype(F32)
        if transposed:
            g = g.T
        g_ref[...] = g
        d_ref[...], nm_ref[...], nv_ref[...] = _adamw(w_ref[...], g, m_ref[...], v_ref[...])

    if transposed:
        blk = pl.BlockSpec((rows, tr), lambda i, *_: (0, i))
        g_rows, steps = rows, cols // tr
    else:
        blk = pl.BlockSpec((tr, cols), lambda i, *_: (i, 0))
        g_rows, steps = cols, rows // tr
    in_specs = [pl.BlockSpec((N_DEV, tr, g_rows), lambda i, *_: (0, i, 0)), blk, blk, blk]
    args = [parts, w, m, v]
    if own is not None:
        in_specs.append(pl.BlockSpec((None, tr, g_rows), lambda i, idx: (idx[0], i, 0)))
        args = [own[1]] + args + [own[0]]
    return pl.pallas_call(
        body, name=name,
        grid_spec=pltpu.PrefetchScalarGridSpec(num_scalar_prefetch=n_pre, grid=(steps,), in_specs=in_specs,
                                               out_specs=[blk] * 4),
        out_shape=[_sds((rows, cols), F32)] * 4,
        compiler_params=_params("parallel"),
    )(*args)


_RPB_SIZE = 8 * 15 * 31


def _pack_small(g_mix, g_mlp, g_ple, g_final, rpb, loss_row):
    flat = jnp.concatenate([g_mix.reshape(-1), g_mlp.reshape(-1), g_ple.reshape(-1), g_final.reshape(-1),
                            rpb.reshape(-1), jnp.zeros((3840 - _RPB_SIZE,), F32), loss_row.reshape(-1),
                            jnp.zeros((128,), F32)])
    return flat.reshape(64, 128)


def _unpack_small(t):
    flat = t.reshape(-1)
    return (flat[0:1024].reshape(1, 1024), flat[4096:4096 + _RPB_SIZE].reshape(1, 8, 15, 31),
            flat[1024:2048].reshape(1, 1024), flat[2048:3072].reshape(1, 1024), flat[3072:4096])


def kernel(x, p, positions, g_mix, w_in, rpb, w_branch_na, w_branch_dil, w_out, g_mlp, w_up, w_down, g_ple, w_ple_gate, w_ple_proj, g_final, loss_target, m_g_mix, m_w_in, m_rpb, m_w_branch_na, m_w_branch_dil, m_w_out, m_g_mlp, m_w_up, m_w_down, m_g_ple, m_w_ple_gate, m_w_ple_proj, m_g_final, v_g_mix, v_w_in, v_rpb, v_w_branch_na, v_w_branch_dil, v_w_out, v_g_mlp, v_w_up, v_w_down, v_g_ple, v_w_ple_gate, v_w_ple_proj, v_g_final):
    sharded = dict(w_in=(w_in, m_w_in, v_w_in), w_branch_na=(w_branch_na, m_w_branch_na, v_w_branch_na),
                   w_branch_dil=(w_branch_dil, m_w_branch_dil, v_w_branch_dil), w_out=(w_out, m_w_out, v_w_out),
                   w_up=(w_up, m_w_up, v_w_up), w_down=(w_down, m_w_down, v_w_down),
                   w_ple_gate=(w_ple_gate, m_w_ple_gate, v_w_ple_gate),
                   w_ple_proj=(w_ple_proj, m_w_ple_proj, v_w_ple_proj))
    shards = {k: tuple(t[0] for t in val) for k, val in sharded.items()}

    me = _my_index()

    shards["w_in"] = tuple(t.T for t in shards["w_in"])

    w_in_b = _cast_bf16(shards["w_in"][0], name="cast_w_in")
    rest_b = [shards[name][0].astype(BF16).T if axis == 1 else shards[name][0].astype(BF16)
              for name, axis, _ in _WEIGHTS[1:]]
    first_in, token_in = _start_copies(_first_leg_copies, [w_in_b], [_sds((N_DEV,) + w_in_b.shape, BF16)], 4,
                                       name="start_gather_w_in")

    def whole(landed, mine):
        return _to_full(lax.dynamic_update_index_in_dim(landed, mine, me, 0))

    rest = {}

    def get_w_in(after):
        (mine,), landed = _wait_copies(_first_leg_copies, first_in, after, name="wait_gather_w_in")
        second, token = _start_copies(_second_leg_copies, [], landed, 3, name="start_forward_w_in")
        _, (landed,) = _wait_copies(_second_leg_copies, second, token, name="wait_forward_w_in")
        rest["first"], token = _start_copies(_first_leg_copies, rest_b,
                                             [_sds((N_DEV,) + t.shape, BF16) for t in rest_b], 4 * len(rest_b),
                                             name="start_gather_rest", after=landed)
        return whole(landed, mine), token

    def relay_rest(after):
        rest["mine"], landed = _wait_copies(_first_leg_copies, rest["first"], after, name="wait_gather_rest")
        rest["second"], token = _start_copies(_second_leg_copies, [], landed, 3 * len(rest_b),
                                              name="start_forward_rest")
        return token

    def get_rest(after, stage):
        n_src, send_sems, recv_sems, bufs = rest["second"]
        part = slice(0, 2) if stage == 0 else slice(2, len(rest_b))
        _, landed = _wait_copies(functools.partial(_second_leg_copies, first=part.start),
                                 (n_src, send_sems, recv_sems, bufs[part]), after,
                                 name=f"wait_forward_rest{stage}")
        return [whole(t, own) for t, own in zip(landed, rest["mine"][part])]

    sent = []

    def send_grads(indices, grads):
        chunked = [_to_chunks(i, g) for i, g in zip(indices, grads)]
        handle, token = _start_copies(_exchange_copies, chunked, [_sds(t.shape, BF16) for t in chunked],
                                      7 * len(chunked),
                                      name="start_exchange_" + ("w_in" if indices == (_W_IN,) else "rest"))
        sent.append((indices, handle))
        return token

    g_mix_0 = g_mix + token_in[0:1, 0:1]
    loss, dx, dgs, drpb = _local_step(
        x[0], p[0, 0].astype(BF16), positions[0], loss_target[0],
        g_mix_0, g_mlp, g_ple, g_final.reshape(1, -1), rpb[0], get_w_in, relay_rest, get_rest, send_grads)

    drpb3 = drpb.reshape(8, 16, 32)[:, :15, :31]
    small = _pack_small(dgs[0], dgs[1], dgs[2], dgs[3], drpb3, loss)
    share, done = _start_copies(_gather_copies, [small], [_sds((N_DEV,) + small.shape, F32)], 7,
                                name="start_share_small")

    out = {}
    for indices, handle in sent:
        chunked, landed = _wait_copies(_exchange_copies, handle, done,
                                       name="wait_exchange_" + ("w_in" if indices == (_W_IN,) else "rest"))
        for i, part, mine in zip(indices, landed, chunked):
            name = _WEIGHTS[i][0]
            w, m, v = shards[name]
            turned = _WEIGHTS[i][1] == 1 and i != _W_IN
            res = _sum_adamw(part, w, m, v, tr=368 if i == _W_IN else 128, name="adamw_" + name,
                             own=(mine, me.reshape(1).astype(jnp.int32)), transposed=turned)
            out[name] = [(t.T if i == _W_IN else t)[None] for t in res]
            done = res[0]
    (small,), (small_landed,) = _wait_copies(_gather_copies, share, done, name="wait_share_small")
    small_all = lax.dynamic_update_index_in_dim(small_landed, small, me, 0)
    small_w = _pack_small(g_mix, g_mlp, g_ple, g_final, rpb, jnp.zeros((128,), F32))
    small_m = _pack_small(m_g_mix, m_g_mlp, m_g_ple, m_g_final, m_rpb, jnp.zeros((128,), F32))
    small_v = _pack_small(v_g_mix, v_g_mlp, v_g_ple, v_g_final, v_rpb, jnp.zeros((128,), F32))
    res = _sum_adamw(small_all, small_w, small_m, small_v, tr=64, name="adamw_small")
    unpacked = [_unpack_small(t) for t in res]
    for i, name in enumerate(("g_mix", "rpb", "g_mlp", "g_ple", "g_final")):
        out[name] = [u[i] for u in unpacked]
    loss_total = res[0][62, 0]

    order = ("g_mix", "w_in", "rpb", "w_branch_na", "w_branch_dil", "w_out", "g_mlp", "w_up", "w_down",
             "g_ple", "w_ple_gate", "w_ple_proj", "g_final")
    grads = [out[k][0] for k in order]
    deltas = [out[k][1] for k in order]
    new_m = [out[k][2] for k in order]
    new_v = [out[k][3] for k in order]
    return (loss_total, dx[None], *grads, *deltas, *new_m, *new_v)
```

```python
import functools

import jax
import jax.numpy as jnp
from jax import lax
from jax.experimental import pallas as pl
from jax.experimental.pallas import tpu as pltpu

F32 = jnp.float32
BF16 = jnp.bfloat16

D_MODEL = 1024
HEAD_DIM = 64
GRID_W = 64
NA_WIDTH = 512
DIL_WIDTH = 768
IN_WIDTH = 5888
DIL_DILATIONS = (1, 4, 16)
DIL_RADIUS = 64
NA_WIN_ROWS = 8
RMS_EPS = 1e-6
NEG_INF = -1e30
QK_SCALE = HEAD_DIM ** -0.5

ADAM_LR = 0.001
ADAM_B1 = 0.9
ADAM_B2 = 0.999
ADAM_EPS = 1e-08
ADAM_WD = 0.01
ADAM_STEP = 10

N_DEV = 8
VMEM_LIMIT = 56 * 1024 * 1024
EPILOGUE_ROWS = 256
MESH = pl.DeviceIdType.MESH

NT_DIMS = (((1,), (1,)), ((), ()))
TN_DIMS = (((0,), (0,)), ((), ()))


def _sds(shape, dtype):
    return jax.ShapeDtypeStruct(shape, dtype)


def _params(*sem):
    return pltpu.CompilerParams(dimension_semantics=sem, vmem_limit_bytes=VMEM_LIMIT)


def _rows(tm, width, col=0):
    return pl.BlockSpec((tm, width), lambda i, c=col: (i, c))


def _const(shape):
    zeros = (0,) * len(shape)
    return pl.BlockSpec(shape, lambda i: zeros)


def _my_index():
    return 4 * lax.axis_index("x") + 2 * lax.axis_index("y") + lax.axis_index("c")


def _peer(k):
    x, y, c = lax.axis_index("x"), lax.axis_index("y"), lax.axis_index("c")
    px = 1 - x if k & 4 else x
    py = 1 - y if k & 2 else y
    pc = 1 - c if k & 1 else c
    return (px, py, pc), 4 * px + 2 * py + pc


def _call(body, *, name, grid, in_specs, out_specs, out_shape, scratch_shapes, args, after=None):
    n_in, n_out = len(in_specs), len(out_specs)
    extra = [] if after is None else [after]
    n_x = n_in + len(extra)

    def plain(*refs):
        body(refs[:n_in], refs[n_x:n_x + n_out], refs[n_x + n_out:])

    res = pl.pallas_call(plain, name=name, grid=grid,
                         in_specs=list(in_specs) + [pl.BlockSpec(memory_space=pl.ANY)] * len(extra),
                         out_specs=out_specs, out_shape=out_shape, scratch_shapes=scratch_shapes,
                         compiler_params=_params(*(("arbitrary",) * len(grid))))(*args, *extra)
    return list(res)


_HBM_SPEC = pl.BlockSpec(memory_space=pltpu.HBM)
_SEM_SPEC = pl.BlockSpec(memory_space=pltpu.SEMAPHORE)
_SIDE_EFFECT = pltpu.SideEffectType.DATAFLOW_SIDE_EFFECTING


_FIRST_LEG = (1, 2, 4, 6)
_SECOND_LEG = (2, 4, 6)


def _gather_copies(srcs, lands, send, recv, sending):
    me = _my_index()
    out = []
    for w in range(len(srcs)):
        for k in range(1, N_DEV):
            dev, idx = _peer(k)
            out.append(pltpu.make_async_remote_copy(
                src_ref=srcs[w], dst_ref=lands[w].at[me if sending else idx],
                send_sem=send.at[w * 7 + k - 1], recv_sem=recv.at[w * 7 + k - 1],
                device_id=dev, device_id_type=MESH))
    return out


def _first_leg_copies(srcs, lands, send, recv, sending):
    me = _my_index()
    out = []
    for w in range(len(srcs)):
        for j, k in enumerate(_FIRST_LEG):
            dev, idx = _peer(k)
            out.append(pltpu.make_async_remote_copy(
                src_ref=srcs[w], dst_ref=lands[w].at[me if sending else idx],
                send_sem=send.at[w * 4 + j], recv_sem=recv.at[w * 4 + j],
                device_id=dev, device_id_type=MESH))
    return out


def _second_leg_copies(srcs, lands, send, recv, sending, first=0):
    sibling, _ = _peer(1)
    out = []
    for w in range(len(lands)):
        for j, k in enumerate(_SECOND_LEG):
            slot = _peer(k if sending else k ^ 1)[1]
            sem = (first + w) * 3 + j
            out.append(pltpu.make_async_remote_copy(
                src_ref=lands[w].at[slot], dst_ref=lands[w].at[slot],
                send_sem=send.at[sem], recv_sem=recv.at[sem],
                device_id=sibling, device_id_type=MESH))
    return out


def _exchange_copies(srcs, lands, send, recv, sending):
    out = []
    for w in range(len(srcs)):
        for k in range(1, N_DEV):
            dev, idx = _peer(k)
            out.append(pltpu.make_async_remote_copy(
                src_ref=srcs[w].at[idx], dst_ref=lands[w].at[k],
                send_sem=send.at[w * 7 + k - 1], recv_sem=recv.at[w * 7 + k - 1],
                device_id=dev, device_id_type=MESH))
    return out


def _start_copies(make, srcs, lands, n_copies, *, name, after=None):
    n_src, n_buf = len(srcs), len(srcs) + len(lands)
    extra = [] if after is None else [after]

    def body(*refs):
        send, recv = refs[n_buf + len(extra)], refs[n_buf + len(extra) + 1]
        for cp in make(refs[:n_src], refs[n_src:n_buf], send, recv, True):
            cp.start()
        refs[-1][...] = jnp.zeros_like(refs[-1])

    bufs = list(srcs) + [lax.empty(t.shape, t.dtype) if isinstance(t, jax.ShapeDtypeStruct) else t for t in lands]
    res = pl.pallas_call(
        body, name=name,
        out_shape=(pltpu.SemaphoreType.DMA((n_copies,)), pltpu.SemaphoreType.DMA((n_copies,)),
                   *[pltpu.HBM(t.shape, t.dtype) for t in bufs], _sds((8, 128), F32)),
        in_specs=[_HBM_SPEC] * n_buf + [pl.BlockSpec(memory_space=pl.ANY)] * len(extra),
        out_specs=(_SEM_SPEC, _SEM_SPEC, *([_HBM_SPEC] * n_buf), pl.BlockSpec(memory_space=pltpu.VMEM)),
        input_output_aliases={i: 2 + i for i in range(n_buf)},
        compiler_params=pltpu.CompilerParams(has_side_effects=_SIDE_EFFECT),
    )(*[pltpu.with_memory_space_constraint(t, pltpu.HBM) for t in bufs], *extra)
    return (n_src, res[0], res[1], res[2:2 + n_buf]), res[-1]


def _wait_copies(make, handle, after, *, name):
    n_src, send_sems, recv_sems, bufs = handle
    n_buf = len(bufs)

    def body(*refs):
        for cp in make(refs[:n_src], refs[n_src:n_buf], refs[n_buf], refs[n_buf + 1], False):
            cp.wait_send()
            cp.wait_recv()

    res = pl.pallas_call(
        body, name=name,
        out_shape=tuple(pltpu.HBM(t.shape, t.dtype) for t in bufs),
        in_specs=[_HBM_SPEC] * n_buf + [_SEM_SPEC, _SEM_SPEC, pl.BlockSpec(memory_space=pl.ANY)],
        out_specs=tuple([_HBM_SPEC] * n_buf),
        input_output_aliases={i: i for i in range(n_buf)},
        compiler_params=pltpu.CompilerParams(has_side_effects=_SIDE_EFFECT),
    )(*bufs, send_sems, recv_sems, after)
    return list(res[:n_src]), list(res[n_src:])


def _add_colsums(s_refs, sums, step):
    for s_ref, val in zip(s_refs, sums):
        @pl.when(step == 0)
        def _(s_ref=s_ref, val=val):
            s_ref[...] = val

        @pl.when(step > 0)
        def _(s_ref=s_ref, val=val):
            s_ref[...] += val


def _matmul(a, b, *, ta=False, tb=False, out_dtype, tm, tn, tk, name, after=None, extra=(), epilogue=None,
            n_colsum=0, transpose_out=False):
    m, k = (a.shape[1], a.shape[0]) if ta else a.shape
    n = b.shape[0] if tb else b.shape[1]
    tm, tn, tk = min(tm, m), min(tn, n), min(tk, k)
    nk = k // tk
    dims = (((0 if ta else 1,), (1 if tb else 0,)), ((), ()))
    out_dtypes = out_dtype if isinstance(out_dtype, tuple) else (out_dtype,)
    n_tiles = len(out_dtypes)

    def add_colsums(o_refs, sums):
        _add_colsums(o_refs[n_tiles:], sums, pl.program_id(1))

    def finish(acc, x_refs, o_refs):
        vals = (acc,) if epilogue is None else epilogue(acc, *[r[...] for r in x_refs])
        for o_ref, val in zip(o_refs[:n_tiles], vals[:n_tiles]):
            o_ref[...] = (val.T if transpose_out else val).astype(o_ref.dtype)
        add_colsums(o_refs, vals[n_tiles:])

    chunk = EPILOGUE_ROWS if (nk == 1 and epilogue is not None and not ta and tm % EPILOGUE_ROWS == 0) else None

    def body(ins, outs, acc):
        a_ref, b_ref = ins[:2]
        if chunk is not None:
            sums = None
            for r0 in range(0, tm, chunk):
                part = lax.dot_general(a_ref[r0:r0 + chunk, :], b_ref[...], dims, preferred_element_type=F32)
                vals = epilogue(part, *[r[...] if r.shape[0] == 1 else r[r0:r0 + chunk, :] for r in ins[2:]])
                for o_ref, val in zip(outs[:n_tiles], vals[:n_tiles]):
                    o_ref[r0:r0 + chunk, :] = val.astype(o_ref.dtype)
                sums = vals[n_tiles:] if sums is None else [s + v for s, v in zip(sums, vals[n_tiles:])]
            add_colsums(outs, sums)
            return
        part = lax.dot_general(a_ref[...], b_ref[...], dims, preferred_element_type=F32)
        if nk == 1:
            finish(part, ins[2:], outs)
            return
        acc_ref, = acc
        kk = pl.program_id(2)

        @pl.when(kk == 0)
        def _():
            acc_ref[...] = part

        @pl.when(kk > 0)
        def _():
            acc_ref[...] += part

        @pl.when(kk == nk - 1)
        def _():
            finish(acc_ref[...], ins[2:], outs)

    a_spec = (pl.BlockSpec((tk, tm), lambda j, i, kk: (kk, i)) if ta
              else pl.BlockSpec((tm, tk), lambda j, i, kk: (i, kk)))
    b_spec = (pl.BlockSpec((tn, tk), lambda j, i, kk: (j, kk)) if tb
              else pl.BlockSpec((tk, tn), lambda j, i, kk: (kk, j)))
    tile = pl.BlockSpec((tm, tn), lambda j, i, kk: (i, j))
    row = pl.BlockSpec((1, tn), lambda j, i, kk: (0, j))

    out_tile, out_dims = (pl.BlockSpec((tn, tm), lambda j, i, kk: (j, i)), (n, m)) if transpose_out else (tile, (m, n))
    res = _call(
        body, name=name, grid=(n // tn, m // tm, nk),
        in_specs=[a_spec, b_spec] + [row if t.shape[0] == 1 else tile for t in extra],
        out_specs=[out_tile] * n_tiles + [row] * n_colsum,
        out_shape=[_sds(out_dims, dt) for dt in out_dtypes] + [_sds((1, n), F32)] * n_colsum,
        scratch_shapes=[] if nk == 1 else [pltpu.VMEM((tm, tn), F32)],
        args=(a, b, *extra), after=after)
    return res if isinstance(out_dtype, tuple) or n_colsum else res[0]


def _rstd(h):
    return lax.rsqrt(jnp.mean(h * h, axis=-1, keepdims=True) + RMS_EPS)


def _sigmoid(z):
    return 1.0 / (1.0 + jnp.exp(-z))


def _rms_fwd(x, g, *, tm, name):
    n = x.shape[0]

    def body(x_ref, g_ref, o_ref):
        h = x_ref[...]
        o_ref[...] = (h * _rstd(h) * g_ref[...]).astype(BF16)

    return pl.pallas_call(
        body, name=name, grid=(n // tm,),
        in_specs=[_rows(tm, D_MODEL), _const((1, D_MODEL))],
        out_specs=_rows(tm, D_MODEL), out_shape=_sds((n, D_MODEL), BF16),
        compiler_params=_params("parallel"),
    )(x, g)


def _swap_halves(t):
    lane = lax.broadcasted_iota(jnp.int32, (t.shape[0], 128), 1)
    pieces = [t[:, c:c + 128] for c in range(0, t.shape[1], 128)]
    return jnp.concatenate([jnp.where((lane & 63) < 32, pltpu.roll(h, 96, 1), pltpu.roll(h, 32, 1))
                            for h in pieces], axis=1)


def _dil_spec(dil, tm):
    return pl.BlockSpec((dil, tm // dil, 256), lambda i: (0, i, 0))


def _dil_scratch(tm):
    return pltpu.VMEM((2, tm, 128), F32)


def _load_token_order(src, scr, dil, rows, row0=0):
    if dil == 1:
        return src[0, row0:row0 + rows, :]
    for j in range(dil):
        for c in range(2):
            scr[c, pl.ds(j, rows // dil, stride=dil), :] = (
                src[j, row0 // dil:(row0 + rows) // dil, c * 128:(c + 1) * 128])
    return jnp.concatenate([scr[0, 0:rows, :], scr[1, 0:rows, :]], axis=1)


def _store_dil_order(val, dst, scr, dil, row0=0):
    rows = val.shape[0]
    if dil == 1:
        dst[0, row0:row0 + rows, :] = val.astype(dst.dtype)
        return
    for c in range(2):
        scr[c] = val[:, c * 128:(c + 1) * 128]
    for j in range(dil):
        for c in range(2):
            dst[j, row0 // dil:(row0 + rows) // dil, c * 128:(c + 1) * 128] = (
                scr[c, pl.ds(j, rows // dil, stride=dil), :].astype(dst.dtype))


def _project_in(a, w_t, cos_t, sin_t, *, tm, name, after=None):
    n = a.shape[0]
    n_dil = len(DIL_DILATIONS)
    na_w, dil_w = 3 * NA_WIDTH, 3 * DIL_WIDTH
    chunk = min(EPILOGUE_ROWS, tm)
    extra = [] if after is None else [after]

    def body(a_ref, w_ref, cos_ref, sin_ref, *rest):
        na_ref, gate_ref = rest[len(extra):len(extra) + 2]
        outs, scr = rest[len(extra) + 2:len(extra) + 2 + 3 * n_dil], rest[-1]

        def part(r0, first, width):
            return lax.dot_general(a_ref[r0:r0 + chunk, :], w_ref[first:first + width, :], NT_DIMS,
                                   preferred_element_type=F32)

        for r0 in range(0, tm, chunk):
            na_ref[r0:r0 + chunk, :] = part(r0, 0, na_w).astype(BF16)
            dil_part = part(r0, na_w, dil_w)
            cosv, sinv = cos_ref[r0:r0 + chunk, :], sin_ref[r0:r0 + chunk, :]
            for t in range(3):
                for gi, dil in enumerate(DIL_DILATIONS):
                    c0 = (t * n_dil + gi) * 256
                    val = dil_part[:, c0:c0 + 256]
                    if t < 2:
                        val = val * cosv + _swap_halves(val) * sinv
                    _store_dil_order(val, outs[t * n_dil + gi], scr, dil, r0)
            gate_ref[r0:r0 + chunk, :] = _sigmoid(part(r0, na_w + dil_w, 2 * D_MODEL)).astype(BF16)

    out_specs = [_rows(tm, na_w), _rows(tm, 2 * D_MODEL)]
    out_shape = [_sds((n, na_w), BF16), _sds((n, 2 * D_MODEL), BF16)]
    for _ in range(3):
        for dil in DIL_DILATIONS:
            out_specs.append(pl.BlockSpec((dil, tm // dil, 256), lambda i: (0, i, 0)))
            out_shape.append(_sds((dil, n // dil, 256), BF16))
    res = pl.pallas_call(
        body, name=name, grid=(n // tm,),
        in_specs=[_rows(tm, D_MODEL), _const(w_t.shape), _rows(tm, 256), _rows(tm, 256)]
                 + [pl.BlockSpec(memory_space=pl.ANY)] * len(extra),
        out_specs=out_specs, out_shape=out_shape,
        scratch_shapes=[pltpu.VMEM((2, chunk, 128), F32)],
        compiler_params=_params("parallel"),
    )(a, w_t, cos_t, sin_t, *extra)
    return res[0], res[1], res[2:5], res[5:8], res[8:11]


def _residual_rms_tile(delta, h, g):
    hn = h + delta
    return hn, hn * _rstd(hn) * g


def _gate_mix_tile(b2, s1, b1, s2):
    return b2, s1.astype(F32) * b1.astype(F32) + s2.astype(F32) * b2


def _gate_bwd_tile(dm, s1, b1, s2, b2):
    s1, b1, s2, b2 = (t.astype(F32) for t in (s1, b1, s2, b2))
    return dm * s1, dm * s2, dm * b1 * s1 * (1.0 - s1), dm * b2 * s2 * (1.0 - s2)


def _tail_tile(gt, pp, h2, target, g):
    sg = _sigmoid(gt)
    h3 = h2 + sg * pp
    r3 = _rstd(h3)
    n3 = h3 * r3
    err = n3 * g - target
    loss = 0.5 * jnp.sum(jnp.sum(err * err, axis=-1, keepdims=True) / D_MODEL)
    dy = err / D_MODEL
    dn = dy * g
    dh3 = r3 * (dn - n3 * jnp.mean(dn * n3, axis=-1, keepdims=True))
    return (dh3, dh3 * sg, dh3 * pp * sg * (1.0 - sg),
            jnp.sum(dy * n3, axis=0, keepdims=True), jnp.full((1, gt.shape[1]), loss, F32))


def _rms_bwd_tile(dz, h, g, dres):
    r = _rstd(h)
    nrm = h * r
    dn = dz * g
    dh = dres + r * (dn - nrm * jnp.mean(dn * nrm, axis=-1, keepdims=True))
    return dh, jnp.sum(dz * nrm, axis=0, keepdims=True)


def _rms_bwd_twice(dz, h, g, dres):
    dh, dg = _rms_bwd_tile(dz, h, g, dres)
    return dh, dh, dg


def _tail_step(f, w_down, h1, w_pg, p, w_pp, target, g_final, g_ple, *, tm, name):
    n = f.shape[0]
    chunk = min(EPILOGUE_ROWS, tm)

    def body(f_ref, wd_ref, h1_ref, wg_ref, p_ref, wp_ref, t_ref, gf_ref, gp_ref,
             e_ref, dpp_ref, dgt_ref, dh2_ref, dh2b_ref, dgf_ref, loss_ref, dgp_ref):
        sums = None
        for r0 in range(0, tm, chunk):
            rows = slice(r0, r0 + chunk)
            delta = jnp.dot(f_ref[rows, :], wd_ref[...], preferred_element_type=F32)
            h2, e = _residual_rms_tile(delta, h1_ref[rows, :], gp_ref[...])
            e = e.astype(BF16)
            e_ref[rows, :] = e
            gt = jnp.dot(e, wg_ref[...], preferred_element_type=F32)
            pp = lax.dot_general(p_ref[rows, :], wp_ref[...], NT_DIMS, preferred_element_type=F32)
            dh3, dpp, dgt, dgf, loss = _tail_tile(gt, pp, h2, t_ref[rows, :], gf_ref[...])
            dgt = dgt.astype(BF16)
            dpp_ref[rows, :] = dpp.astype(BF16)
            dgt_ref[rows, :] = dgt
            dz = lax.dot_general(dgt, wg_ref[...], NT_DIMS, preferred_element_type=F32)
            dh2, dgp = _rms_bwd_tile(dz, h2, gp_ref[...], dh3)
            dh2_ref[rows, :] = dh2
            dh2b_ref[rows, :] = dh2.astype(BF16)
            vals = (dgf, loss, dgp)
            sums = vals if sums is None else [s + v for s, v in zip(sums, vals)]
        _add_colsums((dgf_ref, loss_ref, dgp_ref), sums, pl.program_id(0))

    wide, gain = _rows(tm, D_MODEL), _const((1, D_MODEL))
    return pl.pallas_call(
        body, name=name, grid=(n // tm,),
        in_specs=[_rows(tm, f.shape[1]), _const(w_down.shape), wide, _const(w_pg.shape),
                  _rows(tm, p.shape[1]), _const(w_pp.shape), wide, gain, gain],
        out_specs=[wide] * 5 + [gain] * 3,
        out_shape=[_sds((n, D_MODEL), dt) for dt in (BF16, BF16, BF16, F32, BF16)]
                  + [_sds((1, D_MODEL), F32)] * 3,
        compiler_params=_params("arbitrary"),
    )(f, w_down, h1, w_pg, p, w_pp, target, g_final, g_ple)


def _assemble_dproj(dna, ddil_q, ddil_k, ddil_v, dgn, dgd, cos_t, sin_t, *, tm, name):
    n = dgn.shape[0]

    def body(*refs):
        dq_ref, dk_ref, dv_ref = refs[0:3]
        dil_in = refs[3:12]
        dgn_ref, dgd_ref, cos_ref, sin_ref, o_ref, scr = refs[12:18]
        o_ref[:, 0:512] = dq_ref[...]
        o_ref[:, 512:1024] = dk_ref[...].astype(BF16)
        o_ref[:, 1024:1536] = dv_ref[...].astype(BF16)
        cosv, sinv = cos_ref[...], sin_ref[...]
        for t in range(3):
            for gi, dil in enumerate(DIL_DILATIONS):
                val = _load_token_order(dil_in[t * 3 + gi], scr, dil, tm)
                if t < 2:
                    val = val * cosv + _swap_halves(val * sinv)
                c0 = 1536 + t * DIL_WIDTH + gi * 256
                o_ref[:, c0:c0 + 256] = val.astype(BF16)
        o_ref[:, 3840:4864] = dgn_ref[...]
        o_ref[:, 4864:5888] = dgd_ref[...]

    in_specs = [_rows(tm, NA_WIDTH)] * 3
    for _ in range(3):
        for dil in DIL_DILATIONS:
            in_specs.append(pl.BlockSpec((dil, tm // dil, 256), lambda i: (0, i, 0)))
    in_specs += [_rows(tm, D_MODEL)] * 2 + [_rows(tm, 256)] * 2
    return pl.pallas_call(
        body, name=name, grid=(n // tm,), in_specs=in_specs,
        out_specs=_rows(tm, IN_WIDTH), out_shape=_sds((n, IN_WIDTH), BF16),
        scratch_shapes=[_dil_scratch(tm)],
        compiler_params=_params("parallel"),
    )(*dna, *ddil_q, *ddil_k, *ddil_v, dgn, dgd, cos_t, sin_t)


N_ROW_OFF = 2 * NA_WIN_ROWS - 1
N_PAIRS = N_ROW_OFF - 1
RB_WIDTH = (N_ROW_OFF + 1) * GRID_W


def _na_bias(rb_ref, pair_scr):
    shape = (GRID_W, RB_WIDTH)
    qc = lax.broadcasted_iota(jnp.int32, shape, 0)
    qc2 = lax.broadcasted_iota(jnp.int32, (GRID_W, 128), 0)
    kc2 = lax.broadcasted_iota(jnp.int32, (GRID_W, 128), 1) & (GRID_W - 1)
    cs = jnp.clip(qc2 - 8, 0, GRID_W - 16)
    valid = (kc2 >= cs) & (kc2 < cs + 16)
    for hh in range(2):
        t = jnp.broadcast_to(rb_ref[hh], shape)
        t = pltpu.roll(t, RB_WIDTH - 15, 1)
        for b in range(6):
            t = jnp.where(((qc >> b) & 1) == 1, pltpu.roll(t, 1 << b, 1), t)
        t_odd = pltpu.roll(t, RB_WIDTH - GRID_W, 1)
        for ro in range(N_PAIRS):
            src = t if ro % 2 == 0 else t_odd
            base = (ro // 2) * 128
            pair_scr[hh, ro] = jnp.where(valid, src[:, base:base + 128], NEG_INF)


NA_GROUP_FWD = 8
NA_GROUP_BWD = 4


def _stack_heads(ref, r, scale=1.0):
    lane = lax.broadcasted_iota(jnp.int32, (GRID_W, 128), 1)
    t = ref[pl.ds(pl.multiple_of(r * GRID_W, GRID_W), GRID_W), :].astype(F32) * scale
    return jnp.concatenate([jnp.where(lane < 64, t, 0.0), jnp.where(lane >= 64, t, 0.0)], axis=0).astype(BF16)


def _unstack_heads(t2):
    lane = lax.broadcasted_iota(jnp.int32, (GRID_W, 128), 1)
    return jnp.where(lane < 64, t2[:GRID_W], t2[GRID_W:])


def _na_window(k_ref, v_ref, r, n_rows):
    rs = jnp.clip(r - NA_WIN_ROWS // 2, 0, n_rows - NA_WIN_ROWS)
    ro0 = (NA_WIN_ROWS - 1) - (r - rs)
    off = pl.multiple_of(rs * GRID_W, GRID_W)
    kw = k_ref[pl.ds(off, NA_WIN_ROWS * GRID_W), :]
    vw = v_ref[pl.ds(off, NA_WIN_ROWS * GRID_W), :]
    return kw, vw, off, ro0


def _na_probs(s_raw, pair_scr, ro0):
    bias = [jnp.concatenate([pair_scr[hh, ro0 + 2 * j] for j in range(NA_WIN_ROWS // 2)], axis=1)
            for hh in range(2)]
    s = s_raw + jnp.concatenate(bias, axis=0)
    m = jnp.max(s, axis=-1, keepdims=True)
    e = jnp.exp(s - m)
    return e * (1.0 / jnp.sum(e, axis=-1, keepdims=True))


def _na_qkv_specs(n):
    pairs = NA_WIDTH // 128
    return [pl.BlockSpec((n, 128), lambda h, first=t * pairs: (0, first + h)) for t in range(3)]


def _na_fwd(qkv, rb, *, name):
    n = qkv.shape[0]
    n_rows = n // GRID_W

    def body(ins, outs, scr):
        q_ref, k_ref, v_ref, rb_ref = ins
        o_ref, = outs
        pair_scr, = scr
        _na_bias(rb_ref, pair_scr)

        def group(g, carry):
            rows = [g * NA_GROUP_FWD + t for t in range(NA_GROUP_FWD)]
            wins = [_na_window(k_ref, v_ref, r, n_rows) for r in rows]
            raw = [lax.dot_general(_stack_heads(q_ref, r, QK_SCALE), w[0], NT_DIMS, preferred_element_type=F32)
                   for r, w in zip(rows, wins)]
            probs = [_na_probs(s, pair_scr, w[3]) for s, w in zip(raw, wins)]
            outs2 = [jnp.dot(p.astype(BF16), w[1], preferred_element_type=F32) for p, w in zip(probs, wins)]
            for r, o2 in zip(rows, outs2):
                o_ref[pl.ds(pl.multiple_of(r * GRID_W, GRID_W), GRID_W), :] = _unstack_heads(o2).astype(BF16)
            return carry

        lax.fori_loop(0, n_rows // NA_GROUP_FWD, group, 0)

    col = pl.BlockSpec((n, 128), lambda h: (0, h))
    return _call(
        body, name=name, grid=(NA_WIDTH // 128,),
        in_specs=_na_qkv_specs(n) + [pl.BlockSpec((2, 1, RB_WIDTH), lambda h: (h, 0, 0))],
        out_specs=[col], out_shape=[_sds((n, NA_WIDTH), BF16)],
        scratch_shapes=[pltpu.VMEM((2, N_PAIRS, GRID_W, 128), F32)],
        args=(qkv, qkv, qkv, rb))[0]


def _na_bwd(qkv, do, rb, *, name, after=None):
    n = qkv.shape[0]
    n_rows = n // GRID_W
    win = NA_WIN_ROWS * GRID_W

    def body(ins, outs, scr):
        q_ref, k_ref, v_ref, do_ref, rb_ref = ins
        dq_ref, dk_ref, dv_ref, drb_ref = outs
        pair_scr, acc_scr = scr
        _na_bias(rb_ref, pair_scr)
        acc_scr[...] = jnp.zeros_like(acc_scr)
        dk_ref[...] = jnp.zeros_like(dk_ref)
        dv_ref[...] = jnp.zeros_like(dv_ref)

        def group(g, carry):
            rows = [g * NA_GROUP_BWD + t for t in range(NA_GROUP_BWD)]
            wins = [_na_window(k_ref, v_ref, r, n_rows) for r in rows]
            qss = [_stack_heads(q_ref, r, QK_SCALE) for r in rows]
            doss = [_stack_heads(do_ref, r) for r in rows]
            raw = [lax.dot_general(qs, w[0], NT_DIMS, preferred_element_type=F32) for qs, w in zip(qss, wins)]
            dps = [lax.dot_general(dos, w[1], NT_DIMS, preferred_element_type=F32) for dos, w in zip(doss, wins)]
            probs = [_na_probs(s, pair_scr, w[3]) for s, w in zip(raw, wins)]
            dss = [p * (dp - jnp.sum(p * dp, axis=-1, keepdims=True)) for p, dp in zip(probs, dps)]
            dsbs = [ds.astype(BF16) for ds in dss]
            dq2s = [jnp.dot(dsb, w[0], preferred_element_type=F32) for dsb, w in zip(dsbs, wins)]
            dkws = [lax.dot_general(dsb, qs, TN_DIMS, preferred_element_type=F32) for dsb, qs in zip(dsbs, qss)]
            dvws = [lax.dot_general(p.astype(BF16), dos, TN_DIMS, preferred_element_type=F32)
                    for p, dos in zip(probs, doss)]
            for t, r in enumerate(rows):
                _, _, off, ro0 = wins[t]
                for hh in range(2):
                    for j in range(NA_WIN_ROWS // 2):
                        acc_scr[hh, ro0 + 2 * j] += dss[t][hh * GRID_W:(hh + 1) * GRID_W, j * 128:(j + 1) * 128]
                dq_ref[pl.ds(pl.multiple_of(r * GRID_W, GRID_W), GRID_W), :] = (
                    _unstack_heads(dq2s[t]) * QK_SCALE).astype(BF16)
                dk_ref[pl.ds(off, win), :] += dkws[t]
                dv_ref[pl.ds(off, win), :] += dvws[t]
            return carry

        lax.fori_loop(0, n_rows // NA_GROUP_BWD, group, 0)

        qc = lax.broadcasted_iota(jnp.int32, (N_PAIRS * GRID_W, 128), 0)
        for hh in range(2):
            t = acc_scr[hh].reshape(N_PAIRS * GRID_W, 128)
            for b in range(6):
                t = jnp.where(((qc >> b) & 1) == 1, pltpu.roll(t, 128 - (1 << b), 1), t)
            t = pltpu.roll(t, 15, 1)
            drb_ref[hh] = jnp.sum(t.reshape(N_PAIRS, GRID_W, 128), axis=1)

    col = pl.BlockSpec((n, 128), lambda h: (0, h))
    return _call(
        body, name=name, grid=(NA_WIDTH // 128,),
        in_specs=_na_qkv_specs(n) + [col, pl.BlockSpec((2, 1, RB_WIDTH), lambda h: (h, 0, 0))],
        out_specs=[col, col, col, pl.BlockSpec((2, N_PAIRS, 128), lambda h: (h, 0, 0))],
        out_shape=[_sds((n, NA_WIDTH), BF16), _sds((n, NA_WIDTH), F32), _sds((n, NA_WIDTH), F32),
                   _sds((8, N_PAIRS, 128), F32)],
        scratch_shapes=[pltpu.VMEM((2, N_PAIRS, GRID_W, 128), F32),
                        pltpu.VMEM((2, N_PAIRS, GRID_W, 128), F32)],
        args=(qkv, qkv, qkv, do, rb), after=after)


def _rpb_table(rpb2):
    t = jnp.pad(rpb2, ((0, 0), (0, 1), (0, GRID_W - rpb2.shape[-1])))
    return t.reshape(8, 1, RB_WIDTH)


def _rpb_grad(drb, *, name):
    kdim = drb.shape[1]

    def body(x_ref, o_ref):
        kk = lax.broadcasted_iota(jnp.int32, (128, 512), 0)
        jj = lax.broadcasted_iota(jnp.int32, (128, 512), 1)
        half, co = kk >> 6, kk & 63
        acc = jnp.zeros((8, 512), F32)
        for ro in range(N_PAIRS):
            hit = ((ro + half) == (jj >> 5)) & (co == (jj & 31)) & (co < 31)
            onehot = jnp.where(hit, 1.0, 0.0).astype(F32)
            acc = acc + jnp.dot(x_ref[:, ro * 128:(ro + 1) * 128], onehot, preferred_element_type=F32,
                                precision=lax.Precision.HIGHEST)
        o_ref[...] = acc

    return pl.pallas_call(
        body, name=name, grid=(1,),
        in_specs=[_const((8, kdim))], out_specs=_const((8, 512)), out_shape=_sds((8, 512), F32),
        compiler_params=_params("arbitrary"),
    )(drb)


DIL_GROUP = 2


def _dil_blocks(length):
    qb = min(128, length)
    return qb, min(qb + 2 * DIL_RADIUS, length), min(DIL_GROUP, length // qb)


def _stack_lanes(ref, t, qb, scale=1.0):
    lane = lax.broadcasted_iota(jnp.int32, (qb, 256), 1)
    val = ref[0, t * qb:(t + 1) * qb, :].astype(F32) * scale
    return jnp.concatenate([jnp.where((lane >> 6) == h, val, 0.0) for h in range(4)], axis=0).astype(BF16)


def _dil_window(k_ref, v_ref, blk, qb, win, length):
    start = pl.multiple_of(jnp.clip(blk * qb - DIL_RADIUS, 0, length - win), DIL_RADIUS)
    return k_ref[0, pl.ds(start, win), :], v_ref[0, pl.ds(start, win), :], start


def _dil_caps_init(caps_scr, qb, win):
    @pl.when((pl.program_id(0) == 0) & (pl.program_id(1) == 0))
    def _():
        gap = ((lax.broadcasted_iota(jnp.int32, (4 * qb, win), 0) & (qb - 1))
               - lax.broadcasted_iota(jnp.int32, (4 * qb, win), 1))
        for v in range(3):
            caps_scr[v] = jnp.where(jnp.abs(gap + v * DIL_RADIUS) <= DIL_RADIUS, jnp.inf, NEG_INF)


def _dil_mask(s, blk, start, qb, caps_scr):
    return jnp.minimum(s, caps_scr[(blk * qb - start) // DIL_RADIUS])


def _pick_heads(stacked, qb):
    lane = lax.broadcasted_iota(jnp.int32, (qb, 256), 1)
    out = jnp.zeros((qb, 256), stacked.dtype)
    for h in range(4):
        out = jnp.where((lane >> 6) == h, stacked[h * qb:(h + 1) * qb], out)
    return out


def _stack_head_cols(ref, t, qb):
    return jnp.concatenate([ref[0, t * qb:(t + 1) * qb, 64 * h:64 * h + 1] for h in range(4)], axis=0)


def _dil_fwd(q, k, v, *, name, after=None):
    dil, length, _ = q.shape
    qb, win, grp = _dil_blocks(length)
    extra = [] if after is None else [after]

    def body(q_ref, k_ref, v_ref, *rest):
        o_ref, lse_ref, caps_scr = rest[-3:]
        _dil_caps_init(caps_scr, qb, win)
        blks = [pl.program_id(1) * grp + t for t in range(grp)]
        wins = [_dil_window(k_ref, v_ref, b, qb, win, length) for b in blks]
        raw = [lax.dot_general(_stack_lanes(q_ref, t, qb, QK_SCALE), w[0], NT_DIMS, preferred_element_type=F32)
               for t, w in enumerate(wins)]
        lses, outs = [], []
        for t, (s, w) in enumerate(zip(raw, wins)):
            s = _dil_mask(s, blks[t], w[2], qb, caps_scr)
            m = jnp.max(s, axis=-1, keepdims=True)
            e = jnp.exp(s - m)
            norm = jnp.sum(e, axis=-1, keepdims=True)
            lses.append(m + jnp.log(norm))
            outs.append(jnp.dot((e * (1.0 / norm)).astype(BF16), w[1], preferred_element_type=F32))
        for t in range(grp):
            o_ref[0, t * qb:(t + 1) * qb, :] = _pick_heads(outs[t], qb)
            lse_ref[0, t * qb:(t + 1) * qb, :] = _pick_heads(jnp.broadcast_to(lses[t], (4 * qb, 256)), qb)

    seq = pl.BlockSpec((1, length, 256), lambda j, i: (j, 0, 0))
    blk = pl.BlockSpec((1, grp * qb, 256), lambda j, i: (j, i, 0))
    return pl.pallas_call(
        body, name=name, grid=(dil, length // (grp * qb)),
        in_specs=[blk, seq, seq] + [pl.BlockSpec(memory_space=pl.ANY)] * len(extra), out_specs=[blk, blk],
        out_shape=[_sds((dil, length, 256), F32)] * 2,
        scratch_shapes=[pltpu.VMEM((3, 4 * qb, win), F32)],
        compiler_params=_params("arbitrary", "arbitrary"),
    )(q, k, v, *extra)


def _dil_bwd(q, k, v, do, lse, cc, *, name):
    dil, length, _ = q.shape
    qb, win, grp = _dil_blocks(length)

    def body(q_ref, k_ref, v_ref, do_ref, lse_ref, cc_ref, dq_ref, dk_ref, dv_ref, caps_scr):
        _dil_caps_init(caps_scr, qb, win)

        @pl.when(pl.program_id(1) == 0)
        def _():
            dk_ref[...] = jnp.zeros_like(dk_ref)
            dv_ref[...] = jnp.zeros_like(dv_ref)

        blks = [pl.program_id(1) * grp + t for t in range(grp)]
        wins = [_dil_window(k_ref, v_ref, b, qb, win, length) for b in blks]
        qss = [_stack_lanes(q_ref, t, qb, QK_SCALE) for t in range(grp)]
        doss = [_stack_lanes(do_ref, t, qb) for t in range(grp)]
        raw = [lax.dot_general(qs, w[0], NT_DIMS, preferred_element_type=F32) for qs, w in zip(qss, wins)]
        dps = [lax.dot_general(dos, w[1], NT_DIMS, preferred_element_type=F32) for dos, w in zip(doss, wins)]
        probs = [jnp.exp(_dil_mask(s, blks[t], wins[t][2], qb, caps_scr) - _stack_head_cols(lse_ref, t, qb))
                 for t, s in enumerate(raw)]
        dsbs = [(p * (dp + _stack_head_cols(cc_ref, t, qb))).astype(BF16)
                for t, (p, dp) in enumerate(zip(probs, dps))]
        dq4s = [jnp.dot(dsb, w[0], preferred_element_type=F32) for dsb, w in zip(dsbs, wins)]
        dkws = [lax.dot_general(dsb, qs, TN_DIMS, preferred_element_type=F32) for dsb, qs in zip(dsbs, qss)]
        dvws = [lax.dot_general(p.astype(BF16), dos, TN_DIMS, preferred_element_type=F32)
                for p, dos in zip(probs, doss)]
        for t in range(grp):
            dq_ref[0, t * qb:(t + 1) * qb, :] = _pick_heads(dq4s[t], qb) * QK_SCALE
            dk_ref[0, pl.ds(wins[t][2], win), :] += dkws[t]
            dv_ref[0, pl.ds(wins[t][2], win), :] += dvws[t]

    seq = pl.BlockSpec((1, length, 256), lambda j, i: (j, 0, 0))
    blk = pl.BlockSpec((1, grp * qb, 256), lambda j, i: (j, i, 0))
    return pl.pallas_call(
        body, name=name, grid=(dil, length // (grp * qb)),
        in_specs=[blk, seq, seq, blk, blk, blk], out_specs=[blk, seq, seq],
        out_shape=[_sds((dil, length, 256), F32)] * 3,
        scratch_shapes=[pltpu.VMEM((3, 4 * qb, win), F32)],
        compiler_params=_params("arbitrary", "arbitrary"),
    )(q, k, v, do, lse, cc)


def _merge_weights(lses):
    m = jnp.maximum(jnp.maximum(lses[0], lses[1]), lses[2])
    es = [jnp.exp(t - m) for t in lses]
    inv = 1.0 / (es[0] + es[1] + es[2])
    return [e * inv for e in es]


def _branch_mix(y_na, w_bna, outs, lses, w_bd, gates, *, tm, name):
    n = y_na.shape[0]
    chunk = min(EPILOGUE_ROWS, tm)

    def body(yna_ref, wn_ref, *rest):
        o_in, l_in = rest[0:3], rest[3:6]
        wd_ref, sn_ref, sd_ref = rest[6:9]
        y_ref, yb_ref, bn_ref, bd_ref, mix_ref, scr = rest[9:15]
        for r0 in range(0, tm, chunk):
            rows = slice(r0, r0 + chunk)
            lv = [_load_token_order(l_in[g], scr, d, chunk, r0) for g, d in enumerate(DIL_DILATIONS)]
            ws = _merge_weights(lv)
            y = jnp.zeros((chunk, 256), F32)
            for g, d in enumerate(DIL_DILATIONS):
                y = y + ws[g] * _load_token_order(o_in[g], scr, d, chunk, r0)
            yb = y.astype(BF16)
            y_ref[rows, :] = y
            yb_ref[rows, :] = yb
            bn = lax.dot_general(yna_ref[rows, :], wn_ref[...], NT_DIMS, preferred_element_type=F32).astype(BF16)
            bd = lax.dot_general(yb, wd_ref[...], NT_DIMS, preferred_element_type=F32)
            bn_ref[rows, :] = bn
            bd, mixed = _gate_mix_tile(bd, sn_ref[rows, :], bn, sd_ref[rows, :])
            bd_ref[rows, :] = bd.astype(BF16)
            mix_ref[rows, :] = mixed.astype(BF16)

    specs = [_dil_spec(d, tm) for d in DIL_DILATIONS]
    return pl.pallas_call(
        body, name=name, grid=(n // tm,),
        in_specs=[_rows(tm, NA_WIDTH), _const(w_bna.shape)] + specs + specs
                 + [_const(w_bd.shape), _rows(tm, D_MODEL, 0), _rows(tm, D_MODEL, 1)],
        out_specs=[_rows(tm, 256)] * 2 + [_rows(tm, D_MODEL)] * 3,
        out_shape=[_sds((n, 256), F32), _sds((n, 256), BF16)] + [_sds((n, D_MODEL), BF16)] * 3,
        scratch_shapes=[_dil_scratch(chunk)],
        compiler_params=_params("parallel"),
    )(y_na, w_bna, *outs, *lses, w_bd, gates, gates)


def _branch_bwd(dh, w_out, gates, bn, bd, w_bna, w_bd, y, lses, *, tm, name):
    n = dh.shape[0]
    chunk = min(EPILOGUE_ROWS, tm)

    def body(dh_ref, wo_ref, sn_ref, sd_ref, bn_ref, bd_ref, wn_ref, wd_ref, y_ref, *rest):
        l_in = rest[0:3]
        dbn_ref, dbd_ref, dgn_ref, dgd_ref, dyna_ref = rest[3:8]
        do_out, cc_out, scr = rest[8:11], rest[11:14], rest[14]
        rr = lax.broadcasted_iota(jnp.int32, (256, 256), 0) >> 6
        cc = lax.broadcasted_iota(jnp.int32, (256, 256), 1) >> 6
        ones = jnp.where(rr == cc, 1.0, 0.0).astype(F32)
        for r0 in range(0, tm, chunk):
            rows = slice(r0, r0 + chunk)
            dm = lax.dot_general(dh_ref[rows, :], wo_ref[...], NT_DIMS, preferred_element_type=F32)
            dbn, dbd, dgn, dgd = (t.astype(BF16) for t in _gate_bwd_tile(
                dm, sn_ref[rows, :], bn_ref[rows, :], sd_ref[rows, :], bd_ref[rows, :]))
            dbn_ref[rows, :] = dbn
            dbd_ref[rows, :] = dbd
            dgn_ref[rows, :] = dgn
            dgd_ref[rows, :] = dgd
            dyna_ref[rows, :] = jnp.dot(dbn, wn_ref[...], preferred_element_type=F32).astype(BF16)
            dyv = jnp.dot(dbd, wd_ref[...], preferred_element_type=F32)
            lv = [_load_token_order(l_in[g], scr, d, chunk, r0) for g, d in enumerate(DIL_DILATIONS)]
            ws = _merge_weights(lv)
            tsum = jnp.dot(dyv * y_ref[rows, :], ones, preferred_element_type=F32,
                           precision=lax.Precision.HIGHEST)
            for g, d in enumerate(DIL_DILATIONS):
                _store_dil_order(ws[g] * dyv, do_out[g], scr, d, r0)
                _store_dil_order(-ws[g] * tsum, cc_out[g], scr, d, r0)

    specs = [_dil_spec(d, tm) for d in DIL_DILATIONS]
    wide = _rows(tm, D_MODEL)
    res = pl.pallas_call(
        body, name=name, grid=(n // tm,),
        in_specs=[wide, _const(w_out.shape), _rows(tm, D_MODEL, 0), _rows(tm, D_MODEL, 1), wide, wide,
                  _const(w_bna.shape), _const(w_bd.shape), _rows(tm, 256)] + specs,
        out_specs=[wide] * 4 + [_rows(tm, NA_WIDTH)] + specs + specs,
        out_shape=[_sds((n, D_MODEL), BF16)] * 4 + [_sds((n, NA_WIDTH), BF16)]
                  + [_sds((d, n // d, 256), BF16) for d in DIL_DILATIONS]
                  + [_sds((d, n // d, 256), F32) for d in DIL_DILATIONS],
        scratch_shapes=[_dil_scratch(chunk)],
        compiler_params=_params("parallel"),
    )(dh, w_out, gates, gates, bn, bd, w_bna, w_bd, y, *lses)
    return res[0], res[1], res[2], res[3], res[4], res[5:8], res[8:11]


_WEIGHTS = (("w_in", 1, 736), ("w_branch_na", 1, 128), ("w_branch_dil", 1, 128), ("w_out", 0, 128),
            ("w_up", 1, 512), ("w_down", 0, 512), ("w_ple_gate", 0, 128), ("w_ple_proj", 1, 128))
_W_IN, _W_BNA, _W_BD, _W_OUT, _W_UP, _W_DOWN, _W_PG, _W_PP = range(8)


def _to_full(gathered):
    return gathered.reshape(-1, gathered.shape[2])


def _to_chunks(widx, mat):
    return mat.reshape(N_DEV, _WEIGHTS[widx][2], mat.shape[1])


def _local_step(x, p_bf16, positions, target, g_mix, g_mlp, g_ple, g_final, rpb2,
                get_w_in, relay_rest, get_rest, send_grads):
    tm = 512
    half = HEAD_DIM // 2
    inv_freq = 10000.0 ** (-jnp.arange(half, dtype=F32) / half)
    ang = positions.astype(F32)[:, None] * inv_freq
    cos, sin = jnp.cos(ang), jnp.sin(ang)
    cos_t = jnp.tile(jnp.concatenate([cos, cos], axis=-1), (1, 4))
    sin_t = jnp.tile(jnp.concatenate([-sin, sin], axis=-1), (1, 4))
    rb = _rpb_table(rpb2)

    a = _rms_fwd(x, g_mix, tm=tm, name="rms_mix")
    w_in, token = get_w_in(a)
    na_qkv, gates, dq_g, dk_g, dv_g = _project_in(a, w_in, cos_t, sin_t, tm=512, name="mm_in", after=token)
    y_na = _na_fwd(na_qkv, rb, name="na_fwd")
    token = relay_rest(y_na)
    d_out, d_lse = [], []
    for g in range(3):
        o, lse = _dil_fwd(dq_g[g], dk_g[g], dv_g[g], name=f"dil_fwd{g}", after=token if g == 0 else None)
        d_out.append(o)
        d_lse.append(lse)
    w_bna, w_bd = get_rest(d_out[2], 0)
    y_dil, y_dil_b, bn, bd, mixed = _branch_mix(y_na, w_bna, d_out, d_lse, w_bd, gates, tm=tm, name="branch_mix")
    w_out, w_up, w_down, w_pg, w_pp = get_rest(mixed, 1)
    h1, c = _matmul(mixed, w_out, out_dtype=(F32, BF16), tm=512, tn=1024, tk=1024, name="mm_out",
                    extra=(x, g_mlp), epilogue=_residual_rms_tile)
    u, f = _matmul(c, w_up, tb=True, out_dtype=(BF16, BF16), tm=512, tn=2048, tk=1024, name="mm_up",
                   epilogue=lambda acc: (acc, jnp.square(jnp.maximum(acc, 0.0))))

    e, dpp, dgt, dh2, dh2_b, dg_final, loss, dg_ple = _tail_step(
        f, w_down, h1, w_pg, p_bf16, w_pp, target, g_final, g_ple, tm=256, name="tail_step")
    loss = loss[:, :128]
    gw_pp = _matmul(p_bf16, dpp, ta=True, transpose_out=True, out_dtype=BF16, tm=256, tn=1024, tk=2048,
                    name="mm_gw_pp")
    gw_pg = _matmul(e, dgt, ta=True, out_dtype=BF16, tm=512, tn=1024, tk=2048, name="mm_gw_pg")
    du = _matmul(dh2_b, w_down, tb=True, out_dtype=BF16, tm=512, tn=2048, tk=1024, name="mm_du",
                 extra=(u,), epilogue=lambda acc, uv: (acc * (2.0 * jnp.maximum(uv.astype(F32), 0.0)),))
    gw_down = _matmul(f, dh2_b, ta=True, out_dtype=BF16, tm=1024, tn=1024, tk=2048, name="mm_gw_down")
    gw_up = _matmul(c, du, ta=True, transpose_out=True, out_dtype=BF16, tm=512, tn=2048, tk=2048, name="mm_gw_up")
    dh1, dh1_b, dg_mlp = _matmul(
        du, w_up, out_dtype=(F32, BF16), tm=512, tn=1024, tk=4096, name="mm_dc",
        extra=(h1, g_mlp, dh2), epilogue=_rms_bwd_twice, n_colsum=1)
    dbn, dbd, dgn, dgd, dy_na, do_g, cc_g = _branch_bwd(dh1_b, w_out, gates, bn, bd, w_bna, w_bd, y_dil, d_lse,
                                                        tm=tm, name="branch_bwd")
    gw_out = _matmul(mixed, dh1_b, ta=True, out_dtype=BF16, tm=512, tn=1024, tk=2048, name="mm_gw_out")
    gw_bna = _matmul(y_na, dbn, ta=True, transpose_out=True, out_dtype=BF16, tm=512, tn=1024, tk=2048,
                     name="mm_gw_bna")
    gw_bd = _matmul(y_dil_b, dbd, ta=True, transpose_out=True, out_dtype=BF16, tm=256, tn=1024, tk=2048,
                    name="mm_gw_bd")
    token = send_grads((_W_PP, _W_PG, _W_DOWN, _W_UP, _W_OUT, _W_BNA, _W_BD),
                       (gw_pp, gw_pg, gw_down, gw_up, gw_out, gw_bna, gw_bd))
    dna = _na_bwd(na_qkv, dy_na, rb, name="na_bwd", after=token)
    drpb = _rpb_grad(dna[3].reshape(8, -1), name="rpb_grad")
    ddq, ddk, ddv = [], [], []
    for g in range(3):
        r = _dil_bwd(dq_g[g], dk_g[g], dv_g[g], do_g[g], d_lse[g], cc_g[g], name=f"dil_bwd{g}")
        ddq.append(r[0])
        ddk.append(r[1])
        ddv.append(r[2])
    dproj = _assemble_dproj(dna[0:3], ddq, ddk, ddv, dgn, dgd, cos_t, sin_t, tm=tm, name="assemble_dproj")
    gw_in = _matmul(a, dproj, ta=True, transpose_out=True, out_dtype=BF16, tm=512, tn=2944, tk=2048, name="mm_gw_in")
    token = send_grads((_W_IN,), (gw_in,))
    dx, dg_mix = _matmul(
        dproj, w_in, out_dtype=(F32,), tm=512, tn=1024, tk=5888, name="mm_da", after=token,
        extra=(x, g_mix, dh1), epilogue=_rms_bwd_tile, n_colsum=1)
    return loss, dx, (dg_mix, dg_mlp, dg_ple, dg_final), drpb


def _cast_bf16(t, *, name):
    def body(t_ref, o_ref):
        o_ref[...] = t_ref[...].astype(BF16)

    rows, cols = t.shape
    tr = 256 if rows % 256 == 0 else rows
    blk = pl.BlockSpec((tr, cols), lambda i: (i, 0))
    return pl.pallas_call(body, name=name, grid=(rows // tr,), in_specs=[blk], out_specs=blk,
                          out_shape=_sds(t.shape, BF16), compiler_params=_params("parallel"))(t)


def _adamw(w, g, m, v):
    m = ADAM_B1 * m + (1.0 - ADAM_B1) * g
    v = ADAM_B2 * v + (1.0 - ADAM_B2) * (g * g)
    m_hat = m / (1.0 - ADAM_B1 ** ADAM_STEP)
    v_hat = v / (1.0 - ADAM_B2 ** ADAM_STEP)
    delta = -ADAM_LR * (m_hat / (jnp.sqrt(v_hat) + ADAM_EPS) + ADAM_WD * w)
    return delta, m, v


def _sum_adamw(parts, w, m, v, *, tr, name, own=None, transposed=False):
    rows, cols = w.shape
    n_pre = 0 if own is None else 1

    def body(*refs):
        p_ref, w_ref, m_ref, v_ref = refs[n_pre:n_pre + 4]
        g_ref, d_ref, nm_ref, nv_ref = refs[-4:]
        g = (p_ref[0] if own is None else refs[n_pre + 4][...]).astype(F32)
        for s in range(1, N_DEV):
            g = g + p_ref[s].astype(F32)
        if transposed:
            g = g.T
        g_ref[...] = g
        d_ref[...], nm_ref[...], nv_ref[...] = _adamw(w_ref[...], g, m_ref[...], v_ref[...])

    if transposed:
        blk = pl.BlockSpec((rows, tr), lambda i, *_: (0, i))
        g_rows, steps = rows, cols // tr
    else:
        blk = pl.BlockSpec((tr, cols), lambda i, *_: (i, 0))
        g_rows, steps = cols, rows // tr
    in_specs = [pl.BlockSpec((N_DEV, tr, g_rows), lambda i, *_: (0, i, 0)), blk, blk, blk]
    args = [parts, w, m, v]
    if own is not None:
        in_specs.append(pl.BlockSpec((None, tr, g_rows), lambda i, idx: (idx[0], i, 0)))
        args = [own[1]] + args + [own[0]]
    return pl.pallas_call(
        body, name=name,
        grid_spec=pltpu.PrefetchScalarGridSpec(num_scalar_prefetch=n_pre, grid=(steps,), in_specs=in_specs,
                                               out_specs=[blk] * 4),
        out_shape=[_sds((rows, cols), F32)] * 4,
        compiler_params=_params("parallel"),
    )(*args)


_RPB_SIZE = 8 * 15 * 31


def _pack_small(g_mix, g_mlp, g_ple, g_final, rpb, loss_row):
    flat = jnp.concatenate([g_mix.reshape(-1), g_mlp.reshape(-1), g_ple.reshape(-1), g_final.reshape(-1),
                            rpb.reshape(-1), jnp.zeros((3840 - _RPB_SIZE,), F32), loss_row.reshape(-1),
                            jnp.zeros((128,), F32)])
    return flat.reshape(64, 128)


def _unpack_small(t):
    flat = t.reshape(-1)
    return (flat[0:1024].reshape(1, 1024), flat[4096:4096 + _RPB_SIZE].reshape(1, 8, 15, 31),
            flat[1024:2048].reshape(1, 1024), flat[2048:3072].reshape(1, 1024), flat[3072:4096])


def kernel(x, p, positions, g_mix, w_in, rpb, w_branch_na, w_branch_dil, w_out, g_mlp, w_up, w_down, g_ple, w_ple_gate, w_ple_proj, g_final, loss_target, m_g_mix, m_w_in, m_rpb, m_w_branch_na, m_w_branch_dil, m_w_out, m_g_mlp, m_w_up, m_w_down, m_g_ple, m_w_ple_gate, m_w_ple_proj, m_g_final, v_g_mix, v_w_in, v_rpb, v_w_branch_na, v_w_branch_dil, v_w_out, v_g_mlp, v_w_up, v_w_down, v_g_ple, v_w_ple_gate, v_w_ple_proj, v_g_final):
    sharded = dict(w_in=(w_in, m_w_in, v_w_in), w_branch_na=(w_branch_na, m_w_branch_na, v_w_branch_na),
                   w_branch_dil=(w_branch_dil, m_w_branch_dil, v_w_branch_dil), w_out=(w_out, m_w_out, v_w_out),
                   w_up=(w_up, m_w_up, v_w_up), w_down=(w_down, m_w_down, v_w_down),
                   w_ple_gate=(w_ple_gate, m_w_ple_gate, v_w_ple_gate),
                   w_ple_proj=(w_ple_proj, m_w_ple_proj, v_w_ple_proj))
    shards = {k: tuple(t[0] for t in val) for k, val in sharded.items()}

    me = _my_index()

    shards["w_in"] = tuple(t.T for t in shards["w_in"])

    w_in_b = _cast_bf16(shards["w_in"][0], name="cast_w_in")
    rest_b = [shards[name][0].astype(BF16).T if axis == 1 else shards[name][0].astype(BF16)
              for name, axis, _ in _WEIGHTS[1:]]
    first_in, token_in = _start_copies(_first_leg_copies, [w_in_b], [_sds((N_DEV,) + w_in_b.shape, BF16)], 4,
                                       name="start_gather_w_in")

    def whole(landed, mine):
        return _to_full(lax.dynamic_update_index_in_dim(landed, mine, me, 0))

    rest = {}

    def get_w_in(after):
        (mine,), landed = _wait_copies(_first_leg_copies, first_in, after, name="wait_gather_w_in")
        second, token = _start_copies(_second_leg_copies, [], landed, 3, name="start_forward_w_in")
        _, (landed,) = _wait_copies(_second_leg_copies, second, token, name="wait_forward_w_in")
        rest["first"], token = _start_copies(_first_leg_copies, rest_b,
                                             [_sds((N_DEV,) + t.shape, BF16) for t in rest_b], 4 * len(rest_b),
                                             name="start_gather_rest", after=landed)
        return whole(landed, mine), token

    def relay_rest(after):
        rest["mine"], landed = _wait_copies(_first_leg_copies, rest["first"], after, name="wait_gather_rest")
        rest["second"], token = _start_copies(_second_leg_copies, [], landed, 3 * len(rest_b),
                                              name="start_forward_rest")
        return token

    def get_rest(after, stage):
        n_src, send_sems, recv_sems, bufs = rest["second"]
        part = slice(0, 2) if stage == 0 else slice(2, len(rest_b))
        _, landed = _wait_copies(functools.partial(_second_leg_copies, first=part.start),
                                 (n_src, send_sems, recv_sems, bufs[part]), after,
                                 name=f"wait_forward_rest{stage}")
        return [whole(t, own) for t, own in zip(landed, rest["mine"][part])]

    sent = []

    def send_grads(indices, grads):
        chunked = [_to_chunks(i, g) for i, g in zip(indices, grads)]
        handle, token = _start_copies(_exchange_copies, chunked, [_sds(t.shape, BF16) for t in chunked],
                                      7 * len(chunked),
                                      name="start_exchange_" + ("w_in" if indices == (_W_IN,) else "rest"))
        sent.append((indices, handle))
        return token

    g_mix_0 = g_mix + token_in[0:1, 0:1]
    loss, dx, dgs, drpb = _local_step(
        x[0], p[0, 0].astype(BF16), positions[0], loss_target[0],
        g_mix_0, g_mlp, g_ple, g_final.reshape(1, -1), rpb[0], get_w_in, relay_rest, get_rest, send_grads)

    drpb3 = drpb.reshape(8, 16, 32)[:, :15, :31]
    small = _pack_small(dgs[0], dgs[1], dgs[2], dgs[3], drpb3, loss)
    share, done = _start_copies(_gather_copies, [small], [_sds((N_DEV,) + small.shape, F32)], 7,
                                name="start_share_small")

    out = {}
    for indices, handle in sent:
        chunked, landed = _wait_copies(_exchange_copies, handle, done,
                                       name="wait_exchange_" + ("w_in" if indices == (_W_IN,) else "rest"))
        for i, part, mine in zip(indices, landed, chunked):
            name = _WEIGHTS[i][0]
            w, m, v = shards[name]
            turned = _WEIGHTS[i][1] == 1 and i != _W_IN
            res = _sum_adamw(part, w, m, v, tr=368 if i == _W_IN else 128, name="adamw_" + name,
                             own=(mine, me.reshape(1).astype(jnp.int32)), transposed=turned)
            out[name] = [(t.T if i == _W_IN else t)[None] for t in res]
            done = res[0]
    (small,), (small_landed,) = _wait_copies(_gather_copies, share, done, name="wait_share_small")
    small_all = lax.dynamic_update_index_in_dim(small_landed, small, me, 0)
    small_w = _pack_small(g_mix, g_mlp, g_ple, g_final, rpb, jnp.zeros((128,), F32))
    small_m = _pack_small(m_g_mix, m_g_mlp, m_g_ple, m_g_final, m_rpb, jnp.zeros((128,), F32))
    small_v = _pack_small(v_g_mix, v_g_mlp, v_g_ple, v_g_final, v_rpb, jnp.zeros((128,), F32))
    res = _sum_adamw(small_all, small_w, small_m, small_v, tr=64, name="adamw_small")
    unpacked = [_unpack_small(t) for t in res]
    for i, name in enumerate(("g_mix", "rpb", "g_mlp", "g_ple", "g_final")):
        out[name] = [u[i] for u in unpacked]
    loss_total = res[0][62, 0]

    order = ("g_mix", "w_in", "rpb", "w_branch_na", "w_branch_dil", "w_out", "g_mlp", "w_up", "w_down",
             "g_ple", "w_ple_gate", "w_ple_proj", "g_final")
    grads = [out[k][0] for k in order]
    deltas = [out[k][1] for k in order]
    new_m = [out[k][2] for k in order]
    new_v = [out[k][3] for k in order]
    return (loss_total, dx[None], *grads, *deltas, *new_m, *new_v)
```

```python
import functools

import jax
import jax.numpy as jnp
from jax import lax
from jax.experimental import pallas as pl
from jax.experimental.pallas import tpu as pltpu

F32 = jnp.float32
BF16 = jnp.bfloat16

D_MODEL = 1024
HEAD_DIM = 64
GRID_W = 64
NA_WIDTH = 512
DIL_WIDTH = 768
IN_WIDTH = 5888
DIL_DILATIONS = (1, 4, 16)
DIL_RADIUS = 64
NA_WIN_ROWS = 8
RMS_EPS = 1e-6
NEG_INF = -1e30
QK_SCALE = HEAD_DIM ** -0.5

ADAM_LR = 0.001
ADAM_B1 = 0.9
ADAM_B2 = 0.999
ADAM_EPS = 1e-08
ADAM_WD = 0.01
ADAM_STEP = 10

N_DEV = 8
VMEM_LIMIT = 56 * 1024 * 1024
EPILOGUE_ROWS = 256
MESH = pl.DeviceIdType.MESH

NT_DIMS = (((1,), (1,)), ((), ()))
TN_DIMS = (((0,), (0,)), ((), ()))


def _sds(shape, dtype):
    return jax.ShapeDtypeStruct(shape, dtype)


def _params(*sem):
    return pltpu.CompilerParams(dimension_semantics=sem, vmem_limit_bytes=VMEM_LIMIT)


def _rows(tm, width, col=0):
    return pl.BlockSpec((tm, width), lambda i, c=col: (i, c))


def _const(shape):
    zeros = (0,) * len(shape)
    return pl.BlockSpec(shape, lambda i: zeros)


def _my_index():
    return 4 * lax.axis_index("x") + 2 * lax.axis_index("y") + lax.axis_index("c")


def _peer(k):
    x, y, c = lax.axis_index("x"), lax.axis_index("y"), lax.axis_index("c")
    px = 1 - x if k & 4 else x
    py = 1 - y if k & 2 else y
    pc = 1 - c if k & 1 else c
    return (px, py, pc), 4 * px + 2 * py + pc


def _call(body, *, name, grid, in_specs, out_specs, out_shape, scratch_shapes, args, after=None):
    n_in, n_out = len(in_specs), len(out_specs)
    extra = [] if after is None else [after]
    n_x = n_in + len(extra)

    def plain(*refs):
        body(refs[:n_in], refs[n_x:n_x + n_out], refs[n_x + n_out:])

    res = pl.pallas_call(plain, name=name, grid=grid,
                         in_specs=list(in_specs) + [pl.BlockSpec(memory_space=pl.ANY)] * len(extra),
                         out_specs=out_specs, out_shape=out_shape, scratch_shapes=scratch_shapes,
                         compiler_params=_params(*(("arbitrary",) * len(grid))))(*args, *extra)
    return list(res)


_HBM_SPEC = pl.BlockSpec(memory_space=pltpu.HBM)
_SEM_SPEC = pl.BlockSpec(memory_space=pltpu.SEMAPHORE)
_SIDE_EFFECT = pltpu.SideEffectType.DATAFLOW_SIDE_EFFECTING


_FIRST_LEG = (1, 2, 4, 6)
_SECOND_LEG = (2, 4, 6)


def _gather_copies(srcs, lands, send, recv, sending):
    me = _my_index()
    out = []
    for w in range(len(srcs)):
        for k in range(1, N_DEV):
            dev, idx = _peer(k)
            out.append(pltpu.make_async_remote_copy(
                src_ref=srcs[w], dst_ref=lands[w].at[me if sending else idx],
                send_sem=send.at[w * 7 + k - 1], recv_sem=recv.at[w * 7 + k - 1],
                device_id=dev, device_id_type=MESH))
    return out


def _first_leg_copies(srcs, lands, send, recv, sending):
    me = _my_index()
    out = []
    for w in range(len(srcs)):
        for j, k in enumerate(_FIRST_LEG):
            dev, idx = _peer(k)
            out.append(pltpu.make_async_remote_copy(
                src_ref=srcs[w], dst_ref=lands[w].at[me if sending else idx],
                send_sem=send.at[w * 4 + j], recv_sem=recv.at[w * 4 + j],
                device_id=dev, device_id_type=MESH))
    return out


def _second_leg_copies(srcs, lands, send, recv, sending, first=0):
    sibling, _ = _peer(1)
    out = []
    for w in range(len(lands)):
        for j, k in enumerate(_SECOND_LEG):
            slot = _peer(k if sending else k ^ 1)[1]
            sem = (first + w) * 3 + j
            out.append(pltpu.make_async_remote_copy(
                src_ref=lands[w].at[slot], dst_ref=lands[w].at[slot],
                send_sem=send.at[sem], recv_sem=recv.at[sem],
                device_id=sibling, device_id_type=MESH))
    return out


def _exchange_copies(srcs, lands, send, recv, sending):
    out = []
    for w in range(len(srcs)):
        for k in range(1, N_DEV):
            dev, idx = _peer(k)
            out.append(pltpu.make_async_remote_copy(
                src_ref=srcs[w].at[idx], dst_ref=lands[w].at[k],
                send_sem=send.at[w * 7 + k - 1], recv_sem=recv.at[w * 7 + k - 1],
                device_id=dev, device_id_type=MESH))
    return out


def _start_copies(make, srcs, lands, n_copies, *, name, after=None):
    n_src, n_buf = len(srcs), len(srcs) + len(lands)
    extra = [] if after is None else [after]

    def body(*refs):
        send, recv = refs[n_buf + len(extra)], refs[n_buf + len(extra) + 1]
        for cp in make(refs[:n_src], refs[n_src:n_buf], send, recv, True):
            cp.start()
        refs[-1][...] = jnp.zeros_like(refs[-1])

    bufs = list(srcs) + [lax.empty(t.shape, t.dtype) if isinstance(t, jax.ShapeDtypeStruct) else t for t in lands]
    res = pl.pallas_call(
        body, name=name,
        out_shape=(pltpu.SemaphoreType.DMA((n_copies,)), pltpu.SemaphoreType.DMA((n_copies,)),
                   *[pltpu.HBM(t.shape, t.dtype) for t in bufs], _sds((8, 128), F32)),
        in_specs=[_HBM_SPEC] * n_buf + [pl.BlockSpec(memory_space=pl.ANY)] * len(extra),
        out_specs=(_SEM_SPEC, _SEM_SPEC, *([_HBM_SPEC] * n_buf), pl.BlockSpec(memory_space=pltpu.VMEM)),
        input_output_aliases={i: 2 + i for i in range(n_buf)},
        compiler_params=pltpu.CompilerParams(has_side_effects=_SIDE_EFFECT),
    )(*[pltpu.with_memory_space_constraint(t, pltpu.HBM) for t in bufs], *extra)
    return (n_src, res[0], res[1], res[2:2 + n_buf]), res[-1]


def _wait_copies(make, handle, after, *, name):
    n_src, send_sems, recv_sems, bufs = handle
    n_buf = len(bufs)
    after = list(after) if isinstance(after, (tuple, list)) else [after]

    def body(*refs):
        for cp in make(refs[:n_src], refs[n_src:n_buf], refs[n_buf], refs[n_buf + 1], False):
            cp.wait_send()
            cp.wait_recv()

    res = pl.pallas_call(
        body, name=name,
        out_shape=tuple(pltpu.HBM(t.shape, t.dtype) for t in bufs),
        in_specs=[_HBM_SPEC] * n_buf + [_SEM_SPEC, _SEM_SPEC] + [pl.BlockSpec(memory_space=pl.ANY)] * len(after),
        out_specs=tuple([_HBM_SPEC] * n_buf),
        input_output_aliases={i: i for i in range(n_buf)},
        compiler_params=pltpu.CompilerParams(has_side_effects=_SIDE_EFFECT),
    )(*bufs, send_sems, recv_sems, *after)
    return list(res[:n_src]), list(res[n_src:])


def _add_colsums(s_refs, sums, step):
    for s_ref, val in zip(s_refs, sums):
        @pl.when(step == 0)
        def _(s_ref=s_ref, val=val):
            s_ref[...] = val

        @pl.when(step > 0)
        def _(s_ref=s_ref, val=val):
            s_ref[...] += val


def _matmul(a, b, *, ta=False, tb=False, out_dtype, tm, tn, tk, name, after=None, extra=(), epilogue=None,
            n_colsum=0, transpose_out=False):
    m, k = (a.shape[1], a.shape[0]) if ta else a.shape
    n = b.shape[0] if tb else b.shape[1]
    tm, tn, tk = min(tm, m), min(tn, n), min(tk, k)
    nk = k // tk
    dims = (((0 if ta else 1,), (1 if tb else 0,)), ((), ()))
    out_dtypes = out_dtype if isinstance(out_dtype, tuple) else (out_dtype,)
    n_tiles = len(out_dtypes)

    def add_colsums(o_refs, sums):
        _add_colsums(o_refs[n_tiles:], sums, pl.program_id(1))

    def finish(acc, x_refs, o_refs):
        vals = (acc,) if epilogue is None else epilogue(acc, *[r[...] for r in x_refs])
        for o_ref, val in zip(o_refs[:n_tiles], vals[:n_tiles]):
            o_ref[...] = (val.T if transpose_out else val).astype(o_ref.dtype)
        add_colsums(o_refs, vals[n_tiles:])

    chunk = EPILOGUE_ROWS if (nk == 1 and epilogue is not None and not ta and tm % EPILOGUE_ROWS == 0) else None

    def body(ins, outs, acc):
        a_ref, b_ref = ins[:2]
        if chunk is not None:
            sums = None
            for r0 in range(0, tm, chunk):
                part = lax.dot_general(a_ref[r0:r0 + chunk, :], b_ref[...], dims, preferred_element_type=F32)
                vals = epilogue(part, *[r[...] if r.shape[0] == 1 else r[r0:r0 + chunk, :] for r in ins[2:]])
                for o_ref, val in zip(outs[:n_tiles], vals[:n_tiles]):
                    o_ref[r0:r0 + chunk, :] = val.astype(o_ref.dtype)
                sums = vals[n_tiles:] if sums is None else [s + v for s, v in zip(sums, vals[n_tiles:])]
            add_colsums(outs, sums)
            return
        part = lax.dot_general(a_ref[...], b_ref[...], dims, preferred_element_type=F32)
        if nk == 1:
            finish(part, ins[2:], outs)
            return
        acc_ref, = acc
        kk = pl.program_id(2)

        @pl.when(kk == 0)
        def _():
            acc_ref[...] = part

        @pl.when(kk > 0)
        def _():
            acc_ref[...] += part

        @pl.when(kk == nk - 1)
        def _():
            finish(acc_ref[...], ins[2:], outs)

    a_spec = (pl.BlockSpec((tk, tm), lambda j, i, kk: (kk, i)) if ta
              else pl.BlockSpec((tm, tk), lambda j, i, kk: (i, kk)))
    b_spec = (pl.BlockSpec((tn, tk), lambda j, i, kk: (j, kk)) if tb
              else pl.BlockSpec((tk, tn), lambda j, i, kk: (kk, j)))
    tile = pl.BlockSpec((tm, tn), lambda j, i, kk: (i, j))
    row = pl.BlockSpec((1, tn), lambda j, i, kk: (0, j))

    out_tile, out_dims = (pl.BlockSpec((tn, tm), lambda j, i, kk: (j, i)), (n, m)) if transpose_out else (tile, (m, n))
    res = _call(
        body, name=name, grid=(n // tn, m // tm, nk),
        in_specs=[a_spec, b_spec] + [row if t.shape[0] == 1 else tile for t in extra],
        out_specs=[out_tile] * n_tiles + [row] * n_colsum,
        out_shape=[_sds(out_dims, dt) for dt in out_dtypes] + [_sds((1, n), F32)] * n_colsum,
        scratch_shapes=[] if nk == 1 else [pltpu.VMEM((tm, tn), F32)],
        args=(a, b, *extra), after=after)
    return res if isinstance(out_dtype, tuple) or n_colsum else res[0]


def _rstd(h):
    return lax.rsqrt(jnp.mean(h * h, axis=-1, keepdims=True) + RMS_EPS)


def _sigmoid(z):
    return 1.0 / (1.0 + jnp.exp(-z))


def _rms_fwd(x, g, *, tm, name):
    n = x.shape[0]

    def body(x_ref, g_ref, o_ref):
        h = x_ref[...]
        o_ref[...] = (h * _rstd(h) * g_ref[...]).astype(BF16)

    return pl.pallas_call(
        body, name=name, grid=(n // tm,),
        in_specs=[_rows(tm, D_MODEL), _const((1, D_MODEL))],
        out_specs=_rows(tm, D_MODEL), out_shape=_sds((n, D_MODEL), BF16),
        compiler_params=_params("parallel"),
    )(x, g)


def _swap_halves(t):
    lane = lax.broadcasted_iota(jnp.int32, (t.shape[0], 128), 1)
    pieces = [t[:, c:c + 128] for c in range(0, t.shape[1], 128)]
    return jnp.concatenate([jnp.where((lane & 63) < 32, pltpu.roll(h, 96, 1), pltpu.roll(h, 32, 1))
                            for h in pieces], axis=1)


def _dil_spec(dil, tm):
    return pl.BlockSpec((dil, tm // dil, 256), lambda i: (0, i, 0))


def _dil_scratch(tm):
    return pltpu.VMEM((2, tm, 128), F32)


def _load_token_order(src, scr, dil, rows, row0=0):
    if dil == 1:
        return src[0, row0:row0 + rows, :]
    for j in range(dil):
        for c in range(2):
            scr[c, pl.ds(j, rows // dil, stride=dil), :] = (
                src[j, row0 // dil:(row0 + rows) // dil, c * 128:(c + 1) * 128])
    return jnp.concatenate([scr[0, 0:rows, :], scr[1, 0:rows, :]], axis=1)


def _store_dil_order(val, dst, scr, dil, row0=0):
    rows = val.shape[0]
    if dil == 1:
        dst[0, row0:row0 + rows, :] = val.astype(dst.dtype)
        return
    for c in range(2):
        scr[c] = val[:, c * 128:(c + 1) * 128]
    for j in range(dil):
        for c in range(2):
            dst[j, row0 // dil:(row0 + rows) // dil, c * 128:(c + 1) * 128] = (
                scr[c, pl.ds(j, rows // dil, stride=dil), :].astype(dst.dtype))


def _project_in(a, w_t, cos_t, sin_t, *, tm, name, after=None):
    n = a.shape[0]
    n_dil = len(DIL_DILATIONS)
    na_w, dil_w = 3 * NA_WIDTH, 3 * DIL_WIDTH
    chunk = min(EPILOGUE_ROWS, tm)
    extra = [] if after is None else [after]

    def body(a_ref, w_ref, cos_ref, sin_ref, *rest):
        na_ref, gate_ref = rest[len(extra):len(extra) + 2]
        outs, scr = rest[len(extra) + 2:len(extra) + 2 + 3 * n_dil], rest[-1]

        def part(r0, first, width):
            return lax.dot_general(a_ref[r0:r0 + chunk, :], w_ref[first:first + width, :], NT_DIMS,
                                   preferred_element_type=F32)

        for r0 in range(0, tm, chunk):
            na_ref[r0:r0 + chunk, :] = part(r0, 0, na_w).astype(BF16)
            dil_part = part(r0, na_w, dil_w)
            cosv, sinv = cos_ref[r0:r0 + chunk, :], sin_ref[r0:r0 + chunk, :]
            for t in range(3):
                for gi, dil in enumerate(DIL_DILATIONS):
                    c0 = (t * n_dil + gi) * 256
                    val = dil_part[:, c0:c0 + 256]
                    if t < 2:
                        val = val * cosv + _swap_halves(val) * sinv
                    _store_dil_order(val, outs[t * n_dil + gi], scr, dil, r0)
            gate_ref[r0:r0 + chunk, :] = _sigmoid(part(r0, na_w + dil_w, 2 * D_MODEL)).astype(BF16)

    out_specs = [_rows(tm, na_w), _rows(tm, 2 * D_MODEL)]
    out_shape = [_sds((n, na_w), BF16), _sds((n, 2 * D_MODEL), BF16)]
    for _ in range(3):
        for dil in DIL_DILATIONS:
            out_specs.append(pl.BlockSpec((dil, tm // dil, 256), lambda i: (0, i, 0)))
            out_shape.append(_sds((dil, n // dil, 256), BF16))
    res = pl.pallas_call(
        body, name=name, grid=(n // tm,),
        in_specs=[_rows(tm, D_MODEL), _const(w_t.shape), _rows(tm, 256), _rows(tm, 256)]
                 + [pl.BlockSpec(memory_space=pl.ANY)] * len(extra),
        out_specs=out_specs, out_shape=out_shape,
        scratch_shapes=[pltpu.VMEM((2, chunk, 128), F32)],
        compiler_params=_params("parallel"),
    )(a, w_t, cos_t, sin_t, *extra)
    return res[0], res[1], res[2:5], res[5:8], res[8:11]


def _residual_rms_tile(delta, h, g):
    hn = h + delta
    return hn, hn * _rstd(hn) * g


def _gate_mix_tile(b2, s1, b1, s2):
    return b2, s1.astype(F32) * b1.astype(F32) + s2.astype(F32) * b2


def _gate_bwd_tile(dm, s1, b1, s2, b2):
    s1, b1, s2, b2 = (t.astype(F32) for t in (s1, b1, s2, b2))
    return dm * s1, dm * s2, dm * b1 * s1 * (1.0 - s1), dm * b2 * s2 * (1.0 - s2)


def _tail_tile(gt, pp, h2, target, g):
    sg = _sigmoid(gt)
    h3 = h2 + sg * pp
    r3 = _rstd(h3)
    n3 = h3 * r3
    err = n3 * g - target
    loss = 0.5 * jnp.sum(jnp.sum(err * err, axis=-1, keepdims=True) / D_MODEL)
    dy = err / D_MODEL
    dn = dy * g
    dh3 = r3 * (dn - n3 * jnp.mean(dn * n3, axis=-1, keepdims=True))
    return (dh3, dh3 * sg, dh3 * pp * sg * (1.0 - sg),
            jnp.sum(dy * n3, axis=0, keepdims=True), jnp.full((1, gt.shape[1]), loss, F32))


def _rms_bwd_tile(dz, h, g, dres):
    r = _rstd(h)
    nrm = h * r
    dn = dz * g
    dh = dres + r * (dn - nrm * jnp.mean(dn * nrm, axis=-1, keepdims=True))
    return dh, jnp.sum(dz * nrm, axis=0, keepdims=True)


def _rms_bwd_twice(dz, h, g, dres):
    dh, dg = _rms_bwd_tile(dz, h, g, dres)
    return dh, dh, dg


def _tail_step(f, w_down, h1, w_pg, p, w_pp, target, g_final, g_ple, *, tm, name):
    n = f.shape[0]
    chunk = min(EPILOGUE_ROWS, tm)

    def body(f_ref, wd_ref, h1_ref, wg_ref, p_ref, wp_ref, t_ref, gf_ref, gp_ref,
             e_ref, dpp_ref, dgt_ref, dh2_ref, dh2b_ref, dgf_ref, loss_ref, dgp_ref):
        sums = None
        for r0 in range(0, tm, chunk):
            rows = slice(r0, r0 + chunk)
            delta = jnp.dot(f_ref[rows, :], wd_ref[...], preferred_element_type=F32)
            h2, e = _residual_rms_tile(delta, h1_ref[rows, :], gp_ref[...])
            e = e.astype(BF16)
            e_ref[rows, :] = e
            gt = jnp.dot(e, wg_ref[...], preferred_element_type=F32)
            pp = lax.dot_general(p_ref[rows, :], wp_ref[...], NT_DIMS, preferred_element_type=F32)
            dh3, dpp, dgt, dgf, loss = _tail_tile(gt, pp, h2, t_ref[rows, :], gf_ref[...])
            dgt = dgt.astype(BF16)
            dpp_ref[rows, :] = dpp.astype(BF16)
            dgt_ref[rows, :] = dgt
            dz = lax.dot_general(dgt, wg_ref[...], NT_DIMS, preferred_element_type=F32)
            dh2, dgp = _rms_bwd_tile(dz, h2, gp_ref[...], dh3)
            dh2_ref[rows, :] = dh2
            dh2b_ref[rows, :] = dh2.astype(BF16)
            vals = (dgf, loss, dgp)
            sums = vals if sums is None else [s + v for s, v in zip(sums, vals)]
        _add_colsums((dgf_ref, loss_ref, dgp_ref), sums, pl.program_id(0))

    wide, gain = _rows(tm, D_MODEL), _const((1, D_MODEL))
    return pl.pallas_call(
        body, name=name, grid=(n // tm,),
        in_specs=[_rows(tm, f.shape[1]), _const(w_down.shape), wide, _const(w_pg.shape),
                  _rows(tm, p.shape[1]), _const(w_pp.shape), wide, gain, gain],
        out_specs=[wide] * 5 + [gain] * 3,
        out_shape=[_sds((n, D_MODEL), dt) for dt in (BF16, BF16, BF16, F32, BF16)]
                  + [_sds((1, D_MODEL), F32)] * 3,
        compiler_params=_params("arbitrary"),
    )(f, w_down, h1, w_pg, p, w_pp, target, g_final, g_ple)


def _assemble_dproj(dna, ddil_q, ddil_k, ddil_v, dgn, dgd, cos_t, sin_t, *, tm, name):
    n = dgn.shape[0]

    def body(*refs):
        dq_ref, dk_ref, dv_ref = refs[0:3]
        dil_in = refs[3:12]
        dgn_ref, dgd_ref, cos_ref, sin_ref, o_ref, scr = refs[12:18]
        o_ref[:, 0:512] = dq_ref[...]
        o_ref[:, 512:1024] = dk_ref[...].astype(BF16)
        o_ref[:, 1024:1536] = dv_ref[...].astype(BF16)
        cosv, sinv = cos_ref[...], sin_ref[...]
        for t in range(3):
            for gi, dil in enumerate(DIL_DILATIONS):
                val = _load_token_order(dil_in[t * 3 + gi], scr, dil, tm)
                if t < 2:
                    val = val * cosv + _swap_halves(val * sinv)
                c0 = 1536 + t * DIL_WIDTH + gi * 256
                o_ref[:, c0:c0 + 256] = val.astype(BF16)
        o_ref[:, 3840:4864] = dgn_ref[...]
        o_ref[:, 4864:5888] = dgd_ref[...]

    in_specs = [_rows(tm, NA_WIDTH)] * 3
    for _ in range(3):
        for dil in DIL_DILATIONS:
            in_specs.append(pl.BlockSpec((dil, tm // dil, 256), lambda i: (0, i, 0)))
    in_specs += [_rows(tm, D_MODEL)] * 2 + [_rows(tm, 256)] * 2
    return pl.pallas_call(
        body, name=name, grid=(n // tm,), in_specs=in_specs,
        out_specs=_rows(tm, IN_WIDTH), out_shape=_sds((n, IN_WIDTH), BF16),
        scratch_shapes=[_dil_scratch(tm)],
        compiler_params=_params("parallel"),
    )(*dna, *ddil_q, *ddil_k, *ddil_v, dgn, dgd, cos_t, sin_t)


N_ROW_OFF = 2 * NA_WIN_ROWS - 1
N_PAIRS = N_ROW_OFF - 1
RB_WIDTH = (N_ROW_OFF + 1) * GRID_W


def _na_bias(rb_ref, pair_scr):
    shape = (GRID_W, RB_WIDTH)
    qc = lax.broadcasted_iota(jnp.int32, shape, 0)
    qc2 = lax.broadcasted_iota(jnp.int32, (GRID_W, 128), 0)
    kc2 = lax.broadcasted_iota(jnp.int32, (GRID_W, 128), 1) & (GRID_W - 1)
    cs = jnp.clip(qc2 - 8, 0, GRID_W - 16)
    valid = (kc2 >= cs) & (kc2 < cs + 16)
    for hh in range(2):
        t = jnp.broadcast_to(rb_ref[hh], shape)
        t = pltpu.roll(t, RB_WIDTH - 15, 1)
        for b in range(6):
            t = jnp.where(((qc >> b) & 1) == 1, pltpu.roll(t, 1 << b, 1), t)
        t_odd = pltpu.roll(t, RB_WIDTH - GRID_W, 1)
        for ro in range(N_PAIRS):
            src = t if ro % 2 == 0 else t_odd
            base = (ro // 2) * 128
            pair_scr[hh, ro] = jnp.where(valid, src[:, base:base + 128], NEG_INF)


NA_GROUP_FWD = 8
NA_GROUP_BWD = 4


def _stack_heads(ref, r, scale=1.0):
    lane = lax.broadcasted_iota(jnp.int32, (GRID_W, 128), 1)
    t = ref[pl.ds(pl.multiple_of(r * GRID_W, GRID_W), GRID_W), :].astype(F32) * scale
    return jnp.concatenate([jnp.where(lane < 64, t, 0.0), jnp.where(lane >= 64, t, 0.0)], axis=0).astype(BF16)


def _unstack_heads(t2):
    lane = lax.broadcasted_iota(jnp.int32, (GRID_W, 128), 1)
    return jnp.where(lane < 64, t2[:GRID_W], t2[GRID_W:])


def _na_window(k_ref, v_ref, r, n_rows):
    rs = jnp.clip(r - NA_WIN_ROWS // 2, 0, n_rows - NA_WIN_ROWS)
    ro0 = (NA_WIN_ROWS - 1) - (r - rs)
    off = pl.multiple_of(rs * GRID_W, GRID_W)
    kw = k_ref[pl.ds(off, NA_WIN_ROWS * GRID_W), :]
    vw = v_ref[pl.ds(off, NA_WIN_ROWS * GRID_W), :]
    return kw, vw, off, ro0


def _na_probs(s_raw, pair_scr, ro0):
    bias = [jnp.concatenate([pair_scr[hh, ro0 + 2 * j] for j in range(NA_WIN_ROWS // 2)], axis=1)
            for hh in range(2)]
    s = s_raw + jnp.concatenate(bias, axis=0)
    m = jnp.max(s, axis=-1, keepdims=True)
    e = jnp.exp(s - m)
    return e * (1.0 / jnp.sum(e, axis=-1, keepdims=True))


def _na_qkv_specs(n):
    pairs = NA_WIDTH // 128
    return [pl.BlockSpec((n, 128), lambda h, first=t * pairs: (0, first + h)) for t in range(3)]


def _na_fwd(qkv, rb, *, name):
    n = qkv.shape[0]
    n_rows = n // GRID_W

    def body(ins, outs, scr):
        q_ref, k_ref, v_ref, rb_ref = ins
        o_ref, = outs
        pair_scr, = scr
        _na_bias(rb_ref, pair_scr)

        def group(g, carry):
            rows = [g * NA_GROUP_FWD + t for t in range(NA_GROUP_FWD)]
            wins = [_na_window(k_ref, v_ref, r, n_rows) for r in rows]
            raw = [lax.dot_general(_stack_heads(q_ref, r, QK_SCALE), w[0], NT_DIMS, preferred_element_type=F32)
                   for r, w in zip(rows, wins)]
            probs = [_na_probs(s, pair_scr, w[3]) for s, w in zip(raw, wins)]
            outs2 = [jnp.dot(p.astype(BF16), w[1], preferred_element_type=F32) for p, w in zip(probs, wins)]
            for r, o2 in zip(rows, outs2):
                o_ref[pl.ds(pl.multiple_of(r * GRID_W, GRID_W), GRID_W), :] = _unstack_heads(o2).astype(BF16)
            return carry

        lax.fori_loop(0, n_rows // NA_GROUP_FWD, group, 0)

    col = pl.BlockSpec((n, 128), lambda h: (0, h))
    return _call(
        body, name=name, grid=(NA_WIDTH // 128,),
        in_specs=_na_qkv_specs(n) + [pl.BlockSpec((2, 1, RB_WIDTH), lambda h: (h, 0, 0))],
        out_specs=[col], out_shape=[_sds((n, NA_WIDTH), BF16)],
        scratch_shapes=[pltpu.VMEM((2, N_PAIRS, GRID_W, 128), F32)],
        args=(qkv, qkv, qkv, rb))[0]


def _na_bwd(qkv, do, rb, *, name, after=None):
    n = qkv.shape[0]
    n_rows = n // GRID_W
    win = NA_WIN_ROWS * GRID_W

    def body(ins, outs, scr):
        q_ref, k_ref, v_ref, do_ref, rb_ref = ins
        dq_ref, dk_ref, dv_ref, drb_ref = outs
        pair_scr, acc_scr = scr
        _na_bias(rb_ref, pair_scr)
        acc_scr[...] = jnp.zeros_like(acc_scr)
        dk_ref[...] = jnp.zeros_like(dk_ref)
        dv_ref[...] = jnp.zeros_like(dv_ref)

        def group(g, carry):
            rows = [g * NA_GROUP_BWD + t for t in range(NA_GROUP_BWD)]
            wins = [_na_window(k_ref, v_ref, r, n_rows) for r in rows]
            qss = [_stack_heads(q_ref, r, QK_SCALE) for r in rows]
            doss = [_stack_heads(do_ref, r) for r in rows]
            raw = [lax.dot_general(qs, w[0], NT_DIMS, preferred_element_type=F32) for qs, w in zip(qss, wins)]
            dps = [lax.dot_general(dos, w[1], NT_DIMS, preferred_element_type=F32) for dos, w in zip(doss, wins)]
            probs = [_na_probs(s, pair_scr, w[3]) for s, w in zip(raw, wins)]
            dss = [p * (dp - jnp.sum(p * dp, axis=-1, keepdims=True)) for p, dp in zip(probs, dps)]
            dsbs = [ds.astype(BF16) for ds in dss]
            dq2s = [jnp.dot(dsb, w[0], preferred_element_type=F32) for dsb, w in zip(dsbs, wins)]
            dkws = [lax.dot_general(dsb, qs, TN_DIMS, preferred_element_type=F32) for dsb, qs in zip(dsbs, qss)]
            dvws = [lax.dot_general(p.astype(BF16), dos, TN_DIMS, preferred_element_type=F32)
                    for p, dos in zip(probs, doss)]
            for t, r in enumerate(rows):
                _, _, off, ro0 = wins[t]
                for hh in range(2):
                    for j in range(NA_WIN_ROWS // 2):
                        acc_scr[hh, ro0 + 2 * j] += dss[t][hh * GRID_W:(hh + 1) * GRID_W, j * 128:(j + 1) * 128]
                dq_ref[pl.ds(pl.multiple_of(r * GRID_W, GRID_W), GRID_W), :] = (
                    _unstack_heads(dq2s[t]) * QK_SCALE).astype(BF16)
                dk_ref[pl.ds(off, win), :] += dkws[t]
                dv_ref[pl.ds(off, win), :] += dvws[t]
            return carry

        lax.fori_loop(0, n_rows // NA_GROUP_BWD, group, 0)

        qc = lax.broadcasted_iota(jnp.int32, (N_PAIRS * GRID_W, 128), 0)
        for hh in range(2):
            t = acc_scr[hh].reshape(N_PAIRS * GRID_W, 128)
            for b in range(6):
                t = jnp.where(((qc >> b) & 1) == 1, pltpu.roll(t, 128 - (1 << b), 1), t)
            t = pltpu.roll(t, 15, 1)
            drb_ref[hh] = jnp.sum(t.reshape(N_PAIRS, GRID_W, 128), axis=1)

    col = pl.BlockSpec((n, 128), lambda h: (0, h))
    return _call(
        body, name=name, grid=(NA_WIDTH // 128,),
        in_specs=_na_qkv_specs(n) + [col, pl.BlockSpec((2, 1, RB_WIDTH), lambda h: (h, 0, 0))],
        out_specs=[col, col, col, pl.BlockSpec((2, N_PAIRS, 128), lambda h: (h, 0, 0))],
        out_shape=[_sds((n, NA_WIDTH), BF16), _sds((n, NA_WIDTH), F32), _sds((n, NA_WIDTH), F32),
                   _sds((8, N_PAIRS, 128), F32)],
        scratch_shapes=[pltpu.VMEM((2, N_PAIRS, GRID_W, 128), F32),
                        pltpu.VMEM((2, N_PAIRS, GRID_W, 128), F32)],
        args=(qkv, qkv, qkv, do, rb), after=after)


def _rpb_table(rpb2):
    t = jnp.pad(rpb2, ((0, 0), (0, 1), (0, GRID_W - rpb2.shape[-1])))
    return t.reshape(8, 1, RB_WIDTH)


def _rpb_grad(drb, *, name):
    kdim = drb.shape[1]

    def body(x_ref, o_ref):
        kk = lax.broadcasted_iota(jnp.int32, (128, 512), 0)
        jj = lax.broadcasted_iota(jnp.int32, (128, 512), 1)
        half, co = kk >> 6, kk & 63
        acc = jnp.zeros((8, 512), F32)
        for ro in range(N_PAIRS):
            hit = ((ro + half) == (jj >> 5)) & (co == (jj & 31)) & (co < 31)
            onehot = jnp.where(hit, 1.0, 0.0).astype(F32)
            acc = acc + jnp.dot(x_ref[:, ro * 128:(ro + 1) * 128], onehot, preferred_element_type=F32,
                                precision=lax.Precision.HIGHEST)
        o_ref[...] = acc

    return pl.pallas_call(
        body, name=name, grid=(1,),
        in_specs=[_const((8, kdim))], out_specs=_const((8, 512)), out_shape=_sds((8, 512), F32),
        compiler_params=_params("arbitrary"),
    )(drb)


DIL_GROUP = 2


def _dil_blocks(length):
    qb = min(128, length)
    return qb, min(qb + 2 * DIL_RADIUS, length), min(DIL_GROUP, length // qb)


def _stack_lanes(ref, t, qb, scale=1.0):
    lane = lax.broadcasted_iota(jnp.int32, (qb, 256), 1)
    val = ref[0, t * qb:(t + 1) * qb, :].astype(F32) * scale
    return jnp.concatenate([jnp.where((lane >> 6) == h, val, 0.0) for h in range(4)], axis=0).astype(BF16)


def _dil_window(k_ref, v_ref, blk, qb, win, length):
    start = pl.multiple_of(jnp.clip(blk * qb - DIL_RADIUS, 0, length - win), DIL_RADIUS)
    return k_ref[0, pl.ds(start, win), :], v_ref[0, pl.ds(start, win), :], start


def _dil_caps_init(caps_scr, qb, win):
    @pl.when((pl.program_id(0) == 0) & (pl.program_id(1) == 0))
    def _():
        gap = ((lax.broadcasted_iota(jnp.int32, (4 * qb, win), 0) & (qb - 1))
               - lax.broadcasted_iota(jnp.int32, (4 * qb, win), 1))
        for v in range(3):
            caps_scr[v] = jnp.where(jnp.abs(gap + v * DIL_RADIUS) <= DIL_RADIUS, jnp.inf, NEG_INF)


def _dil_mask(s, blk, start, qb, caps_scr):
    return jnp.minimum(s, caps_scr[(blk * qb - start) // DIL_RADIUS])


def _pick_heads(stacked, qb):
    lane = lax.broadcasted_iota(jnp.int32, (qb, 256), 1)
    out = jnp.zeros((qb, 256), stacked.dtype)
    for h in range(4):
        out = jnp.where((lane >> 6) == h, stacked[h * qb:(h + 1) * qb], out)
    return out


def _stack_head_cols(ref, t, qb):
    return jnp.concatenate([ref[0, t * qb:(t + 1) * qb, 64 * h:64 * h + 1] for h in range(4)], axis=0)


def _dil_fwd(q, k, v, *, name, after=None):
    dil, length, _ = q.shape
    qb, win, grp = _dil_blocks(length)
    extra = [] if after is None else [after]

    def body(q_ref, k_ref, v_ref, *rest):
        o_ref, lse_ref, caps_scr = rest[-3:]
        _dil_caps_init(caps_scr, qb, win)
        blks = [pl.program_id(1) * grp + t for t in range(grp)]
        wins = [_dil_window(k_ref, v_ref, b, qb, win, length) for b in blks]
        raw = [lax.dot_general(_stack_lanes(q_ref, t, qb, QK_SCALE), w[0], NT_DIMS, preferred_element_type=F32)
               for t, w in enumerate(wins)]
        lses, outs = [], []
        for t, (s, w) in enumerate(zip(raw, wins)):
            s = _dil_mask(s, blks[t], w[2], qb, caps_scr)
            m = jnp.max(s, axis=-1, keepdims=True)
            e = jnp.exp(s - m)
            norm = jnp.sum(e, axis=-1, keepdims=True)
            lses.append(m + jnp.log(norm))
            outs.append(jnp.dot((e * (1.0 / norm)).astype(BF16), w[1], preferred_element_type=F32))
        for t in range(grp):
            o_ref[0, t * qb:(t + 1) * qb, :] = _pick_heads(outs[t], qb)
            lse_ref[0, t * qb:(t + 1) * qb, :] = _pick_heads(jnp.broadcast_to(lses[t], (4 * qb, 256)), qb)

    seq = pl.BlockSpec((1, length, 256), lambda j, i: (j, 0, 0))
    blk = pl.BlockSpec((1, grp * qb, 256), lambda j, i: (j, i, 0))
    return pl.pallas_call(
        body, name=name, grid=(dil, length // (grp * qb)),
        in_specs=[blk, seq, seq] + [pl.BlockSpec(memory_space=pl.ANY)] * len(extra), out_specs=[blk, blk],
        out_shape=[_sds((dil, length, 256), F32)] * 2,
        scratch_shapes=[pltpu.VMEM((3, 4 * qb, win), F32)],
        compiler_params=_params("arbitrary", "arbitrary"),
    )(q, k, v, *extra)


def _dil_bwd(q, k, v, do, lse, cc, *, name):
    dil, length, _ = q.shape
    qb, win, grp = _dil_blocks(length)

    def body(q_ref, k_ref, v_ref, do_ref, lse_ref, cc_ref, dq_ref, dk_ref, dv_ref, caps_scr):
        _dil_caps_init(caps_scr, qb, win)

        @pl.when(pl.program_id(1) == 0)
        def _():
            dk_ref[...] = jnp.zeros_like(dk_ref)
            dv_ref[...] = jnp.zeros_like(dv_ref)

        blks = [pl.program_id(1) * grp + t for t in range(grp)]
        wins = [_dil_window(k_ref, v_ref, b, qb, win, length) for b in blks]
        qss = [_stack_lanes(q_ref, t, qb, QK_SCALE) for t in range(grp)]
        doss = [_stack_lanes(do_ref, t, qb) for t in range(grp)]
        raw = [lax.dot_general(qs, w[0], NT_DIMS, preferred_element_type=F32) for qs, w in zip(qss, wins)]
        dps = [lax.dot_general(dos, w[1], NT_DIMS, preferred_element_type=F32) for dos, w in zip(doss, wins)]
        probs = [jnp.exp(_dil_mask(s, blks[t], wins[t][2], qb, caps_scr) - _stack_head_cols(lse_ref, t, qb))
                 for t, s in enumerate(raw)]
        dsbs = [(p * (dp + _stack_head_cols(cc_ref, t, qb))).astype(BF16)
                for t, (p, dp) in enumerate(zip(probs, dps))]
        dq4s = [jnp.dot(dsb, w[0], preferred_element_type=F32) for dsb, w in zip(dsbs, wins)]
        dkws = [lax.dot_general(dsb, qs, TN_DIMS, preferred_element_type=F32) for dsb, qs in zip(dsbs, qss)]
        dvws = [lax.dot_general(p.astype(BF16), dos, TN_DIMS, preferred_element_type=F32)
                for p, dos in zip(probs, doss)]
        for t in range(grp):
            dq_ref[0, t * qb:(t + 1) * qb, :] = _pick_heads(dq4s[t], qb) * QK_SCALE
            dk_ref[0, pl.ds(wins[t][2], win), :] += dkws[t]
            dv_ref[0, pl.ds(wins[t][2], win), :] += dvws[t]

    seq = pl.BlockSpec((1, length, 256), lambda j, i: (j, 0, 0))
    blk = pl.BlockSpec((1, grp * qb, 256), lambda j, i: (j, i, 0))
    return pl.pallas_call(
        body, name=name, grid=(dil, length // (grp * qb)),
        in_specs=[blk, seq, seq, blk, blk, blk], out_specs=[blk, seq, seq],
        out_shape=[_sds((dil, length, 256), F32)] * 3,
        scratch_shapes=[pltpu.VMEM((3, 4 * qb, win), F32)],
        compiler_params=_params("arbitrary", "arbitrary"),
    )(q, k, v, do, lse, cc)


def _merge_weights(lses):
    m = jnp.maximum(jnp.maximum(lses[0], lses[1]), lses[2])
    es = [jnp.exp(t - m) for t in lses]
    inv = 1.0 / (es[0] + es[1] + es[2])
    return [e * inv for e in es]


def _branch_mix(y_na, w_bna, outs, lses, w_bd, gates, *, tm, name):
    n = y_na.shape[0]
    chunk = min(EPILOGUE_ROWS, tm)

    def body(yna_ref, wn_ref, *rest):
        o_in, l_in = rest[0:3], rest[3:6]
        wd_ref, sn_ref, sd_ref = rest[6:9]
        y_ref, yb_ref, bn_ref, bd_ref, mix_ref, scr = rest[9:15]
        for r0 in range(0, tm, chunk):
            rows = slice(r0, r0 + chunk)
            lv = [_load_token_order(l_in[g], scr, d, chunk, r0) for g, d in enumerate(DIL_DILATIONS)]
            ws = _merge_weights(lv)
            y = jnp.zeros((chunk, 256), F32)
            for g, d in enumerate(DIL_DILATIONS):
                y = y + ws[g] * _load_token_order(o_in[g], scr, d, chunk, r0)
            yb = y.astype(BF16)
            y_ref[rows, :] = y
            yb_ref[rows, :] = yb
            bn = lax.dot_general(yna_ref[rows, :], wn_ref[...], NT_DIMS, preferred_element_type=F32).astype(BF16)
            bd = lax.dot_general(yb, wd_ref[...], NT_DIMS, preferred_element_type=F32)
            bn_ref[rows, :] = bn
            bd, mixed = _gate_mix_tile(bd, sn_ref[rows, :], bn, sd_ref[rows, :])
            bd_ref[rows, :] = bd.astype(BF16)
            mix_ref[rows, :] = mixed.astype(BF16)

    specs = [_dil_spec(d, tm) for d in DIL_DILATIONS]
    return pl.pallas_call(
        body, name=name, grid=(n // tm,),
        in_specs=[_rows(tm, NA_WIDTH), _const(w_bna.shape)] + specs + specs
                 + [_const(w_bd.shape), _rows(tm, D_MODEL, 0), _rows(tm, D_MODEL, 1)],
        out_specs=[_rows(tm, 256)] * 2 + [_rows(tm, D_MODEL)] * 3,
        out_shape=[_sds((n, 256), F32), _sds((n, 256), BF16)] + [_sds((n, D_MODEL), BF16)] * 3,
        scratch_shapes=[_dil_scratch(chunk)],
        compiler_params=_params("parallel"),
    )(y_na, w_bna, *outs, *lses, w_bd, gates, gates)


def _branch_bwd(dh, w_out, gates, bn, bd, w_bna, w_bd, y, lses, *, tm, name):
    n = dh.shape[0]
    chunk = min(EPILOGUE_ROWS, tm)

    def body(dh_ref, wo_ref, sn_ref, sd_ref, bn_ref, bd_ref, wn_ref, wd_ref, y_ref, *rest):
        l_in = rest[0:3]
        dbn_ref, dbd_ref, dgn_ref, dgd_ref, dyna_ref = rest[3:8]
        do_out, cc_out, scr = rest[8:11], rest[11:14], rest[14]
        rr = lax.broadcasted_iota(jnp.int32, (256, 256), 0) >> 6
        cc = lax.broadcasted_iota(jnp.int32, (256, 256), 1) >> 6
        ones = jnp.where(rr == cc, 1.0, 0.0).astype(F32)
        for r0 in range(0, tm, chunk):
            rows = slice(r0, r0 + chunk)
            dm = lax.dot_general(dh_ref[rows, :], wo_ref[...], NT_DIMS, preferred_element_type=F32)
            dbn, dbd, dgn, dgd = (t.astype(BF16) for t in _gate_bwd_tile(
                dm, sn_ref[rows, :], bn_ref[rows, :], sd_ref[rows, :], bd_ref[rows, :]))
            dbn_ref[rows, :] = dbn
            dbd_ref[rows, :] = dbd
            dgn_ref[rows, :] = dgn
            dgd_ref[rows, :] = dgd
            dyna_ref[rows, :] = jnp.dot(dbn, wn_ref[...], preferred_element_type=F32).astype(BF16)
            dyv = jnp.dot(dbd, wd_ref[...], preferred_element_type=F32)
            lv = [_load_token_order(l_in[g], scr, d, chunk, r0) for g, d in enumerate(DIL_DILATIONS)]
            ws = _merge_weights(lv)
            tsum = jnp.dot(dyv * y_ref[rows, :], ones, preferred_element_type=F32,
                           precision=lax.Precision.HIGHEST)
            for g, d in enumerate(DIL_DILATIONS):
                _store_dil_order(ws[g] * dyv, do_out[g], scr, d, r0)
                _store_dil_order(-ws[g] * tsum, cc_out[g], scr, d, r0)

    specs = [_dil_spec(d, tm) for d in DIL_DILATIONS]
    wide = _rows(tm, D_MODEL)
    res = pl.pallas_call(
        body, name=name, grid=(n // tm,),
        in_specs=[wide, _const(w_out.shape), _rows(tm, D_MODEL, 0), _rows(tm, D_MODEL, 1), wide, wide,
                  _const(w_bna.shape), _const(w_bd.shape), _rows(tm, 256)] + specs,
        out_specs=[wide] * 4 + [_rows(tm, NA_WIDTH)] + specs + specs,
        out_shape=[_sds((n, D_MODEL), BF16)] * 4 + [_sds((n, NA_WIDTH), BF16)]
                  + [_sds((d, n // d, 256), BF16) for d in DIL_DILATIONS]
                  + [_sds((d, n // d, 256), F32) for d in DIL_DILATIONS],
        scratch_shapes=[_dil_scratch(chunk)],
        compiler_params=_params("parallel"),
    )(dh, w_out, gates, gates, bn, bd, w_bna, w_bd, y, *lses)
    return res[0], res[1], res[2], res[3], res[4], res[5:8], res[8:11]


_WEIGHTS = (("w_in", 1, 736), ("w_branch_na", 1, 128), ("w_branch_dil", 1, 128), ("w_out", 0, 128),
            ("w_up", 1, 512), ("w_down", 0, 512), ("w_ple_gate", 0, 128), ("w_ple_proj", 1, 128))
_W_IN, _W_BNA, _W_BD, _W_OUT, _W_UP, _W_DOWN, _W_PG, _W_PP = range(8)


def _to_full(gathered):
    return gathered.reshape(-1, gathered.shape[2])


def _to_chunks(widx, mat):
    return mat.reshape(N_DEV, _WEIGHTS[widx][2], mat.shape[1])


def _local_step(x, p_bf16, positions, target, g_mix, g_mlp, g_ple, g_final, rpb2,
                get_w_in, relay_rest, get_rest, send_grads):
    tm = 512
    half = HEAD_DIM // 2
    inv_freq = 10000.0 ** (-jnp.arange(half, dtype=F32) / half)
    ang = positions.astype(F32)[:, None] * inv_freq
    cos, sin = jnp.cos(ang), jnp.sin(ang)
    cos_t = jnp.tile(jnp.concatenate([cos, cos], axis=-1), (1, 4))
    sin_t = jnp.tile(jnp.concatenate([-sin, sin], axis=-1), (1, 4))
    rb = _rpb_table(rpb2)

    a = _rms_fwd(x, g_mix, tm=tm, name="rms_mix")
    w_in, token = get_w_in((a, cos_t, sin_t, p_bf16))
    na_qkv, gates, dq_g, dk_g, dv_g = _project_in(a, w_in, cos_t, sin_t, tm=512, name="mm_in", after=token)
    y_na = _na_fwd(na_qkv, rb, name="na_fwd")
    token = relay_rest(y_na)
    d_out, d_lse = [], []
    for g in range(3):
        o, lse = _dil_fwd(dq_g[g], dk_g[g], dv_g[g], name=f"dil_fwd{g}", after=token if g == 0 else None)
        d_out.append(o)
        d_lse.append(lse)
    w_bna, w_bd = get_rest(d_out[2], 0)
    y_dil, y_dil_b, bn, bd, mixed = _branch_mix(y_na, w_bna, d_out, d_lse, w_bd, gates, tm=tm, name="branch_mix")
    w_out, w_up, w_down, w_pg, w_pp = get_rest(mixed, 1)
    h1, c = _matmul(mixed, w_out, out_dtype=(F32, BF16), tm=512, tn=1024, tk=1024, name="mm_out",
                    extra=(x, g_mlp), epilogue=_residual_rms_tile)
    u, f = _matmul(c, w_up, tb=True, out_dtype=(BF16, BF16), tm=512, tn=2048, tk=1024, name="mm_up",
                   epilogue=lambda acc: (acc, jnp.square(jnp.maximum(acc, 0.0))))

    e, dpp, dgt, dh2, dh2_b, dg_final, loss, dg_ple = _tail_step(
        f, w_down, h1, w_pg, p_bf16, w_pp, target, g_final, g_ple, tm=256, name="tail_step")
    loss = loss[:, :128]
    gw_pp = _matmul(p_bf16, dpp, ta=True, transpose_out=True, out_dtype=BF16, tm=256, tn=1024, tk=2048,
                    name="mm_gw_pp")
    gw_pg = _matmul(e, dgt, ta=True, out_dtype=BF16, tm=512, tn=1024, tk=2048, name="mm_gw_pg")
    du = _matmul(dh2_b, w_down, tb=True, out_dtype=BF16, tm=512, tn=2048, tk=1024, name="mm_du",
                 extra=(u,), epilogue=lambda acc, uv: (acc * (2.0 * jnp.maximum(uv.astype(F32), 0.0)),))
    gw_down = _matmul(f, dh2_b, ta=True, out_dtype=BF16, tm=1024, tn=1024, tk=2048, name="mm_gw_down")
    gw_up = _matmul(c, du, ta=True, transpose_out=True, out_dtype=BF16, tm=512, tn=2048, tk=2048, name="mm_gw_up")
    dh1, dh1_b, dg_mlp = _matmul(
        du, w_up, out_dtype=(F32, BF16), tm=512, tn=1024, tk=4096, name="mm_dc",
        extra=(h1, g_mlp, dh2), epilogue=_rms_bwd_twice, n_colsum=1)
    dbn, dbd, dgn, dgd, dy_na, do_g, cc_g = _branch_bwd(dh1_b, w_out, gates, bn, bd, w_bna, w_bd, y_dil, d_lse,
                                                        tm=tm, name="branch_bwd")
    gw_out = _matmul(mixed, dh1_b, ta=True, out_dtype=BF16, tm=512, tn=1024, tk=2048, name="mm_gw_out")
    gw_bna = _matmul(y_na, dbn, ta=True, transpose_out=True, out_dtype=BF16, tm=512, tn=1024, tk=2048,
                     name="mm_gw_bna")
    gw_bd = _matmul(y_dil_b, dbd, ta=True, transpose_out=True, out_dtype=BF16, tm=256, tn=1024, tk=2048,
                    name="mm_gw_bd")
    token = send_grads((_W_PP, _W_PG, _W_DOWN, _W_UP, _W_OUT, _W_BNA, _W_BD),
                       (gw_pp, gw_pg, gw_down, gw_up, gw_out, gw_bna, gw_bd))
    dna = _na_bwd(na_qkv, dy_na, rb, name="na_bwd", after=token)
    drpb = _rpb_grad(dna[3].reshape(8, -1), name="rpb_grad")
    ddq, ddk, ddv = [], [], []
    for g in range(3):
        r = _dil_bwd(dq_g[g], dk_g[g], dv_g[g], do_g[g], d_lse[g], cc_g[g], name=f"dil_bwd{g}")
        ddq.append(r[0])
        ddk.append(r[1])
        ddv.append(r[2])
    dproj = _assemble_dproj(dna[0:3], ddq, ddk, ddv, dgn, dgd, cos_t, sin_t, tm=tm, name="assemble_dproj")
    gw_in = _matmul(a, dproj, ta=True, transpose_out=True, out_dtype=BF16, tm=512, tn=2944, tk=2048, name="mm_gw_in")
    token = send_grads((_W_IN,), (gw_in,))
    dx, dg_mix = _matmul(
        dproj, w_in, out_dtype=(F32,), tm=512, tn=1024, tk=5888, name="mm_da", after=token,
        extra=(x, g_mix, dh1), epilogue=_rms_bwd_tile, n_colsum=1)
    return loss, dx, (dg_mix, dg_mlp, dg_ple, dg_final), drpb


def _cast_bf16(t, *, name):
    def body(t_ref, o_ref):
        o_ref[...] = t_ref[...].astype(BF16)

    rows, cols = t.shape
    tr = 256 if rows % 256 == 0 else rows
    blk = pl.BlockSpec((tr, cols), lambda i: (i, 0))
    return pl.pallas_call(body, name=name, grid=(rows // tr,), in_specs=[blk], out_specs=blk,
                          out_shape=_sds(t.shape, BF16), compiler_params=_params("parallel"))(t)


def _adamw(w, g, m, v):
    m = ADAM_B1 * m + (1.0 - ADAM_B1) * g
    v = ADAM_B2 * v + (1.0 - ADAM_B2) * (g * g)
    m_hat = m / (1.0 - ADAM_B1 ** ADAM_STEP)
    v_hat = v / (1.0 - ADAM_B2 ** ADAM_STEP)
    delta = -ADAM_LR * (m_hat / (jnp.sqrt(v_hat) + ADAM_EPS) + ADAM_WD * w)
    return delta, m, v


def _sum_adamw(parts, w, m, v, *, tr, name, own=None, transposed=False):
    rows, cols = w.shape
    n_pre = 0 if own is None else 1

    def body(*refs):
        p_ref, w_ref, m_ref, v_ref = refs[n_pre:n_pre + 4]
        g_ref, d_ref, nm_ref, nv_ref = refs[-4:]
        g = (p_ref[0] if own is None else refs[n_pre + 4][...]).astype(F32)
        for s in range(1, N_DEV):
            g = g + p_ref[s].astype(F32)
        if transposed:
            g = g.T
        g_ref[...] = g
        d_ref[...], nm_ref[...], nv_ref[...] = _adamw(w_ref[...], g, m_ref[...], v_ref[...])

    if transposed:
        blk = pl.BlockSpec((rows, tr), lambda i, *_: (0, i))
        g_rows, steps = rows, cols // tr
    else:
        blk = pl.BlockSpec((tr, cols), lambda i, *_: (i, 0))
        g_rows, steps = cols, rows // tr
    in_specs = [pl.BlockSpec((N_DEV, tr, g_rows), lambda i, *_: (0, i, 0)), blk, blk, blk]
    args = [parts, w, m, v]
    if own is not None:
        in_specs.append(pl.BlockSpec((None, tr, g_rows), lambda i, idx: (idx[0], i, 0)))
        args = [own[1]] + args + [own[0]]
    return pl.pallas_call(
        body, name=name,
        grid_spec=pltpu.PrefetchScalarGridSpec(num_scalar_prefetch=n_pre, grid=(steps,), in_specs=in_specs,
                                               out_specs=[blk] * 4),
        out_shape=[_sds((rows, cols), F32)] * 4,
        compiler_params=_params("parallel"),
    )(*args)


_RPB_SIZE = 8 * 15 * 31


def _pack_small(g_mix, g_mlp, g_ple, g_final, rpb, loss_row):
    flat = jnp.concatenate([g_mix.reshape(-1), g_mlp.reshape(-1), g_ple.reshape(-1), g_final.reshape(-1),
                            rpb.reshape(-1), jnp.zeros((3840 - _RPB_SIZE,), F32), loss_row.reshape(-1),
                            jnp.zeros((128,), F32)])
    return flat.reshape(64, 128)


def _unpack_small(t):
    flat = t.reshape(-1)
    return (flat[0:1024].reshape(1, 1024), flat[4096:4096 + _RPB_SIZE].reshape(1, 8, 15, 31),
            flat[1024:2048].reshape(1, 1024), flat[2048:3072].reshape(1, 1024), flat[3072:4096])


def kernel(x, p, positions, g_mix, w_in, rpb, w_branch_na, w_branch_dil, w_out, g_mlp, w_up, w_down, g_ple, w_ple_gate, w_ple_proj, g_final, loss_target, m_g_mix, m_w_in, m_rpb, m_w_branch_na, m_w_branch_dil, m_w_out, m_g_mlp, m_w_up, m_w_down, m_g_ple, m_w_ple_gate, m_w_ple_proj, m_g_final, v_g_mix, v_w_in, v_rpb, v_w_branch_na, v_w_branch_dil, v_w_out, v_g_mlp, v_w_up, v_w_down, v_g_ple, v_w_ple_gate, v_w_ple_proj, v_g_final):
    sharded = dict(w_in=(w_in, m_w_in, v_w_in), w_branch_na=(w_branch_na, m_w_branch_na, v_w_branch_na),
                   w_branch_dil=(w_branch_dil, m_w_branch_dil, v_w_branch_dil), w_out=(w_out, m_w_out, v_w_out),
                   w_up=(w_up, m_w_up, v_w_up), w_down=(w_down, m_w_down, v_w_down),
                   w_ple_gate=(w_ple_gate, m_w_ple_gate, v_w_ple_gate),
                   w_ple_proj=(w_ple_proj, m_w_ple_proj, v_w_ple_proj))
    shards = {k: tuple(t[0] for t in val) for k, val in sharded.items()}

    me = _my_index()

    shards["w_in"] = tuple(t.T for t in shards["w_in"])

    w_in_b = _cast_bf16(shards["w_in"][0], name="cast_w_in")
    rest_b = [shards[name][0].astype(BF16).T if axis == 1 else shards[name][0].astype(BF16)
              for name, axis, _ in _WEIGHTS[1:]]
    first_in, token_in = _start_copies(_first_leg_copies, [w_in_b], [_sds((N_DEV,) + w_in_b.shape, BF16)], 4,
                                       name="start_gather_w_in")

    def whole(landed, mine):
        return _to_full(lax.dynamic_update_index_in_dim(landed, mine, me, 0))

    rest = {}

    def get_w_in(after):
        (mine,), landed = _wait_copies(_first_leg_copies, first_in, (*after, *rest_b), name="wait_gather_w_in")
        second, token = _start_copies(_second_leg_copies, [], landed, 3, name="start_forward_w_in")
        _, (landed,) = _wait_copies(_second_leg_copies, second, token, name="wait_forward_w_in")
        rest["first"], token = _start_copies(_first_leg_copies, rest_b,
                                             [_sds((N_DEV,) + t.shape, BF16) for t in rest_b], 4 * len(rest_b),
                                             name="start_gather_rest", after=landed)
        return whole(landed, mine), token

    def relay_rest(after):
        rest["mine"], landed = _wait_copies(_first_leg_copies, rest["first"], after, name="wait_gather_rest")
        rest["second"], token = _start_copies(_second_leg_copies, [], landed, 3 * len(rest_b),
                                              name="start_forward_rest")
        return token

    def get_rest(after, stage):
        n_src, send_sems, recv_sems, bufs = rest["second"]
        part = slice(0, 2) if stage == 0 else slice(2, len(rest_b))
        _, landed = _wait_copies(functools.partial(_second_leg_copies, first=part.start),
                                 (n_src, send_sems, recv_sems, bufs[part]), after,
                                 name=f"wait_forward_rest{stage}")
        return [whole(t, own) for t, own in zip(landed, rest["mine"][part])]

    sent = []

    def send_grads(indices, grads):
        chunked = [_to_chunks(i, g) for i, g in zip(indices, grads)]
        handle, token = _start_copies(_exchange_copies, chunked, [_sds(t.shape, BF16) for t in chunked],
                                      7 * len(chunked),
                                      name="start_exchange_" + ("w_in" if indices == (_W_IN,) else "rest"))
        sent.append((indices, handle))
        return token

    g_mix_0 = g_mix + token_in[0:1, 0:1]
    loss, dx, dgs, drpb = _local_step(
        x[0], p[0, 0].astype(BF16), positions[0], loss_target[0],
        g_mix_0, g_mlp, g_ple, g_final.reshape(1, -1), rpb[0], get_w_in, relay_rest, get_rest, send_grads)

    drpb3 = drpb.reshape(8, 16, 32)[:, :15, :31]
    small = _pack_small(dgs[0], dgs[1], dgs[2], dgs[3], drpb3, loss)
    share, done = _start_copies(_gather_copies, [small], [_sds((N_DEV,) + small.shape, F32)], 7,
                                name="start_share_small")

    out = {}
    for indices, handle in sent:
        chunked, landed = _wait_copies(_exchange_copies, handle, done,
                                       name="wait_exchange_" + ("w_in" if indices == (_W_IN,) else "rest"))
        for i, part, mine in zip(indices, landed, chunked):
            name = _WEIGHTS[i][0]
            w, m, v = shards[name]
            turned = _WEIGHTS[i][1] == 1 and i != _W_IN
            res = _sum_adamw(part, w, m, v, tr=368 if i == _W_IN else 128, name="adamw_" + name,
                             own=(mine, me.reshape(1).astype(jnp.int32)), transposed=turned)
            out[name] = [(t.T if i == _W_IN else t)[None] for t in res]
            done = res[0]
    (small,), (small_landed,) = _wait_copies(_gather_copies, share, done, name="wait_share_small")
    small_all = lax.dynamic_update_index_in_dim(small_landed, small, me, 0)
    small_w = _pack_small(g_mix, g_mlp, g_ple, g_final, rpb, jnp.zeros((128,), F32))
    small_m = _pack_small(m_g_mix, m_g_mlp, m_g_ple, m_g_final, m_rpb, jnp.zeros((128,), F32))
    small_v = _pack_small(v_g_mix, v_g_mlp, v_g_ple, v_g_final, v_rpb, jnp.zeros((128,), F32))
    res = _sum_adamw(small_all, small_w, small_m, small_v, tr=64, name="adamw_small")
    unpacked = [_unpack_small(t) for t in res]
    for i, name in enumerate(("g_mix", "rpb", "g_mlp", "g_ple", "g_final")):
        out[name] = [u[i] for u in unpacked]
    loss_total = res[0][62, 0]

    order = ("g_mix", "w_in", "rpb", "w_branch_na", "w_branch_dil", "w_out", "g_mlp", "w_up", "w_down",
             "g_ple", "w_ple_gate", "w_ple_proj", "g_final")
    grads = [out[k][0] for k in order]
    deltas = [out[k][1] for k in order]
    new_m = [out[k][2] for k in order]
    new_v = [out[k][3] for k in order]
    return (loss_total, dx[None], *grads, *deltas, *new_m, *new_v)
```

```python
import functools

import jax
import jax.numpy as jnp
from jax import lax
from jax.experimental import pallas as pl
from jax.experimental.pallas import tpu as pltpu

F32 = jnp.float32
BF16 = jnp.bfloat16

D_MODEL = 1024
HEAD_DIM = 64
GRID_W = 64
NA_WIDTH = 512
DIL_WIDTH = 768
IN_WIDTH = 5888
DIL_DILATIONS = (1, 4, 16)
DIL_RADIUS = 64
NA_WIN_ROWS = 8
RMS_EPS = 1e-6
NEG_INF = -1e30
QK_SCALE = HEAD_DIM ** -0.5

ADAM_LR = 0.001
ADAM_B1 = 0.9
ADAM_B2 = 0.999
ADAM_EPS = 1e-08
ADAM_WD = 0.01
ADAM_STEP = 10

N_DEV = 8
VMEM_LIMIT = 56 * 1024 * 1024
EPILOGUE_ROWS = 256
MESH = pl.DeviceIdType.MESH

NT_DIMS = (((1,), (1,)), ((), ()))
TN_DIMS = (((0,), (0,)), ((), ()))


def _sds(shape, dtype):
    return jax.ShapeDtypeStruct(shape, dtype)


def _params(*sem):
    return pltpu.CompilerParams(dimension_semantics=sem, vmem_limit_bytes=VMEM_LIMIT)


def _rows(tm, width, col=0):
    return pl.BlockSpec((tm, width), lambda i, c=col: (i, c))


def _const(shape):
    zeros = (0,) * len(shape)
    return pl.BlockSpec(shape, lambda i: zeros)


def _my_index():
    return 4 * lax.axis_index("x") + 2 * lax.axis_index("y") + lax.axis_index("c")


def _peer(k):
    x, y, c = lax.axis_index("x"), lax.axis_index("y"), lax.axis_index("c")
    px = 1 - x if k & 4 else x
    py = 1 - y if k & 2 else y
    pc = 1 - c if k & 1 else c
    return (px, py, pc), 4 * px + 2 * py + pc


def _call(body, *, name, grid, in_specs, out_specs, out_shape, scratch_shapes, args, after=None):
    n_in, n_out = len(in_specs), len(out_specs)
    extra = [] if after is None else [after]
    n_x = n_in + len(extra)

    def plain(*refs):
        body(refs[:n_in], refs[n_x:n_x + n_out], refs[n_x + n_out:])

    res = pl.pallas_call(plain, name=name, grid=grid,
                         in_specs=list(in_specs) + [pl.BlockSpec(memory_space=pl.ANY)] * len(extra),
                         out_specs=out_specs, out_shape=out_shape, scratch_shapes=scratch_shapes,
                         compiler_params=_params(*(("arbitrary",) * len(grid))))(*args, *extra)
    return list(res)


_HBM_SPEC = pl.BlockSpec(memory_space=pltpu.HBM)
_SEM_SPEC = pl.BlockSpec(memory_space=pltpu.SEMAPHORE)
_SIDE_EFFECT = pltpu.SideEffectType.DATAFLOW_SIDE_EFFECTING


_FIRST_LEG = (1, 2, 4, 6)
_SECOND_LEG = (2, 4, 6)


def _gather_copies(srcs, lands, send, recv, sending):
    me = _my_index()
    out = []
    for w in range(len(srcs)):
        for k in range(1, N_DEV):
            dev, idx = _peer(k)
            out.append(pltpu.make_async_remote_copy(
                src_ref=srcs[w], dst_ref=lands[w].at[me if sending else idx],
                send_sem=send.at[w * 7 + k - 1], recv_sem=recv.at[w * 7 + k - 1],
                device_id=dev, device_id_type=MESH))
    return out


def _first_leg_copies(srcs, lands, send, recv, sending):
    me = _my_index()
    out = []
    for w in range(len(srcs)):
        for j, k in enumerate(_FIRST_LEG):
            dev, idx = _peer(k)
            out.append(pltpu.make_async_remote_copy(
                src_ref=srcs[w], dst_ref=lands[w].at[me if sending else idx],
                send_sem=send.at[w * 4 + j], recv_sem=recv.at[w * 4 + j],
                device_id=dev, device_id_type=MESH))
    return out


def _second_leg_copies(srcs, lands, send, recv, sending, first=0):
    sibling, _ = _peer(1)
    out = []
    for w in range(len(lands)):
        for j, k in enumerate(_SECOND_LEG):
            slot = _peer(k if sending else k ^ 1)[1]
            sem = (first + w) * 3 + j
            out.append(pltpu.make_async_remote_copy(
                src_ref=lands[w].at[slot], dst_ref=lands[w].at[slot],
                send_sem=send.at[sem], recv_sem=recv.at[sem],
                device_id=sibling, device_id_type=MESH))
    return out


def _exchange_copies(srcs, lands, send, recv, sending):
    out = []
    for w in range(len(srcs)):
        for k in range(1, N_DEV):
            dev, idx = _peer(k)
            out.append(pltpu.make_async_remote_copy(
                src_ref=srcs[w].at[idx], dst_ref=lands[w].at[k],
                send_sem=send.at[w * 7 + k - 1], recv_sem=recv.at[w * 7 + k - 1],
                device_id=dev, device_id_type=MESH))
    return out


def _start_copies(make, srcs, lands, n_copies, *, name, after=None):
    n_src, n_buf = len(srcs), len(srcs) + len(lands)
    extra = [] if after is None else [after]

    def body(*refs):
        send, recv = refs[n_buf + len(extra)], refs[n_buf + len(extra) + 1]
        for cp in make(refs[:n_src], refs[n_src:n_buf], send, recv, True):
            cp.start()
        refs[-1][...] = jnp.zeros_like(refs[-1])

    bufs = list(srcs) + [lax.empty(t.shape, t.dtype) if isinstance(t, jax.ShapeDtypeStruct) else t for t in lands]
    res = pl.pallas_call(
        body, name=name,
        out_shape=(pltpu.SemaphoreType.DMA((n_copies,)), pltpu.SemaphoreType.DMA((n_copies,)),
                   *[pltpu.HBM(t.shape, t.dtype) for t in bufs], _sds((8, 128), F32)),
        in_specs=[_HBM_SPEC] * n_buf + [pl.BlockSpec(memory_space=pl.ANY)] * len(extra),
        out_specs=(_SEM_SPEC, _SEM_SPEC, *([_HBM_SPEC] * n_buf), pl.BlockSpec(memory_space=pltpu.VMEM)),
        input_output_aliases={i: 2 + i for i in range(n_buf)},
        compiler_params=pltpu.CompilerParams(has_side_effects=_SIDE_EFFECT),
    )(*[pltpu.with_memory_space_constraint(t, pltpu.HBM) for t in bufs], *extra)
    return (n_src, res[0], res[1], res[2:2 + n_buf]), res[-1]


def _wait_copies(make, handle, after, *, name):
    n_src, send_sems, recv_sems, bufs = handle
    n_buf = len(bufs)
    after = list(after) if isinstance(after, (tuple, list)) else [after]

    def body(*refs):
        for cp in make(refs[:n_src], refs[n_src:n_buf], refs[n_buf], refs[n_buf + 1], False):
            cp.wait_send()
            cp.wait_recv()

    res = pl.pallas_call(
        body, name=name,
        out_shape=tuple(pltpu.HBM(t.shape, t.dtype) for t in bufs),
        in_specs=[_HBM_SPEC] * n_buf + [_SEM_SPEC, _SEM_SPEC] + [pl.BlockSpec(memory_space=pl.ANY)] * len(after),
        out_specs=tuple([_HBM_SPEC] * n_buf),
        input_output_aliases={i: i for i in range(n_buf)},
        compiler_params=pltpu.CompilerParams(has_side_effects=_SIDE_EFFECT),
    )(*bufs, send_sems, recv_sems, *after)
    return list(res[:n_src]), list(res[n_src:])


def _add_colsums(s_refs, sums, step):
    for s_ref, val in zip(s_refs, sums):
        @pl.when(step == 0)
        def _(s_ref=s_ref, val=val):
            s_ref[...] = val

        @pl.when(step > 0)
        def _(s_ref=s_ref, val=val):
            s_ref[...] += val


def _matmul(a, b, *, ta=False, tb=False, out_dtype, tm, tn, tk, name, after=None, extra=(), epilogue=None,
            n_colsum=0, transpose_out=False):
    m, k = (a.shape[1], a.shape[0]) if ta else a.shape
    n = b.shape[0] if tb else b.shape[1]
    tm, tn, tk = min(tm, m), min(tn, n), min(tk, k)
    nk = k // tk
    dims = (((0 if ta else 1,), (1 if tb else 0,)), ((), ()))
    out_dtypes = out_dtype if isinstance(out_dtype, tuple) else (out_dtype,)
    n_tiles = len(out_dtypes)

    def add_colsums(o_refs, sums):
        _add_colsums(o_refs[n_tiles:], sums, pl.program_id(1))

    def finish(acc, x_refs, o_refs):
        vals = (acc,) if epilogue is None else epilogue(acc, *[r[...] for r in x_refs])
        for o_ref, val in zip(o_refs[:n_tiles], vals[:n_tiles]):
            o_ref[...] = (val.T if transpose_out else val).astype(o_ref.dtype)
        add_colsums(o_refs, vals[n_tiles:])

    chunk = EPILOGUE_ROWS if (nk == 1 and epilogue is not None and not ta and tm % EPILOGUE_ROWS == 0) else None

    def body(ins, outs, acc):
        a_ref, b_ref = ins[:2]
        if chunk is not None:
            sums = None
            for r0 in range(0, tm, chunk):
                part = lax.dot_general(a_ref[r0:r0 + chunk, :], b_ref[...], dims, preferred_element_type=F32)
                vals = epilogue(part, *[r[...] if r.shape[0] == 1 else r[r0:r0 + chunk, :] for r in ins[2:]])
                for o_ref, val in zip(outs[:n_tiles], vals[:n_tiles]):
                    o_ref[r0:r0 + chunk, :] = val.astype(o_ref.dtype)
                sums = vals[n_tiles:] if sums is None else [s + v for s, v in zip(sums, vals[n_tiles:])]
            add_colsums(outs, sums)
            return
        part = lax.dot_general(a_ref[...], b_ref[...], dims, preferred_element_type=F32)
        if nk == 1:
            finish(part, ins[2:], outs)
            return
        acc_ref, = acc
        kk = pl.program_id(2)

        @pl.when(kk == 0)
        def _():
            acc_ref[...] = part

        @pl.when(kk > 0)
        def _():
            acc_ref[...] += part

        @pl.when(kk == nk - 1)
        def _():
            finish(acc_ref[...], ins[2:], outs)

    a_spec = (pl.BlockSpec((tk, tm), lambda j, i, kk: (kk, i)) if ta
              else pl.BlockSpec((tm, tk), lambda j, i, kk: (i, kk)))
    b_spec = (pl.BlockSpec((tn, tk), lambda j, i, kk: (j, kk)) if tb
              else pl.BlockSpec((tk, tn), lambda j, i, kk: (kk, j)))
    tile = pl.BlockSpec((tm, tn), lambda j, i, kk: (i, j))
    row = pl.BlockSpec((1, tn), lambda j, i, kk: (0, j))

    out_tile, out_dims = (pl.BlockSpec((tn, tm), lambda j, i, kk: (j, i)), (n, m)) if transpose_out else (tile, (m, n))
    res = _call(
        body, name=name, grid=(n // tn, m // tm, nk),
        in_specs=[a_spec, b_spec] + [row if t.shape[0] == 1 else tile for t in extra],
        out_specs=[out_tile] * n_tiles + [row] * n_colsum,
        out_shape=[_sds(out_dims, dt) for dt in out_dtypes] + [_sds((1, n), F32)] * n_colsum,
        scratch_shapes=[] if nk == 1 else [pltpu.VMEM((tm, tn), F32)],
        args=(a, b, *extra), after=after)
    return res if isinstance(out_dtype, tuple) or n_colsum else res[0]


def _rstd(h):
    return lax.rsqrt(jnp.mean(h * h, axis=-1, keepdims=True) + RMS_EPS)


def _sigmoid(z):
    return 1.0 / (1.0 + jnp.exp(-z))


def _rms_fwd(x, g, *, tm, name):
    n = x.shape[0]

    def body(x_ref, g_ref, o_ref):
        h = x_ref[...]
        o_ref[...] = (h * _rstd(h) * g_ref[...]).astype(BF16)

    return pl.pallas_call(
        body, name=name, grid=(n // tm,),
        in_specs=[_rows(tm, D_MODEL), _const((1, D_MODEL))],
        out_specs=_rows(tm, D_MODEL), out_shape=_sds((n, D_MODEL), BF16),
        compiler_params=_params("parallel"),
    )(x, g)


def _swap_halves(t):
    lane = lax.broadcasted_iota(jnp.int32, (t.shape[0], 128), 1)
    pieces = [t[:, c:c + 128] for c in range(0, t.shape[1], 128)]
    return jnp.concatenate([jnp.where((lane & 63) < 32, pltpu.roll(h, 96, 1), pltpu.roll(h, 32, 1))
                            for h in pieces], axis=1)


def _dil_spec(dil, tm):
    return pl.BlockSpec((dil, tm // dil, 256), lambda i: (0, i, 0))


def _dil_scratch(tm):
    return pltpu.VMEM((2, tm, 128), F32)


def _load_token_order(src, scr, dil, rows, row0=0):
    if dil == 1:
        return src[0, row0:row0 + rows, :]
    for j in range(dil):
        for c in range(2):
            scr[c, pl.ds(j, rows // dil, stride=dil), :] = (
                src[j, row0 // dil:(row0 + rows) // dil, c * 128:(c + 1) * 128])
    return jnp.concatenate([scr[0, 0:rows, :], scr[1, 0:rows, :]], axis=1)


def _store_dil_order(val, dst, scr, dil, row0=0):
    rows = val.shape[0]
    if dil == 1:
        dst[0, row0:row0 + rows, :] = val.astype(dst.dtype)
        return
    for c in range(2):
        scr[c] = val[:, c * 128:(c + 1) * 128]
    for j in range(dil):
        for c in range(2):
            dst[j, row0 // dil:(row0 + rows) // dil, c * 128:(c + 1) * 128] = (
                scr[c, pl.ds(j, rows // dil, stride=dil), :].astype(dst.dtype))


def _project_in(a, w_t, cos_t, sin_t, *, tm, name, after=None):
    n = a.shape[0]
    n_dil = len(DIL_DILATIONS)
    na_w, dil_w = 3 * NA_WIDTH, 3 * DIL_WIDTH
    chunk = min(EPILOGUE_ROWS, tm)
    extra = [] if after is None else [after]

    def body(a_ref, w_ref, cos_ref, sin_ref, *rest):
        na_ref, gate_ref = rest[len(extra):len(extra) + 2]
        outs, scr = rest[len(extra) + 2:len(extra) + 2 + 3 * n_dil], rest[-1]

        def part(r0, first, width):
            return lax.dot_general(a_ref[r0:r0 + chunk, :], w_ref[first:first + width, :], NT_DIMS,
                                   preferred_element_type=F32)

        for r0 in range(0, tm, chunk):
            na_ref[r0:r0 + chunk, :] = part(r0, 0, na_w).astype(BF16)
            dil_part = part(r0, na_w, dil_w)
            cosv, sinv = cos_ref[r0:r0 + chunk, :], sin_ref[r0:r0 + chunk, :]
            for t in range(3):
                for gi, dil in enumerate(DIL_DILATIONS):
                    c0 = (t * n_dil + gi) * 256
                    val = dil_part[:, c0:c0 + 256]
                    if t < 2:
                        val = val * cosv + _swap_halves(val) * sinv
                    _store_dil_order(val, outs[t * n_dil + gi], scr, dil, r0)
            gate_ref[r0:r0 + chunk, :] = _sigmoid(part(r0, na_w + dil_w, 2 * D_MODEL)).astype(BF16)

    out_specs = [_rows(tm, na_w), _rows(tm, 2 * D_MODEL)]
    out_shape = [_sds((n, na_w), BF16), _sds((n, 2 * D_MODEL), BF16)]
    for _ in range(3):
        for dil in DIL_DILATIONS:
            out_specs.append(pl.BlockSpec((dil, tm // dil, 256), lambda i: (0, i, 0)))
            out_shape.append(_sds((dil, n // dil, 256), BF16))
    res = pl.pallas_call(
        body, name=name, grid=(n // tm,),
        in_specs=[_rows(tm, D_MODEL), _const(w_t.shape), _rows(tm, 256), _rows(tm, 256)]
                 + [pl.BlockSpec(memory_space=pl.ANY)] * len(extra),
        out_specs=out_specs, out_shape=out_shape,
        scratch_shapes=[pltpu.VMEM((2, chunk, 128), F32)],
        compiler_params=_params("parallel"),
    )(a, w_t, cos_t, sin_t, *extra)
    return res[0], res[1], res[2:5], res[5:8], res[8:11]


def _residual_rms_tile(delta, h, g):
    hn = h + delta
    return hn, hn * _rstd(hn) * g


def _gate_mix_tile(b2, s1, b1, s2):
    return b2, s1.astype(F32) * b1.astype(F32) + s2.astype(F32) * b2


def _gate_bwd_tile(dm, s1, b1, s2, b2):
    s1, b1, s2, b2 = (t.astype(F32) for t in (s1, b1, s2, b2))
    return dm * s1, dm * s2, dm * b1 * s1 * (1.0 - s1), dm * b2 * s2 * (1.0 - s2)


def _tail_tile(gt, pp, h2, target, g):
    sg = _sigmoid(gt)
    h3 = h2 + sg * pp
    r3 = _rstd(h3)
    n3 = h3 * r3
    err = n3 * g - target
    loss = 0.5 * jnp.sum(jnp.sum(err * err, axis=-1, keepdims=True) / D_MODEL)
    dy = err / D_MODEL
    dn = dy * g
    dh3 = r3 * (dn - n3 * jnp.mean(dn * n3, axis=-1, keepdims=True))
    return (dh3, dh3 * sg, dh3 * pp * sg * (1.0 - sg),
            jnp.sum(dy * n3, axis=0, keepdims=True), jnp.full((1, gt.shape[1]), loss, F32))


def _rms_bwd_tile(dz, h, g, dres):
    r = _rstd(h)
    nrm = h * r
    dn = dz * g
    dh = dres + r * (dn - nrm * jnp.mean(dn * nrm, axis=-1, keepdims=True))
    return dh, jnp.sum(dz * nrm, axis=0, keepdims=True)


def _rms_bwd_twice(dz, h, g, dres):
    dh, dg = _rms_bwd_tile(dz, h, g, dres)
    return dh, dh, dg


def _tail_step(f, w_down, h1, w_pg, p, w_pp, target, g_final, g_ple, *, tm, name):
    n = f.shape[0]
    chunk = min(EPILOGUE_ROWS, tm)

    def body(f_ref, wd_ref, h1_ref, wg_ref, p_ref, wp_ref, t_ref, gf_ref, gp_ref,
             e_ref, dpp_ref, dgt_ref, dh2_ref, dh2b_ref, dgf_ref, loss_ref, dgp_ref):
        sums = None
        for r0 in range(0, tm, chunk):
            rows = slice(r0, r0 + chunk)
            delta = jnp.dot(f_ref[rows, :], wd_ref[...], preferred_element_type=F32)
            h2, e = _residual_rms_tile(delta, h1_ref[rows, :], gp_ref[...])
            e = e.astype(BF16)
            e_ref[rows, :] = e
            gt = jnp.dot(e, wg_ref[...], preferred_element_type=F32)
            pp = lax.dot_general(p_ref[rows, :], wp_ref[...], NT_DIMS, preferred_element_type=F32)
            dh3, dpp, dgt, dgf, loss = _tail_tile(gt, pp, h2, t_ref[rows, :], gf_ref[...])
            dgt = dgt.astype(BF16)
            dpp_ref[rows, :] = dpp.astype(BF16)
            dgt_ref[rows, :] = dgt
            dz = lax.dot_general(dgt, wg_ref[...], NT_DIMS, preferred_element_type=F32)
            dh2, dgp = _rms_bwd_tile(dz, h2, gp_ref[...], dh3)
            dh2_ref[rows, :] = dh2
            dh2b_ref[rows, :] = dh2.astype(BF16)
            vals = (dgf, loss, dgp)
            sums = vals if sums is None else [s + v for s, v in zip(sums, vals)]
        _add_colsums((dgf_ref, loss_ref, dgp_ref), sums, pl.program_id(0))

    wide, gain = _rows(tm, D_MODEL), _const((1, D_MODEL))
    return pl.pallas_call(
        body, name=name, grid=(n // tm,),
        in_specs=[_rows(tm, f.shape[1]), _const(w_down.shape), wide, _const(w_pg.shape),
                  _rows(tm, p.shape[1]), _const(w_pp.shape), wide, gain, gain],
        out_specs=[wide] * 5 + [gain] * 3,
        out_shape=[_sds((n, D_MODEL), dt) for dt in (BF16, BF16, BF16, F32, BF16)]
                  + [_sds((1, D_MODEL), F32)] * 3,
        compiler_params=_params("arbitrary"),
    )(f, w_down, h1, w_pg, p, w_pp, target, g_final, g_ple)


def _assemble_dproj(dna, ddil_q, ddil_k, ddil_v, dgn, dgd, cos_t, sin_t, *, tm, name):
    n = dgn.shape[0]

    def body(*refs):
        dq_ref, dk_ref, dv_ref = refs[0:3]
        dil_in = refs[3:12]
        dgn_ref, dgd_ref, cos_ref, sin_ref, o_ref, scr = refs[12:18]
        o_ref[:, 0:512] = dq_ref[...]
        o_ref[:, 512:1024] = dk_ref[...].astype(BF16)
        o_ref[:, 1024:1536] = dv_ref[...].astype(BF16)
        cosv, sinv = cos_ref[...], sin_ref[...]
        for t in range(3):
            for gi, dil in enumerate(DIL_DILATIONS):
                val = _load_token_order(dil_in[t * 3 + gi], scr, dil, tm)
                if t < 2:
                    val = val * cosv + _swap_halves(val * sinv)
                c0 = 1536 + t * DIL_WIDTH + gi * 256
                o_ref[:, c0:c0 + 256] = val.astype(BF16)
        o_ref[:, 3840:4864] = dgn_ref[...]
        o_ref[:, 4864:5888] = dgd_ref[...]

    in_specs = [_rows(tm, NA_WIDTH)] * 3
    for _ in range(3):
        for dil in DIL_DILATIONS:
            in_specs.append(pl.BlockSpec((dil, tm // dil, 256), lambda i: (0, i, 0)))
    in_specs += [_rows(tm, D_MODEL)] * 2 + [_rows(tm, 256)] * 2
    return pl.pallas_call(
        body, name=name, grid=(n // tm,), in_specs=in_specs,
        out_specs=_rows(tm, IN_WIDTH), out_shape=_sds((n, IN_WIDTH), BF16),
        scratch_shapes=[_dil_scratch(tm)],
        compiler_params=_params("parallel"),
    )(*dna, *ddil_q, *ddil_k, *ddil_v, dgn, dgd, cos_t, sin_t)


N_ROW_OFF = 2 * NA_WIN_ROWS - 1
N_PAIRS = N_ROW_OFF - 1
RB_WIDTH = (N_ROW_OFF + 1) * GRID_W


def _na_bias(rb_ref, pair_scr):
    shape = (GRID_W, RB_WIDTH)
    qc = lax.broadcasted_iota(jnp.int32, shape, 0)
    qc2 = lax.broadcasted_iota(jnp.int32, (GRID_W, 128), 0)
    kc2 = lax.broadcasted_iota(jnp.int32, (GRID_W, 128), 1) & (GRID_W - 1)
    cs = jnp.clip(qc2 - 8, 0, GRID_W - 16)
    valid = (kc2 >= cs) & (kc2 < cs + 16)
    for hh in range(2):
        t = jnp.broadcast_to(rb_ref[hh], shape)
        t = pltpu.roll(t, RB_WIDTH - 15, 1)
        for b in range(6):
            t = jnp.where(((qc >> b) & 1) == 1, pltpu.roll(t, 1 << b, 1), t)
        t_odd = pltpu.roll(t, RB_WIDTH - GRID_W, 1)
        for ro in range(N_PAIRS):
            src = t if ro % 2 == 0 else t_odd
            base = (ro // 2) * 128
            pair_scr[hh, ro] = jnp.where(valid, src[:, base:base + 128], NEG_INF)


NA_GROUP_FWD = 8
NA_GROUP_BWD = 4


def _stack_heads(ref, r, scale=1.0):
    lane = lax.broadcasted_iota(jnp.int32, (GRID_W, 128), 1)
    t = ref[pl.ds(pl.multiple_of(r * GRID_W, GRID_W), GRID_W), :].astype(F32) * scale
    return jnp.concatenate([jnp.where(lane < 64, t, 0.0), jnp.where(lane >= 64, t, 0.0)], axis=0).astype(BF16)


def _unstack_heads(t2):
    lane = lax.broadcasted_iota(jnp.int32, (GRID_W, 128), 1)
    return jnp.where(lane < 64, t2[:GRID_W], t2[GRID_W:])


def _na_window(k_ref, v_ref, r, n_rows):
    rs = jnp.clip(r - NA_WIN_ROWS // 2, 0, n_rows - NA_WIN_ROWS)
    ro0 = (NA_WIN_ROWS - 1) - (r - rs)
    off = pl.multiple_of(rs * GRID_W, GRID_W)
    kw = k_ref[pl.ds(off, NA_WIN_ROWS * GRID_W), :]
    vw = v_ref[pl.ds(off, NA_WIN_ROWS * GRID_W), :]
    return kw, vw, off, ro0


def _na_probs(s_raw, pair_scr, ro0):
    bias = [jnp.concatenate([pair_scr[hh, ro0 + 2 * j] for j in range(NA_WIN_ROWS // 2)], axis=1)
            for hh in range(2)]
    s = s_raw + jnp.concatenate(bias, axis=0)
    m = jnp.max(s, axis=-1, keepdims=True)
    e = jnp.exp(s - m)
    return e * (1.0 / jnp.sum(e, axis=-1, keepdims=True))


def _na_qkv_specs(n):
    pairs = NA_WIDTH // 128
    return [pl.BlockSpec((n, 128), lambda h, first=t * pairs: (0, first + h)) for t in range(3)]


def _na_fwd(qkv, rb, *, name):
    n = qkv.shape[0]
    n_rows = n // GRID_W

    def body(ins, outs, scr):
        q_ref, k_ref, v_ref, rb_ref = ins
        o_ref, = outs
        pair_scr, = scr
        _na_bias(rb_ref, pair_scr)

        def group(g, carry):
            rows = [g * NA_GROUP_FWD + t for t in range(NA_GROUP_FWD)]
            wins = [_na_window(k_ref, v_ref, r, n_rows) for r in rows]
            raw = [lax.dot_general(_stack_heads(q_ref, r, QK_SCALE), w[0], NT_DIMS, preferred_element_type=F32)
                   for r, w in zip(rows, wins)]
            probs = [_na_probs(s, pair_scr, w[3]) for s, w in zip(raw, wins)]
            outs2 = [jnp.dot(p.astype(BF16), w[1], preferred_element_type=F32) for p, w in zip(probs, wins)]
            for r, o2 in zip(rows, outs2):
                o_ref[pl.ds(pl.multiple_of(r * GRID_W, GRID_W), GRID_W), :] = _unstack_heads(o2).astype(BF16)
            return carry

        lax.fori_loop(0, n_rows // NA_GROUP_FWD, group, 0)

    col = pl.BlockSpec((n, 128), lambda h: (0, h))
    return _call(
        body, name=name, grid=(NA_WIDTH // 128,),
        in_specs=_na_qkv_specs(n) + [pl.BlockSpec((2, 1, RB_WIDTH), lambda h: (h, 0, 0))],
        out_specs=[col], out_shape=[_sds((n, NA_WIDTH), BF16)],
        scratch_shapes=[pltpu.VMEM((2, N_PAIRS, GRID_W, 128), F32)],
        args=(qkv, qkv, qkv, rb))[0]


def _na_bwd(qkv, do, rb, *, name, after=None):
    n = qkv.shape[0]
    n_rows = n // GRID_W
    win = NA_WIN_ROWS * GRID_W

    def body(ins, outs, scr):
        q_ref, k_ref, v_ref, do_ref, rb_ref = ins
        dq_ref, dk_ref, dv_ref, drb_ref = outs
        pair_scr, acc_scr = scr
        _na_bias(rb_ref, pair_scr)
        acc_scr[...] = jnp.zeros_like(acc_scr)
        dk_ref[...] = jnp.zeros_like(dk_ref)
        dv_ref[...] = jnp.zeros_like(dv_ref)

        def group(g, carry):
            rows = [g * NA_GROUP_BWD + t for t in range(NA_GROUP_BWD)]
            wins = [_na_window(k_ref, v_ref, r, n_rows) for r in rows]
            qss = [_stack_heads(q_ref, r, QK_SCALE) for r in rows]
            doss = [_stack_heads(do_ref, r) for r in rows]
            raw = [lax.dot_general(qs, w[0], NT_DIMS, preferred_element_type=F32) for qs, w in zip(qss, wins)]
            dps = [lax.dot_general(dos, w[1], NT_DIMS, preferred_element_type=F32) for dos, w in zip(doss, wins)]
            probs = [_na_probs(s, pair_scr, w[3]) for s, w in zip(raw, wins)]
            dss = [p * (dp - jnp.sum(p * dp, axis=-1, keepdims=True)) for p, dp in zip(probs, dps)]
            dsbs = [ds.astype(BF16) for ds in dss]
            dq2s = [jnp.dot(dsb, w[0], preferred_element_type=F32) for dsb, w in zip(dsbs, wins)]
            dkws = [lax.dot_general(dsb, qs, TN_DIMS, preferred_element_type=F32) for dsb, qs in zip(dsbs, qss)]
            dvws = [lax.dot_general(p.astype(BF16), dos, TN_DIMS, preferred_element_type=F32)
                    for p, dos in zip(probs, doss)]
            for t, r in enumerate(rows):
                _, _, off, ro0 = wins[t]
                for hh in range(2):
                    for j in range(NA_WIN_ROWS // 2):
                        acc_scr[hh, ro0 + 2 * j] += dss[t][hh * GRID_W:(hh + 1) * GRID_W, j * 128:(j + 1) * 128]
                dq_ref[pl.ds(pl.multiple_of(r * GRID_W, GRID_W), GRID_W), :] = (
                    _unstack_heads(dq2s[t]) * QK_SCALE).astype(BF16)
                dk_ref[pl.ds(off, win), :] += dkws[t]
                dv_ref[pl.ds(off, win), :] += dvws[t]
            return carry

        lax.fori_loop(0, n_rows // NA_GROUP_BWD, group, 0)

        qc = lax.broadcasted_iota(jnp.int32, (N_PAIRS * GRID_W, 128), 0)
        for hh in range(2):
            t = acc_scr[hh].reshape(N_PAIRS * GRID_W, 128)
            for b in range(6):
                t = jnp.where(((qc >> b) & 1) == 1, pltpu.roll(t, 128 - (1 << b), 1), t)
            t = pltpu.roll(t, 15, 1)
            drb_ref[hh] = jnp.sum(t.reshape(N_PAIRS, GRID_W, 128), axis=1)

    col = pl.BlockSpec((n, 128), lambda h: (0, h))
    return _call(
        body, name=name, grid=(NA_WIDTH // 128,),
        in_specs=_na_qkv_specs(n) + [col, pl.BlockSpec((2, 1, RB_WIDTH), lambda h: (h, 0, 0))],
        out_specs=[col, col, col, pl.BlockSpec((2, N_PAIRS, 128), lambda h: (h, 0, 0))],
        out_shape=[_sds((n, NA_WIDTH), BF16), _sds((n, NA_WIDTH), F32), _sds((n, NA_WIDTH), F32),
                   _sds((8, N_PAIRS, 128), F32)],
        scratch_shapes=[pltpu.VMEM((2, N_PAIRS, GRID_W, 128), F32),
                        pltpu.VMEM((2, N_PAIRS, GRID_W, 128), F32)],
        args=(qkv, qkv, qkv, do, rb), after=after)


def _rpb_table(rpb2):
    t = jnp.pad(rpb2, ((0, 0), (0, 1), (0, GRID_W - rpb2.shape[-1])))
    return t.reshape(8, 1, RB_WIDTH)


def _rpb_grad(drb, *, name):
    kdim = drb.shape[1]

    def body(x_ref, o_ref):
        kk = lax.broadcasted_iota(jnp.int32, (128, 512), 0)
        jj = lax.broadcasted_iota(jnp.int32, (128, 512), 1)
        half, co = kk >> 6, kk & 63
        acc = jnp.zeros((8, 512), F32)
        for ro in range(N_PAIRS):
            hit = ((ro + half) == (jj >> 5)) & (co == (jj & 31)) & (co < 31)
            onehot = jnp.where(hit, 1.0, 0.0).astype(F32)
            acc = acc + jnp.dot(x_ref[:, ro * 128:(ro + 1) * 128], onehot, preferred_element_type=F32,
                                precision=lax.Precision.HIGHEST)
        o_ref[...] = acc

    return pl.pallas_call(
        body, name=name, grid=(1,),
        in_specs=[_const((8, kdim))], out_specs=_const((8, 512)), out_shape=_sds((8, 512), F32),
        compiler_params=_params("arbitrary"),
    )(drb)


DIL_GROUP = 2


def _dil_blocks(length):
    qb = min(128, length)
    return qb, min(qb + 2 * DIL_RADIUS, length), min(DIL_GROUP, length // qb)


def _stack_lanes(ref, t, qb, scale=1.0):
    lane = lax.broadcasted_iota(jnp.int32, (qb, 256), 1)
    val = ref[0, t * qb:(t + 1) * qb, :].astype(F32) * scale
    return jnp.concatenate([jnp.where((lane >> 6) == h, val, 0.0) for h in range(4)], axis=0).astype(BF16)


def _dil_window(k_ref, v_ref, blk, qb, win, length):
    start = pl.multiple_of(jnp.clip(blk * qb - DIL_RADIUS, 0, length - win), DIL_RADIUS)
    return k_ref[0, pl.ds(start, win), :], v_ref[0, pl.ds(start, win), :], start


def _dil_caps_init(caps_scr, qb, win):
    @pl.when((pl.program_id(0) == 0) & (pl.program_id(1) == 0))
    def _():
        gap = ((lax.broadcasted_iota(jnp.int32, (4 * qb, win), 0) & (qb - 1))
               - lax.broadcasted_iota(jnp.int32, (4 * qb, win), 1))
        for v in range(3):
            caps_scr[v] = jnp.where(jnp.abs(gap + v * DIL_RADIUS) <= DIL_RADIUS, jnp.inf, NEG_INF)


def _dil_mask(s, blk, start, qb, caps_scr):
    return jnp.minimum(s, caps_scr[(blk * qb - start) // DIL_RADIUS])


def _pick_heads(stacked, qb):
    lane = lax.broadcasted_iota(jnp.int32, (qb, 256), 1)
    out = jnp.zeros((qb, 256), stacked.dtype)
    for h in range(4):
        out = jnp.where((lane >> 6) == h, stacked[h * qb:(h + 1) * qb], out)
    return out


def _stack_head_cols(ref, t, qb):
    return jnp.concatenate([ref[0, t * qb:(t + 1) * qb, 64 * h:64 * h + 1] for h in range(4)], axis=0)


def _dil_fwd(q, k, v, *, name, after=None):
    dil, length, _ = q.shape
    qb, win, grp = _dil_blocks(length)
    extra = [] if after is None else [after]

    def body(q_ref, k_ref, v_ref, *rest):
        o_ref, lse_ref, caps_scr = rest[-3:]
        _dil_caps_init(caps_scr, qb, win)
        blks = [pl.program_id(1) * grp + t for t in range(grp)]
        wins = [_dil_window(k_ref, v_ref, b, qb, win, length) for b in blks]
        raw = [lax.dot_general(_stack_lanes(q_ref, t, qb, QK_SCALE), w[0], NT_DIMS, preferred_element_type=F32)
               for t, w in enumerate(wins)]
        lses, outs = [], []
        for t, (s, w) in enumerate(zip(raw, wins)):
            s = _dil_mask(s, blks[t], w[2], qb, caps_scr)
            m = jnp.max(s, axis=-1, keepdims=True)
            e = jnp.exp(s - m)
            norm = jnp.sum(e, axis=-1, keepdims=True)
            lses.append(m + jnp.log(norm))
            outs.append(jnp.dot((e * (1.0 / norm)).astype(BF16), w[1], preferred_element_type=F32))
        for t in range(grp):
            o_ref[0, t * qb:(t + 1) * qb, :] = _pick_heads(outs[t], qb)
            lse_ref[0, t * qb:(t + 1) * qb, :] = _pick_heads(jnp.broadcast_to(lses[t], (4 * qb, 256)), qb)

    seq = pl.BlockSpec((1, length, 256), lambda j, i: (j, 0, 0))
    blk = pl.BlockSpec((1, grp * qb, 256), lambda j, i: (j, i, 0))
    return pl.pallas_call(
        body, name=name, grid=(dil, length // (grp * qb)),
        in_specs=[blk, seq, seq] + [pl.BlockSpec(memory_space=pl.ANY)] * len(extra), out_specs=[blk, blk],
        out_shape=[_sds((dil, length, 256), F32)] * 2,
        scratch_shapes=[pltpu.VMEM((3, 4 * qb, win), F32)],
        compiler_params=_params("arbitrary", "arbitrary"),
    )(q, k, v, *extra)


def _dil_bwd(q, k, v, do, lse, cc, *, name):
    dil, length, _ = q.shape
    qb, win, grp = _dil_blocks(length)

    def body(q_ref, k_ref, v_ref, do_ref, lse_ref, cc_ref, dq_ref, dk_ref, dv_ref, caps_scr):
        _dil_caps_init(caps_scr, qb, win)

        @pl.when(pl.program_id(1) == 0)
        def _():
            dk_ref[...] = jnp.zeros_like(dk_ref)
            dv_ref[...] = jnp.zeros_like(dv_ref)

        blks = [pl.program_id(1) * grp + t for t in range(grp)]
        wins = [_dil_window(k_ref, v_ref, b, qb, win, length) for b in blks]
        qss = [_stack_lanes(q_ref, t, qb, QK_SCALE) for t in range(grp)]
        doss = [_stack_lanes(do_ref, t, qb) for t in range(grp)]
        raw = [lax.dot_general(qs, w[0], NT_DIMS, preferred_element_type=F32) for qs, w in zip(qss, wins)]
        dps = [lax.dot_general(dos, w[1], NT_DIMS, preferred_element_type=F32) for dos, w in zip(doss, wins)]
        probs = [jnp.exp(_dil_mask(s, blks[t], wins[t][2], qb, caps_scr) - _stack_head_cols(lse_ref, t, qb))
                 for t, s in enumerate(raw)]
        dsbs = [(p * (dp + _stack_head_cols(cc_ref, t, qb))).astype(BF16)
                for t, (p, dp) in enumerate(zip(probs, dps))]
        dq4s = [jnp.dot(dsb, w[0], preferred_element_type=F32) for dsb, w in zip(dsbs, wins)]
        dkws = [lax.dot_general(dsb, qs, TN_DIMS, preferred_element_type=F32) for dsb, qs in zip(dsbs, qss)]
        dvws = [lax.dot_general(p.astype(BF16), dos, TN_DIMS, preferred_element_type=F32)
                for p, dos in zip(probs, doss)]
        for t in range(grp):
            dq_ref[0, t * qb:(t + 1) * qb, :] = _pick_heads(dq4s[t], qb) * QK_SCALE
            dk_ref[0, pl.ds(wins[t][2], win), :] += dkws[t]
            dv_ref[0, pl.ds(wins[t][2], win), :] += dvws[t]

    seq = pl.BlockSpec((1, length, 256), lambda j, i: (j, 0, 0))
    blk = pl.BlockSpec((1, grp * qb, 256), lambda j, i: (j, i, 0))
    return pl.pallas_call(
        body, name=name, grid=(dil, length // (grp * qb)),
        in_specs=[blk, seq, seq, blk, blk, blk], out_specs=[blk, seq, seq],
        out_shape=[_sds((dil, length, 256), F32)] * 3,
        scratch_shapes=[pltpu.VMEM((3, 4 * qb, win), F32)],
        compiler_params=_params("arbitrary", "arbitrary"),
    )(q, k, v, do, lse, cc)


def _merge_weights(lses):
    m = jnp.maximum(jnp.maximum(lses[0], lses[1]), lses[2])
    es = [jnp.exp(t - m) for t in lses]
    inv = 1.0 / (es[0] + es[1] + es[2])
    return [e * inv for e in es]


def _branch_mix(y_na, w_bna, outs, lses, w_bd, gates, *, tm, name):
    n = y_na.shape[0]
    chunk = min(EPILOGUE_ROWS, tm)

    def body(yna_ref, wn_ref, *rest):
        o_in, l_in = rest[0:3], rest[3:6]
        wd_ref, sn_ref, sd_ref = rest[6:9]
        y_ref, yb_ref, bn_ref, bd_ref, mix_ref, scr = rest[9:15]
        for r0 in range(0, tm, chunk):
            rows = slice(r0, r0 + chunk)
            lv = [_load_token_order(l_in[g], scr, d, chunk, r0) for g, d in enumerate(DIL_DILATIONS)]
            ws = _merge_weights(lv)
            y = jnp.zeros((chunk, 256), F32)
            for g, d in enumerate(DIL_DILATIONS):
                y = y + ws[g] * _load_token_order(o_in[g], scr, d, chunk, r0)
            yb = y.astype(BF16)
            y_ref[rows, :] = y
            yb_ref[rows, :] = yb
            bn = lax.dot_general(yna_ref[rows, :], wn_ref[...], NT_DIMS, preferred_element_type=F32).astype(BF16)
            bd = lax.dot_general(yb, wd_ref[...], NT_DIMS, preferred_element_type=F32)
            bn_ref[rows, :] = bn
            bd, mixed = _gate_mix_tile(bd, sn_ref[rows, :], bn, sd_ref[rows, :])
            bd_ref[rows, :] = bd.astype(BF16)
            mix_ref[rows, :] = mixed.astype(BF16)

    specs = [_dil_spec(d, tm) for d in DIL_DILATIONS]
    return pl.pallas_call(
        body, name=name, grid=(n // tm,),
        in_specs=[_rows(tm, NA_WIDTH), _const(w_bna.shape)] + specs + specs
                 + [_const(w_bd.shape), _rows(tm, D_MODEL, 0), _rows(tm, D_MODEL, 1)],
        out_specs=[_rows(tm, 256)] * 2 + [_rows(tm, D_MODEL)] * 3,
        out_shape=[_sds((n, 256), F32), _sds((n, 256), BF16)] + [_sds((n, D_MODEL), BF16)] * 3,
        scratch_shapes=[_dil_scratch(chunk)],
        compiler_params=_params("parallel"),
    )(y_na, w_bna, *outs, *lses, w_bd, gates, gates)


def _branch_bwd(dh, w_out, gates, bn, bd, w_bna, w_bd, y, lses, *, tm, name):
    n = dh.shape[0]
    chunk = min(EPILOGUE_ROWS, tm)

    def body(dh_ref, wo_ref, sn_ref, sd_ref, bn_ref, bd_ref, wn_ref, wd_ref, y_ref, *rest):
        l_in = rest[0:3]
        dbn_ref, dbd_ref, dgn_ref, dgd_ref, dyna_ref = rest[3:8]
        do_out, cc_out, scr = rest[8:11], rest[11:14], rest[14]
        rr = lax.broadcasted_iota(jnp.int32, (256, 256), 0) >> 6
        cc = lax.broadcasted_iota(jnp.int32, (256, 256), 1) >> 6
        ones = jnp.where(rr == cc, 1.0, 0.0).astype(F32)
        for r0 in range(0, tm, chunk):
            rows = slice(r0, r0 + chunk)
            dm = lax.dot_general(dh_ref[rows, :], wo_ref[...], NT_DIMS, preferred_element_type=F32)
            dbn, dbd, dgn, dgd = (t.astype(BF16) for t in _gate_bwd_tile(
                dm, sn_ref[rows, :], bn_ref[rows, :], sd_ref[rows, :], bd_ref[rows, :]))
            dbn_ref[rows, :] = dbn
            dbd_ref[rows, :] = dbd
            dgn_ref[rows, :] = dgn
            dgd_ref[rows, :] = dgd
            dyna_ref[rows, :] = jnp.dot(dbn, wn_ref[...], preferred_element_type=F32).astype(BF16)
            dyv = jnp.dot(dbd, wd_ref[...], preferred_element_type=F32)
            lv = [_load_token_order(l_in[g], scr, d, chunk, r0) for g, d in enumerate(DIL_DILATIONS)]
            ws = _merge_weights(lv)
            tsum = jnp.dot(dyv * y_ref[rows, :], ones, preferred_element_type=F32,
                           precision=lax.Precision.HIGHEST)
            for g, d in enumerate(DIL_DILATIONS):
                _store_dil_order(ws[g] * dyv, do_out[g], scr, d, r0)
                _store_dil_order(-ws[g] * tsum, cc_out[g], scr, d, r0)

    specs = [_dil_spec(d, tm) for d in DIL_DILATIONS]
    wide = _rows(tm, D_MODEL)
    res = pl.pallas_call(
        body, name=name, grid=(n // tm,),
        in_specs=[wide, _const(w_out.shape), _rows(tm, D_MODEL, 0), _rows(tm, D_MODEL, 1), wide, wide,
                  _const(w_bna.shape), _const(w_bd.shape), _rows(tm, 256)] + specs,
        out_specs=[wide] * 4 + [_rows(tm, NA_WIDTH)] + specs + specs,
        out_shape=[_sds((n, D_MODEL), BF16)] * 4 + [_sds((n, NA_WIDTH), BF16)]
                  + [_sds((d, n // d, 256), BF16) for d in DIL_DILATIONS]
                  + [_sds((d, n // d, 256), F32) for d in DIL_DILATIONS],
        scratch_shapes=[_dil_scratch(chunk)],
        compiler_params=_params("parallel"),
    )(dh, w_out, gates, gates, bn, bd, w_bna, w_bd, y, *lses)
    return res[0], res[1], res[2], res[3], res[4], res[5:8], res[8:11]


_WEIGHTS = (("w_in", 1, 736), ("w_branch_na", 1, 128), ("w_branch_dil", 1, 128), ("w_out", 0, 128),
            ("w_up", 1, 512), ("w_down", 0, 512), ("w_ple_gate", 0, 128), ("w_ple_proj", 1, 128))
_W_IN, _W_BNA, _W_BD, _W_OUT, _W_UP, _W_DOWN, _W_PG, _W_PP = range(8)


def _to_full(gathered):
    return gathered.reshape(-1, gathered.shape[2])


def _to_chunks(widx, mat):
    return mat.reshape(N_DEV, _WEIGHTS[widx][2], mat.shape[1])


def _local_step(x, p_bf16, positions, target, g_mix, g_mlp, g_ple, g_final, rpb2,
                get_w_in, relay_rest, get_rest, send_grads):
    tm = 512
    half = HEAD_DIM // 2
    inv_freq = 10000.0 ** (-jnp.arange(half, dtype=F32) / half)
    ang = positions.astype(F32)[:, None] * inv_freq
    cos, sin = jnp.cos(ang), jnp.sin(ang)
    cos_t = jnp.tile(jnp.concatenate([cos, cos], axis=-1), (1, 4))
    sin_t = jnp.tile(jnp.concatenate([-sin, sin], axis=-1), (1, 4))
    rb = _rpb_table(rpb2)

    a = _rms_fwd(x, g_mix, tm=tm, name="rms_mix")
    w_in, token = get_w_in((a, cos_t, sin_t, p_bf16))
    na_qkv, gates, dq_g, dk_g, dv_g = _project_in(a, w_in, cos_t, sin_t, tm=512, name="mm_in", after=token)
    y_na = _na_fwd(na_qkv, rb, name="na_fwd")
    token = relay_rest(y_na)
    d_out, d_lse = [], []
    for g in range(3):
        o, lse = _dil_fwd(dq_g[g], dk_g[g], dv_g[g], name=f"dil_fwd{g}", after=token if g == 0 else None)
        d_out.append(o)
        d_lse.append(lse)
    w_bna, w_bd = get_rest(d_out[2], 0)
    y_dil, y_dil_b, bn, bd, mixed = _branch_mix(y_na, w_bna, d_out, d_lse, w_bd, gates, tm=tm, name="branch_mix")
    w_out, w_up, w_down, w_pg, w_pp = get_rest(mixed, 1)
    h1, c = _matmul(mixed, w_out, out_dtype=(F32, BF16), tm=512, tn=1024, tk=1024, name="mm_out",
                    extra=(x, g_mlp), epilogue=_residual_rms_tile)
    u, f = _matmul(c, w_up, tb=True, out_dtype=(BF16, BF16), tm=512, tn=2048, tk=1024, name="mm_up",
                   epilogue=lambda acc: (acc, jnp.square(jnp.maximum(acc, 0.0))))

    e, dpp, dgt, dh2, dh2_b, dg_final, loss, dg_ple = _tail_step(
        f, w_down, h1, w_pg, p_bf16, w_pp, target, g_final, g_ple, tm=256, name="tail_step")
    loss = loss[:, :128]
    gw_pp = _matmul(p_bf16, dpp, ta=True, transpose_out=True, out_dtype=BF16, tm=256, tn=1024, tk=2048,
                    name="mm_gw_pp")
    gw_pg = _matmul(e, dgt, ta=True, out_dtype=BF16, tm=512, tn=1024, tk=2048, name="mm_gw_pg")
    du = _matmul(dh2_b, w_down, tb=True, out_dtype=BF16, tm=512, tn=2048, tk=1024, name="mm_du",
                 extra=(u,), epilogue=lambda acc, uv: (acc * (2.0 * jnp.maximum(uv.astype(F32), 0.0)),))
    gw_down = _matmul(f, dh2_b, ta=True, out_dtype=BF16, tm=1024, tn=1024, tk=2048, name="mm_gw_down")
    gw_up = _matmul(c, du, ta=True, transpose_out=True, out_dtype=BF16, tm=512, tn=2048, tk=2048, name="mm_gw_up")
    dh1, dh1_b, dg_mlp = _matmul(
        du, w_up, out_dtype=(F32, BF16), tm=512, tn=1024, tk=4096, name="mm_dc",
        extra=(h1, g_mlp, dh2), epilogue=_rms_bwd_twice, n_colsum=1)
    dbn, dbd, dgn, dgd, dy_na, do_g, cc_g = _branch_bwd(dh1_b, w_out, gates, bn, bd, w_bna, w_bd, y_dil, d_lse,
                                                        tm=tm, name="branch_bwd")
    gw_out = _matmul(mixed, dh1_b, ta=True, out_dtype=BF16, tm=512, tn=1024, tk=2048, name="mm_gw_out")
    gw_bna = _matmul(y_na, dbn, ta=True, transpose_out=True, out_dtype=BF16, tm=512, tn=1024, tk=2048,
                     name="mm_gw_bna")
    gw_bd = _matmul(y_dil_b, dbd, ta=True, transpose_out=True, out_dtype=BF16, tm=256, tn=1024, tk=2048,
                    name="mm_gw_bd")
    token = send_grads((_W_PP, _W_PG, _W_DOWN, _W_UP, _W_OUT, _W_BNA, _W_BD),
                       (gw_pp, gw_pg, gw_down, gw_up, gw_out, gw_bna, gw_bd))
    dna = _na_bwd(na_qkv, dy_na, rb, name="na_bwd", after=token)
    drpb = _rpb_grad(dna[3].reshape(8, -1), name="rpb_grad")
    ddq, ddk, ddv = [], [], []
    for g in range(3):
        r = _dil_bwd(dq_g[g], dk_g[g], dv_g[g], do_g[g], d_lse[g], cc_g[g], name=f"dil_bwd{g}")
        ddq.append(r[0])
        ddk.append(r[1])
        ddv.append(r[2])
    dproj = _assemble_dproj(dna[0:3], ddq, ddk, ddv, dgn, dgd, cos_t, sin_t, tm=tm, name="assemble_dproj")
    gw_in = _matmul(a, dproj, ta=True, transpose_out=True, out_dtype=BF16, tm=512, tn=2944, tk=2048, name="mm_gw_in")
    token = send_grads((_W_IN,), (gw_in,))
    dx, dg_mix = _matmul(
        dproj, w_in, out_dtype=(F32,), tm=512, tn=1024, tk=5888, name="mm_da", after=token,
        extra=(x, g_mix, dh1), epilogue=_rms_bwd_tile, n_colsum=1)
    return loss, dx, (dg_mix, dg_mlp, dg_ple, dg_final), drpb


def _cast_bf16(t, *, name):
    def body(t_ref, o_ref):
        o_ref[...] = t_ref[...].astype(BF16)

    rows, cols = t.shape
    tr = 256 if rows % 256 == 0 else rows
    blk = pl.BlockSpec((tr, cols), lambda i: (i, 0))
    return pl.pallas_call(body, name=name, grid=(rows // tr,), in_specs=[blk], out_specs=blk,
                          out_shape=_sds(t.shape, BF16), compiler_params=_params("parallel"))(t)


def _adamw(w, g, m, v):
    m = ADAM_B1 * m + (1.0 - ADAM_B1) * g
    v = ADAM_B2 * v + (1.0 - ADAM_B2) * (g * g)
    m_hat = m / (1.0 - ADAM_B1 ** ADAM_STEP)
    v_hat = v / (1.0 - ADAM_B2 ** ADAM_STEP)
    delta = -ADAM_LR * (m_hat / (jnp.sqrt(v_hat) + ADAM_EPS) + ADAM_WD * w)
    return delta, m, v


def _sum_adamw(parts, w, m, v, *, tr, name, own=None, transposed=False):
    rows, cols = w.shape
    n_pre = 0 if own is None else 1

    def body(*refs):
        p_ref, w_ref, m_ref, v_ref = refs[n_pre:n_pre + 4]
        g_ref, d_ref, nm_ref, nv_ref = refs[-4:]
        g = (p_ref[0] if own is None else refs[n_pre + 4][...]).astype(F32)
        for s in range(1, N_DEV):
            g = g + p_ref[s].astype(F32)
        if transposed:
            g = g.T
        g_ref[...] = g
        d_ref[...], nm_ref[...], nv_ref[...] = _adamw(w_ref[...], g, m_ref[...], v_ref[...])

    if transposed:
        blk = pl.BlockSpec((rows, tr), lambda i, *_: (0, i))
        g_rows, steps = rows, cols // tr
    else:
        blk = pl.BlockSpec((tr, cols), lambda i, *_: (i, 0))
        g_rows, steps = cols, rows // tr
    in_specs = [pl.BlockSpec((N_DEV, tr, g_rows), lambda i, *_: (0, i, 0)), blk, blk, blk]
    args = [parts, w, m, v]
    if own is not None:
        in_specs.append(pl.BlockSpec((None, tr, g_rows), lambda i, idx: (idx[0], i, 0)))
        args = [own[1]] + args + [own[0]]
    return pl.pallas_call(
        body, name=name,
        grid_spec=pltpu.PrefetchScalarGridSpec(num_scalar_prefetch=n_pre, grid=(steps,), in_specs=in_specs,
                                               out_specs=[blk] * 4),
        out_shape=[_sds((rows, cols), F32)] * 4,
        compiler_params=_params("parallel"),
    )(*args)


_RPB_SIZE = 8 * 15 * 31


def _pack_small(g_mix, g_mlp, g_ple, g_final, rpb, loss_row):
    flat = jnp.concatenate([g_mix.reshape(-1), g_mlp.reshape(-1), g_ple.reshape(-1), g_final.reshape(-1),
                            rpb.reshape(-1), jnp.zeros((3840 - _RPB_SIZE,), F32), loss_row.reshape(-1),
                            jnp.zeros((128,), F32)])
    return flat.reshape(64, 128)


def _unpack_small(t):
    flat = t.reshape(-1)
    return (flat[0:1024].reshape(1, 1024), flat[4096:4096 + _RPB_SIZE].reshape(1, 8, 15, 31),
            flat[1024:2048].reshape(1, 1024), flat[2048:3072].reshape(1, 1024), flat[3072:4096])


def kernel(x, p, positions, g_mix, w_in, rpb, w_branch_na, w_branch_dil, w_out, g_mlp, w_up, w_down, g_ple, w_ple_gate, w_ple_proj, g_final, loss_target, m_g_mix, m_w_in, m_rpb, m_w_branch_na, m_w_branch_dil, m_w_out, m_g_mlp, m_w_up, m_w_down, m_g_ple, m_w_ple_gate, m_w_ple_proj, m_g_final, v_g_mix, v_w_in, v_rpb, v_w_branch_na, v_w_branch_dil, v_w_out, v_g_mlp, v_w_up, v_w_down, v_g_ple, v_w_ple_gate, v_w_ple_proj, v_g_final):
    sharded = dict(w_in=(w_in, m_w_in, v_w_in), w_branch_na=(w_branch_na, m_w_branch_na, v_w_branch_na),
                   w_branch_dil=(w_branch_dil, m_w_branch_dil, v_w_branch_dil), w_out=(w_out, m_w_out, v_w_out),
                   w_up=(w_up, m_w_up, v_w_up), w_down=(w_down, m_w_down, v_w_down),
                   w_ple_gate=(w_ple_gate, m_w_ple_gate, v_w_ple_gate),
                   w_ple_proj=(w_ple_proj, m_w_ple_proj, v_w_ple_proj))
    shards = {k: tuple(t[0] for t in val) for k, val in sharded.items()}

    me = _my_index()

    shards["w_in"] = tuple(t.T for t in shards["w_in"])

    w_in_b = _cast_bf16(shards["w_in"][0], name="cast_w_in")
    rest_b = [shards[name][0].astype(BF16).T if axis == 1 else shards[name][0].astype(BF16)
              for name, axis, _ in _WEIGHTS[1:]]
    first_in, token_in = _start_copies(_first_leg_copies, [w_in_b], [_sds((N_DEV,) + w_in_b.shape, BF16)], 4,
                                       name="start_gather_w_in")

    def whole(landed, mine):
        return _to_full(lax.dynamic_update_index_in_dim(landed, mine, me, 0))

    rest = {}

    def get_w_in(after):
        (mine,), landed = _wait_copies(_first_leg_copies, first_in, (*after, *rest_b), name="wait_gather_w_in")
        second, token = _start_copies(_second_leg_copies, [], landed, 3, name="start_forward_w_in")
        _, (landed,) = _wait_copies(_second_leg_copies, second, token, name="wait_forward_w_in")
        rest["first"], token = _start_copies(_first_leg_copies, rest_b,
                                             [_sds((N_DEV,) + t.shape, BF16) for t in rest_b], 4 * len(rest_b),
                                             name="start_gather_rest", after=landed)
        return whole(landed, mine), token

    def relay_rest(after):
        rest["mine"], landed = _wait_copies(_first_leg_copies, rest["first"], after, name="wait_gather_rest")
        rest["second"], token = _start_copies(_second_leg_copies, [], landed, 3 * len(rest_b),
                                              name="start_forward_rest")
        return token

    def get_rest(after, stage):
        n_src, send_sems, recv_sems, bufs = rest["second"]
        part = slice(0, 2) if stage == 0 else slice(2, len(rest_b))
        _, landed = _wait_copies(functools.partial(_second_leg_copies, first=part.start),
                                 (n_src, send_sems, recv_sems, bufs[part]), after,
                                 name=f"wait_forward_rest{stage}")
        return [whole(t, own) for t, own in zip(landed, rest["mine"][part])]

    sent = []

    def send_grads(indices, grads):
        chunked = [_to_chunks(i, g) for i, g in zip(indices, grads)]
        handle, token = _start_copies(_exchange_copies, chunked, [_sds(t.shape, BF16) for t in chunked],
                                      7 * len(chunked),
                                      name="start_exchange_" + ("w_in" if indices == (_W_IN,) else "rest"))
        sent.append((indices, handle))
        return token

    g_mix_0 = g_mix + token_in[0:1, 0:1]
    loss, dx, dgs, drpb = _local_step(
        x[0], p[0, 0].astype(BF16), positions[0], loss_target[0],
        g_mix_0, g_mlp, g_ple, g_final.reshape(1, -1), rpb[0], get_w_in, relay_rest, get_rest, send_grads)

    drpb3 = drpb.reshape(8, 16, 32)[:, :15, :31]
    small = _pack_small(dgs[0], dgs[1], dgs[2], dgs[3], drpb3, loss)
    share, done = _start_copies(_gather_copies, [small], [_sds((N_DEV,) + small.shape, F32)], 7,
                                name="start_share_small")

    out = {}

    def update(indices, handle, after):
        chunked, landed = _wait_copies(_exchange_copies, handle, after,
                                       name="wait_exchange_" + ("w_in" if indices == (_W_IN,) else "rest"))
        for i, part, mine in zip(indices, landed, chunked):
            name = _WEIGHTS[i][0]
            w, m, v = shards[name]
            turned = _WEIGHTS[i][1] == 1 and i != _W_IN
            res = _sum_adamw(part, w, m, v, tr=368 if i == _W_IN else 128, name="adamw_" + name,
                             own=(mine, me.reshape(1).astype(jnp.int32)), transposed=turned)
            out[name] = [(t.T if i == _W_IN else t)[None] for t in res]
        return res[0]

    rest_sent, w_in_sent = sent
    done = update(*rest_sent, done)
    (small,), (small_landed,) = _wait_copies(_gather_copies, share, done, name="wait_share_small")
    small_all = lax.dynamic_update_index_in_dim(small_landed, small, me, 0)
    small_w = _pack_small(g_mix, g_mlp, g_ple, g_final, rpb, jnp.zeros((128,), F32))
    small_m = _pack_small(m_g_mix, m_g_mlp, m_g_ple, m_g_final, m_rpb, jnp.zeros((128,), F32))
    small_v = _pack_small(v_g_mix, v_g_mlp, v_g_ple, v_g_final, v_rpb, jnp.zeros((128,), F32))
    res = _sum_adamw(small_all, small_w, small_m, small_v, tr=64, name="adamw_small")
    unpacked = [_unpack_small(t) for t in res]
    for i, name in enumerate(("g_mix", "rpb", "g_mlp", "g_ple", "g_final")):
        out[name] = [u[i] for u in unpacked]
    loss_total = res[0][62, 0]
    update(*w_in_sent, [pltpu.with_memory_space_constraint(t, pltpu.HBM) for u in unpacked for t in u])

    order = ("g_mix", "w_in", "rpb", "w_branch_na", "w_branch_dil", "w_out", "g_mlp", "w_up", "w_down",
             "g_ple", "w_ple_gate", "w_ple_proj", "g_final")
    grads = [out[k][0] for k in order]
    deltas = [out[k][1] for k in order]
    new_m = [out[k][2] for k in order]
    new_v = [out[k][3] for k in order]
    return (loss_total, dx[None], *grads, *deltas, *new_m, *new_v)
```

```python
import functools

import jax
import jax.numpy as jnp
from jax import lax
from jax.experimental import pallas as pl
from jax.experimental.pallas import tpu as pltpu

F32 = jnp.float32
BF16 = jnp.bfloat16

D_MODEL = 1024
HEAD_DIM = 64
GRID_W = 64
NA_WIDTH = 512
DIL_WIDTH = 768
IN_WIDTH = 5888
DIL_DILATIONS = (1, 4, 16)
DIL_RADIUS = 64
NA_WIN_ROWS = 8
RMS_EPS = 1e-6
NEG_INF = -1e30
QK_SCALE = HEAD_DIM ** -0.5

ADAM_LR = 0.001
ADAM_B1 = 0.9
ADAM_B2 = 0.999
ADAM_EPS = 1e-08
ADAM_WD = 0.01
ADAM_STEP = 10

N_DEV = 8
VMEM_LIMIT = 56 * 1024 * 1024
EPILOGUE_ROWS = 256
MESH = pl.DeviceIdType.MESH

NT_DIMS = (((1,), (1,)), ((), ()))
TN_DIMS = (((0,), (0,)), ((), ()))


def _sds(shape, dtype):
    return jax.ShapeDtypeStruct(shape, dtype)


def _params(*sem):
    return pltpu.CompilerParams(dimension_semantics=sem, vmem_limit_bytes=VMEM_LIMIT)


def _rows(tm, width, col=0):
    return pl.BlockSpec((tm, width), lambda i, c=col: (i, c))


def _const(shape):
    zeros = (0,) * len(shape)
    return pl.BlockSpec(shape, lambda i: zeros)


def _my_index():
    return 4 * lax.axis_index("x") + 2 * lax.axis_index("y") + lax.axis_index("c")


def _peer(k):
    x, y, c = lax.axis_index("x"), lax.axis_index("y"), lax.axis_index("c")
    px = 1 - x if k & 4 else x
    py = 1 - y if k & 2 else y
    pc = 1 - c if k & 1 else c
    return (px, py, pc), 4 * px + 2 * py + pc


def _call(body, *, name, grid, in_specs, out_specs, out_shape, scratch_shapes, args, after=None):
    n_in, n_out = len(in_specs), len(out_specs)
    extra = [] if after is None else [after]
    n_x = n_in + len(extra)

    def plain(*refs):
        body(refs[:n_in], refs[n_x:n_x + n_out], refs[n_x + n_out:])

    res = pl.pallas_call(plain, name=name, grid=grid,
                         in_specs=list(in_specs) + [pl.BlockSpec(memory_space=pl.ANY)] * len(extra),
                         out_specs=out_specs, out_shape=out_shape, scratch_shapes=scratch_shapes,
                         compiler_params=_params(*(("arbitrary",) * len(grid))))(*args, *extra)
    return list(res)


_HBM_SPEC = pl.BlockSpec(memory_space=pltpu.HBM)
_SEM_SPEC = pl.BlockSpec(memory_space=pltpu.SEMAPHORE)
_SIDE_EFFECT = pltpu.SideEffectType.DATAFLOW_SIDE_EFFECTING


_FIRST_LEG = (6, 2, 4, 1)
_SECOND_LEG = (2, 4, 6)


def _gather_copies(srcs, lands, send, recv, sending):
    me = _my_index()
    out = []
    for w in range(len(srcs)):
        for k in range(1, N_DEV):
            dev, idx = _peer(k)
            out.append(pltpu.make_async_remote_copy(
                src_ref=srcs[w], dst_ref=lands[w].at[me if sending else idx],
                send_sem=send.at[w * 7 + k - 1], recv_sem=recv.at[w * 7 + k - 1],
                device_id=dev, device_id_type=MESH))
    return out


def _first_leg_copies(srcs, lands, send, recv, sending):
    me = _my_index()
    out = []
    for w in range(len(srcs)):
        for j, k in enumerate(_FIRST_LEG):
            dev, idx = _peer(k)
            out.append(pltpu.make_async_remote_copy(
                src_ref=srcs[w], dst_ref=lands[w].at[me if sending else idx],
                send_sem=send.at[w * 4 + j], recv_sem=recv.at[w * 4 + j],
                device_id=dev, device_id_type=MESH))
    return out


def _second_leg_copies(srcs, lands, send, recv, sending, first=0):
    sibling, _ = _peer(1)
    out = []
    for w in range(len(lands)):
        for j, k in enumerate(_SECOND_LEG):
            slot = _peer(k if sending else k ^ 1)[1]
            sem = (first + w) * 3 + j
            out.append(pltpu.make_async_remote_copy(
                src_ref=lands[w].at[slot], dst_ref=lands[w].at[slot],
                send_sem=send.at[sem], recv_sem=recv.at[sem],
                device_id=sibling, device_id_type=MESH))
    return out


def _exchange_copies(srcs, lands, send, recv, sending):
    out = []
    for w in range(len(srcs)):
        for k in range(1, N_DEV):
            dev, idx = _peer(k)
            out.append(pltpu.make_async_remote_copy(
                src_ref=srcs[w].at[idx], dst_ref=lands[w].at[k],
                send_sem=send.at[w * 7 + k - 1], recv_sem=recv.at[w * 7 + k - 1],
                device_id=dev, device_id_type=MESH))
    return out


def _start_copies(make, srcs, lands, n_copies, *, name, after=None):
    n_src, n_buf = len(srcs), len(srcs) + len(lands)
    extra = [] if after is None else [after]

    def body(*refs):
        send, recv = refs[n_buf + len(extra)], refs[n_buf + len(extra) + 1]
        for cp in make(refs[:n_src], refs[n_src:n_buf], send, recv, True):
            cp.start()
        refs[-1][...] = jnp.zeros_like(refs[-1])

    bufs = list(srcs) + [lax.empty(t.shape, t.dtype) if isinstance(t, jax.ShapeDtypeStruct) else t for t in lands]
    res = pl.pallas_call(
        body, name=name,
        out_shape=(pltpu.SemaphoreType.DMA((n_copies,)), pltpu.SemaphoreType.DMA((n_copies,)),
                   *[pltpu.HBM(t.shape, t.dtype) for t in bufs], _sds((8, 128), F32)),
        in_specs=[_HBM_SPEC] * n_buf + [pl.BlockSpec(memory_space=pl.ANY)] * len(extra),
        out_specs=(_SEM_SPEC, _SEM_SPEC, *([_HBM_SPEC] * n_buf), pl.BlockSpec(memory_space=pltpu.VMEM)),
        input_output_aliases={i: 2 + i for i in range(n_buf)},
        compiler_params=pltpu.CompilerParams(has_side_effects=_SIDE_EFFECT),
    )(*[pltpu.with_memory_space_constraint(t, pltpu.HBM) for t in bufs], *extra)
    return (n_src, res[0], res[1], res[2:2 + n_buf]), res[-1]


def _wait_copies(make, handle, after, *, name):
    n_src, send_sems, recv_sems, bufs = handle
    n_buf = len(bufs)
    after = list(after) if isinstance(after, (tuple, list)) else [after]

    def body(*refs):
        for cp in make(refs[:n_src], refs[n_src:n_buf], refs[n_buf], refs[n_buf + 1], False):
            cp.wait_send()
            cp.wait_recv()

    res = pl.pallas_call(
        body, name=name,
        out_shape=tuple(pltpu.HBM(t.shape, t.dtype) for t in bufs),
        in_specs=[_HBM_SPEC] * n_buf + [_SEM_SPEC, _SEM_SPEC] + [pl.BlockSpec(memory_space=pl.ANY)] * len(after),
        out_specs=tuple([_HBM_SPEC] * n_buf),
        input_output_aliases={i: i for i in range(n_buf)},
        compiler_params=pltpu.CompilerParams(has_side_effects=_SIDE_EFFECT),
    )(*bufs, send_sems, recv_sems, *after)
    return list(res[:n_src]), list(res[n_src:])


def _add_colsums(s_refs, sums, step):
    for s_ref, val in zip(s_refs, sums):
        @pl.when(step == 0)
        def _(s_ref=s_ref, val=val):
            s_ref[...] = val

        @pl.when(step > 0)
        def _(s_ref=s_ref, val=val):
            s_ref[...] += val


def _matmul(a, b, *, ta=False, tb=False, out_dtype, tm, tn, tk, name, after=None, extra=(), epilogue=None,
            n_colsum=0, transpose_out=False):
    m, k = (a.shape[1], a.shape[0]) if ta else a.shape
    n = b.shape[0] if tb else b.shape[1]
    tm, tn, tk = min(tm, m), min(tn, n), min(tk, k)
    nk = k // tk
    dims = (((0 if ta else 1,), (1 if tb else 0,)), ((), ()))
    out_dtypes = out_dtype if isinstance(out_dtype, tuple) else (out_dtype,)
    n_tiles = len(out_dtypes)

    def add_colsums(o_refs, sums):
        _add_colsums(o_refs[n_tiles:], sums, pl.program_id(1))

    def finish(acc, x_refs, o_refs):
        vals = (acc,) if epilogue is None else epilogue(acc, *[r[...] for r in x_refs])
        for o_ref, val in zip(o_refs[:n_tiles], vals[:n_tiles]):
            o_ref[...] = (val.T if transpose_out else val).astype(o_ref.dtype)
        add_colsums(o_refs, vals[n_tiles:])

    chunk = EPILOGUE_ROWS if (nk == 1 and epilogue is not None and not ta and tm % EPILOGUE_ROWS == 0) else None

    def body(ins, outs, acc):
        a_ref, b_ref = ins[:2]
        if chunk is not None:
            sums = None
            for r0 in range(0, tm, chunk):
                part = lax.dot_general(a_ref[r0:r0 + chunk, :], b_ref[...], dims, preferred_element_type=F32)
                vals = epilogue(part, *[r[...] if r.shape[0] == 1 else r[r0:r0 + chunk, :] for r in ins[2:]])
                for o_ref, val in zip(outs[:n_tiles], vals[:n_tiles]):
                    o_ref[r0:r0 + chunk, :] = val.astype(o_ref.dtype)
                sums = vals[n_tiles:] if sums is None else [s + v for s, v in zip(sums, vals[n_tiles:])]
            add_colsums(outs, sums)
            return
        part = lax.dot_general(a_ref[...], b_ref[...], dims, preferred_element_type=F32)
        if nk == 1:
            finish(part, ins[2:], outs)
            return
        acc_ref, = acc
        kk = pl.program_id(2)

        @pl.when(kk == 0)
        def _():
            acc_ref[...] = part

        @pl.when(kk > 0)
        def _():
            acc_ref[...] += part

        @pl.when(kk == nk - 1)
        def _():
            finish(acc_ref[...], ins[2:], outs)

    a_spec = (pl.BlockSpec((tk, tm), lambda j, i, kk: (kk, i)) if ta
              else pl.BlockSpec((tm, tk), lambda j, i, kk: (i, kk)))
    b_spec = (pl.BlockSpec((tn, tk), lambda j, i, kk: (j, kk)) if tb
              else pl.BlockSpec((tk, tn), lambda j, i, kk: (kk, j)))
    tile = pl.BlockSpec((tm, tn), lambda j, i, kk: (i, j))
    row = pl.BlockSpec((1, tn), lambda j, i, kk: (0, j))

    out_tile, out_dims = (pl.BlockSpec((tn, tm), lambda j, i, kk: (j, i)), (n, m)) if transpose_out else (tile, (m, n))
    res = _call(
        body, name=name, grid=(n // tn, m // tm, nk),
        in_specs=[a_spec, b_spec] + [row if t.shape[0] == 1 else tile for t in extra],
        out_specs=[out_tile] * n_tiles + [row] * n_colsum,
        out_shape=[_sds(out_dims, dt) for dt in out_dtypes] + [_sds((1, n), F32)] * n_colsum,
        scratch_shapes=[] if nk == 1 else [pltpu.VMEM((tm, tn), F32)],
        args=(a, b, *extra), after=after)
    return res if isinstance(out_dtype, tuple) or n_colsum else res[0]


def _rstd(h):
    return lax.rsqrt(jnp.mean(h * h, axis=-1, keepdims=True) + RMS_EPS)


def _sigmoid(z):
    return 1.0 / (1.0 + jnp.exp(-z))


def _rms_fwd(x, g, *, tm, name):
    n = x.shape[0]

    def body(x_ref, g_ref, o_ref):
        h = x_ref[...]
        o_ref[...] = (h * _rstd(h) * g_ref[...]).astype(BF16)

    return pl.pallas_call(
        body, name=name, grid=(n // tm,),
        in_specs=[_rows(tm, D_MODEL), _const((1, D_MODEL))],
        out_specs=_rows(tm, D_MODEL), out_shape=_sds((n, D_MODEL), BF16),
        compiler_params=_params("parallel"),
    )(x, g)


def _swap_halves(t):
    lane = lax.broadcasted_iota(jnp.int32, (t.shape[0], 128), 1)
    pieces = [t[:, c:c + 128] for c in range(0, t.shape[1], 128)]
    return jnp.concatenate([jnp.where((lane & 63) < 32, pltpu.roll(h, 96, 1), pltpu.roll(h, 32, 1))
                            for h in pieces], axis=1)


def _dil_spec(dil, tm):
    return pl.BlockSpec((dil, tm // dil, 256), lambda i: (0, i, 0))


def _dil_scratch(tm):
    return pltpu.VMEM((2, tm, 128), F32)


def _load_token_order(src, scr, dil, rows, row0=0):
    if dil == 1:
        return src[0, row0:row0 + rows, :]
    for j in range(dil):
        for c in range(2):
            scr[c, pl.ds(j, rows // dil, stride=dil), :] = (
                src[j, row0 // dil:(row0 + rows) // dil, c * 128:(c + 1) * 128])
    return jnp.concatenate([scr[0, 0:rows, :], scr[1, 0:rows, :]], axis=1)


def _store_dil_order(val, dst, scr, dil, row0=0):
    rows = val.shape[0]
    if dil == 1:
        dst[0, row0:row0 + rows, :] = val.astype(dst.dtype)
        return
    for c in range(2):
        scr[c] = val[:, c * 128:(c + 1) * 128]
    for j in range(dil):
        for c in range(2):
            dst[j, row0 // dil:(row0 + rows) // dil, c * 128:(c + 1) * 128] = (
                scr[c, pl.ds(j, rows // dil, stride=dil), :].astype(dst.dtype))


def _project_in(a, w_t, cos_t, sin_t, *, tm, name, after=None):
    n = a.shape[0]
    n_dil = len(DIL_DILATIONS)
    na_w, dil_w = 3 * NA_WIDTH, 3 * DIL_WIDTH
    chunk = min(EPILOGUE_ROWS, tm)
    extra = [] if after is None else [after]

    def body(a_ref, w_ref, cos_ref, sin_ref, *rest):
        na_ref, gate_ref = rest[len(extra):len(extra) + 2]
        outs, scr = rest[len(extra) + 2:len(extra) + 2 + 3 * n_dil], rest[-1]

        def part(r0, first, width):
            return lax.dot_general(a_ref[r0:r0 + chunk, :], w_ref[first:first + width, :], NT_DIMS,
                                   preferred_element_type=F32)

        for r0 in range(0, tm, chunk):
            na_ref[r0:r0 + chunk, :] = part(r0, 0, na_w).astype(BF16)
            dil_part = part(r0, na_w, dil_w)
            cosv, sinv = cos_ref[r0:r0 + chunk, :], sin_ref[r0:r0 + chunk, :]
            for t in range(3):
                for gi, dil in enumerate(DIL_DILATIONS):
                    c0 = (t * n_dil + gi) * 256
                    val = dil_part[:, c0:c0 + 256]
                    if t < 2:
                        val = val * cosv + _swap_halves(val) * sinv
                    _store_dil_order(val, outs[t * n_dil + gi], scr, dil, r0)
            gate_ref[r0:r0 + chunk, :] = _sigmoid(part(r0, na_w + dil_w, 2 * D_MODEL)).astype(BF16)

    out_specs = [_rows(tm, na_w), _rows(tm, 2 * D_MODEL)]
    out_shape = [_sds((n, na_w), BF16), _sds((n, 2 * D_MODEL), BF16)]
    for _ in range(3):
        for dil in DIL_DILATIONS:
            out_specs.append(pl.BlockSpec((dil, tm // dil, 256), lambda i: (0, i, 0)))
            out_shape.append(_sds((dil, n // dil, 256), BF16))
    res = pl.pallas_call(
        body, name=name, grid=(n // tm,),
        in_specs=[_rows(tm, D_MODEL), _const(w_t.shape), _rows(tm, 256), _rows(tm, 256)]
                 + [pl.BlockSpec(memory_space=pl.ANY)] * len(extra),
        out_specs=out_specs, out_shape=out_shape,
        scratch_shapes=[pltpu.VMEM((2, chunk, 128), F32)],
        compiler_params=_params("parallel"),
    )(a, w_t, cos_t, sin_t, *extra)
    return res[0], res[1], res[2:5], res[5:8], res[8:11]


def _residual_rms_tile(delta, h, g):
    hn = h + delta
    return hn, hn * _rstd(hn) * g


def _gate_mix_tile(b2, s1, b1, s2):
    return b2, s1.astype(F32) * b1.astype(F32) + s2.astype(F32) * b2


def _gate_bwd_tile(dm, s1, b1, s2, b2):
    s1, b1, s2, b2 = (t.astype(F32) for t in (s1, b1, s2, b2))
    return dm * s1, dm * s2, dm * b1 * s1 * (1.0 - s1), dm * b2 * s2 * (1.0 - s2)


def _tail_tile(gt, pp, h2, target, g):
    sg = _sigmoid(gt)
    h3 = h2 + sg * pp
    r3 = _rstd(h3)
    n3 = h3 * r3
    err = n3 * g - target
    loss = 0.5 * jnp.sum(jnp.sum(err * err, axis=-1, keepdims=True) / D_MODEL)
    dy = err / D_MODEL
    dn = dy * g
    dh3 = r3 * (dn - n3 * jnp.mean(dn * n3, axis=-1, keepdims=True))
    return (dh3, dh3 * sg, dh3 * pp * sg * (1.0 - sg),
            jnp.sum(dy * n3, axis=0, keepdims=True), jnp.full((1, gt.shape[1]), loss, F32))


def _rms_bwd_tile(dz, h, g, dres):
    r = _rstd(h)
    nrm = h * r
    dn = dz * g
    dh = dres + r * (dn - nrm * jnp.mean(dn * nrm, axis=-1, keepdims=True))
    return dh, jnp.sum(dz * nrm, axis=0, keepdims=True)


def _rms_bwd_twice(dz, h, g, dres):
    dh, dg = _rms_bwd_tile(dz, h, g, dres)
    return dh, dh, dg


def _tail_step(f, w_down, h1, w_pg, p, w_pp, target, g_final, g_ple, *, tm, name):
    n = f.shape[0]
    chunk = min(EPILOGUE_ROWS, tm)

    def body(f_ref, wd_ref, h1_ref, wg_ref, p_ref, wp_ref, t_ref, gf_ref, gp_ref,
             e_ref, dpp_ref, dgt_ref, dh2_ref, dh2b_ref, dgf_ref, loss_ref, dgp_ref):
        sums = None
        for r0 in range(0, tm, chunk):
            rows = slice(r0, r0 + chunk)
            delta = jnp.dot(f_ref[rows, :], wd_ref[...], preferred_element_type=F32)
            h2, e = _residual_rms_tile(delta, h1_ref[rows, :], gp_ref[...])
            e = e.astype(BF16)
            e_ref[rows, :] = e
            gt = jnp.dot(e, wg_ref[...], preferred_element_type=F32)
            pp = lax.dot_general(p_ref[rows, :], wp_ref[...], NT_DIMS, preferred_element_type=F32)
            dh3, dpp, dgt, dgf, loss = _tail_tile(gt, pp, h2, t_ref[rows, :], gf_ref[...])
            dgt = dgt.astype(BF16)
            dpp_ref[rows, :] = dpp.astype(BF16)
            dgt_ref[rows, :] = dgt
            dz = lax.dot_general(dgt, wg_ref[...], NT_DIMS, preferred_element_type=F32)
            dh2, dgp = _rms_bwd_tile(dz, h2, gp_ref[...], dh3)
            dh2_ref[rows, :] = dh2
            dh2b_ref[rows, :] = dh2.astype(BF16)
            vals = (dgf, loss, dgp)
            sums = vals if sums is None else [s + v for s, v in zip(sums, vals)]
        _add_colsums((dgf_ref, loss_ref, dgp_ref), sums, pl.program_id(0))

    wide, gain = _rows(tm, D_MODEL), _const((1, D_MODEL))
    return pl.pallas_call(
        body, name=name, grid=(n // tm,),
        in_specs=[_rows(tm, f.shape[1]), _const(w_down.shape), wide, _const(w_pg.shape),
                  _rows(tm, p.shape[1]), _const(w_pp.shape), wide, gain, gain],
        out_specs=[wide] * 5 + [gain] * 3,
        out_shape=[_sds((n, D_MODEL), dt) for dt in (BF16, BF16, BF16, F32, BF16)]
                  + [_sds((1, D_MODEL), F32)] * 3,
        compiler_params=_params("arbitrary"),
    )(f, w_down, h1, w_pg, p, w_pp, target, g_final, g_ple)


def _assemble_dproj(dna, ddil_q, ddil_k, ddil_v, dgn, dgd, cos_t, sin_t, *, tm, name):
    n = dgn.shape[0]

    def body(*refs):
        dq_ref, dk_ref, dv_ref = refs[0:3]
        dil_in = refs[3:12]
        dgn_ref, dgd_ref, cos_ref, sin_ref, o_ref, scr = refs[12:18]
        o_ref[:, 0:512] = dq_ref[...]
        o_ref[:, 512:1024] = dk_ref[...].astype(BF16)
        o_ref[:, 1024:1536] = dv_ref[...].astype(BF16)
        cosv, sinv = cos_ref[...], sin_ref[...]
        for t in range(3):
            for gi, dil in enumerate(DIL_DILATIONS):
                val = _load_token_order(dil_in[t * 3 + gi], scr, dil, tm)
                if t < 2:
                    val = val * cosv + _swap_halves(val * sinv)
                c0 = 1536 + t * DIL_WIDTH + gi * 256
                o_ref[:, c0:c0 + 256] = val.astype(BF16)
        o_ref[:, 3840:4864] = dgn_ref[...]
        o_ref[:, 4864:5888] = dgd_ref[...]

    in_specs = [_rows(tm, NA_WIDTH)] * 3
    for _ in range(3):
        for dil in DIL_DILATIONS:
            in_specs.append(pl.BlockSpec((dil, tm // dil, 256), lambda i: (0, i, 0)))
    in_specs += [_rows(tm, D_MODEL)] * 2 + [_rows(tm, 256)] * 2
    return pl.pallas_call(
        body, name=name, grid=(n // tm,), in_specs=in_specs,
        out_specs=_rows(tm, IN_WIDTH), out_shape=_sds((n, IN_WIDTH), BF16),
        scratch_shapes=[_dil_scratch(tm)],
        compiler_params=_params("parallel"),
    )(*dna, *ddil_q, *ddil_k, *ddil_v, dgn, dgd, cos_t, sin_t)


N_ROW_OFF = 2 * NA_WIN_ROWS - 1
N_PAIRS = N_ROW_OFF - 1
RB_WIDTH = (N_ROW_OFF + 1) * GRID_W


def _na_bias(rb_ref, pair_scr):
    shape = (GRID_W, RB_WIDTH)
    qc = lax.broadcasted_iota(jnp.int32, shape, 0)
    qc2 = lax.broadcasted_iota(jnp.int32, (GRID_W, 128), 0)
    kc2 = lax.broadcasted_iota(jnp.int32, (GRID_W, 128), 1) & (GRID_W - 1)
    cs = jnp.clip(qc2 - 8, 0, GRID_W - 16)
    valid = (kc2 >= cs) & (kc2 < cs + 16)
    for hh in range(2):
        t = jnp.broadcast_to(rb_ref[hh], shape)
        t = pltpu.roll(t, RB_WIDTH - 15, 1)
        for b in range(6):
            t = jnp.where(((qc >> b) & 1) == 1, pltpu.roll(t, 1 << b, 1), t)
        t_odd = pltpu.roll(t, RB_WIDTH - GRID_W, 1)
        for ro in range(N_PAIRS):
            src = t if ro % 2 == 0 else t_odd
            base = (ro // 2) * 128
            pair_scr[hh, ro] = jnp.where(valid, src[:, base:base + 128], NEG_INF)


NA_GROUP_FWD = 8
NA_GROUP_BWD = 4


def _stack_heads(ref, r, scale=1.0):
    lane = lax.broadcasted_iota(jnp.int32, (GRID_W, 128), 1)
    t = ref[pl.ds(pl.multiple_of(r * GRID_W, GRID_W), GRID_W), :].astype(F32) * scale
    return jnp.concatenate([jnp.where(lane < 64, t, 0.0), jnp.where(lane >= 64, t, 0.0)], axis=0).astype(BF16)


def _unstack_heads(t2):
    lane = lax.broadcasted_iota(jnp.int32, (GRID_W, 128), 1)
    return jnp.where(lane < 64, t2[:GRID_W], t2[GRID_W:])


def _na_window(k_ref, v_ref, r, n_rows):
    rs = jnp.clip(r - NA_WIN_ROWS // 2, 0, n_rows - NA_WIN_ROWS)
    ro0 = (NA_WIN_ROWS - 1) - (r - rs)
    off = pl.multiple_of(rs * GRID_W, GRID_W)
    kw = k_ref[pl.ds(off, NA_WIN_ROWS * GRID_W), :]
    vw = v_ref[pl.ds(off, NA_WIN_ROWS * GRID_W), :]
    return kw, vw, off, ro0


def _na_probs(s_raw, pair_scr, ro0):
    bias = [jnp.concatenate([pair_scr[hh, ro0 + 2 * j] for j in range(NA_WIN_ROWS // 2)], axis=1)
            for hh in range(2)]
    s = s_raw + jnp.concatenate(bias, axis=0)
    m = jnp.max(s, axis=-1, keepdims=True)
    e = jnp.exp(s - m)
    return e * (1.0 / jnp.sum(e, axis=-1, keepdims=True))


def _na_qkv_specs(n):
    pairs = NA_WIDTH // 128
    return [pl.BlockSpec((n, 128), lambda h, first=t * pairs: (0, first + h)) for t in range(3)]


def _na_fwd(qkv, rb, *, name):
    n = qkv.shape[0]
    n_rows = n // GRID_W

    def body(ins, outs, scr):
        q_ref, k_ref, v_ref, rb_ref = ins
        o_ref, = outs
        pair_scr, = scr
        _na_bias(rb_ref, pair_scr)

        def group(g, carry):
            rows = [g * NA_GROUP_FWD + t for t in range(NA_GROUP_FWD)]
            wins = [_na_window(k_ref, v_ref, r, n_rows) for r in rows]
            raw = [lax.dot_general(_stack_heads(q_ref, r, QK_SCALE), w[0], NT_DIMS, preferred_element_type=F32)
                   for r, w in zip(rows, wins)]
            probs = [_na_probs(s, pair_scr, w[3]) for s, w in zip(raw, wins)]
            outs2 = [jnp.dot(p.astype(BF16), w[1], preferred_element_type=F32) for p, w in zip(probs, wins)]
            for r, o2 in zip(rows, outs2):
                o_ref[pl.ds(pl.multiple_of(r * GRID_W, GRID_W), GRID_W), :] = _unstack_heads(o2).astype(BF16)
            return carry

        lax.fori_loop(0, n_rows // NA_GROUP_FWD, group, 0)

    col = pl.BlockSpec((n, 128), lambda h: (0, h))
    return _call(
        body, name=name, grid=(NA_WIDTH // 128,),
        in_specs=_na_qkv_specs(n) + [pl.BlockSpec((2, 1, RB_WIDTH), lambda h: (h, 0, 0))],
        out_specs=[col], out_shape=[_sds((n, NA_WIDTH), BF16)],
        scratch_shapes=[pltpu.VMEM((2, N_PAIRS, GRID_W, 128), F32)],
        args=(qkv, qkv, qkv, rb))[0]


def _na_bwd(qkv, do, rb, *, name, after=None):
    n = qkv.shape[0]
    n_rows = n // GRID_W
    win = NA_WIN_ROWS * GRID_W

    def body(ins, outs, scr):
        q_ref, k_ref, v_ref, do_ref, rb_ref = ins
        dq_ref, dk_ref, dv_ref, drb_ref = outs
        pair_scr, acc_scr = scr
        _na_bias(rb_ref, pair_scr)
        acc_scr[...] = jnp.zeros_like(acc_scr)
        dk_ref[...] = jnp.zeros_like(dk_ref)
        dv_ref[...] = jnp.zeros_like(dv_ref)

        def group(g, carry):
            rows = [g * NA_GROUP_BWD + t for t in range(NA_GROUP_BWD)]
            wins = [_na_window(k_ref, v_ref, r, n_rows) for r in rows]
            qss = [_stack_heads(q_ref, r, QK_SCALE) for r in rows]
            doss = [_stack_heads(do_ref, r) for r in rows]
            raw = [lax.dot_general(qs, w[0], NT_DIMS, preferred_element_type=F32) for qs, w in zip(qss, wins)]
            dps = [lax.dot_general(dos, w[1], NT_DIMS, preferred_element_type=F32) for dos, w in zip(doss, wins)]
            probs = [_na_probs(s, pair_scr, w[3]) for s, w in zip(raw, wins)]
            dss = [p * (dp - jnp.sum(p * dp, axis=-1, keepdims=True)) for p, dp in zip(probs, dps)]
            dsbs = [ds.astype(BF16) for ds in dss]
            dq2s = [jnp.dot(dsb, w[0], preferred_element_type=F32) for dsb, w in zip(dsbs, wins)]
            dkws = [lax.dot_general(dsb, qs, TN_DIMS, preferred_element_type=F32) for dsb, qs in zip(dsbs, qss)]
            dvws = [lax.dot_general(p.astype(BF16), dos, TN_DIMS, preferred_element_type=F32)
                    for p, dos in zip(probs, doss)]
            for t, r in enumerate(rows):
                _, _, off, ro0 = wins[t]
                for hh in range(2):
                    for j in range(NA_WIN_ROWS // 2):
                        acc_scr[hh, ro0 + 2 * j] += dss[t][hh * GRID_W:(hh + 1) * GRID_W, j * 128:(j + 1) * 128]
                dq_ref[pl.ds(pl.multiple_of(r * GRID_W, GRID_W), GRID_W), :] = (
                    _unstack_heads(dq2s[t]) * QK_SCALE).astype(BF16)
                dk_ref[pl.ds(off, win), :] += dkws[t]
                dv_ref[pl.ds(off, win), :] += dvws[t]
            return carry

        lax.fori_loop(0, n_rows // NA_GROUP_BWD, group, 0)

        qc = lax.broadcasted_iota(jnp.int32, (N_PAIRS * GRID_W, 128), 0)
        for hh in range(2):
            t = acc_scr[hh].reshape(N_PAIRS * GRID_W, 128)
            for b in range(6):
                t = jnp.where(((qc >> b) & 1) == 1, pltpu.roll(t, 128 - (1 << b), 1), t)
            t = pltpu.roll(t, 15, 1)
            drb_ref[hh] = jnp.sum(t.reshape(N_PAIRS, GRID_W, 128), axis=1)

    col = pl.BlockSpec((n, 128), lambda h: (0, h))
    return _call(
        body, name=name, grid=(NA_WIDTH // 128,),
        in_specs=_na_qkv_specs(n) + [col, pl.BlockSpec((2, 1, RB_WIDTH), lambda h: (h, 0, 0))],
        out_specs=[col, col, col, pl.BlockSpec((2, N_PAIRS, 128), lambda h: (h, 0, 0))],
        out_shape=[_sds((n, NA_WIDTH), BF16), _sds((n, NA_WIDTH), F32), _sds((n, NA_WIDTH), F32),
                   _sds((8, N_PAIRS, 128), F32)],
        scratch_shapes=[pltpu.VMEM((2, N_PAIRS, GRID_W, 128), F32),
                        pltpu.VMEM((2, N_PAIRS, GRID_W, 128), F32)],
        args=(qkv, qkv, qkv, do, rb), after=after)


def _rpb_table(rpb2):
    t = jnp.pad(rpb2, ((0, 0), (0, 1), (0, GRID_W - rpb2.shape[-1])))
    return t.reshape(8, 1, RB_WIDTH)


def _rpb_grad(drb, *, name):
    kdim = drb.shape[1]

    def body(x_ref, o_ref):
        kk = lax.broadcasted_iota(jnp.int32, (128, 512), 0)
        jj = lax.broadcasted_iota(jnp.int32, (128, 512), 1)
        half, co = kk >> 6, kk & 63
        acc = jnp.zeros((8, 512), F32)
        for ro in range(N_PAIRS):
            hit = ((ro + half) == (jj >> 5)) & (co == (jj & 31)) & (co < 31)
            onehot = jnp.where(hit, 1.0, 0.0).astype(F32)
            acc = acc + jnp.dot(x_ref[:, ro * 128:(ro + 1) * 128], onehot, preferred_element_type=F32,
                                precision=lax.Precision.HIGHEST)
        o_ref[...] = acc

    return pl.pallas_call(
        body, name=name, grid=(1,),
        in_specs=[_const((8, kdim))], out_specs=_const((8, 512)), out_shape=_sds((8, 512), F32),
        compiler_params=_params("arbitrary"),
    )(drb)


DIL_GROUP = 2


def _dil_blocks(length):
    qb = min(128, length)
    return qb, min(qb + 2 * DIL_RADIUS, length), min(DIL_GROUP, length // qb)


def _stack_lanes(ref, t, qb, scale=1.0):
    lane = lax.broadcasted_iota(jnp.int32, (qb, 256), 1)
    val = ref[0, t * qb:(t + 1) * qb, :].astype(F32) * scale
    return jnp.concatenate([jnp.where((lane >> 6) == h, val, 0.0) for h in range(4)], axis=0).astype(BF16)


def _dil_window(k_ref, v_ref, blk, qb, win, length):
    start = pl.multiple_of(jnp.clip(blk * qb - DIL_RADIUS, 0, length - win), DIL_RADIUS)
    return k_ref[0, pl.ds(start, win), :], v_ref[0, pl.ds(start, win), :], start


def _dil_caps_init(caps_scr, qb, win):
    @pl.when((pl.program_id(0) == 0) & (pl.program_id(1) == 0))
    def _():
        gap = ((lax.broadcasted_iota(jnp.int32, (4 * qb, win), 0) & (qb - 1))
               - lax.broadcasted_iota(jnp.int32, (4 * qb, win), 1))
        for v in range(3):
            caps_scr[v] = jnp.where(jnp.abs(gap + v * DIL_RADIUS) <= DIL_RADIUS, jnp.inf, NEG_INF)


def _dil_mask(s, blk, start, qb, caps_scr):
    return jnp.minimum(s, caps_scr[(blk * qb - start) // DIL_RADIUS])


def _pick_heads(stacked, qb):
    lane = lax.broadcasted_iota(jnp.int32, (qb, 256), 1)
    out = jnp.zeros((qb, 256), stacked.dtype)
    for h in range(4):
        out = jnp.where((lane >> 6) == h, stacked[h * qb:(h + 1) * qb], out)
    return out


def _stack_head_cols(ref, t, qb):
    return jnp.concatenate([ref[0, t * qb:(t + 1) * qb, 64 * h:64 * h + 1] for h in range(4)], axis=0)


def _dil_fwd(q, k, v, *, name, after=None):
    dil, length, _ = q.shape
    qb, win, grp = _dil_blocks(length)
    extra = [] if after is None else [after]

    def body(q_ref, k_ref, v_ref, *rest):
        o_ref, lse_ref, caps_scr = rest[-3:]
        _dil_caps_init(caps_scr, qb, win)
        blks = [pl.program_id(1) * grp + t for t in range(grp)]
        wins = [_dil_window(k_ref, v_ref, b, qb, win, length) for b in blks]
        raw = [lax.dot_general(_stack_lanes(q_ref, t, qb, QK_SCALE), w[0], NT_DIMS, preferred_element_type=F32)
               for t, w in enumerate(wins)]
        lses, outs = [], []
        for t, (s, w) in enumerate(zip(raw, wins)):
            s = _dil_mask(s, blks[t], w[2], qb, caps_scr)
            m = jnp.max(s, axis=-1, keepdims=True)
            e = jnp.exp(s - m)
            norm = jnp.sum(e, axis=-1, keepdims=True)
            lses.append(m + jnp.log(norm))
            outs.append(jnp.dot((e * (1.0 / norm)).astype(BF16), w[1], preferred_element_type=F32))
        for t in range(grp):
            o_ref[0, t * qb:(t + 1) * qb, :] = _pick_heads(outs[t], qb)
            lse_ref[0, t * qb:(t + 1) * qb, :] = _pick_heads(jnp.broadcast_to(lses[t], (4 * qb, 256)), qb)

    seq = pl.BlockSpec((1, length, 256), lambda j, i: (j, 0, 0))
    blk = pl.BlockSpec((1, grp * qb, 256), lambda j, i: (j, i, 0))
    return pl.pallas_call(
        body, name=name, grid=(dil, length // (grp * qb)),
        in_specs=[blk, seq, seq] + [pl.BlockSpec(memory_space=pl.ANY)] * len(extra), out_specs=[blk, blk],
        out_shape=[_sds((dil, length, 256), F32)] * 2,
        scratch_shapes=[pltpu.VMEM((3, 4 * qb, win), F32)],
        compiler_params=_params("arbitrary", "arbitrary"),
    )(q, k, v, *extra)


def _dil_bwd(q, k, v, do, lse, cc, *, name):
    dil, length, _ = q.shape
    qb, win, grp = _dil_blocks(length)

    def body(q_ref, k_ref, v_ref, do_ref, lse_ref, cc_ref, dq_ref, dk_ref, dv_ref, caps_scr):
        _dil_caps_init(caps_scr, qb, win)

        @pl.when(pl.program_id(1) == 0)
        def _():
            dk_ref[...] = jnp.zeros_like(dk_ref)
            dv_ref[...] = jnp.zeros_like(dv_ref)

        blks = [pl.program_id(1) * grp + t for t in range(grp)]
        wins = [_dil_window(k_ref, v_ref, b, qb, win, length) for b in blks]
        qss = [_stack_lanes(q_ref, t, qb, QK_SCALE) for t in range(grp)]
        doss = [_stack_lanes(do_ref, t, qb) for t in range(grp)]
        raw = [lax.dot_general(qs, w[0], NT_DIMS, preferred_element_type=F32) for qs, w in zip(qss, wins)]
        dps = [lax.dot_general(dos, w[1], NT_DIMS, preferred_element_type=F32) for dos, w in zip(doss, wins)]
        probs = [jnp.exp(_dil_mask(s, blks[t], wins[t][2], qb, caps_scr) - _stack_head_cols(lse_ref, t, qb))
                 for t, s in enumerate(raw)]
        dsbs = [(p * (dp + _stack_head_cols(cc_ref, t, qb))).astype(BF16)
                for t, (p, dp) in enumerate(zip(probs, dps))]
        dq4s = [jnp.dot(dsb, w[0], preferred_element_type=F32) for dsb, w in zip(dsbs, wins)]
        dkws = [lax.dot_general(dsb, qs, TN_DIMS, preferred_element_type=F32) for dsb, qs in zip(dsbs, qss)]
        dvws = [lax.dot_general(p.astype(BF16), dos, TN_DIMS, preferred_element_type=F32)
                for p, dos in zip(probs, doss)]
        for t in range(grp):
            dq_ref[0, t * qb:(t + 1) * qb, :] = _pick_heads(dq4s[t], qb) * QK_SCALE
            dk_ref[0, pl.ds(wins[t][2], win), :] += dkws[t]
            dv_ref[0, pl.ds(wins[t][2], win), :] += dvws[t]

    seq = pl.BlockSpec((1, length, 256), lambda j, i: (j, 0, 0))
    blk = pl.BlockSpec((1, grp * qb, 256), lambda j, i: (j, i, 0))
    return pl.pallas_call(
        body, name=name, grid=(dil, length // (grp * qb)),
        in_specs=[blk, seq, seq, blk, blk, blk], out_specs=[blk, seq, seq],
        out_shape=[_sds((dil, length, 256), F32)] * 3,
        scratch_shapes=[pltpu.VMEM((3, 4 * qb, win), F32)],
        compiler_params=_params("arbitrary", "arbitrary"),
    )(q, k, v, do, lse, cc)


def _merge_weights(lses):
    m = jnp.maximum(jnp.maximum(lses[0], lses[1]), lses[2])
    es = [jnp.exp(t - m) for t in lses]
    inv = 1.0 / (es[0] + es[1] + es[2])
    return [e * inv for e in es]


def _branch_mix(y_na, w_bna, outs, lses, w_bd, gates, *, tm, name):
    n = y_na.shape[0]
    chunk = min(EPILOGUE_ROWS, tm)

    def body(yna_ref, wn_ref, *rest):
        o_in, l_in = rest[0:3], rest[3:6]
        wd_ref, sn_ref, sd_ref = rest[6:9]
        y_ref, yb_ref, bn_ref, bd_ref, mix_ref, scr = rest[9:15]
        for r0 in range(0, tm, chunk):
            rows = slice(r0, r0 + chunk)
            lv = [_load_token_order(l_in[g], scr, d, chunk, r0) for g, d in enumerate(DIL_DILATIONS)]
            ws = _merge_weights(lv)
            y = jnp.zeros((chunk, 256), F32)
            for g, d in enumerate(DIL_DILATIONS):
                y = y + ws[g] * _load_token_order(o_in[g], scr, d, chunk, r0)
            yb = y.astype(BF16)
            y_ref[rows, :] = y
            yb_ref[rows, :] = yb
            bn = lax.dot_general(yna_ref[rows, :], wn_ref[...], NT_DIMS, preferred_element_type=F32).astype(BF16)
            bd = lax.dot_general(yb, wd_ref[...], NT_DIMS, preferred_element_type=F32)
            bn_ref[rows, :] = bn
            bd, mixed = _gate_mix_tile(bd, sn_ref[rows, :], bn, sd_ref[rows, :])
            bd_ref[rows, :] = bd.astype(BF16)
            mix_ref[rows, :] = mixed.astype(BF16)

    specs = [_dil_spec(d, tm) for d in DIL_DILATIONS]
    return pl.pallas_call(
        body, name=name, grid=(n // tm,),
        in_specs=[_rows(tm, NA_WIDTH), _const(w_bna.shape)] + specs + specs
                 + [_const(w_bd.shape), _rows(tm, D_MODEL, 0), _rows(tm, D_MODEL, 1)],
        out_specs=[_rows(tm, 256)] * 2 + [_rows(tm, D_MODEL)] * 3,
        out_shape=[_sds((n, 256), F32), _sds((n, 256), BF16)] + [_sds((n, D_MODEL), BF16)] * 3,
        scratch_shapes=[_dil_scratch(chunk)],
        compiler_params=_params("parallel"),
    )(y_na, w_bna, *outs, *lses, w_bd, gates, gates)


def _branch_bwd(dh, w_out, gates, bn, bd, w_bna, w_bd, y, lses, *, tm, name):
    n = dh.shape[0]
    chunk = min(EPILOGUE_ROWS, tm)

    def body(dh_ref, wo_ref, sn_ref, sd_ref, bn_ref, bd_ref, wn_ref, wd_ref, y_ref, *rest):
        l_in = rest[0:3]
        dbn_ref, dbd_ref, dgn_ref, dgd_ref, dyna_ref = rest[3:8]
        do_out, cc_out, scr = rest[8:11], rest[11:14], rest[14]
        rr = lax.broadcasted_iota(jnp.int32, (256, 256), 0) >> 6
        cc = lax.broadcasted_iota(jnp.int32, (256, 256), 1) >> 6
        ones = jnp.where(rr == cc, 1.0, 0.0).astype(F32)
        for r0 in range(0, tm, chunk):
            rows = slice(r0, r0 + chunk)
            dm = lax.dot_general(dh_ref[rows, :], wo_ref[...], NT_DIMS, preferred_element_type=F32)
            dbn, dbd, dgn, dgd = (t.astype(BF16) for t in _gate_bwd_tile(
                dm, sn_ref[rows, :], bn_ref[rows, :], sd_ref[rows, :], bd_ref[rows, :]))
            dbn_ref[rows, :] = dbn
            dbd_ref[rows, :] = dbd
            dgn_ref[rows, :] = dgn
            dgd_ref[rows, :] = dgd
            dyna_ref[rows, :] = jnp.dot(dbn, wn_ref[...], preferred_element_type=F32).astype(BF16)
            dyv = jnp.dot(dbd, wd_ref[...], preferred_element_type=F32)
            lv = [_load_token_order(l_in[g], scr, d, chunk, r0) for g, d in enumerate(DIL_DILATIONS)]
            ws = _merge_weights(lv)
            tsum = jnp.dot(dyv * y_ref[rows, :], ones, preferred_element_type=F32,
                           precision=lax.Precision.HIGHEST)
            for g, d in enumerate(DIL_DILATIONS):
                _store_dil_order(ws[g] * dyv, do_out[g], scr, d, r0)
                _store_dil_order(-ws[g] * tsum, cc_out[g], scr, d, r0)

    specs = [_dil_spec(d, tm) for d in DIL_DILATIONS]
    wide = _rows(tm, D_MODEL)
    res = pl.pallas_call(
        body, name=name, grid=(n // tm,),
        in_specs=[wide, _const(w_out.shape), _rows(tm, D_MODEL, 0), _rows(tm, D_MODEL, 1), wide, wide,
                  _const(w_bna.shape), _const(w_bd.shape), _rows(tm, 256)] + specs,
        out_specs=[wide] * 4 + [_rows(tm, NA_WIDTH)] + specs + specs,
        out_shape=[_sds((n, D_MODEL), BF16)] * 4 + [_sds((n, NA_WIDTH), BF16)]
                  + [_sds((d, n // d, 256), BF16) for d in DIL_DILATIONS]
                  + [_sds((d, n // d, 256), F32) for d in DIL_DILATIONS],
        scratch_shapes=[_dil_scratch(chunk)],
        compiler_params=_params("parallel"),
    )(dh, w_out, gates, gates, bn, bd, w_bna, w_bd, y, *lses)
    return res[0], res[1], res[2], res[3], res[4], res[5:8], res[8:11]


_WEIGHTS = (("w_in", 1, 736), ("w_branch_na", 1, 128), ("w_branch_dil", 1, 128), ("w_out", 0, 128),
            ("w_up", 1, 512), ("w_down", 0, 512), ("w_ple_gate", 0, 128), ("w_ple_proj", 1, 128))
_W_IN, _W_BNA, _W_BD, _W_OUT, _W_UP, _W_DOWN, _W_PG, _W_PP = range(8)


def _to_full(gathered):
    return gathered.reshape(-1, gathered.shape[2])


def _to_chunks(widx, mat):
    return mat.reshape(N_DEV, _WEIGHTS[widx][2], mat.shape[1])


def _local_step(x, p_bf16, positions, target, g_mix, g_mlp, g_ple, g_final, rpb2,
                get_w_in, relay_rest, get_rest, send_grads):
    tm = 512
    half = HEAD_DIM // 2
    inv_freq = 10000.0 ** (-jnp.arange(half, dtype=F32) / half)
    ang = positions.astype(F32)[:, None] * inv_freq
    cos, sin = jnp.cos(ang), jnp.sin(ang)
    cos_t = jnp.tile(jnp.concatenate([cos, cos], axis=-1), (1, 4))
    sin_t = jnp.tile(jnp.concatenate([-sin, sin], axis=-1), (1, 4))
    rb = _rpb_table(rpb2)

    a = _rms_fwd(x, g_mix, tm=tm, name="rms_mix")
    w_in, token = get_w_in((a, cos_t, sin_t, p_bf16))
    na_qkv, gates, dq_g, dk_g, dv_g = _project_in(a, w_in, cos_t, sin_t, tm=512, name="mm_in", after=token)
    y_na = _na_fwd(na_qkv, rb, name="na_fwd")
    token = relay_rest(y_na)
    d_out, d_lse = [], []
    for g in range(3):
        o, lse = _dil_fwd(dq_g[g], dk_g[g], dv_g[g], name=f"dil_fwd{g}", after=token if g == 0 else None)
        d_out.append(o)
        d_lse.append(lse)
    w_bna, w_bd = get_rest(d_out[2], 0)
    y_dil, y_dil_b, bn, bd, mixed = _branch_mix(y_na, w_bna, d_out, d_lse, w_bd, gates, tm=tm, name="branch_mix")
    w_out, w_up, w_down, w_pg, w_pp = get_rest(mixed, 1)
    h1, c = _matmul(mixed, w_out, out_dtype=(F32, BF16), tm=512, tn=1024, tk=1024, name="mm_out",
                    extra=(x, g_mlp), epilogue=_residual_rms_tile)
    u, f = _matmul(c, w_up, tb=True, out_dtype=(BF16, BF16), tm=512, tn=2048, tk=1024, name="mm_up",
                   epilogue=lambda acc: (acc, jnp.square(jnp.maximum(acc, 0.0))))

    e, dpp, dgt, dh2, dh2_b, dg_final, loss, dg_ple = _tail_step(
        f, w_down, h1, w_pg, p_bf16, w_pp, target, g_final, g_ple, tm=256, name="tail_step")
    loss = loss[:, :128]
    gw_pp = _matmul(p_bf16, dpp, ta=True, transpose_out=True, out_dtype=BF16, tm=256, tn=1024, tk=2048,
                    name="mm_gw_pp")
    gw_pg = _matmul(e, dgt, ta=True, out_dtype=BF16, tm=512, tn=1024, tk=2048, name="mm_gw_pg")
    du = _matmul(dh2_b, w_down, tb=True, out_dtype=BF16, tm=512, tn=2048, tk=1024, name="mm_du",
                 extra=(u,), epilogue=lambda acc, uv: (acc * (2.0 * jnp.maximum(uv.astype(F32), 0.0)),))
    gw_down = _matmul(f, dh2_b, ta=True, out_dtype=BF16, tm=1024, tn=1024, tk=2048, name="mm_gw_down")
    gw_up = _matmul(c, du, ta=True, transpose_out=True, out_dtype=BF16, tm=512, tn=2048, tk=2048, name="mm_gw_up")
    dh1, dh1_b, dg_mlp = _matmul(
        du, w_up, out_dtype=(F32, BF16), tm=512, tn=1024, tk=4096, name="mm_dc",
        extra=(h1, g_mlp, dh2), epilogue=_rms_bwd_twice, n_colsum=1)
    dbn, dbd, dgn, dgd, dy_na, do_g, cc_g = _branch_bwd(dh1_b, w_out, gates, bn, bd, w_bna, w_bd, y_dil, d_lse,
                                                        tm=tm, name="branch_bwd")
    gw_out = _matmul(mixed, dh1_b, ta=True, out_dtype=BF16, tm=512, tn=1024, tk=2048, name="mm_gw_out")
    gw_bna = _matmul(y_na, dbn, ta=True, transpose_out=True, out_dtype=BF16, tm=512, tn=1024, tk=2048,
                     name="mm_gw_bna")
    gw_bd = _matmul(y_dil_b, dbd, ta=True, transpose_out=True, out_dtype=BF16, tm=256, tn=1024, tk=2048,
                    name="mm_gw_bd")
    token = send_grads((_W_PP, _W_PG, _W_DOWN, _W_UP, _W_OUT, _W_BNA, _W_BD),
                       (gw_pp, gw_pg, gw_down, gw_up, gw_out, gw_bna, gw_bd))
    dna = _na_bwd(na_qkv, dy_na, rb, name="na_bwd", after=token)
    drpb = _rpb_grad(dna[3].reshape(8, -1), name="rpb_grad")
    ddq, ddk, ddv = [], [], []
    for g in range(3):
        r = _dil_bwd(dq_g[g], dk_g[g], dv_g[g], do_g[g], d_lse[g], cc_g[g], name=f"dil_bwd{g}")
        ddq.append(r[0])
        ddk.append(r[1])
        ddv.append(r[2])
    dproj = _assemble_dproj(dna[0:3], ddq, ddk, ddv, dgn, dgd, cos_t, sin_t, tm=tm, name="assemble_dproj")
    gw_in = _matmul(a, dproj, ta=True, transpose_out=True, out_dtype=BF16, tm=512, tn=2944, tk=2048, name="mm_gw_in")
    token = send_grads((_W_IN,), (gw_in,))
    dx, dg_mix = _matmul(
        dproj, w_in, out_dtype=(F32,), tm=512, tn=1024, tk=5888, name="mm_da", after=token,
        extra=(x, g_mix, dh1), epilogue=_rms_bwd_tile, n_colsum=1)
    return loss, dx, (dg_mix, dg_mlp, dg_ple, dg_final), drpb


def _cast_bf16(t, *, name):
    def body(t_ref, o_ref):
        o_ref[...] = t_ref[...].astype(BF16)

    rows, cols = t.shape
    tr = 256 if rows % 256 == 0 else rows
    blk = pl.BlockSpec((tr, cols), lambda i: (i, 0))
    return pl.pallas_call(body, name=name, grid=(rows // tr,), in_specs=[blk], out_specs=blk,
                          out_shape=_sds(t.shape, BF16), compiler_params=_params("parallel"))(t)


def _adamw(w, g, m, v):
    m = ADAM_B1 * m + (1.0 - ADAM_B1) * g
    v = ADAM_B2 * v + (1.0 - ADAM_B2) * (g * g)
    m_hat = m / (1.0 - ADAM_B1 ** ADAM_STEP)
    v_hat = v / (1.0 - ADAM_B2 ** ADAM_STEP)
    delta = -ADAM_LR * (m_hat / (jnp.sqrt(v_hat) + ADAM_EPS) + ADAM_WD * w)
    return delta, m, v


def _sum_adamw(parts, w, m, v, *, tr, name, own=None, transposed=False):
    rows, cols = w.shape
    n_pre = 0 if own is None else 1

    def body(*refs):
        p_ref, w_ref, m_ref, v_ref = refs[n_pre:n_pre + 4]
        g_ref, d_ref, nm_ref, nv_ref = refs[-4:]
        g = (p_ref[0] if own is None else refs[n_pre + 4][...]).astype(F32)
        for s in range(1, N_DEV):
            g = g + p_ref[s].astype(F32)
        if transposed:
            g = g.T
        g_ref[...] = g
        d_ref[...], nm_ref[...], nv_ref[...] = _adamw(w_ref[...], g, m_ref[...], v_ref[...])

    if transposed:
        blk = pl.BlockSpec((rows, tr), lambda i, *_: (0, i))
        g_rows, steps = rows, cols // tr
    else:
        blk = pl.BlockSpec((tr, cols), lambda i, *_: (i, 0))
        g_rows, steps = cols, rows // tr
    in_specs = [pl.BlockSpec((N_DEV, tr, g_rows), lambda i, *_: (0, i, 0)), blk, blk, blk]
    args = [parts, w, m, v]
    if own is not None:
        in_specs.append(pl.BlockSpec((None, tr, g_rows), lambda i, idx: (idx[0], i, 0)))
        args = [own[1]] + args + [own[0]]
    return pl.pallas_call(
        body, name=name,
        grid_spec=pltpu.PrefetchScalarGridSpec(num_scalar_prefetch=n_pre, grid=(steps,), in_specs=in_specs,
                                               out_specs=[blk] * 4),
        out_shape=[_sds((rows, cols), F32)] * 4,
        compiler_params=_params("parallel"),
    )(*args)


_RPB_SIZE = 8 * 15 * 31


def _pack_small(g_mix, g_mlp, g_ple, g_final, rpb, loss_row):
    flat = jnp.concatenate([g_mix.reshape(-1), g_mlp.reshape(-1), g_ple.reshape(-1), g_final.reshape(-1),
                            rpb.reshape(-1), jnp.zeros((3840 - _RPB_SIZE,), F32), loss_row.reshape(-1),
                            jnp.zeros((128,), F32)])
    return flat.reshape(64, 128)


def _unpack_small(t):
    flat = t.reshape(-1)
    return (flat[0:1024].reshape(1, 1024), flat[4096:4096 + _RPB_SIZE].reshape(1, 8, 15, 31),
            flat[1024:2048].reshape(1, 1024), flat[2048:3072].reshape(1, 1024), flat[3072:4096])


def kernel(x, p, positions, g_mix, w_in, rpb, w_branch_na, w_branch_dil, w_out, g_mlp, w_up, w_down, g_ple, w_ple_gate, w_ple_proj, g_final, loss_target, m_g_mix, m_w_in, m_rpb, m_w_branch_na, m_w_branch_dil, m_w_out, m_g_mlp, m_w_up, m_w_down, m_g_ple, m_w_ple_gate, m_w_ple_proj, m_g_final, v_g_mix, v_w_in, v_rpb, v_w_branch_na, v_w_branch_dil, v_w_out, v_g_mlp, v_w_up, v_w_down, v_g_ple, v_w_ple_gate, v_w_ple_proj, v_g_final):
    sharded = dict(w_in=(w_in, m_w_in, v_w_in), w_branch_na=(w_branch_na, m_w_branch_na, v_w_branch_na),
                   w_branch_dil=(w_branch_dil, m_w_branch_dil, v_w_branch_dil), w_out=(w_out, m_w_out, v_w_out),
                   w_up=(w_up, m_w_up, v_w_up), w_down=(w_down, m_w_down, v_w_down),
                   w_ple_gate=(w_ple_gate, m_w_ple_gate, v_w_ple_gate),
                   w_ple_proj=(w_ple_proj, m_w_ple_proj, v_w_ple_proj))
    shards = {k: tuple(t[0] for t in val) for k, val in sharded.items()}

    me = _my_index()

    shards["w_in"] = tuple(t.T for t in shards["w_in"])

    w_in_b = _cast_bf16(shards["w_in"][0], name="cast_w_in")
    rest_b = [shards[name][0].astype(BF16).T if axis == 1 else shards[name][0].astype(BF16)
              for name, axis, _ in _WEIGHTS[1:]]
    first_in, token_in = _start_copies(_first_leg_copies, [w_in_b], [_sds((N_DEV,) + w_in_b.shape, BF16)], 4,
                                       name="start_gather_w_in")

    def whole(landed, mine):
        return _to_full(lax.dynamic_update_index_in_dim(landed, mine, me, 0))

    rest = {}

    def get_w_in(after):
        (mine,), landed = _wait_copies(_first_leg_copies, first_in, (*after, *rest_b), name="wait_gather_w_in")
        second, token = _start_copies(_second_leg_copies, [], landed, 3, name="start_forward_w_in")
        _, (landed,) = _wait_copies(_second_leg_copies, second, token, name="wait_forward_w_in")
        rest["first"], token = _start_copies(_first_leg_copies, rest_b,
                                             [_sds((N_DEV,) + t.shape, BF16) for t in rest_b], 4 * len(rest_b),
                                             name="start_gather_rest", after=landed)
        return whole(landed, mine), token

    def relay_rest(after):
        rest["mine"], landed = _wait_copies(_first_leg_copies, rest["first"], after, name="wait_gather_rest")
        rest["second"], token = _start_copies(_second_leg_copies, [], landed, 3 * len(rest_b),
                                              name="start_forward_rest")
        return token

    def get_rest(after, stage):
        n_src, send_sems, recv_sems, bufs = rest["second"]
        part = slice(0, 2) if stage == 0 else slice(2, len(rest_b))
        _, landed = _wait_copies(functools.partial(_second_leg_copies, first=part.start),
                                 (n_src, send_sems, recv_sems, bufs[part]), after,
                                 name=f"wait_forward_rest{stage}")
        return [whole(t, own) for t, own in zip(landed, rest["mine"][part])]

    sent = []

    def send_grads(indices, grads):
        chunked = [_to_chunks(i, g) for i, g in zip(indices, grads)]
        handle, token = _start_copies(_exchange_copies, chunked, [_sds(t.shape, BF16) for t in chunked],
                                      7 * len(chunked),
                                      name="start_exchange_" + ("w_in" if indices == (_W_IN,) else "rest"))
        sent.append((indices, handle))
        return token

    g_mix_0 = g_mix + token_in[0:1, 0:1]
    loss, dx, dgs, drpb = _local_step(
        x[0], p[0, 0].astype(BF16), positions[0], loss_target[0],
        g_mix_0, g_mlp, g_ple, g_final.reshape(1, -1), rpb[0], get_w_in, relay_rest, get_rest, send_grads)

    drpb3 = drpb.reshape(8, 16, 32)[:, :15, :31]
    small = _pack_small(dgs[0], dgs[1], dgs[2], dgs[3], drpb3, loss)
    share, done = _start_copies(_gather_copies, [small], [_sds((N_DEV,) + small.shape, F32)], 7,
                                name="start_share_small")

    out = {}
    for indices, handle in sent:
        chunked, landed = _wait_copies(_exchange_copies, handle, done,
                                       name="wait_exchange_" + ("w_in" if indices == (_W_IN,) else "rest"))
        for i, part, mine in zip(indices, landed, chunked):
            name = _WEIGHTS[i][0]
            w, m, v = shards[name]
            turned = _WEIGHTS[i][1] == 1 and i != _W_IN
            res = _sum_adamw(part, w, m, v, tr=368 if i == _W_IN else 128, name="adamw_" + name,
                             own=(mine, me.reshape(1).astype(jnp.int32)), transposed=turned)
            out[name] = [(t.T if i == _W_IN else t)[None] for t in res]
            done = res[0]
    (small,), (small_landed,) = _wait_copies(_gather_copies, share, done, name="wait_share_small")
    small_all = lax.dynamic_update_index_in_dim(small_landed, small, me, 0)
    small_w = _pack_small(g_mix, g_mlp, g_ple, g_final, rpb, jnp.zeros((128,), F32))
    small_m = _pack_small(m_g_mix, m_g_mlp, m_g_ple, m_g_final, m_rpb, jnp.zeros((128,), F32))
    small_v = _pack_small(v_g_mix, v_g_mlp, v_g_ple, v_g_final, v_rpb, jnp.zeros((128,), F32))
    res = _sum_adamw(small_all, small_w, small_m, small_v, tr=64, name="adamw_small")
    unpacked = [_unpack_small(t) for t in res]
    for i, name in enumerate(("g_mix", "rpb", "g_mlp", "g_ple", "g_final")):
        out[name] = [u[i] for u in unpacked]
    loss_total = res[0][62, 0]

    order = ("g_mix", "w_in", "rpb", "w_branch_na", "w_branch_dil", "w_out", "g_mlp", "w_up", "w_down",
             "g_ple", "w_ple_gate", "w_ple_proj", "g_final")
    grads = [out[k][0] for k in order]
    deltas = [out[k][1] for k in order]
    new_m = [out[k][2] for k in order]
    new_v = [out[k][3] for k in order]
    return (loss_total, dx[None], *grads, *deltas, *new_m, *new_v)
```

```python
import functools

import jax
import jax.numpy as jnp
from jax import lax
from jax.experimental import pallas as pl
from jax.experimental.pallas import tpu as pltpu

F32 = jnp.float32
BF16 = jnp.bfloat16

D_MODEL = 1024
HEAD_DIM = 64
GRID_W = 64
NA_WIDTH = 512
DIL_WIDTH = 768
IN_WIDTH = 5888
DIL_DILATIONS = (1, 4, 16)
DIL_RADIUS = 64
NA_WIN_ROWS = 8
RMS_EPS = 1e-6
NEG_INF = -1e30
QK_SCALE = HEAD_DIM ** -0.5

ADAM_LR = 0.001
ADAM_B1 = 0.9
ADAM_B2 = 0.999
ADAM_EPS = 1e-08
ADAM_WD = 0.01
ADAM_STEP = 10

N_DEV = 8
VMEM_LIMIT = 56 * 1024 * 1024
EPILOGUE_ROWS = 256
MESH = pl.DeviceIdType.MESH

NT_DIMS = (((1,), (1,)), ((), ()))
TN_DIMS = (((0,), (0,)), ((), ()))


def _sds(shape, dtype):
    return jax.ShapeDtypeStruct(shape, dtype)


def _params(*sem):
    return pltpu.CompilerParams(dimension_semantics=sem, vmem_limit_bytes=VMEM_LIMIT)


def _rows(tm, width, col=0):
    return pl.BlockSpec((tm, width), lambda i, c=col: (i, c))


def _const(shape):
    zeros = (0,) * len(shape)
    return pl.BlockSpec(shape, lambda i: zeros)


def _resident(shape):
    zeros = (0,) * len(shape)
    return pl.BlockSpec(shape, lambda i: zeros, pipeline_mode=pl.Buffered(1))


def _my_index():
    return 4 * lax.axis_index("x") + 2 * lax.axis_index("y") + lax.axis_index("c")


def _peer(k):
    x, y, c = lax.axis_index("x"), lax.axis_index("y"), lax.axis_index("c")
    px = 1 - x if k & 4 else x
    py = 1 - y if k & 2 else y
    pc = 1 - c if k & 1 else c
    return (px, py, pc), 4 * px + 2 * py + pc


def _call(body, *, name, grid, in_specs, out_specs, out_shape, scratch_shapes, args, after=None):
    n_in, n_out = len(in_specs), len(out_specs)
    extra = [] if after is None else [after]
    n_x = n_in + len(extra)

    def plain(*refs):
        body(refs[:n_in], refs[n_x:n_x + n_out], refs[n_x + n_out:])

    res = pl.pallas_call(plain, name=name, grid=grid,
                         in_specs=list(in_specs) + [pl.BlockSpec(memory_space=pl.ANY)] * len(extra),
                         out_specs=out_specs, out_shape=out_shape, scratch_shapes=scratch_shapes,
                         compiler_params=_params(*(("arbitrary",) * len(grid))))(*args, *extra)
    return list(res)


_HBM_SPEC = pl.BlockSpec(memory_space=pltpu.HBM)
_SEM_SPEC = pl.BlockSpec(memory_space=pltpu.SEMAPHORE)
_SIDE_EFFECT = pltpu.SideEffectType.DATAFLOW_SIDE_EFFECTING


_FIRST_LEG = (1, 2, 4, 6)
_SECOND_LEG = (2, 4, 6)


def _gather_copies(srcs, lands, send, recv, sending):
    me = _my_index()
    out = []
    for w in range(len(srcs)):
        for k in range(1, N_DEV):
            dev, idx = _peer(k)
            out.append(pltpu.make_async_remote_copy(
                src_ref=srcs[w], dst_ref=lands[w].at[me if sending else idx],
                send_sem=send.at[w * 7 + k - 1], recv_sem=recv.at[w * 7 + k - 1],
                device_id=dev, device_id_type=MESH))
    return out


def _first_leg_copies(srcs, lands, send, recv, sending):
    me = _my_index()
    out = []
    for w in range(len(srcs)):
        for j, k in enumerate(_FIRST_LEG):
            dev, idx = _peer(k)
            out.append(pltpu.make_async_remote_copy(
                src_ref=srcs[w], dst_ref=lands[w].at[me if sending else idx],
                send_sem=send.at[w * 4 + j], recv_sem=recv.at[w * 4 + j],
                device_id=dev, device_id_type=MESH))
    return out


def _second_leg_copies(srcs, lands, send, recv, sending, first=0):
    sibling, _ = _peer(1)
    out = []
    for w in range(len(lands)):
        for j, k in enumerate(_SECOND_LEG):
            slot = _peer(k if sending else k ^ 1)[1]
            sem = (first + w) * 3 + j
            out.append(pltpu.make_async_remote_copy(
                src_ref=lands[w].at[slot], dst_ref=lands[w].at[slot],
                send_sem=send.at[sem], recv_sem=recv.at[sem],
                device_id=sibling, device_id_type=MESH))
    return out


def _exchange_copies(srcs, lands, send, recv, sending):
    out = []
    for w in range(len(srcs)):
        for k in range(1, N_DEV):
            dev, idx = _peer(k)
            out.append(pltpu.make_async_remote_copy(
                src_ref=srcs[w].at[idx], dst_ref=lands[w].at[k],
                send_sem=send.at[w * 7 + k - 1], recv_sem=recv.at[w * 7 + k - 1],
                device_id=dev, device_id_type=MESH))
    return out


def _start_copies(make, srcs, lands, n_copies, *, name, after=None):
    n_src, n_buf = len(srcs), len(srcs) + len(lands)
    extra = [] if after is None else [after]

    def body(*refs):
        send, recv = refs[n_buf + len(extra)], refs[n_buf + len(extra) + 1]
        for cp in make(refs[:n_src], refs[n_src:n_buf], send, recv, True):
            cp.start()
        refs[-1][...] = jnp.zeros_like(refs[-1])

    bufs = list(srcs) + [lax.empty(t.shape, t.dtype) if isinstance(t, jax.ShapeDtypeStruct) else t for t in lands]
    res = pl.pallas_call(
        body, name=name,
        out_shape=(pltpu.SemaphoreType.DMA((n_copies,)), pltpu.SemaphoreType.DMA((n_copies,)),
                   *[pltpu.HBM(t.shape, t.dtype) for t in bufs], _sds((8, 128), F32)),
        in_specs=[_HBM_SPEC] * n_buf + [pl.BlockSpec(memory_space=pl.ANY)] * len(extra),
        out_specs=(_SEM_SPEC, _SEM_SPEC, *([_HBM_SPEC] * n_buf), pl.BlockSpec(memory_space=pltpu.VMEM)),
        input_output_aliases={i: 2 + i for i in range(n_buf)},
        compiler_params=pltpu.CompilerParams(has_side_effects=_SIDE_EFFECT),
    )(*[pltpu.with_memory_space_constraint(t, pltpu.HBM) for t in bufs], *extra)
    return (n_src, res[0], res[1], res[2:2 + n_buf]), res[-1]


def _wait_copies(make, handle, after, *, name):
    n_src, send_sems, recv_sems, bufs = handle
    n_buf = len(bufs)
    after = list(after) if isinstance(after, (tuple, list)) else [after]

    def body(*refs):
        for cp in make(refs[:n_src], refs[n_src:n_buf], refs[n_buf], refs[n_buf + 1], False):
            cp.wait_send()
            cp.wait_recv()

    res = pl.pallas_call(
        body, name=name,
        out_shape=tuple(pltpu.HBM(t.shape, t.dtype) for t in bufs),
        in_specs=[_HBM_SPEC] * n_buf + [_SEM_SPEC, _SEM_SPEC] + [pl.BlockSpec(memory_space=pl.ANY)] * len(after),
        out_specs=tuple([_HBM_SPEC] * n_buf),
        input_output_aliases={i: i for i in range(n_buf)},
        compiler_params=pltpu.CompilerParams(has_side_effects=_SIDE_EFFECT),
    )(*bufs, send_sems, recv_sems, *after)
    return list(res[:n_src]), list(res[n_src:])


def _add_colsums(s_refs, sums, step):
    for s_ref, val in zip(s_refs, sums):
        @pl.when(step == 0)
        def _(s_ref=s_ref, val=val):
            s_ref[...] = val

        @pl.when(step > 0)
        def _(s_ref=s_ref, val=val):
            s_ref[...] += val


def _matmul(a, b, *, ta=False, tb=False, out_dtype, tm, tn, tk, name, after=None, extra=(), epilogue=None,
            n_colsum=0, transpose_out=False):
    m, k = (a.shape[1], a.shape[0]) if ta else a.shape
    n = b.shape[0] if tb else b.shape[1]
    tm, tn, tk = min(tm, m), min(tn, n), min(tk, k)
    nk = k // tk
    dims = (((0 if ta else 1,), (1 if tb else 0,)), ((), ()))
    out_dtypes = out_dtype if isinstance(out_dtype, tuple) else (out_dtype,)
    n_tiles = len(out_dtypes)

    def add_colsums(o_refs, sums):
        _add_colsums(o_refs[n_tiles:], sums, pl.program_id(1))

    def finish(acc, x_refs, o_refs):
        vals = (acc,) if epilogue is None else epilogue(acc, *[r[...] for r in x_refs])
        for o_ref, val in zip(o_refs[:n_tiles], vals[:n_tiles]):
            o_ref[...] = (val.T if transpose_out else val).astype(o_ref.dtype)
        add_colsums(o_refs, vals[n_tiles:])

    chunk = EPILOGUE_ROWS if (nk == 1 and epilogue is not None and not ta and tm % EPILOGUE_ROWS == 0) else None

    def body(ins, outs, acc):
        a_ref, b_ref = ins[:2]
        if chunk is not None:
            sums = None
            for r0 in range(0, tm, chunk):
                part = lax.dot_general(a_ref[r0:r0 + chunk, :], b_ref[...], dims, preferred_element_type=F32)
                vals = epilogue(part, *[r[...] if r.shape[0] == 1 else r[r0:r0 + chunk, :] for r in ins[2:]])
                for o_ref, val in zip(outs[:n_tiles], vals[:n_tiles]):
                    o_ref[r0:r0 + chunk, :] = val.astype(o_ref.dtype)
                sums = vals[n_tiles:] if sums is None else [s + v for s, v in zip(sums, vals[n_tiles:])]
            add_colsums(outs, sums)
            return
        part = lax.dot_general(a_ref[...], b_ref[...], dims, preferred_element_type=F32)
        if nk == 1:
            finish(part, ins[2:], outs)
            return
        acc_ref, = acc
        kk = pl.program_id(2)

        @pl.when(kk == 0)
        def _():
            acc_ref[...] = part

        @pl.when(kk > 0)
        def _():
            acc_ref[...] += part

        @pl.when(kk == nk - 1)
        def _():
            finish(acc_ref[...], ins[2:], outs)

    a_spec = (pl.BlockSpec((tk, tm), lambda j, i, kk: (kk, i)) if ta
              else pl.BlockSpec((tm, tk), lambda j, i, kk: (i, kk)))
    b_spec = (pl.BlockSpec((tn, tk), lambda j, i, kk: (j, kk)) if tb
              else pl.BlockSpec((tk, tn), lambda j, i, kk: (kk, j)))
    tile = pl.BlockSpec((tm, tn), lambda j, i, kk: (i, j))
    row = pl.BlockSpec((1, tn), lambda j, i, kk: (0, j))

    out_tile, out_dims = (pl.BlockSpec((tn, tm), lambda j, i, kk: (j, i)), (n, m)) if transpose_out else (tile, (m, n))
    res = _call(
        body, name=name, grid=(n // tn, m // tm, nk),
        in_specs=[a_spec, b_spec] + [row if t.shape[0] == 1 else tile for t in extra],
        out_specs=[out_tile] * n_tiles + [row] * n_colsum,
        out_shape=[_sds(out_dims, dt) for dt in out_dtypes] + [_sds((1, n), F32)] * n_colsum,
        scratch_shapes=[] if nk == 1 else [pltpu.VMEM((tm, tn), F32)],
        args=(a, b, *extra), after=after)
    return res if isinstance(out_dtype, tuple) or n_colsum else res[0]


def _rstd(h):
    return lax.rsqrt(jnp.mean(h * h, axis=-1, keepdims=True) + RMS_EPS)


def _sigmoid(z):
    return 1.0 / (1.0 + jnp.exp(-z))


def _rms_fwd(x, g, *, tm, name):
    n = x.shape[0]

    def body(x_ref, g_ref, o_ref):
        h = x_ref[...]
        o_ref[...] = (h * _rstd(h) * g_ref[...]).astype(BF16)

    return pl.pallas_call(
        body, name=name, grid=(n // tm,),
        in_specs=[_rows(tm, D_MODEL), _const((1, D_MODEL))],
        out_specs=_rows(tm, D_MODEL), out_shape=_sds((n, D_MODEL), BF16),
        compiler_params=_params("parallel"),
    )(x, g)


def _swap_halves(t):
    lane = lax.broadcasted_iota(jnp.int32, (t.shape[0], 128), 1)
    pieces = [t[:, c:c + 128] for c in range(0, t.shape[1], 128)]
    return jnp.concatenate([jnp.where((lane & 63) < 32, pltpu.roll(h, 96, 1), pltpu.roll(h, 32, 1))
                            for h in pieces], axis=1)


def _dil_spec(dil, tm):
    return pl.BlockSpec((dil, tm // dil, 256), lambda i: (0, i, 0))


def _dil_scratch(tm):
    return pltpu.VMEM((2, tm, 128), F32)


def _load_token_order(src, scr, dil, rows, row0=0):
    if dil == 1:
        return src[0, row0:row0 + rows, :]
    for j in range(dil):
        for c in range(2):
            scr[c, pl.ds(j, rows // dil, stride=dil), :] = (
                src[j, row0 // dil:(row0 + rows) // dil, c * 128:(c + 1) * 128])
    return jnp.concatenate([scr[0, 0:rows, :], scr[1, 0:rows, :]], axis=1)


def _store_dil_order(val, dst, scr, dil, row0=0):
    rows = val.shape[0]
    if dil == 1:
        dst[0, row0:row0 + rows, :] = val.astype(dst.dtype)
        return
    for c in range(2):
        scr[c] = val[:, c * 128:(c + 1) * 128]
    for j in range(dil):
        for c in range(2):
            dst[j, row0 // dil:(row0 + rows) // dil, c * 128:(c + 1) * 128] = (
                scr[c, pl.ds(j, rows // dil, stride=dil), :].astype(dst.dtype))


def _project_in(a, w_t, cos_t, sin_t, *, tm, name, after=None):
    n = a.shape[0]
    n_dil = len(DIL_DILATIONS)
    na_w, dil_w = 3 * NA_WIDTH, 3 * DIL_WIDTH
    chunk = min(EPILOGUE_ROWS, tm)
    extra = [] if after is None else [after]

    def body(a_ref, w_ref, cos_ref, sin_ref, *rest):
        na_ref, gate_ref = rest[len(extra):len(extra) + 2]
        outs, scr = rest[len(extra) + 2:len(extra) + 2 + 3 * n_dil], rest[-1]

        def part(r0, first, width):
            return lax.dot_general(a_ref[r0:r0 + chunk, :], w_ref[first:first + width, :], NT_DIMS,
                                   preferred_element_type=F32)

        for r0 in range(0, tm, chunk):
            na_ref[r0:r0 + chunk, :] = part(r0, 0, na_w).astype(BF16)
            dil_part = part(r0, na_w, dil_w)
            cosv, sinv = cos_ref[r0:r0 + chunk, :], sin_ref[r0:r0 + chunk, :]
            for t in range(3):
                for gi, dil in enumerate(DIL_DILATIONS):
                    c0 = (t * n_dil + gi) * 256
                    val = dil_part[:, c0:c0 + 256]
                    if t < 2:
                        val = val * cosv + _swap_halves(val) * sinv
                    _store_dil_order(val, outs[t * n_dil + gi], scr, dil, r0)
            gate_ref[r0:r0 + chunk, :] = _sigmoid(part(r0, na_w + dil_w, 2 * D_MODEL)).astype(BF16)

    out_specs = [_rows(tm, na_w), _rows(tm, 2 * D_MODEL)]
    out_shape = [_sds((n, na_w), BF16), _sds((n, 2 * D_MODEL), BF16)]
    for _ in range(3):
        for dil in DIL_DILATIONS:
            out_specs.append(pl.BlockSpec((dil, tm // dil, 256), lambda i: (0, i, 0)))
            out_shape.append(_sds((dil, n // dil, 256), BF16))
    res = pl.pallas_call(
        body, name=name, grid=(n // tm,),
        in_specs=[_rows(tm, D_MODEL), _const(w_t.shape), _rows(tm, 256), _rows(tm, 256)]
                 + [pl.BlockSpec(memory_space=pl.ANY)] * len(extra),
        out_specs=out_specs, out_shape=out_shape,
        scratch_shapes=[pltpu.VMEM((2, chunk, 128), F32)],
        compiler_params=_params("parallel"),
    )(a, w_t, cos_t, sin_t, *extra)
    return res[0], res[1], res[2:5], res[5:8], res[8:11]


def _residual_rms_tile(delta, h, g):
    hn = h + delta
    return hn, hn * _rstd(hn) * g


def _gate_mix_tile(b2, s1, b1, s2):
    return b2, s1.astype(F32) * b1.astype(F32) + s2.astype(F32) * b2


def _gate_bwd_tile(dm, s1, b1, s2, b2):
    s1, b1, s2, b2 = (t.astype(F32) for t in (s1, b1, s2, b2))
    return dm * s1, dm * s2, dm * b1 * s1 * (1.0 - s1), dm * b2 * s2 * (1.0 - s2)


def _tail_tile(gt, pp, h2, target, g):
    sg = _sigmoid(gt)
    h3 = h2 + sg * pp
    r3 = _rstd(h3)
    n3 = h3 * r3
    err = n3 * g - target
    loss = 0.5 * jnp.sum(jnp.sum(err * err, axis=-1, keepdims=True) / D_MODEL)
    dy = err / D_MODEL
    dn = dy * g
    dh3 = r3 * (dn - n3 * jnp.mean(dn * n3, axis=-1, keepdims=True))
    return (dh3, dh3 * sg, dh3 * pp * sg * (1.0 - sg),
            jnp.sum(dy * n3, axis=0, keepdims=True), jnp.full((1, gt.shape[1]), loss, F32))


def _rms_bwd_tile(dz, h, g, dres):
    r = _rstd(h)
    nrm = h * r
    dn = dz * g
    dh = dres + r * (dn - nrm * jnp.mean(dn * nrm, axis=-1, keepdims=True))
    return dh, jnp.sum(dz * nrm, axis=0, keepdims=True)


def _rms_bwd_twice(dz, h, g, dres):
    dh, dg = _rms_bwd_tile(dz, h, g, dres)
    return dh, dh, dg


def _tail_step(f, w_down, h1, w_pg, p, w_pp, target, g_final, g_ple, *, tm, name):
    n = f.shape[0]
    chunk = min(EPILOGUE_ROWS, tm)

    def body(f_ref, wd_ref, h1_ref, wg_ref, p_ref, wp_ref, t_ref, gf_ref, gp_ref,
             e_ref, dpp_ref, dgt_ref, dh2_ref, dh2b_ref, dgf_ref, loss_ref, dgp_ref):
        sums = None
        for r0 in range(0, tm, chunk):
            rows = slice(r0, r0 + chunk)
            delta = jnp.dot(f_ref[rows, :], wd_ref[...], preferred_element_type=F32)
            h2, e = _residual_rms_tile(delta, h1_ref[rows, :], gp_ref[...])
            e = e.astype(BF16)
            e_ref[rows, :] = e
            gt = jnp.dot(e, wg_ref[...], preferred_element_type=F32)
            pp = lax.dot_general(p_ref[rows, :], wp_ref[...], NT_DIMS, preferred_element_type=F32)
            dh3, dpp, dgt, dgf, loss = _tail_tile(gt, pp, h2, t_ref[rows, :], gf_ref[...])
            dgt = dgt.astype(BF16)
            dpp_ref[rows, :] = dpp.astype(BF16)
            dgt_ref[rows, :] = dgt
            dz = lax.dot_general(dgt, wg_ref[...], NT_DIMS, preferred_element_type=F32)
            dh2, dgp = _rms_bwd_tile(dz, h2, gp_ref[...], dh3)
            dh2_ref[rows, :] = dh2
            dh2b_ref[rows, :] = dh2.astype(BF16)
            vals = (dgf, loss, dgp)
            sums = vals if sums is None else [s + v for s, v in zip(sums, vals)]
        _add_colsums((dgf_ref, loss_ref, dgp_ref), sums, pl.program_id(0))

    wide, gain = _rows(tm, D_MODEL), _const((1, D_MODEL))
    return pl.pallas_call(
        body, name=name, grid=(n // tm,),
        in_specs=[_rows(tm, f.shape[1]), _resident(w_down.shape), wide, _resident(w_pg.shape),
                  _rows(tm, p.shape[1]), _resident(w_pp.shape), wide, gain, gain],
        out_specs=[wide] * 5 + [gain] * 3,
        out_shape=[_sds((n, D_MODEL), dt) for dt in (BF16, BF16, BF16, F32, BF16)]
                  + [_sds((1, D_MODEL), F32)] * 3,
        compiler_params=_params("arbitrary"),
    )(f, w_down, h1, w_pg, p, w_pp, target, g_final, g_ple)


def _assemble_dproj(dna, ddil_q, ddil_k, ddil_v, dgn, dgd, cos_t, sin_t, *, tm, name):
    n = dgn.shape[0]

    def body(*refs):
        dq_ref, dk_ref, dv_ref = refs[0:3]
        dil_in = refs[3:12]
        dgn_ref, dgd_ref, cos_ref, sin_ref, o_ref, scr = refs[12:18]
        o_ref[:, 0:512] = dq_ref[...]
        o_ref[:, 512:1024] = dk_ref[...].astype(BF16)
        o_ref[:, 1024:1536] = dv_ref[...].astype(BF16)
        cosv, sinv = cos_ref[...], sin_ref[...]
        for t in range(3):
            for gi, dil in enumerate(DIL_DILATIONS):
                val = _load_token_order(dil_in[t * 3 + gi], scr, dil, tm)
                if t < 2:
                    val = val * cosv + _swap_halves(val * sinv)
                c0 = 1536 + t * DIL_WIDTH + gi * 256
                o_ref[:, c0:c0 + 256] = val.astype(BF16)
        o_ref[:, 3840:4864] = dgn_ref[...]
        o_ref[:, 4864:5888] = dgd_ref[...]

    in_specs = [_rows(tm, NA_WIDTH)] * 3
    for _ in range(3):
        for dil in DIL_DILATIONS:
            in_specs.append(pl.BlockSpec((dil, tm // dil, 256), lambda i: (0, i, 0)))
    in_specs += [_rows(tm, D_MODEL)] * 2 + [_rows(tm, 256)] * 2
    return pl.pallas_call(
        body, name=name, grid=(n // tm,), in_specs=in_specs,
        out_specs=_rows(tm, IN_WIDTH), out_shape=_sds((n, IN_WIDTH), BF16),
        scratch_shapes=[_dil_scratch(tm)],
        compiler_params=_params("parallel"),
    )(*dna, *ddil_q, *ddil_k, *ddil_v, dgn, dgd, cos_t, sin_t)


N_ROW_OFF = 2 * NA_WIN_ROWS - 1
N_PAIRS = N_ROW_OFF - 1
RB_WIDTH = (N_ROW_OFF + 1) * GRID_W


def _na_bias(rb_ref, pair_scr):
    shape = (GRID_W, RB_WIDTH)
    qc = lax.broadcasted_iota(jnp.int32, shape, 0)
    qc2 = lax.broadcasted_iota(jnp.int32, (GRID_W, 128), 0)
    kc2 = lax.broadcasted_iota(jnp.int32, (GRID_W, 128), 1) & (GRID_W - 1)
    cs = jnp.clip(qc2 - 8, 0, GRID_W - 16)
    valid = (kc2 >= cs) & (kc2 < cs + 16)
    for hh in range(2):
        t = jnp.broadcast_to(rb_ref[hh], shape)
        t = pltpu.roll(t, RB_WIDTH - 15, 1)
        for b in range(6):
            t = jnp.where(((qc >> b) & 1) == 1, pltpu.roll(t, 1 << b, 1), t)
        t_odd = pltpu.roll(t, RB_WIDTH - GRID_W, 1)
        for ro in range(N_PAIRS):
            src = t if ro % 2 == 0 else t_odd
            base = (ro // 2) * 128
            pair_scr[hh, ro] = jnp.where(valid, src[:, base:base + 128], NEG_INF)


NA_GROUP_FWD = 8
NA_GROUP_BWD = 4


def _stack_heads(ref, r, scale=1.0):
    lane = lax.broadcasted_iota(jnp.int32, (GRID_W, 128), 1)
    t = ref[pl.ds(pl.multiple_of(r * GRID_W, GRID_W), GRID_W), :].astype(F32) * scale
    return jnp.concatenate([jnp.where(lane < 64, t, 0.0), jnp.where(lane >= 64, t, 0.0)], axis=0).astype(BF16)


def _unstack_heads(t2):
    lane = lax.broadcasted_iota(jnp.int32, (GRID_W, 128), 1)
    return jnp.where(lane < 64, t2[:GRID_W], t2[GRID_W:])


def _na_window(k_ref, v_ref, r, n_rows):
    rs = jnp.clip(r - NA_WIN_ROWS // 2, 0, n_rows - NA_WIN_ROWS)
    ro0 = (NA_WIN_ROWS - 1) - (r - rs)
    off = pl.multiple_of(rs * GRID_W, GRID_W)
    kw = k_ref[pl.ds(off, NA_WIN_ROWS * GRID_W), :]
    vw = v_ref[pl.ds(off, NA_WIN_ROWS * GRID_W), :]
    return kw, vw, off, ro0


def _na_probs(s_raw, pair_scr, ro0):
    bias = [jnp.concatenate([pair_scr[hh, ro0 + 2 * j] for j in range(NA_WIN_ROWS // 2)], axis=1)
            for hh in range(2)]
    s = s_raw + jnp.concatenate(bias, axis=0)
    m = jnp.max(s, axis=-1, keepdims=True)
    e = jnp.exp(s - m)
    return e * (1.0 / jnp.sum(e, axis=-1, keepdims=True))


def _na_qkv_specs(n):
    pairs = NA_WIDTH // 128
    return [pl.BlockSpec((n, 128), lambda h, first=t * pairs: (0, first + h)) for t in range(3)]


def _na_fwd(qkv, rb, *, name):
    n = qkv.shape[0]
    n_rows = n // GRID_W

    def body(ins, outs, scr):
        q_ref, k_ref, v_ref, rb_ref = ins
        o_ref, = outs
        pair_scr, = scr
        _na_bias(rb_ref, pair_scr)

        def group(g, carry):
            rows = [g * NA_GROUP_FWD + t for t in range(NA_GROUP_FWD)]
            wins = [_na_window(k_ref, v_ref, r, n_rows) for r in rows]
            raw = [lax.dot_general(_stack_heads(q_ref, r, QK_SCALE), w[0], NT_DIMS, preferred_element_type=F32)
                   for r, w in zip(rows, wins)]
            probs = [_na_probs(s, pair_scr, w[3]) for s, w in zip(raw, wins)]
            outs2 = [jnp.dot(p.astype(BF16), w[1], preferred_element_type=F32) for p, w in zip(probs, wins)]
            for r, o2 in zip(rows, outs2):
                o_ref[pl.ds(pl.multiple_of(r * GRID_W, GRID_W), GRID_W), :] = _unstack_heads(o2).astype(BF16)
            return carry

        lax.fori_loop(0, n_rows // NA_GROUP_FWD, group, 0)

    col = pl.BlockSpec((n, 128), lambda h: (0, h))
    return _call(
        body, name=name, grid=(NA_WIDTH // 128,),
        in_specs=_na_qkv_specs(n) + [pl.BlockSpec((2, 1, RB_WIDTH), lambda h: (h, 0, 0))],
        out_specs=[col], out_shape=[_sds((n, NA_WIDTH), BF16)],
        scratch_shapes=[pltpu.VMEM((2, N_PAIRS, GRID_W, 128), F32)],
        args=(qkv, qkv, qkv, rb))[0]


def _na_bwd(qkv, do, rb, *, name, after=None):
    n = qkv.shape[0]
    n_rows = n // GRID_W
    win = NA_WIN_ROWS * GRID_W

    def body(ins, outs, scr):
        q_ref, k_ref, v_ref, do_ref, rb_ref = ins
        dq_ref, dk_ref, dv_ref, drb_ref = outs
        pair_scr, acc_scr = scr
        _na_bias(rb_ref, pair_scr)
        acc_scr[...] = jnp.zeros_like(acc_scr)
        dk_ref[...] = jnp.zeros_like(dk_ref)
        dv_ref[...] = jnp.zeros_like(dv_ref)

        def group(g, carry):
            rows = [g * NA_GROUP_BWD + t for t in range(NA_GROUP_BWD)]
            wins = [_na_window(k_ref, v_ref, r, n_rows) for r in rows]
            qss = [_stack_heads(q_ref, r, QK_SCALE) for r in rows]
            doss = [_stack_heads(do_ref, r) for r in rows]
            raw = [lax.dot_general(qs, w[0], NT_DIMS, preferred_element_type=F32) for qs, w in zip(qss, wins)]
            dps = [lax.dot_general(dos, w[1], NT_DIMS, preferred_element_type=F32) for dos, w in zip(doss, wins)]
            probs = [_na_probs(s, pair_scr, w[3]) for s, w in zip(raw, wins)]
            dss = [p * (dp - jnp.sum(p * dp, axis=-1, keepdims=True)) for p, dp in zip(probs, dps)]
            dsbs = [ds.astype(BF16) for ds in dss]
            dq2s = [jnp.dot(dsb, w[0], preferred_element_type=F32) for dsb, w in zip(dsbs, wins)]
            dkws = [lax.dot_general(dsb, qs, TN_DIMS, preferred_element_type=F32) for dsb, qs in zip(dsbs, qss)]
            dvws = [lax.dot_general(p.astype(BF16), dos, TN_DIMS, preferred_element_type=F32)
                    for p, dos in zip(probs, doss)]
            for t, r in enumerate(rows):
                _, _, off, ro0 = wins[t]
                for hh in range(2):
                    for j in range(NA_WIN_ROWS // 2):
                        acc_scr[hh, ro0 + 2 * j] += dss[t][hh * GRID_W:(hh + 1) * GRID_W, j * 128:(j + 1) * 128]
                dq_ref[pl.ds(pl.multiple_of(r * GRID_W, GRID_W), GRID_W), :] = (
                    _unstack_heads(dq2s[t]) * QK_SCALE).astype(BF16)
                dk_ref[pl.ds(off, win), :] += dkws[t]
                dv_ref[pl.ds(off, win), :] += dvws[t]
            return carry

        lax.fori_loop(0, n_rows // NA_GROUP_BWD, group, 0)

        qc = lax.broadcasted_iota(jnp.int32, (N_PAIRS * GRID_W, 128), 0)
        for hh in range(2):
            t = acc_scr[hh].reshape(N_PAIRS * GRID_W, 128)
            for b in range(6):
                t = jnp.where(((qc >> b) & 1) == 1, pltpu.roll(t, 128 - (1 << b), 1), t)
            t = pltpu.roll(t, 15, 1)
            drb_ref[hh] = jnp.sum(t.reshape(N_PAIRS, GRID_W, 128), axis=1)

    col = pl.BlockSpec((n, 128), lambda h: (0, h))
    return _call(
        body, name=name, grid=(NA_WIDTH // 128,),
        in_specs=_na_qkv_specs(n) + [col, pl.BlockSpec((2, 1, RB_WIDTH), lambda h: (h, 0, 0))],
        out_specs=[col, col, col, pl.BlockSpec((2, N_PAIRS, 128), lambda h: (h, 0, 0))],
        out_shape=[_sds((n, NA_WIDTH), BF16), _sds((n, NA_WIDTH), F32), _sds((n, NA_WIDTH), F32),
                   _sds((8, N_PAIRS, 128), F32)],
        scratch_shapes=[pltpu.VMEM((2, N_PAIRS, GRID_W, 128), F32),
                        pltpu.VMEM((2, N_PAIRS, GRID_W, 128), F32)],
        args=(qkv, qkv, qkv, do, rb), after=after)


def _rpb_table(rpb2):
    t = jnp.pad(rpb2, ((0, 0), (0, 1), (0, GRID_W - rpb2.shape[-1])))
    return t.reshape(8, 1, RB_WIDTH)


def _rpb_grad(drb, *, name):
    kdim = drb.shape[1]

    def body(x_ref, o_ref):
        kk = lax.broadcasted_iota(jnp.int32, (128, 512), 0)
        jj = lax.broadcasted_iota(jnp.int32, (128, 512), 1)
        half, co = kk >> 6, kk & 63
        acc = jnp.zeros((8, 512), F32)
        for ro in range(N_PAIRS):
            hit = ((ro + half) == (jj >> 5)) & (co == (jj & 31)) & (co < 31)
            onehot = jnp.where(hit, 1.0, 0.0).astype(F32)
            acc = acc + jnp.dot(x_ref[:, ro * 128:(ro + 1) * 128], onehot, preferred_element_type=F32,
                                precision=lax.Precision.HIGHEST)
        o_ref[...] = acc

    return pl.pallas_call(
        body, name=name, grid=(1,),
        in_specs=[_const((8, kdim))], out_specs=_const((8, 512)), out_shape=_sds((8, 512), F32),
        compiler_params=_params("arbitrary"),
    )(drb)


DIL_GROUP = 2


def _dil_blocks(length):
    qb = min(128, length)
    return qb, min(qb + 2 * DIL_RADIUS, length), min(DIL_GROUP, length // qb)


def _stack_lanes(ref, t, qb, scale=1.0):
    lane = lax.broadcasted_iota(jnp.int32, (qb, 256), 1)
    val = ref[0, t * qb:(t + 1) * qb, :].astype(F32) * scale
    return jnp.concatenate([jnp.where((lane >> 6) == h, val, 0.0) for h in range(4)], axis=0).astype(BF16)


def _dil_window(k_ref, v_ref, blk, qb, win, length):
    start = pl.multiple_of(jnp.clip(blk * qb - DIL_RADIUS, 0, length - win), DIL_RADIUS)
    return k_ref[0, pl.ds(start, win), :], v_ref[0, pl.ds(start, win), :], start


def _dil_caps_init(caps_scr, qb, win):
    @pl.when((pl.program_id(0) == 0) & (pl.program_id(1) == 0))
    def _():
        gap = ((lax.broadcasted_iota(jnp.int32, (4 * qb, win), 0) & (qb - 1))
               - lax.broadcasted_iota(jnp.int32, (4 * qb, win), 1))
        for v in range(3):
            caps_scr[v] = jnp.where(jnp.abs(gap + v * DIL_RADIUS) <= DIL_RADIUS, jnp.inf, NEG_INF)


def _dil_mask(s, blk, start, qb, caps_scr):
    return jnp.minimum(s, caps_scr[(blk * qb - start) // DIL_RADIUS])


def _pick_heads(stacked, qb):
    lane = lax.broadcasted_iota(jnp.int32, (qb, 256), 1)
    out = jnp.zeros((qb, 256), stacked.dtype)
    for h in range(4):
        out = jnp.where((lane >> 6) == h, stacked[h * qb:(h + 1) * qb], out)
    return out


def _stack_head_cols(ref, t, qb):
    return jnp.concatenate([ref[0, t * qb:(t + 1) * qb, 64 * h:64 * h + 1] for h in range(4)], axis=0)


def _dil_fwd(q, k, v, *, name, after=None):
    dil, length, _ = q.shape
    qb, win, grp = _dil_blocks(length)
    extra = [] if after is None else [after]

    def body(q_ref, k_ref, v_ref, *rest):
        o_ref, lse_ref, caps_scr = rest[-3:]
        _dil_caps_init(caps_scr, qb, win)
        blks = [pl.program_id(1) * grp + t for t in range(grp)]
        wins = [_dil_window(k_ref, v_ref, b, qb, win, length) for b in blks]
        raw = [lax.dot_general(_stack_lanes(q_ref, t, qb, QK_SCALE), w[0], NT_DIMS, preferred_element_type=F32)
               for t, w in enumerate(wins)]
        lses, outs = [], []
        for t, (s, w) in enumerate(zip(raw, wins)):
            s = _dil_mask(s, blks[t], w[2], qb, caps_scr)
            m = jnp.max(s, axis=-1, keepdims=True)
            e = jnp.exp(s - m)
            norm = jnp.sum(e, axis=-1, keepdims=True)
            lses.append(m + jnp.log(norm))
            outs.append(jnp.dot((e * (1.0 / norm)).astype(BF16), w[1], preferred_element_type=F32))
        for t in range(grp):
            o_ref[0, t * qb:(t + 1) * qb, :] = _pick_heads(outs[t], qb)
            lse_ref[0, t * qb:(t + 1) * qb, :] = _pick_heads(jnp.broadcast_to(lses[t], (4 * qb, 256)), qb)

    seq = pl.BlockSpec((1, length, 256), lambda j, i: (j, 0, 0))
    blk = pl.BlockSpec((1, grp * qb, 256), lambda j, i: (j, i, 0))
    return pl.pallas_call(
        body, name=name, grid=(dil, length // (grp * qb)),
        in_specs=[blk, seq, seq] + [pl.BlockSpec(memory_space=pl.ANY)] * len(extra), out_specs=[blk, blk],
        out_shape=[_sds((dil, length, 256), F32)] * 2,
        scratch_shapes=[pltpu.VMEM((3, 4 * qb, win), F32)],
        compiler_params=_params("arbitrary", "arbitrary"),
    )(q, k, v, *extra)


def _dil_bwd(q, k, v, do, lse, cc, *, name):
    dil, length, _ = q.shape
    qb, win, grp = _dil_blocks(length)

    def body(q_ref, k_ref, v_ref, do_ref, lse_ref, cc_ref, dq_ref, dk_ref, dv_ref, caps_scr):
        _dil_caps_init(caps_scr, qb, win)

        @pl.when(pl.program_id(1) == 0)
        def _():
            dk_ref[...] = jnp.zeros_like(dk_ref)
            dv_ref[...] = jnp.zeros_like(dv_ref)

        blks = [pl.program_id(1) * grp + t for t in range(grp)]
        wins = [_dil_window(k_ref, v_ref, b, qb, win, length) for b in blks]
        qss = [_stack_lanes(q_ref, t, qb, QK_SCALE) for t in range(grp)]
        doss = [_stack_lanes(do_ref, t, qb) for t in range(grp)]
        raw = [lax.dot_general(qs, w[0], NT_DIMS, preferred_element_type=F32) for qs, w in zip(qss, wins)]
        dps = [lax.dot_general(dos, w[1], NT_DIMS, preferred_element_type=F32) for dos, w in zip(doss, wins)]
        probs = [jnp.exp(_dil_mask(s, blks[t], wins[t][2], qb, caps_scr) - _stack_head_cols(lse_ref, t, qb))
                 for t, s in enumerate(raw)]
        dsbs = [(p * (dp + _stack_head_cols(cc_ref, t, qb))).astype(BF16)
                for t, (p, dp) in enumerate(zip(probs, dps))]
        dq4s = [jnp.dot(dsb, w[0], preferred_element_type=F32) for dsb, w in zip(dsbs, wins)]
        dkws = [lax.dot_general(dsb, qs, TN_DIMS, preferred_element_type=F32) for dsb, qs in zip(dsbs, qss)]
        dvws = [lax.dot_general(p.astype(BF16), dos, TN_DIMS, preferred_element_type=F32)
                for p, dos in zip(probs, doss)]
        for t in range(grp):
            dq_ref[0, t * qb:(t + 1) * qb, :] = _pick_heads(dq4s[t], qb) * QK_SCALE
            dk_ref[0, pl.ds(wins[t][2], win), :] += dkws[t]
            dv_ref[0, pl.ds(wins[t][2], win), :] += dvws[t]

    seq = pl.BlockSpec((1, length, 256), lambda j, i: (j, 0, 0))
    blk = pl.BlockSpec((1, grp * qb, 256), lambda j, i: (j, i, 0))
    return pl.pallas_call(
        body, name=name, grid=(dil, length // (grp * qb)),
        in_specs=[blk, seq, seq, blk, blk, blk], out_specs=[blk, seq, seq],
        out_shape=[_sds((dil, length, 256), F32)] * 3,
        scratch_shapes=[pltpu.VMEM((3, 4 * qb, win), F32)],
        compiler_params=_params("arbitrary", "arbitrary"),
    )(q, k, v, do, lse, cc)


def _merge_weights(lses):
    m = jnp.maximum(jnp.maximum(lses[0], lses[1]), lses[2])
    es = [jnp.exp(t - m) for t in lses]
    inv = 1.0 / (es[0] + es[1] + es[2])
    return [e * inv for e in es]


def _branch_mix(y_na, w_bna, outs, lses, w_bd, gates, *, tm, name):
    n = y_na.shape[0]
    chunk = min(EPILOGUE_ROWS, tm)

    def body(yna_ref, wn_ref, *rest):
        o_in, l_in = rest[0:3], rest[3:6]
        wd_ref, sn_ref, sd_ref = rest[6:9]
        y_ref, yb_ref, bn_ref, bd_ref, mix_ref, scr = rest[9:15]
        for r0 in range(0, tm, chunk):
            rows = slice(r0, r0 + chunk)
            lv = [_load_token_order(l_in[g], scr, d, chunk, r0) for g, d in enumerate(DIL_DILATIONS)]
            ws = _merge_weights(lv)
            y = jnp.zeros((chunk, 256), F32)
            for g, d in enumerate(DIL_DILATIONS):
                y = y + ws[g] * _load_token_order(o_in[g], scr, d, chunk, r0)
            yb = y.astype(BF16)
            y_ref[rows, :] = y
            yb_ref[rows, :] = yb
            bn = lax.dot_general(yna_ref[rows, :], wn_ref[...], NT_DIMS, preferred_element_type=F32).astype(BF16)
            bd = lax.dot_general(yb, wd_ref[...], NT_DIMS, preferred_element_type=F32)
            bn_ref[rows, :] = bn
            bd, mixed = _gate_mix_tile(bd, sn_ref[rows, :], bn, sd_ref[rows, :])
            bd_ref[rows, :] = bd.astype(BF16)
            mix_ref[rows, :] = mixed.astype(BF16)

    specs = [_dil_spec(d, tm) for d in DIL_DILATIONS]
    return pl.pallas_call(
        body, name=name, grid=(n // tm,),
        in_specs=[_rows(tm, NA_WIDTH), _const(w_bna.shape)] + specs + specs
                 + [_const(w_bd.shape), _rows(tm, D_MODEL, 0), _rows(tm, D_MODEL, 1)],
        out_specs=[_rows(tm, 256)] * 2 + [_rows(tm, D_MODEL)] * 3,
        out_shape=[_sds((n, 256), F32), _sds((n, 256), BF16)] + [_sds((n, D_MODEL), BF16)] * 3,
        scratch_shapes=[_dil_scratch(chunk)],
        compiler_params=_params("parallel"),
    )(y_na, w_bna, *outs, *lses, w_bd, gates, gates)


def _branch_bwd(dh, w_out, gates, bn, bd, w_bna, w_bd, y, lses, *, tm, name):
    n = dh.shape[0]
    chunk = min(EPILOGUE_ROWS, tm)

    def body(dh_ref, wo_ref, sn_ref, sd_ref, bn_ref, bd_ref, wn_ref, wd_ref, y_ref, *rest):
        l_in = rest[0:3]
        dbn_ref, dbd_ref, dgn_ref, dgd_ref, dyna_ref = rest[3:8]
        do_out, cc_out, scr = rest[8:11], rest[11:14], rest[14]
        rr = lax.broadcasted_iota(jnp.int32, (256, 256), 0) >> 6
        cc = lax.broadcasted_iota(jnp.int32, (256, 256), 1) >> 6
        ones = jnp.where(rr == cc, 1.0, 0.0).astype(F32)
        for r0 in range(0, tm, chunk):
            rows = slice(r0, r0 + chunk)
            dm = lax.dot_general(dh_ref[rows, :], wo_ref[...], NT_DIMS, preferred_element_type=F32)
            dbn, dbd, dgn, dgd = (t.astype(BF16) for t in _gate_bwd_tile(
                dm, sn_ref[rows, :], bn_ref[rows, :], sd_ref[rows, :], bd_ref[rows, :]))
            dbn_ref[rows, :] = dbn
            dbd_ref[rows, :] = dbd
            dgn_ref[rows, :] = dgn
            dgd_ref[rows, :] = dgd
            dyna_ref[rows, :] = jnp.dot(dbn, wn_ref[...], preferred_element_type=F32).astype(BF16)
            dyv = jnp.dot(dbd, wd_ref[...], preferred_element_type=F32)
            lv = [_load_token_order(l_in[g], scr, d, chunk, r0) for g, d in enumerate(DIL_DILATIONS)]
            ws = _merge_weights(lv)
            tsum = jnp.dot(dyv * y_ref[rows, :], ones, preferred_element_type=F32,
                           precision=lax.Precision.HIGHEST)
            for g, d in enumerate(DIL_DILATIONS):
                _store_dil_order(ws[g] * dyv, do_out[g], scr, d, r0)
                _store_dil_order(-ws[g] * tsum, cc_out[g], scr, d, r0)

    specs = [_dil_spec(d, tm) for d in DIL_DILATIONS]
    wide = _rows(tm, D_MODEL)
    res = pl.pallas_call(
        body, name=name, grid=(n // tm,),
        in_specs=[wide, _const(w_out.shape), _rows(tm, D_MODEL, 0), _rows(tm, D_MODEL, 1), wide, wide,
                  _const(w_bna.shape), _const(w_bd.shape), _rows(tm, 256)] + specs,
        out_specs=[wide] * 4 + [_rows(tm, NA_WIDTH)] + specs + specs,
        out_shape=[_sds((n, D_MODEL), BF16)] * 4 + [_sds((n, NA_WIDTH), BF16)]
                  + [_sds((d, n // d, 256), BF16) for d in DIL_DILATIONS]
                  + [_sds((d, n // d, 256), F32) for d in DIL_DILATIONS],
        scratch_shapes=[_dil_scratch(chunk)],
        compiler_params=_params("parallel"),
    )(dh, w_out, gates, gates, bn, bd, w_bna, w_bd, y, *lses)
    return res[0], res[1], res[2], res[3], res[4], res[5:8], res[8:11]


_WEIGHTS = (("w_in", 1, 736), ("w_branch_na", 1, 128), ("w_branch_dil", 1, 128), ("w_out", 0, 128),
            ("w_up", 1, 512), ("w_down", 0, 512), ("w_ple_gate", 0, 128), ("w_ple_proj", 1, 128))
_W_IN, _W_BNA, _W_BD, _W_OUT, _W_UP, _W_DOWN, _W_PG, _W_PP = range(8)


def _to_full(gathered):
    return gathered.reshape(-1, gathered.shape[2])


def _to_chunks(widx, mat):
    return mat.reshape(N_DEV, _WEIGHTS[widx][2], mat.shape[1])


def _local_step(x, p_bf16, positions, target, g_mix, g_mlp, g_ple, g_final, rpb2,
                get_w_in, relay_rest, get_rest, send_grads):
    tm = 512
    half = HEAD_DIM // 2
    inv_freq = 10000.0 ** (-jnp.arange(half, dtype=F32) / half)
    ang = positions.astype(F32)[:, None] * inv_freq
    cos, sin = jnp.cos(ang), jnp.sin(ang)
    cos_t = jnp.tile(jnp.concatenate([cos, cos], axis=-1), (1, 4))
    sin_t = jnp.tile(jnp.concatenate([-sin, sin], axis=-1), (1, 4))
    rb = _rpb_table(rpb2)

    a = _rms_fwd(x, g_mix, tm=tm, name="rms_mix")
    w_in, token = get_w_in((a, cos_t, sin_t, p_bf16))
    na_qkv, gates, dq_g, dk_g, dv_g = _project_in(a, w_in, cos_t, sin_t, tm=512, name="mm_in", after=token)
    y_na = _na_fwd(na_qkv, rb, name="na_fwd")
    token = relay_rest(y_na)
    d_out, d_lse = [], []
    for g in range(3):
        o, lse = _dil_fwd(dq_g[g], dk_g[g], dv_g[g], name=f"dil_fwd{g}", after=token if g == 0 else None)
        d_out.append(o)
        d_lse.append(lse)
    w_bna, w_bd = get_rest(d_out[2], 0)
    y_dil, y_dil_b, bn, bd, mixed = _branch_mix(y_na, w_bna, d_out, d_lse, w_bd, gates, tm=tm, name="branch_mix")
    w_out, w_up, w_down, w_pg, w_pp = get_rest(mixed, 1)
    h1, c = _matmul(mixed, w_out, out_dtype=(F32, BF16), tm=512, tn=1024, tk=1024, name="mm_out",
                    extra=(x, g_mlp), epilogue=_residual_rms_tile)
    u, f = _matmul(c, w_up, tb=True, out_dtype=(BF16, BF16), tm=512, tn=2048, tk=1024, name="mm_up",
                   epilogue=lambda acc: (acc, jnp.square(jnp.maximum(acc, 0.0))))

    e, dpp, dgt, dh2, dh2_b, dg_final, loss, dg_ple = _tail_step(
        f, w_down, h1, w_pg, p_bf16, w_pp, target, g_final, g_ple, tm=tm, name="tail_step")
    loss = loss[:, :128]
    gw_pp = _matmul(p_bf16, dpp, ta=True, transpose_out=True, out_dtype=BF16, tm=256, tn=1024, tk=2048,
                    name="mm_gw_pp")
    gw_pg = _matmul(e, dgt, ta=True, out_dtype=BF16, tm=512, tn=1024, tk=2048, name="mm_gw_pg")
    du = _matmul(dh2_b, w_down, tb=True, out_dtype=BF16, tm=512, tn=2048, tk=1024, name="mm_du",
                 extra=(u,), epilogue=lambda acc, uv: (acc * (2.0 * jnp.maximum(uv.astype(F32), 0.0)),))
    gw_down = _matmul(f, dh2_b, ta=True, out_dtype=BF16, tm=1024, tn=1024, tk=2048, name="mm_gw_down")
    gw_up = _matmul(c, du, ta=True, transpose_out=True, out_dtype=BF16, tm=512, tn=2048, tk=2048, name="mm_gw_up")
    dh1, dh1_b, dg_mlp = _matmul(
        du, w_up, out_dtype=(F32, BF16), tm=512, tn=1024, tk=4096, name="mm_dc",
        extra=(h1, g_mlp, dh2), epilogue=_rms_bwd_twice, n_colsum=1)
    dbn, dbd, dgn, dgd, dy_na, do_g, cc_g = _branch_bwd(dh1_b, w_out, gates, bn, bd, w_bna, w_bd, y_dil, d_lse,
                                                        tm=tm, name="branch_bwd")
    gw_out = _matmul(mixed, dh1_b, ta=True, out_dtype=BF16, tm=512, tn=1024, tk=2048, name="mm_gw_out")
    gw_bna = _matmul(y_na, dbn, ta=True, transpose_out=True, out_dtype=BF16, tm=512, tn=1024, tk=2048,
                     name="mm_gw_bna")
    gw_bd = _matmul(y_dil_b, dbd, ta=True, transpose_out=True, out_dtype=BF16, tm=256, tn=1024, tk=2048,
                    name="mm_gw_bd")
    token = send_grads((_W_PP, _W_PG, _W_DOWN, _W_UP, _W_OUT, _W_BNA, _W_BD),
                       (gw_pp, gw_pg, gw_down, gw_up, gw_out, gw_bna, gw_bd))
    dna = _na_bwd(na_qkv, dy_na, rb, name="na_bwd", after=token)
    drpb = _rpb_grad(dna[3].reshape(8, -1), name="rpb_grad")
    ddq, ddk, ddv = [], [], []
    for g in range(3):
        r = _dil_bwd(dq_g[g], dk_g[g], dv_g[g], do_g[g], d_lse[g], cc_g[g], name=f"dil_bwd{g}")
        ddq.append(r[0])
        ddk.append(r[1])
        ddv.append(r[2])
    dproj = _assemble_dproj(dna[0:3], ddq, ddk, ddv, dgn, dgd, cos_t, sin_t, tm=tm, name="assemble_dproj")
    gw_in = _matmul(a, dproj, ta=True, transpose_out=True, out_dtype=BF16, tm=512, tn=2944, tk=2048, name="mm_gw_in")
    token = send_grads((_W_IN,), (gw_in,))
    dx, dg_mix = _matmul(
        dproj, w_in, out_dtype=(F32,), tm=512, tn=1024, tk=5888, name="mm_da", after=token,
        extra=(x, g_mix, dh1), epilogue=_rms_bwd_tile, n_colsum=1)
    return loss, dx, (dg_mix, dg_mlp, dg_ple, dg_final), drpb


def _cast_bf16(t, *, name):
    def body(t_ref, o_ref):
        o_ref[...] = t_ref[...].astype(BF16)

    rows, cols = t.shape
    tr = 256 if rows % 256 == 0 else rows
    blk = pl.BlockSpec((tr, cols), lambda i: (i, 0))
    return pl.pallas_call(body, name=name, grid=(rows // tr,), in_specs=[blk], out_specs=blk,
                          out_shape=_sds(t.shape, BF16), compiler_params=_params("parallel"))(t)


def _adamw(w, g, m, v):
    m = ADAM_B1 * m + (1.0 - ADAM_B1) * g
    v = ADAM_B2 * v + (1.0 - ADAM_B2) * (g * g)
    m_hat = m / (1.0 - ADAM_B1 ** ADAM_STEP)
    v_hat = v / (1.0 - ADAM_B2 ** ADAM_STEP)
    delta = -ADAM_LR * (m_hat / (jnp.sqrt(v_hat) + ADAM_EPS) + ADAM_WD * w)
    return delta, m, v


def _sum_adamw(parts, w, m, v, *, tr, name, own=None, transposed=False):
    rows, cols = w.shape
    n_pre = 0 if own is None else 1

    def body(*refs):
        p_ref, w_ref, m_ref, v_ref = refs[n_pre:n_pre + 4]
        g_ref, d_ref, nm_ref, nv_ref = refs[-4:]
        g = (p_ref[0] if own is None else refs[n_pre + 4][...]).astype(F32)
        for s in range(1, N_DEV):
            g = g + p_ref[s].astype(F32)
        if transposed:
            g = g.T
        g_ref[...] = g
        d_ref[...], nm_ref[...], nv_ref[...] = _adamw(w_ref[...], g, m_ref[...], v_ref[...])

    if transposed:
        blk = pl.BlockSpec((rows, tr), lambda i, *_: (0, i))
        g_rows, steps = rows, cols // tr
    else:
        blk = pl.BlockSpec((tr, cols), lambda i, *_: (i, 0))
        g_rows, steps = cols, rows // tr
    in_specs = [pl.BlockSpec((N_DEV, tr, g_rows), lambda i, *_: (0, i, 0)), blk, blk, blk]
    args = [parts, w, m, v]
    if own is not None:
        in_specs.append(pl.BlockSpec((None, tr, g_rows), lambda i, idx: (idx[0], i, 0)))
        args = [own[1]] + args + [own[0]]
    return pl.pallas_call(
        body, name=name,
        grid_spec=pltpu.PrefetchScalarGridSpec(num_scalar_prefetch=n_pre, grid=(steps,), in_specs=in_specs,
                                               out_specs=[blk] * 4),
        out_shape=[_sds((rows, cols), F32)] * 4,
        compiler_params=_params("parallel"),
    )(*args)


_RPB_SIZE = 8 * 15 * 31


def _pack_small(g_mix, g_mlp, g_ple, g_final, rpb, loss_row):
    flat = jnp.concatenate([g_mix.reshape(-1), g_mlp.reshape(-1), g_ple.reshape(-1), g_final.reshape(-1),
                            rpb.reshape(-1), jnp.zeros((3840 - _RPB_SIZE,), F32), loss_row.reshape(-1),
                            jnp.zeros((128,), F32)])
    return flat.reshape(64, 128)


def _unpack_small(t):
    flat = t.reshape(-1)
    return (flat[0:1024].reshape(1, 1024), flat[4096:4096 + _RPB_SIZE].reshape(1, 8, 15, 31),
            flat[1024:2048].reshape(1, 1024), flat[2048:3072].reshape(1, 1024), flat[3072:4096])


def kernel(x, p, positions, g_mix, w_in, rpb, w_branch_na, w_branch_dil, w_out, g_mlp, w_up, w_down, g_ple, w_ple_gate, w_ple_proj, g_final, loss_target, m_g_mix, m_w_in, m_rpb, m_w_branch_na, m_w_branch_dil, m_w_out, m_g_mlp, m_w_up, m_w_down, m_g_ple, m_w_ple_gate, m_w_ple_proj, m_g_final, v_g_mix, v_w_in, v_rpb, v_w_branch_na, v_w_branch_dil, v_w_out, v_g_mlp, v_w_up, v_w_down, v_g_ple, v_w_ple_gate, v_w_ple_proj, v_g_final):
    sharded = dict(w_in=(w_in, m_w_in, v_w_in), w_branch_na=(w_branch_na, m_w_branch_na, v_w_branch_na),
                   w_branch_dil=(w_branch_dil, m_w_branch_dil, v_w_branch_dil), w_out=(w_out, m_w_out, v_w_out),
                   w_up=(w_up, m_w_up, v_w_up), w_down=(w_down, m_w_down, v_w_down),
                   w_ple_gate=(w_ple_gate, m_w_ple_gate, v_w_ple_gate),
                   w_ple_proj=(w_ple_proj, m_w_ple_proj, v_w_ple_proj))
    shards = {k: tuple(t[0] for t in val) for k, val in sharded.items()}

    me = _my_index()

    shards["w_in"] = tuple(t.T for t in shards["w_in"])

    w_in_b = _cast_bf16(shards["w_in"][0], name="cast_w_in")
    rest_b = [shards[name][0].astype(BF16).T if axis == 1 else shards[name][0].astype(BF16)
              for name, axis, _ in _WEIGHTS[1:]]
    first_in, token_in = _start_copies(_first_leg_copies, [w_in_b], [_sds((N_DEV,) + w_in_b.shape, BF16)], 4,
                                       name="start_gather_w_in")

    def whole(landed, mine):
        return _to_full(lax.dynamic_update_index_in_dim(landed, mine, me, 0))

    rest = {}

    def get_w_in(after):
        (mine,), landed = _wait_copies(_first_leg_copies, first_in, (*after, *rest_b), name="wait_gather_w_in")
        second, token = _start_copies(_second_leg_copies, [], landed, 3, name="start_forward_w_in")
        _, (landed,) = _wait_copies(_second_leg_copies, second, token, name="wait_forward_w_in")
        rest["first"], token = _start_copies(_first_leg_copies, rest_b,
                                             [_sds((N_DEV,) + t.shape, BF16) for t in rest_b], 4 * len(rest_b),
                                             name="start_gather_rest", after=landed)
        return whole(landed, mine), token

    def relay_rest(after):
        rest["mine"], landed = _wait_copies(_first_leg_copies, rest["first"], after, name="wait_gather_rest")
        rest["second"], token = _start_copies(_second_leg_copies, [], landed, 3 * len(rest_b),
                                              name="start_forward_rest")
        return token

    def get_rest(after, stage):
        n_src, send_sems, recv_sems, bufs = rest["second"]
        part = slice(0, 2) if stage == 0 else slice(2, len(rest_b))
        _, landed = _wait_copies(functools.partial(_second_leg_copies, first=part.start),
                                 (n_src, send_sems, recv_sems, bufs[part]), after,
                                 name=f"wait_forward_rest{stage}")
        return [whole(t, own) for t, own in zip(landed, rest["mine"][part])]

    sent = []

    def send_grads(indices, grads):
        chunked = [_to_chunks(i, g) for i, g in zip(indices, grads)]
        handle, token = _start_copies(_exchange_copies, chunked, [_sds(t.shape, BF16) for t in chunked],
                                      7 * len(chunked),
                                      name="start_exchange_" + ("w_in" if indices == (_W_IN,) else "rest"))
        sent.append((indices, handle))
        return token

    g_mix_0 = g_mix + token_in[0:1, 0:1]
    loss, dx, dgs, drpb = _local_step(
        x[0], p[0, 0].astype(BF16), positions[0], loss_target[0],
        g_mix_0, g_mlp, g_ple, g_final.reshape(1, -1), rpb[0], get_w_in, relay_rest, get_rest, send_grads)

    drpb3 = drpb.reshape(8, 16, 32)[:, :15, :31]
    small = _pack_small(dgs[0], dgs[1], dgs[2], dgs[3], drpb3, loss)
    share, done = _start_copies(_gather_copies, [small], [_sds((N_DEV,) + small.shape, F32)], 7,
                                name="start_share_small")

    out = {}
    for indices, handle in sent:
        chunked, landed = _wait_copies(_exchange_copies, handle, done,
                                       name="wait_exchange_" + ("w_in" if indices == (_W_IN,) else "rest"))
        for i, part, mine in zip(indices, landed, chunked):
            name = _WEIGHTS[i][0]
            w, m, v = shards[name]
            turned = _WEIGHTS[i][1] == 1 and i != _W_IN
            res = _sum_adamw(part, w, m, v, tr=368 if i == _W_IN else 128, name="adamw_" + name,
                             own=(mine, me.reshape(1).astype(jnp.int32)), transposed=turned)
            out[name] = [(t.T if i == _W_IN else t)[None] for t in res]
            done = res[0]
    (small,), (small_landed,) = _wait_copies(_gather_copies, share, done, name="wait_share_small")
    small_all = lax.dynamic_update_index_in_dim(small_landed, small, me, 0)
    small_w = _pack_small(g_mix, g_mlp, g_ple, g_final, rpb, jnp.zeros((128,), F32))
    small_m = _pack_small(m_g_mix, m_g_mlp, m_g_ple, m_g_final, m_rpb, jnp.zeros((128,), F32))
    small_v = _pack_small(v_g_mix, v_g_mlp, v_g_ple, v_g_final, v_rpb, jnp.zeros((128,), F32))
    res = _sum_adamw(small_all, small_w, small_m, small_v, tr=64, name="adamw_small")
    unpacked = [_unpack_small(t) for t in res]
    for i, name in enumerate(("g_mix", "rpb", "g_mlp", "g_ple", "g_final")):
        out[name] = [u[i] for u in unpacked]
    loss_total = res[0][62, 0]

    order = ("g_mix", "w_in", "rpb", "w_branch_na", "w_branch_dil", "w_out", "g_mlp", "w_up", "w_down",
             "g_ple", "w_ple_gate", "w_ple_proj", "g_final")
    grads = [out[k][0] for k in order]
    deltas = [out[k][1] for k in order]
    new_m = [out[k][2] for k in order]
    new_v = [out[k][3] for k in order]
    return (loss_total, dx[None], *grads, *deltas, *new_m, *new_v)
```

```python
import functools

import jax
import jax.numpy as jnp
from jax import lax
from jax.experimental import pallas as pl
from jax.experimental.pallas import tpu as pltpu

F32 = jnp.float32
BF16 = jnp.bfloat16

D_MODEL = 1024
HEAD_DIM = 64
GRID_W = 64
NA_WIDTH = 512
DIL_WIDTH = 768
IN_WIDTH = 5888
DIL_DILATIONS = (1, 4, 16)
DIL_RADIUS = 64
NA_WIN_ROWS = 8
RMS_EPS = 1e-6
NEG_INF = -1e30
QK_SCALE = HEAD_DIM ** -0.5

ADAM_LR = 0.001
ADAM_B1 = 0.9
ADAM_B2 = 0.999
ADAM_EPS = 1e-08
ADAM_WD = 0.01
ADAM_STEP = 10

N_DEV = 8
VMEM_LIMIT = 56 * 1024 * 1024
EPILOGUE_ROWS = 256
MESH = pl.DeviceIdType.MESH

NT_DIMS = (((1,), (1,)), ((), ()))
TN_DIMS = (((0,), (0,)), ((), ()))


def _sds(shape, dtype):
    return jax.ShapeDtypeStruct(shape, dtype)


def _params(*sem):
    return pltpu.CompilerParams(dimension_semantics=sem, vmem_limit_bytes=VMEM_LIMIT)


def _rows(tm, width, col=0):
    return pl.BlockSpec((tm, width), lambda i, c=col: (i, c))


def _const(shape):
    zeros = (0,) * len(shape)
    return pl.BlockSpec(shape, lambda i: zeros)


def _my_index():
    return 4 * lax.axis_index("x") + 2 * lax.axis_index("y") + lax.axis_index("c")


def _peer(k):
    x, y, c = lax.axis_index("x"), lax.axis_index("y"), lax.axis_index("c")
    px = 1 - x if k & 4 else x
    py = 1 - y if k & 2 else y
    pc = 1 - c if k & 1 else c
    return (px, py, pc), 4 * px + 2 * py + pc


def _call(body, *, name, grid, in_specs, out_specs, out_shape, scratch_shapes, args, after=None):
    n_in, n_out = len(in_specs), len(out_specs)
    extra = [] if after is None else [after]
    n_x = n_in + len(extra)

    def plain(*refs):
        body(refs[:n_in], refs[n_x:n_x + n_out], refs[n_x + n_out:])

    res = pl.pallas_call(plain, name=name, grid=grid,
                         in_specs=list(in_specs) + [pl.BlockSpec(memory_space=pl.ANY)] * len(extra),
                         out_specs=out_specs, out_shape=out_shape, scratch_shapes=scratch_shapes,
                         compiler_params=_params(*(("arbitrary",) * len(grid))))(*args, *extra)
    return list(res)


_HBM_SPEC = pl.BlockSpec(memory_space=pltpu.HBM)
_SEM_SPEC = pl.BlockSpec(memory_space=pltpu.SEMAPHORE)
_SIDE_EFFECT = pltpu.SideEffectType.DATAFLOW_SIDE_EFFECTING


_FIRST_LEG = (1, 2, 4, 6)
_SECOND_LEG = (2, 4, 6)


def _gather_copies(srcs, lands, send, recv, sending):
    me = _my_index()
    out = []
    for w in range(len(srcs)):
        for k in range(1, N_DEV):
            dev, idx = _peer(k)
            out.append(pltpu.make_async_remote_copy(
                src_ref=srcs[w], dst_ref=lands[w].at[me if sending else idx],
                send_sem=send.at[w * 7 + k - 1], recv_sem=recv.at[w * 7 + k - 1],
                device_id=dev, device_id_type=MESH))
    return out


def _first_leg_copies(srcs, lands, send, recv, sending):
    me = _my_index()
    out = []
    for w in range(len(srcs)):
        for j, k in enumerate(_FIRST_LEG):
            dev, idx = _peer(k)
            out.append(pltpu.make_async_remote_copy(
                src_ref=srcs[w], dst_ref=lands[w].at[me if sending else idx],
                send_sem=send.at[w * 4 + j], recv_sem=recv.at[w * 4 + j],
                device_id=dev, device_id_type=MESH))
    return out


def _second_leg_copies(srcs, lands, send, recv, sending, first=0):
    sibling, _ = _peer(1)
    out = []
    for w in range(len(lands)):
        for j, k in enumerate(_SECOND_LEG):
            slot = _peer(k if sending else k ^ 1)[1]
            sem = (first + w) * 3 + j
            out.append(pltpu.make_async_remote_copy(
                src_ref=lands[w].at[slot], dst_ref=lands[w].at[slot],
                send_sem=send.at[sem], recv_sem=recv.at[sem],
                device_id=sibling, device_id_type=MESH))
    return out


def _exchange_copies(srcs, lands, send, recv, sending):
    out = []
    for w in range(len(srcs)):
        for k in range(1, N_DEV):
            dev, idx = _peer(k)
            out.append(pltpu.make_async_remote_copy(
                src_ref=srcs[w].at[idx], dst_ref=lands[w].at[k],
                send_sem=send.at[w * 7 + k - 1], recv_sem=recv.at[w * 7 + k - 1],
                device_id=dev, device_id_type=MESH))
    return out


def _start_copies(make, srcs, lands, n_copies, *, name, after=None):
    n_src, n_buf = len(srcs), len(srcs) + len(lands)
    extra = [] if after is None else [after]

    def body(*refs):
        send, recv = refs[n_buf + len(extra)], refs[n_buf + len(extra) + 1]
        for cp in make(refs[:n_src], refs[n_src:n_buf], send, recv, True):
            cp.start()
        refs[-1][...] = jnp.zeros_like(refs[-1])

    bufs = list(srcs) + [lax.empty(t.shape, t.dtype) if isinstance(t, jax.ShapeDtypeStruct) else t for t in lands]
    res = pl.pallas_call(
        body, name=name,
        out_shape=(pltpu.SemaphoreType.DMA((n_copies,)), pltpu.SemaphoreType.DMA((n_copies,)),
                   *[pltpu.HBM(t.shape, t.dtype) for t in bufs], _sds((8, 128), F32)),
        in_specs=[_HBM_SPEC] * n_buf + [pl.BlockSpec(memory_space=pl.ANY)] * len(extra),
        out_specs=(_SEM_SPEC, _SEM_SPEC, *([_HBM_SPEC] * n_buf), pl.BlockSpec(memory_space=pltpu.VMEM)),
        input_output_aliases={i: 2 + i for i in range(n_buf)},
        compiler_params=pltpu.CompilerParams(has_side_effects=_SIDE_EFFECT),
    )(*[pltpu.with_memory_space_constraint(t, pltpu.HBM) for t in bufs], *extra)
    return (n_src, res[0], res[1], res[2:2 + n_buf]), res[-1]


def _wait_copies(make, handle, after, *, name):
    n_src, send_sems, recv_sems, bufs = handle
    n_buf = len(bufs)
    after = list(after) if isinstance(after, (tuple, list)) else [after]

    def body(*refs):
        for cp in make(refs[:n_src], refs[n_src:n_buf], refs[n_buf], refs[n_buf + 1], False):
            cp.wait_send()
            cp.wait_recv()

    res = pl.pallas_call(
        body, name=name,
        out_shape=tuple(pltpu.HBM(t.shape, t.dtype) for t in bufs),
        in_specs=[_HBM_SPEC] * n_buf + [_SEM_SPEC, _SEM_SPEC] + [pl.BlockSpec(memory_space=pl.ANY)] * len(after),
        out_specs=tuple([_HBM_SPEC] * n_buf),
        input_output_aliases={i: i for i in range(n_buf)},
        compiler_params=pltpu.CompilerParams(has_side_effects=_SIDE_EFFECT),
    )(*bufs, send_sems, recv_sems, *after)
    return list(res[:n_src]), list(res[n_src:])


def _add_colsums(s_refs, sums, step):
    for s_ref, val in zip(s_refs, sums):
        @pl.when(step == 0)
        def _(s_ref=s_ref, val=val):
            s_ref[...] = val

        @pl.when(step > 0)
        def _(s_ref=s_ref, val=val):
            s_ref[...] += val


def _matmul(a, b, *, ta=False, tb=False, out_dtype, tm, tn, tk, name, after=None, extra=(), epilogue=None,
            n_colsum=0, transpose_out=False):
    m, k = (a.shape[1], a.shape[0]) if ta else a.shape
    n = b.shape[0] if tb else b.shape[1]
    tm, tn, tk = min(tm, m), min(tn, n), min(tk, k)
    nk = k // tk
    dims = (((0 if ta else 1,), (1 if tb else 0,)), ((), ()))
    out_dtypes = out_dtype if isinstance(out_dtype, tuple) else (out_dtype,)
    n_tiles = len(out_dtypes)

    def add_colsums(o_refs, sums):
        _add_colsums(o_refs[n_tiles:], sums, pl.program_id(1))

    def finish(acc, x_refs, o_refs):
        vals = (acc,) if epilogue is None else epilogue(acc, *[r[...] for r in x_refs])
        for o_ref, val in zip(o_refs[:n_tiles], vals[:n_tiles]):
            o_ref[...] = (val.T if transpose_out else val).astype(o_ref.dtype)
        add_colsums(o_refs, vals[n_tiles:])

    chunk = EPILOGUE_ROWS if (nk == 1 and epilogue is not None and not ta and tm % EPILOGUE_ROWS == 0) else None

    def body(ins, outs, acc):
        a_ref, b_ref = ins[:2]
        if chunk is not None:
            sums = None
            for r0 in range(0, tm, chunk):
                part = lax.dot_general(a_ref[r0:r0 + chunk, :], b_ref[...], dims, preferred_element_type=F32)
                vals = epilogue(part, *[r[...] if r.shape[0] == 1 else r[r0:r0 + chunk, :] for r in ins[2:]])
                for o_ref, val in zip(outs[:n_tiles], vals[:n_tiles]):
                    o_ref[r0:r0 + chunk, :] = val.astype(o_ref.dtype)
                sums = vals[n_tiles:] if sums is None else [s + v for s, v in zip(sums, vals[n_tiles:])]
            add_colsums(outs, sums)
            return
        part = lax.dot_general(a_ref[...], b_ref[...], dims, preferred_element_type=F32)
        if nk == 1:
            finish(part, ins[2:], outs)
            return
        acc_ref, = acc
        kk = pl.program_id(2)

        @pl.when(kk == 0)
        def _():
            acc_ref[...] = part

        @pl.when(kk > 0)
        def _():
            acc_ref[...] += part

        @pl.when(kk == nk - 1)
        def _():
            finish(acc_ref[...], ins[2:], outs)

    a_spec = (pl.BlockSpec((tk, tm), lambda j, i, kk: (kk, i)) if ta
              else pl.BlockSpec((tm, tk), lambda j, i, kk: (i, kk)))
    b_spec = (pl.BlockSpec((tn, tk), lambda j, i, kk: (j, kk)) if tb
              else pl.BlockSpec((tk, tn), lambda j, i, kk: (kk, j)))
    tile = pl.BlockSpec((tm, tn), lambda j, i, kk: (i, j))
    row = pl.BlockSpec((1, tn), lambda j, i, kk: (0, j))

    out_tile, out_dims = (pl.BlockSpec((tn, tm), lambda j, i, kk: (j, i)), (n, m)) if transpose_out else (tile, (m, n))
    res = _call(
        body, name=name, grid=(n // tn, m // tm, nk),
        in_specs=[a_spec, b_spec] + [row if t.shape[0] == 1 else tile for t in extra],
        out_specs=[out_tile] * n_tiles + [row] * n_colsum,
        out_shape=[_sds(out_dims, dt) for dt in out_dtypes] + [_sds((1, n), F32)] * n_colsum,
        scratch_shapes=[] if nk == 1 else [pltpu.VMEM((tm, tn), F32)],
        args=(a, b, *extra), after=after)
    return res if isinstance(out_dtype, tuple) or n_colsum else res[0]


def _rstd(h):
    return lax.rsqrt(jnp.mean(h * h, axis=-1, keepdims=True) + RMS_EPS)


def _sigmoid(z):
    return 1.0 / (1.0 + jnp.exp(-z))


def _rms_fwd(x, g, *, tm, name):
    n = x.shape[0]

    def body(x_ref, g_ref, o_ref):
        h = x_ref[...]
        o_ref[...] = (h * _rstd(h) * g_ref[...]).astype(BF16)

    return pl.pallas_call(
        body, name=name, grid=(n // tm,),
        in_specs=[_rows(tm, D_MODEL), _const((1, D_MODEL))],
        out_specs=_rows(tm, D_MODEL), out_shape=_sds((n, D_MODEL), BF16),
        compiler_params=_params("parallel"),
    )(x, g)


def _swap_halves(t):
    lane = lax.broadcasted_iota(jnp.int32, (t.shape[0], 128), 1)
    pieces = [t[:, c:c + 128] for c in range(0, t.shape[1], 128)]
    return jnp.concatenate([jnp.where((lane & 63) < 32, pltpu.roll(h, 96, 1), pltpu.roll(h, 32, 1))
                            for h in pieces], axis=1)


def _dil_spec(dil, tm):
    return pl.BlockSpec((dil, tm // dil, 256), lambda i: (0, i, 0))


def _dil_scratch(tm):
    return pltpu.VMEM((2, tm, 128), F32)


def _load_token_order(src, scr, dil, rows, row0=0):
    if dil == 1:
        return src[0, row0:row0 + rows, :]
    for j in range(dil):
        for c in range(2):
            scr[c, pl.ds(j, rows // dil, stride=dil), :] = (
                src[j, row0 // dil:(row0 + rows) // dil, c * 128:(c + 1) * 128])
    return jnp.concatenate([scr[0, 0:rows, :], scr[1, 0:rows, :]], axis=1)


def _store_dil_order(val, dst, scr, dil, row0=0):
    rows = val.shape[0]
    if dil == 1:
        dst[0, row0:row0 + rows, :] = val.astype(dst.dtype)
        return
    for c in range(2):
        scr[c] = val[:, c * 128:(c + 1) * 128]
    for j in range(dil):
        for c in range(2):
            dst[j, row0 // dil:(row0 + rows) // dil, c * 128:(c + 1) * 128] = (
                scr[c, pl.ds(j, rows // dil, stride=dil), :].astype(dst.dtype))


def _project_in(a, w_t, cos_t, sin_t, *, tm, name, after=None):
    n = a.shape[0]
    n_dil = len(DIL_DILATIONS)
    na_w, dil_w = 3 * NA_WIDTH, 3 * DIL_WIDTH
    chunk = min(EPILOGUE_ROWS, tm)
    extra = [] if after is None else [after]

    def body(a_ref, w_ref, cos_ref, sin_ref, *rest):
        na_ref, gate_ref = rest[len(extra):len(extra) + 2]
        outs, scr = rest[len(extra) + 2:len(extra) + 2 + 3 * n_dil], rest[-1]

        def part(r0, first, width):
            return lax.dot_general(a_ref[r0:r0 + chunk, :], w_ref[first:first + width, :], NT_DIMS,
                                   preferred_element_type=F32)

        for r0 in range(0, tm, chunk):
            na_ref[r0:r0 + chunk, :] = part(r0, 0, na_w).astype(BF16)
            dil_part = part(r0, na_w, dil_w)
            cosv, sinv = cos_ref[r0:r0 + chunk, :], sin_ref[r0:r0 + chunk, :]
            for t in range(3):
                for gi, dil in enumerate(DIL_DILATIONS):
                    c0 = (t * n_dil + gi) * 256
                    val = dil_part[:, c0:c0 + 256]
                    if t < 2:
                        val = val * cosv + _swap_halves(val) * sinv
                    _store_dil_order(val, outs[t * n_dil + gi], scr, dil, r0)
            gate_ref[r0:r0 + chunk, :] = _sigmoid(part(r0, na_w + dil_w, 2 * D_MODEL)).astype(BF16)

    out_specs = [_rows(tm, na_w), _rows(tm, 2 * D_MODEL)]
    out_shape = [_sds((n, na_w), BF16), _sds((n, 2 * D_MODEL), BF16)]
    for _ in range(3):
        for dil in DIL_DILATIONS:
            out_specs.append(pl.BlockSpec((dil, tm // dil, 256), lambda i: (0, i, 0)))
            out_shape.append(_sds((dil, n // dil, 256), BF16))
    res = pl.pallas_call(
        body, name=name, grid=(n // tm,),
        in_specs=[_rows(tm, D_MODEL), _const(w_t.shape), _rows(tm, 256), _rows(tm, 256)]
                 + [pl.BlockSpec(memory_space=pl.ANY)] * len(extra),
        out_specs=out_specs, out_shape=out_shape,
        scratch_shapes=[pltpu.VMEM((2, chunk, 128), F32)],
        compiler_params=_params("parallel"),
    )(a, w_t, cos_t, sin_t, *extra)
    return res[0], res[1], res[2:5], res[5:8], res[8:11]


def _residual_rms_tile(delta, h, g):
    hn = h + delta
    return hn, hn * _rstd(hn) * g


def _gate_mix_tile(b2, s1, b1, s2):
    return b2, s1.astype(F32) * b1.astype(F32) + s2.astype(F32) * b2


def _gate_bwd_tile(dm, s1, b1, s2, b2):
    s1, b1, s2, b2 = (t.astype(F32) for t in (s1, b1, s2, b2))
    return dm * s1, dm * s2, dm * b1 * s1 * (1.0 - s1), dm * b2 * s2 * (1.0 - s2)


def _tail_tile(gt, pp, h2, target, g):
    sg = _sigmoid(gt)
    h3 = h2 + sg * pp
    r3 = _rstd(h3)
    n3 = h3 * r3
    err = n3 * g - target
    loss = 0.5 * jnp.sum(jnp.sum(err * err, axis=-1, keepdims=True) / D_MODEL)
    dy = err / D_MODEL
    dn = dy * g
    dh3 = r3 * (dn - n3 * jnp.mean(dn * n3, axis=-1, keepdims=True))
    return (dh3, dh3 * sg, dh3 * pp * sg * (1.0 - sg),
            jnp.sum(dy * n3, axis=0, keepdims=True), jnp.full((1, gt.shape[1]), loss, F32))


def _rms_bwd_tile(dz, h, g, dres):
    r = _rstd(h)
    nrm = h * r
    dn = dz * g
    dh = dres + r * (dn - nrm * jnp.mean(dn * nrm, axis=-1, keepdims=True))
    return dh, jnp.sum(dz * nrm, axis=0, keepdims=True)


def _rms_bwd_twice(dz, h, g, dres):
    dh, dg = _rms_bwd_tile(dz, h, g, dres)
    return dh, dh, dg


def _tail_step(f, w_down, h1, w_pg, p, w_pp, target, g_final, g_ple, *, tm, name):
    n = f.shape[0]
    chunk = min(EPILOGUE_ROWS, tm)

    def body(f_ref, wd_ref, h1_ref, wg_ref, p_ref, wp_ref, t_ref, gf_ref, gp_ref,
             e_ref, dpp_ref, dgt_ref, dh2_ref, dh2b_ref, dgf_ref, loss_ref, dgp_ref):
        sums = None
        for r0 in range(0, tm, chunk):
            rows = slice(r0, r0 + chunk)
            delta = jnp.dot(f_ref[rows, :], wd_ref[...], preferred_element_type=F32)
            h2, e = _residual_rms_tile(delta, h1_ref[rows, :], gp_ref[...])
            e = e.astype(BF16)
            e_ref[rows, :] = e
            gt = jnp.dot(e, wg_ref[...], preferred_element_type=F32)
            pp = lax.dot_general(p_ref[rows, :], wp_ref[...], NT_DIMS, preferred_element_type=F32)
            dh3, dpp, dgt, dgf, loss = _tail_tile(gt, pp, h2, t_ref[rows, :], gf_ref[...])
            dgt = dgt.astype(BF16)
            dpp_ref[rows, :] = dpp.astype(BF16)
            dgt_ref[rows, :] = dgt
            dz = lax.dot_general(dgt, wg_ref[...], NT_DIMS, preferred_element_type=F32)
            dh2, dgp = _rms_bwd_tile(dz, h2, gp_ref[...], dh3)
            dh2_ref[rows, :] = dh2
            dh2b_ref[rows, :] = dh2.astype(BF16)
            vals = (dgf, loss, dgp)
            sums = vals if sums is None else [s + v for s, v in zip(sums, vals)]
        _add_colsums((dgf_ref, loss_ref, dgp_ref), sums, pl.program_id(0))

    wide, gain = _rows(tm, D_MODEL), _const((1, D_MODEL))
    return pl.pallas_call(
        body, name=name, grid=(n // tm,),
        in_specs=[_rows(tm, f.shape[1]), _const(w_down.shape), wide, _const(w_pg.shape),
                  _rows(tm, p.shape[1]), _const(w_pp.shape), wide, gain, gain],
        out_specs=[wide] * 5 + [gain] * 3,
        out_shape=[_sds((n, D_MODEL), dt) for dt in (BF16, BF16, BF16, F32, BF16)]
                  + [_sds((1, D_MODEL), F32)] * 3,
        compiler_params=_params("arbitrary"),
    )(f, w_down, h1, w_pg, p, w_pp, target, g_final, g_ple)


def _assemble_dproj(dna, ddil_q, ddil_k, ddil_v, dgn, dgd, cos_t, sin_t, *, tm, name):
    n = dgn.shape[0]

    def body(*refs):
        dq_ref, dk_ref, dv_ref = refs[0:3]
        dil_in = refs[3:12]
        dgn_ref, dgd_ref, cos_ref, sin_ref, o_ref, scr = refs[12:18]
        o_ref[:, 0:512] = dq_ref[...]
        o_ref[:, 512:1024] = dk_ref[...].astype(BF16)
        o_ref[:, 1024:1536] = dv_ref[...].astype(BF16)
        cosv, sinv = cos_ref[...], sin_ref[...]
        for t in range(3):
            for gi, dil in enumerate(DIL_DILATIONS):
                val = _load_token_order(dil_in[t * 3 + gi], scr, dil, tm)
                if t < 2:
                    val = val * cosv + _swap_halves(val * sinv)
                c0 = 1536 + t * DIL_WIDTH + gi * 256
                o_ref[:, c0:c0 + 256] = val.astype(BF16)
        o_ref[:, 3840:4864] = dgn_ref[...]
        o_ref[:, 4864:5888] = dgd_ref[...]

    in_specs = [_rows(tm, NA_WIDTH)] * 3
    for _ in range(3):
        for dil in DIL_DILATIONS:
            in_specs.append(pl.BlockSpec((dil, tm // dil, 256), lambda i: (0, i, 0)))
    in_specs += [_rows(tm, D_MODEL)] * 2 + [_rows(tm, 256)] * 2
    return pl.pallas_call(
        body, name=name, grid=(n // tm,), in_specs=in_specs,
        out_specs=_rows(tm, IN_WIDTH), out_shape=_sds((n, IN_WIDTH), BF16),
        scratch_shapes=[_dil_scratch(tm)],
        compiler_params=_params("parallel"),
    )(*dna, *ddil_q, *ddil_k, *ddil_v, dgn, dgd, cos_t, sin_t)


N_ROW_OFF = 2 * NA_WIN_ROWS - 1
N_PAIRS = N_ROW_OFF - 1
RB_WIDTH = (N_ROW_OFF + 1) * GRID_W


def _na_bias(rb_ref, pair_scr):
    shape = (GRID_W, RB_WIDTH)
    qc = lax.broadcasted_iota(jnp.int32, shape, 0)
    qc2 = lax.broadcasted_iota(jnp.int32, (GRID_W, 128), 0)
    kc2 = lax.broadcasted_iota(jnp.int32, (GRID_W, 128), 1) & (GRID_W - 1)
    cs = jnp.clip(qc2 - 8, 0, GRID_W - 16)
    valid = (kc2 >= cs) & (kc2 < cs + 16)
    for hh in range(2):
        t = jnp.broadcast_to(rb_ref[hh], shape)
        t = pltpu.roll(t, RB_WIDTH - 15, 1)
        for b in range(6):
            t = jnp.where(((qc >> b) & 1) == 1, pltpu.roll(t, 1 << b, 1), t)
        t_odd = pltpu.roll(t, RB_WIDTH - GRID_W, 1)
        for ro in range(N_PAIRS):
            src = t if ro % 2 == 0 else t_odd
            base = (ro // 2) * 128
            pair_scr[hh, ro] = jnp.where(valid, src[:, base:base + 128], NEG_INF)


NA_GROUP_FWD = 8
NA_GROUP_BWD = 4


def _stack_heads(ref, r, scale=1.0):
    lane = lax.broadcasted_iota(jnp.int32, (GRID_W, 128), 1)
    t = ref[pl.ds(pl.multiple_of(r * GRID_W, GRID_W), GRID_W), :].astype(F32) * scale
    return jnp.concatenate([jnp.where(lane < 64, t, 0.0), jnp.where(lane >= 64, t, 0.0)], axis=0).astype(BF16)


def _unstack_heads(t2):
    lane = lax.broadcasted_iota(jnp.int32, (GRID_W, 128), 1)
    return jnp.where(lane < 64, t2[:GRID_W], t2[GRID_W:])


def _na_window(k_ref, v_ref, r, n_rows):
    rs = jnp.clip(r - NA_WIN_ROWS // 2, 0, n_rows - NA_WIN_ROWS)
    ro0 = (NA_WIN_ROWS - 1) - (r - rs)
    off = pl.multiple_of(rs * GRID_W, GRID_W)
    kw = k_ref[pl.ds(off, NA_WIN_ROWS * GRID_W), :]
    vw = v_ref[pl.ds(off, NA_WIN_ROWS * GRID_W), :]
    return kw, vw, off, ro0


def _na_probs(s_raw, pair_scr, ro0):
    bias = [jnp.concatenate([pair_scr[hh, ro0 + 2 * j] for j in range(NA_WIN_ROWS // 2)], axis=1)
            for hh in range(2)]
    s = s_raw + jnp.concatenate(bias, axis=0)
    m = jnp.max(s, axis=-1, keepdims=True)
    e = jnp.exp(s - m)
    return e * (1.0 / jnp.sum(e, axis=-1, keepdims=True))


def _na_qkv_specs(n):
    pairs = NA_WIDTH // 128
    return [pl.BlockSpec((n, 128), lambda h, first=t * pairs: (0, first + h)) for t in range(3)]


def _na_fwd(qkv, rb, *, name):
    n = qkv.shape[0]
    n_rows = n // GRID_W

    def body(ins, outs, scr):
        q_ref, k_ref, v_ref, rb_ref = ins
        o_ref, = outs
        pair_scr, = scr
        _na_bias(rb_ref, pair_scr)

        def group(g, carry):
            rows = [g * NA_GROUP_FWD + t for t in range(NA_GROUP_FWD)]
            wins = [_na_window(k_ref, v_ref, r, n_rows) for r in rows]
            raw = [lax.dot_general(_stack_heads(q_ref, r, QK_SCALE), w[0], NT_DIMS, preferred_element_type=F32)
                   for r, w in zip(rows, wins)]
            probs = [_na_probs(s, pair_scr, w[3]) for s, w in zip(raw, wins)]
            outs2 = [jnp.dot(p.astype(BF16), w[1], preferred_element_type=F32) for p, w in zip(probs, wins)]
            for r, o2 in zip(rows, outs2):
                o_ref[pl.ds(pl.multiple_of(r * GRID_W, GRID_W), GRID_W), :] = _unstack_heads(o2).astype(BF16)
            return carry

        lax.fori_loop(0, n_rows // NA_GROUP_FWD, group, 0)

    col = pl.BlockSpec((n, 128), lambda h: (0, h))
    return _call(
        body, name=name, grid=(NA_WIDTH // 128,),
        in_specs=_na_qkv_specs(n) + [pl.BlockSpec((2, 1, RB_WIDTH), lambda h: (h, 0, 0))],
        out_specs=[col], out_shape=[_sds((n, NA_WIDTH), BF16)],
        scratch_shapes=[pltpu.VMEM((2, N_PAIRS, GRID_W, 128), F32)],
        args=(qkv, qkv, qkv, rb))[0]


def _na_bwd(qkv, do, rb, *, name, after=None):
    n = qkv.shape[0]
    n_rows = n // GRID_W
    win = NA_WIN_ROWS * GRID_W

    def body(ins, outs, scr):
        q_ref, k_ref, v_ref, do_ref, rb_ref = ins
        dq_ref, dk_ref, dv_ref, drb_ref = outs
        pair_scr, acc_scr = scr
        _na_bias(rb_ref, pair_scr)
        acc_scr[...] = jnp.zeros_like(acc_scr)
        dk_ref[...] = jnp.zeros_like(dk_ref)
        dv_ref[...] = jnp.zeros_like(dv_ref)

        def group(g, carry):
            rows = [g * NA_GROUP_BWD + t for t in range(NA_GROUP_BWD)]
            wins = [_na_window(k_ref, v_ref, r, n_rows) for r in rows]
            qss = [_stack_heads(q_ref, r, QK_SCALE) for r in rows]
            doss = [_stack_heads(do_ref, r) for r in rows]
            raw = [lax.dot_general(qs, w[0], NT_DIMS, preferred_element_type=F32) for qs, w in zip(qss, wins)]
            dps = [lax.dot_general(dos, w[1], NT_DIMS, preferred_element_type=F32) for dos, w in zip(doss, wins)]
            probs = [_na_probs(s, pair_scr, w[3]) for s, w in zip(raw, wins)]
            dss = [p * (dp - jnp.sum(p * dp, axis=-1, keepdims=True)) for p, dp in zip(probs, dps)]
            dsbs = [ds.astype(BF16) for ds in dss]
            dq2s = [jnp.dot(dsb, w[0], preferred_element_type=F32) for dsb, w in zip(dsbs, wins)]
            dkws = [lax.dot_general(dsb, qs, TN_DIMS, preferred_element_type=F32) for dsb, qs in zip(dsbs, qss)]
            dvws = [lax.dot_general(p.astype(BF16), dos, TN_DIMS, preferred_element_type=F32)
                    for p, dos in zip(probs, doss)]
            for t, r in enumerate(rows):
                _, _, off, ro0 = wins[t]
                for hh in range(2):
                    for j in range(NA_WIN_ROWS // 2):
                        acc_scr[hh, ro0 + 2 * j] += dss[t][hh * GRID_W:(hh + 1) * GRID_W, j * 128:(j + 1) * 128]
                dq_ref[pl.ds(pl.multiple_of(r * GRID_W, GRID_W), GRID_W), :] = (
                    _unstack_heads(dq2s[t]) * QK_SCALE).astype(BF16)
                dk_ref[pl.ds(off, win), :] += dkws[t]
                dv_ref[pl.ds(off, win), :] += dvws[t]
            return carry

        lax.fori_loop(0, n_rows // NA_GROUP_BWD, group, 0)

        qc = lax.broadcasted_iota(jnp.int32, (N_PAIRS * GRID_W, 128), 0)
        for hh in range(2):
            t = acc_scr[hh].reshape(N_PAIRS * GRID_W, 128)
            for b in range(6):
                t = jnp.where(((qc >> b) & 1) == 1, pltpu.roll(t, 128 - (1 << b), 1), t)
            t = pltpu.roll(t, 15, 1)
            drb_ref[hh] = jnp.sum(t.reshape(N_PAIRS, GRID_W, 128), axis=1)

    col = pl.BlockSpec((n, 128), lambda h: (0, h))
    return _call(
        body, name=name, grid=(NA_WIDTH // 128,),
        in_specs=_na_qkv_specs(n) + [col, pl.BlockSpec((2, 1, RB_WIDTH), lambda h: (h, 0, 0))],
        out_specs=[col, col, col, pl.BlockSpec((2, N_PAIRS, 128), lambda h: (h, 0, 0))],
        out_shape=[_sds((n, NA_WIDTH), BF16), _sds((n, NA_WIDTH), F32), _sds((n, NA_WIDTH), F32),
                   _sds((8, N_PAIRS, 128), F32)],
        scratch_shapes=[pltpu.VMEM((2, N_PAIRS, GRID_W, 128), F32),
                        pltpu.VMEM((2, N_PAIRS, GRID_W, 128), F32)],
        args=(qkv, qkv, qkv, do, rb), after=after)


def _rpb_table(rpb2):
    t = jnp.pad(rpb2, ((0, 0), (0, 1), (0, GRID_W - rpb2.shape[-1])))
    return t.reshape(8, 1, RB_WIDTH)


def _rpb_grad(drb, *, name):
    kdim = drb.shape[1]

    def body(x_ref, o_ref):
        kk = lax.broadcasted_iota(jnp.int32, (128, 512), 0)
        jj = lax.broadcasted_iota(jnp.int32, (128, 512), 1)
        half, co = kk >> 6, kk & 63
        acc = jnp.zeros((8, 512), F32)
        for ro in range(N_PAIRS):
            hit = ((ro + half) == (jj >> 5)) & (co == (jj & 31)) & (co < 31)
            onehot = jnp.where(hit, 1.0, 0.0).astype(F32)
            acc = acc + jnp.dot(x_ref[:, ro * 128:(ro + 1) * 128], onehot, preferred_element_type=F32,
                                precision=lax.Precision.HIGHEST)
        o_ref[...] = acc

    return pl.pallas_call(
        body, name=name, grid=(1,),
        in_specs=[_const((8, kdim))], out_specs=_const((8, 512)), out_shape=_sds((8, 512), F32),
        compiler_params=_params("arbitrary"),
    )(drb)


DIL_GROUP = 4


def _dil_blocks(length):
    qb = min(128, length)
    return qb, min(qb + 2 * DIL_RADIUS, length), min(DIL_GROUP, length // qb)


def _stack_lanes(ref, t, qb, scale=1.0):
    lane = lax.broadcasted_iota(jnp.int32, (qb, 256), 1)
    val = ref[0, t * qb:(t + 1) * qb, :].astype(F32) * scale
    return jnp.concatenate([jnp.where((lane >> 6) == h, val, 0.0) for h in range(4)], axis=0).astype(BF16)


def _dil_window(k_ref, v_ref, blk, qb, win, length):
    start = pl.multiple_of(jnp.clip(blk * qb - DIL_RADIUS, 0, length - win), DIL_RADIUS)
    return k_ref[0, pl.ds(start, win), :], v_ref[0, pl.ds(start, win), :], start


def _dil_caps_init(caps_scr, qb, win):
    @pl.when((pl.program_id(0) == 0) & (pl.program_id(1) == 0))
    def _():
        gap = ((lax.broadcasted_iota(jnp.int32, (4 * qb, win), 0) & (qb - 1))
               - lax.broadcasted_iota(jnp.int32, (4 * qb, win), 1))
        for v in range(3):
            caps_scr[v] = jnp.where(jnp.abs(gap + v * DIL_RADIUS) <= DIL_RADIUS, jnp.inf, NEG_INF)


def _dil_mask(s, blk, start, qb, caps_scr):
    return jnp.minimum(s, caps_scr[(blk * qb - start) // DIL_RADIUS])


def _pick_heads(stacked, qb):
    lane = lax.broadcasted_iota(jnp.int32, (qb, 256), 1)
    out = jnp.zeros((qb, 256), stacked.dtype)
    for h in range(4):
        out = jnp.where((lane >> 6) == h, stacked[h * qb:(h + 1) * qb], out)
    return out


def _stack_head_cols(ref, t, qb):
    return jnp.concatenate([ref[0, t * qb:(t + 1) * qb, 64 * h:64 * h + 1] for h in range(4)], axis=0)


def _dil_fwd(q, k, v, *, name, after=None):
    dil, length, _ = q.shape
    qb, win, grp = _dil_blocks(length)
    extra = [] if after is None else [after]

    def body(q_ref, k_ref, v_ref, *rest):
        o_ref, lse_ref, caps_scr = rest[-3:]
        _dil_caps_init(caps_scr, qb, win)
        blks = [pl.program_id(1) * grp + t for t in range(grp)]
        wins = [_dil_window(k_ref, v_ref, b, qb, win, length) for b in blks]
        raw = [lax.dot_general(_stack_lanes(q_ref, t, qb, QK_SCALE), w[0], NT_DIMS, preferred_element_type=F32)
               for t, w in enumerate(wins)]
        lses, outs = [], []
        for t, (s, w) in enumerate(zip(raw, wins)):
            s = _dil_mask(s, blks[t], w[2], qb, caps_scr)
            m = jnp.max(s, axis=-1, keepdims=True)
            e = jnp.exp(s - m)
            norm = jnp.sum(e, axis=-1, keepdims=True)
            lses.append(m + jnp.log(norm))
            outs.append(jnp.dot((e * (1.0 / norm)).astype(BF16), w[1], preferred_element_type=F32))
        for t in range(grp):
            o_ref[0, t * qb:(t + 1) * qb, :] = _pick_heads(outs[t], qb)
            lse_ref[0, t * qb:(t + 1) * qb, :] = _pick_heads(jnp.broadcast_to(lses[t], (4 * qb, 256)), qb)

    seq = pl.BlockSpec((1, length, 256), lambda j, i: (j, 0, 0))
    blk = pl.BlockSpec((1, grp * qb, 256), lambda j, i: (j, i, 0))
    return pl.pallas_call(
        body, name=name, grid=(dil, length // (grp * qb)),
        in_specs=[blk, seq, seq] + [pl.BlockSpec(memory_space=pl.ANY)] * len(extra), out_specs=[blk, blk],
        out_shape=[_sds((dil, length, 256), F32)] * 2,
        scratch_shapes=[pltpu.VMEM((3, 4 * qb, win), F32)],
        compiler_params=_params("arbitrary", "arbitrary"),
    )(q, k, v, *extra)


def _dil_bwd(q, k, v, do, lse, cc, *, name):
    dil, length, _ = q.shape
    qb, win, grp = _dil_blocks(length)

    def body(q_ref, k_ref, v_ref, do_ref, lse_ref, cc_ref, dq_ref, dk_ref, dv_ref, caps_scr):
        _dil_caps_init(caps_scr, qb, win)

        @pl.when(pl.program_id(1) == 0)
        def _():
            dk_ref[...] = jnp.zeros_like(dk_ref)
            dv_ref[...] = jnp.zeros_like(dv_ref)

        blks = [pl.program_id(1) * grp + t for t in range(grp)]
        wins = [_dil_window(k_ref, v_ref, b, qb, win, length) for b in blks]
        qss = [_stack_lanes(q_ref, t, qb, QK_SCALE) for t in range(grp)]
        doss = [_stack_lanes(do_ref, t, qb) for t in range(grp)]
        raw = [lax.dot_general(qs, w[0], NT_DIMS, preferred_element_type=F32) for qs, w in zip(qss, wins)]
        dps = [lax.dot_general(dos, w[1], NT_DIMS, preferred_element_type=F32) for dos, w in zip(doss, wins)]
        probs = [jnp.exp(_dil_mask(s, blks[t], wins[t][2], qb, caps_scr) - _stack_head_cols(lse_ref, t, qb))
                 for t, s in enumerate(raw)]
        dsbs = [(p * (dp + _stack_head_cols(cc_ref, t, qb))).astype(BF16)
                for t, (p, dp) in enumerate(zip(probs, dps))]
        dq4s = [jnp.dot(dsb, w[0], preferred_element_type=F32) for dsb, w in zip(dsbs, wins)]
        dkws = [lax.dot_general(dsb, qs, TN_DIMS, preferred_element_type=F32) for dsb, qs in zip(dsbs, qss)]
        dvws = [lax.dot_general(p.astype(BF16), dos, TN_DIMS, preferred_element_type=F32)
                for p, dos in zip(probs, doss)]
        for t in range(grp):
            dq_ref[0, t * qb:(t + 1) * qb, :] = _pick_heads(dq4s[t], qb) * QK_SCALE
            dk_ref[0, pl.ds(wins[t][2], win), :] += dkws[t]
            dv_ref[0, pl.ds(wins[t][2], win), :] += dvws[t]

    seq = pl.BlockSpec((1, length, 256), lambda j, i: (j, 0, 0))
    blk = pl.BlockSpec((1, grp * qb, 256), lambda j, i: (j, i, 0))
    return pl.pallas_call(
        body, name=name, grid=(dil, length // (grp * qb)),
        in_specs=[blk, seq, seq, blk, blk, blk], out_specs=[blk, seq, seq],
        out_shape=[_sds((dil, length, 256), F32)] * 3,
        scratch_shapes=[pltpu.VMEM((3, 4 * qb, win), F32)],
        compiler_params=_params("arbitrary", "arbitrary"),
    )(q, k, v, do, lse, cc)


def _merge_weights(lses):
    m = jnp.maximum(jnp.maximum(lses[0], lses[1]), lses[2])
    es = [jnp.exp(t - m) for t in lses]
    inv = 1.0 / (es[0] + es[1] + es[2])
    return [e * inv for e in es]


def _branch_mix(y_na, w_bna, outs, lses, w_bd, gates, *, tm, name):
    n = y_na.shape[0]
    chunk = min(EPILOGUE_ROWS, tm)

    def body(yna_ref, wn_ref, *rest):
        o_in, l_in = rest[0:3], rest[3:6]
        wd_ref, sn_ref, sd_ref = rest[6:9]
        y_ref, yb_ref, bn_ref, bd_ref, mix_ref, scr = rest[9:15]
        for r0 in range(0, tm, chunk):
            rows = slice(r0, r0 + chunk)
            lv = [_load_token_order(l_in[g], scr, d, chunk, r0) for g, d in enumerate(DIL_DILATIONS)]
            ws = _merge_weights(lv)
            y = jnp.zeros((chunk, 256), F32)
            for g, d in enumerate(DIL_DILATIONS):
                y = y + ws[g] * _load_token_order(o_in[g], scr, d, chunk, r0)
            yb = y.astype(BF16)
            y_ref[rows, :] = y
            yb_ref[rows, :] = yb
            bn = lax.dot_general(yna_ref[rows, :], wn_ref[...], NT_DIMS, preferred_element_type=F32).astype(BF16)
            bd = lax.dot_general(yb, wd_ref[...], NT_DIMS, preferred_element_type=F32)
            bn_ref[rows, :] = bn
            bd, mixed = _gate_mix_tile(bd, sn_ref[rows, :], bn, sd_ref[rows, :])
            bd_ref[rows, :] = bd.astype(BF16)
            mix_ref[rows, :] = mixed.astype(BF16)

    specs = [_dil_spec(d, tm) for d in DIL_DILATIONS]
    return pl.pallas_call(
        body, name=name, grid=(n // tm,),
        in_specs=[_rows(tm, NA_WIDTH), _const(w_bna.shape)] + specs + specs
                 + [_const(w_bd.shape), _rows(tm, D_MODEL, 0), _rows(tm, D_MODEL, 1)],
        out_specs=[_rows(tm, 256)] * 2 + [_rows(tm, D_MODEL)] * 3,
        out_shape=[_sds((n, 256), F32), _sds((n, 256), BF16)] + [_sds((n, D_MODEL), BF16)] * 3,
        scratch_shapes=[_dil_scratch(chunk)],
        compiler_params=_params("parallel"),
    )(y_na, w_bna, *outs, *lses, w_bd, gates, gates)


def _branch_bwd(dh, w_out, gates, bn, bd, w_bna, w_bd, y, lses, *, tm, name):
    n = dh.shape[0]
    chunk = min(EPILOGUE_ROWS, tm)

    def body(dh_ref, wo_ref, sn_ref, sd_ref, bn_ref, bd_ref, wn_ref, wd_ref, y_ref, *rest):
        l_in = rest[0:3]
        dbn_ref, dbd_ref, dgn_ref, dgd_ref, dyna_ref = rest[3:8]
        do_out, cc_out, scr = rest[8:11], rest[11:14], rest[14]
        rr = lax.broadcasted_iota(jnp.int32, (256, 256), 0) >> 6
        cc = lax.broadcasted_iota(jnp.int32, (256, 256), 1) >> 6
        ones = jnp.where(rr == cc, 1.0, 0.0).astype(F32)
        for r0 in range(0, tm, chunk):
            rows = slice(r0, r0 + chunk)
            dm = lax.dot_general(dh_ref[rows, :], wo_ref[...], NT_DIMS, preferred_element_type=F32)
            dbn, dbd, dgn, dgd = (t.astype(BF16) for t in _gate_bwd_tile(
                dm, sn_ref[rows, :], bn_ref[rows, :], sd_ref[rows, :], bd_ref[rows, :]))
            dbn_ref[rows, :] = dbn
            dbd_ref[rows, :] = dbd
            dgn_ref[rows, :] = dgn
            dgd_ref[rows, :] = dgd
            dyna_ref[rows, :] = jnp.dot(dbn, wn_ref[...], preferred_element_type=F32).astype(BF16)
            dyv = jnp.dot(dbd, wd_ref[...], preferred_element_type=F32)
            lv = [_load_token_order(l_in[g], scr, d, chunk, r0) for g, d in enumerate(DIL_DILATIONS)]
            ws = _merge_weights(lv)
            tsum = jnp.dot(dyv * y_ref[rows, :], ones, preferred_element_type=F32,
                           precision=lax.Precision.HIGHEST)
            for g, d in enumerate(DIL_DILATIONS):
                _store_dil_order(ws[g] * dyv, do_out[g], scr, d, r0)
                _store_dil_order(-ws[g] * tsum, cc_out[g], scr, d, r0)

    specs = [_dil_spec(d, tm) for d in DIL_DILATIONS]
    wide = _rows(tm, D_MODEL)
    res = pl.pallas_call(
        body, name=name, grid=(n // tm,),
        in_specs=[wide, _const(w_out.shape), _rows(tm, D_MODEL, 0), _rows(tm, D_MODEL, 1), wide, wide,
                  _const(w_bna.shape), _const(w_bd.shape), _rows(tm, 256)] + specs,
        out_specs=[wide] * 4 + [_rows(tm, NA_WIDTH)] + specs + specs,
        out_shape=[_sds((n, D_MODEL), BF16)] * 4 + [_sds((n, NA_WIDTH), BF16)]
                  + [_sds((d, n // d, 256), BF16) for d in DIL_DILATIONS]
                  + [_sds((d, n // d, 256), F32) for d in DIL_DILATIONS],
        scratch_shapes=[_dil_scratch(chunk)],
        compiler_params=_params("parallel"),
    )(dh, w_out, gates, gates, bn, bd, w_bna, w_bd, y, *lses)
    return res[0], res[1], res[2], res[3], res[4], res[5:8], res[8:11]


_WEIGHTS = (("w_in", 1, 736), ("w_branch_na", 1, 128), ("w_branch_dil", 1, 128), ("w_out", 0, 128),
            ("w_up", 1, 512), ("w_down", 0, 512), ("w_ple_gate", 0, 128), ("w_ple_proj", 1, 128))
_W_IN, _W_BNA, _W_BD, _W_OUT, _W_UP, _W_DOWN, _W_PG, _W_PP = range(8)


def _to_full(gathered):
    return gathered.reshape(-1, gathered.shape[2])


def _to_chunks(widx, mat):
    return mat.reshape(N_DEV, _WEIGHTS[widx][2], mat.shape[1])


def _local_step(x, p_bf16, positions, target, g_mix, g_mlp, g_ple, g_final, rpb2,
                get_w_in, relay_rest, get_rest, send_grads):
    tm = 512
    half = HEAD_DIM // 2
    inv_freq = 10000.0 ** (-jnp.arange(half, dtype=F32) / half)
    ang = positions.astype(F32)[:, None] * inv_freq
    cos, sin = jnp.cos(ang), jnp.sin(ang)
    cos_t = jnp.tile(jnp.concatenate([cos, cos], axis=-1), (1, 4))
    sin_t = jnp.tile(jnp.concatenate([-sin, sin], axis=-1), (1, 4))
    rb = _rpb_table(rpb2)

    a = _rms_fwd(x, g_mix, tm=tm, name="rms_mix")
    w_in, token = get_w_in((a, cos_t, sin_t, p_bf16))
    na_qkv, gates, dq_g, dk_g, dv_g = _project_in(a, w_in, cos_t, sin_t, tm=512, name="mm_in", after=token)
    y_na = _na_fwd(na_qkv, rb, name="na_fwd")
    token = relay_rest(y_na)
    d_out, d_lse = [], []
    for g in range(3):
        o, lse = _dil_fwd(dq_g[g], dk_g[g], dv_g[g], name=f"dil_fwd{g}", after=token if g == 0 else None)
        d_out.append(o)
        d_lse.append(lse)
    w_bna, w_bd = get_rest(d_out[2], 0)
    y_dil, y_dil_b, bn, bd, mixed = _branch_mix(y_na, w_bna, d_out, d_lse, w_bd, gates, tm=tm, name="branch_mix")
    w_out, w_up, w_down, w_pg, w_pp = get_rest(mixed, 1)
    h1, c = _matmul(mixed, w_out, out_dtype=(F32, BF16), tm=512, tn=1024, tk=1024, name="mm_out",
                    extra=(x, g_mlp), epilogue=_residual_rms_tile)
    u, f = _matmul(c, w_up, tb=True, out_dtype=(BF16, BF16), tm=512, tn=2048, tk=1024, name="mm_up",
                   epilogue=lambda acc: (acc, jnp.square(jnp.maximum(acc, 0.0))))

    e, dpp, dgt, dh2, dh2_b, dg_final, loss, dg_ple = _tail_step(
        f, w_down, h1, w_pg, p_bf16, w_pp, target, g_final, g_ple, tm=256, name="tail_step")
    loss = loss[:, :128]
    gw_pp = _matmul(p_bf16, dpp, ta=True, transpose_out=True, out_dtype=BF16, tm=256, tn=1024, tk=2048,
                    name="mm_gw_pp")
    gw_pg = _matmul(e, dgt, ta=True, out_dtype=BF16, tm=512, tn=1024, tk=2048, name="mm_gw_pg")
    du = _matmul(dh2_b, w_down, tb=True, out_dtype=BF16, tm=512, tn=2048, tk=1024, name="mm_du",
                 extra=(u,), epilogue=lambda acc, uv: (acc * (2.0 * jnp.maximum(uv.astype(F32), 0.0)),))
    gw_down = _matmul(f, dh2_b, ta=True, out_dtype=BF16, tm=1024, tn=1024, tk=2048, name="mm_gw_down")
    gw_up = _matmul(c, du, ta=True, transpose_out=True, out_dtype=BF16, tm=512, tn=2048, tk=2048, name="mm_gw_up")
    dh1, dh1_b, dg_mlp = _matmul(
        du, w_up, out_dtype=(F32, BF16), tm=512, tn=1024, tk=4096, name="mm_dc",
        extra=(h1, g_mlp, dh2), epilogue=_rms_bwd_twice, n_colsum=1)
    dbn, dbd, dgn, dgd, dy_na, do_g, cc_g = _branch_bwd(dh1_b, w_out, gates, bn, bd, w_bna, w_bd, y_dil, d_lse,
                                                        tm=tm, name="branch_bwd")
    gw_out = _matmul(mixed, dh1_b, ta=True, out_dtype=BF16, tm=512, tn=1024, tk=2048, name="mm_gw_out")
    gw_bna = _matmul(y_na, dbn, ta=True, transpose_out=True, out_dtype=BF16, tm=512, tn=1024, tk=2048,
                     name="mm_gw_bna")
    gw_bd = _matmul(y_dil_b, dbd, ta=True, transpose_out=True, out_dtype=BF16, tm=256, tn=1024, tk=2048,
                    name="mm_gw_bd")
    token = send_grads((_W_PP, _W_PG, _W_DOWN, _W_UP, _W_OUT, _W_BNA, _W_BD),
                       (gw_pp, gw_pg, gw_down, gw_up, gw_out, gw_bna, gw_bd))
    dna = _na_bwd(na_qkv, dy_na, rb, name="na_bwd", after=token)
    drpb = _rpb_grad(dna[3].reshape(8, -1), name="rpb_grad")
    ddq, ddk, ddv = [], [], []
    for g in range(3):
        r = _dil_bwd(dq_g[g], dk_g[g], dv_g[g], do_g[g], d_lse[g], cc_g[g], name=f"dil_bwd{g}")
        ddq.append(r[0])
        ddk.append(r[1])
        ddv.append(r[2])
    dproj = _assemble_dproj(dna[0:3], ddq, ddk, ddv, dgn, dgd, cos_t, sin_t, tm=tm, name="assemble_dproj")
    gw_in = _matmul(a, dproj, ta=True, transpose_out=True, out_dtype=BF16, tm=512, tn=2944, tk=2048, name="mm_gw_in")
    token = send_grads((_W_IN,), (gw_in,))
    dx, dg_mix = _matmul(
        dproj, w_in, out_dtype=(F32,), tm=512, tn=1024, tk=5888, name="mm_da", after=token,
        extra=(x, g_mix, dh1), epilogue=_rms_bwd_tile, n_colsum=1)
    return loss, dx, (dg_mix, dg_mlp, dg_ple, dg_final), drpb


def _cast_bf16(t, *, name):
    def body(t_ref, o_ref):
        o_ref[...] = t_ref[...].astype(BF16)

    rows, cols = t.shape
    tr = 256 if rows % 256 == 0 else rows
    blk = pl.BlockSpec((tr, cols), lambda i: (i, 0))
    return pl.pallas_call(body, name=name, grid=(rows // tr,), in_specs=[blk], out_specs=blk,
                          out_shape=_sds(t.shape, BF16), compiler_params=_params("parallel"))(t)


def _adamw(w, g, m, v):
    m = ADAM_B1 * m + (1.0 - ADAM_B1) * g
    v = ADAM_B2 * v + (1.0 - ADAM_B2) * (g * g)
    m_hat = m / (1.0 - ADAM_B1 ** ADAM_STEP)
    v_hat = v / (1.0 - ADAM_B2 ** ADAM_STEP)
    delta = -ADAM_LR * (m_hat / (jnp.sqrt(v_hat) + ADAM_EPS) + ADAM_WD * w)
    return delta, m, v


def _sum_adamw(parts, w, m, v, *, tr, name, own=None, transposed=False):
    rows, cols = w.shape
    n_pre = 0 if own is None else 1

    def body(*refs):
        p_ref, w_ref, m_ref, v_ref = refs[n_pre:n_pre + 4]
        g_ref, d_ref, nm_ref, nv_ref = refs[-4:]
        g = (p_ref[0] if own is None else refs[n_pre + 4][...]).astype(F32)
        for s in range(1, N_DEV):
            g = g + p_ref[s].astype(F32)
        if transposed:
            g = g.T
        g_ref[...] = g
        d_ref[...], nm_ref[...], nv_ref[...] = _adamw(w_ref[...], g, m_ref[...], v_ref[...])

    if transposed:
        blk = pl.BlockSpec((rows, tr), lambda i, *_: (0, i))
        g_rows, steps = rows, cols // tr
    else:
        blk = pl.BlockSpec((tr, cols), lambda i, *_: (i, 0))
        g_rows, steps = cols, rows // tr
    in_specs = [pl.BlockSpec((N_DEV, tr, g_rows), lambda i, *_: (0, i, 0)), blk, blk, blk]
    args = [parts, w, m, v]
    if own is not None:
        in_specs.append(pl.BlockSpec((None, tr, g_rows), lambda i, idx: (idx[0], i, 0)))
        args = [own[1]] + args + [own[0]]
    return pl.pallas_call(
        body, name=name,
        grid_spec=pltpu.PrefetchScalarGridSpec(num_scalar_prefetch=n_pre, grid=(steps,), in_specs=in_specs,
                                               out_specs=[blk] * 4),
        out_shape=[_sds((rows, cols), F32)] * 4,
        compiler_params=_params("parallel"),
    )(*args)


_RPB_SIZE = 8 * 15 * 31


def _pack_small(g_mix, g_mlp, g_ple, g_final, rpb, loss_row):
    flat = jnp.concatenate([g_mix.reshape(-1), g_mlp.reshape(-1), g_ple.reshape(-1), g_final.reshape(-1),
                            rpb.reshape(-1), jnp.zeros((3840 - _RPB_SIZE,), F32), loss_row.reshape(-1),
                            jnp.zeros((128,), F32)])
    return flat.reshape(64, 128)


def _unpack_small(t):
    flat = t.reshape(-1)
    return (flat[0:1024].reshape(1, 1024), flat[4096:4096 + _RPB_SIZE].reshape(1, 8, 15, 31),
            flat[1024:2048].reshape(1, 1024), flat[2048:3072].reshape(1, 1024), flat[3072:4096])


def kernel(x, p, positions, g_mix, w_in, rpb, w_branch_na, w_branch_dil, w_out, g_mlp, w_up, w_down, g_ple, w_ple_gate, w_ple_proj, g_final, loss_target, m_g_mix, m_w_in, m_rpb, m_w_branch_na, m_w_branch_dil, m_w_out, m_g_mlp, m_w_up, m_w_down, m_g_ple, m_w_ple_gate, m_w_ple_proj, m_g_final, v_g_mix, v_w_in, v_rpb, v_w_branch_na, v_w_branch_dil, v_w_out, v_g_mlp, v_w_up, v_w_down, v_g_ple, v_w_ple_gate, v_w_ple_proj, v_g_final):
    sharded = dict(w_in=(w_in, m_w_in, v_w_in), w_branch_na=(w_branch_na, m_w_branch_na, v_w_branch_na),
                   w_branch_dil=(w_branch_dil, m_w_branch_dil, v_w_branch_dil), w_out=(w_out, m_w_out, v_w_out),
                   w_up=(w_up, m_w_up, v_w_up), w_down=(w_down, m_w_down, v_w_down),
                   w_ple_gate=(w_ple_gate, m_w_ple_gate, v_w_ple_gate),
                   w_ple_proj=(w_ple_proj, m_w_ple_proj, v_w_ple_proj))
    shards = {k: tuple(t[0] for t in val) for k, val in sharded.items()}

    me = _my_index()

    shards["w_in"] = tuple(t.T for t in shards["w_in"])

    w_in_b = _cast_bf16(shards["w_in"][0], name="cast_w_in")
    rest_b = [shards[name][0].astype(BF16).T if axis == 1 else shards[name][0].astype(BF16)
              for name, axis, _ in _WEIGHTS[1:]]
    first_in, token_in = _start_copies(_first_leg_copies, [w_in_b], [_sds((N_DEV,) + w_in_b.shape, BF16)], 4,
                                       name="start_gather_w_in")

    def whole(landed, mine):
        return _to_full(lax.dynamic_update_index_in_dim(landed, mine, me, 0))

    rest = {}

    def get_w_in(after):
        (mine,), landed = _wait_copies(_first_leg_copies, first_in, (*after, *rest_b), name="wait_gather_w_in")
        second, token = _start_copies(_second_leg_copies, [], landed, 3, name="start_forward_w_in")
        _, (landed,) = _wait_copies(_second_leg_copies, second, token, name="wait_forward_w_in")
        rest["first"], token = _start_copies(_first_leg_copies, rest_b,
                                             [_sds((N_DEV,) + t.shape, BF16) for t in rest_b], 4 * len(rest_b),
                                             name="start_gather_rest", after=landed)
        return whole(landed, mine), token

    def relay_rest(after):
        rest["mine"], landed = _wait_copies(_first_leg_copies, rest["first"], after, name="wait_gather_rest")
        rest["second"], token = _start_copies(_second_leg_copies, [], landed, 3 * len(rest_b),
                                              name="start_forward_rest")
        return token

    def get_rest(after, stage):
        n_src, send_sems, recv_sems, bufs = rest["second"]
        part = slice(0, 2) if stage == 0 else slice(2, len(rest_b))
        _, landed = _wait_copies(functools.partial(_second_leg_copies, first=part.start),
                                 (n_src, send_sems, recv_sems, bufs[part]), after,
                                 name=f"wait_forward_rest{stage}")
        return [whole(t, own) for t, own in zip(landed, rest["mine"][part])]

    sent = []

    def send_grads(indices, grads):
        chunked = [_to_chunks(i, g) for i, g in zip(indices, grads)]
        handle, token = _start_copies(_exchange_copies, chunked, [_sds(t.shape, BF16) for t in chunked],
                                      7 * len(chunked),
                                      name="start_exchange_" + ("w_in" if indices == (_W_IN,) else "rest"))
        sent.append((indices, handle))
        return token

    g_mix_0 = g_mix + token_in[0:1, 0:1]
    loss, dx, dgs, drpb = _local_step(
        x[0], p[0, 0].astype(BF16), positions[0], loss_target[0],
        g_mix_0, g_mlp, g_ple, g_final.reshape(1, -1), rpb[0], get_w_in, relay_rest, get_rest, send_grads)

    drpb3 = drpb.reshape(8, 16, 32)[:, :15, :31]
    small = _pack_small(dgs[0], dgs[1], dgs[2], dgs[3], drpb3, loss)
    share, done = _start_copies(_gather_copies, [small], [_sds((N_DEV,) + small.shape, F32)], 7,
                                name="start_share_small")

    out = {}
    for indices, handle in sent:
        chunked, landed = _wait_copies(_exchange_copies, handle, done,
                                       name="wait_exchange_" + ("w_in" if indices == (_W_IN,) else "rest"))
        for i, part, mine in zip(indices, landed, chunked):
            name = _WEIGHTS[i][0]
            w, m, v = shards[name]
            turned = _WEIGHTS[i][1] == 1 and i != _W_IN
            res = _sum_adamw(part, w, m, v, tr=368 if i == _W_IN else 128, name="adamw_" + name,
                             own=(mine, me.reshape(1).astype(jnp.int32)), transposed=turned)
            out[name] = [(t.T if i == _W_IN else t)[None] for t in res]
            done = res[0]
    (small,), (small_landed,) = _wait_copies(_gather_copies, share, done, name="wait_share_small")
    small_all = lax.dynamic_update_index_in_dim(small_landed, small, me, 0)
    small_w = _pack_small(g_mix, g_mlp, g_ple, g_final, rpb, jnp.zeros((128,), F32))
    small_m = _pack_small(m_g_mix, m_g_mlp, m_g_ple, m_g_final, m_rpb, jnp.zeros((128,), F32))
    small_v = _pack_small(v_g_mix, v_g_mlp, v_g_ple, v_g_final, v_rpb, jnp.zeros((128,), F32))
    res = _sum_adamw(small_all, small_w, small_m, small_v, tr=64, name="adamw_small")
    unpacked = [_unpack_small(t) for t in res]
    for i, name in enumerate(("g_mix", "rpb", "g_mlp", "g_ple", "g_final")):
        out[name] = [u[i] for u in unpacked]
    loss_total = res[0][62, 0]

    order = ("g_mix", "w_in", "rpb", "w_branch_na", "w_branch_dil", "w_out", "g_mlp", "w_up", "w_down",
             "g_ple", "w_ple_gate", "w_ple_proj", "g_final")
    grads = [out[k][0] for k in order]
    deltas = [out[k][1] for k in order]
    new_m = [out[k][2] for k in order]
    new_v = [out[k][3] for k in order]
    return (loss_total, dx[None], *grads, *deltas, *new_m, *new_v)
```

```python
import functools

import jax
import jax.numpy as jnp
from jax import lax
from jax.experimental import pallas as pl
from jax.experimental.pallas import tpu as pltpu

F32 = jnp.float32
BF16 = jnp.bfloat16

D_MODEL = 1024
HEAD_DIM = 64
GRID_W = 64
NA_WIDTH = 512
DIL_WIDTH = 768
IN_WIDTH = 5888
DIL_DILATIONS = (1, 4, 16)
DIL_RADIUS = 64
NA_WIN_ROWS = 8
RMS_EPS = 1e-6
NEG_INF = -1e30
QK_SCALE = HEAD_DIM ** -0.5

ADAM_LR = 0.001
ADAM_B1 = 0.9
ADAM_B2 = 0.999
ADAM_EPS = 1e-08
ADAM_WD = 0.01
ADAM_STEP = 10

N_DEV = 8
VMEM_LIMIT = 56 * 1024 * 1024
EPILOGUE_ROWS = 256
MESH = pl.DeviceIdType.MESH

NT_DIMS = (((1,), (1,)), ((), ()))
TN_DIMS = (((0,), (0,)), ((), ()))


def _sds(shape, dtype):
    return jax.ShapeDtypeStruct(shape, dtype)


def _params(*sem):
    return pltpu.CompilerParams(dimension_semantics=sem, vmem_limit_bytes=VMEM_LIMIT)


def _rows(tm, width, col=0):
    return pl.BlockSpec((tm, width), lambda i, c=col: (i, c))


def _const(shape):
    zeros = (0,) * len(shape)
    return pl.BlockSpec(shape, lambda i: zeros)


def _my_index():
    return 4 * lax.axis_index("x") + 2 * lax.axis_index("y") + lax.axis_index("c")


def _peer(k):
    x, y, c = lax.axis_index("x"), lax.axis_index("y"), lax.axis_index("c")
    px = 1 - x if k & 4 else x
    py = 1 - y if k & 2 else y
    pc = 1 - c if k & 1 else c
    return (px, py, pc), 4 * px + 2 * py + pc


def _call(body, *, name, grid, in_specs, out_specs, out_shape, scratch_shapes, args, after=None):
    n_in, n_out = len(in_specs), len(out_specs)
    extra = [] if after is None else [after]
    n_x = n_in + len(extra)

    def plain(*refs):
        body(refs[:n_in], refs[n_x:n_x + n_out], refs[n_x + n_out:])

    res = pl.pallas_call(plain, name=name, grid=grid,
                         in_specs=list(in_specs) + [pl.BlockSpec(memory_space=pl.ANY)] * len(extra),
                         out_specs=out_specs, out_shape=out_shape, scratch_shapes=scratch_shapes,
                         compiler_params=_params(*(("arbitrary",) * len(grid))))(*args, *extra)
    return list(res)


_HBM_SPEC = pl.BlockSpec(memory_space=pltpu.HBM)
_SEM_SPEC = pl.BlockSpec(memory_space=pltpu.SEMAPHORE)
_SIDE_EFFECT = pltpu.SideEffectType.DATAFLOW_SIDE_EFFECTING


_FIRST_LEG = (1, 2, 4, 6)
_SECOND_LEG = (2, 4, 6)


def _gather_copies(srcs, lands, send, recv, sending):
    me = _my_index()
    out = []
    for w in range(len(srcs)):
        for k in range(1, N_DEV):
            dev, idx = _peer(k)
            out.append(pltpu.make_async_remote_copy(
                src_ref=srcs[w], dst_ref=lands[w].at[me if sending else idx],
                send_sem=send.at[w * 7 + k - 1], recv_sem=recv.at[w * 7 + k - 1],
                device_id=dev, device_id_type=MESH))
    return out


def _first_leg_copies(srcs, lands, send, recv, sending):
    me = _my_index()
    out = []
    for w in range(len(srcs)):
        for j, k in enumerate(_FIRST_LEG):
            dev, idx = _peer(k)
            out.append(pltpu.make_async_remote_copy(
                src_ref=srcs[w], dst_ref=lands[w].at[me if sending else idx],
                send_sem=send.at[w * 4 + j], recv_sem=recv.at[w * 4 + j],
                device_id=dev, device_id_type=MESH))
    return out


def _second_leg_copies(srcs, lands, send, recv, sending, first=0):
    sibling, _ = _peer(1)
    out = []
    for w in range(len(lands)):
        for j, k in enumerate(_SECOND_LEG):
            slot = _peer(k if sending else k ^ 1)[1]
            sem = (first + w) * 3 + j
            out.append(pltpu.make_async_remote_copy(
                src_ref=lands[w].at[slot], dst_ref=lands[w].at[slot],
                send_sem=send.at[sem], recv_sem=recv.at[sem],
                device_id=sibling, device_id_type=MESH))
    return out


def _exchange_copies(srcs, lands, send, recv, sending):
    out = []
    for w in range(len(srcs)):
        for k in range(1, N_DEV):
            dev, idx = _peer(k)
            out.append(pltpu.make_async_remote_copy(
                src_ref=srcs[w].at[idx], dst_ref=lands[w].at[k],
                send_sem=send.at[w * 7 + k - 1], recv_sem=recv.at[w * 7 + k - 1],
                device_id=dev, device_id_type=MESH))
    return out


def _start_copies(make, srcs, lands, n_copies, *, name, after=None):
    n_src, n_buf = len(srcs), len(srcs) + len(lands)
    extra = [] if after is None else [after]

    def body(*refs):
        send, recv = refs[n_buf + len(extra)], refs[n_buf + len(extra) + 1]
        for cp in make(refs[:n_src], refs[n_src:n_buf], send, recv, True):
            cp.start()
        refs[-1][...] = jnp.zeros_like(refs[-1])

    bufs = list(srcs) + [lax.empty(t.shape, t.dtype) if isinstance(t, jax.ShapeDtypeStruct) else t for t in lands]
    res = pl.pallas_call(
        body, name=name,
        out_shape=(pltpu.SemaphoreType.DMA((n_copies,)), pltpu.SemaphoreType.DMA((n_copies,)),
                   *[pltpu.HBM(t.shape, t.dtype) for t in bufs], _sds((8, 128), F32)),
        in_specs=[_HBM_SPEC] * n_buf + [pl.BlockSpec(memory_space=pl.ANY)] * len(extra),
        out_specs=(_SEM_SPEC, _SEM_SPEC, *([_HBM_SPEC] * n_buf), pl.BlockSpec(memory_space=pltpu.VMEM)),
        input_output_aliases={i: 2 + i for i in range(n_buf)},
        compiler_params=pltpu.CompilerParams(has_side_effects=_SIDE_EFFECT),
    )(*[pltpu.with_memory_space_constraint(t, pltpu.HBM) for t in bufs], *extra)
    return (n_src, res[0], res[1], res[2:2 + n_buf]), res[-1]


def _wait_copies(make, handle, after, *, name):
    n_src, send_sems, recv_sems, bufs = handle
    n_buf = len(bufs)
    after = list(after) if isinstance(after, (tuple, list)) else [after]

    def body(*refs):
        for cp in make(refs[:n_src], refs[n_src:n_buf], refs[n_buf], refs[n_buf + 1], False):
            cp.wait_send()
            cp.wait_recv()

    res = pl.pallas_call(
        body, name=name,
        out_shape=tuple(pltpu.HBM(t.shape, t.dtype) for t in bufs),
        in_specs=[_HBM_SPEC] * n_buf + [_SEM_SPEC, _SEM_SPEC] + [pl.BlockSpec(memory_space=pl.ANY)] * len(after),
        out_specs=tuple([_HBM_SPEC] * n_buf),
        input_output_aliases={i: i for i in range(n_buf)},
        compiler_params=pltpu.CompilerParams(has_side_effects=_SIDE_EFFECT),
    )(*bufs, send_sems, recv_sems, *after)
    return list(res[:n_src]), list(res[n_src:])


def _add_colsums(s_refs, sums, step):
    for s_ref, val in zip(s_refs, sums):
        @pl.when(step == 0)
        def _(s_ref=s_ref, val=val):
            s_ref[...] = val

        @pl.when(step > 0)
        def _(s_ref=s_ref, val=val):
            s_ref[...] += val


def _matmul(a, b, *, ta=False, tb=False, out_dtype, tm, tn, tk, name, after=None, extra=(), epilogue=None,
            n_colsum=0, transpose_out=False):
    m, k = (a.shape[1], a.shape[0]) if ta else a.shape
    n = b.shape[0] if tb else b.shape[1]
    tm, tn, tk = min(tm, m), min(tn, n), min(tk, k)
    nk = k // tk
    dims = (((0 if ta else 1,), (1 if tb else 0,)), ((), ()))
    out_dtypes = out_dtype if isinstance(out_dtype, tuple) else (out_dtype,)
    n_tiles = len(out_dtypes)

    def add_colsums(o_refs, sums):
        _add_colsums(o_refs[n_tiles:], sums, pl.program_id(1))

    def finish(acc, x_refs, o_refs):
        vals = (acc,) if epilogue is None else epilogue(acc, *[r[...] for r in x_refs])
        for o_ref, val in zip(o_refs[:n_tiles], vals[:n_tiles]):
            o_ref[...] = (val.T if transpose_out else val).astype(o_ref.dtype)
        add_colsums(o_refs, vals[n_tiles:])

    chunk = EPILOGUE_ROWS if (nk == 1 and epilogue is not None and not ta and tm % EPILOGUE_ROWS == 0) else None

    def body(ins, outs, acc):
        a_ref, b_ref = ins[:2]
        if chunk is not None:
            sums = None
            for r0 in range(0, tm, chunk):
                part = lax.dot_general(a_ref[r0:r0 + chunk, :], b_ref[...], dims, preferred_element_type=F32)
                vals = epilogue(part, *[r[...] if r.shape[0] == 1 else r[r0:r0 + chunk, :] for r in ins[2:]])
                for o_ref, val in zip(outs[:n_tiles], vals[:n_tiles]):
                    o_ref[r0:r0 + chunk, :] = val.astype(o_ref.dtype)
                sums = vals[n_tiles:] if sums is None else [s + v for s, v in zip(sums, vals[n_tiles:])]
            add_colsums(outs, sums)
            return
        part = lax.dot_general(a_ref[...], b_ref[...], dims, preferred_element_type=F32)
        if nk == 1:
            finish(part, ins[2:], outs)
            return
        acc_ref, = acc
        kk = pl.program_id(2)

        @pl.when(kk == 0)
        def _():
            acc_ref[...] = part

        @pl.when(kk > 0)
        def _():
            acc_ref[...] += part

        @pl.when(kk == nk - 1)
        def _():
            finish(acc_ref[...], ins[2:], outs)

    a_spec = (pl.BlockSpec((tk, tm), lambda j, i, kk: (kk, i)) if ta
              else pl.BlockSpec((tm, tk), lambda j, i, kk: (i, kk)))
    b_spec = (pl.BlockSpec((tn, tk), lambda j, i, kk: (j, kk)) if tb
              else pl.BlockSpec((tk, tn), lambda j, i, kk: (kk, j)))
    tile = pl.BlockSpec((tm, tn), lambda j, i, kk: (i, j))
    row = pl.BlockSpec((1, tn), lambda j, i, kk: (0, j))

    out_tile, out_dims = (pl.BlockSpec((tn, tm), lambda j, i, kk: (j, i)), (n, m)) if transpose_out else (tile, (m, n))
    res = _call(
        body, name=name, grid=(n // tn, m // tm, nk),
        in_specs=[a_spec, b_spec] + [row if t.shape[0] == 1 else tile for t in extra],
        out_specs=[out_tile] * n_tiles + [row] * n_colsum,
        out_shape=[_sds(out_dims, dt) for dt in out_dtypes] + [_sds((1, n), F32)] * n_colsum,
        scratch_shapes=[] if nk == 1 else [pltpu.VMEM((tm, tn), F32)],
        args=(a, b, *extra), after=after)
    return res if isinstance(out_dtype, tuple) or n_colsum else res[0]


def _rstd(h):
    return lax.rsqrt(jnp.mean(h * h, axis=-1, keepdims=True) + RMS_EPS)


def _sigmoid(z):
    return 1.0 / (1.0 + jnp.exp(-z))


def _rms_fwd(x, g, *, tm, name):
    n = x.shape[0]

    def body(x_ref, g_ref, o_ref):
        h = x_ref[...]
        o_ref[...] = (h * _rstd(h) * g_ref[...]).astype(BF16)

    return pl.pallas_call(
        body, name=name, grid=(n // tm,),
        in_specs=[_rows(tm, D_MODEL), _const((1, D_MODEL))],
        out_specs=_rows(tm, D_MODEL), out_shape=_sds((n, D_MODEL), BF16),
        compiler_params=_params("parallel"),
    )(x, g)


def _swap_halves(t):
    lane = lax.broadcasted_iota(jnp.int32, (t.shape[0], 128), 1)
    pieces = [t[:, c:c + 128] for c in range(0, t.shape[1], 128)]
    return jnp.concatenate([jnp.where((lane & 63) < 32, pltpu.roll(h, 96, 1), pltpu.roll(h, 32, 1))
                            for h in pieces], axis=1)


def _dil_spec(dil, tm):
    return pl.BlockSpec((dil, tm // dil, 256), lambda i: (0, i, 0))


def _dil_scratch(tm):
    return pltpu.VMEM((2, tm, 128), F32)


def _load_token_order(src, scr, dil, rows, row0=0):
    if dil == 1:
        return src[0, row0:row0 + rows, :]
    for j in range(dil):
        for c in range(2):
            scr[c, pl.ds(j, rows // dil, stride=dil), :] = (
                src[j, row0 // dil:(row0 + rows) // dil, c * 128:(c + 1) * 128])
    return jnp.concatenate([scr[0, 0:rows, :], scr[1, 0:rows, :]], axis=1)


def _store_dil_order(val, dst, scr, dil, row0=0):
    rows = val.shape[0]
    if dil == 1:
        dst[0, row0:row0 + rows, :] = val.astype(dst.dtype)
        return
    for c in range(2):
        scr[c] = val[:, c * 128:(c + 1) * 128]
    for j in range(dil):
        for c in range(2):
            dst[j, row0 // dil:(row0 + rows) // dil, c * 128:(c + 1) * 128] = (
                scr[c, pl.ds(j, rows // dil, stride=dil), :].astype(dst.dtype))


def _project_in(a, w_t, cos_t, sin_t, *, tm, name, after=None):
    n = a.shape[0]
    n_dil = len(DIL_DILATIONS)
    na_w, dil_w = 3 * NA_WIDTH, 3 * DIL_WIDTH
    chunk = min(EPILOGUE_ROWS, tm)
    extra = [] if after is None else [after]

    def body(a_ref, w_ref, cos_ref, sin_ref, *rest):
        na_ref, gate_ref = rest[len(extra):len(extra) + 2]
        outs, scr = rest[len(extra) + 2:len(extra) + 2 + 3 * n_dil], rest[-1]

        def part(r0, first, width):
            return lax.dot_general(a_ref[r0:r0 + chunk, :], w_ref[first:first + width, :], NT_DIMS,
                                   preferred_element_type=F32)

        for r0 in range(0, tm, chunk):
            na_ref[r0:r0 + chunk, :] = part(r0, 0, na_w).astype(BF16)
            dil_part = part(r0, na_w, dil_w)
            cosv, sinv = cos_ref[r0:r0 + chunk, :], sin_ref[r0:r0 + chunk, :]
            for t in range(3):
                for gi, dil in enumerate(DIL_DILATIONS):
                    c0 = (t * n_dil + gi) * 256
                    val = dil_part[:, c0:c0 + 256]
                    if t < 2:
                        val = val * cosv + _swap_halves(val) * sinv
                    _store_dil_order(val, outs[t * n_dil + gi], scr, dil, r0)
            gate_ref[r0:r0 + chunk, :] = _sigmoid(part(r0, na_w + dil_w, 2 * D_MODEL)).astype(BF16)

    out_specs = [_rows(tm, na_w), _rows(tm, 2 * D_MODEL)]
    out_shape = [_sds((n, na_w), BF16), _sds((n, 2 * D_MODEL), BF16)]
    for _ in range(3):
        for dil in DIL_DILATIONS:
            out_specs.append(pl.BlockSpec((dil, tm // dil, 256), lambda i: (0, i, 0)))
            out_shape.append(_sds((dil, n // dil, 256), BF16))
    res = pl.pallas_call(
        body, name=name, grid=(n // tm,),
        in_specs=[_rows(tm, D_MODEL), _const(w_t.shape), _rows(tm, 256), _rows(tm, 256)]
                 + [pl.BlockSpec(memory_space=pl.ANY)] * len(extra),
        out_specs=out_specs, out_shape=out_shape,
        scratch_shapes=[pltpu.VMEM((2, chunk, 128), F32)],
        compiler_params=_params("parallel"),
    )(a, w_t, cos_t, sin_t, *extra)
    return res[0], res[1], res[2:5], res[5:8], res[8:11]


def _residual_rms_tile(delta, h, g):
    hn = h + delta
    return hn, hn * _rstd(hn) * g


def _gate_mix_tile(b2, s1, b1, s2):
    return b2, s1.astype(F32) * b1.astype(F32) + s2.astype(F32) * b2


def _gate_bwd_tile(dm, s1, b1, s2, b2):
    s1, b1, s2, b2 = (t.astype(F32) for t in (s1, b1, s2, b2))
    return dm * s1, dm * s2, dm * b1 * s1 * (1.0 - s1), dm * b2 * s2 * (1.0 - s2)


def _tail_tile(gt, pp, h2, target, g):
    sg = _sigmoid(gt)
    h3 = h2 + sg * pp
    r3 = _rstd(h3)
    n3 = h3 * r3
    err = n3 * g - target
    loss = 0.5 * jnp.sum(jnp.sum(err * err, axis=-1, keepdims=True) / D_MODEL)
    dy = err / D_MODEL
    dn = dy * g
    dh3 = r3 * (dn - n3 * jnp.mean(dn * n3, axis=-1, keepdims=True))
    return (dh3, dh3 * sg, dh3 * pp * sg * (1.0 - sg),
            jnp.sum(dy * n3, axis=0, keepdims=True), jnp.full((1, gt.shape[1]), loss, F32))


def _rms_bwd_tile(dz, h, g, dres):
    r = _rstd(h)
    nrm = h * r
    dn = dz * g
    dh = dres + r * (dn - nrm * jnp.mean(dn * nrm, axis=-1, keepdims=True))
    return dh, jnp.sum(dz * nrm, axis=0, keepdims=True)


def _rms_bwd_twice(dz, h, g, dres):
    dh, dg = _rms_bwd_tile(dz, h, g, dres)
    return dh, dh, dg


def _tail_step(f, w_down, h1, w_pg, p, w_pp, target, g_final, g_ple, *, tm, name):
    n = f.shape[0]
    chunk = min(EPILOGUE_ROWS, tm)

    def body(f_ref, wd_ref, h1_ref, wg_ref, p_ref, wp_ref, t_ref, gf_ref, gp_ref,
             e_ref, dpp_ref, dgt_ref, dh2_ref, dh2b_ref, dgf_ref, loss_ref, dgp_ref):
        sums = None
        for r0 in range(0, tm, chunk):
            rows = slice(r0, r0 + chunk)
            delta = jnp.dot(f_ref[rows, :], wd_ref[...], preferred_element_type=F32)
            h2, e = _residual_rms_tile(delta, h1_ref[rows, :], gp_ref[...])
            e = e.astype(BF16)
            e_ref[rows, :] = e
            gt = jnp.dot(e, wg_ref[...], preferred_element_type=F32)
            pp = lax.dot_general(p_ref[rows, :], wp_ref[...], NT_DIMS, preferred_element_type=F32)
            dh3, dpp, dgt, dgf, loss = _tail_tile(gt, pp, h2, t_ref[rows, :], gf_ref[...])
            dgt = dgt.astype(BF16)
            dpp_ref[rows, :] = dpp.astype(BF16)
            dgt_ref[rows, :] = dgt
            dz = lax.dot_general(dgt, wg_ref[...], NT_DIMS, preferred_element_type=F32)
            dh2, dgp = _rms_bwd_tile(dz, h2, gp_ref[...], dh3)
            dh2_ref[rows, :] = dh2
            dh2b_ref[rows, :] = dh2.astype(BF16)
            vals = (dgf, loss, dgp)
            sums = vals if sums is None else [s + v for s, v in zip(sums, vals)]
        _add_colsums((dgf_ref, loss_ref, dgp_ref), sums, pl.program_id(0))

    wide, gain = _rows(tm, D_MODEL), _const((1, D_MODEL))
    return pl.pallas_call(
        body, name=name, grid=(n // tm,),
        in_specs=[_rows(tm, f.shape[1]), _const(w_down.shape), wide, _const(w_pg.shape),
                  _rows(tm, p.shape[1]), _const(w_pp.shape), wide, gain, gain],
        out_specs=[wide] * 5 + [gain] * 3,
        out_shape=[_sds((n, D_MODEL), dt) for dt in (BF16, BF16, BF16, F32, BF16)]
                  + [_sds((1, D_MODEL), F32)] * 3,
        compiler_params=_params("arbitrary"),
    )(f, w_down, h1, w_pg, p, w_pp, target, g_final, g_ple)


def _assemble_dproj(dna, ddil_q, ddil_k, ddil_v, dgn, dgd, cos_t, sin_t, *, tm, name):
    n = dgn.shape[0]

    def body(*refs):
        dq_ref, dk_ref, dv_ref = refs[0:3]
        dil_in = refs[3:12]
        dgn_ref, dgd_ref, cos_ref, sin_ref, o_ref, scr = refs[12:18]
        o_ref[:, 0:512] = dq_ref[...]
        o_ref[:, 512:1024] = dk_ref[...].astype(BF16)
        o_ref[:, 1024:1536] = dv_ref[...].astype(BF16)
        cosv, sinv = cos_ref[...], sin_ref[...]
        for t in range(3):
            for gi, dil in enumerate(DIL_DILATIONS):
                val = _load_token_order(dil_in[t * 3 + gi], scr, dil, tm)
                if t < 2:
                    val = val * cosv + _swap_halves(val * sinv)
                c0 = 1536 + t * DIL_WIDTH + gi * 256
                o_ref[:, c0:c0 + 256] = val.astype(BF16)
        o_ref[:, 3840:4864] = dgn_ref[...]
        o_ref[:, 4864:5888] = dgd_ref[...]

    in_specs = [_rows(tm, NA_WIDTH)] * 3
    for _ in range(3):
        for dil in DIL_DILATIONS:
            in_specs.append(pl.BlockSpec((dil, tm // dil, 256), lambda i: (0, i, 0)))
    in_specs += [_rows(tm, D_MODEL)] * 2 + [_rows(tm, 256)] * 2
    return pl.pallas_call(
        body, name=name, grid=(n // tm,), in_specs=in_specs,
        out_specs=_rows(tm, IN_WIDTH), out_shape=_sds((n, IN_WIDTH), BF16),
        scratch_shapes=[_dil_scratch(tm)],
        compiler_params=_params("parallel"),
    )(*dna, *ddil_q, *ddil_k, *ddil_v, dgn, dgd, cos_t, sin_t)


N_ROW_OFF = 2 * NA_WIN_ROWS - 1
N_PAIRS = N_ROW_OFF - 1
RB_WIDTH = (N_ROW_OFF + 1) * GRID_W


def _na_bias(rb_ref, pair_scr):
    shape = (GRID_W, RB_WIDTH)
    qc = lax.broadcasted_iota(jnp.int32, shape, 0)
    qc2 = lax.broadcasted_iota(jnp.int32, (GRID_W, 128), 0)
    kc2 = lax.broadcasted_iota(jnp.int32, (GRID_W, 128), 1) & (GRID_W - 1)
    cs = jnp.clip(qc2 - 8, 0, GRID_W - 16)
    valid = (kc2 >= cs) & (kc2 < cs + 16)
    for hh in range(2):
        t = jnp.broadcast_to(rb_ref[hh], shape)
        t = pltpu.roll(t, RB_WIDTH - 15, 1)
        for b in range(6):
            t = jnp.where(((qc >> b) & 1) == 1, pltpu.roll(t, 1 << b, 1), t)
        t_odd = pltpu.roll(t, RB_WIDTH - GRID_W, 1)
        for ro in range(N_PAIRS):
            src = t if ro % 2 == 0 else t_odd
            base = (ro // 2) * 128
            pair_scr[hh, ro] = jnp.where(valid, src[:, base:base + 128], NEG_INF)


NA_GROUP_FWD = 8
NA_GROUP_BWD = 4


def _stack_heads(ref, r, scale=1.0):
    lane = lax.broadcasted_iota(jnp.int32, (GRID_W, 128), 1)
    t = ref[pl.ds(pl.multiple_of(r * GRID_W, GRID_W), GRID_W), :].astype(F32) * scale
    return jnp.concatenate([jnp.where(lane < 64, t, 0.0), jnp.where(lane >= 64, t, 0.0)], axis=0).astype(BF16)


def _unstack_heads(t2):
    lane = lax.broadcasted_iota(jnp.int32, (GRID_W, 128), 1)
    return jnp.where(lane < 64, t2[:GRID_W], t2[GRID_W:])


def _na_window(k_ref, v_ref, r, n_rows):
    rs = jnp.clip(r - NA_WIN_ROWS // 2, 0, n_rows - NA_WIN_ROWS)
    ro0 = (NA_WIN_ROWS - 1) - (r - rs)
    off = pl.multiple_of(rs * GRID_W, GRID_W)
    kw = k_ref[pl.ds(off, NA_WIN_ROWS * GRID_W), :]
    vw = v_ref[pl.ds(off, NA_WIN_ROWS * GRID_W), :]
    return kw, vw, off, ro0


def _na_probs(s_raw, pair_scr, ro0):
    bias = [jnp.concatenate([pair_scr[hh, ro0 + 2 * j] for j in range(NA_WIN_ROWS // 2)], axis=1)
            for hh in range(2)]
    s = s_raw + jnp.concatenate(bias, axis=0)
    m = jnp.max(s, axis=-1, keepdims=True)
    e = jnp.exp(s - m)
    return e * (1.0 / jnp.sum(e, axis=-1, keepdims=True))


def _na_qkv_specs(n):
    pairs = NA_WIDTH // 128
    return [pl.BlockSpec((n, 128), lambda h, first=t * pairs: (0, first + h)) for t in range(3)]


def _na_fwd(qkv, rb, *, name):
    n = qkv.shape[0]
    n_rows = n // GRID_W

    def body(ins, outs, scr):
        q_ref, k_ref, v_ref, rb_ref = ins
        o_ref, = outs
        pair_scr, = scr
        _na_bias(rb_ref, pair_scr)

        def group(g, carry):
            rows = [g * NA_GROUP_FWD + t for t in range(NA_GROUP_FWD)]
            wins = [_na_window(k_ref, v_ref, r, n_rows) for r in rows]
            raw = [lax.dot_general(_stack_heads(q_ref, r, QK_SCALE), w[0], NT_DIMS, preferred_element_type=F32)
                   for r, w in zip(rows, wins)]
            probs = [_na_probs(s, pair_scr, w[3]) for s, w in zip(raw, wins)]
            outs2 = [jnp.dot(p.astype(BF16), w[1], preferred_element_type=F32) for p, w in zip(probs, wins)]
            for r, o2 in zip(rows, outs2):
                o_ref[pl.ds(pl.multiple_of(r * GRID_W, GRID_W), GRID_W), :] = _unstack_heads(o2).astype(BF16)
            return carry

        lax.fori_loop(0, n_rows // NA_GROUP_FWD, group, 0)

    col = pl.BlockSpec((n, 128), lambda h: (0, h))
    return _call(
        body, name=name, grid=(NA_WIDTH // 128,),
        in_specs=_na_qkv_specs(n) + [pl.BlockSpec((2, 1, RB_WIDTH), lambda h: (h, 0, 0))],
        out_specs=[col], out_shape=[_sds((n, NA_WIDTH), BF16)],
        scratch_shapes=[pltpu.VMEM((2, N_PAIRS, GRID_W, 128), F32)],
        args=(qkv, qkv, qkv, rb))[0]


def _na_bwd(qkv, do, rb, *, name, after=None):
    n = qkv.shape[0]
    n_rows = n // GRID_W
    win = NA_WIN_ROWS * GRID_W

    def body(ins, outs, scr):
        q_ref, k_ref, v_ref, do_ref, rb_ref = ins
        dq_ref, dk_ref, dv_ref, drb_ref = outs
        pair_scr, acc_scr = scr
        _na_bias(rb_ref, pair_scr)
        acc_scr[...] = jnp.zeros_like(acc_scr)
        dk_ref[...] = jnp.zeros_like(dk_ref)
        dv_ref[...] = jnp.zeros_like(dv_ref)

        def group(g, carry):
            rows = [g * NA_GROUP_BWD + t for t in range(NA_GROUP_BWD)]
            wins = [_na_window(k_ref, v_ref, r, n_rows) for r in rows]
            qss = [_stack_heads(q_ref, r, QK_SCALE) for r in rows]
            doss = [_stack_heads(do_ref, r) for r in rows]
            raw = [lax.dot_general(qs, w[0], NT_DIMS, preferred_element_type=F32) for qs, w in zip(qss, wins)]
            dps = [lax.dot_general(dos, w[1], NT_DIMS, preferred_element_type=F32) for dos, w in zip(doss, wins)]
            probs = [_na_probs(s, pair_scr, w[3]) for s, w in zip(raw, wins)]
            dss = [p * (dp - jnp.sum(p * dp, axis=-1, keepdims=True)) for p, dp in zip(probs, dps)]
            dsbs = [ds.astype(BF16) for ds in dss]
            dq2s = [jnp.dot(dsb, w[0], preferred_element_type=F32) for dsb, w in zip(dsbs, wins)]
            dkws = [lax.dot_general(dsb, qs, TN_DIMS, preferred_element_type=F32) for dsb, qs in zip(dsbs, qss)]
            dvws = [lax.dot_general(p.astype(BF16), dos, TN_DIMS, preferred_element_type=F32)
                    for p, dos in zip(probs, doss)]
            for t, r in enumerate(rows):
                _, _, off, ro0 = wins[t]
                for hh in range(2):
                    for j in range(NA_WIN_ROWS // 2):
                        acc_scr[hh, ro0 + 2 * j] += dss[t][hh * GRID_W:(hh + 1) * GRID_W, j * 128:(j + 1) * 128]
                dq_ref[pl.ds(pl.multiple_of(r * GRID_W, GRID_W), GRID_W), :] = (
                    _unstack_heads(dq2s[t]) * QK_SCALE).astype(BF16)
                dk_ref[pl.ds(off, win), :] += dkws[t]
                dv_ref[pl.ds(off, win), :] += dvws[t]
            return carry

        lax.fori_loop(0, n_rows // NA_GROUP_BWD, group, 0)

        qc = lax.broadcasted_iota(jnp.int32, (N_PAIRS * GRID_W, 128), 0)
        for hh in range(2):
            t = acc_scr[hh].reshape(N_PAIRS * GRID_W, 128)
            for b in range(6):
                t = jnp.where(((qc >> b) & 1) == 1, pltpu.roll(t, 128 - (1 << b), 1), t)
            t = pltpu.roll(t, 15, 1)
            drb_ref[hh] = jnp.sum(t.reshape(N_PAIRS, GRID_W, 128), axis=1)

    col = pl.BlockSpec((n, 128), lambda h: (0, h))
    return _call(
        body, name=name, grid=(NA_WIDTH // 128,),
        in_specs=_na_qkv_specs(n) + [col, pl.BlockSpec((2, 1, RB_WIDTH), lambda h: (h, 0, 0))],
        out_specs=[col, col, col, pl.BlockSpec((2, N_PAIRS, 128), lambda h: (h, 0, 0))],
        out_shape=[_sds((n, NA_WIDTH), BF16), _sds((n, NA_WIDTH), F32), _sds((n, NA_WIDTH), F32),
                   _sds((8, N_PAIRS, 128), F32)],
        scratch_shapes=[pltpu.VMEM((2, N_PAIRS, GRID_W, 128), F32),
                        pltpu.VMEM((2, N_PAIRS, GRID_W, 128), F32)],
        args=(qkv, qkv, qkv, do, rb), after=after)


def _rpb_table(rpb2):
    t = jnp.pad(rpb2, ((0, 0), (0, 1), (0, GRID_W - rpb2.shape[-1])))
    return t.reshape(8, 1, RB_WIDTH)


def _rpb_grad(drb, *, name):
    kdim = drb.shape[1]

    def body(x_ref, o_ref):
        kk = lax.broadcasted_iota(jnp.int32, (128, 512), 0)
        jj = lax.broadcasted_iota(jnp.int32, (128, 512), 1)
        half, co = kk >> 6, kk & 63
        acc = jnp.zeros((8, 512), F32)
        for ro in range(N_PAIRS):
            hit = ((ro + half) == (jj >> 5)) & (co == (jj & 31)) & (co < 31)
            onehot = jnp.where(hit, 1.0, 0.0).astype(F32)
            acc = acc + jnp.dot(x_ref[:, ro * 128:(ro + 1) * 128], onehot, preferred_element_type=F32,
                                precision=lax.Precision.HIGHEST)
        o_ref[...] = acc

    return pl.pallas_call(
        body, name=name, grid=(1,),
        in_specs=[_const((8, kdim))], out_specs=_const((8, 512)), out_shape=_sds((8, 512), F32),
        compiler_params=_params("arbitrary"),
    )(drb)


DIL_GROUP = 8


def _dil_blocks(length):
    qb = min(128, length)
    return qb, min(qb + 2 * DIL_RADIUS, length), min(DIL_GROUP, length // qb)


def _stack_lanes(ref, t, qb, scale=1.0):
    lane = lax.broadcasted_iota(jnp.int32, (qb, 256), 1)
    val = ref[0, t * qb:(t + 1) * qb, :].astype(F32) * scale
    return jnp.concatenate([jnp.where((lane >> 6) == h, val, 0.0) for h in range(4)], axis=0).astype(BF16)


def _dil_window(k_ref, v_ref, blk, qb, win, length):
    start = pl.multiple_of(jnp.clip(blk * qb - DIL_RADIUS, 0, length - win), DIL_RADIUS)
    return k_ref[0, pl.ds(start, win), :], v_ref[0, pl.ds(start, win), :], start


def _dil_caps_init(caps_scr, qb, win):
    @pl.when((pl.program_id(0) == 0) & (pl.program_id(1) == 0))
    def _():
        gap = ((lax.broadcasted_iota(jnp.int32, (4 * qb, win), 0) & (qb - 1))
               - lax.broadcasted_iota(jnp.int32, (4 * qb, win), 1))
        for v in range(3):
            caps_scr[v] = jnp.where(jnp.abs(gap + v * DIL_RADIUS) <= DIL_RADIUS, jnp.inf, NEG_INF)


def _dil_mask(s, blk, start, qb, caps_scr):
    return jnp.minimum(s, caps_scr[(blk * qb - start) // DIL_RADIUS])


def _pick_heads(stacked, qb):
    lane = lax.broadcasted_iota(jnp.int32, (qb, 256), 1)
    out = jnp.zeros((qb, 256), stacked.dtype)
    for h in range(4):
        out = jnp.where((lane >> 6) == h, stacked[h * qb:(h + 1) * qb], out)
    return out


def _stack_head_cols(ref, t, qb):
    return jnp.concatenate([ref[0, t * qb:(t + 1) * qb, 64 * h:64 * h + 1] for h in range(4)], axis=0)


def _dil_fwd(q, k, v, *, name, after=None):
    dil, length, _ = q.shape
    qb, win, grp = _dil_blocks(length)
    extra = [] if after is None else [after]

    def body(q_ref, k_ref, v_ref, *rest):
        o_ref, lse_ref, caps_scr = rest[-3:]
        _dil_caps_init(caps_scr, qb, win)
        blks = [pl.program_id(1) * grp + t for t in range(grp)]
        wins = [_dil_window(k_ref, v_ref, b, qb, win, length) for b in blks]
        raw = [lax.dot_general(_stack_lanes(q_ref, t, qb, QK_SCALE), w[0], NT_DIMS, preferred_element_type=F32)
               for t, w in enumerate(wins)]
        lses, outs = [], []
        for t, (s, w) in enumerate(zip(raw, wins)):
            s = _dil_mask(s, blks[t], w[2], qb, caps_scr)
            m = jnp.max(s, axis=-1, keepdims=True)
            e = jnp.exp(s - m)
            norm = jnp.sum(e, axis=-1, keepdims=True)
            lses.append(m + jnp.log(norm))
            outs.append(jnp.dot((e * (1.0 / norm)).astype(BF16), w[1], preferred_element_type=F32))
        for t in range(grp):
            o_ref[0, t * qb:(t + 1) * qb, :] = _pick_heads(outs[t], qb)
            lse_ref[0, t * qb:(t + 1) * qb, :] = _pick_heads(jnp.broadcast_to(lses[t], (4 * qb, 256)), qb)

    seq = pl.BlockSpec((1, length, 256), lambda j, i: (j, 0, 0))
    blk = pl.BlockSpec((1, grp * qb, 256), lambda j, i: (j, i, 0))
    return pl.pallas_call(
        body, name=name, grid=(dil, length // (grp * qb)),
        in_specs=[blk, seq, seq] + [pl.BlockSpec(memory_space=pl.ANY)] * len(extra), out_specs=[blk, blk],
        out_shape=[_sds((dil, length, 256), F32)] * 2,
        scratch_shapes=[pltpu.VMEM((3, 4 * qb, win), F32)],
        compiler_params=_params("arbitrary", "arbitrary"),
    )(q, k, v, *extra)


def _dil_bwd(q, k, v, do, lse, cc, *, name):
    dil, length, _ = q.shape
    qb, win, grp = _dil_blocks(length)

    def body(q_ref, k_ref, v_ref, do_ref, lse_ref, cc_ref, dq_ref, dk_ref, dv_ref, caps_scr):
        _dil_caps_init(caps_scr, qb, win)

        @pl.when(pl.program_id(1) == 0)
        def _():
            dk_ref[...] = jnp.zeros_like(dk_ref)
            dv_ref[...] = jnp.zeros_like(dv_ref)

        blks = [pl.program_id(1) * grp + t for t in range(grp)]
        wins = [_dil_window(k_ref, v_ref, b, qb, win, length) for b in blks]
        qss = [_stack_lanes(q_ref, t, qb, QK_SCALE) for t in range(grp)]
        doss = [_stack_lanes(do_ref, t, qb) for t in range(grp)]
        raw = [lax.dot_general(qs, w[0], NT_DIMS, preferred_element_type=F32) for qs, w in zip(qss, wins)]
        dps = [lax.dot_general(dos, w[1], NT_DIMS, preferred_element_type=F32) for dos, w in zip(doss, wins)]
        probs = [jnp.exp(_dil_mask(s, blks[t], wins[t][2], qb, caps_scr) - _stack_head_cols(lse_ref, t, qb))
                 for t, s in enumerate(raw)]
        dsbs = [(p * (dp + _stack_head_cols(cc_ref, t, qb))).astype(BF16)
                for t, (p, dp) in enumerate(zip(probs, dps))]
        dq4s = [jnp.dot(dsb, w[0], preferred_element_type=F32) for dsb, w in zip(dsbs, wins)]
        dkws = [lax.dot_general(dsb, qs, TN_DIMS, preferred_element_type=F32) for dsb, qs in zip(dsbs, qss)]
        dvws = [lax.dot_general(p.astype(BF16), dos, TN_DIMS, preferred_element_type=F32)
                for p, dos in zip(probs, doss)]
        for t in range(grp):
            dq_ref[0, t * qb:(t + 1) * qb, :] = _pick_heads(dq4s[t], qb) * QK_SCALE
            dk_ref[0, pl.ds(wins[t][2], win), :] += dkws[t]
            dv_ref[0, pl.ds(wins[t][2], win), :] += dvws[t]

    seq = pl.BlockSpec((1, length, 256), lambda j, i: (j, 0, 0))
    blk = pl.BlockSpec((1, grp * qb, 256), lambda j, i: (j, i, 0))
    return pl.pallas_call(
        body, name=name, grid=(dil, length // (grp * qb)),
        in_specs=[blk, seq, seq, blk, blk, blk], out_specs=[blk, seq, seq],
        out_shape=[_sds((dil, length, 256), F32)] * 3,
        scratch_shapes=[pltpu.VMEM((3, 4 * qb, win), F32)],
        compiler_params=_params("arbitrary", "arbitrary"),
    )(q, k, v, do, lse, cc)


def _merge_weights(lses):
    m = jnp.maximum(jnp.maximum(lses[0], lses[1]), lses[2])
    es = [jnp.exp(t - m) for t in lses]
    inv = 1.0 / (es[0] + es[1] + es[2])
    return [e * inv for e in es]


def _branch_mix(y_na, w_bna, outs, lses, w_bd, gates, *, tm, name):
    n = y_na.shape[0]
    chunk = min(EPILOGUE_ROWS, tm)

    def body(yna_ref, wn_ref, *rest):
        o_in, l_in = rest[0:3], rest[3:6]
        wd_ref, sn_ref, sd_ref = rest[6:9]
        y_ref, yb_ref, bn_ref, bd_ref, mix_ref, scr = rest[9:15]
        for r0 in range(0, tm, chunk):
            rows = slice(r0, r0 + chunk)
            lv = [_load_token_order(l_in[g], scr, d, chunk, r0) for g, d in enumerate(DIL_DILATIONS)]
            ws = _merge_weights(lv)
            y = jnp.zeros((chunk, 256), F32)
            for g, d in enumerate(DIL_DILATIONS):
                y = y + ws[g] * _load_token_order(o_in[g], scr, d, chunk, r0)
            yb = y.astype(BF16)
            y_ref[rows, :] = y
            yb_ref[rows, :] = yb
            bn = lax.dot_general(yna_ref[rows, :], wn_ref[...], NT_DIMS, preferred_element_type=F32).astype(BF16)
            bd = lax.dot_general(yb, wd_ref[...], NT_DIMS, preferred_element_type=F32)
            bn_ref[rows, :] = bn
            bd, mixed = _gate_mix_tile(bd, sn_ref[rows, :], bn, sd_ref[rows, :])
            bd_ref[rows, :] = bd.astype(BF16)
            mix_ref[rows, :] = mixed.astype(BF16)

    specs = [_dil_spec(d, tm) for d in DIL_DILATIONS]
    return pl.pallas_call(
        body, name=name, grid=(n // tm,),
        in_specs=[_rows(tm, NA_WIDTH), _const(w_bna.shape)] + specs + specs
                 + [_const(w_bd.shape), _rows(tm, D_MODEL, 0), _rows(tm, D_MODEL, 1)],
        out_specs=[_rows(tm, 256)] * 2 + [_rows(tm, D_MODEL)] * 3,
        out_shape=[_sds((n, 256), F32), _sds((n, 256), BF16)] + [_sds((n, D_MODEL), BF16)] * 3,
        scratch_shapes=[_dil_scratch(chunk)],
        compiler_params=_params("parallel"),
    )(y_na, w_bna, *outs, *lses, w_bd, gates, gates)


def _branch_bwd(dh, w_out, gates, bn, bd, w_bna, w_bd, y, lses, *, tm, name):
    n = dh.shape[0]
    chunk = min(EPILOGUE_ROWS, tm)

    def body(dh_ref, wo_ref, sn_ref, sd_ref, bn_ref, bd_ref, wn_ref, wd_ref, y_ref, *rest):
        l_in = rest[0:3]
        dbn_ref, dbd_ref, dgn_ref, dgd_ref, dyna_ref = rest[3:8]
        do_out, cc_out, scr = rest[8:11], rest[11:14], rest[14]
        rr = lax.broadcasted_iota(jnp.int32, (256, 256), 0) >> 6
        cc = lax.broadcasted_iota(jnp.int32, (256, 256), 1) >> 6
        ones = jnp.where(rr == cc, 1.0, 0.0).astype(F32)
        for r0 in range(0, tm, chunk):
            rows = slice(r0, r0 + chunk)
            dm = lax.dot_general(dh_ref[rows, :], wo_ref[...], NT_DIMS, preferred_element_type=F32)
            dbn, dbd, dgn, dgd = (t.astype(BF16) for t in _gate_bwd_tile(
                dm, sn_ref[rows, :], bn_ref[rows, :], sd_ref[rows, :], bd_ref[rows, :]))
            dbn_ref[rows, :] = dbn
            dbd_ref[rows, :] = dbd
            dgn_ref[rows, :] = dgn
            dgd_ref[rows, :] = dgd
            dyna_ref[rows, :] = jnp.dot(dbn, wn_ref[...], preferred_element_type=F32).astype(BF16)
            dyv = jnp.dot(dbd, wd_ref[...], preferred_element_type=F32)
            lv = [_load_token_order(l_in[g], scr, d, chunk, r0) for g, d in enumerate(DIL_DILATIONS)]
            ws = _merge_weights(lv)
            tsum = jnp.dot(dyv * y_ref[rows, :], ones, preferred_element_type=F32,
                           precision=lax.Precision.HIGHEST)
            for g, d in enumerate(DIL_DILATIONS):
                _store_dil_order(ws[g] * dyv, do_out[g], scr, d, r0)
                _store_dil_order(-ws[g] * tsum, cc_out[g], scr, d, r0)

    specs = [_dil_spec(d, tm) for d in DIL_DILATIONS]
    wide = _rows(tm, D_MODEL)
    res = pl.pallas_call(
        body, name=name, grid=(n // tm,),
        in_specs=[wide, _const(w_out.shape), _rows(tm, D_MODEL, 0), _rows(tm, D_MODEL, 1), wide, wide,
                  _const(w_bna.shape), _const(w_bd.shape), _rows(tm, 256)] + specs,
        out_specs=[wide] * 4 + [_rows(tm, NA_WIDTH)] + specs + specs,
        out_shape=[_sds((n, D_MODEL), BF16)] * 4 + [_sds((n, NA_WIDTH), BF16)]
                  + [_sds((d, n // d, 256), BF16) for d in DIL_DILATIONS]
                  + [_sds((d, n // d, 256), F32) for d in DIL_DILATIONS],
        scratch_shapes=[_dil_scratch(chunk)],
        compiler_params=_params("parallel"),
    )(dh, w_out, gates, gates, bn, bd, w_bna, w_bd, y, *lses)
    return res[0], res[1], res[2], res[3], res[4], res[5:8], res[8:11]


_WEIGHTS = (("w_in", 1, 736), ("w_branch_na", 1, 128), ("w_branch_dil", 1, 128), ("w_out", 0, 128),
            ("w_up", 1, 512), ("w_down", 0, 512), ("w_ple_gate", 0, 128), ("w_ple_proj", 1, 128))
_W_IN, _W_BNA, _W_BD, _W_OUT, _W_UP, _W_DOWN, _W_PG, _W_PP = range(8)


def _to_full(gathered):
    return gathered.reshape(-1, gathered.shape[2])


def _to_chunks(widx, mat):
    return mat.reshape(N_DEV, _WEIGHTS[widx][2], mat.shape[1])


def _local_step(x, p_bf16, positions, target, g_mix, g_mlp, g_ple, g_final, rpb2,
                get_w_in, relay_rest, get_rest, send_grads):
    tm = 512
    half = HEAD_DIM // 2
    inv_freq = 10000.0 ** (-jnp.arange(half, dtype=F32) / half)
    ang = positions.astype(F32)[:, None] * inv_freq
    cos, sin = jnp.cos(ang), jnp.sin(ang)
    cos_t = jnp.tile(jnp.concatenate([cos, cos], axis=-1), (1, 4))
    sin_t = jnp.tile(jnp.concatenate([-sin, sin], axis=-1), (1, 4))
    rb = _rpb_table(rpb2)

    a = _rms_fwd(x, g_mix, tm=tm, name="rms_mix")
    w_in, token = get_w_in((a, cos_t, sin_t, p_bf16))
    na_qkv, gates, dq_g, dk_g, dv_g = _project_in(a, w_in, cos_t, sin_t, tm=512, name="mm_in", after=token)
    y_na = _na_fwd(na_qkv, rb, name="na_fwd")
    token = relay_rest(y_na)
    d_out, d_lse = [], []
    for g in range(3):
        o, lse = _dil_fwd(dq_g[g], dk_g[g], dv_g[g], name=f"dil_fwd{g}", after=token if g == 0 else None)
        d_out.append(o)
        d_lse.append(lse)
    w_bna, w_bd = get_rest(d_out[2], 0)
    y_dil, y_dil_b, bn, bd, mixed = _branch_mix(y_na, w_bna, d_out, d_lse, w_bd, gates, tm=tm, name="branch_mix")
    w_out, w_up, w_down, w_pg, w_pp = get_rest(mixed, 1)
    h1, c = _matmul(mixed, w_out, out_dtype=(F32, BF16), tm=512, tn=1024, tk=1024, name="mm_out",
                    extra=(x, g_mlp), epilogue=_residual_rms_tile)
    u, f = _matmul(c, w_up, tb=True, out_dtype=(BF16, BF16), tm=512, tn=2048, tk=1024, name="mm_up",
                   epilogue=lambda acc: (acc, jnp.square(jnp.maximum(acc, 0.0))))

    e, dpp, dgt, dh2, dh2_b, dg_final, loss, dg_ple = _tail_step(
        f, w_down, h1, w_pg, p_bf16, w_pp, target, g_final, g_ple, tm=256, name="tail_step")
    loss = loss[:, :128]
    gw_pp = _matmul(p_bf16, dpp, ta=True, transpose_out=True, out_dtype=BF16, tm=256, tn=1024, tk=2048,
                    name="mm_gw_pp")
    gw_pg = _matmul(e, dgt, ta=True, out_dtype=BF16, tm=512, tn=1024, tk=2048, name="mm_gw_pg")
    du = _matmul(dh2_b, w_down, tb=True, out_dtype=BF16, tm=512, tn=2048, tk=1024, name="mm_du",
                 extra=(u,), epilogue=lambda acc, uv: (acc * (2.0 * jnp.maximum(uv.astype(F32), 0.0)),))
    gw_down = _matmul(f, dh2_b, ta=True, out_dtype=BF16, tm=1024, tn=1024, tk=2048, name="mm_gw_down")
    gw_up = _matmul(c, du, ta=True, transpose_out=True, out_dtype=BF16, tm=512, tn=2048, tk=2048, name="mm_gw_up")
    dh1, dh1_b, dg_mlp = _matmul(
        du, w_up, out_dtype=(F32, BF16), tm=512, tn=1024, tk=4096, name="mm_dc",
        extra=(h1, g_mlp, dh2), epilogue=_rms_bwd_twice, n_colsum=1)
    dbn, dbd, dgn, dgd, dy_na, do_g, cc_g = _branch_bwd(dh1_b, w_out, gates, bn, bd, w_bna, w_bd, y_dil, d_lse,
                                                        tm=tm, name="branch_bwd")
    gw_out = _matmul(mixed, dh1_b, ta=True, out_dtype=BF16, tm=512, tn=1024, tk=2048, name="mm_gw_out")
    gw_bna = _matmul(y_na, dbn, ta=True, transpose_out=True, out_dtype=BF16, tm=512, tn=1024, tk=2048,
                     name="mm_gw_bna")
    gw_bd = _matmul(y_dil_b, dbd, ta=True, transpose_out=True, out_dtype=BF16, tm=256, tn=1024, tk=2048,
                    name="mm_gw_bd")
    token = send_grads((_W_PP, _W_PG, _W_DOWN, _W_UP, _W_OUT, _W_BNA, _W_BD),
                       (gw_pp, gw_pg, gw_down, gw_up, gw_out, gw_bna, gw_bd))
    dna = _na_bwd(na_qkv, dy_na, rb, name="na_bwd", after=token)
    drpb = _rpb_grad(dna[3].reshape(8, -1), name="rpb_grad")
    ddq, ddk, ddv = [], [], []
    for g in range(3):
        r = _dil_bwd(dq_g[g], dk_g[g], dv_g[g], do_g[g], d_lse[g], cc_g[g], name=f"dil_bwd{g}")
        ddq.append(r[0])
        ddk.append(r[1])
        ddv.append(r[2])
    dproj = _assemble_dproj(dna[0:3], ddq, ddk, ddv, dgn, dgd, cos_t, sin_t, tm=tm, name="assemble_dproj")
    gw_in = _matmul(a, dproj, ta=True, transpose_out=True, out_dtype=BF16, tm=512, tn=2944, tk=2048, name="mm_gw_in")
    token = send_grads((_W_IN,), (gw_in,))
    dx, dg_mix = _matmul(
        dproj, w_in, out_dtype=(F32,), tm=512, tn=1024, tk=5888, name="mm_da", after=token,
        extra=(x, g_mix, dh1), epilogue=_rms_bwd_tile, n_colsum=1)
    return loss, dx, (dg_mix, dg_mlp, dg_ple, dg_final), drpb


def _cast_bf16(t, *, name):
    def body(t_ref, o_ref):
        o_ref[...] = t_ref[...].astype(BF16)

    rows, cols = t.shape
    tr = 256 if rows % 256 == 0 else rows
    blk = pl.BlockSpec((tr, cols), lambda i: (i, 0))
    return pl.pallas_call(body, name=name, grid=(rows // tr,), in_specs=[blk], out_specs=blk,
                          out_shape=_sds(t.shape, BF16), compiler_params=_params("parallel"))(t)


def _adamw(w, g, m, v):
    m = ADAM_B1 * m + (1.0 - ADAM_B1) * g
    v = ADAM_B2 * v + (1.0 - ADAM_B2) * (g * g)
    m_hat = m / (1.0 - ADAM_B1 ** ADAM_STEP)
    v_hat = v / (1.0 - ADAM_B2 ** ADAM_STEP)
    delta = -ADAM_LR * (m_hat / (jnp.sqrt(v_hat) + ADAM_EPS) + ADAM_WD * w)
    return delta, m, v


def _sum_adamw(parts, w, m, v, *, tr, name, own=None, transposed=False):
    rows, cols = w.shape
    n_pre = 0 if own is None else 1

    def body(*refs):
        p_ref, w_ref, m_ref, v_ref = refs[n_pre:n_pre + 4]
        g_ref, d_ref, nm_ref, nv_ref = refs[-4:]
        g = (p_ref[0] if own is None else refs[n_pre + 4][...]).astype(F32)
        for s in range(1, N_DEV):
            g = g + p_ref[s].astype(F32)
        if transposed:
            g = g.T
        g_ref[...] = g
        d_ref[...], nm_ref[...], nv_ref[...] = _adamw(w_ref[...], g, m_ref[...], v_ref[...])

    if transposed:
        blk = pl.BlockSpec((rows, tr), lambda i, *_: (0, i))
        g_rows, steps = rows, cols // tr
    else:
        blk = pl.BlockSpec((tr, cols), lambda i, *_: (i, 0))
        g_rows, steps = cols, rows // tr
    in_specs = [pl.BlockSpec((N_DEV, tr, g_rows), lambda i, *_: (0, i, 0)), blk, blk, blk]
    args = [parts, w, m, v]
    if own is not None:
        in_specs.append(pl.BlockSpec((None, tr, g_rows), lambda i, idx: (idx[0], i, 0)))
        args = [own[1]] + args + [own[0]]
    return pl.pallas_call(
        body, name=name,
        grid_spec=pltpu.PrefetchScalarGridSpec(num_scalar_prefetch=n_pre, grid=(steps,), in_specs=in_specs,
                                               out_specs=[blk] * 4),
        out_shape=[_sds((rows, cols), F32)] * 4,
        compiler_params=_params("parallel"),
    )(*args)


_RPB_SIZE = 8 * 15 * 31


def _pack_small(g_mix, g_mlp, g_ple, g_final, rpb, loss_row):
    flat = jnp.concatenate([g_mix.reshape(-1), g_mlp.reshape(-1), g_ple.reshape(-1), g_final.reshape(-1),
                            rpb.reshape(-1), jnp.zeros((3840 - _RPB_SIZE,), F32), loss_row.reshape(-1),
                            jnp.zeros((128,), F32)])
    return flat.reshape(64, 128)


def _unpack_small(t):
    flat = t.reshape(-1)
    return (flat[0:1024].reshape(1, 1024), flat[4096:4096 + _RPB_SIZE].reshape(1, 8, 15, 31),
            flat[1024:2048].reshape(1, 1024), flat[2048:3072].reshape(1, 1024), flat[3072:4096])


def kernel(x, p, positions, g_mix, w_in, rpb, w_branch_na, w_branch_dil, w_out, g_mlp, w_up, w_down, g_ple, w_ple_gate, w_ple_proj, g_final, loss_target, m_g_mix, m_w_in, m_rpb, m_w_branch_na, m_w_branch_dil, m_w_out, m_g_mlp, m_w_up, m_w_down, m_g_ple, m_w_ple_gate, m_w_ple_proj, m_g_final, v_g_mix, v_w_in, v_rpb, v_w_branch_na, v_w_branch_dil, v_w_out, v_g_mlp, v_w_up, v_w_down, v_g_ple, v_w_ple_gate, v_w_ple_proj, v_g_final):
    sharded = dict(w_in=(w_in, m_w_in, v_w_in), w_branch_na=(w_branch_na, m_w_branch_na, v_w_branch_na),
                   w_branch_dil=(w_branch_dil, m_w_branch_dil, v_w_branch_dil), w_out=(w_out, m_w_out, v_w_out),
                   w_up=(w_up, m_w_up, v_w_up), w_down=(w_down, m_w_down, v_w_down),
                   w_ple_gate=(w_ple_gate, m_w_ple_gate, v_w_ple_gate),
                   w_ple_proj=(w_ple_proj, m_w_ple_proj, v_w_ple_proj))
    shards = {k: tuple(t[0] for t in val) for k, val in sharded.items()}

    me = _my_index()

    shards["w_in"] = tuple(t.T for t in shards["w_in"])

    w_in_b = _cast_bf16(shards["w_in"][0], name="cast_w_in")
    rest_b = [shards[name][0].astype(BF16).T if axis == 1 else shards[name][0].astype(BF16)
              for name, axis, _ in _WEIGHTS[1:]]
    first_in, token_in = _start_copies(_first_leg_copies, [w_in_b], [_sds((N_DEV,) + w_in_b.shape, BF16)], 4,
                                       name="start_gather_w_in")

    def whole(landed, mine):
        return _to_full(lax.dynamic_update_index_in_dim(landed, mine, me, 0))

    rest = {}

    def get_w_in(after):
        (mine,), landed = _wait_copies(_first_leg_copies, first_in, (*after, *rest_b), name="wait_gather_w_in")
        second, token = _start_copies(_second_leg_copies, [], landed, 3, name="start_forward_w_in")
        _, (landed,) = _wait_copies(_second_leg_copies, second, token, name="wait_forward_w_in")
        rest["first"], token = _start_copies(_first_leg_copies, rest_b,
                                             [_sds((N_DEV,) + t.shape, BF16) for t in rest_b], 4 * len(rest_b),
                                             name="start_gather_rest", after=landed)
        return whole(landed, mine), token

    def relay_rest(after):
        rest["mine"], landed = _wait_copies(_first_leg_copies, rest["first"], after, name="wait_gather_rest")
        rest["second"], token = _start_copies(_second_leg_copies, [], landed, 3 * len(rest_b),
                                              name="start_forward_rest")
        return token

    def get_rest(after, stage):
        n_src, send_sems, recv_sems, bufs = rest["second"]
        part = slice(0, 2) if stage == 0 else slice(2, len(rest_b))
        _, landed = _wait_copies(functools.partial(_second_leg_copies, first=part.start),
                                 (n_src, send_sems, recv_sems, bufs[part]), after,
                                 name=f"wait_forward_rest{stage}")
        return [whole(t, own) for t, own in zip(landed, rest["mine"][part])]

    sent = []

    def send_grads(indices, grads):
        chunked = [_to_chunks(i, g) for i, g in zip(indices, grads)]
        handle, token = _start_copies(_exchange_copies, chunked, [_sds(t.shape, BF16) for t in chunked],
                                      7 * len(chunked),
                                      name="start_exchange_" + ("w_in" if indices == (_W_IN,) else "rest"))
        sent.append((indices, handle))
        return token

    g_mix_0 = g_mix + token_in[0:1, 0:1]
    loss, dx, dgs, drpb = _local_step(
        x[0], p[0, 0].astype(BF16), positions[0], loss_target[0],
        g_mix_0, g_mlp, g_ple, g_final.reshape(1, -1), rpb[0], get_w_in, relay_rest, get_rest, send_grads)

    drpb3 = drpb.reshape(8, 16, 32)[:, :15, :31]
    small = _pack_small(dgs[0], dgs[1], dgs[2], dgs[3], drpb3, loss)
    share, done = _start_copies(_gather_copies, [small], [_sds((N_DEV,) + small.shape, F32)], 7,
                                name="start_share_small")

    out = {}
    for indices, handle in sent:
        chunked, landed = _wait_copies(_exchange_copies, handle, done,
                                       name="wait_exchange_" + ("w_in" if indices == (_W_IN,) else "rest"))
        for i, part, mine in zip(indices, landed, chunked):
            name = _WEIGHTS[i][0]
            w, m, v = shards[name]
            turned = _WEIGHTS[i][1] == 1 and i != _W_IN
            res = _sum_adamw(part, w, m, v, tr=368 if i == _W_IN else 128, name="adamw_" + name,
                             own=(mine, me.reshape(1).astype(jnp.int32)), transposed=turned)
            out[name] = [(t.T if i == _W_IN else t)[None] for t in res]
            done = res[0]
    (small,), (small_landed,) = _wait_copies(_gather_copies, share, done, name="wait_share_small")
    small_all = lax.dynamic_update_index_in_dim(small_landed, small, me, 0)
    small_w = _pack_small(g_mix, g_mlp, g_ple, g_final, rpb, jnp.zeros((128,), F32))
    small_m = _pack_small(m_g_mix, m_g_mlp, m_g_ple, m_g_final, m_rpb, jnp.zeros((128,), F32))
    small_v = _pack_small(v_g_mix, v_g_mlp, v_g_ple, v_g_final, v_rpb, jnp.zeros((128,), F32))
    res = _sum_adamw(small_all, small_w, small_m, small_v, tr=64, name="adamw_small")
    unpacked = [_unpack_small(t) for t in res]
    for i, name in enumerate(("g_mix", "rpb", "g_mlp", "g_ple", "g_final")):
        out[name] = [u[i] for u in unpacked]
    loss_total = res[0][62, 0]

    order = ("g_mix", "w_in", "rpb", "w_branch_na", "w_branch_dil", "w_out", "g_mlp", "w_up", "w_down",
             "g_ple", "w_ple_gate", "w_ple_proj", "g_final")
    grads = [out[k][0] for k in order]
    deltas = [out[k][1] for k in order]
    new_m = [out[k][2] for k in order]
    new_v = [out[k][3] for k in order]
    return (loss_total, dx[None], *grads, *deltas, *new_m, *new_v)
```

```python
import functools

import jax
import jax.numpy as jnp
from jax import lax
from jax.experimental import pallas as pl
from jax.experimental.pallas import tpu as pltpu

F32 = jnp.float32
BF16 = jnp.bfloat16

D_MODEL = 1024
HEAD_DIM = 64
GRID_W = 64
NA_WIDTH = 512
DIL_WIDTH = 768
IN_WIDTH = 5888
DIL_DILATIONS = (1, 4, 16)
DIL_RADIUS = 64
NA_WIN_ROWS = 8
RMS_EPS = 1e-6
NEG_INF = -1e30
QK_SCALE = HEAD_DIM ** -0.5

ADAM_LR = 0.001
ADAM_B1 = 0.9
ADAM_B2 = 0.999
ADAM_EPS = 1e-08
ADAM_WD = 0.01
ADAM_STEP = 10

N_DEV = 8
VMEM_LIMIT = 56 * 1024 * 1024
EPILOGUE_ROWS = 256
MESH = pl.DeviceIdType.MESH

NT_DIMS = (((1,), (1,)), ((), ()))
TN_DIMS = (((0,), (0,)), ((), ()))


def _sds(shape, dtype):
    return jax.ShapeDtypeStruct(shape, dtype)


def _params(*sem):
    return pltpu.CompilerParams(dimension_semantics=sem, vmem_limit_bytes=VMEM_LIMIT)


def _rows(tm, width, col=0):
    return pl.BlockSpec((tm, width), lambda i, c=col: (i, c))


def _const(shape):
    zeros = (0,) * len(shape)
    return pl.BlockSpec(shape, lambda i: zeros)


def _my_index():
    return 4 * lax.axis_index("x") + 2 * lax.axis_index("y") + lax.axis_index("c")


def _peer(k):
    x, y, c = lax.axis_index("x"), lax.axis_index("y"), lax.axis_index("c")
    px = 1 - x if k & 4 else x
    py = 1 - y if k & 2 else y
    pc = 1 - c if k & 1 else c
    return (px, py, pc), 4 * px + 2 * py + pc


def _call(body, *, name, grid, in_specs, out_specs, out_shape, scratch_shapes, args, after=None):
    n_in, n_out = len(in_specs), len(out_specs)
    extra = [] if after is None else [after]
    n_x = n_in + len(extra)

    def plain(*refs):
        body(refs[:n_in], refs[n_x:n_x + n_out], refs[n_x + n_out:])

    res = pl.pallas_call(plain, name=name, grid=grid,
                         in_specs=list(in_specs) + [pl.BlockSpec(memory_space=pl.ANY)] * len(extra),
                         out_specs=out_specs, out_shape=out_shape, scratch_shapes=scratch_shapes,
                         compiler_params=_params(*(("arbitrary",) * len(grid))))(*args, *extra)
    return list(res)


_HBM_SPEC = pl.BlockSpec(memory_space=pltpu.HBM)
_SEM_SPEC = pl.BlockSpec(memory_space=pltpu.SEMAPHORE)
_SIDE_EFFECT = pltpu.SideEffectType.DATAFLOW_SIDE_EFFECTING


_FIRST_LEG = (1, 2, 4, 6)
_SECOND_LEG = (2, 4, 6)


def _gather_copies(srcs, lands, send, recv, sending):
    me = _my_index()
    out = []
    for w in range(len(srcs)):
        for k in range(1, N_DEV):
            dev, idx = _peer(k)
            out.append(pltpu.make_async_remote_copy(
                src_ref=srcs[w], dst_ref=lands[w].at[me if sending else idx],
                send_sem=send.at[w * 7 + k - 1], recv_sem=recv.at[w * 7 + k - 1],
                device_id=dev, device_id_type=MESH))
    return out


def _first_leg_copies(srcs, lands, send, recv, sending):
    me = _my_index()
    out = []
    for w in range(len(srcs)):
        for j, k in enumerate(_FIRST_LEG):
            dev, idx = _peer(k)
            out.append(pltpu.make_async_remote_copy(
                src_ref=srcs[w], dst_ref=lands[w].at[me if sending else idx],
                send_sem=send.at[w * 4 + j], recv_sem=recv.at[w * 4 + j],
                device_id=dev, device_id_type=MESH))
    return out


def _second_leg_copies(srcs, lands, send, recv, sending, first=0):
    sibling, _ = _peer(1)
    out = []
    for w in range(len(lands)):
        for j, k in enumerate(_SECOND_LEG):
            slot = _peer(k if sending else k ^ 1)[1]
            sem = (first + w) * 3 + j
            out.append(pltpu.make_async_remote_copy(
                src_ref=lands[w].at[slot], dst_ref=lands[w].at[slot],
                send_sem=send.at[sem], recv_sem=recv.at[sem],
                device_id=sibling, device_id_type=MESH))
    return out


def _exchange_copies(srcs, lands, send, recv, sending):
    out = []
    for w in range(len(srcs)):
        for k in range(1, N_DEV):
            dev, idx = _peer(k)
            out.append(pltpu.make_async_remote_copy(
                src_ref=srcs[w].at[idx], dst_ref=lands[w].at[k],
                send_sem=send.at[w * 7 + k - 1], recv_sem=recv.at[w * 7 + k - 1],
                device_id=dev, device_id_type=MESH))
    return out


def _start_copies(make, srcs, lands, n_copies, *, name, after=None):
    n_src, n_buf = len(srcs), len(srcs) + len(lands)
    extra = [] if after is None else [after]

    def body(*refs):
        send, recv = refs[n_buf + len(extra)], refs[n_buf + len(extra) + 1]
        for cp in make(refs[:n_src], refs[n_src:n_buf], send, recv, True):
            cp.start()
        refs[-1][...] = jnp.zeros_like(refs[-1])

    bufs = list(srcs) + [lax.empty(t.shape, t.dtype) if isinstance(t, jax.ShapeDtypeStruct) else t for t in lands]
    res = pl.pallas_call(
        body, name=name,
        out_shape=(pltpu.SemaphoreType.DMA((n_copies,)), pltpu.SemaphoreType.DMA((n_copies,)),
                   *[pltpu.HBM(t.shape, t.dtype) for t in bufs], _sds((8, 128), F32)),
        in_specs=[_HBM_SPEC] * n_buf + [pl.BlockSpec(memory_space=pl.ANY)] * len(extra),
        out_specs=(_SEM_SPEC, _SEM_SPEC, *([_HBM_SPEC] * n_buf), pl.BlockSpec(memory_space=pltpu.VMEM)),
        input_output_aliases={i: 2 + i for i in range(n_buf)},
        compiler_params=pltpu.CompilerParams(has_side_effects=_SIDE_EFFECT),
    )(*[pltpu.with_memory_space_constraint(t, pltpu.HBM) for t in bufs], *extra)
    return (n_src, res[0], res[1], res[2:2 + n_buf]), res[-1]


def _wait_copies(make, handle, after, *, name):
    n_src, send_sems, recv_sems, bufs = handle
    n_buf = len(bufs)
    after = list(after) if isinstance(after, (tuple, list)) else [after]

    def body(*refs):
        for cp in make(refs[:n_src], refs[n_src:n_buf], refs[n_buf], refs[n_buf + 1], False):
            cp.wait_send()
            cp.wait_recv()

    res = pl.pallas_call(
        body, name=name,
        out_shape=tuple(pltpu.HBM(t.shape, t.dtype) for t in bufs),
        in_specs=[_HBM_SPEC] * n_buf + [_SEM_SPEC, _SEM_SPEC] + [pl.BlockSpec(memory_space=pl.ANY)] * len(after),
        out_specs=tuple([_HBM_SPEC] * n_buf),
        input_output_aliases={i: i for i in range(n_buf)},
        compiler_params=pltpu.CompilerParams(has_side_effects=_SIDE_EFFECT),
    )(*bufs, send_sems, recv_sems, *after)
    return list(res[:n_src]), list(res[n_src:])


def _add_colsums(s_refs, sums, step):
    for s_ref, val in zip(s_refs, sums):
        @pl.when(step == 0)
        def _(s_ref=s_ref, val=val):
            s_ref[...] = val

        @pl.when(step > 0)
        def _(s_ref=s_ref, val=val):
            s_ref[...] += val


def _matmul(a, b, *, ta=False, tb=False, out_dtype, tm, tn, tk, name, after=None, extra=(), epilogue=None,
            n_colsum=0, transpose_out=False):
    m, k = (a.shape[1], a.shape[0]) if ta else a.shape
    n = b.shape[0] if tb else b.shape[1]
    tm, tn, tk = min(tm, m), min(tn, n), min(tk, k)
    nk = k // tk
    dims = (((0 if ta else 1,), (1 if tb else 0,)), ((), ()))
    out_dtypes = out_dtype if isinstance(out_dtype, tuple) else (out_dtype,)
    n_tiles = len(out_dtypes)

    def add_colsums(o_refs, sums):
        _add_colsums(o_refs[n_tiles:], sums, pl.program_id(1))

    def finish(acc, x_refs, o_refs):
        vals = (acc,) if epilogue is None else epilogue(acc, *[r[...] for r in x_refs])
        for o_ref, val in zip(o_refs[:n_tiles], vals[:n_tiles]):
            o_ref[...] = (val.T if transpose_out else val).astype(o_ref.dtype)
        add_colsums(o_refs, vals[n_tiles:])

    chunk = EPILOGUE_ROWS if (nk == 1 and epilogue is not None and not ta and tm % EPILOGUE_ROWS == 0) else None

    def body(ins, outs, acc):
        a_ref, b_ref = ins[:2]
        if chunk is not None:
            sums = None
            for r0 in range(0, tm, chunk):
                part = lax.dot_general(a_ref[r0:r0 + chunk, :], b_ref[...], dims, preferred_element_type=F32)
                vals = epilogue(part, *[r[...] if r.shape[0] == 1 else r[r0:r0 + chunk, :] for r in ins[2:]])
                for o_ref, val in zip(outs[:n_tiles], vals[:n_tiles]):
                    o_ref[r0:r0 + chunk, :] = val.astype(o_ref.dtype)
                sums = vals[n_tiles:] if sums is None else [s + v for s, v in zip(sums, vals[n_tiles:])]
            add_colsums(outs, sums)
            return
        part = lax.dot_general(a_ref[...], b_ref[...], dims, preferred_element_type=F32)
        if nk == 1:
            finish(part, ins[2:], outs)
            return
        acc_ref, = acc
        kk = pl.program_id(2)

        @pl.when(kk == 0)
        def _():
            acc_ref[...] = part

        @pl.when(kk > 0)
        def _():
            acc_ref[...] += part

        @pl.when(kk == nk - 1)
        def _():
            finish(acc_ref[...], ins[2:], outs)

    a_spec = (pl.BlockSpec((tk, tm), lambda j, i, kk: (kk, i)) if ta
              else pl.BlockSpec((tm, tk), lambda j, i, kk: (i, kk)))
    b_spec = (pl.BlockSpec((tn, tk), lambda j, i, kk: (j, kk)) if tb
              else pl.BlockSpec((tk, tn), lambda j, i, kk: (kk, j)))
    tile = pl.BlockSpec((tm, tn), lambda j, i, kk: (i, j))
    row = pl.BlockSpec((1, tn), lambda j, i, kk: (0, j))

    out_tile, out_dims = (pl.BlockSpec((tn, tm), lambda j, i, kk: (j, i)), (n, m)) if transpose_out else (tile, (m, n))
    res = _call(
        body, name=name, grid=(n // tn, m // tm, nk),
        in_specs=[a_spec, b_spec] + [row if t.shape[0] == 1 else tile for t in extra],
        out_specs=[out_tile] * n_tiles + [row] * n_colsum,
        out_shape=[_sds(out_dims, dt) for dt in out_dtypes] + [_sds((1, n), F32)] * n_colsum,
        scratch_shapes=[] if nk == 1 else [pltpu.VMEM((tm, tn), F32)],
        args=(a, b, *extra), after=after)
    return res if isinstance(out_dtype, tuple) or n_colsum else res[0]


def _rstd(h):
    return lax.rsqrt(jnp.mean(h * h, axis=-1, keepdims=True) + RMS_EPS)


def _sigmoid(z):
    return 1.0 / (1.0 + jnp.exp(-z))


def _rms_fwd(x, g, *, tm, name):
    n = x.shape[0]

    def body(x_ref, g_ref, o_ref):
        h = x_ref[...]
        o_ref[...] = (h * _rstd(h) * g_ref[...]).astype(BF16)

    return pl.pallas_call(
        body, name=name, grid=(n // tm,),
        in_specs=[_rows(tm, D_MODEL), _const((1, D_MODEL))],
        out_specs=_rows(tm, D_MODEL), out_shape=_sds((n, D_MODEL), BF16),
        compiler_params=_params("parallel"),
    )(x, g)


def _swap_halves(t):
    lane = lax.broadcasted_iota(jnp.int32, (t.shape[0], 128), 1)
    pieces = [t[:, c:c + 128] for c in range(0, t.shape[1], 128)]
    return jnp.concatenate([jnp.where((lane & 63) < 32, pltpu.roll(h, 96, 1), pltpu.roll(h, 32, 1))
                            for h in pieces], axis=1)


def _dil_spec(dil, tm):
    return pl.BlockSpec((dil, tm // dil, 256), lambda i: (0, i, 0))


def _dil_scratch(tm):
    return pltpu.VMEM((2, tm, 128), F32)


def _load_token_order(src, scr, dil, rows, row0=0):
    if dil == 1:
        return src[0, row0:row0 + rows, :]
    for j in range(dil):
        for c in range(2):
            scr[c, pl.ds(j, rows // dil, stride=dil), :] = (
                src[j, row0 // dil:(row0 + rows) // dil, c * 128:(c + 1) * 128])
    return jnp.concatenate([scr[0, 0:rows, :], scr[1, 0:rows, :]], axis=1)


def _store_dil_order(val, dst, scr, dil, row0=0):
    rows = val.shape[0]
    if dil == 1:
        dst[0, row0:row0 + rows, :] = val.astype(dst.dtype)
        return
    for c in range(2):
        scr[c] = val[:, c * 128:(c + 1) * 128]
    for j in range(dil):
        for c in range(2):
            dst[j, row0 // dil:(row0 + rows) // dil, c * 128:(c + 1) * 128] = (
                scr[c, pl.ds(j, rows // dil, stride=dil), :].astype(dst.dtype))


def _project_in(a, w_t, cos_t, sin_t, *, tm, name, after=None):
    n = a.shape[0]
    n_dil = len(DIL_DILATIONS)
    na_w, dil_w = 3 * NA_WIDTH, 3 * DIL_WIDTH
    chunk = min(EPILOGUE_ROWS, tm)
    extra = [] if after is None else [after]

    def body(a_ref, w_ref, cos_ref, sin_ref, *rest):
        na_ref, gate_ref = rest[len(extra):len(extra) + 2]
        outs, scr = rest[len(extra) + 2:len(extra) + 2 + 3 * n_dil], rest[-1]

        def part(r0, first, width):
            return lax.dot_general(a_ref[r0:r0 + chunk, :], w_ref[first:first + width, :], NT_DIMS,
                                   preferred_element_type=F32)

        for r0 in range(0, tm, chunk):
            na_ref[r0:r0 + chunk, :] = part(r0, 0, na_w).astype(BF16)
            dil_part = part(r0, na_w, dil_w)
            cosv, sinv = cos_ref[r0:r0 + chunk, :], sin_ref[r0:r0 + chunk, :]
            for t in range(3):
                for gi, dil in enumerate(DIL_DILATIONS):
                    c0 = (t * n_dil + gi) * 256
                    val = dil_part[:, c0:c0 + 256]
                    if t < 2:
                        val = val * cosv + _swap_halves(val) * sinv
                    _store_dil_order(val, outs[t * n_dil + gi], scr, dil, r0)
            gate_ref[r0:r0 + chunk, :] = _sigmoid(part(r0, na_w + dil_w, 2 * D_MODEL)).astype(BF16)

    out_specs = [_rows(tm, na_w), _rows(tm, 2 * D_MODEL)]
    out_shape = [_sds((n, na_w), BF16), _sds((n, 2 * D_MODEL), BF16)]
    for _ in range(3):
        for dil in DIL_DILATIONS:
            out_specs.append(pl.BlockSpec((dil, tm // dil, 256), lambda i: (0, i, 0)))
            out_shape.append(_sds((dil, n // dil, 256), BF16))
    res = pl.pallas_call(
        body, name=name, grid=(n // tm,),
        in_specs=[_rows(tm, D_MODEL), _const(w_t.shape), _rows(tm, 256), _rows(tm, 256)]
                 + [pl.BlockSpec(memory_space=pl.ANY)] * len(extra),
        out_specs=out_specs, out_shape=out_shape,
        scratch_shapes=[pltpu.VMEM((2, chunk, 128), F32)],
        compiler_params=_params("parallel"),
    )(a, w_t, cos_t, sin_t, *extra)
    return res[0], res[1], res[2:5], res[5:8], res[8:11]


def _residual_rms_tile(delta, h, g):
    hn = h + delta
    return hn, hn * _rstd(hn) * g


def _gate_mix_tile(b2, s1, b1, s2):
    return b2, s1.astype(F32) * b1.astype(F32) + s2.astype(F32) * b2


def _gate_bwd_tile(dm, s1, b1, s2, b2):
    s1, b1, s2, b2 = (t.astype(F32) for t in (s1, b1, s2, b2))
    return dm * s1, dm * s2, dm * b1 * s1 * (1.0 - s1), dm * b2 * s2 * (1.0 - s2)


def _tail_tile(gt, pp, h2, target, g):
    sg = _sigmoid(gt)
    h3 = h2 + sg * pp
    r3 = _rstd(h3)
    n3 = h3 * r3
    err = n3 * g - target
    loss = 0.5 * jnp.sum(jnp.sum(err * err, axis=-1, keepdims=True) / D_MODEL)
    dy = err / D_MODEL
    dn = dy * g
    dh3 = r3 * (dn - n3 * jnp.mean(dn * n3, axis=-1, keepdims=True))
    return (dh3, dh3 * sg, dh3 * pp * sg * (1.0 - sg),
            jnp.sum(dy * n3, axis=0, keepdims=True), jnp.full((1, gt.shape[1]), loss, F32))


def _rms_bwd_tile(dz, h, g, dres):
    r = _rstd(h)
    nrm = h * r
    dn = dz * g
    dh = dres + r * (dn - nrm * jnp.mean(dn * nrm, axis=-1, keepdims=True))
    return dh, jnp.sum(dz * nrm, axis=0, keepdims=True)


def _rms_bwd_twice(dz, h, g, dres):
    dh, dg = _rms_bwd_tile(dz, h, g, dres)
    return dh, dh, dg


def _tail_step(f, w_down, h1, w_pg, p, w_pp, target, g_final, g_ple, *, tm, name):
    n = f.shape[0]
    chunk = min(EPILOGUE_ROWS, tm)

    def body(f_ref, wd_ref, h1_ref, wg_ref, p_ref, wp_ref, t_ref, gf_ref, gp_ref,
             e_ref, dpp_ref, dgt_ref, dh2_ref, dh2b_ref, dgf_ref, loss_ref, dgp_ref):
        sums = None
        for r0 in range(0, tm, chunk):
            rows = slice(r0, r0 + chunk)
            delta = jnp.dot(f_ref[rows, :], wd_ref[...], preferred_element_type=F32)
            h2, e = _residual_rms_tile(delta, h1_ref[rows, :], gp_ref[...])
            e = e.astype(BF16)
            e_ref[rows, :] = e
            gt = jnp.dot(e, wg_ref[...], preferred_element_type=F32)
            pp = lax.dot_general(p_ref[rows, :], wp_ref[...], NT_DIMS, preferred_element_type=F32)
            dh3, dpp, dgt, dgf, loss = _tail_tile(gt, pp, h2, t_ref[rows, :], gf_ref[...])
            dgt = dgt.astype(BF16)
            dpp_ref[rows, :] = dpp.astype(BF16)
            dgt_ref[rows, :] = dgt
            dz = lax.dot_general(dgt, wg_ref[...], NT_DIMS, preferred_element_type=F32)
            dh2, dgp = _rms_bwd_tile(dz, h2, gp_ref[...], dh3)
            dh2_ref[rows, :] = dh2
            dh2b_ref[rows, :] = dh2.astype(BF16)
            vals = (dgf, loss, dgp)
            sums = vals if sums is None else [s + v for s, v in zip(sums, vals)]
        _add_colsums((dgf_ref, loss_ref, dgp_ref), sums, pl.program_id(0))

    wide, gain = _rows(tm, D_MODEL), _const((1, D_MODEL))
    return pl.pallas_call(
        body, name=name, grid=(n // tm,),
        in_specs=[_rows(tm, f.shape[1]), _const(w_down.shape), wide, _const(w_pg.shape),
                  _rows(tm, p.shape[1]), _const(w_pp.shape), wide, gain, gain],
        out_specs=[wide] * 5 + [gain] * 3,
        out_shape=[_sds((n, D_MODEL), dt) for dt in (BF16, BF16, BF16, F32, BF16)]
                  + [_sds((1, D_MODEL), F32)] * 3,
        compiler_params=_params("arbitrary"),
    )(f, w_down, h1, w_pg, p, w_pp, target, g_final, g_ple)


def _assemble_dproj(dna, ddil_q, ddil_k, ddil_v, dgn, dgd, cos_t, sin_t, *, tm, name):
    n = dgn.shape[0]

    def body(*refs):
        dq_ref, dk_ref, dv_ref = refs[0:3]
        dil_in = refs[3:12]
        dgn_ref, dgd_ref, cos_ref, sin_ref, o_ref, scr = refs[12:18]
        o_ref[:, 0:512] = dq_ref[...]
        o_ref[:, 512:1024] = dk_ref[...].astype(BF16)
        o_ref[:, 1024:1536] = dv_ref[...].astype(BF16)
        cosv, sinv = cos_ref[...], sin_ref[...]
        for t in range(3):
            for gi, dil in enumerate(DIL_DILATIONS):
                val = _load_token_order(dil_in[t * 3 + gi], scr, dil, tm)
                if t < 2:
                    val = val * cosv + _swap_halves(val * sinv)
                c0 = 1536 + t * DIL_WIDTH + gi * 256
                o_ref[:, c0:c0 + 256] = val.astype(BF16)
        o_ref[:, 3840:4864] = dgn_ref[...]
        o_ref[:, 4864:5888] = dgd_ref[...]

    in_specs = [_rows(tm, NA_WIDTH)] * 3
    for _ in range(3):
        for dil in DIL_DILATIONS:
            in_specs.append(pl.BlockSpec((dil, tm // dil, 256), lambda i: (0, i, 0)))
    in_specs += [_rows(tm, D_MODEL)] * 2 + [_rows(tm, 256)] * 2
    return pl.pallas_call(
        body, name=name, grid=(n // tm,), in_specs=in_specs,
        out_specs=_rows(tm, IN_WIDTH), out_shape=_sds((n, IN_WIDTH), BF16),
        scratch_shapes=[_dil_scratch(tm)],
        compiler_params=_params("parallel"),
    )(*dna, *ddil_q, *ddil_k, *ddil_v, dgn, dgd, cos_t, sin_t)


N_ROW_OFF = 2 * NA_WIN_ROWS - 1
N_PAIRS = N_ROW_OFF - 1
RB_WIDTH = (N_ROW_OFF + 1) * GRID_W


def _na_bias(rb_ref, pair_scr):
    shape = (GRID_W, RB_WIDTH)
    qc = lax.broadcasted_iota(jnp.int32, shape, 0)
    qc2 = lax.broadcasted_iota(jnp.int32, (GRID_W, 128), 0)
    kc2 = lax.broadcasted_iota(jnp.int32, (GRID_W, 128), 1) & (GRID_W - 1)
    cs = jnp.clip(qc2 - 8, 0, GRID_W - 16)
    valid = (kc2 >= cs) & (kc2 < cs + 16)
    for hh in range(2):
        t = jnp.broadcast_to(rb_ref[hh], shape)
        t = pltpu.roll(t, RB_WIDTH - 15, 1)
        for b in range(6):
            t = jnp.where(((qc >> b) & 1) == 1, pltpu.roll(t, 1 << b, 1), t)
        t_odd = pltpu.roll(t, RB_WIDTH - GRID_W, 1)
        for ro in range(N_PAIRS):
            src = t if ro % 2 == 0 else t_odd
            base = (ro // 2) * 128
            pair_scr[hh, ro] = jnp.where(valid, src[:, base:base + 128], NEG_INF)


NA_GROUP_FWD = 16
NA_GROUP_BWD = 4


def _stack_heads(ref, r, scale=1.0):
    lane = lax.broadcasted_iota(jnp.int32, (GRID_W, 128), 1)
    t = ref[pl.ds(pl.multiple_of(r * GRID_W, GRID_W), GRID_W), :].astype(F32) * scale
    return jnp.concatenate([jnp.where(lane < 64, t, 0.0), jnp.where(lane >= 64, t, 0.0)], axis=0).astype(BF16)


def _unstack_heads(t2):
    lane = lax.broadcasted_iota(jnp.int32, (GRID_W, 128), 1)
    return jnp.where(lane < 64, t2[:GRID_W], t2[GRID_W:])


def _na_window(k_ref, v_ref, r, n_rows):
    rs = jnp.clip(r - NA_WIN_ROWS // 2, 0, n_rows - NA_WIN_ROWS)
    ro0 = (NA_WIN_ROWS - 1) - (r - rs)
    off = pl.multiple_of(rs * GRID_W, GRID_W)
    kw = k_ref[pl.ds(off, NA_WIN_ROWS * GRID_W), :]
    vw = v_ref[pl.ds(off, NA_WIN_ROWS * GRID_W), :]
    return kw, vw, off, ro0


def _na_probs(s_raw, pair_scr, ro0):
    bias = [jnp.concatenate([pair_scr[hh, ro0 + 2 * j] for j in range(NA_WIN_ROWS // 2)], axis=1)
            for hh in range(2)]
    s = s_raw + jnp.concatenate(bias, axis=0)
    m = jnp.max(s, axis=-1, keepdims=True)
    e = jnp.exp(s - m)
    return e * (1.0 / jnp.sum(e, axis=-1, keepdims=True))


def _na_qkv_specs(n):
    pairs = NA_WIDTH // 128
    return [pl.BlockSpec((n, 128), lambda h, first=t * pairs: (0, first + h)) for t in range(3)]


def _na_fwd(qkv, rb, *, name):
    n = qkv.shape[0]
    n_rows = n // GRID_W

    def body(ins, outs, scr):
        q_ref, k_ref, v_ref, rb_ref = ins
        o_ref, = outs
        pair_scr, = scr
        _na_bias(rb_ref, pair_scr)

        def group(g, carry):
            rows = [g * NA_GROUP_FWD + t for t in range(NA_GROUP_FWD)]
            wins = [_na_window(k_ref, v_ref, r, n_rows) for r in rows]
            raw = [lax.dot_general(_stack_heads(q_ref, r, QK_SCALE), w[0], NT_DIMS, preferred_element_type=F32)
                   for r, w in zip(rows, wins)]
            probs = [_na_probs(s, pair_scr, w[3]) for s, w in zip(raw, wins)]
            outs2 = [jnp.dot(p.astype(BF16), w[1], preferred_element_type=F32) for p, w in zip(probs, wins)]
            for r, o2 in zip(rows, outs2):
                o_ref[pl.ds(pl.multiple_of(r * GRID_W, GRID_W), GRID_W), :] = _unstack_heads(o2).astype(BF16)
            return carry

        lax.fori_loop(0, n_rows // NA_GROUP_FWD, group, 0)

    col = pl.BlockSpec((n, 128), lambda h: (0, h))
    return _call(
        body, name=name, grid=(NA_WIDTH // 128,),
        in_specs=_na_qkv_specs(n) + [pl.BlockSpec((2, 1, RB_WIDTH), lambda h: (h, 0, 0))],
        out_specs=[col], out_shape=[_sds((n, NA_WIDTH), BF16)],
        scratch_shapes=[pltpu.VMEM((2, N_PAIRS, GRID_W, 128), F32)],
        args=(qkv, qkv, qkv, rb))[0]


def _na_bwd(qkv, do, rb, *, name, after=None):
    n = qkv.shape[0]
    n_rows = n // GRID_W
    win = NA_WIN_ROWS * GRID_W

    def body(ins, outs, scr):
        q_ref, k_ref, v_ref, do_ref, rb_ref = ins
        dq_ref, dk_ref, dv_ref, drb_ref = outs
        pair_scr, acc_scr = scr
        _na_bias(rb_ref, pair_scr)
        acc_scr[...] = jnp.zeros_like(acc_scr)
        dk_ref[...] = jnp.zeros_like(dk_ref)
        dv_ref[...] = jnp.zeros_like(dv_ref)

        def group(g, carry):
            rows = [g * NA_GROUP_BWD + t for t in range(NA_GROUP_BWD)]
            wins = [_na_window(k_ref, v_ref, r, n_rows) for r in rows]
            qss = [_stack_heads(q_ref, r, QK_SCALE) for r in rows]
            doss = [_stack_heads(do_ref, r) for r in rows]
            raw = [lax.dot_general(qs, w[0], NT_DIMS, preferred_element_type=F32) for qs, w in zip(qss, wins)]
            dps = [lax.dot_general(dos, w[1], NT_DIMS, preferred_element_type=F32) for dos, w in zip(doss, wins)]
            probs = [_na_probs(s, pair_scr, w[3]) for s, w in zip(raw, wins)]
            dss = [p * (dp - jnp.sum(p * dp, axis=-1, keepdims=True)) for p, dp in zip(probs, dps)]
            dsbs = [ds.astype(BF16) for ds in dss]
            dq2s = [jnp.dot(dsb, w[0], preferred_element_type=F32) for dsb, w in zip(dsbs, wins)]
            dkws = [lax.dot_general(dsb, qs, TN_DIMS, preferred_element_type=F32) for dsb, qs in zip(dsbs, qss)]
            dvws = [lax.dot_general(p.astype(BF16), dos, TN_DIMS, preferred_element_type=F32)
                    for p, dos in zip(probs, doss)]
            for t, r in enumerate(rows):
                _, _, off, ro0 = wins[t]
                for hh in range(2):
                    for j in range(NA_WIN_ROWS // 2):
                        acc_scr[hh, ro0 + 2 * j] += dss[t][hh * GRID_W:(hh + 1) * GRID_W, j * 128:(j + 1) * 128]
                dq_ref[pl.ds(pl.multiple_of(r * GRID_W, GRID_W), GRID_W), :] = (
                    _unstack_heads(dq2s[t]) * QK_SCALE).astype(BF16)
                dk_ref[pl.ds(off, win), :] += dkws[t]
                dv_ref[pl.ds(off, win), :] += dvws[t]
            return carry

        lax.fori_loop(0, n_rows // NA_GROUP_BWD, group, 0)

        qc = lax.broadcasted_iota(jnp.int32, (N_PAIRS * GRID_W, 128), 0)
        for hh in range(2):
            t = acc_scr[hh].reshape(N_PAIRS * GRID_W, 128)
            for b in range(6):
                t = jnp.where(((qc >> b) & 1) == 1, pltpu.roll(t, 128 - (1 << b), 1), t)
            t = pltpu.roll(t, 15, 1)
            drb_ref[hh] = jnp.sum(t.reshape(N_PAIRS, GRID_W, 128), axis=1)

    col = pl.BlockSpec((n, 128), lambda h: (0, h))
    return _call(
        body, name=name, grid=(NA_WIDTH // 128,),
        in_specs=_na_qkv_specs(n) + [col, pl.BlockSpec((2, 1, RB_WIDTH), lambda h: (h, 0, 0))],
        out_specs=[col, col, col, pl.BlockSpec((2, N_PAIRS, 128), lambda h: (h, 0, 0))],
        out_shape=[_sds((n, NA_WIDTH), BF16), _sds((n, NA_WIDTH), F32), _sds((n, NA_WIDTH), F32),
                   _sds((8, N_PAIRS, 128), F32)],
        scratch_shapes=[pltpu.VMEM((2, N_PAIRS, GRID_W, 128), F32),
                        pltpu.VMEM((2, N_PAIRS, GRID_W, 128), F32)],
        args=(qkv, qkv, qkv, do, rb), after=after)


def _rpb_table(rpb2):
    t = jnp.pad(rpb2, ((0, 0), (0, 1), (0, GRID_W - rpb2.shape[-1])))
    return t.reshape(8, 1, RB_WIDTH)


def _rpb_grad(drb, *, name):
    kdim = drb.shape[1]

    def body(x_ref, o_ref):
        kk = lax.broadcasted_iota(jnp.int32, (128, 512), 0)
        jj = lax.broadcasted_iota(jnp.int32, (128, 512), 1)
        half, co = kk >> 6, kk & 63
        acc = jnp.zeros((8, 512), F32)
        for ro in range(N_PAIRS):
            hit = ((ro + half) == (jj >> 5)) & (co == (jj & 31)) & (co < 31)
            onehot = jnp.where(hit, 1.0, 0.0).astype(F32)
            acc = acc + jnp.dot(x_ref[:, ro * 128:(ro + 1) * 128], onehot, preferred_element_type=F32,
                                precision=lax.Precision.HIGHEST)
        o_ref[...] = acc

    return pl.pallas_call(
        body, name=name, grid=(1,),
        in_specs=[_const((8, kdim))], out_specs=_const((8, 512)), out_shape=_sds((8, 512), F32),
        compiler_params=_params("arbitrary"),
    )(drb)


DIL_GROUP = 4


def _dil_blocks(length):
    qb = min(128, length)
    return qb, min(qb + 2 * DIL_RADIUS, length), min(DIL_GROUP, length // qb)


def _stack_lanes(ref, t, qb, scale=1.0):
    lane = lax.broadcasted_iota(jnp.int32, (qb, 256), 1)
    val = ref[0, t * qb:(t + 1) * qb, :].astype(F32) * scale
    return jnp.concatenate([jnp.where((lane >> 6) == h, val, 0.0) for h in range(4)], axis=0).astype(BF16)


def _dil_window(k_ref, v_ref, blk, qb, win, length):
    start = pl.multiple_of(jnp.clip(blk * qb - DIL_RADIUS, 0, length - win), DIL_RADIUS)
    return k_ref[0, pl.ds(start, win), :], v_ref[0, pl.ds(start, win), :], start


def _dil_caps_init(caps_scr, qb, win):
    @pl.when((pl.program_id(0) == 0) & (pl.program_id(1) == 0))
    def _():
        gap = ((lax.broadcasted_iota(jnp.int32, (4 * qb, win), 0) & (qb - 1))
               - lax.broadcasted_iota(jnp.int32, (4 * qb, win), 1))
        for v in range(3):
            caps_scr[v] = jnp.where(jnp.abs(gap + v * DIL_RADIUS) <= DIL_RADIUS, jnp.inf, NEG_INF)


def _dil_mask(s, blk, start, qb, caps_scr):
    return jnp.minimum(s, caps_scr[(blk * qb - start) // DIL_RADIUS])


def _pick_heads(stacked, qb):
    lane = lax.broadcasted_iota(jnp.int32, (qb, 256), 1)
    out = jnp.zeros((qb, 256), stacked.dtype)
    for h in range(4):
        out = jnp.where((lane >> 6) == h, stacked[h * qb:(h + 1) * qb], out)
    return out


def _stack_head_cols(ref, t, qb):
    return jnp.concatenate([ref[0, t * qb:(t + 1) * qb, 64 * h:64 * h + 1] for h in range(4)], axis=0)


def _dil_fwd(q, k, v, *, name, after=None):
    dil, length, _ = q.shape
    qb, win, grp = _dil_blocks(length)
    extra = [] if after is None else [after]

    def body(q_ref, k_ref, v_ref, *rest):
        o_ref, lse_ref, caps_scr = rest[-3:]
        _dil_caps_init(caps_scr, qb, win)
        blks = [pl.program_id(1) * grp + t for t in range(grp)]
        wins = [_dil_window(k_ref, v_ref, b, qb, win, length) for b in blks]
        raw = [lax.dot_general(_stack_lanes(q_ref, t, qb, QK_SCALE), w[0], NT_DIMS, preferred_element_type=F32)
               for t, w in enumerate(wins)]
        lses, outs = [], []
        for t, (s, w) in enumerate(zip(raw, wins)):
            s = _dil_mask(s, blks[t], w[2], qb, caps_scr)
            m = jnp.max(s, axis=-1, keepdims=True)
            e = jnp.exp(s - m)
            norm = jnp.sum(e, axis=-1, keepdims=True)
            lses.append(m + jnp.log(norm))
            outs.append(jnp.dot((e * (1.0 / norm)).astype(BF16), w[1], preferred_element_type=F32))
        for t in range(grp):
            o_ref[0, t * qb:(t + 1) * qb, :] = _pick_heads(outs[t], qb)
            lse_ref[0, t * qb:(t + 1) * qb, :] = _pick_heads(jnp.broadcast_to(lses[t], (4 * qb, 256)), qb)

    seq = pl.BlockSpec((1, length, 256), lambda j, i: (j, 0, 0))
    blk = pl.BlockSpec((1, grp * qb, 256), lambda j, i: (j, i, 0))
    return pl.pallas_call(
        body, name=name, grid=(dil, length // (grp * qb)),
        in_specs=[blk, seq, seq] + [pl.BlockSpec(memory_space=pl.ANY)] * len(extra), out_specs=[blk, blk],
        out_shape=[_sds((dil, length, 256), F32)] * 2,
        scratch_shapes=[pltpu.VMEM((3, 4 * qb, win), F32)],
        compiler_params=_params("arbitrary", "arbitrary"),
    )(q, k, v, *extra)


def _dil_bwd(q, k, v, do, lse, cc, *, name):
    dil, length, _ = q.shape
    qb, win, grp = _dil_blocks(length)

    def body(q_ref, k_ref, v_ref, do_ref, lse_ref, cc_ref, dq_ref, dk_ref, dv_ref, caps_scr):
        _dil_caps_init(caps_scr, qb, win)

        @pl.when(pl.program_id(1) == 0)
        def _():
            dk_ref[...] = jnp.zeros_like(dk_ref)
            dv_ref[...] = jnp.zeros_like(dv_ref)

        blks = [pl.program_id(1) * grp + t for t in range(grp)]
        wins = [_dil_window(k_ref, v_ref, b, qb, win, length) for b in blks]
        qss = [_stack_lanes(q_ref, t, qb, QK_SCALE) for t in range(grp)]
        doss = [_stack_lanes(do_ref, t, qb) for t in range(grp)]
        raw = [lax.dot_general(qs, w[0], NT_DIMS, preferred_element_type=F32) for qs, w in zip(qss, wins)]
        dps = [lax.dot_general(dos, w[1], NT_DIMS, preferred_element_type=F32) for dos, w in zip(doss, wins)]
        probs = [jnp.exp(_dil_mask(s, blks[t], wins[t][2], qb, caps_scr) - _stack_head_cols(lse_ref, t, qb))
                 for t, s in enumerate(raw)]
        dsbs = [(p * (dp + _stack_head_cols(cc_ref, t, qb))).astype(BF16)
                for t, (p, dp) in enumerate(zip(probs, dps))]
        dq4s = [jnp.dot(dsb, w[0], preferred_element_type=F32) for dsb, w in zip(dsbs, wins)]
        dkws = [lax.dot_general(dsb, qs, TN_DIMS, preferred_element_type=F32) for dsb, qs in zip(dsbs, qss)]
        dvws = [lax.dot_general(p.astype(BF16), dos, TN_DIMS, preferred_element_type=F32)
                for p, dos in zip(probs, doss)]
        for t in range(grp):
            dq_ref[0, t * qb:(t + 1) * qb, :] = _pick_heads(dq4s[t], qb) * QK_SCALE
            dk_ref[0, pl.ds(wins[t][2], win), :] += dkws[t]
            dv_ref[0, pl.ds(wins[t][2], win), :] += dvws[t]

    seq = pl.BlockSpec((1, length, 256), lambda j, i: (j, 0, 0))
    blk = pl.BlockSpec((1, grp * qb, 256), lambda j, i: (j, i, 0))
    return pl.pallas_call(
        body, name=name, grid=(dil, length // (grp * qb)),
        in_specs=[blk, seq, seq, blk, blk, blk], out_specs=[blk, seq, seq],
        out_shape=[_sds((dil, length, 256), F32)] * 3,
        scratch_shapes=[pltpu.VMEM((3, 4 * qb, win), F32)],
        compiler_params=_params("arbitrary", "arbitrary"),
    )(q, k, v, do, lse, cc)


def _merge_weights(lses):
    m = jnp.maximum(jnp.maximum(lses[0], lses[1]), lses[2])
    es = [jnp.exp(t - m) for t in lses]
    inv = 1.0 / (es[0] + es[1] + es[2])
    return [e * inv for e in es]


def _branch_mix(y_na, w_bna, outs, lses, w_bd, gates, *, tm, name):
    n = y_na.shape[0]
    chunk = min(EPILOGUE_ROWS, tm)

    def body(yna_ref, wn_ref, *rest):
        o_in, l_in = rest[0:3], rest[3:6]
        wd_ref, sn_ref, sd_ref = rest[6:9]
        y_ref, yb_ref, bn_ref, bd_ref, mix_ref, scr = rest[9:15]
        for r0 in range(0, tm, chunk):
            rows = slice(r0, r0 + chunk)
            lv = [_load_token_order(l_in[g], scr, d, chunk, r0) for g, d in enumerate(DIL_DILATIONS)]
            ws = _merge_weights(lv)
            y = jnp.zeros((chunk, 256), F32)
            for g, d in enumerate(DIL_DILATIONS):
                y = y + ws[g] * _load_token_order(o_in[g], scr, d, chunk, r0)
            yb = y.astype(BF16)
            y_ref[rows, :] = y
            yb_ref[rows, :] = yb
            bn = lax.dot_general(yna_ref[rows, :], wn_ref[...], NT_DIMS, preferred_element_type=F32).astype(BF16)
            bd = lax.dot_general(yb, wd_ref[...], NT_DIMS, preferred_element_type=F32)
            bn_ref[rows, :] = bn
            bd, mixed = _gate_mix_tile(bd, sn_ref[rows, :], bn, sd_ref[rows, :])
            bd_ref[rows, :] = bd.astype(BF16)
            mix_ref[rows, :] = mixed.astype(BF16)

    specs = [_dil_spec(d, tm) for d in DIL_DILATIONS]
    return pl.pallas_call(
        body, name=name, grid=(n // tm,),
        in_specs=[_rows(tm, NA_WIDTH), _const(w_bna.shape)] + specs + specs
                 + [_const(w_bd.shape), _rows(tm, D_MODEL, 0), _rows(tm, D_MODEL, 1)],
        out_specs=[_rows(tm, 256)] * 2 + [_rows(tm, D_MODEL)] * 3,
        out_shape=[_sds((n, 256), F32), _sds((n, 256), BF16)] + [_sds((n, D_MODEL), BF16)] * 3,
        scratch_shapes=[_dil_scratch(chunk)],
        compiler_params=_params("parallel"),
    )(y_na, w_bna, *outs, *lses, w_bd, gates, gates)


def _branch_bwd(dh, w_out, gates, bn, bd, w_bna, w_bd, y, lses, *, tm, name):
    n = dh.shape[0]
    chunk = min(EPILOGUE_ROWS, tm)

    def body(dh_ref, wo_ref, sn_ref, sd_ref, bn_ref, bd_ref, wn_ref, wd_ref, y_ref, *rest):
        l_in = rest[0:3]
        dbn_ref, dbd_ref, dgn_ref, dgd_ref, dyna_ref = rest[3:8]
        do_out, cc_out, scr = rest[8:11], rest[11:14], rest[14]
        rr = lax.broadcasted_iota(jnp.int32, (256, 256), 0) >> 6
        cc = lax.broadcasted_iota(jnp.int32, (256, 256), 1) >> 6
        ones = jnp.where(rr == cc, 1.0, 0.0).astype(F32)
        for r0 in range(0, tm, chunk):
            rows = slice(r0, r0 + chunk)
            dm = lax.dot_general(dh_ref[rows, :], wo_ref[...], NT_DIMS, preferred_element_type=F32)
            dbn, dbd, dgn, dgd = (t.astype(BF16) for t in _gate_bwd_tile(
                dm, sn_ref[rows, :], bn_ref[rows, :], sd_ref[rows, :], bd_ref[rows, :]))
            dbn_ref[rows, :] = dbn
            dbd_ref[rows, :] = dbd
            dgn_ref[rows, :] = dgn
            dgd_ref[rows, :] = dgd
            dyna_ref[rows, :] = jnp.dot(dbn, wn_ref[...], preferred_element_type=F32).astype(BF16)
            dyv = jnp.dot(dbd, wd_ref[...], preferred_element_type=F32)
            lv = [_load_token_order(l_in[g], scr, d, chunk, r0) for g, d in enumerate(DIL_DILATIONS)]
            ws = _merge_weights(lv)
            tsum = jnp.dot(dyv * y_ref[rows, :], ones, preferred_element_type=F32,
                           precision=lax.Precision.HIGHEST)
            for g, d in enumerate(DIL_DILATIONS):
                _store_dil_order(ws[g] * dyv, do_out[g], scr, d, r0)
                _store_dil_order(-ws[g] * tsum, cc_out[g], scr, d, r0)

    specs = [_dil_spec(d, tm) for d in DIL_DILATIONS]
    wide = _rows(tm, D_MODEL)
    res = pl.pallas_call(
        body, name=name, grid=(n // tm,),
        in_specs=[wide, _const(w_out.shape), _rows(tm, D_MODEL, 0), _rows(tm, D_MODEL, 1), wide, wide,
                  _const(w_bna.shape), _const(w_bd.shape), _rows(tm, 256)] + specs,
        out_specs=[wide] * 4 + [_rows(tm, NA_WIDTH)] + specs + specs,
        out_shape=[_sds((n, D_MODEL), BF16)] * 4 + [_sds((n, NA_WIDTH), BF16)]
                  + [_sds((d, n // d, 256), BF16) for d in DIL_DILATIONS]
                  + [_sds((d, n // d, 256), F32) for d in DIL_DILATIONS],
        scratch_shapes=[_dil_scratch(chunk)],
        compiler_params=_params("parallel"),
    )(dh, w_out, gates, gates, bn, bd, w_bna, w_bd, y, *lses)
    return res[0], res[1], res[2], res[3], res[4], res[5:8], res[8:11]


_WEIGHTS = (("w_in", 1, 736), ("w_branch_na", 1, 128), ("w_branch_dil", 1, 128), ("w_out", 0, 128),
            ("w_up", 1, 512), ("w_down", 0, 512), ("w_ple_gate", 0, 128), ("w_ple_proj", 1, 128))
_W_IN, _W_BNA, _W_BD, _W_OUT, _W_UP, _W_DOWN, _W_PG, _W_PP = range(8)


def _to_full(gathered):
    return gathered.reshape(-1, gathered.shape[2])


def _to_chunks(widx, mat):
    return mat.reshape(N_DEV, _WEIGHTS[widx][2], mat.shape[1])


def _local_step(x, p_bf16, positions, target, g_mix, g_mlp, g_ple, g_final, rpb2,
                get_w_in, relay_rest, get_rest, send_grads):
    tm = 512
    half = HEAD_DIM // 2
    inv_freq = 10000.0 ** (-jnp.arange(half, dtype=F32) / half)
    ang = positions.astype(F32)[:, None] * inv_freq
    cos, sin = jnp.cos(ang), jnp.sin(ang)
    cos_t = jnp.tile(jnp.concatenate([cos, cos], axis=-1), (1, 4))
    sin_t = jnp.tile(jnp.concatenate([-sin, sin], axis=-1), (1, 4))
    rb = _rpb_table(rpb2)

    a = _rms_fwd(x, g_mix, tm=tm, name="rms_mix")
    w_in, token = get_w_in((a, cos_t, sin_t, p_bf16))
    na_qkv, gates, dq_g, dk_g, dv_g = _project_in(a, w_in, cos_t, sin_t, tm=512, name="mm_in", after=token)
    y_na = _na_fwd(na_qkv, rb, name="na_fwd")
    token = relay_rest(y_na)
    d_out, d_lse = [], []
    for g in range(3):
        o, lse = _dil_fwd(dq_g[g], dk_g[g], dv_g[g], name=f"dil_fwd{g}", after=token if g == 0 else None)
        d_out.append(o)
        d_lse.append(lse)
    w_bna, w_bd = get_rest(d_out[2], 0)
    y_dil, y_dil_b, bn, bd, mixed = _branch_mix(y_na, w_bna, d_out, d_lse, w_bd, gates, tm=tm, name="branch_mix")
    w_out, w_up, w_down, w_pg, w_pp = get_rest(mixed, 1)
    h1, c = _matmul(mixed, w_out, out_dtype=(F32, BF16), tm=512, tn=1024, tk=1024, name="mm_out",
                    extra=(x, g_mlp), epilogue=_residual_rms_tile)
    u, f = _matmul(c, w_up, tb=True, out_dtype=(BF16, BF16), tm=512, tn=2048, tk=1024, name="mm_up",
                   epilogue=lambda acc: (acc, jnp.square(jnp.maximum(acc, 0.0))))

    e, dpp, dgt, dh2, dh2_b, dg_final, loss, dg_ple = _tail_step(
        f, w_down, h1, w_pg, p_bf16, w_pp, target, g_final, g_ple, tm=256, name="tail_step")
    loss = loss[:, :128]
    gw_pp = _matmul(p_bf16, dpp, ta=True, transpose_out=True, out_dtype=BF16, tm=256, tn=1024, tk=2048,
                    name="mm_gw_pp")
    gw_pg = _matmul(e, dgt, ta=True, out_dtype=BF16, tm=512, tn=1024, tk=2048, name="mm_gw_pg")
    du = _matmul(dh2_b, w_down, tb=True, out_dtype=BF16, tm=512, tn=2048, tk=1024, name="mm_du",
                 extra=(u,), epilogue=lambda acc, uv: (acc * (2.0 * jnp.maximum(uv.astype(F32), 0.0)),))
    gw_down = _matmul(f, dh2_b, ta=True, out_dtype=BF16, tm=1024, tn=1024, tk=2048, name="mm_gw_down")
    gw_up = _matmul(c, du, ta=True, transpose_out=True, out_dtype=BF16, tm=512, tn=2048, tk=2048, name="mm_gw_up")
    dh1, dh1_b, dg_mlp = _matmul(
        du, w_up, out_dtype=(F32, BF16), tm=512, tn=1024, tk=4096, name="mm_dc",
        extra=(h1, g_mlp, dh2), epilogue=_rms_bwd_twice, n_colsum=1)
    dbn, dbd, dgn, dgd, dy_na, do_g, cc_g = _branch_bwd(dh1_b, w_out, gates, bn, bd, w_bna, w_bd, y_dil, d_lse,
                                                        tm=tm, name="branch_bwd")
    gw_out = _matmul(mixed, dh1_b, ta=True, out_dtype=BF16, tm=512, tn=1024, tk=2048, name="mm_gw_out")
    gw_bna = _matmul(y_na, dbn, ta=True, transpose_out=True, out_dtype=BF16, tm=512, tn=1024, tk=2048,
                     name="mm_gw_bna")
    gw_bd = _matmul(y_dil_b, dbd, ta=True, transpose_out=True, out_dtype=BF16, tm=256, tn=1024, tk=2048,
                    name="mm_gw_bd")
    token = send_grads((_W_PP, _W_PG, _W_DOWN, _W_UP, _W_OUT, _W_BNA, _W_BD),
                       (gw_pp, gw_pg, gw_down, gw_up, gw_out, gw_bna, gw_bd))
    dna = _na_bwd(na_qkv, dy_na, rb, name="na_bwd", after=token)
    drpb = _rpb_grad(dna[3].reshape(8, -1), name="rpb_grad")
    ddq, ddk, ddv = [], [], []
    for g in range(3):
        r = _dil_bwd(dq_g[g], dk_g[g], dv_g[g], do_g[g], d_lse[g], cc_g[g], name=f"dil_bwd{g}")
        ddq.append(r[0])
        ddk.append(r[1])
        ddv.append(r[2])
    dproj = _assemble_dproj(dna[0:3], ddq, ddk, ddv, dgn, dgd, cos_t, sin_t, tm=tm, name="assemble_dproj")
    gw_in = _matmul(a, dproj, ta=True, transpose_out=True, out_dtype=BF16, tm=512, tn=2944, tk=2048, name="mm_gw_in")
    token = send_grads((_W_IN,), (gw_in,))
    dx, dg_mix = _matmul(
        dproj, w_in, out_dtype=(F32,), tm=512, tn=1024, tk=5888, name="mm_da", after=token,
        extra=(x, g_mix, dh1), epilogue=_rms_bwd_tile, n_colsum=1)
    return loss, dx, (dg_mix, dg_mlp, dg_ple, dg_final), drpb


def _cast_bf16(t, *, name):
    def body(t_ref, o_ref):
        o_ref[...] = t_ref[...].astype(BF16)

    rows, cols = t.shape
    tr = 256 if rows % 256 == 0 else rows
    blk = pl.BlockSpec((tr, cols), lambda i: (i, 0))
    return pl.pallas_call(body, name=name, grid=(rows // tr,), in_specs=[blk], out_specs=blk,
                          out_shape=_sds(t.shape, BF16), compiler_params=_params("parallel"))(t)


def _adamw(w, g, m, v):
    m = ADAM_B1 * m + (1.0 - ADAM_B1) * g
    v = ADAM_B2 * v + (1.0 - ADAM_B2) * (g * g)
    m_hat = m / (1.0 - ADAM_B1 ** ADAM_STEP)
    v_hat = v / (1.0 - ADAM_B2 ** ADAM_STEP)
    delta = -ADAM_LR * (m_hat / (jnp.sqrt(v_hat) + ADAM_EPS) + ADAM_WD * w)
    return delta, m, v


def _sum_adamw(parts, w, m, v, *, tr, name, own=None, transposed=False):
    rows, cols = w.shape
    n_pre = 0 if own is None else 1

    def body(*refs):
        p_ref, w_ref, m_ref, v_ref = refs[n_pre:n_pre + 4]
        g_ref, d_ref, nm_ref, nv_ref = refs[-4:]
        g = (p_ref[0] if own is None else refs[n_pre + 4][...]).astype(F32)
        for s in range(1, N_DEV):
            g = g + p_ref[s].astype(F32)
        if transposed:
            g = g.T
        g_ref[...] = g
        d_ref[...], nm_ref[...], nv_ref[...] = _adamw(w_ref[...], g, m_ref[...], v_ref[...])

    if transposed:
        blk = pl.BlockSpec((rows, tr), lambda i, *_: (0, i))
        g_rows, steps = rows, cols // tr
    else:
        blk = pl.BlockSpec((tr, cols), lambda i, *_: (i, 0))
        g_rows, steps = cols, rows // tr
    in_specs = [pl.BlockSpec((N_DEV, tr, g_rows), lambda i, *_: (0, i, 0)), blk, blk, blk]
    args = [parts, w, m, v]
    if own is not None:
        in_specs.append(pl.BlockSpec((None, tr, g_rows), lambda i, idx: (idx[0], i, 0)))
        args = [own[1]] + args + [own[0]]
    return pl.pallas_call(
        body, name=name,
        grid_spec=pltpu.PrefetchScalarGridSpec(num_scalar_prefetch=n_pre, grid=(steps,), in_specs=in_specs,
                                               out_specs=[blk] * 4),
        out_shape=[_sds((rows, cols), F32)] * 4,
        compiler_params=_params("parallel"),
    )(*args)


_RPB_SIZE = 8 * 15 * 31


def _pack_small(g_mix, g_mlp, g_ple, g_final, rpb, loss_row):
    flat = jnp.concatenate([g_mix.reshape(-1), g_mlp.reshape(-1), g_ple.reshape(-1), g_final.reshape(-1),
                            rpb.reshape(-1), jnp.zeros((3840 - _RPB_SIZE,), F32), loss_row.reshape(-1),
                            jnp.zeros((128,), F32)])
    return flat.reshape(64, 128)


def _unpack_small(t):
    flat = t.reshape(-1)
    return (flat[0:1024].reshape(1, 1024), flat[4096:4096 + _RPB_SIZE].reshape(1, 8, 15, 31),
            flat[1024:2048].reshape(1, 1024), flat[2048:3072].reshape(1, 1024), flat[3072:4096])


def kernel(x, p, positions, g_mix, w_in, rpb, w_branch_na, w_branch_dil, w_out, g_mlp, w_up, w_down, g_ple, w_ple_gate, w_ple_proj, g_final, loss_target, m_g_mix, m_w_in, m_rpb, m_w_branch_na, m_w_branch_dil, m_w_out, m_g_mlp, m_w_up, m_w_down, m_g_ple, m_w_ple_gate, m_w_ple_proj, m_g_final, v_g_mix, v_w_in, v_rpb, v_w_branch_na, v_w_branch_dil, v_w_out, v_g_mlp, v_w_up, v_w_down, v_g_ple, v_w_ple_gate, v_w_ple_proj, v_g_final):
    sharded = dict(w_in=(w_in, m_w_in, v_w_in), w_branch_na=(w_branch_na, m_w_branch_na, v_w_branch_na),
                   w_branch_dil=(w_branch_dil, m_w_branch_dil, v_w_branch_dil), w_out=(w_out, m_w_out, v_w_out),
                   w_up=(w_up, m_w_up, v_w_up), w_down=(w_down, m_w_down, v_w_down),
                   w_ple_gate=(w_ple_gate, m_w_ple_gate, v_w_ple_gate),
                   w_ple_proj=(w_ple_proj, m_w_ple_proj, v_w_ple_proj))
    shards = {k: tuple(t[0] for t in val) for k, val in sharded.items()}

    me = _my_index()

    shards["w_in"] = tuple(t.T for t in shards["w_in"])

    w_in_b = _cast_bf16(shards["w_in"][0], name="cast_w_in")
    rest_b = [shards[name][0].astype(BF16).T if axis == 1 else shards[name][0].astype(BF16)
              for name, axis, _ in _WEIGHTS[1:]]
    first_in, token_in = _start_copies(_first_leg_copies, [w_in_b], [_sds((N_DEV,) + w_in_b.shape, BF16)], 4,
                                       name="start_gather_w_in")

    def whole(landed, mine):
        return _to_full(lax.dynamic_update_index_in_dim(landed, mine, me, 0))

    rest = {}

    def get_w_in(after):
        (mine,), landed = _wait_copies(_first_leg_copies, first_in, (*after, *rest_b), name="wait_gather_w_in")
        second, token = _start_copies(_second_leg_copies, [], landed, 3, name="start_forward_w_in")
        _, (landed,) = _wait_copies(_second_leg_copies, second, token, name="wait_forward_w_in")
        rest["first"], token = _start_copies(_first_leg_copies, rest_b,
                                             [_sds((N_DEV,) + t.shape, BF16) for t in rest_b], 4 * len(rest_b),
                                             name="start_gather_rest", after=landed)
        return whole(landed, mine), token

    def relay_rest(after):
        rest["mine"], landed = _wait_copies(_first_leg_copies, rest["first"], after, name="wait_gather_rest")
        rest["second"], token = _start_copies(_second_leg_copies, [], landed, 3 * len(rest_b),
                                              name="start_forward_rest")
        return token

    def get_rest(after, stage):
        n_src, send_sems, recv_sems, bufs = rest["second"]
        part = slice(0, 2) if stage == 0 else slice(2, len(rest_b))
        _, landed = _wait_copies(functools.partial(_second_leg_copies, first=part.start),
                                 (n_src, send_sems, recv_sems, bufs[part]), after,
                                 name=f"wait_forward_rest{stage}")
        return [whole(t, own) for t, own in zip(landed, rest["mine"][part])]

    sent = []

    def send_grads(indices, grads):
        chunked = [_to_chunks(i, g) for i, g in zip(indices, grads)]
        handle, token = _start_copies(_exchange_copies, chunked, [_sds(t.shape, BF16) for t in chunked],
                                      7 * len(chunked),
                                      name="start_exchange_" + ("w_in" if indices == (_W_IN,) else "rest"))
        sent.append((indices, handle))
        return token

    g_mix_0 = g_mix + token_in[0:1, 0:1]
    loss, dx, dgs, drpb = _local_step(
        x[0], p[0, 0].astype(BF16), positions[0], loss_target[0],
        g_mix_0, g_mlp, g_ple, g_final.reshape(1, -1), rpb[0], get_w_in, relay_rest, get_rest, send_grads)

    drpb3 = drpb.reshape(8, 16, 32)[:, :15, :31]
    small = _pack_small(dgs[0], dgs[1], dgs[2], dgs[3], drpb3, loss)
    share, done = _start_copies(_gather_copies, [small], [_sds((N_DEV,) + small.shape, F32)], 7,
                                name="start_share_small")

    out = {}
    for indices, handle in sent:
        chunked, landed = _wait_copies(_exchange_copies, handle, done,
                                       name="wait_exchange_" + ("w_in" if indices == (_W_IN,) else "rest"))
        for i, part, mine in zip(indices, landed, chunked):
            name = _WEIGHTS[i][0]
            w, m, v = shards[name]
            turned = _WEIGHTS[i][1] == 1 and i != _W_IN
            res = _sum_adamw(part, w, m, v, tr=368 if i == _W_IN else 128, name="adamw_" + name,
                             own=(mine, me.reshape(1).astype(jnp.int32)), transposed=turned)
            out[name] = [(t.T if i == _W_IN else t)[None] for t in res]
            done = res[0]
    (small,), (small_landed,) = _wait_copies(_gather_copies, share, done, name="wait_share_small")
    small_all = lax.dynamic_update_index_in_dim(small_landed, small, me, 0)
    small_w = _pack_small(g_mix, g_mlp, g_ple, g_final, rpb, jnp.zeros((128,), F32))
    small_m = _pack_small(m_g_mix, m_g_mlp, m_g_ple, m_g_final, m_rpb, jnp.zeros((128,), F32))
    small_v = _pack_small(v_g_mix, v_g_mlp, v_g_ple, v_g_final, v_rpb, jnp.zeros((128,), F32))
    res = _sum_adamw(small_all, small_w, small_m, small_v, tr=64, name="adamw_small")
    unpacked = [_unpack_small(t) for t in res]
    for i, name in enumerate(("g_mix", "rpb", "g_mlp", "g_ple", "g_final")):
        out[name] = [u[i] for u in unpacked]
    loss_total = res[0][62, 0]

    order = ("g_mix", "w_in", "rpb", "w_branch_na", "w_branch_dil", "w_out", "g_mlp", "w_up", "w_down",
             "g_ple", "w_ple_gate", "w_ple_proj", "g_final")
    grads = [out[k][0] for k in order]
    deltas = [out[k][1] for k in order]
    new_m = [out[k][2] for k in order]
    new_v = [out[k][3] for k in order]
    return (loss_total, dx[None], *grads, *deltas, *new_m, *new_v)
```

```python
import functools

import jax
import jax.numpy as jnp
from jax import lax
from jax.experimental import pallas as pl
from jax.experimental.pallas import tpu as pltpu

F32 = jnp.float32
BF16 = jnp.bfloat16

D_MODEL = 1024
HEAD_DIM = 64
GRID_W = 64
NA_WIDTH = 512
DIL_WIDTH = 768
IN_WIDTH = 5888
DIL_DILATIONS = (1, 4, 16)
DIL_RADIUS = 64
NA_WIN_ROWS = 8
RMS_EPS = 1e-6
NEG_INF = -1e30
QK_SCALE = HEAD_DIM ** -0.5

ADAM_LR = 0.001
ADAM_B1 = 0.9
ADAM_B2 = 0.999
ADAM_EPS = 1e-08
ADAM_WD = 0.01
ADAM_STEP = 10

N_DEV = 8
VMEM_LIMIT = 56 * 1024 * 1024
EPILOGUE_ROWS = 256
MESH = pl.DeviceIdType.MESH

NT_DIMS = (((1,), (1,)), ((), ()))
TN_DIMS = (((0,), (0,)), ((), ()))


def _sds(shape, dtype):
    return jax.ShapeDtypeStruct(shape, dtype)


def _params(*sem):
    return pltpu.CompilerParams(dimension_semantics=sem, vmem_limit_bytes=VMEM_LIMIT)


def _rows(tm, width, col=0):
    return pl.BlockSpec((tm, width), lambda i, c=col: (i, c))


def _const(shape):
    zeros = (0,) * len(shape)
    return pl.BlockSpec(shape, lambda i: zeros)


def _my_index():
    return 4 * lax.axis_index("x") + 2 * lax.axis_index("y") + lax.axis_index("c")


def _peer(k):
    x, y, c = lax.axis_index("x"), lax.axis_index("y"), lax.axis_index("c")
    px = 1 - x if k & 4 else x
    py = 1 - y if k & 2 else y
    pc = 1 - c if k & 1 else c
    return (px, py, pc), 4 * px + 2 * py + pc


def _call(body, *, name, grid, in_specs, out_specs, out_shape, scratch_shapes, args, after=None):
    n_in, n_out = len(in_specs), len(out_specs)
    extra = [] if after is None else [after]
    n_x = n_in + len(extra)

    def plain(*refs):
        body(refs[:n_in], refs[n_x:n_x + n_out], refs[n_x + n_out:])

    res = pl.pallas_call(plain, name=name, grid=grid,
                         in_specs=list(in_specs) + [pl.BlockSpec(memory_space=pl.ANY)] * len(extra),
                         out_specs=out_specs, out_shape=out_shape, scratch_shapes=scratch_shapes,
                         compiler_params=_params(*(("arbitrary",) * len(grid))))(*args, *extra)
    return list(res)


_HBM_SPEC = pl.BlockSpec(memory_space=pltpu.HBM)
_SEM_SPEC = pl.BlockSpec(memory_space=pltpu.SEMAPHORE)
_SIDE_EFFECT = pltpu.SideEffectType.DATAFLOW_SIDE_EFFECTING


_FIRST_LEG = (1, 2, 4, 6)
_SECOND_LEG = (2, 4, 6)


def _gather_copies(srcs, lands, send, recv, sending):
    me = _my_index()
    out = []
    for w in range(len(srcs)):
        for k in range(1, N_DEV):
            dev, idx = _peer(k)
            out.append(pltpu.make_async_remote_copy(
                src_ref=srcs[w], dst_ref=lands[w].at[me if sending else idx],
                send_sem=send.at[w * 7 + k - 1], recv_sem=recv.at[w * 7 + k - 1],
                device_id=dev, device_id_type=MESH))
    return out


def _first_leg_copies(srcs, lands, send, recv, sending):
    me = _my_index()
    out = []
    for w in range(len(srcs)):
        for j, k in enumerate(_FIRST_LEG):
            dev, idx = _peer(k)
            out.append(pltpu.make_async_remote_copy(
                src_ref=srcs[w], dst_ref=lands[w].at[me if sending else idx],
                send_sem=send.at[w * 4 + j], recv_sem=recv.at[w * 4 + j],
                device_id=dev, device_id_type=MESH))
    return out


def _second_leg_copies(srcs, lands, send, recv, sending, first=0):
    sibling, _ = _peer(1)
    out = []
    for w in range(len(lands)):
        for j, k in enumerate(_SECOND_LEG):
            slot = _peer(k if sending else k ^ 1)[1]
            sem = (first + w) * 3 + j
            out.append(pltpu.make_async_remote_copy(
                src_ref=lands[w].at[slot], dst_ref=lands[w].at[slot],
                send_sem=send.at[sem], recv_sem=recv.at[sem],
                device_id=sibling, device_id_type=MESH))
    return out


def _exchange_copies(srcs, lands, send, recv, sending):
    out = []
    for w in range(len(srcs)):
        for k in range(1, N_DEV):
            dev, idx = _peer(k)
            out.append(pltpu.make_async_remote_copy(
                src_ref=srcs[w].at[idx], dst_ref=lands[w].at[k],
                send_sem=send.at[w * 7 + k - 1], recv_sem=recv.at[w * 7 + k - 1],
                device_id=dev, device_id_type=MESH))
    return out


def _start_copies(make, srcs, lands, n_copies, *, name, after=None):
    n_src, n_buf = len(srcs), len(srcs) + len(lands)
    extra = [] if after is None else [after]

    def body(*refs):
        send, recv = refs[n_buf + len(extra)], refs[n_buf + len(extra) + 1]
        for cp in make(refs[:n_src], refs[n_src:n_buf], send, recv, True):
            cp.start()
        refs[-1][...] = jnp.zeros_like(refs[-1])

    bufs = list(srcs) + [lax.empty(t.shape, t.dtype) if isinstance(t, jax.ShapeDtypeStruct) else t for t in lands]
    res = pl.pallas_call(
        body, name=name,
        out_shape=(pltpu.SemaphoreType.DMA((n_copies,)), pltpu.SemaphoreType.DMA((n_copies,)),
                   *[pltpu.HBM(t.shape, t.dtype) for t in bufs], _sds((8, 128), F32)),
        in_specs=[_HBM_SPEC] * n_buf + [pl.BlockSpec(memory_space=pl.ANY)] * len(extra),
        out_specs=(_SEM_SPEC, _SEM_SPEC, *([_HBM_SPEC] * n_buf), pl.BlockSpec(memory_space=pltpu.VMEM)),
        input_output_aliases={i: 2 + i for i in range(n_buf)},
        compiler_params=pltpu.CompilerParams(has_side_effects=_SIDE_EFFECT),
    )(*[pltpu.with_memory_space_constraint(t, pltpu.HBM) for t in bufs], *extra)
    return (n_src, res[0], res[1], res[2:2 + n_buf]), res[-1]


def _wait_copies(make, handle, after, *, name):
    n_src, send_sems, recv_sems, bufs = handle
    n_buf = len(bufs)
    after = list(after) if isinstance(after, (tuple, list)) else [after]

    def body(*refs):
        for cp in make(refs[:n_src], refs[n_src:n_buf], refs[n_buf], refs[n_buf + 1], False):
            cp.wait_send()
            cp.wait_recv()

    res = pl.pallas_call(
        body, name=name,
        out_shape=tuple(pltpu.HBM(t.shape, t.dtype) for t in bufs),
        in_specs=[_HBM_SPEC] * n_buf + [_SEM_SPEC, _SEM_SPEC] + [pl.BlockSpec(memory_space=pl.ANY)] * len(after),
        out_specs=tuple([_HBM_SPEC] * n_buf),
        input_output_aliases={i: i for i in range(n_buf)},
        compiler_params=pltpu.CompilerParams(has_side_effects=_SIDE_EFFECT),
    )(*bufs, send_sems, recv_sems, *after)
    return list(res[:n_src]), list(res[n_src:])


def _add_colsums(s_refs, sums, step):
    for s_ref, val in zip(s_refs, sums):
        @pl.when(step == 0)
        def _(s_ref=s_ref, val=val):
            s_ref[...] = val

        @pl.when(step > 0)
        def _(s_ref=s_ref, val=val):
            s_ref[...] += val


def _matmul(a, b, *, ta=False, tb=False, out_dtype, tm, tn, tk, name, after=None, extra=(), epilogue=None,
            n_colsum=0, transpose_out=False):
    m, k = (a.shape[1], a.shape[0]) if ta else a.shape
    n = b.shape[0] if tb else b.shape[1]
    tm, tn, tk = min(tm, m), min(tn, n), min(tk, k)
    nk = k // tk
    dims = (((0 if ta else 1,), (1 if tb else 0,)), ((), ()))
    out_dtypes = out_dtype if isinstance(out_dtype, tuple) else (out_dtype,)
    n_tiles = len(out_dtypes)

    def add_colsums(o_refs, sums):
        _add_colsums(o_refs[n_tiles:], sums, pl.program_id(1))

    def finish(acc, x_refs, o_refs):
        vals = (acc,) if epilogue is None else epilogue(acc, *[r[...] for r in x_refs])
        for o_ref, val in zip(o_refs[:n_tiles], vals[:n_tiles]):
            o_ref[...] = (val.T if transpose_out else val).astype(o_ref.dtype)
        add_colsums(o_refs, vals[n_tiles:])

    chunk = EPILOGUE_ROWS if (nk == 1 and epilogue is not None and not ta and tm % EPILOGUE_ROWS == 0) else None

    def body(ins, outs, acc):
        a_ref, b_ref = ins[:2]
        if chunk is not None:
            sums = None
            for r0 in range(0, tm, chunk):
                part = lax.dot_general(a_ref[r0:r0 + chunk, :], b_ref[...], dims, preferred_element_type=F32)
                vals = epilogue(part, *[r[...] if r.shape[0] == 1 else r[r0:r0 + chunk, :] for r in ins[2:]])
                for o_ref, val in zip(outs[:n_tiles], vals[:n_tiles]):
                    o_ref[r0:r0 + chunk, :] = val.astype(o_ref.dtype)
                sums = vals[n_tiles:] if sums is None else [s + v for s, v in zip(sums, vals[n_tiles:])]
            add_colsums(outs, sums)
            return
        part = lax.dot_general(a_ref[...], b_ref[...], dims, preferred_element_type=F32)
        if nk == 1:
            finish(part, ins[2:], outs)
            return
        acc_ref, = acc
        kk = pl.program_id(2)

        @pl.when(kk == 0)
        def _():
            acc_ref[...] = part

        @pl.when(kk > 0)
        def _():
            acc_ref[...] += part

        @pl.when(kk == nk - 1)
        def _():
            finish(acc_ref[...], ins[2:], outs)

    a_spec = (pl.BlockSpec((tk, tm), lambda j, i, kk: (kk, i)) if ta
              else pl.BlockSpec((tm, tk), lambda j, i, kk: (i, kk)))
    b_spec = (pl.BlockSpec((tn, tk), lambda j, i, kk: (j, kk)) if tb
              else pl.BlockSpec((tk, tn), lambda j, i, kk: (kk, j)))
    tile = pl.BlockSpec((tm, tn), lambda j, i, kk: (i, j))
    row = pl.BlockSpec((1, tn), lambda j, i, kk: (0, j))

    out_tile, out_dims = (pl.BlockSpec((tn, tm), lambda j, i, kk: (j, i)), (n, m)) if transpose_out else (tile, (m, n))
    res = _call(
        body, name=name, grid=(n // tn, m // tm, nk),
        in_specs=[a_spec, b_spec] + [row if t.shape[0] == 1 else tile for t in extra],
        out_specs=[out_tile] * n_tiles + [row] * n_colsum,
        out_shape=[_sds(out_dims, dt) for dt in out_dtypes] + [_sds((1, n), F32)] * n_colsum,
        scratch_shapes=[] if nk == 1 else [pltpu.VMEM((tm, tn), F32)],
        args=(a, b, *extra), after=after)
    return res if isinstance(out_dtype, tuple) or n_colsum else res[0]


def _rstd(h):
    return lax.rsqrt(jnp.mean(h * h, axis=-1, keepdims=True) + RMS_EPS)


def _sigmoid(z):
    return 1.0 / (1.0 + jnp.exp(-z))


def _rms_fwd(x, g, *, tm, name):
    n = x.shape[0]

    def body(x_ref, g_ref, o_ref):
        h = x_ref[...]
        o_ref[...] = (h * _rstd(h) * g_ref[...]).astype(BF16)

    return pl.pallas_call(
        body, name=name, grid=(n // tm,),
        in_specs=[_rows(tm, D_MODEL), _const((1, D_MODEL))],
        out_specs=_rows(tm, D_MODEL), out_shape=_sds((n, D_MODEL), BF16),
        compiler_params=_params("parallel"),
    )(x, g)


def _swap_halves(t):
    lane = lax.broadcasted_iota(jnp.int32, (t.shape[0], 128), 1)
    pieces = [t[:, c:c + 128] for c in range(0, t.shape[1], 128)]
    return jnp.concatenate([jnp.where((lane & 63) < 32, pltpu.roll(h, 96, 1), pltpu.roll(h, 32, 1))
                            for h in pieces], axis=1)


def _dil_spec(dil, tm):
    return pl.BlockSpec((dil, tm // dil, 256), lambda i: (0, i, 0))


def _dil_scratch(tm):
    return pltpu.VMEM((2, tm, 128), F32)


def _load_token_order(src, scr, dil, rows, row0=0):
    if dil == 1:
        return src[0, row0:row0 + rows, :]
    for j in range(dil):
        for c in range(2):
            scr[c, pl.ds(j, rows // dil, stride=dil), :] = (
                src[j, row0 // dil:(row0 + rows) // dil, c * 128:(c + 1) * 128])
    return jnp.concatenate([scr[0, 0:rows, :], scr[1, 0:rows, :]], axis=1)


def _store_dil_order(val, dst, scr, dil, row0=0):
    rows = val.shape[0]
    if dil == 1:
        dst[0, row0:row0 + rows, :] = val.astype(dst.dtype)
        return
    for c in range(2):
        scr[c] = val[:, c * 128:(c + 1) * 128]
    for j in range(dil):
        for c in range(2):
            dst[j, row0 // dil:(row0 + rows) // dil, c * 128:(c + 1) * 128] = (
                scr[c, pl.ds(j, rows // dil, stride=dil), :].astype(dst.dtype))


def _project_in(a, w_t, cos_t, sin_t, *, tm, name, after=None):
    n = a.shape[0]
    n_dil = len(DIL_DILATIONS)
    na_w, dil_w = 3 * NA_WIDTH, 3 * DIL_WIDTH
    chunk = min(EPILOGUE_ROWS, tm)
    extra = [] if after is None else [after]

    def body(a_ref, w_ref, cos_ref, sin_ref, *rest):
        na_ref, gate_ref = rest[len(extra):len(extra) + 2]
        outs, scr = rest[len(extra) + 2:len(extra) + 2 + 3 * n_dil], rest[-1]

        def part(r0, first, width):
            return lax.dot_general(a_ref[r0:r0 + chunk, :], w_ref[first:first + width, :], NT_DIMS,
                                   preferred_element_type=F32)

        for r0 in range(0, tm, chunk):
            na_ref[r0:r0 + chunk, :] = part(r0, 0, na_w).astype(BF16)
            dil_part = part(r0, na_w, dil_w)
            cosv, sinv = cos_ref[r0:r0 + chunk, :], sin_ref[r0:r0 + chunk, :]
            for t in range(3):
                for gi, dil in enumerate(DIL_DILATIONS):
                    c0 = (t * n_dil + gi) * 256
                    val = dil_part[:, c0:c0 + 256]
                    if t < 2:
                        val = val * cosv + _swap_halves(val) * sinv
                    _store_dil_order(val, outs[t * n_dil + gi], scr, dil, r0)
            gate_ref[r0:r0 + chunk, :] = _sigmoid(part(r0, na_w + dil_w, 2 * D_MODEL)).astype(BF16)

    out_specs = [_rows(tm, na_w), _rows(tm, 2 * D_MODEL)]
    out_shape = [_sds((n, na_w), BF16), _sds((n, 2 * D_MODEL), BF16)]
    for _ in range(3):
        for dil in DIL_DILATIONS:
            out_specs.append(pl.BlockSpec((dil, tm // dil, 256), lambda i: (0, i, 0)))
            out_shape.append(_sds((dil, n // dil, 256), BF16))
    res = pl.pallas_call(
        body, name=name, grid=(n // tm,),
        in_specs=[_rows(tm, D_MODEL), _const(w_t.shape), _rows(tm, 256), _rows(tm, 256)]
                 + [pl.BlockSpec(memory_space=pl.ANY)] * len(extra),
        out_specs=out_specs, out_shape=out_shape,
        scratch_shapes=[pltpu.VMEM((2, chunk, 128), F32)],
        compiler_params=_params("parallel"),
    )(a, w_t, cos_t, sin_t, *extra)
    return res[0], res[1], res[2:5], res[5:8], res[8:11]


def _residual_rms_tile(delta, h, g):
    hn = h + delta
    return hn, hn * _rstd(hn) * g


def _gate_mix_tile(b2, s1, b1, s2):
    return b2, s1.astype(F32) * b1.astype(F32) + s2.astype(F32) * b2


def _gate_bwd_tile(dm, s1, b1, s2, b2):
    s1, b1, s2, b2 = (t.astype(F32) for t in (s1, b1, s2, b2))
    return dm * s1, dm * s2, dm * b1 * s1 * (1.0 - s1), dm * b2 * s2 * (1.0 - s2)


def _tail_tile(gt, pp, h2, target, g):
    sg = _sigmoid(gt)
    h3 = h2 + sg * pp
    r3 = _rstd(h3)
    n3 = h3 * r3
    err = n3 * g - target
    loss = 0.5 * jnp.sum(jnp.sum(err * err, axis=-1, keepdims=True) / D_MODEL)
    dy = err / D_MODEL
    dn = dy * g
    dh3 = r3 * (dn - n3 * jnp.mean(dn * n3, axis=-1, keepdims=True))
    return (dh3, dh3 * sg, dh3 * pp * sg * (1.0 - sg),
            jnp.sum(dy * n3, axis=0, keepdims=True), jnp.full((1, gt.shape[1]), loss, F32))


def _rms_bwd_tile(dz, h, g, dres):
    r = _rstd(h)
    nrm = h * r
    dn = dz * g
    dh = dres + r * (dn - nrm * jnp.mean(dn * nrm, axis=-1, keepdims=True))
    return dh, jnp.sum(dz * nrm, axis=0, keepdims=True)


def _rms_bwd_twice(dz, h, g, dres):
    dh, dg = _rms_bwd_tile(dz, h, g, dres)
    return dh, dh, dg


def _tail_step(f, w_down, h1, w_pg, p, w_pp, target, g_final, g_ple, *, tm, name):
    n = f.shape[0]
    chunk = min(EPILOGUE_ROWS, tm)

    def body(f_ref, wd_ref, h1_ref, wg_ref, p_ref, wp_ref, t_ref, gf_ref, gp_ref,
             e_ref, dpp_ref, dgt_ref, dh2_ref, dh2b_ref, dgf_ref, loss_ref, dgp_ref):
        sums = None
        for r0 in range(0, tm, chunk):
            rows = slice(r0, r0 + chunk)
            delta = jnp.dot(f_ref[rows, :], wd_ref[...], preferred_element_type=F32)
            h2, e = _residual_rms_tile(delta, h1_ref[rows, :], gp_ref[...])
            e = e.astype(BF16)
            e_ref[rows, :] = e
            gt = jnp.dot(e, wg_ref[...], preferred_element_type=F32)
            pp = lax.dot_general(p_ref[rows, :], wp_ref[...], NT_DIMS, preferred_element_type=F32)
            dh3, dpp, dgt, dgf, loss = _tail_tile(gt, pp, h2, t_ref[rows, :], gf_ref[...])
            dgt = dgt.astype(BF16)
            dpp_ref[rows, :] = dpp.astype(BF16)
            dgt_ref[rows, :] = dgt
            dz = lax.dot_general(dgt, wg_ref[...], NT_DIMS, preferred_element_type=F32)
            dh2, dgp = _rms_bwd_tile(dz, h2, gp_ref[...], dh3)
            dh2_ref[rows, :] = dh2
            dh2b_ref[rows, :] = dh2.astype(BF16)
            vals = (dgf, loss, dgp)
            sums = vals if sums is None else [s + v for s, v in zip(sums, vals)]
        _add_colsums((dgf_ref, loss_ref, dgp_ref), sums, pl.program_id(0))

    wide, gain = _rows(tm, D_MODEL), _const((1, D_MODEL))
    return pl.pallas_call(
        body, name=name, grid=(n // tm,),
        in_specs=[_rows(tm, f.shape[1]), _const(w_down.shape), wide, _const(w_pg.shape),
                  _rows(tm, p.shape[1]), _const(w_pp.shape), wide, gain, gain],
        out_specs=[wide] * 5 + [gain] * 3,
        out_shape=[_sds((n, D_MODEL), dt) for dt in (BF16, BF16, BF16, F32, BF16)]
                  + [_sds((1, D_MODEL), F32)] * 3,
        compiler_params=_params("arbitrary"),
    )(f, w_down, h1, w_pg, p, w_pp, target, g_final, g_ple)


def _assemble_dproj(dna, ddil_q, ddil_k, ddil_v, dgn, dgd, cos_t, sin_t, *, tm, name):
    n = dgn.shape[0]

    def body(*refs):
        dq_ref, dk_ref, dv_ref = refs[0:3]
        dil_in = refs[3:12]
        dgn_ref, dgd_ref, cos_ref, sin_ref, o_ref, scr = refs[12:18]
        o_ref[:, 0:512] = dq_ref[...]
        o_ref[:, 512:1024] = dk_ref[...].astype(BF16)
        o_ref[:, 1024:1536] = dv_ref[...].astype(BF16)
        cosv, sinv = cos_ref[...], sin_ref[...]
        for t in range(3):
            for gi, dil in enumerate(DIL_DILATIONS):
                val = _load_token_order(dil_in[t * 3 + gi], scr, dil, tm)
                if t < 2:
                    val = val * cosv + _swap_halves(val * sinv)
                c0 = 1536 + t * DIL_WIDTH + gi * 256
                o_ref[:, c0:c0 + 256] = val.astype(BF16)
        o_ref[:, 3840:4864] = dgn_ref[...]
        o_ref[:, 4864:5888] = dgd_ref[...]

    in_specs = [_rows(tm, NA_WIDTH)] * 3
    for _ in range(3):
        for dil in DIL_DILATIONS:
            in_specs.append(pl.BlockSpec((dil, tm // dil, 256), lambda i: (0, i, 0)))
    in_specs += [_rows(tm, D_MODEL)] * 2 + [_rows(tm, 256)] * 2
    return pl.pallas_call(
        body, name=name, grid=(n // tm,), in_specs=in_specs,
        out_specs=_rows(tm, IN_WIDTH), out_shape=_sds((n, IN_WIDTH), BF16),
        scratch_shapes=[_dil_scratch(tm)],
        compiler_params=_params("parallel"),
    )(*dna, *ddil_q, *ddil_k, *ddil_v, dgn, dgd, cos_t, sin_t)


N_ROW_OFF = 2 * NA_WIN_ROWS - 1
N_PAIRS = N_ROW_OFF - 1
RB_WIDTH = (N_ROW_OFF + 1) * GRID_W


def _na_bias(rb_ref, pair_scr):
    shape = (GRID_W, RB_WIDTH)
    qc = lax.broadcasted_iota(jnp.int32, shape, 0)
    qc2 = lax.broadcasted_iota(jnp.int32, (GRID_W, 128), 0)
    kc2 = lax.broadcasted_iota(jnp.int32, (GRID_W, 128), 1) & (GRID_W - 1)
    cs = jnp.clip(qc2 - 8, 0, GRID_W - 16)
    valid = (kc2 >= cs) & (kc2 < cs + 16)
    for hh in range(2):
        t = jnp.broadcast_to(rb_ref[hh], shape)
        t = pltpu.roll(t, RB_WIDTH - 15, 1)
        for b in range(6):
            t = jnp.where(((qc >> b) & 1) == 1, pltpu.roll(t, 1 << b, 1), t)
        t_odd = pltpu.roll(t, RB_WIDTH - GRID_W, 1)
        for ro in range(N_PAIRS):
            src = t if ro % 2 == 0 else t_odd
            base = (ro // 2) * 128
            pair_scr[hh, ro] = jnp.where(valid, src[:, base:base + 128], NEG_INF)


NA_GROUP_FWD = 32
NA_GROUP_BWD = 4


def _stack_heads(ref, r, scale=1.0):
    lane = lax.broadcasted_iota(jnp.int32, (GRID_W, 128), 1)
    t = ref[pl.ds(pl.multiple_of(r * GRID_W, GRID_W), GRID_W), :].astype(F32) * scale
    return jnp.concatenate([jnp.where(lane < 64, t, 0.0), jnp.where(lane >= 64, t, 0.0)], axis=0).astype(BF16)


def _unstack_heads(t2):
    lane = lax.broadcasted_iota(jnp.int32, (GRID_W, 128), 1)
    return jnp.where(lane < 64, t2[:GRID_W], t2[GRID_W:])


def _na_window(k_ref, v_ref, r, n_rows):
    rs = jnp.clip(r - NA_WIN_ROWS // 2, 0, n_rows - NA_WIN_ROWS)
    ro0 = (NA_WIN_ROWS - 1) - (r - rs)
    off = pl.multiple_of(rs * GRID_W, GRID_W)
    kw = k_ref[pl.ds(off, NA_WIN_ROWS * GRID_W), :]
    vw = v_ref[pl.ds(off, NA_WIN_ROWS * GRID_W), :]
    return kw, vw, off, ro0


def _na_probs(s_raw, pair_scr, ro0):
    bias = [jnp.concatenate([pair_scr[hh, ro0 + 2 * j] for j in range(NA_WIN_ROWS // 2)], axis=1)
            for hh in range(2)]
    s = s_raw + jnp.concatenate(bias, axis=0)
    m = jnp.max(s, axis=-1, keepdims=True)
    e = jnp.exp(s - m)
    return e * (1.0 / jnp.sum(e, axis=-1, keepdims=True))


def _na_qkv_specs(n):
    pairs = NA_WIDTH // 128
    return [pl.BlockSpec((n, 128), lambda h, first=t * pairs: (0, first + h)) for t in range(3)]


def _na_fwd(qkv, rb, *, name):
    n = qkv.shape[0]
    n_rows = n // GRID_W

    def body(ins, outs, scr):
        q_ref, k_ref, v_ref, rb_ref = ins
        o_ref, = outs
        pair_scr, = scr
        _na_bias(rb_ref, pair_scr)

        def group(g, carry):
            rows = [g * NA_GROUP_FWD + t for t in range(NA_GROUP_FWD)]
            wins = [_na_window(k_ref, v_ref, r, n_rows) for r in rows]
            raw = [lax.dot_general(_stack_heads(q_ref, r, QK_SCALE), w[0], NT_DIMS, preferred_element_type=F32)
                   for r, w in zip(rows, wins)]
            probs = [_na_probs(s, pair_scr, w[3]) for s, w in zip(raw, wins)]
            outs2 = [jnp.dot(p.astype(BF16), w[1], preferred_element_type=F32) for p, w in zip(probs, wins)]
            for r, o2 in zip(rows, outs2):
                o_ref[pl.ds(pl.multiple_of(r * GRID_W, GRID_W), GRID_W), :] = _unstack_heads(o2).astype(BF16)
            return carry

        lax.fori_loop(0, n_rows // NA_GROUP_FWD, group, 0)

    col = pl.BlockSpec((n, 128), lambda h: (0, h))
    return _call(
        body, name=name, grid=(NA_WIDTH // 128,),
        in_specs=_na_qkv_specs(n) + [pl.BlockSpec((2, 1, RB_WIDTH), lambda h: (h, 0, 0))],
        out_specs=[col], out_shape=[_sds((n, NA_WIDTH), BF16)],
        scratch_shapes=[pltpu.VMEM((2, N_PAIRS, GRID_W, 128), F32)],
        args=(qkv, qkv, qkv, rb))[0]


def _na_bwd(qkv, do, rb, *, name, after=None):
    n = qkv.shape[0]
    n_rows = n // GRID_W
    win = NA_WIN_ROWS * GRID_W

    def body(ins, outs, scr):
        q_ref, k_ref, v_ref, do_ref, rb_ref = ins
        dq_ref, dk_ref, dv_ref, drb_ref = outs
        pair_scr, acc_scr = scr
        _na_bias(rb_ref, pair_scr)
        acc_scr[...] = jnp.zeros_like(acc_scr)
        dk_ref[...] = jnp.zeros_like(dk_ref)
        dv_ref[...] = jnp.zeros_like(dv_ref)

        def group(g, carry):
            rows = [g * NA_GROUP_BWD + t for t in range(NA_GROUP_BWD)]
            wins = [_na_window(k_ref, v_ref, r, n_rows) for r in rows]
            qss = [_stack_heads(q_ref, r, QK_SCALE) for r in rows]
            doss = [_stack_heads(do_ref, r) for r in rows]
            raw = [lax.dot_general(qs, w[0], NT_DIMS, preferred_element_type=F32) for qs, w in zip(qss, wins)]
            dps = [lax.dot_general(dos, w[1], NT_DIMS, preferred_element_type=F32) for dos, w in zip(doss, wins)]
            probs = [_na_probs(s, pair_scr, w[3]) for s, w in zip(raw, wins)]
            dss = [p * (dp - jnp.sum(p * dp, axis=-1, keepdims=True)) for p, dp in zip(probs, dps)]
            dsbs = [ds.astype(BF16) for ds in dss]
            dq2s = [jnp.dot(dsb, w[0], preferred_element_type=F32) for dsb, w in zip(dsbs, wins)]
            dkws = [lax.dot_general(dsb, qs, TN_DIMS, preferred_element_type=F32) for dsb, qs in zip(dsbs, qss)]
            dvws = [lax.dot_general(p.astype(BF16), dos, TN_DIMS, preferred_element_type=F32)
                    for p, dos in zip(probs, doss)]
            for t, r in enumerate(rows):
                _, _, off, ro0 = wins[t]
                for hh in range(2):
                    for j in range(NA_WIN_ROWS // 2):
                        acc_scr[hh, ro0 + 2 * j] += dss[t][hh * GRID_W:(hh + 1) * GRID_W, j * 128:(j + 1) * 128]
                dq_ref[pl.ds(pl.multiple_of(r * GRID_W, GRID_W), GRID_W), :] = (
                    _unstack_heads(dq2s[t]) * QK_SCALE).astype(BF16)
                dk_ref[pl.ds(off, win), :] += dkws[t]
                dv_ref[pl.ds(off, win), :] += dvws[t]
            return carry

        lax.fori_loop(0, n_rows // NA_GROUP_BWD, group, 0)

        qc = lax.broadcasted_iota(jnp.int32, (N_PAIRS * GRID_W, 128), 0)
        for hh in range(2):
            t = acc_scr[hh].reshape(N_PAIRS * GRID_W, 128)
            for b in range(6):
                t = jnp.where(((qc >> b) & 1) == 1, pltpu.roll(t, 128 - (1 << b), 1), t)
            t = pltpu.roll(t, 15, 1)
            drb_ref[hh] = jnp.sum(t.reshape(N_PAIRS, GRID_W, 128), axis=1)

    col = pl.BlockSpec((n, 128), lambda h: (0, h))
    return _call(
        body, name=name, grid=(NA_WIDTH // 128,),
        in_specs=_na_qkv_specs(n) + [col, pl.BlockSpec((2, 1, RB_WIDTH), lambda h: (h, 0, 0))],
        out_specs=[col, col, col, pl.BlockSpec((2, N_PAIRS, 128), lambda h: (h, 0, 0))],
        out_shape=[_sds((n, NA_WIDTH), BF16), _sds((n, NA_WIDTH), F32), _sds((n, NA_WIDTH), F32),
                   _sds((8, N_PAIRS, 128), F32)],
        scratch_shapes=[pltpu.VMEM((2, N_PAIRS, GRID_W, 128), F32),
                        pltpu.VMEM((2, N_PAIRS, GRID_W, 128), F32)],
        args=(qkv, qkv, qkv, do, rb), after=after)


def _rpb_table(rpb2):
    t = jnp.pad(rpb2, ((0, 0), (0, 1), (0, GRID_W - rpb2.shape[-1])))
    return t.reshape(8, 1, RB_WIDTH)


def _rpb_grad(drb, *, name):
    kdim = drb.shape[1]

    def body(x_ref, o_ref):
        kk = lax.broadcasted_iota(jnp.int32, (128, 512), 0)
        jj = lax.broadcasted_iota(jnp.int32, (128, 512), 1)
        half, co = kk >> 6, kk & 63
        acc = jnp.zeros((8, 512), F32)
        for ro in range(N_PAIRS):
            hit = ((ro + half) == (jj >> 5)) & (co == (jj & 31)) & (co < 31)
            onehot = jnp.where(hit, 1.0, 0.0).astype(F32)
            acc = acc + jnp.dot(x_ref[:, ro * 128:(ro + 1) * 128], onehot, preferred_element_type=F32,
                                precision=lax.Precision.HIGHEST)
        o_ref[...] = acc

    return pl.pallas_call(
        body, name=name, grid=(1,),
        in_specs=[_const((8, kdim))], out_specs=_const((8, 512)), out_shape=_sds((8, 512), F32),
        compiler_params=_params("arbitrary"),
    )(drb)


DIL_GROUP = 4


def _dil_blocks(length):
    qb = min(128, length)
    return qb, min(qb + 2 * DIL_RADIUS, length), min(DIL_GROUP, length // qb)


def _stack_lanes(ref, t, qb, scale=1.0):
    lane = lax.broadcasted_iota(jnp.int32, (qb, 256), 1)
    val = ref[0, t * qb:(t + 1) * qb, :].astype(F32) * scale
    return jnp.concatenate([jnp.where((lane >> 6) == h, val, 0.0) for h in range(4)], axis=0).astype(BF16)


def _dil_window(k_ref, v_ref, blk, qb, win, length):
    start = pl.multiple_of(jnp.clip(blk * qb - DIL_RADIUS, 0, length - win), DIL_RADIUS)
    return k_ref[0, pl.ds(start, win), :], v_ref[0, pl.ds(start, win), :], start


def _dil_caps_init(caps_scr, qb, win):
    @pl.when((pl.program_id(0) == 0) & (pl.program_id(1) == 0))
    def _():
        gap = ((lax.broadcasted_iota(jnp.int32, (4 * qb, win), 0) & (qb - 1))
               - lax.broadcasted_iota(jnp.int32, (4 * qb, win), 1))
        for v in range(3):
            caps_scr[v] = jnp.where(jnp.abs(gap + v * DIL_RADIUS) <= DIL_RADIUS, jnp.inf, NEG_INF)


def _dil_mask(s, blk, start, qb, caps_scr):
    return jnp.minimum(s, caps_scr[(blk * qb - start) // DIL_RADIUS])


def _pick_heads(stacked, qb):
    lane = lax.broadcasted_iota(jnp.int32, (qb, 256), 1)
    out = jnp.zeros((qb, 256), stacked.dtype)
    for h in range(4):
        out = jnp.where((lane >> 6) == h, stacked[h * qb:(h + 1) * qb], out)
    return out


def _stack_head_cols(ref, t, qb):
    return jnp.concatenate([ref[0, t * qb:(t + 1) * qb, 64 * h:64 * h + 1] for h in range(4)], axis=0)


def _dil_fwd(q, k, v, *, name, after=None):
    dil, length, _ = q.shape
    qb, win, grp = _dil_blocks(length)
    extra = [] if after is None else [after]

    def body(q_ref, k_ref, v_ref, *rest):
        o_ref, lse_ref, caps_scr = rest[-3:]
        _dil_caps_init(caps_scr, qb, win)
        blks = [pl.program_id(1) * grp + t for t in range(grp)]
        wins = [_dil_window(k_ref, v_ref, b, qb, win, length) for b in blks]
        raw = [lax.dot_general(_stack_lanes(q_ref, t, qb, QK_SCALE), w[0], NT_DIMS, preferred_element_type=F32)
               for t, w in enumerate(wins)]
        lses, outs = [], []
        for t, (s, w) in enumerate(zip(raw, wins)):
            s = _dil_mask(s, blks[t], w[2], qb, caps_scr)
            m = jnp.max(s, axis=-1, keepdims=True)
            e = jnp.exp(s - m)
            norm = jnp.sum(e, axis=-1, keepdims=True)
            lses.append(m + jnp.log(norm))
            outs.append(jnp.dot((e * (1.0 / norm)).astype(BF16), w[1], preferred_element_type=F32))
        for t in range(grp):
            o_ref[0, t * qb:(t + 1) * qb, :] = _pick_heads(outs[t], qb)
            lse_ref[0, t * qb:(t + 1) * qb, :] = _pick_heads(jnp.broadcast_to(lses[t], (4 * qb, 256)), qb)

    seq = pl.BlockSpec((1, length, 256), lambda j, i: (j, 0, 0))
    blk = pl.BlockSpec((1, grp * qb, 256), lambda j, i: (j, i, 0))
    return pl.pallas_call(
        body, name=name, grid=(dil, length // (grp * qb)),
        in_specs=[blk, seq, seq] + [pl.BlockSpec(memory_space=pl.ANY)] * len(extra), out_specs=[blk, blk],
        out_shape=[_sds((dil, length, 256), F32)] * 2,
        scratch_shapes=[pltpu.VMEM((3, 4 * qb, win), F32)],
        compiler_params=_params("arbitrary", "arbitrary"),
    )(q, k, v, *extra)


def _dil_bwd(q, k, v, do, lse, cc, *, name):
    dil, length, _ = q.shape
    qb, win, grp = _dil_blocks(length)

    def body(q_ref, k_ref, v_ref, do_ref, lse_ref, cc_ref, dq_ref, dk_ref, dv_ref, caps_scr):
        _dil_caps_init(caps_scr, qb, win)

        @pl.when(pl.program_id(1) == 0)
        def _():
            dk_ref[...] = jnp.zeros_like(dk_ref)
            dv_ref[...] = jnp.zeros_like(dv_ref)

        blks = [pl.program_id(1) * grp + t for t in range(grp)]
        wins = [_dil_window(k_ref, v_ref, b, qb, win, length) for b in blks]
        qss = [_stack_lanes(q_ref, t, qb, QK_SCALE) for t in range(grp)]
        doss = [_stack_lanes(do_ref, t, qb) for t in range(grp)]
        raw = [lax.dot_general(qs, w[0], NT_DIMS, preferred_element_type=F32) for qs, w in zip(qss, wins)]
        dps = [lax.dot_general(dos, w[1], NT_DIMS, preferred_element_type=F32) for dos, w in zip(doss, wins)]
        probs = [jnp.exp(_dil_mask(s, blks[t], wins[t][2], qb, caps_scr) - _stack_head_cols(lse_ref, t, qb))
                 for t, s in enumerate(raw)]
        dsbs = [(p * (dp + _stack_head_cols(cc_ref, t, qb))).astype(BF16)
                for t, (p, dp) in enumerate(zip(probs, dps))]
        dq4s = [jnp.dot(dsb, w[0], preferred_element_type=F32) for dsb, w in zip(dsbs, wins)]
        dkws = [lax.dot_general(dsb, qs, TN_DIMS, preferred_element_type=F32) for dsb, qs in zip(dsbs, qss)]
        dvws = [lax.dot_general(p.astype(BF16), dos, TN_DIMS, preferred_element_type=F32)
                for p, dos in zip(probs, doss)]
        for t in range(grp):
            dq_ref[0, t * qb:(t + 1) * qb, :] = _pick_heads(dq4s[t], qb) * QK_SCALE
            dk_ref[0, pl.ds(wins[t][2], win), :] += dkws[t]
            dv_ref[0, pl.ds(wins[t][2], win), :] += dvws[t]

    seq = pl.BlockSpec((1, length, 256), lambda j, i: (j, 0, 0))
    blk = pl.BlockSpec((1, grp * qb, 256), lambda j, i: (j, i, 0))
    return pl.pallas_call(
        body, name=name, grid=(dil, length // (grp * qb)),
        in_specs=[blk, seq, seq, blk, blk, blk], out_specs=[blk, seq, seq],
        out_shape=[_sds((dil, length, 256), F32)] * 3,
        scratch_shapes=[pltpu.VMEM((3, 4 * qb, win), F32)],
        compiler_params=_params("arbitrary", "arbitrary"),
    )(q, k, v, do, lse, cc)


def _merge_weights(lses):
    m = jnp.maximum(jnp.maximum(lses[0], lses[1]), lses[2])
    es = [jnp.exp(t - m) for t in lses]
    inv = 1.0 / (es[0] + es[1] + es[2])
    return [e * inv for e in es]


def _branch_mix(y_na, w_bna, outs, lses, w_bd, gates, *, tm, name):
    n = y_na.shape[0]
    chunk = min(EPILOGUE_ROWS, tm)

    def body(yna_ref, wn_ref, *rest):
        o_in, l_in = rest[0:3], rest[3:6]
        wd_ref, sn_ref, sd_ref = rest[6:9]
        y_ref, yb_ref, bn_ref, bd_ref, mix_ref, scr = rest[9:15]
        for r0 in range(0, tm, chunk):
            rows = slice(r0, r0 + chunk)
            lv = [_load_token_order(l_in[g], scr, d, chunk, r0) for g, d in enumerate(DIL_DILATIONS)]
            ws = _merge_weights(lv)
            y = jnp.zeros((chunk, 256), F32)
            for g, d in enumerate(DIL_DILATIONS):
                y = y + ws[g] * _load_token_order(o_in[g], scr, d, chunk, r0)
            yb = y.astype(BF16)
            y_ref[rows, :] = y
            yb_ref[rows, :] = yb
            bn = lax.dot_general(yna_ref[rows, :], wn_ref[...], NT_DIMS, preferred_element_type=F32).astype(BF16)
            bd = lax.dot_general(yb, wd_ref[...], NT_DIMS, preferred_element_type=F32)
            bn_ref[rows, :] = bn
            bd, mixed = _gate_mix_tile(bd, sn_ref[rows, :], bn, sd_ref[rows, :])
            bd_ref[rows, :] = bd.astype(BF16)
            mix_ref[rows, :] = mixed.astype(BF16)

    specs = [_dil_spec(d, tm) for d in DIL_DILATIONS]
    return pl.pallas_call(
        body, name=name, grid=(n // tm,),
        in_specs=[_rows(tm, NA_WIDTH), _const(w_bna.shape)] + specs + specs
                 + [_const(w_bd.shape), _rows(tm, D_MODEL, 0), _rows(tm, D_MODEL, 1)],
        out_specs=[_rows(tm, 256)] * 2 + [_rows(tm, D_MODEL)] * 3,
        out_shape=[_sds((n, 256), F32), _sds((n, 256), BF16)] + [_sds((n, D_MODEL), BF16)] * 3,
        scratch_shapes=[_dil_scratch(chunk)],
        compiler_params=_params("parallel"),
    )(y_na, w_bna, *outs, *lses, w_bd, gates, gates)


def _branch_bwd(dh, w_out, gates, bn, bd, w_bna, w_bd, y, lses, *, tm, name):
    n = dh.shape[0]
    chunk = min(EPILOGUE_ROWS, tm)

    def body(dh_ref, wo_ref, sn_ref, sd_ref, bn_ref, bd_ref, wn_ref, wd_ref, y_ref, *rest):
        l_in = rest[0:3]
        dbn_ref, dbd_ref, dgn_ref, dgd_ref, dyna_ref = rest[3:8]
        do_out, cc_out, scr = rest[8:11], rest[11:14], rest[14]
        rr = lax.broadcasted_iota(jnp.int32, (256, 256), 0) >> 6
        cc = lax.broadcasted_iota(jnp.int32, (256, 256), 1) >> 6
        ones = jnp.where(rr == cc, 1.0, 0.0).astype(F32)
        for r0 in range(0, tm, chunk):
            rows = slice(r0, r0 + chunk)
            dm = lax.dot_general(dh_ref[rows, :], wo_ref[...], NT_DIMS, preferred_element_type=F32)
            dbn, dbd, dgn, dgd = (t.astype(BF16) for t in _gate_bwd_tile(
                dm, sn_ref[rows, :], bn_ref[rows, :], sd_ref[rows, :], bd_ref[rows, :]))
            dbn_ref[rows, :] = dbn
            dbd_ref[rows, :] = dbd
            dgn_ref[rows, :] = dgn
            dgd_ref[rows, :] = dgd
            dyna_ref[rows, :] = jnp.dot(dbn, wn_ref[...], preferred_element_type=F32).astype(BF16)
            dyv = jnp.dot(dbd, wd_ref[...], preferred_element_type=F32)
            lv = [_load_token_order(l_in[g], scr, d, chunk, r0) for g, d in enumerate(DIL_DILATIONS)]
            ws = _merge_weights(lv)
            tsum = jnp.dot(dyv * y_ref[rows, :], ones, preferred_element_type=F32,
                           precision=lax.Precision.HIGHEST)
            for g, d in enumerate(DIL_DILATIONS):
                _store_dil_order(ws[g] * dyv, do_out[g], scr, d, r0)
                _store_dil_order(-ws[g] * tsum, cc_out[g], scr, d, r0)

    specs = [_dil_spec(d, tm) for d in DIL_DILATIONS]
    wide = _rows(tm, D_MODEL)
    res = pl.pallas_call(
        body, name=name, grid=(n // tm,),
        in_specs=[wide, _const(w_out.shape), _rows(tm, D_MODEL, 0), _rows(tm, D_MODEL, 1), wide, wide,
                  _const(w_bna.shape), _const(w_bd.shape), _rows(tm, 256)] + specs,
        out_specs=[wide] * 4 + [_rows(tm, NA_WIDTH)] + specs + specs,
        out_shape=[_sds((n, D_MODEL), BF16)] * 4 + [_sds((n, NA_WIDTH), BF16)]
                  + [_sds((d, n // d, 256), BF16) for d in DIL_DILATIONS]
                  + [_sds((d, n // d, 256), F32) for d in DIL_DILATIONS],
        scratch_shapes=[_dil_scratch(chunk)],
        compiler_params=_params("parallel"),
    )(dh, w_out, gates, gates, bn, bd, w_bna, w_bd, y, *lses)
    return res[0], res[1], res[2], res[3], res[4], res[5:8], res[8:11]


_WEIGHTS = (("w_in", 1, 736), ("w_branch_na", 1, 128), ("w_branch_dil", 1, 128), ("w_out", 0, 128),
            ("w_up", 1, 512), ("w_down", 0, 512), ("w_ple_gate", 0, 128), ("w_ple_proj", 1, 128))
_W_IN, _W_BNA, _W_BD, _W_OUT, _W_UP, _W_DOWN, _W_PG, _W_PP = range(8)


def _to_full(gathered):
    return gathered.reshape(-1, gathered.shape[2])


def _to_chunks(widx, mat):
    return mat.reshape(N_DEV, _WEIGHTS[widx][2], mat.shape[1])


def _local_step(x, p_bf16, positions, target, g_mix, g_mlp, g_ple, g_final, rpb2,
                get_w_in, relay_rest, get_rest, send_grads):
    tm = 512
    half = HEAD_DIM // 2
    inv_freq = 10000.0 ** (-jnp.arange(half, dtype=F32) / half)
    ang = positions.astype(F32)[:, None] * inv_freq
    cos, sin = jnp.cos(ang), jnp.sin(ang)
    cos_t = jnp.tile(jnp.concatenate([cos, cos], axis=-1), (1, 4))
    sin_t = jnp.tile(jnp.concatenate([-sin, sin], axis=-1), (1, 4))
    rb = _rpb_table(rpb2)

    a = _rms_fwd(x, g_mix, tm=tm, name="rms_mix")
    w_in, token = get_w_in((a, cos_t, sin_t, p_bf16))
    na_qkv, gates, dq_g, dk_g, dv_g = _project_in(a, w_in, cos_t, sin_t, tm=512, name="mm_in", after=token)
    y_na = _na_fwd(na_qkv, rb, name="na_fwd")
    token = relay_rest(y_na)
    d_out, d_lse = [], []
    for g in range(3):
        o, lse = _dil_fwd(dq_g[g], dk_g[g], dv_g[g], name=f"dil_fwd{g}", after=token if g == 0 else None)
        d_out.append(o)
        d_lse.append(lse)
    w_bna, w_bd = get_rest(d_out[2], 0)
    y_dil, y_dil_b, bn, bd, mixed = _branch_mix(y_na, w_bna, d_out, d_lse, w_bd, gates, tm=tm, name="branch_mix")
    w_out, w_up, w_down, w_pg, w_pp = get_rest(mixed, 1)
    h1, c = _matmul(mixed, w_out, out_dtype=(F32, BF16), tm=512, tn=1024, tk=1024, name="mm_out",
                    extra=(x, g_mlp), epilogue=_residual_rms_tile)
    u, f = _matmul(c, w_up, tb=True, out_dtype=(BF16, BF16), tm=512, tn=2048, tk=1024, name="mm_up",
                   epilogue=lambda acc: (acc, jnp.square(jnp.maximum(acc, 0.0))))

    e, dpp, dgt, dh2, dh2_b, dg_final, loss, dg_ple = _tail_step(
        f, w_down, h1, w_pg, p_bf16, w_pp, target, g_final, g_ple, tm=256, name="tail_step")
    loss = loss[:, :128]
    gw_pp = _matmul(p_bf16, dpp, ta=True, transpose_out=True, out_dtype=BF16, tm=256, tn=1024, tk=2048,
                    name="mm_gw_pp")
    gw_pg = _matmul(e, dgt, ta=True, out_dtype=BF16, tm=512, tn=1024, tk=2048, name="mm_gw_pg")
    du = _matmul(dh2_b, w_down, tb=True, out_dtype=BF16, tm=512, tn=2048, tk=1024, name="mm_du",
                 extra=(u,), epilogue=lambda acc, uv: (acc * (2.0 * jnp.maximum(uv.astype(F32), 0.0)),))
    gw_down = _matmul(f, dh2_b, ta=True, out_dtype=BF16, tm=1024, tn=1024, tk=2048, name="mm_gw_down")
    gw_up = _matmul(c, du, ta=True, transpose_out=True, out_dtype=BF16, tm=512, tn=2048, tk=2048, name="mm_gw_up")
    dh1, dh1_b, dg_mlp = _matmul(
        du, w_up, out_dtype=(F32, BF16), tm=512, tn=1024, tk=4096, name="mm_dc",
        extra=(h1, g_mlp, dh2), epilogue=_rms_bwd_twice, n_colsum=1)
    dbn, dbd, dgn, dgd, dy_na, do_g, cc_g = _branch_bwd(dh1_b, w_out, gates, bn, bd, w_bna, w_bd, y_dil, d_lse,
                                                        tm=tm, name="branch_bwd")
    gw_out = _matmul(mixed, dh1_b, ta=True, out_dtype=BF16, tm=512, tn=1024, tk=2048, name="mm_gw_out")
    gw_bna = _matmul(y_na, dbn, ta=True, transpose_out=True, out_dtype=BF16, tm=512, tn=1024, tk=2048,
                     name="mm_gw_bna")
    gw_bd = _matmul(y_dil_b, dbd, ta=True, transpose_out=True, out_dtype=BF16, tm=256, tn=1024, tk=2048,
                    name="mm_gw_bd")
    token = send_grads((_W_PP, _W_PG, _W_DOWN, _W_UP, _W_OUT, _W_BNA, _W_BD),
                       (gw_pp, gw_pg, gw_down, gw_up, gw_out, gw_bna, gw_bd))
    dna = _na_bwd(na_qkv, dy_na, rb, name="na_bwd", after=token)
    drpb = _rpb_grad(dna[3].reshape(8, -1), name="rpb_grad")
    ddq, ddk, ddv = [], [], []
    for g in range(3):
        r = _dil_bwd(dq_g[g], dk_g[g], dv_g[g], do_g[g], d_lse[g], cc_g[g], name=f"dil_bwd{g}")
        ddq.append(r[0])
        ddk.append(r[1])
        ddv.append(r[2])
    dproj = _assemble_dproj(dna[0:3], ddq, ddk, ddv, dgn, dgd, cos_t, sin_t, tm=tm, name="assemble_dproj")
    gw_in = _matmul(a, dproj, ta=True, transpose_out=True, out_dtype=BF16, tm=512, tn=2944, tk=2048, name="mm_gw_in")
    token = send_grads((_W_IN,), (gw_in,))
    dx, dg_mix = _matmul(
        dproj, w_in, out_dtype=(F32,), tm=512, tn=1024, tk=5888, name="mm_da", after=token,
        extra=(x, g_mix, dh1), epilogue=_rms_bwd_tile, n_colsum=1)
    return loss, dx, (dg_mix, dg_mlp, dg_ple, dg_final), drpb


def _cast_bf16(t, *, name):
    def body(t_ref, o_ref):
        o_ref[...] = t_ref[...].astype(BF16)

    rows, cols = t.shape
    tr = 256 if rows % 256 == 0 else rows
    blk = pl.BlockSpec((tr, cols), lambda i: (i, 0))
    return pl.pallas_call(body, name=name, grid=(rows // tr,), in_specs=[blk], out_specs=blk,
                          out_shape=_sds(t.shape, BF16), compiler_params=_params("parallel"))(t)


def _adamw(w, g, m, v):
    m = ADAM_B1 * m + (1.0 - ADAM_B1) * g
    v = ADAM_B2 * v + (1.0 - ADAM_B2) * (g * g)
    m_hat = m / (1.0 - ADAM_B1 ** ADAM_STEP)
    v_hat = v / (1.0 - ADAM_B2 ** ADAM_STEP)
    delta = -ADAM_LR * (m_hat / (jnp.sqrt(v_hat) + ADAM_EPS) + ADAM_WD * w)
    return delta, m, v


def _sum_adamw(parts, w, m, v, *, tr, name, own=None, transposed=False):
    rows, cols = w.shape
    n_pre = 0 if own is None else 1

    def body(*refs):
        p_ref, w_ref, m_ref, v_ref = refs[n_pre:n_pre + 4]
        g_ref, d_ref, nm_ref, nv_ref = refs[-4:]
        g = (p_ref[0] if own is None else refs[n_pre + 4][...]).astype(F32)
        for s in range(1, N_DEV):
            g = g + p_ref[s].astype(F32)
        if transposed:
            g = g.T
        g_ref[...] = g
        d_ref[...], nm_ref[...], nv_ref[...] = _adamw(w_ref[...], g, m_ref[...], v_ref[...])

    if transposed:
        blk = pl.BlockSpec((rows, tr), lambda i, *_: (0, i))
        g_rows, steps = rows, cols // tr
    else:
        blk = pl.BlockSpec((tr, cols), lambda i, *_: (i, 0))
        g_rows, steps = cols, rows // tr
    in_specs = [pl.BlockSpec((N_DEV, tr, g_rows), lambda i, *_: (0, i, 0)), blk, blk, blk]
    args = [parts, w, m, v]
    if own is not None:
        in_specs.append(pl.BlockSpec((None, tr, g_rows), lambda i, idx: (idx[0], i, 0)))
        args = [own[1]] + args + [own[0]]
    return pl.pallas_call(
        body, name=name,
        grid_spec=pltpu.PrefetchScalarGridSpec(num_scalar_prefetch=n_pre, grid=(steps,), in_specs=in_specs,
                                               out_specs=[blk] * 4),
        out_shape=[_sds((rows, cols), F32)] * 4,
        compiler_params=_params("parallel"),
    )(*args)


_RPB_SIZE = 8 * 15 * 31


def _pack_small(g_mix, g_mlp, g_ple, g_final, rpb, loss_row):
    flat = jnp.concatenate([g_mix.reshape(-1), g_mlp.reshape(-1), g_ple.reshape(-1), g_final.reshape(-1),
                            rpb.reshape(-1), jnp.zeros((3840 - _RPB_SIZE,), F32), loss_row.reshape(-1),
                            jnp.zeros((128,), F32)])
    return flat.reshape(64, 128)


def _unpack_small(t):
    flat = t.reshape(-1)
    return (flat[0:1024].reshape(1, 1024), flat[4096:4096 + _RPB_SIZE].reshape(1, 8, 15, 31),
            flat[1024:2048].reshape(1, 1024), flat[2048:3072].reshape(1, 1024), flat[3072:4096])


def kernel(x, p, positions, g_mix, w_in, rpb, w_branch_na, w_branch_dil, w_out, g_mlp, w_up, w_down, g_ple, w_ple_gate, w_ple_proj, g_final, loss_target, m_g_mix, m_w_in, m_rpb, m_w_branch_na, m_w_branch_dil, m_w_out, m_g_mlp, m_w_up, m_w_down, m_g_ple, m_w_ple_gate, m_w_ple_proj, m_g_final, v_g_mix, v_w_in, v_rpb, v_w_branch_na, v_w_branch_dil, v_w_out, v_g_mlp, v_w_up, v_w_down, v_g_ple, v_w_ple_gate, v_w_ple_proj, v_g_final):
    sharded = dict(w_in=(w_in, m_w_in, v_w_in), w_branch_na=(w_branch_na, m_w_branch_na, v_w_branch_na),
                   w_branch_dil=(w_branch_dil, m_w_branch_dil, v_w_branch_dil), w_out=(w_out, m_w_out, v_w_out),
                   w_up=(w_up, m_w_up, v_w_up), w_down=(w_down, m_w_down, v_w_down),
                   w_ple_gate=(w_ple_gate, m_w_ple_gate, v_w_ple_gate),
                   w_ple_proj=(w_ple_proj, m_w_ple_proj, v_w_ple_proj))
    shards = {k: tuple(t[0] for t in val) for k, val in sharded.items()}

    me = _my_index()

    shards["w_in"] = tuple(t.T for t in shards["w_in"])

    w_in_b = _cast_bf16(shards["w_in"][0], name="cast_w_in")
    rest_b = [shards[name][0].astype(BF16).T if axis == 1 else shards[name][0].astype(BF16)
              for name, axis, _ in _WEIGHTS[1:]]
    first_in, token_in = _start_copies(_first_leg_copies, [w_in_b], [_sds((N_DEV,) + w_in_b.shape, BF16)], 4,
                                       name="start_gather_w_in")

    def whole(landed, mine):
        return _to_full(lax.dynamic_update_index_in_dim(landed, mine, me, 0))

    rest = {}

    def get_w_in(after):
        (mine,), landed = _wait_copies(_first_leg_copies, first_in, (*after, *rest_b), name="wait_gather_w_in")
        second, token = _start_copies(_second_leg_copies, [], landed, 3, name="start_forward_w_in")
        _, (landed,) = _wait_copies(_second_leg_copies, second, token, name="wait_forward_w_in")
        rest["first"], token = _start_copies(_first_leg_copies, rest_b,
                                             [_sds((N_DEV,) + t.shape, BF16) for t in rest_b], 4 * len(rest_b),
                                             name="start_gather_rest", after=landed)
        return whole(landed, mine), token

    def relay_rest(after):
        rest["mine"], landed = _wait_copies(_first_leg_copies, rest["first"], after, name="wait_gather_rest")
        rest["second"], token = _start_copies(_second_leg_copies, [], landed, 3 * len(rest_b),
                                              name="start_forward_rest")
        return token

    def get_rest(after, stage):
        n_src, send_sems, recv_sems, bufs = rest["second"]
        part = slice(0, 2) if stage == 0 else slice(2, len(rest_b))
        _, landed = _wait_copies(functools.partial(_second_leg_copies, first=part.start),
                                 (n_src, send_sems, recv_sems, bufs[part]), after,
                                 name=f"wait_forward_rest{stage}")
        return [whole(t, own) for t, own in zip(landed, rest["mine"][part])]

    sent = []

    def send_grads(indices, grads):
        chunked = [_to_chunks(i, g) for i, g in zip(indices, grads)]
        handle, token = _start_copies(_exchange_copies, chunked, [_sds(t.shape, BF16) for t in chunked],
                                      7 * len(chunked),
                                      name="start_exchange_" + ("w_in" if indices == (_W_IN,) else "rest"))
        sent.append((indices, handle))
        return token

    g_mix_0 = g_mix + token_in[0:1, 0:1]
    loss, dx, dgs, drpb = _local_step(
        x[0], p[0, 0].astype(BF16), positions[0], loss_target[0],
        g_mix_0, g_mlp, g_ple, g_final.reshape(1, -1), rpb[0], get_w_in, relay_rest, get_rest, send_grads)

    drpb3 = drpb.reshape(8, 16, 32)[:, :15, :31]
    small = _pack_small(dgs[0], dgs[1], dgs[2], dgs[3], drpb3, loss)
    share, done = _start_copies(_gather_copies, [small], [_sds((N_DEV,) + small.shape, F32)], 7,
                                name="start_share_small")

    out = {}
    for indices, handle in sent:
        chunked, landed = _wait_copies(_exchange_copies, handle, done,
                                       name="wait_exchange_" + ("w_in" if indices == (_W_IN,) else "rest"))
        for i, part, mine in zip(indices, landed, chunked):
            name = _WEIGHTS[i][0]
            w, m, v = shards[name]
            turned = _WEIGHTS[i][1] == 1 and i != _W_IN
            res = _sum_adamw(part, w, m, v, tr=368 if i == _W_IN else 128, name="adamw_" + name,
                             own=(mine, me.reshape(1).astype(jnp.int32)), transposed=turned)
            out[name] = [(t.T if i == _W_IN else t)[None] for t in res]
            done = res[0]
    (small,), (small_landed,) = _wait_copies(_gather_copies, share, done, name="wait_share_small")
    small_all = lax.dynamic_update_index_in_dim(small_landed, small, me, 0)
    small_w = _pack_small(g_mix, g_mlp, g_ple, g_final, rpb, jnp.zeros((128,), F32))
    small_m = _pack_small(m_g_mix, m_g_mlp, m_g_ple, m_g_final, m_rpb, jnp.zeros((128,), F32))
    small_v = _pack_small(v_g_mix, v_g_mlp, v_g_ple, v_g_final, v_rpb, jnp.zeros((128,), F32))
    res = _sum_adamw(small_all, small_w, small_m, small_v, tr=64, name="adamw_small")
    unpacked = [_unpack_small(t) for t in res]
    for i, name in enumerate(("g_mix", "rpb", "g_mlp", "g_ple", "g_final")):
        out[name] = [u[i] for u in unpacked]
    loss_total = res[0][62, 0]

    order = ("g_mix", "w_in", "rpb", "w_branch_na", "w_branch_dil", "w_out", "g_mlp", "w_up", "w_down",
             "g_ple", "w_ple_gate", "w_ple_proj", "g_final")
    grads = [out[k][0] for k in order]
    deltas = [out[k][1] for k in order]
    new_m = [out[k][2] for k in order]
    new_v = [out[k][3] for k in order]
    return (loss_total, dx[None], *grads, *deltas, *new_m, *new_v)
```

```python
import functools

import jax
import jax.numpy as jnp
from jax import lax
from jax.experimental import pallas as pl
from jax.experimental.pallas import tpu as pltpu

F32 = jnp.float32
BF16 = jnp.bfloat16

D_MODEL = 1024
HEAD_DIM = 64
GRID_W = 64
NA_WIDTH = 512
DIL_WIDTH = 768
IN_WIDTH = 5888
DIL_DILATIONS = (1, 4, 16)
DIL_RADIUS = 64
NA_WIN_ROWS = 8
RMS_EPS = 1e-6
NEG_INF = -1e30
QK_SCALE = HEAD_DIM ** -0.5

ADAM_LR = 0.001
ADAM_B1 = 0.9
ADAM_B2 = 0.999
ADAM_EPS = 1e-08
ADAM_WD = 0.01
ADAM_STEP = 10

N_DEV = 8
VMEM_LIMIT = 56 * 1024 * 1024
EPILOGUE_ROWS = 256
MESH = pl.DeviceIdType.MESH

NT_DIMS = (((1,), (1,)), ((), ()))
TN_DIMS = (((0,), (0,)), ((), ()))


def _sds(shape, dtype):
    return jax.ShapeDtypeStruct(shape, dtype)


def _params(*sem):
    return pltpu.CompilerParams(dimension_semantics=sem, vmem_limit_bytes=VMEM_LIMIT)


def _rows(tm, width, col=0):
    return pl.BlockSpec((tm, width), lambda i, c=col: (i, c))


def _const(shape):
    zeros = (0,) * len(shape)
    return pl.BlockSpec(shape, lambda i: zeros)


def _my_index():
    return 4 * lax.axis_index("x") + 2 * lax.axis_index("y") + lax.axis_index("c")


def _peer(k):
    x, y, c = lax.axis_index("x"), lax.axis_index("y"), lax.axis_index("c")
    px = 1 - x if k & 4 else x
    py = 1 - y if k & 2 else y
    pc = 1 - c if k & 1 else c
    return (px, py, pc), 4 * px + 2 * py + pc


def _call(body, *, name, grid, in_specs, out_specs, out_shape, scratch_shapes, args, after=None):
    n_in, n_out = len(in_specs), len(out_specs)
    extra = [] if after is None else [after]
    n_x = n_in + len(extra)

    def plain(*refs):
        body(refs[:n_in], refs[n_x:n_x + n_out], refs[n_x + n_out:])

    res = pl.pallas_call(plain, name=name, grid=grid,
                         in_specs=list(in_specs) + [pl.BlockSpec(memory_space=pl.ANY)] * len(extra),
                         out_specs=out_specs, out_shape=out_shape, scratch_shapes=scratch_shapes,
                         compiler_params=_params(*(("arbitrary",) * len(grid))))(*args, *extra)
    return list(res)


_HBM_SPEC = pl.BlockSpec(memory_space=pltpu.HBM)
_SEM_SPEC = pl.BlockSpec(memory_space=pltpu.SEMAPHORE)
_SIDE_EFFECT = pltpu.SideEffectType.DATAFLOW_SIDE_EFFECTING


_FIRST_LEG = (1, 2, 4, 6)
_SECOND_LEG = (2, 4, 6)


def _gather_copies(srcs, lands, send, recv, sending):
    me = _my_index()
    out = []
    for w in range(len(srcs)):
        for k in range(1, N_DEV):
            dev, idx = _peer(k)
            out.append(pltpu.make_async_remote_copy(
                src_ref=srcs[w], dst_ref=lands[w].at[me if sending else idx],
                send_sem=send.at[w * 7 + k - 1], recv_sem=recv.at[w * 7 + k - 1],
                device_id=dev, device_id_type=MESH))
    return out


def _first_leg_copies(srcs, lands, send, recv, sending):
    me = _my_index()
    out = []
    for w in range(len(srcs)):
        for j, k in enumerate(_FIRST_LEG):
            dev, idx = _peer(k)
            out.append(pltpu.make_async_remote_copy(
                src_ref=srcs[w], dst_ref=lands[w].at[me if sending else idx],
                send_sem=send.at[w * 4 + j], recv_sem=recv.at[w * 4 + j],
                device_id=dev, device_id_type=MESH))
    return out


def _second_leg_copies(srcs, lands, send, recv, sending, first=0):
    sibling, _ = _peer(1)
    out = []
    for w in range(len(lands)):
        for j, k in enumerate(_SECOND_LEG):
            slot = _peer(k if sending else k ^ 1)[1]
            sem = (first + w) * 3 + j
            out.append(pltpu.make_async_remote_copy(
                src_ref=lands[w].at[slot], dst_ref=lands[w].at[slot],
                send_sem=send.at[sem], recv_sem=recv.at[sem],
                device_id=sibling, device_id_type=MESH))
    return out


def _exchange_copies(srcs, lands, send, recv, sending):
    out = []
    for w in range(len(srcs)):
        for k in range(1, N_DEV):
            dev, idx = _peer(k)
            out.append(pltpu.make_async_remote_copy(
                src_ref=srcs[w].at[idx], dst_ref=lands[w].at[k],
                send_sem=send.at[w * 7 + k - 1], recv_sem=recv.at[w * 7 + k - 1],
                device_id=dev, device_id_type=MESH))
    return out


def _start_copies(make, srcs, lands, n_copies, *, name, after=None):
    n_src, n_buf = len(srcs), len(srcs) + len(lands)
    extra = [] if after is None else [after]

    def body(*refs):
        send, recv = refs[n_buf + len(extra)], refs[n_buf + len(extra) + 1]
        for cp in make(refs[:n_src], refs[n_src:n_buf], send, recv, True):
            cp.start()
        refs[-1][...] = jnp.zeros_like(refs[-1])

    bufs = list(srcs) + [lax.empty(t.shape, t.dtype) if isinstance(t, jax.ShapeDtypeStruct) else t for t in lands]
    res = pl.pallas_call(
        body, name=name,
        out_shape=(pltpu.SemaphoreType.DMA((n_copies,)), pltpu.SemaphoreType.DMA((n_copies,)),
                   *[pltpu.HBM(t.shape, t.dtype) for t in bufs], _sds((8, 128), F32)),
        in_specs=[_HBM_SPEC] * n_buf + [pl.BlockSpec(memory_space=pl.ANY)] * len(extra),
        out_specs=(_SEM_SPEC, _SEM_SPEC, *([_HBM_SPEC] * n_buf), pl.BlockSpec(memory_space=pltpu.VMEM)),
        input_output_aliases={i: 2 + i for i in range(n_buf)},
        compiler_params=pltpu.CompilerParams(has_side_effects=_SIDE_EFFECT),
    )(*[pltpu.with_memory_space_constraint(t, pltpu.HBM) for t in bufs], *extra)
    return (n_src, res[0], res[1], res[2:2 + n_buf]), res[-1]


def _wait_copies(make, handle, after, *, name):
    n_src, send_sems, recv_sems, bufs = handle
    n_buf = len(bufs)
    after = list(after) if isinstance(after, (tuple, list)) else [after]

    def body(*refs):
        for cp in make(refs[:n_src], refs[n_src:n_buf], refs[n_buf], refs[n_buf + 1], False):
            cp.wait_send()
            cp.wait_recv()

    res = pl.pallas_call(
        body, name=name,
        out_shape=tuple(pltpu.HBM(t.shape, t.dtype) for t in bufs),
        in_specs=[_HBM_SPEC] * n_buf + [_SEM_SPEC, _SEM_SPEC] + [pl.BlockSpec(memory_space=pl.ANY)] * len(after),
        out_specs=tuple([_HBM_SPEC] * n_buf),
        input_output_aliases={i: i for i in range(n_buf)},
        compiler_params=pltpu.CompilerParams(has_side_effects=_SIDE_EFFECT),
    )(*bufs, send_sems, recv_sems, *after)
    return list(res[:n_src]), list(res[n_src:])


def _add_colsums(s_refs, sums, step):
    for s_ref, val in zip(s_refs, sums):
        @pl.when(step == 0)
        def _(s_ref=s_ref, val=val):
            s_ref[...] = val

        @pl.when(step > 0)
        def _(s_ref=s_ref, val=val):
            s_ref[...] += val


def _matmul(a, b, *, ta=False, tb=False, out_dtype, tm, tn, tk, name, after=None, extra=(), epilogue=None,
            n_colsum=0, transpose_out=False):
    m, k = (a.shape[1], a.shape[0]) if ta else a.shape
    n = b.shape[0] if tb else b.shape[1]
    tm, tn, tk = min(tm, m), min(tn, n), min(tk, k)
    nk = k // tk
    dims = (((0 if ta else 1,), (1 if tb else 0,)), ((), ()))
    out_dtypes = out_dtype if isinstance(out_dtype, tuple) else (out_dtype,)
    n_tiles = len(out_dtypes)

    def add_colsums(o_refs, sums):
        _add_colsums(o_refs[n_tiles:], sums, pl.program_id(1))

    def finish(acc, x_refs, o_refs):
        vals = (acc,) if epilogue is None else epilogue(acc, *[r[...] for r in x_refs])
        for o_ref, val in zip(o_refs[:n_tiles], vals[:n_tiles]):
            o_ref[...] = (val.T if transpose_out else val).astype(o_ref.dtype)
        add_colsums(o_refs, vals[n_tiles:])

    chunk = EPILOGUE_ROWS if (nk == 1 and epilogue is not None and not ta and tm % EPILOGUE_ROWS == 0) else None

    def body(ins, outs, acc):
        a_ref, b_ref = ins[:2]
        if chunk is not None:
            sums = None
            for r0 in range(0, tm, chunk):
                part = lax.dot_general(a_ref[r0:r0 + chunk, :], b_ref[...], dims, preferred_element_type=F32)
                vals = epilogue(part, *[r[...] if r.shape[0] == 1 else r[r0:r0 + chunk, :] for r in ins[2:]])
                for o_ref, val in zip(outs[:n_tiles], vals[:n_tiles]):
                    o_ref[r0:r0 + chunk, :] = val.astype(o_ref.dtype)
                sums = vals[n_tiles:] if sums is None else [s + v for s, v in zip(sums, vals[n_tiles:])]
            add_colsums(outs, sums)
            return
        part = lax.dot_general(a_ref[...], b_ref[...], dims, preferred_element_type=F32)
        if nk == 1:
            finish(part, ins[2:], outs)
            return
        acc_ref, = acc
        kk = pl.program_id(2)

        @pl.when(kk == 0)
        def _():
            acc_ref[...] = part

        @pl.when(kk > 0)
        def _():
            acc_ref[...] += part

        @pl.when(kk == nk - 1)
        def _():
            finish(acc_ref[...], ins[2:], outs)

    a_spec = (pl.BlockSpec((tk, tm), lambda j, i, kk: (kk, i)) if ta
              else pl.BlockSpec((tm, tk), lambda j, i, kk: (i, kk)))
    b_spec = (pl.BlockSpec((tn, tk), lambda j, i, kk: (j, kk)) if tb
              else pl.BlockSpec((tk, tn), lambda j, i, kk: (kk, j)))
    tile = pl.BlockSpec((tm, tn), lambda j, i, kk: (i, j))
    row = pl.BlockSpec((1, tn), lambda j, i, kk: (0, j))

    out_tile, out_dims = (pl.BlockSpec((tn, tm), lambda j, i, kk: (j, i)), (n, m)) if transpose_out else (tile, (m, n))
    res = _call(
        body, name=name, grid=(n // tn, m // tm, nk),
        in_specs=[a_spec, b_spec] + [row if t.shape[0] == 1 else tile for t in extra],
        out_specs=[out_tile] * n_tiles + [row] * n_colsum,
        out_shape=[_sds(out_dims, dt) for dt in out_dtypes] + [_sds((1, n), F32)] * n_colsum,
        scratch_shapes=[] if nk == 1 else [pltpu.VMEM((tm, tn), F32)],
        args=(a, b, *extra), after=after)
    return res if isinstance(out_dtype, tuple) or n_colsum else res[0]


def _rstd(h):
    return lax.rsqrt(jnp.mean(h * h, axis=-1, keepdims=True) + RMS_EPS)


def _sigmoid(z):
    return 1.0 / (1.0 + jnp.exp(-z))


def _rms_fwd(x, g, *, tm, name):
    n = x.shape[0]

    def body(x_ref, g_ref, o_ref):
        h = x_ref[...]
        o_ref[...] = (h * _rstd(h) * g_ref[...]).astype(BF16)

    return pl.pallas_call(
        body, name=name, grid=(n // tm,),
        in_specs=[_rows(tm, D_MODEL), _const((1, D_MODEL))],
        out_specs=_rows(tm, D_MODEL), out_shape=_sds((n, D_MODEL), BF16),
        compiler_params=_params("parallel"),
    )(x, g)


def _swap_halves(t):
    lane = lax.broadcasted_iota(jnp.int32, (t.shape[0], 128), 1)
    pieces = [t[:, c:c + 128] for c in range(0, t.shape[1], 128)]
    return jnp.concatenate([jnp.where((lane & 63) < 32, pltpu.roll(h, 96, 1), pltpu.roll(h, 32, 1))
                            for h in pieces], axis=1)


def _dil_spec(dil, tm):
    return pl.BlockSpec((dil, tm // dil, 256), lambda i: (0, i, 0))


def _dil_scratch(tm):
    return pltpu.VMEM((2, tm, 128), F32)


def _load_token_order(src, scr, dil, rows, row0=0):
    if dil == 1:
        return src[0, row0:row0 + rows, :]
    for j in range(dil):
        for c in range(2):
            scr[c, pl.ds(j, rows // dil, stride=dil), :] = (
                src[j, row0 // dil:(row0 + rows) // dil, c * 128:(c + 1) * 128])
    return jnp.concatenate([scr[0, 0:rows, :], scr[1, 0:rows, :]], axis=1)


def _store_dil_order(val, dst, scr, dil, row0=0):
    rows = val.shape[0]
    if dil == 1:
        dst[0, row0:row0 + rows, :] = val.astype(dst.dtype)
        return
    for c in range(2):
        scr[c] = val[:, c * 128:(c + 1) * 128]
    for j in range(dil):
        for c in range(2):
            dst[j, row0 // dil:(row0 + rows) // dil, c * 128:(c + 1) * 128] = (
                scr[c, pl.ds(j, rows // dil, stride=dil), :].astype(dst.dtype))


def _project_in(a, w_t, cos_t, sin_t, *, tm, name, after=None):
    n = a.shape[0]
    n_dil = len(DIL_DILATIONS)
    na_w, dil_w = 3 * NA_WIDTH, 3 * DIL_WIDTH
    chunk = min(EPILOGUE_ROWS, tm)
    extra = [] if after is None else [after]

    def body(a_ref, w_ref, cos_ref, sin_ref, *rest):
        na_ref, gate_ref = rest[len(extra):len(extra) + 2]
        outs, scr = rest[len(extra) + 2:len(extra) + 2 + 3 * n_dil], rest[-1]

        def part(r0, first, width):
            return lax.dot_general(a_ref[r0:r0 + chunk, :], w_ref[first:first + width, :], NT_DIMS,
                                   preferred_element_type=F32)

        for r0 in range(0, tm, chunk):
            na_ref[r0:r0 + chunk, :] = part(r0, 0, na_w).astype(BF16)
            dil_part = part(r0, na_w, dil_w)
            cosv, sinv = cos_ref[r0:r0 + chunk, :], sin_ref[r0:r0 + chunk, :]
            for t in range(3):
                for gi, dil in enumerate(DIL_DILATIONS):
                    c0 = (t * n_dil + gi) * 256
                    val = dil_part[:, c0:c0 + 256]
                    if t < 2:
                        val = val * cosv + _swap_halves(val) * sinv
                    _store_dil_order(val, outs[t * n_dil + gi], scr, dil, r0)
            gate_ref[r0:r0 + chunk, :] = _sigmoid(part(r0, na_w + dil_w, 2 * D_MODEL)).astype(BF16)

    out_specs = [_rows(tm, na_w), _rows(tm, 2 * D_MODEL)]
    out_shape = [_sds((n, na_w), BF16), _sds((n, 2 * D_MODEL), BF16)]
    for _ in range(3):
        for dil in DIL_DILATIONS:
            out_specs.append(pl.BlockSpec((dil, tm // dil, 256), lambda i: (0, i, 0)))
            out_shape.append(_sds((dil, n // dil, 256), BF16))
    res = pl.pallas_call(
        body, name=name, grid=(n // tm,),
        in_specs=[_rows(tm, D_MODEL), _const(w_t.shape), _rows(tm, 256), _rows(tm, 256)]
                 + [pl.BlockSpec(memory_space=pl.ANY)] * len(extra),
        out_specs=out_specs, out_shape=out_shape,
        scratch_shapes=[pltpu.VMEM((2, chunk, 128), F32)],
        compiler_params=_params("parallel"),
    )(a, w_t, cos_t, sin_t, *extra)
    return res[0], res[1], res[2:5], res[5:8], res[8:11]


def _residual_rms_tile(delta, h, g):
    hn = h + delta
    return hn, hn * _rstd(hn) * g


def _gate_mix_tile(b2, s1, b1, s2):
    return b2, s1.astype(F32) * b1.astype(F32) + s2.astype(F32) * b2


def _gate_bwd_tile(dm, s1, b1, s2, b2):
    s1, b1, s2, b2 = (t.astype(F32) for t in (s1, b1, s2, b2))
    return dm * s1, dm * s2, dm * b1 * s1 * (1.0 - s1), dm * b2 * s2 * (1.0 - s2)


def _tail_tile(gt, pp, h2, target, g):
    sg = _sigmoid(gt)
    h3 = h2 + sg * pp
    r3 = _rstd(h3)
    n3 = h3 * r3
    err = n3 * g - target
    loss = 0.5 * jnp.sum(jnp.sum(err * err, axis=-1, keepdims=True) / D_MODEL)
    dy = err / D_MODEL
    dn = dy * g
    dh3 = r3 * (dn - n3 * jnp.mean(dn * n3, axis=-1, keepdims=True))
    return (dh3, dh3 * sg, dh3 * pp * sg * (1.0 - sg),
            jnp.sum(dy * n3, axis=0, keepdims=True), jnp.full((1, gt.shape[1]), loss, F32))


def _rms_bwd_tile(dz, h, g, dres):
    r = _rstd(h)
    nrm = h * r
    dn = dz * g
    dh = dres + r * (dn - nrm * jnp.mean(dn * nrm, axis=-1, keepdims=True))
    return dh, jnp.sum(dz * nrm, axis=0, keepdims=True)


def _rms_bwd_twice(dz, h, g, dres):
    dh, dg = _rms_bwd_tile(dz, h, g, dres)
    return dh, dh, dg


def _tail_step(f, w_down, h1, w_pg, p, w_pp, target, g_final, g_ple, *, tm, name):
    n = f.shape[0]
    chunk = min(EPILOGUE_ROWS, tm)

    def body(f_ref, wd_ref, h1_ref, wg_ref, p_ref, wp_ref, t_ref, gf_ref, gp_ref,
             e_ref, dpp_ref, dgt_ref, dh2_ref, dh2b_ref, dgf_ref, loss_ref, dgp_ref):
        sums = None
        for r0 in range(0, tm, chunk):
            rows = slice(r0, r0 + chunk)
            delta = jnp.dot(f_ref[rows, :], wd_ref[...], preferred_element_type=F32)
            h2, e = _residual_rms_tile(delta, h1_ref[rows, :], gp_ref[...])
            e = e.astype(BF16)
            e_ref[rows, :] = e
            gt = jnp.dot(e, wg_ref[...], preferred_element_type=F32)
            pp = lax.dot_general(p_ref[rows, :], wp_ref[...], NT_DIMS, preferred_element_type=F32)
            dh3, dpp, dgt, dgf, loss = _tail_tile(gt, pp, h2, t_ref[rows, :], gf_ref[...])
            dgt = dgt.astype(BF16)
            dpp_ref[rows, :] = dpp.astype(BF16)
            dgt_ref[rows, :] = dgt
            dz = lax.dot_general(dgt, wg_ref[...], NT_DIMS, preferred_element_type=F32)
            dh2, dgp = _rms_bwd_tile(dz, h2, gp_ref[...], dh3)
            dh2_ref[rows, :] = dh2
            dh2b_ref[rows, :] = dh2.astype(BF16)
            vals = (dgf, loss, dgp)
            sums = vals if sums is None else [s + v for s, v in zip(sums, vals)]
        _add_colsums((dgf_ref, loss_ref, dgp_ref), sums, pl.program_id(0))

    wide, gain = _rows(tm, D_MODEL), _const((1, D_MODEL))
    return pl.pallas_call(
        body, name=name, grid=(n // tm,),
        in_specs=[_rows(tm, f.shape[1]), _const(w_down.shape), wide, _const(w_pg.shape),
                  _rows(tm, p.shape[1]), _const(w_pp.shape), wide, gain, gain],
        out_specs=[wide] * 5 + [gain] * 3,
        out_shape=[_sds((n, D_MODEL), dt) for dt in (BF16, BF16, BF16, F32, BF16)]
                  + [_sds((1, D_MODEL), F32)] * 3,
        compiler_params=_params("arbitrary"),
    )(f, w_down, h1, w_pg, p, w_pp, target, g_final, g_ple)


def _assemble_dproj(dna, ddil_q, ddil_k, ddil_v, dgn, dgd, cos_t, sin_t, *, tm, name):
    n = dgn.shape[0]

    def body(*refs):
        dq_ref, dk_ref, dv_ref = refs[0:3]
        dil_in = refs[3:12]
        dgn_ref, dgd_ref, cos_ref, sin_ref, o_ref, scr = refs[12:18]
        o_ref[:, 0:512] = dq_ref[...]
        o_ref[:, 512:1024] = dk_ref[...].astype(BF16)
        o_ref[:, 1024:1536] = dv_ref[...].astype(BF16)
        cosv, sinv = cos_ref[...], sin_ref[...]
        for t in range(3):
            for gi, dil in enumerate(DIL_DILATIONS):
                val = _load_token_order(dil_in[t * 3 + gi], scr, dil, tm)
                if t < 2:
                    val = val * cosv + _swap_halves(val * sinv)
                c0 = 1536 + t * DIL_WIDTH + gi * 256
                o_ref[:, c0:c0 + 256] = val.astype(BF16)
        o_ref[:, 3840:4864] = dgn_ref[...]
        o_ref[:, 4864:5888] = dgd_ref[...]

    in_specs = [_rows(tm, NA_WIDTH)] * 3
    for _ in range(3):
        for dil in DIL_DILATIONS:
            in_specs.append(pl.BlockSpec((dil, tm // dil, 256), lambda i: (0, i, 0)))
    in_specs += [_rows(tm, D_MODEL)] * 2 + [_rows(tm, 256)] * 2
    return pl.pallas_call(
        body, name=name, grid=(n // tm,), in_specs=in_specs,
        out_specs=_rows(tm, IN_WIDTH), out_shape=_sds((n, IN_WIDTH), BF16),
        scratch_shapes=[_dil_scratch(tm)],
        compiler_params=_params("parallel"),
    )(*dna, *ddil_q, *ddil_k, *ddil_v, dgn, dgd, cos_t, sin_t)


N_ROW_OFF = 2 * NA_WIN_ROWS - 1
N_PAIRS = N_ROW_OFF - 1
RB_WIDTH = (N_ROW_OFF + 1) * GRID_W


def _na_bias(rb_ref, pair_scr):
    shape = (GRID_W, RB_WIDTH)
    qc = lax.broadcasted_iota(jnp.int32, shape, 0)
    qc2 = lax.broadcasted_iota(jnp.int32, (GRID_W, 128), 0)
    kc2 = lax.broadcasted_iota(jnp.int32, (GRID_W, 128), 1) & (GRID_W - 1)
    cs = jnp.clip(qc2 - 8, 0, GRID_W - 16)
    valid = (kc2 >= cs) & (kc2 < cs + 16)
    for hh in range(2):
        t = jnp.broadcast_to(rb_ref[hh], shape)
        t = pltpu.roll(t, RB_WIDTH - 15, 1)
        for b in range(6):
            t = jnp.where(((qc >> b) & 1) == 1, pltpu.roll(t, 1 << b, 1), t)
        t_odd = pltpu.roll(t, RB_WIDTH - GRID_W, 1)
        for ro in range(N_PAIRS):
            src = t if ro % 2 == 0 else t_odd
            base = (ro // 2) * 128
            pair_scr[hh, ro] = jnp.where(valid, src[:, base:base + 128], NEG_INF)


NA_GROUP_FWD = 32
NA_GROUP_BWD = 4


def _stack_heads(ref, r, scale=1.0):
    lane = lax.broadcasted_iota(jnp.int32, (GRID_W, 128), 1)
    t = ref[pl.ds(pl.multiple_of(r * GRID_W, GRID_W), GRID_W), :].astype(F32) * scale
    return jnp.concatenate([jnp.where(lane < 64, t, 0.0), jnp.where(lane >= 64, t, 0.0)], axis=0).astype(BF16)


def _unstack_heads(t2):
    lane = lax.broadcasted_iota(jnp.int32, (GRID_W, 128), 1)
    return jnp.where(lane < 64, t2[:GRID_W], t2[GRID_W:])


def _na_window(k_ref, v_ref, r, n_rows):
    rs = jnp.clip(r - NA_WIN_ROWS // 2, 0, n_rows - NA_WIN_ROWS)
    ro0 = (NA_WIN_ROWS - 1) - (r - rs)
    off = pl.multiple_of(rs * GRID_W, GRID_W)
    kw = k_ref[pl.ds(off, NA_WIN_ROWS * GRID_W), :]
    vw = v_ref[pl.ds(off, NA_WIN_ROWS * GRID_W), :]
    return kw, vw, off, ro0


def _na_probs(s_raw, pair_scr, ro0):
    bias = [jnp.concatenate([pair_scr[hh, ro0 + 2 * j] for j in range(NA_WIN_ROWS // 2)], axis=1)
            for hh in range(2)]
    s = s_raw + jnp.concatenate(bias, axis=0)
    m = jnp.max(s, axis=-1, keepdims=True)
    e = jnp.exp(s - m)
    return e * (1.0 / jnp.sum(e, axis=-1, keepdims=True))


def _na_qkv_specs(n):
    pairs = NA_WIDTH // 128
    return [pl.BlockSpec((n, 128), lambda h, first=t * pairs: (0, first + h)) for t in range(3)]


def _na_fwd(qkv, rb, *, name):
    n = qkv.shape[0]
    n_rows = n // GRID_W

    def body(ins, outs, scr):
        q_ref, k_ref, v_ref, rb_ref = ins
        o_ref, = outs
        pair_scr, = scr
        _na_bias(rb_ref, pair_scr)

        def group(g, carry):
            rows = [g * NA_GROUP_FWD + t for t in range(NA_GROUP_FWD)]
            wins = [_na_window(k_ref, v_ref, r, n_rows) for r in rows]
            raw = [lax.dot_general(_stack_heads(q_ref, r, QK_SCALE), w[0], NT_DIMS, preferred_element_type=F32)
                   for r, w in zip(rows, wins)]
            probs = [_na_probs(s, pair_scr, w[3]) for s, w in zip(raw, wins)]
            outs2 = [jnp.dot(p.astype(BF16), w[1], preferred_element_type=F32) for p, w in zip(probs, wins)]
            for r, o2 in zip(rows, outs2):
                o_ref[pl.ds(pl.multiple_of(r * GRID_W, GRID_W), GRID_W), :] = _unstack_heads(o2).astype(BF16)
            return carry

        lax.fori_loop(0, n_rows // NA_GROUP_FWD, group, 0)

    col = pl.BlockSpec((n, 128), lambda h: (0, h))
    return _call(
        body, name=name, grid=(NA_WIDTH // 128,),
        in_specs=_na_qkv_specs(n) + [pl.BlockSpec((2, 1, RB_WIDTH), lambda h: (h, 0, 0))],
        out_specs=[col], out_shape=[_sds((n, NA_WIDTH), BF16)],
        scratch_shapes=[pltpu.VMEM((2, N_PAIRS, GRID_W, 128), F32)],
        args=(qkv, qkv, qkv, rb))[0]


def _na_bwd(qkv, do, rb, *, name, after=None):
    n = qkv.shape[0]
    n_rows = n // GRID_W
    win = NA_WIN_ROWS * GRID_W

    def body(ins, outs, scr):
        q_ref, k_ref, v_ref, do_ref, rb_ref = ins
        dq_ref, dk_ref, dv_ref, drb_ref = outs
        pair_scr, acc_scr = scr
        _na_bias(rb_ref, pair_scr)
        acc_scr[...] = jnp.zeros_like(acc_scr)
        dk_ref[...] = jnp.zeros_like(dk_ref)
        dv_ref[...] = jnp.zeros_like(dv_ref)

        def group(g, carry):
            rows = [g * NA_GROUP_BWD + t for t in range(NA_GROUP_BWD)]
            wins = [_na_window(k_ref, v_ref, r, n_rows) for r in rows]
            qss = [_stack_heads(q_ref, r, QK_SCALE) for r in rows]
            doss = [_stack_heads(do_ref, r) for r in rows]
            raw = [lax.dot_general(qs, w[0], NT_DIMS, preferred_element_type=F32) for qs, w in zip(qss, wins)]
            dps = [lax.dot_general(dos, w[1], NT_DIMS, preferred_element_type=F32) for dos, w in zip(doss, wins)]
            probs = [_na_probs(s, pair_scr, w[3]) for s, w in zip(raw, wins)]
            dss = [p * (dp - jnp.sum(p * dp, axis=-1, keepdims=True)) for p, dp in zip(probs, dps)]
            dsbs = [ds.astype(BF16) for ds in dss]
            dq2s = [jnp.dot(dsb, w[0], preferred_element_type=F32) for dsb, w in zip(dsbs, wins)]
            dkws = [lax.dot_general(dsb, qs, TN_DIMS, preferred_element_type=F32) for dsb, qs in zip(dsbs, qss)]
            dvws = [lax.dot_general(p.astype(BF16), dos, TN_DIMS, preferred_element_type=F32)
                    for p, dos in zip(probs, doss)]
            for t, r in enumerate(rows):
                _, _, off, ro0 = wins[t]
                for hh in range(2):
                    for j in range(NA_WIN_ROWS // 2):
                        acc_scr[hh, ro0 + 2 * j] += dss[t][hh * GRID_W:(hh + 1) * GRID_W, j * 128:(j + 1) * 128]
                dq_ref[pl.ds(pl.multiple_of(r * GRID_W, GRID_W), GRID_W), :] = (
                    _unstack_heads(dq2s[t]) * QK_SCALE).astype(BF16)
                dk_ref[pl.ds(off, win), :] += dkws[t]
                dv_ref[pl.ds(off, win), :] += dvws[t]
            return carry

        lax.fori_loop(0, n_rows // NA_GROUP_BWD, group, 0)

        qc = lax.broadcasted_iota(jnp.int32, (N_PAIRS * GRID_W, 128), 0)
        for hh in range(2):
            t = acc_scr[hh].reshape(N_PAIRS * GRID_W, 128)
            for b in range(6):
                t = jnp.where(((qc >> b) & 1) == 1, pltpu.roll(t, 128 - (1 << b), 1), t)
            t = pltpu.roll(t, 15, 1)
            drb_ref[hh] = jnp.sum(t.reshape(N_PAIRS, GRID_W, 128), axis=1)

    col = pl.BlockSpec((n, 128), lambda h: (0, h))
    return _call(
        body, name=name, grid=(NA_WIDTH // 128,),
        in_specs=_na_qkv_specs(n) + [col, pl.BlockSpec((2, 1, RB_WIDTH), lambda h: (h, 0, 0))],
        out_specs=[col, col, col, pl.BlockSpec((2, N_PAIRS, 128), lambda h: (h, 0, 0))],
        out_shape=[_sds((n, NA_WIDTH), BF16), _sds((n, NA_WIDTH), F32), _sds((n, NA_WIDTH), F32),
                   _sds((8, N_PAIRS, 128), F32)],
        scratch_shapes=[pltpu.VMEM((2, N_PAIRS, GRID_W, 128), F32),
                        pltpu.VMEM((2, N_PAIRS, GRID_W, 128), F32)],
        args=(qkv, qkv, qkv, do, rb), after=after)


def _rpb_table(rpb2):
    t = jnp.pad(rpb2, ((0, 0), (0, 1), (0, GRID_W - rpb2.shape[-1])))
    return t.reshape(8, 1, RB_WIDTH)


def _rpb_grad(drb, *, name):
    kdim = drb.shape[1]

    def body(x_ref, o_ref):
        kk = lax.broadcasted_iota(jnp.int32, (128, 512), 0)
        jj = lax.broadcasted_iota(jnp.int32, (128, 512), 1)
        half, co = kk >> 6, kk & 63
        acc = jnp.zeros((8, 512), F32)
        for ro in range(N_PAIRS):
            hit = ((ro + half) == (jj >> 5)) & (co == (jj & 31)) & (co < 31)
            onehot = jnp.where(hit, 1.0, 0.0).astype(F32)
            acc = acc + jnp.dot(x_ref[:, ro * 128:(ro + 1) * 128], onehot, preferred_element_type=F32,
                                precision=lax.Precision.HIGHEST)
        o_ref[...] = acc

    return pl.pallas_call(
        body, name=name, grid=(1,),
        in_specs=[_const((8, kdim))], out_specs=_const((8, 512)), out_shape=_sds((8, 512), F32),
        compiler_params=_params("arbitrary"),
    )(drb)


DIL_GROUP = 4


def _dil_blocks(length):
    qb = min(128, length)
    return qb, min(qb + 2 * DIL_RADIUS, length), min(DIL_GROUP, length // qb)


def _stack_lanes(ref, t, qb, scale=1.0):
    lane = lax.broadcasted_iota(jnp.int32, (qb, 256), 1)
    val = ref[0, t * qb:(t + 1) * qb, :].astype(F32) * scale
    return jnp.concatenate([jnp.where((lane >> 6) == h, val, 0.0) for h in range(4)], axis=0).astype(BF16)


def _dil_window(k_ref, v_ref, blk, qb, win, length):
    start = pl.multiple_of(jnp.clip(blk * qb - DIL_RADIUS, 0, length - win), DIL_RADIUS)
    return k_ref[0, pl.ds(start, win), :], v_ref[0, pl.ds(start, win), :], start


def _dil_caps_init(caps_scr, qb, win):
    @pl.when((pl.program_id(0) == 0) & (pl.program_id(1) == 0))
    def _():
        gap = ((lax.broadcasted_iota(jnp.int32, (4 * qb, win), 0) & (qb - 1))
               - lax.broadcasted_iota(jnp.int32, (4 * qb, win), 1))
        for v in range(3):
            caps_scr[v] = jnp.where(jnp.abs(gap + v * DIL_RADIUS) <= DIL_RADIUS, jnp.inf, NEG_INF)


def _dil_mask(s, blk, start, qb, caps_scr):
    return jnp.minimum(s, caps_scr[(blk * qb - start) // DIL_RADIUS])


def _pick_heads(stacked, qb):
    lane = lax.broadcasted_iota(jnp.int32, (qb, 256), 1)
    out = jnp.zeros((qb, 256), stacked.dtype)
    for h in range(4):
        out = jnp.where((lane >> 6) == h, stacked[h * qb:(h + 1) * qb], out)
    return out


def _stack_head_cols(ref, t, qb):
    return jnp.concatenate([ref[0, t * qb:(t + 1) * qb, 64 * h:64 * h + 1] for h in range(4)], axis=0)


def _dil_fwd(q, k, v, *, name, after=None):
    dil, length, _ = q.shape
    qb, win, grp = _dil_blocks(length)
    extra = [] if after is None else [after]

    def body(q_ref, k_ref, v_ref, *rest):
        o_ref, lse_ref, caps_scr = rest[-3:]
        _dil_caps_init(caps_scr, qb, win)
        blks = [pl.program_id(1) * grp + t for t in range(grp)]
        wins = [_dil_window(k_ref, v_ref, b, qb, win, length) for b in blks]
        raw = [lax.dot_general(_stack_lanes(q_ref, t, qb, QK_SCALE), w[0], NT_DIMS, preferred_element_type=F32)
               for t, w in enumerate(wins)]
        lses, outs = [], []
        for t, (s, w) in enumerate(zip(raw, wins)):
            s = _dil_mask(s, blks[t], w[2], qb, caps_scr)
            m = jnp.max(s, axis=-1, keepdims=True)
            e = jnp.exp(s - m)
            norm = jnp.sum(e, axis=-1, keepdims=True)
            lses.append(m + jnp.log(norm))
            outs.append(jnp.dot((e * (1.0 / norm)).astype(BF16), w[1], preferred_element_type=F32))
        for t in range(grp):
            o_ref[0, t * qb:(t + 1) * qb, :] = _pick_heads(outs[t], qb)
            lse_ref[0, t * qb:(t + 1) * qb, :] = _pick_heads(jnp.broadcast_to(lses[t], (4 * qb, 256)), qb)

    seq = pl.BlockSpec((1, length, 256), lambda j, i: (j, 0, 0))
    blk = pl.BlockSpec((1, grp * qb, 256), lambda j, i: (j, i, 0))
    return pl.pallas_call(
        body, name=name, grid=(dil, length // (grp * qb)),
        in_specs=[blk, seq, seq] + [pl.BlockSpec(memory_space=pl.ANY)] * len(extra), out_specs=[blk, blk],
        out_shape=[_sds((dil, length, 256), F32)] * 2,
        scratch_shapes=[pltpu.VMEM((3, 4 * qb, win), F32)],
        compiler_params=_params("arbitrary", "arbitrary"),
    )(q, k, v, *extra)


def _dil_bwd(q, k, v, do, lse, cc, *, name):
    dil, length, _ = q.shape
    qb, win, grp = _dil_blocks(length)

    def body(q_ref, k_ref, v_ref, do_ref, lse_ref, cc_ref, dq_ref, dk_ref, dv_ref, caps_scr):
        _dil_caps_init(caps_scr, qb, win)

        @pl.when(pl.program_id(1) == 0)
        def _():
            dk_ref[...] = jnp.zeros_like(dk_ref)
            dv_ref[...] = jnp.zeros_like(dv_ref)

        blks = [pl.program_id(1) * grp + t for t in range(grp)]
        wins = [_dil_window(k_ref, v_ref, b, qb, win, length) for b in blks]
        qss = [_stack_lanes(q_ref, t, qb, QK_SCALE) for t in range(grp)]
        doss = [_stack_lanes(do_ref, t, qb) for t in range(grp)]
        raw = [lax.dot_general(qs, w[0], NT_DIMS, preferred_element_type=F32) for qs, w in zip(qss, wins)]
        dps = [lax.dot_general(dos, w[1], NT_DIMS, preferred_element_type=F32) for dos, w in zip(doss, wins)]
        probs = [jnp.exp(_dil_mask(s, blks[t], wins[t][2], qb, caps_scr) - _stack_head_cols(lse_ref, t, qb))
                 for t, s in enumerate(raw)]
        dsbs = [(p * (dp + _stack_head_cols(cc_ref, t, qb))).astype(BF16)
                for t, (p, dp) in enumerate(zip(probs, dps))]
        dq4s = [jnp.dot(dsb, w[0], preferred_element_type=F32) for dsb, w in zip(dsbs, wins)]
        dkws = [lax.dot_general(dsb, qs, TN_DIMS, preferred_element_type=F32) for dsb, qs in zip(dsbs, qss)]
        dvws = [lax.dot_general(p.astype(BF16), dos, TN_DIMS, preferred_element_type=F32)
                for p, dos in zip(probs, doss)]
        for t in range(grp):
            dq_ref[0, t * qb:(t + 1) * qb, :] = _pick_heads(dq4s[t], qb) * QK_SCALE
            dk_ref[0, pl.ds(wins[t][2], win), :] += dkws[t]
            dv_ref[0, pl.ds(wins[t][2], win), :] += dvws[t]

    seq = pl.BlockSpec((1, length, 256), lambda j, i: (j, 0, 0))
    blk = pl.BlockSpec((1, grp * qb, 256), lambda j, i: (j, i, 0))
    return pl.pallas_call(
        body, name=name, grid=(dil, length // (grp * qb)),
        in_specs=[blk, seq, seq, blk, blk, blk], out_specs=[blk, seq, seq],
        out_shape=[_sds((dil, length, 256), F32)] * 3,
        scratch_shapes=[pltpu.VMEM((3, 4 * qb, win), F32)],
        compiler_params=_params("arbitrary", "arbitrary"),
    )(q, k, v, do, lse, cc)


def _merge_weights(lses):
    m = jnp.maximum(jnp.maximum(lses[0], lses[1]), lses[2])
    es = [jnp.exp(t - m) for t in lses]
    inv = 1.0 / (es[0] + es[1] + es[2])
    return [e * inv for e in es]


def _branch_mix(y_na, w_bna, outs, lses, w_bd, gates, *, tm, name):
    n = y_na.shape[0]
    chunk = min(EPILOGUE_ROWS, tm)

    def body(yna_ref, wn_ref, *rest):
        o_in, l_in = rest[0:3], rest[3:6]
        wd_ref, sn_ref, sd_ref = rest[6:9]
        y_ref, yb_ref, bn_ref, bd_ref, mix_ref, scr = rest[9:15]
        for r0 in range(0, tm, chunk):
            rows = slice(r0, r0 + chunk)
            lv = [_load_token_order(l_in[g], scr, d, chunk, r0) for g, d in enumerate(DIL_DILATIONS)]
            ws = _merge_weights(lv)
            y = jnp.zeros((chunk, 256), F32)
            for g, d in enumerate(DIL_DILATIONS):
                y = y + ws[g] * _load_token_order(o_in[g], scr, d, chunk, r0)
            yb = y.astype(BF16)
            y_ref[rows, :] = y
            yb_ref[rows, :] = yb
            bn = lax.dot_general(yna_ref[rows, :], wn_ref[...], NT_DIMS, preferred_element_type=F32).astype(BF16)
            bd = lax.dot_general(yb, wd_ref[...], NT_DIMS, preferred_element_type=F32)
            bn_ref[rows, :] = bn
            bd, mixed = _gate_mix_tile(bd, sn_ref[rows, :], bn, sd_ref[rows, :])
            bd_ref[rows, :] = bd.astype(BF16)
            mix_ref[rows, :] = mixed.astype(BF16)

    specs = [_dil_spec(d, tm) for d in DIL_DILATIONS]
    return pl.pallas_call(
        body, name=name, grid=(n // tm,),
        in_specs=[_rows(tm, NA_WIDTH), _const(w_bna.shape)] + specs + specs
                 + [_const(w_bd.shape), _rows(tm, D_MODEL, 0), _rows(tm, D_MODEL, 1)],
        out_specs=[_rows(tm, 256)] * 2 + [_rows(tm, D_MODEL)] * 3,
        out_shape=[_sds((n, 256), F32), _sds((n, 256), BF16)] + [_sds((n, D_MODEL), BF16)] * 3,
        scratch_shapes=[_dil_scratch(chunk)],
        compiler_params=_params("parallel"),
    )(y_na, w_bna, *outs, *lses, w_bd, gates, gates)


def _branch_bwd(dh, w_out, gates, bn, bd, w_bna, w_bd, y, lses, *, tm, name):
    n = dh.shape[0]
    chunk = min(EPILOGUE_ROWS, tm)

    def body(dh_ref, wo_ref, sn_ref, sd_ref, bn_ref, bd_ref, wn_ref, wd_ref, y_ref, *rest):
        l_in = rest[0:3]
        dbn_ref, dbd_ref, dgn_ref, dgd_ref, dyna_ref = rest[3:8]
        do_out, cc_out, scr = rest[8:11], rest[11:14], rest[14]
        rr = lax.broadcasted_iota(jnp.int32, (256, 256), 0) >> 6
        cc = lax.broadcasted_iota(jnp.int32, (256, 256), 1) >> 6
        ones = jnp.where(rr == cc, 1.0, 0.0).astype(F32)
        for r0 in range(0, tm, chunk):
            rows = slice(r0, r0 + chunk)
            dm = lax.dot_general(dh_ref[rows, :], wo_ref[...], NT_DIMS, preferred_element_type=F32)
            dbn, dbd, dgn, dgd = (t.astype(BF16) for t in _gate_bwd_tile(
                dm, sn_ref[rows, :], bn_ref[rows, :], sd_ref[rows, :], bd_ref[rows, :]))
            dbn_ref[rows, :] = dbn
            dbd_ref[rows, :] = dbd
            dgn_ref[rows, :] = dgn
            dgd_ref[rows, :] = dgd
            dyna_ref[rows, :] = jnp.dot(dbn, wn_ref[...], preferred_element_type=F32).astype(BF16)
            dyv = jnp.dot(dbd, wd_ref[...], preferred_element_type=F32)
            lv = [_load_token_order(l_in[g], scr, d, chunk, r0) for g, d in enumerate(DIL_DILATIONS)]
            ws = _merge_weights(lv)
            tsum = jnp.dot(dyv * y_ref[rows, :], ones, preferred_element_type=F32,
                           precision=lax.Precision.HIGHEST)
            for g, d in enumerate(DIL_DILATIONS):
                _store_dil_order(ws[g] * dyv, do_out[g], scr, d, r0)
                _store_dil_order(-ws[g] * tsum, cc_out[g], scr, d, r0)

    specs = [_dil_spec(d, tm) for d in DIL_DILATIONS]
    wide = _rows(tm, D_MODEL)
    res = pl.pallas_call(
        body, name=name, grid=(n // tm,),
        in_specs=[wide, _const(w_out.shape), _rows(tm, D_MODEL, 0), _rows(tm, D_MODEL, 1), wide, wide,
                  _const(w_bna.shape), _const(w_bd.shape), _rows(tm, 256)] + specs,
        out_specs=[wide] * 4 + [_rows(tm, NA_WIDTH)] + specs + specs,
        out_shape=[_sds((n, D_MODEL), BF16)] * 4 + [_sds((n, NA_WIDTH), BF16)]
                  + [_sds((d, n // d, 256), BF16) for d in DIL_DILATIONS]
                  + [_sds((d, n // d, 256), F32) for d in DIL_DILATIONS],
        scratch_shapes=[_dil_scratch(chunk)],
        compiler_params=_params("parallel"),
    )(dh, w_out, gates, gates, bn, bd, w_bna, w_bd, y, *lses)
    return res[0], res[1], res[2], res[3], res[4], res[5:8], res[8:11]


_WEIGHTS = (("w_in", 1, 736), ("w_branch_na", 1, 128), ("w_branch_dil", 1, 128), ("w_out", 0, 128),
            ("w_up", 1, 512), ("w_down", 0, 512), ("w_ple_gate", 0, 128), ("w_ple_proj", 1, 128))
_W_IN, _W_BNA, _W_BD, _W_OUT, _W_UP, _W_DOWN, _W_PG, _W_PP = range(8)


def _to_full(gathered):
    return gathered.reshape(-1, gathered.shape[2])


def _to_chunks(widx, mat):
    return mat.reshape(N_DEV, _WEIGHTS[widx][2], mat.shape[1])


def _local_step(x, p_bf16, positions, target, g_mix, g_mlp, g_ple, g_final, rpb2,
                get_w_in, relay_rest, get_rest, send_grads):
    tm = 512
    half = HEAD_DIM // 2
    inv_freq = 10000.0 ** (-jnp.arange(half, dtype=F32) / half)
    ang = positions.astype(F32)[:, None] * inv_freq
    cos, sin = jnp.cos(ang), jnp.sin(ang)
    cos_t = jnp.tile(jnp.concatenate([cos, cos], axis=-1), (1, 4))
    sin_t = jnp.tile(jnp.concatenate([-sin, sin], axis=-1), (1, 4))
    rb = _rpb_table(rpb2)

    a = _rms_fwd(x, g_mix, tm=tm, name="rms_mix")
    w_in, token = get_w_in((a, cos_t, sin_t, p_bf16))
    na_qkv, gates, dq_g, dk_g, dv_g = _project_in(a, w_in, cos_t, sin_t, tm=512, name="mm_in", after=token)
    y_na = _na_fwd(na_qkv, rb, name="na_fwd")
    token = relay_rest(y_na)
    d_out, d_lse = [], []
    for g in range(3):
        o, lse = _dil_fwd(dq_g[g], dk_g[g], dv_g[g], name=f"dil_fwd{g}", after=token if g == 0 else None)
        d_out.append(o)
        d_lse.append(lse)
    w_bna, w_bd = get_rest(d_out[2], 0)
    y_dil, y_dil_b, bn, bd, mixed = _branch_mix(y_na, w_bna, d_out, d_lse, w_bd, gates, tm=tm, name="branch_mix")
    w_out, w_up, w_down, w_pg, w_pp = get_rest(mixed, 1)
    h1, c = _matmul(mixed, w_out, out_dtype=(F32, BF16), tm=512, tn=1024, tk=1024, name="mm_out",
                    extra=(x, g_mlp), epilogue=_residual_rms_tile)
    u, f = _matmul(c, w_up, tb=True, out_dtype=(BF16, BF16), tm=512, tn=4096, tk=1024, name="mm_up",
                   epilogue=lambda acc: (acc, jnp.square(jnp.maximum(acc, 0.0))))

    e, dpp, dgt, dh2, dh2_b, dg_final, loss, dg_ple = _tail_step(
        f, w_down, h1, w_pg, p_bf16, w_pp, target, g_final, g_ple, tm=256, name="tail_step")
    loss = loss[:, :128]
    gw_pp = _matmul(p_bf16, dpp, ta=True, transpose_out=True, out_dtype=BF16, tm=256, tn=1024, tk=2048,
                    name="mm_gw_pp")
    gw_pg = _matmul(e, dgt, ta=True, out_dtype=BF16, tm=512, tn=1024, tk=2048, name="mm_gw_pg")
    du = _matmul(dh2_b, w_down, tb=True, out_dtype=BF16, tm=512, tn=4096, tk=1024, name="mm_du",
                 extra=(u,), epilogue=lambda acc, uv: (acc * (2.0 * jnp.maximum(uv.astype(F32), 0.0)),))
    gw_down = _matmul(f, dh2_b, ta=True, out_dtype=BF16, tm=1024, tn=1024, tk=2048, name="mm_gw_down")
    gw_up = _matmul(c, du, ta=True, transpose_out=True, out_dtype=BF16, tm=512, tn=2048, tk=2048, name="mm_gw_up")
    dh1, dh1_b, dg_mlp = _matmul(
        du, w_up, out_dtype=(F32, BF16), tm=512, tn=1024, tk=4096, name="mm_dc",
        extra=(h1, g_mlp, dh2), epilogue=_rms_bwd_twice, n_colsum=1)
    dbn, dbd, dgn, dgd, dy_na, do_g, cc_g = _branch_bwd(dh1_b, w_out, gates, bn, bd, w_bna, w_bd, y_dil, d_lse,
                                                        tm=tm, name="branch_bwd")
    gw_out = _matmul(mixed, dh1_b, ta=True, out_dtype=BF16, tm=512, tn=1024, tk=2048, name="mm_gw_out")
    gw_bna = _matmul(y_na, dbn, ta=True, transpose_out=True, out_dtype=BF16, tm=512, tn=1024, tk=2048,
                     name="mm_gw_bna")
    gw_bd = _matmul(y_dil_b, dbd, ta=True, transpose_out=True, out_dtype=BF16, tm=256, tn=1024, tk=2048,
                    name="mm_gw_bd")
    token = send_grads((_W_PP, _W_PG, _W_DOWN, _W_UP, _W_OUT, _W_BNA, _W_BD),
                       (gw_pp, gw_pg, gw_down, gw_up, gw_out, gw_bna, gw_bd))
    dna = _na_bwd(na_qkv, dy_na, rb, name="na_bwd", after=token)
    drpb = _rpb_grad(dna[3].reshape(8, -1), name="rpb_grad")
    ddq, ddk, ddv = [], [], []
    for g in range(3):
        r = _dil_bwd(dq_g[g], dk_g[g], dv_g[g], do_g[g], d_lse[g], cc_g[g], name=f"dil_bwd{g}")
        ddq.append(r[0])
        ddk.append(r[1])
        ddv.append(r[2])
    dproj = _assemble_dproj(dna[0:3], ddq, ddk, ddv, dgn, dgd, cos_t, sin_t, tm=tm, name="assemble_dproj")
    gw_in = _matmul(a, dproj, ta=True, transpose_out=True, out_dtype=BF16, tm=512, tn=2944, tk=2048, name="mm_gw_in")
    token = send_grads((_W_IN,), (gw_in,))
    dx, dg_mix = _matmul(
        dproj, w_in, out_dtype=(F32,), tm=512, tn=1024, tk=5888, name="mm_da", after=token,
        extra=(x, g_mix, dh1), epilogue=_rms_bwd_tile, n_colsum=1)
    return loss, dx, (dg_mix, dg_mlp, dg_ple, dg_final), drpb


def _cast_bf16(t, *, name):
    def body(t_ref, o_ref):
        o_ref[...] = t_ref[...].astype(BF16)

    rows, cols = t.shape
    tr = 256 if rows % 256 == 0 else rows
    blk = pl.BlockSpec((tr, cols), lambda i: (i, 0))
    return pl.pallas_call(body, name=name, grid=(rows // tr,), in_specs=[blk], out_specs=blk,
                          out_shape=_sds(t.shape, BF16), compiler_params=_params("parallel"))(t)


def _adamw(w, g, m, v):
    m = ADAM_B1 * m + (1.0 - ADAM_B1) * g
    v = ADAM_B2 * v + (1.0 - ADAM_B2) * (g * g)
    m_hat = m / (1.0 - ADAM_B1 ** ADAM_STEP)
    v_hat = v / (1.0 - ADAM_B2 ** ADAM_STEP)
    delta = -ADAM_LR * (m_hat / (jnp.sqrt(v_hat) + ADAM_EPS) + ADAM_WD * w)
    return delta, m, v


def _sum_adamw(parts, w, m, v, *, tr, name, own=None, transposed=False):
    rows, cols = w.shape
    n_pre = 0 if own is None else 1

    def body(*refs):
        p_ref, w_ref, m_ref, v_ref = refs[n_pre:n_pre + 4]
        g_ref, d_ref, nm_ref, nv_ref = refs[-4:]
        g = (p_ref[0] if own is None else refs[n_pre + 4][...]).astype(F32)
        for s in range(1, N_DEV):
            g = g + p_ref[s].astype(F32)
        if transposed:
            g = g.T
        g_ref[...] = g
        d_ref[...], nm_ref[...], nv_ref[...] = _adamw(w_ref[...], g, m_ref[...], v_ref[...])

    if transposed:
        blk = pl.BlockSpec((rows, tr), lambda i, *_: (0, i))
        g_rows, steps = rows, cols // tr
    else:
        blk = pl.BlockSpec((tr, cols), lambda i, *_: (i, 0))
        g_rows, steps = cols, rows // tr
    in_specs = [pl.BlockSpec((N_DEV, tr, g_rows), lambda i, *_: (0, i, 0)), blk, blk, blk]
    args = [parts, w, m, v]
    if own is not None:
        in_specs.append(pl.BlockSpec((None, tr, g_rows), lambda i, idx: (idx[0], i, 0)))
        args = [own[1]] + args + [own[0]]
    return pl.pallas_call(
        body, name=name,
        grid_spec=pltpu.PrefetchScalarGridSpec(num_scalar_prefetch=n_pre, grid=(steps,), in_specs=in_specs,
                                               out_specs=[blk] * 4),
        out_shape=[_sds((rows, cols), F32)] * 4,
        compiler_params=_params("parallel"),
    )(*args)


_RPB_SIZE = 8 * 15 * 31


def _pack_small(g_mix, g_mlp, g_ple, g_final, rpb, loss_row):
    flat = jnp.concatenate([g_mix.reshape(-1), g_mlp.reshape(-1), g_ple.reshape(-1), g_final.reshape(-1),
                            rpb.reshape(-1), jnp.zeros((3840 - _RPB_SIZE,), F32), loss_row.reshape(-1),
                            jnp.zeros((128,), F32)])
    return flat.reshape(64, 128)


def _unpack_small(t):
    flat = t.reshape(-1)
    return (flat[0:1024].reshape(1, 1024), flat[4096:4096 + _RPB_SIZE].reshape(1, 8, 15, 31),
            flat[1024:2048].reshape(1, 1024), flat[2048:3072].reshape(1, 1024), flat[3072:4096])


def kernel(x, p, positions, g_mix, w_in, rpb, w_branch_na, w_branch_dil, w_out, g_mlp, w_up, w_down, g_ple, w_ple_gate, w_ple_proj, g_final, loss_target, m_g_mix, m_w_in, m_rpb, m_w_branch_na, m_w_branch_dil, m_w_out, m_g_mlp, m_w_up, m_w_down, m_g_ple, m_w_ple_gate, m_w_ple_proj, m_g_final, v_g_mix, v_w_in, v_rpb, v_w_branch_na, v_w_branch_dil, v_w_out, v_g_mlp, v_w_up, v_w_down, v_g_ple, v_w_ple_gate, v_w_ple_proj, v_g_final):
    sharded = dict(w_in=(w_in, m_w_in, v_w_in), w_branch_na=(w_branch_na, m_w_branch_na, v_w_branch_na),
                   w_branch_dil=(w_branch_dil, m_w_branch_dil, v_w_branch_dil), w_out=(w_out, m_w_out, v_w_out),
                   w_up=(w_up, m_w_up, v_w_up), w_down=(w_down, m_w_down, v_w_down),
                   w_ple_gate=(w_ple_gate, m_w_ple_gate, v_w_ple_gate),
                   w_ple_proj=(w_ple_proj, m_w_ple_proj, v_w_ple_proj))
    shards = {k: tuple(t[0] for t in val) for k, val in sharded.items()}

    me = _my_index()

    shards["w_in"] = tuple(t.T for t in shards["w_in"])

    w_in_b = _cast_bf16(shards["w_in"][0], name="cast_w_in")
    rest_b = [shards[name][0].astype(BF16).T if axis == 1 else shards[name][0].astype(BF16)
              for name, axis, _ in _WEIGHTS[1:]]
    first_in, token_in = _start_copies(_first_leg_copies, [w_in_b], [_sds((N_DEV,) + w_in_b.shape, BF16)], 4,
                                       name="start_gather_w_in")

    def whole(landed, mine):
        return _to_full(lax.dynamic_update_index_in_dim(landed, mine, me, 0))

    rest = {}

    def get_w_in(after):
        (mine,), landed = _wait_copies(_first_leg_copies, first_in, (*after, *rest_b), name="wait_gather_w_in")
        second, token = _start_copies(_second_leg_copies, [], landed, 3, name="start_forward_w_in")
        _, (landed,) = _wait_copies(_second_leg_copies, second, token, name="wait_forward_w_in")
        rest["first"], token = _start_copies(_first_leg_copies, rest_b,
                                             [_sds((N_DEV,) + t.shape, BF16) for t in rest_b], 4 * len(rest_b),
                                             name="start_gather_rest", after=landed)
        return whole(landed, mine), token

    def relay_rest(after):
        rest["mine"], landed = _wait_copies(_first_leg_copies, rest["first"], after, name="wait_gather_rest")
        rest["second"], token = _start_copies(_second_leg_copies, [], landed, 3 * len(rest_b),
                                              name="start_forward_rest")
        return token

    def get_rest(after, stage):
        n_src, send_sems, recv_sems, bufs = rest["second"]
        part = slice(0, 2) if stage == 0 else slice(2, len(rest_b))
        _, landed = _wait_copies(functools.partial(_second_leg_copies, first=part.start),
                                 (n_src, send_sems, recv_sems, bufs[part]), after,
                                 name=f"wait_forward_rest{stage}")
        return [whole(t, own) for t, own in zip(landed, rest["mine"][part])]

    sent = []

    def send_grads(indices, grads):
        chunked = [_to_chunks(i, g) for i, g in zip(indices, grads)]
        handle, token = _start_copies(_exchange_copies, chunked, [_sds(t.shape, BF16) for t in chunked],
                                      7 * len(chunked),
                                      name="start_exchange_" + ("w_in" if indices == (_W_IN,) else "rest"))
        sent.append((indices, handle))
        return token

    g_mix_0 = g_mix + token_in[0:1, 0:1]
    loss, dx, dgs, drpb = _local_step(
        x[0], p[0, 0].astype(BF16), positions[0], loss_target[0],
        g_mix_0, g_mlp, g_ple, g_final.reshape(1, -1), rpb[0], get_w_in, relay_rest, get_rest, send_grads)

    drpb3 = drpb.reshape(8, 16, 32)[:, :15, :31]
    small = _pack_small(dgs[0], dgs[1], dgs[2], dgs[3], drpb3, loss)
    share, done = _start_copies(_gather_copies, [small], [_sds((N_DEV,) + small.shape, F32)], 7,
                                name="start_share_small")

    out = {}
    for indices, handle in sent:
        chunked, landed = _wait_copies(_exchange_copies, handle, done,
                                       name="wait_exchange_" + ("w_in" if indices == (_W_IN,) else "rest"))
        for i, part, mine in zip(indices, landed, chunked):
            name = _WEIGHTS[i][0]
            w, m, v = shards[name]
            turned = _WEIGHTS[i][1] == 1 and i != _W_IN
            res = _sum_adamw(part, w, m, v, tr=368 if i == _W_IN else 128, name="adamw_" + name,
                             own=(mine, me.reshape(1).astype(jnp.int32)), transposed=turned)
            out[name] = [(t.T if i == _W_IN else t)[None] for t in res]
            done = res[0]
    (small,), (small_landed,) = _wait_copies(_gather_copies, share, done, name="wait_share_small")
    small_all = lax.dynamic_update_index_in_dim(small_landed, small, me, 0)
    small_w = _pack_small(g_mix, g_mlp, g_ple, g_final, rpb, jnp.zeros((128,), F32))
    small_m = _pack_small(m_g_mix, m_g_mlp, m_g_ple, m_g_final, m_rpb, jnp.zeros((128,), F32))
    small_v = _pack_small(v_g_mix, v_g_mlp, v_g_ple, v_g_final, v_rpb, jnp.zeros((128,), F32))
    res = _sum_adamw(small_all, small_w, small_m, small_v, tr=64, name="adamw_small")
    unpacked = [_unpack_small(t) for t in res]
    for i, name in enumerate(("g_mix", "rpb", "g_mlp", "g_ple", "g_final")):
        out[name] = [u[i] for u in unpacked]
    loss_total = res[0][62, 0]

    order = ("g_mix", "w_in", "rpb", "w_branch_na", "w_branch_dil", "w_out", "g_mlp", "w_up", "w_down",
             "g_ple", "w_ple_gate", "w_ple_proj", "g_final")
    grads = [out[k][0] for k in order]
    deltas = [out[k][1] for k in order]
    new_m = [out[k][2] for k in order]
    new_v = [out[k][3] for k in order]
    return (loss_total, dx[None], *grads, *deltas, *new_m, *new_v)
```
